```python
import functools
import jax, jax.numpy as jnp
from jax import lax
import numpy as np

D_MODEL = 1024
BATCH = 8
SEQ = 2048
DEPTH = 2

CHUNK = 64
Q_BLOCK = 128
MLA_HEADS = 8
MLA_NOPE = 64
MLA_ROPE = 32
MLA_V = 64
Q_LORA = 384
KV_LORA = 256
ROPE_THETA = 10000.0
SB_HEADS = 8
SB_DIM = 64
C_HEADS = 16
C_DIM = 64
LEFT_CHUNKS = 8
BAND = (LEFT_CHUNKS + 1) * CHUNK
REL_CLIP = 256
D_FF = -(-8 * D_MODEL // (3 * 256)) * 256
EVEN_IN = Q_LORA + KV_LORA + MLA_ROPE + 3 * SB_HEADS * SB_DIM
MIX_EVEN = MLA_HEADS * MLA_V + SB_HEADS * SB_DIM
MIX_ODD = C_HEADS * C_DIM
N_EVEN = (DEPTH + 1) // 2
N_ODD = DEPTH // 2
RMS_EPS = 1e-6

kernel_name = "hybrid_mla_stickbreak_chunkband_encoder"


def rms_norm(x, g):
    x32 = x.astype(jnp.float32)
    y = x32 * lax.rsqrt(jnp.mean(x32 * x32, axis=-1, keepdims=True) + RMS_EPS)
    return (y * g.astype(jnp.float32)).astype(x.dtype)


def rope_tables(seq, dim):
    pos = jnp.arange(seq, dtype=jnp.float32)
    inv_freq = ROPE_THETA ** (-jnp.arange(0, dim, 2, dtype=jnp.float32) / dim)
    ang = pos[:, None] * inv_freq[None, :]
    return jnp.cos(ang), jnp.sin(ang)


def apply_rope(x, cos, sin):
    half = x.shape[-1] // 2
    c, s = cos.astype(x.dtype), sin.astype(x.dtype)
    x1, x2 = x[..., :half], x[..., half:]
    return jnp.concatenate([x1 * c - x2 * s, x2 * c + x1 * s], axis=-1)


def swiglu(u, w_gate, w_up, w_down):
    return (jax.nn.silu(u @ w_gate) * (u @ w_up)) @ w_down


def mla_stick_breaking_mixer(u, w_in, g_cq, w_uq, g_ckv, w_ukv, w_out):
    bsz, seq, _ = u.shape
    proj = u @ w_in
    o1 = Q_LORA
    o2 = o1 + KV_LORA
    o3 = o2 + MLA_ROPE
    nb = SB_HEADS * SB_DIM
    c_q, c_kv, k_r = proj[..., :o1], proj[..., o1:o2], proj[..., o2:o3]
    q_b = proj[..., o3:o3 + nb].reshape(bsz, seq, SB_HEADS, SB_DIM)
    k_b = proj[..., o3 + nb:o3 + 2 * nb].reshape(bsz, seq, SB_HEADS, SB_DIM)
    v_b = proj[..., o3 + 2 * nb:].reshape(bsz, seq, SB_HEADS, SB_DIM)

    cos, sin = rope_tables(seq, MLA_ROPE)
    q_a = (rms_norm(c_q, g_cq) @ w_uq).reshape(bsz, seq, MLA_HEADS, MLA_NOPE + MLA_ROPE)
    q_a = jnp.concatenate([q_a[..., :MLA_NOPE],
                           apply_rope(q_a[..., MLA_NOPE:], cos[:, None, :], sin[:, None, :])], axis=-1)
    kv = (rms_norm(c_kv, g_ckv) @ w_ukv).reshape(bsz, seq, MLA_HEADS, MLA_NOPE + MLA_V)
    k_rope = apply_rope(k_r, cos, sin)
    k_a = jnp.concatenate([kv[..., :MLA_NOPE],
                           jnp.broadcast_to(k_rope[:, :, None, :], (bsz, seq, MLA_HEADS, MLA_ROPE))], axis=-1)
    v_a = kv[..., MLA_NOPE:]
    scale_a = (MLA_NOPE + MLA_ROPE) ** -0.5
    scale_b = SB_DIM ** -0.5

    outs_a, outs_b = [], []
    for blk in range(seq // Q_BLOCK):
        t0 = blk * Q_BLOCK
        kend = t0 + Q_BLOCK
        t_pos = t0 + jnp.arange(Q_BLOCK)[:, None]
        s_pos = jnp.arange(kend)[None, :]
        s_a = jnp.einsum('bqhd,bkhd->bhqk', q_a[:, t0:kend], k_a[:, :kend]).astype(jnp.float32) * scale_a
        chunk_ok = (s_pos // CHUNK) <= (t_pos // CHUNK)
        p_a = jax.nn.softmax(jnp.where(chunk_ok, s_a, -jnp.inf), axis=-1)
        outs_a.append(jnp.einsum('bhqk,bkhd->bqhd', p_a.astype(v_a.dtype), v_a[:, :kend]))
        z = jnp.einsum('bqhd,bkhd->bhqk', q_b[:, t0:kend], k_b[:, :kend]).astype(jnp.float32) * scale_b
        before = s_pos < t_pos
        log_keep = jnp.where(before, jax.nn.log_sigmoid(-z), 0.0)
        log_between = lax.cumsum(log_keep, axis=3, reverse=True) - log_keep
        w_b = jnp.where(before, jnp.exp(jax.nn.log_sigmoid(z) + log_between), 0.0)
        outs_b.append(jnp.einsum('bhqk,bkhd->bqhd', w_b.astype(v_b.dtype), v_b[:, :kend]))

    o_a = jnp.concatenate(outs_a, axis=1).reshape(bsz, seq, MLA_HEADS * MLA_V)
    o_b = jnp.concatenate(outs_b, axis=1).reshape(bsz, seq, SB_HEADS * SB_DIM)
    return jnp.concatenate([o_a, o_b], axis=-1) @ w_out


def chunk_band_mixer(u, w_qkv, rel_bias, w_out):
    bsz, seq, _ = u.shape
    n_chunks = seq // CHUNK
    qkv = (u @ w_qkv).reshape(bsz, n_chunks, CHUNK, 3, C_HEADS, C_DIM)
    q, k, v = qkv[:, :, :, 0], qkv[:, :, :, 1], qkv[:, :, :, 2]
    pad = ((0, 0), (LEFT_CHUNKS, 0), (0, 0), (0, 0), (0, 0))
    k_p = jnp.pad(k, pad)
    v_p = jnp.pad(v, pad)
    scores = jnp.concatenate(
        [jnp.einsum('bnqhd,bnkhd->bnhqk', q, k_p[:, i:i + n_chunks]) for i in range(LEFT_CHUNKS + 1)],
        axis=-1).astype(jnp.float32) * (C_DIM ** -0.5)
    q_in = jnp.arange(CHUNK)[:, None]
    j = jnp.arange(BAND)[None, :]
    rel = (LEFT_CHUNKS - j // CHUNK) * CHUNK + q_in - j % CHUNK
    bias = rel_bias[:, jnp.clip(rel, -REL_CLIP, REL_CLIP) + REL_CLIP]
    valid = (jnp.arange(n_chunks)[:, None] - LEFT_CHUNKS + j // CHUNK) >= 0
    scores = scores + bias.astype(jnp.float32)[None, None]
    p = jax.nn.softmax(jnp.where(valid[None, :, None, None, :], scores, -jnp.inf), axis=-1).astype(v.dtype)
    parts = [jnp.einsum('bnhqk,bnkhd->bnqhd', p[..., i * CHUNK:(i + 1) * CHUNK], v_p[:, i:i + n_chunks])
             for i in range(LEFT_CHUNKS + 1)]
    o = functools.reduce(jnp.add, parts)
    return o.reshape(bsz, seq, MIX_ODD) @ w_out


def _fwd_setup_inputs(seed: int = 0) -> dict:
    key = jax.random.key(seed)
    ks = jax.random.split(key, 17)
    f32 = jnp.float32

    def nrm(k, shape, fan_in):
        return jax.random.normal(k, shape, f32) * (fan_in ** -0.5)

    def gain(k, shape):
        return 1.0 + 0.05 * jax.random.normal(k, shape, f32)

    return {
        "x": jax.random.normal(ks[0], (BATCH, SEQ, D_MODEL), f32),
        "ev_w_in": nrm(ks[1], (N_EVEN, D_MODEL, EVEN_IN), D_MODEL),
        "ev_g_cq": gain(ks[2], (N_EVEN, Q_LORA)),
        "ev_w_uq": nrm(ks[3], (N_EVEN, Q_LORA, MLA_HEADS * (MLA_NOPE + MLA_ROPE)), Q_LORA),
        "ev_g_ckv": gain(ks[4], (N_EVEN, KV_LORA)),
        "ev_w_ukv": nrm(ks[5], (N_EVEN, KV_LORA, MLA_HEADS * (MLA_NOPE + MLA_V)), KV_LORA),
        "ev_w_out": nrm(ks[6], (N_EVEN, MIX_EVEN, D_MODEL), MIX_EVEN),
        "od_w_qkv": nrm(ks[7], (N_ODD, D_MODEL, 3 * MIX_ODD), D_MODEL),
        "od_rel_bias": 0.1 * jax.random.normal(ks[8], (N_ODD, C_HEADS, 2 * REL_CLIP + 1), f32),
        "od_w_out": nrm(ks[9], (N_ODD, MIX_ODD, D_MODEL), MIX_ODD),
        "g_mix": gain(ks[10], (DEPTH, D_MODEL)),
        "g_ffn": gain(ks[11], (DEPTH, D_MODEL)),
        "w_gate": nrm(ks[12], (DEPTH, D_MODEL, D_FF), D_MODEL),
        "w_up": nrm(ks[13], (DEPTH, D_MODEL, D_FF), D_MODEL),
        "w_down": nrm(ks[14], (DEPTH, D_FF, D_MODEL), D_FF),
        "g_final": gain(ks[15], (D_MODEL,)),
    }


def _fwd_reference(x, ev_w_in, ev_g_cq, ev_w_uq, ev_g_ckv, ev_w_ukv, ev_w_out,
              od_w_qkv, od_rel_bias, od_w_out, g_mix, g_ffn, w_gate, w_up, w_down, g_final):
    h = x
    for layer in range(DEPTH):
        u = rms_norm(h, g_mix[layer])
        if layer % 2 == 0:
            i = layer // 2
            h = h + mla_stick_breaking_mixer(u, ev_w_in[i], ev_g_cq[i], ev_w_uq[i],
                                             ev_g_ckv[i], ev_w_ukv[i], ev_w_out[i])
        else:
            i = layer // 2
            h = h + chunk_band_mixer(u, od_w_qkv[i], od_rel_bias[i], od_w_out[i])
        u = rms_norm(h, g_ffn[layer])
        h = h + swiglu(u, w_gate[layer], w_up[layer], w_down[layer])
    return rms_norm(h, g_final)


import jax as _jax
import jax.numpy as _jnp

TWIN_FORMAT = 'train_step'
FWD_PARAMS = ['x', 'ev_w_in', 'ev_g_cq', 'ev_w_uq', 'ev_g_ckv', 'ev_w_ukv', 'ev_w_out', 'od_w_qkv', 'od_rel_bias', 'od_w_out', 'g_mix', 'g_ffn', 'w_gate', 'w_up', 'w_down', 'g_final']
TWIN_WEIGHTS = ['ev_w_in', 'ev_g_cq', 'ev_w_uq', 'ev_g_ckv', 'ev_w_ukv', 'ev_w_out', 'od_w_qkv', 'od_rel_bias', 'od_w_out', 'g_mix', 'g_ffn', 'w_gate', 'w_up', 'w_down', 'g_final']
TWIN_DIFF_INPUT = 'x'
TWIN_INPUTS = ['x', 'ev_w_in', 'ev_g_cq', 'ev_w_uq', 'ev_g_ckv', 'ev_w_ukv', 'ev_w_out', 'od_w_qkv', 'od_rel_bias', 'od_w_out', 'g_mix', 'g_ffn', 'w_gate', 'w_up', 'w_down', 'g_final', 'loss_target', 'm_ev_w_in', 'm_ev_g_cq', 'm_ev_w_uq', 'm_ev_g_ckv', 'm_ev_w_ukv', 'm_ev_w_out', 'm_od_w_qkv', 'm_od_rel_bias', 'm_od_w_out', 'm_g_mix', 'm_g_ffn', 'm_w_gate', 'm_w_up', 'm_w_down', 'm_g_final', 'v_ev_w_in', 'v_ev_g_cq', 'v_ev_w_uq', 'v_ev_g_ckv', 'v_ev_w_ukv', 'v_ev_w_out', 'v_od_w_qkv', 'v_od_rel_bias', 'v_od_w_out', 'v_g_mix', 'v_g_ffn', 'v_w_gate', 'v_w_up', 'v_w_down', 'v_g_final']
TWIN_OUTPUTS = ['loss', 'grad_x', 'grad_ev_w_in', 'grad_ev_g_cq', 'grad_ev_w_uq', 'grad_ev_g_ckv', 'grad_ev_w_ukv', 'grad_ev_w_out', 'grad_od_w_qkv', 'grad_od_rel_bias', 'grad_od_w_out', 'grad_g_mix', 'grad_g_ffn', 'grad_w_gate', 'grad_w_up', 'grad_w_down', 'grad_g_final', 'delta_ev_w_in', 'delta_ev_g_cq', 'delta_ev_w_uq', 'delta_ev_g_ckv', 'delta_ev_w_ukv', 'delta_ev_w_out', 'delta_od_w_qkv', 'delta_od_rel_bias', 'delta_od_w_out', 'delta_g_mix', 'delta_g_ffn', 'delta_w_gate', 'delta_w_up', 'delta_w_down', 'delta_g_final', 'new_m_ev_w_in', 'new_m_ev_g_cq', 'new_m_ev_w_uq', 'new_m_ev_g_ckv', 'new_m_ev_w_ukv', 'new_m_ev_w_out', 'new_m_od_w_qkv', 'new_m_od_rel_bias', 'new_m_od_w_out', 'new_m_g_mix', 'new_m_g_ffn', 'new_m_w_gate', 'new_m_w_up', 'new_m_w_down', 'new_m_g_final', 'new_v_ev_w_in', 'new_v_ev_g_cq', 'new_v_ev_w_uq', 'new_v_ev_g_ckv', 'new_v_ev_w_ukv', 'new_v_ev_w_out', 'new_v_od_w_qkv', 'new_v_od_rel_bias', 'new_v_od_w_out', 'new_v_g_mix', 'new_v_g_ffn', 'new_v_w_gate', 'new_v_w_up', 'new_v_w_down', 'new_v_g_final']
TWIN_LEAF_KINDS = {'loss': 'loss', 'grad_x': 'grad_x', 'grad_ev_w_in': 'grad_w', 'grad_ev_g_cq': 'grad_w', 'grad_ev_w_uq': 'grad_w', 'grad_ev_g_ckv': 'grad_w', 'grad_ev_w_ukv': 'grad_w', 'grad_ev_w_out': 'grad_w', 'grad_od_w_qkv': 'grad_w', 'grad_od_rel_bias': 'grad_w', 'grad_od_w_out': 'grad_w', 'grad_g_mix': 'grad_w', 'grad_g_ffn': 'grad_w', 'grad_w_gate': 'grad_w', 'grad_w_up': 'grad_w', 'grad_w_down': 'grad_w', 'grad_g_final': 'grad_w', 'delta_ev_w_in': 'delta_w', 'delta_ev_g_cq': 'delta_w', 'delta_ev_w_uq': 'delta_w', 'delta_ev_g_ckv': 'delta_w', 'delta_ev_w_ukv': 'delta_w', 'delta_ev_w_out': 'delta_w', 'delta_od_w_qkv': 'delta_w', 'delta_od_rel_bias': 'delta_w', 'delta_od_w_out': 'delta_w', 'delta_g_mix': 'delta_w', 'delta_g_ffn': 'delta_w', 'delta_w_gate': 'delta_w', 'delta_w_up': 'delta_w', 'delta_w_down': 'delta_w', 'delta_g_final': 'delta_w', 'new_m_ev_w_in': 'new_m', 'new_m_ev_g_cq': 'new_m', 'new_m_ev_w_uq': 'new_m', 'new_m_ev_g_ckv': 'new_m', 'new_m_ev_w_ukv': 'new_m', 'new_m_ev_w_out': 'new_m', 'new_m_od_w_qkv': 'new_m', 'new_m_od_rel_bias': 'new_m', 'new_m_od_w_out': 'new_m', 'new_m_g_mix': 'new_m', 'new_m_g_ffn': 'new_m', 'new_m_w_gate': 'new_m', 'new_m_w_up': 'new_m', 'new_m_w_down': 'new_m', 'new_m_g_final': 'new_m', 'new_v_ev_w_in': 'new_v', 'new_v_ev_g_cq': 'new_v', 'new_v_ev_w_uq': 'new_v', 'new_v_ev_g_ckv': 'new_v', 'new_v_ev_w_ukv': 'new_v', 'new_v_ev_w_out': 'new_v', 'new_v_od_w_qkv': 'new_v', 'new_v_od_rel_bias': 'new_v', 'new_v_od_w_out': 'new_v', 'new_v_g_mix': 'new_v', 'new_v_g_ffn': 'new_v', 'new_v_w_gate': 'new_v', 'new_v_w_up': 'new_v', 'new_v_w_down': 'new_v', 'new_v_g_final': 'new_v'}


def _forward(args):
    return _fwd_reference(*[args[k] for k in FWD_PARAMS])


def _output_shape():
    out = _jax.eval_shape(lambda: _forward(_fwd_setup_inputs(0)))
    return out.shape, out.dtype

N_MICROBATCH = 1
ADAM_LR = 0.001
ADAM_B1 = 0.9
ADAM_B2 = 0.999
ADAM_EPS = 1e-08
ADAM_WD = 0.01
ADAM_STEP = 10
PER_EXAMPLE_BATCH_AXIS = {'x': 0, 'loss_target': 0}
SHARED_INPUTS = []
_WEIGHT_DTYPES = {'ev_w_in': _jnp.float32, 'ev_g_cq': _jnp.float32, 'ev_w_uq': _jnp.float32, 'ev_g_ckv': _jnp.float32, 'ev_w_ukv': _jnp.float32, 'ev_w_out': _jnp.float32, 'od_w_qkv': _jnp.float32, 'od_rel_bias': _jnp.float32, 'od_w_out': _jnp.float32, 'g_mix': _jnp.float32, 'g_ffn': _jnp.float32, 'w_gate': _jnp.float32, 'w_up': _jnp.float32, 'w_down': _jnp.float32, 'g_final': _jnp.float32}
MOMENT_SCALE = {'ev_w_in': 5.505353e-02, 'ev_g_cq': 2.381504e-02, 'ev_w_uq': 1.711554e-02, 'ev_g_ckv': 4.449017e-02, 'ev_w_ukv': 2.085936e-02, 'ev_w_out': 6.596855e-02, 'od_w_qkv': 1.642169e-02, 'od_rel_bias': 5.420092e-03, 'od_w_out': 1.791145e-02, 'g_mix': 6.078837e-02, 'g_ffn': 8.706690e-02, 'w_gate': 3.755674e-02, 'w_up': 3.639940e-02, 'w_down': 6.050160e-02, 'g_final': 1.601513e+01}


def _to_microbatches(a, axis):
    t = _jnp.moveaxis(a, axis, 0)
    t = t.reshape((N_MICROBATCH, t.shape[0] // N_MICROBATCH) + t.shape[1:])
    return _jnp.moveaxis(t, 1, axis + 1)


def setup_inputs(seed: int = 0) -> dict:
    inp = _fwd_setup_inputs(seed)
    key = _jax.random.fold_in(_jax.random.key(seed), 7919)
    shape, _ = _output_shape()
    out = dict(inp)
    out["loss_target"] = _jax.random.normal(_jax.random.fold_in(key, 0), shape, _jnp.float32)
    for i, name in enumerate(TWIN_WEIGHTS):
        w = inp[name].astype(_jnp.float32)
        if MOMENT_SCALE is None:
            s = _jnp.sqrt(_jnp.mean(_jnp.square(w)) + 1e-30)
        else:
            s = MOMENT_SCALE[name]
        km, kv = _jax.random.split(_jax.random.fold_in(key, i + 1))
        out[name] = w
        out["m_" + name] = s * _jax.random.normal(km, w.shape, _jnp.float32)
        out["v_" + name] = (s * s) * _jax.random.uniform(kv, w.shape, _jnp.float32, 0.5, 1.5)
    if N_MICROBATCH > 1:
        for name, axis in PER_EXAMPLE_BATCH_AXIS.items():
            out[name] = _to_microbatches(out[name], axis)
    return {'x': out['x'], 'ev_w_in': out['ev_w_in'], 'ev_g_cq': out['ev_g_cq'], 'ev_w_uq': out['ev_w_uq'], 'ev_g_ckv': out['ev_g_ckv'], 'ev_w_ukv': out['ev_w_ukv'], 'ev_w_out': out['ev_w_out'], 'od_w_qkv': out['od_w_qkv'], 'od_rel_bias': out['od_rel_bias'], 'od_w_out': out['od_w_out'], 'g_mix': out['g_mix'], 'g_ffn': out['g_ffn'], 'w_gate': out['w_gate'], 'w_up': out['w_up'], 'w_down': out['w_down'], 'g_final': out['g_final'], 'loss_target': out['loss_target'], 'm_ev_w_in': out['m_ev_w_in'], 'm_ev_g_cq': out['m_ev_g_cq'], 'm_ev_w_uq': out['m_ev_w_uq'], 'm_ev_g_ckv': out['m_ev_g_ckv'], 'm_ev_w_ukv': out['m_ev_w_ukv'], 'm_ev_w_out': out['m_ev_w_out'], 'm_od_w_qkv': out['m_od_w_qkv'], 'm_od_rel_bias': out['m_od_rel_bias'], 'm_od_w_out': out['m_od_w_out'], 'm_g_mix': out['m_g_mix'], 'm_g_ffn': out['m_g_ffn'], 'm_w_gate': out['m_w_gate'], 'm_w_up': out['m_w_up'], 'm_w_down': out['m_w_down'], 'm_g_final': out['m_g_final'], 'v_ev_w_in': out['v_ev_w_in'], 'v_ev_g_cq': out['v_ev_g_cq'], 'v_ev_w_uq': out['v_ev_w_uq'], 'v_ev_g_ckv': out['v_ev_g_ckv'], 'v_ev_w_ukv': out['v_ev_w_ukv'], 'v_ev_w_out': out['v_ev_w_out'], 'v_od_w_qkv': out['v_od_w_qkv'], 'v_od_rel_bias': out['v_od_rel_bias'], 'v_od_w_out': out['v_od_w_out'], 'v_g_mix': out['v_g_mix'], 'v_g_ffn': out['v_g_ffn'], 'v_w_gate': out['v_w_gate'], 'v_w_up': out['v_w_up'], 'v_w_down': out['v_w_down'], 'v_g_final': out['v_g_final']}


def _loss(weights, diff, rest, loss_target):
    with _jax.named_scope("forward"):
        args = {**rest, TWIN_DIFF_INPUT: diff, **{k: w.astype(_WEIGHT_DTYPES[k]) for k, w in weights.items()}}
        y = _forward(args)
    with _jax.named_scope("loss_head"):
        err = _jnp.square(y.astype(_jnp.float32) - loss_target)
        return 0.5 * _jnp.sum(_jnp.mean(err, axis=-1)) if err.ndim else 0.5 * err


def _adamw(w, g, m, v):
    m = ADAM_B1 * m + (1.0 - ADAM_B1) * g
    v = ADAM_B2 * v + (1.0 - ADAM_B2) * _jnp.square(g)
    m_hat = m / (1.0 - ADAM_B1 ** ADAM_STEP)
    v_hat = v / (1.0 - ADAM_B2 ** ADAM_STEP)
    delta = -ADAM_LR * (m_hat / (_jnp.sqrt(v_hat) + ADAM_EPS) + ADAM_WD * w)
    return delta, m, v


def reference(x, ev_w_in, ev_g_cq, ev_w_uq, ev_g_ckv, ev_w_ukv, ev_w_out, od_w_qkv, od_rel_bias, od_w_out, g_mix, g_ffn, w_gate, w_up, w_down, g_final, loss_target, m_ev_w_in, m_ev_g_cq, m_ev_w_uq, m_ev_g_ckv, m_ev_w_ukv, m_ev_w_out, m_od_w_qkv, m_od_rel_bias, m_od_w_out, m_g_mix, m_g_ffn, m_w_gate, m_w_up, m_w_down, m_g_final, v_ev_w_in, v_ev_g_cq, v_ev_w_uq, v_ev_g_ckv, v_ev_w_ukv, v_ev_w_out, v_od_w_qkv, v_od_rel_bias, v_od_w_out, v_g_mix, v_g_ffn, v_w_gate, v_w_up, v_w_down, v_g_final):
    given = dict(x=x, ev_w_in=ev_w_in, ev_g_cq=ev_g_cq, ev_w_uq=ev_w_uq, ev_g_ckv=ev_g_ckv, ev_w_ukv=ev_w_ukv, ev_w_out=ev_w_out, od_w_qkv=od_w_qkv, od_rel_bias=od_rel_bias, od_w_out=od_w_out, g_mix=g_mix, g_ffn=g_ffn, w_gate=w_gate, w_up=w_up, w_down=w_down, g_final=g_final, loss_target=loss_target, m_ev_w_in=m_ev_w_in, m_ev_g_cq=m_ev_g_cq, m_ev_w_uq=m_ev_w_uq, m_ev_g_ckv=m_ev_g_ckv, m_ev_w_ukv=m_ev_w_ukv, m_ev_w_out=m_ev_w_out, m_od_w_qkv=m_od_w_qkv, m_od_rel_bias=m_od_rel_bias, m_od_w_out=m_od_w_out, m_g_mix=m_g_mix, m_g_ffn=m_g_ffn, m_w_gate=m_w_gate, m_w_up=m_w_up, m_w_down=m_w_down, m_g_final=m_g_final, v_ev_w_in=v_ev_w_in, v_ev_g_cq=v_ev_g_cq, v_ev_w_uq=v_ev_w_uq, v_ev_g_ckv=v_ev_g_ckv, v_ev_w_ukv=v_ev_w_ukv, v_ev_w_out=v_ev_w_out, v_od_w_qkv=v_od_w_qkv, v_od_rel_bias=v_od_rel_bias, v_od_w_out=v_od_w_out, v_g_mix=v_g_mix, v_g_ffn=v_g_ffn, v_w_gate=v_w_gate, v_w_up=v_w_up, v_w_down=v_w_down, v_g_final=v_g_final)
    weights = {n: given[n] for n in TWIN_WEIGHTS}
    shared = {n: given[n] for n in SHARED_INPUTS}
    per_example = {n: given[n] for n in ['x']}
    grad_fn = _jax.value_and_grad(_loss, argnums=(0, 1))

    def one_microbatch(ex, loss_target):
        ex = dict(ex)
        diff = ex.pop(TWIN_DIFF_INPUT)
        return grad_fn(weights, diff, {**shared, **ex}, loss_target)

    if N_MICROBATCH == 1:
        loss, (grad_w, grad_x) = one_microbatch(per_example, given["loss_target"])
    else:
        def body(carry, xs):
            loss_sum, grad_sum = carry
            l_k, (gw_k, gx_k) = one_microbatch(xs[0], xs[1])
            with _jax.named_scope("update"):
                return (loss_sum + l_k, _jax.tree.map(_jnp.add, grad_sum, gw_k)), gx_k

        init = (_jnp.zeros((), _jnp.float32), _jax.tree.map(_jnp.zeros_like, weights))
        (loss, grad_w), grad_x = _jax.lax.scan(body, init, (per_example, given["loss_target"]))
    with _jax.named_scope("update"):
        delta_w, new_m, new_v = {}, {}, {}
        for n in TWIN_WEIGHTS:
            delta_w[n], new_m[n], new_v[n] = _adamw(weights[n], grad_w[n], given["m_" + n], given["v_" + n])
    return (loss, grad_x, *[grad_w[n] for n in TWIN_WEIGHTS], *[delta_w[n] for n in TWIN_WEIGHTS],
            *[new_m[n] for n in TWIN_WEIGHTS], *[new_v[n] for n in TWIN_WEIGHTS])
```

```python
import functools

import numpy as np
import jax
import jax.numpy as jnp
from jax import lax
from jax.experimental import pallas as pl
from jax.experimental.pallas import tpu as pltpu

F32 = jnp.float32
BF16 = jnp.bfloat16

D_MODEL = 1024
CHUNK = 64
MLA_HEADS = 8
MLA_NOPE = 64
MLA_ROPE = 32
MLA_V = 64
Q_LORA = 384
KV_LORA = 256
ROPE_THETA = 10000.0
SB_HEADS = 8
SB_DIM = 64
C_HEADS = 16
C_DIM = 64
LEFT_CHUNKS = 8
REL_CLIP = 256
D_FF = 2816
RMS_EPS = 1e-6
ADAM_LR = 0.001
ADAM_B1 = 0.9
ADAM_B2 = 0.999
ADAM_EPS = 1e-08
ADAM_WD = 0.01
ADAM_STEP = 10

N_DEV = 8
LANES = 128
VMEM_LIMIT = 56 * 1024 * 1024
NEG = -1e30
PAD_KEYS = LEFT_CHUNKS * CHUNK
BAND_TQ = 128
BAND_W = BAND_TQ + PAD_KEYS
TOEP_W = BAND_W + BAND_TQ

NN = (((1,), (0,)), ((), ()))
NT = (((1,), (1,)), ((), ()))
TN = (((0,), (0,)), ((), ()))


def _dot(a, b, dn):
    return lax.dot_general(a, b, dn, preferred_element_type=F32)


def _pick(dim, pref):
    if dim <= pref:
        return dim
    best = None
    for t in range(LANES, pref + 1, LANES):
        if dim % t == 0:
            best = t
    assert best is not None, (dim, pref)
    return best


def _params(sem):
    return pltpu.CompilerParams(dimension_semantics=sem, vmem_limit_bytes=VMEM_LIMIT)


def _mm(a, b, dims="nn", res=None, out_dtype=F32, name="mm"):
    if dims == "nn":
        (M, K), (K2, N) = a.shape, b.shape
    elif dims == "nt":
        (M, K), (N, K2) = a.shape, b.shape
    else:
        (K, M), (K2, N) = a.shape, b.shape
    assert K == K2, (a.shape, b.shape, dims)
    tm, tn, tk = _pick(M, 512), _pick(N, 1408), _pick(K, 1408)
    nk = K // tk
    dn = {"nn": NN, "nt": NT, "tn": TN}[dims]
    has_res = res is not None

    def body(*refs):
        if has_res:
            a_ref, b_ref, r_ref, o_ref, acc = refs
        else:
            a_ref, b_ref, o_ref, acc = refs
        k = pl.program_id(2)

        @pl.when(k == 0)
        def _():
            acc[...] = jnp.zeros_like(acc)

        acc[...] += _dot(a_ref[...].astype(BF16), b_ref[...].astype(BF16), dn)

        @pl.when(k == nk - 1)
        def _():
            r = acc[...]
            if has_res:
                r = r + r_ref[...]
            o_ref[...] = r.astype(out_dtype)

    a_spec = (pl.BlockSpec((tk, tm), lambda i, j, k: (k, i)) if dims == "tn"
              else pl.BlockSpec((tm, tk), lambda i, j, k: (i, k)))
    b_spec = (pl.BlockSpec((tn, tk), lambda i, j, k: (j, k)) if dims == "nt"
              else pl.BlockSpec((tk, tn), lambda i, j, k: (k, j)))
    o_spec = pl.BlockSpec((tm, tn), lambda i, j, k: (i, j))
    in_specs = [a_spec, b_spec] + ([o_spec] if has_res else [])
    args = (a, b) + ((res,) if has_res else ())
    return pl.pallas_call(
        body, name=name, grid=(M // tm, N // tn, nk),
        in_specs=in_specs, out_specs=o_spec,
        out_shape=jax.ShapeDtypeStruct((M, N), out_dtype),
        scratch_shapes=[pltpu.VMEM((tm, tn), F32)],
        compiler_params=_params(("parallel", "parallel", "arbitrary")),
    )(*args)


def _rms_fwd(x, g, out_dtype=BF16, name="rms_fwd"):
    T, Fd = x.shape
    tm = _pick(T, 256)

    def body(x_ref, g_ref, o_ref):
        xv = x_ref[...]
        r = lax.rsqrt(jnp.mean(xv * xv, axis=-1, keepdims=True) + RMS_EPS)
        o_ref[...] = (xv * r * g_ref[...]).astype(out_dtype)

    return pl.pallas_call(
        body, name=name, grid=(T // tm,),
        in_specs=[pl.BlockSpec((tm, Fd), lambda i: (i, 0)), pl.BlockSpec((1, Fd), lambda i: (0, 0))],
        out_specs=pl.BlockSpec((tm, Fd), lambda i: (i, 0)),
        out_shape=jax.ShapeDtypeStruct((T, Fd), out_dtype),
        compiler_params=_params(("parallel",)),
    )(x, g)


def _rms_bwd(x, g, dy, dres=None, name="rms_bwd"):
    T, Fd = x.shape
    tm = _pick(T, 256)
    has_res = dres is not None

    def body(*refs):
        if has_res:
            x_ref, g_ref, dy_ref, r_ref, dx_ref, dg_ref = refs
        else:
            x_ref, g_ref, dy_ref, dx_ref, dg_ref = refs
        xv, dyv = x_ref[...], dy_ref[...]
        r = lax.rsqrt(jnp.mean(xv * xv, axis=-1, keepdims=True) + RMS_EPS)
        gdy = dyv * g_ref[...]
        dot = jnp.mean(xv * gdy, axis=-1, keepdims=True)
        dx = r * gdy - xv * (r * r * r * dot)
        if has_res:
            dx = dx + r_ref[...]
        dx_ref[...] = dx

        @pl.when(pl.program_id(0) == 0)
        def _():
            dg_ref[...] = jnp.zeros_like(dg_ref)

        dg_ref[...] += jnp.sum(dyv * xv * r, axis=0, keepdims=True)

    row = pl.BlockSpec((tm, Fd), lambda i: (i, 0))
    vec = pl.BlockSpec((1, Fd), lambda i: (0, 0))
    in_specs = [row, vec, row] + ([row] if has_res else [])
    args = (x, g, dy) + ((dres,) if has_res else ())
    return pl.pallas_call(
        body, name=name, grid=(T // tm,),
        in_specs=in_specs, out_specs=[row, vec],
        out_shape=[jax.ShapeDtypeStruct((T, Fd), F32), jax.ShapeDtypeStruct((1, Fd), F32)],
        compiler_params=_params(("arbitrary",)),
    )(*args)


def _loss_head(h, g, target, name="loss_head"):
    T, Fd = h.shape
    tm = _pick(T, 256)

    def body(h_ref, g_ref, t_ref, loss_ref, dh_ref, dg_ref):
        xv = h_ref[...]
        r = lax.rsqrt(jnp.mean(xv * xv, axis=-1, keepdims=True) + RMS_EPS)
        diff = xv * r * g_ref[...] - t_ref[...]
        part = 0.5 * jnp.sum(jnp.mean(diff * diff, axis=-1, keepdims=True), axis=0, keepdims=True)
        dyv = diff * (1.0 / Fd)
        gdy = dyv * g_ref[...]
        dot = jnp.mean(xv * gdy, axis=-1, keepdims=True)
        dh_ref[...] = r * gdy - xv * (r * r * r * dot)

        @pl.when(pl.program_id(0) == 0)
        def _():
            dg_ref[...] = jnp.zeros_like(dg_ref)
            loss_ref[...] = jnp.zeros_like(loss_ref)

        dg_ref[...] += jnp.sum(dyv * xv * r, axis=0, keepdims=True)
        loss_ref[...] += jnp.broadcast_to(part, loss_ref.shape)

    row = pl.BlockSpec((tm, Fd), lambda i: (i, 0))
    vec = pl.BlockSpec((1, Fd), lambda i: (0, 0))
    return pl.pallas_call(
        body, name=name, grid=(T // tm,),
        in_specs=[row, vec, row],
        out_specs=[pl.BlockSpec((1, LANES), lambda i: (0, 0)), row, vec],
        out_shape=[jax.ShapeDtypeStruct((1, LANES), F32), jax.ShapeDtypeStruct((T, Fd), F32),
                   jax.ShapeDtypeStruct((1, Fd), F32)],
        compiler_params=_params(("arbitrary",)),
    )(h, g, target)


def _swiglu_fwd(ab, name="swiglu_fwd"):
    T, two_f = ab.shape
    Fh = two_f // 2
    tm, tc = _pick(T, 256), _pick(Fh, 1408)
    nc = Fh // tc

    def body(a_ref, b_ref, s_ref):
        a = a_ref[...]
        s_ref[...] = (a * jax.nn.sigmoid(a) * b_ref[...]).astype(BF16)

    return pl.pallas_call(
        body, name=name, grid=(T // tm, nc),
        in_specs=[pl.BlockSpec((tm, tc), lambda i, j: (i, j)),
                  pl.BlockSpec((tm, tc), lambda i, j: (i, j + nc))],
        out_specs=pl.BlockSpec((tm, tc), lambda i, j: (i, j)),
        out_shape=jax.ShapeDtypeStruct((T, Fh), BF16),
        compiler_params=_params(("parallel", "parallel")),
    )(ab, ab)


def _swiglu_bwd(ab, ds, name="swiglu_bwd"):
    T, two_f = ab.shape
    Fh = two_f // 2
    tm, tc = _pick(T, 256), _pick(Fh, 1408)
    nc = Fh // tc

    def body(a_ref, b_ref, ds_ref, o_ref):
        j = pl.program_id(1)
        a, dsv = a_ref[...], ds_ref[...]
        sig = jax.nn.sigmoid(a)

        @pl.when(j < nc)
        def _():
            o_ref[...] = dsv * b_ref[...] * (sig * (1.0 + a * (1.0 - sig)))

        @pl.when(j >= nc)
        def _():
            o_ref[...] = dsv * (a * sig)

    return pl.pallas_call(
        body, name=name, grid=(T // tm, 2 * nc),
        in_specs=[pl.BlockSpec((tm, tc), lambda i, j: (i, j % nc)),
                  pl.BlockSpec((tm, tc), lambda i, j: (i, j % nc + nc)),
                  pl.BlockSpec((tm, tc), lambda i, j: (i, j % nc))],
        out_specs=pl.BlockSpec((tm, tc), lambda i, j: (i, j)),
        out_shape=jax.ShapeDtypeStruct((T, two_f), F32),
        compiler_params=_params(("parallel", "parallel")),
    )(ab, ab, ds)


def _rope(x, cos_t, sin_t, col0, ncols, out_dtype, name="rope"):
    T = x.shape[0]
    wt = cos_t.shape[1]
    tm = _pick(T, 256)
    nb = ncols * LANES // wt
    half = MLA_ROPE // 2

    def body(x_ref, c_ref, s_ref, o_ref):
        xv = x_ref[...].astype(F32)
        lane = lax.broadcasted_iota(jnp.int32, xv.shape, 1)
        first = (lane & (MLA_ROPE - 1)) < half
        swapped = jnp.where(first, pltpu.roll(xv, wt - half, 1), pltpu.roll(xv, half, 1))
        o_ref[...] = (xv * c_ref[...] + swapped * s_ref[...]).astype(out_dtype)

    off = col0 * LANES // wt
    return pl.pallas_call(
        body, name=name, grid=(T // tm, nb),
        in_specs=[pl.BlockSpec((tm, wt), lambda i, j: (i, j + off)),
                  pl.BlockSpec((tm, wt), lambda i, j: (i, 0)),
                  pl.BlockSpec((tm, wt), lambda i, j: (i, 0))],
        out_specs=pl.BlockSpec((tm, wt), lambda i, j: (i, j)),
        out_shape=jax.ShapeDtypeStruct((T, ncols * LANES), out_dtype),
        compiler_params=_params(("parallel", "parallel")),
    )(x, cos_t, sin_t)


ATT_T = 256


def _mla_masks(shape):
    lane = lax.broadcasted_iota(jnp.int32, shape, 1)
    m0 = (lane < 64) | ((lane >= 128) & (lane < 160))
    m1 = ((lane >= 64) & (lane < 128)) | ((lane >= 160) & (lane < 192))
    return m0, m1


def _chunk_ok(m_idx, kb, tq, tk):
    tpos = m_idx * tq + lax.broadcasted_iota(jnp.int32, (tq, tk), 0)
    spos = kb * tk + lax.broadcasted_iota(jnp.int32, (tq, tk), 1)
    return (spos >> 6) <= (tpos >> 6)


def _mla_fwd(q, kv, kr, name="mla_fwd"):
    T = q.shape[0]
    tq = tk = _pick(T, ATT_T)
    npair = MLA_HEADS // 2
    scale = (MLA_NOPE + MLA_ROPE) ** -0.5

    def body(q_ref, kn_ref, v_ref, kr_ref, o_ref, lse_ref):
        m_idx = pl.program_id(1)
        qv = q_ref[...]
        m0, m1 = _mla_masks(qv.shape)
        qh = (jnp.where(m0, qv, 0).astype(BF16), jnp.where(m1, qv, 0).astype(BF16))

        def step(kb, carry):
            ks = pl.ds(pl.multiple_of(kb * tk, tk), tk)
            kcat = jnp.concatenate([kn_ref[ks, :], kr_ref[ks, :]], axis=1)
            vv = v_ref[ks, :]
            ok = _chunk_ok(m_idx, kb, tq, tk)
            out = []
            for h in range(2):
                mx, l, acc = carry[3 * h:3 * h + 3]
                s = jnp.where(ok, _dot(qh[h], kcat, NT) * scale, NEG)
                mn = jnp.maximum(mx, jnp.max(s, axis=-1, keepdims=True))
                alpha = jnp.exp(mx - mn)
                p = jnp.exp(s - mn)
                l = alpha * l + jnp.sum(p, axis=-1, keepdims=True)
                acc = alpha * acc + _dot(p.astype(BF16), vv, NN)
                out += [mn, l, acc]
            return tuple(out)

        init = (jnp.full((tq, 1), NEG, F32), jnp.zeros((tq, 1), F32), jnp.zeros((tq, LANES), F32)) * 2
        res = lax.fori_loop(0, m_idx + 1, step, init)
        lane = lax.broadcasted_iota(jnp.int32, (tq, LANES), 1)
        o0 = res[2] / res[1]
        o1 = res[5] / res[4]
        o_ref[...] = jnp.where(lane < 64, o0, o1).astype(o_ref.dtype)
        lse_ref[...] = jnp.where(lane < 64, res[0] + jnp.log(res[1]), res[3] + jnp.log(res[4]))

    full = lambda col: pl.BlockSpec((T, LANES), col)
    return pl.pallas_call(
        body, name=name, grid=(npair, T // tq),
        in_specs=[pl.BlockSpec((tq, 2 * LANES), lambda p, m: (m, p)),
                  full(lambda p, m: (0, p)), full(lambda p, m: (0, npair + p)), full(lambda p, m: (0, 0))],
        out_specs=[pl.BlockSpec((tq, LANES), lambda p, m: (m, p)),
                   pl.BlockSpec((tq, LANES), lambda p, m: (m, p))],
        out_shape=[jax.ShapeDtypeStruct((T, npair * LANES), BF16),
                   jax.ShapeDtypeStruct((T, npair * LANES), F32)],
        compiler_params=_params(("parallel", "arbitrary")),
    )(q, kv, kv, kr)


def _mla_bwd(q, kv, kr, o, lse, do, do_col0, name="mla_bwd"):
    T = q.shape[0]
    tq = tk = _pick(T, ATT_T)
    npair = MLA_HEADS // 2
    scale = (MLA_NOPE + MLA_ROPE) ** -0.5

    def body(q_ref, kn_ref, v_ref, kr_ref, o_ref, lse_ref, do_ref, dq_ref, dkn_ref, dv_ref, dkr_ref):
        p_idx, m_idx = pl.program_id(0), pl.program_id(1)

        @pl.when(m_idx == 0)
        def _():
            dkn_ref[...] = jnp.zeros_like(dkn_ref)
            dv_ref[...] = jnp.zeros_like(dv_ref)

        @pl.when((m_idx == 0) & (p_idx == 0))
        def _():
            dkr_ref[...] = jnp.zeros_like(dkr_ref)

        qv = q_ref[...]
        m0, m1 = _mla_masks(qv.shape)
        qh = (jnp.where(m0, qv, 0).astype(BF16), jnp.where(m1, qv, 0).astype(BF16))
        dov = do_ref[...].astype(F32)
        lane = lax.broadcasted_iota(jnp.int32, (tq, LANES), 1)
        h0 = lane < 64
        prod = dov * o_ref[...].astype(F32)
        delta = (jnp.sum(jnp.where(h0, prod, 0.0), axis=-1, keepdims=True),
                 jnp.sum(jnp.where(h0, 0.0, prod), axis=-1, keepdims=True))
        doh = (jnp.where(h0, dov, 0.0).astype(BF16), jnp.where(h0, 0.0, dov).astype(BF16))
        lsev = lse_ref[...]
        lse_h = (lsev[:, 0:1], lsev[:, 64:65])

        def step(kb, carry):
            ks = pl.ds(pl.multiple_of(kb * tk, tk), tk)
            kcat = jnp.concatenate([kn_ref[ks, :], kr_ref[ks, :]], axis=1)
            vv = v_ref[ks, :]
            ok = _chunk_ok(m_idx, kb, tq, tk)
            dkc = jnp.zeros((tk, 2 * LANES), F32)
            dvv = jnp.zeros((tk, LANES), F32)
            out = []
            for h in range(2):
                s = _dot(qh[h], kcat, NT) * scale
                p = jnp.where(ok, jnp.exp(s - lse_h[h]), 0.0)
                dp = _dot(doh[h], vv, NT)
                ds = (p * (dp - delta[h]) * scale).astype(BF16)
                out.append(carry[h] + _dot(ds, kcat, NN))
                dkc = dkc + _dot(ds, qh[h], TN)
                dvv = dvv + _dot(p.astype(BF16), doh[h], TN)
            dkn_ref[ks, :] += dkc[:, :LANES]
            dkr_ref[ks, :] += dkc[:, LANES:]
            dv_ref[ks, :] += dvv
            return tuple(out)

        init = (jnp.zeros((tq, 2 * LANES), F32),) * 2
        dq0, dq1 = lax.fori_loop(0, m_idx + 1, step, init)
        dq_ref[...] = jnp.where(m0, dq0, jnp.where(m1, dq1, 0.0))

    full = lambda col: pl.BlockSpec((T, LANES), col)
    blk = lambda col: pl.BlockSpec((tq, LANES), col)
    return pl.pallas_call(
        body, name=name, grid=(npair, T // tq),
        in_specs=[pl.BlockSpec((tq, 2 * LANES), lambda p, m: (m, p)),
                  full(lambda p, m: (0, p)), full(lambda p, m: (0, npair + p)), full(lambda p, m: (0, 0)),
                  blk(lambda p, m: (m, p)), blk(lambda p, m: (m, p)),
                  blk(lambda p, m: (m, do_col0 + p))],
        out_specs=[pl.BlockSpec((tq, 2 * LANES), lambda p, m: (m, p)),
                   full(lambda p, m: (0, p)), full(lambda p, m: (0, p)), full(lambda p, m: (0, 0))],
        out_shape=[jax.ShapeDtypeStruct((T, npair * 2 * LANES), F32),
                   jax.ShapeDtypeStruct((T, npair * LANES), F32),
                   jax.ShapeDtypeStruct((T, npair * LANES), F32),
                   jax.ShapeDtypeStruct((T, LANES), F32)],
        compiler_params=_params(("arbitrary", "arbitrary")),
    )(q, kv, kv, kr, o, lse, do)


def _split_dot(x, tri):
    hi = x.astype(BF16)
    lo = (x - hi.astype(F32)).astype(BF16)
    return _dot(hi, tri, NN) + _dot(lo, tri, NN)


def _sb_terms(qh, kk, m_idx, kb, tq, tk, scale):
    z = _dot(qh, kk, NT) * scale
    tpos = m_idx * tq + lax.broadcasted_iota(jnp.int32, (tq, tk), 0)
    spos = kb * tk + lax.broadcasted_iota(jnp.int32, (tq, tk), 1)
    before = spos < tpos
    sp = jnp.maximum(z, 0.0) + jnp.log(1.0 + jnp.exp(-jnp.abs(z)))
    lk = jnp.where(before, -sp, 0.0)
    return z, sp, lk, before


def _sb_fwd(qkv, col0, name="sb_fwd"):
    T = qkv.shape[0]
    tq = tk = _pick(T, ATT_T)
    npair = SB_HEADS // 2
    scale = SB_DIM ** -0.5

    def body(q_ref, k_ref, v_ref, o_ref):
        m_idx = pl.program_id(1)
        qv = q_ref[...].astype(BF16)
        lane = lax.broadcasted_iota(jnp.int32, (tq, LANES), 1)
        h0 = lane < 64
        qh = (jnp.where(h0, qv, 0).astype(BF16), jnp.where(h0, 0, qv).astype(BF16))
        row = lax.broadcasted_iota(jnp.int32, (tk, tk), 0)
        col = lax.broadcasted_iota(jnp.int32, (tk, tk), 1)
        later = (row > col).astype(BF16)

        def step(i, carry):
            kb = m_idx - i
            ks = pl.ds(pl.multiple_of(kb * tk, tk), tk)
            kk = k_ref[ks, :].astype(BF16)
            vv = v_ref[ks, :].astype(BF16)
            out = []
            for h in range(2):
                c, acc = carry[2 * h:2 * h + 2]
                z, sp, lk, before = _sb_terms(qh[h], kk, m_idx, kb, tq, tk, scale)
                a = (z - sp) + _split_dot(lk, later) + c
                w = jnp.where(before, jnp.exp(a), 0.0)
                out += [c + jnp.sum(lk, axis=-1, keepdims=True), acc + _dot(w.astype(BF16), vv, NN)]
            return tuple(out)

        init = (jnp.zeros((tq, 1), F32), jnp.zeros((tq, LANES), F32)) * 2
        res = lax.fori_loop(0, m_idx + 1, step, init)
        o_ref[...] = jnp.where(h0, res[1], res[3]).astype(o_ref.dtype)

    full = lambda col: pl.BlockSpec((T, LANES), col)
    return pl.pallas_call(
        body, name=name, grid=(npair, T // tq),
        in_specs=[pl.BlockSpec((tq, LANES), lambda p, m: (m, col0 + p)),
                  full(lambda p, m: (0, col0 + npair + p)), full(lambda p, m: (0, col0 + 2 * npair + p))],
        out_specs=pl.BlockSpec((tq, LANES), lambda p, m: (m, p)),
        out_shape=jax.ShapeDtypeStruct((T, npair * LANES), BF16),
        compiler_params=_params(("parallel", "arbitrary")),
    )(qkv, qkv, qkv)


def _sb_bwd(qkv, col0, do, do_col0, name="sb_bwd"):
    T = qkv.shape[0]
    tq = tk = _pick(T, ATT_T)
    npair = SB_HEADS // 2
    scale = SB_DIM ** -0.5

    def body(q_ref, k_ref, v_ref, do_ref, dq_ref, dk_ref, dv_ref, e_scr, z_scr, sp_scr):
        m_idx = pl.program_id(1)

        @pl.when(m_idx == 0)
        def _():
            dk_ref[...] = jnp.zeros_like(dk_ref)
            dv_ref[...] = jnp.zeros_like(dv_ref)

        qv = q_ref[...].astype(BF16)
        lane = lax.broadcasted_iota(jnp.int32, (tq, LANES), 1)
        h0 = lane < 64
        qh = (jnp.where(h0, qv, 0).astype(BF16), jnp.where(h0, 0, qv).astype(BF16))
        dov = do_ref[...].astype(F32)
        doh = (jnp.where(h0, dov, 0.0).astype(BF16), jnp.where(h0, 0.0, dov).astype(BF16))
        row = lax.broadcasted_iota(jnp.int32, (tk, tk), 0)
        col = lax.broadcasted_iota(jnp.int32, (tk, tk), 1)
        later = (row > col).astype(BF16)
        earlier = (row < col).astype(BF16)
        dq = []
        for h in range(2):
            def rl(i, c, h=h):
                kb = m_idx - i
                ks = pl.ds(pl.multiple_of(kb * tk, tk), tk)
                kk = k_ref[ks, :].astype(BF16)
                vv = v_ref[ks, :].astype(BF16)
                z, sp, lk, before = _sb_terms(qh[h], kk, m_idx, kb, tq, tk, scale)
                a = (z - sp) + _split_dot(lk, later) + c
                w = jnp.where(before, jnp.exp(a), 0.0)
                dw = _dot(doh[h], vv, NT)
                e_scr[kb] = w * dw
                z_scr[kb] = z
                sp_scr[kb] = sp
                dv_ref[ks, :] += _dot(w.astype(BF16), doh[h], TN)
                return c + jnp.sum(lk, axis=-1, keepdims=True)

            lax.fori_loop(0, m_idx + 1, rl, jnp.zeros((tq, 1), F32))

            def lr(kb, carry, h=h):
                esum, dqa = carry
                ks = pl.ds(pl.multiple_of(kb * tk, tk), tk)
                kk = k_ref[ks, :].astype(BF16)
                e, z, sp = e_scr[kb], z_scr[kb], sp_scr[kb]
                tpos = m_idx * tq + lax.broadcasted_iota(jnp.int32, (tq, tk), 0)
                spos = kb * tk + lax.broadcasted_iota(jnp.int32, (tq, tk), 1)
                prev = _split_dot(e, earlier) + esum
                dz = jnp.where(spos < tpos, e * jnp.exp(-sp) - jnp.exp(z - sp) * prev, 0.0)
                dzb = (dz * scale).astype(BF16)
                dk_ref[ks, :] += _dot(dzb, qh[h], TN)
                return esum + jnp.sum(e, axis=-1, keepdims=True), dqa + _dot(dzb, kk, NN)

            _, dqh = lax.fori_loop(0, m_idx + 1, lr, (jnp.zeros((tq, 1), F32), jnp.zeros((tq, LANES), F32)))
            dq.append(dqh)
        dq_ref[...] = jnp.where(h0, dq[0], dq[1])

    full = lambda col: pl.BlockSpec((T, LANES), col)
    blk = lambda col: pl.BlockSpec((tq, LANES), col)
    return pl.pallas_call(
        body, name=name, grid=(npair, T // tq),
        in_specs=[blk(lambda p, m: (m, col0 + p)),
                  full(lambda p, m: (0, col0 + npair + p)), full(lambda p, m: (0, col0 + 2 * npair + p)),
                  blk(lambda p, m: (m, do_col0 + p))],
        out_specs=[blk(lambda p, m: (m, p)), full(lambda p, m: (0, p)), full(lambda p, m: (0, p))],
        out_shape=[jax.ShapeDtypeStruct((T, npair * LANES), F32)] * 3,
        scratch_shapes=[pltpu.VMEM((T // tk, tq, tk), F32)] * 3,
        compiler_params=_params(("arbitrary", "arbitrary")),
    )(qkv, qkv, qkv, do)


def _band_valid(m_idx):
    qi = lax.broadcasted_iota(jnp.int32, (BAND_TQ, BAND_W), 0)
    j = lax.broadcasted_iota(jnp.int32, (BAND_TQ, BAND_W), 1)
    cq = (m_idx * BAND_TQ + qi) >> 6
    ckp = (m_idx * BAND_TQ + j) >> 6
    return (ckp >= LEFT_CHUNKS) & (ckp >= cq) & (ckp <= cq + LEFT_CHUNKS)


def _band_probs(qh, kw, bias, valid, scale):
    s = _dot(qh, kw, NT) * scale + bias
    s = jnp.where(valid, s, NEG)
    e = jnp.exp(s - jnp.max(s, axis=-1, keepdims=True))
    return e / jnp.sum(e, axis=-1, keepdims=True)


def _band_fwd(qkv, k_pad, v_pad, bias_w, name="band_fwd"):
    T = qkv.shape[0]
    npair = C_HEADS // 2
    scale = C_DIM ** -0.5

    def body(q_ref, k_ref, v_ref, b_ref, o_ref):
        m_idx = pl.program_id(1)
        win = pl.ds(pl.multiple_of(m_idx * BAND_TQ, BAND_TQ), BAND_W)
        kw, vw = k_ref[win, :], v_ref[win, :]
        qv = q_ref[...]
        lane = lax.broadcasted_iota(jnp.int32, (BAND_TQ, LANES), 1)
        h0 = lane < 64
        qh = (jnp.where(h0, qv, 0).astype(BF16), jnp.where(h0, 0, qv).astype(BF16))
        valid = _band_valid(m_idx)
        o = [_dot(_band_probs(qh[h], kw, b_ref[h], valid, scale).astype(BF16), vw, NN) for h in range(2)]
        o_ref[...] = jnp.where(h0, o[0], o[1]).astype(o_ref.dtype)

    Tp = T + PAD_KEYS
    return pl.pallas_call(
        body, name=name, grid=(npair, T // BAND_TQ),
        in_specs=[pl.BlockSpec((BAND_TQ, LANES), lambda p, m: (m, p)),
                  pl.BlockSpec((Tp, LANES), lambda p, m: (0, p)),
                  pl.BlockSpec((Tp, LANES), lambda p, m: (0, p)),
                  pl.BlockSpec((2, BAND_TQ, BAND_W), lambda p, m: (p, 0, 0))],
        out_specs=pl.BlockSpec((BAND_TQ, LANES), lambda p, m: (m, p)),
        out_shape=jax.ShapeDtypeStruct((T, npair * LANES), BF16),
        compiler_params=_params(("parallel", "arbitrary")),
    )(qkv, k_pad, v_pad, bias_w)


def _band_bwd(qkv, k_pad, v_pad, bias_w, do, name="band_bwd"):
    T = qkv.shape[0]
    npair = C_HEADS // 2
    scale = C_DIM ** -0.5

    def body(q_ref, k_ref, v_ref, b_ref, do_ref, dq_ref, dk_ref, dv_ref, db_ref):
        m_idx = pl.program_id(1)

        @pl.when(m_idx == 0)
        def _():
            dk_ref[...] = jnp.zeros_like(dk_ref)
            dv_ref[...] = jnp.zeros_like(dv_ref)
            db_ref[...] = jnp.zeros_like(db_ref)

        win = pl.ds(pl.multiple_of(m_idx * BAND_TQ, BAND_TQ), BAND_W)
        kw, vw = k_ref[win, :], v_ref[win, :]
        qv = q_ref[...]
        dov = do_ref[...].astype(F32)
        lane = lax.broadcasted_iota(jnp.int32, (BAND_TQ, LANES), 1)
        h0 = lane < 64
        qh = (jnp.where(h0, qv, 0).astype(BF16), jnp.where(h0, 0, qv).astype(BF16))
        doh = (jnp.where(h0, dov, 0.0).astype(BF16), jnp.where(h0, 0.0, dov).astype(BF16))
        valid = _band_valid(m_idx)
        dq = []
        dkw = jnp.zeros((BAND_W, LANES), F32)
        dvw = jnp.zeros((BAND_W, LANES), F32)
        for h in range(2):
            p = _band_probs(qh[h], kw, b_ref[h], valid, scale)
            dp = _dot(doh[h], vw, NT)
            dsb = p * (dp - jnp.sum(p * dp, axis=-1, keepdims=True))
            db_ref[h] += dsb
            dsq = (dsb * scale).astype(BF16)
            dq.append(_dot(dsq, kw, NN))
            dkw = dkw + _dot(dsq, qh[h], TN)
            dvw = dvw + _dot(p.astype(BF16), doh[h], TN)
        dq_ref[...] = jnp.where(h0, dq[0], dq[1])
        dk_ref[win, :] += dkw
        dv_ref[win, :] += dvw

    Tp = T + PAD_KEYS
    blk = lambda col: pl.BlockSpec((BAND_TQ, LANES), col)
    full = pl.BlockSpec((Tp, LANES), lambda p, m: (0, p))
    bias = pl.BlockSpec((2, BAND_TQ, BAND_W), lambda p, m: (p, 0, 0))
    return pl.pallas_call(
        body, name=name, grid=(npair, T // BAND_TQ),
        in_specs=[blk(lambda p, m: (m, p)), full, full, bias, blk(lambda p, m: (m, p))],
        out_specs=[blk(lambda p, m: (m, p)), full, full, bias],
        out_shape=[jax.ShapeDtypeStruct((T, npair * LANES), F32),
                   jax.ShapeDtypeStruct((Tp, npair * LANES), F32),
                   jax.ShapeDtypeStruct((Tp, npair * LANES), F32),
                   jax.ShapeDtypeStruct((C_HEADS, BAND_TQ, BAND_W), F32)],
        compiler_params=_params(("arbitrary", "arbitrary")),
    )(qkv, k_pad, v_pad, bias_w, do)


def _skew_bits(x, left):
    w = x.shape[1]
    row = lax.broadcasted_iota(jnp.int32, x.shape, 0)
    for b in range(BAND_TQ.bit_length() - 1):
        amt = (w - (1 << b)) if left else (1 << b)
        x = jnp.where((row >> b) & 1 == 1, pltpu.roll(x, amt, 1), x)
    return x


def _toeplitz(diag, name="toeplitz"):
    H = diag.shape[0]

    def body(d_ref, o_ref):
        x = jnp.broadcast_to(d_ref[0], (BAND_TQ, TOEP_W))
        o_ref[0] = _skew_bits(x, left=False)[:, BAND_TQ:]

    return pl.pallas_call(
        body, name=name, grid=(H,),
        in_specs=[pl.BlockSpec((1, 1, TOEP_W), lambda h: (h, 0, 0))],
        out_specs=pl.BlockSpec((1, BAND_TQ, BAND_W), lambda h: (h, 0, 0)),
        out_shape=jax.ShapeDtypeStruct((H, BAND_TQ, BAND_W), F32),
        compiler_params=_params(("parallel",)),
    )(diag.reshape(H, 1, TOEP_W))


def _toeplitz_bwd(dbias, name="toeplitz_bwd"):
    H = dbias.shape[0]

    def body(d_ref, o_ref):
        x = jnp.concatenate([jnp.zeros((BAND_TQ, BAND_TQ), F32), d_ref[0]], axis=1)
        o_ref[0] = jnp.sum(_skew_bits(x, left=True), axis=0, keepdims=True)

    return pl.pallas_call(
        body, name=name, grid=(H,),
        in_specs=[pl.BlockSpec((1, BAND_TQ, BAND_W), lambda h: (h, 0, 0))],
        out_specs=pl.BlockSpec((1, 1, TOEP_W), lambda h: (h, 0, 0)),
        out_shape=jax.ShapeDtypeStruct((H, 1, TOEP_W), F32),
        compiler_params=_params(("parallel",)),
    )(dbias).reshape(H, TOEP_W)


def _exchange(arrays, scatter, name):
    n = len(arrays)

    def body(*refs):
        ins, outs = refs[:n], refs[n:2 * n]
        send_sems, recv_sems, local_sems = refs[2 * n:]
        x, y, c = lax.axis_index("x"), lax.axis_index("y"), lax.axis_index("c")
        me = 4 * x + 2 * y + c
        copies = []
        for a in range(n):
            src_own = ins[a].at[me] if scatter[a] else ins[a]
            own = pltpu.make_async_copy(src_own, outs[a].at[me], local_sems.at[a])
            own.start()
            copies.append(own)
        remote = []
        for k in range(1, N_DEV):
            kx, ky, kc = (k >> 2) & 1, (k >> 1) & 1, k & 1
            peer = (1 - x if kx else x, 1 - y if ky else y, 1 - c if kc else c)
            peer_idx = 4 * peer[0] + 2 * peer[1] + peer[2]
            for a in range(n):
                src = ins[a].at[peer_idx] if scatter[a] else ins[a]
                cp = pltpu.make_async_remote_copy(
                    src_ref=src, dst_ref=outs[a].at[me],
                    send_sem=send_sems.at[a, k], recv_sem=recv_sems.at[a, k],
                    device_id=peer, device_id_type=pl.DeviceIdType.MESH)
                cp.start()
                remote.append((cp, pltpu.make_async_remote_copy(
                    src_ref=src, dst_ref=outs[a].at[peer_idx],
                    send_sem=send_sems.at[a, k], recv_sem=recv_sems.at[a, k],
                    device_id=peer, device_id_type=pl.DeviceIdType.MESH)))
        for cp, landing in remote:
            landing.wait_recv()
        for cp, landing in remote:
            cp.wait_send()
        for own in copies:
            own.wait()

    out_shape = [jax.ShapeDtypeStruct((N_DEV,) + (a.shape[1:] if s else a.shape), a.dtype)
                 for a, s in zip(arrays, scatter)]
    any_spec = pl.BlockSpec(memory_space=pl.ANY)
    return pl.pallas_call(
        body, name=name,
        in_specs=[any_spec] * n, out_specs=[any_spec] * n, out_shape=out_shape,
        scratch_shapes=[pltpu.SemaphoreType.DMA((n, N_DEV)), pltpu.SemaphoreType.DMA((n, N_DEV)),
                        pltpu.SemaphoreType.DMA((n,))],
        compiler_params=pltpu.CompilerParams(has_side_effects=True),
    )(*arrays)


def _adamw(w, parts, m, v, name="adamw"):
    R, C = w.shape
    tr = next(t for t in (256, 128, 64, 32, 16, 8) if R % t == 0)
    c1 = 1.0 - ADAM_B1 ** ADAM_STEP
    c2 = 1.0 - ADAM_B2 ** ADAM_STEP

    def body(w_ref, p_ref, m_ref, v_ref, g_ref, d_ref, nm_ref, nv_ref):
        g = p_ref[0].astype(F32)
        for i in range(1, N_DEV):
            g = g + p_ref[i].astype(F32)
        nm = ADAM_B1 * m_ref[...] + (1.0 - ADAM_B1) * g
        nv = ADAM_B2 * v_ref[...] + (1.0 - ADAM_B2) * (g * g)
        g_ref[...] = g
        nm_ref[...] = nm
        nv_ref[...] = nv
        d_ref[...] = -ADAM_LR * ((nm / c1) / (jnp.sqrt(nv / c2) + ADAM_EPS) + ADAM_WD * w_ref[...])

    blk = pl.BlockSpec((tr, C), lambda i: (i, 0))
    return pl.pallas_call(
        body, name=name, grid=(R // tr,),
        in_specs=[blk, pl.BlockSpec((N_DEV, tr, C), lambda i: (0, i, 0)), blk, blk],
        out_specs=[blk] * 4,
        out_shape=[jax.ShapeDtypeStruct((R, C), F32)] * 4,
        compiler_params=_params(("parallel",)),
    )(w, parts, m, v)


_O1 = Q_LORA
_O2 = _O1 + KV_LORA
_O3 = _O2 + MLA_ROPE
_NB = SB_HEADS * SB_DIM
IN_W = _O2 + LANES + 3 * _NB
COL_KR = _O2 // LANES
COL_SB = COL_KR + 1


def _w_in_local(w):
    kr = w[:, _O2:_O3]
    pad = jnp.zeros((w.shape[0], LANES - 2 * MLA_ROPE), w.dtype)
    return jnp.concatenate([w[:, :_O2], kr, kr, pad, w[:, _O3:]], axis=1)


def _w_in_grad(g):
    kr = g[:, _O2:_O2 + MLA_ROPE] + g[:, _O2 + MLA_ROPE:_O2 + 2 * MLA_ROPE]
    return jnp.concatenate([g[:, :_O2], kr, g[:, _O2 + LANES:]], axis=1)


def _w_uq_local(w):
    w3 = w.reshape(w.shape[0], MLA_HEADS // 2, 2, MLA_NOPE + MLA_ROPE)
    nope = w3[..., :MLA_NOPE].reshape(w.shape[0], MLA_HEADS // 2, 2 * MLA_NOPE)
    rope = w3[..., MLA_NOPE:].reshape(w.shape[0], MLA_HEADS // 2, 2 * MLA_ROPE)
    pad = jnp.zeros((w.shape[0], MLA_HEADS // 2, LANES - 2 * MLA_ROPE), w.dtype)
    return jnp.concatenate([nope, rope, pad], axis=2).reshape(w.shape[0], -1)


def _w_uq_grad(g):
    g3 = g.reshape(g.shape[0], MLA_HEADS // 2, 2 * LANES)
    nope = g3[..., :2 * MLA_NOPE].reshape(g.shape[0], MLA_HEADS // 2, 2, MLA_NOPE)
    rope = g3[..., LANES:LANES + 2 * MLA_ROPE].reshape(g.shape[0], MLA_HEADS // 2, 2, MLA_ROPE)
    return jnp.concatenate([nope, rope], axis=3).reshape(g.shape[0], -1)


def _w_ukv_local(w):
    w3 = w.reshape(w.shape[0], MLA_HEADS, MLA_NOPE + MLA_V)
    return jnp.concatenate([w3[..., :MLA_NOPE].reshape(w.shape[0], -1),
                            w3[..., MLA_NOPE:].reshape(w.shape[0], -1)], axis=1)


def _w_ukv_grad(g):
    half = MLA_HEADS * MLA_NOPE
    kn = g[:, :half].reshape(g.shape[0], MLA_HEADS, MLA_NOPE)
    vv = g[:, half:].reshape(g.shape[0], MLA_HEADS, MLA_V)
    return jnp.concatenate([kn, vv], axis=2).reshape(g.shape[0], -1)


def _rope_tables(T):
    pos = jnp.arange(T, dtype=F32)
    inv_freq = ROPE_THETA ** (-jnp.arange(0, MLA_ROPE, 2, dtype=F32) / MLA_ROPE)
    ang = pos[:, None] * inv_freq[None, :]
    cos, sin = jnp.cos(ang), jnp.sin(ang)
    ones = jnp.ones((T, LANES - 2 * MLA_ROPE), F32)
    cos_k = jnp.concatenate([cos, cos, cos, cos, ones], axis=1)
    sin_k = jnp.concatenate([-sin, sin, -sin, sin, 0.0 * ones], axis=1)
    cos_q = jnp.concatenate([jnp.ones((T, LANES), F32), cos_k], axis=1)
    sin_q = jnp.concatenate([jnp.zeros((T, LANES), F32), sin_k], axis=1)
    return cos_q, sin_q, cos_k, sin_k


def _bias_diag_index():
    ell = np.arange(TOEP_W)
    return np.clip(BAND_W - ell, -REL_CLIP, REL_CLIP) + REL_CLIP


def _local_step(x, target, W):
    T = x.shape[0]
    cos_q, sin_q, cos_k, sin_k = _rope_tables(T)
    G = {}

    u0 = _rms_fwd(x, W["g_mix"][0:1], name="rms_mix0")
    proj = _mm(u0, W["w_in"], name="proj_in")
    c_q, c_kv = proj[:, :_O1], proj[:, _O1:_O2]
    nq = _rms_fwd(c_q, W["g_cq"], name="rms_cq")
    nkv = _rms_fwd(c_kv, W["g_ckv"], name="rms_ckv")
    qa_raw = _mm(nq, W["w_uq"], name="proj_uq")
    qa = _rope(qa_raw, cos_q, sin_q, 0, qa_raw.shape[1] // LANES, BF16, name="rope_q")
    kv = _mm(nkv, W["w_ukv"], out_dtype=BF16, name="proj_ukv")
    kr = _rope(proj, cos_k, sin_k, COL_KR, 1, BF16, name="rope_k")
    o_a, lse = _mla_fwd(qa, kv, kr)
    o_b = _sb_fwd(proj, COL_SB)
    o_ab = jnp.concatenate([o_a, o_b], axis=1)
    h1 = _mm(o_ab, W["ev_w_out"], res=x, name="out_ev")

    def ffn_fwd(h, layer):
        u = _rms_fwd(h, W["g_ffn"][layer:layer + 1], name=f"rms_ffn{layer}")
        ab = _mm(u, W["w_gu"][layer], name=f"ffn_gu{layer}")
        s = _swiglu_fwd(ab, name=f"swiglu_fwd{layer}")
        return u, ab, s, _mm(s, W["w_down"][layer], res=h, name=f"ffn_down{layer}")

    u1, ab0, s0, h2 = ffn_fwd(h1, 0)

    u2 = _rms_fwd(h2, W["g_mix"][1:2], name="rms_mix1")
    qkv = _mm(u2, W["od_w_qkv"], out_dtype=BF16, name="proj_qkv")
    nc = C_HEADS * C_DIM
    pad = ((PAD_KEYS, 0), (0, 0))
    k_pad, v_pad = jnp.pad(qkv[:, nc:2 * nc], pad), jnp.pad(qkv[:, 2 * nc:], pad)
    diag_idx = _bias_diag_index()
    bias_w = _toeplitz(W["od_rel_bias"][:, diag_idx])
    o_c = _band_fwd(qkv, k_pad, v_pad, bias_w)
    h3 = _mm(o_c, W["od_w_out"], res=h2, name="out_od")
    u3, ab1, s1, h4 = ffn_fwd(h3, 1)

    loss, dh, G["g_final"] = _loss_head(h4, W["g_final"], target)

    def ffn_bwd(dh, h, u, ab, s, layer):
        ds = _mm(dh, W["w_down"][layer], dims="nt", name=f"ffn_down_dx{layer}")
        g_down = _mm(s, dh, dims="tn", name=f"ffn_down_dw{layer}")
        dab = _swiglu_bwd(ab, ds, name=f"swiglu_bwd{layer}")
        du = _mm(dab, W["w_gu"][layer], dims="nt", name=f"ffn_gu_dx{layer}")
        g_gu = _mm(u, dab, dims="tn", name=f"ffn_gu_dw{layer}")
        dh_in, g_g = _rms_bwd(h, W["g_ffn"][layer:layer + 1], du, dres=dh, name=f"rms_ffn_bwd{layer}")
        return dh_in, g_down, g_gu, g_g

    dh3, g_down1, g_gu1, g_gffn1 = ffn_bwd(dh, h3, u3, ab1, s1, 1)

    do_c = _mm(dh3, W["od_w_out"], dims="nt", name="out_od_dx")
    G["od_w_out"] = _mm(o_c, dh3, dims="tn", name="out_od_dw")
    dq_c, dk_p, dv_p, dbias_w = _band_bwd(qkv, k_pad, v_pad, bias_w, do_c)
    dqkv = jnp.concatenate([dq_c, dk_p[PAD_KEYS:], dv_p[PAD_KEYS:]], axis=1)
    du2 = _mm(dqkv, W["od_w_qkv"], dims="nt", name="proj_qkv_dx")
    G["od_w_qkv"] = _mm(u2, dqkv, dims="tn", name="proj_qkv_dw")
    ddiag = _toeplitz_bwd(dbias_w)
    n_far = BAND_W - REL_CLIP + 1
    G["od_rel_bias"] = jnp.concatenate(
        [jnp.zeros((C_HEADS, REL_CLIP - BAND_TQ + 1), F32), ddiag[:, n_far:][:, ::-1],
         jnp.sum(ddiag[:, :n_far], axis=1, keepdims=True)], axis=1)
    dh2, g_gmix1 = _rms_bwd(h2, W["g_mix"][1:2], du2, dres=dh3, name="rms_mix_bwd1")

    dh1, g_down0, g_gu0, g_gffn0 = ffn_bwd(dh2, h1, u1, ab0, s0, 0)
    G["w_down"] = jnp.stack([g_down0, g_down1])
    G["w_gu"] = jnp.stack([g_gu0, g_gu1])
    G["g_ffn"] = jnp.concatenate([g_gffn0, g_gffn1], axis=0)

    do_ab = _mm(dh1, W["ev_w_out"], dims="nt", name="out_ev_dx")
    G["ev_w_out"] = _mm(o_ab, dh1, dims="tn", name="out_ev_dw")
    dqa, dkn, dva, dkr = _mla_bwd(qa, kv, kr, o_a, lse, do_ab, 0)
    dqb, dkb, dvb = _sb_bwd(proj, COL_SB, do_ab, MLA_HEADS // 2)
    dqa_raw = _rope(dqa, cos_q, -sin_q, 0, dqa.shape[1] // LANES, F32, name="rope_q_bwd")
    G["w_uq"] = _mm(nq, dqa_raw, dims="tn", name="proj_uq_dw")
    dnq = _mm(dqa_raw, W["w_uq"], dims="nt", name="proj_uq_dx")
    dc_q, G["g_cq"] = _rms_bwd(c_q, W["g_cq"], dnq, name="rms_cq_bwd")
    dkv = jnp.concatenate([dkn, dva], axis=1)
    G["w_ukv"] = _mm(nkv, dkv, dims="tn", name="proj_ukv_dw")
    dnkv = _mm(dkv, W["w_ukv"], dims="nt", name="proj_ukv_dx")
    dc_kv, G["g_ckv"] = _rms_bwd(c_kv, W["g_ckv"], dnkv, name="rms_ckv_bwd")
    dkr_raw = _rope(dkr, cos_k, -sin_k, 0, 1, F32, name="rope_k_bwd")
    dproj = jnp.concatenate([dc_q, dc_kv, dkr_raw, dqb, dkb, dvb], axis=1)
    du0 = _mm(dproj, W["w_in"], dims="nt", name="proj_in_dx")
    G["w_in"] = _mm(u0, dproj, dims="tn", name="proj_in_dw")
    dx, g_gmix0 = _rms_bwd(x, W["g_mix"][0:1], du0, dres=dh1, name="rms_mix_bwd0")
    G["g_mix"] = jnp.concatenate([g_gmix0, g_gmix1], axis=0)
    return loss[0, 0], dx, G


_BIG = ["ev_w_in", "ev_w_uq", "ev_w_ukv", "ev_w_out", "od_w_qkv", "od_w_out", "w_gate", "w_up", "w_down"]
_COL_SHARDED = {"ev_w_in", "ev_w_uq", "ev_w_ukv", "od_w_qkv", "w_gate", "w_up"}
_SMALL = ["ev_g_cq", "ev_g_ckv", "od_rel_bias", "g_mix", "g_ffn", "g_final"]
_SMALL_ROWS = 8
_SMALL_COLS = 1792


def _full_from_gathered(name, g):
    if name in _COL_SHARDED:
        L, K, n = g.shape[1:]
        return jnp.transpose(g, (1, 2, 0, 3)).reshape(L, K, N_DEV * n)
    L, r, N = g.shape[1:]
    return jnp.transpose(g, (1, 0, 2, 3)).reshape(L, N_DEV * r, N)


def _shards_from_full(name, g):
    L, K, N = g.shape
    if name in _COL_SHARDED:
        return jnp.transpose(g.reshape(L, K, N_DEV, N // N_DEV), (2, 0, 1, 3))
    return jnp.transpose(g.reshape(L, N_DEV, K // N_DEV, N), (1, 0, 2, 3))


def _pack_small(vals):
    flat = jnp.concatenate([v.reshape(-1).astype(F32) for v in vals])
    flat = jnp.pad(flat, (0, _SMALL_ROWS * _SMALL_COLS - flat.shape[0]))
    return flat.reshape(_SMALL_ROWS, _SMALL_COLS)


def _unpack_small(packed, like):
    flat = packed.reshape(-1)
    out, off = [], 0
    for v in like:
        out.append(flat[off:off + v.size].reshape(v.shape))
        off += v.size
    return out


def kernel(x, ev_w_in, ev_g_cq, ev_w_uq, ev_g_ckv, ev_w_ukv, ev_w_out, od_w_qkv, od_rel_bias, od_w_out, g_mix, g_ffn, w_gate, w_up, w_down, g_final, loss_target, m_ev_w_in, m_ev_g_cq, m_ev_w_uq, m_ev_g_ckv, m_ev_w_ukv, m_ev_w_out, m_od_w_qkv, m_od_rel_bias, m_od_w_out, m_g_mix, m_g_ffn, m_w_gate, m_w_up, m_w_down, m_g_final, v_ev_w_in, v_ev_g_cq, v_ev_w_uq, v_ev_g_ckv, v_ev_w_ukv, v_ev_w_out, v_od_w_qkv, v_od_rel_bias, v_od_w_out, v_g_mix, v_g_ffn, v_w_gate, v_w_up, v_w_down, v_g_final):
    args = dict(locals())
    w = {n: args[n] for n in _BIG + _SMALL}
    mom = {n: args["m_" + n] for n in _BIG + _SMALL}
    var = {n: args["v_" + n] for n in _BIG + _SMALL}

    gathered = _exchange([w[n].astype(BF16) for n in _BIG], [False] * len(_BIG), name="gather_weights")
    full = {n: _full_from_gathered(n, g) for n, g in zip(_BIG, gathered)}
    W = {
        "w_in": _w_in_local(full["ev_w_in"][0]),
        "w_uq": _w_uq_local(full["ev_w_uq"][0]),
        "w_ukv": _w_ukv_local(full["ev_w_ukv"][0]),
        "ev_w_out": full["ev_w_out"][0],
        "od_w_qkv": full["od_w_qkv"][0],
        "od_w_out": full["od_w_out"][0],
        "w_gu": jnp.concatenate([full["w_gate"], full["w_up"]], axis=2),
        "w_down": full["w_down"],
        "g_cq": ev_g_cq, "g_ckv": ev_g_ckv, "od_rel_bias": od_rel_bias[0],
        "g_mix": g_mix, "g_ffn": g_ffn, "g_final": g_final.reshape(1, -1),
    }

    loss_part, dx, G = _local_step(x[0], loss_target[0], W)
    loss = lax.psum(loss_part, ("x", "y", "c"))

    g_full = {
        "ev_w_in": _w_in_grad(G["w_in"])[None],
        "ev_w_uq": _w_uq_grad(G["w_uq"])[None],
        "ev_w_ukv": _w_ukv_grad(G["w_ukv"])[None],
        "ev_w_out": G["ev_w_out"][None],
        "od_w_qkv": G["od_w_qkv"][None],
        "od_w_out": G["od_w_out"][None],
        "w_gate": G["w_gu"][:, :, :D_FF],
        "w_up": G["w_gu"][:, :, D_FF:],
        "w_down": G["w_down"],
    }
    g_small = [G["g_cq"], G["g_ckv"], G["od_rel_bias"], G["g_mix"], G["g_ffn"], G["g_final"]]
    send = [_shards_from_full(n, g_full[n]).astype(BF16) for n in _BIG] + [_pack_small(g_small)]
    parts = _exchange(send, [True] * len(_BIG) + [False], name="scatter_grads")

    grads, deltas, new_m, new_v = {}, {}, {}, {}
    for n, p in zip(_BIG, parts[:-1]):
        shp = w[n].shape
        r2 = (-1, shp[-1])
        res = _adamw(w[n].reshape(r2), p.reshape((N_DEV,) + w[n].reshape(r2).shape),
                     mom[n].reshape(r2), var[n].reshape(r2), name="adamw_" + n)
        grads[n], deltas[n], new_m[n], new_v[n] = [r.reshape(shp) for r in res]
    small_w = [w[n] for n in _SMALL]
    res = _adamw(_pack_small(small_w), parts[-1], _pack_small([mom[n] for n in _SMALL]),
                 _pack_small([var[n] for n in _SMALL]), name="adamw_small")
    for d, packed in zip((grads, deltas, new_m, new_v), res):
        for n, val in zip(_SMALL, _unpack_small(packed, small_w)):
            d[n] = val

    order = ["ev_w_in", "ev_g_cq", "ev_w_uq", "ev_g_ckv", "ev_w_ukv", "ev_w_out", "od_w_qkv", "od_rel_bias",
             "od_w_out", "g_mix", "g_ffn", "w_gate", "w_up", "w_down", "g_final"]
    out = [loss, dx[None]]
    for d in (grads, deltas, new_m, new_v):
        out += [d[n] for n in order]
    return tuple(out)
```

```python
import functools

import numpy as np
import jax
import jax.numpy as jnp
from jax import lax
from jax.experimental import pallas as pl
from jax.experimental.pallas import tpu as pltpu

F32 = jnp.float32
BF16 = jnp.bfloat16

D_MODEL = 1024
CHUNK = 64
MLA_HEADS = 8
MLA_NOPE = 64
MLA_ROPE = 32
MLA_V = 64
Q_LORA = 384
KV_LORA = 256
ROPE_THETA = 10000.0
SB_HEADS = 8
SB_DIM = 64
C_HEADS = 16
C_DIM = 64
LEFT_CHUNKS = 8
REL_CLIP = 256
D_FF = 2816
RMS_EPS = 1e-6
ADAM_LR = 0.001
ADAM_B1 = 0.9
ADAM_B2 = 0.999
ADAM_EPS = 1e-08
ADAM_WD = 0.01
ADAM_STEP = 10

N_DEV = 8
LANES = 128
VMEM_LIMIT = 56 * 1024 * 1024
NEG = -1e30
PAD_KEYS = LEFT_CHUNKS * CHUNK
BAND_TQ = 128
BAND_W = BAND_TQ + PAD_KEYS
TOEP_W = BAND_W + BAND_TQ

NN = (((1,), (0,)), ((), ()))
NT = (((1,), (1,)), ((), ()))
TN = (((0,), (0,)), ((), ()))


def _dot(a, b, dn):
    return lax.dot_general(a, b, dn, preferred_element_type=F32)


def _pick(dim, pref):
    if dim <= pref:
        return dim
    best = None
    for t in range(LANES, pref + 1, LANES):
        if dim % t == 0:
            best = t
    assert best is not None, (dim, pref)
    return best


def _params(sem):
    return pltpu.CompilerParams(dimension_semantics=sem, vmem_limit_bytes=VMEM_LIMIT)


def _mm(a, b, dims="nn", res=None, out_dtype=F32, name="mm"):
    if dims == "nn":
        (M, K), (K2, N) = a.shape, b.shape
    elif dims == "nt":
        (M, K), (N, K2) = a.shape, b.shape
    else:
        (K, M), (K2, N) = a.shape, b.shape
    assert K == K2, (a.shape, b.shape, dims)
    tm, tn, tk = _pick(M, 512), _pick(N, 1408), _pick(K, 1408)
    nk = K // tk
    dn = {"nn": NN, "nt": NT, "tn": TN}[dims]
    has_res = res is not None

    def body(*refs):
        if has_res:
            a_ref, b_ref, r_ref, o_ref, acc = refs
        else:
            a_ref, b_ref, o_ref, acc = refs
        k = pl.program_id(2)

        @pl.when(k == 0)
        def _():
            acc[...] = jnp.zeros_like(acc)

        acc[...] += _dot(a_ref[...].astype(BF16), b_ref[...].astype(BF16), dn)

        @pl.when(k == nk - 1)
        def _():
            r = acc[...]
            if has_res:
                r = r + r_ref[...]
            o_ref[...] = r.astype(out_dtype)

    a_spec = (pl.BlockSpec((tk, tm), lambda i, j, k: (k, i)) if dims == "tn"
              else pl.BlockSpec((tm, tk), lambda i, j, k: (i, k)))
    b_spec = (pl.BlockSpec((tn, tk), lambda i, j, k: (j, k)) if dims == "nt"
              else pl.BlockSpec((tk, tn), lambda i, j, k: (k, j)))
    o_spec = pl.BlockSpec((tm, tn), lambda i, j, k: (i, j))
    in_specs = [a_spec, b_spec] + ([o_spec] if has_res else [])
    args = (a, b) + ((res,) if has_res else ())
    return pl.pallas_call(
        body, name=name, grid=(M // tm, N // tn, nk),
        in_specs=in_specs, out_specs=o_spec,
        out_shape=jax.ShapeDtypeStruct((M, N), out_dtype),
        scratch_shapes=[pltpu.VMEM((tm, tn), F32)],
        compiler_params=_params(("parallel", "parallel", "arbitrary")),
    )(*args)


def _rms_fwd(x, g, out_dtype=BF16, name="rms_fwd"):
    T, Fd = x.shape
    tm = _pick(T, 256)

    def body(x_ref, g_ref, o_ref):
        xv = x_ref[...]
        r = lax.rsqrt(jnp.mean(xv * xv, axis=-1, keepdims=True) + RMS_EPS)
        o_ref[...] = (xv * r * g_ref[...]).astype(out_dtype)

    return pl.pallas_call(
        body, name=name, grid=(T // tm,),
        in_specs=[pl.BlockSpec((tm, Fd), lambda i: (i, 0)), pl.BlockSpec((1, Fd), lambda i: (0, 0))],
        out_specs=pl.BlockSpec((tm, Fd), lambda i: (i, 0)),
        out_shape=jax.ShapeDtypeStruct((T, Fd), out_dtype),
        compiler_params=_params(("parallel",)),
    )(x, g)


def _rms_bwd(x, g, dy, dres=None, name="rms_bwd"):
    T, Fd = x.shape
    tm = _pick(T, 256)
    has_res = dres is not None

    def body(*refs):
        if has_res:
            x_ref, g_ref, dy_ref, r_ref, dx_ref, dg_ref = refs
        else:
            x_ref, g_ref, dy_ref, dx_ref, dg_ref = refs
        xv, dyv = x_ref[...], dy_ref[...]
        r = lax.rsqrt(jnp.mean(xv * xv, axis=-1, keepdims=True) + RMS_EPS)
        gdy = dyv * g_ref[...]
        dot = jnp.mean(xv * gdy, axis=-1, keepdims=True)
        dx = r * gdy - xv * (r * r * r * dot)
        if has_res:
            dx = dx + r_ref[...]
        dx_ref[...] = dx

        @pl.when(pl.program_id(0) == 0)
        def _():
            dg_ref[...] = jnp.zeros_like(dg_ref)

        dg_ref[...] += jnp.sum(dyv * xv * r, axis=0, keepdims=True)

    row = pl.BlockSpec((tm, Fd), lambda i: (i, 0))
    vec = pl.BlockSpec((1, Fd), lambda i: (0, 0))
    in_specs = [row, vec, row] + ([row] if has_res else [])
    args = (x, g, dy) + ((dres,) if has_res else ())
    return pl.pallas_call(
        body, name=name, grid=(T // tm,),
        in_specs=in_specs, out_specs=[row, vec],
        out_shape=[jax.ShapeDtypeStruct((T, Fd), F32), jax.ShapeDtypeStruct((1, Fd), F32)],
        compiler_params=_params(("arbitrary",)),
    )(*args)


def _loss_head(h, g, target, name="loss_head"):
    T, Fd = h.shape
    tm = _pick(T, 256)

    def body(h_ref, g_ref, t_ref, loss_ref, dh_ref, dg_ref):
        xv = h_ref[...]
        r = lax.rsqrt(jnp.mean(xv * xv, axis=-1, keepdims=True) + RMS_EPS)
        diff = xv * r * g_ref[...] - t_ref[...]
        part = 0.5 * jnp.sum(jnp.mean(diff * diff, axis=-1, keepdims=True), axis=0, keepdims=True)
        dyv = diff * (1.0 / Fd)
        gdy = dyv * g_ref[...]
        dot = jnp.mean(xv * gdy, axis=-1, keepdims=True)
        dh_ref[...] = r * gdy - xv * (r * r * r * dot)

        @pl.when(pl.program_id(0) == 0)
        def _():
            dg_ref[...] = jnp.zeros_like(dg_ref)
            loss_ref[...] = jnp.zeros_like(loss_ref)

        dg_ref[...] += jnp.sum(dyv * xv * r, axis=0, keepdims=True)
        loss_ref[...] += jnp.broadcast_to(part, loss_ref.shape)

    row = pl.BlockSpec((tm, Fd), lambda i: (i, 0))
    vec = pl.BlockSpec((1, Fd), lambda i: (0, 0))
    return pl.pallas_call(
        body, name=name, grid=(T // tm,),
        in_specs=[row, vec, row],
        out_specs=[pl.BlockSpec((1, LANES), lambda i: (0, 0)), row, vec],
        out_shape=[jax.ShapeDtypeStruct((1, LANES), F32), jax.ShapeDtypeStruct((T, Fd), F32),
                   jax.ShapeDtypeStruct((1, Fd), F32)],
        compiler_params=_params(("arbitrary",)),
    )(h, g, target)


def _swiglu_fwd(ab, name="swiglu_fwd"):
    T, two_f = ab.shape
    Fh = two_f // 2
    tm, tc = _pick(T, 256), _pick(Fh, 1408)
    nc = Fh // tc

    def body(a_ref, b_ref, s_ref):
        a = a_ref[...]
        s_ref[...] = (a * jax.nn.sigmoid(a) * b_ref[...]).astype(BF16)

    return pl.pallas_call(
        body, name=name, grid=(T // tm, nc),
        in_specs=[pl.BlockSpec((tm, tc), lambda i, j: (i, j)),
                  pl.BlockSpec((tm, tc), lambda i, j: (i, j + nc))],
        out_specs=pl.BlockSpec((tm, tc), lambda i, j: (i, j)),
        out_shape=jax.ShapeDtypeStruct((T, Fh), BF16),
        compiler_params=_params(("parallel", "parallel")),
    )(ab, ab)


def _swiglu_bwd(ab, ds, name="swiglu_bwd"):
    T, two_f = ab.shape
    Fh = two_f // 2
    tm, tc = _pick(T, 256), _pick(Fh, 1408)
    nc = Fh // tc

    def body(a_ref, b_ref, ds_ref, o_ref):
        j = pl.program_id(1)
        a, dsv = a_ref[...], ds_ref[...]
        sig = jax.nn.sigmoid(a)

        @pl.when(j < nc)
        def _():
            o_ref[...] = dsv * b_ref[...] * (sig * (1.0 + a * (1.0 - sig)))

        @pl.when(j >= nc)
        def _():
            o_ref[...] = dsv * (a * sig)

    return pl.pallas_call(
        body, name=name, grid=(T // tm, 2 * nc),
        in_specs=[pl.BlockSpec((tm, tc), lambda i, j: (i, j % nc)),
                  pl.BlockSpec((tm, tc), lambda i, j: (i, j % nc + nc)),
                  pl.BlockSpec((tm, tc), lambda i, j: (i, j % nc))],
        out_specs=pl.BlockSpec((tm, tc), lambda i, j: (i, j)),
        out_shape=jax.ShapeDtypeStruct((T, two_f), F32),
        compiler_params=_params(("parallel", "parallel")),
    )(ab, ab, ds)


def _rope(x, cos_t, sin_t, col0, ncols, out_dtype, name="rope"):
    T = x.shape[0]
    wt = cos_t.shape[1]
    tm = _pick(T, 256)
    nb = ncols * LANES // wt
    half = MLA_ROPE // 2

    def body(x_ref, c_ref, s_ref, o_ref):
        xv = x_ref[...].astype(F32)
        lane = lax.broadcasted_iota(jnp.int32, xv.shape, 1)
        first = (lane & (MLA_ROPE - 1)) < half
        swapped = jnp.where(first, pltpu.roll(xv, wt - half, 1), pltpu.roll(xv, half, 1))
        o_ref[...] = (xv * c_ref[...] + swapped * s_ref[...]).astype(out_dtype)

    off = col0 * LANES // wt
    return pl.pallas_call(
        body, name=name, grid=(T // tm, nb),
        in_specs=[pl.BlockSpec((tm, wt), lambda i, j: (i, j + off)),
                  pl.BlockSpec((tm, wt), lambda i, j: (i, 0)),
                  pl.BlockSpec((tm, wt), lambda i, j: (i, 0))],
        out_specs=pl.BlockSpec((tm, wt), lambda i, j: (i, j)),
        out_shape=jax.ShapeDtypeStruct((T, ncols * LANES), out_dtype),
        compiler_params=_params(("parallel", "parallel")),
    )(x, cos_t, sin_t)


ATT_T = 256


def _mla_masks(shape):
    lane = lax.broadcasted_iota(jnp.int32, shape, 1)
    m0 = (lane < 64) | ((lane >= 128) & (lane < 160))
    m1 = ((lane >= 64) & (lane < 128)) | ((lane >= 160) & (lane < 192))
    return m0, m1


def _chunk_ok(m_idx, kb, tq, tk):
    tpos = m_idx * tq + lax.broadcasted_iota(jnp.int32, (tq, tk), 0)
    spos = kb * tk + lax.broadcasted_iota(jnp.int32, (tq, tk), 1)
    return (spos >> 6) <= (tpos >> 6)


def _mla_fwd(q, kv, kr, name="mla_fwd"):
    T = q.shape[0]
    tq = tk = _pick(T, ATT_T)
    npair = MLA_HEADS // 2
    scale = (MLA_NOPE + MLA_ROPE) ** -0.5

    def body(q_ref, kn_ref, v_ref, kr_ref, o_ref, lse_ref):
        m_idx = pl.program_id(1)
        qv = q_ref[...]
        m0, m1 = _mla_masks(qv.shape)
        qh = (jnp.where(m0, qv, 0).astype(BF16), jnp.where(m1, qv, 0).astype(BF16))

        def step(kb, carry):
            ks = pl.ds(pl.multiple_of(kb * tk, tk), tk)
            kcat = jnp.concatenate([kn_ref[ks, :], kr_ref[ks, :]], axis=1)
            vv = v_ref[ks, :]
            ok = _chunk_ok(m_idx, kb, tq, tk)
            out = []
            for h in range(2):
                mx, l, acc = carry[3 * h:3 * h + 3]
                s = jnp.where(ok, _dot(qh[h], kcat, NT) * scale, NEG)
                mn = jnp.maximum(mx, jnp.max(s, axis=-1, keepdims=True))
                alpha = jnp.exp(mx - mn)
                p = jnp.exp(s - mn)
                l = alpha * l + jnp.sum(p, axis=-1, keepdims=True)
                acc = alpha * acc + _dot(p.astype(BF16), vv, NN)
                out += [mn, l, acc]
            return tuple(out)

        init = (jnp.full((tq, 1), NEG, F32), jnp.zeros((tq, 1), F32), jnp.zeros((tq, LANES), F32)) * 2
        res = lax.fori_loop(0, m_idx + 1, step, init)
        lane = lax.broadcasted_iota(jnp.int32, (tq, LANES), 1)
        o0 = res[2] / res[1]
        o1 = res[5] / res[4]
        o_ref[...] = jnp.where(lane < 64, o0, o1).astype(o_ref.dtype)
        lse_ref[...] = jnp.where(lane < 64, res[0] + jnp.log(res[1]), res[3] + jnp.log(res[4]))

    full = lambda col: pl.BlockSpec((T, LANES), col)
    return pl.pallas_call(
        body, name=name, grid=(npair, T // tq),
        in_specs=[pl.BlockSpec((tq, 2 * LANES), lambda p, m: (m, p)),
                  full(lambda p, m: (0, p)), full(lambda p, m: (0, npair + p)), full(lambda p, m: (0, 0))],
        out_specs=[pl.BlockSpec((tq, LANES), lambda p, m: (m, p)),
                   pl.BlockSpec((tq, LANES), lambda p, m: (m, p))],
        out_shape=[jax.ShapeDtypeStruct((T, npair * LANES), BF16),
                   jax.ShapeDtypeStruct((T, npair * LANES), F32)],
        compiler_params=_params(("parallel", "arbitrary")),
    )(q, kv, kv, kr)


def _mla_bwd(q, kv, kr, o, lse, do, do_col0, name="mla_bwd"):
    T = q.shape[0]
    tq = tk = _pick(T, ATT_T)
    npair = MLA_HEADS // 2
    scale = (MLA_NOPE + MLA_ROPE) ** -0.5

    def body(q_ref, kn_ref, v_ref, kr_ref, o_ref, lse_ref, do_ref, dq_ref, dkn_ref, dv_ref, dkr_ref):
        p_idx, m_idx = pl.program_id(0), pl.program_id(1)

        @pl.when(m_idx == 0)
        def _():
            dkn_ref[...] = jnp.zeros_like(dkn_ref)
            dv_ref[...] = jnp.zeros_like(dv_ref)

        @pl.when((m_idx == 0) & (p_idx == 0))
        def _():
            dkr_ref[...] = jnp.zeros_like(dkr_ref)

        qv = q_ref[...]
        m0, m1 = _mla_masks(qv.shape)
        qh = (jnp.where(m0, qv, 0).astype(BF16), jnp.where(m1, qv, 0).astype(BF16))
        dov = do_ref[...].astype(F32)
        lane = lax.broadcasted_iota(jnp.int32, (tq, LANES), 1)
        h0 = lane < 64
        prod = dov * o_ref[...].astype(F32)
        delta = (jnp.sum(jnp.where(h0, prod, 0.0), axis=-1, keepdims=True),
                 jnp.sum(jnp.where(h0, 0.0, prod), axis=-1, keepdims=True))
        doh = (jnp.where(h0, dov, 0.0).astype(BF16), jnp.where(h0, 0.0, dov).astype(BF16))
        lsev = lse_ref[...]
        lse_h = (lsev[:, 0:1], lsev[:, 64:65])

        def step(kb, carry):
            ks = pl.ds(pl.multiple_of(kb * tk, tk), tk)
            kcat = jnp.concatenate([kn_ref[ks, :], kr_ref[ks, :]], axis=1)
            vv = v_ref[ks, :]
            ok = _chunk_ok(m_idx, kb, tq, tk)
            dkc = jnp.zeros((tk, 2 * LANES), F32)
            dvv = jnp.zeros((tk, LANES), F32)
            out = []
            for h in range(2):
                s = _dot(qh[h], kcat, NT) * scale
                p = jnp.where(ok, jnp.exp(s - lse_h[h]), 0.0)
                dp = _dot(doh[h], vv, NT)
                ds = (p * (dp - delta[h]) * scale).astype(BF16)
                out.append(carry[h] + _dot(ds, kcat, NN))
                dkc = dkc + _dot(ds, qh[h], TN)
                dvv = dvv + _dot(p.astype(BF16), doh[h], TN)
            dkn_ref[ks, :] += dkc[:, :LANES]
            dkr_ref[ks, :] += dkc[:, LANES:]
            dv_ref[ks, :] += dvv
            return tuple(out)

        init = (jnp.zeros((tq, 2 * LANES), F32),) * 2
        dq0, dq1 = lax.fori_loop(0, m_idx + 1, step, init)
        dq_ref[...] = jnp.where(m0, dq0, jnp.where(m1, dq1, 0.0))

    full = lambda col: pl.BlockSpec((T, LANES), col)
    blk = lambda col: pl.BlockSpec((tq, LANES), col)
    return pl.pallas_call(
        body, name=name, grid=(npair, T // tq),
        in_specs=[pl.BlockSpec((tq, 2 * LANES), lambda p, m: (m, p)),
                  full(lambda p, m: (0, p)), full(lambda p, m: (0, npair + p)), full(lambda p, m: (0, 0)),
                  blk(lambda p, m: (m, p)), blk(lambda p, m: (m, p)),
                  blk(lambda p, m: (m, do_col0 + p))],
        out_specs=[pl.BlockSpec((tq, 2 * LANES), lambda p, m: (m, p)),
                   full(lambda p, m: (0, p)), full(lambda p, m: (0, p)), full(lambda p, m: (0, 0))],
        out_shape=[jax.ShapeDtypeStruct((T, npair * 2 * LANES), F32),
                   jax.ShapeDtypeStruct((T, npair * LANES), F32),
                   jax.ShapeDtypeStruct((T, npair * LANES), F32),
                   jax.ShapeDtypeStruct((T, LANES), F32)],
        compiler_params=_params(("arbitrary", "arbitrary")),
    )(q, kv, kv, kr, o, lse, do)


def _split_dot(x, tri):
    hi = x.astype(BF16)
    lo = (x - hi.astype(F32)).astype(BF16)
    return _dot(hi, tri, NN) + _dot(lo, tri, NN)


def _sb_terms(qh, kk, m_idx, kb, tq, tk, scale):
    z = _dot(qh, kk, NT) * scale
    tpos = m_idx * tq + lax.broadcasted_iota(jnp.int32, (tq, tk), 0)
    spos = kb * tk + lax.broadcasted_iota(jnp.int32, (tq, tk), 1)
    before = spos < tpos
    sp = jnp.maximum(z, 0.0) + jnp.log(1.0 + jnp.exp(-jnp.abs(z)))
    lk = jnp.where(before, -sp, 0.0)
    return z, sp, lk, before


def _sb_fwd(qkv, col0, name="sb_fwd"):
    T = qkv.shape[0]
    tq = tk = _pick(T, ATT_T)
    npair = SB_HEADS // 2
    scale = SB_DIM ** -0.5

    def body(q_ref, k_ref, v_ref, o_ref):
        m_idx = pl.program_id(1)
        qv = q_ref[...].astype(BF16)
        lane = lax.broadcasted_iota(jnp.int32, (tq, LANES), 1)
        h0 = lane < 64
        qh = (jnp.where(h0, qv, 0).astype(BF16), jnp.where(h0, 0, qv).astype(BF16))
        row = lax.broadcasted_iota(jnp.int32, (tk, tk), 0)
        col = lax.broadcasted_iota(jnp.int32, (tk, tk), 1)
        later = (row > col).astype(BF16)

        def step(i, carry):
            kb = m_idx - i
            ks = pl.ds(pl.multiple_of(kb * tk, tk), tk)
            kk = k_ref[ks, :].astype(BF16)
            vv = v_ref[ks, :].astype(BF16)
            out = []
            for h in range(2):
                c, acc = carry[2 * h:2 * h + 2]
                z, sp, lk, before = _sb_terms(qh[h], kk, m_idx, kb, tq, tk, scale)
                a = (z - sp) + _split_dot(lk, later) + c
                w = jnp.where(before, jnp.exp(a), 0.0)
                out += [c + jnp.sum(lk, axis=-1, keepdims=True), acc + _dot(w.astype(BF16), vv, NN)]
            return tuple(out)

        init = (jnp.zeros((tq, 1), F32), jnp.zeros((tq, LANES), F32)) * 2
        res = lax.fori_loop(0, m_idx + 1, step, init)
        o_ref[...] = jnp.where(h0, res[1], res[3]).astype(o_ref.dtype)

    full = lambda col: pl.BlockSpec((T, LANES), col)
    return pl.pallas_call(
        body, name=name, grid=(npair, T // tq),
        in_specs=[pl.BlockSpec((tq, LANES), lambda p, m: (m, col0 + p)),
                  full(lambda p, m: (0, col0 + npair + p)), full(lambda p, m: (0, col0 + 2 * npair + p))],
        out_specs=pl.BlockSpec((tq, LANES), lambda p, m: (m, p)),
        out_shape=jax.ShapeDtypeStruct((T, npair * LANES), BF16),
        compiler_params=_params(("parallel", "arbitrary")),
    )(qkv, qkv, qkv)


def _sb_bwd(qkv, col0, do, do_col0, name="sb_bwd"):
    T = qkv.shape[0]
    tq = tk = _pick(T, ATT_T)
    npair = SB_HEADS // 2
    scale = SB_DIM ** -0.5

    def body(q_ref, k_ref, v_ref, do_ref, dq_ref, dk_ref, dv_ref, e_scr, z_scr, sp_scr):
        m_idx = pl.program_id(1)

        @pl.when(m_idx == 0)
        def _():
            dk_ref[...] = jnp.zeros_like(dk_ref)
            dv_ref[...] = jnp.zeros_like(dv_ref)

        qv = q_ref[...].astype(BF16)
        lane = lax.broadcasted_iota(jnp.int32, (tq, LANES), 1)
        h0 = lane < 64
        qh = (jnp.where(h0, qv, 0).astype(BF16), jnp.where(h0, 0, qv).astype(BF16))
        dov = do_ref[...].astype(F32)
        doh = (jnp.where(h0, dov, 0.0).astype(BF16), jnp.where(h0, 0.0, dov).astype(BF16))
        row = lax.broadcasted_iota(jnp.int32, (tk, tk), 0)
        col = lax.broadcasted_iota(jnp.int32, (tk, tk), 1)
        later = (row > col).astype(BF16)
        earlier = (row < col).astype(BF16)
        dq = []
        for h in range(2):
            def rl(i, c, h=h):
                kb = m_idx - i
                ks = pl.ds(pl.multiple_of(kb * tk, tk), tk)
                kk = k_ref[ks, :].astype(BF16)
                vv = v_ref[ks, :].astype(BF16)
                z, sp, lk, before = _sb_terms(qh[h], kk, m_idx, kb, tq, tk, scale)
                a = (z - sp) + _split_dot(lk, later) + c
                w = jnp.where(before, jnp.exp(a), 0.0)
                dw = _dot(doh[h], vv, NT)
                e_scr[kb] = w * dw
                z_scr[kb] = z
                sp_scr[kb] = sp
                dv_ref[ks, :] += _dot(w.astype(BF16), doh[h], TN)
                return c + jnp.sum(lk, axis=-1, keepdims=True)

            lax.fori_loop(0, m_idx + 1, rl, jnp.zeros((tq, 1), F32))

            def lr(kb, carry, h=h):
                esum, dqa = carry
                ks = pl.ds(pl.multiple_of(kb * tk, tk), tk)
                kk = k_ref[ks, :].astype(BF16)
                e, z, sp = e_scr[kb], z_scr[kb], sp_scr[kb]
                tpos = m_idx * tq + lax.broadcasted_iota(jnp.int32, (tq, tk), 0)
                spos = kb * tk + lax.broadcasted_iota(jnp.int32, (tq, tk), 1)
                prev = _split_dot(e, earlier) + esum
                dz = jnp.where(spos < tpos, e * jnp.exp(-sp) - jnp.exp(z - sp) * prev, 0.0)
                dzb = (dz * scale).astype(BF16)
                dk_ref[ks, :] += _dot(dzb, qh[h], TN)
                return esum + jnp.sum(e, axis=-1, keepdims=True), dqa + _dot(dzb, kk, NN)

            _, dqh = lax.fori_loop(0, m_idx + 1, lr, (jnp.zeros((tq, 1), F32), jnp.zeros((tq, LANES), F32)))
            dq.append(dqh)
        dq_ref[...] = jnp.where(h0, dq[0], dq[1])

    full = lambda col: pl.BlockSpec((T, LANES), col)
    blk = lambda col: pl.BlockSpec((tq, LANES), col)
    return pl.pallas_call(
        body, name=name, grid=(npair, T // tq),
        in_specs=[blk(lambda p, m: (m, col0 + p)),
                  full(lambda p, m: (0, col0 + npair + p)), full(lambda p, m: (0, col0 + 2 * npair + p)),
                  blk(lambda p, m: (m, do_col0 + p))],
        out_specs=[blk(lambda p, m: (m, p)), full(lambda p, m: (0, p)), full(lambda p, m: (0, p))],
        out_shape=[jax.ShapeDtypeStruct((T, npair * LANES), F32)] * 3,
        scratch_shapes=[pltpu.VMEM((T // tk, tq, tk), F32)] * 3,
        compiler_params=_params(("arbitrary", "arbitrary")),
    )(qkv, qkv, qkv, do)


def _band_valid(m_idx):
    qi = lax.broadcasted_iota(jnp.int32, (BAND_TQ, BAND_W), 0)
    j = lax.broadcasted_iota(jnp.int32, (BAND_TQ, BAND_W), 1)
    cq = (m_idx * BAND_TQ + qi) >> 6
    ckp = (m_idx * BAND_TQ + j) >> 6
    return (ckp >= LEFT_CHUNKS) & (ckp >= cq) & (ckp <= cq + LEFT_CHUNKS)


def _band_probs(qh, kw, bias, valid, scale):
    s = _dot(qh, kw, NT) * scale + bias
    s = jnp.where(valid, s, NEG)
    e = jnp.exp(s - jnp.max(s, axis=-1, keepdims=True))
    return e / jnp.sum(e, axis=-1, keepdims=True)


def _band_fwd(qkv, k_pad, v_pad, bias_w, name="band_fwd"):
    T = qkv.shape[0]
    npair = C_HEADS // 2
    scale = C_DIM ** -0.5

    def body(q_ref, k_ref, v_ref, b_ref, o_ref):
        m_idx = pl.program_id(1)
        win = pl.ds(pl.multiple_of(m_idx * BAND_TQ, BAND_TQ), BAND_W)
        kw, vw = k_ref[win, :], v_ref[win, :]
        qv = q_ref[...]
        lane = lax.broadcasted_iota(jnp.int32, (BAND_TQ, LANES), 1)
        h0 = lane < 64
        qh = (jnp.where(h0, qv, 0).astype(BF16), jnp.where(h0, 0, qv).astype(BF16))
        valid = _band_valid(m_idx)
        o = [_dot(_band_probs(qh[h], kw, b_ref[h], valid, scale).astype(BF16), vw, NN) for h in range(2)]
        o_ref[...] = jnp.where(h0, o[0], o[1]).astype(o_ref.dtype)

    Tp = T + PAD_KEYS
    return pl.pallas_call(
        body, name=name, grid=(npair, T // BAND_TQ),
        in_specs=[pl.BlockSpec((BAND_TQ, LANES), lambda p, m: (m, p)),
                  pl.BlockSpec((Tp, LANES), lambda p, m: (0, p)),
                  pl.BlockSpec((Tp, LANES), lambda p, m: (0, p)),
                  pl.BlockSpec((2, BAND_TQ, BAND_W), lambda p, m: (p, 0, 0))],
        out_specs=pl.BlockSpec((BAND_TQ, LANES), lambda p, m: (m, p)),
        out_shape=jax.ShapeDtypeStruct((T, npair * LANES), BF16),
        compiler_params=_params(("parallel", "arbitrary")),
    )(qkv, k_pad, v_pad, bias_w)


def _band_bwd(qkv, k_pad, v_pad, bias_w, do, name="band_bwd"):
    T = qkv.shape[0]
    npair = C_HEADS // 2
    scale = C_DIM ** -0.5

    def body(q_ref, k_ref, v_ref, b_ref, do_ref, dq_ref, dk_ref, dv_ref, db_ref):
        m_idx = pl.program_id(1)

        @pl.when(m_idx == 0)
        def _():
            dk_ref[...] = jnp.zeros_like(dk_ref)
            dv_ref[...] = jnp.zeros_like(dv_ref)
            db_ref[...] = jnp.zeros_like(db_ref)

        win = pl.ds(pl.multiple_of(m_idx * BAND_TQ, BAND_TQ), BAND_W)
        kw, vw = k_ref[win, :], v_ref[win, :]
        qv = q_ref[...]
        dov = do_ref[...].astype(F32)
        lane = lax.broadcasted_iota(jnp.int32, (BAND_TQ, LANES), 1)
        h0 = lane < 64
        qh = (jnp.where(h0, qv, 0).astype(BF16), jnp.where(h0, 0, qv).astype(BF16))
        doh = (jnp.where(h0, dov, 0.0).astype(BF16), jnp.where(h0, 0.0, dov).astype(BF16))
        valid = _band_valid(m_idx)
        dq = []
        dkw = jnp.zeros((BAND_W, LANES), F32)
        dvw = jnp.zeros((BAND_W, LANES), F32)
        for h in range(2):
            p = _band_probs(qh[h], kw, b_ref[h], valid, scale)
            dp = _dot(doh[h], vw, NT)
            dsb = p * (dp - jnp.sum(p * dp, axis=-1, keepdims=True))
            db_ref[h] += dsb
            dsq = (dsb * scale).astype(BF16)
            dq.append(_dot(dsq, kw, NN))
            dkw = dkw + _dot(dsq, qh[h], TN)
            dvw = dvw + _dot(p.astype(BF16), doh[h], TN)
        dq_ref[...] = jnp.where(h0, dq[0], dq[1])
        dk_ref[win, :] += dkw
        dv_ref[win, :] += dvw

    Tp = T + PAD_KEYS
    blk = lambda col: pl.BlockSpec((BAND_TQ, LANES), col)
    full = pl.BlockSpec((Tp, LANES), lambda p, m: (0, p))
    bias = pl.BlockSpec((2, BAND_TQ, BAND_W), lambda p, m: (p, 0, 0))
    return pl.pallas_call(
        body, name=name, grid=(npair, T // BAND_TQ),
        in_specs=[blk(lambda p, m: (m, p)), full, full, bias, blk(lambda p, m: (m, p))],
        out_specs=[blk(lambda p, m: (m, p)), full, full, bias],
        out_shape=[jax.ShapeDtypeStruct((T, npair * LANES), F32),
                   jax.ShapeDtypeStruct((Tp, npair * LANES), F32),
                   jax.ShapeDtypeStruct((Tp, npair * LANES), F32),
                   jax.ShapeDtypeStruct((C_HEADS, BAND_TQ, BAND_W), F32)],
        compiler_params=_params(("arbitrary", "arbitrary")),
    )(qkv, k_pad, v_pad, bias_w, do)


def _skew_bits(x, left):
    w = x.shape[1]
    row = lax.broadcasted_iota(jnp.int32, x.shape, 0)
    for b in range(BAND_TQ.bit_length() - 1):
        amt = (w - (1 << b)) if left else (1 << b)
        x = jnp.where((row >> b) & 1 == 1, pltpu.roll(x, amt, 1), x)
    return x


def _toeplitz(diag, name="toeplitz"):
    H = diag.shape[0]

    def body(d_ref, o_ref):
        x = jnp.broadcast_to(d_ref[0], (BAND_TQ, TOEP_W))
        o_ref[0] = _skew_bits(x, left=False)[:, BAND_TQ:]

    return pl.pallas_call(
        body, name=name, grid=(H,),
        in_specs=[pl.BlockSpec((1, 1, TOEP_W), lambda h: (h, 0, 0))],
        out_specs=pl.BlockSpec((1, BAND_TQ, BAND_W), lambda h: (h, 0, 0)),
        out_shape=jax.ShapeDtypeStruct((H, BAND_TQ, BAND_W), F32),
        compiler_params=_params(("parallel",)),
    )(diag.reshape(H, 1, TOEP_W))


def _toeplitz_bwd(dbias, name="toeplitz_bwd"):
    H = dbias.shape[0]

    def body(d_ref, o_ref):
        x = jnp.concatenate([jnp.zeros((BAND_TQ, BAND_TQ), F32), d_ref[0]], axis=1)
        o_ref[0] = jnp.sum(_skew_bits(x, left=True), axis=0, keepdims=True)

    return pl.pallas_call(
        body, name=name, grid=(H,),
        in_specs=[pl.BlockSpec((1, BAND_TQ, BAND_W), lambda h: (h, 0, 0))],
        out_specs=pl.BlockSpec((1, 1, TOEP_W), lambda h: (h, 0, 0)),
        out_shape=jax.ShapeDtypeStruct((H, 1, TOEP_W), F32),
        compiler_params=_params(("parallel",)),
    )(dbias).reshape(H, TOEP_W)


_HBM = pl.BlockSpec(memory_space=pltpu.HBM)
_SEM = pl.BlockSpec(memory_space=pltpu.SEMAPHORE)
_EFFECT = pltpu.SideEffectType.DATAFLOW_SIDE_EFFECTING


def _peers():
    x, y, c = lax.axis_index("x"), lax.axis_index("y"), lax.axis_index("c")
    out = []
    for k in range(1, N_DEV):
        peer = (1 - x if (k >> 2) & 1 else x, 1 - y if (k >> 1) & 1 else y, 1 - c if k & 1 else c)
        out.append((peer, 4 * peer[0] + 2 * peer[1] + peer[2]))
    return 4 * x + 2 * y + c, out


def _split_copies(ins, lands, scatter, send_sem, recv_sem):
    me, peers = _peers()
    starts, arrivals = [], []
    for a in range(len(ins)):
        for peer, idx in peers:
            src = ins[a].at[idx] if scatter[a] else ins[a]
            mk = lambda dst: pltpu.make_async_remote_copy(
                src_ref=src, dst_ref=dst, send_sem=send_sem, recv_sem=recv_sem,
                device_id=peer, device_id_type=pl.DeviceIdType.MESH)
            starts.append(mk(lands[a].at[me]))
            arrivals.append(mk(lands[a].at[idx]))
    return starts, arrivals


def _exchange_start(arrays, scatter, after, name):
    n = len(arrays)
    lands = [lax.empty((N_DEV,) + (a.shape[1:] if s else a.shape), a.dtype) for a, s in zip(arrays, scatter)]

    def body(*refs):
        ins, lnd = refs[:n], refs[n:2 * n]
        send_sem, recv_sem = refs[2 * n + 1:2 * n + 3]
        token = refs[-1]
        starts, _ = _split_copies(ins, lnd, scatter, send_sem, recv_sem)
        for cp in starts:
            cp.start()
        token[...] = jnp.zeros_like(token)

    hbm = lambda a: pltpu.HBM(a.shape, a.dtype)
    out = pl.pallas_call(
        body, name=name,
        out_shape=(pltpu.SemaphoreType.DMA(()), pltpu.SemaphoreType.DMA(()),
                   *[hbm(a) for a in arrays], *[hbm(a) for a in lands],
                   jax.ShapeDtypeStruct((8, LANES), F32)),
        in_specs=[_HBM] * (2 * n) + [pl.BlockSpec(memory_space=pl.ANY)],
        out_specs=(_SEM, _SEM, *([_HBM] * (2 * n)), pl.BlockSpec(memory_space=pltpu.VMEM)),
        input_output_aliases={i: 2 + i for i in range(2 * n)},
        compiler_params=pltpu.CompilerParams(has_side_effects=_EFFECT),
    )(*[pltpu.with_memory_space_constraint(a, pltpu.HBM) for a in list(arrays) + lands], after)
    return (out[0], out[1], list(out[2:2 + n]), list(out[2 + n:2 + 2 * n]), tuple(scatter)), out[-1]


def _exchange_wait(handle, after, name):
    send_sem, recv_sem, ins, lands, scatter = handle
    n = len(ins)

    def body(*refs):
        i_ref, l_ref = refs[:n], refs[n:2 * n]
        s_sem, r_sem = refs[2 * n:2 * n + 2]
        starts, arrivals = _split_copies(i_ref, l_ref, scatter, s_sem, r_sem)
        for cp in starts:
            cp.wait_send()
        for cp in arrivals:
            cp.wait_recv()

    hbm = lambda a: pltpu.HBM(a.shape, a.dtype)
    out = pl.pallas_call(
        body, name=name,
        out_shape=tuple(hbm(a) for a in ins + lands),
        in_specs=[_HBM] * (2 * n) + [_SEM, _SEM, pl.BlockSpec(memory_space=pl.ANY)],
        out_specs=tuple([_HBM] * (2 * n)),
        input_output_aliases={i: i for i in range(2 * n)},
        compiler_params=pltpu.CompilerParams(has_side_effects=_EFFECT),
    )(*ins, *lands, send_sem, recv_sem, after)
    return list(out[n:])


def _with_own_row(land, own):
    me = 4 * lax.axis_index("x") + 2 * lax.axis_index("y") + lax.axis_index("c")
    return lax.dynamic_update_index_in_dim(land, own.astype(land.dtype), me, 0)


def _adamw(w, parts, m, v, name="adamw"):
    R, C = w.shape
    tr = next(t for t in (256, 128, 64, 32, 16, 8) if R % t == 0)
    c1 = 1.0 - ADAM_B1 ** ADAM_STEP
    c2 = 1.0 - ADAM_B2 ** ADAM_STEP

    def body(w_ref, p_ref, m_ref, v_ref, g_ref, d_ref, nm_ref, nv_ref):
        g = p_ref[0].astype(F32)
        for i in range(1, N_DEV):
            g = g + p_ref[i].astype(F32)
        nm = ADAM_B1 * m_ref[...] + (1.0 - ADAM_B1) * g
        nv = ADAM_B2 * v_ref[...] + (1.0 - ADAM_B2) * (g * g)
        g_ref[...] = g
        nm_ref[...] = nm
        nv_ref[...] = nv
        d_ref[...] = -ADAM_LR * ((nm / c1) / (jnp.sqrt(nv / c2) + ADAM_EPS) + ADAM_WD * w_ref[...])

    blk = pl.BlockSpec((tr, C), lambda i: (i, 0))
    return pl.pallas_call(
        body, name=name, grid=(R // tr,),
        in_specs=[blk, pl.BlockSpec((N_DEV, tr, C), lambda i: (0, i, 0)), blk, blk],
        out_specs=[blk] * 4,
        out_shape=[jax.ShapeDtypeStruct((R, C), F32)] * 4,
        compiler_params=_params(("parallel",)),
    )(w, parts, m, v)


_O1 = Q_LORA
_O2 = _O1 + KV_LORA
_O3 = _O2 + MLA_ROPE
_NB = SB_HEADS * SB_DIM
IN_W = _O2 + LANES + 3 * _NB
COL_KR = _O2 // LANES
COL_SB = COL_KR + 1


def _w_in_local(w):
    kr = w[:, _O2:_O3]
    pad = jnp.zeros((w.shape[0], LANES - 2 * MLA_ROPE), w.dtype)
    return jnp.concatenate([w[:, :_O2], kr, kr, pad, w[:, _O3:]], axis=1)


def _w_in_grad(g):
    kr = g[:, _O2:_O2 + MLA_ROPE] + g[:, _O2 + MLA_ROPE:_O2 + 2 * MLA_ROPE]
    return jnp.concatenate([g[:, :_O2], kr, g[:, _O2 + LANES:]], axis=1)


def _w_uq_local(w):
    w3 = w.reshape(w.shape[0], MLA_HEADS // 2, 2, MLA_NOPE + MLA_ROPE)
    nope = w3[..., :MLA_NOPE].reshape(w.shape[0], MLA_HEADS // 2, 2 * MLA_NOPE)
    rope = w3[..., MLA_NOPE:].reshape(w.shape[0], MLA_HEADS // 2, 2 * MLA_ROPE)
    pad = jnp.zeros((w.shape[0], MLA_HEADS // 2, LANES - 2 * MLA_ROPE), w.dtype)
    return jnp.concatenate([nope, rope, pad], axis=2).reshape(w.shape[0], -1)


def _w_uq_grad(g):
    g3 = g.reshape(g.shape[0], MLA_HEADS // 2, 2 * LANES)
    nope = g3[..., :2 * MLA_NOPE].reshape(g.shape[0], MLA_HEADS // 2, 2, MLA_NOPE)
    rope = g3[..., LANES:LANES + 2 * MLA_ROPE].reshape(g.shape[0], MLA_HEADS // 2, 2, MLA_ROPE)
    return jnp.concatenate([nope, rope], axis=3).reshape(g.shape[0], -1)


def _w_ukv_local(w):
    w3 = w.reshape(w.shape[0], MLA_HEADS, MLA_NOPE + MLA_V)
    return jnp.concatenate([w3[..., :MLA_NOPE].reshape(w.shape[0], -1),
                            w3[..., MLA_NOPE:].reshape(w.shape[0], -1)], axis=1)


def _w_ukv_grad(g):
    half = MLA_HEADS * MLA_NOPE
    kn = g[:, :half].reshape(g.shape[0], MLA_HEADS, MLA_NOPE)
    vv = g[:, half:].reshape(g.shape[0], MLA_HEADS, MLA_V)
    return jnp.concatenate([kn, vv], axis=2).reshape(g.shape[0], -1)


def _rope_tables(T):
    pos = jnp.arange(T, dtype=F32)
    inv_freq = ROPE_THETA ** (-jnp.arange(0, MLA_ROPE, 2, dtype=F32) / MLA_ROPE)
    ang = pos[:, None] * inv_freq[None, :]
    cos, sin = jnp.cos(ang), jnp.sin(ang)
    ones = jnp.ones((T, LANES - 2 * MLA_ROPE), F32)
    cos_k = jnp.concatenate([cos, cos, cos, cos, ones], axis=1)
    sin_k = jnp.concatenate([-sin, sin, -sin, sin, 0.0 * ones], axis=1)
    cos_q = jnp.concatenate([jnp.ones((T, LANES), F32), cos_k], axis=1)
    sin_q = jnp.concatenate([jnp.zeros((T, LANES), F32), sin_k], axis=1)
    return cos_q, sin_q, cos_k, sin_k


def _bias_diag_index():
    ell = np.arange(TOEP_W)
    return np.clip(BAND_W - ell, -REL_CLIP, REL_CLIP) + REL_CLIP


def _local_step(x, target, small, get_weights, put_grads):
    T = x.shape[0]
    cos_q, sin_q, cos_k, sin_k = _rope_tables(T)
    G = {}
    W = dict(small)

    W.update(get_weights("mix0", None))
    u0 = _rms_fwd(x, W["g_mix"][0:1], name="rms_mix0")
    proj = _mm(u0, W["w_in"], name="proj_in")
    c_q, c_kv = proj[:, :_O1], proj[:, _O1:_O2]
    nq = _rms_fwd(c_q, W["g_cq"], name="rms_cq")
    nkv = _rms_fwd(c_kv, W["g_ckv"], name="rms_ckv")
    qa_raw = _mm(nq, W["w_uq"], name="proj_uq")
    qa = _rope(qa_raw, cos_q, sin_q, 0, qa_raw.shape[1] // LANES, BF16, name="rope_q")
    kv = _mm(nkv, W["w_ukv"], out_dtype=BF16, name="proj_ukv")
    kr = _rope(proj, cos_k, sin_k, COL_KR, 1, BF16, name="rope_k")
    o_a, lse = _mla_fwd(qa, kv, kr)
    o_b = _sb_fwd(proj, COL_SB)
    o_ab = jnp.concatenate([o_a, o_b], axis=1)
    h1 = _mm(o_ab, W["ev_w_out"], res=x, name="out_ev")

    def ffn_fwd(h, layer):
        W.update(get_weights(f"ffn{layer}", h))
        u = _rms_fwd(h, W["g_ffn"][layer:layer + 1], name=f"rms_ffn{layer}")
        ab = _mm(u, W[f"w_gu{layer}"], name=f"ffn_gu{layer}")
        s = _swiglu_fwd(ab, name=f"swiglu_fwd{layer}")
        return u, ab, s, _mm(s, W[f"w_down{layer}"], res=h, name=f"ffn_down{layer}")

    u1, ab0, s0, h2 = ffn_fwd(h1, 0)

    W.update(get_weights("mix1", h2))
    u2 = _rms_fwd(h2, W["g_mix"][1:2], name="rms_mix1")
    qkv = _mm(u2, W["od_w_qkv"], out_dtype=BF16, name="proj_qkv")
    nc = C_HEADS * C_DIM
    pad = ((PAD_KEYS, 0), (0, 0))
    k_pad, v_pad = jnp.pad(qkv[:, nc:2 * nc], pad), jnp.pad(qkv[:, 2 * nc:], pad)
    diag_idx = _bias_diag_index()
    bias_w = _toeplitz(W["od_rel_bias"][:, diag_idx])
    o_c = _band_fwd(qkv, k_pad, v_pad, bias_w)
    h3 = _mm(o_c, W["od_w_out"], res=h2, name="out_od")
    u3, ab1, s1, h4 = ffn_fwd(h3, 1)

    loss, dh, G["g_final"] = _loss_head(h4, W["g_final"], target)

    def ffn_bwd(dh, h, u, ab, s, layer):
        ds = _mm(dh, W[f"w_down{layer}"], dims="nt", name=f"ffn_down_dx{layer}")
        g_down = _mm(s, dh, dims="tn", name=f"ffn_down_dw{layer}")
        dab = _swiglu_bwd(ab, ds, name=f"swiglu_bwd{layer}")
        du = _mm(dab, W[f"w_gu{layer}"], dims="nt", name=f"ffn_gu_dx{layer}")
        g_gu = _mm(u, dab, dims="tn", name=f"ffn_gu_dw{layer}")
        put_grads(f"ffn{layer}", {"w_gu": g_gu, "w_down": g_down})
        return _rms_bwd(h, W["g_ffn"][layer:layer + 1], du, dres=dh, name=f"rms_ffn_bwd{layer}")

    dh3, g_gffn1 = ffn_bwd(dh, h3, u3, ab1, s1, 1)

    do_c = _mm(dh3, W["od_w_out"], dims="nt", name="out_od_dx")
    g_od_out = _mm(o_c, dh3, dims="tn", name="out_od_dw")
    dq_c, dk_p, dv_p, dbias_w = _band_bwd(qkv, k_pad, v_pad, bias_w, do_c)
    dqkv = jnp.concatenate([dq_c, dk_p[PAD_KEYS:], dv_p[PAD_KEYS:]], axis=1)
    du2 = _mm(dqkv, W["od_w_qkv"], dims="nt", name="proj_qkv_dx")
    put_grads("mix1", {"od_w_qkv": _mm(u2, dqkv, dims="tn", name="proj_qkv_dw"), "od_w_out": g_od_out})
    ddiag = _toeplitz_bwd(dbias_w)
    n_far = BAND_W - REL_CLIP + 1
    G["od_rel_bias"] = jnp.concatenate(
        [jnp.zeros((C_HEADS, REL_CLIP - BAND_TQ + 1), F32), ddiag[:, n_far:][:, ::-1],
         jnp.sum(ddiag[:, :n_far], axis=1, keepdims=True)], axis=1)
    dh2, g_gmix1 = _rms_bwd(h2, W["g_mix"][1:2], du2, dres=dh3, name="rms_mix_bwd1")

    dh1, g_gffn0 = ffn_bwd(dh2, h1, u1, ab0, s0, 0)
    G["g_ffn"] = jnp.concatenate([g_gffn0, g_gffn1], axis=0)

    do_ab = _mm(dh1, W["ev_w_out"], dims="nt", name="out_ev_dx")
    g0 = {"ev_w_out": _mm(o_ab, dh1, dims="tn", name="out_ev_dw")}
    dqa, dkn, dva, dkr = _mla_bwd(qa, kv, kr, o_a, lse, do_ab, 0)
    dqb, dkb, dvb = _sb_bwd(proj, COL_SB, do_ab, MLA_HEADS // 2)
    dqa_raw = _rope(dqa, cos_q, -sin_q, 0, dqa.shape[1] // LANES, F32, name="rope_q_bwd")
    g0["w_uq"] = _mm(nq, dqa_raw, dims="tn", name="proj_uq_dw")
    dnq = _mm(dqa_raw, W["w_uq"], dims="nt", name="proj_uq_dx")
    dc_q, G["g_cq"] = _rms_bwd(c_q, W["g_cq"], dnq, name="rms_cq_bwd")
    dkv = jnp.concatenate([dkn, dva], axis=1)
    g0["w_ukv"] = _mm(nkv, dkv, dims="tn", name="proj_ukv_dw")
    dnkv = _mm(dkv, W["w_ukv"], dims="nt", name="proj_ukv_dx")
    dc_kv, G["g_ckv"] = _rms_bwd(c_kv, W["g_ckv"], dnkv, name="rms_ckv_bwd")
    dkr_raw = _rope(dkr, cos_k, -sin_k, 0, 1, F32, name="rope_k_bwd")
    dproj = jnp.concatenate([dc_q, dc_kv, dkr_raw, dqb, dkb, dvb], axis=1)
    du0 = _mm(dproj, W["w_in"], dims="nt", name="proj_in_dx")
    g0["w_in"] = _mm(u0, dproj, dims="tn", name="proj_in_dw")
    put_grads("mix0", g0)
    dx, g_gmix0 = _rms_bwd(x, W["g_mix"][0:1], du0, dres=dh1, name="rms_mix_bwd0")
    G["g_mix"] = jnp.concatenate([g_gmix0, g_gmix1], axis=0)
    return loss[0, 0], dx, G


_BIG = ["ev_w_in", "ev_w_uq", "ev_w_ukv", "ev_w_out", "od_w_qkv", "od_w_out", "w_gate", "w_up", "w_down"]
_COL_SHARDED = {"ev_w_in", "ev_w_uq", "ev_w_ukv", "od_w_qkv", "w_gate", "w_up"}
_SMALL = ["ev_g_cq", "ev_g_ckv", "od_rel_bias", "g_mix", "g_ffn", "g_final"]
_GROUPS = {
    "mix0": ["ev_w_in", "ev_w_uq", "ev_w_ukv", "ev_w_out"],
    "ffn0": ["w_gate0", "w_up0", "w_down0"],
    "mix1": ["od_w_qkv", "od_w_out"],
    "ffn1": ["w_gate1", "w_up1", "w_down1"],
}
_GROUP_SRC = {n + str(l): (n, l) for n in ("w_gate", "w_up", "w_down") for l in (0, 1)}
_SMALL_ROWS = 8
_SMALL_COLS = 1792


def _full_from_gathered(name, g):
    if name in _COL_SHARDED:
        L, K, n = g.shape[1:]
        return jnp.transpose(g, (1, 2, 0, 3)).reshape(L, K, N_DEV * n)
    L, r, N = g.shape[1:]
    return jnp.transpose(g, (1, 0, 2, 3)).reshape(L, N_DEV * r, N)


def _shards_from_full(name, g):
    L, K, N = g.shape
    if name in _COL_SHARDED:
        return jnp.transpose(g.reshape(L, K, N_DEV, N // N_DEV), (2, 0, 1, 3))
    return jnp.transpose(g.reshape(L, N_DEV, K // N_DEV, N), (1, 0, 2, 3))


def _pack_small(vals):
    flat = jnp.concatenate([v.reshape(-1).astype(F32) for v in vals])
    flat = jnp.pad(flat, (0, _SMALL_ROWS * _SMALL_COLS - flat.shape[0]))
    return flat.reshape(_SMALL_ROWS, _SMALL_COLS)


def _unpack_small(packed, like):
    flat = packed.reshape(-1)
    out, off = [], 0
    for v in like:
        out.append(flat[off:off + v.size].reshape(v.shape))
        off += v.size
    return out


def kernel(x, ev_w_in, ev_g_cq, ev_w_uq, ev_g_ckv, ev_w_ukv, ev_w_out, od_w_qkv, od_rel_bias, od_w_out, g_mix, g_ffn, w_gate, w_up, w_down, g_final, loss_target, m_ev_w_in, m_ev_g_cq, m_ev_w_uq, m_ev_g_ckv, m_ev_w_ukv, m_ev_w_out, m_od_w_qkv, m_od_rel_bias, m_od_w_out, m_g_mix, m_g_ffn, m_w_gate, m_w_up, m_w_down, m_g_final, v_ev_w_in, v_ev_g_cq, v_ev_w_uq, v_ev_g_ckv, v_ev_w_ukv, v_ev_w_out, v_od_w_qkv, v_od_rel_bias, v_od_w_out, v_g_mix, v_g_ffn, v_w_gate, v_w_up, v_w_down, v_g_final):
    args = dict(locals())
    w = {n: args[n] for n in _BIG + _SMALL}
    mom = {n: args["m_" + n] for n in _BIG + _SMALL}
    var = {n: args["v_" + n] for n in _BIG + _SMALL}

    own = {}
    for grp, names in _GROUPS.items():
        for n in names:
            base, layer = _GROUP_SRC.get(n, (n, 0))
            own[n] = w[base][layer:layer + 1].astype(BF16)
    gather, token = {}, x[0, :8, :LANES]
    for grp, names in _GROUPS.items():
        gather[grp], token = _exchange_start([own[n] for n in names], [False] * len(names), token,
                                             name="gather_start_" + grp)

    def get_weights(grp, after):
        names = _GROUPS[grp]
        lands = _exchange_wait(gather[grp], token if after is None else after, name="gather_wait_" + grp)
        full = {n: _full_from_gathered(_GROUP_SRC.get(n, (n, 0))[0], _with_own_row(l, own[n]))[0]
                for n, l in zip(names, lands)}
        if grp == "mix0":
            return {"w_in": _w_in_local(full["ev_w_in"]), "w_uq": _w_uq_local(full["ev_w_uq"]),
                    "w_ukv": _w_ukv_local(full["ev_w_ukv"]), "ev_w_out": full["ev_w_out"]}
        if grp == "mix1":
            return full
        layer = grp[-1]
        return {"w_gu" + layer: jnp.concatenate([full["w_gate" + layer], full["w_up" + layer]], axis=1),
                "w_down" + layer: full["w_down" + layer]}

    scatter = {}

    def put_grads(grp, g):
        if grp == "mix0":
            g = {"ev_w_in": _w_in_grad(g["w_in"]), "ev_w_uq": _w_uq_grad(g["w_uq"]),
                 "ev_w_ukv": _w_ukv_grad(g["w_ukv"]), "ev_w_out": g["ev_w_out"]}
        elif grp != "mix1":
            layer = grp[-1]
            g = {"w_gate" + layer: g["w_gu"][:, :D_FF], "w_up" + layer: g["w_gu"][:, D_FF:],
                 "w_down" + layer: g["w_down"]}
        names = _GROUPS[grp]
        send = [_shards_from_full(_GROUP_SRC.get(n, (n, 0))[0], g[n][None]).astype(BF16) for n in names]
        scatter[grp] = (_exchange_start(send, [True] * len(names), send[0], name="scatter_start_" + grp)[0], send)

    small = {"g_cq": ev_g_cq, "g_ckv": ev_g_ckv, "od_rel_bias": od_rel_bias[0],
             "g_mix": g_mix + token[0, 0], "g_ffn": g_ffn, "g_final": g_final.reshape(1, -1)}
    loss_part, dx, G = _local_step(x[0], loss_target[0], small, get_weights, put_grads)
    loss = lax.psum(loss_part, ("x", "y", "c"))
    g_small = _pack_small([G["g_cq"], G["g_ckv"], G["od_rel_bias"], G["g_mix"], G["g_ffn"], G["g_final"]])
    small_handle, _ = _exchange_start([g_small], [False], dx, name="gather_start_small")

    grads, deltas, new_m, new_v = {}, {}, {}, {}
    parts, after = {}, dx
    me = 4 * lax.axis_index("x") + 2 * lax.axis_index("y") + lax.axis_index("c")
    for grp in ("ffn1", "mix1", "ffn0", "mix0"):
        handle, send = scatter[grp]
        lands = _exchange_wait(handle, after, name="scatter_wait_" + grp)
        for n, l, s in zip(_GROUPS[grp], lands, send):
            parts[n] = _with_own_row(l, lax.dynamic_index_in_dim(s, me, 0, keepdims=False))
        after = lands[0]
    for n in ("w_gate", "w_up", "w_down"):
        parts[n] = jnp.concatenate([parts[n + "0"], parts[n + "1"]], axis=1)
    for n in _BIG:
        shp = w[n].shape
        r2 = (-1, shp[-1])
        res = _adamw(w[n].reshape(r2), parts[n].reshape((N_DEV,) + w[n].reshape(r2).shape),
                     mom[n].reshape(r2), var[n].reshape(r2), name="adamw_" + n)
        grads[n], deltas[n], new_m[n], new_v[n] = [r.reshape(shp) for r in res]
    small_w = [w[n] for n in _SMALL]
    small_parts = _with_own_row(_exchange_wait(small_handle, after, name="gather_wait_small")[0], g_small)
    res = _adamw(_pack_small(small_w), small_parts, _pack_small([mom[n] for n in _SMALL]),
                 _pack_small([var[n] for n in _SMALL]), name="adamw_small")
    for d, packed in zip((grads, deltas, new_m, new_v), res):
        for n, val in zip(_SMALL, _unpack_small(packed, small_w)):
            d[n] = val

    order = ["ev_w_in", "ev_g_cq", "ev_w_uq", "ev_g_ckv", "ev_w_ukv", "ev_w_out", "od_w_qkv", "od_rel_bias",
             "od_w_out", "g_mix", "g_ffn", "w_gate", "w_up", "w_down", "g_final"]
    out = [loss, dx[None]]
    for d in (grads, deltas, new_m, new_v):
        out += [d[n] for n in order]
    return tuple(out)
```

```python
import functools

import numpy as np
import jax
import jax.numpy as jnp
from jax import lax
from jax.experimental import pallas as pl
from jax.experimental.pallas import tpu as pltpu

F32 = jnp.float32
BF16 = jnp.bfloat16

D_MODEL = 1024
CHUNK = 64
MLA_HEADS = 8
MLA_NOPE = 64
MLA_ROPE = 32
MLA_V = 64
Q_LORA = 384
KV_LORA = 256
ROPE_THETA = 10000.0
SB_HEADS = 8
SB_DIM = 64
C_HEADS = 16
C_DIM = 64
LEFT_CHUNKS = 8
REL_CLIP = 256
D_FF = 2816
RMS_EPS = 1e-6
ADAM_LR = 0.001
ADAM_B1 = 0.9
ADAM_B2 = 0.999
ADAM_EPS = 1e-08
ADAM_WD = 0.01
ADAM_STEP = 10

N_DEV = 8
LANES = 128
VMEM_LIMIT = 56 * 1024 * 1024
NEG = -1e30
PAD_KEYS = LEFT_CHUNKS * CHUNK
BAND_TQ = 128
BAND_W = BAND_TQ + PAD_KEYS
TOEP_W = BAND_W + BAND_TQ

NN = (((1,), (0,)), ((), ()))
NT = (((1,), (1,)), ((), ()))
TN = (((0,), (0,)), ((), ()))


def _dot(a, b, dn):
    return lax.dot_general(a, b, dn, preferred_element_type=F32)


def _pick(dim, pref):
    if dim <= pref:
        return dim
    best = None
    for t in range(LANES, pref + 1, LANES):
        if dim % t == 0:
            best = t
    assert best is not None, (dim, pref)
    return best


def _params(sem):
    return pltpu.CompilerParams(dimension_semantics=sem, vmem_limit_bytes=VMEM_LIMIT)


def _mm(a, b, dims="nn", res=None, out_dtype=F32, name="mm"):
    if dims == "nn":
        (M, K), (K2, N) = a.shape, b.shape
    elif dims == "nt":
        (M, K), (N, K2) = a.shape, b.shape
    else:
        (K, M), (K2, N) = a.shape, b.shape
    assert K == K2, (a.shape, b.shape, dims)
    tm, tn, tk = _pick(M, 512), _pick(N, 1408), _pick(K, 1408)
    nk = K // tk
    dn = {"nn": NN, "nt": NT, "tn": TN}[dims]
    has_res = res is not None

    def body(*refs):
        if has_res:
            a_ref, b_ref, r_ref, o_ref, acc = refs
        else:
            a_ref, b_ref, o_ref, acc = refs
        k = pl.program_id(2)

        @pl.when(k == 0)
        def _():
            acc[...] = jnp.zeros_like(acc)

        acc[...] += _dot(a_ref[...].astype(BF16), b_ref[...].astype(BF16), dn)

        @pl.when(k == nk - 1)
        def _():
            r = acc[...]
            if has_res:
                r = r + r_ref[...]
            o_ref[...] = r.astype(out_dtype)

    a_spec = (pl.BlockSpec((tk, tm), lambda i, j, k: (k, i)) if dims == "tn"
              else pl.BlockSpec((tm, tk), lambda i, j, k: (i, k)))
    b_spec = (pl.BlockSpec((tn, tk), lambda i, j, k: (j, k)) if dims == "nt"
              else pl.BlockSpec((tk, tn), lambda i, j, k: (k, j)))
    o_spec = pl.BlockSpec((tm, tn), lambda i, j, k: (i, j))
    in_specs = [a_spec, b_spec] + ([o_spec] if has_res else [])
    args = (a, b) + ((res,) if has_res else ())
    return pl.pallas_call(
        body, name=name, grid=(M // tm, N // tn, nk),
        in_specs=in_specs, out_specs=o_spec,
        out_shape=jax.ShapeDtypeStruct((M, N), out_dtype),
        scratch_shapes=[pltpu.VMEM((tm, tn), F32)],
        compiler_params=_params(("parallel", "parallel", "arbitrary")),
    )(*args)


def _rms_fwd(x, g, out_dtype=BF16, name="rms_fwd"):
    T, Fd = x.shape
    tm = _pick(T, 256)

    def body(x_ref, g_ref, o_ref):
        xv = x_ref[...]
        r = lax.rsqrt(jnp.mean(xv * xv, axis=-1, keepdims=True) + RMS_EPS)
        o_ref[...] = (xv * r * g_ref[...]).astype(out_dtype)

    return pl.pallas_call(
        body, name=name, grid=(T // tm,),
        in_specs=[pl.BlockSpec((tm, Fd), lambda i: (i, 0)), pl.BlockSpec((1, Fd), lambda i: (0, 0))],
        out_specs=pl.BlockSpec((tm, Fd), lambda i: (i, 0)),
        out_shape=jax.ShapeDtypeStruct((T, Fd), out_dtype),
        compiler_params=_params(("parallel",)),
    )(x, g)


def _rms_bwd(x, g, dy, dres=None, name="rms_bwd"):
    T, Fd = x.shape
    tm = _pick(T, 256)
    has_res = dres is not None

    def body(*refs):
        if has_res:
            x_ref, g_ref, dy_ref, r_ref, dx_ref, dg_ref = refs
        else:
            x_ref, g_ref, dy_ref, dx_ref, dg_ref = refs
        xv, dyv = x_ref[...], dy_ref[...]
        r = lax.rsqrt(jnp.mean(xv * xv, axis=-1, keepdims=True) + RMS_EPS)
        gdy = dyv * g_ref[...]
        dot = jnp.mean(xv * gdy, axis=-1, keepdims=True)
        dx = r * gdy - xv * (r * r * r * dot)
        if has_res:
            dx = dx + r_ref[...]
        dx_ref[...] = dx

        @pl.when(pl.program_id(0) == 0)
        def _():
            dg_ref[...] = jnp.zeros_like(dg_ref)

        dg_ref[...] += jnp.sum(dyv * xv * r, axis=0, keepdims=True)

    row = pl.BlockSpec((tm, Fd), lambda i: (i, 0))
    vec = pl.BlockSpec((1, Fd), lambda i: (0, 0))
    in_specs = [row, vec, row] + ([row] if has_res else [])
    args = (x, g, dy) + ((dres,) if has_res else ())
    return pl.pallas_call(
        body, name=name, grid=(T // tm,),
        in_specs=in_specs, out_specs=[row, vec],
        out_shape=[jax.ShapeDtypeStruct((T, Fd), F32), jax.ShapeDtypeStruct((1, Fd), F32)],
        compiler_params=_params(("arbitrary",)),
    )(*args)


def _loss_head(h, g, target, name="loss_head"):
    T, Fd = h.shape
    tm = _pick(T, 256)

    def body(h_ref, g_ref, t_ref, loss_ref, dh_ref, dg_ref):
        xv = h_ref[...]
        r = lax.rsqrt(jnp.mean(xv * xv, axis=-1, keepdims=True) + RMS_EPS)
        diff = xv * r * g_ref[...] - t_ref[...]
        part = 0.5 * jnp.sum(jnp.mean(diff * diff, axis=-1, keepdims=True), axis=0, keepdims=True)
        dyv = diff * (1.0 / Fd)
        gdy = dyv * g_ref[...]
        dot = jnp.mean(xv * gdy, axis=-1, keepdims=True)
        dh_ref[...] = r * gdy - xv * (r * r * r * dot)

        @pl.when(pl.program_id(0) == 0)
        def _():
            dg_ref[...] = jnp.zeros_like(dg_ref)
            loss_ref[...] = jnp.zeros_like(loss_ref)

        dg_ref[...] += jnp.sum(dyv * xv * r, axis=0, keepdims=True)
        loss_ref[...] += jnp.broadcast_to(part, loss_ref.shape)

    row = pl.BlockSpec((tm, Fd), lambda i: (i, 0))
    vec = pl.BlockSpec((1, Fd), lambda i: (0, 0))
    return pl.pallas_call(
        body, name=name, grid=(T // tm,),
        in_specs=[row, vec, row],
        out_specs=[pl.BlockSpec((1, LANES), lambda i: (0, 0)), row, vec],
        out_shape=[jax.ShapeDtypeStruct((1, LANES), F32), jax.ShapeDtypeStruct((T, Fd), F32),
                   jax.ShapeDtypeStruct((1, Fd), F32)],
        compiler_params=_params(("arbitrary",)),
    )(h, g, target)


def _swiglu_fwd(ab, name="swiglu_fwd"):
    T, two_f = ab.shape
    Fh = two_f // 2
    tm, tc = _pick(T, 256), _pick(Fh, 1408)
    nc = Fh // tc

    def body(a_ref, b_ref, s_ref):
        a = a_ref[...]
        s_ref[...] = (a * jax.nn.sigmoid(a) * b_ref[...]).astype(BF16)

    return pl.pallas_call(
        body, name=name, grid=(T // tm, nc),
        in_specs=[pl.BlockSpec((tm, tc), lambda i, j: (i, j)),
                  pl.BlockSpec((tm, tc), lambda i, j: (i, j + nc))],
        out_specs=pl.BlockSpec((tm, tc), lambda i, j: (i, j)),
        out_shape=jax.ShapeDtypeStruct((T, Fh), BF16),
        compiler_params=_params(("parallel", "parallel")),
    )(ab, ab)


def _swiglu_bwd(ab, ds, name="swiglu_bwd"):
    T, two_f = ab.shape
    Fh = two_f // 2
    tm, tc = _pick(T, 256), _pick(Fh, 1408)
    nc = Fh // tc

    def body(a_ref, b_ref, ds_ref, o_ref):
        j = pl.program_id(1)
        a, dsv = a_ref[...], ds_ref[...]
        sig = jax.nn.sigmoid(a)

        @pl.when(j < nc)
        def _():
            o_ref[...] = dsv * b_ref[...] * (sig * (1.0 + a * (1.0 - sig)))

        @pl.when(j >= nc)
        def _():
            o_ref[...] = dsv * (a * sig)

    return pl.pallas_call(
        body, name=name, grid=(T // tm, 2 * nc),
        in_specs=[pl.BlockSpec((tm, tc), lambda i, j: (i, j % nc)),
                  pl.BlockSpec((tm, tc), lambda i, j: (i, j % nc + nc)),
                  pl.BlockSpec((tm, tc), lambda i, j: (i, j % nc))],
        out_specs=pl.BlockSpec((tm, tc), lambda i, j: (i, j)),
        out_shape=jax.ShapeDtypeStruct((T, two_f), F32),
        compiler_params=_params(("parallel", "parallel")),
    )(ab, ab, ds)


def _rope(x, cos_t, sin_t, col0, ncols, out_dtype, name="rope"):
    T = x.shape[0]
    wt = cos_t.shape[1]
    tm = _pick(T, 256)
    nb = ncols * LANES // wt
    half = MLA_ROPE // 2

    def body(x_ref, c_ref, s_ref, o_ref):
        xv = x_ref[...].astype(F32)
        lane = lax.broadcasted_iota(jnp.int32, xv.shape, 1)
        first = (lane & (MLA_ROPE - 1)) < half
        swapped = jnp.where(first, pltpu.roll(xv, wt - half, 1), pltpu.roll(xv, half, 1))
        o_ref[...] = (xv * c_ref[...] + swapped * s_ref[...]).astype(out_dtype)

    off = col0 * LANES // wt
    return pl.pallas_call(
        body, name=name, grid=(T // tm, nb),
        in_specs=[pl.BlockSpec((tm, wt), lambda i, j: (i, j + off)),
                  pl.BlockSpec((tm, wt), lambda i, j: (i, 0)),
                  pl.BlockSpec((tm, wt), lambda i, j: (i, 0))],
        out_specs=pl.BlockSpec((tm, wt), lambda i, j: (i, j)),
        out_shape=jax.ShapeDtypeStruct((T, ncols * LANES), out_dtype),
        compiler_params=_params(("parallel", "parallel")),
    )(x, cos_t, sin_t)


ATT_T = 256


def _mla_masks(shape):
    lane = lax.broadcasted_iota(jnp.int32, shape, 1)
    m0 = (lane < 64) | ((lane >= 128) & (lane < 160))
    m1 = ((lane >= 64) & (lane < 128)) | ((lane >= 160) & (lane < 192))
    return m0, m1


def _chunk_ok(m_idx, kb, tq, tk):
    tpos = m_idx * tq + lax.broadcasted_iota(jnp.int32, (tq, tk), 0)
    spos = kb * tk + lax.broadcasted_iota(jnp.int32, (tq, tk), 1)
    return (spos >> 6) <= (tpos >> 6)


def _mla_fwd(q, kv, kr, name="mla_fwd"):
    T = q.shape[0]
    tq = tk = _pick(T, ATT_T)
    npair = MLA_HEADS // 2
    scale = (MLA_NOPE + MLA_ROPE) ** -0.5

    def body(q_ref, kn_ref, v_ref, kr_ref, o_ref, lse_ref):
        m_idx = pl.program_id(1)
        qv = q_ref[...]
        m0, m1 = _mla_masks(qv.shape)
        qh = (jnp.where(m0, qv, 0).astype(BF16), jnp.where(m1, qv, 0).astype(BF16))

        def step(kb, carry):
            ks = pl.ds(pl.multiple_of(kb * tk, tk), tk)
            kcat = jnp.concatenate([kn_ref[ks, :], kr_ref[ks, :]], axis=1)
            vv = v_ref[ks, :]
            ok = _chunk_ok(m_idx, kb, tq, tk)
            out = []
            for h in range(2):
                mx, l, acc = carry[3 * h:3 * h + 3]
                s = jnp.where(ok, _dot(qh[h], kcat, NT) * scale, NEG)
                mn = jnp.maximum(mx, jnp.max(s, axis=-1, keepdims=True))
                alpha = jnp.exp(mx - mn)
                p = jnp.exp(s - mn)
                l = alpha * l + jnp.sum(p, axis=-1, keepdims=True)
                acc = alpha * acc + _dot(p.astype(BF16), vv, NN)
                out += [mn, l, acc]
            return tuple(out)

        init = (jnp.full((tq, 1), NEG, F32), jnp.zeros((tq, 1), F32), jnp.zeros((tq, LANES), F32)) * 2
        res = lax.fori_loop(0, m_idx + 1, step, init)
        lane = lax.broadcasted_iota(jnp.int32, (tq, LANES), 1)
        o0 = res[2] / res[1]
        o1 = res[5] / res[4]
        o_ref[...] = jnp.where(lane < 64, o0, o1).astype(o_ref.dtype)
        lse_ref[...] = jnp.where(lane < 64, res[0] + jnp.log(res[1]), res[3] + jnp.log(res[4]))

    full = lambda col: pl.BlockSpec((T, LANES), col)
    return pl.pallas_call(
        body, name=name, grid=(npair, T // tq),
        in_specs=[pl.BlockSpec((tq, 2 * LANES), lambda p, m: (m, p)),
                  full(lambda p, m: (0, p)), full(lambda p, m: (0, npair + p)), full(lambda p, m: (0, 0))],
        out_specs=[pl.BlockSpec((tq, LANES), lambda p, m: (m, p)),
                   pl.BlockSpec((tq, LANES), lambda p, m: (m, p))],
        out_shape=[jax.ShapeDtypeStruct((T, npair * LANES), BF16),
                   jax.ShapeDtypeStruct((T, npair * LANES), F32)],
        compiler_params=_params(("parallel", "arbitrary")),
    )(q, kv, kv, kr)


def _mla_bwd(q, kv, kr, o, lse, do, do_col0, name="mla_bwd"):
    T = q.shape[0]
    tq = tk = _pick(T, ATT_T)
    npair = MLA_HEADS // 2
    scale = (MLA_NOPE + MLA_ROPE) ** -0.5

    def body(q_ref, kn_ref, v_ref, kr_ref, o_ref, lse_ref, do_ref, dq_ref, dkn_ref, dv_ref, dkr_ref):
        p_idx, m_idx = pl.program_id(0), pl.program_id(1)

        @pl.when(m_idx == 0)
        def _():
            dkn_ref[...] = jnp.zeros_like(dkn_ref)
            dv_ref[...] = jnp.zeros_like(dv_ref)

        @pl.when((m_idx == 0) & (p_idx == 0))
        def _():
            dkr_ref[...] = jnp.zeros_like(dkr_ref)

        qv = q_ref[...]
        m0, m1 = _mla_masks(qv.shape)
        qh = (jnp.where(m0, qv, 0).astype(BF16), jnp.where(m1, qv, 0).astype(BF16))
        dov = do_ref[...].astype(F32)
        lane = lax.broadcasted_iota(jnp.int32, (tq, LANES), 1)
        h0 = lane < 64
        prod = dov * o_ref[...].astype(F32)
        delta = (jnp.sum(jnp.where(h0, prod, 0.0), axis=-1, keepdims=True),
                 jnp.sum(jnp.where(h0, 0.0, prod), axis=-1, keepdims=True))
        doh = (jnp.where(h0, dov, 0.0).astype(BF16), jnp.where(h0, 0.0, dov).astype(BF16))
        lsev = lse_ref[...]
        lse_h = (lsev[:, 0:1], lsev[:, 64:65])

        def step(kb, carry):
            ks = pl.ds(pl.multiple_of(kb * tk, tk), tk)
            kcat = jnp.concatenate([kn_ref[ks, :], kr_ref[ks, :]], axis=1)
            vv = v_ref[ks, :]
            ok = _chunk_ok(m_idx, kb, tq, tk)
            dkc = jnp.zeros((tk, 2 * LANES), F32)
            dvv = jnp.zeros((tk, LANES), F32)
            out = []
            for h in range(2):
                s = _dot(qh[h], kcat, NT) * scale
                p = jnp.where(ok, jnp.exp(s - lse_h[h]), 0.0)
                dp = _dot(doh[h], vv, NT)
                ds = (p * (dp - delta[h]) * scale).astype(BF16)
                out.append(carry[h] + _dot(ds, kcat, NN))
                dkc = dkc + _dot(ds, qh[h], TN)
                dvv = dvv + _dot(p.astype(BF16), doh[h], TN)
            dkn_ref[ks, :] += dkc[:, :LANES]
            dkr_ref[ks, :] += dkc[:, LANES:]
            dv_ref[ks, :] += dvv
            return tuple(out)

        init = (jnp.zeros((tq, 2 * LANES), F32),) * 2
        dq0, dq1 = lax.fori_loop(0, m_idx + 1, step, init)
        dq_ref[...] = jnp.where(m0, dq0, jnp.where(m1, dq1, 0.0))

    full = lambda col: pl.BlockSpec((T, LANES), col)
    blk = lambda col: pl.BlockSpec((tq, LANES), col)
    return pl.pallas_call(
        body, name=name, grid=(npair, T // tq),
        in_specs=[pl.BlockSpec((tq, 2 * LANES), lambda p, m: (m, p)),
                  full(lambda p, m: (0, p)), full(lambda p, m: (0, npair + p)), full(lambda p, m: (0, 0)),
                  blk(lambda p, m: (m, p)), blk(lambda p, m: (m, p)),
                  blk(lambda p, m: (m, do_col0 + p))],
        out_specs=[pl.BlockSpec((tq, 2 * LANES), lambda p, m: (m, p)),
                   full(lambda p, m: (0, p)), full(lambda p, m: (0, p)), full(lambda p, m: (0, 0))],
        out_shape=[jax.ShapeDtypeStruct((T, npair * 2 * LANES), F32),
                   jax.ShapeDtypeStruct((T, npair * LANES), F32),
                   jax.ShapeDtypeStruct((T, npair * LANES), F32),
                   jax.ShapeDtypeStruct((T, LANES), F32)],
        compiler_params=_params(("arbitrary", "arbitrary")),
    )(q, kv, kv, kr, o, lse, do)


def _split_dot(x, tri):
    hi = x.astype(BF16)
    lo = (x - hi.astype(F32)).astype(BF16)
    return _dot(hi, tri, NN) + _dot(lo, tri, NN)


def _sb_terms(qh, kk, m_idx, kb, tq, tk, scale):
    z = _dot(qh, kk, NT) * scale
    tpos = m_idx * tq + lax.broadcasted_iota(jnp.int32, (tq, tk), 0)
    spos = kb * tk + lax.broadcasted_iota(jnp.int32, (tq, tk), 1)
    before = spos < tpos
    sp = jnp.maximum(z, 0.0) + jnp.log(1.0 + jnp.exp(-jnp.abs(z)))
    lk = jnp.where(before, -sp, 0.0)
    return z, sp, lk, before


def _sb_fwd(qkv, col0, name="sb_fwd"):
    T = qkv.shape[0]
    tq = tk = _pick(T, ATT_T)
    npair = SB_HEADS // 2
    scale = SB_DIM ** -0.5

    def body(q_ref, k_ref, v_ref, o_ref):
        m_idx = pl.program_id(1)
        qv = q_ref[...].astype(BF16)
        lane = lax.broadcasted_iota(jnp.int32, (tq, LANES), 1)
        h0 = lane < 64
        qh = (jnp.where(h0, qv, 0).astype(BF16), jnp.where(h0, 0, qv).astype(BF16))
        row = lax.broadcasted_iota(jnp.int32, (tk, tk), 0)
        col = lax.broadcasted_iota(jnp.int32, (tk, tk), 1)
        later = (row > col).astype(BF16)

        def step(i, carry):
            kb = m_idx - i
            ks = pl.ds(pl.multiple_of(kb * tk, tk), tk)
            kk = k_ref[ks, :].astype(BF16)
            vv = v_ref[ks, :].astype(BF16)
            out = []
            for h in range(2):
                c, acc = carry[2 * h:2 * h + 2]
                z, sp, lk, before = _sb_terms(qh[h], kk, m_idx, kb, tq, tk, scale)
                a = (z - sp) + _split_dot(lk, later) + c
                w = jnp.where(before, jnp.exp(a), 0.0)
                out += [c + jnp.sum(lk, axis=-1, keepdims=True), acc + _dot(w.astype(BF16), vv, NN)]
            return tuple(out)

        init = (jnp.zeros((tq, 1), F32), jnp.zeros((tq, LANES), F32)) * 2
        res = lax.fori_loop(0, m_idx + 1, step, init)
        o_ref[...] = jnp.where(h0, res[1], res[3]).astype(o_ref.dtype)

    full = lambda col: pl.BlockSpec((T, LANES), col)
    return pl.pallas_call(
        body, name=name, grid=(npair, T // tq),
        in_specs=[pl.BlockSpec((tq, LANES), lambda p, m: (m, col0 + p)),
                  full(lambda p, m: (0, col0 + npair + p)), full(lambda p, m: (0, col0 + 2 * npair + p))],
        out_specs=pl.BlockSpec((tq, LANES), lambda p, m: (m, p)),
        out_shape=jax.ShapeDtypeStruct((T, npair * LANES), BF16),
        compiler_params=_params(("parallel", "arbitrary")),
    )(qkv, qkv, qkv)


def _sb_bwd(qkv, col0, do, do_col0, name="sb_bwd"):
    T = qkv.shape[0]
    tq = tk = _pick(T, ATT_T)
    npair = SB_HEADS // 2
    scale = SB_DIM ** -0.5

    def body(q_ref, k_ref, v_ref, do_ref, dq_ref, dk_ref, dv_ref, e_scr, z_scr, sp_scr):
        m_idx = pl.program_id(1)

        @pl.when(m_idx == 0)
        def _():
            dk_ref[...] = jnp.zeros_like(dk_ref)
            dv_ref[...] = jnp.zeros_like(dv_ref)

        qv = q_ref[...].astype(BF16)
        lane = lax.broadcasted_iota(jnp.int32, (tq, LANES), 1)
        h0 = lane < 64
        qh = (jnp.where(h0, qv, 0).astype(BF16), jnp.where(h0, 0, qv).astype(BF16))
        dov = do_ref[...].astype(F32)
        doh = (jnp.where(h0, dov, 0.0).astype(BF16), jnp.where(h0, 0.0, dov).astype(BF16))
        row = lax.broadcasted_iota(jnp.int32, (tk, tk), 0)
        col = lax.broadcasted_iota(jnp.int32, (tk, tk), 1)
        later = (row > col).astype(BF16)
        earlier = (row < col).astype(BF16)
        dq = []
        for h in range(2):
            def rl(i, c, h=h):
                kb = m_idx - i
                ks = pl.ds(pl.multiple_of(kb * tk, tk), tk)
                kk = k_ref[ks, :].astype(BF16)
                vv = v_ref[ks, :].astype(BF16)
                z, sp, lk, before = _sb_terms(qh[h], kk, m_idx, kb, tq, tk, scale)
                a = (z - sp) + _split_dot(lk, later) + c
                w = jnp.where(before, jnp.exp(a), 0.0)
                dw = _dot(doh[h], vv, NT)
                e_scr[kb] = w * dw
                z_scr[kb] = z
                sp_scr[kb] = sp
                dv_ref[ks, :] += _dot(w.astype(BF16), doh[h], TN)
                return c + jnp.sum(lk, axis=-1, keepdims=True)

            lax.fori_loop(0, m_idx + 1, rl, jnp.zeros((tq, 1), F32))

            def lr(kb, carry, h=h):
                esum, dqa = carry
                ks = pl.ds(pl.multiple_of(kb * tk, tk), tk)
                kk = k_ref[ks, :].astype(BF16)
                e, z, sp = e_scr[kb], z_scr[kb], sp_scr[kb]
                tpos = m_idx * tq + lax.broadcasted_iota(jnp.int32, (tq, tk), 0)
                spos = kb * tk + lax.broadcasted_iota(jnp.int32, (tq, tk), 1)
                prev = _split_dot(e, earlier) + esum
                dz = jnp.where(spos < tpos, e * jnp.exp(-sp) - jnp.exp(z - sp) * prev, 0.0)
                dzb = (dz * scale).astype(BF16)
                dk_ref[ks, :] += _dot(dzb, qh[h], TN)
                return esum + jnp.sum(e, axis=-1, keepdims=True), dqa + _dot(dzb, kk, NN)

            _, dqh = lax.fori_loop(0, m_idx + 1, lr, (jnp.zeros((tq, 1), F32), jnp.zeros((tq, LANES), F32)))
            dq.append(dqh)
        dq_ref[...] = jnp.where(h0, dq[0], dq[1])

    full = lambda col: pl.BlockSpec((T, LANES), col)
    blk = lambda col: pl.BlockSpec((tq, LANES), col)
    return pl.pallas_call(
        body, name=name, grid=(npair, T // tq),
        in_specs=[blk(lambda p, m: (m, col0 + p)),
                  full(lambda p, m: (0, col0 + npair + p)), full(lambda p, m: (0, col0 + 2 * npair + p)),
                  blk(lambda p, m: (m, do_col0 + p))],
        out_specs=[blk(lambda p, m: (m, p)), full(lambda p, m: (0, p)), full(lambda p, m: (0, p))],
        out_shape=[jax.ShapeDtypeStruct((T, npair * LANES), F32)] * 3,
        scratch_shapes=[pltpu.VMEM((T // tk, tq, tk), F32)] * 3,
        compiler_params=_params(("arbitrary", "arbitrary")),
    )(qkv, qkv, qkv, do)


def _band_valid(m_idx):
    qi = lax.broadcasted_iota(jnp.int32, (BAND_TQ, BAND_W), 0)
    j = lax.broadcasted_iota(jnp.int32, (BAND_TQ, BAND_W), 1)
    cq = (m_idx * BAND_TQ + qi) >> 6
    ckp = (m_idx * BAND_TQ + j) >> 6
    return (ckp >= LEFT_CHUNKS) & (ckp >= cq) & (ckp <= cq + LEFT_CHUNKS)


def _band_probs(qh, kw, bias, valid, scale):
    s = _dot(qh, kw, NT) * scale + bias
    s = jnp.where(valid, s, NEG)
    e = jnp.exp(s - jnp.max(s, axis=-1, keepdims=True))
    return e / jnp.sum(e, axis=-1, keepdims=True)


def _band_fwd(qkv, k_pad, v_pad, bias_w, name="band_fwd"):
    T = qkv.shape[0]
    npair = C_HEADS // 2
    scale = C_DIM ** -0.5

    def body(q_ref, k_ref, v_ref, b_ref, o_ref):
        m_idx = pl.program_id(1)
        win = pl.ds(pl.multiple_of(m_idx * BAND_TQ, BAND_TQ), BAND_W)
        kw, vw = k_ref[win, :], v_ref[win, :]
        qv = q_ref[...]
        lane = lax.broadcasted_iota(jnp.int32, (BAND_TQ, LANES), 1)
        h0 = lane < 64
        qh = (jnp.where(h0, qv, 0).astype(BF16), jnp.where(h0, 0, qv).astype(BF16))
        valid = _band_valid(m_idx)
        o = [_dot(_band_probs(qh[h], kw, b_ref[h], valid, scale).astype(BF16), vw, NN) for h in range(2)]
        o_ref[...] = jnp.where(h0, o[0], o[1]).astype(o_ref.dtype)

    Tp = T + PAD_KEYS
    return pl.pallas_call(
        body, name=name, grid=(npair, T // BAND_TQ),
        in_specs=[pl.BlockSpec((BAND_TQ, LANES), lambda p, m: (m, p)),
                  pl.BlockSpec((Tp, LANES), lambda p, m: (0, p)),
                  pl.BlockSpec((Tp, LANES), lambda p, m: (0, p)),
                  pl.BlockSpec((2, BAND_TQ, BAND_W), lambda p, m: (p, 0, 0))],
        out_specs=pl.BlockSpec((BAND_TQ, LANES), lambda p, m: (m, p)),
        out_shape=jax.ShapeDtypeStruct((T, npair * LANES), BF16),
        compiler_params=_params(("parallel", "arbitrary")),
    )(qkv, k_pad, v_pad, bias_w)


def _band_bwd(qkv, k_pad, v_pad, bias_w, do, name="band_bwd"):
    T = qkv.shape[0]
    npair = C_HEADS // 2
    scale = C_DIM ** -0.5

    def body(q_ref, k_ref, v_ref, b_ref, do_ref, dq_ref, dk_ref, dv_ref, db_ref):
        m_idx = pl.program_id(1)

        @pl.when(m_idx == 0)
        def _():
            dk_ref[...] = jnp.zeros_like(dk_ref)
            dv_ref[...] = jnp.zeros_like(dv_ref)
            db_ref[...] = jnp.zeros_like(db_ref)

        win = pl.ds(pl.multiple_of(m_idx * BAND_TQ, BAND_TQ), BAND_W)
        kw, vw = k_ref[win, :], v_ref[win, :]
        qv = q_ref[...]
        dov = do_ref[...].astype(F32)
        lane = lax.broadcasted_iota(jnp.int32, (BAND_TQ, LANES), 1)
        h0 = lane < 64
        qh = (jnp.where(h0, qv, 0).astype(BF16), jnp.where(h0, 0, qv).astype(BF16))
        doh = (jnp.where(h0, dov, 0.0).astype(BF16), jnp.where(h0, 0.0, dov).astype(BF16))
        valid = _band_valid(m_idx)
        dq = []
        dkw = jnp.zeros((BAND_W, LANES), F32)
        dvw = jnp.zeros((BAND_W, LANES), F32)
        for h in range(2):
            p = _band_probs(qh[h], kw, b_ref[h], valid, scale)
            dp = _dot(doh[h], vw, NT)
            dsb = p * (dp - jnp.sum(p * dp, axis=-1, keepdims=True))
            db_ref[h] += dsb
            dsq = (dsb * scale).astype(BF16)
            dq.append(_dot(dsq, kw, NN))
            dkw = dkw + _dot(dsq, qh[h], TN)
            dvw = dvw + _dot(p.astype(BF16), doh[h], TN)
        dq_ref[...] = jnp.where(h0, dq[0], dq[1])
        dk_ref[win, :] += dkw
        dv_ref[win, :] += dvw

    Tp = T + PAD_KEYS
    blk = lambda col: pl.BlockSpec((BAND_TQ, LANES), col)
    full = pl.BlockSpec((Tp, LANES), lambda p, m: (0, p))
    bias = pl.BlockSpec((2, BAND_TQ, BAND_W), lambda p, m: (p, 0, 0))
    return pl.pallas_call(
        body, name=name, grid=(npair, T // BAND_TQ),
        in_specs=[blk(lambda p, m: (m, p)), full, full, bias, blk(lambda p, m: (m, p))],
        out_specs=[blk(lambda p, m: (m, p)), full, full, bias],
        out_shape=[jax.ShapeDtypeStruct((T, npair * LANES), F32),
                   jax.ShapeDtypeStruct((Tp, npair * LANES), F32),
                   jax.ShapeDtypeStruct((Tp, npair * LANES), F32),
                   jax.ShapeDtypeStruct((C_HEADS, BAND_TQ, BAND_W), F32)],
        compiler_params=_params(("arbitrary", "arbitrary")),
    )(qkv, k_pad, v_pad, bias_w, do)


def _skew_bits(x, left):
    w = x.shape[1]
    row = lax.broadcasted_iota(jnp.int32, x.shape, 0)
    for b in range(BAND_TQ.bit_length() - 1):
        amt = (w - (1 << b)) if left else (1 << b)
        x = jnp.where((row >> b) & 1 == 1, pltpu.roll(x, amt, 1), x)
    return x


def _toeplitz(diag, name="toeplitz"):
    H = diag.shape[0]

    def body(d_ref, o_ref):
        x = jnp.broadcast_to(d_ref[0], (BAND_TQ, TOEP_W))
        o_ref[0] = _skew_bits(x, left=False)[:, BAND_TQ:]

    return pl.pallas_call(
        body, name=name, grid=(H,),
        in_specs=[pl.BlockSpec((1, 1, TOEP_W), lambda h: (h, 0, 0))],
        out_specs=pl.BlockSpec((1, BAND_TQ, BAND_W), lambda h: (h, 0, 0)),
        out_shape=jax.ShapeDtypeStruct((H, BAND_TQ, BAND_W), F32),
        compiler_params=_params(("parallel",)),
    )(diag.reshape(H, 1, TOEP_W))


def _toeplitz_bwd(dbias, name="toeplitz_bwd"):
    H = dbias.shape[0]

    def body(d_ref, o_ref):
        x = jnp.concatenate([jnp.zeros((BAND_TQ, BAND_TQ), F32), d_ref[0]], axis=1)
        o_ref[0] = jnp.sum(_skew_bits(x, left=True), axis=0, keepdims=True)

    return pl.pallas_call(
        body, name=name, grid=(H,),
        in_specs=[pl.BlockSpec((1, BAND_TQ, BAND_W), lambda h: (h, 0, 0))],
        out_specs=pl.BlockSpec((1, 1, TOEP_W), lambda h: (h, 0, 0)),
        out_shape=jax.ShapeDtypeStruct((H, 1, TOEP_W), F32),
        compiler_params=_params(("parallel",)),
    )(dbias).reshape(H, TOEP_W)


_HBM = pl.BlockSpec(memory_space=pltpu.HBM)
_SEM = pl.BlockSpec(memory_space=pltpu.SEMAPHORE)
_EFFECT = pltpu.SideEffectType.DATAFLOW_SIDE_EFFECTING


def _peers():
    x, y, c = lax.axis_index("x"), lax.axis_index("y"), lax.axis_index("c")
    out = []
    for k in range(1, N_DEV):
        peer = (1 - x if (k >> 2) & 1 else x, 1 - y if (k >> 1) & 1 else y, 1 - c if k & 1 else c)
        out.append((peer, 4 * peer[0] + 2 * peer[1] + peer[2]))
    return 4 * x + 2 * y + c, out


def _split_copies(ins, lands, scatter, send_sem, recv_sem):
    me, peers = _peers()
    starts, arrivals = [], []
    for a in range(len(ins)):
        for peer, idx in peers:
            src = ins[a].at[idx] if scatter[a] else ins[a]
            mk = lambda dst: pltpu.make_async_remote_copy(
                src_ref=src, dst_ref=dst, send_sem=send_sem, recv_sem=recv_sem,
                device_id=peer, device_id_type=pl.DeviceIdType.MESH)
            starts.append(mk(lands[a].at[me]))
            arrivals.append(mk(lands[a].at[idx]))
    return starts, arrivals


def _exchange_start(arrays, scatter, after, name):
    n = len(arrays)
    lands = [lax.empty((N_DEV,) + (a.shape[1:] if s else a.shape), a.dtype) for a, s in zip(arrays, scatter)]

    def body(*refs):
        ins, lnd = refs[:n], refs[n:2 * n]
        send_sem, recv_sem = refs[2 * n + 1:2 * n + 3]
        token = refs[-1]
        starts, _ = _split_copies(ins, lnd, scatter, send_sem, recv_sem)
        for cp in starts:
            cp.start()
        token[...] = jnp.zeros_like(token)

    hbm = lambda a: pltpu.HBM(a.shape, a.dtype)
    out = pl.pallas_call(
        body, name=name,
        out_shape=(pltpu.SemaphoreType.DMA(()), pltpu.SemaphoreType.DMA(()),
                   *[hbm(a) for a in arrays], *[hbm(a) for a in lands],
                   jax.ShapeDtypeStruct((8, LANES), F32)),
        in_specs=[_HBM] * (2 * n) + [pl.BlockSpec(memory_space=pl.ANY)],
        out_specs=(_SEM, _SEM, *([_HBM] * (2 * n)), pl.BlockSpec(memory_space=pltpu.VMEM)),
        input_output_aliases={i: 2 + i for i in range(2 * n)},
        compiler_params=pltpu.CompilerParams(has_side_effects=_EFFECT),
    )(*[pltpu.with_memory_space_constraint(a, pltpu.HBM) for a in list(arrays) + lands], after)
    return (out[0], out[1], list(out[2:2 + n]), list(out[2 + n:2 + 2 * n]), tuple(scatter)), out[-1]


def _exchange_wait(handle, after, name):
    send_sem, recv_sem, ins, lands, scatter = handle
    n = len(ins)

    def body(*refs):
        i_ref, l_ref = refs[:n], refs[n:2 * n]
        s_sem, r_sem = refs[2 * n:2 * n + 2]
        starts, arrivals = _split_copies(i_ref, l_ref, scatter, s_sem, r_sem)
        for cp in starts:
            cp.wait_send()
        for cp in arrivals:
            cp.wait_recv()

    hbm = lambda a: pltpu.HBM(a.shape, a.dtype)
    out = pl.pallas_call(
        body, name=name,
        out_shape=tuple(hbm(a) for a in ins + lands),
        in_specs=[_HBM] * (2 * n) + [_SEM, _SEM, pl.BlockSpec(memory_space=pl.ANY)],
        out_specs=tuple([_HBM] * (2 * n)),
        input_output_aliases={i: i for i in range(2 * n)},
        compiler_params=pltpu.CompilerParams(has_side_effects=_EFFECT),
    )(*ins, *lands, send_sem, recv_sem, after)
    return list(out[n:])


def _with_own_row(land, own):
    me = 4 * lax.axis_index("x") + 2 * lax.axis_index("y") + lax.axis_index("c")
    return lax.dynamic_update_index_in_dim(land, own.astype(land.dtype), me, 0)


def _adamw(w, parts, m, v, name="adamw"):
    R, C = w.shape
    tr = next(t for t in (256, 128, 64, 32, 16, 8) if R % t == 0)
    c1 = 1.0 - ADAM_B1 ** ADAM_STEP
    c2 = 1.0 - ADAM_B2 ** ADAM_STEP

    def body(w_ref, p_ref, m_ref, v_ref, g_ref, d_ref, nm_ref, nv_ref):
        g = p_ref[0].astype(F32)
        for i in range(1, N_DEV):
            g = g + p_ref[i].astype(F32)
        nm = ADAM_B1 * m_ref[...] + (1.0 - ADAM_B1) * g
        nv = ADAM_B2 * v_ref[...] + (1.0 - ADAM_B2) * (g * g)
        g_ref[...] = g
        nm_ref[...] = nm
        nv_ref[...] = nv
        d_ref[...] = -ADAM_LR * ((nm / c1) / (jnp.sqrt(nv / c2) + ADAM_EPS) + ADAM_WD * w_ref[...])

    blk = pl.BlockSpec((tr, C), lambda i: (i, 0))
    return pl.pallas_call(
        body, name=name, grid=(R // tr,),
        in_specs=[blk, pl.BlockSpec((N_DEV, tr, C), lambda i: (0, i, 0)), blk, blk],
        out_specs=[blk] * 4,
        out_shape=[jax.ShapeDtypeStruct((R, C), F32)] * 4,
        compiler_params=_params(("parallel",)),
    )(w, parts, m, v)


_O1 = Q_LORA
_O2 = _O1 + KV_LORA
_O3 = _O2 + MLA_ROPE
_NB = SB_HEADS * SB_DIM
IN_W = _O2 + LANES + 3 * _NB
COL_KR = _O2 // LANES
COL_SB = COL_KR + 1


def _w_in_local(w):
    kr = w[:, _O2:_O3]
    pad = jnp.zeros((w.shape[0], LANES - 2 * MLA_ROPE), w.dtype)
    return jnp.concatenate([w[:, :_O2], kr, kr, pad, w[:, _O3:]], axis=1)


def _w_in_grad(g):
    kr = g[:, _O2:_O2 + MLA_ROPE] + g[:, _O2 + MLA_ROPE:_O2 + 2 * MLA_ROPE]
    return jnp.concatenate([g[:, :_O2], kr, g[:, _O2 + LANES:]], axis=1)


def _w_uq_local(w):
    w3 = w.reshape(w.shape[0], MLA_HEADS // 2, 2, MLA_NOPE + MLA_ROPE)
    nope = w3[..., :MLA_NOPE].reshape(w.shape[0], MLA_HEADS // 2, 2 * MLA_NOPE)
    rope = w3[..., MLA_NOPE:].reshape(w.shape[0], MLA_HEADS // 2, 2 * MLA_ROPE)
    pad = jnp.zeros((w.shape[0], MLA_HEADS // 2, LANES - 2 * MLA_ROPE), w.dtype)
    return jnp.concatenate([nope, rope, pad], axis=2).reshape(w.shape[0], -1)


def _w_uq_grad(g):
    g3 = g.reshape(g.shape[0], MLA_HEADS // 2, 2 * LANES)
    nope = g3[..., :2 * MLA_NOPE].reshape(g.shape[0], MLA_HEADS // 2, 2, MLA_NOPE)
    rope = g3[..., LANES:LANES + 2 * MLA_ROPE].reshape(g.shape[0], MLA_HEADS // 2, 2, MLA_ROPE)
    return jnp.concatenate([nope, rope], axis=3).reshape(g.shape[0], -1)


def _w_ukv_local(w):
    w3 = w.reshape(w.shape[0], MLA_HEADS, MLA_NOPE + MLA_V)
    return jnp.concatenate([w3[..., :MLA_NOPE].reshape(w.shape[0], -1),
                            w3[..., MLA_NOPE:].reshape(w.shape[0], -1)], axis=1)


def _w_ukv_grad(g):
    half = MLA_HEADS * MLA_NOPE
    kn = g[:, :half].reshape(g.shape[0], MLA_HEADS, MLA_NOPE)
    vv = g[:, half:].reshape(g.shape[0], MLA_HEADS, MLA_V)
    return jnp.concatenate([kn, vv], axis=2).reshape(g.shape[0], -1)


def _rope_tables(T):
    pos = jnp.arange(T, dtype=F32)
    inv_freq = ROPE_THETA ** (-jnp.arange(0, MLA_ROPE, 2, dtype=F32) / MLA_ROPE)
    ang = pos[:, None] * inv_freq[None, :]
    cos, sin = jnp.cos(ang), jnp.sin(ang)
    ones = jnp.ones((T, LANES - 2 * MLA_ROPE), F32)
    cos_k = jnp.concatenate([cos, cos, cos, cos, ones], axis=1)
    sin_k = jnp.concatenate([-sin, sin, -sin, sin, 0.0 * ones], axis=1)
    cos_q = jnp.concatenate([jnp.ones((T, LANES), F32), cos_k], axis=1)
    sin_q = jnp.concatenate([jnp.zeros((T, LANES), F32), sin_k], axis=1)
    return cos_q, sin_q, cos_k, sin_k


def _bias_diag_index():
    ell = np.arange(TOEP_W)
    return np.clip(BAND_W - ell, -REL_CLIP, REL_CLIP) + REL_CLIP


def _local_step(x, target, small, get_weights, put_grads):
    T = x.shape[0]
    cos_q, sin_q, cos_k, sin_k = _rope_tables(T)
    G = {}
    W = dict(small)

    W.update(get_weights("mix0", None))
    u0 = _rms_fwd(x, W["g_mix"][0:1], name="rms_mix0")
    proj = _mm(u0, W["w_in"], name="proj_in")
    c_q, c_kv = proj[:, :_O1], proj[:, _O1:_O2]
    nq = _rms_fwd(c_q, W["g_cq"], name="rms_cq")
    nkv = _rms_fwd(c_kv, W["g_ckv"], name="rms_ckv")
    qa_raw = _mm(nq, W["w_uq"], name="proj_uq")
    qa = _rope(qa_raw, cos_q, sin_q, 0, qa_raw.shape[1] // LANES, BF16, name="rope_q")
    kv = _mm(nkv, W["w_ukv"], out_dtype=BF16, name="proj_ukv")
    kr = _rope(proj, cos_k, sin_k, COL_KR, 1, BF16, name="rope_k")
    o_a, lse = _mla_fwd(qa, kv, kr)
    o_b = _sb_fwd(proj, COL_SB)
    o_ab = jnp.concatenate([o_a, o_b], axis=1)
    h1 = _mm(o_ab, W["ev_w_out"], res=x, name="out_ev")

    def ffn_fwd(h, layer):
        W.update(get_weights(f"ffn{layer}", h))
        u = _rms_fwd(h, W["g_ffn"][layer:layer + 1], name=f"rms_ffn{layer}")
        ab = _mm(u, W[f"w_gu{layer}"], name=f"ffn_gu{layer}")
        s = _swiglu_fwd(ab, name=f"swiglu_fwd{layer}")
        return u, ab, s, _mm(s, W[f"w_down{layer}"], res=h, name=f"ffn_down{layer}")

    u1, ab0, s0, h2 = ffn_fwd(h1, 0)

    W.update(get_weights("mix1", h2))
    u2 = _rms_fwd(h2, W["g_mix"][1:2], name="rms_mix1")
    qkv = _mm(u2, W["od_w_qkv"], out_dtype=BF16, name="proj_qkv")
    nc = C_HEADS * C_DIM
    pad = ((PAD_KEYS, 0), (0, 0))
    k_pad, v_pad = jnp.pad(qkv[:, nc:2 * nc], pad), jnp.pad(qkv[:, 2 * nc:], pad)
    diag_idx = _bias_diag_index()
    bias_w = _toeplitz(W["od_rel_bias"][:, diag_idx])
    o_c = _band_fwd(qkv, k_pad, v_pad, bias_w)
    h3 = _mm(o_c, W["od_w_out"], res=h2, name="out_od")
    u3, ab1, s1, h4 = ffn_fwd(h3, 1)

    loss, dh, G["g_final"] = _loss_head(h4, W["g_final"], target)

    def ffn_bwd(dh, h, u, ab, s, layer):
        ds = _mm(dh, W[f"w_down{layer}"], dims="nt", name=f"ffn_down_dx{layer}")
        g_down = _mm(s, dh, dims="tn", name=f"ffn_down_dw{layer}")
        dab = _swiglu_bwd(ab, ds, name=f"swiglu_bwd{layer}")
        du = _mm(dab, W[f"w_gu{layer}"], dims="nt", name=f"ffn_gu_dx{layer}")
        g_gu = _mm(u, dab, dims="tn", name=f"ffn_gu_dw{layer}")
        tok = put_grads(f"ffn{layer}", {"w_gu": g_gu, "w_down": g_down})
        return _rms_bwd(h, W["g_ffn"][layer:layer + 1] + tok[:1, :1], du, dres=dh, name=f"rms_ffn_bwd{layer}")

    dh3, g_gffn1 = ffn_bwd(dh, h3, u3, ab1, s1, 1)

    do_c = _mm(dh3, W["od_w_out"], dims="nt", name="out_od_dx")
    g_od_out = _mm(o_c, dh3, dims="tn", name="out_od_dw")
    dq_c, dk_p, dv_p, dbias_w = _band_bwd(qkv, k_pad, v_pad, bias_w, do_c)
    dqkv = jnp.concatenate([dq_c, dk_p[PAD_KEYS:], dv_p[PAD_KEYS:]], axis=1)
    du2 = _mm(dqkv, W["od_w_qkv"], dims="nt", name="proj_qkv_dx")
    tok = put_grads("mix1", {"od_w_qkv": _mm(u2, dqkv, dims="tn", name="proj_qkv_dw"), "od_w_out": g_od_out})
    ddiag = _toeplitz_bwd(dbias_w)
    n_far = BAND_W - REL_CLIP + 1
    G["od_rel_bias"] = jnp.concatenate(
        [jnp.zeros((C_HEADS, REL_CLIP - BAND_TQ + 1), F32), ddiag[:, n_far:][:, ::-1],
         jnp.sum(ddiag[:, :n_far], axis=1, keepdims=True)], axis=1)
    dh2, g_gmix1 = _rms_bwd(h2, W["g_mix"][1:2] + tok[:1, :1], du2, dres=dh3, name="rms_mix_bwd1")

    dh1, g_gffn0 = ffn_bwd(dh2, h1, u1, ab0, s0, 0)
    G["g_ffn"] = jnp.concatenate([g_gffn0, g_gffn1], axis=0)

    do_ab = _mm(dh1, W["ev_w_out"], dims="nt", name="out_ev_dx")
    g0 = {"ev_w_out": _mm(o_ab, dh1, dims="tn", name="out_ev_dw")}
    dqa, dkn, dva, dkr = _mla_bwd(qa, kv, kr, o_a, lse, do_ab, 0)
    dqb, dkb, dvb = _sb_bwd(proj, COL_SB, do_ab, MLA_HEADS // 2)
    dqa_raw = _rope(dqa, cos_q, -sin_q, 0, dqa.shape[1] // LANES, F32, name="rope_q_bwd")
    g0["w_uq"] = _mm(nq, dqa_raw, dims="tn", name="proj_uq_dw")
    dnq = _mm(dqa_raw, W["w_uq"], dims="nt", name="proj_uq_dx")
    dc_q, G["g_cq"] = _rms_bwd(c_q, W["g_cq"], dnq, name="rms_cq_bwd")
    dkv = jnp.concatenate([dkn, dva], axis=1)
    g0["w_ukv"] = _mm(nkv, dkv, dims="tn", name="proj_ukv_dw")
    dnkv = _mm(dkv, W["w_ukv"], dims="nt", name="proj_ukv_dx")
    dc_kv, G["g_ckv"] = _rms_bwd(c_kv, W["g_ckv"], dnkv, name="rms_ckv_bwd")
    dkr_raw = _rope(dkr, cos_k, -sin_k, 0, 1, F32, name="rope_k_bwd")
    dproj = jnp.concatenate([dc_q, dc_kv, dkr_raw, dqb, dkb, dvb], axis=1)
    du0 = _mm(dproj, W["w_in"], dims="nt", name="proj_in_dx")
    g0["w_in"] = _mm(u0, dproj, dims="tn", name="proj_in_dw")
    tok = put_grads("mix0", g0)
    dx, g_gmix0 = _rms_bwd(x, W["g_mix"][0:1] + tok[:1, :1], du0, dres=dh1, name="rms_mix_bwd0")
    G["g_mix"] = jnp.concatenate([g_gmix0, g_gmix1], axis=0)
    return loss[0, 0], dx, G


_BIG = ["ev_w_in", "ev_w_uq", "ev_w_ukv", "ev_w_out", "od_w_qkv", "od_w_out", "w_gate", "w_up", "w_down"]
_COL_SHARDED = {"ev_w_in", "ev_w_uq", "ev_w_ukv", "od_w_qkv", "w_gate", "w_up"}
_SMALL = ["ev_g_cq", "ev_g_ckv", "od_rel_bias", "g_mix", "g_ffn", "g_final"]
_GROUPS = {
    "mix0": ["ev_w_in", "ev_w_uq", "ev_w_ukv", "ev_w_out"],
    "ffn0": ["w_gate0", "w_up0", "w_down0"],
    "mix1": ["od_w_qkv", "od_w_out"],
    "ffn1": ["w_gate1", "w_up1", "w_down1"],
}
_GROUP_SRC = {n + str(l): (n, l) for n in ("w_gate", "w_up", "w_down") for l in (0, 1)}
_SMALL_ROWS = 8
_SMALL_COLS = 1792


def _full_from_gathered(name, g):
    if name in _COL_SHARDED:
        L, K, n = g.shape[1:]
        return jnp.transpose(g, (1, 2, 0, 3)).reshape(L, K, N_DEV * n)
    L, r, N = g.shape[1:]
    return jnp.transpose(g, (1, 0, 2, 3)).reshape(L, N_DEV * r, N)


def _shards_from_full(name, g):
    L, K, N = g.shape
    if name in _COL_SHARDED:
        return jnp.transpose(g.reshape(L, K, N_DEV, N // N_DEV), (2, 0, 1, 3))
    return jnp.transpose(g.reshape(L, N_DEV, K // N_DEV, N), (1, 0, 2, 3))


def _pack_small(vals):
    flat = jnp.concatenate([v.reshape(-1).astype(F32) for v in vals])
    flat = jnp.pad(flat, (0, _SMALL_ROWS * _SMALL_COLS - flat.shape[0]))
    return flat.reshape(_SMALL_ROWS, _SMALL_COLS)


def _unpack_small(packed, like):
    flat = packed.reshape(-1)
    out, off = [], 0
    for v in like:
        out.append(flat[off:off + v.size].reshape(v.shape))
        off += v.size
    return out


def kernel(x, ev_w_in, ev_g_cq, ev_w_uq, ev_g_ckv, ev_w_ukv, ev_w_out, od_w_qkv, od_rel_bias, od_w_out, g_mix, g_ffn, w_gate, w_up, w_down, g_final, loss_target, m_ev_w_in, m_ev_g_cq, m_ev_w_uq, m_ev_g_ckv, m_ev_w_ukv, m_ev_w_out, m_od_w_qkv, m_od_rel_bias, m_od_w_out, m_g_mix, m_g_ffn, m_w_gate, m_w_up, m_w_down, m_g_final, v_ev_w_in, v_ev_g_cq, v_ev_w_uq, v_ev_g_ckv, v_ev_w_ukv, v_ev_w_out, v_od_w_qkv, v_od_rel_bias, v_od_w_out, v_g_mix, v_g_ffn, v_w_gate, v_w_up, v_w_down, v_g_final):
    args = dict(locals())
    w = {n: args[n] for n in _BIG + _SMALL}
    mom = {n: args["m_" + n] for n in _BIG + _SMALL}
    var = {n: args["v_" + n] for n in _BIG + _SMALL}

    own = {}
    for grp, names in _GROUPS.items():
        for n in names:
            base, layer = _GROUP_SRC.get(n, (n, 0))
            own[n] = w[base][layer:layer + 1].astype(BF16)
    gather, token = {}, x[0, :8, :LANES]
    for grp, names in _GROUPS.items():
        gather[grp], token = _exchange_start([own[n] for n in names], [False] * len(names), token,
                                             name="gather_start_" + grp)

    def get_weights(grp, after):
        names = _GROUPS[grp]
        lands = _exchange_wait(gather[grp], token if after is None else after, name="gather_wait_" + grp)
        full = {n: _full_from_gathered(_GROUP_SRC.get(n, (n, 0))[0], _with_own_row(l, own[n]))[0]
                for n, l in zip(names, lands)}
        if grp == "mix0":
            return {"w_in": _w_in_local(full["ev_w_in"]), "w_uq": _w_uq_local(full["ev_w_uq"]),
                    "w_ukv": _w_ukv_local(full["ev_w_ukv"]), "ev_w_out": full["ev_w_out"]}
        if grp == "mix1":
            return full
        layer = grp[-1]
        return {"w_gu" + layer: jnp.concatenate([full["w_gate" + layer], full["w_up" + layer]], axis=1),
                "w_down" + layer: full["w_down" + layer]}

    scatter = {}

    def put_grads(grp, g):
        if grp == "mix0":
            g = {"ev_w_in": _w_in_grad(g["w_in"]), "ev_w_uq": _w_uq_grad(g["w_uq"]),
                 "ev_w_ukv": _w_ukv_grad(g["w_ukv"]), "ev_w_out": g["ev_w_out"]}
        elif grp != "mix1":
            layer = grp[-1]
            g = {"w_gate" + layer: g["w_gu"][:, :D_FF], "w_up" + layer: g["w_gu"][:, D_FF:],
                 "w_down" + layer: g["w_down"]}
        names = _GROUPS[grp]
        send = [_shards_from_full(_GROUP_SRC.get(n, (n, 0))[0], g[n][None]).astype(BF16) for n in names]
        handle, tok = _exchange_start(send, [True] * len(names), send[0], name="scatter_start_" + grp)
        scatter[grp] = (handle, send)
        return tok

    small = {"g_cq": ev_g_cq, "g_ckv": ev_g_ckv, "od_rel_bias": od_rel_bias[0],
             "g_mix": g_mix + token[0, 0], "g_ffn": g_ffn, "g_final": g_final.reshape(1, -1)}
    loss_part, dx, G = _local_step(x[0], loss_target[0], small, get_weights, put_grads)
    loss = lax.psum(loss_part, ("x", "y", "c"))
    g_small = _pack_small([G["g_cq"], G["g_ckv"], G["od_rel_bias"], G["g_mix"], G["g_ffn"], G["g_final"]])
    small_handle, _ = _exchange_start([g_small], [False], dx, name="gather_start_small")

    grads, deltas, new_m, new_v = {}, {}, {}, {}
    parts, after = {}, dx
    me = 4 * lax.axis_index("x") + 2 * lax.axis_index("y") + lax.axis_index("c")
    for grp in ("ffn1", "mix1", "ffn0", "mix0"):
        handle, send = scatter[grp]
        lands = _exchange_wait(handle, after, name="scatter_wait_" + grp)
        for n, l, s in zip(_GROUPS[grp], lands, send):
            parts[n] = _with_own_row(l, lax.dynamic_index_in_dim(s, me, 0, keepdims=False))
        after = lands[0]
    for n in ("w_gate", "w_up", "w_down"):
        parts[n] = jnp.concatenate([parts[n + "0"], parts[n + "1"]], axis=1)
    for n in _BIG:
        shp = w[n].shape
        r2 = (-1, shp[-1])
        res = _adamw(w[n].reshape(r2), parts[n].reshape((N_DEV,) + w[n].reshape(r2).shape),
                     mom[n].reshape(r2), var[n].reshape(r2), name="adamw_" + n)
        grads[n], deltas[n], new_m[n], new_v[n] = [r.reshape(shp) for r in res]
    small_w = [w[n] for n in _SMALL]
    small_parts = _with_own_row(_exchange_wait(small_handle, after, name="gather_wait_small")[0], g_small)
    res = _adamw(_pack_small(small_w), small_parts, _pack_small([mom[n] for n in _SMALL]),
                 _pack_small([var[n] for n in _SMALL]), name="adamw_small")
    for d, packed in zip((grads, deltas, new_m, new_v), res):
        for n, val in zip(_SMALL, _unpack_small(packed, small_w)):
            d[n] = val

    order = ["ev_w_in", "ev_g_cq", "ev_w_uq", "ev_g_ckv", "ev_w_ukv", "ev_w_out", "od_w_qkv", "od_rel_bias",
             "od_w_out", "g_mix", "g_ffn", "w_gate", "w_up", "w_down", "g_final"]
    out = [loss, dx[None]]
    for d in (grads, deltas, new_m, new_v):
        out += [d[n] for n in order]
    return tuple(out)
```

```python
import functools

import numpy as np
import jax
import jax.numpy as jnp
from jax import lax
from jax.experimental import pallas as pl
from jax.experimental.pallas import tpu as pltpu

F32 = jnp.float32
BF16 = jnp.bfloat16

D_MODEL = 1024
CHUNK = 64
MLA_HEADS = 8
MLA_NOPE = 64
MLA_ROPE = 32
MLA_V = 64
Q_LORA = 384
KV_LORA = 256
ROPE_THETA = 10000.0
SB_HEADS = 8
SB_DIM = 64
C_HEADS = 16
C_DIM = 64
LEFT_CHUNKS = 8
REL_CLIP = 256
D_FF = 2816
RMS_EPS = 1e-6
ADAM_LR = 0.001
ADAM_B1 = 0.9
ADAM_B2 = 0.999
ADAM_EPS = 1e-08
ADAM_WD = 0.01
ADAM_STEP = 10

N_DEV = 8
LANES = 128
VMEM_LIMIT = 56 * 1024 * 1024
NEG = -1e30
PAD_KEYS = LEFT_CHUNKS * CHUNK
BAND_TQ = 128
BAND_W = BAND_TQ + PAD_KEYS
TOEP_W = BAND_W + BAND_TQ

NN = (((1,), (0,)), ((), ()))
NT = (((1,), (1,)), ((), ()))
TN = (((0,), (0,)), ((), ()))


def _dot(a, b, dn):
    return lax.dot_general(a, b, dn, preferred_element_type=F32)


def _pick(dim, pref):
    if dim <= pref:
        return dim
    best = None
    for t in range(LANES, pref + 1, LANES):
        if dim % t == 0:
            best = t
    assert best is not None, (dim, pref)
    return best


def _params(sem):
    return pltpu.CompilerParams(dimension_semantics=sem, vmem_limit_bytes=VMEM_LIMIT)


def _mm(a, b, dims="nn", res=None, out_dtype=F32, name="mm"):
    if dims == "nn":
        (M, K), (K2, N) = a.shape, b.shape
    elif dims == "nt":
        (M, K), (N, K2) = a.shape, b.shape
    else:
        (K, M), (K2, N) = a.shape, b.shape
    assert K == K2, (a.shape, b.shape, dims)
    tm, tn, tk = _pick(M, 512), _pick(N, 1408), _pick(K, 1408)
    nk = K // tk
    dn = {"nn": NN, "nt": NT, "tn": TN}[dims]
    has_res = res is not None

    def body(*refs):
        if has_res:
            a_ref, b_ref, r_ref, o_ref, acc = refs
        else:
            a_ref, b_ref, o_ref, acc = refs
        k = pl.program_id(2)

        @pl.when(k == 0)
        def _():
            acc[...] = jnp.zeros_like(acc)

        acc[...] += _dot(a_ref[...].astype(BF16), b_ref[...].astype(BF16), dn)

        @pl.when(k == nk - 1)
        def _():
            r = acc[...]
            if has_res:
                r = r + r_ref[...]
            o_ref[...] = r.astype(out_dtype)

    a_spec = (pl.BlockSpec((tk, tm), lambda i, j, k: (k, i)) if dims == "tn"
              else pl.BlockSpec((tm, tk), lambda i, j, k: (i, k)))
    b_spec = (pl.BlockSpec((tn, tk), lambda i, j, k: (j, k)) if dims == "nt"
              else pl.BlockSpec((tk, tn), lambda i, j, k: (k, j)))
    o_spec = pl.BlockSpec((tm, tn), lambda i, j, k: (i, j))
    in_specs = [a_spec, b_spec] + ([o_spec] if has_res else [])
    args = (a, b) + ((res,) if has_res else ())
    return pl.pallas_call(
        body, name=name, grid=(M // tm, N // tn, nk),
        in_specs=in_specs, out_specs=o_spec,
        out_shape=jax.ShapeDtypeStruct((M, N), out_dtype),
        scratch_shapes=[pltpu.VMEM((tm, tn), F32)],
        compiler_params=_params(("parallel", "parallel", "arbitrary")),
    )(*args)


def _rms_fwd(x, g, out_dtype=BF16, name="rms_fwd"):
    T, Fd = x.shape
    tm = _pick(T, 256)

    def body(x_ref, g_ref, o_ref):
        xv = x_ref[...]
        r = lax.rsqrt(jnp.mean(xv * xv, axis=-1, keepdims=True) + RMS_EPS)
        o_ref[...] = (xv * r * g_ref[...]).astype(out_dtype)

    return pl.pallas_call(
        body, name=name, grid=(T // tm,),
        in_specs=[pl.BlockSpec((tm, Fd), lambda i: (i, 0)), pl.BlockSpec((1, Fd), lambda i: (0, 0))],
        out_specs=pl.BlockSpec((tm, Fd), lambda i: (i, 0)),
        out_shape=jax.ShapeDtypeStruct((T, Fd), out_dtype),
        compiler_params=_params(("parallel",)),
    )(x, g)


def _rms_bwd(x, g, dy, dres=None, name="rms_bwd"):
    T, Fd = x.shape
    tm = _pick(T, 256)
    has_res = dres is not None

    def body(*refs):
        if has_res:
            x_ref, g_ref, dy_ref, r_ref, dx_ref, dxb_ref, dg_ref = refs
        else:
            x_ref, g_ref, dy_ref, dx_ref, dxb_ref, dg_ref = refs
        xv, dyv = x_ref[...], dy_ref[...]
        r = lax.rsqrt(jnp.mean(xv * xv, axis=-1, keepdims=True) + RMS_EPS)
        gdy = dyv * g_ref[...]
        dot = jnp.mean(xv * gdy, axis=-1, keepdims=True)
        dx = r * gdy - xv * (r * r * r * dot)
        if has_res:
            dx = dx + r_ref[...]
        dx_ref[...] = dx
        dxb_ref[...] = dx.astype(BF16)

        @pl.when(pl.program_id(0) == 0)
        def _():
            dg_ref[...] = jnp.zeros_like(dg_ref)

        dg_ref[...] += jnp.sum(dyv * xv * r, axis=0, keepdims=True)

    row = pl.BlockSpec((tm, Fd), lambda i: (i, 0))
    vec = pl.BlockSpec((1, Fd), lambda i: (0, 0))
    in_specs = [row, vec, row] + ([row] if has_res else [])
    args = (x, g, dy) + ((dres,) if has_res else ())
    return pl.pallas_call(
        body, name=name, grid=(T // tm,),
        in_specs=in_specs, out_specs=[row, row, vec],
        out_shape=[jax.ShapeDtypeStruct((T, Fd), F32), jax.ShapeDtypeStruct((T, Fd), BF16),
                   jax.ShapeDtypeStruct((1, Fd), F32)],
        compiler_params=_params(("arbitrary",)),
    )(*args)


def _loss_head(h, g, target, name="loss_head"):
    T, Fd = h.shape
    tm = _pick(T, 256)

    def body(h_ref, g_ref, t_ref, loss_ref, dh_ref, dhb_ref, dg_ref):
        xv = h_ref[...]
        r = lax.rsqrt(jnp.mean(xv * xv, axis=-1, keepdims=True) + RMS_EPS)
        diff = xv * r * g_ref[...] - t_ref[...]
        part = 0.5 * jnp.sum(jnp.mean(diff * diff, axis=-1, keepdims=True), axis=0, keepdims=True)
        dyv = diff * (1.0 / Fd)
        gdy = dyv * g_ref[...]
        dot = jnp.mean(xv * gdy, axis=-1, keepdims=True)
        dh = r * gdy - xv * (r * r * r * dot)
        dh_ref[...] = dh
        dhb_ref[...] = dh.astype(BF16)

        @pl.when(pl.program_id(0) == 0)
        def _():
            dg_ref[...] = jnp.zeros_like(dg_ref)
            loss_ref[...] = jnp.zeros_like(loss_ref)

        dg_ref[...] += jnp.sum(dyv * xv * r, axis=0, keepdims=True)
        loss_ref[...] += jnp.broadcast_to(part, loss_ref.shape)

    row = pl.BlockSpec((tm, Fd), lambda i: (i, 0))
    vec = pl.BlockSpec((1, Fd), lambda i: (0, 0))
    return pl.pallas_call(
        body, name=name, grid=(T // tm,),
        in_specs=[row, vec, row],
        out_specs=[pl.BlockSpec((1, LANES), lambda i: (0, 0)), row, row, vec],
        out_shape=[jax.ShapeDtypeStruct((1, LANES), F32), jax.ShapeDtypeStruct((T, Fd), F32),
                   jax.ShapeDtypeStruct((T, Fd), BF16), jax.ShapeDtypeStruct((1, Fd), F32)],
        compiler_params=_params(("arbitrary",)),
    )(h, g, target)


FFN_TF = 256


def _ffn_fwd(h, g, wg_t, wu_t, wd, name="ffn_fwd"):
    T, Dm = h.shape
    Fh = wd.shape[0]
    tm = _pick(T, 1024)
    nf = Fh // FFN_TF

    def body(h_ref, g_ref, wg_ref, wu_ref, wd_ref, o_ref, u_ref, a_ref, b_ref):
        j = pl.program_id(1)

        @pl.when(j == 0)
        def _():
            xv = h_ref[...]
            r = lax.rsqrt(jnp.mean(xv * xv, axis=-1, keepdims=True) + RMS_EPS)
            u_ref[...] = (xv * r * g_ref[...]).astype(BF16)
            o_ref[...] = xv

        u = u_ref[...]
        a = _dot(u, wg_ref[...], NT).astype(BF16)
        b = _dot(u, wu_ref[...], NT).astype(BF16)
        a_ref[...] = a
        b_ref[...] = b
        af = a.astype(F32)
        s = (af * jax.nn.sigmoid(af) * b.astype(F32)).astype(BF16)
        o_ref[...] += _dot(s, wd_ref[...], NN)

    row = pl.BlockSpec((tm, Dm), lambda i, j: (i, 0))
    wblk = pl.BlockSpec((FFN_TF, Dm), lambda i, j: (j, 0))
    ablk = pl.BlockSpec((tm, FFN_TF), lambda i, j: (i, j))
    return pl.pallas_call(
        body, name=name, grid=(T // tm, nf),
        in_specs=[row, pl.BlockSpec((1, Dm), lambda i, j: (0, 0)), wblk, wblk, wblk],
        out_specs=[row, row, ablk, ablk],
        out_shape=[jax.ShapeDtypeStruct((T, Dm), F32), jax.ShapeDtypeStruct((T, Dm), BF16),
                   jax.ShapeDtypeStruct((T, Fh), BF16), jax.ShapeDtypeStruct((T, Fh), BF16)],
        compiler_params=_params(("parallel", "arbitrary")),
    )(h, g, wg_t, wu_t, wd)


def _ffn_bwd(dh, u, a, b, wg_t, wu_t, wd, name="ffn_bwd"):
    T, Dm = dh.shape
    Fh = wd.shape[0]
    nf = Fh // FFN_TF
    once = pl.Buffered(1)

    def body(dh_ref, u_ref, a_ref, b_ref, wg_ref, wu_ref, wd_ref, du_ref, dwg_ref, dwu_ref, dwd_ref):
        j = pl.program_id(0)

        @pl.when(j == 0)
        def _():
            du_ref[...] = jnp.zeros_like(du_ref)

        ds = _dot(dh_ref[...], wd_ref[...], NT)
        af, bf = a_ref[...].astype(F32), b_ref[...].astype(F32)
        sig = jax.nn.sigmoid(af)
        sa = af * sig
        dwd_ref[...] = _dot((sa * bf).astype(BF16), dh_ref[...], TN).astype(BF16)
        dab = jnp.concatenate([(ds * bf * (sig * (1.0 + af * (1.0 - sig)))).astype(BF16),
                               (ds * sa).astype(BF16)], axis=1)
        dw = _dot(dab, u_ref[...], TN)
        dwg_ref[...] = dw[:FFN_TF].astype(BF16)
        dwu_ref[...] = dw[FFN_TF:].astype(BF16)
        du_ref[...] += _dot(dab, jnp.concatenate([wg_ref[...], wu_ref[...]], axis=0), NN)

    full = lambda: pl.BlockSpec((T, Dm), lambda j: (0, 0), pipeline_mode=once)
    wblk = pl.BlockSpec((FFN_TF, Dm), lambda j: (j, 0))
    ablk = pl.BlockSpec((T, FFN_TF), lambda j: (0, j))
    return pl.pallas_call(
        body, name=name, grid=(nf,),
        in_specs=[full(), full(), ablk, ablk, wblk, wblk, wblk],
        out_specs=[pl.BlockSpec((T, Dm), lambda j: (0, 0)), wblk, wblk, wblk],
        out_shape=[jax.ShapeDtypeStruct((T, Dm), F32)] + [jax.ShapeDtypeStruct((Fh, Dm), BF16)] * 3,
        compiler_params=_params(("arbitrary",)),
    )(dh, u, a, b, wg_t, wu_t, wd)


def _rope(x, cos_t, sin_t, col0, ncols, out_dtype, name="rope"):
    T = x.shape[0]
    wt = cos_t.shape[1]
    tm = _pick(T, 256)
    nb = ncols * LANES // wt
    half = MLA_ROPE // 2

    def body(x_ref, c_ref, s_ref, o_ref):
        xv = x_ref[...].astype(F32)
        lane = lax.broadcasted_iota(jnp.int32, xv.shape, 1)
        first = (lane & (MLA_ROPE - 1)) < half
        swapped = jnp.where(first, pltpu.roll(xv, wt - half, 1), pltpu.roll(xv, half, 1))
        o_ref[...] = (xv * c_ref[...] + swapped * s_ref[...]).astype(out_dtype)

    off = col0 * LANES // wt
    return pl.pallas_call(
        body, name=name, grid=(T // tm, nb),
        in_specs=[pl.BlockSpec((tm, wt), lambda i, j: (i, j + off)),
                  pl.BlockSpec((tm, wt), lambda i, j: (i, 0)),
                  pl.BlockSpec((tm, wt), lambda i, j: (i, 0))],
        out_specs=pl.BlockSpec((tm, wt), lambda i, j: (i, j)),
        out_shape=jax.ShapeDtypeStruct((T, ncols * LANES), out_dtype),
        compiler_params=_params(("parallel", "parallel")),
    )(x, cos_t, sin_t)


ATT_T = 256


def _mla_masks(shape):
    lane = lax.broadcasted_iota(jnp.int32, shape, 1)
    m0 = (lane < 64) | ((lane >= 128) & (lane < 160))
    m1 = ((lane >= 64) & (lane < 128)) | ((lane >= 160) & (lane < 192))
    return m0, m1


def _chunk_ok(m_idx, kb, tq, tk):
    tpos = m_idx * tq + lax.broadcasted_iota(jnp.int32, (tq, tk), 0)
    spos = kb * tk + lax.broadcasted_iota(jnp.int32, (tq, tk), 1)
    return (spos >> 6) <= (tpos >> 6)


def _mla_fwd(q, kv, kr, name="mla_fwd"):
    T = q.shape[0]
    tq = tk = _pick(T, ATT_T)
    npair = MLA_HEADS // 2
    scale = (MLA_NOPE + MLA_ROPE) ** -0.5

    def body(q_ref, kn_ref, v_ref, kr_ref, o_ref, lse_ref):
        m_idx = pl.program_id(1)
        qv = q_ref[...]
        m0, m1 = _mla_masks(qv.shape)
        qh = (jnp.where(m0, qv, 0).astype(BF16), jnp.where(m1, qv, 0).astype(BF16))

        def step(kb, carry):
            ks = pl.ds(pl.multiple_of(kb * tk, tk), tk)
            kcat = jnp.concatenate([kn_ref[ks, :], kr_ref[ks, :]], axis=1)
            vv = v_ref[ks, :]
            ok = _chunk_ok(m_idx, kb, tq, tk)
            out = []
            for h in range(2):
                mx, l, acc = carry[3 * h:3 * h + 3]
                s = jnp.where(ok, _dot(qh[h], kcat, NT) * scale, NEG)
                mn = jnp.maximum(mx, jnp.max(s, axis=-1, keepdims=True))
                alpha = jnp.exp(mx - mn)
                p = jnp.exp(s - mn)
                l = alpha * l + jnp.sum(p, axis=-1, keepdims=True)
                acc = alpha * acc + _dot(p.astype(BF16), vv, NN)
                out += [mn, l, acc]
            return tuple(out)

        init = (jnp.full((tq, 1), NEG, F32), jnp.zeros((tq, 1), F32), jnp.zeros((tq, LANES), F32)) * 2
        res = lax.fori_loop(0, m_idx + 1, step, init)
        lane = lax.broadcasted_iota(jnp.int32, (tq, LANES), 1)
        o0 = res[2] / res[1]
        o1 = res[5] / res[4]
        o_ref[...] = jnp.where(lane < 64, o0, o1).astype(o_ref.dtype)
        lse_ref[...] = jnp.where(lane < 64, res[0] + jnp.log(res[1]), res[3] + jnp.log(res[4]))

    full = lambda col: pl.BlockSpec((T, LANES), col)
    return pl.pallas_call(
        body, name=name, grid=(npair, T // tq),
        in_specs=[pl.BlockSpec((tq, 2 * LANES), lambda p, m: (m, p)),
                  full(lambda p, m: (0, p)), full(lambda p, m: (0, npair + p)), full(lambda p, m: (0, 0))],
        out_specs=[pl.BlockSpec((tq, LANES), lambda p, m: (m, p)),
                   pl.BlockSpec((tq, LANES), lambda p, m: (m, p))],
        out_shape=[jax.ShapeDtypeStruct((T, npair * LANES), BF16),
                   jax.ShapeDtypeStruct((T, npair * LANES), F32)],
        compiler_params=_params(("parallel", "arbitrary")),
    )(q, kv, kv, kr)


def _mla_bwd(q, kv, kr, o, lse, do, do_col0, name="mla_bwd"):
    T = q.shape[0]
    tq = tk = _pick(T, ATT_T)
    npair = MLA_HEADS // 2
    scale = (MLA_NOPE + MLA_ROPE) ** -0.5

    def body(q_ref, kn_ref, v_ref, kr_ref, o_ref, lse_ref, do_ref, dq_ref, dkn_ref, dv_ref, dkr_ref):
        p_idx, m_idx = pl.program_id(0), pl.program_id(1)

        @pl.when(m_idx == 0)
        def _():
            dkn_ref[...] = jnp.zeros_like(dkn_ref)
            dv_ref[...] = jnp.zeros_like(dv_ref)

        @pl.when((m_idx == 0) & (p_idx == 0))
        def _():
            dkr_ref[...] = jnp.zeros_like(dkr_ref)

        qv = q_ref[...]
        m0, m1 = _mla_masks(qv.shape)
        qh = (jnp.where(m0, qv, 0).astype(BF16), jnp.where(m1, qv, 0).astype(BF16))
        dov = do_ref[...].astype(F32)
        lane = lax.broadcasted_iota(jnp.int32, (tq, LANES), 1)
        h0 = lane < 64
        prod = dov * o_ref[...].astype(F32)
        delta = (jnp.sum(jnp.where(h0, prod, 0.0), axis=-1, keepdims=True),
                 jnp.sum(jnp.where(h0, 0.0, prod), axis=-1, keepdims=True))
        doh = (jnp.where(h0, dov, 0.0).astype(BF16), jnp.where(h0, 0.0, dov).astype(BF16))
        lsev = lse_ref[...]
        lse_h = (lsev[:, 0:1], lsev[:, 64:65])

        def step(kb, carry):
            ks = pl.ds(pl.multiple_of(kb * tk, tk), tk)
            kcat = jnp.concatenate([kn_ref[ks, :], kr_ref[ks, :]], axis=1)
            vv = v_ref[ks, :]
            ok = _chunk_ok(m_idx, kb, tq, tk)
            dkc = jnp.zeros((tk, 2 * LANES), F32)
            dvv = jnp.zeros((tk, LANES), F32)
            out = []
            for h in range(2):
                s = _dot(qh[h], kcat, NT) * scale
                p = jnp.where(ok, jnp.exp(s - lse_h[h]), 0.0)
                dp = _dot(doh[h], vv, NT)
                ds = (p * (dp - delta[h]) * scale).astype(BF16)
                out.append(carry[h] + _dot(ds, kcat, NN))
                dkc = dkc + _dot(ds, qh[h], TN)
                dvv = dvv + _dot(p.astype(BF16), doh[h], TN)
            dkn_ref[ks, :] += dkc[:, :LANES]
            dkr_ref[ks, :] += dkc[:, LANES:]
            dv_ref[ks, :] += dvv
            return tuple(out)

        init = (jnp.zeros((tq, 2 * LANES), F32),) * 2
        dq0, dq1 = lax.fori_loop(0, m_idx + 1, step, init)
        dq_ref[...] = jnp.where(m0, dq0, jnp.where(m1, dq1, 0.0))

    full = lambda col: pl.BlockSpec((T, LANES), col)
    blk = lambda col: pl.BlockSpec((tq, LANES), col)
    return pl.pallas_call(
        body, name=name, grid=(npair, T // tq),
        in_specs=[pl.BlockSpec((tq, 2 * LANES), lambda p, m: (m, p)),
                  full(lambda p, m: (0, p)), full(lambda p, m: (0, npair + p)), full(lambda p, m: (0, 0)),
                  blk(lambda p, m: (m, p)), blk(lambda p, m: (m, p)),
                  blk(lambda p, m: (m, do_col0 + p))],
        out_specs=[pl.BlockSpec((tq, 2 * LANES), lambda p, m: (m, p)),
                   full(lambda p, m: (0, p)), full(lambda p, m: (0, p)), full(lambda p, m: (0, 0))],
        out_shape=[jax.ShapeDtypeStruct((T, npair * 2 * LANES), F32),
                   jax.ShapeDtypeStruct((T, npair * LANES), F32),
                   jax.ShapeDtypeStruct((T, npair * LANES), F32),
                   jax.ShapeDtypeStruct((T, LANES), F32)],
        compiler_params=_params(("arbitrary", "arbitrary")),
    )(q, kv, kv, kr, o, lse, do)


def _split_dot(x, tri):
    hi = x.astype(BF16)
    lo = (x - hi.astype(F32)).astype(BF16)
    return _dot(hi, tri, NN) + _dot(lo, tri, NN)


def _sb_terms(qh, kk, m_idx, kb, tq, tk, scale):
    z = _dot(qh, kk, NT) * scale
    tpos = m_idx * tq + lax.broadcasted_iota(jnp.int32, (tq, tk), 0)
    spos = kb * tk + lax.broadcasted_iota(jnp.int32, (tq, tk), 1)
    before = spos < tpos
    sp = jnp.maximum(z, 0.0) + jnp.log(1.0 + jnp.exp(-jnp.abs(z)))
    lk = jnp.where(before, -sp, 0.0)
    return z, sp, lk, before


def _sb_fwd(qkv, col0, name="sb_fwd"):
    T = qkv.shape[0]
    tq = tk = _pick(T, ATT_T)
    npair = SB_HEADS // 2
    scale = SB_DIM ** -0.5

    def body(q_ref, k_ref, v_ref, o_ref):
        m_idx = pl.program_id(1)
        qv = q_ref[...].astype(BF16)
        lane = lax.broadcasted_iota(jnp.int32, (tq, LANES), 1)
        h0 = lane < 64
        qh = (jnp.where(h0, qv, 0).astype(BF16), jnp.where(h0, 0, qv).astype(BF16))
        row = lax.broadcasted_iota(jnp.int32, (tk, tk), 0)
        col = lax.broadcasted_iota(jnp.int32, (tk, tk), 1)
        later = (row > col).astype(BF16)

        def step(i, carry):
            kb = m_idx - i
            ks = pl.ds(pl.multiple_of(kb * tk, tk), tk)
            kk = k_ref[ks, :].astype(BF16)
            vv = v_ref[ks, :].astype(BF16)
            out = []
            for h in range(2):
                c, acc = carry[2 * h:2 * h + 2]
                z, sp, lk, before = _sb_terms(qh[h], kk, m_idx, kb, tq, tk, scale)
                a = (z - sp) + _split_dot(lk, later) + c
                w = jnp.where(before, jnp.exp(a), 0.0)
                out += [c + jnp.sum(lk, axis=-1, keepdims=True), acc + _dot(w.astype(BF16), vv, NN)]
            return tuple(out)

        init = (jnp.zeros((tq, 1), F32), jnp.zeros((tq, LANES), F32)) * 2
        res = lax.fori_loop(0, m_idx + 1, step, init)
        o_ref[...] = jnp.where(h0, res[1], res[3]).astype(o_ref.dtype)

    full = lambda col: pl.BlockSpec((T, LANES), col)
    return pl.pallas_call(
        body, name=name, grid=(npair, T // tq),
        in_specs=[pl.BlockSpec((tq, LANES), lambda p, m: (m, col0 + p)),
                  full(lambda p, m: (0, col0 + npair + p)), full(lambda p, m: (0, col0 + 2 * npair + p))],
        out_specs=pl.BlockSpec((tq, LANES), lambda p, m: (m, p)),
        out_shape=jax.ShapeDtypeStruct((T, npair * LANES), BF16),
        compiler_params=_params(("parallel", "arbitrary")),
    )(qkv, qkv, qkv)


def _sb_bwd(qkv, col0, do, do_col0, name="sb_bwd"):
    T = qkv.shape[0]
    tq = tk = _pick(T, ATT_T)
    npair = SB_HEADS // 2
    scale = SB_DIM ** -0.5

    def body(q_ref, k_ref, v_ref, do_ref, dq_ref, dk_ref, dv_ref, e_scr, z_scr, sp_scr):
        m_idx = pl.program_id(1)

        @pl.when(m_idx == 0)
        def _():
            dk_ref[...] = jnp.zeros_like(dk_ref)
            dv_ref[...] = jnp.zeros_like(dv_ref)

        qv = q_ref[...].astype(BF16)
        lane = lax.broadcasted_iota(jnp.int32, (tq, LANES), 1)
        h0 = lane < 64
        qh = (jnp.where(h0, qv, 0).astype(BF16), jnp.where(h0, 0, qv).astype(BF16))
        dov = do_ref[...].astype(F32)
        doh = (jnp.where(h0, dov, 0.0).astype(BF16), jnp.where(h0, 0.0, dov).astype(BF16))
        row = lax.broadcasted_iota(jnp.int32, (tk, tk), 0)
        col = lax.broadcasted_iota(jnp.int32, (tk, tk), 1)
        later = (row > col).astype(BF16)
        earlier = (row < col).astype(BF16)
        dq = []
        for h in range(2):
            def rl(i, c, h=h):
                kb = m_idx - i
                ks = pl.ds(pl.multiple_of(kb * tk, tk), tk)
                kk = k_ref[ks, :].astype(BF16)
                vv = v_ref[ks, :].astype(BF16)
                z, sp, lk, before = _sb_terms(qh[h], kk, m_idx, kb, tq, tk, scale)
                a = (z - sp) + _split_dot(lk, later) + c
                w = jnp.where(before, jnp.exp(a), 0.0)
                dw = _dot(doh[h], vv, NT)
                e_scr[kb] = w * dw
                z_scr[kb] = z
                sp_scr[kb] = sp
                dv_ref[ks, :] += _dot(w.astype(BF16), doh[h], TN)
                return c + jnp.sum(lk, axis=-1, keepdims=True)

            lax.fori_loop(0, m_idx + 1, rl, jnp.zeros((tq, 1), F32))

            def lr(kb, carry, h=h):
                esum, dqa = carry
                ks = pl.ds(pl.multiple_of(kb * tk, tk), tk)
                kk = k_ref[ks, :].astype(BF16)
                e, z, sp = e_scr[kb], z_scr[kb], sp_scr[kb]
                tpos = m_idx * tq + lax.broadcasted_iota(jnp.int32, (tq, tk), 0)
                spos = kb * tk + lax.broadcasted_iota(jnp.int32, (tq, tk), 1)
                prev = _split_dot(e, earlier) + esum
                dz = jnp.where(spos < tpos, e * jnp.exp(-sp) - jnp.exp(z - sp) * prev, 0.0)
                dzb = (dz * scale).astype(BF16)
                dk_ref[ks, :] += _dot(dzb, qh[h], TN)
                return esum + jnp.sum(e, axis=-1, keepdims=True), dqa + _dot(dzb, kk, NN)

            _, dqh = lax.fori_loop(0, m_idx + 1, lr, (jnp.zeros((tq, 1), F32), jnp.zeros((tq, LANES), F32)))
            dq.append(dqh)
        dq_ref[...] = jnp.where(h0, dq[0], dq[1])

    full = lambda col: pl.BlockSpec((T, LANES), col)
    blk = lambda col: pl.BlockSpec((tq, LANES), col)
    return pl.pallas_call(
        body, name=name, grid=(npair, T // tq),
        in_specs=[blk(lambda p, m: (m, col0 + p)),
                  full(lambda p, m: (0, col0 + npair + p)), full(lambda p, m: (0, col0 + 2 * npair + p)),
                  blk(lambda p, m: (m, do_col0 + p))],
        out_specs=[blk(lambda p, m: (m, p)), full(lambda p, m: (0, p)), full(lambda p, m: (0, p))],
        out_shape=[jax.ShapeDtypeStruct((T, npair * LANES), F32)] * 3,
        scratch_shapes=[pltpu.VMEM((T // tk, tq, tk), F32)] * 3,
        compiler_params=_params(("arbitrary", "arbitrary")),
    )(qkv, qkv, qkv, do)


def _band_valid(m_idx):
    qi = lax.broadcasted_iota(jnp.int32, (BAND_TQ, BAND_W), 0)
    j = lax.broadcasted_iota(jnp.int32, (BAND_TQ, BAND_W), 1)
    cq = (m_idx * BAND_TQ + qi) >> 6
    ckp = (m_idx * BAND_TQ + j) >> 6
    return (ckp >= LEFT_CHUNKS) & (ckp >= cq) & (ckp <= cq + LEFT_CHUNKS)


def _band_probs(qh, kw, bias, valid, scale):
    s = _dot(qh, kw, NT) * scale + bias
    s = jnp.where(valid, s, NEG)
    e = jnp.exp(s - jnp.max(s, axis=-1, keepdims=True))
    return e / jnp.sum(e, axis=-1, keepdims=True)


def _band_fwd(qkv, k_pad, v_pad, bias_w, name="band_fwd"):
    T = qkv.shape[0]
    npair = C_HEADS // 2
    scale = C_DIM ** -0.5

    def body(q_ref, k_ref, v_ref, b_ref, o_ref):
        m_idx = pl.program_id(1)
        win = pl.ds(pl.multiple_of(m_idx * BAND_TQ, BAND_TQ), BAND_W)
        kw, vw = k_ref[win, :], v_ref[win, :]
        qv = q_ref[...]
        lane = lax.broadcasted_iota(jnp.int32, (BAND_TQ, LANES), 1)
        h0 = lane < 64
        qh = (jnp.where(h0, qv, 0).astype(BF16), jnp.where(h0, 0, qv).astype(BF16))
        valid = _band_valid(m_idx)
        o = [_dot(_band_probs(qh[h], kw, b_ref[h], valid, scale).astype(BF16), vw, NN) for h in range(2)]
        o_ref[...] = jnp.where(h0, o[0], o[1]).astype(o_ref.dtype)

    Tp = T + PAD_KEYS
    return pl.pallas_call(
        body, name=name, grid=(npair, T // BAND_TQ),
        in_specs=[pl.BlockSpec((BAND_TQ, LANES), lambda p, m: (m, p)),
                  pl.BlockSpec((Tp, LANES), lambda p, m: (0, p)),
                  pl.BlockSpec((Tp, LANES), lambda p, m: (0, p)),
                  pl.BlockSpec((2, BAND_TQ, BAND_W), lambda p, m: (p, 0, 0))],
        out_specs=pl.BlockSpec((BAND_TQ, LANES), lambda p, m: (m, p)),
        out_shape=jax.ShapeDtypeStruct((T, npair * LANES), BF16),
        compiler_params=_params(("parallel", "arbitrary")),
    )(qkv, k_pad, v_pad, bias_w)


def _band_bwd(qkv, k_pad, v_pad, bias_w, do, name="band_bwd"):
    T = qkv.shape[0]
    npair = C_HEADS // 2
    scale = C_DIM ** -0.5

    def body(q_ref, k_ref, v_ref, b_ref, do_ref, dq_ref, dk_ref, dv_ref, db_ref):
        m_idx = pl.program_id(1)

        @pl.when(m_idx == 0)
        def _():
            dk_ref[...] = jnp.zeros_like(dk_ref)
            dv_ref[...] = jnp.zeros_like(dv_ref)
            db_ref[...] = jnp.zeros_like(db_ref)

        win = pl.ds(pl.multiple_of(m_idx * BAND_TQ, BAND_TQ), BAND_W)
        kw, vw = k_ref[win, :], v_ref[win, :]
        qv = q_ref[...]
        dov = do_ref[...].astype(F32)
        lane = lax.broadcasted_iota(jnp.int32, (BAND_TQ, LANES), 1)
        h0 = lane < 64
        qh = (jnp.where(h0, qv, 0).astype(BF16), jnp.where(h0, 0, qv).astype(BF16))
        doh = (jnp.where(h0, dov, 0.0).astype(BF16), jnp.where(h0, 0.0, dov).astype(BF16))
        valid = _band_valid(m_idx)
        dq = []
        dkw = jnp.zeros((BAND_W, LANES), F32)
        dvw = jnp.zeros((BAND_W, LANES), F32)
        for h in range(2):
            p = _band_probs(qh[h], kw, b_ref[h], valid, scale)
            dp = _dot(doh[h], vw, NT)
            dsb = p * (dp - jnp.sum(p * dp, axis=-1, keepdims=True))
            db_ref[h] += dsb
            dsq = (dsb * scale).astype(BF16)
            dq.append(_dot(dsq, kw, NN))
            dkw = dkw + _dot(dsq, qh[h], TN)
            dvw = dvw + _dot(p.astype(BF16), doh[h], TN)
        dq_ref[...] = jnp.where(h0, dq[0], dq[1])
        dk_ref[win, :] += dkw
        dv_ref[win, :] += dvw

    Tp = T + PAD_KEYS
    blk = lambda col: pl.BlockSpec((BAND_TQ, LANES), col)
    full = pl.BlockSpec((Tp, LANES), lambda p, m: (0, p))
    bias = pl.BlockSpec((2, BAND_TQ, BAND_W), lambda p, m: (p, 0, 0))
    return pl.pallas_call(
        body, name=name, grid=(npair, T // BAND_TQ),
        in_specs=[blk(lambda p, m: (m, p)), full, full, bias, blk(lambda p, m: (m, p))],
        out_specs=[blk(lambda p, m: (m, p)), full, full, bias],
        out_shape=[jax.ShapeDtypeStruct((T, npair * LANES), F32),
                   jax.ShapeDtypeStruct((Tp, npair * LANES), F32),
                   jax.ShapeDtypeStruct((Tp, npair * LANES), F32),
                   jax.ShapeDtypeStruct((C_HEADS, BAND_TQ, BAND_W), F32)],
        compiler_params=_params(("arbitrary", "arbitrary")),
    )(qkv, k_pad, v_pad, bias_w, do)


def _skew_bits(x, left):
    w = x.shape[1]
    row = lax.broadcasted_iota(jnp.int32, x.shape, 0)
    for b in range(BAND_TQ.bit_length() - 1):
        amt = (w - (1 << b)) if left else (1 << b)
        x = jnp.where((row >> b) & 1 == 1, pltpu.roll(x, amt, 1), x)
    return x


def _toeplitz(diag, name="toeplitz"):
    H = diag.shape[0]

    def body(d_ref, o_ref):
        x = jnp.broadcast_to(d_ref[0], (BAND_TQ, TOEP_W))
        o_ref[0] = _skew_bits(x, left=False)[:, BAND_TQ:]

    return pl.pallas_call(
        body, name=name, grid=(H,),
        in_specs=[pl.BlockSpec((1, 1, TOEP_W), lambda h: (h, 0, 0))],
        out_specs=pl.BlockSpec((1, BAND_TQ, BAND_W), lambda h: (h, 0, 0)),
        out_shape=jax.ShapeDtypeStruct((H, BAND_TQ, BAND_W), F32),
        compiler_params=_params(("parallel",)),
    )(diag.reshape(H, 1, TOEP_W))


def _toeplitz_bwd(dbias, name="toeplitz_bwd"):
    H = dbias.shape[0]

    def body(d_ref, o_ref):
        x = jnp.concatenate([jnp.zeros((BAND_TQ, BAND_TQ), F32), d_ref[0]], axis=1)
        o_ref[0] = jnp.sum(_skew_bits(x, left=True), axis=0, keepdims=True)

    return pl.pallas_call(
        body, name=name, grid=(H,),
        in_specs=[pl.BlockSpec((1, BAND_TQ, BAND_W), lambda h: (h, 0, 0))],
        out_specs=pl.BlockSpec((1, 1, TOEP_W), lambda h: (h, 0, 0)),
        out_shape=jax.ShapeDtypeStruct((H, 1, TOEP_W), F32),
        compiler_params=_params(("parallel",)),
    )(dbias).reshape(H, TOEP_W)


_HBM = pl.BlockSpec(memory_space=pltpu.HBM)
_SEM = pl.BlockSpec(memory_space=pltpu.SEMAPHORE)
_EFFECT = pltpu.SideEffectType.DATAFLOW_SIDE_EFFECTING


def _peers():
    x, y, c = lax.axis_index("x"), lax.axis_index("y"), lax.axis_index("c")
    out = []
    for k in range(1, N_DEV):
        peer = (1 - x if (k >> 2) & 1 else x, 1 - y if (k >> 1) & 1 else y, 1 - c if k & 1 else c)
        out.append((peer, 4 * peer[0] + 2 * peer[1] + peer[2]))
    return 4 * x + 2 * y + c, out


def _split_copies(ins, lands, scatter, send_sem, recv_sem, arriving):
    me, peers = _peers()
    out = []
    for a in range(len(ins)):
        for peer, idx in peers:
            out.append(pltpu.make_async_remote_copy(
                src_ref=ins[a].at[idx] if scatter[a] else ins[a],
                dst_ref=lands[a].at[idx if arriving else me], send_sem=send_sem, recv_sem=recv_sem,
                device_id=peer, device_id_type=pl.DeviceIdType.MESH))
    return out


def _exchange_start(arrays, scatter, after, name):
    n = len(arrays)
    lands = [lax.empty((N_DEV,) + (a.shape[1:] if s else a.shape), a.dtype) for a, s in zip(arrays, scatter)]

    def body(*refs):
        ins, lnd = refs[:n], refs[n:2 * n]
        send_sem, recv_sem = refs[2 * n + 1:2 * n + 3]
        token, own_sems = refs[-2], refs[-1]
        for cp in _split_copies(ins, lnd, scatter, send_sem, recv_sem, arriving=False):
            cp.start()
        me, _ = _peers()
        own = [pltpu.make_async_copy(ins[a].at[me] if scatter[a] else ins[a], lnd[a].at[me], own_sems.at[a])
               for a in range(n)]
        for cp in own:
            cp.start()
        for cp in own:
            cp.wait()
        token[...] = jnp.zeros_like(token)

    hbm = lambda a: pltpu.HBM(a.shape, a.dtype)
    out = pl.pallas_call(
        body, name=name,
        out_shape=(pltpu.SemaphoreType.DMA(()), pltpu.SemaphoreType.DMA(()),
                   *[hbm(a) for a in arrays], *[hbm(a) for a in lands],
                   jax.ShapeDtypeStruct((8, LANES), F32)),
        in_specs=[_HBM] * (2 * n) + [pl.BlockSpec(memory_space=pl.ANY)],
        out_specs=(_SEM, _SEM, *([_HBM] * (2 * n)), pl.BlockSpec(memory_space=pltpu.VMEM)),
        input_output_aliases={i: 2 + i for i in range(2 * n)},
        scratch_shapes=[pltpu.SemaphoreType.DMA((n,))],
        compiler_params=pltpu.CompilerParams(has_side_effects=_EFFECT),
    )(*[pltpu.with_memory_space_constraint(a, pltpu.HBM) for a in list(arrays) + lands], after)
    return (out[0], out[1], list(out[2:2 + n]), list(out[2 + n:2 + 2 * n]), tuple(scatter)), out[-1]


def _exchange_wait(handle, after, name):
    send_sem, recv_sem, ins, lands, scatter = handle
    n = len(ins)

    def body(*refs):
        i_ref, l_ref = refs[:n], refs[n:2 * n]
        s_sem, r_sem = refs[2 * n:2 * n + 2]
        for cp in _split_copies(i_ref, l_ref, scatter, s_sem, r_sem, arriving=False):
            cp.wait_send()
        for cp in _split_copies(i_ref, l_ref, scatter, s_sem, r_sem, arriving=True):
            cp.wait_recv()

    hbm = lambda a: pltpu.HBM(a.shape, a.dtype)
    out = pl.pallas_call(
        body, name=name,
        out_shape=tuple(hbm(a) for a in ins + lands),
        in_specs=[_HBM] * (2 * n) + [_SEM, _SEM, pl.BlockSpec(memory_space=pl.ANY)],
        out_specs=tuple([_HBM] * (2 * n)),
        input_output_aliases={i: i for i in range(2 * n)},
        compiler_params=pltpu.CompilerParams(has_side_effects=_EFFECT),
    )(*ins, *lands, send_sem, recv_sem, after)
    return list(out[n:])


def _adamw(w, parts, m, v, name="adamw"):
    R, C = w.shape
    tr = next(t for t in (256, 128, 64, 32, 16, 8) if R % t == 0)
    c1 = 1.0 - ADAM_B1 ** ADAM_STEP
    c2 = 1.0 - ADAM_B2 ** ADAM_STEP

    def body(w_ref, p_ref, m_ref, v_ref, g_ref, d_ref, nm_ref, nv_ref):
        g = p_ref[0].astype(F32)
        for i in range(1, N_DEV):
            g = g + p_ref[i].astype(F32)
        nm = ADAM_B1 * m_ref[...] + (1.0 - ADAM_B1) * g
        nv = ADAM_B2 * v_ref[...] + (1.0 - ADAM_B2) * (g * g)
        g_ref[...] = g
        nm_ref[...] = nm
        nv_ref[...] = nv
        d_ref[...] = -ADAM_LR * ((nm / c1) / (jnp.sqrt(nv / c2) + ADAM_EPS) + ADAM_WD * w_ref[...])

    blk = pl.BlockSpec((tr, C), lambda i: (i, 0))
    return pl.pallas_call(
        body, name=name, grid=(R // tr,),
        in_specs=[blk, pl.BlockSpec((N_DEV, tr, C), lambda i: (0, i, 0)), blk, blk],
        out_specs=[blk] * 4,
        out_shape=[jax.ShapeDtypeStruct((R, C), F32)] * 4,
        compiler_params=_params(("parallel",)),
    )(w, parts, m, v)


_O1 = Q_LORA
_O2 = _O1 + KV_LORA
_O3 = _O2 + MLA_ROPE
_NB = SB_HEADS * SB_DIM
IN_W = _O2 + LANES + 3 * _NB
COL_KR = _O2 // LANES
COL_SB = COL_KR + 1


def _w_in_local(w):
    kr = w[_O2:_O3]
    pad = jnp.zeros((LANES - 2 * MLA_ROPE, w.shape[1]), w.dtype)
    return jnp.concatenate([w[:_O2], kr, kr, pad, w[_O3:]], axis=0)


def _w_in_grad(g):
    kr = (g[_O2:_O2 + MLA_ROPE].astype(F32) + g[_O2 + MLA_ROPE:_O2 + 2 * MLA_ROPE].astype(F32)).astype(g.dtype)
    return jnp.concatenate([g[:_O2], kr, g[_O2 + LANES:]], axis=0)


def _w_uq_local(w):
    w3 = w.reshape(MLA_HEADS // 2, 2, MLA_NOPE + MLA_ROPE, w.shape[1])
    nope = w3[:, :, :MLA_NOPE].reshape(MLA_HEADS // 2, 2 * MLA_NOPE, w.shape[1])
    rope = w3[:, :, MLA_NOPE:].reshape(MLA_HEADS // 2, 2 * MLA_ROPE, w.shape[1])
    pad = jnp.zeros((MLA_HEADS // 2, LANES - 2 * MLA_ROPE, w.shape[1]), w.dtype)
    return jnp.concatenate([nope, rope, pad], axis=1).reshape(-1, w.shape[1])


def _w_uq_grad(g):
    g3 = g.reshape(MLA_HEADS // 2, 2 * LANES, g.shape[1])
    nope = g3[:, :2 * MLA_NOPE].reshape(MLA_HEADS // 2, 2, MLA_NOPE, g.shape[1])
    rope = g3[:, LANES:LANES + 2 * MLA_ROPE].reshape(MLA_HEADS // 2, 2, MLA_ROPE, g.shape[1])
    return jnp.concatenate([nope, rope], axis=2).reshape(-1, g.shape[1])


def _w_ukv_local(w):
    w3 = w.reshape(MLA_HEADS, MLA_NOPE + MLA_V, w.shape[1])
    return jnp.concatenate([w3[:, :MLA_NOPE].reshape(-1, w.shape[1]),
                            w3[:, MLA_NOPE:].reshape(-1, w.shape[1])], axis=0)


def _w_ukv_grad(g):
    half = MLA_HEADS * MLA_NOPE
    kn = g[:half].reshape(MLA_HEADS, MLA_NOPE, g.shape[1])
    vv = g[half:].reshape(MLA_HEADS, MLA_V, g.shape[1])
    return jnp.concatenate([kn, vv], axis=1).reshape(-1, g.shape[1])


def _rope_tables(T):
    pos = jnp.arange(T, dtype=F32)
    inv_freq = ROPE_THETA ** (-jnp.arange(0, MLA_ROPE, 2, dtype=F32) / MLA_ROPE)
    ang = pos[:, None] * inv_freq[None, :]
    cos, sin = jnp.cos(ang), jnp.sin(ang)
    ones = jnp.ones((T, LANES - 2 * MLA_ROPE), F32)
    cos_k = jnp.concatenate([cos, cos, cos, cos, ones], axis=1)
    sin_k = jnp.concatenate([-sin, sin, -sin, sin, 0.0 * ones], axis=1)
    cos_q = jnp.concatenate([jnp.ones((T, LANES), F32), cos_k], axis=1)
    sin_q = jnp.concatenate([jnp.zeros((T, LANES), F32), sin_k], axis=1)
    return cos_q, sin_q, cos_k, sin_k


def _bias_diag_index():
    ell = np.arange(TOEP_W)
    return np.clip(BAND_W - ell, -REL_CLIP, REL_CLIP) + REL_CLIP


def _local_step(x, target, small, get_weights, put_grads):
    T = x.shape[0]
    cos_q, sin_q, cos_k, sin_k = _rope_tables(T)
    G = {}
    W = dict(small)

    W.update(get_weights("mix0", None))
    u0 = _rms_fwd(x, W["g_mix"][0:1], name="rms_mix0")
    proj = _mm(u0, W["w_in_t"], dims="nt", name="proj_in")
    c_q, c_kv = proj[:, :_O1], proj[:, _O1:_O2]
    nq = _rms_fwd(c_q, W["g_cq"], name="rms_cq")
    nkv = _rms_fwd(c_kv, W["g_ckv"], name="rms_ckv")
    qa_raw = _mm(nq, W["w_uq_t"], dims="nt", name="proj_uq")
    qa = _rope(qa_raw, cos_q, sin_q, 0, qa_raw.shape[1] // LANES, BF16, name="rope_q")
    kv = _mm(nkv, W["w_ukv_t"], dims="nt", out_dtype=BF16, name="proj_ukv")
    kr = _rope(proj, cos_k, sin_k, COL_KR, 1, BF16, name="rope_k")
    o_a, lse = _mla_fwd(qa, kv, kr)
    o_b = _sb_fwd(proj, COL_SB)
    o_ab = jnp.concatenate([o_a, o_b], axis=1)
    h1 = _mm(o_ab, W["ev_w_out"], res=x, name="out_ev")

    def ffn_fwd(h, layer):
        W.update(get_weights(f"ffn{layer}", h))
        return _ffn_fwd(h, W["g_ffn"][layer:layer + 1], W[f"w_gate_t{layer}"], W[f"w_up_t{layer}"],
                        W[f"w_down{layer}"], name=f"ffn_fwd{layer}")

    h2, u1, a0, b0 = ffn_fwd(h1, 0)

    W.update(get_weights("mix1", h2))
    u2 = _rms_fwd(h2, W["g_mix"][1:2], name="rms_mix1")
    qkv = _mm(u2, W["od_w_qkv_t"], dims="nt", out_dtype=BF16, name="proj_qkv")
    nc = C_HEADS * C_DIM
    pad = ((PAD_KEYS, 0), (0, 0))
    k_pad, v_pad = jnp.pad(qkv[:, nc:2 * nc], pad), jnp.pad(qkv[:, 2 * nc:], pad)
    diag_idx = _bias_diag_index()
    bias_w = _toeplitz(W["od_rel_bias"][:, diag_idx])
    o_c = _band_fwd(qkv, k_pad, v_pad, bias_w)
    h3 = _mm(o_c, W["od_w_out"], res=h2, name="out_od")
    h4, u3, a1, b1 = ffn_fwd(h3, 1)

    loss, dh, dhb, G["g_final"] = _loss_head(h4, W["g_final"], target)

    def ffn_bwd(dh, dhb, h, u, a, b, layer):
        du, g_gate, g_up, g_down = _ffn_bwd(dhb, u, a, b, W[f"w_gate_t{layer}"], W[f"w_up_t{layer}"],
                                            W[f"w_down{layer}"], name=f"ffn_bwd{layer}")
        tok = put_grads(f"ffn{layer}", {"w_gate_t": g_gate, "w_up_t": g_up, "w_down": g_down})
        return _rms_bwd(h, W["g_ffn"][layer:layer + 1] + tok[:1, :1], du, dres=dh, name=f"rms_ffn_bwd{layer}")

    dh3, dh3b, g_gffn1 = ffn_bwd(dh, dhb, h3, u3, a1, b1, 1)

    do_c = _mm(dh3b, W["od_w_out"], dims="nt", name="out_od_dx")
    g_od_out = _mm(o_c, dh3b, dims="tn", out_dtype=BF16, name="out_od_dw")
    dq_c, dk_p, dv_p, dbias_w = _band_bwd(qkv, k_pad, v_pad, bias_w, do_c)
    dqkv = jnp.concatenate([dq_c, dk_p[PAD_KEYS:], dv_p[PAD_KEYS:]], axis=1)
    du2 = _mm(dqkv, W["od_w_qkv_t"], name="proj_qkv_dx")
    tok = put_grads("mix1", {"od_w_qkv_t": _mm(dqkv, u2, dims="tn", out_dtype=BF16, name="proj_qkv_dw"),
                             "od_w_out": g_od_out})
    ddiag = _toeplitz_bwd(dbias_w)
    n_far = BAND_W - REL_CLIP + 1
    G["od_rel_bias"] = jnp.concatenate(
        [jnp.zeros((C_HEADS, REL_CLIP - BAND_TQ + 1), F32), ddiag[:, n_far:][:, ::-1],
         jnp.sum(ddiag[:, :n_far], axis=1, keepdims=True)], axis=1)
    dh2, dh2b, g_gmix1 = _rms_bwd(h2, W["g_mix"][1:2] + tok[:1, :1], du2, dres=dh3, name="rms_mix_bwd1")

    dh1, dh1b, g_gffn0 = ffn_bwd(dh2, dh2b, h1, u1, a0, b0, 0)
    G["g_ffn"] = jnp.concatenate([g_gffn0, g_gffn1], axis=0)

    do_ab = _mm(dh1b, W["ev_w_out"], dims="nt", name="out_ev_dx")
    g0 = {"ev_w_out": _mm(o_ab, dh1b, dims="tn", out_dtype=BF16, name="out_ev_dw")}
    dqa, dkn, dva, dkr = _mla_bwd(qa, kv, kr, o_a, lse, do_ab, 0)
    dqb, dkb, dvb = _sb_bwd(proj, COL_SB, do_ab, MLA_HEADS // 2)
    dqa_raw = _rope(dqa, cos_q, -sin_q, 0, dqa.shape[1] // LANES, F32, name="rope_q_bwd")
    g0["w_uq_t"] = _mm(dqa_raw, nq, dims="tn", name="proj_uq_dw")
    dnq = _mm(dqa_raw, W["w_uq_t"], name="proj_uq_dx")
    dc_q, _, G["g_cq"] = _rms_bwd(c_q, W["g_cq"], dnq, name="rms_cq_bwd")
    dkv = jnp.concatenate([dkn, dva], axis=1)
    g0["w_ukv_t"] = _mm(dkv, nkv, dims="tn", name="proj_ukv_dw")
    dnkv = _mm(dkv, W["w_ukv_t"], name="proj_ukv_dx")
    dc_kv, _, G["g_ckv"] = _rms_bwd(c_kv, W["g_ckv"], dnkv, name="rms_ckv_bwd")
    dkr_raw = _rope(dkr, cos_k, -sin_k, 0, 1, F32, name="rope_k_bwd")
    dproj = jnp.concatenate([dc_q, dc_kv, dkr_raw, dqb, dkb, dvb], axis=1)
    du0 = _mm(dproj, W["w_in_t"], name="proj_in_dx")
    g0["w_in_t"] = _mm(dproj, u0, dims="tn", name="proj_in_dw")
    tok = put_grads("mix0", g0)
    dx, _, g_gmix0 = _rms_bwd(x, W["g_mix"][0:1] + tok[:1, :1], du0, dres=dh1, name="rms_mix_bwd0")
    G["g_mix"] = jnp.concatenate([g_gmix0, g_gmix1], axis=0)
    return loss[0, 0], dx, G


_BIG = ["ev_w_in", "ev_w_uq", "ev_w_ukv", "ev_w_out", "od_w_qkv", "od_w_out", "w_gate", "w_up", "w_down"]
_COL_SHARDED = {"ev_w_in", "ev_w_uq", "ev_w_ukv", "od_w_qkv", "w_gate", "w_up"}
_SMALL = ["ev_g_cq", "ev_g_ckv", "od_rel_bias", "g_mix", "g_ffn", "g_final"]
_GROUPS = {
    "mix0": ["ev_w_in", "ev_w_uq", "ev_w_ukv", "ev_w_out"],
    "ffn0": ["w_gate0", "w_up0", "w_down0"],
    "mix1": ["od_w_qkv", "od_w_out"],
    "ffn1": ["w_gate1", "w_up1", "w_down1"],
}
_GROUP_SRC = {n + str(l): (n, l) for n in ("w_gate", "w_up", "w_down") for l in (0, 1)}
_SMALL_ROWS = 8
_SMALL_COLS = 1792


def _pack_small(vals):
    flat = jnp.concatenate([v.reshape(-1).astype(F32) for v in vals])
    flat = jnp.pad(flat, (0, _SMALL_ROWS * _SMALL_COLS - flat.shape[0]))
    return flat.reshape(_SMALL_ROWS, _SMALL_COLS)


def _unpack_small(packed, like):
    flat = packed.reshape(-1)
    out, off = [], 0
    for v in like:
        out.append(flat[off:off + v.size].reshape(v.shape))
        off += v.size
    return out


def kernel(x, ev_w_in, ev_g_cq, ev_w_uq, ev_g_ckv, ev_w_ukv, ev_w_out, od_w_qkv, od_rel_bias, od_w_out, g_mix, g_ffn, w_gate, w_up, w_down, g_final, loss_target, m_ev_w_in, m_ev_g_cq, m_ev_w_uq, m_ev_g_ckv, m_ev_w_ukv, m_ev_w_out, m_od_w_qkv, m_od_rel_bias, m_od_w_out, m_g_mix, m_g_ffn, m_w_gate, m_w_up, m_w_down, m_g_final, v_ev_w_in, v_ev_g_cq, v_ev_w_uq, v_ev_g_ckv, v_ev_w_ukv, v_ev_w_out, v_od_w_qkv, v_od_rel_bias, v_od_w_out, v_g_mix, v_g_ffn, v_w_gate, v_w_up, v_w_down, v_g_final):
    args = dict(locals())
    w = {n: args[n] for n in _BIG + _SMALL}
    mom = {n: args["m_" + n] for n in _BIG + _SMALL}
    var = {n: args["v_" + n] for n in _BIG + _SMALL}

    own = {}
    for grp, names in _GROUPS.items():
        for n in names:
            base, layer = _GROUP_SRC.get(n, (n, 0))
            shard = w[base][layer:layer + 1]
            own[n] = (jnp.swapaxes(shard, 1, 2) if base in _COL_SHARDED else shard).astype(BF16)
    gather, token = {}, x[0, :8, :LANES]
    for grp, names in _GROUPS.items():
        gather[grp], token = _exchange_start([own[n] for n in names], [False] * len(names), token,
                                             name="gather_start_" + grp)

    def get_weights(grp, after):
        names = _GROUPS[grp]
        lands = _exchange_wait(gather[grp], token if after is None else after, name="gather_wait_" + grp)
        full = {n: l.reshape(-1, l.shape[-1]) for n, l in zip(names, lands)}
        if grp == "mix0":
            return {"w_in_t": _w_in_local(full["ev_w_in"]), "w_uq_t": _w_uq_local(full["ev_w_uq"]),
                    "w_ukv_t": _w_ukv_local(full["ev_w_ukv"]), "ev_w_out": full["ev_w_out"]}
        if grp == "mix1":
            return {"od_w_qkv_t": full["od_w_qkv"], "od_w_out": full["od_w_out"]}
        layer = grp[-1]
        return {"w_gate_t" + layer: full["w_gate" + layer], "w_up_t" + layer: full["w_up" + layer],
                "w_down" + layer: full["w_down" + layer]}

    scatter = {}

    def put_grads(grp, g):
        if grp == "mix0":
            g = {"ev_w_in": _w_in_grad(g["w_in_t"]), "ev_w_uq": _w_uq_grad(g["w_uq_t"]),
                 "ev_w_ukv": _w_ukv_grad(g["w_ukv_t"]), "ev_w_out": g["ev_w_out"]}
        elif grp == "mix1":
            g = {"od_w_qkv": g["od_w_qkv_t"], "od_w_out": g["od_w_out"]}
        else:
            layer = grp[-1]
            g = {"w_gate" + layer: g["w_gate_t"], "w_up" + layer: g["w_up_t"], "w_down" + layer: g["w_down"]}
        names = _GROUPS[grp]
        send = [g[n].reshape(N_DEV, 1, g[n].shape[0] // N_DEV, g[n].shape[1]).astype(BF16) for n in names]
        handle, tok = _exchange_start(send, [True] * len(names), send[0], name="scatter_start_" + grp)
        scatter[grp] = handle
        return tok

    small = {"g_cq": ev_g_cq, "g_ckv": ev_g_ckv, "od_rel_bias": od_rel_bias[0],
             "g_mix": g_mix + token[0, 0], "g_ffn": g_ffn, "g_final": g_final.reshape(1, -1)}
    loss_part, dx, G = _local_step(x[0], loss_target[0], small, get_weights, put_grads)
    loss = lax.psum(loss_part, ("x", "y", "c"))
    g_small = _pack_small([G["g_cq"], G["g_ckv"], G["od_rel_bias"], G["g_mix"], G["g_ffn"], G["g_final"]])
    small_handle, _ = _exchange_start([g_small], [False], dx, name="gather_start_small")

    grads, deltas, new_m, new_v = {}, {}, {}, {}
    parts, after = {}, dx
    for grp in ("ffn1", "mix1", "ffn0", "mix0"):
        lands = _exchange_wait(scatter[grp], after, name="scatter_wait_" + grp)
        for n, l in zip(_GROUPS[grp], lands):
            parts[n] = jnp.swapaxes(l, 2, 3) if _GROUP_SRC.get(n, (n, 0))[0] in _COL_SHARDED else l
        after = lands[0]
    for n in ("w_gate", "w_up", "w_down"):
        parts[n] = jnp.concatenate([parts[n + "0"], parts[n + "1"]], axis=1)
    for n in _BIG:
        shp = w[n].shape
        r2 = (-1, shp[-1])
        res = _adamw(w[n].reshape(r2), parts[n].reshape((N_DEV,) + w[n].reshape(r2).shape),
                     mom[n].reshape(r2), var[n].reshape(r2), name="adamw_" + n)
        grads[n], deltas[n], new_m[n], new_v[n] = [r.reshape(shp) for r in res]
    small_w = [w[n] for n in _SMALL]
    small_parts = _exchange_wait(small_handle, after, name="gather_wait_small")[0]
    res = _adamw(_pack_small(small_w), small_parts, _pack_small([mom[n] for n in _SMALL]),
                 _pack_small([var[n] for n in _SMALL]), name="adamw_small")
    for d, packed in zip((grads, deltas, new_m, new_v), res):
        for n, val in zip(_SMALL, _unpack_small(packed, small_w)):
            d[n] = val

    order = ["ev_w_in", "ev_g_cq", "ev_w_uq", "ev_g_ckv", "ev_w_ukv", "ev_w_out", "od_w_qkv", "od_rel_bias",
             "od_w_out", "g_mix", "g_ffn", "w_gate", "w_up", "w_down", "g_final"]
    out = [loss, dx[None]]
    for d in (grads, deltas, new_m, new_v):
        out += [d[n] for n in order]
    return tuple(out)
```

```python
import functools

import numpy as np
import jax
import jax.numpy as jnp
from jax import lax
from jax.experimental import pallas as pl
from jax.experimental.pallas import tpu as pltpu

F32 = jnp.float32
BF16 = jnp.bfloat16

D_MODEL = 1024
CHUNK = 64
MLA_HEADS = 8
MLA_NOPE = 64
MLA_ROPE = 32
MLA_V = 64
Q_LORA = 384
KV_LORA = 256
ROPE_THETA = 10000.0
SB_HEADS = 8
SB_DIM = 64
C_HEADS = 16
C_DIM = 64
LEFT_CHUNKS = 8
REL_CLIP = 256
D_FF = 2816
RMS_EPS = 1e-6
ADAM_LR = 0.001
ADAM_B1 = 0.9
ADAM_B2 = 0.999
ADAM_EPS = 1e-08
ADAM_WD = 0.01
ADAM_STEP = 10

N_DEV = 8
LANES = 128
VMEM_LIMIT = 56 * 1024 * 1024
NEG = -1e30
PAD_KEYS = LEFT_CHUNKS * CHUNK
BAND_TQ = 128
BAND_W = BAND_TQ + PAD_KEYS
TOEP_W = BAND_W + BAND_TQ

NN = (((1,), (0,)), ((), ()))
NT = (((1,), (1,)), ((), ()))
TN = (((0,), (0,)), ((), ()))


def _dot(a, b, dn):
    return lax.dot_general(a, b, dn, preferred_element_type=F32)


def _pick(dim, pref):
    if dim <= pref:
        return dim
    best = None
    for t in range(LANES, pref + 1, LANES):
        if dim % t == 0:
            best = t
    assert best is not None, (dim, pref)
    return best


def _params(sem):
    return pltpu.CompilerParams(dimension_semantics=sem, vmem_limit_bytes=VMEM_LIMIT)


def _mm(a, b, dims="nn", res=None, out_dtype=F32, name="mm"):
    if dims == "nn":
        (M, K), (K2, N) = a.shape, b.shape
    elif dims == "nt":
        (M, K), (N, K2) = a.shape, b.shape
    else:
        (K, M), (K2, N) = a.shape, b.shape
    assert K == K2, (a.shape, b.shape, dims)
    tm, tn, tk = _pick(M, 512), _pick(N, 1408), _pick(K, 1408)
    nk = K // tk
    dn = {"nn": NN, "nt": NT, "tn": TN}[dims]
    has_res = res is not None

    def body(*refs):
        if has_res:
            a_ref, b_ref, r_ref, o_ref, acc = refs
        else:
            a_ref, b_ref, o_ref, acc = refs
        k = pl.program_id(2)

        @pl.when(k == 0)
        def _():
            acc[...] = jnp.zeros_like(acc)

        acc[...] += _dot(a_ref[...].astype(BF16), b_ref[...].astype(BF16), dn)

        @pl.when(k == nk - 1)
        def _():
            r = acc[...]
            if has_res:
                r = r + r_ref[...]
            o_ref[...] = r.astype(out_dtype)

    a_spec = (pl.BlockSpec((tk, tm), lambda i, j, k: (k, i)) if dims == "tn"
              else pl.BlockSpec((tm, tk), lambda i, j, k: (i, k)))
    b_spec = (pl.BlockSpec((tn, tk), lambda i, j, k: (j, k)) if dims == "nt"
              else pl.BlockSpec((tk, tn), lambda i, j, k: (k, j)))
    o_spec = pl.BlockSpec((tm, tn), lambda i, j, k: (i, j))
    in_specs = [a_spec, b_spec] + ([o_spec] if has_res else [])
    args = (a, b) + ((res,) if has_res else ())
    return pl.pallas_call(
        body, name=name, grid=(M // tm, N // tn, nk),
        in_specs=in_specs, out_specs=o_spec,
        out_shape=jax.ShapeDtypeStruct((M, N), out_dtype),
        scratch_shapes=[pltpu.VMEM((tm, tn), F32)],
        compiler_params=_params(("parallel", "parallel", "arbitrary")),
    )(*args)


def _rms_fwd(x, g, out_dtype=BF16, name="rms_fwd"):
    T, Fd = x.shape
    tm = _pick(T, 256)

    def body(x_ref, g_ref, o_ref):
        xv = x_ref[...]
        r = lax.rsqrt(jnp.mean(xv * xv, axis=-1, keepdims=True) + RMS_EPS)
        o_ref[...] = (xv * r * g_ref[...]).astype(out_dtype)

    return pl.pallas_call(
        body, name=name, grid=(T // tm,),
        in_specs=[pl.BlockSpec((tm, Fd), lambda i: (i, 0)), pl.BlockSpec((1, Fd), lambda i: (0, 0))],
        out_specs=pl.BlockSpec((tm, Fd), lambda i: (i, 0)),
        out_shape=jax.ShapeDtypeStruct((T, Fd), out_dtype),
        compiler_params=_params(("parallel",)),
    )(x, g)


def _rms_bwd(x, g, dy, dres=None, name="rms_bwd"):
    T, Fd = x.shape
    tm = _pick(T, 256)
    has_res = dres is not None

    def body(*refs):
        if has_res:
            x_ref, g_ref, dy_ref, r_ref, dx_ref, dxb_ref, dg_ref = refs
        else:
            x_ref, g_ref, dy_ref, dx_ref, dxb_ref, dg_ref = refs
        xv, dyv = x_ref[...], dy_ref[...]
        r = lax.rsqrt(jnp.mean(xv * xv, axis=-1, keepdims=True) + RMS_EPS)
        gdy = dyv * g_ref[...]
        dot = jnp.mean(xv * gdy, axis=-1, keepdims=True)
        dx = r * gdy - xv * (r * r * r * dot)
        if has_res:
            dx = dx + r_ref[...]
        dx_ref[...] = dx
        dxb_ref[...] = dx.astype(BF16)

        @pl.when(pl.program_id(0) == 0)
        def _():
            dg_ref[...] = jnp.zeros_like(dg_ref)

        dg_ref[...] += jnp.sum(dyv * xv * r, axis=0, keepdims=True)

    row = pl.BlockSpec((tm, Fd), lambda i: (i, 0))
    vec = pl.BlockSpec((1, Fd), lambda i: (0, 0))
    in_specs = [row, vec, row] + ([row] if has_res else [])
    args = (x, g, dy) + ((dres,) if has_res else ())
    return pl.pallas_call(
        body, name=name, grid=(T // tm,),
        in_specs=in_specs, out_specs=[row, row, vec],
        out_shape=[jax.ShapeDtypeStruct((T, Fd), F32), jax.ShapeDtypeStruct((T, Fd), BF16),
                   jax.ShapeDtypeStruct((1, Fd), F32)],
        compiler_params=_params(("arbitrary",)),
    )(*args)


def _loss_head(h, g, target, name="loss_head"):
    T, Fd = h.shape
    tm = _pick(T, 256)

    def body(h_ref, g_ref, t_ref, loss_ref, dh_ref, dhb_ref, dg_ref):
        xv = h_ref[...]
        r = lax.rsqrt(jnp.mean(xv * xv, axis=-1, keepdims=True) + RMS_EPS)
        diff = xv * r * g_ref[...] - t_ref[...]
        part = 0.5 * jnp.sum(jnp.mean(diff * diff, axis=-1, keepdims=True), axis=0, keepdims=True)
        dyv = diff * (1.0 / Fd)
        gdy = dyv * g_ref[...]
        dot = jnp.mean(xv * gdy, axis=-1, keepdims=True)
        dh = r * gdy - xv * (r * r * r * dot)
        dh_ref[...] = dh
        dhb_ref[...] = dh.astype(BF16)

        @pl.when(pl.program_id(0) == 0)
        def _():
            dg_ref[...] = jnp.zeros_like(dg_ref)
            loss_ref[...] = jnp.zeros_like(loss_ref)

        dg_ref[...] += jnp.sum(dyv * xv * r, axis=0, keepdims=True)
        loss_ref[...] += jnp.broadcast_to(part, loss_ref.shape)

    row = pl.BlockSpec((tm, Fd), lambda i: (i, 0))
    vec = pl.BlockSpec((1, Fd), lambda i: (0, 0))
    return pl.pallas_call(
        body, name=name, grid=(T // tm,),
        in_specs=[row, vec, row],
        out_specs=[pl.BlockSpec((1, LANES), lambda i: (0, 0)), row, row, vec],
        out_shape=[jax.ShapeDtypeStruct((1, LANES), F32), jax.ShapeDtypeStruct((T, Fd), F32),
                   jax.ShapeDtypeStruct((T, Fd), BF16), jax.ShapeDtypeStruct((1, Fd), F32)],
        compiler_params=_params(("arbitrary",)),
    )(h, g, target)


FFN_TF = 256


def _ffn_fwd(h, g, wg_t, wu_t, wd, name="ffn_fwd"):
    T, Dm = h.shape
    Fh = wd.shape[0]
    tm = _pick(T, 1024)
    nf = Fh // FFN_TF

    def body(h_ref, g_ref, wg_ref, wu_ref, wd_ref, o_ref, u_ref, a_ref, b_ref):
        j = pl.program_id(1)

        @pl.when(j == 0)
        def _():
            xv = h_ref[...]
            r = lax.rsqrt(jnp.mean(xv * xv, axis=-1, keepdims=True) + RMS_EPS)
            u_ref[...] = (xv * r * g_ref[...]).astype(BF16)
            o_ref[...] = xv

        u = u_ref[...]
        a = _dot(u, wg_ref[...], NT).astype(BF16)
        b = _dot(u, wu_ref[...], NT).astype(BF16)
        a_ref[...] = a
        b_ref[...] = b
        af = a.astype(F32)
        s = (af * jax.nn.sigmoid(af) * b.astype(F32)).astype(BF16)
        o_ref[...] += _dot(s, wd_ref[...], NN)

    row = pl.BlockSpec((tm, Dm), lambda i, j: (i, 0))
    wblk = pl.BlockSpec((FFN_TF, Dm), lambda i, j: (j, 0))
    ablk = pl.BlockSpec((tm, FFN_TF), lambda i, j: (i, j))
    return pl.pallas_call(
        body, name=name, grid=(T // tm, nf),
        in_specs=[row, pl.BlockSpec((1, Dm), lambda i, j: (0, 0)), wblk, wblk, wblk],
        out_specs=[row, row, ablk, ablk],
        out_shape=[jax.ShapeDtypeStruct((T, Dm), F32), jax.ShapeDtypeStruct((T, Dm), BF16),
                   jax.ShapeDtypeStruct((T, Fh), BF16), jax.ShapeDtypeStruct((T, Fh), BF16)],
        compiler_params=_params(("parallel", "arbitrary")),
    )(h, g, wg_t, wu_t, wd)


def _ffn_bwd(dh, u, a, b, wg_t, wu_t, wd, name="ffn_bwd"):
    T, Dm = dh.shape
    Fh = wd.shape[0]
    nf = Fh // FFN_TF
    once = pl.Buffered(1)

    def body(dh_ref, u_ref, a_ref, b_ref, wg_ref, wu_ref, wd_ref, du_ref, dwg_ref, dwu_ref, dwd_ref):
        j = pl.program_id(0)

        @pl.when(j == 0)
        def _():
            du_ref[...] = jnp.zeros_like(du_ref)

        ds = _dot(dh_ref[...], wd_ref[...], NT)
        af, bf = a_ref[...].astype(F32), b_ref[...].astype(F32)
        sig = jax.nn.sigmoid(af)
        sa = af * sig
        dwd_ref[...] = _dot((sa * bf).astype(BF16), dh_ref[...], TN).astype(BF16)
        dab = jnp.concatenate([(ds * bf * (sig * (1.0 + af * (1.0 - sig)))).astype(BF16),
                               (ds * sa).astype(BF16)], axis=1)
        dw = _dot(dab, u_ref[...], TN)
        dwg_ref[...] = dw[:FFN_TF].astype(BF16)
        dwu_ref[...] = dw[FFN_TF:].astype(BF16)
        du_ref[...] += _dot(dab, jnp.concatenate([wg_ref[...], wu_ref[...]], axis=0), NN)

    full = lambda: pl.BlockSpec((T, Dm), lambda j: (0, 0), pipeline_mode=once)
    wblk = pl.BlockSpec((FFN_TF, Dm), lambda j: (j, 0))
    ablk = pl.BlockSpec((T, FFN_TF), lambda j: (0, j))
    return pl.pallas_call(
        body, name=name, grid=(nf,),
        in_specs=[full(), full(), ablk, ablk, wblk, wblk, wblk],
        out_specs=[pl.BlockSpec((T, Dm), lambda j: (0, 0)), wblk, wblk, wblk],
        out_shape=[jax.ShapeDtypeStruct((T, Dm), F32)] + [jax.ShapeDtypeStruct((Fh, Dm), BF16)] * 3,
        compiler_params=_params(("arbitrary",)),
    )(dh, u, a, b, wg_t, wu_t, wd)


def _rope(x, cos_t, sin_t, col0, ncols, out_dtype, name="rope"):
    T = x.shape[0]
    wt = cos_t.shape[1]
    tm = _pick(T, 256)
    nb = ncols * LANES // wt
    half = MLA_ROPE // 2

    def body(x_ref, c_ref, s_ref, o_ref):
        xv = x_ref[...].astype(F32)
        lane = lax.broadcasted_iota(jnp.int32, xv.shape, 1)
        first = (lane & (MLA_ROPE - 1)) < half
        swapped = jnp.where(first, pltpu.roll(xv, wt - half, 1), pltpu.roll(xv, half, 1))
        o_ref[...] = (xv * c_ref[...] + swapped * s_ref[...]).astype(out_dtype)

    off = col0 * LANES // wt
    return pl.pallas_call(
        body, name=name, grid=(T // tm, nb),
        in_specs=[pl.BlockSpec((tm, wt), lambda i, j: (i, j + off)),
                  pl.BlockSpec((tm, wt), lambda i, j: (i, 0)),
                  pl.BlockSpec((tm, wt), lambda i, j: (i, 0))],
        out_specs=pl.BlockSpec((tm, wt), lambda i, j: (i, j)),
        out_shape=jax.ShapeDtypeStruct((T, ncols * LANES), out_dtype),
        compiler_params=_params(("parallel", "parallel")),
    )(x, cos_t, sin_t)


ATT_T = 256


def _mla_masks(shape):
    lane = lax.broadcasted_iota(jnp.int32, shape, 1)
    m0 = (lane < 64) | ((lane >= 128) & (lane < 160))
    m1 = ((lane >= 64) & (lane < 128)) | ((lane >= 160) & (lane < 192))
    return m0, m1


def _chunk_ok(m_idx, kb, tq, tk):
    tpos = m_idx * tq + lax.broadcasted_iota(jnp.int32, (tq, tk), 0)
    spos = kb * tk + lax.broadcasted_iota(jnp.int32, (tq, tk), 1)
    return (spos >> 6) <= (tpos >> 6)


def _mla_fwd(q, kv, kr, name="mla_fwd"):
    T = q.shape[0]
    tq = tk = _pick(T, ATT_T)
    npair = MLA_HEADS // 2
    scale = (MLA_NOPE + MLA_ROPE) ** -0.5

    def body(q_ref, kn_ref, v_ref, kr_ref, o_ref, lse_ref):
        m_idx = pl.program_id(1)
        qv = q_ref[...]
        m0, m1 = _mla_masks(qv.shape)
        qh = (jnp.where(m0, qv, 0).astype(BF16), jnp.where(m1, qv, 0).astype(BF16))

        def step(kb, carry):
            ks = pl.ds(pl.multiple_of(kb * tk, tk), tk)
            kcat = jnp.concatenate([kn_ref[ks, :], kr_ref[ks, :]], axis=1)
            vv = v_ref[ks, :]
            ok = _chunk_ok(m_idx, kb, tq, tk)
            out = []
            for h in range(2):
                mx, l, acc = carry[3 * h:3 * h + 3]
                s = jnp.where(ok, _dot(qh[h], kcat, NT) * scale, NEG)
                mn = jnp.maximum(mx, jnp.max(s, axis=-1, keepdims=True))
                alpha = jnp.exp(mx - mn)
                p = jnp.exp(s - mn)
                l = alpha * l + jnp.sum(p, axis=-1, keepdims=True)
                acc = alpha * acc + _dot(p.astype(BF16), vv, NN)
                out += [mn, l, acc]
            return tuple(out)

        init = (jnp.full((tq, 1), NEG, F32), jnp.zeros((tq, 1), F32), jnp.zeros((tq, LANES), F32)) * 2
        res = lax.fori_loop(0, m_idx + 1, step, init)
        lane = lax.broadcasted_iota(jnp.int32, (tq, LANES), 1)
        o0 = res[2] / res[1]
        o1 = res[5] / res[4]
        o_ref[...] = jnp.where(lane < 64, o0, o1).astype(o_ref.dtype)
        lse_ref[...] = jnp.where(lane < 64, res[0] + jnp.log(res[1]), res[3] + jnp.log(res[4]))

    full = lambda col: pl.BlockSpec((T, LANES), col)
    return pl.pallas_call(
        body, name=name, grid=(npair, T // tq),
        in_specs=[pl.BlockSpec((tq, 2 * LANES), lambda p, m: (m, p)),
                  full(lambda p, m: (0, p)), full(lambda p, m: (0, npair + p)), full(lambda p, m: (0, 0))],
        out_specs=[pl.BlockSpec((tq, LANES), lambda p, m: (m, p)),
                   pl.BlockSpec((tq, LANES), lambda p, m: (m, p))],
        out_shape=[jax.ShapeDtypeStruct((T, npair * LANES), BF16),
                   jax.ShapeDtypeStruct((T, npair * LANES), F32)],
        compiler_params=_params(("parallel", "arbitrary")),
    )(q, kv, kv, kr)


def _mla_bwd(q, kv, kr, o, lse, do, do_col0, name="mla_bwd"):
    T = q.shape[0]
    tq = tk = _pick(T, ATT_T)
    npair = MLA_HEADS // 2
    scale = (MLA_NOPE + MLA_ROPE) ** -0.5

    def body(q_ref, kn_ref, v_ref, kr_ref, o_ref, lse_ref, do_ref, dq_ref, dkn_ref, dv_ref, dkr_ref):
        p_idx, m_idx = pl.program_id(0), pl.program_id(1)

        @pl.when(m_idx == 0)
        def _():
            dkn_ref[...] = jnp.zeros_like(dkn_ref)
            dv_ref[...] = jnp.zeros_like(dv_ref)

        @pl.when((m_idx == 0) & (p_idx == 0))
        def _():
            dkr_ref[...] = jnp.zeros_like(dkr_ref)

        qv = q_ref[...]
        m0, m1 = _mla_masks(qv.shape)
        qh = (jnp.where(m0, qv, 0).astype(BF16), jnp.where(m1, qv, 0).astype(BF16))
        dov = do_ref[...].astype(F32)
        lane = lax.broadcasted_iota(jnp.int32, (tq, LANES), 1)
        h0 = lane < 64
        prod = dov * o_ref[...].astype(F32)
        delta = (jnp.sum(jnp.where(h0, prod, 0.0), axis=-1, keepdims=True),
                 jnp.sum(jnp.where(h0, 0.0, prod), axis=-1, keepdims=True))
        doh = (jnp.where(h0, dov, 0.0).astype(BF16), jnp.where(h0, 0.0, dov).astype(BF16))
        lsev = lse_ref[...]
        lse_h = (lsev[:, 0:1], lsev[:, 64:65])

        def step(kb, carry):
            ks = pl.ds(pl.multiple_of(kb * tk, tk), tk)
            kcat = jnp.concatenate([kn_ref[ks, :], kr_ref[ks, :]], axis=1)
            vv = v_ref[ks, :]
            ok = _chunk_ok(m_idx, kb, tq, tk)
            dkc = jnp.zeros((tk, 2 * LANES), F32)
            dvv = jnp.zeros((tk, LANES), F32)
            out = []
            for h in range(2):
                s = _dot(qh[h], kcat, NT) * scale
                p = jnp.where(ok, jnp.exp(s - lse_h[h]), 0.0)
                dp = _dot(doh[h], vv, NT)
                ds = (p * (dp - delta[h]) * scale).astype(BF16)
                out.append(carry[h] + _dot(ds, kcat, NN))
                dkc = dkc + _dot(ds, qh[h], TN)
                dvv = dvv + _dot(p.astype(BF16), doh[h], TN)
            dkn_ref[ks, :] += dkc[:, :LANES]
            dkr_ref[ks, :] += dkc[:, LANES:]
            dv_ref[ks, :] += dvv
            return tuple(out)

        init = (jnp.zeros((tq, 2 * LANES), F32),) * 2
        dq0, dq1 = lax.fori_loop(0, m_idx + 1, step, init)
        dq_ref[...] = jnp.where(m0, dq0, jnp.where(m1, dq1, 0.0))

    full = lambda col: pl.BlockSpec((T, LANES), col)
    blk = lambda col: pl.BlockSpec((tq, LANES), col)
    return pl.pallas_call(
        body, name=name, grid=(npair, T // tq),
        in_specs=[pl.BlockSpec((tq, 2 * LANES), lambda p, m: (m, p)),
                  full(lambda p, m: (0, p)), full(lambda p, m: (0, npair + p)), full(lambda p, m: (0, 0)),
                  blk(lambda p, m: (m, p)), blk(lambda p, m: (m, p)),
                  blk(lambda p, m: (m, do_col0 + p))],
        out_specs=[pl.BlockSpec((tq, 2 * LANES), lambda p, m: (m, p)),
                   full(lambda p, m: (0, p)), full(lambda p, m: (0, p)), full(lambda p, m: (0, 0))],
        out_shape=[jax.ShapeDtypeStruct((T, npair * 2 * LANES), F32),
                   jax.ShapeDtypeStruct((T, npair * LANES), F32),
                   jax.ShapeDtypeStruct((T, npair * LANES), F32),
                   jax.ShapeDtypeStruct((T, LANES), F32)],
        compiler_params=_params(("arbitrary", "arbitrary")),
    )(q, kv, kv, kr, o, lse, do)


def _split_dot(x, tri):
    hi = x.astype(BF16)
    lo = (x - hi.astype(F32)).astype(BF16)
    return _dot(hi, tri, NN) + _dot(lo, tri, NN)


def _sb_terms(qh, kk, m_idx, kb, tq, tk, scale):
    z = _dot(qh, kk, NT) * scale
    tpos = m_idx * tq + lax.broadcasted_iota(jnp.int32, (tq, tk), 0)
    spos = kb * tk + lax.broadcasted_iota(jnp.int32, (tq, tk), 1)
    before = spos < tpos
    sp = jnp.maximum(z, 0.0) + jnp.log(1.0 + jnp.exp(-jnp.abs(z)))
    lk = jnp.where(before, -sp, 0.0)
    return z, sp, lk, before


def _sb_fwd(qkv, col0, name="sb_fwd"):
    T = qkv.shape[0]
    tq = tk = _pick(T, ATT_T)
    npair = SB_HEADS // 2
    scale = SB_DIM ** -0.5

    def body(q_ref, k_ref, v_ref, o_ref):
        m_idx = pl.program_id(1)
        qv = q_ref[...].astype(BF16)
        lane = lax.broadcasted_iota(jnp.int32, (tq, LANES), 1)
        h0 = lane < 64
        qh = (jnp.where(h0, qv, 0).astype(BF16), jnp.where(h0, 0, qv).astype(BF16))
        row = lax.broadcasted_iota(jnp.int32, (tk, tk), 0)
        col = lax.broadcasted_iota(jnp.int32, (tk, tk), 1)
        later = (row > col).astype(BF16)

        def step(i, carry):
            kb = m_idx - i
            ks = pl.ds(pl.multiple_of(kb * tk, tk), tk)
            kk = k_ref[ks, :].astype(BF16)
            vv = v_ref[ks, :].astype(BF16)
            out = []
            for h in range(2):
                c, acc = carry[2 * h:2 * h + 2]
                z, sp, lk, before = _sb_terms(qh[h], kk, m_idx, kb, tq, tk, scale)
                a = (z - sp) + _split_dot(lk, later) + c
                w = jnp.where(before, jnp.exp(a), 0.0)
                out += [c + jnp.sum(lk, axis=-1, keepdims=True), acc + _dot(w.astype(BF16), vv, NN)]
            return tuple(out)

        init = (jnp.zeros((tq, 1), F32), jnp.zeros((tq, LANES), F32)) * 2
        res = lax.fori_loop(0, m_idx + 1, step, init)
        o_ref[...] = jnp.where(h0, res[1], res[3]).astype(o_ref.dtype)

    full = lambda col: pl.BlockSpec((T, LANES), col)
    return pl.pallas_call(
        body, name=name, grid=(npair, T // tq),
        in_specs=[pl.BlockSpec((tq, LANES), lambda p, m: (m, col0 + p)),
                  full(lambda p, m: (0, col0 + npair + p)), full(lambda p, m: (0, col0 + 2 * npair + p))],
        out_specs=pl.BlockSpec((tq, LANES), lambda p, m: (m, p)),
        out_shape=jax.ShapeDtypeStruct((T, npair * LANES), BF16),
        compiler_params=_params(("parallel", "arbitrary")),
    )(qkv, qkv, qkv)


def _sb_bwd(qkv, col0, do, do_col0, dep, name="sb_bwd"):
    T = qkv.shape[0]
    tq = tk = _pick(T, ATT_T)
    npair = SB_HEADS // 2
    scale = SB_DIM ** -0.5

    def body(q_ref, k_ref, v_ref, do_ref, dep_ref, dq_ref, dk_ref, dv_ref, e_scr, z_scr, sp_scr):
        m_idx = pl.program_id(1)

        @pl.when(m_idx == 0)
        def _():
            dk_ref[...] = jnp.zeros_like(dk_ref)
            dv_ref[...] = jnp.zeros_like(dv_ref)

        qv = q_ref[...].astype(BF16)
        lane = lax.broadcasted_iota(jnp.int32, (tq, LANES), 1)
        h0 = lane < 64
        qh = (jnp.where(h0, qv, 0).astype(BF16), jnp.where(h0, 0, qv).astype(BF16))
        dov = do_ref[...].astype(F32)
        doh = (jnp.where(h0, dov, 0.0).astype(BF16), jnp.where(h0, 0.0, dov).astype(BF16))
        row = lax.broadcasted_iota(jnp.int32, (tk, tk), 0)
        col = lax.broadcasted_iota(jnp.int32, (tk, tk), 1)
        later = (row > col).astype(BF16)
        earlier = (row < col).astype(BF16)
        dq = []
        for h in range(2):
            def rl(i, c, h=h):
                kb = m_idx - i
                ks = pl.ds(pl.multiple_of(kb * tk, tk), tk)
                kk = k_ref[ks, :].astype(BF16)
                vv = v_ref[ks, :].astype(BF16)
                z, sp, lk, before = _sb_terms(qh[h], kk, m_idx, kb, tq, tk, scale)
                a = (z - sp) + _split_dot(lk, later) + c
                w = jnp.where(before, jnp.exp(a), 0.0)
                dw = _dot(doh[h], vv, NT)
                e_scr[kb] = w * dw
                z_scr[kb] = z
                sp_scr[kb] = sp
                dv_ref[ks, :] += _dot(w.astype(BF16), doh[h], TN)
                return c + jnp.sum(lk, axis=-1, keepdims=True)

            lax.fori_loop(0, m_idx + 1, rl, jnp.zeros((tq, 1), F32))

            def lr(kb, carry, h=h):
                esum, dqa = carry
                ks = pl.ds(pl.multiple_of(kb * tk, tk), tk)
                kk = k_ref[ks, :].astype(BF16)
                e, z, sp = e_scr[kb], z_scr[kb], sp_scr[kb]
                tpos = m_idx * tq + lax.broadcasted_iota(jnp.int32, (tq, tk), 0)
                spos = kb * tk + lax.broadcasted_iota(jnp.int32, (tq, tk), 1)
                prev = _split_dot(e, earlier) + esum
                dz = jnp.where(spos < tpos, e * jnp.exp(-sp) - jnp.exp(z - sp) * prev, 0.0)
                dzb = (dz * scale).astype(BF16)
                dk_ref[ks, :] += _dot(dzb, qh[h], TN)
                return esum + jnp.sum(e, axis=-1, keepdims=True), dqa + _dot(dzb, kk, NN)

            _, dqh = lax.fori_loop(0, m_idx + 1, lr, (jnp.zeros((tq, 1), F32), jnp.zeros((tq, LANES), F32)))
            dq.append(dqh)
        dq_ref[...] = jnp.where(h0, dq[0], dq[1])

    full = lambda col: pl.BlockSpec((T, LANES), col)
    blk = lambda col: pl.BlockSpec((tq, LANES), col)
    return pl.pallas_call(
        body, name=name, grid=(npair, T // tq),
        in_specs=[blk(lambda p, m: (m, col0 + p)),
                  full(lambda p, m: (0, col0 + npair + p)), full(lambda p, m: (0, col0 + 2 * npair + p)),
                  blk(lambda p, m: (m, do_col0 + p)), pl.BlockSpec((8, LANES), lambda p, m: (0, 0))],
        out_specs=[blk(lambda p, m: (m, p)), full(lambda p, m: (0, p)), full(lambda p, m: (0, p))],
        out_shape=[jax.ShapeDtypeStruct((T, npair * LANES), F32)] * 3,
        scratch_shapes=[pltpu.VMEM((T // tk, tq, tk), F32)] * 3,
        compiler_params=_params(("arbitrary", "arbitrary")),
    )(qkv, qkv, qkv, do, dep)


def _band_valid(m_idx):
    qi = lax.broadcasted_iota(jnp.int32, (BAND_TQ, BAND_W), 0)
    j = lax.broadcasted_iota(jnp.int32, (BAND_TQ, BAND_W), 1)
    cq = (m_idx * BAND_TQ + qi) >> 6
    ckp = (m_idx * BAND_TQ + j) >> 6
    return (ckp >= LEFT_CHUNKS) & (ckp >= cq) & (ckp <= cq + LEFT_CHUNKS)


def _band_probs(qh, kw, bias, valid, scale):
    s = _dot(qh, kw, NT) * scale + bias
    s = jnp.where(valid, s, NEG)
    e = jnp.exp(s - jnp.max(s, axis=-1, keepdims=True))
    return e / jnp.sum(e, axis=-1, keepdims=True)


def _band_fwd(qkv, k_pad, v_pad, bias_w, name="band_fwd"):
    T = qkv.shape[0]
    npair = C_HEADS // 2
    scale = C_DIM ** -0.5

    def body(q_ref, k_ref, v_ref, b_ref, o_ref):
        m_idx = pl.program_id(1)
        win = pl.ds(pl.multiple_of(m_idx * BAND_TQ, BAND_TQ), BAND_W)
        kw, vw = k_ref[win, :], v_ref[win, :]
        qv = q_ref[...]
        lane = lax.broadcasted_iota(jnp.int32, (BAND_TQ, LANES), 1)
        h0 = lane < 64
        qh = (jnp.where(h0, qv, 0).astype(BF16), jnp.where(h0, 0, qv).astype(BF16))
        valid = _band_valid(m_idx)
        o = [_dot(_band_probs(qh[h], kw, b_ref[h], valid, scale).astype(BF16), vw, NN) for h in range(2)]
        o_ref[...] = jnp.where(h0, o[0], o[1]).astype(o_ref.dtype)

    Tp = T + PAD_KEYS
    return pl.pallas_call(
        body, name=name, grid=(npair, T // BAND_TQ),
        in_specs=[pl.BlockSpec((BAND_TQ, LANES), lambda p, m: (m, p)),
                  pl.BlockSpec((Tp, LANES), lambda p, m: (0, p)),
                  pl.BlockSpec((Tp, LANES), lambda p, m: (0, p)),
                  pl.BlockSpec((2, BAND_TQ, BAND_W), lambda p, m: (p, 0, 0))],
        out_specs=pl.BlockSpec((BAND_TQ, LANES), lambda p, m: (m, p)),
        out_shape=jax.ShapeDtypeStruct((T, npair * LANES), BF16),
        compiler_params=_params(("parallel", "arbitrary")),
    )(qkv, k_pad, v_pad, bias_w)


def _band_bwd(qkv, k_pad, v_pad, bias_w, do, name="band_bwd"):
    T = qkv.shape[0]
    npair = C_HEADS // 2
    scale = C_DIM ** -0.5

    def body(q_ref, k_ref, v_ref, b_ref, do_ref, dq_ref, dk_ref, dv_ref, db_ref):
        m_idx = pl.program_id(1)

        @pl.when(m_idx == 0)
        def _():
            dk_ref[...] = jnp.zeros_like(dk_ref)
            dv_ref[...] = jnp.zeros_like(dv_ref)
            db_ref[...] = jnp.zeros_like(db_ref)

        win = pl.ds(pl.multiple_of(m_idx * BAND_TQ, BAND_TQ), BAND_W)
        kw, vw = k_ref[win, :], v_ref[win, :]
        qv = q_ref[...]
        dov = do_ref[...].astype(F32)
        lane = lax.broadcasted_iota(jnp.int32, (BAND_TQ, LANES), 1)
        h0 = lane < 64
        qh = (jnp.where(h0, qv, 0).astype(BF16), jnp.where(h0, 0, qv).astype(BF16))
        doh = (jnp.where(h0, dov, 0.0).astype(BF16), jnp.where(h0, 0.0, dov).astype(BF16))
        valid = _band_valid(m_idx)
        dq = []
        dkw = jnp.zeros((BAND_W, LANES), F32)
        dvw = jnp.zeros((BAND_W, LANES), F32)
        for h in range(2):
            p = _band_probs(qh[h], kw, b_ref[h], valid, scale)
            dp = _dot(doh[h], vw, NT)
            dsb = p * (dp - jnp.sum(p * dp, axis=-1, keepdims=True))
            db_ref[h] += dsb
            dsq = (dsb * scale).astype(BF16)
            dq.append(_dot(dsq, kw, NN))
            dkw = dkw + _dot(dsq, qh[h], TN)
            dvw = dvw + _dot(p.astype(BF16), doh[h], TN)
        dq_ref[...] = jnp.where(h0, dq[0], dq[1])
        dk_ref[win, :] += dkw
        dv_ref[win, :] += dvw

    Tp = T + PAD_KEYS
    blk = lambda col: pl.BlockSpec((BAND_TQ, LANES), col)
    full = pl.BlockSpec((Tp, LANES), lambda p, m: (0, p))
    bias = pl.BlockSpec((2, BAND_TQ, BAND_W), lambda p, m: (p, 0, 0))
    return pl.pallas_call(
        body, name=name, grid=(npair, T // BAND_TQ),
        in_specs=[blk(lambda p, m: (m, p)), full, full, bias, blk(lambda p, m: (m, p))],
        out_specs=[blk(lambda p, m: (m, p)), full, full, bias],
        out_shape=[jax.ShapeDtypeStruct((T, npair * LANES), F32),
                   jax.ShapeDtypeStruct((Tp, npair * LANES), F32),
                   jax.ShapeDtypeStruct((Tp, npair * LANES), F32),
                   jax.ShapeDtypeStruct((C_HEADS, BAND_TQ, BAND_W), F32)],
        compiler_params=_params(("arbitrary", "arbitrary")),
    )(qkv, k_pad, v_pad, bias_w, do)


def _skew_bits(x, left):
    w = x.shape[1]
    row = lax.broadcasted_iota(jnp.int32, x.shape, 0)
    for b in range(BAND_TQ.bit_length() - 1):
        amt = (w - (1 << b)) if left else (1 << b)
        x = jnp.where((row >> b) & 1 == 1, pltpu.roll(x, amt, 1), x)
    return x


def _toeplitz(diag, name="toeplitz"):
    H = diag.shape[0]

    def body(d_ref, o_ref):
        x = jnp.broadcast_to(d_ref[0], (BAND_TQ, TOEP_W))
        o_ref[0] = _skew_bits(x, left=False)[:, BAND_TQ:]

    return pl.pallas_call(
        body, name=name, grid=(H,),
        in_specs=[pl.BlockSpec((1, 1, TOEP_W), lambda h: (h, 0, 0))],
        out_specs=pl.BlockSpec((1, BAND_TQ, BAND_W), lambda h: (h, 0, 0)),
        out_shape=jax.ShapeDtypeStruct((H, BAND_TQ, BAND_W), F32),
        compiler_params=_params(("parallel",)),
    )(diag.reshape(H, 1, TOEP_W))


def _toeplitz_bwd(dbias, name="toeplitz_bwd"):
    H = dbias.shape[0]

    def body(d_ref, o_ref):
        x = jnp.concatenate([jnp.zeros((BAND_TQ, BAND_TQ), F32), d_ref[0]], axis=1)
        o_ref[0] = jnp.sum(_skew_bits(x, left=True), axis=0, keepdims=True)

    return pl.pallas_call(
        body, name=name, grid=(H,),
        in_specs=[pl.BlockSpec((1, BAND_TQ, BAND_W), lambda h: (h, 0, 0))],
        out_specs=pl.BlockSpec((1, 1, TOEP_W), lambda h: (h, 0, 0)),
        out_shape=jax.ShapeDtypeStruct((H, 1, TOEP_W), F32),
        compiler_params=_params(("parallel",)),
    )(dbias).reshape(H, TOEP_W)


_HBM = pl.BlockSpec(memory_space=pltpu.HBM)
_SEM = pl.BlockSpec(memory_space=pltpu.SEMAPHORE)
_EFFECT = pltpu.SideEffectType.DATAFLOW_SIDE_EFFECTING


def _peers():
    x, y, c = lax.axis_index("x"), lax.axis_index("y"), lax.axis_index("c")
    out = []
    for k in range(1, N_DEV):
        peer = (1 - x if (k >> 2) & 1 else x, 1 - y if (k >> 1) & 1 else y, 1 - c if k & 1 else c)
        out.append((peer, 4 * peer[0] + 2 * peer[1] + peer[2]))
    return 4 * x + 2 * y + c, out


def _split_copies(ins, lands, scatter, send_sem, recv_sem, arriving):
    me, peers = _peers()
    out = []
    for a in range(len(ins)):
        for peer, idx in peers:
            out.append(pltpu.make_async_remote_copy(
                src_ref=ins[a].at[idx] if scatter[a] else ins[a],
                dst_ref=lands[a].at[idx if arriving else me], send_sem=send_sem, recv_sem=recv_sem,
                device_id=peer, device_id_type=pl.DeviceIdType.MESH))
    return out


def _exchange_start(arrays, scatter, after, name):
    n = len(arrays)
    lands = [lax.empty((N_DEV,) + (a.shape[1:] if s else a.shape), a.dtype) for a, s in zip(arrays, scatter)]

    def body(*refs):
        ins, lnd = refs[:n], refs[n:2 * n]
        send_sem, recv_sem = refs[2 * n + 1:2 * n + 3]
        token, own_sems = refs[-2], refs[-1]
        me, _ = _peers()
        own = [pltpu.make_async_copy(ins[a].at[me] if scatter[a] else ins[a], lnd[a].at[me], own_sems.at[a])
               for a in range(n)]
        for cp in own:
            cp.start()
        for cp in _split_copies(ins, lnd, scatter, send_sem, recv_sem, arriving=False):
            cp.start()
        for cp in own:
            cp.wait()
        token[...] = jnp.zeros_like(token)

    hbm = lambda a: pltpu.HBM(a.shape, a.dtype)
    out = pl.pallas_call(
        body, name=name,
        out_shape=(pltpu.SemaphoreType.DMA(()), pltpu.SemaphoreType.DMA(()),
                   *[hbm(a) for a in arrays], *[hbm(a) for a in lands],
                   jax.ShapeDtypeStruct((8, LANES), F32)),
        in_specs=[_HBM] * (2 * n) + [pl.BlockSpec(memory_space=pl.ANY)],
        out_specs=(_SEM, _SEM, *([_HBM] * (2 * n)), pl.BlockSpec(memory_space=pltpu.VMEM)),
        input_output_aliases={i: 2 + i for i in range(2 * n)},
        scratch_shapes=[pltpu.SemaphoreType.DMA((n,))],
        compiler_params=pltpu.CompilerParams(has_side_effects=_EFFECT),
    )(*[pltpu.with_memory_space_constraint(a, pltpu.HBM) for a in list(arrays) + lands], after)
    return (out[0], out[1], list(out[2:2 + n]), list(out[2 + n:2 + 2 * n]), tuple(scatter)), out[-1]


def _exchange_wait(handle, after, name):
    send_sem, recv_sem, ins, lands, scatter = handle
    n = len(ins)

    def body(*refs):
        i_ref, l_ref = refs[:n], refs[n:2 * n]
        s_sem, r_sem = refs[2 * n:2 * n + 2]
        for cp in _split_copies(i_ref, l_ref, scatter, s_sem, r_sem, arriving=False):
            cp.wait_send()
        for cp in _split_copies(i_ref, l_ref, scatter, s_sem, r_sem, arriving=True):
            cp.wait_recv()

    hbm = lambda a: pltpu.HBM(a.shape, a.dtype)
    out = pl.pallas_call(
        body, name=name,
        out_shape=tuple(hbm(a) for a in ins + lands),
        in_specs=[_HBM] * (2 * n) + [_SEM, _SEM, pl.BlockSpec(memory_space=pl.ANY)],
        out_specs=tuple([_HBM] * (2 * n)),
        input_output_aliases={i: i for i in range(2 * n)},
        compiler_params=pltpu.CompilerParams(has_side_effects=_EFFECT),
    )(*ins, *lands, send_sem, recv_sem, after)
    return list(out[n:])


def _adamw(w, parts, m, v, name="adamw"):
    R, C = w.shape
    tr = next(t for t in (256, 128, 64, 32, 16, 8) if R % t == 0)
    c1 = 1.0 - ADAM_B1 ** ADAM_STEP
    c2 = 1.0 - ADAM_B2 ** ADAM_STEP

    def body(w_ref, p_ref, m_ref, v_ref, g_ref, d_ref, nm_ref, nv_ref):
        g = p_ref[0].astype(F32)
        for i in range(1, N_DEV):
            g = g + p_ref[i].astype(F32)
        nm = ADAM_B1 * m_ref[...] + (1.0 - ADAM_B1) * g
        nv = ADAM_B2 * v_ref[...] + (1.0 - ADAM_B2) * (g * g)
        g_ref[...] = g
        nm_ref[...] = nm
        nv_ref[...] = nv
        d_ref[...] = -ADAM_LR * ((nm / c1) / (jnp.sqrt(nv / c2) + ADAM_EPS) + ADAM_WD * w_ref[...])

    blk = pl.BlockSpec((tr, C), lambda i: (i, 0))
    return pl.pallas_call(
        body, name=name, grid=(R // tr,),
        in_specs=[blk, pl.BlockSpec((N_DEV, tr, C), lambda i: (0, i, 0)), blk, blk],
        out_specs=[blk] * 4,
        out_shape=[jax.ShapeDtypeStruct((R, C), F32)] * 4,
        compiler_params=_params(("parallel",)),
    )(w, parts, m, v)


_O1 = Q_LORA
_O2 = _O1 + KV_LORA
_O3 = _O2 + MLA_ROPE
_NB = SB_HEADS * SB_DIM
IN_W = _O2 + LANES + 3 * _NB
COL_KR = _O2 // LANES
COL_SB = COL_KR + 1


def _w_in_local(w):
    kr = w[_O2:_O3]
    pad = jnp.zeros((LANES - 2 * MLA_ROPE, w.shape[1]), w.dtype)
    return jnp.concatenate([w[:_O2], kr, kr, pad, w[_O3:]], axis=0)


def _w_in_grad(g):
    kr = (g[_O2:_O2 + MLA_ROPE].astype(F32) + g[_O2 + MLA_ROPE:_O2 + 2 * MLA_ROPE].astype(F32)).astype(g.dtype)
    return jnp.concatenate([g[:_O2], kr, g[_O2 + LANES:]], axis=0)


def _w_uq_local(w):
    w3 = w.reshape(MLA_HEADS // 2, 2, MLA_NOPE + MLA_ROPE, w.shape[1])
    nope = w3[:, :, :MLA_NOPE].reshape(MLA_HEADS // 2, 2 * MLA_NOPE, w.shape[1])
    rope = w3[:, :, MLA_NOPE:].reshape(MLA_HEADS // 2, 2 * MLA_ROPE, w.shape[1])
    pad = jnp.zeros((MLA_HEADS // 2, LANES - 2 * MLA_ROPE, w.shape[1]), w.dtype)
    return jnp.concatenate([nope, rope, pad], axis=1).reshape(-1, w.shape[1])


def _w_uq_grad(g):
    g3 = g.reshape(MLA_HEADS // 2, 2 * LANES, g.shape[1])
    nope = g3[:, :2 * MLA_NOPE].reshape(MLA_HEADS // 2, 2, MLA_NOPE, g.shape[1])
    rope = g3[:, LANES:LANES + 2 * MLA_ROPE].reshape(MLA_HEADS // 2, 2, MLA_ROPE, g.shape[1])
    return jnp.concatenate([nope, rope], axis=2).reshape(-1, g.shape[1])


def _w_ukv_local(w):
    w3 = w.reshape(MLA_HEADS, MLA_NOPE + MLA_V, w.shape[1])
    return jnp.concatenate([w3[:, :MLA_NOPE].reshape(-1, w.shape[1]),
                            w3[:, MLA_NOPE:].reshape(-1, w.shape[1])], axis=0)


def _w_ukv_grad(g):
    half = MLA_HEADS * MLA_NOPE
    kn = g[:half].reshape(MLA_HEADS, MLA_NOPE, g.shape[1])
    vv = g[half:].reshape(MLA_HEADS, MLA_V, g.shape[1])
    return jnp.concatenate([kn, vv], axis=1).reshape(-1, g.shape[1])


def _rope_tables(T):
    pos = jnp.arange(T, dtype=F32)
    inv_freq = ROPE_THETA ** (-jnp.arange(0, MLA_ROPE, 2, dtype=F32) / MLA_ROPE)
    ang = pos[:, None] * inv_freq[None, :]
    cos, sin = jnp.cos(ang), jnp.sin(ang)
    ones = jnp.ones((T, LANES - 2 * MLA_ROPE), F32)
    cos_k = jnp.concatenate([cos, cos, cos, cos, ones], axis=1)
    sin_k = jnp.concatenate([-sin, sin, -sin, sin, 0.0 * ones], axis=1)
    cos_q = jnp.concatenate([jnp.ones((T, LANES), F32), cos_k], axis=1)
    sin_q = jnp.concatenate([jnp.zeros((T, LANES), F32), sin_k], axis=1)
    return cos_q, sin_q, cos_k, sin_k


def _bias_diag_index():
    ell = np.arange(TOEP_W)
    return np.clip(BAND_W - ell, -REL_CLIP, REL_CLIP) + REL_CLIP


def _local_step(x, target, small, get_weights, put_grads):
    T = x.shape[0]
    cos_q, sin_q, cos_k, sin_k = _rope_tables(T)
    G = {}
    W = dict(small)

    u0 = _rms_fwd(x, W["g_mix"][0:1], name="rms_mix0")
    W.update(get_weights("in0", u0))
    proj = _mm(u0, W["w_in_t"], dims="nt", name="proj_in")
    W.update(get_weights("mix0", proj))
    c_q, c_kv = proj[:, :_O1], proj[:, _O1:_O2]
    nq = _rms_fwd(c_q, W["g_cq"], name="rms_cq")
    nkv = _rms_fwd(c_kv, W["g_ckv"], name="rms_ckv")
    qa_raw = _mm(nq, W["w_uq_t"], dims="nt", name="proj_uq")
    qa = _rope(qa_raw, cos_q, sin_q, 0, qa_raw.shape[1] // LANES, BF16, name="rope_q")
    kv = _mm(nkv, W["w_ukv_t"], dims="nt", out_dtype=BF16, name="proj_ukv")
    kr = _rope(proj, cos_k, sin_k, COL_KR, 1, BF16, name="rope_k")
    o_a, lse = _mla_fwd(qa, kv, kr)
    o_b = _sb_fwd(proj, COL_SB)
    o_ab = jnp.concatenate([o_a, o_b], axis=1)
    h1 = _mm(o_ab, W["ev_w_out"], res=x, name="out_ev")

    def ffn_fwd(h, layer):
        W.update(get_weights(f"ffn{layer}", h))
        return _ffn_fwd(h, W["g_ffn"][layer:layer + 1], W[f"w_gate_t{layer}"], W[f"w_up_t{layer}"],
                        W[f"w_down{layer}"], name=f"ffn_fwd{layer}")

    h2, u1, a0, b0 = ffn_fwd(h1, 0)

    W.update(get_weights("mix1", h2))
    u2 = _rms_fwd(h2, W["g_mix"][1:2], name="rms_mix1")
    qkv = _mm(u2, W["od_w_qkv_t"], dims="nt", out_dtype=BF16, name="proj_qkv")
    nc = C_HEADS * C_DIM
    pad = ((PAD_KEYS, 0), (0, 0))
    k_pad, v_pad = jnp.pad(qkv[:, nc:2 * nc], pad), jnp.pad(qkv[:, 2 * nc:], pad)
    diag_idx = _bias_diag_index()
    bias_w = _toeplitz(W["od_rel_bias"][:, diag_idx])
    o_c = _band_fwd(qkv, k_pad, v_pad, bias_w)
    h3 = _mm(o_c, W["od_w_out"], res=h2, name="out_od")
    h4, u3, a1, b1 = ffn_fwd(h3, 1)

    loss, dh, dhb, G["g_final"] = _loss_head(h4, W["g_final"], target)

    def ffn_bwd(dh, dhb, h, u, a, b, layer):
        du, g_gate, g_up, g_down = _ffn_bwd(dhb, u, a, b, W[f"w_gate_t{layer}"], W[f"w_up_t{layer}"],
                                            W[f"w_down{layer}"], name=f"ffn_bwd{layer}")
        tok = put_grads(f"ffn{layer}", {"w_gate_t": g_gate, "w_up_t": g_up, "w_down": g_down})
        return _rms_bwd(h, W["g_ffn"][layer:layer + 1] + tok[:1, :1], du, dres=dh, name=f"rms_ffn_bwd{layer}")

    dh3, dh3b, g_gffn1 = ffn_bwd(dh, dhb, h3, u3, a1, b1, 1)

    do_c = _mm(dh3b, W["od_w_out"], dims="nt", name="out_od_dx")
    g_od_out = _mm(o_c, dh3b, dims="tn", out_dtype=BF16, name="out_od_dw")
    dq_c, dk_p, dv_p, dbias_w = _band_bwd(qkv, k_pad, v_pad, bias_w, do_c)
    dqkv = jnp.concatenate([dq_c, dk_p[PAD_KEYS:], dv_p[PAD_KEYS:]], axis=1)
    du2 = _mm(dqkv, W["od_w_qkv_t"], name="proj_qkv_dx")
    tok = put_grads("mix1", {"od_w_qkv_t": _mm(dqkv, u2, dims="tn", out_dtype=BF16, name="proj_qkv_dw"),
                             "od_w_out": g_od_out})
    ddiag = _toeplitz_bwd(dbias_w)
    n_far = BAND_W - REL_CLIP + 1
    G["od_rel_bias"] = jnp.concatenate(
        [jnp.zeros((C_HEADS, REL_CLIP - BAND_TQ + 1), F32), ddiag[:, n_far:][:, ::-1],
         jnp.sum(ddiag[:, :n_far], axis=1, keepdims=True)], axis=1)
    dh2, dh2b, g_gmix1 = _rms_bwd(h2, W["g_mix"][1:2] + tok[:1, :1], du2, dres=dh3, name="rms_mix_bwd1")

    dh1, dh1b, g_gffn0 = ffn_bwd(dh2, dh2b, h1, u1, a0, b0, 0)
    G["g_ffn"] = jnp.concatenate([g_gffn0, g_gffn1], axis=0)

    do_ab = _mm(dh1b, W["ev_w_out"], dims="nt", name="out_ev_dx")
    g0 = {"ev_w_out": _mm(o_ab, dh1b, dims="tn", out_dtype=BF16, name="out_ev_dw")}
    dqa, dkn, dva, dkr = _mla_bwd(qa, kv, kr, o_a, lse, do_ab, 0)
    dqa_raw = _rope(dqa, cos_q, -sin_q, 0, dqa.shape[1] // LANES, F32, name="rope_q_bwd")
    g0["w_uq_t"] = _mm(dqa_raw, nq, dims="tn", name="proj_uq_dw")
    dnq = _mm(dqa_raw, W["w_uq_t"], name="proj_uq_dx")
    dc_q, _, G["g_cq"] = _rms_bwd(c_q, W["g_cq"], dnq, name="rms_cq_bwd")
    dkv = jnp.concatenate([dkn, dva], axis=1)
    g0["w_ukv_t"] = _mm(dkv, nkv, dims="tn", name="proj_ukv_dw")
    dnkv = _mm(dkv, W["w_ukv_t"], name="proj_ukv_dx")
    dc_kv, _, G["g_ckv"] = _rms_bwd(c_kv, W["g_ckv"], dnkv, name="rms_ckv_bwd")
    tok = put_grads("mix0", g0)
    dqb, dkb, dvb = _sb_bwd(proj, COL_SB, do_ab, MLA_HEADS // 2, tok)
    dkr_raw = _rope(dkr, cos_k, -sin_k, 0, 1, F32, name="rope_k_bwd")
    dproj = jnp.concatenate([dc_q, dc_kv, dkr_raw, dqb, dkb, dvb], axis=1)
    du0 = _mm(dproj, W["w_in_t"], name="proj_in_dx")
    tok = put_grads("in0", {"w_in_t": _mm(dproj, u0, dims="tn", name="proj_in_dw")})
    dx, _, g_gmix0 = _rms_bwd(x, W["g_mix"][0:1] + tok[:1, :1], du0, dres=dh1, name="rms_mix_bwd0")
    G["g_mix"] = jnp.concatenate([g_gmix0, g_gmix1], axis=0)
    return loss[0, 0], dx, G


_BIG = ["ev_w_in", "ev_w_uq", "ev_w_ukv", "ev_w_out", "od_w_qkv", "od_w_out", "w_gate", "w_up", "w_down"]
_COL_SHARDED = {"ev_w_in", "ev_w_uq", "ev_w_ukv", "od_w_qkv", "w_gate", "w_up"}
_SMALL = ["ev_g_cq", "ev_g_ckv", "od_rel_bias", "g_mix", "g_ffn", "g_final"]
_GROUPS = {
    "in0": ["ev_w_in"],
    "mix0": ["ev_w_uq", "ev_w_ukv", "ev_w_out"],
    "ffn0": ["w_gate0", "w_up0", "w_down0"],
    "mix1": ["od_w_qkv", "od_w_out"],
    "ffn1": ["w_gate1", "w_up1", "w_down1"],
}
_GROUP_SRC = {n + str(l): (n, l) for n in ("w_gate", "w_up", "w_down") for l in (0, 1)}
_SMALL_ROWS = 8
_SMALL_COLS = 1792


def _pack_small(vals):
    flat = jnp.concatenate([v.reshape(-1).astype(F32) for v in vals])
    flat = jnp.pad(flat, (0, _SMALL_ROWS * _SMALL_COLS - flat.shape[0]))
    return flat.reshape(_SMALL_ROWS, _SMALL_COLS)


def _unpack_small(packed, like):
    flat = packed.reshape(-1)
    out, off = [], 0
    for v in like:
        out.append(flat[off:off + v.size].reshape(v.shape))
        off += v.size
    return out


def kernel(x, ev_w_in, ev_g_cq, ev_w_uq, ev_g_ckv, ev_w_ukv, ev_w_out, od_w_qkv, od_rel_bias, od_w_out, g_mix, g_ffn, w_gate, w_up, w_down, g_final, loss_target, m_ev_w_in, m_ev_g_cq, m_ev_w_uq, m_ev_g_ckv, m_ev_w_ukv, m_ev_w_out, m_od_w_qkv, m_od_rel_bias, m_od_w_out, m_g_mix, m_g_ffn, m_w_gate, m_w_up, m_w_down, m_g_final, v_ev_w_in, v_ev_g_cq, v_ev_w_uq, v_ev_g_ckv, v_ev_w_ukv, v_ev_w_out, v_od_w_qkv, v_od_rel_bias, v_od_w_out, v_g_mix, v_g_ffn, v_w_gate, v_w_up, v_w_down, v_g_final):
    args = dict(locals())
    w = {n: args[n] for n in _BIG + _SMALL}
    mom = {n: args["m_" + n] for n in _BIG + _SMALL}
    var = {n: args["v_" + n] for n in _BIG + _SMALL}

    own = {}
    for grp, names in _GROUPS.items():
        for n in names:
            base, layer = _GROUP_SRC.get(n, (n, 0))
            shard = w[base][layer:layer + 1]
            own[n] = (jnp.swapaxes(shard, 1, 2) if base in _COL_SHARDED else shard).astype(BF16)
    gather, token = {}, x[0, :8, :LANES]
    for grp, names in _GROUPS.items():
        gather[grp], token = _exchange_start([own[n] for n in names], [False] * len(names), token,
                                             name="gather_start_" + grp)

    def get_weights(grp, after):
        names = _GROUPS[grp]
        lands = _exchange_wait(gather[grp], token if after is None else after, name="gather_wait_" + grp)
        full = {n: l.reshape(-1, l.shape[-1]) for n, l in zip(names, lands)}
        if grp == "in0":
            return {"w_in_t": _w_in_local(full["ev_w_in"])}
        if grp == "mix0":
            return {"w_uq_t": _w_uq_local(full["ev_w_uq"]), "w_ukv_t": _w_ukv_local(full["ev_w_ukv"]),
                    "ev_w_out": full["ev_w_out"]}
        if grp == "mix1":
            return {"od_w_qkv_t": full["od_w_qkv"], "od_w_out": full["od_w_out"]}
        layer = grp[-1]
        return {"w_gate_t" + layer: full["w_gate" + layer], "w_up_t" + layer: full["w_up" + layer],
                "w_down" + layer: full["w_down" + layer]}

    scatter = {}

    def put_grads(grp, g):
        if grp == "in0":
            g = {"ev_w_in": _w_in_grad(g["w_in_t"])}
        elif grp == "mix0":
            g = {"ev_w_uq": _w_uq_grad(g["w_uq_t"]), "ev_w_ukv": _w_ukv_grad(g["w_ukv_t"]),
                 "ev_w_out": g["ev_w_out"]}
        elif grp == "mix1":
            g = {"od_w_qkv": g["od_w_qkv_t"], "od_w_out": g["od_w_out"]}
        else:
            layer = grp[-1]
            g = {"w_gate" + layer: g["w_gate_t"], "w_up" + layer: g["w_up_t"], "w_down" + layer: g["w_down"]}
        names = _GROUPS[grp]
        send = [g[n].reshape(N_DEV, 1, g[n].shape[0] // N_DEV, g[n].shape[1]).astype(BF16) for n in names]
        handle, tok = _exchange_start(send, [True] * len(names), send[0], name="scatter_start_" + grp)
        scatter[grp] = handle
        return tok

    small = {"g_cq": ev_g_cq, "g_ckv": ev_g_ckv, "od_rel_bias": od_rel_bias[0],
             "g_mix": g_mix + token[0, 0], "g_ffn": g_ffn, "g_final": g_final.reshape(1, -1)}
    loss_part, dx, G = _local_step(x[0], loss_target[0], small, get_weights, put_grads)
    loss = lax.psum(loss_part, ("x", "y", "c"))
    g_small = _pack_small([G["g_cq"], G["g_ckv"], G["od_rel_bias"], G["g_mix"], G["g_ffn"], G["g_final"]])
    small_handle, _ = _exchange_start([g_small], [False], dx, name="gather_start_small")

    grads, deltas, new_m, new_v = {}, {}, {}, {}
    parts, after = {}, dx

    def wait_parts(grp, after):
        lands = _exchange_wait(scatter[grp], after, name="scatter_wait_" + grp)
        for n, l in zip(_GROUPS[grp], lands):
            parts[n] = jnp.swapaxes(l, 2, 3) if _GROUP_SRC.get(n, (n, 0))[0] in _COL_SHARDED else l
        return lands[0]

    def adamw(n):
        shp = w[n].shape
        r2 = (-1, shp[-1])
        res = _adamw(w[n].reshape(r2), parts[n].reshape((N_DEV,) + w[n].reshape(r2).shape),
                     mom[n].reshape(r2), var[n].reshape(r2), name="adamw_" + n)
        grads[n], deltas[n], new_m[n], new_v[n] = [r.reshape(shp) for r in res]
        return res[0]

    for grp in ("ffn1", "mix1", "ffn0", "mix0"):
        after = wait_parts(grp, after)
    for n in ("w_gate", "w_up", "w_down"):
        parts[n] = jnp.concatenate([parts[n + "0"], parts[n + "1"]], axis=1)
    for n in _BIG[1:]:
        after = adamw(n)
    after = wait_parts("in0", after)
    after = adamw("ev_w_in")
    small_w = [w[n] for n in _SMALL]
    small_parts = _exchange_wait(small_handle, after, name="gather_wait_small")[0]
    res = _adamw(_pack_small(small_w), small_parts, _pack_small([mom[n] for n in _SMALL]),
                 _pack_small([var[n] for n in _SMALL]), name="adamw_small")
    for d, packed in zip((grads, deltas, new_m, new_v), res):
        for n, val in zip(_SMALL, _unpack_small(packed, small_w)):
            d[n] = val

    order = ["ev_w_in", "ev_g_cq", "ev_w_uq", "ev_g_ckv", "ev_w_ukv", "ev_w_out", "od_w_qkv", "od_rel_bias",
             "od_w_out", "g_mix", "g_ffn", "w_gate", "w_up", "w_down", "g_final"]
    out = [loss, dx[None]]
    for d in (grads, deltas, new_m, new_v):
        out += [d[n] for n in order]
    return tuple(out)
```

```python
import functools

import numpy as np
import jax
import jax.numpy as jnp
from jax import lax
from jax.experimental import pallas as pl
from jax.experimental.pallas import tpu as pltpu

F32 = jnp.float32
BF16 = jnp.bfloat16

D_MODEL = 1024
CHUNK = 64
MLA_HEADS = 8
MLA_NOPE = 64
MLA_ROPE = 32
MLA_V = 64
Q_LORA = 384
KV_LORA = 256
ROPE_THETA = 10000.0
SB_HEADS = 8
SB_DIM = 64
C_HEADS = 16
C_DIM = 64
LEFT_CHUNKS = 8
REL_CLIP = 256
D_FF = 2816
RMS_EPS = 1e-6
ADAM_LR = 0.001
ADAM_B1 = 0.9
ADAM_B2 = 0.999
ADAM_EPS = 1e-08
ADAM_WD = 0.01
ADAM_STEP = 10

N_DEV = 8
LANES = 128
VMEM_LIMIT = 56 * 1024 * 1024
NEG = -1e30
PAD_KEYS = LEFT_CHUNKS * CHUNK
BAND_TQ = 128
BAND_W = BAND_TQ + PAD_KEYS
TOEP_W = BAND_W + BAND_TQ

NN = (((1,), (0,)), ((), ()))
NT = (((1,), (1,)), ((), ()))
TN = (((0,), (0,)), ((), ()))


def _dot(a, b, dn):
    return lax.dot_general(a, b, dn, preferred_element_type=F32)


def _pick(dim, pref):
    if dim <= pref:
        return dim
    best = None
    for t in range(LANES, pref + 1, LANES):
        if dim % t == 0:
            best = t
    assert best is not None, (dim, pref)
    return best


def _params(sem):
    return pltpu.CompilerParams(dimension_semantics=sem, vmem_limit_bytes=VMEM_LIMIT)


def _mm(a, b, dims="nn", res=None, out_dtype=F32, name="mm"):
    if dims == "nn":
        (M, K), (K2, N) = a.shape, b.shape
    elif dims == "nt":
        (M, K), (N, K2) = a.shape, b.shape
    else:
        (K, M), (K2, N) = a.shape, b.shape
    assert K == K2, (a.shape, b.shape, dims)
    tm, tn, tk = _pick(M, 512), _pick(N, 1408), _pick(K, 1408)
    nk = K // tk
    dn = {"nn": NN, "nt": NT, "tn": TN}[dims]
    has_res = res is not None

    def body(*refs):
        if has_res:
            a_ref, b_ref, r_ref, o_ref, acc = refs
        else:
            a_ref, b_ref, o_ref, acc = refs
        k = pl.program_id(2)

        @pl.when(k == 0)
        def _():
            acc[...] = jnp.zeros_like(acc)

        acc[...] += _dot(a_ref[...].astype(BF16), b_ref[...].astype(BF16), dn)

        @pl.when(k == nk - 1)
        def _():
            r = acc[...]
            if has_res:
                r = r + r_ref[...]
            o_ref[...] = r.astype(out_dtype)

    a_spec = (pl.BlockSpec((tk, tm), lambda i, j, k: (k, i)) if dims == "tn"
              else pl.BlockSpec((tm, tk), lambda i, j, k: (i, k)))
    b_spec = (pl.BlockSpec((tn, tk), lambda i, j, k: (j, k)) if dims == "nt"
              else pl.BlockSpec((tk, tn), lambda i, j, k: (k, j)))
    o_spec = pl.BlockSpec((tm, tn), lambda i, j, k: (i, j))
    in_specs = [a_spec, b_spec] + ([o_spec] if has_res else [])
    args = (a, b) + ((res,) if has_res else ())
    return pl.pallas_call(
        body, name=name, grid=(M // tm, N // tn, nk),
        in_specs=in_specs, out_specs=o_spec,
        out_shape=jax.ShapeDtypeStruct((M, N), out_dtype),
        scratch_shapes=[pltpu.VMEM((tm, tn), F32)],
        compiler_params=_params(("parallel", "parallel", "arbitrary")),
    )(*args)


def _rms_fwd(x, g, out_dtype=BF16, name="rms_fwd"):
    T, Fd = x.shape
    tm = _pick(T, 256)

    def body(x_ref, g_ref, o_ref):
        xv = x_ref[...]
        r = lax.rsqrt(jnp.mean(xv * xv, axis=-1, keepdims=True) + RMS_EPS)
        o_ref[...] = (xv * r * g_ref[...]).astype(out_dtype)

    return pl.pallas_call(
        body, name=name, grid=(T // tm,),
        in_specs=[pl.BlockSpec((tm, Fd), lambda i: (i, 0)), pl.BlockSpec((1, Fd), lambda i: (0, 0))],
        out_specs=pl.BlockSpec((tm, Fd), lambda i: (i, 0)),
        out_shape=jax.ShapeDtypeStruct((T, Fd), out_dtype),
        compiler_params=_params(("parallel",)),
    )(x, g)


def _rms_bwd(x, g, dy, dres=None, name="rms_bwd"):
    T, Fd = x.shape
    tm = _pick(T, 256)
    has_res = dres is not None

    def body(*refs):
        if has_res:
            x_ref, g_ref, dy_ref, r_ref, dx_ref, dxb_ref, dg_ref = refs
        else:
            x_ref, g_ref, dy_ref, dx_ref, dxb_ref, dg_ref = refs
        xv, dyv = x_ref[...], dy_ref[...]
        r = lax.rsqrt(jnp.mean(xv * xv, axis=-1, keepdims=True) + RMS_EPS)
        gdy = dyv * g_ref[...]
        dot = jnp.mean(xv * gdy, axis=-1, keepdims=True)
        dx = r * gdy - xv * (r * r * r * dot)
        if has_res:
            dx = dx + r_ref[...]
        dx_ref[...] = dx
        dxb_ref[...] = dx.astype(BF16)

        @pl.when(pl.program_id(0) == 0)
        def _():
            dg_ref[...] = jnp.zeros_like(dg_ref)

        dg_ref[...] += jnp.sum(dyv * xv * r, axis=0, keepdims=True)

    row = pl.BlockSpec((tm, Fd), lambda i: (i, 0))
    vec = pl.BlockSpec((1, Fd), lambda i: (0, 0))
    in_specs = [row, vec, row] + ([row] if has_res else [])
    args = (x, g, dy) + ((dres,) if has_res else ())
    return pl.pallas_call(
        body, name=name, grid=(T // tm,),
        in_specs=in_specs, out_specs=[row, row, vec],
        out_shape=[jax.ShapeDtypeStruct((T, Fd), F32), jax.ShapeDtypeStruct((T, Fd), BF16),
                   jax.ShapeDtypeStruct((1, Fd), F32)],
        compiler_params=_params(("arbitrary",)),
    )(*args)


def _loss_head(h, g, target, name="loss_head"):
    T, Fd = h.shape
    tm = _pick(T, 256)

    def body(h_ref, g_ref, t_ref, loss_ref, dh_ref, dhb_ref, dg_ref):
        xv = h_ref[...]
        r = lax.rsqrt(jnp.mean(xv * xv, axis=-1, keepdims=True) + RMS_EPS)
        diff = xv * r * g_ref[...] - t_ref[...]
        part = 0.5 * jnp.sum(jnp.mean(diff * diff, axis=-1, keepdims=True), axis=0, keepdims=True)
        dyv = diff * (1.0 / Fd)
        gdy = dyv * g_ref[...]
        dot = jnp.mean(xv * gdy, axis=-1, keepdims=True)
        dh = r * gdy - xv * (r * r * r * dot)
        dh_ref[...] = dh
        dhb_ref[...] = dh.astype(BF16)

        @pl.when(pl.program_id(0) == 0)
        def _():
            dg_ref[...] = jnp.zeros_like(dg_ref)
            loss_ref[...] = jnp.zeros_like(loss_ref)

        dg_ref[...] += jnp.sum(dyv * xv * r, axis=0, keepdims=True)
        loss_ref[...] += jnp.broadcast_to(part, loss_ref.shape)

    row = pl.BlockSpec((tm, Fd), lambda i: (i, 0))
    vec = pl.BlockSpec((1, Fd), lambda i: (0, 0))
    return pl.pallas_call(
        body, name=name, grid=(T // tm,),
        in_specs=[row, vec, row],
        out_specs=[pl.BlockSpec((1, LANES), lambda i: (0, 0)), row, row, vec],
        out_shape=[jax.ShapeDtypeStruct((1, LANES), F32), jax.ShapeDtypeStruct((T, Fd), F32),
                   jax.ShapeDtypeStruct((T, Fd), BF16), jax.ShapeDtypeStruct((1, Fd), F32)],
        compiler_params=_params(("arbitrary",)),
    )(h, g, target)


FFN_TF = 256


def _ffn_fwd(h, g, wg_t, wu_t, wd, name="ffn_fwd"):
    T, Dm = h.shape
    Fh = wd.shape[0]
    tm = _pick(T, 1024)
    nf = Fh // FFN_TF

    def body(h_ref, g_ref, wg_ref, wu_ref, wd_ref, o_ref, u_ref, a_ref, b_ref):
        j = pl.program_id(1)

        @pl.when(j == 0)
        def _():
            xv = h_ref[...]
            r = lax.rsqrt(jnp.mean(xv * xv, axis=-1, keepdims=True) + RMS_EPS)
            u_ref[...] = (xv * r * g_ref[...]).astype(BF16)
            o_ref[...] = xv

        u = u_ref[...]
        a = _dot(u, wg_ref[...], NT).astype(BF16)
        b = _dot(u, wu_ref[...], NT).astype(BF16)
        a_ref[...] = a
        b_ref[...] = b
        af = a.astype(F32)
        s = (af * jax.nn.sigmoid(af) * b.astype(F32)).astype(BF16)
        o_ref[...] += _dot(s, wd_ref[...], NN)

    row = pl.BlockSpec((tm, Dm), lambda i, j: (i, 0))
    wblk = pl.BlockSpec((FFN_TF, Dm), lambda i, j: (j, 0))
    ablk = pl.BlockSpec((tm, FFN_TF), lambda i, j: (i, j))
    return pl.pallas_call(
        body, name=name, grid=(T // tm, nf),
        in_specs=[row, pl.BlockSpec((1, Dm), lambda i, j: (0, 0)), wblk, wblk, wblk],
        out_specs=[row, row, ablk, ablk],
        out_shape=[jax.ShapeDtypeStruct((T, Dm), F32), jax.ShapeDtypeStruct((T, Dm), BF16),
                   jax.ShapeDtypeStruct((T, Fh), BF16), jax.ShapeDtypeStruct((T, Fh), BF16)],
        compiler_params=_params(("parallel", "arbitrary")),
    )(h, g, wg_t, wu_t, wd)


def _ffn_bwd(dh, u, a, b, wg_t, wu_t, wd, name="ffn_bwd"):
    T, Dm = dh.shape
    Fh = wd.shape[0]
    nf = Fh // FFN_TF
    once = pl.Buffered(1)

    def body(dh_ref, u_ref, a_ref, b_ref, wg_ref, wu_ref, wd_ref, du_ref, dwg_ref, dwu_ref, dwd_ref):
        j = pl.program_id(0)

        @pl.when(j == 0)
        def _():
            du_ref[...] = jnp.zeros_like(du_ref)

        ds = _dot(dh_ref[...], wd_ref[...], NT)
        af, bf = a_ref[...].astype(F32), b_ref[...].astype(F32)
        sig = jax.nn.sigmoid(af)
        sa = af * sig
        dwd_ref[...] = _dot((sa * bf).astype(BF16), dh_ref[...], TN).astype(BF16)
        dab = jnp.concatenate([(ds * bf * (sig * (1.0 + af * (1.0 - sig)))).astype(BF16),
                               (ds * sa).astype(BF16)], axis=1)
        dw = _dot(dab, u_ref[...], TN)
        dwg_ref[...] = dw[:FFN_TF].astype(BF16)
        dwu_ref[...] = dw[FFN_TF:].astype(BF16)
        du_ref[...] += _dot(dab, jnp.concatenate([wg_ref[...], wu_ref[...]], axis=0), NN)

    full = lambda: pl.BlockSpec((T, Dm), lambda j: (0, 0), pipeline_mode=once)
    wblk = pl.BlockSpec((FFN_TF, Dm), lambda j: (j, 0))
    ablk = pl.BlockSpec((T, FFN_TF), lambda j: (0, j))
    return pl.pallas_call(
        body, name=name, grid=(nf,),
        in_specs=[full(), full(), ablk, ablk, wblk, wblk, wblk],
        out_specs=[pl.BlockSpec((T, Dm), lambda j: (0, 0)), wblk, wblk, wblk],
        out_shape=[jax.ShapeDtypeStruct((T, Dm), F32)] + [jax.ShapeDtypeStruct((Fh, Dm), BF16)] * 3,
        compiler_params=_params(("arbitrary",)),
    )(dh, u, a, b, wg_t, wu_t, wd)


def _rope(x, cos_t, sin_t, col0, ncols, out_dtype, name="rope"):
    T = x.shape[0]
    wt = cos_t.shape[1]
    tm = _pick(T, 256)
    nb = ncols * LANES // wt
    half = MLA_ROPE // 2

    def body(x_ref, c_ref, s_ref, o_ref):
        xv = x_ref[...].astype(F32)
        lane = lax.broadcasted_iota(jnp.int32, xv.shape, 1)
        first = (lane & (MLA_ROPE - 1)) < half
        swapped = jnp.where(first, pltpu.roll(xv, wt - half, 1), pltpu.roll(xv, half, 1))
        o_ref[...] = (xv * c_ref[...] + swapped * s_ref[...]).astype(out_dtype)

    off = col0 * LANES // wt
    return pl.pallas_call(
        body, name=name, grid=(T // tm, nb),
        in_specs=[pl.BlockSpec((tm, wt), lambda i, j: (i, j + off)),
                  pl.BlockSpec((tm, wt), lambda i, j: (i, 0)),
                  pl.BlockSpec((tm, wt), lambda i, j: (i, 0))],
        out_specs=pl.BlockSpec((tm, wt), lambda i, j: (i, j)),
        out_shape=jax.ShapeDtypeStruct((T, ncols * LANES), out_dtype),
        compiler_params=_params(("parallel", "parallel")),
    )(x, cos_t, sin_t)


ATT_T = 256


def _mla_masks(shape):
    lane = lax.broadcasted_iota(jnp.int32, shape, 1)
    m0 = (lane < 64) | ((lane >= 128) & (lane < 160))
    m1 = ((lane >= 64) & (lane < 128)) | ((lane >= 160) & (lane < 192))
    return m0, m1


def _chunk_ok(m_idx, kb, tq, tk):
    tpos = m_idx * tq + lax.broadcasted_iota(jnp.int32, (tq, tk), 0)
    spos = kb * tk + lax.broadcasted_iota(jnp.int32, (tq, tk), 1)
    return (spos >> 6) <= (tpos >> 6)


def _mla_fwd(q, kv, kr, name="mla_fwd"):
    T = q.shape[0]
    tq = tk = _pick(T, ATT_T)
    npair = MLA_HEADS // 2
    scale = (MLA_NOPE + MLA_ROPE) ** -0.5

    def body(q_ref, kn_ref, v_ref, kr_ref, o_ref, lse_ref):
        m_idx = pl.program_id(1)
        qv = q_ref[...]
        m0, m1 = _mla_masks(qv.shape)
        qh = (jnp.where(m0, qv, 0).astype(BF16), jnp.where(m1, qv, 0).astype(BF16))

        def step(kb, carry):
            ks = pl.ds(pl.multiple_of(kb * tk, tk), tk)
            kcat = jnp.concatenate([kn_ref[ks, :], kr_ref[ks, :]], axis=1)
            vv = v_ref[ks, :]
            ok = _chunk_ok(m_idx, kb, tq, tk)
            out = []
            for h in range(2):
                mx, l, acc = carry[3 * h:3 * h + 3]
                s = jnp.where(ok, _dot(qh[h], kcat, NT) * scale, NEG)
                mn = jnp.maximum(mx, jnp.max(s, axis=-1, keepdims=True))
                alpha = jnp.exp(mx - mn)
                p = jnp.exp(s - mn)
                l = alpha * l + jnp.sum(p, axis=-1, keepdims=True)
                acc = alpha * acc + _dot(p.astype(BF16), vv, NN)
                out += [mn, l, acc]
            return tuple(out)

        init = (jnp.full((tq, 1), NEG, F32), jnp.zeros((tq, 1), F32), jnp.zeros((tq, LANES), F32)) * 2
        res = lax.fori_loop(0, m_idx + 1, step, init)
        lane = lax.broadcasted_iota(jnp.int32, (tq, LANES), 1)
        o0 = res[2] / res[1]
        o1 = res[5] / res[4]
        o_ref[...] = jnp.where(lane < 64, o0, o1).astype(o_ref.dtype)
        lse_ref[...] = jnp.where(lane < 64, res[0] + jnp.log(res[1]), res[3] + jnp.log(res[4]))

    full = lambda col: pl.BlockSpec((T, LANES), col)
    return pl.pallas_call(
        body, name=name, grid=(npair, T // tq),
        in_specs=[pl.BlockSpec((tq, 2 * LANES), lambda p, m: (m, p)),
                  full(lambda p, m: (0, p)), full(lambda p, m: (0, npair + p)), full(lambda p, m: (0, 0))],
        out_specs=[pl.BlockSpec((tq, LANES), lambda p, m: (m, p)),
                   pl.BlockSpec((tq, LANES), lambda p, m: (m, p))],
        out_shape=[jax.ShapeDtypeStruct((T, npair * LANES), BF16),
                   jax.ShapeDtypeStruct((T, npair * LANES), F32)],
        compiler_params=_params(("parallel", "arbitrary")),
    )(q, kv, kv, kr)


def _mla_bwd(q, kv, kr, o, lse, do, do_col0, name="mla_bwd"):
    T = q.shape[0]
    tq = tk = _pick(T, ATT_T)
    npair = MLA_HEADS // 2
    scale = (MLA_NOPE + MLA_ROPE) ** -0.5

    def body(q_ref, kn_ref, v_ref, kr_ref, o_ref, lse_ref, do_ref, dq_ref, dkn_ref, dv_ref, dkr_ref):
        p_idx, m_idx = pl.program_id(0), pl.program_id(1)

        @pl.when(m_idx == 0)
        def _():
            dkn_ref[...] = jnp.zeros_like(dkn_ref)
            dv_ref[...] = jnp.zeros_like(dv_ref)

        @pl.when((m_idx == 0) & (p_idx == 0))
        def _():
            dkr_ref[...] = jnp.zeros_like(dkr_ref)

        qv = q_ref[...]
        m0, m1 = _mla_masks(qv.shape)
        qh = (jnp.where(m0, qv, 0).astype(BF16), jnp.where(m1, qv, 0).astype(BF16))
        dov = do_ref[...].astype(F32)
        lane = lax.broadcasted_iota(jnp.int32, (tq, LANES), 1)
        h0 = lane < 64
        prod = dov * o_ref[...].astype(F32)
        delta = (jnp.sum(jnp.where(h0, prod, 0.0), axis=-1, keepdims=True),
                 jnp.sum(jnp.where(h0, 0.0, prod), axis=-1, keepdims=True))
        doh = (jnp.where(h0, dov, 0.0).astype(BF16), jnp.where(h0, 0.0, dov).astype(BF16))
        lsev = lse_ref[...]
        lse_h = (lsev[:, 0:1], lsev[:, 64:65])

        def step(kb, carry):
            ks = pl.ds(pl.multiple_of(kb * tk, tk), tk)
            kcat = jnp.concatenate([kn_ref[ks, :], kr_ref[ks, :]], axis=1)
            vv = v_ref[ks, :]
            ok = _chunk_ok(m_idx, kb, tq, tk)
            dkc = jnp.zeros((tk, 2 * LANES), F32)
            dvv = jnp.zeros((tk, LANES), F32)
            out = []
            for h in range(2):
                s = _dot(qh[h], kcat, NT) * scale
                p = jnp.where(ok, jnp.exp(s - lse_h[h]), 0.0)
                dp = _dot(doh[h], vv, NT)
                ds = (p * (dp - delta[h]) * scale).astype(BF16)
                out.append(carry[h] + _dot(ds, kcat, NN))
                dkc = dkc + _dot(ds, qh[h], TN)
                dvv = dvv + _dot(p.astype(BF16), doh[h], TN)
            dkn_ref[ks, :] += dkc[:, :LANES]
            dkr_ref[ks, :] += dkc[:, LANES:]
            dv_ref[ks, :] += dvv
            return tuple(out)

        init = (jnp.zeros((tq, 2 * LANES), F32),) * 2
        dq0, dq1 = lax.fori_loop(0, m_idx + 1, step, init)
        dq_ref[...] = jnp.where(m0, dq0, jnp.where(m1, dq1, 0.0))

    full = lambda col: pl.BlockSpec((T, LANES), col)
    blk = lambda col: pl.BlockSpec((tq, LANES), col)
    return pl.pallas_call(
        body, name=name, grid=(npair, T // tq),
        in_specs=[pl.BlockSpec((tq, 2 * LANES), lambda p, m: (m, p)),
                  full(lambda p, m: (0, p)), full(lambda p, m: (0, npair + p)), full(lambda p, m: (0, 0)),
                  blk(lambda p, m: (m, p)), blk(lambda p, m: (m, p)),
                  blk(lambda p, m: (m, do_col0 + p))],
        out_specs=[pl.BlockSpec((tq, 2 * LANES), lambda p, m: (m, p)),
                   full(lambda p, m: (0, p)), full(lambda p, m: (0, p)), full(lambda p, m: (0, 0))],
        out_shape=[jax.ShapeDtypeStruct((T, npair * 2 * LANES), F32),
                   jax.ShapeDtypeStruct((T, npair * LANES), F32),
                   jax.ShapeDtypeStruct((T, npair * LANES), F32),
                   jax.ShapeDtypeStruct((T, LANES), F32)],
        compiler_params=_params(("arbitrary", "arbitrary")),
    )(q, kv, kv, kr, o, lse, do)


def _split_dot(x, tri):
    hi = x.astype(BF16)
    lo = (x - hi.astype(F32)).astype(BF16)
    return _dot(hi, tri, NN) + _dot(lo, tri, NN)


def _sb_terms(qh, kk, m_idx, kb, tq, tk, scale):
    z = _dot(qh, kk, NT) * scale
    tpos = m_idx * tq + lax.broadcasted_iota(jnp.int32, (tq, tk), 0)
    spos = kb * tk + lax.broadcasted_iota(jnp.int32, (tq, tk), 1)
    before = spos < tpos
    sp = jnp.maximum(z, 0.0) + jnp.log(1.0 + jnp.exp(-jnp.abs(z)))
    lk = jnp.where(before, -sp, 0.0)
    return z, sp, lk, before


def _sb_fwd(qkv, col0, name="sb_fwd"):
    T = qkv.shape[0]
    tq = tk = _pick(T, ATT_T)
    npair = SB_HEADS // 2
    scale = SB_DIM ** -0.5

    def body(q_ref, k_ref, v_ref, o_ref):
        m_idx = pl.program_id(1)
        qv = q_ref[...].astype(BF16)
        lane = lax.broadcasted_iota(jnp.int32, (tq, LANES), 1)
        h0 = lane < 64
        qh = (jnp.where(h0, qv, 0).astype(BF16), jnp.where(h0, 0, qv).astype(BF16))
        row = lax.broadcasted_iota(jnp.int32, (tk, tk), 0)
        col = lax.broadcasted_iota(jnp.int32, (tk, tk), 1)
        later = (row > col).astype(BF16)

        def step(i, carry):
            kb = m_idx - i
            ks = pl.ds(pl.multiple_of(kb * tk, tk), tk)
            kk = k_ref[ks, :].astype(BF16)
            vv = v_ref[ks, :].astype(BF16)
            out = []
            for h in range(2):
                c, acc = carry[2 * h:2 * h + 2]
                z, sp, lk, before = _sb_terms(qh[h], kk, m_idx, kb, tq, tk, scale)
                a = (z - sp) + _split_dot(lk, later) + c
                w = jnp.where(before, jnp.exp(a), 0.0)
                out += [c + jnp.sum(lk, axis=-1, keepdims=True), acc + _dot(w.astype(BF16), vv, NN)]
            return tuple(out)

        init = (jnp.zeros((tq, 1), F32), jnp.zeros((tq, LANES), F32)) * 2
        res = lax.fori_loop(0, m_idx + 1, step, init)
        o_ref[...] = jnp.where(h0, res[1], res[3]).astype(o_ref.dtype)

    full = lambda col: pl.BlockSpec((T, LANES), col)
    return pl.pallas_call(
        body, name=name, grid=(npair, T // tq),
        in_specs=[pl.BlockSpec((tq, LANES), lambda p, m: (m, col0 + p)),
                  full(lambda p, m: (0, col0 + npair + p)), full(lambda p, m: (0, col0 + 2 * npair + p))],
        out_specs=pl.BlockSpec((tq, LANES), lambda p, m: (m, p)),
        out_shape=jax.ShapeDtypeStruct((T, npair * LANES), BF16),
        compiler_params=_params(("parallel", "arbitrary")),
    )(qkv, qkv, qkv)


def _sb_bwd(qkv, col0, do, do_col0, dep, name="sb_bwd"):
    T = qkv.shape[0]
    tq = tk = _pick(T, ATT_T)
    npair = SB_HEADS // 2
    scale = SB_DIM ** -0.5

    def body(q_ref, k_ref, v_ref, do_ref, dep_ref, dq_ref, dk_ref, dv_ref, e_scr, z_scr, sp_scr):
        m_idx = pl.program_id(1)

        @pl.when(m_idx == 0)
        def _():
            dk_ref[...] = jnp.zeros_like(dk_ref)
            dv_ref[...] = jnp.zeros_like(dv_ref)

        qv = q_ref[...].astype(BF16)
        lane = lax.broadcasted_iota(jnp.int32, (tq, LANES), 1)
        h0 = lane < 64
        qh = (jnp.where(h0, qv, 0).astype(BF16), jnp.where(h0, 0, qv).astype(BF16))
        dov = do_ref[...].astype(F32)
        doh = (jnp.where(h0, dov, 0.0).astype(BF16), jnp.where(h0, 0.0, dov).astype(BF16))
        row = lax.broadcasted_iota(jnp.int32, (tk, tk), 0)
        col = lax.broadcasted_iota(jnp.int32, (tk, tk), 1)
        later = (row > col).astype(BF16)
        earlier = (row < col).astype(BF16)
        dq = []
        for h in range(2):
            def rl(i, c, h=h):
                kb = m_idx - i
                ks = pl.ds(pl.multiple_of(kb * tk, tk), tk)
                kk = k_ref[ks, :].astype(BF16)
                vv = v_ref[ks, :].astype(BF16)
                z, sp, lk, before = _sb_terms(qh[h], kk, m_idx, kb, tq, tk, scale)
                a = (z - sp) + _split_dot(lk, later) + c
                w = jnp.where(before, jnp.exp(a), 0.0)
                dw = _dot(doh[h], vv, NT)
                e_scr[kb] = w * dw
                z_scr[kb] = z
                sp_scr[kb] = sp
                dv_ref[ks, :] += _dot(w.astype(BF16), doh[h], TN)
                return c + jnp.sum(lk, axis=-1, keepdims=True)

            lax.fori_loop(0, m_idx + 1, rl, jnp.zeros((tq, 1), F32))

            def lr(kb, carry, h=h):
                esum, dqa = carry
                ks = pl.ds(pl.multiple_of(kb * tk, tk), tk)
                kk = k_ref[ks, :].astype(BF16)
                e, z, sp = e_scr[kb], z_scr[kb], sp_scr[kb]
                tpos = m_idx * tq + lax.broadcasted_iota(jnp.int32, (tq, tk), 0)
                spos = kb * tk + lax.broadcasted_iota(jnp.int32, (tq, tk), 1)
                prev = _split_dot(e, earlier) + esum
                dz = jnp.where(spos < tpos, e * jnp.exp(-sp) - jnp.exp(z - sp) * prev, 0.0)
                dzb = (dz * scale).astype(BF16)
                dk_ref[ks, :] += _dot(dzb, qh[h], TN)
                return esum + jnp.sum(e, axis=-1, keepdims=True), dqa + _dot(dzb, kk, NN)

            _, dqh = lax.fori_loop(0, m_idx + 1, lr, (jnp.zeros((tq, 1), F32), jnp.zeros((tq, LANES), F32)))
            dq.append(dqh)
        dq_ref[...] = jnp.where(h0, dq[0], dq[1])

    full = lambda col: pl.BlockSpec((T, LANES), col)
    blk = lambda col: pl.BlockSpec((tq, LANES), col)
    return pl.pallas_call(
        body, name=name, grid=(npair, T // tq),
        in_specs=[blk(lambda p, m: (m, col0 + p)),
                  full(lambda p, m: (0, col0 + npair + p)), full(lambda p, m: (0, col0 + 2 * npair + p)),
                  blk(lambda p, m: (m, do_col0 + p)), pl.BlockSpec((8, LANES), lambda p, m: (0, 0))],
        out_specs=[blk(lambda p, m: (m, p)), full(lambda p, m: (0, p)), full(lambda p, m: (0, p))],
        out_shape=[jax.ShapeDtypeStruct((T, npair * LANES), F32)] * 3,
        scratch_shapes=[pltpu.VMEM((T // tk, tq, tk), F32)] * 3,
        compiler_params=_params(("arbitrary", "arbitrary")),
    )(qkv, qkv, qkv, do, dep)


def _band_valid(m_idx):
    qi = lax.broadcasted_iota(jnp.int32, (BAND_TQ, BAND_W), 0)
    j = lax.broadcasted_iota(jnp.int32, (BAND_TQ, BAND_W), 1)
    cq = (m_idx * BAND_TQ + qi) >> 6
    ckp = (m_idx * BAND_TQ + j) >> 6
    return (ckp >= LEFT_CHUNKS) & (ckp >= cq) & (ckp <= cq + LEFT_CHUNKS)


def _band_probs(qh, kw, bias, valid, scale):
    s = _dot(qh, kw, NT) * scale + bias
    s = jnp.where(valid, s, NEG)
    e = jnp.exp(s - jnp.max(s, axis=-1, keepdims=True))
    return e / jnp.sum(e, axis=-1, keepdims=True)


def _band_fwd(qkv, k_pad, v_pad, bias_w, name="band_fwd"):
    T = qkv.shape[0]
    npair = C_HEADS // 2
    scale = C_DIM ** -0.5

    def body(q_ref, k_ref, v_ref, b_ref, o_ref):
        m_idx = pl.program_id(1)
        win = pl.ds(pl.multiple_of(m_idx * BAND_TQ, BAND_TQ), BAND_W)
        kw, vw = k_ref[win, :], v_ref[win, :]
        qv = q_ref[...]
        lane = lax.broadcasted_iota(jnp.int32, (BAND_TQ, LANES), 1)
        h0 = lane < 64
        qh = (jnp.where(h0, qv, 0).astype(BF16), jnp.where(h0, 0, qv).astype(BF16))
        valid = _band_valid(m_idx)
        o = [_dot(_band_probs(qh[h], kw, b_ref[h], valid, scale).astype(BF16), vw, NN) for h in range(2)]
        o_ref[...] = jnp.where(h0, o[0], o[1]).astype(o_ref.dtype)

    Tp = T + PAD_KEYS
    return pl.pallas_call(
        body, name=name, grid=(npair, T // BAND_TQ),
        in_specs=[pl.BlockSpec((BAND_TQ, LANES), lambda p, m: (m, p)),
                  pl.BlockSpec((Tp, LANES), lambda p, m: (0, p)),
                  pl.BlockSpec((Tp, LANES), lambda p, m: (0, p)),
                  pl.BlockSpec((2, BAND_TQ, BAND_W), lambda p, m: (p, 0, 0))],
        out_specs=pl.BlockSpec((BAND_TQ, LANES), lambda p, m: (m, p)),
        out_shape=jax.ShapeDtypeStruct((T, npair * LANES), BF16),
        compiler_params=_params(("parallel", "arbitrary")),
    )(qkv, k_pad, v_pad, bias_w)


def _band_bwd(qkv, k_pad, v_pad, bias_w, do, name="band_bwd"):
    T = qkv.shape[0]
    npair = C_HEADS // 2
    scale = C_DIM ** -0.5

    def body(q_ref, k_ref, v_ref, b_ref, do_ref, dq_ref, dk_ref, dv_ref, db_ref):
        m_idx = pl.program_id(1)

        @pl.when(m_idx == 0)
        def _():
            dk_ref[...] = jnp.zeros_like(dk_ref)
            dv_ref[...] = jnp.zeros_like(dv_ref)
            db_ref[...] = jnp.zeros_like(db_ref)

        win = pl.ds(pl.multiple_of(m_idx * BAND_TQ, BAND_TQ), BAND_W)
        kw, vw = k_ref[win, :], v_ref[win, :]
        qv = q_ref[...]
        dov = do_ref[...].astype(F32)
        lane = lax.broadcasted_iota(jnp.int32, (BAND_TQ, LANES), 1)
        h0 = lane < 64
        qh = (jnp.where(h0, qv, 0).astype(BF16), jnp.where(h0, 0, qv).astype(BF16))
        doh = (jnp.where(h0, dov, 0.0).astype(BF16), jnp.where(h0, 0.0, dov).astype(BF16))
        valid = _band_valid(m_idx)
        dq = []
        dkw = jnp.zeros((BAND_W, LANES), F32)
        dvw = jnp.zeros((BAND_W, LANES), F32)
        for h in range(2):
            p = _band_probs(qh[h], kw, b_ref[h], valid, scale)
            dp = _dot(doh[h], vw, NT)
            dsb = p * (dp - jnp.sum(p * dp, axis=-1, keepdims=True))
            db_ref[h] += dsb
            dsq = (dsb * scale).astype(BF16)
            dq.append(_dot(dsq, kw, NN))
            dkw = dkw + _dot(dsq, qh[h], TN)
            dvw = dvw + _dot(p.astype(BF16), doh[h], TN)
        dq_ref[...] = jnp.where(h0, dq[0], dq[1])
        dk_ref[win, :] += dkw
        dv_ref[win, :] += dvw

    Tp = T + PAD_KEYS
    blk = lambda col: pl.BlockSpec((BAND_TQ, LANES), col)
    full = pl.BlockSpec((Tp, LANES), lambda p, m: (0, p))
    bias = pl.BlockSpec((2, BAND_TQ, BAND_W), lambda p, m: (p, 0, 0))
    return pl.pallas_call(
        body, name=name, grid=(npair, T // BAND_TQ),
        in_specs=[blk(lambda p, m: (m, p)), full, full, bias, blk(lambda p, m: (m, p))],
        out_specs=[blk(lambda p, m: (m, p)), full, full, bias],
        out_shape=[jax.ShapeDtypeStruct((T, npair * LANES), F32),
                   jax.ShapeDtypeStruct((Tp, npair * LANES), F32),
                   jax.ShapeDtypeStruct((Tp, npair * LANES), F32),
                   jax.ShapeDtypeStruct((C_HEADS, BAND_TQ, BAND_W), F32)],
        compiler_params=_params(("arbitrary", "arbitrary")),
    )(qkv, k_pad, v_pad, bias_w, do)


def _skew_bits(x, left):
    w = x.shape[1]
    row = lax.broadcasted_iota(jnp.int32, x.shape, 0)
    for b in range(BAND_TQ.bit_length() - 1):
        amt = (w - (1 << b)) if left else (1 << b)
        x = jnp.where((row >> b) & 1 == 1, pltpu.roll(x, amt, 1), x)
    return x


def _toeplitz(diag, name="toeplitz"):
    H = diag.shape[0]

    def body(d_ref, o_ref):
        x = jnp.broadcast_to(d_ref[0], (BAND_TQ, TOEP_W))
        o_ref[0] = _skew_bits(x, left=False)[:, BAND_TQ:]

    return pl.pallas_call(
        body, name=name, grid=(H,),
        in_specs=[pl.BlockSpec((1, 1, TOEP_W), lambda h: (h, 0, 0))],
        out_specs=pl.BlockSpec((1, BAND_TQ, BAND_W), lambda h: (h, 0, 0)),
        out_shape=jax.ShapeDtypeStruct((H, BAND_TQ, BAND_W), F32),
        compiler_params=_params(("parallel",)),
    )(diag.reshape(H, 1, TOEP_W))


def _toeplitz_bwd(dbias, name="toeplitz_bwd"):
    H = dbias.shape[0]

    def body(d_ref, o_ref):
        x = jnp.concatenate([jnp.zeros((BAND_TQ, BAND_TQ), F32), d_ref[0]], axis=1)
        o_ref[0] = jnp.sum(_skew_bits(x, left=True), axis=0, keepdims=True)

    return pl.pallas_call(
        body, name=name, grid=(H,),
        in_specs=[pl.BlockSpec((1, BAND_TQ, BAND_W), lambda h: (h, 0, 0))],
        out_specs=pl.BlockSpec((1, 1, TOEP_W), lambda h: (h, 0, 0)),
        out_shape=jax.ShapeDtypeStruct((H, 1, TOEP_W), F32),
        compiler_params=_params(("parallel",)),
    )(dbias).reshape(H, TOEP_W)


_HBM = pl.BlockSpec(memory_space=pltpu.HBM)
_SEM = pl.BlockSpec(memory_space=pltpu.SEMAPHORE)
_EFFECT = pltpu.SideEffectType.DATAFLOW_SIDE_EFFECTING


def _peers():
    x, y, c = lax.axis_index("x"), lax.axis_index("y"), lax.axis_index("c")
    out = []
    for k in range(1, N_DEV):
        peer = (1 - x if (k >> 2) & 1 else x, 1 - y if (k >> 1) & 1 else y, 1 - c if k & 1 else c)
        out.append((peer, 4 * peer[0] + 2 * peer[1] + peer[2]))
    return 4 * x + 2 * y + c, out


def _split_copies(ins, lands, scatter, send_sem, recv_sem, arriving):
    me, peers = _peers()
    out = []
    for a in range(len(ins)):
        for peer, idx in peers:
            out.append(pltpu.make_async_remote_copy(
                src_ref=ins[a].at[idx] if scatter[a] else ins[a],
                dst_ref=lands[a].at[idx if arriving else me], send_sem=send_sem, recv_sem=recv_sem,
                device_id=peer, device_id_type=pl.DeviceIdType.MESH))
    return out


def _landing_zones(arrays, scatter):
    return [lax.empty((N_DEV,) + (a.shape[1:] if s else a.shape), a.dtype) for a, s in zip(arrays, scatter)]


def _place_own(arrays, name):
    n = len(arrays)
    lands = _landing_zones(arrays, [False] * n)

    def body(*refs):
        ins, lnd, sems = refs[:n], refs[2 * n:3 * n], refs[-1]
        me, _ = _peers()
        own = [pltpu.make_async_copy(ins[a], lnd[a].at[me], sems.at[a]) for a in range(n)]
        for cp in own:
            cp.start()
        for cp in own:
            cp.wait()

    any_spec = pl.BlockSpec(memory_space=pl.ANY)
    return pl.pallas_call(
        body, name=name,
        out_shape=[jax.ShapeDtypeStruct(l.shape, l.dtype) for l in lands],
        in_specs=[any_spec] * (2 * n), out_specs=[any_spec] * n,
        input_output_aliases={n + i: i for i in range(n)},
        scratch_shapes=[pltpu.SemaphoreType.DMA((n,))],
        compiler_params=pltpu.CompilerParams(has_side_effects=True),
    )(*arrays, *lands)


def _exchange_start(arrays, scatter, after, name, lands=None):
    n = len(arrays)
    copy_own = lands is None
    if copy_own:
        lands = _landing_zones(arrays, scatter)

    def body(*refs):
        ins, lnd = refs[:n], refs[n:2 * n]
        send_sem, recv_sem = refs[2 * n + 1:2 * n + 3]
        token, own_sems = refs[-2], refs[-1]
        if copy_own:
            me, _ = _peers()
            own = [pltpu.make_async_copy(ins[a].at[me] if scatter[a] else ins[a], lnd[a].at[me], own_sems.at[a])
                   for a in range(n)]
            for cp in own:
                cp.start()
            for cp in own:
                cp.wait()
        for cp in _split_copies(ins, lnd, scatter, send_sem, recv_sem, arriving=False):
            cp.start()
        token[...] = jnp.zeros_like(token)

    hbm = lambda a: pltpu.HBM(a.shape, a.dtype)
    out = pl.pallas_call(
        body, name=name,
        out_shape=(pltpu.SemaphoreType.DMA(()), pltpu.SemaphoreType.DMA(()),
                   *[hbm(a) for a in arrays], *[hbm(a) for a in lands],
                   jax.ShapeDtypeStruct((8, LANES), F32)),
        in_specs=[_HBM] * (2 * n) + [pl.BlockSpec(memory_space=pl.ANY)],
        out_specs=(_SEM, _SEM, *([_HBM] * (2 * n)), pl.BlockSpec(memory_space=pltpu.VMEM)),
        input_output_aliases={i: 2 + i for i in range(2 * n)},
        scratch_shapes=[pltpu.SemaphoreType.DMA((n,))],
        compiler_params=pltpu.CompilerParams(has_side_effects=_EFFECT),
    )(*[pltpu.with_memory_space_constraint(a, pltpu.HBM) for a in list(arrays) + lands], after)
    return (out[0], out[1], list(out[2:2 + n]), list(out[2 + n:2 + 2 * n]), tuple(scatter)), out[-1]


def _exchange_wait(handle, after, name):
    send_sem, recv_sem, ins, lands, scatter = handle
    n = len(ins)

    def body(*refs):
        i_ref, l_ref = refs[:n], refs[n:2 * n]
        s_sem, r_sem = refs[2 * n:2 * n + 2]
        for cp in _split_copies(i_ref, l_ref, scatter, s_sem, r_sem, arriving=False):
            cp.wait_send()
        for cp in _split_copies(i_ref, l_ref, scatter, s_sem, r_sem, arriving=True):
            cp.wait_recv()

    hbm = lambda a: pltpu.HBM(a.shape, a.dtype)
    out = pl.pallas_call(
        body, name=name,
        out_shape=tuple(hbm(a) for a in ins + lands),
        in_specs=[_HBM] * (2 * n) + [_SEM, _SEM, pl.BlockSpec(memory_space=pl.ANY)],
        out_specs=tuple([_HBM] * (2 * n)),
        input_output_aliases={i: i for i in range(2 * n)},
        compiler_params=pltpu.CompilerParams(has_side_effects=_EFFECT),
    )(*ins, *lands, send_sem, recv_sem, after)
    return list(out[n:])


def _adamw(w, parts, m, v, name="adamw"):
    R, C = w.shape
    tr = next(t for t in (256, 128, 64, 32, 16, 8) if R % t == 0)
    c1 = 1.0 - ADAM_B1 ** ADAM_STEP
    c2 = 1.0 - ADAM_B2 ** ADAM_STEP

    def body(w_ref, p_ref, m_ref, v_ref, g_ref, d_ref, nm_ref, nv_ref):
        g = p_ref[0].astype(F32)
        for i in range(1, N_DEV):
            g = g + p_ref[i].astype(F32)
        nm = ADAM_B1 * m_ref[...] + (1.0 - ADAM_B1) * g
        nv = ADAM_B2 * v_ref[...] + (1.0 - ADAM_B2) * (g * g)
        g_ref[...] = g
        nm_ref[...] = nm
        nv_ref[...] = nv
        d_ref[...] = -ADAM_LR * ((nm / c1) / (jnp.sqrt(nv / c2) + ADAM_EPS) + ADAM_WD * w_ref[...])

    blk = pl.BlockSpec((tr, C), lambda i: (i, 0))
    return pl.pallas_call(
        body, name=name, grid=(R // tr,),
        in_specs=[blk, pl.BlockSpec((N_DEV, tr, C), lambda i: (0, i, 0)), blk, blk],
        out_specs=[blk] * 4,
        out_shape=[jax.ShapeDtypeStruct((R, C), F32)] * 4,
        compiler_params=_params(("parallel",)),
    )(w, parts, m, v)


_O1 = Q_LORA
_O2 = _O1 + KV_LORA
_O3 = _O2 + MLA_ROPE
_NB = SB_HEADS * SB_DIM
IN_W = _O2 + LANES + 3 * _NB
COL_KR = _O2 // LANES
COL_SB = COL_KR + 1


def _w_in_local(w):
    kr = w[_O2:_O3]
    pad = jnp.zeros((LANES - 2 * MLA_ROPE, w.shape[1]), w.dtype)
    return jnp.concatenate([w[:_O2], kr, kr, pad, w[_O3:]], axis=0)


def _w_in_grad(g):
    kr = (g[_O2:_O2 + MLA_ROPE].astype(F32) + g[_O2 + MLA_ROPE:_O2 + 2 * MLA_ROPE].astype(F32)).astype(g.dtype)
    return jnp.concatenate([g[:_O2], kr, g[_O2 + LANES:]], axis=0)


def _w_uq_local(w):
    w3 = w.reshape(MLA_HEADS // 2, 2, MLA_NOPE + MLA_ROPE, w.shape[1])
    nope = w3[:, :, :MLA_NOPE].reshape(MLA_HEADS // 2, 2 * MLA_NOPE, w.shape[1])
    rope = w3[:, :, MLA_NOPE:].reshape(MLA_HEADS // 2, 2 * MLA_ROPE, w.shape[1])
    pad = jnp.zeros((MLA_HEADS // 2, LANES - 2 * MLA_ROPE, w.shape[1]), w.dtype)
    return jnp.concatenate([nope, rope, pad], axis=1).reshape(-1, w.shape[1])


def _w_uq_grad(g):
    g3 = g.reshape(MLA_HEADS // 2, 2 * LANES, g.shape[1])
    nope = g3[:, :2 * MLA_NOPE].reshape(MLA_HEADS // 2, 2, MLA_NOPE, g.shape[1])
    rope = g3[:, LANES:LANES + 2 * MLA_ROPE].reshape(MLA_HEADS // 2, 2, MLA_ROPE, g.shape[1])
    return jnp.concatenate([nope, rope], axis=2).reshape(-1, g.shape[1])


def _w_ukv_local(w):
    w3 = w.reshape(MLA_HEADS, MLA_NOPE + MLA_V, w.shape[1])
    return jnp.concatenate([w3[:, :MLA_NOPE].reshape(-1, w.shape[1]),
                            w3[:, MLA_NOPE:].reshape(-1, w.shape[1])], axis=0)


def _w_ukv_grad(g):
    half = MLA_HEADS * MLA_NOPE
    kn = g[:half].reshape(MLA_HEADS, MLA_NOPE, g.shape[1])
    vv = g[half:].reshape(MLA_HEADS, MLA_V, g.shape[1])
    return jnp.concatenate([kn, vv], axis=1).reshape(-1, g.shape[1])


def _rope_tables(T):
    pos = jnp.arange(T, dtype=F32)
    inv_freq = ROPE_THETA ** (-jnp.arange(0, MLA_ROPE, 2, dtype=F32) / MLA_ROPE)
    ang = pos[:, None] * inv_freq[None, :]
    cos, sin = jnp.cos(ang), jnp.sin(ang)
    ones = jnp.ones((T, LANES - 2 * MLA_ROPE), F32)
    cos_k = jnp.concatenate([cos, cos, cos, cos, ones], axis=1)
    sin_k = jnp.concatenate([-sin, sin, -sin, sin, 0.0 * ones], axis=1)
    cos_q = jnp.concatenate([jnp.ones((T, LANES), F32), cos_k], axis=1)
    sin_q = jnp.concatenate([jnp.zeros((T, LANES), F32), sin_k], axis=1)
    return cos_q, sin_q, cos_k, sin_k


def _bias_diag_index():
    ell = np.arange(TOEP_W)
    return np.clip(BAND_W - ell, -REL_CLIP, REL_CLIP) + REL_CLIP


def _local_step(x, target, small, get_weights, put_grads):
    T = x.shape[0]
    cos_q, sin_q, cos_k, sin_k = _rope_tables(T)
    G = {}
    W = dict(small)

    u0 = _rms_fwd(x, W["g_mix"][0:1], name="rms_mix0")
    W.update(get_weights("in0", u0))
    proj = _mm(u0, W["w_in_t"], dims="nt", name="proj_in")
    W.update(get_weights("mix0", proj))
    c_q, c_kv = proj[:, :_O1], proj[:, _O1:_O2]
    nq = _rms_fwd(c_q, W["g_cq"], name="rms_cq")
    nkv = _rms_fwd(c_kv, W["g_ckv"], name="rms_ckv")
    qa_raw = _mm(nq, W["w_uq_t"], dims="nt", name="proj_uq")
    qa = _rope(qa_raw, cos_q, sin_q, 0, qa_raw.shape[1] // LANES, BF16, name="rope_q")
    kv = _mm(nkv, W["w_ukv_t"], dims="nt", out_dtype=BF16, name="proj_ukv")
    kr = _rope(proj, cos_k, sin_k, COL_KR, 1, BF16, name="rope_k")
    o_a, lse = _mla_fwd(qa, kv, kr)
    o_b = _sb_fwd(proj, COL_SB)
    o_ab = jnp.concatenate([o_a, o_b], axis=1)
    h1 = _mm(o_ab, W["ev_w_out"], res=x, name="out_ev")

    def ffn_fwd(h, layer):
        W.update(get_weights(f"ffn{layer}", h))
        return _ffn_fwd(h, W["g_ffn"][layer:layer + 1], W[f"w_gate_t{layer}"], W[f"w_up_t{layer}"],
                        W[f"w_down{layer}"], name=f"ffn_fwd{layer}")

    h2, u1, a0, b0 = ffn_fwd(h1, 0)

    W.update(get_weights("mix1", h2))
    u2 = _rms_fwd(h2, W["g_mix"][1:2], name="rms_mix1")
    qkv = _mm(u2, W["od_w_qkv_t"], dims="nt", out_dtype=BF16, name="proj_qkv")
    nc = C_HEADS * C_DIM
    pad = ((PAD_KEYS, 0), (0, 0))
    k_pad, v_pad = jnp.pad(qkv[:, nc:2 * nc], pad), jnp.pad(qkv[:, 2 * nc:], pad)
    diag_idx = _bias_diag_index()
    bias_w = _toeplitz(W["od_rel_bias"][:, diag_idx])
    o_c = _band_fwd(qkv, k_pad, v_pad, bias_w)
    h3 = _mm(o_c, W["od_w_out"], res=h2, name="out_od")
    h4, u3, a1, b1 = ffn_fwd(h3, 1)

    loss, dh, dhb, G["g_final"] = _loss_head(h4, W["g_final"], target)

    def ffn_bwd(dh, dhb, h, u, a, b, layer):
        du, g_gate, g_up, g_down = _ffn_bwd(dhb, u, a, b, W[f"w_gate_t{layer}"], W[f"w_up_t{layer}"],
                                            W[f"w_down{layer}"], name=f"ffn_bwd{layer}")
        tok = put_grads(f"ffn{layer}", {"w_gate_t": g_gate, "w_up_t": g_up, "w_down": g_down})
        return _rms_bwd(h, W["g_ffn"][layer:layer + 1] + tok[:1, :1], du, dres=dh, name=f"rms_ffn_bwd{layer}")

    dh3, dh3b, g_gffn1 = ffn_bwd(dh, dhb, h3, u3, a1, b1, 1)

    do_c = _mm(dh3b, W["od_w_out"], dims="nt", name="out_od_dx")
    g_od_out = _mm(o_c, dh3b, dims="tn", out_dtype=BF16, name="out_od_dw")
    dq_c, dk_p, dv_p, dbias_w = _band_bwd(qkv, k_pad, v_pad, bias_w, do_c)
    dqkv = jnp.concatenate([dq_c, dk_p[PAD_KEYS:], dv_p[PAD_KEYS:]], axis=1)
    du2 = _mm(dqkv, W["od_w_qkv_t"], name="proj_qkv_dx")
    tok = put_grads("mix1", {"od_w_qkv_t": _mm(dqkv, u2, dims="tn", out_dtype=BF16, name="proj_qkv_dw"),
                             "od_w_out": g_od_out})
    ddiag = _toeplitz_bwd(dbias_w)
    n_far = BAND_W - REL_CLIP + 1
    G["od_rel_bias"] = jnp.concatenate(
        [jnp.zeros((C_HEADS, REL_CLIP - BAND_TQ + 1), F32), ddiag[:, n_far:][:, ::-1],
         jnp.sum(ddiag[:, :n_far], axis=1, keepdims=True)], axis=1)
    dh2, dh2b, g_gmix1 = _rms_bwd(h2, W["g_mix"][1:2] + tok[:1, :1], du2, dres=dh3, name="rms_mix_bwd1")

    dh1, dh1b, g_gffn0 = ffn_bwd(dh2, dh2b, h1, u1, a0, b0, 0)
    G["g_ffn"] = jnp.concatenate([g_gffn0, g_gffn1], axis=0)

    do_ab = _mm(dh1b, W["ev_w_out"], dims="nt", name="out_ev_dx")
    g0 = {"ev_w_out": _mm(o_ab, dh1b, dims="tn", out_dtype=BF16, name="out_ev_dw")}
    dqa, dkn, dva, dkr = _mla_bwd(qa, kv, kr, o_a, lse, do_ab, 0)
    dqa_raw = _rope(dqa, cos_q, -sin_q, 0, dqa.shape[1] // LANES, F32, name="rope_q_bwd")
    g0["w_uq_t"] = _mm(dqa_raw, nq, dims="tn", name="proj_uq_dw")
    dnq = _mm(dqa_raw, W["w_uq_t"], name="proj_uq_dx")
    dc_q, _, G["g_cq"] = _rms_bwd(c_q, W["g_cq"], dnq, name="rms_cq_bwd")
    dkv = jnp.concatenate([dkn, dva], axis=1)
    g0["w_ukv_t"] = _mm(dkv, nkv, dims="tn", name="proj_ukv_dw")
    dnkv = _mm(dkv, W["w_ukv_t"], name="proj_ukv_dx")
    dc_kv, _, G["g_ckv"] = _rms_bwd(c_kv, W["g_ckv"], dnkv, name="rms_ckv_bwd")
    tok = put_grads("mix0", g0)
    dqb, dkb, dvb = _sb_bwd(proj, COL_SB, do_ab, MLA_HEADS // 2, tok)
    dkr_raw = _rope(dkr, cos_k, -sin_k, 0, 1, F32, name="rope_k_bwd")
    dproj = jnp.concatenate([dc_q, dc_kv, dkr_raw, dqb, dkb, dvb], axis=1)
    du0 = _mm(dproj, W["w_in_t"], name="proj_in_dx")
    tok = put_grads("in0", {"w_in_t": _mm(dproj, u0, dims="tn", name="proj_in_dw")})
    dx, _, g_gmix0 = _rms_bwd(x, W["g_mix"][0:1] + tok[:1, :1], du0, dres=dh1, name="rms_mix_bwd0")
    G["g_mix"] = jnp.concatenate([g_gmix0, g_gmix1], axis=0)
    return loss[0, 0], dx, G


_BIG = ["ev_w_in", "ev_w_uq", "ev_w_ukv", "ev_w_out", "od_w_qkv", "od_w_out", "w_gate", "w_up", "w_down"]
_COL_SHARDED = {"ev_w_in", "ev_w_uq", "ev_w_ukv", "od_w_qkv", "w_gate", "w_up"}
_SMALL = ["ev_g_cq", "ev_g_ckv", "od_rel_bias", "g_mix", "g_ffn", "g_final"]
_GROUPS = {
    "in0": ["ev_w_in"],
    "mix0": ["ev_w_uq", "ev_w_ukv", "ev_w_out"],
    "ffn0": ["w_gate0", "w_up0", "w_down0"],
    "mix1": ["od_w_qkv", "od_w_out"],
    "ffn1": ["w_gate1", "w_up1", "w_down1"],
}
_GROUP_SRC = {n + str(l): (n, l) for n in ("w_gate", "w_up", "w_down") for l in (0, 1)}
_SMALL_ROWS = 8
_SMALL_COLS = 1792


def _pack_small(vals):
    flat = jnp.concatenate([v.reshape(-1).astype(F32) for v in vals])
    flat = jnp.pad(flat, (0, _SMALL_ROWS * _SMALL_COLS - flat.shape[0]))
    return flat.reshape(_SMALL_ROWS, _SMALL_COLS)


def _unpack_small(packed, like):
    flat = packed.reshape(-1)
    out, off = [], 0
    for v in like:
        out.append(flat[off:off + v.size].reshape(v.shape))
        off += v.size
    return out


def kernel(x, ev_w_in, ev_g_cq, ev_w_uq, ev_g_ckv, ev_w_ukv, ev_w_out, od_w_qkv, od_rel_bias, od_w_out, g_mix, g_ffn, w_gate, w_up, w_down, g_final, loss_target, m_ev_w_in, m_ev_g_cq, m_ev_w_uq, m_ev_g_ckv, m_ev_w_ukv, m_ev_w_out, m_od_w_qkv, m_od_rel_bias, m_od_w_out, m_g_mix, m_g_ffn, m_w_gate, m_w_up, m_w_down, m_g_final, v_ev_w_in, v_ev_g_cq, v_ev_w_uq, v_ev_g_ckv, v_ev_w_ukv, v_ev_w_out, v_od_w_qkv, v_od_rel_bias, v_od_w_out, v_g_mix, v_g_ffn, v_w_gate, v_w_up, v_w_down, v_g_final):
    args = dict(locals())
    w = {n: args[n] for n in _BIG + _SMALL}
    mom = {n: args["m_" + n] for n in _BIG + _SMALL}
    var = {n: args["v_" + n] for n in _BIG + _SMALL}

    own = {}
    for grp, names in _GROUPS.items():
        for n in names:
            base, layer = _GROUP_SRC.get(n, (n, 0))
            shard = w[base][layer:layer + 1]
            own[n] = (jnp.swapaxes(shard, 1, 2) if base in _COL_SHARDED else shard).astype(BF16)
    placed = dict(zip(own, _place_own(list(own.values()), name="place_own_shards")))
    gather, token = {}, x[0, :8, :LANES]
    for grp, names in _GROUPS.items():
        gather[grp], token = _exchange_start([own[n] for n in names], [False] * len(names), token,
                                             name="gather_start_" + grp, lands=[placed[n] for n in names])

    def get_weights(grp, after):
        names = _GROUPS[grp]
        lands = _exchange_wait(gather[grp], token if after is None else after, name="gather_wait_" + grp)
        full = {n: l.reshape(-1, l.shape[-1]) for n, l in zip(names, lands)}
        if grp == "in0":
            return {"w_in_t": _w_in_local(full["ev_w_in"])}
        if grp == "mix0":
            return {"w_uq_t": _w_uq_local(full["ev_w_uq"]), "w_ukv_t": _w_ukv_local(full["ev_w_ukv"]),
                    "ev_w_out": full["ev_w_out"]}
        if grp == "mix1":
            return {"od_w_qkv_t": full["od_w_qkv"], "od_w_out": full["od_w_out"]}
        layer = grp[-1]
        return {"w_gate_t" + layer: full["w_gate" + layer], "w_up_t" + layer: full["w_up" + layer],
                "w_down" + layer: full["w_down" + layer]}

    scatter = {}

    def put_grads(grp, g):
        if grp == "in0":
            g = {"ev_w_in": _w_in_grad(g["w_in_t"])}
        elif grp == "mix0":
            g = {"ev_w_uq": _w_uq_grad(g["w_uq_t"]), "ev_w_ukv": _w_ukv_grad(g["w_ukv_t"]),
                 "ev_w_out": g["ev_w_out"]}
        elif grp == "mix1":
            g = {"od_w_qkv": g["od_w_qkv_t"], "od_w_out": g["od_w_out"]}
        else:
            layer = grp[-1]
            g = {"w_gate" + layer: g["w_gate_t"], "w_up" + layer: g["w_up_t"], "w_down" + layer: g["w_down"]}
        names = _GROUPS[grp]
        send = [g[n].reshape(N_DEV, 1, g[n].shape[0] // N_DEV, g[n].shape[1]).astype(BF16) for n in names]
        handle, tok = _exchange_start(send, [True] * len(names), send[0], name="scatter_start_" + grp)
        scatter[grp] = handle
        return tok

    small = {"g_cq": ev_g_cq, "g_ckv": ev_g_ckv, "od_rel_bias": od_rel_bias[0],
             "g_mix": g_mix + token[0, 0], "g_ffn": g_ffn, "g_final": g_final.reshape(1, -1)}
    loss_part, dx, G = _local_step(x[0], loss_target[0], small, get_weights, put_grads)
    loss = lax.psum(loss_part, ("x", "y", "c"))
    g_small = _pack_small([G["g_cq"], G["g_ckv"], G["od_rel_bias"], G["g_mix"], G["g_ffn"], G["g_final"]])
    small_handle, _ = _exchange_start([g_small], [False], dx, name="gather_start_small")

    grads, deltas, new_m, new_v = {}, {}, {}, {}
    parts, after = {}, dx

    def wait_parts(grp, after):
        lands = _exchange_wait(scatter[grp], after, name="scatter_wait_" + grp)
        for n, l in zip(_GROUPS[grp], lands):
            parts[n] = jnp.swapaxes(l, 2, 3) if _GROUP_SRC.get(n, (n, 0))[0] in _COL_SHARDED else l
        return lands[0]

    def adamw(n):
        shp = w[n].shape
        r2 = (-1, shp[-1])
        res = _adamw(w[n].reshape(r2), parts[n].reshape((N_DEV,) + w[n].reshape(r2).shape),
                     mom[n].reshape(r2), var[n].reshape(r2), name="adamw_" + n)
        grads[n], deltas[n], new_m[n], new_v[n] = [r.reshape(shp) for r in res]
        return res[0]

    for grp in ("ffn1", "mix1", "ffn0", "mix0"):
        after = wait_parts(grp, after)
    for n in ("w_gate", "w_up", "w_down"):
        parts[n] = jnp.concatenate([parts[n + "0"], parts[n + "1"]], axis=1)
    for n in _BIG[1:]:
        after = adamw(n)
    after = wait_parts("in0", after)
    after = adamw("ev_w_in")
    small_w = [w[n] for n in _SMALL]
    small_parts = _exchange_wait(small_handle, after, name="gather_wait_small")[0]
    res = _adamw(_pack_small(small_w), small_parts, _pack_small([mom[n] for n in _SMALL]),
                 _pack_small([var[n] for n in _SMALL]), name="adamw_small")
    for d, packed in zip((grads, deltas, new_m, new_v), res):
        for n, val in zip(_SMALL, _unpack_small(packed, small_w)):
            d[n] = val

    order = ["ev_w_in", "ev_g_cq", "ev_w_uq", "ev_g_ckv", "ev_w_ukv", "ev_w_out", "od_w_qkv", "od_rel_bias",
             "od_w_out", "g_mix", "g_ffn", "w_gate", "w_up", "w_down", "g_final"]
    out = [loss, dx[None]]
    for d in (grads, deltas, new_m, new_v):
        out += [d[n] for n in order]
    return tuple(out)
```

```python
import functools

import numpy as np
import jax
import jax.numpy as jnp
from jax import lax
from jax.experimental import pallas as pl
from jax.experimental.pallas import tpu as pltpu

F32 = jnp.float32
BF16 = jnp.bfloat16

D_MODEL = 1024
CHUNK = 64
MLA_HEADS = 8
MLA_NOPE = 64
MLA_ROPE = 32
MLA_V = 64
Q_LORA = 384
KV_LORA = 256
ROPE_THETA = 10000.0
SB_HEADS = 8
SB_DIM = 64
C_HEADS = 16
C_DIM = 64
LEFT_CHUNKS = 8
REL_CLIP = 256
D_FF = 2816
RMS_EPS = 1e-6
ADAM_LR = 0.001
ADAM_B1 = 0.9
ADAM_B2 = 0.999
ADAM_EPS = 1e-08
ADAM_WD = 0.01
ADAM_STEP = 10

N_DEV = 8
LANES = 128
VMEM_LIMIT = 56 * 1024 * 1024
NEG = -1e30
PAD_KEYS = LEFT_CHUNKS * CHUNK
BAND_TQ = 128
BAND_W = BAND_TQ + PAD_KEYS
TOEP_W = BAND_W + BAND_TQ

NN = (((1,), (0,)), ((), ()))
NT = (((1,), (1,)), ((), ()))
TN = (((0,), (0,)), ((), ()))


def _dot(a, b, dn):
    return lax.dot_general(a, b, dn, preferred_element_type=F32)


def _pick(dim, pref):
    if dim <= pref:
        return dim
    best = None
    for t in range(LANES, pref + 1, LANES):
        if dim % t == 0:
            best = t
    assert best is not None, (dim, pref)
    return best


def _params(sem):
    return pltpu.CompilerParams(dimension_semantics=sem, vmem_limit_bytes=VMEM_LIMIT)


def _mm(a, b, dims="nn", res=None, out_dtype=F32, name="mm"):
    if dims == "nn":
        (M, K), (K2, N) = a.shape, b.shape
    elif dims == "nt":
        (M, K), (N, K2) = a.shape, b.shape
    else:
        (K, M), (K2, N) = a.shape, b.shape
    assert K == K2, (a.shape, b.shape, dims)
    tm, tn, tk = _pick(M, 512), _pick(N, 1408), _pick(K, 1408)
    nk = K // tk
    dn = {"nn": NN, "nt": NT, "tn": TN}[dims]
    has_res = res is not None

    def body(*refs):
        if has_res:
            a_ref, b_ref, r_ref, o_ref, acc = refs
        else:
            a_ref, b_ref, o_ref, acc = refs
        k = pl.program_id(2)

        @pl.when(k == 0)
        def _():
            acc[...] = jnp.zeros_like(acc)

        acc[...] += _dot(a_ref[...].astype(BF16), b_ref[...].astype(BF16), dn)

        @pl.when(k == nk - 1)
        def _():
            r = acc[...]
            if has_res:
                r = r + r_ref[...]
            o_ref[...] = r.astype(out_dtype)

    a_spec = (pl.BlockSpec((tk, tm), lambda i, j, k: (k, i)) if dims == "tn"
              else pl.BlockSpec((tm, tk), lambda i, j, k: (i, k)))
    b_spec = (pl.BlockSpec((tn, tk), lambda i, j, k: (j, k)) if dims == "nt"
              else pl.BlockSpec((tk, tn), lambda i, j, k: (k, j)))
    o_spec = pl.BlockSpec((tm, tn), lambda i, j, k: (i, j))
    in_specs = [a_spec, b_spec] + ([o_spec] if has_res else [])
    args = (a, b) + ((res,) if has_res else ())
    return pl.pallas_call(
        body, name=name, grid=(M // tm, N // tn, nk),
        in_specs=in_specs, out_specs=o_spec,
        out_shape=jax.ShapeDtypeStruct((M, N), out_dtype),
        scratch_shapes=[pltpu.VMEM((tm, tn), F32)],
        compiler_params=_params(("parallel", "parallel", "arbitrary")),
    )(*args)


def _rms_fwd(x, g, out_dtype=BF16, name="rms_fwd"):
    T, Fd = x.shape
    tm = _pick(T, 256)

    def body(x_ref, g_ref, o_ref):
        xv = x_ref[...]
        r = lax.rsqrt(jnp.mean(xv * xv, axis=-1, keepdims=True) + RMS_EPS)
        o_ref[...] = (xv * r * g_ref[...]).astype(out_dtype)

    return pl.pallas_call(
        body, name=name, grid=(T // tm,),
        in_specs=[pl.BlockSpec((tm, Fd), lambda i: (i, 0)), pl.BlockSpec((1, Fd), lambda i: (0, 0))],
        out_specs=pl.BlockSpec((tm, Fd), lambda i: (i, 0)),
        out_shape=jax.ShapeDtypeStruct((T, Fd), out_dtype),
        compiler_params=_params(("parallel",)),
    )(x, g)


def _rms_bwd(x, g, dy, dres=None, name="rms_bwd"):
    T, Fd = x.shape
    tm = _pick(T, 256)
    has_res = dres is not None

    def body(*refs):
        if has_res:
            x_ref, g_ref, dy_ref, r_ref, dx_ref, dxb_ref, dg_ref = refs
        else:
            x_ref, g_ref, dy_ref, dx_ref, dxb_ref, dg_ref = refs
        xv, dyv = x_ref[...], dy_ref[...]
        r = lax.rsqrt(jnp.mean(xv * xv, axis=-1, keepdims=True) + RMS_EPS)
        gdy = dyv * g_ref[...]
        dot = jnp.mean(xv * gdy, axis=-1, keepdims=True)
        dx = r * gdy - xv * (r * r * r * dot)
        if has_res:
            dx = dx + r_ref[...]
        dx_ref[...] = dx
        dxb_ref[...] = dx.astype(BF16)

        @pl.when(pl.program_id(0) == 0)
        def _():
            dg_ref[...] = jnp.zeros_like(dg_ref)

        dg_ref[...] += jnp.sum(dyv * xv * r, axis=0, keepdims=True)

    row = pl.BlockSpec((tm, Fd), lambda i: (i, 0))
    vec = pl.BlockSpec((1, Fd), lambda i: (0, 0))
    in_specs = [row, vec, row] + ([row] if has_res else [])
    args = (x, g, dy) + ((dres,) if has_res else ())
    return pl.pallas_call(
        body, name=name, grid=(T // tm,),
        in_specs=in_specs, out_specs=[row, row, vec],
        out_shape=[jax.ShapeDtypeStruct((T, Fd), F32), jax.ShapeDtypeStruct((T, Fd), BF16),
                   jax.ShapeDtypeStruct((1, Fd), F32)],
        compiler_params=_params(("arbitrary",)),
    )(*args)


def _loss_head(h, g, target, name="loss_head"):
    T, Fd = h.shape
    tm = _pick(T, 256)

    def body(h_ref, g_ref, t_ref, loss_ref, dh_ref, dhb_ref, dg_ref):
        xv = h_ref[...]
        r = lax.rsqrt(jnp.mean(xv * xv, axis=-1, keepdims=True) + RMS_EPS)
        diff = xv * r * g_ref[...] - t_ref[...]
        part = 0.5 * jnp.sum(jnp.mean(diff * diff, axis=-1, keepdims=True), axis=0, keepdims=True)
        dyv = diff * (1.0 / Fd)
        gdy = dyv * g_ref[...]
        dot = jnp.mean(xv * gdy, axis=-1, keepdims=True)
        dh = r * gdy - xv * (r * r * r * dot)
        dh_ref[...] = dh
        dhb_ref[...] = dh.astype(BF16)

        @pl.when(pl.program_id(0) == 0)
        def _():
            dg_ref[...] = jnp.zeros_like(dg_ref)
            loss_ref[...] = jnp.zeros_like(loss_ref)

        dg_ref[...] += jnp.sum(dyv * xv * r, axis=0, keepdims=True)
        loss_ref[...] += jnp.broadcast_to(part, loss_ref.shape)

    row = pl.BlockSpec((tm, Fd), lambda i: (i, 0))
    vec = pl.BlockSpec((1, Fd), lambda i: (0, 0))
    return pl.pallas_call(
        body, name=name, grid=(T // tm,),
        in_specs=[row, vec, row],
        out_specs=[pl.BlockSpec((1, LANES), lambda i: (0, 0)), row, row, vec],
        out_shape=[jax.ShapeDtypeStruct((1, LANES), F32), jax.ShapeDtypeStruct((T, Fd), F32),
                   jax.ShapeDtypeStruct((T, Fd), BF16), jax.ShapeDtypeStruct((1, Fd), F32)],
        compiler_params=_params(("arbitrary",)),
    )(h, g, target)


FFN_TF = 256


def _ffn_fwd(h, g, wg_t, wu_t, wd, name="ffn_fwd"):
    T, Dm = h.shape
    Fh = wd.shape[0]
    tm = _pick(T, 1024)
    nf = Fh // FFN_TF

    def body(h_ref, g_ref, wg_ref, wu_ref, wd_ref, o_ref, u_ref, a_ref, b_ref):
        j = pl.program_id(1)

        @pl.when(j == 0)
        def _():
            xv = h_ref[...]
            r = lax.rsqrt(jnp.mean(xv * xv, axis=-1, keepdims=True) + RMS_EPS)
            u_ref[...] = (xv * r * g_ref[...]).astype(BF16)
            o_ref[...] = xv

        u = u_ref[...]
        a = _dot(u, wg_ref[...], NT).astype(BF16)
        b = _dot(u, wu_ref[...], NT).astype(BF16)
        a_ref[...] = a
        b_ref[...] = b
        af = a.astype(F32)
        s = (af * jax.nn.sigmoid(af) * b.astype(F32)).astype(BF16)
        o_ref[...] += _dot(s, wd_ref[...], NN)

    row = pl.BlockSpec((tm, Dm), lambda i, j: (i, 0))
    wblk = pl.BlockSpec((FFN_TF, Dm), lambda i, j: (j, 0))
    ablk = pl.BlockSpec((tm, FFN_TF), lambda i, j: (i, j))
    return pl.pallas_call(
        body, name=name, grid=(T // tm, nf),
        in_specs=[row, pl.BlockSpec((1, Dm), lambda i, j: (0, 0)), wblk, wblk, wblk],
        out_specs=[row, row, ablk, ablk],
        out_shape=[jax.ShapeDtypeStruct((T, Dm), F32), jax.ShapeDtypeStruct((T, Dm), BF16),
                   jax.ShapeDtypeStruct((T, Fh), BF16), jax.ShapeDtypeStruct((T, Fh), BF16)],
        compiler_params=_params(("parallel", "arbitrary")),
    )(h, g, wg_t, wu_t, wd)


def _ffn_bwd(dh, u, a, b, wg_t, wu_t, wd, name="ffn_bwd"):
    T, Dm = dh.shape
    Fh = wd.shape[0]
    nf = Fh // FFN_TF
    once = pl.Buffered(1)

    def body(dh_ref, u_ref, a_ref, b_ref, wg_ref, wu_ref, wd_ref, du_ref, dwg_ref, dwu_ref, dwd_ref):
        j = pl.program_id(0)

        @pl.when(j == 0)
        def _():
            du_ref[...] = jnp.zeros_like(du_ref)

        ds = _dot(dh_ref[...], wd_ref[...], NT)
        af, bf = a_ref[...].astype(F32), b_ref[...].astype(F32)
        sig = jax.nn.sigmoid(af)
        sa = af * sig
        dwd_ref[...] = _dot((sa * bf).astype(BF16), dh_ref[...], TN).astype(BF16)
        dab = jnp.concatenate([(ds * bf * (sig * (1.0 + af * (1.0 - sig)))).astype(BF16),
                               (ds * sa).astype(BF16)], axis=1)
        dw = _dot(dab, u_ref[...], TN)
        dwg_ref[...] = dw[:FFN_TF].astype(BF16)
        dwu_ref[...] = dw[FFN_TF:].astype(BF16)
        du_ref[...] += _dot(dab, jnp.concatenate([wg_ref[...], wu_ref[...]], axis=0), NN)

    full = lambda: pl.BlockSpec((T, Dm), lambda j: (0, 0), pipeline_mode=once)
    wblk = pl.BlockSpec((FFN_TF, Dm), lambda j: (j, 0))
    ablk = pl.BlockSpec((T, FFN_TF), lambda j: (0, j))
    return pl.pallas_call(
        body, name=name, grid=(nf,),
        in_specs=[full(), full(), ablk, ablk, wblk, wblk, wblk],
        out_specs=[pl.BlockSpec((T, Dm), lambda j: (0, 0)), wblk, wblk, wblk],
        out_shape=[jax.ShapeDtypeStruct((T, Dm), F32)] + [jax.ShapeDtypeStruct((Fh, Dm), BF16)] * 3,
        compiler_params=_params(("arbitrary",)),
    )(dh, u, a, b, wg_t, wu_t, wd)


def _rope(x, cos_t, sin_t, col0, ncols, out_dtype, name="rope"):
    T = x.shape[0]
    wt = cos_t.shape[1]
    tm = _pick(T, 256)
    nb = ncols * LANES // wt
    half = MLA_ROPE // 2

    def body(x_ref, c_ref, s_ref, o_ref):
        xv = x_ref[...].astype(F32)
        lane = lax.broadcasted_iota(jnp.int32, xv.shape, 1)
        first = (lane & (MLA_ROPE - 1)) < half
        swapped = jnp.where(first, pltpu.roll(xv, wt - half, 1), pltpu.roll(xv, half, 1))
        o_ref[...] = (xv * c_ref[...] + swapped * s_ref[...]).astype(out_dtype)

    off = col0 * LANES // wt
    return pl.pallas_call(
        body, name=name, grid=(T // tm, nb),
        in_specs=[pl.BlockSpec((tm, wt), lambda i, j: (i, j + off)),
                  pl.BlockSpec((tm, wt), lambda i, j: (i, 0)),
                  pl.BlockSpec((tm, wt), lambda i, j: (i, 0))],
        out_specs=pl.BlockSpec((tm, wt), lambda i, j: (i, j)),
        out_shape=jax.ShapeDtypeStruct((T, ncols * LANES), out_dtype),
        compiler_params=_params(("parallel", "parallel")),
    )(x, cos_t, sin_t)


ATT_T = 256


def _mla_masks(shape):
    lane = lax.broadcasted_iota(jnp.int32, shape, 1)
    m0 = (lane < 64) | ((lane >= 128) & (lane < 160))
    m1 = ((lane >= 64) & (lane < 128)) | ((lane >= 160) & (lane < 192))
    return m0, m1


def _chunk_ok(m_idx, kb, tq, tk):
    tpos = m_idx * tq + lax.broadcasted_iota(jnp.int32, (tq, tk), 0)
    spos = kb * tk + lax.broadcasted_iota(jnp.int32, (tq, tk), 1)
    return (spos >> 6) <= (tpos >> 6)


def _mla_fwd(q, kv, kr, name="mla_fwd"):
    T = q.shape[0]
    tq = tk = _pick(T, ATT_T)
    npair = MLA_HEADS // 2
    scale = (MLA_NOPE + MLA_ROPE) ** -0.5

    def body(q_ref, kn_ref, v_ref, kr_ref, o_ref, lse_ref):
        m_idx = pl.program_id(1)
        qv = q_ref[...]
        m0, m1 = _mla_masks(qv.shape)
        qh = (jnp.where(m0, qv, 0).astype(BF16), jnp.where(m1, qv, 0).astype(BF16))

        def step(kb, carry):
            ks = pl.ds(pl.multiple_of(kb * tk, tk), tk)
            kcat = jnp.concatenate([kn_ref[ks, :], kr_ref[ks, :]], axis=1)
            vv = v_ref[ks, :]
            ok = _chunk_ok(m_idx, kb, tq, tk)
            out = []
            for h in range(2):
                mx, l, acc = carry[3 * h:3 * h + 3]
                s = jnp.where(ok, _dot(qh[h], kcat, NT) * scale, NEG)
                mn = jnp.maximum(mx, jnp.max(s, axis=-1, keepdims=True))
                alpha = jnp.exp(mx - mn)
                p = jnp.exp(s - mn)
                l = alpha * l + jnp.sum(p, axis=-1, keepdims=True)
                acc = alpha * acc + _dot(p.astype(BF16), vv, NN)
                out += [mn, l, acc]
            return tuple(out)

        init = (jnp.full((tq, 1), NEG, F32), jnp.zeros((tq, 1), F32), jnp.zeros((tq, LANES), F32)) * 2
        res = lax.fori_loop(0, m_idx + 1, step, init)
        lane = lax.broadcasted_iota(jnp.int32, (tq, LANES), 1)
        o0 = res[2] / res[1]
        o1 = res[5] / res[4]
        o_ref[...] = jnp.where(lane < 64, o0, o1).astype(o_ref.dtype)
        lse_ref[...] = jnp.where(lane < 64, res[0] + jnp.log(res[1]), res[3] + jnp.log(res[4]))

    full = lambda col: pl.BlockSpec((T, LANES), col)
    return pl.pallas_call(
        body, name=name, grid=(npair, T // tq),
        in_specs=[pl.BlockSpec((tq, 2 * LANES), lambda p, m: (m, p)),
                  full(lambda p, m: (0, p)), full(lambda p, m: (0, npair + p)), full(lambda p, m: (0, 0))],
        out_specs=[pl.BlockSpec((tq, LANES), lambda p, m: (m, p)),
                   pl.BlockSpec((tq, LANES), lambda p, m: (m, p))],
        out_shape=[jax.ShapeDtypeStruct((T, npair * LANES), BF16),
                   jax.ShapeDtypeStruct((T, npair * LANES), F32)],
        compiler_params=_params(("parallel", "arbitrary")),
    )(q, kv, kv, kr)


def _mla_bwd(q, kv, kr, o, lse, do, do_col0, name="mla_bwd"):
    T = q.shape[0]
    tq = tk = _pick(T, ATT_T)
    npair = MLA_HEADS // 2
    scale = (MLA_NOPE + MLA_ROPE) ** -0.5

    def body(q_ref, kn_ref, v_ref, kr_ref, o_ref, lse_ref, do_ref, dq_ref, dkn_ref, dv_ref, dkr_ref):
        p_idx, m_idx = pl.program_id(0), pl.program_id(1)

        @pl.when(m_idx == 0)
        def _():
            dkn_ref[...] = jnp.zeros_like(dkn_ref)
            dv_ref[...] = jnp.zeros_like(dv_ref)

        @pl.when((m_idx == 0) & (p_idx == 0))
        def _():
            dkr_ref[...] = jnp.zeros_like(dkr_ref)

        qv = q_ref[...]
        m0, m1 = _mla_masks(qv.shape)
        qh = (jnp.where(m0, qv, 0).astype(BF16), jnp.where(m1, qv, 0).astype(BF16))
        dov = do_ref[...].astype(F32)
        lane = lax.broadcasted_iota(jnp.int32, (tq, LANES), 1)
        h0 = lane < 64
        prod = dov * o_ref[...].astype(F32)
        delta = (jnp.sum(jnp.where(h0, prod, 0.0), axis=-1, keepdims=True),
                 jnp.sum(jnp.where(h0, 0.0, prod), axis=-1, keepdims=True))
        doh = (jnp.where(h0, dov, 0.0).astype(BF16), jnp.where(h0, 0.0, dov).astype(BF16))
        lsev = lse_ref[...]
        lse_h = (lsev[:, 0:1], lsev[:, 64:65])

        def step(kb, carry):
            ks = pl.ds(pl.multiple_of(kb * tk, tk), tk)
            kcat = jnp.concatenate([kn_ref[ks, :], kr_ref[ks, :]], axis=1)
            vv = v_ref[ks, :]
            ok = _chunk_ok(m_idx, kb, tq, tk)
            dkc = jnp.zeros((tk, 2 * LANES), F32)
            dvv = jnp.zeros((tk, LANES), F32)
            out = []
            for h in range(2):
                s = _dot(qh[h], kcat, NT) * scale
                p = jnp.where(ok, jnp.exp(s - lse_h[h]), 0.0)
                dp = _dot(doh[h], vv, NT)
                ds = (p * (dp - delta[h]) * scale).astype(BF16)
                out.append(carry[h] + _dot(ds, kcat, NN))
                dkc = dkc + _dot(ds, qh[h], TN)
                dvv = dvv + _dot(p.astype(BF16), doh[h], TN)
            dkn_ref[ks, :] += dkc[:, :LANES]
            dkr_ref[ks, :] += dkc[:, LANES:]
            dv_ref[ks, :] += dvv
            return tuple(out)

        init = (jnp.zeros((tq, 2 * LANES), F32),) * 2
        dq0, dq1 = lax.fori_loop(0, m_idx + 1, step, init)
        dq_ref[...] = jnp.where(m0, dq0, jnp.where(m1, dq1, 0.0))

    full = lambda col: pl.BlockSpec((T, LANES), col)
    blk = lambda col: pl.BlockSpec((tq, LANES), col)
    return pl.pallas_call(
        body, name=name, grid=(npair, T // tq),
        in_specs=[pl.BlockSpec((tq, 2 * LANES), lambda p, m: (m, p)),
                  full(lambda p, m: (0, p)), full(lambda p, m: (0, npair + p)), full(lambda p, m: (0, 0)),
                  blk(lambda p, m: (m, p)), blk(lambda p, m: (m, p)),
                  blk(lambda p, m: (m, do_col0 + p))],
        out_specs=[pl.BlockSpec((tq, 2 * LANES), lambda p, m: (m, p)),
                   full(lambda p, m: (0, p)), full(lambda p, m: (0, p)), full(lambda p, m: (0, 0))],
        out_shape=[jax.ShapeDtypeStruct((T, npair * 2 * LANES), F32),
                   jax.ShapeDtypeStruct((T, npair * LANES), F32),
                   jax.ShapeDtypeStruct((T, npair * LANES), F32),
                   jax.ShapeDtypeStruct((T, LANES), F32)],
        compiler_params=_params(("arbitrary", "arbitrary")),
    )(q, kv, kv, kr, o, lse, do)


def _split_dot(x, tri):
    hi = x.astype(BF16)
    lo = (x - hi.astype(F32)).astype(BF16)
    return _dot(hi, tri, NN) + _dot(lo, tri, NN)


def _sb_terms(qh, kk, m_idx, kb, tq, tk, scale):
    z = _dot(qh, kk, NT) * scale
    tpos = m_idx * tq + lax.broadcasted_iota(jnp.int32, (tq, tk), 0)
    spos = kb * tk + lax.broadcasted_iota(jnp.int32, (tq, tk), 1)
    before = spos < tpos
    sp = jnp.maximum(z, 0.0) + jnp.log(1.0 + jnp.exp(-jnp.abs(z)))
    lk = jnp.where(before, -sp, 0.0)
    return z, sp, lk, before


def _sb_fwd(qkv, col0, name="sb_fwd"):
    T = qkv.shape[0]
    tq = tk = _pick(T, ATT_T)
    npair = SB_HEADS // 2
    scale = SB_DIM ** -0.5

    def body(q_ref, k_ref, v_ref, o_ref):
        m_idx = pl.program_id(1)
        qv = q_ref[...].astype(BF16)
        lane = lax.broadcasted_iota(jnp.int32, (tq, LANES), 1)
        h0 = lane < 64
        qh = (jnp.where(h0, qv, 0).astype(BF16), jnp.where(h0, 0, qv).astype(BF16))
        row = lax.broadcasted_iota(jnp.int32, (tk, tk), 0)
        col = lax.broadcasted_iota(jnp.int32, (tk, tk), 1)
        later = (row > col).astype(BF16)

        def step(i, carry):
            kb = m_idx - i
            ks = pl.ds(pl.multiple_of(kb * tk, tk), tk)
            kk = k_ref[ks, :].astype(BF16)
            vv = v_ref[ks, :].astype(BF16)
            out = []
            for h in range(2):
                c, acc = carry[2 * h:2 * h + 2]
                z, sp, lk, before = _sb_terms(qh[h], kk, m_idx, kb, tq, tk, scale)
                a = (z - sp) + _split_dot(lk, later) + c
                w = jnp.where(before, jnp.exp(a), 0.0)
                out += [c + jnp.sum(lk, axis=-1, keepdims=True), acc + _dot(w.astype(BF16), vv, NN)]
            return tuple(out)

        init = (jnp.zeros((tq, 1), F32), jnp.zeros((tq, LANES), F32)) * 2
        res = lax.fori_loop(0, m_idx + 1, step, init)
        o_ref[...] = jnp.where(h0, res[1], res[3]).astype(o_ref.dtype)

    full = lambda col: pl.BlockSpec((T, LANES), col)
    return pl.pallas_call(
        body, name=name, grid=(npair, T // tq),
        in_specs=[pl.BlockSpec((tq, LANES), lambda p, m: (m, col0 + p)),
                  full(lambda p, m: (0, col0 + npair + p)), full(lambda p, m: (0, col0 + 2 * npair + p))],
        out_specs=pl.BlockSpec((tq, LANES), lambda p, m: (m, p)),
        out_shape=jax.ShapeDtypeStruct((T, npair * LANES), BF16),
        compiler_params=_params(("parallel", "arbitrary")),
    )(qkv, qkv, qkv)


def _sb_bwd(qkv, col0, do, do_col0, dep, name="sb_bwd"):
    T = qkv.shape[0]
    tq = tk = _pick(T, ATT_T)
    npair = SB_HEADS // 2
    scale = SB_DIM ** -0.5

    def body(q_ref, k_ref, v_ref, do_ref, dep_ref, dq_ref, dk_ref, dv_ref, e_scr, z_scr, sp_scr):
        m_idx = pl.program_id(1)

        @pl.when(m_idx == 0)
        def _():
            dk_ref[...] = jnp.zeros_like(dk_ref)
            dv_ref[...] = jnp.zeros_like(dv_ref)

        qv = q_ref[...].astype(BF16)
        lane = lax.broadcasted_iota(jnp.int32, (tq, LANES), 1)
        h0 = lane < 64
        qh = (jnp.where(h0, qv, 0).astype(BF16), jnp.where(h0, 0, qv).astype(BF16))
        dov = do_ref[...].astype(F32)
        doh = (jnp.where(h0, dov, 0.0).astype(BF16), jnp.where(h0, 0.0, dov).astype(BF16))
        row = lax.broadcasted_iota(jnp.int32, (tk, tk), 0)
        col = lax.broadcasted_iota(jnp.int32, (tk, tk), 1)
        later = (row > col).astype(BF16)
        earlier = (row < col).astype(BF16)
        dq = []
        for h in range(2):
            def rl(i, c, h=h):
                kb = m_idx - i
                ks = pl.ds(pl.multiple_of(kb * tk, tk), tk)
                kk = k_ref[ks, :].astype(BF16)
                vv = v_ref[ks, :].astype(BF16)
                z, sp, lk, before = _sb_terms(qh[h], kk, m_idx, kb, tq, tk, scale)
                a = (z - sp) + _split_dot(lk, later) + c
                w = jnp.where(before, jnp.exp(a), 0.0)
                dw = _dot(doh[h], vv, NT)
                e_scr[kb] = w * dw
                z_scr[kb] = z
                sp_scr[kb] = sp
                dv_ref[ks, :] += _dot(w.astype(BF16), doh[h], TN)
                return c + jnp.sum(lk, axis=-1, keepdims=True)

            lax.fori_loop(0, m_idx + 1, rl, jnp.zeros((tq, 1), F32))

            def lr(kb, carry, h=h):
                esum, dqa = carry
                ks = pl.ds(pl.multiple_of(kb * tk, tk), tk)
                kk = k_ref[ks, :].astype(BF16)
                e, z, sp = e_scr[kb], z_scr[kb], sp_scr[kb]
                tpos = m_idx * tq + lax.broadcasted_iota(jnp.int32, (tq, tk), 0)
                spos = kb * tk + lax.broadcasted_iota(jnp.int32, (tq, tk), 1)
                prev = _split_dot(e, earlier) + esum
                dz = jnp.where(spos < tpos, e * jnp.exp(-sp) - jnp.exp(z - sp) * prev, 0.0)
                dzb = (dz * scale).astype(BF16)
                dk_ref[ks, :] += _dot(dzb, qh[h], TN)
                return esum + jnp.sum(e, axis=-1, keepdims=True), dqa + _dot(dzb, kk, NN)

            _, dqh = lax.fori_loop(0, m_idx + 1, lr, (jnp.zeros((tq, 1), F32), jnp.zeros((tq, LANES), F32)))
            dq.append(dqh)
        dq_ref[...] = jnp.where(h0, dq[0], dq[1])

    full = lambda col: pl.BlockSpec((T, LANES), col)
    blk = lambda col: pl.BlockSpec((tq, LANES), col)
    return pl.pallas_call(
        body, name=name, grid=(npair, T // tq),
        in_specs=[blk(lambda p, m: (m, col0 + p)),
                  full(lambda p, m: (0, col0 + npair + p)), full(lambda p, m: (0, col0 + 2 * npair + p)),
                  blk(lambda p, m: (m, do_col0 + p)), pl.BlockSpec((8, LANES), lambda p, m: (0, 0))],
        out_specs=[blk(lambda p, m: (m, p)), full(lambda p, m: (0, p)), full(lambda p, m: (0, p))],
        out_shape=[jax.ShapeDtypeStruct((T, npair * LANES), F32)] * 3,
        scratch_shapes=[pltpu.VMEM((T // tk, tq, tk), F32)] * 3,
        compiler_params=_params(("arbitrary", "arbitrary")),
    )(qkv, qkv, qkv, do, dep)


def _band_valid(m_idx):
    qi = lax.broadcasted_iota(jnp.int32, (BAND_TQ, BAND_W), 0)
    j = lax.broadcasted_iota(jnp.int32, (BAND_TQ, BAND_W), 1)
    cq = (m_idx * BAND_TQ + qi) >> 6
    ckp = (m_idx * BAND_TQ + j) >> 6
    return (ckp >= LEFT_CHUNKS) & (ckp >= cq) & (ckp <= cq + LEFT_CHUNKS)


def _band_probs(qh, kw, bias, valid, scale):
    s = _dot(qh, kw, NT) * scale + bias
    s = jnp.where(valid, s, NEG)
    e = jnp.exp(s - jnp.max(s, axis=-1, keepdims=True))
    return e / jnp.sum(e, axis=-1, keepdims=True)


def _band_fwd(qkv, k_pad, v_pad, bias_w, name="band_fwd"):
    T = qkv.shape[0]
    npair = C_HEADS // 2
    scale = C_DIM ** -0.5

    def body(q_ref, k_ref, v_ref, b_ref, o_ref):
        m_idx = pl.program_id(1)
        win = pl.ds(pl.multiple_of(m_idx * BAND_TQ, BAND_TQ), BAND_W)
        kw, vw = k_ref[win, :], v_ref[win, :]
        qv = q_ref[...]
        lane = lax.broadcasted_iota(jnp.int32, (BAND_TQ, LANES), 1)
        h0 = lane < 64
        qh = (jnp.where(h0, qv, 0).astype(BF16), jnp.where(h0, 0, qv).astype(BF16))
        valid = _band_valid(m_idx)
        o = [_dot(_band_probs(qh[h], kw, b_ref[h], valid, scale).astype(BF16), vw, NN) for h in range(2)]
        o_ref[...] = jnp.where(h0, o[0], o[1]).astype(o_ref.dtype)

    Tp = T + PAD_KEYS
    return pl.pallas_call(
        body, name=name, grid=(npair, T // BAND_TQ),
        in_specs=[pl.BlockSpec((BAND_TQ, LANES), lambda p, m: (m, p)),
                  pl.BlockSpec((Tp, LANES), lambda p, m: (0, p)),
                  pl.BlockSpec((Tp, LANES), lambda p, m: (0, p)),
                  pl.BlockSpec((2, BAND_TQ, BAND_W), lambda p, m: (p, 0, 0))],
        out_specs=pl.BlockSpec((BAND_TQ, LANES), lambda p, m: (m, p)),
        out_shape=jax.ShapeDtypeStruct((T, npair * LANES), BF16),
        compiler_params=_params(("parallel", "arbitrary")),
    )(qkv, k_pad, v_pad, bias_w)


def _band_bwd(qkv, k_pad, v_pad, bias_w, do, name="band_bwd"):
    T = qkv.shape[0]
    npair = C_HEADS // 2
    scale = C_DIM ** -0.5

    def body(q_ref, k_ref, v_ref, b_ref, do_ref, dq_ref, dk_ref, dv_ref, db_ref):
        m_idx = pl.program_id(1)

        @pl.when(m_idx == 0)
        def _():
            dk_ref[...] = jnp.zeros_like(dk_ref)
            dv_ref[...] = jnp.zeros_like(dv_ref)
            db_ref[...] = jnp.zeros_like(db_ref)

        win = pl.ds(pl.multiple_of(m_idx * BAND_TQ, BAND_TQ), BAND_W)
        kw, vw = k_ref[win, :], v_ref[win, :]
        qv = q_ref[...]
        dov = do_ref[...].astype(F32)
        lane = lax.broadcasted_iota(jnp.int32, (BAND_TQ, LANES), 1)
        h0 = lane < 64
        qh = (jnp.where(h0, qv, 0).astype(BF16), jnp.where(h0, 0, qv).astype(BF16))
        doh = (jnp.where(h0, dov, 0.0).astype(BF16), jnp.where(h0, 0.0, dov).astype(BF16))
        valid = _band_valid(m_idx)
        dq = []
        dkw = jnp.zeros((BAND_W, LANES), F32)
        dvw = jnp.zeros((BAND_W, LANES), F32)
        for h in range(2):
            p = _band_probs(qh[h], kw, b_ref[h], valid, scale)
            dp = _dot(doh[h], vw, NT)
            dsb = p * (dp - jnp.sum(p * dp, axis=-1, keepdims=True))
            db_ref[h] += dsb
            dsq = (dsb * scale).astype(BF16)
            dq.append(_dot(dsq, kw, NN))
            dkw = dkw + _dot(dsq, qh[h], TN)
            dvw = dvw + _dot(p.astype(BF16), doh[h], TN)
        dq_ref[...] = jnp.where(h0, dq[0], dq[1])
        dk_ref[win, :] += dkw
        dv_ref[win, :] += dvw

    Tp = T + PAD_KEYS
    blk = lambda col: pl.BlockSpec((BAND_TQ, LANES), col)
    full = pl.BlockSpec((Tp, LANES), lambda p, m: (0, p))
    bias = pl.BlockSpec((2, BAND_TQ, BAND_W), lambda p, m: (p, 0, 0))
    return pl.pallas_call(
        body, name=name, grid=(npair, T // BAND_TQ),
        in_specs=[blk(lambda p, m: (m, p)), full, full, bias, blk(lambda p, m: (m, p))],
        out_specs=[blk(lambda p, m: (m, p)), full, full, bias],
        out_shape=[jax.ShapeDtypeStruct((T, npair * LANES), F32),
                   jax.ShapeDtypeStruct((Tp, npair * LANES), F32),
                   jax.ShapeDtypeStruct((Tp, npair * LANES), F32),
                   jax.ShapeDtypeStruct((C_HEADS, BAND_TQ, BAND_W), F32)],
        compiler_params=_params(("arbitrary", "arbitrary")),
    )(qkv, k_pad, v_pad, bias_w, do)


def _skew_bits(x, left):
    w = x.shape[1]
    row = lax.broadcasted_iota(jnp.int32, x.shape, 0)
    for b in range(BAND_TQ.bit_length() - 1):
        amt = (w - (1 << b)) if left else (1 << b)
        x = jnp.where((row >> b) & 1 == 1, pltpu.roll(x, amt, 1), x)
    return x


def _toeplitz(diag, name="toeplitz"):
    H = diag.shape[0]

    def body(d_ref, o_ref):
        x = jnp.broadcast_to(d_ref[0], (BAND_TQ, TOEP_W))
        o_ref[0] = _skew_bits(x, left=False)[:, BAND_TQ:]

    return pl.pallas_call(
        body, name=name, grid=(H,),
        in_specs=[pl.BlockSpec((1, 1, TOEP_W), lambda h: (h, 0, 0))],
        out_specs=pl.BlockSpec((1, BAND_TQ, BAND_W), lambda h: (h, 0, 0)),
        out_shape=jax.ShapeDtypeStruct((H, BAND_TQ, BAND_W), F32),
        compiler_params=_params(("parallel",)),
    )(diag.reshape(H, 1, TOEP_W))


def _toeplitz_bwd(dbias, name="toeplitz_bwd"):
    H = dbias.shape[0]

    def body(d_ref, o_ref):
        x = jnp.concatenate([jnp.zeros((BAND_TQ, BAND_TQ), F32), d_ref[0]], axis=1)
        o_ref[0] = jnp.sum(_skew_bits(x, left=True), axis=0, keepdims=True)

    return pl.pallas_call(
        body, name=name, grid=(H,),
        in_specs=[pl.BlockSpec((1, BAND_TQ, BAND_W), lambda h: (h, 0, 0))],
        out_specs=pl.BlockSpec((1, 1, TOEP_W), lambda h: (h, 0, 0)),
        out_shape=jax.ShapeDtypeStruct((H, 1, TOEP_W), F32),
        compiler_params=_params(("parallel",)),
    )(dbias).reshape(H, TOEP_W)


_HBM = pl.BlockSpec(memory_space=pltpu.HBM)
_SEM = pl.BlockSpec(memory_space=pltpu.SEMAPHORE)
_EFFECT = pltpu.SideEffectType.DATAFLOW_SIDE_EFFECTING


def _peers():
    x, y, c = lax.axis_index("x"), lax.axis_index("y"), lax.axis_index("c")
    out = []
    for k in range(1, N_DEV):
        peer = (1 - x if (k >> 2) & 1 else x, 1 - y if (k >> 1) & 1 else y, 1 - c if k & 1 else c)
        out.append((peer, 4 * peer[0] + 2 * peer[1] + peer[2]))
    return 4 * x + 2 * y + c, out


def _split_copies(ins, lands, scatter, send_sem, recv_sem, arriving):
    me, peers = _peers()
    out = []
    for a in range(len(ins)):
        for peer, idx in peers:
            out.append(pltpu.make_async_remote_copy(
                src_ref=ins[a].at[idx] if scatter[a] else ins[a],
                dst_ref=lands[a].at[idx if arriving else me], send_sem=send_sem, recv_sem=recv_sem,
                device_id=peer, device_id_type=pl.DeviceIdType.MESH))
    return out


def _landing_zones(arrays, scatter):
    return [lax.empty((N_DEV,) + (a.shape[1:] if s else a.shape), a.dtype) for a, s in zip(arrays, scatter)]


def _place_own(arrays, scatter, name):
    n = len(arrays)
    lands = _landing_zones(arrays, scatter)
    me = (4 * lax.axis_index("x") + 2 * lax.axis_index("y") + lax.axis_index("c")).astype(jnp.int32).reshape(1)

    def body(me_ref, *refs):
        for a in range(n):
            refs[2 * n + a][...] = refs[a][...].reshape(refs[2 * n + a].shape)

    def row_spec(shape):
        zeros = (0,) * (len(shape) - 1)
        return pl.BlockSpec((1,) + tuple(shape[1:]), lambda i, me_ref: (me_ref[0],) + zeros)

    in_specs = [row_spec(a.shape) if s else pl.BlockSpec(a.shape, lambda i, me_ref, nd=a.ndim: (0,) * nd)
                for a, s in zip(arrays, scatter)]
    return pl.pallas_call(
        body, name=name,
        out_shape=[jax.ShapeDtypeStruct(l.shape, l.dtype) for l in lands],
        grid_spec=pltpu.PrefetchScalarGridSpec(
            num_scalar_prefetch=1, grid=(1,),
            in_specs=in_specs + [pl.BlockSpec(memory_space=pl.ANY)] * n,
            out_specs=[row_spec(l.shape) for l in lands]),
        input_output_aliases={1 + n + i: i for i in range(n)},
        compiler_params=_params(("arbitrary",)),
    )(me, *arrays, *lands)


def _exchange_start(arrays, scatter, after, name):
    n = len(arrays)
    lands = list(_place_own(arrays, scatter, name=name.replace("_start_", "_own_")))

    def body(*refs):
        ins, lnd = refs[:n], refs[n:2 * n]
        send_sem, recv_sem = refs[2 * n + 1:2 * n + 3]
        token = refs[-1]
        for cp in _split_copies(ins, lnd, scatter, send_sem, recv_sem, arriving=False):
            cp.start()
        token[...] = jnp.zeros_like(token)

    hbm = lambda a: pltpu.HBM(a.shape, a.dtype)
    out = pl.pallas_call(
        body, name=name,
        out_shape=(pltpu.SemaphoreType.DMA(()), pltpu.SemaphoreType.DMA(()),
                   *[hbm(a) for a in arrays], *[hbm(a) for a in lands],
                   jax.ShapeDtypeStruct((8, LANES), F32)),
        in_specs=[_HBM] * (2 * n) + [pl.BlockSpec(memory_space=pl.ANY)],
        out_specs=(_SEM, _SEM, *([_HBM] * (2 * n)), pl.BlockSpec(memory_space=pltpu.VMEM)),
        input_output_aliases={i: 2 + i for i in range(2 * n)},
        compiler_params=pltpu.CompilerParams(has_side_effects=_EFFECT),
    )(*[pltpu.with_memory_space_constraint(a, pltpu.HBM) for a in list(arrays) + lands], after)
    return (out[0], out[1], list(out[2:2 + n]), list(out[2 + n:2 + 2 * n]), tuple(scatter)), out[-1]


def _exchange_wait(handle, after, name):
    send_sem, recv_sem, ins, lands, scatter = handle
    n = len(ins)

    def body(*refs):
        i_ref, l_ref = refs[:n], refs[n:2 * n]
        s_sem, r_sem = refs[2 * n:2 * n + 2]
        for cp in _split_copies(i_ref, l_ref, scatter, s_sem, r_sem, arriving=False):
            cp.wait_send()
        for cp in _split_copies(i_ref, l_ref, scatter, s_sem, r_sem, arriving=True):
            cp.wait_recv()

    hbm = lambda a: pltpu.HBM(a.shape, a.dtype)
    out = pl.pallas_call(
        body, name=name,
        out_shape=tuple(hbm(a) for a in ins + lands),
        in_specs=[_HBM] * (2 * n) + [_SEM, _SEM, pl.BlockSpec(memory_space=pl.ANY)],
        out_specs=tuple([_HBM] * (2 * n)),
        input_output_aliases={i: i for i in range(2 * n)},
        compiler_params=pltpu.CompilerParams(has_side_effects=_EFFECT),
    )(*ins, *lands, send_sem, recv_sem, after)
    return list(out[n:])


def _adamw(w, parts, m, v, name="adamw"):
    R, C = w.shape
    tr = next(t for t in (256, 128, 64, 32, 16, 8) if R % t == 0)
    c1 = 1.0 - ADAM_B1 ** ADAM_STEP
    c2 = 1.0 - ADAM_B2 ** ADAM_STEP

    def body(w_ref, p_ref, m_ref, v_ref, g_ref, d_ref, nm_ref, nv_ref):
        g = p_ref[0].astype(F32)
        for i in range(1, N_DEV):
            g = g + p_ref[i].astype(F32)
        nm = ADAM_B1 * m_ref[...] + (1.0 - ADAM_B1) * g
        nv = ADAM_B2 * v_ref[...] + (1.0 - ADAM_B2) * (g * g)
        g_ref[...] = g
        nm_ref[...] = nm
        nv_ref[...] = nv
        d_ref[...] = -ADAM_LR * ((nm / c1) / (jnp.sqrt(nv / c2) + ADAM_EPS) + ADAM_WD * w_ref[...])

    blk = pl.BlockSpec((tr, C), lambda i: (i, 0))
    return pl.pallas_call(
        body, name=name, grid=(R // tr,),
        in_specs=[blk, pl.BlockSpec((N_DEV, tr, C), lambda i: (0, i, 0)), blk, blk],
        out_specs=[blk] * 4,
        out_shape=[jax.ShapeDtypeStruct((R, C), F32)] * 4,
        compiler_params=_params(("parallel",)),
    )(w, parts, m, v)


_O1 = Q_LORA
_O2 = _O1 + KV_LORA
_O3 = _O2 + MLA_ROPE
_NB = SB_HEADS * SB_DIM
IN_W = _O2 + LANES + 3 * _NB
COL_KR = _O2 // LANES
COL_SB = COL_KR + 1


def _w_in_local(w):
    kr = w[_O2:_O3]
    pad = jnp.zeros((LANES - 2 * MLA_ROPE, w.shape[1]), w.dtype)
    return jnp.concatenate([w[:_O2], kr, kr, pad, w[_O3:]], axis=0)


def _w_in_grad(g):
    kr = (g[_O2:_O2 + MLA_ROPE].astype(F32) + g[_O2 + MLA_ROPE:_O2 + 2 * MLA_ROPE].astype(F32)).astype(g.dtype)
    return jnp.concatenate([g[:_O2], kr, g[_O2 + LANES:]], axis=0)


def _w_uq_local(w):
    w3 = w.reshape(MLA_HEADS // 2, 2, MLA_NOPE + MLA_ROPE, w.shape[1])
    nope = w3[:, :, :MLA_NOPE].reshape(MLA_HEADS // 2, 2 * MLA_NOPE, w.shape[1])
    rope = w3[:, :, MLA_NOPE:].reshape(MLA_HEADS // 2, 2 * MLA_ROPE, w.shape[1])
    pad = jnp.zeros((MLA_HEADS // 2, LANES - 2 * MLA_ROPE, w.shape[1]), w.dtype)
    return jnp.concatenate([nope, rope, pad], axis=1).reshape(-1, w.shape[1])


def _w_uq_grad(g):
    g3 = g.reshape(MLA_HEADS // 2, 2 * LANES, g.shape[1])
    nope = g3[:, :2 * MLA_NOPE].reshape(MLA_HEADS // 2, 2, MLA_NOPE, g.shape[1])
    rope = g3[:, LANES:LANES + 2 * MLA_ROPE].reshape(MLA_HEADS // 2, 2, MLA_ROPE, g.shape[1])
    return jnp.concatenate([nope, rope], axis=2).reshape(-1, g.shape[1])


def _w_ukv_local(w):
    w3 = w.reshape(MLA_HEADS, MLA_NOPE + MLA_V, w.shape[1])
    return jnp.concatenate([w3[:, :MLA_NOPE].reshape(-1, w.shape[1]),
                            w3[:, MLA_NOPE:].reshape(-1, w.shape[1])], axis=0)


def _w_ukv_grad(g):
    half = MLA_HEADS * MLA_NOPE
    kn = g[:half].reshape(MLA_HEADS, MLA_NOPE, g.shape[1])
    vv = g[half:].reshape(MLA_HEADS, MLA_V, g.shape[1])
    return jnp.concatenate([kn, vv], axis=1).reshape(-1, g.shape[1])


def _rope_tables(T):
    pos = jnp.arange(T, dtype=F32)
    inv_freq = ROPE_THETA ** (-jnp.arange(0, MLA_ROPE, 2, dtype=F32) / MLA_ROPE)
    ang = pos[:, None] * inv_freq[None, :]
    cos, sin = jnp.cos(ang), jnp.sin(ang)
    ones = jnp.ones((T, LANES - 2 * MLA_ROPE), F32)
    cos_k = jnp.concatenate([cos, cos, cos, cos, ones], axis=1)
    sin_k = jnp.concatenate([-sin, sin, -sin, sin, 0.0 * ones], axis=1)
    cos_q = jnp.concatenate([jnp.ones((T, LANES), F32), cos_k], axis=1)
    sin_q = jnp.concatenate([jnp.zeros((T, LANES), F32), sin_k], axis=1)
    return cos_q, sin_q, cos_k, sin_k


def _bias_diag_index():
    ell = np.arange(TOEP_W)
    return np.clip(BAND_W - ell, -REL_CLIP, REL_CLIP) + REL_CLIP


def _local_step(x, target, small, get_weights, put_grads):
    T = x.shape[0]
    cos_q, sin_q, cos_k, sin_k = _rope_tables(T)
    G = {}
    W = dict(small)

    u0 = _rms_fwd(x, W["g_mix"][0:1], name="rms_mix0")
    W.update(get_weights("in0", u0))
    proj = _mm(u0, W["w_in_t"], dims="nt", name="proj_in")
    W.update(get_weights("mix0", proj))
    c_q, c_kv = proj[:, :_O1], proj[:, _O1:_O2]
    nq = _rms_fwd(c_q, W["g_cq"], name="rms_cq")
    nkv = _rms_fwd(c_kv, W["g_ckv"], name="rms_ckv")
    qa_raw = _mm(nq, W["w_uq_t"], dims="nt", name="proj_uq")
    qa = _rope(qa_raw, cos_q, sin_q, 0, qa_raw.shape[1] // LANES, BF16, name="rope_q")
    kv = _mm(nkv, W["w_ukv_t"], dims="nt", out_dtype=BF16, name="proj_ukv")
    kr = _rope(proj, cos_k, sin_k, COL_KR, 1, BF16, name="rope_k")
    o_a, lse = _mla_fwd(qa, kv, kr)
    o_b = _sb_fwd(proj, COL_SB)
    o_ab = jnp.concatenate([o_a, o_b], axis=1)
    h1 = _mm(o_ab, W["ev_w_out"], res=x, name="out_ev")

    def ffn_fwd(h, layer):
        W.update(get_weights(f"ffn{layer}", h))
        return _ffn_fwd(h, W["g_ffn"][layer:layer + 1], W[f"w_gate_t{layer}"], W[f"w_up_t{layer}"],
                        W[f"w_down{layer}"], name=f"ffn_fwd{layer}")

    h2, u1, a0, b0 = ffn_fwd(h1, 0)

    W.update(get_weights("mix1", h2))
    u2 = _rms_fwd(h2, W["g_mix"][1:2], name="rms_mix1")
    qkv = _mm(u2, W["od_w_qkv_t"], dims="nt", out_dtype=BF16, name="proj_qkv")
    nc = C_HEADS * C_DIM
    pad = ((PAD_KEYS, 0), (0, 0))
    k_pad, v_pad = jnp.pad(qkv[:, nc:2 * nc], pad), jnp.pad(qkv[:, 2 * nc:], pad)
    diag_idx = _bias_diag_index()
    bias_w = _toeplitz(W["od_rel_bias"][:, diag_idx])
    o_c = _band_fwd(qkv, k_pad, v_pad, bias_w)
    h3 = _mm(o_c, W["od_w_out"], res=h2, name="out_od")
    h4, u3, a1, b1 = ffn_fwd(h3, 1)

    loss, dh, dhb, G["g_final"] = _loss_head(h4, W["g_final"], target)

    def ffn_bwd(dh, dhb, h, u, a, b, layer):
        du, g_gate, g_up, g_down = _ffn_bwd(dhb, u, a, b, W[f"w_gate_t{layer}"], W[f"w_up_t{layer}"],
                                            W[f"w_down{layer}"], name=f"ffn_bwd{layer}")
        tok = put_grads(f"ffn{layer}", {"w_gate_t": g_gate, "w_up_t": g_up, "w_down": g_down})
        return _rms_bwd(h, W["g_ffn"][layer:layer + 1] + tok[:1, :1], du, dres=dh, name=f"rms_ffn_bwd{layer}")

    dh3, dh3b, g_gffn1 = ffn_bwd(dh, dhb, h3, u3, a1, b1, 1)

    do_c = _mm(dh3b, W["od_w_out"], dims="nt", name="out_od_dx")
    g_od_out = _mm(o_c, dh3b, dims="tn", out_dtype=BF16, name="out_od_dw")
    dq_c, dk_p, dv_p, dbias_w = _band_bwd(qkv, k_pad, v_pad, bias_w, do_c)
    dqkv = jnp.concatenate([dq_c, dk_p[PAD_KEYS:], dv_p[PAD_KEYS:]], axis=1)
    du2 = _mm(dqkv, W["od_w_qkv_t"], name="proj_qkv_dx")
    tok = put_grads("mix1", {"od_w_qkv_t": _mm(dqkv, u2, dims="tn", out_dtype=BF16, name="proj_qkv_dw"),
                             "od_w_out": g_od_out})
    ddiag = _toeplitz_bwd(dbias_w)
    n_far = BAND_W - REL_CLIP + 1
    G["od_rel_bias"] = jnp.concatenate(
        [jnp.zeros((C_HEADS, REL_CLIP - BAND_TQ + 1), F32), ddiag[:, n_far:][:, ::-1],
         jnp.sum(ddiag[:, :n_far], axis=1, keepdims=True)], axis=1)
    dh2, dh2b, g_gmix1 = _rms_bwd(h2, W["g_mix"][1:2] + tok[:1, :1], du2, dres=dh3, name="rms_mix_bwd1")

    dh1, dh1b, g_gffn0 = ffn_bwd(dh2, dh2b, h1, u1, a0, b0, 0)
    G["g_ffn"] = jnp.concatenate([g_gffn0, g_gffn1], axis=0)

    do_ab = _mm(dh1b, W["ev_w_out"], dims="nt", name="out_ev_dx")
    g0 = {"ev_w_out": _mm(o_ab, dh1b, dims="tn", out_dtype=BF16, name="out_ev_dw")}
    dqa, dkn, dva, dkr = _mla_bwd(qa, kv, kr, o_a, lse, do_ab, 0)
    dqa_raw = _rope(dqa, cos_q, -sin_q, 0, dqa.shape[1] // LANES, F32, name="rope_q_bwd")
    g0["w_uq_t"] = _mm(dqa_raw, nq, dims="tn", name="proj_uq_dw")
    dnq = _mm(dqa_raw, W["w_uq_t"], name="proj_uq_dx")
    dc_q, _, G["g_cq"] = _rms_bwd(c_q, W["g_cq"], dnq, name="rms_cq_bwd")
    dkv = jnp.concatenate([dkn, dva], axis=1)
    g0["w_ukv_t"] = _mm(dkv, nkv, dims="tn", name="proj_ukv_dw")
    dnkv = _mm(dkv, W["w_ukv_t"], name="proj_ukv_dx")
    dc_kv, _, G["g_ckv"] = _rms_bwd(c_kv, W["g_ckv"], dnkv, name="rms_ckv_bwd")
    tok = put_grads("mix0", g0)
    dqb, dkb, dvb = _sb_bwd(proj, COL_SB, do_ab, MLA_HEADS // 2, tok)
    dkr_raw = _rope(dkr, cos_k, -sin_k, 0, 1, F32, name="rope_k_bwd")
    dproj = jnp.concatenate([dc_q, dc_kv, dkr_raw, dqb, dkb, dvb], axis=1)
    du0 = _mm(dproj, W["w_in_t"], name="proj_in_dx")
    tok = put_grads("in0", {"w_in_t": _mm(dproj, u0, dims="tn", name="proj_in_dw")})
    dx, _, g_gmix0 = _rms_bwd(x, W["g_mix"][0:1] + tok[:1, :1], du0, dres=dh1, name="rms_mix_bwd0")
    G["g_mix"] = jnp.concatenate([g_gmix0, g_gmix1], axis=0)
    return loss[0, 0], dx, G


_BIG = ["ev_w_in", "ev_w_uq", "ev_w_ukv", "ev_w_out", "od_w_qkv", "od_w_out", "w_gate", "w_up", "w_down"]
_COL_SHARDED = {"ev_w_in", "ev_w_uq", "ev_w_ukv", "od_w_qkv", "w_gate", "w_up"}
_SMALL = ["ev_g_cq", "ev_g_ckv", "od_rel_bias", "g_mix", "g_ffn", "g_final"]
_GROUPS = {
    "in0": ["ev_w_in"],
    "mix0": ["ev_w_uq", "ev_w_ukv", "ev_w_out"],
    "ffn0": ["w_gate0", "w_up0", "w_down0"],
    "mix1": ["od_w_qkv", "od_w_out"],
    "ffn1": ["w_gate1", "w_up1", "w_down1"],
}
_GROUP_SRC = {n + str(l): (n, l) for n in ("w_gate", "w_up", "w_down") for l in (0, 1)}
_SMALL_ROWS = 8
_SMALL_COLS = 1792


def _pack_small(vals):
    flat = jnp.concatenate([v.reshape(-1).astype(F32) for v in vals])
    flat = jnp.pad(flat, (0, _SMALL_ROWS * _SMALL_COLS - flat.shape[0]))
    return flat.reshape(_SMALL_ROWS, _SMALL_COLS)


def _unpack_small(packed, like):
    flat = packed.reshape(-1)
    out, off = [], 0
    for v in like:
        out.append(flat[off:off + v.size].reshape(v.shape))
        off += v.size
    return out


def kernel(x, ev_w_in, ev_g_cq, ev_w_uq, ev_g_ckv, ev_w_ukv, ev_w_out, od_w_qkv, od_rel_bias, od_w_out, g_mix, g_ffn, w_gate, w_up, w_down, g_final, loss_target, m_ev_w_in, m_ev_g_cq, m_ev_w_uq, m_ev_g_ckv, m_ev_w_ukv, m_ev_w_out, m_od_w_qkv, m_od_rel_bias, m_od_w_out, m_g_mix, m_g_ffn, m_w_gate, m_w_up, m_w_down, m_g_final, v_ev_w_in, v_ev_g_cq, v_ev_w_uq, v_ev_g_ckv, v_ev_w_ukv, v_ev_w_out, v_od_w_qkv, v_od_rel_bias, v_od_w_out, v_g_mix, v_g_ffn, v_w_gate, v_w_up, v_w_down, v_g_final):
    args = dict(locals())
    w = {n: args[n] for n in _BIG + _SMALL}
    mom = {n: args["m_" + n] for n in _BIG + _SMALL}
    var = {n: args["v_" + n] for n in _BIG + _SMALL}

    own = {}
    for grp, names in _GROUPS.items():
        for n in names:
            base, layer = _GROUP_SRC.get(n, (n, 0))
            shard = w[base][layer:layer + 1]
            own[n] = (jnp.swapaxes(shard, 1, 2) if base in _COL_SHARDED else shard).astype(BF16)
    gather, token = {}, x[0, :8, :LANES]
    for grp, names in _GROUPS.items():
        gather[grp], token = _exchange_start([own[n] for n in names], [False] * len(names), token,
                                             name="gather_start_" + grp)

    def get_weights(grp, after):
        names = _GROUPS[grp]
        lands = _exchange_wait(gather[grp], token if after is None else after, name="gather_wait_" + grp)
        full = {n: l.reshape(-1, l.shape[-1]) for n, l in zip(names, lands)}
        if grp == "in0":
            return {"w_in_t": _w_in_local(full["ev_w_in"])}
        if grp == "mix0":
            return {"w_uq_t": _w_uq_local(full["ev_w_uq"]), "w_ukv_t": _w_ukv_local(full["ev_w_ukv"]),
                    "ev_w_out": full["ev_w_out"]}
        if grp == "mix1":
            return {"od_w_qkv_t": full["od_w_qkv"], "od_w_out": full["od_w_out"]}
        layer = grp[-1]
        return {"w_gate_t" + layer: full["w_gate" + layer], "w_up_t" + layer: full["w_up" + layer],
                "w_down" + layer: full["w_down" + layer]}

    scatter = {}

    def put_grads(grp, g):
        if grp == "in0":
            g = {"ev_w_in": _w_in_grad(g["w_in_t"])}
        elif grp == "mix0":
            g = {"ev_w_uq": _w_uq_grad(g["w_uq_t"]), "ev_w_ukv": _w_ukv_grad(g["w_ukv_t"]),
                 "ev_w_out": g["ev_w_out"]}
        elif grp == "mix1":
            g = {"od_w_qkv": g["od_w_qkv_t"], "od_w_out": g["od_w_out"]}
        else:
            layer = grp[-1]
            g = {"w_gate" + layer: g["w_gate_t"], "w_up" + layer: g["w_up_t"], "w_down" + layer: g["w_down"]}
        names = _GROUPS[grp]
        send = [g[n].reshape(N_DEV, 1, g[n].shape[0] // N_DEV, g[n].shape[1]).astype(BF16) for n in names]
        handle, tok = _exchange_start(send, [True] * len(names), send[0], name="scatter_start_" + grp)
        scatter[grp] = handle
        return tok

    small = {"g_cq": ev_g_cq, "g_ckv": ev_g_ckv, "od_rel_bias": od_rel_bias[0],
             "g_mix": g_mix + token[0, 0], "g_ffn": g_ffn, "g_final": g_final.reshape(1, -1)}
    loss_part, dx, G = _local_step(x[0], loss_target[0], small, get_weights, put_grads)
    loss = lax.psum(loss_part, ("x", "y", "c"))
    g_small = _pack_small([G["g_cq"], G["g_ckv"], G["od_rel_bias"], G["g_mix"], G["g_ffn"], G["g_final"]])
    small_handle, _ = _exchange_start([g_small], [False], dx, name="gather_start_small")

    grads, deltas, new_m, new_v = {}, {}, {}, {}
    parts, after = {}, dx

    def wait_parts(grp, after):
        lands = _exchange_wait(scatter[grp], after, name="scatter_wait_" + grp)
        for n, l in zip(_GROUPS[grp], lands):
            parts[n] = jnp.swapaxes(l, 2, 3) if _GROUP_SRC.get(n, (n, 0))[0] in _COL_SHARDED else l
        return lands[0]

    def adamw(n):
        shp = w[n].shape
        r2 = (-1, shp[-1])
        res = _adamw(w[n].reshape(r2), parts[n].reshape((N_DEV,) + w[n].reshape(r2).shape),
                     mom[n].reshape(r2), var[n].reshape(r2), name="adamw_" + n)
        grads[n], deltas[n], new_m[n], new_v[n] = [r.reshape(shp) for r in res]
        return res[0]

    for grp in ("ffn1", "mix1", "ffn0", "mix0"):
        after = wait_parts(grp, after)
    for n in ("w_gate", "w_up", "w_down"):
        parts[n] = jnp.concatenate([parts[n + "0"], parts[n + "1"]], axis=1)
    for n in _BIG[1:]:
        after = adamw(n)
    after = wait_parts("in0", after)
    after = adamw("ev_w_in")
    small_w = [w[n] for n in _SMALL]
    small_parts = _exchange_wait(small_handle, after, name="gather_wait_small")[0]
    res = _adamw(_pack_small(small_w), small_parts, _pack_small([mom[n] for n in _SMALL]),
                 _pack_small([var[n] for n in _SMALL]), name="adamw_small")
    for d, packed in zip((grads, deltas, new_m, new_v), res):
        for n, val in zip(_SMALL, _unpack_small(packed, small_w)):
            d[n] = val

    order = ["ev_w_in", "ev_g_cq", "ev_w_uq", "ev_g_ckv", "ev_w_ukv", "ev_w_out", "od_w_qkv", "od_rel_bias",
             "od_w_out", "g_mix", "g_ffn", "w_gate", "w_up", "w_down", "g_final"]
    out = [loss, dx[None]]
    for d in (grads, deltas, new_m, new_v):
        out += [d[n] for n in order]
    return tuple(out)
```

```python
import functools

import numpy as np
import jax
import jax.numpy as jnp
from jax import lax
from jax.experimental import pallas as pl
from jax.experimental.pallas import tpu as pltpu

F32 = jnp.float32
BF16 = jnp.bfloat16

D_MODEL = 1024
CHUNK = 64
MLA_HEADS = 8
MLA_NOPE = 64
MLA_ROPE = 32
MLA_V = 64
Q_LORA = 384
KV_LORA = 256
ROPE_THETA = 10000.0
SB_HEADS = 8
SB_DIM = 64
C_HEADS = 16
C_DIM = 64
LEFT_CHUNKS = 8
REL_CLIP = 256
D_FF = 2816
RMS_EPS = 1e-6
ADAM_LR = 0.001
ADAM_B1 = 0.9
ADAM_B2 = 0.999
ADAM_EPS = 1e-08
ADAM_WD = 0.01
ADAM_STEP = 10

N_DEV = 8
LANES = 128
VMEM_LIMIT = 56 * 1024 * 1024
NEG = -1e30
PAD_KEYS = LEFT_CHUNKS * CHUNK
BAND_TQ = 128
BAND_W = BAND_TQ + PAD_KEYS
TOEP_W = BAND_W + BAND_TQ

NN = (((1,), (0,)), ((), ()))
NT = (((1,), (1,)), ((), ()))
TN = (((0,), (0,)), ((), ()))


def _dot(a, b, dn):
    return lax.dot_general(a, b, dn, preferred_element_type=F32)


def _pick(dim, pref):
    if dim <= pref:
        return dim
    best = None
    for t in range(LANES, pref + 1, LANES):
        if dim % t == 0:
            best = t
    assert best is not None, (dim, pref)
    return best


def _params(sem):
    return pltpu.CompilerParams(dimension_semantics=sem, vmem_limit_bytes=VMEM_LIMIT)


def _mm(a, b, dims="nn", res=None, out_dtype=F32, name="mm"):
    if dims == "nn":
        (M, K), (K2, N) = a.shape, b.shape
    elif dims == "nt":
        (M, K), (N, K2) = a.shape, b.shape
    else:
        (K, M), (K2, N) = a.shape, b.shape
    assert K == K2, (a.shape, b.shape, dims)
    tm, tn, tk = _pick(M, 512), _pick(N, 1408), _pick(K, 1408)
    nk = K // tk
    dn = {"nn": NN, "nt": NT, "tn": TN}[dims]
    has_res = res is not None

    def body(*refs):
        if has_res:
            a_ref, b_ref, r_ref, o_ref, acc = refs
        else:
            a_ref, b_ref, o_ref, acc = refs
        k = pl.program_id(2)

        @pl.when(k == 0)
        def _():
            acc[...] = jnp.zeros_like(acc)

        acc[...] += _dot(a_ref[...].astype(BF16), b_ref[...].astype(BF16), dn)

        @pl.when(k == nk - 1)
        def _():
            r = acc[...]
            if has_res:
                r = r + r_ref[...]
            o_ref[...] = r.astype(out_dtype)

    a_spec = (pl.BlockSpec((tk, tm), lambda i, j, k: (k, i)) if dims == "tn"
              else pl.BlockSpec((tm, tk), lambda i, j, k: (i, k)))
    b_spec = (pl.BlockSpec((tn, tk), lambda i, j, k: (j, k)) if dims == "nt"
              else pl.BlockSpec((tk, tn), lambda i, j, k: (k, j)))
    o_spec = pl.BlockSpec((tm, tn), lambda i, j, k: (i, j))
    in_specs = [a_spec, b_spec] + ([o_spec] if has_res else [])
    args = (a, b) + ((res,) if has_res else ())
    return pl.pallas_call(
        body, name=name, grid=(M // tm, N // tn, nk),
        in_specs=in_specs, out_specs=o_spec,
        out_shape=jax.ShapeDtypeStruct((M, N), out_dtype),
        scratch_shapes=[pltpu.VMEM((tm, tn), F32)],
        compiler_params=_params(("parallel", "parallel", "arbitrary")),
    )(*args)


def _rms_fwd(x, g, out_dtype=BF16, name="rms_fwd"):
    T, Fd = x.shape
    tm = _pick(T, 256)

    def body(x_ref, g_ref, o_ref):
        xv = x_ref[...]
        r = lax.rsqrt(jnp.mean(xv * xv, axis=-1, keepdims=True) + RMS_EPS)
        o_ref[...] = (xv * r * g_ref[...]).astype(out_dtype)

    return pl.pallas_call(
        body, name=name, grid=(T // tm,),
        in_specs=[pl.BlockSpec((tm, Fd), lambda i: (i, 0)), pl.BlockSpec((1, Fd), lambda i: (0, 0))],
        out_specs=pl.BlockSpec((tm, Fd), lambda i: (i, 0)),
        out_shape=jax.ShapeDtypeStruct((T, Fd), out_dtype),
        compiler_params=_params(("parallel",)),
    )(x, g)


def _rms_bwd(x, g, dy, dres=None, name="rms_bwd"):
    T, Fd = x.shape
    tm = _pick(T, 256)
    has_res = dres is not None

    def body(*refs):
        if has_res:
            x_ref, g_ref, dy_ref, r_ref, dx_ref, dxb_ref, dg_ref = refs
        else:
            x_ref, g_ref, dy_ref, dx_ref, dxb_ref, dg_ref = refs
        xv, dyv = x_ref[...], dy_ref[...]
        r = lax.rsqrt(jnp.mean(xv * xv, axis=-1, keepdims=True) + RMS_EPS)
        gdy = dyv * g_ref[...]
        dot = jnp.mean(xv * gdy, axis=-1, keepdims=True)
        dx = r * gdy - xv * (r * r * r * dot)
        if has_res:
            dx = dx + r_ref[...]
        dx_ref[...] = dx
        dxb_ref[...] = dx.astype(BF16)

        @pl.when(pl.program_id(0) == 0)
        def _():
            dg_ref[...] = jnp.zeros_like(dg_ref)

        dg_ref[...] += jnp.sum(dyv * xv * r, axis=0, keepdims=True)

    row = pl.BlockSpec((tm, Fd), lambda i: (i, 0))
    vec = pl.BlockSpec((1, Fd), lambda i: (0, 0))
    in_specs = [row, vec, row] + ([row] if has_res else [])
    args = (x, g, dy) + ((dres,) if has_res else ())
    return pl.pallas_call(
        body, name=name, grid=(T // tm,),
        in_specs=in_specs, out_specs=[row, row, vec],
        out_shape=[jax.ShapeDtypeStruct((T, Fd), F32), jax.ShapeDtypeStruct((T, Fd), BF16),
                   jax.ShapeDtypeStruct((1, Fd), F32)],
        compiler_params=_params(("arbitrary",)),
    )(*args)


def _loss_head(h, g, target, name="loss_head"):
    T, Fd = h.shape
    tm = _pick(T, 256)

    def body(h_ref, g_ref, t_ref, loss_ref, dh_ref, dhb_ref, dg_ref):
        xv = h_ref[...]
        r = lax.rsqrt(jnp.mean(xv * xv, axis=-1, keepdims=True) + RMS_EPS)
        diff = xv * r * g_ref[...] - t_ref[...]
        part = 0.5 * jnp.sum(jnp.mean(diff * diff, axis=-1, keepdims=True), axis=0, keepdims=True)
        dyv = diff * (1.0 / Fd)
        gdy = dyv * g_ref[...]
        dot = jnp.mean(xv * gdy, axis=-1, keepdims=True)
        dh = r * gdy - xv * (r * r * r * dot)
        dh_ref[...] = dh
        dhb_ref[...] = dh.astype(BF16)

        @pl.when(pl.program_id(0) == 0)
        def _():
            dg_ref[...] = jnp.zeros_like(dg_ref)
            loss_ref[...] = jnp.zeros_like(loss_ref)

        dg_ref[...] += jnp.sum(dyv * xv * r, axis=0, keepdims=True)
        loss_ref[...] += jnp.broadcast_to(part, loss_ref.shape)

    row = pl.BlockSpec((tm, Fd), lambda i: (i, 0))
    vec = pl.BlockSpec((1, Fd), lambda i: (0, 0))
    return pl.pallas_call(
        body, name=name, grid=(T // tm,),
        in_specs=[row, vec, row],
        out_specs=[pl.BlockSpec((1, LANES), lambda i: (0, 0)), row, row, vec],
        out_shape=[jax.ShapeDtypeStruct((1, LANES), F32), jax.ShapeDtypeStruct((T, Fd), F32),
                   jax.ShapeDtypeStruct((T, Fd), BF16), jax.ShapeDtypeStruct((1, Fd), F32)],
        compiler_params=_params(("arbitrary",)),
    )(h, g, target)


FFN_TF = 256


def _ffn_fwd(h, g, wg_t, wu_t, wd, name="ffn_fwd"):
    T, Dm = h.shape
    Fh = wd.shape[0]
    tm = _pick(T, 1024)
    nf = Fh // FFN_TF

    def body(h_ref, g_ref, wg_ref, wu_ref, wd_ref, o_ref, u_ref, a_ref, b_ref):
        j = pl.program_id(1)

        @pl.when(j == 0)
        def _():
            xv = h_ref[...]
            r = lax.rsqrt(jnp.mean(xv * xv, axis=-1, keepdims=True) + RMS_EPS)
            u_ref[...] = (xv * r * g_ref[...]).astype(BF16)
            o_ref[...] = xv

        u = u_ref[...]
        a = _dot(u, wg_ref[...], NT).astype(BF16)
        b = _dot(u, wu_ref[...], NT).astype(BF16)
        a_ref[...] = a
        b_ref[...] = b
        af = a.astype(F32)
        s = (af * jax.nn.sigmoid(af) * b.astype(F32)).astype(BF16)
        o_ref[...] += _dot(s, wd_ref[...], NN)

    row = pl.BlockSpec((tm, Dm), lambda i, j: (i, 0))
    wblk = pl.BlockSpec((FFN_TF, Dm), lambda i, j: (j, 0))
    ablk = pl.BlockSpec((tm, FFN_TF), lambda i, j: (i, j))
    return pl.pallas_call(
        body, name=name, grid=(T // tm, nf),
        in_specs=[row, pl.BlockSpec((1, Dm), lambda i, j: (0, 0)), wblk, wblk, wblk],
        out_specs=[row, row, ablk, ablk],
        out_shape=[jax.ShapeDtypeStruct((T, Dm), F32), jax.ShapeDtypeStruct((T, Dm), BF16),
                   jax.ShapeDtypeStruct((T, Fh), BF16), jax.ShapeDtypeStruct((T, Fh), BF16)],
        compiler_params=_params(("parallel", "arbitrary")),
    )(h, g, wg_t, wu_t, wd)


def _ffn_bwd(dh, u, a, b, wg_t, wu_t, wd, name="ffn_bwd"):
    T, Dm = dh.shape
    Fh = wd.shape[0]
    nf = Fh // FFN_TF
    once = pl.Buffered(1)

    def body(dh_ref, u_ref, a_ref, b_ref, wg_ref, wu_ref, wd_ref, du_ref, dwg_ref, dwu_ref, dwd_ref):
        j = pl.program_id(0)

        @pl.when(j == 0)
        def _():
            du_ref[...] = jnp.zeros_like(du_ref)

        ds = _dot(dh_ref[...], wd_ref[...], NT)
        af, bf = a_ref[...].astype(F32), b_ref[...].astype(F32)
        sig = jax.nn.sigmoid(af)
        sa = af * sig
        dwd_ref[...] = _dot((sa * bf).astype(BF16), dh_ref[...], TN).astype(BF16)
        dab = jnp.concatenate([(ds * bf * (sig * (1.0 + af * (1.0 - sig)))).astype(BF16),
                               (ds * sa).astype(BF16)], axis=1)
        dw = _dot(dab, u_ref[...], TN)
        dwg_ref[...] = dw[:FFN_TF].astype(BF16)
        dwu_ref[...] = dw[FFN_TF:].astype(BF16)
        du_ref[...] += _dot(dab, jnp.concatenate([wg_ref[...], wu_ref[...]], axis=0), NN)

    full = lambda: pl.BlockSpec((T, Dm), lambda j: (0, 0), pipeline_mode=once)
    wblk = pl.BlockSpec((FFN_TF, Dm), lambda j: (j, 0))
    ablk = pl.BlockSpec((T, FFN_TF), lambda j: (0, j))
    return pl.pallas_call(
        body, name=name, grid=(nf,),
        in_specs=[full(), full(), ablk, ablk, wblk, wblk, wblk],
        out_specs=[pl.BlockSpec((T, Dm), lambda j: (0, 0)), wblk, wblk, wblk],
        out_shape=[jax.ShapeDtypeStruct((T, Dm), F32)] + [jax.ShapeDtypeStruct((Fh, Dm), BF16)] * 3,
        compiler_params=_params(("arbitrary",)),
    )(dh, u, a, b, wg_t, wu_t, wd)


def _rope(x, cos_t, sin_t, col0, ncols, out_dtype, name="rope"):
    T = x.shape[0]
    wt = cos_t.shape[1]
    tm = _pick(T, 256)
    nb = ncols * LANES // wt
    half = MLA_ROPE // 2

    def body(x_ref, c_ref, s_ref, o_ref):
        xv = x_ref[...].astype(F32)
        lane = lax.broadcasted_iota(jnp.int32, xv.shape, 1)
        first = (lane & (MLA_ROPE - 1)) < half
        swapped = jnp.where(first, pltpu.roll(xv, wt - half, 1), pltpu.roll(xv, half, 1))
        o_ref[...] = (xv * c_ref[...] + swapped * s_ref[...]).astype(out_dtype)

    off = col0 * LANES // wt
    return pl.pallas_call(
        body, name=name, grid=(T // tm, nb),
        in_specs=[pl.BlockSpec((tm, wt), lambda i, j: (i, j + off)),
                  pl.BlockSpec((tm, wt), lambda i, j: (i, 0)),
                  pl.BlockSpec((tm, wt), lambda i, j: (i, 0))],
        out_specs=pl.BlockSpec((tm, wt), lambda i, j: (i, j)),
        out_shape=jax.ShapeDtypeStruct((T, ncols * LANES), out_dtype),
        compiler_params=_params(("parallel", "parallel")),
    )(x, cos_t, sin_t)


ATT_T = 256


def _mla_masks(shape):
    lane = lax.broadcasted_iota(jnp.int32, shape, 1)
    m0 = (lane < 64) | ((lane >= 128) & (lane < 160))
    m1 = ((lane >= 64) & (lane < 128)) | ((lane >= 160) & (lane < 192))
    return m0, m1


def _chunk_ok(tq, tk):
    row = lax.broadcasted_iota(jnp.int32, (tq, tk), 0)
    col = lax.broadcasted_iota(jnp.int32, (tq, tk), 1)
    return (col >> 6) <= (row >> 6)


def _mla_fwd(q, kv, kr, name="mla_fwd"):
    T = q.shape[0]
    tq = tk = _pick(T, ATT_T)
    npair = MLA_HEADS // 2
    scale = (MLA_NOPE + MLA_ROPE) ** -0.5

    def body(q_ref, kn_ref, v_ref, kr_ref, o_ref, lse_ref):
        m_idx = pl.program_id(1)
        qv = q_ref[...]
        m0, m1 = _mla_masks(qv.shape)
        qh = (jnp.where(m0, qv, 0).astype(BF16), jnp.where(m1, qv, 0).astype(BF16))

        def block(kb, carry, ok):
            ks = pl.ds(pl.multiple_of(kb * tk, tk), tk)
            kcat = jnp.concatenate([kn_ref[ks, :], kr_ref[ks, :]], axis=1)
            vv = v_ref[ks, :]
            out = []
            for h in range(2):
                mx, l, acc = carry[3 * h:3 * h + 3]
                s = _dot(qh[h], kcat, NT) * scale
                if ok is not None:
                    s = jnp.where(ok, s, NEG)
                mn = jnp.maximum(mx, jnp.max(s, axis=-1, keepdims=True))
                alpha = jnp.exp(mx - mn)
                p = jnp.exp(s - mn)
                l = alpha * l + jnp.sum(p, axis=-1, keepdims=True)
                acc = alpha * acc + _dot(p.astype(BF16), vv, NN)
                out += [mn, l, acc]
            return tuple(out)

        init = (jnp.full((tq, 1), NEG, F32), jnp.zeros((tq, 1), F32), jnp.zeros((tq, LANES), F32)) * 2
        res = block(m_idx, init, _chunk_ok(tq, tk))
        res = lax.fori_loop(0, m_idx, lambda kb, c: block(kb, c, None), res)
        lane = lax.broadcasted_iota(jnp.int32, (tq, LANES), 1)
        o0 = res[2] / res[1]
        o1 = res[5] / res[4]
        o_ref[...] = jnp.where(lane < 64, o0, o1).astype(o_ref.dtype)
        lse_ref[...] = jnp.where(lane < 64, res[0] + jnp.log(res[1]), res[3] + jnp.log(res[4]))

    full = lambda col: pl.BlockSpec((T, LANES), col)
    return pl.pallas_call(
        body, name=name, grid=(npair, T // tq),
        in_specs=[pl.BlockSpec((tq, 2 * LANES), lambda p, m: (m, p)),
                  full(lambda p, m: (0, p)), full(lambda p, m: (0, npair + p)), full(lambda p, m: (0, 0))],
        out_specs=[pl.BlockSpec((tq, LANES), lambda p, m: (m, p)),
                   pl.BlockSpec((tq, LANES), lambda p, m: (m, p))],
        out_shape=[jax.ShapeDtypeStruct((T, npair * LANES), BF16),
                   jax.ShapeDtypeStruct((T, npair * LANES), F32)],
        compiler_params=_params(("parallel", "arbitrary")),
    )(q, kv, kv, kr)


def _mla_bwd(q, kv, kr, o, lse, do, do_col0, name="mla_bwd"):
    T = q.shape[0]
    tq = tk = _pick(T, ATT_T)
    npair = MLA_HEADS // 2
    scale = (MLA_NOPE + MLA_ROPE) ** -0.5

    def body(q_ref, kn_ref, v_ref, kr_ref, o_ref, lse_ref, do_ref, dq_ref, dkn_ref, dv_ref, dkr_ref):
        p_idx, m_idx = pl.program_id(0), pl.program_id(1)

        @pl.when(m_idx == 0)
        def _():
            dkn_ref[...] = jnp.zeros_like(dkn_ref)
            dv_ref[...] = jnp.zeros_like(dv_ref)

        @pl.when((m_idx == 0) & (p_idx == 0))
        def _():
            dkr_ref[...] = jnp.zeros_like(dkr_ref)

        qv = q_ref[...]
        m0, m1 = _mla_masks(qv.shape)
        qh = (jnp.where(m0, qv, 0).astype(BF16), jnp.where(m1, qv, 0).astype(BF16))
        dov = do_ref[...].astype(F32)
        lane = lax.broadcasted_iota(jnp.int32, (tq, LANES), 1)
        h0 = lane < 64
        prod = dov * o_ref[...].astype(F32)
        delta = (jnp.sum(jnp.where(h0, prod, 0.0), axis=-1, keepdims=True),
                 jnp.sum(jnp.where(h0, 0.0, prod), axis=-1, keepdims=True))
        doh = (jnp.where(h0, dov, 0.0).astype(BF16), jnp.where(h0, 0.0, dov).astype(BF16))
        lsev = lse_ref[...]
        lse_h = (lsev[:, 0:1], lsev[:, 64:65])

        def block(kb, carry, ok):
            ks = pl.ds(pl.multiple_of(kb * tk, tk), tk)
            kcat = jnp.concatenate([kn_ref[ks, :], kr_ref[ks, :]], axis=1)
            vv = v_ref[ks, :]
            dkc = jnp.zeros((tk, 2 * LANES), F32)
            dvv = jnp.zeros((tk, LANES), F32)
            out = []
            for h in range(2):
                s = _dot(qh[h], kcat, NT) * scale
                p = jnp.exp(s - lse_h[h])
                if ok is not None:
                    p = jnp.where(ok, p, 0.0)
                dp = _dot(doh[h], vv, NT)
                ds = (p * (dp - delta[h]) * scale).astype(BF16)
                out.append(carry[h] + _dot(ds, kcat, NN))
                dkc = dkc + _dot(ds, qh[h], TN)
                dvv = dvv + _dot(p.astype(BF16), doh[h], TN)
            dkn_ref[ks, :] += dkc[:, :LANES]
            dkr_ref[ks, :] += dkc[:, LANES:]
            dv_ref[ks, :] += dvv
            return tuple(out)

        init = (jnp.zeros((tq, 2 * LANES), F32),) * 2
        res = block(m_idx, init, _chunk_ok(tq, tk))
        dq0, dq1 = lax.fori_loop(0, m_idx, lambda kb, c: block(kb, c, None), res)
        dq_ref[...] = jnp.where(m0, dq0, jnp.where(m1, dq1, 0.0))

    full = lambda col: pl.BlockSpec((T, LANES), col)
    blk = lambda col: pl.BlockSpec((tq, LANES), col)
    return pl.pallas_call(
        body, name=name, grid=(npair, T // tq),
        in_specs=[pl.BlockSpec((tq, 2 * LANES), lambda p, m: (m, p)),
                  full(lambda p, m: (0, p)), full(lambda p, m: (0, npair + p)), full(lambda p, m: (0, 0)),
                  blk(lambda p, m: (m, p)), blk(lambda p, m: (m, p)),
                  blk(lambda p, m: (m, do_col0 + p))],
        out_specs=[pl.BlockSpec((tq, 2 * LANES), lambda p, m: (m, p)),
                   full(lambda p, m: (0, p)), full(lambda p, m: (0, p)), full(lambda p, m: (0, 0))],
        out_shape=[jax.ShapeDtypeStruct((T, npair * 2 * LANES), F32),
                   jax.ShapeDtypeStruct((T, npair * LANES), F32),
                   jax.ShapeDtypeStruct((T, npair * LANES), F32),
                   jax.ShapeDtypeStruct((T, LANES), F32)],
        compiler_params=_params(("arbitrary", "arbitrary")),
    )(q, kv, kv, kr, o, lse, do)


def _split_dot(x, tri):
    hi = x.astype(BF16)
    lo = (x - hi.astype(F32)).astype(BF16)
    return _dot(hi, tri, NN) + _dot(lo, tri, NN)


def _sb_terms(qh, kk, before):
    z = _dot(qh, kk, NT)
    sp = jnp.maximum(z, 0.0) + jnp.log(1.0 + jnp.exp(-jnp.abs(z)))
    lk = -sp if before is None else jnp.where(before, -sp, 0.0)
    return z, sp, lk


def _sb_setup(q_ref, tq, tk, scale):
    qv = (q_ref[...].astype(F32) * scale).astype(BF16)
    lane = lax.broadcasted_iota(jnp.int32, (tq, LANES), 1)
    h0 = lane < 64
    qh = (jnp.where(h0, qv, 0).astype(BF16), jnp.where(h0, 0, qv).astype(BF16))
    row = lax.broadcasted_iota(jnp.int32, (tk, tk), 0)
    col = lax.broadcasted_iota(jnp.int32, (tk, tk), 1)
    return qh, h0, row, col


def _sb_fwd(qkv, col0, name="sb_fwd"):
    T = qkv.shape[0]
    tq = tk = _pick(T, ATT_T)
    npair = SB_HEADS // 2
    scale = SB_DIM ** -0.5

    def body(q_ref, k_ref, v_ref, o_ref, o32_ref):
        m_idx = pl.program_id(1)
        qh, h0, row, col = _sb_setup(q_ref, tq, tk, scale)
        later = (row > col).astype(BF16)

        def block(kb, carry, before):
            ks = pl.ds(pl.multiple_of(kb * tk, tk), tk)
            kk = k_ref[ks, :].astype(BF16)
            vv = v_ref[ks, :].astype(BF16)
            out = []
            for h in range(2):
                c, acc = carry[2 * h:2 * h + 2]
                z, sp, lk = _sb_terms(qh[h], kk, before)
                w = jnp.exp((z - sp) + _split_dot(lk, later) + c)
                if before is not None:
                    w = jnp.where(before, w, 0.0)
                out += [c + jnp.sum(lk, axis=-1, keepdims=True), acc + _dot(w.astype(BF16), vv, NN)]
            return tuple(out)

        init = (jnp.zeros((tq, 1), F32), jnp.zeros((tq, LANES), F32)) * 2
        res = block(m_idx, init, col < row)
        res = lax.fori_loop(1, m_idx + 1, lambda i, c: block(m_idx - i, c, None), res)
        o = jnp.where(h0, res[1], res[3])
        o_ref[...] = o.astype(o_ref.dtype)
        o32_ref[...] = o

    full = lambda col: pl.BlockSpec((T, LANES), col)
    blk = pl.BlockSpec((tq, LANES), lambda p, m: (m, p))
    return pl.pallas_call(
        body, name=name, grid=(npair, T // tq),
        in_specs=[pl.BlockSpec((tq, LANES), lambda p, m: (m, col0 + p)),
                  full(lambda p, m: (0, col0 + npair + p)), full(lambda p, m: (0, col0 + 2 * npair + p))],
        out_specs=[blk, blk],
        out_shape=[jax.ShapeDtypeStruct((T, npair * LANES), BF16), jax.ShapeDtypeStruct((T, npair * LANES), F32)],
        compiler_params=_params(("parallel", "arbitrary")),
    )(qkv, qkv, qkv)


def _sb_bwd(qkv, col0, o32, do, do_col0, dep, name="sb_bwd"):
    T = qkv.shape[0]
    tq = tk = _pick(T, ATT_T)
    npair = SB_HEADS // 2
    scale = SB_DIM ** -0.5

    def body(q_ref, k_ref, v_ref, o_ref, do_ref, dep_ref, dq_ref, dk_ref, dv_ref):
        m_idx = pl.program_id(1)

        @pl.when(m_idx == 0)
        def _():
            dk_ref[...] = jnp.zeros_like(dk_ref)
            dv_ref[...] = jnp.zeros_like(dv_ref)

        qh, h0, row, col = _sb_setup(q_ref, tq, tk, scale)
        dov = do_ref[...].astype(F32)
        doh = (jnp.where(h0, dov, 0.0).astype(BF16), jnp.where(h0, 0.0, dov).astype(BF16))
        ov = o_ref[...]
        etot = [jnp.sum(doh[h].astype(F32) * ov, axis=-1, keepdims=True) for h in range(2)]
        later = (row > col).astype(BF16)
        from_here = (row >= col).astype(BF16)

        def block(kb, carry, before):
            ks = pl.ds(pl.multiple_of(kb * tk, tk), tk)
            kk = k_ref[ks, :].astype(BF16)
            vv = v_ref[ks, :].astype(BF16)
            dkk = jnp.zeros((tk, LANES), F32)
            dvv = jnp.zeros((tk, LANES), F32)
            out = []
            for h in range(2):
                c, es, dqa = carry[3 * h:3 * h + 3]
                z, sp, lk = _sb_terms(qh[h], kk, before)
                w = jnp.exp((z - sp) + _split_dot(lk, later) + c)
                if before is not None:
                    w = jnp.where(before, w, 0.0)
                wb = w.astype(BF16)
                e = wb.astype(F32) * _dot(doh[h], vv, NT)
                prev = etot[h] - (_split_dot(e, from_here) + es)
                sig_neg = jnp.exp(-sp)
                dz = e * sig_neg - (1.0 - sig_neg) * prev
                if before is not None:
                    dz = jnp.where(before, dz, 0.0)
                dzb = dz.astype(BF16)
                dkk = dkk + _dot(dzb, qh[h], TN)
                dvv = dvv + _dot(wb, doh[h], TN)
                out += [c + jnp.sum(lk, axis=-1, keepdims=True), es + jnp.sum(e, axis=-1, keepdims=True),
                        dqa + _dot(dzb, kk, NN)]
            dk_ref[ks, :] += dkk
            dv_ref[ks, :] += dvv
            return tuple(out)

        init = (jnp.zeros((tq, 1), F32), jnp.zeros((tq, 1), F32), jnp.zeros((tq, LANES), F32)) * 2
        res = block(m_idx, init, col < row)
        res = lax.fori_loop(1, m_idx + 1, lambda i, c: block(m_idx - i, c, None), res)
        dq_ref[...] = jnp.where(h0, res[2], res[5]) * scale

    full = lambda col: pl.BlockSpec((T, LANES), col)
    blk = lambda col: pl.BlockSpec((tq, LANES), col)
    return pl.pallas_call(
        body, name=name, grid=(npair, T // tq),
        in_specs=[blk(lambda p, m: (m, col0 + p)),
                  full(lambda p, m: (0, col0 + npair + p)), full(lambda p, m: (0, col0 + 2 * npair + p)),
                  blk(lambda p, m: (m, p)), blk(lambda p, m: (m, do_col0 + p)),
                  pl.BlockSpec((8, LANES), lambda p, m: (0, 0))],
        out_specs=[blk(lambda p, m: (m, p)), full(lambda p, m: (0, p)), full(lambda p, m: (0, p))],
        out_shape=[jax.ShapeDtypeStruct((T, npair * LANES), F32)] * 3,
        compiler_params=_params(("arbitrary", "arbitrary")),
    )(qkv, qkv, qkv, o32, do, dep)


def _band_in_window():
    cq = lax.broadcasted_iota(jnp.int32, (BAND_TQ, BAND_W), 0) >> 6
    ckp = lax.broadcasted_iota(jnp.int32, (BAND_TQ, BAND_W), 1) >> 6
    return (ckp >= cq) & (ckp <= cq + LEFT_CHUNKS)


def _band_real(m_idx):
    j = lax.broadcasted_iota(jnp.int32, (BAND_TQ, BAND_W), 1)
    return j >= PAD_KEYS - m_idx * BAND_TQ


def _band_probs(qh, kw, bias, real, scale):
    s = jnp.where(real, _dot(qh, kw, NT) * scale + bias, NEG)
    e = jnp.exp(s - jnp.max(s, axis=-1, keepdims=True))
    return e / jnp.sum(e, axis=-1, keepdims=True)


def _band_fwd(qkv, k_pad, v_pad, bias_w, name="band_fwd"):
    T = qkv.shape[0]
    npair = C_HEADS // 2
    scale = C_DIM ** -0.5

    def body(q_ref, k_ref, v_ref, b_ref, o_ref):
        m_idx = pl.program_id(1)
        win = pl.ds(pl.multiple_of(m_idx * BAND_TQ, BAND_TQ), BAND_W)
        kw, vw = k_ref[win, :], v_ref[win, :]
        qv = q_ref[...]
        lane = lax.broadcasted_iota(jnp.int32, (BAND_TQ, LANES), 1)
        h0 = lane < 64
        qh = (jnp.where(h0, qv, 0).astype(BF16), jnp.where(h0, 0, qv).astype(BF16))
        real = _band_real(m_idx)
        o = [_dot(_band_probs(qh[h], kw, b_ref[h], real, scale).astype(BF16), vw, NN) for h in range(2)]
        o_ref[...] = jnp.where(h0, o[0], o[1]).astype(o_ref.dtype)

    Tp = T + PAD_KEYS
    return pl.pallas_call(
        body, name=name, grid=(npair, T // BAND_TQ),
        in_specs=[pl.BlockSpec((BAND_TQ, LANES), lambda p, m: (m, p)),
                  pl.BlockSpec((Tp, LANES), lambda p, m: (0, p)),
                  pl.BlockSpec((Tp, LANES), lambda p, m: (0, p)),
                  pl.BlockSpec((2, BAND_TQ, BAND_W), lambda p, m: (p, 0, 0))],
        out_specs=pl.BlockSpec((BAND_TQ, LANES), lambda p, m: (m, p)),
        out_shape=jax.ShapeDtypeStruct((T, npair * LANES), BF16),
        compiler_params=_params(("parallel", "arbitrary")),
    )(qkv, k_pad, v_pad, bias_w)


def _band_bwd(qkv, k_pad, v_pad, bias_w, do, name="band_bwd"):
    T = qkv.shape[0]
    npair = C_HEADS // 2
    scale = C_DIM ** -0.5

    def body(q_ref, k_ref, v_ref, b_ref, do_ref, dq_ref, dk_ref, dv_ref, db_ref):
        m_idx = pl.program_id(1)

        @pl.when(m_idx == 0)
        def _():
            dk_ref[...] = jnp.zeros_like(dk_ref)
            dv_ref[...] = jnp.zeros_like(dv_ref)
            db_ref[...] = jnp.zeros_like(db_ref)

        win = pl.ds(pl.multiple_of(m_idx * BAND_TQ, BAND_TQ), BAND_W)
        kw, vw = k_ref[win, :], v_ref[win, :]
        qv = q_ref[...]
        dov = do_ref[...].astype(F32)
        lane = lax.broadcasted_iota(jnp.int32, (BAND_TQ, LANES), 1)
        h0 = lane < 64
        qh = (jnp.where(h0, qv, 0).astype(BF16), jnp.where(h0, 0, qv).astype(BF16))
        doh = (jnp.where(h0, dov, 0.0).astype(BF16), jnp.where(h0, 0.0, dov).astype(BF16))
        real = _band_real(m_idx)
        dq = []
        dkw = jnp.zeros((BAND_W, LANES), F32)
        dvw = jnp.zeros((BAND_W, LANES), F32)
        for h in range(2):
            p = _band_probs(qh[h], kw, b_ref[h], real, scale)
            dp = _dot(doh[h], vw, NT)
            dsb = p * (dp - jnp.sum(p * dp, axis=-1, keepdims=True))
            db_ref[h] += dsb
            dsq = (dsb * scale).astype(BF16)
            dq.append(_dot(dsq, kw, NN))
            dkw = dkw + _dot(dsq, qh[h], TN)
            dvw = dvw + _dot(p.astype(BF16), doh[h], TN)
        dq_ref[...] = jnp.where(h0, dq[0], dq[1])
        dk_ref[win, :] += dkw
        dv_ref[win, :] += dvw

    Tp = T + PAD_KEYS
    blk = lambda col: pl.BlockSpec((BAND_TQ, LANES), col)
    full = pl.BlockSpec((Tp, LANES), lambda p, m: (0, p))
    bias = pl.BlockSpec((2, BAND_TQ, BAND_W), lambda p, m: (p, 0, 0))
    return pl.pallas_call(
        body, name=name, grid=(npair, T // BAND_TQ),
        in_specs=[blk(lambda p, m: (m, p)), full, full, bias, blk(lambda p, m: (m, p))],
        out_specs=[blk(lambda p, m: (m, p)), full, full, bias],
        out_shape=[jax.ShapeDtypeStruct((T, npair * LANES), F32),
                   jax.ShapeDtypeStruct((Tp, npair * LANES), F32),
                   jax.ShapeDtypeStruct((Tp, npair * LANES), F32),
                   jax.ShapeDtypeStruct((C_HEADS, BAND_TQ, BAND_W), F32)],
        compiler_params=_params(("arbitrary", "arbitrary")),
    )(qkv, k_pad, v_pad, bias_w, do)


def _skew_bits(x, left):
    w = x.shape[1]
    row = lax.broadcasted_iota(jnp.int32, x.shape, 0)
    for b in range(BAND_TQ.bit_length() - 1):
        amt = (w - (1 << b)) if left else (1 << b)
        x = jnp.where((row >> b) & 1 == 1, pltpu.roll(x, amt, 1), x)
    return x


def _toeplitz(diag, name="toeplitz"):
    H = diag.shape[0]

    def body(d_ref, o_ref):
        x = jnp.broadcast_to(d_ref[0], (BAND_TQ, TOEP_W))
        o_ref[0] = jnp.where(_band_in_window(), _skew_bits(x, left=False)[:, BAND_TQ:], NEG)

    return pl.pallas_call(
        body, name=name, grid=(H,),
        in_specs=[pl.BlockSpec((1, 1, TOEP_W), lambda h: (h, 0, 0))],
        out_specs=pl.BlockSpec((1, BAND_TQ, BAND_W), lambda h: (h, 0, 0)),
        out_shape=jax.ShapeDtypeStruct((H, BAND_TQ, BAND_W), F32),
        compiler_params=_params(("parallel",)),
    )(diag.reshape(H, 1, TOEP_W))


def _toeplitz_bwd(dbias, name="toeplitz_bwd"):
    H = dbias.shape[0]

    def body(d_ref, o_ref):
        x = jnp.concatenate([jnp.zeros((BAND_TQ, BAND_TQ), F32), d_ref[0]], axis=1)
        o_ref[0] = jnp.sum(_skew_bits(x, left=True), axis=0, keepdims=True)

    return pl.pallas_call(
        body, name=name, grid=(H,),
        in_specs=[pl.BlockSpec((1, BAND_TQ, BAND_W), lambda h: (h, 0, 0))],
        out_specs=pl.BlockSpec((1, 1, TOEP_W), lambda h: (h, 0, 0)),
        out_shape=jax.ShapeDtypeStruct((H, 1, TOEP_W), F32),
        compiler_params=_params(("parallel",)),
    )(dbias).reshape(H, TOEP_W)


_HBM = pl.BlockSpec(memory_space=pltpu.HBM)
_SEM = pl.BlockSpec(memory_space=pltpu.SEMAPHORE)
_EFFECT = pltpu.SideEffectType.DATAFLOW_SIDE_EFFECTING


def _peers():
    x, y, c = lax.axis_index("x"), lax.axis_index("y"), lax.axis_index("c")
    out = []
    for k in range(1, N_DEV):
        peer = (1 - x if (k >> 2) & 1 else x, 1 - y if (k >> 1) & 1 else y, 1 - c if k & 1 else c)
        out.append((peer, 4 * peer[0] + 2 * peer[1] + peer[2]))
    return 4 * x + 2 * y + c, out


def _split_copies(ins, lands, scatter, send_sem, recv_sem, arriving):
    me, peers = _peers()
    out = []
    for a in range(len(ins)):
        for peer, idx in peers:
            out.append(pltpu.make_async_remote_copy(
                src_ref=ins[a].at[idx] if scatter[a] else ins[a],
                dst_ref=lands[a].at[idx if arriving else me], send_sem=send_sem, recv_sem=recv_sem,
                device_id=peer, device_id_type=pl.DeviceIdType.MESH))
    return out


def _landing_zones(arrays, scatter):
    return [lax.empty((N_DEV,) + (a.shape[1:] if s else a.shape), a.dtype) for a, s in zip(arrays, scatter)]


def _place_own(arrays, scatter, name):
    n = len(arrays)
    lands = _landing_zones(arrays, scatter)
    me = (4 * lax.axis_index("x") + 2 * lax.axis_index("y") + lax.axis_index("c")).astype(jnp.int32).reshape(1)

    def body(me_ref, *refs):
        for a in range(n):
            refs[2 * n + a][...] = refs[a][...].reshape(refs[2 * n + a].shape)

    def row_spec(shape):
        zeros = (0,) * (len(shape) - 1)
        return pl.BlockSpec((1,) + tuple(shape[1:]), lambda i, me_ref: (me_ref[0],) + zeros)

    in_specs = [row_spec(a.shape) if s else pl.BlockSpec(a.shape, lambda i, me_ref, nd=a.ndim: (0,) * nd)
                for a, s in zip(arrays, scatter)]
    return pl.pallas_call(
        body, name=name,
        out_shape=[jax.ShapeDtypeStruct(l.shape, l.dtype) for l in lands],
        grid_spec=pltpu.PrefetchScalarGridSpec(
            num_scalar_prefetch=1, grid=(1,),
            in_specs=in_specs + [pl.BlockSpec(memory_space=pl.ANY)] * n,
            out_specs=[row_spec(l.shape) for l in lands]),
        input_output_aliases={1 + n + i: i for i in range(n)},
        compiler_params=_params(("arbitrary",)),
    )(me, *arrays, *lands)


def _exchange_start(arrays, scatter, after, name):
    n = len(arrays)
    lands = list(_place_own(arrays, scatter, name=name.replace("_start_", "_own_")))

    def body(*refs):
        ins, lnd = refs[:n], refs[n:2 * n]
        send_sem, recv_sem = refs[2 * n + 1:2 * n + 3]
        token = refs[-1]
        for cp in _split_copies(ins, lnd, scatter, send_sem, recv_sem, arriving=False):
            cp.start()
        token[...] = jnp.zeros_like(token)

    hbm = lambda a: pltpu.HBM(a.shape, a.dtype)
    out = pl.pallas_call(
        body, name=name,
        out_shape=(pltpu.SemaphoreType.DMA(()), pltpu.SemaphoreType.DMA(()),
                   *[hbm(a) for a in arrays], *[hbm(a) for a in lands],
                   jax.ShapeDtypeStruct((8, LANES), F32)),
        in_specs=[_HBM] * (2 * n) + [pl.BlockSpec(memory_space=pl.ANY)],
        out_specs=(_SEM, _SEM, *([_HBM] * (2 * n)), pl.BlockSpec(memory_space=pltpu.VMEM)),
        input_output_aliases={i: 2 + i for i in range(2 * n)},
        compiler_params=pltpu.CompilerParams(has_side_effects=_EFFECT),
    )(*[pltpu.with_memory_space_constraint(a, pltpu.HBM) for a in list(arrays) + lands], after)
    return (out[0], out[1], list(out[2:2 + n]), list(out[2 + n:2 + 2 * n]), tuple(scatter)), out[-1]


def _exchange_wait(handle, after, name):
    send_sem, recv_sem, ins, lands, scatter = handle
    n = len(ins)

    def body(*refs):
        i_ref, l_ref = refs[:n], refs[n:2 * n]
        s_sem, r_sem = refs[2 * n:2 * n + 2]
        for cp in _split_copies(i_ref, l_ref, scatter, s_sem, r_sem, arriving=False):
            cp.wait_send()
        for cp in _split_copies(i_ref, l_ref, scatter, s_sem, r_sem, arriving=True):
            cp.wait_recv()

    hbm = lambda a: pltpu.HBM(a.shape, a.dtype)
    out = pl.pallas_call(
        body, name=name,
        out_shape=tuple(hbm(a) for a in ins + lands),
        in_specs=[_HBM] * (2 * n) + [_SEM, _SEM, pl.BlockSpec(memory_space=pl.ANY)],
        out_specs=tuple([_HBM] * (2 * n)),
        input_output_aliases={i: i for i in range(2 * n)},
        compiler_params=pltpu.CompilerParams(has_side_effects=_EFFECT),
    )(*ins, *lands, send_sem, recv_sem, after)
    return list(out[n:])


def _adamw(w, parts, m, v, name="adamw"):
    R, C = w.shape
    tr = next(t for t in (256, 128, 64, 32, 16, 8) if R % t == 0)
    c1 = 1.0 - ADAM_B1 ** ADAM_STEP
    c2 = 1.0 - ADAM_B2 ** ADAM_STEP

    def body(w_ref, p_ref, m_ref, v_ref, g_ref, d_ref, nm_ref, nv_ref):
        g = p_ref[0].astype(F32)
        for i in range(1, N_DEV):
            g = g + p_ref[i].astype(F32)
        nm = ADAM_B1 * m_ref[...] + (1.0 - ADAM_B1) * g
        nv = ADAM_B2 * v_ref[...] + (1.0 - ADAM_B2) * (g * g)
        g_ref[...] = g
        nm_ref[...] = nm
        nv_ref[...] = nv
        d_ref[...] = -ADAM_LR * ((nm / c1) / (jnp.sqrt(nv / c2) + ADAM_EPS) + ADAM_WD * w_ref[...])

    blk = pl.BlockSpec((tr, C), lambda i: (i, 0))
    return pl.pallas_call(
        body, name=name, grid=(R // tr,),
        in_specs=[blk, pl.BlockSpec((N_DEV, tr, C), lambda i: (0, i, 0)), blk, blk],
        out_specs=[blk] * 4,
        out_shape=[jax.ShapeDtypeStruct((R, C), F32)] * 4,
        compiler_params=_params(("parallel",)),
    )(w, parts, m, v)


_O1 = Q_LORA
_O2 = _O1 + KV_LORA
_O3 = _O2 + MLA_ROPE
_NB = SB_HEADS * SB_DIM
IN_W = _O2 + LANES + 3 * _NB
COL_KR = _O2 // LANES
COL_SB = COL_KR + 1


def _w_in_local(w):
    kr = w[_O2:_O3]
    pad = jnp.zeros((LANES - 2 * MLA_ROPE, w.shape[1]), w.dtype)
    return jnp.concatenate([w[:_O2], kr, kr, pad, w[_O3:]], axis=0)


def _w_in_grad(g):
    kr = (g[_O2:_O2 + MLA_ROPE].astype(F32) + g[_O2 + MLA_ROPE:_O2 + 2 * MLA_ROPE].astype(F32)).astype(g.dtype)
    return jnp.concatenate([g[:_O2], kr, g[_O2 + LANES:]], axis=0)


def _w_uq_local(w):
    w3 = w.reshape(MLA_HEADS // 2, 2, MLA_NOPE + MLA_ROPE, w.shape[1])
    nope = w3[:, :, :MLA_NOPE].reshape(MLA_HEADS // 2, 2 * MLA_NOPE, w.shape[1])
    rope = w3[:, :, MLA_NOPE:].reshape(MLA_HEADS // 2, 2 * MLA_ROPE, w.shape[1])
    pad = jnp.zeros((MLA_HEADS // 2, LANES - 2 * MLA_ROPE, w.shape[1]), w.dtype)
    return jnp.concatenate([nope, rope, pad], axis=1).reshape(-1, w.shape[1])


def _w_uq_grad(g):
    g3 = g.reshape(MLA_HEADS // 2, 2 * LANES, g.shape[1])
    nope = g3[:, :2 * MLA_NOPE].reshape(MLA_HEADS // 2, 2, MLA_NOPE, g.shape[1])
    rope = g3[:, LANES:LANES + 2 * MLA_ROPE].reshape(MLA_HEADS // 2, 2, MLA_ROPE, g.shape[1])
    return jnp.concatenate([nope, rope], axis=2).reshape(-1, g.shape[1])


def _w_ukv_local(w):
    w3 = w.reshape(MLA_HEADS, MLA_NOPE + MLA_V, w.shape[1])
    return jnp.concatenate([w3[:, :MLA_NOPE].reshape(-1, w.shape[1]),
                            w3[:, MLA_NOPE:].reshape(-1, w.shape[1])], axis=0)


def _w_ukv_grad(g):
    half = MLA_HEADS * MLA_NOPE
    kn = g[:half].reshape(MLA_HEADS, MLA_NOPE, g.shape[1])
    vv = g[half:].reshape(MLA_HEADS, MLA_V, g.shape[1])
    return jnp.concatenate([kn, vv], axis=1).reshape(-1, g.shape[1])


def _rope_tables(T):
    pos = jnp.arange(T, dtype=F32)
    inv_freq = ROPE_THETA ** (-jnp.arange(0, MLA_ROPE, 2, dtype=F32) / MLA_ROPE)
    ang = pos[:, None] * inv_freq[None, :]
    cos, sin = jnp.cos(ang), jnp.sin(ang)
    ones = jnp.ones((T, LANES - 2 * MLA_ROPE), F32)
    cos_k = jnp.concatenate([cos, cos, cos, cos, ones], axis=1)
    sin_k = jnp.concatenate([-sin, sin, -sin, sin, 0.0 * ones], axis=1)
    cos_q = jnp.concatenate([jnp.ones((T, LANES), F32), cos_k], axis=1)
    sin_q = jnp.concatenate([jnp.zeros((T, LANES), F32), sin_k], axis=1)
    return cos_q, sin_q, cos_k, sin_k


def _bias_diag_index():
    ell = np.arange(TOEP_W)
    return np.clip(BAND_W - ell, -REL_CLIP, REL_CLIP) + REL_CLIP


def _local_step(x, target, small, get_weights, put_grads):
    T = x.shape[0]
    cos_q, sin_q, cos_k, sin_k = _rope_tables(T)
    G = {}
    W = dict(small)

    u0 = _rms_fwd(x, W["g_mix"][0:1], name="rms_mix0")
    W.update(get_weights("in0", u0))
    proj = _mm(u0, W["w_in_t"], dims="nt", name="proj_in")
    W.update(get_weights("mix0", proj))
    c_q, c_kv = proj[:, :_O1], proj[:, _O1:_O2]
    nq = _rms_fwd(c_q, W["g_cq"], name="rms_cq")
    nkv = _rms_fwd(c_kv, W["g_ckv"], name="rms_ckv")
    qa_raw = _mm(nq, W["w_uq_t"], dims="nt", name="proj_uq")
    qa = _rope(qa_raw, cos_q, sin_q, 0, qa_raw.shape[1] // LANES, BF16, name="rope_q")
    kv = _mm(nkv, W["w_ukv_t"], dims="nt", out_dtype=BF16, name="proj_ukv")
    kr = _rope(proj, cos_k, sin_k, COL_KR, 1, BF16, name="rope_k")
    o_a, lse = _mla_fwd(qa, kv, kr)
    o_b, o_b32 = _sb_fwd(proj, COL_SB)
    o_ab = jnp.concatenate([o_a, o_b], axis=1)
    h1 = _mm(o_ab, W["ev_w_out"], res=x, name="out_ev")

    def ffn_fwd(h, layer):
        W.update(get_weights(f"ffn{layer}", h))
        return _ffn_fwd(h, W["g_ffn"][layer:layer + 1], W[f"w_gate_t{layer}"], W[f"w_up_t{layer}"],
                        W[f"w_down{layer}"], name=f"ffn_fwd{layer}")

    h2, u1, a0, b0 = ffn_fwd(h1, 0)

    W.update(get_weights("mix1", h2))
    u2 = _rms_fwd(h2, W["g_mix"][1:2], name="rms_mix1")
    qkv = _mm(u2, W["od_w_qkv_t"], dims="nt", out_dtype=BF16, name="proj_qkv")
    nc = C_HEADS * C_DIM
    pad = ((PAD_KEYS, 0), (0, 0))
    k_pad, v_pad = jnp.pad(qkv[:, nc:2 * nc], pad), jnp.pad(qkv[:, 2 * nc:], pad)
    diag_idx = _bias_diag_index()
    bias_w = _toeplitz(W["od_rel_bias"][:, diag_idx])
    o_c = _band_fwd(qkv, k_pad, v_pad, bias_w)
    h3 = _mm(o_c, W["od_w_out"], res=h2, name="out_od")
    h4, u3, a1, b1 = ffn_fwd(h3, 1)

    loss, dh, dhb, G["g_final"] = _loss_head(h4, W["g_final"], target)

    def ffn_bwd(dh, dhb, h, u, a, b, layer):
        du, g_gate, g_up, g_down = _ffn_bwd(dhb, u, a, b, W[f"w_gate_t{layer}"], W[f"w_up_t{layer}"],
                                            W[f"w_down{layer}"], name=f"ffn_bwd{layer}")
        tok = put_grads(f"ffn{layer}", {"w_gate_t": g_gate, "w_up_t": g_up, "w_down": g_down})
        return _rms_bwd(h, W["g_ffn"][layer:layer + 1] + tok[:1, :1], du, dres=dh, name=f"rms_ffn_bwd{layer}")

    dh3, dh3b, g_gffn1 = ffn_bwd(dh, dhb, h3, u3, a1, b1, 1)

    do_c = _mm(dh3b, W["od_w_out"], dims="nt", name="out_od_dx")
    g_od_out = _mm(o_c, dh3b, dims="tn", out_dtype=BF16, name="out_od_dw")
    dq_c, dk_p, dv_p, dbias_w = _band_bwd(qkv, k_pad, v_pad, bias_w, do_c)
    dqkv = jnp.concatenate([dq_c, dk_p[PAD_KEYS:], dv_p[PAD_KEYS:]], axis=1)
    du2 = _mm(dqkv, W["od_w_qkv_t"], name="proj_qkv_dx")
    tok = put_grads("mix1", {"od_w_qkv_t": _mm(dqkv, u2, dims="tn", out_dtype=BF16, name="proj_qkv_dw"),
                             "od_w_out": g_od_out})
    ddiag = _toeplitz_bwd(dbias_w)
    n_far = BAND_W - REL_CLIP + 1
    G["od_rel_bias"] = jnp.concatenate(
        [jnp.zeros((C_HEADS, REL_CLIP - BAND_TQ + 1), F32), ddiag[:, n_far:][:, ::-1],
         jnp.sum(ddiag[:, :n_far], axis=1, keepdims=True)], axis=1)
    dh2, dh2b, g_gmix1 = _rms_bwd(h2, W["g_mix"][1:2] + tok[:1, :1], du2, dres=dh3, name="rms_mix_bwd1")

    dh1, dh1b, g_gffn0 = ffn_bwd(dh2, dh2b, h1, u1, a0, b0, 0)
    G["g_ffn"] = jnp.concatenate([g_gffn0, g_gffn1], axis=0)

    do_ab = _mm(dh1b, W["ev_w_out"], dims="nt", name="out_ev_dx")
    g0 = {"ev_w_out": _mm(o_ab, dh1b, dims="tn", out_dtype=BF16, name="out_ev_dw")}
    dqa, dkn, dva, dkr = _mla_bwd(qa, kv, kr, o_a, lse, do_ab, 0)
    dqa_raw = _rope(dqa, cos_q, -sin_q, 0, dqa.shape[1] // LANES, F32, name="rope_q_bwd")
    g0["w_uq_t"] = _mm(dqa_raw, nq, dims="tn", name="proj_uq_dw")
    dnq = _mm(dqa_raw, W["w_uq_t"], name="proj_uq_dx")
    dc_q, _, G["g_cq"] = _rms_bwd(c_q, W["g_cq"], dnq, name="rms_cq_bwd")
    dkv = jnp.concatenate([dkn, dva], axis=1)
    g0["w_ukv_t"] = _mm(dkv, nkv, dims="tn", name="proj_ukv_dw")
    dnkv = _mm(dkv, W["w_ukv_t"], name="proj_ukv_dx")
    dc_kv, _, G["g_ckv"] = _rms_bwd(c_kv, W["g_ckv"], dnkv, name="rms_ckv_bwd")
    tok = put_grads("mix0", g0)
    dqb, dkb, dvb = _sb_bwd(proj, COL_SB, o_b32, do_ab, MLA_HEADS // 2, tok)
    dkr_raw = _rope(dkr, cos_k, -sin_k, 0, 1, F32, name="rope_k_bwd")
    dproj = jnp.concatenate([dc_q, dc_kv, dkr_raw, dqb, dkb, dvb], axis=1)
    du0 = _mm(dproj, W["w_in_t"], name="proj_in_dx")
    tok = put_grads("in0", {"w_in_t": _mm(dproj, u0, dims="tn", name="proj_in_dw")})
    dx, _, g_gmix0 = _rms_bwd(x, W["g_mix"][0:1] + tok[:1, :1], du0, dres=dh1, name="rms_mix_bwd0")
    G["g_mix"] = jnp.concatenate([g_gmix0, g_gmix1], axis=0)
    return loss[0, 0], dx, G


_BIG = ["ev_w_in", "ev_w_uq", "ev_w_ukv", "ev_w_out", "od_w_qkv", "od_w_out", "w_gate", "w_up", "w_down"]
_COL_SHARDED = {"ev_w_in", "ev_w_uq", "ev_w_ukv", "od_w_qkv", "w_gate", "w_up"}
_SMALL = ["ev_g_cq", "ev_g_ckv", "od_rel_bias", "g_mix", "g_ffn", "g_final"]
_GROUPS = {
    "in0": ["ev_w_in"],
    "mix0": ["ev_w_uq", "ev_w_ukv", "ev_w_out"],
    "ffn0": ["w_gate0", "w_up0", "w_down0"],
    "mix1": ["od_w_qkv", "od_w_out"],
    "ffn1": ["w_gate1", "w_up1", "w_down1"],
}
_GROUP_SRC = {n + str(l): (n, l) for n in ("w_gate", "w_up", "w_down") for l in (0, 1)}
_SMALL_ROWS = 8
_SMALL_COLS = 1792


def _pack_small(vals):
    flat = jnp.concatenate([v.reshape(-1).astype(F32) for v in vals])
    flat = jnp.pad(flat, (0, _SMALL_ROWS * _SMALL_COLS - flat.shape[0]))
    return flat.reshape(_SMALL_ROWS, _SMALL_COLS)


def _unpack_small(packed, like):
    flat = packed.reshape(-1)
    out, off = [], 0
    for v in like:
        out.append(flat[off:off + v.size].reshape(v.shape))
        off += v.size
    return out


def kernel(x, ev_w_in, ev_g_cq, ev_w_uq, ev_g_ckv, ev_w_ukv, ev_w_out, od_w_qkv, od_rel_bias, od_w_out, g_mix, g_ffn, w_gate, w_up, w_down, g_final, loss_target, m_ev_w_in, m_ev_g_cq, m_ev_w_uq, m_ev_g_ckv, m_ev_w_ukv, m_ev_w_out, m_od_w_qkv, m_od_rel_bias, m_od_w_out, m_g_mix, m_g_ffn, m_w_gate, m_w_up, m_w_down, m_g_final, v_ev_w_in, v_ev_g_cq, v_ev_w_uq, v_ev_g_ckv, v_ev_w_ukv, v_ev_w_out, v_od_w_qkv, v_od_rel_bias, v_od_w_out, v_g_mix, v_g_ffn, v_w_gate, v_w_up, v_w_down, v_g_final):
    args = dict(locals())
    w = {n: args[n] for n in _BIG + _SMALL}
    mom = {n: args["m_" + n] for n in _BIG + _SMALL}
    var = {n: args["v_" + n] for n in _BIG + _SMALL}

    own = {}
    for grp, names in _GROUPS.items():
        for n in names:
            base, layer = _GROUP_SRC.get(n, (n, 0))
            shard = w[base][layer:layer + 1]
            own[n] = (jnp.swapaxes(shard, 1, 2) if base in _COL_SHARDED else shard).astype(BF16)
    gather, token = {}, x[0, :8, :LANES]
    for grp, names in _GROUPS.items():
        gather[grp], token = _exchange_start([own[n] for n in names], [False] * len(names), token,
                                             name="gather_start_" + grp)

    def get_weights(grp, after):
        names = _GROUPS[grp]
        lands = _exchange_wait(gather[grp], token if after is None else after, name="gather_wait_" + grp)
        full = {n: l.reshape(-1, l.shape[-1]) for n, l in zip(names, lands)}
        if grp == "in0":
            return {"w_in_t": _w_in_local(full["ev_w_in"])}
        if grp == "mix0":
            return {"w_uq_t": _w_uq_local(full["ev_w_uq"]), "w_ukv_t": _w_ukv_local(full["ev_w_ukv"]),
                    "ev_w_out": full["ev_w_out"]}
        if grp == "mix1":
            return {"od_w_qkv_t": full["od_w_qkv"], "od_w_out": full["od_w_out"]}
        layer = grp[-1]
        return {"w_gate_t" + layer: full["w_gate" + layer], "w_up_t" + layer: full["w_up" + layer],
                "w_down" + layer: full["w_down" + layer]}

    scatter = {}

    def put_grads(grp, g):
        if grp == "in0":
            g = {"ev_w_in": _w_in_grad(g["w_in_t"])}
        elif grp == "mix0":
            g = {"ev_w_uq": _w_uq_grad(g["w_uq_t"]), "ev_w_ukv": _w_ukv_grad(g["w_ukv_t"]),
                 "ev_w_out": g["ev_w_out"]}
        elif grp == "mix1":
            g = {"od_w_qkv": g["od_w_qkv_t"], "od_w_out": g["od_w_out"]}
        else:
            layer = grp[-1]
            g = {"w_gate" + layer: g["w_gate_t"], "w_up" + layer: g["w_up_t"], "w_down" + layer: g["w_down"]}
        names = _GROUPS[grp]
        send = [g[n].reshape(N_DEV, 1, g[n].shape[0] // N_DEV, g[n].shape[1]).astype(BF16) for n in names]
        handle, tok = _exchange_start(send, [True] * len(names), send[0], name="scatter_start_" + grp)
        scatter[grp] = handle
        return tok

    small = {"g_cq": ev_g_cq, "g_ckv": ev_g_ckv, "od_rel_bias": od_rel_bias[0],
             "g_mix": g_mix + token[0, 0], "g_ffn": g_ffn, "g_final": g_final.reshape(1, -1)}
    loss_part, dx, G = _local_step(x[0], loss_target[0], small, get_weights, put_grads)
    loss = lax.psum(loss_part, ("x", "y", "c"))
    g_small = _pack_small([G["g_cq"], G["g_ckv"], G["od_rel_bias"], G["g_mix"], G["g_ffn"], G["g_final"]])
    small_handle, _ = _exchange_start([g_small], [False], dx, name="gather_start_small")

    grads, deltas, new_m, new_v = {}, {}, {}, {}
    parts, after = {}, dx

    def wait_parts(grp, after):
        lands = _exchange_wait(scatter[grp], after, name="scatter_wait_" + grp)
        for n, l in zip(_GROUPS[grp], lands):
            parts[n] = jnp.swapaxes(l, 2, 3) if _GROUP_SRC.get(n, (n, 0))[0] in _COL_SHARDED else l
        return lands[0]

    def adamw(n):
        shp = w[n].shape
        r2 = (-1, shp[-1])
        res = _adamw(w[n].reshape(r2), parts[n].reshape((N_DEV,) + w[n].reshape(r2).shape),
                     mom[n].reshape(r2), var[n].reshape(r2), name="adamw_" + n)
        grads[n], deltas[n], new_m[n], new_v[n] = [r.reshape(shp) for r in res]
        return res[0]

    for grp in ("ffn1", "mix1", "ffn0", "mix0"):
        after = wait_parts(grp, after)
    for n in ("w_gate", "w_up", "w_down"):
        parts[n] = jnp.concatenate([parts[n + "0"], parts[n + "1"]], axis=1)
    for n in _BIG[1:]:
        after = adamw(n)
    after = wait_parts("in0", after)
    after = adamw("ev_w_in")
    small_w = [w[n] for n in _SMALL]
    small_parts = _exchange_wait(small_handle, after, name="gather_wait_small")[0]
    res = _adamw(_pack_small(small_w), small_parts, _pack_small([mom[n] for n in _SMALL]),
                 _pack_small([var[n] for n in _SMALL]), name="adamw_small")
    for d, packed in zip((grads, deltas, new_m, new_v), res):
        for n, val in zip(_SMALL, _unpack_small(packed, small_w)):
            d[n] = val

    order = ["ev_w_in", "ev_g_cq", "ev_w_uq", "ev_g_ckv", "ev_w_ukv", "ev_w_out", "od_w_qkv", "od_rel_bias",
             "od_w_out", "g_mix", "g_ffn", "w_gate", "w_up", "w_down", "g_final"]
    out = [loss, dx[None]]
    for d in (grads, deltas, new_m, new_v):
        out += [d[n] for n in order]
    return tuple(out)
```

```python
import functools

import numpy as np
import jax
import jax.numpy as jnp
from jax import lax
from jax.experimental import pallas as pl
from jax.experimental.pallas import tpu as pltpu

F32 = jnp.float32
BF16 = jnp.bfloat16

D_MODEL = 1024
CHUNK = 64
MLA_HEADS = 8
MLA_NOPE = 64
MLA_ROPE = 32
MLA_V = 64
Q_LORA = 384
KV_LORA = 256
ROPE_THETA = 10000.0
SB_HEADS = 8
SB_DIM = 64
C_HEADS = 16
C_DIM = 64
LEFT_CHUNKS = 8
REL_CLIP = 256
D_FF = 2816
RMS_EPS = 1e-6
ADAM_LR = 0.001
ADAM_B1 = 0.9
ADAM_B2 = 0.999
ADAM_EPS = 1e-08
ADAM_WD = 0.01
ADAM_STEP = 10

N_DEV = 8
LANES = 128
VMEM_LIMIT = 56 * 1024 * 1024
NEG = -1e30
PAD_KEYS = LEFT_CHUNKS * CHUNK
BAND_TQ = 128
BAND_W = BAND_TQ + PAD_KEYS
TOEP_W = BAND_W + BAND_TQ

NN = (((1,), (0,)), ((), ()))
NT = (((1,), (1,)), ((), ()))
TN = (((0,), (0,)), ((), ()))


def _dot(a, b, dn):
    return lax.dot_general(a, b, dn, preferred_element_type=F32)


def _pick(dim, pref):
    if dim <= pref:
        return dim
    best = None
    for t in range(LANES, pref + 1, LANES):
        if dim % t == 0:
            best = t
    assert best is not None, (dim, pref)
    return best


def _params(sem):
    return pltpu.CompilerParams(dimension_semantics=sem, vmem_limit_bytes=VMEM_LIMIT)


def _mm(a, b, dims="nn", res=None, out_dtype=F32, name="mm"):
    if dims == "nn":
        (M, K), (K2, N) = a.shape, b.shape
    elif dims == "nt":
        (M, K), (N, K2) = a.shape, b.shape
    else:
        (K, M), (K2, N) = a.shape, b.shape
    assert K == K2, (a.shape, b.shape, dims)
    tm, tn, tk = _pick(M, 512), _pick(N, 1408), _pick(K, 1408)
    nk = K // tk
    dn = {"nn": NN, "nt": NT, "tn": TN}[dims]
    has_res = res is not None

    def body(*refs):
        if has_res:
            a_ref, b_ref, r_ref, o_ref, acc = refs
        else:
            a_ref, b_ref, o_ref, acc = refs
        k = pl.program_id(2)

        @pl.when(k == 0)
        def _():
            acc[...] = jnp.zeros_like(acc)

        acc[...] += _dot(a_ref[...].astype(BF16), b_ref[...].astype(BF16), dn)

        @pl.when(k == nk - 1)
        def _():
            r = acc[...]
            if has_res:
                r = r + r_ref[...]
            o_ref[...] = r.astype(out_dtype)

    a_spec = (pl.BlockSpec((tk, tm), lambda i, j, k: (k, i)) if dims == "tn"
              else pl.BlockSpec((tm, tk), lambda i, j, k: (i, k)))
    b_spec = (pl.BlockSpec((tn, tk), lambda i, j, k: (j, k)) if dims == "nt"
              else pl.BlockSpec((tk, tn), lambda i, j, k: (k, j)))
    o_spec = pl.BlockSpec((tm, tn), lambda i, j, k: (i, j))
    in_specs = [a_spec, b_spec] + ([o_spec] if has_res else [])
    args = (a, b) + ((res,) if has_res else ())
    return pl.pallas_call(
        body, name=name, grid=(M // tm, N // tn, nk),
        in_specs=in_specs, out_specs=o_spec,
        out_shape=jax.ShapeDtypeStruct((M, N), out_dtype),
        scratch_shapes=[pltpu.VMEM((tm, tn), F32)],
        compiler_params=_params(("parallel", "parallel", "arbitrary")),
    )(*args)


def _rms_fwd(x, g, out_dtype=BF16, name="rms_fwd"):
    T, Fd = x.shape
    tm = _pick(T, 256)

    def body(x_ref, g_ref, o_ref):
        xv = x_ref[...]
        r = lax.rsqrt(jnp.mean(xv * xv, axis=-1, keepdims=True) + RMS_EPS)
        o_ref[...] = (xv * r * g_ref[...]).astype(out_dtype)

    return pl.pallas_call(
        body, name=name, grid=(T // tm,),
        in_specs=[pl.BlockSpec((tm, Fd), lambda i: (i, 0)), pl.BlockSpec((1, Fd), lambda i: (0, 0))],
        out_specs=pl.BlockSpec((tm, Fd), lambda i: (i, 0)),
        out_shape=jax.ShapeDtypeStruct((T, Fd), out_dtype),
        compiler_params=_params(("parallel",)),
    )(x, g)


def _rms_bwd(x, g, dy, dres=None, name="rms_bwd"):
    T, Fd = x.shape
    tm = _pick(T, 256)
    has_res = dres is not None

    def body(*refs):
        if has_res:
            x_ref, g_ref, dy_ref, r_ref, dx_ref, dxb_ref, dg_ref = refs
        else:
            x_ref, g_ref, dy_ref, dx_ref, dxb_ref, dg_ref = refs
        xv, dyv = x_ref[...], dy_ref[...]
        r = lax.rsqrt(jnp.mean(xv * xv, axis=-1, keepdims=True) + RMS_EPS)
        gdy = dyv * g_ref[...]
        dot = jnp.mean(xv * gdy, axis=-1, keepdims=True)
        dx = r * gdy - xv * (r * r * r * dot)
        if has_res:
            dx = dx + r_ref[...]
        dx_ref[...] = dx
        dxb_ref[...] = dx.astype(BF16)

        @pl.when(pl.program_id(0) == 0)
        def _():
            dg_ref[...] = jnp.zeros_like(dg_ref)

        dg_ref[...] += jnp.sum(dyv * xv * r, axis=0, keepdims=True)

    row = pl.BlockSpec((tm, Fd), lambda i: (i, 0))
    vec = pl.BlockSpec((1, Fd), lambda i: (0, 0))
    in_specs = [row, vec, row] + ([row] if has_res else [])
    args = (x, g, dy) + ((dres,) if has_res else ())
    return pl.pallas_call(
        body, name=name, grid=(T // tm,),
        in_specs=in_specs, out_specs=[row, row, vec],
        out_shape=[jax.ShapeDtypeStruct((T, Fd), F32), jax.ShapeDtypeStruct((T, Fd), BF16),
                   jax.ShapeDtypeStruct((1, Fd), F32)],
        compiler_params=_params(("arbitrary",)),
    )(*args)


def _loss_head(h, g, target, name="loss_head"):
    T, Fd = h.shape
    tm = _pick(T, 256)

    def body(h_ref, g_ref, t_ref, loss_ref, dh_ref, dhb_ref, dg_ref):
        xv = h_ref[...]
        r = lax.rsqrt(jnp.mean(xv * xv, axis=-1, keepdims=True) + RMS_EPS)
        diff = xv * r * g_ref[...] - t_ref[...]
        part = 0.5 * jnp.sum(jnp.mean(diff * diff, axis=-1, keepdims=True), axis=0, keepdims=True)
        dyv = diff * (1.0 / Fd)
        gdy = dyv * g_ref[...]
        dot = jnp.mean(xv * gdy, axis=-1, keepdims=True)
        dh = r * gdy - xv * (r * r * r * dot)
        dh_ref[...] = dh
        dhb_ref[...] = dh.astype(BF16)

        @pl.when(pl.program_id(0) == 0)
        def _():
            dg_ref[...] = jnp.zeros_like(dg_ref)
            loss_ref[...] = jnp.zeros_like(loss_ref)

        dg_ref[...] += jnp.sum(dyv * xv * r, axis=0, keepdims=True)
        loss_ref[...] += jnp.broadcast_to(part, loss_ref.shape)

    row = pl.BlockSpec((tm, Fd), lambda i: (i, 0))
    vec = pl.BlockSpec((1, Fd), lambda i: (0, 0))
    return pl.pallas_call(
        body, name=name, grid=(T // tm,),
        in_specs=[row, vec, row],
        out_specs=[pl.BlockSpec((1, LANES), lambda i: (0, 0)), row, row, vec],
        out_shape=[jax.ShapeDtypeStruct((1, LANES), F32), jax.ShapeDtypeStruct((T, Fd), F32),
                   jax.ShapeDtypeStruct((T, Fd), BF16), jax.ShapeDtypeStruct((1, Fd), F32)],
        compiler_params=_params(("arbitrary",)),
    )(h, g, target)


FFN_TF = 256


def _ffn_fwd(h, g, wg_t, wu_t, wd, name="ffn_fwd"):
    T, Dm = h.shape
    Fh = wd.shape[0]
    tm = _pick(T, 1024)
    nf = Fh // FFN_TF

    def body(h_ref, g_ref, wg_ref, wu_ref, wd_ref, o_ref, u_ref, a_ref, b_ref):
        j = pl.program_id(1)

        @pl.when(j == 0)
        def _():
            xv = h_ref[...]
            r = lax.rsqrt(jnp.mean(xv * xv, axis=-1, keepdims=True) + RMS_EPS)
            u_ref[...] = (xv * r * g_ref[...]).astype(BF16)
            o_ref[...] = xv

        u = u_ref[...]
        a = _dot(u, wg_ref[...], NT).astype(BF16)
        b = _dot(u, wu_ref[...], NT).astype(BF16)
        a_ref[...] = a
        b_ref[...] = b
        af = a.astype(F32)
        s = (af * jax.nn.sigmoid(af) * b.astype(F32)).astype(BF16)
        o_ref[...] += _dot(s, wd_ref[...], NN)

    row = pl.BlockSpec((tm, Dm), lambda i, j: (i, 0))
    wblk = pl.BlockSpec((FFN_TF, Dm), lambda i, j: (j, 0))
    ablk = pl.BlockSpec((tm, FFN_TF), lambda i, j: (i, j))
    return pl.pallas_call(
        body, name=name, grid=(T // tm, nf),
        in_specs=[row, pl.BlockSpec((1, Dm), lambda i, j: (0, 0)), wblk, wblk, wblk],
        out_specs=[row, row, ablk, ablk],
        out_shape=[jax.ShapeDtypeStruct((T, Dm), F32), jax.ShapeDtypeStruct((T, Dm), BF16),
                   jax.ShapeDtypeStruct((T, Fh), BF16), jax.ShapeDtypeStruct((T, Fh), BF16)],
        compiler_params=_params(("parallel", "arbitrary")),
    )(h, g, wg_t, wu_t, wd)


def _ffn_bwd(dh, u, a, b, wg_t, wu_t, wd, name="ffn_bwd"):
    T, Dm = dh.shape
    Fh = wd.shape[0]
    nf = Fh // FFN_TF
    once = pl.Buffered(1)

    def body(dh_ref, u_ref, a_ref, b_ref, wg_ref, wu_ref, wd_ref, du_ref, dwg_ref, dwu_ref, dwd_ref):
        j = pl.program_id(0)

        @pl.when(j == 0)
        def _():
            du_ref[...] = jnp.zeros_like(du_ref)

        ds = _dot(dh_ref[...], wd_ref[...], NT)
        af, bf = a_ref[...].astype(F32), b_ref[...].astype(F32)
        sig = jax.nn.sigmoid(af)
        sa = af * sig
        dwd_ref[...] = _dot((sa * bf).astype(BF16), dh_ref[...], TN).astype(BF16)
        dab = jnp.concatenate([(ds * bf * (sig * (1.0 + af * (1.0 - sig)))).astype(BF16),
                               (ds * sa).astype(BF16)], axis=1)
        dw = _dot(dab, u_ref[...], TN)
        dwg_ref[...] = dw[:FFN_TF].astype(BF16)
        dwu_ref[...] = dw[FFN_TF:].astype(BF16)
        du_ref[...] += _dot(dab, jnp.concatenate([wg_ref[...], wu_ref[...]], axis=0), NN)

    full = lambda: pl.BlockSpec((T, Dm), lambda j: (0, 0), pipeline_mode=once)
    wblk = pl.BlockSpec((FFN_TF, Dm), lambda j: (j, 0))
    ablk = pl.BlockSpec((T, FFN_TF), lambda j: (0, j))
    return pl.pallas_call(
        body, name=name, grid=(nf,),
        in_specs=[full(), full(), ablk, ablk, wblk, wblk, wblk],
        out_specs=[pl.BlockSpec((T, Dm), lambda j: (0, 0)), wblk, wblk, wblk],
        out_shape=[jax.ShapeDtypeStruct((T, Dm), F32)] + [jax.ShapeDtypeStruct((Fh, Dm), BF16)] * 3,
        compiler_params=_params(("arbitrary",)),
    )(dh, u, a, b, wg_t, wu_t, wd)


def _rope(x, cos_t, sin_t, col0, ncols, out_dtype, name="rope"):
    T = x.shape[0]
    wt = cos_t.shape[1]
    tm = _pick(T, 256)
    nb = ncols * LANES // wt
    half = MLA_ROPE // 2

    def body(x_ref, c_ref, s_ref, o_ref):
        xv = x_ref[...].astype(F32)
        lane = lax.broadcasted_iota(jnp.int32, xv.shape, 1)
        first = (lane & (MLA_ROPE - 1)) < half
        swapped = jnp.where(first, pltpu.roll(xv, wt - half, 1), pltpu.roll(xv, half, 1))
        o_ref[...] = (xv * c_ref[...] + swapped * s_ref[...]).astype(out_dtype)

    off = col0 * LANES // wt
    return pl.pallas_call(
        body, name=name, grid=(T // tm, nb),
        in_specs=[pl.BlockSpec((tm, wt), lambda i, j: (i, j + off)),
                  pl.BlockSpec((tm, wt), lambda i, j: (i, 0)),
                  pl.BlockSpec((tm, wt), lambda i, j: (i, 0))],
        out_specs=pl.BlockSpec((tm, wt), lambda i, j: (i, j)),
        out_shape=jax.ShapeDtypeStruct((T, ncols * LANES), out_dtype),
        compiler_params=_params(("parallel", "parallel")),
    )(x, cos_t, sin_t)


ATT_T = 256


def _mla_masks(shape):
    lane = lax.broadcasted_iota(jnp.int32, shape, 1)
    m0 = (lane < 64) | ((lane >= 128) & (lane < 160))
    m1 = ((lane >= 64) & (lane < 128)) | ((lane >= 160) & (lane < 192))
    return m0, m1


def _by_twos(n, step, carry):
    carry = lax.fori_loop(0, n // 2, lambda i, c: step(2 * i + 1, step(2 * i, c)), carry)
    return lax.fori_loop(0, n % 2, lambda _, c: step(n - 1, c), carry)


def _chunk_ok(tq, tk):
    row = lax.broadcasted_iota(jnp.int32, (tq, tk), 0)
    col = lax.broadcasted_iota(jnp.int32, (tq, tk), 1)
    return (col >> 6) <= (row >> 6)


def _mla_fwd(q, kv, kr, name="mla_fwd"):
    T = q.shape[0]
    tq = tk = _pick(T, ATT_T)
    npair = MLA_HEADS // 2
    scale = (MLA_NOPE + MLA_ROPE) ** -0.5

    def body(q_ref, kn_ref, v_ref, kr_ref, o_ref, lse_ref):
        m_idx = pl.program_id(1)
        qv = q_ref[...]
        m0, m1 = _mla_masks(qv.shape)
        qh = (jnp.where(m0, qv, 0).astype(BF16), jnp.where(m1, qv, 0).astype(BF16))

        def block(kb, carry, ok):
            ks = pl.ds(pl.multiple_of(kb * tk, tk), tk)
            kcat = jnp.concatenate([kn_ref[ks, :], kr_ref[ks, :]], axis=1)
            vv = v_ref[ks, :]
            out = []
            for h in range(2):
                mx, l, acc = carry[3 * h:3 * h + 3]
                s = _dot(qh[h], kcat, NT) * scale
                if ok is not None:
                    s = jnp.where(ok, s, NEG)
                mn = jnp.maximum(mx, jnp.max(s, axis=-1, keepdims=True))
                alpha = jnp.exp(mx - mn)
                p = jnp.exp(s - mn)
                l = alpha * l + jnp.sum(p, axis=-1, keepdims=True)
                acc = alpha * acc + _dot(p.astype(BF16), vv, NN)
                out += [mn, l, acc]
            return tuple(out)

        init = (jnp.full((tq, 1), NEG, F32), jnp.zeros((tq, 1), F32), jnp.zeros((tq, LANES), F32)) * 2
        res = block(m_idx, init, _chunk_ok(tq, tk))
        res = _by_twos(m_idx, lambda kb, c: block(kb, c, None), res)
        lane = lax.broadcasted_iota(jnp.int32, (tq, LANES), 1)
        o0 = res[2] / res[1]
        o1 = res[5] / res[4]
        o_ref[...] = jnp.where(lane < 64, o0, o1).astype(o_ref.dtype)
        lse_ref[...] = jnp.where(lane < 64, res[0] + jnp.log(res[1]), res[3] + jnp.log(res[4]))

    full = lambda col: pl.BlockSpec((T, LANES), col)
    return pl.pallas_call(
        body, name=name, grid=(npair, T // tq),
        in_specs=[pl.BlockSpec((tq, 2 * LANES), lambda p, m: (m, p)),
                  full(lambda p, m: (0, p)), full(lambda p, m: (0, npair + p)), full(lambda p, m: (0, 0))],
        out_specs=[pl.BlockSpec((tq, LANES), lambda p, m: (m, p)),
                   pl.BlockSpec((tq, LANES), lambda p, m: (m, p))],
        out_shape=[jax.ShapeDtypeStruct((T, npair * LANES), BF16),
                   jax.ShapeDtypeStruct((T, npair * LANES), F32)],
        compiler_params=_params(("parallel", "arbitrary")),
    )(q, kv, kv, kr)


def _mla_bwd(q, kv, kr, o, lse, do, do_col0, name="mla_bwd"):
    T = q.shape[0]
    tq = tk = _pick(T, ATT_T)
    npair = MLA_HEADS // 2
    scale = (MLA_NOPE + MLA_ROPE) ** -0.5

    def body(q_ref, kn_ref, v_ref, kr_ref, o_ref, lse_ref, do_ref, dq_ref, dkn_ref, dv_ref, dkr_ref):
        p_idx, m_idx = pl.program_id(0), pl.program_id(1)

        @pl.when(m_idx == 0)
        def _():
            dkn_ref[...] = jnp.zeros_like(dkn_ref)
            dv_ref[...] = jnp.zeros_like(dv_ref)

        @pl.when((m_idx == 0) & (p_idx == 0))
        def _():
            dkr_ref[...] = jnp.zeros_like(dkr_ref)

        qv = q_ref[...]
        m0, m1 = _mla_masks(qv.shape)
        qh = (jnp.where(m0, qv, 0).astype(BF16), jnp.where(m1, qv, 0).astype(BF16))
        dov = do_ref[...].astype(F32)
        lane = lax.broadcasted_iota(jnp.int32, (tq, LANES), 1)
        h0 = lane < 64
        prod = dov * o_ref[...].astype(F32)
        delta = (jnp.sum(jnp.where(h0, prod, 0.0), axis=-1, keepdims=True),
                 jnp.sum(jnp.where(h0, 0.0, prod), axis=-1, keepdims=True))
        doh = (jnp.where(h0, dov, 0.0).astype(BF16), jnp.where(h0, 0.0, dov).astype(BF16))
        lsev = lse_ref[...]
        lse_h = (lsev[:, 0:1], lsev[:, 64:65])

        def block(kb, carry, ok):
            ks = pl.ds(pl.multiple_of(kb * tk, tk), tk)
            kcat = jnp.concatenate([kn_ref[ks, :], kr_ref[ks, :]], axis=1)
            vv = v_ref[ks, :]
            dkc = jnp.zeros((tk, 2 * LANES), F32)
            dvv = jnp.zeros((tk, LANES), F32)
            out = []
            for h in range(2):
                s = _dot(qh[h], kcat, NT) * scale
                p = jnp.exp(s - lse_h[h])
                if ok is not None:
                    p = jnp.where(ok, p, 0.0)
                dp = _dot(doh[h], vv, NT)
                ds = (p * (dp - delta[h]) * scale).astype(BF16)
                out.append(carry[h] + _dot(ds, kcat, NN))
                dkc = dkc + _dot(ds, qh[h], TN)
                dvv = dvv + _dot(p.astype(BF16), doh[h], TN)
            dkn_ref[ks, :] += dkc[:, :LANES]
            dkr_ref[ks, :] += dkc[:, LANES:]
            dv_ref[ks, :] += dvv
            return tuple(out)

        init = (jnp.zeros((tq, 2 * LANES), F32),) * 2
        res = block(m_idx, init, _chunk_ok(tq, tk))
        dq0, dq1 = _by_twos(m_idx, lambda kb, c: block(kb, c, None), res)
        dq_ref[...] = jnp.where(m0, dq0, jnp.where(m1, dq1, 0.0))

    full = lambda col: pl.BlockSpec((T, LANES), col)
    blk = lambda col: pl.BlockSpec((tq, LANES), col)
    return pl.pallas_call(
        body, name=name, grid=(npair, T // tq),
        in_specs=[pl.BlockSpec((tq, 2 * LANES), lambda p, m: (m, p)),
                  full(lambda p, m: (0, p)), full(lambda p, m: (0, npair + p)), full(lambda p, m: (0, 0)),
                  blk(lambda p, m: (m, p)), blk(lambda p, m: (m, p)),
                  blk(lambda p, m: (m, do_col0 + p))],
        out_specs=[pl.BlockSpec((tq, 2 * LANES), lambda p, m: (m, p)),
                   full(lambda p, m: (0, p)), full(lambda p, m: (0, p)), full(lambda p, m: (0, 0))],
        out_shape=[jax.ShapeDtypeStruct((T, npair * 2 * LANES), F32),
                   jax.ShapeDtypeStruct((T, npair * LANES), F32),
                   jax.ShapeDtypeStruct((T, npair * LANES), F32),
                   jax.ShapeDtypeStruct((T, LANES), F32)],
        compiler_params=_params(("arbitrary", "arbitrary")),
    )(q, kv, kv, kr, o, lse, do)


def _split_dot(x, tri):
    hi = x.astype(BF16)
    lo = (x - hi.astype(F32)).astype(BF16)
    return _dot(hi, tri, NN) + _dot(lo, tri, NN)


def _sb_terms(qh, kk, before):
    z = _dot(qh, kk, NT)
    sp = jnp.maximum(z, 0.0) + jnp.log(1.0 + jnp.exp(-jnp.abs(z)))
    lk = -sp if before is None else jnp.where(before, -sp, 0.0)
    return z, sp, lk


def _sb_setup(q_ref, tq, tk, scale):
    qv = (q_ref[...].astype(F32) * scale).astype(BF16)
    lane = lax.broadcasted_iota(jnp.int32, (tq, LANES), 1)
    h0 = lane < 64
    qh = (jnp.where(h0, qv, 0).astype(BF16), jnp.where(h0, 0, qv).astype(BF16))
    row = lax.broadcasted_iota(jnp.int32, (tk, tk), 0)
    col = lax.broadcasted_iota(jnp.int32, (tk, tk), 1)
    return qh, h0, row, col


def _sb_fwd(qkv, col0, name="sb_fwd"):
    T = qkv.shape[0]
    tq = tk = _pick(T, ATT_T)
    npair = SB_HEADS // 2
    scale = SB_DIM ** -0.5

    def body(q_ref, k_ref, v_ref, o_ref, o32_ref):
        m_idx = pl.program_id(1)
        qh, h0, row, col = _sb_setup(q_ref, tq, tk, scale)
        later = (row > col).astype(BF16)

        def block(kb, carry, before):
            ks = pl.ds(pl.multiple_of(kb * tk, tk), tk)
            kk = k_ref[ks, :].astype(BF16)
            vv = v_ref[ks, :].astype(BF16)
            out = []
            for h in range(2):
                c, acc = carry[2 * h:2 * h + 2]
                z, sp, lk = _sb_terms(qh[h], kk, before)
                w = jnp.exp((z - sp) + _split_dot(lk, later) + c)
                if before is not None:
                    w = jnp.where(before, w, 0.0)
                out += [c + jnp.sum(lk, axis=-1, keepdims=True), acc + _dot(w.astype(BF16), vv, NN)]
            return tuple(out)

        init = (jnp.zeros((tq, 1), F32), jnp.zeros((tq, LANES), F32)) * 2
        res = block(m_idx, init, col < row)
        res = _by_twos(m_idx, lambda i, c: block(m_idx - 1 - i, c, None), res)
        o = jnp.where(h0, res[1], res[3])
        o_ref[...] = o.astype(o_ref.dtype)
        o32_ref[...] = o

    full = lambda col: pl.BlockSpec((T, LANES), col)
    blk = pl.BlockSpec((tq, LANES), lambda p, m: (m, p))
    return pl.pallas_call(
        body, name=name, grid=(npair, T // tq),
        in_specs=[pl.BlockSpec((tq, LANES), lambda p, m: (m, col0 + p)),
                  full(lambda p, m: (0, col0 + npair + p)), full(lambda p, m: (0, col0 + 2 * npair + p))],
        out_specs=[blk, blk],
        out_shape=[jax.ShapeDtypeStruct((T, npair * LANES), BF16), jax.ShapeDtypeStruct((T, npair * LANES), F32)],
        compiler_params=_params(("parallel", "arbitrary")),
    )(qkv, qkv, qkv)


def _sb_bwd(qkv, col0, o32, do, do_col0, dep, name="sb_bwd"):
    T = qkv.shape[0]
    tq = tk = _pick(T, ATT_T)
    npair = SB_HEADS // 2
    scale = SB_DIM ** -0.5

    def body(q_ref, k_ref, v_ref, o_ref, do_ref, dep_ref, dq_ref, dk_ref, dv_ref):
        m_idx = pl.program_id(1)

        @pl.when(m_idx == 0)
        def _():
            dk_ref[...] = jnp.zeros_like(dk_ref)
            dv_ref[...] = jnp.zeros_like(dv_ref)

        qh, h0, row, col = _sb_setup(q_ref, tq, tk, scale)
        dov = do_ref[...].astype(F32)
        doh = (jnp.where(h0, dov, 0.0).astype(BF16), jnp.where(h0, 0.0, dov).astype(BF16))
        ov = o_ref[...]
        etot = [jnp.sum(doh[h].astype(F32) * ov, axis=-1, keepdims=True) for h in range(2)]
        later = (row > col).astype(BF16)
        from_here = (row >= col).astype(BF16)

        def block(kb, carry, before):
            ks = pl.ds(pl.multiple_of(kb * tk, tk), tk)
            kk = k_ref[ks, :].astype(BF16)
            vv = v_ref[ks, :].astype(BF16)
            dkk = jnp.zeros((tk, LANES), F32)
            dvv = jnp.zeros((tk, LANES), F32)
            out = []
            for h in range(2):
                c, es, dqa = carry[3 * h:3 * h + 3]
                z, sp, lk = _sb_terms(qh[h], kk, before)
                w = jnp.exp((z - sp) + _split_dot(lk, later) + c)
                if before is not None:
                    w = jnp.where(before, w, 0.0)
                wb = w.astype(BF16)
                e = wb.astype(F32) * _dot(doh[h], vv, NT)
                prev = etot[h] - (_split_dot(e, from_here) + es)
                sig_neg = jnp.exp(-sp)
                dz = e * sig_neg - (1.0 - sig_neg) * prev
                if before is not None:
                    dz = jnp.where(before, dz, 0.0)
                dzb = dz.astype(BF16)
                dkk = dkk + _dot(dzb, qh[h], TN)
                dvv = dvv + _dot(wb, doh[h], TN)
                out += [c + jnp.sum(lk, axis=-1, keepdims=True), es + jnp.sum(e, axis=-1, keepdims=True),
                        dqa + _dot(dzb, kk, NN)]
            dk_ref[ks, :] += dkk
            dv_ref[ks, :] += dvv
            return tuple(out)

        init = (jnp.zeros((tq, 1), F32), jnp.zeros((tq, 1), F32), jnp.zeros((tq, LANES), F32)) * 2
        res = block(m_idx, init, col < row)
        res = _by_twos(m_idx, lambda i, c: block(m_idx - 1 - i, c, None), res)
        dq_ref[...] = jnp.where(h0, res[2], res[5]) * scale

    full = lambda col: pl.BlockSpec((T, LANES), col)
    blk = lambda col: pl.BlockSpec((tq, LANES), col)
    return pl.pallas_call(
        body, name=name, grid=(npair, T // tq),
        in_specs=[blk(lambda p, m: (m, col0 + p)),
                  full(lambda p, m: (0, col0 + npair + p)), full(lambda p, m: (0, col0 + 2 * npair + p)),
                  blk(lambda p, m: (m, p)), blk(lambda p, m: (m, do_col0 + p)),
                  pl.BlockSpec((8, LANES), lambda p, m: (0, 0))],
        out_specs=[blk(lambda p, m: (m, p)), full(lambda p, m: (0, p)), full(lambda p, m: (0, p))],
        out_shape=[jax.ShapeDtypeStruct((T, npair * LANES), F32)] * 3,
        compiler_params=_params(("arbitrary", "arbitrary")),
    )(qkv, qkv, qkv, o32, do, dep)


def _band_in_window():
    cq = lax.broadcasted_iota(jnp.int32, (BAND_TQ, BAND_W), 0) >> 6
    ckp = lax.broadcasted_iota(jnp.int32, (BAND_TQ, BAND_W), 1) >> 6
    return (ckp >= cq) & (ckp <= cq + LEFT_CHUNKS)


def _band_real(m_idx):
    j = lax.broadcasted_iota(jnp.int32, (BAND_TQ, BAND_W), 1)
    return j >= PAD_KEYS - m_idx * BAND_TQ


def _band_probs(qh, kw, bias, real, scale):
    s = jnp.where(real, _dot(qh, kw, NT) * scale + bias, NEG)
    e = jnp.exp(s - jnp.max(s, axis=-1, keepdims=True))
    return e / jnp.sum(e, axis=-1, keepdims=True)


BAND_SUB = 2


def _band_fwd(qkv, k_pad, v_pad, bias_w, name="band_fwd"):
    T = qkv.shape[0]
    npair = C_HEADS // 2
    scale = C_DIM ** -0.5
    rows = BAND_SUB * BAND_TQ

    def body(q_ref, k_ref, v_ref, b_ref, o_ref):
        lane = lax.broadcasted_iota(jnp.int32, (BAND_TQ, LANES), 1)
        h0 = lane < 64
        for sub in range(BAND_SUB):
            m_idx = pl.program_id(1) * BAND_SUB + sub
            win = pl.ds(pl.multiple_of(m_idx * BAND_TQ, BAND_TQ), BAND_W)
            kw, vw = k_ref[win, :], v_ref[win, :]
            qv = q_ref[sub * BAND_TQ:(sub + 1) * BAND_TQ, :]
            qh = (jnp.where(h0, qv, 0).astype(BF16), jnp.where(h0, 0, qv).astype(BF16))
            real = _band_real(m_idx)
            o = [_dot(_band_probs(qh[h], kw, b_ref[h], real, scale).astype(BF16), vw, NN) for h in range(2)]
            o_ref[sub * BAND_TQ:(sub + 1) * BAND_TQ, :] = jnp.where(h0, o[0], o[1]).astype(o_ref.dtype)

    Tp = T + PAD_KEYS
    return pl.pallas_call(
        body, name=name, grid=(npair, T // rows),
        in_specs=[pl.BlockSpec((rows, LANES), lambda p, m: (m, p)),
                  pl.BlockSpec((Tp, LANES), lambda p, m: (0, p)),
                  pl.BlockSpec((Tp, LANES), lambda p, m: (0, p)),
                  pl.BlockSpec((2, BAND_TQ, BAND_W), lambda p, m: (p, 0, 0))],
        out_specs=pl.BlockSpec((rows, LANES), lambda p, m: (m, p)),
        out_shape=jax.ShapeDtypeStruct((T, npair * LANES), BF16),
        compiler_params=_params(("parallel", "arbitrary")),
    )(qkv, k_pad, v_pad, bias_w)


def _band_bwd(qkv, k_pad, v_pad, bias_w, do, name="band_bwd"):
    T = qkv.shape[0]
    npair = C_HEADS // 2
    scale = C_DIM ** -0.5

    rows = BAND_SUB * BAND_TQ

    def body(q_ref, k_ref, v_ref, b_ref, do_ref, dq_ref, dk_ref, dv_ref, db_ref):
        @pl.when(pl.program_id(1) == 0)
        def _():
            dk_ref[...] = jnp.zeros_like(dk_ref)
            dv_ref[...] = jnp.zeros_like(dv_ref)
            db_ref[...] = jnp.zeros_like(db_ref)

        lane = lax.broadcasted_iota(jnp.int32, (BAND_TQ, LANES), 1)
        h0 = lane < 64
        dbs = [jnp.zeros((BAND_TQ, BAND_W), F32)] * 2
        for sub in range(BAND_SUB):
            m_idx = pl.program_id(1) * BAND_SUB + sub
            win = pl.ds(pl.multiple_of(m_idx * BAND_TQ, BAND_TQ), BAND_W)
            kw, vw = k_ref[win, :], v_ref[win, :]
            qv = q_ref[sub * BAND_TQ:(sub + 1) * BAND_TQ, :]
            dov = do_ref[sub * BAND_TQ:(sub + 1) * BAND_TQ, :].astype(F32)
            qh = (jnp.where(h0, qv, 0).astype(BF16), jnp.where(h0, 0, qv).astype(BF16))
            doh = (jnp.where(h0, dov, 0.0).astype(BF16), jnp.where(h0, 0.0, dov).astype(BF16))
            real = _band_real(m_idx)
            dq = []
            dkw = jnp.zeros((BAND_W, LANES), F32)
            dvw = jnp.zeros((BAND_W, LANES), F32)
            for h in range(2):
                p = _band_probs(qh[h], kw, b_ref[h], real, scale)
                dp = _dot(doh[h], vw, NT)
                dsb = p * (dp - jnp.sum(p * dp, axis=-1, keepdims=True))
                dbs[h] = dbs[h] + dsb
                dsq = (dsb * scale).astype(BF16)
                dq.append(_dot(dsq, kw, NN))
                dkw = dkw + _dot(dsq, qh[h], TN)
                dvw = dvw + _dot(p.astype(BF16), doh[h], TN)
            dq_ref[sub * BAND_TQ:(sub + 1) * BAND_TQ, :] = jnp.where(h0, dq[0], dq[1])
            dk_ref[win, :] += dkw
            dv_ref[win, :] += dvw
        for h in range(2):
            db_ref[h] += dbs[h]

    Tp = T + PAD_KEYS
    blk = lambda col: pl.BlockSpec((rows, LANES), col)
    full = pl.BlockSpec((Tp, LANES), lambda p, m: (0, p))
    bias = pl.BlockSpec((2, BAND_TQ, BAND_W), lambda p, m: (p, 0, 0))
    return pl.pallas_call(
        body, name=name, grid=(npair, T // rows),
        in_specs=[blk(lambda p, m: (m, p)), full, full, bias, blk(lambda p, m: (m, p))],
        out_specs=[blk(lambda p, m: (m, p)), full, full, bias],
        out_shape=[jax.ShapeDtypeStruct((T, npair * LANES), F32),
                   jax.ShapeDtypeStruct((Tp, npair * LANES), F32),
                   jax.ShapeDtypeStruct((Tp, npair * LANES), F32),
                   jax.ShapeDtypeStruct((C_HEADS, BAND_TQ, BAND_W), F32)],
        compiler_params=_params(("arbitrary", "arbitrary")),
    )(qkv, k_pad, v_pad, bias_w, do)


def _skew_bits(x, left):
    w = x.shape[1]
    row = lax.broadcasted_iota(jnp.int32, x.shape, 0)
    for b in range(BAND_TQ.bit_length() - 1):
        amt = (w - (1 << b)) if left else (1 << b)
        x = jnp.where((row >> b) & 1 == 1, pltpu.roll(x, amt, 1), x)
    return x


def _toeplitz(diag, name="toeplitz"):
    H = diag.shape[0]

    def body(d_ref, o_ref):
        x = jnp.broadcast_to(d_ref[0], (BAND_TQ, TOEP_W))
        o_ref[0] = jnp.where(_band_in_window(), _skew_bits(x, left=False)[:, BAND_TQ:], NEG)

    return pl.pallas_call(
        body, name=name, grid=(H,),
        in_specs=[pl.BlockSpec((1, 1, TOEP_W), lambda h: (h, 0, 0))],
        out_specs=pl.BlockSpec((1, BAND_TQ, BAND_W), lambda h: (h, 0, 0)),
        out_shape=jax.ShapeDtypeStruct((H, BAND_TQ, BAND_W), F32),
        compiler_params=_params(("parallel",)),
    )(diag.reshape(H, 1, TOEP_W))


def _toeplitz_bwd(dbias, name="toeplitz_bwd"):
    H = dbias.shape[0]

    def body(d_ref, o_ref):
        x = jnp.concatenate([jnp.zeros((BAND_TQ, BAND_TQ), F32), d_ref[0]], axis=1)
        o_ref[0] = jnp.sum(_skew_bits(x, left=True), axis=0, keepdims=True)

    return pl.pallas_call(
        body, name=name, grid=(H,),
        in_specs=[pl.BlockSpec((1, BAND_TQ, BAND_W), lambda h: (h, 0, 0))],
        out_specs=pl.BlockSpec((1, 1, TOEP_W), lambda h: (h, 0, 0)),
        out_shape=jax.ShapeDtypeStruct((H, 1, TOEP_W), F32),
        compiler_params=_params(("parallel",)),
    )(dbias).reshape(H, TOEP_W)


_HBM = pl.BlockSpec(memory_space=pltpu.HBM)
_SEM = pl.BlockSpec(memory_space=pltpu.SEMAPHORE)
_EFFECT = pltpu.SideEffectType.DATAFLOW_SIDE_EFFECTING


def _peers():
    x, y, c = lax.axis_index("x"), lax.axis_index("y"), lax.axis_index("c")
    out = []
    for k in range(1, N_DEV):
        peer = (1 - x if (k >> 2) & 1 else x, 1 - y if (k >> 1) & 1 else y, 1 - c if k & 1 else c)
        out.append((peer, 4 * peer[0] + 2 * peer[1] + peer[2]))
    return 4 * x + 2 * y + c, out


def _split_copies(ins, lands, scatter, send_sem, recv_sem, arriving):
    me, peers = _peers()
    out = []
    for a in range(len(ins)):
        for peer, idx in peers:
            out.append(pltpu.make_async_remote_copy(
                src_ref=ins[a].at[idx] if scatter[a] else ins[a],
                dst_ref=lands[a].at[idx if arriving else me], send_sem=send_sem, recv_sem=recv_sem,
                device_id=peer, device_id_type=pl.DeviceIdType.MESH))
    return out


def _landing_zones(arrays, scatter):
    return [lax.empty((N_DEV,) + (a.shape[1:] if s else a.shape), a.dtype) for a, s in zip(arrays, scatter)]


def _place_own(arrays, scatter, name):
    n = len(arrays)
    lands = _landing_zones(arrays, scatter)
    me = (4 * lax.axis_index("x") + 2 * lax.axis_index("y") + lax.axis_index("c")).astype(jnp.int32).reshape(1)

    def body(me_ref, *refs):
        for a in range(n):
            refs[2 * n + a][...] = refs[a][...].reshape(refs[2 * n + a].shape)

    def row_spec(shape):
        zeros = (0,) * (len(shape) - 1)
        return pl.BlockSpec((1,) + tuple(shape[1:]), lambda i, me_ref: (me_ref[0],) + zeros)

    in_specs = [row_spec(a.shape) if s else pl.BlockSpec(a.shape, lambda i, me_ref, nd=a.ndim: (0,) * nd)
                for a, s in zip(arrays, scatter)]
    return pl.pallas_call(
        body, name=name,
        out_shape=[jax.ShapeDtypeStruct(l.shape, l.dtype) for l in lands],
        grid_spec=pltpu.PrefetchScalarGridSpec(
            num_scalar_prefetch=1, grid=(1,),
            in_specs=in_specs + [pl.BlockSpec(memory_space=pl.ANY)] * n,
            out_specs=[row_spec(l.shape) for l in lands]),
        input_output_aliases={1 + n + i: i for i in range(n)},
        compiler_params=_params(("arbitrary",)),
    )(me, *arrays, *lands)


def _exchange_start(arrays, scatter, after, name):
    n = len(arrays)
    lands = list(_place_own(arrays, scatter, name=name.replace("_start_", "_own_")))

    def body(*refs):
        ins, lnd = refs[:n], refs[n:2 * n]
        send_sem, recv_sem = refs[2 * n + 1:2 * n + 3]
        token = refs[-1]
        for cp in _split_copies(ins, lnd, scatter, send_sem, recv_sem, arriving=False):
            cp.start()
        token[...] = jnp.zeros_like(token)

    hbm = lambda a: pltpu.HBM(a.shape, a.dtype)
    out = pl.pallas_call(
        body, name=name,
        out_shape=(pltpu.SemaphoreType.DMA(()), pltpu.SemaphoreType.DMA(()),
                   *[hbm(a) for a in arrays], *[hbm(a) for a in lands],
                   jax.ShapeDtypeStruct((8, LANES), F32)),
        in_specs=[_HBM] * (2 * n) + [pl.BlockSpec(memory_space=pl.ANY)],
        out_specs=(_SEM, _SEM, *([_HBM] * (2 * n)), pl.BlockSpec(memory_space=pltpu.VMEM)),
        input_output_aliases={i: 2 + i for i in range(2 * n)},
        compiler_params=pltpu.CompilerParams(has_side_effects=_EFFECT),
    )(*[pltpu.with_memory_space_constraint(a, pltpu.HBM) for a in list(arrays) + lands], after)
    return (out[0], out[1], list(out[2:2 + n]), list(out[2 + n:2 + 2 * n]), tuple(scatter)), out[-1]


def _exchange_wait(handle, after, name):
    send_sem, recv_sem, ins, lands, scatter = handle
    n = len(ins)

    def body(*refs):
        i_ref, l_ref = refs[:n], refs[n:2 * n]
        s_sem, r_sem = refs[2 * n:2 * n + 2]
        for cp in _split_copies(i_ref, l_ref, scatter, s_sem, r_sem, arriving=False):
            cp.wait_send()
        for cp in _split_copies(i_ref, l_ref, scatter, s_sem, r_sem, arriving=True):
            cp.wait_recv()

    hbm = lambda a: pltpu.HBM(a.shape, a.dtype)
    out = pl.pallas_call(
        body, name=name,
        out_shape=tuple(hbm(a) for a in ins + lands),
        in_specs=[_HBM] * (2 * n) + [_SEM, _SEM, pl.BlockSpec(memory_space=pl.ANY)],
        out_specs=tuple([_HBM] * (2 * n)),
        input_output_aliases={i: i for i in range(2 * n)},
        compiler_params=pltpu.CompilerParams(has_side_effects=_EFFECT),
    )(*ins, *lands, send_sem, recv_sem, after)
    return list(out[n:])


def _adamw(w, parts, m, v, name="adamw"):
    R, C = w.shape
    tr = next(t for t in (256, 128, 64, 32, 16, 8) if R % t == 0)
    c1 = 1.0 - ADAM_B1 ** ADAM_STEP
    c2 = 1.0 - ADAM_B2 ** ADAM_STEP

    def body(w_ref, p_ref, m_ref, v_ref, g_ref, d_ref, nm_ref, nv_ref):
        g = p_ref[0].astype(F32)
        for i in range(1, N_DEV):
            g = g + p_ref[i].astype(F32)
        nm = ADAM_B1 * m_ref[...] + (1.0 - ADAM_B1) * g
        nv = ADAM_B2 * v_ref[...] + (1.0 - ADAM_B2) * (g * g)
        g_ref[...] = g
        nm_ref[...] = nm
        nv_ref[...] = nv
        d_ref[...] = -ADAM_LR * ((nm / c1) / (jnp.sqrt(nv / c2) + ADAM_EPS) + ADAM_WD * w_ref[...])

    blk = pl.BlockSpec((tr, C), lambda i: (i, 0))
    return pl.pallas_call(
        body, name=name, grid=(R // tr,),
        in_specs=[blk, pl.BlockSpec((N_DEV, tr, C), lambda i: (0, i, 0)), blk, blk],
        out_specs=[blk] * 4,
        out_shape=[jax.ShapeDtypeStruct((R, C), F32)] * 4,
        compiler_params=_params(("parallel",)),
    )(w, parts, m, v)


_O1 = Q_LORA
_O2 = _O1 + KV_LORA
_O3 = _O2 + MLA_ROPE
_NB = SB_HEADS * SB_DIM
IN_W = _O2 + LANES + 3 * _NB
COL_KR = _O2 // LANES
COL_SB = COL_KR + 1


def _w_in_local(w):
    kr = w[_O2:_O3]
    pad = jnp.zeros((LANES - 2 * MLA_ROPE, w.shape[1]), w.dtype)
    return jnp.concatenate([w[:_O2], kr, kr, pad, w[_O3:]], axis=0)


def _w_in_grad(g):
    kr = (g[_O2:_O2 + MLA_ROPE].astype(F32) + g[_O2 + MLA_ROPE:_O2 + 2 * MLA_ROPE].astype(F32)).astype(g.dtype)
    return jnp.concatenate([g[:_O2], kr, g[_O2 + LANES:]], axis=0)


def _w_uq_local(w):
    w3 = w.reshape(MLA_HEADS // 2, 2, MLA_NOPE + MLA_ROPE, w.shape[1])
    nope = w3[:, :, :MLA_NOPE].reshape(MLA_HEADS // 2, 2 * MLA_NOPE, w.shape[1])
    rope = w3[:, :, MLA_NOPE:].reshape(MLA_HEADS // 2, 2 * MLA_ROPE, w.shape[1])
    pad = jnp.zeros((MLA_HEADS // 2, LANES - 2 * MLA_ROPE, w.shape[1]), w.dtype)
    return jnp.concatenate([nope, rope, pad], axis=1).reshape(-1, w.shape[1])


def _w_uq_grad(g):
    g3 = g.reshape(MLA_HEADS // 2, 2 * LANES, g.shape[1])
    nope = g3[:, :2 * MLA_NOPE].reshape(MLA_HEADS // 2, 2, MLA_NOPE, g.shape[1])
    rope = g3[:, LANES:LANES + 2 * MLA_ROPE].reshape(MLA_HEADS // 2, 2, MLA_ROPE, g.shape[1])
    return jnp.concatenate([nope, rope], axis=2).reshape(-1, g.shape[1])


def _w_ukv_local(w):
    w3 = w.reshape(MLA_HEADS, MLA_NOPE + MLA_V, w.shape[1])
    return jnp.concatenate([w3[:, :MLA_NOPE].reshape(-1, w.shape[1]),
                            w3[:, MLA_NOPE:].reshape(-1, w.shape[1])], axis=0)


def _w_ukv_grad(g):
    half = MLA_HEADS * MLA_NOPE
    kn = g[:half].reshape(MLA_HEADS, MLA_NOPE, g.shape[1])
    vv = g[half:].reshape(MLA_HEADS, MLA_V, g.shape[1])
    return jnp.concatenate([kn, vv], axis=1).reshape(-1, g.shape[1])


def _rope_tables(T):
    pos = jnp.arange(T, dtype=F32)
    inv_freq = ROPE_THETA ** (-jnp.arange(0, MLA_ROPE, 2, dtype=F32) / MLA_ROPE)
    ang = pos[:, None] * inv_freq[None, :]
    cos, sin = jnp.cos(ang), jnp.sin(ang)
    ones = jnp.ones((T, LANES - 2 * MLA_ROPE), F32)
    cos_k = jnp.concatenate([cos, cos, cos, cos, ones], axis=1)
    sin_k = jnp.concatenate([-sin, sin, -sin, sin, 0.0 * ones], axis=1)
    cos_q = jnp.concatenate([jnp.ones((T, LANES), F32), cos_k], axis=1)
    sin_q = jnp.concatenate([jnp.zeros((T, LANES), F32), sin_k], axis=1)
    return cos_q, sin_q, cos_k, sin_k


def _bias_diag_index():
    ell = np.arange(TOEP_W)
    return np.clip(BAND_W - ell, -REL_CLIP, REL_CLIP) + REL_CLIP


def _local_step(x, target, small, get_weights, put_grads):
    T = x.shape[0]
    cos_q, sin_q, cos_k, sin_k = _rope_tables(T)
    G = {}
    W = dict(small)

    u0 = _rms_fwd(x, W["g_mix"][0:1], name="rms_mix0")
    W.update(get_weights("in0", u0))
    proj = _mm(u0, W["w_in_t"], dims="nt", name="proj_in")
    W.update(get_weights("mix0", proj))
    c_q, c_kv = proj[:, :_O1], proj[:, _O1:_O2]
    nq = _rms_fwd(c_q, W["g_cq"], name="rms_cq")
    nkv = _rms_fwd(c_kv, W["g_ckv"], name="rms_ckv")
    qa_raw = _mm(nq, W["w_uq_t"], dims="nt", name="proj_uq")
    qa = _rope(qa_raw, cos_q, sin_q, 0, qa_raw.shape[1] // LANES, BF16, name="rope_q")
    kv = _mm(nkv, W["w_ukv_t"], dims="nt", out_dtype=BF16, name="proj_ukv")
    kr = _rope(proj, cos_k, sin_k, COL_KR, 1, BF16, name="rope_k")
    o_a, lse = _mla_fwd(qa, kv, kr)
    o_b, o_b32 = _sb_fwd(proj, COL_SB)
    o_ab = jnp.concatenate([o_a, o_b], axis=1)
    h1 = _mm(o_ab, W["ev_w_out"], res=x, name="out_ev")

    def ffn_fwd(h, layer):
        W.update(get_weights(f"ffn{layer}", h))
        return _ffn_fwd(h, W["g_ffn"][layer:layer + 1], W[f"w_gate_t{layer}"], W[f"w_up_t{layer}"],
                        W[f"w_down{layer}"], name=f"ffn_fwd{layer}")

    h2, u1, a0, b0 = ffn_fwd(h1, 0)

    W.update(get_weights("mix1", h2))
    u2 = _rms_fwd(h2, W["g_mix"][1:2], name="rms_mix1")
    qkv = _mm(u2, W["od_w_qkv_t"], dims="nt", out_dtype=BF16, name="proj_qkv")
    nc = C_HEADS * C_DIM
    pad = ((PAD_KEYS, 0), (0, 0))
    k_pad, v_pad = jnp.pad(qkv[:, nc:2 * nc], pad), jnp.pad(qkv[:, 2 * nc:], pad)
    diag_idx = _bias_diag_index()
    bias_w = _toeplitz(W["od_rel_bias"][:, diag_idx])
    o_c = _band_fwd(qkv, k_pad, v_pad, bias_w)
    h3 = _mm(o_c, W["od_w_out"], res=h2, name="out_od")
    h4, u3, a1, b1 = ffn_fwd(h3, 1)

    loss, dh, dhb, G["g_final"] = _loss_head(h4, W["g_final"], target)

    def ffn_bwd(dh, dhb, h, u, a, b, layer):
        du, g_gate, g_up, g_down = _ffn_bwd(dhb, u, a, b, W[f"w_gate_t{layer}"], W[f"w_up_t{layer}"],
                                            W[f"w_down{layer}"], name=f"ffn_bwd{layer}")
        tok = put_grads(f"ffn{layer}", {"w_gate_t": g_gate, "w_up_t": g_up, "w_down": g_down})
        return _rms_bwd(h, W["g_ffn"][layer:layer + 1] + tok[:1, :1], du, dres=dh, name=f"rms_ffn_bwd{layer}")

    dh3, dh3b, g_gffn1 = ffn_bwd(dh, dhb, h3, u3, a1, b1, 1)

    do_c = _mm(dh3b, W["od_w_out"], dims="nt", name="out_od_dx")
    g_od_out = _mm(o_c, dh3b, dims="tn", out_dtype=BF16, name="out_od_dw")
    dq_c, dk_p, dv_p, dbias_w = _band_bwd(qkv, k_pad, v_pad, bias_w, do_c)
    dqkv = jnp.concatenate([dq_c, dk_p[PAD_KEYS:], dv_p[PAD_KEYS:]], axis=1)
    du2 = _mm(dqkv, W["od_w_qkv_t"], name="proj_qkv_dx")
    tok = put_grads("mix1", {"od_w_qkv_t": _mm(dqkv, u2, dims="tn", out_dtype=BF16, name="proj_qkv_dw"),
                             "od_w_out": g_od_out})
    ddiag = _toeplitz_bwd(dbias_w)
    n_far = BAND_W - REL_CLIP + 1
    G["od_rel_bias"] = jnp.concatenate(
        [jnp.zeros((C_HEADS, REL_CLIP - BAND_TQ + 1), F32), ddiag[:, n_far:][:, ::-1],
         jnp.sum(ddiag[:, :n_far], axis=1, keepdims=True)], axis=1)
    dh2, dh2b, g_gmix1 = _rms_bwd(h2, W["g_mix"][1:2] + tok[:1, :1], du2, dres=dh3, name="rms_mix_bwd1")

    dh1, dh1b, g_gffn0 = ffn_bwd(dh2, dh2b, h1, u1, a0, b0, 0)
    G["g_ffn"] = jnp.concatenate([g_gffn0, g_gffn1], axis=0)

    do_ab = _mm(dh1b, W["ev_w_out"], dims="nt", name="out_ev_dx")
    g0 = {"ev_w_out": _mm(o_ab, dh1b, dims="tn", out_dtype=BF16, name="out_ev_dw")}
    dqa, dkn, dva, dkr = _mla_bwd(qa, kv, kr, o_a, lse, do_ab, 0)
    dqa_raw = _rope(dqa, cos_q, -sin_q, 0, dqa.shape[1] // LANES, F32, name="rope_q_bwd")
    g0["w_uq_t"] = _mm(dqa_raw, nq, dims="tn", name="proj_uq_dw")
    dnq = _mm(dqa_raw, W["w_uq_t"], name="proj_uq_dx")
    dc_q, _, G["g_cq"] = _rms_bwd(c_q, W["g_cq"], dnq, name="rms_cq_bwd")
    dkv = jnp.concatenate([dkn, dva], axis=1)
    g0["w_ukv_t"] = _mm(dkv, nkv, dims="tn", name="proj_ukv_dw")
    dnkv = _mm(dkv, W["w_ukv_t"], name="proj_ukv_dx")
    dc_kv, _, G["g_ckv"] = _rms_bwd(c_kv, W["g_ckv"], dnkv, name="rms_ckv_bwd")
    tok = put_grads("mix0", g0)
    dqb, dkb, dvb = _sb_bwd(proj, COL_SB, o_b32, do_ab, MLA_HEADS // 2, tok)
    dkr_raw = _rope(dkr, cos_k, -sin_k, 0, 1, F32, name="rope_k_bwd")
    dproj = jnp.concatenate([dc_q, dc_kv, dkr_raw, dqb, dkb, dvb], axis=1)
    du0 = _mm(dproj, W["w_in_t"], name="proj_in_dx")
    tok = put_grads("in0", {"w_in_t": _mm(dproj, u0, dims="tn", name="proj_in_dw")})
    dx, _, g_gmix0 = _rms_bwd(x, W["g_mix"][0:1] + tok[:1, :1], du0, dres=dh1, name="rms_mix_bwd0")
    G["g_mix"] = jnp.concatenate([g_gmix0, g_gmix1], axis=0)
    return loss[0, 0], dx, G


_BIG = ["ev_w_in", "ev_w_uq", "ev_w_ukv", "ev_w_out", "od_w_qkv", "od_w_out", "w_gate", "w_up", "w_down"]
_COL_SHARDED = {"ev_w_in", "ev_w_uq", "ev_w_ukv", "od_w_qkv", "w_gate", "w_up"}
_SMALL = ["ev_g_cq", "ev_g_ckv", "od_rel_bias", "g_mix", "g_ffn", "g_final"]
_GROUPS = {
    "in0": ["ev_w_in"],
    "mix0": ["ev_w_uq", "ev_w_ukv", "ev_w_out"],
    "ffn0": ["w_gate0", "w_up0", "w_down0"],
    "mix1": ["od_w_qkv", "od_w_out"],
    "ffn1": ["w_gate1", "w_up1", "w_down1"],
}
_GROUP_SRC = {n + str(l): (n, l) for n in ("w_gate", "w_up", "w_down") for l in (0, 1)}
_SMALL_ROWS = 8
_SMALL_COLS = 1792


def _pack_small(vals):
    flat = jnp.concatenate([v.reshape(-1).astype(F32) for v in vals])
    flat = jnp.pad(flat, (0, _SMALL_ROWS * _SMALL_COLS - flat.shape[0]))
    return flat.reshape(_SMALL_ROWS, _SMALL_COLS)


def _unpack_small(packed, like):
    flat = packed.reshape(-1)
    out, off = [], 0
    for v in like:
        out.append(flat[off:off + v.size].reshape(v.shape))
        off += v.size
    return out


def kernel(x, ev_w_in, ev_g_cq, ev_w_uq, ev_g_ckv, ev_w_ukv, ev_w_out, od_w_qkv, od_rel_bias, od_w_out, g_mix, g_ffn, w_gate, w_up, w_down, g_final, loss_target, m_ev_w_in, m_ev_g_cq, m_ev_w_uq, m_ev_g_ckv, m_ev_w_ukv, m_ev_w_out, m_od_w_qkv, m_od_rel_bias, m_od_w_out, m_g_mix, m_g_ffn, m_w_gate, m_w_up, m_w_down, m_g_final, v_ev_w_in, v_ev_g_cq, v_ev_w_uq, v_ev_g_ckv, v_ev_w_ukv, v_ev_w_out, v_od_w_qkv, v_od_rel_bias, v_od_w_out, v_g_mix, v_g_ffn, v_w_gate, v_w_up, v_w_down, v_g_final):
    args = dict(locals())
    w = {n: args[n] for n in _BIG + _SMALL}
    mom = {n: args["m_" + n] for n in _BIG + _SMALL}
    var = {n: args["v_" + n] for n in _BIG + _SMALL}

    own = {}
    for grp, names in _GROUPS.items():
        for n in names:
            base, layer = _GROUP_SRC.get(n, (n, 0))
            shard = w[base][layer:layer + 1]
            own[n] = (jnp.swapaxes(shard, 1, 2) if base in _COL_SHARDED else shard).astype(BF16)
    gather, token = {}, x[0, :8, :LANES]
    for grp, names in _GROUPS.items():
        gather[grp], token = _exchange_start([own[n] for n in names], [False] * len(names), token,
                                             name="gather_start_" + grp)

    def get_weights(grp, after):
        names = _GROUPS[grp]
        lands = _exchange_wait(gather[grp], token if after is None else after, name="gather_wait_" + grp)
        full = {n: l.reshape(-1, l.shape[-1]) for n, l in zip(names, lands)}
        if grp == "in0":
            return {"w_in_t": _w_in_local(full["ev_w_in"])}
        if grp == "mix0":
            return {"w_uq_t": _w_uq_local(full["ev_w_uq"]), "w_ukv_t": _w_ukv_local(full["ev_w_ukv"]),
                    "ev_w_out": full["ev_w_out"]}
        if grp == "mix1":
            return {"od_w_qkv_t": full["od_w_qkv"], "od_w_out": full["od_w_out"]}
        layer = grp[-1]
        return {"w_gate_t" + layer: full["w_gate" + layer], "w_up_t" + layer: full["w_up" + layer],
                "w_down" + layer: full["w_down" + layer]}

    scatter = {}

    def put_grads(grp, g):
        if grp == "in0":
            g = {"ev_w_in": _w_in_grad(g["w_in_t"])}
        elif grp == "mix0":
            g = {"ev_w_uq": _w_uq_grad(g["w_uq_t"]), "ev_w_ukv": _w_ukv_grad(g["w_ukv_t"]),
                 "ev_w_out": g["ev_w_out"]}
        elif grp == "mix1":
            g = {"od_w_qkv": g["od_w_qkv_t"], "od_w_out": g["od_w_out"]}
        else:
            layer = grp[-1]
            g = {"w_gate" + layer: g["w_gate_t"], "w_up" + layer: g["w_up_t"], "w_down" + layer: g["w_down"]}
        names = _GROUPS[grp]
        send = [g[n].reshape(N_DEV, 1, g[n].shape[0] // N_DEV, g[n].shape[1]).astype(BF16) for n in names]
        handle, tok = _exchange_start(send, [True] * len(names), send[0], name="scatter_start_" + grp)
        scatter[grp] = handle
        return tok

    small = {"g_cq": ev_g_cq, "g_ckv": ev_g_ckv, "od_rel_bias": od_rel_bias[0],
             "g_mix": g_mix + token[0, 0], "g_ffn": g_ffn, "g_final": g_final.reshape(1, -1)}
    loss_part, dx, G = _local_step(x[0], loss_target[0], small, get_weights, put_grads)
    loss = lax.psum(loss_part, ("x", "y", "c"))
    g_small = _pack_small([G["g_cq"], G["g_ckv"], G["od_rel_bias"], G["g_mix"], G["g_ffn"], G["g_final"]])
    small_handle, _ = _exchange_start([g_small], [False], dx, name="gather_start_small")

    grads, deltas, new_m, new_v = {}, {}, {}, {}
    parts, after = {}, dx

    def wait_parts(grp, after):
        lands = _exchange_wait(scatter[grp], after, name="scatter_wait_" + grp)
        for n, l in zip(_GROUPS[grp], lands):
            parts[n] = jnp.swapaxes(l, 2, 3) if _GROUP_SRC.get(n, (n, 0))[0] in _COL_SHARDED else l
        return lands[0]

    def adamw(n):
        shp = w[n].shape
        r2 = (-1, shp[-1])
        res = _adamw(w[n].reshape(r2), parts[n].reshape((N_DEV,) + w[n].reshape(r2).shape),
                     mom[n].reshape(r2), var[n].reshape(r2), name="adamw_" + n)
        grads[n], deltas[n], new_m[n], new_v[n] = [r.reshape(shp) for r in res]
        return res[0]

    for grp in ("ffn1", "mix1", "ffn0", "mix0"):
        after = wait_parts(grp, after)
    for n in ("w_gate", "w_up", "w_down"):
        parts[n] = jnp.concatenate([parts[n + "0"], parts[n + "1"]], axis=1)
    for n in _BIG[1:]:
        after = adamw(n)
    after = wait_parts("in0", after)
    after = adamw("ev_w_in")
    small_w = [w[n] for n in _SMALL]
    small_parts = _exchange_wait(small_handle, after, name="gather_wait_small")[0]
    res = _adamw(_pack_small(small_w), small_parts, _pack_small([mom[n] for n in _SMALL]),
                 _pack_small([var[n] for n in _SMALL]), name="adamw_small")
    for d, packed in zip((grads, deltas, new_m, new_v), res):
        for n, val in zip(_SMALL, _unpack_small(packed, small_w)):
            d[n] = val

    order = ["ev_w_in", "ev_g_cq", "ev_w_uq", "ev_g_ckv", "ev_w_ukv", "ev_w_out", "od_w_qkv", "od_rel_bias",
             "od_w_out", "g_mix", "g_ffn", "w_gate", "w_up", "w_down", "g_final"]
    out = [loss, dx[None]]
    for d in (grads, deltas, new_m, new_v):
        out += [d[n] for n in order]
    return tuple(out)
```

```python
import functools

import numpy as np
import jax
import jax.numpy as jnp
from jax import lax
from jax.experimental import pallas as pl
from jax.experimental.pallas import tpu as pltpu

F32 = jnp.float32
BF16 = jnp.bfloat16

D_MODEL = 1024
CHUNK = 64
MLA_HEADS = 8
MLA_NOPE = 64
MLA_ROPE = 32
MLA_V = 64
Q_LORA = 384
KV_LORA = 256
ROPE_THETA = 10000.0
SB_HEADS = 8
SB_DIM = 64
C_HEADS = 16
C_DIM = 64
LEFT_CHUNKS = 8
REL_CLIP = 256
D_FF = 2816
RMS_EPS = 1e-6
ADAM_LR = 0.001
ADAM_B1 = 0.9
ADAM_B2 = 0.999
ADAM_EPS = 1e-08
ADAM_WD = 0.01
ADAM_STEP = 10

N_DEV = 8
LANES = 128
VMEM_LIMIT = 56 * 1024 * 1024
NEG = -1e30
PAD_KEYS = LEFT_CHUNKS * CHUNK
BAND_TQ = 128
BAND_W = BAND_TQ + PAD_KEYS
TOEP_W = BAND_W + BAND_TQ

NN = (((1,), (0,)), ((), ()))
NT = (((1,), (1,)), ((), ()))
TN = (((0,), (0,)), ((), ()))


def _dot(a, b, dn):
    return lax.dot_general(a, b, dn, preferred_element_type=F32)


def _pick(dim, pref):
    if dim <= pref:
        return dim
    best = None
    for t in range(LANES, pref + 1, LANES):
        if dim % t == 0:
            best = t
    assert best is not None, (dim, pref)
    return best


def _params(sem):
    return pltpu.CompilerParams(dimension_semantics=sem, vmem_limit_bytes=VMEM_LIMIT)


def _mm(a, b, dims="nn", res=None, out_dtype=F32, name="mm"):
    if dims == "nn":
        (M, K), (K2, N) = a.shape, b.shape
    elif dims == "nt":
        (M, K), (N, K2) = a.shape, b.shape
    else:
        (K, M), (K2, N) = a.shape, b.shape
    assert K == K2, (a.shape, b.shape, dims)
    tm, tn, tk = _pick(M, 512), _pick(N, 1408), _pick(K, 1408)
    nk = K // tk
    dn = {"nn": NN, "nt": NT, "tn": TN}[dims]
    has_res = res is not None

    def body(*refs):
        if has_res:
            a_ref, b_ref, r_ref, o_ref, acc = refs
        else:
            a_ref, b_ref, o_ref, acc = refs
        k = pl.program_id(2)

        @pl.when(k == 0)
        def _():
            acc[...] = jnp.zeros_like(acc)

        acc[...] += _dot(a_ref[...].astype(BF16), b_ref[...].astype(BF16), dn)

        @pl.when(k == nk - 1)
        def _():
            r = acc[...]
            if has_res:
                r = r + r_ref[...]
            o_ref[...] = r.astype(out_dtype)

    a_spec = (pl.BlockSpec((tk, tm), lambda i, j, k: (k, i)) if dims == "tn"
              else pl.BlockSpec((tm, tk), lambda i, j, k: (i, k)))
    b_spec = (pl.BlockSpec((tn, tk), lambda i, j, k: (j, k)) if dims == "nt"
              else pl.BlockSpec((tk, tn), lambda i, j, k: (k, j)))
    o_spec = pl.BlockSpec((tm, tn), lambda i, j, k: (i, j))
    in_specs = [a_spec, b_spec] + ([o_spec] if has_res else [])
    args = (a, b) + ((res,) if has_res else ())
    return pl.pallas_call(
        body, name=name, grid=(M // tm, N // tn, nk),
        in_specs=in_specs, out_specs=o_spec,
        out_shape=jax.ShapeDtypeStruct((M, N), out_dtype),
        scratch_shapes=[pltpu.VMEM((tm, tn), F32)],
        compiler_params=_params(("parallel", "parallel", "arbitrary")),
    )(*args)


def _rms_fwd(x, g, out_dtype=BF16, name="rms_fwd"):
    T, Fd = x.shape
    tm = _pick(T, 256)

    def body(x_ref, g_ref, o_ref):
        xv = x_ref[...]
        r = lax.rsqrt(jnp.mean(xv * xv, axis=-1, keepdims=True) + RMS_EPS)
        o_ref[...] = (xv * r * g_ref[...]).astype(out_dtype)

    return pl.pallas_call(
        body, name=name, grid=(T // tm,),
        in_specs=[pl.BlockSpec((tm, Fd), lambda i: (i, 0)), pl.BlockSpec((1, Fd), lambda i: (0, 0))],
        out_specs=pl.BlockSpec((tm, Fd), lambda i: (i, 0)),
        out_shape=jax.ShapeDtypeStruct((T, Fd), out_dtype),
        compiler_params=_params(("parallel",)),
    )(x, g)


def _rms_bwd(x, g, dy, dres=None, name="rms_bwd"):
    T, Fd = x.shape
    tm = _pick(T, 256)
    has_res = dres is not None

    def body(*refs):
        if has_res:
            x_ref, g_ref, dy_ref, r_ref, dx_ref, dxb_ref, dg_ref = refs
        else:
            x_ref, g_ref, dy_ref, dx_ref, dxb_ref, dg_ref = refs
        xv, dyv = x_ref[...], dy_ref[...]
        r = lax.rsqrt(jnp.mean(xv * xv, axis=-1, keepdims=True) + RMS_EPS)
        gdy = dyv * g_ref[...]
        dot = jnp.mean(xv * gdy, axis=-1, keepdims=True)
        dx = r * gdy - xv * (r * r * r * dot)
        if has_res:
            dx = dx + r_ref[...]
        dx_ref[...] = dx
        dxb_ref[...] = dx.astype(BF16)

        @pl.when(pl.program_id(0) == 0)
        def _():
            dg_ref[...] = jnp.zeros_like(dg_ref)

        dg_ref[...] += jnp.sum(dyv * xv * r, axis=0, keepdims=True)

    row = pl.BlockSpec((tm, Fd), lambda i: (i, 0))
    vec = pl.BlockSpec((1, Fd), lambda i: (0, 0))
    in_specs = [row, vec, row] + ([row] if has_res else [])
    args = (x, g, dy) + ((dres,) if has_res else ())
    return pl.pallas_call(
        body, name=name, grid=(T // tm,),
        in_specs=in_specs, out_specs=[row, row, vec],
        out_shape=[jax.ShapeDtypeStruct((T, Fd), F32), jax.ShapeDtypeStruct((T, Fd), BF16),
                   jax.ShapeDtypeStruct((1, Fd), F32)],
        compiler_params=_params(("arbitrary",)),
    )(*args)


def _loss_head(h, g, target, name="loss_head"):
    T, Fd = h.shape
    tm = _pick(T, 256)

    def body(h_ref, g_ref, t_ref, loss_ref, dh_ref, dhb_ref, dg_ref):
        xv = h_ref[...]
        r = lax.rsqrt(jnp.mean(xv * xv, axis=-1, keepdims=True) + RMS_EPS)
        diff = xv * r * g_ref[...] - t_ref[...]
        part = 0.5 * jnp.sum(jnp.mean(diff * diff, axis=-1, keepdims=True), axis=0, keepdims=True)
        dyv = diff * (1.0 / Fd)
        gdy = dyv * g_ref[...]
        dot = jnp.mean(xv * gdy, axis=-1, keepdims=True)
        dh = r * gdy - xv * (r * r * r * dot)
        dh_ref[...] = dh
        dhb_ref[...] = dh.astype(BF16)

        @pl.when(pl.program_id(0) == 0)
        def _():
            dg_ref[...] = jnp.zeros_like(dg_ref)
            loss_ref[...] = jnp.zeros_like(loss_ref)

        dg_ref[...] += jnp.sum(dyv * xv * r, axis=0, keepdims=True)
        loss_ref[...] += jnp.broadcast_to(part, loss_ref.shape)

    row = pl.BlockSpec((tm, Fd), lambda i: (i, 0))
    vec = pl.BlockSpec((1, Fd), lambda i: (0, 0))
    return pl.pallas_call(
        body, name=name, grid=(T // tm,),
        in_specs=[row, vec, row],
        out_specs=[pl.BlockSpec((1, LANES), lambda i: (0, 0)), row, row, vec],
        out_shape=[jax.ShapeDtypeStruct((1, LANES), F32), jax.ShapeDtypeStruct((T, Fd), F32),
                   jax.ShapeDtypeStruct((T, Fd), BF16), jax.ShapeDtypeStruct((1, Fd), F32)],
        compiler_params=_params(("arbitrary",)),
    )(h, g, target)


FFN_TF = 256


def _ffn_fwd(h, g, wg_t, wu_t, wd, name="ffn_fwd"):
    T, Dm = h.shape
    Fh = wd.shape[0]
    tm = _pick(T, 1024)
    nf = Fh // FFN_TF

    def body(h_ref, g_ref, wg_ref, wu_ref, wd_ref, o_ref, u_ref, a_ref, b_ref):
        j = pl.program_id(1)

        @pl.when(j == 0)
        def _():
            xv = h_ref[...]
            r = lax.rsqrt(jnp.mean(xv * xv, axis=-1, keepdims=True) + RMS_EPS)
            u_ref[...] = (xv * r * g_ref[...]).astype(BF16)
            o_ref[...] = xv

        u = u_ref[...]
        a = _dot(u, wg_ref[...], NT).astype(BF16)
        b = _dot(u, wu_ref[...], NT).astype(BF16)
        a_ref[...] = a
        b_ref[...] = b
        af = a.astype(F32)
        s = (af * jax.nn.sigmoid(af) * b.astype(F32)).astype(BF16)
        o_ref[...] += _dot(s, wd_ref[...], NN)

    row = pl.BlockSpec((tm, Dm), lambda i, j: (i, 0))
    wblk = pl.BlockSpec((FFN_TF, Dm), lambda i, j: (j, 0))
    ablk = pl.BlockSpec((tm, FFN_TF), lambda i, j: (i, j))
    return pl.pallas_call(
        body, name=name, grid=(T // tm, nf),
        in_specs=[row, pl.BlockSpec((1, Dm), lambda i, j: (0, 0)), wblk, wblk, wblk],
        out_specs=[row, row, ablk, ablk],
        out_shape=[jax.ShapeDtypeStruct((T, Dm), F32), jax.ShapeDtypeStruct((T, Dm), BF16),
                   jax.ShapeDtypeStruct((T, Fh), BF16), jax.ShapeDtypeStruct((T, Fh), BF16)],
        compiler_params=_params(("parallel", "arbitrary")),
    )(h, g, wg_t, wu_t, wd)


def _ffn_bwd(dh, u, a, b, wg_t, wu_t, wd, name="ffn_bwd"):
    T, Dm = dh.shape
    Fh = wd.shape[0]
    nf = Fh // FFN_TF
    once = pl.Buffered(1)

    def body(dh_ref, u_ref, a_ref, b_ref, wg_ref, wu_ref, wd_ref, du_ref, dwg_ref, dwu_ref, dwd_ref):
        j = pl.program_id(0)

        @pl.when(j == 0)
        def _():
            du_ref[...] = jnp.zeros_like(du_ref)

        ds = _dot(dh_ref[...], wd_ref[...], NT)
        af, bf = a_ref[...].astype(F32), b_ref[...].astype(F32)
        sig = jax.nn.sigmoid(af)
        sa = af * sig
        dwd_ref[...] = _dot((sa * bf).astype(BF16), dh_ref[...], TN).astype(BF16)
        dab = jnp.concatenate([(ds * bf * (sig * (1.0 + af * (1.0 - sig)))).astype(BF16),
                               (ds * sa).astype(BF16)], axis=1)
        dw = _dot(dab, u_ref[...], TN)
        dwg_ref[...] = dw[:FFN_TF].astype(BF16)
        dwu_ref[...] = dw[FFN_TF:].astype(BF16)
        du_ref[...] += _dot(dab, jnp.concatenate([wg_ref[...], wu_ref[...]], axis=0), NN)

    full = lambda: pl.BlockSpec((T, Dm), lambda j: (0, 0), pipeline_mode=once)
    wblk = pl.BlockSpec((FFN_TF, Dm), lambda j: (j, 0))
    ablk = pl.BlockSpec((T, FFN_TF), lambda j: (0, j))
    return pl.pallas_call(
        body, name=name, grid=(nf,),
        in_specs=[full(), full(), ablk, ablk, wblk, wblk, wblk],
        out_specs=[pl.BlockSpec((T, Dm), lambda j: (0, 0)), wblk, wblk, wblk],
        out_shape=[jax.ShapeDtypeStruct((T, Dm), F32)] + [jax.ShapeDtypeStruct((Fh, Dm), BF16)] * 3,
        compiler_params=_params(("arbitrary",)),
    )(dh, u, a, b, wg_t, wu_t, wd)


def _rope(x, cos_t, sin_t, col0, ncols, out_dtype, name="rope"):
    T = x.shape[0]
    wt = cos_t.shape[1]
    tm = _pick(T, 256)
    nb = ncols * LANES // wt
    half = MLA_ROPE // 2

    def body(x_ref, c_ref, s_ref, o_ref):
        xv = x_ref[...].astype(F32)
        lane = lax.broadcasted_iota(jnp.int32, xv.shape, 1)
        first = (lane & (MLA_ROPE - 1)) < half
        swapped = jnp.where(first, pltpu.roll(xv, wt - half, 1), pltpu.roll(xv, half, 1))
        o_ref[...] = (xv * c_ref[...] + swapped * s_ref[...]).astype(out_dtype)

    off = col0 * LANES // wt
    return pl.pallas_call(
        body, name=name, grid=(T // tm, nb),
        in_specs=[pl.BlockSpec((tm, wt), lambda i, j: (i, j + off)),
                  pl.BlockSpec((tm, wt), lambda i, j: (i, 0)),
                  pl.BlockSpec((tm, wt), lambda i, j: (i, 0))],
        out_specs=pl.BlockSpec((tm, wt), lambda i, j: (i, j)),
        out_shape=jax.ShapeDtypeStruct((T, ncols * LANES), out_dtype),
        compiler_params=_params(("parallel", "parallel")),
    )(x, cos_t, sin_t)


ATT_T = 256


def _mla_masks(shape):
    lane = lax.broadcasted_iota(jnp.int32, shape, 1)
    m0 = (lane < 64) | ((lane >= 128) & (lane < 160))
    m1 = ((lane >= 64) & (lane < 128)) | ((lane >= 160) & (lane < 192))
    return m0, m1


def _by_twos(n, step, carry):
    carry = lax.fori_loop(0, n // 2, lambda i, c: step(2 * i + 1, step(2 * i, c)), carry)
    return lax.fori_loop(0, n % 2, lambda _, c: step(n - 1, c), carry)


def _chunk_ok(tq, tk):
    row = lax.broadcasted_iota(jnp.int32, (tq, tk), 0)
    col = lax.broadcasted_iota(jnp.int32, (tq, tk), 1)
    return (col >> 6) <= (row >> 6)


def _mla_fwd(q, kv, kr, name="mla_fwd"):
    T = q.shape[0]
    tq = tk = _pick(T, ATT_T)
    npair = MLA_HEADS // 2
    scale = (MLA_NOPE + MLA_ROPE) ** -0.5

    def body(q_ref, kn_ref, v_ref, kr_ref, o_ref, lse_ref):
        m_idx = pl.program_id(1)
        qv = q_ref[...]
        m0, m1 = _mla_masks(qv.shape)
        qh = (jnp.where(m0, qv, 0).astype(BF16), jnp.where(m1, qv, 0).astype(BF16))

        def block(kb, carry, ok):
            ks = pl.ds(pl.multiple_of(kb * tk, tk), tk)
            kcat = jnp.concatenate([kn_ref[ks, :], kr_ref[ks, :]], axis=1)
            vv = v_ref[ks, :]
            out = []
            for h in range(2):
                mx, l, acc = carry[3 * h:3 * h + 3]
                s = _dot(qh[h], kcat, NT) * scale
                if ok is not None:
                    s = jnp.where(ok, s, NEG)
                mn = jnp.maximum(mx, jnp.max(s, axis=-1, keepdims=True))
                alpha = jnp.exp(mx - mn)
                p = jnp.exp(s - mn)
                l = alpha * l + jnp.sum(p, axis=-1, keepdims=True)
                acc = alpha * acc + _dot(p.astype(BF16), vv, NN)
                out += [mn, l, acc]
            return tuple(out)

        init = (jnp.full((tq, 1), NEG, F32), jnp.zeros((tq, 1), F32), jnp.zeros((tq, LANES), F32)) * 2
        res = block(m_idx, init, _chunk_ok(tq, tk))
        res = _by_twos(m_idx, lambda kb, c: block(kb, c, None), res)
        lane = lax.broadcasted_iota(jnp.int32, (tq, LANES), 1)
        o0 = res[2] / res[1]
        o1 = res[5] / res[4]
        o_ref[...] = jnp.where(lane < 64, o0, o1).astype(o_ref.dtype)
        lse_ref[...] = jnp.where(lane < 64, res[0] + jnp.log(res[1]), res[3] + jnp.log(res[4]))

    full = lambda col: pl.BlockSpec((T, LANES), col)
    return pl.pallas_call(
        body, name=name, grid=(npair, T // tq),
        in_specs=[pl.BlockSpec((tq, 2 * LANES), lambda p, m: (m, p)),
                  full(lambda p, m: (0, p)), full(lambda p, m: (0, npair + p)), full(lambda p, m: (0, 0))],
        out_specs=[pl.BlockSpec((tq, LANES), lambda p, m: (m, p)),
                   pl.BlockSpec((tq, LANES), lambda p, m: (m, p))],
        out_shape=[jax.ShapeDtypeStruct((T, npair * LANES), BF16),
                   jax.ShapeDtypeStruct((T, npair * LANES), F32)],
        compiler_params=_params(("parallel", "arbitrary")),
    )(q, kv, kv, kr)


def _mla_bwd(q, kv, kr, o, lse, do, do_col0, name="mla_bwd"):
    T = q.shape[0]
    tq = tk = _pick(T, ATT_T)
    npair = MLA_HEADS // 2
    scale = (MLA_NOPE + MLA_ROPE) ** -0.5

    def body(q_ref, kn_ref, v_ref, kr_ref, o_ref, lse_ref, do_ref, dq_ref, dkn_ref, dv_ref, dkr_ref):
        p_idx, m_idx = pl.program_id(0), pl.program_id(1)

        @pl.when(m_idx == 0)
        def _():
            dkn_ref[...] = jnp.zeros_like(dkn_ref)
            dv_ref[...] = jnp.zeros_like(dv_ref)

        @pl.when((m_idx == 0) & (p_idx == 0))
        def _():
            dkr_ref[...] = jnp.zeros_like(dkr_ref)

        qv = q_ref[...]
        m0, m1 = _mla_masks(qv.shape)
        qh = (jnp.where(m0, qv, 0).astype(BF16), jnp.where(m1, qv, 0).astype(BF16))
        dov = do_ref[...].astype(F32)
        lane = lax.broadcasted_iota(jnp.int32, (tq, LANES), 1)
        h0 = lane < 64
        prod = dov * o_ref[...].astype(F32)
        delta = (jnp.sum(jnp.where(h0, prod, 0.0), axis=-1, keepdims=True),
                 jnp.sum(jnp.where(h0, 0.0, prod), axis=-1, keepdims=True))
        doh = (jnp.where(h0, dov, 0.0).astype(BF16), jnp.where(h0, 0.0, dov).astype(BF16))
        lsev = lse_ref[...]
        lse_h = (lsev[:, 0:1], lsev[:, 64:65])

        def block(kb, carry, ok):
            ks = pl.ds(pl.multiple_of(kb * tk, tk), tk)
            kcat = jnp.concatenate([kn_ref[ks, :], kr_ref[ks, :]], axis=1)
            vv = v_ref[ks, :]
            dkc = jnp.zeros((tk, 2 * LANES), F32)
            dvv = jnp.zeros((tk, LANES), F32)
            out = []
            for h in range(2):
                s = _dot(qh[h], kcat, NT) * scale
                p = jnp.exp(s - lse_h[h])
                if ok is not None:
                    p = jnp.where(ok, p, 0.0)
                dp = _dot(doh[h], vv, NT)
                ds = (p * (dp - delta[h]) * scale).astype(BF16)
                out.append(carry[h] + _dot(ds, kcat, NN))
                dkc = dkc + _dot(ds, qh[h], TN)
                dvv = dvv + _dot(p.astype(BF16), doh[h], TN)
            dkn_ref[ks, :] += dkc[:, :LANES]
            dkr_ref[ks, :] += dkc[:, LANES:]
            dv_ref[ks, :] += dvv
            return tuple(out)

        init = (jnp.zeros((tq, 2 * LANES), F32),) * 2
        res = block(m_idx, init, _chunk_ok(tq, tk))
        dq0, dq1 = _by_twos(m_idx, lambda kb, c: block(kb, c, None), res)
        dq_ref[...] = jnp.where(m0, dq0, jnp.where(m1, dq1, 0.0))

    full = lambda col: pl.BlockSpec((T, LANES), col)
    blk = lambda col: pl.BlockSpec((tq, LANES), col)
    return pl.pallas_call(
        body, name=name, grid=(npair, T // tq),
        in_specs=[pl.BlockSpec((tq, 2 * LANES), lambda p, m: (m, p)),
                  full(lambda p, m: (0, p)), full(lambda p, m: (0, npair + p)), full(lambda p, m: (0, 0)),
                  blk(lambda p, m: (m, p)), blk(lambda p, m: (m, p)),
                  blk(lambda p, m: (m, do_col0 + p))],
        out_specs=[pl.BlockSpec((tq, 2 * LANES), lambda p, m: (m, p)),
                   full(lambda p, m: (0, p)), full(lambda p, m: (0, p)), full(lambda p, m: (0, 0))],
        out_shape=[jax.ShapeDtypeStruct((T, npair * 2 * LANES), F32),
                   jax.ShapeDtypeStruct((T, npair * LANES), F32),
                   jax.ShapeDtypeStruct((T, npair * LANES), F32),
                   jax.ShapeDtypeStruct((T, LANES), F32)],
        compiler_params=_params(("arbitrary", "arbitrary")),
    )(q, kv, kv, kr, o, lse, do)


def _split_dot(x, tri):
    hi = x.astype(BF16)
    lo = (x - hi.astype(F32)).astype(BF16)
    return _dot(hi, tri, NN) + _dot(lo, tri, NN)


def _sb_terms(qh, kk, before):
    z = _dot(qh, kk, NT)
    sp = jnp.maximum(z, 0.0) + jnp.log(1.0 + jnp.exp(-jnp.abs(z)))
    lk = -sp if before is None else jnp.where(before, -sp, 0.0)
    return z, sp, lk


def _sb_setup(q_ref, tq, tk, scale):
    qv = (q_ref[...].astype(F32) * scale).astype(BF16)
    lane = lax.broadcasted_iota(jnp.int32, (tq, LANES), 1)
    h0 = lane < 64
    qh = (jnp.where(h0, qv, 0).astype(BF16), jnp.where(h0, 0, qv).astype(BF16))
    row = lax.broadcasted_iota(jnp.int32, (tk, tk), 0)
    col = lax.broadcasted_iota(jnp.int32, (tk, tk), 1)
    return qh, h0, row, col


def _sb_fwd(qkv, col0, name="sb_fwd"):
    T = qkv.shape[0]
    tq = tk = _pick(T, ATT_T)
    npair = SB_HEADS // 2
    scale = SB_DIM ** -0.5

    def body(q_ref, k_ref, v_ref, o_ref, o32_ref):
        m_idx = pl.program_id(1)
        qh, h0, row, col = _sb_setup(q_ref, tq, tk, scale)
        later = (row > col).astype(BF16)

        def block(kb, carry, before):
            ks = pl.ds(pl.multiple_of(kb * tk, tk), tk)
            kk = k_ref[ks, :].astype(BF16)
            vv = v_ref[ks, :].astype(BF16)
            out = []
            for h in range(2):
                c, acc = carry[2 * h:2 * h + 2]
                z, sp, lk = _sb_terms(qh[h], kk, before)
                w = jnp.exp((z - sp) + _split_dot(lk, later) + c)
                if before is not None:
                    w = jnp.where(before, w, 0.0)
                out += [c + jnp.sum(lk, axis=-1, keepdims=True), acc + _dot(w.astype(BF16), vv, NN)]
            return tuple(out)

        init = (jnp.zeros((tq, 1), F32), jnp.zeros((tq, LANES), F32)) * 2
        res = block(m_idx, init, col < row)
        res = _by_twos(m_idx, lambda i, c: block(m_idx - 1 - i, c, None), res)
        o = jnp.where(h0, res[1], res[3])
        o_ref[...] = o.astype(o_ref.dtype)
        o32_ref[...] = o

    full = lambda col: pl.BlockSpec((T, LANES), col)
    blk = pl.BlockSpec((tq, LANES), lambda p, m: (m, p))
    return pl.pallas_call(
        body, name=name, grid=(npair, T // tq),
        in_specs=[pl.BlockSpec((tq, LANES), lambda p, m: (m, col0 + p)),
                  full(lambda p, m: (0, col0 + npair + p)), full(lambda p, m: (0, col0 + 2 * npair + p))],
        out_specs=[blk, blk],
        out_shape=[jax.ShapeDtypeStruct((T, npair * LANES), BF16), jax.ShapeDtypeStruct((T, npair * LANES), F32)],
        compiler_params=_params(("parallel", "arbitrary")),
    )(qkv, qkv, qkv)


def _sb_bwd(qkv, col0, o32, do, do_col0, dep, name="sb_bwd"):
    T = qkv.shape[0]
    tq = tk = _pick(T, ATT_T)
    npair = SB_HEADS // 2
    scale = SB_DIM ** -0.5

    def body(q_ref, k_ref, v_ref, o_ref, do_ref, dep_ref, dq_ref, dk_ref, dv_ref):
        m_idx = pl.program_id(1)

        @pl.when(m_idx == 0)
        def _():
            dk_ref[...] = jnp.zeros_like(dk_ref)
            dv_ref[...] = jnp.zeros_like(dv_ref)

        qh, h0, row, col = _sb_setup(q_ref, tq, tk, scale)
        dov = do_ref[...].astype(F32)
        doh = (jnp.where(h0, dov, 0.0).astype(BF16), jnp.where(h0, 0.0, dov).astype(BF16))
        ov = o_ref[...]
        etot = [jnp.sum(doh[h].astype(F32) * ov, axis=-1, keepdims=True) for h in range(2)]
        later = (row > col).astype(BF16)
        from_here = (row >= col).astype(BF16)

        def block(kb, carry, before):
            ks = pl.ds(pl.multiple_of(kb * tk, tk), tk)
            kk = k_ref[ks, :].astype(BF16)
            vv = v_ref[ks, :].astype(BF16)
            dkk = jnp.zeros((tk, LANES), F32)
            dvv = jnp.zeros((tk, LANES), F32)
            out = []
            for h in range(2):
                c, es, dqa = carry[3 * h:3 * h + 3]
                z, sp, lk = _sb_terms(qh[h], kk, before)
                w = jnp.exp((z - sp) + _split_dot(lk, later) + c)
                if before is not None:
                    w = jnp.where(before, w, 0.0)
                wb = w.astype(BF16)
                e = wb.astype(F32) * _dot(doh[h], vv, NT)
                prev = etot[h] - (_split_dot(e, from_here) + es)
                sig_neg = jnp.exp(-sp)
                dz = e * sig_neg - (1.0 - sig_neg) * prev
                if before is not None:
                    dz = jnp.where(before, dz, 0.0)
                dzb = dz.astype(BF16)
                dkk = dkk + _dot(dzb, qh[h], TN)
                dvv = dvv + _dot(wb, doh[h], TN)
                out += [c + jnp.sum(lk, axis=-1, keepdims=True), es + jnp.sum(e, axis=-1, keepdims=True),
                        dqa + _dot(dzb, kk, NN)]
            dk_ref[ks, :] += dkk
            dv_ref[ks, :] += dvv
            return tuple(out)

        init = (jnp.zeros((tq, 1), F32), jnp.zeros((tq, 1), F32), jnp.zeros((tq, LANES), F32)) * 2
        res = block(m_idx, init, col < row)
        res = _by_twos(m_idx, lambda i, c: block(m_idx - 1 - i, c, None), res)
        dq_ref[...] = jnp.where(h0, res[2], res[5]) * scale

    full = lambda col: pl.BlockSpec((T, LANES), col)
    blk = lambda col: pl.BlockSpec((tq, LANES), col)
    return pl.pallas_call(
        body, name=name, grid=(npair, T // tq),
        in_specs=[blk(lambda p, m: (m, col0 + p)),
                  full(lambda p, m: (0, col0 + npair + p)), full(lambda p, m: (0, col0 + 2 * npair + p)),
                  blk(lambda p, m: (m, p)), blk(lambda p, m: (m, do_col0 + p)),
                  pl.BlockSpec((8, LANES), lambda p, m: (0, 0))],
        out_specs=[blk(lambda p, m: (m, p)), full(lambda p, m: (0, p)), full(lambda p, m: (0, p))],
        out_shape=[jax.ShapeDtypeStruct((T, npair * LANES), F32)] * 3,
        compiler_params=_params(("arbitrary", "arbitrary")),
    )(qkv, qkv, qkv, o32, do, dep)


def _band_in_window():
    cq = lax.broadcasted_iota(jnp.int32, (BAND_TQ, BAND_W), 0) >> 6
    ckp = lax.broadcasted_iota(jnp.int32, (BAND_TQ, BAND_W), 1) >> 6
    return (ckp >= cq) & (ckp <= cq + LEFT_CHUNKS)


def _band_real(m_idx):
    j = lax.broadcasted_iota(jnp.int32, (BAND_TQ, BAND_W), 1)
    return j >= PAD_KEYS - m_idx * BAND_TQ


def _band_probs(qh, kw, bias, real, scale):
    s = jnp.where(real, _dot(qh, kw, NT) * scale + bias, NEG)
    e = jnp.exp(s - jnp.max(s, axis=-1, keepdims=True))
    return e / jnp.sum(e, axis=-1, keepdims=True)


BAND_SUB = 2


def _band_fwd(qkv, k_pad, v_pad, bias_w, name="band_fwd"):
    T = qkv.shape[0]
    npair = C_HEADS // 2
    scale = C_DIM ** -0.5
    rows = BAND_SUB * BAND_TQ

    def body(q_ref, k_ref, v_ref, b_ref, o_ref):
        lane = lax.broadcasted_iota(jnp.int32, (BAND_TQ, LANES), 1)
        h0 = lane < 64
        for sub in range(BAND_SUB):
            m_idx = pl.program_id(1) * BAND_SUB + sub
            win = pl.ds(pl.multiple_of(m_idx * BAND_TQ, BAND_TQ), BAND_W)
            kw, vw = k_ref[win, :], v_ref[win, :]
            qv = q_ref[sub * BAND_TQ:(sub + 1) * BAND_TQ, :]
            qh = (jnp.where(h0, qv, 0).astype(BF16), jnp.where(h0, 0, qv).astype(BF16))
            real = _band_real(m_idx)
            o = [_dot(_band_probs(qh[h], kw, b_ref[h], real, scale).astype(BF16), vw, NN) for h in range(2)]
            o_ref[sub * BAND_TQ:(sub + 1) * BAND_TQ, :] = jnp.where(h0, o[0], o[1]).astype(o_ref.dtype)

    Tp = T + PAD_KEYS
    return pl.pallas_call(
        body, name=name, grid=(npair, T // rows),
        in_specs=[pl.BlockSpec((rows, LANES), lambda p, m: (m, p)),
                  pl.BlockSpec((Tp, LANES), lambda p, m: (0, p)),
                  pl.BlockSpec((Tp, LANES), lambda p, m: (0, p)),
                  pl.BlockSpec((2, BAND_TQ, BAND_W), lambda p, m: (p, 0, 0))],
        out_specs=pl.BlockSpec((rows, LANES), lambda p, m: (m, p)),
        out_shape=jax.ShapeDtypeStruct((T, npair * LANES), BF16),
        compiler_params=_params(("parallel", "arbitrary")),
    )(qkv, k_pad, v_pad, bias_w)


def _band_bwd(qkv, k_pad, v_pad, bias_w, do, name="band_bwd"):
    T = qkv.shape[0]
    npair = C_HEADS // 2
    scale = C_DIM ** -0.5

    rows = BAND_SUB * BAND_TQ

    def body(q_ref, k_ref, v_ref, b_ref, do_ref, dq_ref, dk_ref, dv_ref, db_ref):
        @pl.when(pl.program_id(1) == 0)
        def _():
            dk_ref[...] = jnp.zeros_like(dk_ref)
            dv_ref[...] = jnp.zeros_like(dv_ref)
            db_ref[...] = jnp.zeros_like(db_ref)

        lane = lax.broadcasted_iota(jnp.int32, (BAND_TQ, LANES), 1)
        h0 = lane < 64
        dbs = [jnp.zeros((BAND_TQ, BAND_W), F32)] * 2
        for sub in range(BAND_SUB):
            m_idx = pl.program_id(1) * BAND_SUB + sub
            win = pl.ds(pl.multiple_of(m_idx * BAND_TQ, BAND_TQ), BAND_W)
            kw, vw = k_ref[win, :], v_ref[win, :]
            qv = q_ref[sub * BAND_TQ:(sub + 1) * BAND_TQ, :]
            dov = do_ref[sub * BAND_TQ:(sub + 1) * BAND_TQ, :].astype(F32)
            qh = (jnp.where(h0, qv, 0).astype(BF16), jnp.where(h0, 0, qv).astype(BF16))
            doh = (jnp.where(h0, dov, 0.0).astype(BF16), jnp.where(h0, 0.0, dov).astype(BF16))
            real = _band_real(m_idx)
            dq = []
            dkw = jnp.zeros((BAND_W, LANES), F32)
            dvw = jnp.zeros((BAND_W, LANES), F32)
            for h in range(2):
                p = _band_probs(qh[h], kw, b_ref[h], real, scale)
                dp = _dot(doh[h], vw, NT)
                dsb = p * (dp - jnp.sum(p * dp, axis=-1, keepdims=True))
                dbs[h] = dbs[h] + dsb
                dsq = (dsb * scale).astype(BF16)
                dq.append(_dot(dsq, kw, NN))
                dkw = dkw + _dot(dsq, qh[h], TN)
                dvw = dvw + _dot(p.astype(BF16), doh[h], TN)
            dq_ref[sub * BAND_TQ:(sub + 1) * BAND_TQ, :] = jnp.where(h0, dq[0], dq[1])
            dk_ref[win, :] += dkw
            dv_ref[win, :] += dvw
        for h in range(2):
            db_ref[h] += dbs[h]

    Tp = T + PAD_KEYS
    blk = lambda col: pl.BlockSpec((rows, LANES), col)
    full = pl.BlockSpec((Tp, LANES), lambda p, m: (0, p))
    bias = pl.BlockSpec((2, BAND_TQ, BAND_W), lambda p, m: (p, 0, 0))
    return pl.pallas_call(
        body, name=name, grid=(npair, T // rows),
        in_specs=[blk(lambda p, m: (m, p)), full, full, bias, blk(lambda p, m: (m, p))],
        out_specs=[blk(lambda p, m: (m, p)), full, full, bias],
        out_shape=[jax.ShapeDtypeStruct((T, npair * LANES), F32),
                   jax.ShapeDtypeStruct((Tp, npair * LANES), F32),
                   jax.ShapeDtypeStruct((Tp, npair * LANES), F32),
                   jax.ShapeDtypeStruct((C_HEADS, BAND_TQ, BAND_W), F32)],
        compiler_params=_params(("arbitrary", "arbitrary")),
    )(qkv, k_pad, v_pad, bias_w, do)


def _skew_bits(x, left):
    w = x.shape[1]
    row = lax.broadcasted_iota(jnp.int32, x.shape, 0)
    for b in range(BAND_TQ.bit_length() - 1):
        amt = (w - (1 << b)) if left else (1 << b)
        x = jnp.where((row >> b) & 1 == 1, pltpu.roll(x, amt, 1), x)
    return x


def _toeplitz(diag, name="toeplitz"):
    H = diag.shape[0]

    def body(d_ref, o_ref):
        x = jnp.broadcast_to(d_ref[0], (BAND_TQ, TOEP_W))
        o_ref[0] = jnp.where(_band_in_window(), _skew_bits(x, left=False)[:, BAND_TQ:], NEG)

    return pl.pallas_call(
        body, name=name, grid=(H,),
        in_specs=[pl.BlockSpec((1, 1, TOEP_W), lambda h: (h, 0, 0))],
        out_specs=pl.BlockSpec((1, BAND_TQ, BAND_W), lambda h: (h, 0, 0)),
        out_shape=jax.ShapeDtypeStruct((H, BAND_TQ, BAND_W), F32),
        compiler_params=_params(("parallel",)),
    )(diag.reshape(H, 1, TOEP_W))


def _toeplitz_bwd(dbias, name="toeplitz_bwd"):
    H = dbias.shape[0]

    def body(d_ref, o_ref):
        x = jnp.concatenate([jnp.zeros((BAND_TQ, BAND_TQ), F32), d_ref[0]], axis=1)
        o_ref[0] = jnp.sum(_skew_bits(x, left=True), axis=0, keepdims=True)

    return pl.pallas_call(
        body, name=name, grid=(H,),
        in_specs=[pl.BlockSpec((1, BAND_TQ, BAND_W), lambda h: (h, 0, 0))],
        out_specs=pl.BlockSpec((1, 1, TOEP_W), lambda h: (h, 0, 0)),
        out_shape=jax.ShapeDtypeStruct((H, 1, TOEP_W), F32),
        compiler_params=_params(("parallel",)),
    )(dbias).reshape(H, TOEP_W)


_HBM = pl.BlockSpec(memory_space=pltpu.HBM)
_SEM = pl.BlockSpec(memory_space=pltpu.SEMAPHORE)
_EFFECT = pltpu.SideEffectType.DATAFLOW_SIDE_EFFECTING


def _peers():
    x, y, c = lax.axis_index("x"), lax.axis_index("y"), lax.axis_index("c")
    out = []
    for k in range(1, N_DEV):
        peer = (1 - x if (k >> 2) & 1 else x, 1 - y if (k >> 1) & 1 else y, 1 - c if k & 1 else c)
        out.append((peer, 4 * peer[0] + 2 * peer[1] + peer[2]))
    return 4 * x + 2 * y + c, out


def _split_copies(ins, lands, scatter, send_sem, recv_sem, arriving):
    me, peers = _peers()
    out = []
    for a in range(len(ins)):
        for peer, idx in peers:
            out.append(pltpu.make_async_remote_copy(
                src_ref=ins[a].at[idx] if scatter[a] else ins[a],
                dst_ref=lands[a].at[idx if arriving else me], send_sem=send_sem, recv_sem=recv_sem,
                device_id=peer, device_id_type=pl.DeviceIdType.MESH))
    return out


def _landing_zones(arrays, scatter):
    return [lax.empty((N_DEV,) + (a.shape[1:] if s else a.shape), a.dtype) for a, s in zip(arrays, scatter)]


def _place_own(arrays, scatter, name):
    n = len(arrays)
    lands = _landing_zones(arrays, scatter)
    me = (4 * lax.axis_index("x") + 2 * lax.axis_index("y") + lax.axis_index("c")).astype(jnp.int32).reshape(1)

    def body(me_ref, *refs):
        for a in range(n):
            refs[2 * n + a][...] = refs[a][...].reshape(refs[2 * n + a].shape)

    def row_spec(shape):
        zeros = (0,) * (len(shape) - 1)
        return pl.BlockSpec((1,) + tuple(shape[1:]), lambda i, me_ref: (me_ref[0],) + zeros)

    in_specs = [row_spec(a.shape) if s else pl.BlockSpec(a.shape, lambda i, me_ref, nd=a.ndim: (0,) * nd)
                for a, s in zip(arrays, scatter)]
    return pl.pallas_call(
        body, name=name,
        out_shape=[jax.ShapeDtypeStruct(l.shape, l.dtype) for l in lands],
        grid_spec=pltpu.PrefetchScalarGridSpec(
            num_scalar_prefetch=1, grid=(1,),
            in_specs=in_specs + [pl.BlockSpec(memory_space=pl.ANY)] * n,
            out_specs=[row_spec(l.shape) for l in lands]),
        input_output_aliases={1 + n + i: i for i in range(n)},
        compiler_params=_params(("arbitrary",)),
    )(me, *arrays, *lands)


def _exchange_start(arrays, scatter, after, name):
    n = len(arrays)
    lands = list(_place_own(arrays, scatter, name=name.replace("_start_", "_own_")))

    def body(*refs):
        ins, lnd = refs[:n], refs[n:2 * n]
        send_sem, recv_sem = refs[2 * n + 1:2 * n + 3]
        token = refs[-1]
        for cp in _split_copies(ins, lnd, scatter, send_sem, recv_sem, arriving=False):
            cp.start()
        token[...] = jnp.zeros_like(token)

    hbm = lambda a: pltpu.HBM(a.shape, a.dtype)
    out = pl.pallas_call(
        body, name=name,
        out_shape=(pltpu.SemaphoreType.DMA(()), pltpu.SemaphoreType.DMA(()),
                   *[hbm(a) for a in arrays], *[hbm(a) for a in lands],
                   jax.ShapeDtypeStruct((8, LANES), F32)),
        in_specs=[_HBM] * (2 * n) + [pl.BlockSpec(memory_space=pl.ANY)],
        out_specs=(_SEM, _SEM, *([_HBM] * (2 * n)), pl.BlockSpec(memory_space=pltpu.VMEM)),
        input_output_aliases={i: 2 + i for i in range(2 * n)},
        compiler_params=pltpu.CompilerParams(has_side_effects=_EFFECT),
    )(*[pltpu.with_memory_space_constraint(a, pltpu.HBM) for a in list(arrays) + lands], after)
    return (out[0], out[1], list(out[2:2 + n]), list(out[2 + n:2 + 2 * n]), tuple(scatter)), out[-1]


def _exchange_wait(handle, after, name):
    send_sem, recv_sem, ins, lands, scatter = handle
    n = len(ins)

    def body(*refs):
        i_ref, l_ref = refs[:n], refs[n:2 * n]
        s_sem, r_sem = refs[2 * n:2 * n + 2]
        for cp in _split_copies(i_ref, l_ref, scatter, s_sem, r_sem, arriving=False):
            cp.wait_send()
        for cp in _split_copies(i_ref, l_ref, scatter, s_sem, r_sem, arriving=True):
            cp.wait_recv()

    hbm = lambda a: pltpu.HBM(a.shape, a.dtype)
    out = pl.pallas_call(
        body, name=name,
        out_shape=tuple(hbm(a) for a in ins + lands),
        in_specs=[_HBM] * (2 * n) + [_SEM, _SEM, pl.BlockSpec(memory_space=pl.ANY)],
        out_specs=tuple([_HBM] * (2 * n)),
        input_output_aliases={i: i for i in range(2 * n)},
        compiler_params=pltpu.CompilerParams(has_side_effects=_EFFECT),
    )(*ins, *lands, send_sem, recv_sem, after)
    return list(out[n:])


def _adamw(w, parts, m, v, name="adamw"):
    R, C = w.shape
    tr = max([t for t in range(16, 513, 16) if R % t == 0], default=R)
    c1 = 1.0 - ADAM_B1 ** ADAM_STEP
    c2 = 1.0 - ADAM_B2 ** ADAM_STEP

    def body(w_ref, p_ref, m_ref, v_ref, g_ref, d_ref, nm_ref, nv_ref):
        g = p_ref[0].astype(F32)
        for i in range(1, N_DEV):
            g = g + p_ref[i].astype(F32)
        nm = ADAM_B1 * m_ref[...] + (1.0 - ADAM_B1) * g
        nv = ADAM_B2 * v_ref[...] + (1.0 - ADAM_B2) * (g * g)
        g_ref[...] = g
        nm_ref[...] = nm
        nv_ref[...] = nv
        d_ref[...] = -ADAM_LR * ((nm / c1) / (jnp.sqrt(nv / c2) + ADAM_EPS) + ADAM_WD * w_ref[...])

    blk = pl.BlockSpec((tr, C), lambda i: (i, 0))
    return pl.pallas_call(
        body, name=name, grid=(R // tr,),
        in_specs=[blk, pl.BlockSpec((N_DEV, tr, C), lambda i: (0, i, 0)), blk, blk],
        out_specs=[blk] * 4,
        out_shape=[jax.ShapeDtypeStruct((R, C), F32)] * 4,
        compiler_params=_params(("parallel",)),
    )(w, parts, m, v)


_O1 = Q_LORA
_O2 = _O1 + KV_LORA
_O3 = _O2 + MLA_ROPE
_NB = SB_HEADS * SB_DIM
IN_W = _O2 + LANES + 3 * _NB
COL_KR = _O2 // LANES
COL_SB = COL_KR + 1


def _w_in_local(w):
    kr = w[_O2:_O3]
    pad = jnp.zeros((LANES - 2 * MLA_ROPE, w.shape[1]), w.dtype)
    return jnp.concatenate([w[:_O2], kr, kr, pad, w[_O3:]], axis=0)


def _w_in_grad(g):
    kr = (g[_O2:_O2 + MLA_ROPE].astype(F32) + g[_O2 + MLA_ROPE:_O2 + 2 * MLA_ROPE].astype(F32)).astype(g.dtype)
    return jnp.concatenate([g[:_O2], kr, g[_O2 + LANES:]], axis=0)


def _w_uq_local(w):
    w3 = w.reshape(MLA_HEADS // 2, 2, MLA_NOPE + MLA_ROPE, w.shape[1])
    nope = w3[:, :, :MLA_NOPE].reshape(MLA_HEADS // 2, 2 * MLA_NOPE, w.shape[1])
    rope = w3[:, :, MLA_NOPE:].reshape(MLA_HEADS // 2, 2 * MLA_ROPE, w.shape[1])
    pad = jnp.zeros((MLA_HEADS // 2, LANES - 2 * MLA_ROPE, w.shape[1]), w.dtype)
    return jnp.concatenate([nope, rope, pad], axis=1).reshape(-1, w.shape[1])


def _w_uq_grad(g):
    g3 = g.reshape(MLA_HEADS // 2, 2 * LANES, g.shape[1])
    nope = g3[:, :2 * MLA_NOPE].reshape(MLA_HEADS // 2, 2, MLA_NOPE, g.shape[1])
    rope = g3[:, LANES:LANES + 2 * MLA_ROPE].reshape(MLA_HEADS // 2, 2, MLA_ROPE, g.shape[1])
    return jnp.concatenate([nope, rope], axis=2).reshape(-1, g.shape[1])


def _w_ukv_local(w):
    w3 = w.reshape(MLA_HEADS, MLA_NOPE + MLA_V, w.shape[1])
    return jnp.concatenate([w3[:, :MLA_NOPE].reshape(-1, w.shape[1]),
                            w3[:, MLA_NOPE:].reshape(-1, w.shape[1])], axis=0)


def _w_ukv_grad(g):
    half = MLA_HEADS * MLA_NOPE
    kn = g[:half].reshape(MLA_HEADS, MLA_NOPE, g.shape[1])
    vv = g[half:].reshape(MLA_HEADS, MLA_V, g.shape[1])
    return jnp.concatenate([kn, vv], axis=1).reshape(-1, g.shape[1])


def _rope_tables(T):
    pos = jnp.arange(T, dtype=F32)
    inv_freq = ROPE_THETA ** (-jnp.arange(0, MLA_ROPE, 2, dtype=F32) / MLA_ROPE)
    ang = pos[:, None] * inv_freq[None, :]
    cos, sin = jnp.cos(ang), jnp.sin(ang)
    ones = jnp.ones((T, LANES - 2 * MLA_ROPE), F32)
    cos_k = jnp.concatenate([cos, cos, cos, cos, ones], axis=1)
    sin_k = jnp.concatenate([-sin, sin, -sin, sin, 0.0 * ones], axis=1)
    cos_q = jnp.concatenate([jnp.ones((T, LANES), F32), cos_k], axis=1)
    sin_q = jnp.concatenate([jnp.zeros((T, LANES), F32), sin_k], axis=1)
    return cos_q, sin_q, cos_k, sin_k


def _bias_diag_index():
    ell = np.arange(TOEP_W)
    return np.clip(BAND_W - ell, -REL_CLIP, REL_CLIP) + REL_CLIP


def _local_step(x, target, small, get_weights, put_grads):
    T = x.shape[0]
    cos_q, sin_q, cos_k, sin_k = _rope_tables(T)
    G = {}
    W = dict(small)

    u0 = _rms_fwd(x, W["g_mix"][0:1], name="rms_mix0")
    W.update(get_weights("in0", u0))
    proj = _mm(u0, W["w_in_t"], dims="nt", name="proj_in")
    W.update(get_weights("mix0", proj))
    c_q, c_kv = proj[:, :_O1], proj[:, _O1:_O2]
    nq = _rms_fwd(c_q, W["g_cq"], name="rms_cq")
    nkv = _rms_fwd(c_kv, W["g_ckv"], name="rms_ckv")
    qa_raw = _mm(nq, W["w_uq_t"], dims="nt", name="proj_uq")
    qa = _rope(qa_raw, cos_q, sin_q, 0, qa_raw.shape[1] // LANES, BF16, name="rope_q")
    kv = _mm(nkv, W["w_ukv_t"], dims="nt", out_dtype=BF16, name="proj_ukv")
    kr = _rope(proj, cos_k, sin_k, COL_KR, 1, BF16, name="rope_k")
    o_a, lse = _mla_fwd(qa, kv, kr)
    o_b, o_b32 = _sb_fwd(proj, COL_SB)
    o_ab = jnp.concatenate([o_a, o_b], axis=1)
    h1 = _mm(o_ab, W["ev_w_out"], res=x, name="out_ev")

    def ffn_fwd(h, layer):
        W.update(get_weights(f"ffn{layer}", h))
        return _ffn_fwd(h, W["g_ffn"][layer:layer + 1], W[f"w_gate_t{layer}"], W[f"w_up_t{layer}"],
                        W[f"w_down{layer}"], name=f"ffn_fwd{layer}")

    h2, u1, a0, b0 = ffn_fwd(h1, 0)

    W.update(get_weights("mix1", h2))
    u2 = _rms_fwd(h2, W["g_mix"][1:2], name="rms_mix1")
    qkv = _mm(u2, W["od_w_qkv_t"], dims="nt", out_dtype=BF16, name="proj_qkv")
    nc = C_HEADS * C_DIM
    pad = ((PAD_KEYS, 0), (0, 0))
    k_pad, v_pad = jnp.pad(qkv[:, nc:2 * nc], pad), jnp.pad(qkv[:, 2 * nc:], pad)
    diag_idx = _bias_diag_index()
    bias_w = _toeplitz(W["od_rel_bias"][:, diag_idx])
    o_c = _band_fwd(qkv, k_pad, v_pad, bias_w)
    h3 = _mm(o_c, W["od_w_out"], res=h2, name="out_od")
    h4, u3, a1, b1 = ffn_fwd(h3, 1)

    loss, dh, dhb, G["g_final"] = _loss_head(h4, W["g_final"], target)

    def ffn_bwd(dh, dhb, h, u, a, b, layer):
        du, g_gate, g_up, g_down = _ffn_bwd(dhb, u, a, b, W[f"w_gate_t{layer}"], W[f"w_up_t{layer}"],
                                            W[f"w_down{layer}"], name=f"ffn_bwd{layer}")
        tok = put_grads(f"ffn{layer}", {"w_gate_t": g_gate, "w_up_t": g_up, "w_down": g_down})
        return _rms_bwd(h, W["g_ffn"][layer:layer + 1] + tok[:1, :1], du, dres=dh, name=f"rms_ffn_bwd{layer}")

    dh3, dh3b, g_gffn1 = ffn_bwd(dh, dhb, h3, u3, a1, b1, 1)

    do_c = _mm(dh3b, W["od_w_out"], dims="nt", name="out_od_dx")
    g_od_out = _mm(o_c, dh3b, dims="tn", out_dtype=BF16, name="out_od_dw")
    dq_c, dk_p, dv_p, dbias_w = _band_bwd(qkv, k_pad, v_pad, bias_w, do_c)
    dqkv = jnp.concatenate([dq_c, dk_p[PAD_KEYS:], dv_p[PAD_KEYS:]], axis=1)
    du2 = _mm(dqkv, W["od_w_qkv_t"], name="proj_qkv_dx")
    tok = put_grads("mix1", {"od_w_qkv_t": _mm(dqkv, u2, dims="tn", out_dtype=BF16, name="proj_qkv_dw"),
                             "od_w_out": g_od_out})
    ddiag = _toeplitz_bwd(dbias_w)
    n_far = BAND_W - REL_CLIP + 1
    G["od_rel_bias"] = jnp.concatenate(
        [jnp.zeros((C_HEADS, REL_CLIP - BAND_TQ + 1), F32), ddiag[:, n_far:][:, ::-1],
         jnp.sum(ddiag[:, :n_far], axis=1, keepdims=True)], axis=1)
    dh2, dh2b, g_gmix1 = _rms_bwd(h2, W["g_mix"][1:2] + tok[:1, :1], du2, dres=dh3, name="rms_mix_bwd1")

    dh1, dh1b, g_gffn0 = ffn_bwd(dh2, dh2b, h1, u1, a0, b0, 0)
    G["g_ffn"] = jnp.concatenate([g_gffn0, g_gffn1], axis=0)

    do_ab = _mm(dh1b, W["ev_w_out"], dims="nt", name="out_ev_dx")
    g0 = {"ev_w_out": _mm(o_ab, dh1b, dims="tn", out_dtype=BF16, name="out_ev_dw")}
    dqa, dkn, dva, dkr = _mla_bwd(qa, kv, kr, o_a, lse, do_ab, 0)
    dqa_raw = _rope(dqa, cos_q, -sin_q, 0, dqa.shape[1] // LANES, F32, name="rope_q_bwd")
    g0["w_uq_t"] = _mm(dqa_raw, nq, dims="tn", name="proj_uq_dw")
    dnq = _mm(dqa_raw, W["w_uq_t"], name="proj_uq_dx")
    dc_q, _, G["g_cq"] = _rms_bwd(c_q, W["g_cq"], dnq, name="rms_cq_bwd")
    dkv = jnp.concatenate([dkn, dva], axis=1)
    g0["w_ukv_t"] = _mm(dkv, nkv, dims="tn", name="proj_ukv_dw")
    dnkv = _mm(dkv, W["w_ukv_t"], name="proj_ukv_dx")
    dc_kv, _, G["g_ckv"] = _rms_bwd(c_kv, W["g_ckv"], dnkv, name="rms_ckv_bwd")
    tok = put_grads("mix0", g0)
    dqb, dkb, dvb = _sb_bwd(proj, COL_SB, o_b32, do_ab, MLA_HEADS // 2, tok)
    dkr_raw = _rope(dkr, cos_k, -sin_k, 0, 1, F32, name="rope_k_bwd")
    dproj = jnp.concatenate([dc_q, dc_kv, dkr_raw, dqb, dkb, dvb], axis=1)
    du0 = _mm(dproj, W["w_in_t"], name="proj_in_dx")
    tok = put_grads("in0", {"w_in_t": _mm(dproj, u0, dims="tn", name="proj_in_dw")})
    dx, _, g_gmix0 = _rms_bwd(x, W["g_mix"][0:1] + tok[:1, :1], du0, dres=dh1, name="rms_mix_bwd0")
    G["g_mix"] = jnp.concatenate([g_gmix0, g_gmix1], axis=0)
    return loss[0, 0], dx, G


_BIG = ["ev_w_in", "ev_w_uq", "ev_w_ukv", "ev_w_out", "od_w_qkv", "od_w_out", "w_gate", "w_up", "w_down"]
_COL_SHARDED = {"ev_w_in", "ev_w_uq", "ev_w_ukv", "od_w_qkv", "w_gate", "w_up"}
_SMALL = ["ev_g_cq", "ev_g_ckv", "od_rel_bias", "g_mix", "g_ffn", "g_final"]
_GROUPS = {
    "in0": ["ev_w_in"],
    "mix0": ["ev_w_uq", "ev_w_ukv", "ev_w_out"],
    "ffn0": ["w_gate0", "w_up0", "w_down0"],
    "mix1": ["od_w_qkv", "od_w_out"],
    "ffn1": ["w_gate1", "w_up1", "w_down1"],
}
_GROUP_SRC = {n + str(l): (n, l) for n in ("w_gate", "w_up", "w_down") for l in (0, 1)}
_SMALL_ROWS = 8
_SMALL_COLS = 1792


def _pack_small(vals):
    flat = jnp.concatenate([v.reshape(-1).astype(F32) for v in vals])
    flat = jnp.pad(flat, (0, _SMALL_ROWS * _SMALL_COLS - flat.shape[0]))
    return flat.reshape(_SMALL_ROWS, _SMALL_COLS)


def _unpack_small(packed, like):
    flat = packed.reshape(-1)
    out, off = [], 0
    for v in like:
        out.append(flat[off:off + v.size].reshape(v.shape))
        off += v.size
    return out


def kernel(x, ev_w_in, ev_g_cq, ev_w_uq, ev_g_ckv, ev_w_ukv, ev_w_out, od_w_qkv, od_rel_bias, od_w_out, g_mix, g_ffn, w_gate, w_up, w_down, g_final, loss_target, m_ev_w_in, m_ev_g_cq, m_ev_w_uq, m_ev_g_ckv, m_ev_w_ukv, m_ev_w_out, m_od_w_qkv, m_od_rel_bias, m_od_w_out, m_g_mix, m_g_ffn, m_w_gate, m_w_up, m_w_down, m_g_final, v_ev_w_in, v_ev_g_cq, v_ev_w_uq, v_ev_g_ckv, v_ev_w_ukv, v_ev_w_out, v_od_w_qkv, v_od_rel_bias, v_od_w_out, v_g_mix, v_g_ffn, v_w_gate, v_w_up, v_w_down, v_g_final):
    args = dict(locals())
    w = {n: args[n] for n in _BIG + _SMALL}
    mom = {n: args["m_" + n] for n in _BIG + _SMALL}
    var = {n: args["v_" + n] for n in _BIG + _SMALL}

    own = {}
    for grp, names in _GROUPS.items():
        for n in names:
            base, layer = _GROUP_SRC.get(n, (n, 0))
            shard = w[base][layer:layer + 1]
            own[n] = (jnp.swapaxes(shard, 1, 2) if base in _COL_SHARDED else shard).astype(BF16)
    gather, token = {}, x[0, :8, :LANES]
    for grp, names in _GROUPS.items():
        gather[grp], token = _exchange_start([own[n] for n in names], [False] * len(names), token,
                                             name="gather_start_" + grp)

    def get_weights(grp, after):
        names = _GROUPS[grp]
        lands = _exchange_wait(gather[grp], token if after is None else after, name="gather_wait_" + grp)
        full = {n: l.reshape(-1, l.shape[-1]) for n, l in zip(names, lands)}
        if grp == "in0":
            return {"w_in_t": _w_in_local(full["ev_w_in"])}
        if grp == "mix0":
            return {"w_uq_t": _w_uq_local(full["ev_w_uq"]), "w_ukv_t": _w_ukv_local(full["ev_w_ukv"]),
                    "ev_w_out": full["ev_w_out"]}
        if grp == "mix1":
            return {"od_w_qkv_t": full["od_w_qkv"], "od_w_out": full["od_w_out"]}
        layer = grp[-1]
        return {"w_gate_t" + layer: full["w_gate" + layer], "w_up_t" + layer: full["w_up" + layer],
                "w_down" + layer: full["w_down" + layer]}

    scatter = {}

    def put_grads(grp, g):
        if grp == "in0":
            g = {"ev_w_in": _w_in_grad(g["w_in_t"])}
        elif grp == "mix0":
            g = {"ev_w_uq": _w_uq_grad(g["w_uq_t"]), "ev_w_ukv": _w_ukv_grad(g["w_ukv_t"]),
                 "ev_w_out": g["ev_w_out"]}
        elif grp == "mix1":
            g = {"od_w_qkv": g["od_w_qkv_t"], "od_w_out": g["od_w_out"]}
        else:
            layer = grp[-1]
            g = {"w_gate" + layer: g["w_gate_t"], "w_up" + layer: g["w_up_t"], "w_down" + layer: g["w_down"]}
        names = _GROUPS[grp]
        send = [g[n].reshape(N_DEV, 1, g[n].shape[0] // N_DEV, g[n].shape[1]).astype(BF16) for n in names]
        handle, tok = _exchange_start(send, [True] * len(names), send[0], name="scatter_start_" + grp)
        scatter[grp] = handle
        return tok

    small = {"g_cq": ev_g_cq, "g_ckv": ev_g_ckv, "od_rel_bias": od_rel_bias[0],
             "g_mix": g_mix + token[0, 0], "g_ffn": g_ffn, "g_final": g_final.reshape(1, -1)}
    loss_part, dx, G = _local_step(x[0], loss_target[0], small, get_weights, put_grads)
    loss = lax.psum(loss_part, ("x", "y", "c"))
    g_small = _pack_small([G["g_cq"], G["g_ckv"], G["od_rel_bias"], G["g_mix"], G["g_ffn"], G["g_final"]])
    small_handle, _ = _exchange_start([g_small], [False], dx, name="gather_start_small")

    grads, deltas, new_m, new_v = {}, {}, {}, {}
    parts, after = {}, dx

    def wait_parts(grp, after):
        lands = _exchange_wait(scatter[grp], after, name="scatter_wait_" + grp)
        parts.update(zip(_GROUPS[grp], lands))
        return lands[0]

    def adamw(n):
        col = n in _COL_SHARDED
        rows = lambda a: (jnp.swapaxes(a, 1, 2) if col else a).reshape(-1, a.shape[1 if col else 2])
        res = _adamw(rows(w[n]), parts[n].reshape(N_DEV, -1, parts[n].shape[-1]), rows(mom[n]), rows(var[n]),
                     name="adamw_" + n)
        L, a1, a2 = w[n].shape
        back = lambda r: jnp.swapaxes(r.reshape(L, a2, a1), 1, 2) if col else r.reshape(L, a1, a2)
        grads[n], deltas[n], new_m[n], new_v[n] = [back(r) for r in res]
        return res[0]

    for grp in ("ffn1", "mix1", "ffn0", "mix0"):
        after = wait_parts(grp, after)
    for n in ("w_gate", "w_up", "w_down"):
        parts[n] = jnp.concatenate([parts[n + "0"], parts[n + "1"]], axis=1)
    for n in _BIG[1:]:
        after = adamw(n)
    after = wait_parts("in0", after)
    after = adamw("ev_w_in")
    small_w = [w[n] for n in _SMALL]
    small_parts = _exchange_wait(small_handle, after, name="gather_wait_small")[0]
    res = _adamw(_pack_small(small_w), small_parts, _pack_small([mom[n] for n in _SMALL]),
                 _pack_small([var[n] for n in _SMALL]), name="adamw_small")
    for d, packed in zip((grads, deltas, new_m, new_v), res):
        for n, val in zip(_SMALL, _unpack_small(packed, small_w)):
            d[n] = val

    order = ["ev_w_in", "ev_g_cq", "ev_w_uq", "ev_g_ckv", "ev_w_ukv", "ev_w_out", "od_w_qkv", "od_rel_bias",
             "od_w_out", "g_mix", "g_ffn", "w_gate", "w_up", "w_down", "g_final"]
    out = [loss, dx[None]]
    for d in (grads, deltas, new_m, new_v):
        out += [d[n] for n in order]
    return tuple(out)
```

```python
import functools

import numpy as np
import jax
import jax.numpy as jnp
from jax import lax
from jax.experimental import pallas as pl
from jax.experimental.pallas import tpu as pltpu

F32 = jnp.float32
BF16 = jnp.bfloat16

D_MODEL = 1024
CHUNK = 64
MLA_HEADS = 8
MLA_NOPE = 64
MLA_ROPE = 32
MLA_V = 64
Q_LORA = 384
KV_LORA = 256
ROPE_THETA = 10000.0
SB_HEADS = 8
SB_DIM = 64
C_HEADS = 16
C_DIM = 64
LEFT_CHUNKS = 8
REL_CLIP = 256
D_FF = 2816
RMS_EPS = 1e-6
ADAM_LR = 0.001
ADAM_B1 = 0.9
ADAM_B2 = 0.999
ADAM_EPS = 1e-08
ADAM_WD = 0.01
ADAM_STEP = 10

N_DEV = 8
LANES = 128
VMEM_LIMIT = 56 * 1024 * 1024
NEG = -1e30
PAD_KEYS = LEFT_CHUNKS * CHUNK
BAND_TQ = 128
BAND_W = BAND_TQ + PAD_KEYS
TOEP_W = BAND_W + BAND_TQ

NN = (((1,), (0,)), ((), ()))
NT = (((1,), (1,)), ((), ()))
TN = (((0,), (0,)), ((), ()))


def _dot(a, b, dn):
    return lax.dot_general(a, b, dn, preferred_element_type=F32)


def _pick(dim, pref):
    if dim <= pref:
        return dim
    best = None
    for t in range(LANES, pref + 1, LANES):
        if dim % t == 0:
            best = t
    assert best is not None, (dim, pref)
    return best


def _params(sem):
    return pltpu.CompilerParams(dimension_semantics=sem, vmem_limit_bytes=VMEM_LIMIT)


def _mm(a, b, dims="nn", res=None, out_dtype=F32, name="mm"):
    if dims == "nn":
        (M, K), (K2, N) = a.shape, b.shape
    elif dims == "nt":
        (M, K), (N, K2) = a.shape, b.shape
    else:
        (K, M), (K2, N) = a.shape, b.shape
    assert K == K2, (a.shape, b.shape, dims)
    tm, tn, tk = _pick(M, 512), _pick(N, 1408), _pick(K, 1408)
    nk = K // tk
    dn = {"nn": NN, "nt": NT, "tn": TN}[dims]
    has_res = res is not None

    def body(*refs):
        if has_res:
            a_ref, b_ref, r_ref, o_ref, acc = refs
        else:
            a_ref, b_ref, o_ref, acc = refs
        k = pl.program_id(2)

        @pl.when(k == 0)
        def _():
            acc[...] = jnp.zeros_like(acc)

        acc[...] += _dot(a_ref[...].astype(BF16), b_ref[...].astype(BF16), dn)

        @pl.when(k == nk - 1)
        def _():
            r = acc[...]
            if has_res:
                r = r + r_ref[...]
            o_ref[...] = r.astype(out_dtype)

    a_spec = (pl.BlockSpec((tk, tm), lambda i, j, k: (k, i)) if dims == "tn"
              else pl.BlockSpec((tm, tk), lambda i, j, k: (i, k)))
    b_spec = (pl.BlockSpec((tn, tk), lambda i, j, k: (j, k)) if dims == "nt"
              else pl.BlockSpec((tk, tn), lambda i, j, k: (k, j)))
    o_spec = pl.BlockSpec((tm, tn), lambda i, j, k: (i, j))
    in_specs = [a_spec, b_spec] + ([o_spec] if has_res else [])
    args = (a, b) + ((res,) if has_res else ())
    return pl.pallas_call(
        body, name=name, grid=(M // tm, N // tn, nk),
        in_specs=in_specs, out_specs=o_spec,
        out_shape=jax.ShapeDtypeStruct((M, N), out_dtype),
        scratch_shapes=[pltpu.VMEM((tm, tn), F32)],
        compiler_params=_params(("parallel", "parallel", "arbitrary")),
    )(*args)


def _rms_fwd(x, g, out_dtype=BF16, name="rms_fwd"):
    T, Fd = x.shape
    tm = _pick(T, 256)

    def body(x_ref, g_ref, o_ref):
        xv = x_ref[...]
        r = lax.rsqrt(jnp.mean(xv * xv, axis=-1, keepdims=True) + RMS_EPS)
        o_ref[...] = (xv * r * g_ref[...]).astype(out_dtype)

    return pl.pallas_call(
        body, name=name, grid=(T // tm,),
        in_specs=[pl.BlockSpec((tm, Fd), lambda i: (i, 0)), pl.BlockSpec((1, Fd), lambda i: (0, 0))],
        out_specs=pl.BlockSpec((tm, Fd), lambda i: (i, 0)),
        out_shape=jax.ShapeDtypeStruct((T, Fd), out_dtype),
        compiler_params=_params(("parallel",)),
    )(x, g)


def _rms_bwd(x, g, dy, dres=None, name="rms_bwd"):
    T, Fd = x.shape
    tm = _pick(T, 256)
    has_res = dres is not None

    def body(*refs):
        if has_res:
            x_ref, g_ref, dy_ref, r_ref, dx_ref, dxb_ref, dg_ref = refs
        else:
            x_ref, g_ref, dy_ref, dx_ref, dxb_ref, dg_ref = refs
        xv, dyv = x_ref[...], dy_ref[...]
        r = lax.rsqrt(jnp.mean(xv * xv, axis=-1, keepdims=True) + RMS_EPS)
        gdy = dyv * g_ref[...]
        dot = jnp.mean(xv * gdy, axis=-1, keepdims=True)
        dx = r * gdy - xv * (r * r * r * dot)
        if has_res:
            dx = dx + r_ref[...]
        dx_ref[...] = dx
        dxb_ref[...] = dx.astype(BF16)

        @pl.when(pl.program_id(0) == 0)
        def _():
            dg_ref[...] = jnp.zeros_like(dg_ref)

        dg_ref[...] += jnp.sum(dyv * xv * r, axis=0, keepdims=True)

    row = pl.BlockSpec((tm, Fd), lambda i: (i, 0))
    vec = pl.BlockSpec((1, Fd), lambda i: (0, 0))
    in_specs = [row, vec, row] + ([row] if has_res else [])
    args = (x, g, dy) + ((dres,) if has_res else ())
    return pl.pallas_call(
        body, name=name, grid=(T // tm,),
        in_specs=in_specs, out_specs=[row, row, vec],
        out_shape=[jax.ShapeDtypeStruct((T, Fd), F32), jax.ShapeDtypeStruct((T, Fd), BF16),
                   jax.ShapeDtypeStruct((1, Fd), F32)],
        compiler_params=_params(("arbitrary",)),
    )(*args)


def _loss_head(h, g, target, name="loss_head"):
    T, Fd = h.shape
    tm = _pick(T, 256)

    def body(h_ref, g_ref, t_ref, loss_ref, dh_ref, dhb_ref, dg_ref):
        xv = h_ref[...]
        r = lax.rsqrt(jnp.mean(xv * xv, axis=-1, keepdims=True) + RMS_EPS)
        diff = xv * r * g_ref[...] - t_ref[...]
        part = 0.5 * jnp.sum(jnp.mean(diff * diff, axis=-1, keepdims=True), axis=0, keepdims=True)
        dyv = diff * (1.0 / Fd)
        gdy = dyv * g_ref[...]
        dot = jnp.mean(xv * gdy, axis=-1, keepdims=True)
        dh = r * gdy - xv * (r * r * r * dot)
        dh_ref[...] = dh
        dhb_ref[...] = dh.astype(BF16)

        @pl.when(pl.program_id(0) == 0)
        def _():
            dg_ref[...] = jnp.zeros_like(dg_ref)
            loss_ref[...] = jnp.zeros_like(loss_ref)

        dg_ref[...] += jnp.sum(dyv * xv * r, axis=0, keepdims=True)
        loss_ref[...] += jnp.broadcast_to(part, loss_ref.shape)

    row = pl.BlockSpec((tm, Fd), lambda i: (i, 0))
    vec = pl.BlockSpec((1, Fd), lambda i: (0, 0))
    return pl.pallas_call(
        body, name=name, grid=(T // tm,),
        in_specs=[row, vec, row],
        out_specs=[pl.BlockSpec((1, LANES), lambda i: (0, 0)), row, row, vec],
        out_shape=[jax.ShapeDtypeStruct((1, LANES), F32), jax.ShapeDtypeStruct((T, Fd), F32),
                   jax.ShapeDtypeStruct((T, Fd), BF16), jax.ShapeDtypeStruct((1, Fd), F32)],
        compiler_params=_params(("arbitrary",)),
    )(h, g, target)


FFN_TF = 256


def _ffn_fwd(h, g, wg_t, wu_t, wd, name="ffn_fwd"):
    T, Dm = h.shape
    Fh = wd.shape[0]
    tm = _pick(T, 1024)
    nf = Fh // FFN_TF

    def body(h_ref, g_ref, wg_ref, wu_ref, wd_ref, o_ref, u_ref, a_ref, b_ref):
        j = pl.program_id(1)

        @pl.when(j == 0)
        def _():
            xv = h_ref[...]
            r = lax.rsqrt(jnp.mean(xv * xv, axis=-1, keepdims=True) + RMS_EPS)
            u_ref[...] = (xv * r * g_ref[...]).astype(BF16)
            o_ref[...] = xv

        u = u_ref[...]
        a = _dot(u, wg_ref[...], NT).astype(BF16)
        b = _dot(u, wu_ref[...], NT).astype(BF16)
        a_ref[...] = a
        b_ref[...] = b
        af = a.astype(F32)
        s = (af * jax.nn.sigmoid(af) * b.astype(F32)).astype(BF16)
        o_ref[...] += _dot(s, wd_ref[...], NN)

    row = pl.BlockSpec((tm, Dm), lambda i, j: (i, 0))
    wblk = pl.BlockSpec((FFN_TF, Dm), lambda i, j: (j, 0))
    ablk = pl.BlockSpec((tm, FFN_TF), lambda i, j: (i, j))
    return pl.pallas_call(
        body, name=name, grid=(T // tm, nf),
        in_specs=[row, pl.BlockSpec((1, Dm), lambda i, j: (0, 0)), wblk, wblk, wblk],
        out_specs=[row, row, ablk, ablk],
        out_shape=[jax.ShapeDtypeStruct((T, Dm), F32), jax.ShapeDtypeStruct((T, Dm), BF16),
                   jax.ShapeDtypeStruct((T, Fh), BF16), jax.ShapeDtypeStruct((T, Fh), BF16)],
        compiler_params=_params(("parallel", "arbitrary")),
    )(h, g, wg_t, wu_t, wd)


def _ffn_bwd(dh, u, a, b, wg_t, wu_t, wd, name="ffn_bwd"):
    T, Dm = dh.shape
    Fh = wd.shape[0]
    nf = Fh // FFN_TF
    once = pl.Buffered(1)

    def body(dh_ref, u_ref, a_ref, b_ref, wg_ref, wu_ref, wd_ref, du_ref, dwg_ref, dwu_ref, dwd_ref):
        j = pl.program_id(0)

        @pl.when(j == 0)
        def _():
            du_ref[...] = jnp.zeros_like(du_ref)

        ds = _dot(dh_ref[...], wd_ref[...], NT)
        af, bf = a_ref[...].astype(F32), b_ref[...].astype(F32)
        sig = jax.nn.sigmoid(af)
        sa = af * sig
        dwd_ref[...] = _dot((sa * bf).astype(BF16), dh_ref[...], TN).astype(BF16)
        dab = jnp.concatenate([(ds * bf * (sig * (1.0 + af * (1.0 - sig)))).astype(BF16),
                               (ds * sa).astype(BF16)], axis=1)
        dw = _dot(dab, u_ref[...], TN)
        dwg_ref[...] = dw[:FFN_TF].astype(BF16)
        dwu_ref[...] = dw[FFN_TF:].astype(BF16)
        du_ref[...] += _dot(dab, jnp.concatenate([wg_ref[...], wu_ref[...]], axis=0), NN)

    full = lambda: pl.BlockSpec((T, Dm), lambda j: (0, 0), pipeline_mode=once)
    wblk = pl.BlockSpec((FFN_TF, Dm), lambda j: (j, 0))
    ablk = pl.BlockSpec((T, FFN_TF), lambda j: (0, j))
    return pl.pallas_call(
        body, name=name, grid=(nf,),
        in_specs=[full(), full(), ablk, ablk, wblk, wblk, wblk],
        out_specs=[pl.BlockSpec((T, Dm), lambda j: (0, 0)), wblk, wblk, wblk],
        out_shape=[jax.ShapeDtypeStruct((T, Dm), F32)] + [jax.ShapeDtypeStruct((Fh, Dm), BF16)] * 3,
        compiler_params=_params(("arbitrary",)),
    )(dh, u, a, b, wg_t, wu_t, wd)


def _rope(x, cos_t, sin_t, col0, ncols, out_dtype, name="rope"):
    T = x.shape[0]
    wt = cos_t.shape[1]
    tm = _pick(T, 256)
    nb = ncols * LANES // wt
    half = MLA_ROPE // 2

    def body(x_ref, c_ref, s_ref, o_ref):
        xv = x_ref[...].astype(F32)
        lane = lax.broadcasted_iota(jnp.int32, xv.shape, 1)
        first = (lane & (MLA_ROPE - 1)) < half
        swapped = jnp.where(first, pltpu.roll(xv, wt - half, 1), pltpu.roll(xv, half, 1))
        o_ref[...] = (xv * c_ref[...] + swapped * s_ref[...]).astype(out_dtype)

    off = col0 * LANES // wt
    return pl.pallas_call(
        body, name=name, grid=(T // tm, nb),
        in_specs=[pl.BlockSpec((tm, wt), lambda i, j: (i, j + off)),
                  pl.BlockSpec((tm, wt), lambda i, j: (i, 0)),
                  pl.BlockSpec((tm, wt), lambda i, j: (i, 0))],
        out_specs=pl.BlockSpec((tm, wt), lambda i, j: (i, j)),
        out_shape=jax.ShapeDtypeStruct((T, ncols * LANES), out_dtype),
        compiler_params=_params(("parallel", "parallel")),
    )(x, cos_t, sin_t)


ATT_T = 256


def _mla_masks(shape):
    lane = lax.broadcasted_iota(jnp.int32, shape, 1)
    m0 = (lane < 64) | ((lane >= 128) & (lane < 160))
    m1 = ((lane >= 64) & (lane < 128)) | ((lane >= 160) & (lane < 192))
    return m0, m1


def _by_twos(n, step, carry):
    carry = lax.fori_loop(0, n // 2, lambda i, c: step(2 * i + 1, step(2 * i, c)), carry)
    return lax.fori_loop(0, n % 2, lambda _, c: step(n - 1, c), carry)


def _chunk_ok(tq, tk):
    row = lax.broadcasted_iota(jnp.int32, (tq, tk), 0)
    col = lax.broadcasted_iota(jnp.int32, (tq, tk), 1)
    return (col >> 6) <= (row >> 6)


def _rotate(x, cos_t, sin_t):
    half = MLA_ROPE // 2
    lane = lax.broadcasted_iota(jnp.int32, x.shape, 1)
    first = (lane & (MLA_ROPE - 1)) < half
    swapped = jnp.where(first, pltpu.roll(x, x.shape[1] - half, 1), pltpu.roll(x, half, 1))
    return x * cos_t + swapped * sin_t


def _mla_fwd(q, cos_q, sin_q, kv, kr, name="mla_fwd"):
    T = q.shape[0]
    tq = tk = _pick(T, ATT_T)
    npair = MLA_HEADS // 2
    scale = (MLA_NOPE + MLA_ROPE) ** -0.5

    def body(q_ref, c_ref, s_ref, kn_ref, v_ref, kr_ref, o_ref, lse_ref):
        m_idx = pl.program_id(1)
        qv = _rotate(q_ref[...], c_ref[...], s_ref[...]).astype(BF16)
        m0, m1 = _mla_masks(qv.shape)
        qh = (jnp.where(m0, qv, 0).astype(BF16), jnp.where(m1, qv, 0).astype(BF16))

        def block(kb, carry, ok):
            ks = pl.ds(pl.multiple_of(kb * tk, tk), tk)
            kcat = jnp.concatenate([kn_ref[ks, :], kr_ref[ks, :]], axis=1)
            vv = v_ref[ks, :]
            out = []
            for h in range(2):
                mx, l, acc = carry[3 * h:3 * h + 3]
                s = _dot(qh[h], kcat, NT) * scale
                if ok is not None:
                    s = jnp.where(ok, s, NEG)
                mn = jnp.maximum(mx, jnp.max(s, axis=-1, keepdims=True))
                alpha = jnp.exp(mx - mn)
                p = jnp.exp(s - mn)
                l = alpha * l + jnp.sum(p, axis=-1, keepdims=True)
                acc = alpha * acc + _dot(p.astype(BF16), vv, NN)
                out += [mn, l, acc]
            return tuple(out)

        init = (jnp.full((tq, 1), NEG, F32), jnp.zeros((tq, 1), F32), jnp.zeros((tq, LANES), F32)) * 2
        res = block(m_idx, init, _chunk_ok(tq, tk))
        res = _by_twos(m_idx, lambda kb, c: block(kb, c, None), res)
        lane = lax.broadcasted_iota(jnp.int32, (tq, LANES), 1)
        o0 = res[2] / res[1]
        o1 = res[5] / res[4]
        o_ref[...] = jnp.where(lane < 64, o0, o1).astype(o_ref.dtype)
        lse_ref[...] = jnp.where(lane < 64, res[0] + jnp.log(res[1]), res[3] + jnp.log(res[4]))

    full = lambda col: pl.BlockSpec((T, LANES), col)
    table = pl.BlockSpec((tq, 2 * LANES), lambda p, m: (m, 0))
    return pl.pallas_call(
        body, name=name, grid=(npair, T // tq),
        in_specs=[pl.BlockSpec((tq, 2 * LANES), lambda p, m: (m, p)), table, table,
                  full(lambda p, m: (0, p)), full(lambda p, m: (0, npair + p)), full(lambda p, m: (0, 0))],
        out_specs=[pl.BlockSpec((tq, LANES), lambda p, m: (m, p)),
                   pl.BlockSpec((tq, LANES), lambda p, m: (m, p))],
        out_shape=[jax.ShapeDtypeStruct((T, npair * LANES), BF16),
                   jax.ShapeDtypeStruct((T, npair * LANES), F32)],
        compiler_params=_params(("parallel", "arbitrary")),
    )(q, cos_q, sin_q, kv, kv, kr)


def _mla_bwd(q, cos_q, sin_q, kv, kr, o, lse, do, do_col0, name="mla_bwd"):
    T = q.shape[0]
    tq = tk = _pick(T, ATT_T)
    npair = MLA_HEADS // 2
    scale = (MLA_NOPE + MLA_ROPE) ** -0.5

    def body(q_ref, c_ref, s_ref, kn_ref, v_ref, kr_ref, o_ref, lse_ref, do_ref, dq_ref, dkn_ref, dv_ref, dkr_ref):
        p_idx, m_idx = pl.program_id(0), pl.program_id(1)

        @pl.when(m_idx == 0)
        def _():
            dkn_ref[...] = jnp.zeros_like(dkn_ref)
            dv_ref[...] = jnp.zeros_like(dv_ref)

        @pl.when((m_idx == 0) & (p_idx == 0))
        def _():
            dkr_ref[...] = jnp.zeros_like(dkr_ref)

        qv = _rotate(q_ref[...], c_ref[...], s_ref[...]).astype(BF16)
        m0, m1 = _mla_masks(qv.shape)
        qh = (jnp.where(m0, qv, 0).astype(BF16), jnp.where(m1, qv, 0).astype(BF16))
        dov = do_ref[...].astype(F32)
        lane = lax.broadcasted_iota(jnp.int32, (tq, LANES), 1)
        h0 = lane < 64
        prod = dov * o_ref[...].astype(F32)
        delta = (jnp.sum(jnp.where(h0, prod, 0.0), axis=-1, keepdims=True),
                 jnp.sum(jnp.where(h0, 0.0, prod), axis=-1, keepdims=True))
        doh = (jnp.where(h0, dov, 0.0).astype(BF16), jnp.where(h0, 0.0, dov).astype(BF16))
        lsev = lse_ref[...]
        lse_h = (lsev[:, 0:1], lsev[:, 64:65])

        def block(kb, carry, ok):
            ks = pl.ds(pl.multiple_of(kb * tk, tk), tk)
            kcat = jnp.concatenate([kn_ref[ks, :], kr_ref[ks, :]], axis=1)
            vv = v_ref[ks, :]
            dkc = jnp.zeros((tk, 2 * LANES), F32)
            dvv = jnp.zeros((tk, LANES), F32)
            out = []
            for h in range(2):
                s = _dot(qh[h], kcat, NT) * scale
                p = jnp.exp(s - lse_h[h])
                if ok is not None:
                    p = jnp.where(ok, p, 0.0)
                dp = _dot(doh[h], vv, NT)
                ds = (p * (dp - delta[h]) * scale).astype(BF16)
                out.append(carry[h] + _dot(ds, kcat, NN))
                dkc = dkc + _dot(ds, qh[h], TN)
                dvv = dvv + _dot(p.astype(BF16), doh[h], TN)
            dkn_ref[ks, :] += dkc[:, :LANES]
            dkr_ref[ks, :] += dkc[:, LANES:]
            dv_ref[ks, :] += dvv
            return tuple(out)

        init = (jnp.zeros((tq, 2 * LANES), F32),) * 2
        res = block(m_idx, init, _chunk_ok(tq, tk))
        dq0, dq1 = _by_twos(m_idx, lambda kb, c: block(kb, c, None), res)
        dq_ref[...] = _rotate(jnp.where(m0, dq0, jnp.where(m1, dq1, 0.0)), c_ref[...], -s_ref[...])

    full = lambda col: pl.BlockSpec((T, LANES), col)
    blk = lambda col: pl.BlockSpec((tq, LANES), col)
    table = pl.BlockSpec((tq, 2 * LANES), lambda p, m: (m, 0))
    return pl.pallas_call(
        body, name=name, grid=(npair, T // tq),
        in_specs=[pl.BlockSpec((tq, 2 * LANES), lambda p, m: (m, p)), table, table,
                  full(lambda p, m: (0, p)), full(lambda p, m: (0, npair + p)), full(lambda p, m: (0, 0)),
                  blk(lambda p, m: (m, p)), blk(lambda p, m: (m, p)),
                  blk(lambda p, m: (m, do_col0 + p))],
        out_specs=[pl.BlockSpec((tq, 2 * LANES), lambda p, m: (m, p)),
                   full(lambda p, m: (0, p)), full(lambda p, m: (0, p)), full(lambda p, m: (0, 0))],
        out_shape=[jax.ShapeDtypeStruct((T, npair * 2 * LANES), F32),
                   jax.ShapeDtypeStruct((T, npair * LANES), F32),
                   jax.ShapeDtypeStruct((T, npair * LANES), F32),
                   jax.ShapeDtypeStruct((T, LANES), F32)],
        compiler_params=_params(("arbitrary", "arbitrary")),
    )(q, cos_q, sin_q, kv, kv, kr, o, lse, do)


def _split_dot(x, tri):
    hi = x.astype(BF16)
    lo = (x - hi.astype(F32)).astype(BF16)
    return _dot(hi, tri, NN) + _dot(lo, tri, NN)


def _sb_terms(qh, kk, before):
    z = _dot(qh, kk, NT)
    sp = jnp.maximum(z, 0.0) + jnp.log(1.0 + jnp.exp(-jnp.abs(z)))
    lk = -sp if before is None else jnp.where(before, -sp, 0.0)
    return z, sp, lk


def _sb_setup(q_ref, tq, tk, scale):
    qv = (q_ref[...].astype(F32) * scale).astype(BF16)
    lane = lax.broadcasted_iota(jnp.int32, (tq, LANES), 1)
    h0 = lane < 64
    qh = (jnp.where(h0, qv, 0).astype(BF16), jnp.where(h0, 0, qv).astype(BF16))
    row = lax.broadcasted_iota(jnp.int32, (tk, tk), 0)
    col = lax.broadcasted_iota(jnp.int32, (tk, tk), 1)
    return qh, h0, row, col


def _sb_fwd(qkv, col0, name="sb_fwd"):
    T = qkv.shape[0]
    tq = tk = _pick(T, ATT_T)
    npair = SB_HEADS // 2
    scale = SB_DIM ** -0.5

    def body(q_ref, k_ref, v_ref, o_ref, o32_ref):
        m_idx = pl.program_id(1)
        qh, h0, row, col = _sb_setup(q_ref, tq, tk, scale)
        later = (row > col).astype(BF16)

        def block(kb, carry, before):
            ks = pl.ds(pl.multiple_of(kb * tk, tk), tk)
            kk = k_ref[ks, :].astype(BF16)
            vv = v_ref[ks, :].astype(BF16)
            out = []
            for h in range(2):
                c, acc = carry[2 * h:2 * h + 2]
                z, sp, lk = _sb_terms(qh[h], kk, before)
                w = jnp.exp((z - sp) + _split_dot(lk, later) + c)
                if before is not None:
                    w = jnp.where(before, w, 0.0)
                out += [c + jnp.sum(lk, axis=-1, keepdims=True), acc + _dot(w.astype(BF16), vv, NN)]
            return tuple(out)

        init = (jnp.zeros((tq, 1), F32), jnp.zeros((tq, LANES), F32)) * 2
        res = block(m_idx, init, col < row)
        res = _by_twos(m_idx, lambda i, c: block(m_idx - 1 - i, c, None), res)
        o = jnp.where(h0, res[1], res[3])
        o_ref[...] = o.astype(o_ref.dtype)
        o32_ref[...] = o

    full = lambda col: pl.BlockSpec((T, LANES), col)
    blk = pl.BlockSpec((tq, LANES), lambda p, m: (m, p))
    return pl.pallas_call(
        body, name=name, grid=(npair, T // tq),
        in_specs=[pl.BlockSpec((tq, LANES), lambda p, m: (m, col0 + p)),
                  full(lambda p, m: (0, col0 + npair + p)), full(lambda p, m: (0, col0 + 2 * npair + p))],
        out_specs=[blk, blk],
        out_shape=[jax.ShapeDtypeStruct((T, npair * LANES), BF16), jax.ShapeDtypeStruct((T, npair * LANES), F32)],
        compiler_params=_params(("parallel", "arbitrary")),
    )(qkv, qkv, qkv)


def _sb_bwd(qkv, col0, o32, do, do_col0, dep, name="sb_bwd"):
    T = qkv.shape[0]
    tq = tk = _pick(T, ATT_T)
    npair = SB_HEADS // 2
    scale = SB_DIM ** -0.5

    def body(q_ref, k_ref, v_ref, o_ref, do_ref, dep_ref, dq_ref, dk_ref, dv_ref):
        m_idx = pl.program_id(1)

        @pl.when(m_idx == 0)
        def _():
            dk_ref[...] = jnp.zeros_like(dk_ref)
            dv_ref[...] = jnp.zeros_like(dv_ref)

        qh, h0, row, col = _sb_setup(q_ref, tq, tk, scale)
        dov = do_ref[...].astype(F32)
        doh = (jnp.where(h0, dov, 0.0).astype(BF16), jnp.where(h0, 0.0, dov).astype(BF16))
        ov = o_ref[...]
        etot = [jnp.sum(doh[h].astype(F32) * ov, axis=-1, keepdims=True) for h in range(2)]
        later = (row > col).astype(BF16)
        from_here = (row >= col).astype(BF16)

        def block(kb, carry, before):
            ks = pl.ds(pl.multiple_of(kb * tk, tk), tk)
            kk = k_ref[ks, :].astype(BF16)
            vv = v_ref[ks, :].astype(BF16)
            dkk = jnp.zeros((tk, LANES), F32)
            dvv = jnp.zeros((tk, LANES), F32)
            out = []
            for h in range(2):
                c, es, dqa = carry[3 * h:3 * h + 3]
                z, sp, lk = _sb_terms(qh[h], kk, before)
                w = jnp.exp((z - sp) + _split_dot(lk, later) + c)
                if before is not None:
                    w = jnp.where(before, w, 0.0)
                wb = w.astype(BF16)
                e = wb.astype(F32) * _dot(doh[h], vv, NT)
                prev = etot[h] - (_split_dot(e, from_here) + es)
                sig_neg = jnp.exp(-sp)
                dz = e * sig_neg - (1.0 - sig_neg) * prev
                if before is not None:
                    dz = jnp.where(before, dz, 0.0)
                dzb = dz.astype(BF16)
                dkk = dkk + _dot(dzb, qh[h], TN)
                dvv = dvv + _dot(wb, doh[h], TN)
                out += [c + jnp.sum(lk, axis=-1, keepdims=True), es + jnp.sum(e, axis=-1, keepdims=True),
                        dqa + _dot(dzb, kk, NN)]
            dk_ref[ks, :] += dkk
            dv_ref[ks, :] += dvv
            return tuple(out)

        init = (jnp.zeros((tq, 1), F32), jnp.zeros((tq, 1), F32), jnp.zeros((tq, LANES), F32)) * 2
        res = block(m_idx, init, col < row)
        res = _by_twos(m_idx, lambda i, c: block(m_idx - 1 - i, c, None), res)
        dq_ref[...] = jnp.where(h0, res[2], res[5]) * scale

    full = lambda col: pl.BlockSpec((T, LANES), col)
    blk = lambda col: pl.BlockSpec((tq, LANES), col)
    return pl.pallas_call(
        body, name=name, grid=(npair, T // tq),
        in_specs=[blk(lambda p, m: (m, col0 + p)),
                  full(lambda p, m: (0, col0 + npair + p)), full(lambda p, m: (0, col0 + 2 * npair + p)),
                  blk(lambda p, m: (m, p)), blk(lambda p, m: (m, do_col0 + p)),
                  pl.BlockSpec((8, LANES), lambda p, m: (0, 0))],
        out_specs=[blk(lambda p, m: (m, p)), full(lambda p, m: (0, p)), full(lambda p, m: (0, p))],
        out_shape=[jax.ShapeDtypeStruct((T, npair * LANES), F32)] * 3,
        compiler_params=_params(("arbitrary", "arbitrary")),
    )(qkv, qkv, qkv, o32, do, dep)


def _band_in_window():
    cq = lax.broadcasted_iota(jnp.int32, (BAND_TQ, BAND_W), 0) >> 6
    ckp = lax.broadcasted_iota(jnp.int32, (BAND_TQ, BAND_W), 1) >> 6
    return (ckp >= cq) & (ckp <= cq + LEFT_CHUNKS)


def _band_real(m_idx):
    j = lax.broadcasted_iota(jnp.int32, (BAND_TQ, BAND_W), 1)
    return j >= PAD_KEYS - m_idx * BAND_TQ


def _band_probs(qh, kw, bias, real, scale):
    s = jnp.where(real, _dot(qh, kw, NT) * scale + bias, NEG)
    e = jnp.exp(s - jnp.max(s, axis=-1, keepdims=True))
    return e * (1.0 / jnp.sum(e, axis=-1, keepdims=True))


BAND_SUB = 2


def _band_fwd(qkv, k_pad, v_pad, bias_w, name="band_fwd"):
    T = qkv.shape[0]
    npair = C_HEADS // 2
    scale = C_DIM ** -0.5
    rows = BAND_SUB * BAND_TQ

    def body(q_ref, k_ref, v_ref, b_ref, o_ref):
        lane = lax.broadcasted_iota(jnp.int32, (BAND_TQ, LANES), 1)
        h0 = lane < 64
        for sub in range(BAND_SUB):
            m_idx = pl.program_id(1) * BAND_SUB + sub
            win = pl.ds(pl.multiple_of(m_idx * BAND_TQ, BAND_TQ), BAND_W)
            kw, vw = k_ref[win, :], v_ref[win, :]
            qv = q_ref[sub * BAND_TQ:(sub + 1) * BAND_TQ, :]
            qh = (jnp.where(h0, qv, 0).astype(BF16), jnp.where(h0, 0, qv).astype(BF16))
            real = _band_real(m_idx)
            o = [_dot(_band_probs(qh[h], kw, b_ref[h], real, scale).astype(BF16), vw, NN) for h in range(2)]
            o_ref[sub * BAND_TQ:(sub + 1) * BAND_TQ, :] = jnp.where(h0, o[0], o[1]).astype(o_ref.dtype)

    Tp = T + PAD_KEYS
    return pl.pallas_call(
        body, name=name, grid=(npair, T // rows),
        in_specs=[pl.BlockSpec((rows, LANES), lambda p, m: (m, p)),
                  pl.BlockSpec((Tp, LANES), lambda p, m: (0, p)),
                  pl.BlockSpec((Tp, LANES), lambda p, m: (0, p)),
                  pl.BlockSpec((2, BAND_TQ, BAND_W), lambda p, m: (p, 0, 0))],
        out_specs=pl.BlockSpec((rows, LANES), lambda p, m: (m, p)),
        out_shape=jax.ShapeDtypeStruct((T, npair * LANES), BF16),
        compiler_params=_params(("parallel", "arbitrary")),
    )(qkv, k_pad, v_pad, bias_w)


def _band_bwd(qkv, k_pad, v_pad, bias_w, do, name="band_bwd"):
    T = qkv.shape[0]
    npair = C_HEADS // 2
    scale = C_DIM ** -0.5

    rows = BAND_SUB * BAND_TQ

    def body(q_ref, k_ref, v_ref, b_ref, do_ref, dq_ref, dk_ref, dv_ref, db_ref):
        @pl.when(pl.program_id(1) == 0)
        def _():
            dk_ref[...] = jnp.zeros_like(dk_ref)
            dv_ref[...] = jnp.zeros_like(dv_ref)
            db_ref[...] = jnp.zeros_like(db_ref)

        lane = lax.broadcasted_iota(jnp.int32, (BAND_TQ, LANES), 1)
        h0 = lane < 64
        dbs = [jnp.zeros((BAND_TQ, BAND_W), F32)] * 2
        for sub in range(BAND_SUB):
            m_idx = pl.program_id(1) * BAND_SUB + sub
            win = pl.ds(pl.multiple_of(m_idx * BAND_TQ, BAND_TQ), BAND_W)
            kw, vw = k_ref[win, :], v_ref[win, :]
            qv = q_ref[sub * BAND_TQ:(sub + 1) * BAND_TQ, :]
            dov = do_ref[sub * BAND_TQ:(sub + 1) * BAND_TQ, :].astype(F32)
            qh = (jnp.where(h0, qv, 0).astype(BF16), jnp.where(h0, 0, qv).astype(BF16))
            doh = (jnp.where(h0, dov, 0.0).astype(BF16), jnp.where(h0, 0.0, dov).astype(BF16))
            real = _band_real(m_idx)
            dq = []
            dkw = jnp.zeros((BAND_W, LANES), F32)
            dvw = jnp.zeros((BAND_W, LANES), F32)
            for h in range(2):
                p = _band_probs(qh[h], kw, b_ref[h], real, scale)
                dp = _dot(doh[h], vw, NT)
                dsb = p * (dp - jnp.sum(p * dp, axis=-1, keepdims=True))
                dbs[h] = dbs[h] + dsb
                dsq = (dsb * scale).astype(BF16)
                dq.append(_dot(dsq, kw, NN))
                dkw = dkw + _dot(dsq, qh[h], TN)
                dvw = dvw + _dot(p.astype(BF16), doh[h], TN)
            dq_ref[sub * BAND_TQ:(sub + 1) * BAND_TQ, :] = jnp.where(h0, dq[0], dq[1])
            dk_ref[win, :] += dkw
            dv_ref[win, :] += dvw
        for h in range(2):
            db_ref[h] += dbs[h]

    Tp = T + PAD_KEYS
    blk = lambda col: pl.BlockSpec((rows, LANES), col)
    full = pl.BlockSpec((Tp, LANES), lambda p, m: (0, p))
    bias = pl.BlockSpec((2, BAND_TQ, BAND_W), lambda p, m: (p, 0, 0))
    return pl.pallas_call(
        body, name=name, grid=(npair, T // rows),
        in_specs=[blk(lambda p, m: (m, p)), full, full, bias, blk(lambda p, m: (m, p))],
        out_specs=[blk(lambda p, m: (m, p)), full, full, bias],
        out_shape=[jax.ShapeDtypeStruct((T, npair * LANES), F32),
                   jax.ShapeDtypeStruct((Tp, npair * LANES), F32),
                   jax.ShapeDtypeStruct((Tp, npair * LANES), F32),
                   jax.ShapeDtypeStruct((C_HEADS, BAND_TQ, BAND_W), F32)],
        compiler_params=_params(("arbitrary", "arbitrary")),
    )(qkv, k_pad, v_pad, bias_w, do)


def _skew_bits(x, left):
    w = x.shape[1]
    row = lax.broadcasted_iota(jnp.int32, x.shape, 0)
    for b in range(BAND_TQ.bit_length() - 1):
        amt = (w - (1 << b)) if left else (1 << b)
        x = jnp.where((row >> b) & 1 == 1, pltpu.roll(x, amt, 1), x)
    return x


def _toeplitz(diag, name="toeplitz"):
    H = diag.shape[0]

    def body(d_ref, o_ref):
        x = jnp.broadcast_to(d_ref[0], (BAND_TQ, TOEP_W))
        o_ref[0] = jnp.where(_band_in_window(), _skew_bits(x, left=False)[:, BAND_TQ:], NEG)

    return pl.pallas_call(
        body, name=name, grid=(H,),
        in_specs=[pl.BlockSpec((1, 1, TOEP_W), lambda h: (h, 0, 0))],
        out_specs=pl.BlockSpec((1, BAND_TQ, BAND_W), lambda h: (h, 0, 0)),
        out_shape=jax.ShapeDtypeStruct((H, BAND_TQ, BAND_W), F32),
        compiler_params=_params(("parallel",)),
    )(diag.reshape(H, 1, TOEP_W))


def _toeplitz_bwd(dbias, name="toeplitz_bwd"):
    H = dbias.shape[0]

    def body(d_ref, o_ref):
        x = jnp.concatenate([jnp.zeros((BAND_TQ, BAND_TQ), F32), d_ref[0]], axis=1)
        o_ref[0] = jnp.sum(_skew_bits(x, left=True), axis=0, keepdims=True)

    return pl.pallas_call(
        body, name=name, grid=(H,),
        in_specs=[pl.BlockSpec((1, BAND_TQ, BAND_W), lambda h: (h, 0, 0))],
        out_specs=pl.BlockSpec((1, 1, TOEP_W), lambda h: (h, 0, 0)),
        out_shape=jax.ShapeDtypeStruct((H, 1, TOEP_W), F32),
        compiler_params=_params(("parallel",)),
    )(dbias).reshape(H, TOEP_W)


_HBM = pl.BlockSpec(memory_space=pltpu.HBM)
_SEM = pl.BlockSpec(memory_space=pltpu.SEMAPHORE)
_EFFECT = pltpu.SideEffectType.DATAFLOW_SIDE_EFFECTING


def _peers():
    x, y, c = lax.axis_index("x"), lax.axis_index("y"), lax.axis_index("c")
    out = []
    for k in range(1, N_DEV):
        peer = (1 - x if (k >> 2) & 1 else x, 1 - y if (k >> 1) & 1 else y, 1 - c if k & 1 else c)
        out.append((peer, 4 * peer[0] + 2 * peer[1] + peer[2]))
    return 4 * x + 2 * y + c, out


def _split_copies(ins, lands, scatter, send_sem, recv_sem, arriving):
    me, peers = _peers()
    out = []
    for a in range(len(ins)):
        for peer, idx in peers:
            out.append(pltpu.make_async_remote_copy(
                src_ref=ins[a].at[idx] if scatter[a] else ins[a],
                dst_ref=lands[a].at[idx if arriving else me], send_sem=send_sem, recv_sem=recv_sem,
                device_id=peer, device_id_type=pl.DeviceIdType.MESH))
    return out


def _landing_zones(arrays, scatter):
    return [lax.empty((N_DEV,) + (a.shape[1:] if s else a.shape), a.dtype) for a, s in zip(arrays, scatter)]


def _place_own(arrays, scatter, name):
    n = len(arrays)
    lands = _landing_zones(arrays, scatter)
    me = (4 * lax.axis_index("x") + 2 * lax.axis_index("y") + lax.axis_index("c")).astype(jnp.int32).reshape(1)

    def body(me_ref, *refs):
        for a in range(n):
            refs[2 * n + a][...] = refs[a][...].reshape(refs[2 * n + a].shape)

    def row_spec(shape):
        zeros = (0,) * (len(shape) - 1)
        return pl.BlockSpec((1,) + tuple(shape[1:]), lambda i, me_ref: (me_ref[0],) + zeros)

    in_specs = [row_spec(a.shape) if s else pl.BlockSpec(a.shape, lambda i, me_ref, nd=a.ndim: (0,) * nd)
                for a, s in zip(arrays, scatter)]
    return pl.pallas_call(
        body, name=name,
        out_shape=[jax.ShapeDtypeStruct(l.shape, l.dtype) for l in lands],
        grid_spec=pltpu.PrefetchScalarGridSpec(
            num_scalar_prefetch=1, grid=(1,),
            in_specs=in_specs + [pl.BlockSpec(memory_space=pl.ANY)] * n,
            out_specs=[row_spec(l.shape) for l in lands]),
        input_output_aliases={1 + n + i: i for i in range(n)},
        compiler_params=_params(("arbitrary",)),
    )(me, *arrays, *lands)


def _exchange_start(arrays, scatter, after, name):
    n = len(arrays)
    lands = list(_place_own(arrays, scatter, name=name.replace("_start_", "_own_")))

    def body(*refs):
        ins, lnd = refs[:n], refs[n:2 * n]
        send_sem, recv_sem = refs[2 * n + 1:2 * n + 3]
        token = refs[-1]
        for cp in _split_copies(ins, lnd, scatter, send_sem, recv_sem, arriving=False):
            cp.start()
        token[...] = jnp.zeros_like(token)

    hbm = lambda a: pltpu.HBM(a.shape, a.dtype)
    out = pl.pallas_call(
        body, name=name,
        out_shape=(pltpu.SemaphoreType.DMA(()), pltpu.SemaphoreType.DMA(()),
                   *[hbm(a) for a in arrays], *[hbm(a) for a in lands],
                   jax.ShapeDtypeStruct((8, LANES), F32)),
        in_specs=[_HBM] * (2 * n) + [pl.BlockSpec(memory_space=pl.ANY)],
        out_specs=(_SEM, _SEM, *([_HBM] * (2 * n)), pl.BlockSpec(memory_space=pltpu.VMEM)),
        input_output_aliases={i: 2 + i for i in range(2 * n)},
        compiler_params=pltpu.CompilerParams(has_side_effects=_EFFECT),
    )(*[pltpu.with_memory_space_constraint(a, pltpu.HBM) for a in list(arrays) + lands], after)
    return (out[0], out[1], list(out[2:2 + n]), list(out[2 + n:2 + 2 * n]), tuple(scatter)), out[-1]


def _exchange_wait(handle, after, name):
    send_sem, recv_sem, ins, lands, scatter = handle
    n = len(ins)
    after = after if isinstance(after, tuple) else (after,)

    def body(*refs):
        i_ref, l_ref = refs[:n], refs[n:2 * n]
        s_sem, r_sem = refs[2 * n:2 * n + 2]
        for cp in _split_copies(i_ref, l_ref, scatter, s_sem, r_sem, arriving=False):
            cp.wait_send()
        for cp in _split_copies(i_ref, l_ref, scatter, s_sem, r_sem, arriving=True):
            cp.wait_recv()

    hbm = lambda a: pltpu.HBM(a.shape, a.dtype)
    out = pl.pallas_call(
        body, name=name,
        out_shape=tuple(hbm(a) for a in ins + lands),
        in_specs=[_HBM] * (2 * n) + [_SEM, _SEM] + [pl.BlockSpec(memory_space=pl.ANY)] * len(after),
        out_specs=tuple([_HBM] * (2 * n)),
        input_output_aliases={i: i for i in range(2 * n)},
        compiler_params=pltpu.CompilerParams(has_side_effects=_EFFECT),
    )(*ins, *lands, send_sem, recv_sem, *after)
    return list(out[n:])


def _adamw(w, parts, m, v, name="adamw"):
    R, C = w.shape
    tr = max([t for t in range(16, 513, 16) if R % t == 0], default=R)
    c1 = 1.0 - ADAM_B1 ** ADAM_STEP
    c2 = 1.0 - ADAM_B2 ** ADAM_STEP

    def body(w_ref, p_ref, m_ref, v_ref, g_ref, d_ref, nm_ref, nv_ref):
        g = p_ref[0].astype(F32)
        for i in range(1, N_DEV):
            g = g + p_ref[i].astype(F32)
        nm = ADAM_B1 * m_ref[...] + (1.0 - ADAM_B1) * g
        nv = ADAM_B2 * v_ref[...] + (1.0 - ADAM_B2) * (g * g)
        g_ref[...] = g
        nm_ref[...] = nm
        nv_ref[...] = nv
        d_ref[...] = -ADAM_LR * ((nm / c1) / (jnp.sqrt(nv / c2) + ADAM_EPS) + ADAM_WD * w_ref[...])

    blk = pl.BlockSpec((tr, C), lambda i: (i, 0))
    return pl.pallas_call(
        body, name=name, grid=(R // tr,),
        in_specs=[blk, pl.BlockSpec((N_DEV, tr, C), lambda i: (0, i, 0)), blk, blk],
        out_specs=[blk] * 4,
        out_shape=[jax.ShapeDtypeStruct((R, C), F32)] * 4,
        compiler_params=_params(("parallel",)),
    )(w, parts, m, v)


_O1 = Q_LORA
_O2 = _O1 + KV_LORA
_O3 = _O2 + MLA_ROPE
_NB = SB_HEADS * SB_DIM
IN_W = _O2 + LANES + 3 * _NB
COL_KR = _O2 // LANES
COL_SB = COL_KR + 1


def _w_in_local(w):
    kr = w[_O2:_O3]
    pad = jnp.zeros((LANES - 2 * MLA_ROPE, w.shape[1]), w.dtype)
    return jnp.concatenate([w[:_O2], kr, kr, pad, w[_O3:]], axis=0)


def _w_in_grad(g):
    kr = (g[_O2:_O2 + MLA_ROPE].astype(F32) + g[_O2 + MLA_ROPE:_O2 + 2 * MLA_ROPE].astype(F32)).astype(g.dtype)
    return jnp.concatenate([g[:_O2], kr, g[_O2 + LANES:]], axis=0)


def _w_uq_local(w):
    w3 = w.reshape(MLA_HEADS // 2, 2, MLA_NOPE + MLA_ROPE, w.shape[1])
    nope = w3[:, :, :MLA_NOPE].reshape(MLA_HEADS // 2, 2 * MLA_NOPE, w.shape[1])
    rope = w3[:, :, MLA_NOPE:].reshape(MLA_HEADS // 2, 2 * MLA_ROPE, w.shape[1])
    pad = jnp.zeros((MLA_HEADS // 2, LANES - 2 * MLA_ROPE, w.shape[1]), w.dtype)
    return jnp.concatenate([nope, rope, pad], axis=1).reshape(-1, w.shape[1])


def _w_uq_grad(g):
    g3 = g.reshape(MLA_HEADS // 2, 2 * LANES, g.shape[1])
    nope = g3[:, :2 * MLA_NOPE].reshape(MLA_HEADS // 2, 2, MLA_NOPE, g.shape[1])
    rope = g3[:, LANES:LANES + 2 * MLA_ROPE].reshape(MLA_HEADS // 2, 2, MLA_ROPE, g.shape[1])
    return jnp.concatenate([nope, rope], axis=2).reshape(-1, g.shape[1])


def _w_ukv_local(w):
    w3 = w.reshape(MLA_HEADS, MLA_NOPE + MLA_V, w.shape[1])
    return jnp.concatenate([w3[:, :MLA_NOPE].reshape(-1, w.shape[1]),
                            w3[:, MLA_NOPE:].reshape(-1, w.shape[1])], axis=0)


def _w_ukv_grad(g):
    half = MLA_HEADS * MLA_NOPE
    kn = g[:half].reshape(MLA_HEADS, MLA_NOPE, g.shape[1])
    vv = g[half:].reshape(MLA_HEADS, MLA_V, g.shape[1])
    return jnp.concatenate([kn, vv], axis=1).reshape(-1, g.shape[1])


def _rope_tables(T):
    pos = jnp.arange(T, dtype=F32)
    inv_freq = ROPE_THETA ** (-jnp.arange(0, MLA_ROPE, 2, dtype=F32) / MLA_ROPE)
    ang = pos[:, None] * inv_freq[None, :]
    cos, sin = jnp.cos(ang), jnp.sin(ang)
    ones = jnp.ones((T, LANES - 2 * MLA_ROPE), F32)
    cos_k = jnp.concatenate([cos, cos, cos, cos, ones], axis=1)
    sin_k = jnp.concatenate([-sin, sin, -sin, sin, 0.0 * ones], axis=1)
    cos_q = jnp.concatenate([jnp.ones((T, LANES), F32), cos_k], axis=1)
    sin_q = jnp.concatenate([jnp.zeros((T, LANES), F32), sin_k], axis=1)
    return cos_q, sin_q, cos_k, sin_k


def _bias_diag_index():
    ell = np.arange(TOEP_W)
    return np.clip(BAND_W - ell, -REL_CLIP, REL_CLIP) + REL_CLIP


def _local_step(x, target, small, get_weights, put_grads):
    T = x.shape[0]
    cos_q, sin_q, cos_k, sin_k = _rope_tables(T)
    G = {}
    W = dict(small)

    u0 = _rms_fwd(x, W["g_mix"][0:1], name="rms_mix0")
    bias_w = _toeplitz(W["od_rel_bias"][:, _bias_diag_index()])
    W.update(get_weights("in0", (u0, bias_w)))
    proj = _mm(u0, W["w_in_t"], dims="nt", name="proj_in")
    W.update(get_weights("mix0", proj))
    c_q, c_kv = proj[:, :_O1], proj[:, _O1:_O2]
    nq = _rms_fwd(c_q, W["g_cq"], name="rms_cq")
    nkv = _rms_fwd(c_kv, W["g_ckv"], name="rms_ckv")
    qa_raw = _mm(nq, W["w_uq_t"], dims="nt", name="proj_uq")
    kv = _mm(nkv, W["w_ukv_t"], dims="nt", out_dtype=BF16, name="proj_ukv")
    kr = _rope(proj, cos_k, sin_k, COL_KR, 1, BF16, name="rope_k")
    o_a, lse = _mla_fwd(qa_raw, cos_q, sin_q, kv, kr)
    o_b, o_b32 = _sb_fwd(proj, COL_SB)
    o_ab = jnp.concatenate([o_a, o_b], axis=1)
    h1 = _mm(o_ab, W["ev_w_out"], res=x, name="out_ev")

    def ffn_fwd(h, layer):
        W.update(get_weights(f"ffn{layer}", h))
        return _ffn_fwd(h, W["g_ffn"][layer:layer + 1], W[f"w_gate_t{layer}"], W[f"w_up_t{layer}"],
                        W[f"w_down{layer}"], name=f"ffn_fwd{layer}")

    h2, u1, a0, b0 = ffn_fwd(h1, 0)

    W.update(get_weights("mix1", h2))
    u2 = _rms_fwd(h2, W["g_mix"][1:2], name="rms_mix1")
    qkv = _mm(u2, W["od_w_qkv_t"], dims="nt", out_dtype=BF16, name="proj_qkv")
    nc = C_HEADS * C_DIM
    pad = ((PAD_KEYS, 0), (0, 0))
    k_pad, v_pad = jnp.pad(qkv[:, nc:2 * nc], pad), jnp.pad(qkv[:, 2 * nc:], pad)
    o_c = _band_fwd(qkv, k_pad, v_pad, bias_w)
    h3 = _mm(o_c, W["od_w_out"], res=h2, name="out_od")
    h4, u3, a1, b1 = ffn_fwd(h3, 1)

    loss, dh, dhb, G["g_final"] = _loss_head(h4, W["g_final"], target)

    def ffn_bwd(dh, dhb, h, u, a, b, layer):
        du, g_gate, g_up, g_down = _ffn_bwd(dhb, u, a, b, W[f"w_gate_t{layer}"], W[f"w_up_t{layer}"],
                                            W[f"w_down{layer}"], name=f"ffn_bwd{layer}")
        tok = put_grads(f"ffn{layer}", {"w_gate_t": g_gate, "w_up_t": g_up, "w_down": g_down})
        return _rms_bwd(h, W["g_ffn"][layer:layer + 1] + tok[:1, :1], du, dres=dh, name=f"rms_ffn_bwd{layer}")

    dh3, dh3b, g_gffn1 = ffn_bwd(dh, dhb, h3, u3, a1, b1, 1)

    do_c = _mm(dh3b, W["od_w_out"], dims="nt", name="out_od_dx")
    g_od_out = _mm(o_c, dh3b, dims="tn", out_dtype=BF16, name="out_od_dw")
    dq_c, dk_p, dv_p, dbias_w = _band_bwd(qkv, k_pad, v_pad, bias_w, do_c)
    dqkv = jnp.concatenate([dq_c, dk_p[PAD_KEYS:], dv_p[PAD_KEYS:]], axis=1)
    du2 = _mm(dqkv, W["od_w_qkv_t"], name="proj_qkv_dx")
    tok = put_grads("mix1", {"od_w_qkv_t": _mm(dqkv, u2, dims="tn", out_dtype=BF16, name="proj_qkv_dw"),
                             "od_w_out": g_od_out})
    ddiag = _toeplitz_bwd(dbias_w)
    n_far = BAND_W - REL_CLIP + 1
    G["od_rel_bias"] = jnp.concatenate(
        [jnp.zeros((C_HEADS, REL_CLIP - BAND_TQ + 1), F32), ddiag[:, n_far:][:, ::-1],
         jnp.sum(ddiag[:, :n_far], axis=1, keepdims=True)], axis=1)
    dh2, dh2b, g_gmix1 = _rms_bwd(h2, W["g_mix"][1:2] + tok[:1, :1], du2, dres=dh3, name="rms_mix_bwd1")

    dh1, dh1b, g_gffn0 = ffn_bwd(dh2, dh2b, h1, u1, a0, b0, 0)
    G["g_ffn"] = jnp.concatenate([g_gffn0, g_gffn1], axis=0)

    do_ab = _mm(dh1b, W["ev_w_out"], dims="nt", name="out_ev_dx")
    g0 = {"ev_w_out": _mm(o_ab, dh1b, dims="tn", out_dtype=BF16, name="out_ev_dw")}
    dqa_raw, dkn, dva, dkr = _mla_bwd(qa_raw, cos_q, sin_q, kv, kr, o_a, lse, do_ab, 0)
    g0["w_uq_t"] = _mm(dqa_raw, nq, dims="tn", name="proj_uq_dw")
    dnq = _mm(dqa_raw, W["w_uq_t"], name="proj_uq_dx")
    dc_q, _, G["g_cq"] = _rms_bwd(c_q, W["g_cq"], dnq, name="rms_cq_bwd")
    dkv = jnp.concatenate([dkn, dva], axis=1)
    g0["w_ukv_t"] = _mm(dkv, nkv, dims="tn", name="proj_ukv_dw")
    dnkv = _mm(dkv, W["w_ukv_t"], name="proj_ukv_dx")
    dc_kv, _, G["g_ckv"] = _rms_bwd(c_kv, W["g_ckv"], dnkv, name="rms_ckv_bwd")
    tok = put_grads("mix0", g0)
    dqb, dkb, dvb = _sb_bwd(proj, COL_SB, o_b32, do_ab, MLA_HEADS // 2, tok)
    dkr_raw = _rope(dkr, cos_k, -sin_k, 0, 1, F32, name="rope_k_bwd")
    dproj = jnp.concatenate([dc_q, dc_kv, dkr_raw, dqb, dkb, dvb], axis=1)
    du0 = _mm(dproj, W["w_in_t"], name="proj_in_dx")
    tok = put_grads("in0", {"w_in_t": _mm(dproj, u0, dims="tn", name="proj_in_dw")})
    dx, _, g_gmix0 = _rms_bwd(x, W["g_mix"][0:1] + tok[:1, :1], du0, dres=dh1, name="rms_mix_bwd0")
    G["g_mix"] = jnp.concatenate([g_gmix0, g_gmix1], axis=0)
    return loss[0, 0], dx, G


_BIG = ["ev_w_in", "ev_w_uq", "ev_w_ukv", "ev_w_out", "od_w_qkv", "od_w_out", "w_gate", "w_up", "w_down"]
_COL_SHARDED = {"ev_w_in", "ev_w_uq", "ev_w_ukv", "od_w_qkv", "w_gate", "w_up"}
_SMALL = ["ev_g_cq", "ev_g_ckv", "od_rel_bias", "g_mix", "g_ffn", "g_final"]
_GROUPS = {
    "in0": ["ev_w_in"],
    "mix0": ["ev_w_uq", "ev_w_ukv", "ev_w_out"],
    "ffn0": ["w_gate0", "w_up0", "w_down0"],
    "mix1": ["od_w_qkv", "od_w_out"],
    "ffn1": ["w_gate1", "w_up1", "w_down1"],
}
_GROUP_SRC = {n + str(l): (n, l) for n in ("w_gate", "w_up", "w_down") for l in (0, 1)}
_BATCHES = {"in0": ["in0"], "layer0": ["mix0", "ffn0"], "layer1": ["mix1", "ffn1"]}
_BATCH_OF = {grp: batch for batch, grps in _BATCHES.items() for grp in grps}
_SMALL_ROWS = 8
_SMALL_COLS = 1792


def _pack_small(vals):
    flat = jnp.concatenate([v.reshape(-1).astype(F32) for v in vals])
    flat = jnp.pad(flat, (0, _SMALL_ROWS * _SMALL_COLS - flat.shape[0]))
    return flat.reshape(_SMALL_ROWS, _SMALL_COLS)


def _unpack_small(packed, like):
    flat = packed.reshape(-1)
    out, off = [], 0
    for v in like:
        out.append(flat[off:off + v.size].reshape(v.shape))
        off += v.size
    return out


def kernel(x, ev_w_in, ev_g_cq, ev_w_uq, ev_g_ckv, ev_w_ukv, ev_w_out, od_w_qkv, od_rel_bias, od_w_out, g_mix, g_ffn, w_gate, w_up, w_down, g_final, loss_target, m_ev_w_in, m_ev_g_cq, m_ev_w_uq, m_ev_g_ckv, m_ev_w_ukv, m_ev_w_out, m_od_w_qkv, m_od_rel_bias, m_od_w_out, m_g_mix, m_g_ffn, m_w_gate, m_w_up, m_w_down, m_g_final, v_ev_w_in, v_ev_g_cq, v_ev_w_uq, v_ev_g_ckv, v_ev_w_ukv, v_ev_w_out, v_od_w_qkv, v_od_rel_bias, v_od_w_out, v_g_mix, v_g_ffn, v_w_gate, v_w_up, v_w_down, v_g_final):
    args = dict(locals())
    w = {n: args[n] for n in _BIG + _SMALL}
    mom = {n: args["m_" + n] for n in _BIG + _SMALL}
    var = {n: args["v_" + n] for n in _BIG + _SMALL}

    own = {}
    for grp, names in _GROUPS.items():
        for n in names:
            base, layer = _GROUP_SRC.get(n, (n, 0))
            shard = w[base][layer:layer + 1]
            own[n] = (jnp.swapaxes(shard, 1, 2) if base in _COL_SHARDED else shard).astype(BF16)
    gather, token = {}, x[0, :8, :LANES]
    for grp, names in _GROUPS.items():
        gather[grp], token = _exchange_start([own[n] for n in names], [False] * len(names), token,
                                             name="gather_start_" + grp)

    def get_weights(grp, after):
        names = _GROUPS[grp]
        lands = _exchange_wait(gather[grp], token if after is None else after, name="gather_wait_" + grp)
        full = {n: l.reshape(-1, l.shape[-1]) for n, l in zip(names, lands)}
        if grp == "in0":
            return {"w_in_t": _w_in_local(full["ev_w_in"])}
        if grp == "mix0":
            return {"w_uq_t": _w_uq_local(full["ev_w_uq"]), "w_ukv_t": _w_ukv_local(full["ev_w_ukv"]),
                    "ev_w_out": full["ev_w_out"]}
        if grp == "mix1":
            return {"od_w_qkv_t": full["od_w_qkv"], "od_w_out": full["od_w_out"]}
        layer = grp[-1]
        return {"w_gate_t" + layer: full["w_gate" + layer], "w_up_t" + layer: full["w_up" + layer],
                "w_down" + layer: full["w_down" + layer]}

    scatter, pending = {}, {}

    def put_grads(grp, g):
        if grp == "in0":
            g = {"ev_w_in": _w_in_grad(g["w_in_t"])}
        elif grp == "mix0":
            g = {"ev_w_uq": _w_uq_grad(g["w_uq_t"]), "ev_w_ukv": _w_ukv_grad(g["w_ukv_t"]),
                 "ev_w_out": g["ev_w_out"]}
        elif grp == "mix1":
            g = {"od_w_qkv": g["od_w_qkv_t"], "od_w_out": g["od_w_out"]}
        else:
            layer = grp[-1]
            g = {"w_gate" + layer: g["w_gate_t"], "w_up" + layer: g["w_up_t"], "w_down" + layer: g["w_down"]}
        pending.update({n: v.reshape(N_DEV, 1, v.shape[0] // N_DEV, v.shape[1]).astype(BF16) for n, v in g.items()})
        batch = _BATCH_OF[grp]
        names = [n for gr in _BATCHES[batch] for n in _GROUPS[gr]]
        if not all(n in pending for n in names):
            return jnp.zeros((8, LANES), F32)
        send = [pending[n] for n in names]
        scatter[batch], tok = _exchange_start(send, [True] * len(names), send[0], name="scatter_start_" + batch)
        return tok

    small = {"g_cq": ev_g_cq, "g_ckv": ev_g_ckv, "od_rel_bias": od_rel_bias[0],
             "g_mix": g_mix + token[0, 0], "g_ffn": g_ffn, "g_final": g_final.reshape(1, -1)}
    loss_part, dx, G = _local_step(x[0], loss_target[0], small, get_weights, put_grads)
    loss = lax.psum(loss_part, ("x", "y", "c"))
    g_small = _pack_small([G["g_cq"], G["g_ckv"], G["od_rel_bias"], G["g_mix"], G["g_ffn"], G["g_final"]])
    small_handle, _ = _exchange_start([g_small], [False], dx, name="gather_start_small")

    grads, deltas, new_m, new_v = {}, {}, {}, {}
    parts, after = {}, dx

    def wait_parts(batch, after):
        lands = _exchange_wait(scatter[batch], after, name="scatter_wait_" + batch)
        parts.update(zip([n for grp in _BATCHES[batch] for n in _GROUPS[grp]], lands))
        return lands[0]

    def adamw(n):
        col = n in _COL_SHARDED
        rows = lambda a: (jnp.swapaxes(a, 1, 2) if col else a).reshape(-1, a.shape[1 if col else 2])
        res = _adamw(rows(w[n]), parts[n].reshape(N_DEV, -1, parts[n].shape[-1]), rows(mom[n]), rows(var[n]),
                     name="adamw_" + n)
        L, a1, a2 = w[n].shape
        back = lambda r: jnp.swapaxes(r.reshape(L, a2, a1), 1, 2) if col else r.reshape(L, a1, a2)
        grads[n], deltas[n], new_m[n], new_v[n] = [back(r) for r in res]
        return res[0]

    for batch in ("layer1", "layer0"):
        after = wait_parts(batch, after)
    for n in ("w_gate", "w_up", "w_down"):
        parts[n] = jnp.concatenate([parts[n + "0"], parts[n + "1"]], axis=1)
    for n in _BIG[1:]:
        after = adamw(n)
    after = wait_parts("in0", after)
    after = adamw("ev_w_in")
    small_w = [w[n] for n in _SMALL]
    small_parts = _exchange_wait(small_handle, after, name="gather_wait_small")[0]
    res = _adamw(_pack_small(small_w), small_parts, _pack_small([mom[n] for n in _SMALL]),
                 _pack_small([var[n] for n in _SMALL]), name="adamw_small")
    for d, packed in zip((grads, deltas, new_m, new_v), res):
        for n, val in zip(_SMALL, _unpack_small(packed, small_w)):
            d[n] = val

    order = ["ev_w_in", "ev_g_cq", "ev_w_uq", "ev_g_ckv", "ev_w_ukv", "ev_w_out", "od_w_qkv", "od_rel_bias",
             "od_w_out", "g_mix", "g_ffn", "w_gate", "w_up", "w_down", "g_final"]
    out = [loss, dx[None]]
    for d in (grads, deltas, new_m, new_v):
        out += [d[n] for n in order]
    return tuple(out)
```

```python
import functools

import numpy as np
import jax
import jax.numpy as jnp
from jax import lax
from jax.experimental import pallas as pl
from jax.experimental.pallas import tpu as pltpu

F32 = jnp.float32
BF16 = jnp.bfloat16

D_MODEL = 1024
CHUNK = 64
MLA_HEADS = 8
MLA_NOPE = 64
MLA_ROPE = 32
MLA_V = 64
Q_LORA = 384
KV_LORA = 256
ROPE_THETA = 10000.0
SB_HEADS = 8
SB_DIM = 64
C_HEADS = 16
C_DIM = 64
LEFT_CHUNKS = 8
REL_CLIP = 256
D_FF = 2816
RMS_EPS = 1e-6
ADAM_LR = 0.001
ADAM_B1 = 0.9
ADAM_B2 = 0.999
ADAM_EPS = 1e-08
ADAM_WD = 0.01
ADAM_STEP = 10

N_DEV = 8
LANES = 128
VMEM_LIMIT = 56 * 1024 * 1024
NEG = -1e30
PAD_KEYS = LEFT_CHUNKS * CHUNK
BAND_TQ = 128
BAND_W = BAND_TQ + PAD_KEYS
TOEP_W = BAND_W + BAND_TQ

NN = (((1,), (0,)), ((), ()))
NT = (((1,), (1,)), ((), ()))
TN = (((0,), (0,)), ((), ()))


def _dot(a, b, dn):
    return lax.dot_general(a, b, dn, preferred_element_type=F32)


def _pick(dim, pref):
    if dim <= pref:
        return dim
    best = None
    for t in range(LANES, pref + 1, LANES):
        if dim % t == 0:
            best = t
    assert best is not None, (dim, pref)
    return best


def _params(sem):
    return pltpu.CompilerParams(dimension_semantics=sem, vmem_limit_bytes=VMEM_LIMIT)


def _mm(a, b, dims="nn", res=None, out_dtype=F32, name="mm"):
    if dims == "nn":
        (M, K), (K2, N) = a.shape, b.shape
    elif dims == "nt":
        (M, K), (N, K2) = a.shape, b.shape
    else:
        (K, M), (K2, N) = a.shape, b.shape
    assert K == K2, (a.shape, b.shape, dims)
    tm, tn, tk = _pick(M, 512), _pick(N, 1408), _pick(K, 1408)
    nk = K // tk
    dn = {"nn": NN, "nt": NT, "tn": TN}[dims]
    has_res = res is not None

    def body(*refs):
        if has_res:
            a_ref, b_ref, r_ref, o_ref, acc = refs
        else:
            a_ref, b_ref, o_ref, acc = refs
        k = pl.program_id(2)

        @pl.when(k == 0)
        def _():
            acc[...] = jnp.zeros_like(acc)

        acc[...] += _dot(a_ref[...].astype(BF16), b_ref[...].astype(BF16), dn)

        @pl.when(k == nk - 1)
        def _():
            r = acc[...]
            if has_res:
                r = r + r_ref[...]
            o_ref[...] = r.astype(out_dtype)

    a_spec = (pl.BlockSpec((tk, tm), lambda i, j, k: (k, i)) if dims == "tn"
              else pl.BlockSpec((tm, tk), lambda i, j, k: (i, k)))
    b_spec = (pl.BlockSpec((tn, tk), lambda i, j, k: (j, k)) if dims == "nt"
              else pl.BlockSpec((tk, tn), lambda i, j, k: (k, j)))
    o_spec = pl.BlockSpec((tm, tn), lambda i, j, k: (i, j))
    in_specs = [a_spec, b_spec] + ([o_spec] if has_res else [])
    args = (a, b) + ((res,) if has_res else ())
    return pl.pallas_call(
        body, name=name, grid=(M // tm, N // tn, nk),
        in_specs=in_specs, out_specs=o_spec,
        out_shape=jax.ShapeDtypeStruct((M, N), out_dtype),
        scratch_shapes=[pltpu.VMEM((tm, tn), F32)],
        compiler_params=_params(("parallel", "parallel", "arbitrary")),
    )(*args)


def _rms_fwd(x, g, out_dtype=BF16, name="rms_fwd"):
    T, Fd = x.shape
    tm = _pick(T, 256)

    def body(x_ref, g_ref, o_ref):
        xv = x_ref[...]
        r = lax.rsqrt(jnp.mean(xv * xv, axis=-1, keepdims=True) + RMS_EPS)
        o_ref[...] = (xv * r * g_ref[...]).astype(out_dtype)

    return pl.pallas_call(
        body, name=name, grid=(T // tm,),
        in_specs=[pl.BlockSpec((tm, Fd), lambda i: (i, 0)), pl.BlockSpec((1, Fd), lambda i: (0, 0))],
        out_specs=pl.BlockSpec((tm, Fd), lambda i: (i, 0)),
        out_shape=jax.ShapeDtypeStruct((T, Fd), out_dtype),
        compiler_params=_params(("parallel",)),
    )(x, g)


def _rms_bwd(x, g, dy, dres=None, name="rms_bwd"):
    T, Fd = x.shape
    tm = _pick(T, 256)
    has_res = dres is not None

    def body(*refs):
        if has_res:
            x_ref, g_ref, dy_ref, r_ref, dx_ref, dxb_ref, dg_ref = refs
        else:
            x_ref, g_ref, dy_ref, dx_ref, dxb_ref, dg_ref = refs
        xv, dyv = x_ref[...], dy_ref[...]
        r = lax.rsqrt(jnp.mean(xv * xv, axis=-1, keepdims=True) + RMS_EPS)
        gdy = dyv * g_ref[...]
        dot = jnp.mean(xv * gdy, axis=-1, keepdims=True)
        dx = r * gdy - xv * (r * r * r * dot)
        if has_res:
            dx = dx + r_ref[...]
        dx_ref[...] = dx
        dxb_ref[...] = dx.astype(BF16)

        @pl.when(pl.program_id(0) == 0)
        def _():
            dg_ref[...] = jnp.zeros_like(dg_ref)

        dg_ref[...] += jnp.sum(dyv * xv * r, axis=0, keepdims=True)

    row = pl.BlockSpec((tm, Fd), lambda i: (i, 0))
    vec = pl.BlockSpec((1, Fd), lambda i: (0, 0))
    in_specs = [row, vec, row] + ([row] if has_res else [])
    args = (x, g, dy) + ((dres,) if has_res else ())
    return pl.pallas_call(
        body, name=name, grid=(T // tm,),
        in_specs=in_specs, out_specs=[row, row, vec],
        out_shape=[jax.ShapeDtypeStruct((T, Fd), F32), jax.ShapeDtypeStruct((T, Fd), BF16),
                   jax.ShapeDtypeStruct((1, Fd), F32)],
        compiler_params=_params(("arbitrary",)),
    )(*args)


def _loss_head(h, g, target, name="loss_head"):
    T, Fd = h.shape
    tm = _pick(T, 256)

    def body(h_ref, g_ref, t_ref, loss_ref, dh_ref, dhb_ref, dg_ref):
        xv = h_ref[...]
        r = lax.rsqrt(jnp.mean(xv * xv, axis=-1, keepdims=True) + RMS_EPS)
        diff = xv * r * g_ref[...] - t_ref[...]
        part = 0.5 * jnp.sum(jnp.mean(diff * diff, axis=-1, keepdims=True), axis=0, keepdims=True)
        dyv = diff * (1.0 / Fd)
        gdy = dyv * g_ref[...]
        dot = jnp.mean(xv * gdy, axis=-1, keepdims=True)
        dh = r * gdy - xv * (r * r * r * dot)
        dh_ref[...] = dh
        dhb_ref[...] = dh.astype(BF16)

        @pl.when(pl.program_id(0) == 0)
        def _():
            dg_ref[...] = jnp.zeros_like(dg_ref)
            loss_ref[...] = jnp.zeros_like(loss_ref)

        dg_ref[...] += jnp.sum(dyv * xv * r, axis=0, keepdims=True)
        loss_ref[...] += jnp.broadcast_to(part, loss_ref.shape)

    row = pl.BlockSpec((tm, Fd), lambda i: (i, 0))
    vec = pl.BlockSpec((1, Fd), lambda i: (0, 0))
    return pl.pallas_call(
        body, name=name, grid=(T // tm,),
        in_specs=[row, vec, row],
        out_specs=[pl.BlockSpec((1, LANES), lambda i: (0, 0)), row, row, vec],
        out_shape=[jax.ShapeDtypeStruct((1, LANES), F32), jax.ShapeDtypeStruct((T, Fd), F32),
                   jax.ShapeDtypeStruct((T, Fd), BF16), jax.ShapeDtypeStruct((1, Fd), F32)],
        compiler_params=_params(("arbitrary",)),
    )(h, g, target)


FFN_TF = 256


def _ffn_fwd(h, g, wg_t, wu_t, wd, name="ffn_fwd"):
    T, Dm = h.shape
    Fh = wd.shape[0]
    tm = _pick(T, 1024)
    nf = Fh // FFN_TF

    def body(h_ref, g_ref, wg_ref, wu_ref, wd_ref, o_ref, u_ref, a_ref, b_ref):
        j = pl.program_id(1)

        @pl.when(j == 0)
        def _():
            xv = h_ref[...]
            r = lax.rsqrt(jnp.mean(xv * xv, axis=-1, keepdims=True) + RMS_EPS)
            u_ref[...] = (xv * r * g_ref[...]).astype(BF16)
            o_ref[...] = xv

        u = u_ref[...]
        a = _dot(u, wg_ref[...], NT).astype(BF16)
        b = _dot(u, wu_ref[...], NT).astype(BF16)
        a_ref[...] = a
        b_ref[...] = b
        af = a.astype(F32)
        s = (af * jax.nn.sigmoid(af) * b.astype(F32)).astype(BF16)
        o_ref[...] += _dot(s, wd_ref[...], NN)

    row = pl.BlockSpec((tm, Dm), lambda i, j: (i, 0))
    wblk = pl.BlockSpec((FFN_TF, Dm), lambda i, j: (j, 0))
    ablk = pl.BlockSpec((tm, FFN_TF), lambda i, j: (i, j))
    return pl.pallas_call(
        body, name=name, grid=(T // tm, nf),
        in_specs=[row, pl.BlockSpec((1, Dm), lambda i, j: (0, 0)), wblk, wblk, wblk],
        out_specs=[row, row, ablk, ablk],
        out_shape=[jax.ShapeDtypeStruct((T, Dm), F32), jax.ShapeDtypeStruct((T, Dm), BF16),
                   jax.ShapeDtypeStruct((T, Fh), BF16), jax.ShapeDtypeStruct((T, Fh), BF16)],
        compiler_params=_params(("parallel", "arbitrary")),
    )(h, g, wg_t, wu_t, wd)


def _ffn_bwd(dh, u, a, b, wg_t, wu_t, wd, name="ffn_bwd"):
    T, Dm = dh.shape
    Fh = wd.shape[0]
    nf = Fh // FFN_TF
    once = pl.Buffered(1)

    def body(dh_ref, u_ref, a_ref, b_ref, wg_ref, wu_ref, wd_ref, du_ref, dwg_ref, dwu_ref, dwd_ref):
        j = pl.program_id(0)

        @pl.when(j == 0)
        def _():
            du_ref[...] = jnp.zeros_like(du_ref)

        ds = _dot(dh_ref[...], wd_ref[...], NT)
        af, bf = a_ref[...].astype(F32), b_ref[...].astype(F32)
        sig = jax.nn.sigmoid(af)
        sa = af * sig
        dwd_ref[...] = _dot((sa * bf).astype(BF16), dh_ref[...], TN).astype(BF16)
        dab = jnp.concatenate([(ds * bf * (sig * (1.0 + af * (1.0 - sig)))).astype(BF16),
                               (ds * sa).astype(BF16)], axis=1)
        dw = _dot(dab, u_ref[...], TN)
        dwg_ref[...] = dw[:FFN_TF].astype(BF16)
        dwu_ref[...] = dw[FFN_TF:].astype(BF16)
        du_ref[...] += _dot(dab, jnp.concatenate([wg_ref[...], wu_ref[...]], axis=0), NN)

    full = lambda: pl.BlockSpec((T, Dm), lambda j: (0, 0), pipeline_mode=once)
    wblk = pl.BlockSpec((FFN_TF, Dm), lambda j: (j, 0))
    ablk = pl.BlockSpec((T, FFN_TF), lambda j: (0, j))
    return pl.pallas_call(
        body, name=name, grid=(nf,),
        in_specs=[full(), full(), ablk, ablk, wblk, wblk, wblk],
        out_specs=[pl.BlockSpec((T, Dm), lambda j: (0, 0)), wblk, wblk, wblk],
        out_shape=[jax.ShapeDtypeStruct((T, Dm), F32)] + [jax.ShapeDtypeStruct((Fh, Dm), BF16)] * 3,
        compiler_params=_params(("arbitrary",)),
    )(dh, u, a, b, wg_t, wu_t, wd)


def _rope(x, cos_t, sin_t, col0, ncols, out_dtype, name="rope"):
    T = x.shape[0]
    wt = cos_t.shape[1]
    tm = _pick(T, 256)
    nb = ncols * LANES // wt
    half = MLA_ROPE // 2

    def body(x_ref, c_ref, s_ref, o_ref):
        xv = x_ref[...].astype(F32)
        lane = lax.broadcasted_iota(jnp.int32, xv.shape, 1)
        first = (lane & (MLA_ROPE - 1)) < half
        swapped = jnp.where(first, pltpu.roll(xv, wt - half, 1), pltpu.roll(xv, half, 1))
        o_ref[...] = (xv * c_ref[...] + swapped * s_ref[...]).astype(out_dtype)

    off = col0 * LANES // wt
    return pl.pallas_call(
        body, name=name, grid=(T // tm, nb),
        in_specs=[pl.BlockSpec((tm, wt), lambda i, j: (i, j + off)),
                  pl.BlockSpec((tm, wt), lambda i, j: (i, 0)),
                  pl.BlockSpec((tm, wt), lambda i, j: (i, 0))],
        out_specs=pl.BlockSpec((tm, wt), lambda i, j: (i, j)),
        out_shape=jax.ShapeDtypeStruct((T, ncols * LANES), out_dtype),
        compiler_params=_params(("parallel", "parallel")),
    )(x, cos_t, sin_t)


ATT_T = 256


def _mla_masks(shape):
    lane = lax.broadcasted_iota(jnp.int32, shape, 1)
    m0 = (lane < 64) | ((lane >= 128) & (lane < 160))
    m1 = ((lane >= 64) & (lane < 128)) | ((lane >= 160) & (lane < 192))
    return m0, m1


def _by_twos(n, step, carry):
    carry = lax.fori_loop(0, n // 2, lambda i, c: step(2 * i + 1, step(2 * i, c)), carry)
    return lax.fori_loop(0, n % 2, lambda _, c: step(n - 1, c), carry)


def _chunk_ok(tq, tk):
    row = lax.broadcasted_iota(jnp.int32, (tq, tk), 0)
    col = lax.broadcasted_iota(jnp.int32, (tq, tk), 1)
    return (col >> 6) <= (row >> 6)


def _rotate(x, cos_t, sin_t):
    half = MLA_ROPE // 2
    lane = lax.broadcasted_iota(jnp.int32, x.shape, 1)
    first = (lane & (MLA_ROPE - 1)) < half
    swapped = jnp.where(first, pltpu.roll(x, x.shape[1] - half, 1), pltpu.roll(x, half, 1))
    return x * cos_t + swapped * sin_t


def _mla_fwd(q, cos_q, sin_q, kv, kr, name="mla_fwd"):
    T = q.shape[0]
    tq = tk = _pick(T, ATT_T)
    npair = MLA_HEADS // 2
    scale = (MLA_NOPE + MLA_ROPE) ** -0.5

    def body(q_ref, c_ref, s_ref, kn_ref, v_ref, kr_ref, o_ref, lse_ref):
        m_idx = pl.program_id(1)
        qv = _rotate(q_ref[...], c_ref[...], s_ref[...]).astype(BF16)
        m0, m1 = _mla_masks(qv.shape)
        qs = jnp.concatenate([jnp.where(m0, qv, 0), jnp.where(m1, qv, 0)], axis=0).astype(BF16)
        diag = jnp.concatenate([_chunk_ok(tq, tk)] * 2, axis=0)

        def block(kb, carry, ok):
            ks = pl.ds(pl.multiple_of(kb * tk, tk), tk)
            kcat = jnp.concatenate([kn_ref[ks, :], kr_ref[ks, :]], axis=1)
            mx, l, acc = carry
            s = _dot(qs, kcat, NT) * scale
            if ok is not None:
                s = jnp.where(ok, s, NEG)
            mn = jnp.maximum(mx, jnp.max(s, axis=-1, keepdims=True))
            alpha = jnp.exp(mx - mn)
            p = jnp.exp(s - mn)
            return (mn, alpha * l + jnp.sum(p, axis=-1, keepdims=True),
                    alpha * acc + _dot(p.astype(BF16), v_ref[ks, :], NN))

        init = (jnp.full((2 * tq, 1), NEG, F32), jnp.zeros((2 * tq, 1), F32), jnp.zeros((2 * tq, LANES), F32))
        mx, l, acc = block(m_idx, init, diag)
        mx, l, acc = _by_twos(m_idx, lambda kb, c: block(kb, c, None), (mx, l, acc))
        h0 = lax.broadcasted_iota(jnp.int32, (tq, LANES), 1) < 64
        o_ref[...] = _two_heads(acc * (1.0 / l), h0).astype(o_ref.dtype)
        lse_ref[...] = _two_heads(jnp.broadcast_to(mx + jnp.log(l), (2 * tq, LANES)), h0)

    full = lambda col: pl.BlockSpec((T, LANES), col)
    table = pl.BlockSpec((tq, 2 * LANES), lambda p, m: (m, 0))
    return pl.pallas_call(
        body, name=name, grid=(npair, T // tq),
        in_specs=[pl.BlockSpec((tq, 2 * LANES), lambda p, m: (m, p)), table, table,
                  full(lambda p, m: (0, p)), full(lambda p, m: (0, npair + p)), full(lambda p, m: (0, 0))],
        out_specs=[pl.BlockSpec((tq, LANES), lambda p, m: (m, p)),
                   pl.BlockSpec((tq, LANES), lambda p, m: (m, p))],
        out_shape=[jax.ShapeDtypeStruct((T, npair * LANES), BF16),
                   jax.ShapeDtypeStruct((T, npair * LANES), F32)],
        compiler_params=_params(("parallel", "arbitrary")),
    )(q, cos_q, sin_q, kv, kv, kr)


def _mla_bwd(q, cos_q, sin_q, kv, kr, o, lse, do, do_col0, name="mla_bwd"):
    T = q.shape[0]
    tq = tk = _pick(T, ATT_T)
    npair = MLA_HEADS // 2
    scale = (MLA_NOPE + MLA_ROPE) ** -0.5

    def body(q_ref, c_ref, s_ref, kn_ref, v_ref, kr_ref, o_ref, lse_ref, do_ref, dq_ref, dkn_ref, dv_ref, dkr_ref):
        p_idx, m_idx = pl.program_id(0), pl.program_id(1)

        @pl.when(m_idx == 0)
        def _():
            dkn_ref[...] = jnp.zeros_like(dkn_ref)
            dv_ref[...] = jnp.zeros_like(dv_ref)

        @pl.when((m_idx == 0) & (p_idx == 0))
        def _():
            dkr_ref[...] = jnp.zeros_like(dkr_ref)

        qv = _rotate(q_ref[...], c_ref[...], s_ref[...]).astype(BF16)
        m0, m1 = _mla_masks(qv.shape)
        qs = jnp.concatenate([jnp.where(m0, qv, 0), jnp.where(m1, qv, 0)], axis=0).astype(BF16)
        dov = do_ref[...].astype(F32)
        h0 = lax.broadcasted_iota(jnp.int32, (tq, LANES), 1) < 64
        dos32 = jnp.concatenate([jnp.where(h0, dov, 0.0), jnp.where(h0, 0.0, dov)], axis=0)
        ov = o_ref[...].astype(F32)
        delta = jnp.sum(dos32 * jnp.concatenate([ov, ov], axis=0), axis=-1, keepdims=True)
        dos = dos32.astype(BF16)
        lsev = lse_ref[...]
        lse = jnp.concatenate([lsev[:, 0:1], lsev[:, 64:65]], axis=0)
        diag = jnp.concatenate([_chunk_ok(tq, tk)] * 2, axis=0)

        def block(kb, dq, ok):
            ks = pl.ds(pl.multiple_of(kb * tk, tk), tk)
            kcat = jnp.concatenate([kn_ref[ks, :], kr_ref[ks, :]], axis=1)
            vv = v_ref[ks, :]
            p = jnp.exp(_dot(qs, kcat, NT) * scale - lse)
            if ok is not None:
                p = jnp.where(ok, p, 0.0)
            ds = (p * (_dot(dos, vv, NT) - delta) * scale).astype(BF16)
            dkc = _dot(ds, qs, TN)
            dkn_ref[ks, :] += dkc[:, :LANES]
            dkr_ref[ks, :] += dkc[:, LANES:]
            dv_ref[ks, :] += _dot(p.astype(BF16), dos, TN)
            return dq + _dot(ds, kcat, NN)

        dq = block(m_idx, jnp.zeros((2 * tq, 2 * LANES), F32), diag)
        dq = _by_twos(m_idx, lambda kb, c: block(kb, c, None), dq)
        dq_ref[...] = _rotate(jnp.where(m0, dq[:tq], jnp.where(m1, dq[tq:], 0.0)), c_ref[...], -s_ref[...])

    full = lambda col: pl.BlockSpec((T, LANES), col)
    blk = lambda col: pl.BlockSpec((tq, LANES), col)
    table = pl.BlockSpec((tq, 2 * LANES), lambda p, m: (m, 0))
    return pl.pallas_call(
        body, name=name, grid=(npair, T // tq),
        in_specs=[pl.BlockSpec((tq, 2 * LANES), lambda p, m: (m, p)), table, table,
                  full(lambda p, m: (0, p)), full(lambda p, m: (0, npair + p)), full(lambda p, m: (0, 0)),
                  blk(lambda p, m: (m, p)), blk(lambda p, m: (m, p)),
                  blk(lambda p, m: (m, do_col0 + p))],
        out_specs=[pl.BlockSpec((tq, 2 * LANES), lambda p, m: (m, p)),
                   full(lambda p, m: (0, p)), full(lambda p, m: (0, p)), full(lambda p, m: (0, 0))],
        out_shape=[jax.ShapeDtypeStruct((T, npair * 2 * LANES), F32),
                   jax.ShapeDtypeStruct((T, npair * LANES), F32),
                   jax.ShapeDtypeStruct((T, npair * LANES), F32),
                   jax.ShapeDtypeStruct((T, LANES), F32)],
        compiler_params=_params(("arbitrary", "arbitrary")),
    )(q, cos_q, sin_q, kv, kv, kr, o, lse, do)


def _split_dot(x, tri):
    hi = x.astype(BF16)
    lo = (x - hi.astype(F32)).astype(BF16)
    both = _dot(jnp.concatenate([hi, lo], axis=0), tri, NN)
    return both[:x.shape[0]] + both[x.shape[0]:]


def _sb_terms(qh, kk, before):
    z = _dot(qh, kk, NT)
    sp = jnp.maximum(z, 0.0) + jnp.log(1.0 + jnp.exp(-jnp.abs(z)))
    lk = -sp if before is None else jnp.where(before, -sp, 0.0)
    return z, sp, lk


def _sb_setup(q_ref, tq, tk, scale):
    qv = (q_ref[...].astype(F32) * scale).astype(BF16)
    lane = lax.broadcasted_iota(jnp.int32, (tq, LANES), 1)
    h0 = lane < 64
    qs = jnp.concatenate([jnp.where(h0, qv, 0), jnp.where(h0, 0, qv)], axis=0).astype(BF16)
    row = lax.broadcasted_iota(jnp.int32, (tk, tk), 0)
    col = lax.broadcasted_iota(jnp.int32, (tk, tk), 1)
    return qs, h0, row, col


def _two_heads(x, h0):
    tq = x.shape[0] // 2
    return jnp.where(h0, x[:tq], x[tq:])


def _sb_fwd(qkv, col0, name="sb_fwd"):
    T = qkv.shape[0]
    tq = tk = _pick(T, ATT_T)
    npair = SB_HEADS // 2
    scale = SB_DIM ** -0.5

    def body(q_ref, k_ref, v_ref, o_ref, o32_ref):
        m_idx = pl.program_id(1)
        qs, h0, row, col = _sb_setup(q_ref, tq, tk, scale)
        later = (row > col).astype(BF16)
        diag = jnp.concatenate([col < row] * 2, axis=0)

        def block(kb, carry, before):
            ks = pl.ds(pl.multiple_of(kb * tk, tk), tk)
            c, acc = carry
            z, sp, lk = _sb_terms(qs, k_ref[ks, :].astype(BF16), before)
            w = jnp.exp((z - sp) + _split_dot(lk, later) + c)
            if before is not None:
                w = jnp.where(before, w, 0.0)
            return (c + jnp.sum(lk, axis=-1, keepdims=True),
                    acc + _dot(w.astype(BF16), v_ref[ks, :].astype(BF16), NN))

        init = (jnp.zeros((2 * tq, 1), F32), jnp.zeros((2 * tq, LANES), F32))
        res = block(m_idx, init, diag)
        res = _by_twos(m_idx, lambda i, c: block(m_idx - 1 - i, c, None), res)
        o = _two_heads(res[1], h0)
        o_ref[...] = o.astype(o_ref.dtype)
        o32_ref[...] = o

    full = lambda col: pl.BlockSpec((T, LANES), col)
    blk = pl.BlockSpec((tq, LANES), lambda p, m: (m, p))
    return pl.pallas_call(
        body, name=name, grid=(npair, T // tq),
        in_specs=[pl.BlockSpec((tq, LANES), lambda p, m: (m, col0 + p)),
                  full(lambda p, m: (0, col0 + npair + p)), full(lambda p, m: (0, col0 + 2 * npair + p))],
        out_specs=[blk, blk],
        out_shape=[jax.ShapeDtypeStruct((T, npair * LANES), BF16), jax.ShapeDtypeStruct((T, npair * LANES), F32)],
        compiler_params=_params(("parallel", "arbitrary")),
    )(qkv, qkv, qkv)


def _sb_bwd(qkv, col0, o32, do, do_col0, dep, name="sb_bwd"):
    T = qkv.shape[0]
    tq = tk = _pick(T, ATT_T)
    npair = SB_HEADS // 2
    scale = SB_DIM ** -0.5

    def body(q_ref, k_ref, v_ref, o_ref, do_ref, dep_ref, dq_ref, dk_ref, dv_ref):
        m_idx = pl.program_id(1)

        @pl.when(m_idx == 0)
        def _():
            dk_ref[...] = jnp.zeros_like(dk_ref)
            dv_ref[...] = jnp.zeros_like(dv_ref)

        qs, h0, row, col = _sb_setup(q_ref, tq, tk, scale)
        dov = do_ref[...].astype(F32)
        dos = jnp.concatenate([jnp.where(h0, dov, 0.0), jnp.where(h0, 0.0, dov)], axis=0).astype(BF16)
        ov = o_ref[...]
        etot = jnp.sum(dos.astype(F32) * jnp.concatenate([ov, ov], axis=0), axis=-1, keepdims=True)
        later = (row > col).astype(BF16)
        from_here = (row >= col).astype(BF16)
        diag = jnp.concatenate([col < row] * 2, axis=0)

        def block(kb, carry, before):
            ks = pl.ds(pl.multiple_of(kb * tk, tk), tk)
            kk = k_ref[ks, :].astype(BF16)
            vv = v_ref[ks, :].astype(BF16)
            c, es, dqa = carry
            z, sp, lk = _sb_terms(qs, kk, before)
            w = jnp.exp((z - sp) + _split_dot(lk, later) + c)
            if before is not None:
                w = jnp.where(before, w, 0.0)
            wb = w.astype(BF16)
            e = wb.astype(F32) * _dot(dos, vv, NT)
            prev = etot - (_split_dot(e, from_here) + es)
            sig_neg = jnp.exp(-sp)
            dz = e * sig_neg - (1.0 - sig_neg) * prev
            if before is not None:
                dz = jnp.where(before, dz, 0.0)
            dzb = dz.astype(BF16)
            dk_ref[ks, :] += _dot(dzb, qs, TN)
            dv_ref[ks, :] += _dot(wb, dos, TN)
            return (c + jnp.sum(lk, axis=-1, keepdims=True), es + jnp.sum(e, axis=-1, keepdims=True),
                    dqa + _dot(dzb, kk, NN))

        init = (jnp.zeros((2 * tq, 1), F32), jnp.zeros((2 * tq, 1), F32), jnp.zeros((2 * tq, LANES), F32))
        res = block(m_idx, init, diag)
        res = _by_twos(m_idx, lambda i, c: block(m_idx - 1 - i, c, None), res)
        dq_ref[...] = _two_heads(res[2], h0) * scale

    full = lambda col: pl.BlockSpec((T, LANES), col)
    blk = lambda col: pl.BlockSpec((tq, LANES), col)
    return pl.pallas_call(
        body, name=name, grid=(npair, T // tq),
        in_specs=[blk(lambda p, m: (m, col0 + p)),
                  full(lambda p, m: (0, col0 + npair + p)), full(lambda p, m: (0, col0 + 2 * npair + p)),
                  blk(lambda p, m: (m, p)), blk(lambda p, m: (m, do_col0 + p)),
                  pl.BlockSpec((8, LANES), lambda p, m: (0, 0))],
        out_specs=[blk(lambda p, m: (m, p)), full(lambda p, m: (0, p)), full(lambda p, m: (0, p))],
        out_shape=[jax.ShapeDtypeStruct((T, npair * LANES), F32)] * 3,
        compiler_params=_params(("arbitrary", "arbitrary")),
    )(qkv, qkv, qkv, o32, do, dep)


def _band_in_window():
    cq = lax.broadcasted_iota(jnp.int32, (BAND_TQ, BAND_W), 0) >> 6
    ckp = lax.broadcasted_iota(jnp.int32, (BAND_TQ, BAND_W), 1) >> 6
    return (ckp >= cq) & (ckp <= cq + LEFT_CHUNKS)


def _band_real(m_idx):
    j = lax.broadcasted_iota(jnp.int32, (BAND_TQ, BAND_W), 1)
    return j >= PAD_KEYS - m_idx * BAND_TQ


def _band_probs(qh, kw, bias, real, scale):
    s = jnp.where(real, _dot(qh, kw, NT) * scale + bias, NEG)
    e = jnp.exp(s - jnp.max(s, axis=-1, keepdims=True))
    return e * (1.0 / jnp.sum(e, axis=-1, keepdims=True))


BAND_SUB = 2


def _band_fwd(qkv, k_pad, v_pad, bias_w, name="band_fwd"):
    T = qkv.shape[0]
    npair = C_HEADS // 2
    scale = C_DIM ** -0.5
    rows = BAND_SUB * BAND_TQ

    def body(q_ref, k_ref, v_ref, b_ref, o_ref):
        lane = lax.broadcasted_iota(jnp.int32, (BAND_TQ, LANES), 1)
        h0 = lane < 64
        bias = jnp.concatenate([b_ref[0], b_ref[1]], axis=0)
        for sub in range(BAND_SUB):
            m_idx = pl.program_id(1) * BAND_SUB + sub
            win = pl.ds(pl.multiple_of(m_idx * BAND_TQ, BAND_TQ), BAND_W)
            kw, vw = k_ref[win, :], v_ref[win, :]
            qv = q_ref[sub * BAND_TQ:(sub + 1) * BAND_TQ, :]
            qs = jnp.concatenate([jnp.where(h0, qv, 0), jnp.where(h0, 0, qv)], axis=0).astype(BF16)
            p = _band_probs(qs, kw, bias, jnp.concatenate([_band_real(m_idx)] * 2, axis=0), scale)
            o = _two_heads(_dot(p.astype(BF16), vw, NN), h0)
            o_ref[sub * BAND_TQ:(sub + 1) * BAND_TQ, :] = o.astype(o_ref.dtype)

    Tp = T + PAD_KEYS
    return pl.pallas_call(
        body, name=name, grid=(npair, T // rows),
        in_specs=[pl.BlockSpec((rows, LANES), lambda p, m: (m, p)),
                  pl.BlockSpec((Tp, LANES), lambda p, m: (0, p)),
                  pl.BlockSpec((Tp, LANES), lambda p, m: (0, p)),
                  pl.BlockSpec((2, BAND_TQ, BAND_W), lambda p, m: (p, 0, 0))],
        out_specs=pl.BlockSpec((rows, LANES), lambda p, m: (m, p)),
        out_shape=jax.ShapeDtypeStruct((T, npair * LANES), BF16),
        compiler_params=_params(("parallel", "arbitrary")),
    )(qkv, k_pad, v_pad, bias_w)


def _band_bwd(qkv, k_pad, v_pad, bias_w, do, name="band_bwd"):
    T = qkv.shape[0]
    npair = C_HEADS // 2
    scale = C_DIM ** -0.5

    rows = BAND_SUB * BAND_TQ

    def body(q_ref, k_ref, v_ref, b_ref, do_ref, dq_ref, dk_ref, dv_ref, db_ref):
        @pl.when(pl.program_id(1) == 0)
        def _():
            dk_ref[...] = jnp.zeros_like(dk_ref)
            dv_ref[...] = jnp.zeros_like(dv_ref)
            db_ref[...] = jnp.zeros_like(db_ref)

        lane = lax.broadcasted_iota(jnp.int32, (BAND_TQ, LANES), 1)
        h0 = lane < 64
        dbs = jnp.zeros((2 * BAND_TQ, BAND_W), F32)
        bias = jnp.concatenate([b_ref[0], b_ref[1]], axis=0)
        for sub in range(BAND_SUB):
            m_idx = pl.program_id(1) * BAND_SUB + sub
            win = pl.ds(pl.multiple_of(m_idx * BAND_TQ, BAND_TQ), BAND_W)
            kw, vw = k_ref[win, :], v_ref[win, :]
            qv = q_ref[sub * BAND_TQ:(sub + 1) * BAND_TQ, :]
            dov = do_ref[sub * BAND_TQ:(sub + 1) * BAND_TQ, :].astype(F32)
            qs = jnp.concatenate([jnp.where(h0, qv, 0), jnp.where(h0, 0, qv)], axis=0).astype(BF16)
            dos = jnp.concatenate([jnp.where(h0, dov, 0.0), jnp.where(h0, 0.0, dov)], axis=0).astype(BF16)
            p = _band_probs(qs, kw, bias, jnp.concatenate([_band_real(m_idx)] * 2, axis=0), scale)
            dp = _dot(dos, vw, NT)
            dsb = p * (dp - jnp.sum(p * dp, axis=-1, keepdims=True))
            dbs = dbs + dsb
            dsq = (dsb * scale).astype(BF16)
            dq_ref[sub * BAND_TQ:(sub + 1) * BAND_TQ, :] = _two_heads(_dot(dsq, kw, NN), h0)
            dk_ref[win, :] += _dot(dsq, qs, TN)
            dv_ref[win, :] += _dot(p.astype(BF16), dos, TN)
        db_ref[0] += dbs[:BAND_TQ]
        db_ref[1] += dbs[BAND_TQ:]

    Tp = T + PAD_KEYS
    blk = lambda col: pl.BlockSpec((rows, LANES), col)
    full = pl.BlockSpec((Tp, LANES), lambda p, m: (0, p))
    bias = pl.BlockSpec((2, BAND_TQ, BAND_W), lambda p, m: (p, 0, 0))
    return pl.pallas_call(
        body, name=name, grid=(npair, T // rows),
        in_specs=[blk(lambda p, m: (m, p)), full, full, bias, blk(lambda p, m: (m, p))],
        out_specs=[blk(lambda p, m: (m, p)), full, full, bias],
        out_shape=[jax.ShapeDtypeStruct((T, npair * LANES), F32),
                   jax.ShapeDtypeStruct((Tp, npair * LANES), F32),
                   jax.ShapeDtypeStruct((Tp, npair * LANES), F32),
                   jax.ShapeDtypeStruct((C_HEADS, BAND_TQ, BAND_W), F32)],
        compiler_params=_params(("arbitrary", "arbitrary")),
    )(qkv, k_pad, v_pad, bias_w, do)


def _skew_bits(x, left):
    w = x.shape[1]
    row = lax.broadcasted_iota(jnp.int32, x.shape, 0)
    for b in range(BAND_TQ.bit_length() - 1):
        amt = (w - (1 << b)) if left else (1 << b)
        x = jnp.where((row >> b) & 1 == 1, pltpu.roll(x, amt, 1), x)
    return x


def _toeplitz(diag, name="toeplitz"):
    H = diag.shape[0]

    def body(d_ref, o_ref):
        x = jnp.broadcast_to(d_ref[0], (BAND_TQ, TOEP_W))
        o_ref[0] = jnp.where(_band_in_window(), _skew_bits(x, left=False)[:, BAND_TQ:], NEG)

    return pl.pallas_call(
        body, name=name, grid=(H,),
        in_specs=[pl.BlockSpec((1, 1, TOEP_W), lambda h: (h, 0, 0))],
        out_specs=pl.BlockSpec((1, BAND_TQ, BAND_W), lambda h: (h, 0, 0)),
        out_shape=jax.ShapeDtypeStruct((H, BAND_TQ, BAND_W), F32),
        compiler_params=_params(("parallel",)),
    )(diag.reshape(H, 1, TOEP_W))


def _toeplitz_bwd(dbias, name="toeplitz_bwd"):
    H = dbias.shape[0]

    def body(d_ref, o_ref):
        x = jnp.concatenate([jnp.zeros((BAND_TQ, BAND_TQ), F32), d_ref[0]], axis=1)
        o_ref[0] = jnp.sum(_skew_bits(x, left=True), axis=0, keepdims=True)

    return pl.pallas_call(
        body, name=name, grid=(H,),
        in_specs=[pl.BlockSpec((1, BAND_TQ, BAND_W), lambda h: (h, 0, 0))],
        out_specs=pl.BlockSpec((1, 1, TOEP_W), lambda h: (h, 0, 0)),
        out_shape=jax.ShapeDtypeStruct((H, 1, TOEP_W), F32),
        compiler_params=_params(("parallel",)),
    )(dbias).reshape(H, TOEP_W)


_HBM = pl.BlockSpec(memory_space=pltpu.HBM)
_SEM = pl.BlockSpec(memory_space=pltpu.SEMAPHORE)
_EFFECT = pltpu.SideEffectType.DATAFLOW_SIDE_EFFECTING


def _peers():
    x, y, c = lax.axis_index("x"), lax.axis_index("y"), lax.axis_index("c")
    out = []
    for k in range(1, N_DEV):
        peer = (1 - x if (k >> 2) & 1 else x, 1 - y if (k >> 1) & 1 else y, 1 - c if k & 1 else c)
        out.append((peer, 4 * peer[0] + 2 * peer[1] + peer[2]))
    return 4 * x + 2 * y + c, out


def _split_copies(ins, lands, scatter, send_sem, recv_sem, arriving):
    me, peers = _peers()
    out = []
    for a in range(len(ins)):
        for peer, idx in peers:
            out.append(pltpu.make_async_remote_copy(
                src_ref=ins[a].at[idx] if scatter[a] else ins[a],
                dst_ref=lands[a].at[idx if arriving else me], send_sem=send_sem, recv_sem=recv_sem,
                device_id=peer, device_id_type=pl.DeviceIdType.MESH))
    return out


def _landing_zones(arrays, scatter):
    return [lax.empty((N_DEV,) + (a.shape[1:] if s else a.shape), a.dtype) for a, s in zip(arrays, scatter)]


def _place_own(arrays, scatter, name):
    n = len(arrays)
    lands = _landing_zones(arrays, scatter)
    me = (4 * lax.axis_index("x") + 2 * lax.axis_index("y") + lax.axis_index("c")).astype(jnp.int32).reshape(1)

    def body(me_ref, *refs):
        for a in range(n):
            refs[2 * n + a][...] = refs[a][...].reshape(refs[2 * n + a].shape)

    def row_spec(shape):
        zeros = (0,) * (len(shape) - 1)
        return pl.BlockSpec((1,) + tuple(shape[1:]), lambda i, me_ref: (me_ref[0],) + zeros)

    in_specs = [row_spec(a.shape) if s else pl.BlockSpec(a.shape, lambda i, me_ref, nd=a.ndim: (0,) * nd)
                for a, s in zip(arrays, scatter)]
    return pl.pallas_call(
        body, name=name,
        out_shape=[jax.ShapeDtypeStruct(l.shape, l.dtype) for l in lands],
        grid_spec=pltpu.PrefetchScalarGridSpec(
            num_scalar_prefetch=1, grid=(1,),
            in_specs=in_specs + [pl.BlockSpec(memory_space=pl.ANY)] * n,
            out_specs=[row_spec(l.shape) for l in lands]),
        input_output_aliases={1 + n + i: i for i in range(n)},
        compiler_params=_params(("arbitrary",)),
    )(me, *arrays, *lands)


def _exchange_start(arrays, scatter, after, name):
    n = len(arrays)
    lands = list(_place_own(arrays, scatter, name=name.replace("_start_", "_own_")))

    def body(*refs):
        ins, lnd = refs[:n], refs[n:2 * n]
        send_sem, recv_sem = refs[2 * n + 1:2 * n + 3]
        token = refs[-1]
        for cp in _split_copies(ins, lnd, scatter, send_sem, recv_sem, arriving=False):
            cp.start()
        token[...] = jnp.zeros_like(token)

    hbm = lambda a: pltpu.HBM(a.shape, a.dtype)
    out = pl.pallas_call(
        body, name=name,
        out_shape=(pltpu.SemaphoreType.DMA(()), pltpu.SemaphoreType.DMA(()),
                   *[hbm(a) for a in arrays], *[hbm(a) for a in lands],
                   jax.ShapeDtypeStruct((8, LANES), F32)),
        in_specs=[_HBM] * (2 * n) + [pl.BlockSpec(memory_space=pl.ANY)],
        out_specs=(_SEM, _SEM, *([_HBM] * (2 * n)), pl.BlockSpec(memory_space=pltpu.VMEM)),
        input_output_aliases={i: 2 + i for i in range(2 * n)},
        compiler_params=pltpu.CompilerParams(has_side_effects=_EFFECT),
    )(*[pltpu.with_memory_space_constraint(a, pltpu.HBM) for a in list(arrays) + lands], after)
    return (out[0], out[1], list(out[2:2 + n]), list(out[2 + n:2 + 2 * n]), tuple(scatter)), out[-1]


def _exchange_wait(handle, after, name):
    send_sem, recv_sem, ins, lands, scatter = handle
    n = len(ins)
    after = after if isinstance(after, tuple) else (after,)

    def body(*refs):
        i_ref, l_ref = refs[:n], refs[n:2 * n]
        s_sem, r_sem = refs[2 * n:2 * n + 2]
        for cp in _split_copies(i_ref, l_ref, scatter, s_sem, r_sem, arriving=False):
            cp.wait_send()
        for cp in _split_copies(i_ref, l_ref, scatter, s_sem, r_sem, arriving=True):
            cp.wait_recv()

    hbm = lambda a: pltpu.HBM(a.shape, a.dtype)
    out = pl.pallas_call(
        body, name=name,
        out_shape=tuple(hbm(a) for a in ins + lands),
        in_specs=[_HBM] * (2 * n) + [_SEM, _SEM] + [pl.BlockSpec(memory_space=pl.ANY)] * len(after),
        out_specs=tuple([_HBM] * (2 * n)),
        input_output_aliases={i: i for i in range(2 * n)},
        compiler_params=pltpu.CompilerParams(has_side_effects=_EFFECT),
    )(*ins, *lands, send_sem, recv_sem, *after)
    return list(out[n:])


def _adamw(w, parts, m, v, name="adamw"):
    R, C = w.shape
    tr = max([t for t in range(16, 513, 16) if R % t == 0], default=R)
    c1 = 1.0 - ADAM_B1 ** ADAM_STEP
    c2 = 1.0 - ADAM_B2 ** ADAM_STEP

    def body(w_ref, p_ref, m_ref, v_ref, g_ref, d_ref, nm_ref, nv_ref):
        g = p_ref[0].astype(F32)
        for i in range(1, N_DEV):
            g = g + p_ref[i].astype(F32)
        nm = ADAM_B1 * m_ref[...] + (1.0 - ADAM_B1) * g
        nv = ADAM_B2 * v_ref[...] + (1.0 - ADAM_B2) * (g * g)
        g_ref[...] = g
        nm_ref[...] = nm
        nv_ref[...] = nv
        d_ref[...] = -ADAM_LR * ((nm / c1) / (jnp.sqrt(nv / c2) + ADAM_EPS) + ADAM_WD * w_ref[...])

    blk = pl.BlockSpec((tr, C), lambda i: (i, 0))
    return pl.pallas_call(
        body, name=name, grid=(R // tr,),
        in_specs=[blk, pl.BlockSpec((N_DEV, tr, C), lambda i: (0, i, 0)), blk, blk],
        out_specs=[blk] * 4,
        out_shape=[jax.ShapeDtypeStruct((R, C), F32)] * 4,
        compiler_params=_params(("parallel",)),
    )(w, parts, m, v)


_O1 = Q_LORA
_O2 = _O1 + KV_LORA
_O3 = _O2 + MLA_ROPE
_NB = SB_HEADS * SB_DIM
IN_W = _O2 + LANES + 3 * _NB
COL_KR = _O2 // LANES
COL_SB = COL_KR + 1


def _w_in_local(w):
    kr = w[_O2:_O3]
    pad = jnp.zeros((LANES - 2 * MLA_ROPE, w.shape[1]), w.dtype)
    return jnp.concatenate([w[:_O2], kr, kr, pad, w[_O3:]], axis=0)


def _w_in_grad(g):
    kr = (g[_O2:_O2 + MLA_ROPE].astype(F32) + g[_O2 + MLA_ROPE:_O2 + 2 * MLA_ROPE].astype(F32)).astype(g.dtype)
    return jnp.concatenate([g[:_O2], kr, g[_O2 + LANES:]], axis=0)


def _w_uq_local(w):
    w3 = w.reshape(MLA_HEADS // 2, 2, MLA_NOPE + MLA_ROPE, w.shape[1])
    nope = w3[:, :, :MLA_NOPE].reshape(MLA_HEADS // 2, 2 * MLA_NOPE, w.shape[1])
    rope = w3[:, :, MLA_NOPE:].reshape(MLA_HEADS // 2, 2 * MLA_ROPE, w.shape[1])
    pad = jnp.zeros((MLA_HEADS // 2, LANES - 2 * MLA_ROPE, w.shape[1]), w.dtype)
    return jnp.concatenate([nope, rope, pad], axis=1).reshape(-1, w.shape[1])


def _w_uq_grad(g):
    g3 = g.reshape(MLA_HEADS // 2, 2 * LANES, g.shape[1])
    nope = g3[:, :2 * MLA_NOPE].reshape(MLA_HEADS // 2, 2, MLA_NOPE, g.shape[1])
    rope = g3[:, LANES:LANES + 2 * MLA_ROPE].reshape(MLA_HEADS // 2, 2, MLA_ROPE, g.shape[1])
    return jnp.concatenate([nope, rope], axis=2).reshape(-1, g.shape[1])


def _w_ukv_local(w):
    w3 = w.reshape(MLA_HEADS, MLA_NOPE + MLA_V, w.shape[1])
    return jnp.concatenate([w3[:, :MLA_NOPE].reshape(-1, w.shape[1]),
                            w3[:, MLA_NOPE:].reshape(-1, w.shape[1])], axis=0)


def _w_ukv_grad(g):
    half = MLA_HEADS * MLA_NOPE
    kn = g[:half].reshape(MLA_HEADS, MLA_NOPE, g.shape[1])
    vv = g[half:].reshape(MLA_HEADS, MLA_V, g.shape[1])
    return jnp.concatenate([kn, vv], axis=1).reshape(-1, g.shape[1])


def _rope_tables(T):
    pos = jnp.arange(T, dtype=F32)
    inv_freq = ROPE_THETA ** (-jnp.arange(0, MLA_ROPE, 2, dtype=F32) / MLA_ROPE)
    ang = pos[:, None] * inv_freq[None, :]
    cos, sin = jnp.cos(ang), jnp.sin(ang)
    ones = jnp.ones((T, LANES - 2 * MLA_ROPE), F32)
    cos_k = jnp.concatenate([cos, cos, cos, cos, ones], axis=1)
    sin_k = jnp.concatenate([-sin, sin, -sin, sin, 0.0 * ones], axis=1)
    cos_q = jnp.concatenate([jnp.ones((T, LANES), F32), cos_k], axis=1)
    sin_q = jnp.concatenate([jnp.zeros((T, LANES), F32), sin_k], axis=1)
    return cos_q, sin_q, cos_k, sin_k


def _bias_diag_index():
    ell = np.arange(TOEP_W)
    return np.clip(BAND_W - ell, -REL_CLIP, REL_CLIP) + REL_CLIP


def _local_step(x, target, small, get_weights, put_grads):
    T = x.shape[0]
    cos_q, sin_q, cos_k, sin_k = _rope_tables(T)
    G = {}
    W = dict(small)

    u0 = _rms_fwd(x, W["g_mix"][0:1], name="rms_mix0")
    bias_w = _toeplitz(W["od_rel_bias"][:, _bias_diag_index()])
    W.update(get_weights("in0", (u0, bias_w)))
    proj = _mm(u0, W["w_in_t"], dims="nt", name="proj_in")
    W.update(get_weights("mix0", proj))
    c_q, c_kv = proj[:, :_O1], proj[:, _O1:_O2]
    nq = _rms_fwd(c_q, W["g_cq"], name="rms_cq")
    nkv = _rms_fwd(c_kv, W["g_ckv"], name="rms_ckv")
    qa_raw = _mm(nq, W["w_uq_t"], dims="nt", name="proj_uq")
    kv = _mm(nkv, W["w_ukv_t"], dims="nt", out_dtype=BF16, name="proj_ukv")
    kr = _rope(proj, cos_k, sin_k, COL_KR, 1, BF16, name="rope_k")
    o_a, lse = _mla_fwd(qa_raw, cos_q, sin_q, kv, kr)
    o_b, o_b32 = _sb_fwd(proj, COL_SB)
    o_ab = jnp.concatenate([o_a, o_b], axis=1)
    h1 = _mm(o_ab, W["ev_w_out"], res=x, name="out_ev")

    def ffn_fwd(h, layer):
        W.update(get_weights(f"ffn{layer}", h))
        return _ffn_fwd(h, W["g_ffn"][layer:layer + 1], W[f"w_gate_t{layer}"], W[f"w_up_t{layer}"],
                        W[f"w_down{layer}"], name=f"ffn_fwd{layer}")

    h2, u1, a0, b0 = ffn_fwd(h1, 0)

    W.update(get_weights("mix1", h2))
    u2 = _rms_fwd(h2, W["g_mix"][1:2], name="rms_mix1")
    qkv = _mm(u2, W["od_w_qkv_t"], dims="nt", out_dtype=BF16, name="proj_qkv")
    nc = C_HEADS * C_DIM
    pad = ((PAD_KEYS, 0), (0, 0))
    k_pad, v_pad = jnp.pad(qkv[:, nc:2 * nc], pad), jnp.pad(qkv[:, 2 * nc:], pad)
    o_c = _band_fwd(qkv, k_pad, v_pad, bias_w)
    h3 = _mm(o_c, W["od_w_out"], res=h2, name="out_od")
    h4, u3, a1, b1 = ffn_fwd(h3, 1)

    loss, dh, dhb, G["g_final"] = _loss_head(h4, W["g_final"], target)

    def ffn_bwd(dh, dhb, h, u, a, b, layer):
        du, g_gate, g_up, g_down = _ffn_bwd(dhb, u, a, b, W[f"w_gate_t{layer}"], W[f"w_up_t{layer}"],
                                            W[f"w_down{layer}"], name=f"ffn_bwd{layer}")
        tok = put_grads(f"ffn{layer}", {"w_gate_t": g_gate, "w_up_t": g_up, "w_down": g_down})
        return _rms_bwd(h, W["g_ffn"][layer:layer + 1] + tok[:1, :1], du, dres=dh, name=f"rms_ffn_bwd{layer}")

    dh3, dh3b, g_gffn1 = ffn_bwd(dh, dhb, h3, u3, a1, b1, 1)

    do_c = _mm(dh3b, W["od_w_out"], dims="nt", name="out_od_dx")
    g_od_out = _mm(o_c, dh3b, dims="tn", out_dtype=BF16, name="out_od_dw")
    dq_c, dk_p, dv_p, dbias_w = _band_bwd(qkv, k_pad, v_pad, bias_w, do_c)
    dqkv = jnp.concatenate([dq_c, dk_p[PAD_KEYS:], dv_p[PAD_KEYS:]], axis=1)
    du2 = _mm(dqkv, W["od_w_qkv_t"], name="proj_qkv_dx")
    tok = put_grads("mix1", {"od_w_qkv_t": _mm(dqkv, u2, dims="tn", out_dtype=BF16, name="proj_qkv_dw"),
                             "od_w_out": g_od_out})
    ddiag = _toeplitz_bwd(dbias_w)
    n_far = BAND_W - REL_CLIP + 1
    G["od_rel_bias"] = jnp.concatenate(
        [jnp.zeros((C_HEADS, REL_CLIP - BAND_TQ + 1), F32), ddiag[:, n_far:][:, ::-1],
         jnp.sum(ddiag[:, :n_far], axis=1, keepdims=True)], axis=1)
    dh2, dh2b, g_gmix1 = _rms_bwd(h2, W["g_mix"][1:2] + tok[:1, :1], du2, dres=dh3, name="rms_mix_bwd1")

    dh1, dh1b, g_gffn0 = ffn_bwd(dh2, dh2b, h1, u1, a0, b0, 0)
    G["g_ffn"] = jnp.concatenate([g_gffn0, g_gffn1], axis=0)

    do_ab = _mm(dh1b, W["ev_w_out"], dims="nt", name="out_ev_dx")
    g0 = {"ev_w_out": _mm(o_ab, dh1b, dims="tn", out_dtype=BF16, name="out_ev_dw")}
    dqa_raw, dkn, dva, dkr = _mla_bwd(qa_raw, cos_q, sin_q, kv, kr, o_a, lse, do_ab, 0)
    g0["w_uq_t"] = _mm(dqa_raw, nq, dims="tn", name="proj_uq_dw")
    dnq = _mm(dqa_raw, W["w_uq_t"], name="proj_uq_dx")
    dc_q, _, G["g_cq"] = _rms_bwd(c_q, W["g_cq"], dnq, name="rms_cq_bwd")
    dkv = jnp.concatenate([dkn, dva], axis=1)
    g0["w_ukv_t"] = _mm(dkv, nkv, dims="tn", name="proj_ukv_dw")
    dnkv = _mm(dkv, W["w_ukv_t"], name="proj_ukv_dx")
    dc_kv, _, G["g_ckv"] = _rms_bwd(c_kv, W["g_ckv"], dnkv, name="rms_ckv_bwd")
    tok = put_grads("mix0", g0)
    dqb, dkb, dvb = _sb_bwd(proj, COL_SB, o_b32, do_ab, MLA_HEADS // 2, tok)
    dkr_raw = _rope(dkr, cos_k, -sin_k, 0, 1, F32, name="rope_k_bwd")
    dproj = jnp.concatenate([dc_q, dc_kv, dkr_raw, dqb, dkb, dvb], axis=1)
    du0 = _mm(dproj, W["w_in_t"], name="proj_in_dx")
    tok = put_grads("in0", {"w_in_t": _mm(dproj, u0, dims="tn", name="proj_in_dw")})
    dx, _, g_gmix0 = _rms_bwd(x, W["g_mix"][0:1] + tok[:1, :1], du0, dres=dh1, name="rms_mix_bwd0")
    G["g_mix"] = jnp.concatenate([g_gmix0, g_gmix1], axis=0)
    return loss[0, 0], dx, G


_BIG = ["ev_w_in", "ev_w_uq", "ev_w_ukv", "ev_w_out", "od_w_qkv", "od_w_out", "w_gate", "w_up", "w_down"]
_COL_SHARDED = {"ev_w_in", "ev_w_uq", "ev_w_ukv", "od_w_qkv", "w_gate", "w_up"}
_SMALL = ["ev_g_cq", "ev_g_ckv", "od_rel_bias", "g_mix", "g_ffn", "g_final"]
_GROUPS = {
    "in0": ["ev_w_in"],
    "mix0": ["ev_w_uq", "ev_w_ukv", "ev_w_out"],
    "ffn0": ["w_gate0", "w_up0", "w_down0"],
    "mix1": ["od_w_qkv", "od_w_out"],
    "ffn1": ["w_gate1", "w_up1", "w_down1"],
}
_GROUP_SRC = {n + str(l): (n, l) for n in ("w_gate", "w_up", "w_down") for l in (0, 1)}
_BATCHES = {"in0": ["in0"], "layer0": ["mix0", "ffn0"], "layer1": ["mix1", "ffn1"]}
_BATCH_OF = {grp: batch for batch, grps in _BATCHES.items() for grp in grps}
_SMALL_ROWS = 8
_SMALL_COLS = 1792


def _pack_small(vals):
    flat = jnp.concatenate([v.reshape(-1).astype(F32) for v in vals])
    flat = jnp.pad(flat, (0, _SMALL_ROWS * _SMALL_COLS - flat.shape[0]))
    return flat.reshape(_SMALL_ROWS, _SMALL_COLS)


def _unpack_small(packed, like):
    flat = packed.reshape(-1)
    out, off = [], 0
    for v in like:
        out.append(flat[off:off + v.size].reshape(v.shape))
        off += v.size
    return out


def kernel(x, ev_w_in, ev_g_cq, ev_w_uq, ev_g_ckv, ev_w_ukv, ev_w_out, od_w_qkv, od_rel_bias, od_w_out, g_mix, g_ffn, w_gate, w_up, w_down, g_final, loss_target, m_ev_w_in, m_ev_g_cq, m_ev_w_uq, m_ev_g_ckv, m_ev_w_ukv, m_ev_w_out, m_od_w_qkv, m_od_rel_bias, m_od_w_out, m_g_mix, m_g_ffn, m_w_gate, m_w_up, m_w_down, m_g_final, v_ev_w_in, v_ev_g_cq, v_ev_w_uq, v_ev_g_ckv, v_ev_w_ukv, v_ev_w_out, v_od_w_qkv, v_od_rel_bias, v_od_w_out, v_g_mix, v_g_ffn, v_w_gate, v_w_up, v_w_down, v_g_final):
    args = dict(locals())
    w = {n: args[n] for n in _BIG + _SMALL}
    mom = {n: args["m_" + n] for n in _BIG + _SMALL}
    var = {n: args["v_" + n] for n in _BIG + _SMALL}

    own = {}
    for grp, names in _GROUPS.items():
        for n in names:
            base, layer = _GROUP_SRC.get(n, (n, 0))
            shard = w[base][layer:layer + 1]
            own[n] = (jnp.swapaxes(shard, 1, 2) if base in _COL_SHARDED else shard).astype(BF16)
    gather, token = {}, x[0, :8, :LANES]
    for grp, names in _GROUPS.items():
        gather[grp], token = _exchange_start([own[n] for n in names], [False] * len(names), token,
                                             name="gather_start_" + grp)

    def get_weights(grp, after):
        names = _GROUPS[grp]
        lands = _exchange_wait(gather[grp], token if after is None else after, name="gather_wait_" + grp)
        full = {n: l.reshape(-1, l.shape[-1]) for n, l in zip(names, lands)}
        if grp == "in0":
            return {"w_in_t": _w_in_local(full["ev_w_in"])}
        if grp == "mix0":
            return {"w_uq_t": _w_uq_local(full["ev_w_uq"]), "w_ukv_t": _w_ukv_local(full["ev_w_ukv"]),
                    "ev_w_out": full["ev_w_out"]}
        if grp == "mix1":
            return {"od_w_qkv_t": full["od_w_qkv"], "od_w_out": full["od_w_out"]}
        layer = grp[-1]
        return {"w_gate_t" + layer: full["w_gate" + layer], "w_up_t" + layer: full["w_up" + layer],
                "w_down" + layer: full["w_down" + layer]}

    scatter, pending = {}, {}

    def put_grads(grp, g):
        if grp == "in0":
            g = {"ev_w_in": _w_in_grad(g["w_in_t"])}
        elif grp == "mix0":
            g = {"ev_w_uq": _w_uq_grad(g["w_uq_t"]), "ev_w_ukv": _w_ukv_grad(g["w_ukv_t"]),
                 "ev_w_out": g["ev_w_out"]}
        elif grp == "mix1":
            g = {"od_w_qkv": g["od_w_qkv_t"], "od_w_out": g["od_w_out"]}
        else:
            layer = grp[-1]
            g = {"w_gate" + layer: g["w_gate_t"], "w_up" + layer: g["w_up_t"], "w_down" + layer: g["w_down"]}
        pending.update({n: v.reshape(N_DEV, 1, v.shape[0] // N_DEV, v.shape[1]).astype(BF16) for n, v in g.items()})
        batch = _BATCH_OF[grp]
        names = [n for gr in _BATCHES[batch] for n in _GROUPS[gr]]
        if not all(n in pending for n in names):
            return jnp.zeros((8, LANES), F32)
        send = [pending[n] for n in names]
        scatter[batch], tok = _exchange_start(send, [True] * len(names), send[0], name="scatter_start_" + batch)
        return tok

    small = {"g_cq": ev_g_cq, "g_ckv": ev_g_ckv, "od_rel_bias": od_rel_bias[0],
             "g_mix": g_mix + token[0, 0], "g_ffn": g_ffn, "g_final": g_final.reshape(1, -1)}
    loss_part, dx, G = _local_step(x[0], loss_target[0], small, get_weights, put_grads)
    loss = lax.psum(loss_part, ("x", "y", "c"))
    g_small = _pack_small([G["g_cq"], G["g_ckv"], G["od_rel_bias"], G["g_mix"], G["g_ffn"], G["g_final"]])
    small_handle, _ = _exchange_start([g_small], [False], dx, name="gather_start_small")

    grads, deltas, new_m, new_v = {}, {}, {}, {}
    parts, after = {}, dx

    def wait_parts(batch, after):
        lands = _exchange_wait(scatter[batch], after, name="scatter_wait_" + batch)
        parts.update(zip([n for grp in _BATCHES[batch] for n in _GROUPS[grp]], lands))
        return lands[0]

    def adamw(n):
        col = n in _COL_SHARDED
        rows = lambda a: (jnp.swapaxes(a, 1, 2) if col else a).reshape(-1, a.shape[1 if col else 2])
        res = _adamw(rows(w[n]), parts[n].reshape(N_DEV, -1, parts[n].shape[-1]), rows(mom[n]), rows(var[n]),
                     name="adamw_" + n)
        L, a1, a2 = w[n].shape
        back = lambda r: jnp.swapaxes(r.reshape(L, a2, a1), 1, 2) if col else r.reshape(L, a1, a2)
        grads[n], deltas[n], new_m[n], new_v[n] = [back(r) for r in res]
        return res[0]

    for batch in ("layer1", "layer0"):
        after = wait_parts(batch, after)
    for n in ("w_gate", "w_up", "w_down"):
        parts[n] = jnp.concatenate([parts[n + "0"], parts[n + "1"]], axis=1)
    for n in _BIG[1:]:
        after = adamw(n)
    after = wait_parts("in0", after)
    after = adamw("ev_w_in")
    small_w = [w[n] for n in _SMALL]
    small_parts = _exchange_wait(small_handle, after, name="gather_wait_small")[0]
    res = _adamw(_pack_small(small_w), small_parts, _pack_small([mom[n] for n in _SMALL]),
                 _pack_small([var[n] for n in _SMALL]), name="adamw_small")
    for d, packed in zip((grads, deltas, new_m, new_v), res):
        for n, val in zip(_SMALL, _unpack_small(packed, small_w)):
            d[n] = val

    order = ["ev_w_in", "ev_g_cq", "ev_w_uq", "ev_g_ckv", "ev_w_ukv", "ev_w_out", "od_w_qkv", "od_rel_bias",
             "od_w_out", "g_mix", "g_ffn", "w_gate", "w_up", "w_down", "g_final"]
    out = [loss, dx[None]]
    for d in (grads, deltas, new_m, new_v):
        out += [d[n] for n in order]
    return tuple(out)
```

```python
import functools

import numpy as np
import jax
import jax.numpy as jnp
from jax import lax
from jax.experimental import pallas as pl
from jax.experimental.pallas import tpu as pltpu

F32 = jnp.float32
BF16 = jnp.bfloat16

D_MODEL = 1024
CHUNK = 64
MLA_HEADS = 8
MLA_NOPE = 64
MLA_ROPE = 32
MLA_V = 64
Q_LORA = 384
KV_LORA = 256
ROPE_THETA = 10000.0
SB_HEADS = 8
SB_DIM = 64
C_HEADS = 16
C_DIM = 64
LEFT_CHUNKS = 8
REL_CLIP = 256
D_FF = 2816
RMS_EPS = 1e-6
ADAM_LR = 0.001
ADAM_B1 = 0.9
ADAM_B2 = 0.999
ADAM_EPS = 1e-08
ADAM_WD = 0.01
ADAM_STEP = 10

N_DEV = 8
LANES = 128
VMEM_LIMIT = 56 * 1024 * 1024
NEG = -1e30
PAD_KEYS = LEFT_CHUNKS * CHUNK
BAND_TQ = 128
BAND_W = BAND_TQ + PAD_KEYS
TOEP_W = BAND_W + BAND_TQ

NN = (((1,), (0,)), ((), ()))
NT = (((1,), (1,)), ((), ()))
TN = (((0,), (0,)), ((), ()))


def _dot(a, b, dn):
    return lax.dot_general(a, b, dn, preferred_element_type=F32)


def _pick(dim, pref):
    if dim <= pref:
        return dim
    best = None
    for t in range(LANES, pref + 1, LANES):
        if dim % t == 0:
            best = t
    assert best is not None, (dim, pref)
    return best


def _params(sem):
    return pltpu.CompilerParams(dimension_semantics=sem, vmem_limit_bytes=VMEM_LIMIT)


def _mm(a, b, dims="nn", res=None, out_dtype=F32, name="mm"):
    if dims == "nn":
        (M, K), (K2, N) = a.shape, b.shape
    elif dims == "nt":
        (M, K), (N, K2) = a.shape, b.shape
    else:
        (K, M), (K2, N) = a.shape, b.shape
    assert K == K2, (a.shape, b.shape, dims)
    tm, tn, tk = _pick(M, 512), _pick(N, 1408), _pick(K, 1408)
    nk = K // tk
    dn = {"nn": NN, "nt": NT, "tn": TN}[dims]
    has_res = res is not None

    def body(*refs):
        if has_res:
            a_ref, b_ref, r_ref, o_ref, acc = refs
        else:
            a_ref, b_ref, o_ref, acc = refs
        k = pl.program_id(2)

        @pl.when(k == 0)
        def _():
            acc[...] = jnp.zeros_like(acc)

        acc[...] += _dot(a_ref[...].astype(BF16), b_ref[...].astype(BF16), dn)

        @pl.when(k == nk - 1)
        def _():
            r = acc[...]
            if has_res:
                r = r + r_ref[...]
            o_ref[...] = r.astype(out_dtype)

    a_spec = (pl.BlockSpec((tk, tm), lambda i, j, k: (k, i)) if dims == "tn"
              else pl.BlockSpec((tm, tk), lambda i, j, k: (i, k)))
    b_spec = (pl.BlockSpec((tn, tk), lambda i, j, k: (j, k)) if dims == "nt"
              else pl.BlockSpec((tk, tn), lambda i, j, k: (k, j)))
    o_spec = pl.BlockSpec((tm, tn), lambda i, j, k: (i, j))
    in_specs = [a_spec, b_spec] + ([o_spec] if has_res else [])
    args = (a, b) + ((res,) if has_res else ())
    return pl.pallas_call(
        body, name=name, grid=(M // tm, N // tn, nk),
        in_specs=in_specs, out_specs=o_spec,
        out_shape=jax.ShapeDtypeStruct((M, N), out_dtype),
        scratch_shapes=[pltpu.VMEM((tm, tn), F32)],
        compiler_params=_params(("parallel", "parallel", "arbitrary")),
    )(*args)


def _rms_fwd(x, g, out_dtype=BF16, name="rms_fwd"):
    T, Fd = x.shape
    tm = _pick(T, 256)

    def body(x_ref, g_ref, o_ref):
        xv = x_ref[...]
        r = lax.rsqrt(jnp.mean(xv * xv, axis=-1, keepdims=True) + RMS_EPS)
        o_ref[...] = (xv * r * g_ref[...]).astype(out_dtype)

    return pl.pallas_call(
        body, name=name, grid=(T // tm,),
        in_specs=[pl.BlockSpec((tm, Fd), lambda i: (i, 0)), pl.BlockSpec((1, Fd), lambda i: (0, 0))],
        out_specs=pl.BlockSpec((tm, Fd), lambda i: (i, 0)),
        out_shape=jax.ShapeDtypeStruct((T, Fd), out_dtype),
        compiler_params=_params(("parallel",)),
    )(x, g)


def _rms_bwd(x, g, dy, dres=None, name="rms_bwd"):
    T, Fd = x.shape
    tm = _pick(T, 256)
    has_res = dres is not None

    def body(*refs):
        if has_res:
            x_ref, g_ref, dy_ref, r_ref, dx_ref, dxb_ref, dg_ref = refs
        else:
            x_ref, g_ref, dy_ref, dx_ref, dxb_ref, dg_ref = refs
        xv, dyv = x_ref[...], dy_ref[...]
        r = lax.rsqrt(jnp.mean(xv * xv, axis=-1, keepdims=True) + RMS_EPS)
        gdy = dyv * g_ref[...]
        dot = jnp.mean(xv * gdy, axis=-1, keepdims=True)
        dx = r * gdy - xv * (r * r * r * dot)
        if has_res:
            dx = dx + r_ref[...]
        dx_ref[...] = dx
        dxb_ref[...] = dx.astype(BF16)

        @pl.when(pl.program_id(0) == 0)
        def _():
            dg_ref[...] = jnp.zeros_like(dg_ref)

        dg_ref[...] += jnp.sum(dyv * xv * r, axis=0, keepdims=True)

    row = pl.BlockSpec((tm, Fd), lambda i: (i, 0))
    vec = pl.BlockSpec((1, Fd), lambda i: (0, 0))
    in_specs = [row, vec, row] + ([row] if has_res else [])
    args = (x, g, dy) + ((dres,) if has_res else ())
    return pl.pallas_call(
        body, name=name, grid=(T // tm,),
        in_specs=in_specs, out_specs=[row, row, vec],
        out_shape=[jax.ShapeDtypeStruct((T, Fd), F32), jax.ShapeDtypeStruct((T, Fd), BF16),
                   jax.ShapeDtypeStruct((1, Fd), F32)],
        compiler_params=_params(("arbitrary",)),
    )(*args)


def _loss_head(h, g, target, name="loss_head"):
    T, Fd = h.shape
    tm = _pick(T, 256)

    def body(h_ref, g_ref, t_ref, loss_ref, dh_ref, dhb_ref, dg_ref):
        xv = h_ref[...]
        r = lax.rsqrt(jnp.mean(xv * xv, axis=-1, keepdims=True) + RMS_EPS)
        diff = xv * r * g_ref[...] - t_ref[...]
        part = 0.5 * jnp.sum(jnp.mean(diff * diff, axis=-1, keepdims=True), axis=0, keepdims=True)
        dyv = diff * (1.0 / Fd)
        gdy = dyv * g_ref[...]
        dot = jnp.mean(xv * gdy, axis=-1, keepdims=True)
        dh = r * gdy - xv * (r * r * r * dot)
        dh_ref[...] = dh
        dhb_ref[...] = dh.astype(BF16)

        @pl.when(pl.program_id(0) == 0)
        def _():
            dg_ref[...] = jnp.zeros_like(dg_ref)
            loss_ref[...] = jnp.zeros_like(loss_ref)

        dg_ref[...] += jnp.sum(dyv * xv * r, axis=0, keepdims=True)
        loss_ref[...] += jnp.broadcast_to(part, loss_ref.shape)

    row = pl.BlockSpec((tm, Fd), lambda i: (i, 0))
    vec = pl.BlockSpec((1, Fd), lambda i: (0, 0))
    return pl.pallas_call(
        body, name=name, grid=(T // tm,),
        in_specs=[row, vec, row],
        out_specs=[pl.BlockSpec((1, LANES), lambda i: (0, 0)), row, row, vec],
        out_shape=[jax.ShapeDtypeStruct((1, LANES), F32), jax.ShapeDtypeStruct((T, Fd), F32),
                   jax.ShapeDtypeStruct((T, Fd), BF16), jax.ShapeDtypeStruct((1, Fd), F32)],
        compiler_params=_params(("arbitrary",)),
    )(h, g, target)


FFN_TF = 256


def _ffn_fwd(h, g, wg_t, wu_t, wd, name="ffn_fwd"):
    T, Dm = h.shape
    Fh = wd.shape[0]
    tm = _pick(T, 1024)
    nf = Fh // FFN_TF

    def body(h_ref, g_ref, wg_ref, wu_ref, wd_ref, o_ref, u_ref, a_ref, b_ref):
        j = pl.program_id(1)

        @pl.when(j == 0)
        def _():
            xv = h_ref[...]
            r = lax.rsqrt(jnp.mean(xv * xv, axis=-1, keepdims=True) + RMS_EPS)
            u_ref[...] = (xv * r * g_ref[...]).astype(BF16)
            o_ref[...] = xv

        u = u_ref[...]
        a = _dot(u, wg_ref[...], NT).astype(BF16)
        b = _dot(u, wu_ref[...], NT).astype(BF16)
        a_ref[...] = a
        b_ref[...] = b
        af = a.astype(F32)
        s = (af * jax.nn.sigmoid(af) * b.astype(F32)).astype(BF16)
        o_ref[...] += _dot(s, wd_ref[...], NN)

    row = pl.BlockSpec((tm, Dm), lambda i, j: (i, 0))
    wblk = pl.BlockSpec((FFN_TF, Dm), lambda i, j: (j, 0))
    ablk = pl.BlockSpec((tm, FFN_TF), lambda i, j: (i, j))
    return pl.pallas_call(
        body, name=name, grid=(T // tm, nf),
        in_specs=[row, pl.BlockSpec((1, Dm), lambda i, j: (0, 0)), wblk, wblk, wblk],
        out_specs=[row, row, ablk, ablk],
        out_shape=[jax.ShapeDtypeStruct((T, Dm), F32), jax.ShapeDtypeStruct((T, Dm), BF16),
                   jax.ShapeDtypeStruct((T, Fh), BF16), jax.ShapeDtypeStruct((T, Fh), BF16)],
        compiler_params=_params(("parallel", "arbitrary")),
    )(h, g, wg_t, wu_t, wd)


def _ffn_bwd(dh, u, a, b, wg_t, wu_t, wd, name="ffn_bwd"):
    T, Dm = dh.shape
    Fh = wd.shape[0]
    nf = Fh // FFN_TF
    once = pl.Buffered(1)

    def body(dh_ref, u_ref, a_ref, b_ref, wg_ref, wu_ref, wd_ref, du_ref, dwg_ref, dwu_ref, dwd_ref):
        j = pl.program_id(0)

        @pl.when(j == 0)
        def _():
            du_ref[...] = jnp.zeros_like(du_ref)

        ds = _dot(dh_ref[...], wd_ref[...], NT)
        af, bf = a_ref[...].astype(F32), b_ref[...].astype(F32)
        sig = jax.nn.sigmoid(af)
        sa = af * sig
        dwd_ref[...] = _dot((sa * bf).astype(BF16), dh_ref[...], TN).astype(BF16)
        dab = jnp.concatenate([(ds * bf * (sig * (1.0 + af * (1.0 - sig)))).astype(BF16),
                               (ds * sa).astype(BF16)], axis=1)
        dw = _dot(dab, u_ref[...], TN)
        dwg_ref[...] = dw[:FFN_TF].astype(BF16)
        dwu_ref[...] = dw[FFN_TF:].astype(BF16)
        du_ref[...] += _dot(dab, jnp.concatenate([wg_ref[...], wu_ref[...]], axis=0), NN)

    full = lambda: pl.BlockSpec((T, Dm), lambda j: (0, 0), pipeline_mode=once)
    wblk = pl.BlockSpec((FFN_TF, Dm), lambda j: (j, 0))
    ablk = pl.BlockSpec((T, FFN_TF), lambda j: (0, j))
    return pl.pallas_call(
        body, name=name, grid=(nf,),
        in_specs=[full(), full(), ablk, ablk, wblk, wblk, wblk],
        out_specs=[pl.BlockSpec((T, Dm), lambda j: (0, 0)), wblk, wblk, wblk],
        out_shape=[jax.ShapeDtypeStruct((T, Dm), F32)] + [jax.ShapeDtypeStruct((Fh, Dm), BF16)] * 3,
        compiler_params=_params(("arbitrary",)),
    )(dh, u, a, b, wg_t, wu_t, wd)


def _rope(x, cos_t, sin_t, col0, ncols, out_dtype, name="rope"):
    T = x.shape[0]
    wt = cos_t.shape[1]
    tm = _pick(T, 256)
    nb = ncols * LANES // wt
    half = MLA_ROPE // 2

    def body(x_ref, c_ref, s_ref, o_ref):
        xv = x_ref[...].astype(F32)
        lane = lax.broadcasted_iota(jnp.int32, xv.shape, 1)
        first = (lane & (MLA_ROPE - 1)) < half
        swapped = jnp.where(first, pltpu.roll(xv, wt - half, 1), pltpu.roll(xv, half, 1))
        o_ref[...] = (xv * c_ref[...] + swapped * s_ref[...]).astype(out_dtype)

    off = col0 * LANES // wt
    return pl.pallas_call(
        body, name=name, grid=(T // tm, nb),
        in_specs=[pl.BlockSpec((tm, wt), lambda i, j: (i, j + off)),
                  pl.BlockSpec((tm, wt), lambda i, j: (i, 0)),
                  pl.BlockSpec((tm, wt), lambda i, j: (i, 0))],
        out_specs=pl.BlockSpec((tm, wt), lambda i, j: (i, j)),
        out_shape=jax.ShapeDtypeStruct((T, ncols * LANES), out_dtype),
        compiler_params=_params(("parallel", "parallel")),
    )(x, cos_t, sin_t)


ATT_T = 256


def _mla_masks(shape):
    lane = lax.broadcasted_iota(jnp.int32, shape, 1)
    m0 = (lane < 64) | ((lane >= 128) & (lane < 160))
    m1 = ((lane >= 64) & (lane < 128)) | ((lane >= 160) & (lane < 192))
    return m0, m1


def _by_twos(n, step, carry):
    carry = lax.fori_loop(0, n // 2, lambda i, c: step(2 * i + 1, step(2 * i, c)), carry)
    return lax.fori_loop(0, n % 2, lambda _, c: step(n - 1, c), carry)


def _chunk_ok(tq, tk):
    row = lax.broadcasted_iota(jnp.int32, (tq, tk), 0)
    col = lax.broadcasted_iota(jnp.int32, (tq, tk), 1)
    return (col >> 6) <= (row >> 6)


def _rotate(x, cos_t, sin_t):
    half = MLA_ROPE // 2
    lane = lax.broadcasted_iota(jnp.int32, x.shape, 1)
    first = (lane & (MLA_ROPE - 1)) < half
    swapped = jnp.where(first, pltpu.roll(x, x.shape[1] - half, 1), pltpu.roll(x, half, 1))
    return x * cos_t + swapped * sin_t


def _mla_fwd(q, cos_q, sin_q, kv, kr, name="mla_fwd"):
    T = q.shape[0]
    tq = tk = _pick(T, ATT_T)
    npair = MLA_HEADS // 2
    scale = (MLA_NOPE + MLA_ROPE) ** -0.5

    def body(q_ref, c_ref, s_ref, kn_ref, v_ref, kr_ref, o_ref, lse_ref):
        m_idx = pl.program_id(1)
        qv = _rotate(q_ref[...], c_ref[...], s_ref[...]).astype(BF16)
        m0, m1 = _mla_masks(qv.shape)
        qs = jnp.concatenate([jnp.where(m0, qv, 0), jnp.where(m1, qv, 0)], axis=0).astype(BF16)
        diag = jnp.concatenate([_chunk_ok(tq, tk)] * 2, axis=0)

        def block(kb, carry, ok):
            ks = pl.ds(pl.multiple_of(kb * tk, tk), tk)
            kcat = jnp.concatenate([kn_ref[ks, :], kr_ref[ks, :]], axis=1)
            mx, l, acc = carry
            s = _dot(qs, kcat, NT) * scale
            if ok is not None:
                s = jnp.where(ok, s, NEG)
            mn = jnp.maximum(mx, jnp.max(s, axis=-1, keepdims=True))
            alpha = jnp.exp(mx - mn)
            p = jnp.exp(s - mn)
            return (mn, alpha * l + jnp.sum(p, axis=-1, keepdims=True),
                    alpha * acc + _dot(p.astype(BF16), v_ref[ks, :], NN))

        init = (jnp.full((2 * tq, 1), NEG, F32), jnp.zeros((2 * tq, 1), F32), jnp.zeros((2 * tq, LANES), F32))
        mx, l, acc = block(m_idx, init, diag)
        mx, l, acc = _by_twos(m_idx, lambda kb, c: block(kb, c, None), (mx, l, acc))
        h0 = lax.broadcasted_iota(jnp.int32, (tq, LANES), 1) < 64
        o_ref[...] = _two_heads(acc * (1.0 / l), h0).astype(o_ref.dtype)
        lse_ref[...] = _two_heads(jnp.broadcast_to(mx + jnp.log(l), (2 * tq, LANES)), h0)

    full = lambda col: pl.BlockSpec((T, LANES), col)
    table = pl.BlockSpec((tq, 2 * LANES), lambda p, m: (m, 0))
    return pl.pallas_call(
        body, name=name, grid=(npair, T // tq),
        in_specs=[pl.BlockSpec((tq, 2 * LANES), lambda p, m: (m, p)), table, table,
                  full(lambda p, m: (0, p)), full(lambda p, m: (0, npair + p)), full(lambda p, m: (0, 0))],
        out_specs=[pl.BlockSpec((tq, LANES), lambda p, m: (m, p)),
                   pl.BlockSpec((tq, LANES), lambda p, m: (m, p))],
        out_shape=[jax.ShapeDtypeStruct((T, npair * LANES), BF16),
                   jax.ShapeDtypeStruct((T, npair * LANES), F32)],
        compiler_params=_params(("parallel", "arbitrary")),
    )(q, cos_q, sin_q, kv, kv, kr)


def _mla_bwd(q, cos_q, sin_q, kv, kr, o, lse, do, do_col0, name="mla_bwd"):
    T = q.shape[0]
    tq = tk = _pick(T, ATT_T)
    npair = MLA_HEADS // 2
    scale = (MLA_NOPE + MLA_ROPE) ** -0.5

    def body(q_ref, c_ref, s_ref, kn_ref, v_ref, kr_ref, o_ref, lse_ref, do_ref, dq_ref, dkn_ref, dv_ref, dkr_ref,
             dkn_acc, dv_acc):
        p_idx, m_idx = pl.program_id(0), pl.program_id(1)

        @pl.when(m_idx == 0)
        def _():
            dkn_acc[...] = jnp.zeros_like(dkn_acc)
            dv_acc[...] = jnp.zeros_like(dv_acc)

        @pl.when((m_idx == 0) & (p_idx == 0))
        def _():
            dkr_ref[...] = jnp.zeros_like(dkr_ref)

        qv = _rotate(q_ref[...], c_ref[...], s_ref[...]).astype(BF16)
        m0, m1 = _mla_masks(qv.shape)
        qs = jnp.concatenate([jnp.where(m0, qv, 0), jnp.where(m1, qv, 0)], axis=0).astype(BF16)
        dov = do_ref[...].astype(F32)
        h0 = lax.broadcasted_iota(jnp.int32, (tq, LANES), 1) < 64
        dos32 = jnp.concatenate([jnp.where(h0, dov, 0.0), jnp.where(h0, 0.0, dov)], axis=0)
        ov = o_ref[...].astype(F32)
        delta = jnp.sum(dos32 * jnp.concatenate([ov, ov], axis=0), axis=-1, keepdims=True)
        dos = dos32.astype(BF16)
        lsev = lse_ref[...]
        lse = jnp.concatenate([lsev[:, 0:1], lsev[:, 64:65]], axis=0)
        diag = jnp.concatenate([_chunk_ok(tq, tk)] * 2, axis=0)

        def block(kb, dq, ok):
            ks = pl.ds(pl.multiple_of(kb * tk, tk), tk)
            kcat = jnp.concatenate([kn_ref[ks, :], kr_ref[ks, :]], axis=1)
            vv = v_ref[ks, :]
            p = jnp.exp(_dot(qs, kcat, NT) * scale - lse)
            if ok is not None:
                p = jnp.where(ok, p, 0.0)
            ds = (p * (_dot(dos, vv, NT) - delta) * scale).astype(BF16)
            dkc = _dot(ds, qs, TN)
            dkn_acc[ks, :] += dkc[:, :LANES]
            dkr_ref[ks, :] += dkc[:, LANES:]
            dv_acc[ks, :] += _dot(p.astype(BF16), dos, TN)
            return dq + _dot(ds, kcat, NN)

        dq = block(m_idx, jnp.zeros((2 * tq, 2 * LANES), F32), diag)
        dq = _by_twos(m_idx, lambda kb, c: block(kb, c, None), dq)
        dq_ref[...] = _rotate(jnp.where(m0, dq[:tq], jnp.where(m1, dq[tq:], 0.0)), c_ref[...],
                              -s_ref[...]).astype(BF16)

        @pl.when(m_idx == T // tq - 1)
        def _():
            dkn_ref[...] = dkn_acc[...].astype(BF16)
            dv_ref[...] = dv_acc[...].astype(BF16)

    full = lambda col: pl.BlockSpec((T, LANES), col)
    blk = lambda col: pl.BlockSpec((tq, LANES), col)
    table = pl.BlockSpec((tq, 2 * LANES), lambda p, m: (m, 0))
    return pl.pallas_call(
        body, name=name, grid=(npair, T // tq),
        in_specs=[pl.BlockSpec((tq, 2 * LANES), lambda p, m: (m, p)), table, table,
                  full(lambda p, m: (0, p)), full(lambda p, m: (0, npair + p)), full(lambda p, m: (0, 0)),
                  blk(lambda p, m: (m, p)), blk(lambda p, m: (m, p)),
                  blk(lambda p, m: (m, do_col0 + p))],
        out_specs=[pl.BlockSpec((tq, 2 * LANES), lambda p, m: (m, p)),
                   full(lambda p, m: (0, p)), full(lambda p, m: (0, p)), full(lambda p, m: (0, 0))],
        out_shape=[jax.ShapeDtypeStruct((T, npair * 2 * LANES), BF16),
                   jax.ShapeDtypeStruct((T, npair * LANES), BF16),
                   jax.ShapeDtypeStruct((T, npair * LANES), BF16),
                   jax.ShapeDtypeStruct((T, LANES), F32)],
        scratch_shapes=[pltpu.VMEM((T, LANES), F32)] * 2,
        compiler_params=_params(("arbitrary", "arbitrary")),
    )(q, cos_q, sin_q, kv, kv, kr, o, lse, do)


def _split_dot(x, tri):
    hi = x.astype(BF16)
    lo = (x - hi.astype(F32)).astype(BF16)
    both = _dot(jnp.concatenate([hi, lo], axis=0), tri, NN)
    return both[:x.shape[0]] + both[x.shape[0]:]


def _sb_terms(qh, kk, before):
    z = _dot(qh, kk, NT)
    sp = jnp.maximum(z, 0.0) + jnp.log(1.0 + jnp.exp(-jnp.abs(z)))
    lk = -sp if before is None else jnp.where(before, -sp, 0.0)
    return z, sp, lk


def _sb_setup(q_ref, tq, tk, scale):
    qv = (q_ref[...].astype(F32) * scale).astype(BF16)
    lane = lax.broadcasted_iota(jnp.int32, (tq, LANES), 1)
    h0 = lane < 64
    qs = jnp.concatenate([jnp.where(h0, qv, 0), jnp.where(h0, 0, qv)], axis=0).astype(BF16)
    row = lax.broadcasted_iota(jnp.int32, (tk, tk), 0)
    col = lax.broadcasted_iota(jnp.int32, (tk, tk), 1)
    return qs, h0, row, col


def _two_heads(x, h0):
    tq = x.shape[0] // 2
    return jnp.where(h0, x[:tq], x[tq:])


def _sb_fwd(qkv, col0, name="sb_fwd"):
    T = qkv.shape[0]
    tq = tk = _pick(T, ATT_T)
    npair = SB_HEADS // 2
    scale = SB_DIM ** -0.5

    def body(q_ref, k_ref, v_ref, o_ref, o32_ref):
        m_idx = pl.program_id(1)
        qs, h0, row, col = _sb_setup(q_ref, tq, tk, scale)
        later = (row > col).astype(BF16)
        diag = jnp.concatenate([col < row] * 2, axis=0)

        def block(kb, carry, before):
            ks = pl.ds(pl.multiple_of(kb * tk, tk), tk)
            c, acc = carry
            z, sp, lk = _sb_terms(qs, k_ref[ks, :].astype(BF16), before)
            w = jnp.exp((z - sp) + _split_dot(lk, later) + c)
            if before is not None:
                w = jnp.where(before, w, 0.0)
            return (c + jnp.sum(lk, axis=-1, keepdims=True),
                    acc + _dot(w.astype(BF16), v_ref[ks, :].astype(BF16), NN))

        init = (jnp.zeros((2 * tq, 1), F32), jnp.zeros((2 * tq, LANES), F32))
        res = block(m_idx, init, diag)
        res = _by_twos(m_idx, lambda i, c: block(m_idx - 1 - i, c, None), res)
        o = _two_heads(res[1], h0)
        o_ref[...] = o.astype(o_ref.dtype)
        o32_ref[...] = o

    full = lambda col: pl.BlockSpec((T, LANES), col)
    blk = pl.BlockSpec((tq, LANES), lambda p, m: (m, p))
    return pl.pallas_call(
        body, name=name, grid=(npair, T // tq),
        in_specs=[pl.BlockSpec((tq, LANES), lambda p, m: (m, col0 + p)),
                  full(lambda p, m: (0, col0 + npair + p)), full(lambda p, m: (0, col0 + 2 * npair + p))],
        out_specs=[blk, blk],
        out_shape=[jax.ShapeDtypeStruct((T, npair * LANES), BF16), jax.ShapeDtypeStruct((T, npair * LANES), F32)],
        compiler_params=_params(("parallel", "arbitrary")),
    )(qkv, qkv, qkv)


def _sb_bwd(qkv, col0, o32, do, do_col0, dep, name="sb_bwd"):
    T = qkv.shape[0]
    tq = tk = _pick(T, ATT_T)
    npair = SB_HEADS // 2
    scale = SB_DIM ** -0.5

    def body(q_ref, k_ref, v_ref, o_ref, do_ref, dep_ref, dq_ref, dk_ref, dv_ref, dk_acc, dv_acc):
        m_idx = pl.program_id(1)

        @pl.when(m_idx == 0)
        def _():
            dk_acc[...] = jnp.zeros_like(dk_acc)
            dv_acc[...] = jnp.zeros_like(dv_acc)

        qs, h0, row, col = _sb_setup(q_ref, tq, tk, scale)
        dov = do_ref[...].astype(F32)
        dos = jnp.concatenate([jnp.where(h0, dov, 0.0), jnp.where(h0, 0.0, dov)], axis=0).astype(BF16)
        ov = o_ref[...]
        etot = jnp.sum(dos.astype(F32) * jnp.concatenate([ov, ov], axis=0), axis=-1, keepdims=True)
        later = (row > col).astype(BF16)
        from_here = (row >= col).astype(BF16)
        diag = jnp.concatenate([col < row] * 2, axis=0)

        def block(kb, carry, before):
            ks = pl.ds(pl.multiple_of(kb * tk, tk), tk)
            kk = k_ref[ks, :].astype(BF16)
            vv = v_ref[ks, :].astype(BF16)
            c, es, dqa = carry
            z, sp, lk = _sb_terms(qs, kk, before)
            w = jnp.exp((z - sp) + _split_dot(lk, later) + c)
            if before is not None:
                w = jnp.where(before, w, 0.0)
            wb = w.astype(BF16)
            e = wb.astype(F32) * _dot(dos, vv, NT)
            prev = etot - (_split_dot(e, from_here) + es)
            sig_neg = jnp.exp(-sp)
            dz = e * sig_neg - (1.0 - sig_neg) * prev
            if before is not None:
                dz = jnp.where(before, dz, 0.0)
            dzb = dz.astype(BF16)
            dk_acc[ks, :] += _dot(dzb, qs, TN)
            dv_acc[ks, :] += _dot(wb, dos, TN)
            return (c + jnp.sum(lk, axis=-1, keepdims=True), es + jnp.sum(e, axis=-1, keepdims=True),
                    dqa + _dot(dzb, kk, NN))

        init = (jnp.zeros((2 * tq, 1), F32), jnp.zeros((2 * tq, 1), F32), jnp.zeros((2 * tq, LANES), F32))
        res = block(m_idx, init, diag)
        res = _by_twos(m_idx, lambda i, c: block(m_idx - 1 - i, c, None), res)
        dq_ref[...] = (_two_heads(res[2], h0) * scale).astype(BF16)

        @pl.when(m_idx == T // tq - 1)
        def _():
            dk_ref[...] = dk_acc[...].astype(BF16)
            dv_ref[...] = dv_acc[...].astype(BF16)

    full = lambda col: pl.BlockSpec((T, LANES), col)
    blk = lambda col: pl.BlockSpec((tq, LANES), col)
    return pl.pallas_call(
        body, name=name, grid=(npair, T // tq),
        in_specs=[blk(lambda p, m: (m, col0 + p)),
                  full(lambda p, m: (0, col0 + npair + p)), full(lambda p, m: (0, col0 + 2 * npair + p)),
                  blk(lambda p, m: (m, p)), blk(lambda p, m: (m, do_col0 + p)),
                  pl.BlockSpec((8, LANES), lambda p, m: (0, 0))],
        out_specs=[blk(lambda p, m: (m, p)), full(lambda p, m: (0, p)), full(lambda p, m: (0, p))],
        out_shape=[jax.ShapeDtypeStruct((T, npair * LANES), BF16)] * 3,
        scratch_shapes=[pltpu.VMEM((T, LANES), F32)] * 2,
        compiler_params=_params(("arbitrary", "arbitrary")),
    )(qkv, qkv, qkv, o32, do, dep)


def _band_in_window():
    cq = lax.broadcasted_iota(jnp.int32, (BAND_TQ, BAND_W), 0) >> 6
    ckp = lax.broadcasted_iota(jnp.int32, (BAND_TQ, BAND_W), 1) >> 6
    return (ckp >= cq) & (ckp <= cq + LEFT_CHUNKS)


def _band_real(m_idx):
    j = lax.broadcasted_iota(jnp.int32, (BAND_TQ, BAND_W), 1)
    return j >= PAD_KEYS - m_idx * BAND_TQ


def _band_probs(qh, kw, bias, real, scale):
    s = jnp.where(real, _dot(qh, kw, NT) * scale + bias, NEG)
    e = jnp.exp(s - jnp.max(s, axis=-1, keepdims=True))
    return e * (1.0 / jnp.sum(e, axis=-1, keepdims=True))


BAND_SUB = 2


def _band_fwd(qkv, k_pad, v_pad, bias_w, name="band_fwd"):
    T = qkv.shape[0]
    npair = C_HEADS // 2
    scale = C_DIM ** -0.5
    rows = BAND_SUB * BAND_TQ

    def body(q_ref, k_ref, v_ref, b_ref, o_ref):
        lane = lax.broadcasted_iota(jnp.int32, (BAND_TQ, LANES), 1)
        h0 = lane < 64
        bias = jnp.concatenate([b_ref[0], b_ref[1]], axis=0)
        for sub in range(BAND_SUB):
            m_idx = pl.program_id(1) * BAND_SUB + sub
            win = pl.ds(pl.multiple_of(m_idx * BAND_TQ, BAND_TQ), BAND_W)
            kw, vw = k_ref[win, :], v_ref[win, :]
            qv = q_ref[sub * BAND_TQ:(sub + 1) * BAND_TQ, :]
            qs = jnp.concatenate([jnp.where(h0, qv, 0), jnp.where(h0, 0, qv)], axis=0).astype(BF16)
            p = _band_probs(qs, kw, bias, jnp.concatenate([_band_real(m_idx)] * 2, axis=0), scale)
            o = _two_heads(_dot(p.astype(BF16), vw, NN), h0)
            o_ref[sub * BAND_TQ:(sub + 1) * BAND_TQ, :] = o.astype(o_ref.dtype)

    Tp = T + PAD_KEYS
    return pl.pallas_call(
        body, name=name, grid=(npair, T // rows),
        in_specs=[pl.BlockSpec((rows, LANES), lambda p, m: (m, p)),
                  pl.BlockSpec((Tp, LANES), lambda p, m: (0, p)),
                  pl.BlockSpec((Tp, LANES), lambda p, m: (0, p)),
                  pl.BlockSpec((2, BAND_TQ, BAND_W), lambda p, m: (p, 0, 0))],
        out_specs=pl.BlockSpec((rows, LANES), lambda p, m: (m, p)),
        out_shape=jax.ShapeDtypeStruct((T, npair * LANES), BF16),
        compiler_params=_params(("parallel", "arbitrary")),
    )(qkv, k_pad, v_pad, bias_w)


def _band_bwd(qkv, k_pad, v_pad, bias_w, do, name="band_bwd"):
    T = qkv.shape[0]
    npair = C_HEADS // 2
    scale = C_DIM ** -0.5

    rows = BAND_SUB * BAND_TQ

    def body(q_ref, k_ref, v_ref, b_ref, do_ref, dq_ref, dk_ref, dv_ref, db_ref, dk_acc, dv_acc):
        @pl.when(pl.program_id(1) == 0)
        def _():
            dk_acc[...] = jnp.zeros_like(dk_acc)
            dv_acc[...] = jnp.zeros_like(dv_acc)
            db_ref[...] = jnp.zeros_like(db_ref)

        lane = lax.broadcasted_iota(jnp.int32, (BAND_TQ, LANES), 1)
        h0 = lane < 64
        dbs = jnp.zeros((2 * BAND_TQ, BAND_W), F32)
        bias = jnp.concatenate([b_ref[0], b_ref[1]], axis=0)
        for sub in range(BAND_SUB):
            m_idx = pl.program_id(1) * BAND_SUB + sub
            win = pl.ds(pl.multiple_of(m_idx * BAND_TQ, BAND_TQ), BAND_W)
            kw, vw = k_ref[win, :], v_ref[win, :]
            qv = q_ref[sub * BAND_TQ:(sub + 1) * BAND_TQ, :]
            dov = do_ref[sub * BAND_TQ:(sub + 1) * BAND_TQ, :].astype(F32)
            qs = jnp.concatenate([jnp.where(h0, qv, 0), jnp.where(h0, 0, qv)], axis=0).astype(BF16)
            dos = jnp.concatenate([jnp.where(h0, dov, 0.0), jnp.where(h0, 0.0, dov)], axis=0).astype(BF16)
            p = _band_probs(qs, kw, bias, jnp.concatenate([_band_real(m_idx)] * 2, axis=0), scale)
            dp = _dot(dos, vw, NT)
            dsb = p * (dp - jnp.sum(p * dp, axis=-1, keepdims=True))
            dbs = dbs + dsb
            dsq = (dsb * scale).astype(BF16)
            dq_ref[sub * BAND_TQ:(sub + 1) * BAND_TQ, :] = _two_heads(_dot(dsq, kw, NN), h0).astype(BF16)
            dk_acc[win, :] += _dot(dsq, qs, TN)
            dv_acc[win, :] += _dot(p.astype(BF16), dos, TN)
        db_ref[0] += dbs[:BAND_TQ]
        db_ref[1] += dbs[BAND_TQ:]

        @pl.when(pl.program_id(1) == T // rows - 1)
        def _():
            dk_ref[...] = dk_acc[...].astype(BF16)
            dv_ref[...] = dv_acc[...].astype(BF16)

    Tp = T + PAD_KEYS
    blk = lambda col: pl.BlockSpec((rows, LANES), col)
    full = pl.BlockSpec((Tp, LANES), lambda p, m: (0, p))
    bias = pl.BlockSpec((2, BAND_TQ, BAND_W), lambda p, m: (p, 0, 0))
    return pl.pallas_call(
        body, name=name, grid=(npair, T // rows),
        in_specs=[blk(lambda p, m: (m, p)), full, full, bias, blk(lambda p, m: (m, p))],
        out_specs=[blk(lambda p, m: (m, p)), full, full, bias],
        out_shape=[jax.ShapeDtypeStruct((T, npair * LANES), BF16),
                   jax.ShapeDtypeStruct((Tp, npair * LANES), BF16),
                   jax.ShapeDtypeStruct((Tp, npair * LANES), BF16),
                   jax.ShapeDtypeStruct((C_HEADS, BAND_TQ, BAND_W), F32)],
        scratch_shapes=[pltpu.VMEM((Tp, LANES), F32)] * 2,
        compiler_params=_params(("arbitrary", "arbitrary")),
    )(qkv, k_pad, v_pad, bias_w, do)


def _skew_bits(x, left):
    w = x.shape[1]
    row = lax.broadcasted_iota(jnp.int32, x.shape, 0)
    for b in range(BAND_TQ.bit_length() - 1):
        amt = (w - (1 << b)) if left else (1 << b)
        x = jnp.where((row >> b) & 1 == 1, pltpu.roll(x, amt, 1), x)
    return x


def _toeplitz(diag, name="toeplitz"):
    H = diag.shape[0]

    def body(d_ref, o_ref):
        x = jnp.broadcast_to(d_ref[0], (BAND_TQ, TOEP_W))
        o_ref[0] = jnp.where(_band_in_window(), _skew_bits(x, left=False)[:, BAND_TQ:], NEG)

    return pl.pallas_call(
        body, name=name, grid=(H,),
        in_specs=[pl.BlockSpec((1, 1, TOEP_W), lambda h: (h, 0, 0))],
        out_specs=pl.BlockSpec((1, BAND_TQ, BAND_W), lambda h: (h, 0, 0)),
        out_shape=jax.ShapeDtypeStruct((H, BAND_TQ, BAND_W), F32),
        compiler_params=_params(("parallel",)),
    )(diag.reshape(H, 1, TOEP_W))


def _toeplitz_bwd(dbias, name="toeplitz_bwd"):
    H = dbias.shape[0]

    def body(d_ref, o_ref):
        x = jnp.concatenate([jnp.zeros((BAND_TQ, BAND_TQ), F32), d_ref[0]], axis=1)
        o_ref[0] = jnp.sum(_skew_bits(x, left=True), axis=0, keepdims=True)

    return pl.pallas_call(
        body, name=name, grid=(H,),
        in_specs=[pl.BlockSpec((1, BAND_TQ, BAND_W), lambda h: (h, 0, 0))],
        out_specs=pl.BlockSpec((1, 1, TOEP_W), lambda h: (h, 0, 0)),
        out_shape=jax.ShapeDtypeStruct((H, 1, TOEP_W), F32),
        compiler_params=_params(("parallel",)),
    )(dbias).reshape(H, TOEP_W)


_HBM = pl.BlockSpec(memory_space=pltpu.HBM)
_SEM = pl.BlockSpec(memory_space=pltpu.SEMAPHORE)
_EFFECT = pltpu.SideEffectType.DATAFLOW_SIDE_EFFECTING


def _peers():
    x, y, c = lax.axis_index("x"), lax.axis_index("y"), lax.axis_index("c")
    out = []
    for k in range(1, N_DEV):
        peer = (1 - x if (k >> 2) & 1 else x, 1 - y if (k >> 1) & 1 else y, 1 - c if k & 1 else c)
        out.append((peer, 4 * peer[0] + 2 * peer[1] + peer[2]))
    return 4 * x + 2 * y + c, out


def _split_copies(ins, lands, scatter, send_sem, recv_sem, arriving):
    me, peers = _peers()
    out = []
    for a in range(len(ins)):
        for peer, idx in peers:
            out.append(pltpu.make_async_remote_copy(
                src_ref=ins[a].at[idx] if scatter[a] else ins[a],
                dst_ref=lands[a].at[idx if arriving else me], send_sem=send_sem, recv_sem=recv_sem,
                device_id=peer, device_id_type=pl.DeviceIdType.MESH))
    return out


def _landing_zones(arrays, scatter):
    return [lax.empty((N_DEV,) + (a.shape[1:] if s else a.shape), a.dtype) for a, s in zip(arrays, scatter)]


def _place_own(arrays, scatter, name):
    n = len(arrays)
    lands = _landing_zones(arrays, scatter)
    me = (4 * lax.axis_index("x") + 2 * lax.axis_index("y") + lax.axis_index("c")).astype(jnp.int32).reshape(1)

    def body(me_ref, *refs):
        for a in range(n):
            refs[2 * n + a][...] = refs[a][...].reshape(refs[2 * n + a].shape)

    def row_spec(shape):
        zeros = (0,) * (len(shape) - 1)
        return pl.BlockSpec((1,) + tuple(shape[1:]), lambda i, me_ref: (me_ref[0],) + zeros)

    in_specs = [row_spec(a.shape) if s else pl.BlockSpec(a.shape, lambda i, me_ref, nd=a.ndim: (0,) * nd)
                for a, s in zip(arrays, scatter)]
    return pl.pallas_call(
        body, name=name,
        out_shape=[jax.ShapeDtypeStruct(l.shape, l.dtype) for l in lands],
        grid_spec=pltpu.PrefetchScalarGridSpec(
            num_scalar_prefetch=1, grid=(1,),
            in_specs=in_specs + [pl.BlockSpec(memory_space=pl.ANY)] * n,
            out_specs=[row_spec(l.shape) for l in lands]),
        input_output_aliases={1 + n + i: i for i in range(n)},
        compiler_params=_params(("arbitrary",)),
    )(me, *arrays, *lands)


def _exchange_start(arrays, scatter, after, name):
    n = len(arrays)
    lands = list(_place_own(arrays, scatter, name=name.replace("_start_", "_own_")))

    def body(*refs):
        ins, lnd = refs[:n], refs[n:2 * n]
        send_sem, recv_sem = refs[2 * n + 1:2 * n + 3]
        token = refs[-1]
        for cp in _split_copies(ins, lnd, scatter, send_sem, recv_sem, arriving=False):
            cp.start()
        token[...] = jnp.zeros_like(token)

    hbm = lambda a: pltpu.HBM(a.shape, a.dtype)
    out = pl.pallas_call(
        body, name=name,
        out_shape=(pltpu.SemaphoreType.DMA(()), pltpu.SemaphoreType.DMA(()),
                   *[hbm(a) for a in arrays], *[hbm(a) for a in lands],
                   jax.ShapeDtypeStruct((8, LANES), F32)),
        in_specs=[_HBM] * (2 * n) + [pl.BlockSpec(memory_space=pl.ANY)],
        out_specs=(_SEM, _SEM, *([_HBM] * (2 * n)), pl.BlockSpec(memory_space=pltpu.VMEM)),
        input_output_aliases={i: 2 + i for i in range(2 * n)},
        compiler_params=pltpu.CompilerParams(has_side_effects=_EFFECT),
    )(*[pltpu.with_memory_space_constraint(a, pltpu.HBM) for a in list(arrays) + lands], after)
    return (out[0], out[1], list(out[2:2 + n]), list(out[2 + n:2 + 2 * n]), tuple(scatter)), out[-1]


def _exchange_wait(handle, after, name):
    send_sem, recv_sem, ins, lands, scatter = handle
    n = len(ins)
    after = after if isinstance(after, tuple) else (after,)

    def body(*refs):
        i_ref, l_ref = refs[:n], refs[n:2 * n]
        s_sem, r_sem = refs[2 * n:2 * n + 2]
        for cp in _split_copies(i_ref, l_ref, scatter, s_sem, r_sem, arriving=False):
            cp.wait_send()
        for cp in _split_copies(i_ref, l_ref, scatter, s_sem, r_sem, arriving=True):
            cp.wait_recv()

    hbm = lambda a: pltpu.HBM(a.shape, a.dtype)
    out = pl.pallas_call(
        body, name=name,
        out_shape=tuple(hbm(a) for a in ins + lands),
        in_specs=[_HBM] * (2 * n) + [_SEM, _SEM] + [pl.BlockSpec(memory_space=pl.ANY)] * len(after),
        out_specs=tuple([_HBM] * (2 * n)),
        input_output_aliases={i: i for i in range(2 * n)},
        compiler_params=pltpu.CompilerParams(has_side_effects=_EFFECT),
    )(*ins, *lands, send_sem, recv_sem, *after)
    return list(out[n:])


def _adamw(w, parts, m, v, name="adamw"):
    R, C = w.shape
    L = len(parts)
    rl = R // L
    tr = max([t for t in range(16, 513, 16) if rl % t == 0], default=rl)
    nb = rl // tr
    c1 = 1.0 - ADAM_B1 ** ADAM_STEP
    c2 = 1.0 - ADAM_B2 ** ADAM_STEP

    def body(*refs):
        w_ref, p_refs, (m_ref, v_ref, g_ref, d_ref, nm_ref, nv_ref) = refs[0], refs[1:1 + L], refs[1 + L:]
        g = None
        for j, p_ref in enumerate(p_refs):
            gj = p_ref[0].astype(F32)
            for i in range(1, N_DEV):
                gj = gj + p_ref[i].astype(F32)
            g = gj if g is None else jnp.where(pl.program_id(0) == j, gj, g)
        nm = ADAM_B1 * m_ref[...] + (1.0 - ADAM_B1) * g
        nv = ADAM_B2 * v_ref[...] + (1.0 - ADAM_B2) * (g * g)
        g_ref[...] = g
        nm_ref[...] = nm
        nv_ref[...] = nv
        d_ref[...] = -ADAM_LR * ((nm / c1) / (jnp.sqrt(nv / c2) + ADAM_EPS) + ADAM_WD * w_ref[...])

    blk = pl.BlockSpec((tr, C), lambda l, i: (l * nb + i, 0))
    part = lambda j: pl.BlockSpec((N_DEV, tr, C), lambda l, i: (0, jnp.where(l == j, i, 0), 0))
    return pl.pallas_call(
        body, name=name, grid=(L, nb),
        in_specs=[blk] + [part(j) for j in range(L)] + [blk, blk],
        out_specs=[blk] * 4,
        out_shape=[jax.ShapeDtypeStruct((R, C), F32)] * 4,
        compiler_params=_params(("arbitrary", "arbitrary")),
    )(w, *parts, m, v)


_O1 = Q_LORA
_O2 = _O1 + KV_LORA
_O3 = _O2 + MLA_ROPE
_NB = SB_HEADS * SB_DIM
IN_W = _O2 + LANES + 3 * _NB
COL_KR = _O2 // LANES
COL_SB = COL_KR + 1


def _w_in_local(w):
    kr = w[_O2:_O3]
    pad = jnp.zeros((LANES - 2 * MLA_ROPE, w.shape[1]), w.dtype)
    return jnp.concatenate([w[:_O2], kr, kr, pad, w[_O3:]], axis=0)


def _w_in_grad(g):
    kr = (g[_O2:_O2 + MLA_ROPE].astype(F32) + g[_O2 + MLA_ROPE:_O2 + 2 * MLA_ROPE].astype(F32)).astype(g.dtype)
    return jnp.concatenate([g[:_O2], kr, g[_O2 + LANES:]], axis=0)


def _w_uq_local(w):
    w3 = w.reshape(MLA_HEADS // 2, 2, MLA_NOPE + MLA_ROPE, w.shape[1])
    nope = w3[:, :, :MLA_NOPE].reshape(MLA_HEADS // 2, 2 * MLA_NOPE, w.shape[1])
    rope = w3[:, :, MLA_NOPE:].reshape(MLA_HEADS // 2, 2 * MLA_ROPE, w.shape[1])
    pad = jnp.zeros((MLA_HEADS // 2, LANES - 2 * MLA_ROPE, w.shape[1]), w.dtype)
    return jnp.concatenate([nope, rope, pad], axis=1).reshape(-1, w.shape[1])


def _w_uq_grad(g):
    g3 = g.reshape(MLA_HEADS // 2, 2 * LANES, g.shape[1])
    nope = g3[:, :2 * MLA_NOPE].reshape(MLA_HEADS // 2, 2, MLA_NOPE, g.shape[1])
    rope = g3[:, LANES:LANES + 2 * MLA_ROPE].reshape(MLA_HEADS // 2, 2, MLA_ROPE, g.shape[1])
    return jnp.concatenate([nope, rope], axis=2).reshape(-1, g.shape[1])


def _w_ukv_local(w):
    w3 = w.reshape(MLA_HEADS, MLA_NOPE + MLA_V, w.shape[1])
    return jnp.concatenate([w3[:, :MLA_NOPE].reshape(-1, w.shape[1]),
                            w3[:, MLA_NOPE:].reshape(-1, w.shape[1])], axis=0)


def _w_ukv_grad(g):
    half = MLA_HEADS * MLA_NOPE
    kn = g[:half].reshape(MLA_HEADS, MLA_NOPE, g.shape[1])
    vv = g[half:].reshape(MLA_HEADS, MLA_V, g.shape[1])
    return jnp.concatenate([kn, vv], axis=1).reshape(-1, g.shape[1])


def _rope_tables(T):
    pos = jnp.arange(T, dtype=F32)
    inv_freq = ROPE_THETA ** (-jnp.arange(0, MLA_ROPE, 2, dtype=F32) / MLA_ROPE)
    ang = pos[:, None] * inv_freq[None, :]
    cos, sin = jnp.cos(ang), jnp.sin(ang)
    ones = jnp.ones((T, LANES - 2 * MLA_ROPE), F32)
    cos_k = jnp.concatenate([cos, cos, cos, cos, ones], axis=1)
    sin_k = jnp.concatenate([-sin, sin, -sin, sin, 0.0 * ones], axis=1)
    cos_q = jnp.concatenate([jnp.ones((T, LANES), F32), cos_k], axis=1)
    sin_q = jnp.concatenate([jnp.zeros((T, LANES), F32), sin_k], axis=1)
    return cos_q, sin_q, cos_k, sin_k


def _bias_diag_index():
    ell = np.arange(TOEP_W)
    return np.clip(BAND_W - ell, -REL_CLIP, REL_CLIP) + REL_CLIP


def _local_step(x, target, small, get_weights, put_grads):
    T = x.shape[0]
    cos_q, sin_q, cos_k, sin_k = _rope_tables(T)
    G = {}
    W = dict(small)

    u0 = _rms_fwd(x, W["g_mix"][0:1], name="rms_mix0")
    bias_w = _toeplitz(W["od_rel_bias"][:, _bias_diag_index()])
    W.update(get_weights("in0", (u0, bias_w)))
    proj = _mm(u0, W["w_in_t"], dims="nt", name="proj_in")
    W.update(get_weights("mix0", proj))
    c_q, c_kv = proj[:, :_O1], proj[:, _O1:_O2]
    nq = _rms_fwd(c_q, W["g_cq"], name="rms_cq")
    nkv = _rms_fwd(c_kv, W["g_ckv"], name="rms_ckv")
    qa_raw = _mm(nq, W["w_uq_t"], dims="nt", name="proj_uq")
    kv = _mm(nkv, W["w_ukv_t"], dims="nt", out_dtype=BF16, name="proj_ukv")
    kr = _rope(proj, cos_k, sin_k, COL_KR, 1, BF16, name="rope_k")
    o_a, lse = _mla_fwd(qa_raw, cos_q, sin_q, kv, kr)
    o_b, o_b32 = _sb_fwd(proj, COL_SB)
    o_ab = jnp.concatenate([o_a, o_b], axis=1)
    h1 = _mm(o_ab, W["ev_w_out"], res=x, name="out_ev")

    def ffn_fwd(h, layer):
        W.update(get_weights(f"ffn{layer}", h))
        return _ffn_fwd(h, W["g_ffn"][layer:layer + 1], W[f"w_gate_t{layer}"], W[f"w_up_t{layer}"],
                        W[f"w_down{layer}"], name=f"ffn_fwd{layer}")

    h2, u1, a0, b0 = ffn_fwd(h1, 0)

    W.update(get_weights("mix1", h2))
    u2 = _rms_fwd(h2, W["g_mix"][1:2], name="rms_mix1")
    qkv = _mm(u2, W["od_w_qkv_t"], dims="nt", out_dtype=BF16, name="proj_qkv")
    nc = C_HEADS * C_DIM
    pad = ((PAD_KEYS, 0), (0, 0))
    k_pad, v_pad = jnp.pad(qkv[:, nc:2 * nc], pad), jnp.pad(qkv[:, 2 * nc:], pad)
    o_c = _band_fwd(qkv, k_pad, v_pad, bias_w)
    h3 = _mm(o_c, W["od_w_out"], res=h2, name="out_od")
    h4, u3, a1, b1 = ffn_fwd(h3, 1)

    loss, dh, dhb, G["g_final"] = _loss_head(h4, W["g_final"], target)

    def ffn_bwd(dh, dhb, h, u, a, b, layer):
        du, g_gate, g_up, g_down = _ffn_bwd(dhb, u, a, b, W[f"w_gate_t{layer}"], W[f"w_up_t{layer}"],
                                            W[f"w_down{layer}"], name=f"ffn_bwd{layer}")
        tok = put_grads(f"ffn{layer}", {"w_gate_t": g_gate, "w_up_t": g_up, "w_down": g_down})
        return _rms_bwd(h, W["g_ffn"][layer:layer + 1] + tok[:1, :1], du, dres=dh, name=f"rms_ffn_bwd{layer}")

    dh3, dh3b, g_gffn1 = ffn_bwd(dh, dhb, h3, u3, a1, b1, 1)

    do_c = _mm(dh3b, W["od_w_out"], dims="nt", name="out_od_dx")
    g_od_out = _mm(o_c, dh3b, dims="tn", out_dtype=BF16, name="out_od_dw")
    dq_c, dk_p, dv_p, dbias_w = _band_bwd(qkv, k_pad, v_pad, bias_w, do_c)
    dqkv = jnp.concatenate([dq_c, dk_p[PAD_KEYS:], dv_p[PAD_KEYS:]], axis=1)
    du2 = _mm(dqkv, W["od_w_qkv_t"], name="proj_qkv_dx")
    tok = put_grads("mix1", {"od_w_qkv_t": _mm(dqkv, u2, dims="tn", out_dtype=BF16, name="proj_qkv_dw"),
                             "od_w_out": g_od_out})
    ddiag = _toeplitz_bwd(dbias_w)
    n_far = BAND_W - REL_CLIP + 1
    G["od_rel_bias"] = jnp.concatenate(
        [jnp.zeros((C_HEADS, REL_CLIP - BAND_TQ + 1), F32), ddiag[:, n_far:][:, ::-1],
         jnp.sum(ddiag[:, :n_far], axis=1, keepdims=True)], axis=1)
    dh2, dh2b, g_gmix1 = _rms_bwd(h2, W["g_mix"][1:2] + tok[:1, :1], du2, dres=dh3, name="rms_mix_bwd1")

    dh1, dh1b, g_gffn0 = ffn_bwd(dh2, dh2b, h1, u1, a0, b0, 0)
    G["g_ffn"] = jnp.concatenate([g_gffn0, g_gffn1], axis=0)

    do_ab = _mm(dh1b, W["ev_w_out"], dims="nt", name="out_ev_dx")
    g0 = {"ev_w_out": _mm(o_ab, dh1b, dims="tn", out_dtype=BF16, name="out_ev_dw")}
    dqa_raw, dkn, dva, dkr = _mla_bwd(qa_raw, cos_q, sin_q, kv, kr, o_a, lse, do_ab, 0)
    g0["w_uq_t"] = _mm(dqa_raw, nq, dims="tn", name="proj_uq_dw")
    dnq = _mm(dqa_raw, W["w_uq_t"], name="proj_uq_dx")
    _, dc_q, G["g_cq"] = _rms_bwd(c_q, W["g_cq"], dnq, name="rms_cq_bwd")
    dkv = jnp.concatenate([dkn, dva], axis=1)
    g0["w_ukv_t"] = _mm(dkv, nkv, dims="tn", name="proj_ukv_dw")
    dnkv = _mm(dkv, W["w_ukv_t"], name="proj_ukv_dx")
    _, dc_kv, G["g_ckv"] = _rms_bwd(c_kv, W["g_ckv"], dnkv, name="rms_ckv_bwd")
    tok = put_grads("mix0", g0)
    dqb, dkb, dvb = _sb_bwd(proj, COL_SB, o_b32, do_ab, MLA_HEADS // 2, tok)
    dkr_raw = _rope(dkr, cos_k, -sin_k, 0, 1, BF16, name="rope_k_bwd")
    dproj = jnp.concatenate([dc_q, dc_kv, dkr_raw, dqb, dkb, dvb], axis=1)
    du0 = _mm(dproj, W["w_in_t"], name="proj_in_dx")
    tok = put_grads("in0", {"w_in_t": _mm(dproj, u0, dims="tn", name="proj_in_dw")})
    dx, _, g_gmix0 = _rms_bwd(x, W["g_mix"][0:1] + tok[:1, :1], du0, dres=dh1, name="rms_mix_bwd0")
    G["g_mix"] = jnp.concatenate([g_gmix0, g_gmix1], axis=0)
    return loss[0, 0], dx, G


_BIG = ["ev_w_in", "ev_w_uq", "ev_w_ukv", "ev_w_out", "od_w_qkv", "od_w_out", "w_gate", "w_up", "w_down"]
_COL_SHARDED = {"ev_w_in", "ev_w_uq", "ev_w_ukv", "od_w_qkv", "w_gate", "w_up"}
_SMALL = ["ev_g_cq", "ev_g_ckv", "od_rel_bias", "g_mix", "g_ffn", "g_final"]
_GROUPS = {
    "in0": ["ev_w_in"],
    "mix0": ["ev_w_uq", "ev_w_ukv", "ev_w_out"],
    "ffn0": ["w_gate0", "w_up0", "w_down0"],
    "mix1": ["od_w_qkv", "od_w_out"],
    "ffn1": ["w_gate1", "w_up1", "w_down1"],
}
_GROUP_SRC = {n + str(l): (n, l) for n in ("w_gate", "w_up", "w_down") for l in (0, 1)}
_BATCHES = {"in0": ["in0"], "layer0": ["mix0", "ffn0"], "layer1": ["mix1", "ffn1"]}
_BATCH_OF = {grp: batch for batch, grps in _BATCHES.items() for grp in grps}
_SMALL_ROWS = 8
_SMALL_COLS = 1792


def _pack_small(vals):
    flat = jnp.concatenate([v.reshape(-1).astype(F32) for v in vals])
    flat = jnp.pad(flat, (0, _SMALL_ROWS * _SMALL_COLS - flat.shape[0]))
    return flat.reshape(_SMALL_ROWS, _SMALL_COLS)


def _unpack_small(packed, like):
    flat = packed.reshape(-1)
    out, off = [], 0
    for v in like:
        out.append(flat[off:off + v.size].reshape(v.shape))
        off += v.size
    return out


def kernel(x, ev_w_in, ev_g_cq, ev_w_uq, ev_g_ckv, ev_w_ukv, ev_w_out, od_w_qkv, od_rel_bias, od_w_out, g_mix, g_ffn, w_gate, w_up, w_down, g_final, loss_target, m_ev_w_in, m_ev_g_cq, m_ev_w_uq, m_ev_g_ckv, m_ev_w_ukv, m_ev_w_out, m_od_w_qkv, m_od_rel_bias, m_od_w_out, m_g_mix, m_g_ffn, m_w_gate, m_w_up, m_w_down, m_g_final, v_ev_w_in, v_ev_g_cq, v_ev_w_uq, v_ev_g_ckv, v_ev_w_ukv, v_ev_w_out, v_od_w_qkv, v_od_rel_bias, v_od_w_out, v_g_mix, v_g_ffn, v_w_gate, v_w_up, v_w_down, v_g_final):
    args = dict(locals())
    w = {n: args[n] for n in _BIG + _SMALL}
    mom = {n: args["m_" + n] for n in _BIG + _SMALL}
    var = {n: args["v_" + n] for n in _BIG + _SMALL}

    own = {}
    for grp, names in _GROUPS.items():
        for n in names:
            base, layer = _GROUP_SRC.get(n, (n, 0))
            shard = w[base][layer:layer + 1]
            own[n] = (jnp.swapaxes(shard, 1, 2) if base in _COL_SHARDED else shard).astype(BF16)
    gather, token = {}, x[0, :8, :LANES]
    for grp, names in _GROUPS.items():
        gather[grp], token = _exchange_start([own[n] for n in names], [False] * len(names), token,
                                             name="gather_start_" + grp)

    def get_weights(grp, after):
        names = _GROUPS[grp]
        lands = _exchange_wait(gather[grp], token if after is None else after, name="gather_wait_" + grp)
        full = {n: l.reshape(-1, l.shape[-1]) for n, l in zip(names, lands)}
        if grp == "in0":
            return {"w_in_t": _w_in_local(full["ev_w_in"])}
        if grp == "mix0":
            return {"w_uq_t": _w_uq_local(full["ev_w_uq"]), "w_ukv_t": _w_ukv_local(full["ev_w_ukv"]),
                    "ev_w_out": full["ev_w_out"]}
        if grp == "mix1":
            return {"od_w_qkv_t": full["od_w_qkv"], "od_w_out": full["od_w_out"]}
        layer = grp[-1]
        return {"w_gate_t" + layer: full["w_gate" + layer], "w_up_t" + layer: full["w_up" + layer],
                "w_down" + layer: full["w_down" + layer]}

    scatter, pending = {}, {}

    def put_grads(grp, g):
        if grp == "in0":
            g = {"ev_w_in": _w_in_grad(g["w_in_t"])}
        elif grp == "mix0":
            g = {"ev_w_uq": _w_uq_grad(g["w_uq_t"]), "ev_w_ukv": _w_ukv_grad(g["w_ukv_t"]),
                 "ev_w_out": g["ev_w_out"]}
        elif grp == "mix1":
            g = {"od_w_qkv": g["od_w_qkv_t"], "od_w_out": g["od_w_out"]}
        else:
            layer = grp[-1]
            g = {"w_gate" + layer: g["w_gate_t"], "w_up" + layer: g["w_up_t"], "w_down" + layer: g["w_down"]}
        pending.update({n: v.reshape(N_DEV, 1, v.shape[0] // N_DEV, v.shape[1]).astype(BF16) for n, v in g.items()})
        batch = _BATCH_OF[grp]
        names = [n for gr in _BATCHES[batch] for n in _GROUPS[gr]]
        if not all(n in pending for n in names):
            return jnp.zeros((8, LANES), F32)
        send = [pending[n] for n in names]
        scatter[batch], tok = _exchange_start(send, [True] * len(names), send[0], name="scatter_start_" + batch)
        return tok

    small = {"g_cq": ev_g_cq, "g_ckv": ev_g_ckv, "od_rel_bias": od_rel_bias[0],
             "g_mix": g_mix + token[0, 0], "g_ffn": g_ffn, "g_final": g_final.reshape(1, -1)}
    loss_part, dx, G = _local_step(x[0], loss_target[0], small, get_weights, put_grads)
    g_small = _pack_small([G["g_cq"], G["g_ckv"], G["od_rel_bias"], G["g_mix"], G["g_ffn"], G["g_final"],
                           loss_part.reshape(1)])
    small_handle, _ = _exchange_start([g_small], [False], dx, name="gather_start_small")

    grads, deltas, new_m, new_v = {}, {}, {}, {}
    parts, after = {}, dx

    def wait_parts(batch, after):
        lands = _exchange_wait(scatter[batch], after, name="scatter_wait_" + batch)
        parts.update(zip([n for grp in _BATCHES[batch] for n in _GROUPS[grp]], lands))
        return lands[0]

    def adamw(n):
        col = n in _COL_SHARDED
        rows = lambda a: (jnp.swapaxes(a, 1, 2) if col else a).reshape(-1, a.shape[1 if col else 2])
        layers = [parts[n]] if n in parts else [parts[n + "0"], parts[n + "1"]]
        res = _adamw(rows(w[n]), [p.reshape(N_DEV, -1, p.shape[-1]) for p in layers], rows(mom[n]), rows(var[n]),
                     name="adamw_" + n)
        L, a1, a2 = w[n].shape
        back = lambda r: jnp.swapaxes(r.reshape(L, a2, a1), 1, 2) if col else r.reshape(L, a1, a2)
        grads[n], deltas[n], new_m[n], new_v[n] = [back(r) for r in res]
        return res[0]

    for batch in ("layer1", "layer0"):
        after = wait_parts(batch, after)
    for n in _BIG[1:]:
        after = adamw(n)
    after = wait_parts("in0", after)
    after = adamw("ev_w_in")
    small_w = [w[n] for n in _SMALL]
    small_parts = _exchange_wait(small_handle, after, name="gather_wait_small")[0]
    loss = jnp.sum(small_parts.reshape(N_DEV, -1)[:, sum(v.size for v in small_w)])
    res = _adamw(_pack_small(small_w), [small_parts], _pack_small([mom[n] for n in _SMALL]),
                 _pack_small([var[n] for n in _SMALL]), name="adamw_small")
    for d, packed in zip((grads, deltas, new_m, new_v), res):
        for n, val in zip(_SMALL, _unpack_small(packed, small_w)):
            d[n] = val

    order = ["ev_w_in", "ev_g_cq", "ev_w_uq", "ev_g_ckv", "ev_w_ukv", "ev_w_out", "od_w_qkv", "od_rel_bias",
             "od_w_out", "g_mix", "g_ffn", "w_gate", "w_up", "w_down", "g_final"]
    out = [loss, dx[None]]
    for d in (grads, deltas, new_m, new_v):
        out += [d[n] for n in order]
    return tuple(out)
```

```python
import functools

import numpy as np
import jax
import jax.numpy as jnp
from jax import lax
from jax.experimental import pallas as pl
from jax.experimental.pallas import tpu as pltpu

F32 = jnp.float32
BF16 = jnp.bfloat16

D_MODEL = 1024
CHUNK = 64
MLA_HEADS = 8
MLA_NOPE = 64
MLA_ROPE = 32
MLA_V = 64
Q_LORA = 384
KV_LORA = 256
ROPE_THETA = 10000.0
SB_HEADS = 8
SB_DIM = 64
C_HEADS = 16
C_DIM = 64
LEFT_CHUNKS = 8
REL_CLIP = 256
D_FF = 2816
RMS_EPS = 1e-6
ADAM_LR = 0.001
ADAM_B1 = 0.9
ADAM_B2 = 0.999
ADAM_EPS = 1e-08
ADAM_WD = 0.01
ADAM_STEP = 10

N_DEV = 8
LANES = 128
VMEM_LIMIT = 56 * 1024 * 1024
NEG = -1e30
PAD_KEYS = LEFT_CHUNKS * CHUNK
BAND_TQ = 128
BAND_W = BAND_TQ + PAD_KEYS
TOEP_W = BAND_W + BAND_TQ

NN = (((1,), (0,)), ((), ()))
NT = (((1,), (1,)), ((), ()))
TN = (((0,), (0,)), ((), ()))


def _dot(a, b, dn):
    return lax.dot_general(a, b, dn, preferred_element_type=F32)


def _pick(dim, pref):
    if dim <= pref:
        return dim
    best = None
    for t in range(LANES, pref + 1, LANES):
        if dim % t == 0:
            best = t
    assert best is not None, (dim, pref)
    return best


def _params(sem):
    return pltpu.CompilerParams(dimension_semantics=sem, vmem_limit_bytes=VMEM_LIMIT)


def _mm(a, b, dims="nn", res=None, out_dtype=F32, name="mm"):
    if dims == "nn":
        (M, K), (K2, N) = a.shape, b.shape
    elif dims == "nt":
        (M, K), (N, K2) = a.shape, b.shape
    else:
        (K, M), (K2, N) = a.shape, b.shape
    assert K == K2, (a.shape, b.shape, dims)
    tm, tn, tk = _pick(M, 512), _pick(N, 1408), _pick(K, 1408)
    nk = K // tk
    dn = {"nn": NN, "nt": NT, "tn": TN}[dims]
    has_res = res is not None

    def body(*refs):
        if has_res:
            a_ref, b_ref, r_ref, o_ref, acc = refs
        else:
            a_ref, b_ref, o_ref, acc = refs
        k = pl.program_id(2)

        @pl.when(k == 0)
        def _():
            acc[...] = jnp.zeros_like(acc)

        acc[...] += _dot(a_ref[...].astype(BF16), b_ref[...].astype(BF16), dn)

        @pl.when(k == nk - 1)
        def _():
            r = acc[...]
            if has_res:
                r = r + r_ref[...]
            o_ref[...] = r.astype(out_dtype)

    a_spec = (pl.BlockSpec((tk, tm), lambda i, j, k: (k, i)) if dims == "tn"
              else pl.BlockSpec((tm, tk), lambda i, j, k: (i, k)))
    b_spec = (pl.BlockSpec((tn, tk), lambda i, j, k: (j, k)) if dims == "nt"
              else pl.BlockSpec((tk, tn), lambda i, j, k: (k, j)))
    o_spec = pl.BlockSpec((tm, tn), lambda i, j, k: (i, j))
    in_specs = [a_spec, b_spec] + ([o_spec] if has_res else [])
    args = (a, b) + ((res,) if has_res else ())
    return pl.pallas_call(
        body, name=name, grid=(M // tm, N // tn, nk),
        in_specs=in_specs, out_specs=o_spec,
        out_shape=jax.ShapeDtypeStruct((M, N), out_dtype),
        scratch_shapes=[pltpu.VMEM((tm, tn), F32)],
        compiler_params=_params(("parallel", "parallel", "arbitrary")),
    )(*args)


def _rms_fwd(x, g, out_dtype=BF16, name="rms_fwd"):
    T, Fd = x.shape
    tm = _pick(T, 256)

    def body(x_ref, g_ref, o_ref):
        xv = x_ref[...]
        r = lax.rsqrt(jnp.mean(xv * xv, axis=-1, keepdims=True) + RMS_EPS)
        o_ref[...] = (xv * r * g_ref[...]).astype(out_dtype)

    return pl.pallas_call(
        body, name=name, grid=(T // tm,),
        in_specs=[pl.BlockSpec((tm, Fd), lambda i: (i, 0)), pl.BlockSpec((1, Fd), lambda i: (0, 0))],
        out_specs=pl.BlockSpec((tm, Fd), lambda i: (i, 0)),
        out_shape=jax.ShapeDtypeStruct((T, Fd), out_dtype),
        compiler_params=_params(("parallel",)),
    )(x, g)


def _rms_bwd(x, g, dy, dres=None, name="rms_bwd"):
    T, Fd = x.shape
    tm = _pick(T, 256)
    has_res = dres is not None

    def body(*refs):
        if has_res:
            x_ref, g_ref, dy_ref, r_ref, dx_ref, dxb_ref, dg_ref = refs
        else:
            x_ref, g_ref, dy_ref, dx_ref, dxb_ref, dg_ref = refs
        xv, dyv = x_ref[...], dy_ref[...]
        r = lax.rsqrt(jnp.mean(xv * xv, axis=-1, keepdims=True) + RMS_EPS)
        gdy = dyv * g_ref[...]
        dot = jnp.mean(xv * gdy, axis=-1, keepdims=True)
        dx = r * gdy - xv * (r * r * r * dot)
        if has_res:
            dx = dx + r_ref[...]
        dx_ref[...] = dx
        dxb_ref[...] = dx.astype(BF16)

        @pl.when(pl.program_id(0) == 0)
        def _():
            dg_ref[...] = jnp.zeros_like(dg_ref)

        dg_ref[...] += jnp.sum(dyv * xv * r, axis=0, keepdims=True)

    row = pl.BlockSpec((tm, Fd), lambda i: (i, 0))
    vec = pl.BlockSpec((1, Fd), lambda i: (0, 0))
    in_specs = [row, vec, row] + ([row] if has_res else [])
    args = (x, g, dy) + ((dres,) if has_res else ())
    return pl.pallas_call(
        body, name=name, grid=(T // tm,),
        in_specs=in_specs, out_specs=[row, row, vec],
        out_shape=[jax.ShapeDtypeStruct((T, Fd), F32), jax.ShapeDtypeStruct((T, Fd), BF16),
                   jax.ShapeDtypeStruct((1, Fd), F32)],
        compiler_params=_params(("arbitrary",)),
    )(*args)


def _loss_head(h, g, target, name="loss_head"):
    T, Fd = h.shape
    tm = _pick(T, 256)

    def body(h_ref, g_ref, t_ref, loss_ref, dh_ref, dhb_ref, dg_ref):
        xv = h_ref[...]
        r = lax.rsqrt(jnp.mean(xv * xv, axis=-1, keepdims=True) + RMS_EPS)
        diff = xv * r * g_ref[...] - t_ref[...]
        part = 0.5 * jnp.sum(jnp.mean(diff * diff, axis=-1, keepdims=True), axis=0, keepdims=True)
        dyv = diff * (1.0 / Fd)
        gdy = dyv * g_ref[...]
        dot = jnp.mean(xv * gdy, axis=-1, keepdims=True)
        dh = r * gdy - xv * (r * r * r * dot)
        dh_ref[...] = dh
        dhb_ref[...] = dh.astype(BF16)

        @pl.when(pl.program_id(0) == 0)
        def _():
            dg_ref[...] = jnp.zeros_like(dg_ref)
            loss_ref[...] = jnp.zeros_like(loss_ref)

        dg_ref[...] += jnp.sum(dyv * xv * r, axis=0, keepdims=True)
        loss_ref[...] += jnp.broadcast_to(part, loss_ref.shape)

    row = pl.BlockSpec((tm, Fd), lambda i: (i, 0))
    vec = pl.BlockSpec((1, Fd), lambda i: (0, 0))
    return pl.pallas_call(
        body, name=name, grid=(T // tm,),
        in_specs=[row, vec, row],
        out_specs=[pl.BlockSpec((1, LANES), lambda i: (0, 0)), row, row, vec],
        out_shape=[jax.ShapeDtypeStruct((1, LANES), F32), jax.ShapeDtypeStruct((T, Fd), F32),
                   jax.ShapeDtypeStruct((T, Fd), BF16), jax.ShapeDtypeStruct((1, Fd), F32)],
        compiler_params=_params(("arbitrary",)),
    )(h, g, target)


FFN_TF = 256


def _ffn_fwd(h, g, wg_t, wu_t, wd, name="ffn_fwd"):
    T, Dm = h.shape
    Fh = wd.shape[0]
    tm = _pick(T, 1024)
    nf = Fh // FFN_TF

    def body(h_ref, g_ref, wg_ref, wu_ref, wd_ref, o_ref, u_ref, a_ref, b_ref):
        j = pl.program_id(1)

        @pl.when(j == 0)
        def _():
            xv = h_ref[...]
            r = lax.rsqrt(jnp.mean(xv * xv, axis=-1, keepdims=True) + RMS_EPS)
            u_ref[...] = (xv * r * g_ref[...]).astype(BF16)
            o_ref[...] = xv

        u = u_ref[...]
        a = _dot(u, wg_ref[...], NT).astype(BF16)
        b = _dot(u, wu_ref[...], NT).astype(BF16)
        a_ref[...] = a
        b_ref[...] = b
        af = a.astype(F32)
        s = (af * jax.nn.sigmoid(af) * b.astype(F32)).astype(BF16)
        o_ref[...] += _dot(s, wd_ref[...], NN)

    row = pl.BlockSpec((tm, Dm), lambda i, j: (i, 0))
    wblk = pl.BlockSpec((FFN_TF, Dm), lambda i, j: (j, 0))
    ablk = pl.BlockSpec((tm, FFN_TF), lambda i, j: (i, j))
    return pl.pallas_call(
        body, name=name, grid=(T // tm, nf),
        in_specs=[row, pl.BlockSpec((1, Dm), lambda i, j: (0, 0)), wblk, wblk, wblk],
        out_specs=[row, row, ablk, ablk],
        out_shape=[jax.ShapeDtypeStruct((T, Dm), F32), jax.ShapeDtypeStruct((T, Dm), BF16),
                   jax.ShapeDtypeStruct((T, Fh), BF16), jax.ShapeDtypeStruct((T, Fh), BF16)],
        compiler_params=_params(("parallel", "arbitrary")),
    )(h, g, wg_t, wu_t, wd)


def _ffn_bwd(dh, u, a, b, wg_t, wu_t, wd, name="ffn_bwd"):
    T, Dm = dh.shape
    Fh = wd.shape[0]
    nf = Fh // FFN_TF
    once = pl.Buffered(1)

    def body(dh_ref, u_ref, a_ref, b_ref, wg_ref, wu_ref, wd_ref, du_ref, dwg_ref, dwu_ref, dwd_ref):
        j = pl.program_id(0)

        @pl.when(j == 0)
        def _():
            du_ref[...] = jnp.zeros_like(du_ref)

        ds = _dot(dh_ref[...], wd_ref[...], NT)
        af, bf = a_ref[...].astype(F32), b_ref[...].astype(F32)
        sig = jax.nn.sigmoid(af)
        sa = af * sig
        dwd_ref[...] = _dot((sa * bf).astype(BF16), dh_ref[...], TN).astype(BF16)
        dab = jnp.concatenate([(ds * bf * (sig * (1.0 + af * (1.0 - sig)))).astype(BF16),
                               (ds * sa).astype(BF16)], axis=1)
        dw = _dot(dab, u_ref[...], TN)
        dwg_ref[...] = dw[:FFN_TF].astype(BF16)
        dwu_ref[...] = dw[FFN_TF:].astype(BF16)
        du_ref[...] += _dot(dab, jnp.concatenate([wg_ref[...], wu_ref[...]], axis=0), NN)

    full = lambda: pl.BlockSpec((T, Dm), lambda j: (0, 0), pipeline_mode=once)
    wblk = pl.BlockSpec((FFN_TF, Dm), lambda j: (j, 0))
    ablk = pl.BlockSpec((T, FFN_TF), lambda j: (0, j))
    return pl.pallas_call(
        body, name=name, grid=(nf,),
        in_specs=[full(), full(), ablk, ablk, wblk, wblk, wblk],
        out_specs=[pl.BlockSpec((T, Dm), lambda j: (0, 0)), wblk, wblk, wblk],
        out_shape=[jax.ShapeDtypeStruct((T, Dm), F32)] + [jax.ShapeDtypeStruct((Fh, Dm), BF16)] * 3,
        compiler_params=_params(("arbitrary",)),
    )(dh, u, a, b, wg_t, wu_t, wd)


def _rope(x, cos_t, sin_t, col0, ncols, out_dtype, name="rope"):
    T = x.shape[0]
    wt = cos_t.shape[1]
    tm = _pick(T, 256)
    nb = ncols * LANES // wt
    half = MLA_ROPE // 2

    def body(x_ref, c_ref, s_ref, o_ref):
        xv = x_ref[...].astype(F32)
        lane = lax.broadcasted_iota(jnp.int32, xv.shape, 1)
        first = (lane & (MLA_ROPE - 1)) < half
        swapped = jnp.where(first, pltpu.roll(xv, wt - half, 1), pltpu.roll(xv, half, 1))
        o_ref[...] = (xv * c_ref[...] + swapped * s_ref[...]).astype(out_dtype)

    off = col0 * LANES // wt
    return pl.pallas_call(
        body, name=name, grid=(T // tm, nb),
        in_specs=[pl.BlockSpec((tm, wt), lambda i, j: (i, j + off)),
                  pl.BlockSpec((tm, wt), lambda i, j: (i, 0)),
                  pl.BlockSpec((tm, wt), lambda i, j: (i, 0))],
        out_specs=pl.BlockSpec((tm, wt), lambda i, j: (i, j)),
        out_shape=jax.ShapeDtypeStruct((T, ncols * LANES), out_dtype),
        compiler_params=_params(("parallel", "parallel")),
    )(x, cos_t, sin_t)


ATT_TQ = 512
ATT_TK = 256


def _mla_masks(shape):
    lane = lax.broadcasted_iota(jnp.int32, shape, 1)
    m0 = (lane < 64) | ((lane >= 128) & (lane < 160))
    m1 = ((lane >= 64) & (lane < 128)) | ((lane >= 160) & (lane < 192))
    return m0, m1


def _by_twos(n, step, carry):
    carry = lax.fori_loop(0, n // 2, lambda i, c: step(2 * i + 1, step(2 * i, c)), carry)
    return lax.fori_loop(0, n % 2, lambda _, c: step(n - 1, c), carry)


def _chunk_ok(tq, tk, d):
    row = lax.broadcasted_iota(jnp.int32, (tq, tk), 0)
    col = lax.broadcasted_iota(jnp.int32, (tq, tk), 1) + d * tk
    return jnp.concatenate([(col >> 6) <= (row >> 6)] * 2, axis=0)


def _rotate(x, cos_t, sin_t):
    half = MLA_ROPE // 2
    lane = lax.broadcasted_iota(jnp.int32, x.shape, 1)
    first = (lane & (MLA_ROPE - 1)) < half
    swapped = jnp.where(first, pltpu.roll(x, x.shape[1] - half, 1), pltpu.roll(x, half, 1))
    return x * cos_t + swapped * sin_t


def _mla_fwd(q, cos_q, sin_q, kv, kr, name="mla_fwd"):
    T = q.shape[0]
    tq, tk = _pick(T, ATT_TQ), _pick(T, ATT_TK)
    nd = tq // tk
    npair = MLA_HEADS // 2
    scale = (MLA_NOPE + MLA_ROPE) ** -0.5

    def body(q_ref, c_ref, s_ref, kn_ref, v_ref, kr_ref, o_ref, lse_ref):
        m_idx = pl.program_id(1)
        qv = _rotate(q_ref[...], c_ref[...], s_ref[...]).astype(BF16)
        m0, m1 = _mla_masks(qv.shape)
        qs = jnp.concatenate([jnp.where(m0, qv, 0), jnp.where(m1, qv, 0)], axis=0).astype(BF16)

        def block(kb, carry, ok):
            ks = pl.ds(pl.multiple_of(kb * tk, tk), tk)
            kcat = jnp.concatenate([kn_ref[ks, :], kr_ref[ks, :]], axis=1)
            mx, l, acc = carry
            s = _dot(qs, kcat, NT) * scale
            if ok is not None:
                s = jnp.where(ok, s, NEG)
            mn = jnp.maximum(mx, jnp.max(s, axis=-1, keepdims=True))
            alpha = jnp.exp(mx - mn)
            p = jnp.exp(s - mn)
            return (mn, alpha * l + jnp.sum(p, axis=-1, keepdims=True),
                    alpha * acc + _dot(p.astype(BF16), v_ref[ks, :], NN))

        init = (jnp.full((2 * tq, 1), NEG, F32), jnp.zeros((2 * tq, 1), F32), jnp.zeros((2 * tq, LANES), F32))
        res = init
        for d in range(nd):
            res = block(m_idx * nd + d, res, _chunk_ok(tq, tk, d))
        mx, l, acc = _by_twos(m_idx * nd, lambda kb, c: block(kb, c, None), res)
        h0 = lax.broadcasted_iota(jnp.int32, (tq, LANES), 1) < 64
        o_ref[...] = _two_heads(acc * (1.0 / l), h0).astype(o_ref.dtype)
        lse_ref[...] = _two_heads(jnp.broadcast_to(mx + jnp.log(l), (2 * tq, LANES)), h0)

    full = lambda col: pl.BlockSpec((T, LANES), col)
    table = pl.BlockSpec((tq, 2 * LANES), lambda p, m: (m, 0))
    return pl.pallas_call(
        body, name=name, grid=(npair, T // tq),
        in_specs=[pl.BlockSpec((tq, 2 * LANES), lambda p, m: (m, p)), table, table,
                  full(lambda p, m: (0, p)), full(lambda p, m: (0, npair + p)), full(lambda p, m: (0, 0))],
        out_specs=[pl.BlockSpec((tq, LANES), lambda p, m: (m, p)),
                   pl.BlockSpec((tq, LANES), lambda p, m: (m, p))],
        out_shape=[jax.ShapeDtypeStruct((T, npair * LANES), BF16),
                   jax.ShapeDtypeStruct((T, npair * LANES), F32)],
        compiler_params=_params(("parallel", "arbitrary")),
    )(q, cos_q, sin_q, kv, kv, kr)


def _mla_bwd(q, cos_q, sin_q, kv, kr, o, lse, do, do_col0, name="mla_bwd"):
    T = q.shape[0]
    tq, tk = _pick(T, ATT_TQ), _pick(T, ATT_TK)
    nd = tq // tk
    npair = MLA_HEADS // 2
    scale = (MLA_NOPE + MLA_ROPE) ** -0.5

    def body(q_ref, c_ref, s_ref, kn_ref, v_ref, kr_ref, o_ref, lse_ref, do_ref, dq_ref, dkn_ref, dv_ref, dkr_ref,
             dkn_acc, dv_acc):
        p_idx, m_idx = pl.program_id(0), pl.program_id(1)

        @pl.when(m_idx == 0)
        def _():
            dkn_acc[...] = jnp.zeros_like(dkn_acc)
            dv_acc[...] = jnp.zeros_like(dv_acc)

        @pl.when((m_idx == 0) & (p_idx == 0))
        def _():
            dkr_ref[...] = jnp.zeros_like(dkr_ref)

        qv = _rotate(q_ref[...], c_ref[...], s_ref[...]).astype(BF16)
        m0, m1 = _mla_masks(qv.shape)
        qs = jnp.concatenate([jnp.where(m0, qv, 0), jnp.where(m1, qv, 0)], axis=0).astype(BF16)
        dov = do_ref[...].astype(F32)
        h0 = lax.broadcasted_iota(jnp.int32, (tq, LANES), 1) < 64
        dos32 = jnp.concatenate([jnp.where(h0, dov, 0.0), jnp.where(h0, 0.0, dov)], axis=0)
        ov = o_ref[...].astype(F32)
        delta = jnp.sum(dos32 * jnp.concatenate([ov, ov], axis=0), axis=-1, keepdims=True)
        dos = dos32.astype(BF16)
        lsev = lse_ref[...]
        lse = jnp.concatenate([lsev[:, 0:1], lsev[:, 64:65]], axis=0)

        def block(kb, dq, ok):
            ks = pl.ds(pl.multiple_of(kb * tk, tk), tk)
            kcat = jnp.concatenate([kn_ref[ks, :], kr_ref[ks, :]], axis=1)
            vv = v_ref[ks, :]
            p = jnp.exp(_dot(qs, kcat, NT) * scale - lse)
            if ok is not None:
                p = jnp.where(ok, p, 0.0)
            ds = (p * (_dot(dos, vv, NT) - delta) * scale).astype(BF16)
            dkc = _dot(ds, qs, TN)
            dkn_acc[ks, :] += dkc[:, :LANES]
            dkr_ref[ks, :] += dkc[:, LANES:]
            dv_acc[ks, :] += _dot(p.astype(BF16), dos, TN)
            return dq + _dot(ds, kcat, NN)

        dq = jnp.zeros((2 * tq, 2 * LANES), F32)
        for d in range(nd):
            dq = block(m_idx * nd + d, dq, _chunk_ok(tq, tk, d))
        dq = _by_twos(m_idx * nd, lambda kb, c: block(kb, c, None), dq)
        dq_ref[...] = _rotate(jnp.where(m0, dq[:tq], jnp.where(m1, dq[tq:], 0.0)), c_ref[...],
                              -s_ref[...]).astype(BF16)

        @pl.when(m_idx == T // tq - 1)
        def _():
            dkn_ref[...] = dkn_acc[...].astype(BF16)
            dv_ref[...] = dv_acc[...].astype(BF16)

    full = lambda col: pl.BlockSpec((T, LANES), col)
    blk = lambda col: pl.BlockSpec((tq, LANES), col)
    table = pl.BlockSpec((tq, 2 * LANES), lambda p, m: (m, 0))
    return pl.pallas_call(
        body, name=name, grid=(npair, T // tq),
        in_specs=[pl.BlockSpec((tq, 2 * LANES), lambda p, m: (m, p)), table, table,
                  full(lambda p, m: (0, p)), full(lambda p, m: (0, npair + p)), full(lambda p, m: (0, 0)),
                  blk(lambda p, m: (m, p)), blk(lambda p, m: (m, p)),
                  blk(lambda p, m: (m, do_col0 + p))],
        out_specs=[pl.BlockSpec((tq, 2 * LANES), lambda p, m: (m, p)),
                   full(lambda p, m: (0, p)), full(lambda p, m: (0, p)), full(lambda p, m: (0, 0))],
        out_shape=[jax.ShapeDtypeStruct((T, npair * 2 * LANES), BF16),
                   jax.ShapeDtypeStruct((T, npair * LANES), BF16),
                   jax.ShapeDtypeStruct((T, npair * LANES), BF16),
                   jax.ShapeDtypeStruct((T, LANES), F32)],
        scratch_shapes=[pltpu.VMEM((T, LANES), F32)] * 2,
        compiler_params=_params(("arbitrary", "arbitrary")),
    )(q, cos_q, sin_q, kv, kv, kr, o, lse, do)


def _split_dot(x, tri):
    hi = x.astype(BF16)
    lo = (x - hi.astype(F32)).astype(BF16)
    both = _dot(jnp.concatenate([hi, lo], axis=0), tri, NN)
    return both[:x.shape[0]] + both[x.shape[0]:]


def _sb_terms(qh, kk, before):
    z = _dot(qh, kk, NT)
    sp = jnp.maximum(z, 0.0) + jnp.log(1.0 + jnp.exp(-jnp.abs(z)))
    lk = -sp if before is None else jnp.where(before, -sp, 0.0)
    return z, sp, lk


def _sb_setup(q_ref, tq, tk, scale):
    qv = (q_ref[...].astype(F32) * scale).astype(BF16)
    lane = lax.broadcasted_iota(jnp.int32, (tq, LANES), 1)
    h0 = lane < 64
    qs = jnp.concatenate([jnp.where(h0, qv, 0), jnp.where(h0, 0, qv)], axis=0).astype(BF16)
    row = lax.broadcasted_iota(jnp.int32, (tk, tk), 0)
    col = lax.broadcasted_iota(jnp.int32, (tk, tk), 1)
    return qs, h0, row, col


def _sb_before(tq, tk, d):
    row = lax.broadcasted_iota(jnp.int32, (tq, tk), 0)
    col = lax.broadcasted_iota(jnp.int32, (tq, tk), 1) + d * tk
    return jnp.concatenate([col < row] * 2, axis=0)


def _two_heads(x, h0):
    tq = x.shape[0] // 2
    return jnp.where(h0, x[:tq], x[tq:])


def _sb_fwd(qkv, col0, name="sb_fwd"):
    T = qkv.shape[0]
    tq, tk = _pick(T, ATT_TQ), _pick(T, ATT_TK)
    nd = tq // tk
    npair = SB_HEADS // 2
    scale = SB_DIM ** -0.5

    def body(q_ref, k_ref, v_ref, o_ref, o32_ref):
        m_idx = pl.program_id(1)
        qs, h0, row, col = _sb_setup(q_ref, tq, tk, scale)
        later = (row > col).astype(BF16)

        def block(kb, carry, before):
            ks = pl.ds(pl.multiple_of(kb * tk, tk), tk)
            c, acc = carry
            z, sp, lk = _sb_terms(qs, k_ref[ks, :].astype(BF16), before)
            w = jnp.exp((z - sp) + _split_dot(lk, later) + c)
            if before is not None:
                w = jnp.where(before, w, 0.0)
            return (c + jnp.sum(lk, axis=-1, keepdims=True),
                    acc + _dot(w.astype(BF16), v_ref[ks, :].astype(BF16), NN))

        init = (jnp.zeros((2 * tq, 1), F32), jnp.zeros((2 * tq, LANES), F32))
        res = init
        for d in reversed(range(nd)):
            res = block(m_idx * nd + d, res, _sb_before(tq, tk, d))
        res = _by_twos(m_idx * nd, lambda i, c: block(m_idx * nd - 1 - i, c, None), res)
        o = _two_heads(res[1], h0)
        o_ref[...] = o.astype(o_ref.dtype)
        o32_ref[...] = o

    full = lambda col: pl.BlockSpec((T, LANES), col)
    blk = pl.BlockSpec((tq, LANES), lambda p, m: (m, p))
    return pl.pallas_call(
        body, name=name, grid=(npair, T // tq),
        in_specs=[pl.BlockSpec((tq, LANES), lambda p, m: (m, col0 + p)),
                  full(lambda p, m: (0, col0 + npair + p)), full(lambda p, m: (0, col0 + 2 * npair + p))],
        out_specs=[blk, blk],
        out_shape=[jax.ShapeDtypeStruct((T, npair * LANES), BF16), jax.ShapeDtypeStruct((T, npair * LANES), F32)],
        compiler_params=_params(("parallel", "arbitrary")),
    )(qkv, qkv, qkv)


def _sb_bwd(qkv, col0, o32, do, do_col0, dep, name="sb_bwd"):
    T = qkv.shape[0]
    tq, tk = _pick(T, ATT_TQ), _pick(T, ATT_TK)
    nd = tq // tk
    npair = SB_HEADS // 2
    scale = SB_DIM ** -0.5

    def body(q_ref, k_ref, v_ref, o_ref, do_ref, dep_ref, dq_ref, dk_ref, dv_ref, dk_acc, dv_acc):
        m_idx = pl.program_id(1)

        @pl.when(m_idx == 0)
        def _():
            dk_acc[...] = jnp.zeros_like(dk_acc)
            dv_acc[...] = jnp.zeros_like(dv_acc)

        qs, h0, row, col = _sb_setup(q_ref, tq, tk, scale)
        dov = do_ref[...].astype(F32)
        dos = jnp.concatenate([jnp.where(h0, dov, 0.0), jnp.where(h0, 0.0, dov)], axis=0).astype(BF16)
        ov = o_ref[...]
        etot = jnp.sum(dos.astype(F32) * jnp.concatenate([ov, ov], axis=0), axis=-1, keepdims=True)
        later = (row > col).astype(BF16)
        from_here = (row >= col).astype(BF16)

        def block(kb, carry, before):
            ks = pl.ds(pl.multiple_of(kb * tk, tk), tk)
            kk = k_ref[ks, :].astype(BF16)
            vv = v_ref[ks, :].astype(BF16)
            c, es, dqa = carry
            z, sp, lk = _sb_terms(qs, kk, before)
            w = jnp.exp((z - sp) + _split_dot(lk, later) + c)
            if before is not None:
                w = jnp.where(before, w, 0.0)
            wb = w.astype(BF16)
            e = wb.astype(F32) * _dot(dos, vv, NT)
            prev = etot - (_split_dot(e, from_here) + es)
            sig_neg = jnp.exp(-sp)
            dz = e * sig_neg - (1.0 - sig_neg) * prev
            if before is not None:
                dz = jnp.where(before, dz, 0.0)
            dzb = dz.astype(BF16)
            dk_acc[ks, :] += _dot(dzb, qs, TN)
            dv_acc[ks, :] += _dot(wb, dos, TN)
            return (c + jnp.sum(lk, axis=-1, keepdims=True), es + jnp.sum(e, axis=-1, keepdims=True),
                    dqa + _dot(dzb, kk, NN))

        init = (jnp.zeros((2 * tq, 1), F32), jnp.zeros((2 * tq, 1), F32), jnp.zeros((2 * tq, LANES), F32))
        res = init
        for d in reversed(range(nd)):
            res = block(m_idx * nd + d, res, _sb_before(tq, tk, d))
        res = _by_twos(m_idx * nd, lambda i, c: block(m_idx * nd - 1 - i, c, None), res)
        dq_ref[...] = (_two_heads(res[2], h0) * scale).astype(BF16)

        @pl.when(m_idx == T // tq - 1)
        def _():
            dk_ref[...] = dk_acc[...].astype(BF16)
            dv_ref[...] = dv_acc[...].astype(BF16)

    full = lambda col: pl.BlockSpec((T, LANES), col)
    blk = lambda col: pl.BlockSpec((tq, LANES), col)
    return pl.pallas_call(
        body, name=name, grid=(npair, T // tq),
        in_specs=[blk(lambda p, m: (m, col0 + p)),
                  full(lambda p, m: (0, col0 + npair + p)), full(lambda p, m: (0, col0 + 2 * npair + p)),
                  blk(lambda p, m: (m, p)), blk(lambda p, m: (m, do_col0 + p)),
                  pl.BlockSpec((8, LANES), lambda p, m: (0, 0))],
        out_specs=[blk(lambda p, m: (m, p)), full(lambda p, m: (0, p)), full(lambda p, m: (0, p))],
        out_shape=[jax.ShapeDtypeStruct((T, npair * LANES), BF16)] * 3,
        scratch_shapes=[pltpu.VMEM((T, LANES), F32)] * 2,
        compiler_params=_params(("arbitrary", "arbitrary")),
    )(qkv, qkv, qkv, o32, do, dep)


def _band_in_window():
    cq = lax.broadcasted_iota(jnp.int32, (BAND_TQ, BAND_W), 0) >> 6
    ckp = lax.broadcasted_iota(jnp.int32, (BAND_TQ, BAND_W), 1) >> 6
    return (ckp >= cq) & (ckp <= cq + LEFT_CHUNKS)


def _band_real(m_idx):
    j = lax.broadcasted_iota(jnp.int32, (BAND_TQ, BAND_W), 1)
    return j >= PAD_KEYS - m_idx * BAND_TQ


def _band_probs(qh, kw, bias, real, scale):
    s = jnp.where(real, _dot(qh, kw, NT) * scale + bias, NEG)
    e = jnp.exp(s - jnp.max(s, axis=-1, keepdims=True))
    return e * (1.0 / jnp.sum(e, axis=-1, keepdims=True))


BAND_SUB = 2


def _band_fwd(qkv, k_pad, v_pad, bias_w, name="band_fwd"):
    T = qkv.shape[0]
    npair = C_HEADS // 2
    scale = C_DIM ** -0.5
    rows = BAND_SUB * BAND_TQ

    def body(q_ref, k_ref, v_ref, b_ref, o_ref):
        lane = lax.broadcasted_iota(jnp.int32, (BAND_TQ, LANES), 1)
        h0 = lane < 64
        bias = jnp.concatenate([b_ref[0], b_ref[1]], axis=0)
        for sub in range(BAND_SUB):
            m_idx = pl.program_id(1) * BAND_SUB + sub
            win = pl.ds(pl.multiple_of(m_idx * BAND_TQ, BAND_TQ), BAND_W)
            kw, vw = k_ref[win, :], v_ref[win, :]
            qv = q_ref[sub * BAND_TQ:(sub + 1) * BAND_TQ, :]
            qs = jnp.concatenate([jnp.where(h0, qv, 0), jnp.where(h0, 0, qv)], axis=0).astype(BF16)
            p = _band_probs(qs, kw, bias, jnp.concatenate([_band_real(m_idx)] * 2, axis=0), scale)
            o = _two_heads(_dot(p.astype(BF16), vw, NN), h0)
            o_ref[sub * BAND_TQ:(sub + 1) * BAND_TQ, :] = o.astype(o_ref.dtype)

    Tp = T + PAD_KEYS
    return pl.pallas_call(
        body, name=name, grid=(npair, T // rows),
        in_specs=[pl.BlockSpec((rows, LANES), lambda p, m: (m, p)),
                  pl.BlockSpec((Tp, LANES), lambda p, m: (0, p)),
                  pl.BlockSpec((Tp, LANES), lambda p, m: (0, p)),
                  pl.BlockSpec((2, BAND_TQ, BAND_W), lambda p, m: (p, 0, 0))],
        out_specs=pl.BlockSpec((rows, LANES), lambda p, m: (m, p)),
        out_shape=jax.ShapeDtypeStruct((T, npair * LANES), BF16),
        compiler_params=_params(("parallel", "arbitrary")),
    )(qkv, k_pad, v_pad, bias_w)


def _band_bwd(qkv, k_pad, v_pad, bias_w, do, name="band_bwd"):
    T = qkv.shape[0]
    npair = C_HEADS // 2
    scale = C_DIM ** -0.5

    rows = BAND_SUB * BAND_TQ

    def body(q_ref, k_ref, v_ref, b_ref, do_ref, dq_ref, dk_ref, dv_ref, db_ref, dk_acc, dv_acc):
        @pl.when(pl.program_id(1) == 0)
        def _():
            dk_acc[...] = jnp.zeros_like(dk_acc)
            dv_acc[...] = jnp.zeros_like(dv_acc)
            db_ref[...] = jnp.zeros_like(db_ref)

        lane = lax.broadcasted_iota(jnp.int32, (BAND_TQ, LANES), 1)
        h0 = lane < 64
        dbs = jnp.zeros((2 * BAND_TQ, BAND_W), F32)
        bias = jnp.concatenate([b_ref[0], b_ref[1]], axis=0)
        for sub in range(BAND_SUB):
            m_idx = pl.program_id(1) * BAND_SUB + sub
            win = pl.ds(pl.multiple_of(m_idx * BAND_TQ, BAND_TQ), BAND_W)
            kw, vw = k_ref[win, :], v_ref[win, :]
            qv = q_ref[sub * BAND_TQ:(sub + 1) * BAND_TQ, :]
            dov = do_ref[sub * BAND_TQ:(sub + 1) * BAND_TQ, :].astype(F32)
            qs = jnp.concatenate([jnp.where(h0, qv, 0), jnp.where(h0, 0, qv)], axis=0).astype(BF16)
            dos = jnp.concatenate([jnp.where(h0, dov, 0.0), jnp.where(h0, 0.0, dov)], axis=0).astype(BF16)
            p = _band_probs(qs, kw, bias, jnp.concatenate([_band_real(m_idx)] * 2, axis=0), scale)
            dp = _dot(dos, vw, NT)
            dsb = p * (dp - jnp.sum(p * dp, axis=-1, keepdims=True))
            dbs = dbs + dsb
            dsq = (dsb * scale).astype(BF16)
            dq_ref[sub * BAND_TQ:(sub + 1) * BAND_TQ, :] = _two_heads(_dot(dsq, kw, NN), h0).astype(BF16)
            dk_acc[win, :] += _dot(dsq, qs, TN)
            dv_acc[win, :] += _dot(p.astype(BF16), dos, TN)
        db_ref[0] += dbs[:BAND_TQ]
        db_ref[1] += dbs[BAND_TQ:]

        @pl.when(pl.program_id(1) == T // rows - 1)
        def _():
            dk_ref[...] = dk_acc[...].astype(BF16)
            dv_ref[...] = dv_acc[...].astype(BF16)

    Tp = T + PAD_KEYS
    blk = lambda col: pl.BlockSpec((rows, LANES), col)
    full = pl.BlockSpec((Tp, LANES), lambda p, m: (0, p))
    bias = pl.BlockSpec((2, BAND_TQ, BAND_W), lambda p, m: (p, 0, 0))
    return pl.pallas_call(
        body, name=name, grid=(npair, T // rows),
        in_specs=[blk(lambda p, m: (m, p)), full, full, bias, blk(lambda p, m: (m, p))],
        out_specs=[blk(lambda p, m: (m, p)), full, full, bias],
        out_shape=[jax.ShapeDtypeStruct((T, npair * LANES), BF16),
                   jax.ShapeDtypeStruct((Tp, npair * LANES), BF16),
                   jax.ShapeDtypeStruct((Tp, npair * LANES), BF16),
                   jax.ShapeDtypeStruct((C_HEADS, BAND_TQ, BAND_W), F32)],
        scratch_shapes=[pltpu.VMEM((Tp, LANES), F32)] * 2,
        compiler_params=_params(("arbitrary", "arbitrary")),
    )(qkv, k_pad, v_pad, bias_w, do)


def _skew_bits(x, left):
    w = x.shape[1]
    row = lax.broadcasted_iota(jnp.int32, x.shape, 0)
    for b in range(BAND_TQ.bit_length() - 1):
        amt = (w - (1 << b)) if left else (1 << b)
        x = jnp.where((row >> b) & 1 == 1, pltpu.roll(x, amt, 1), x)
    return x


def _toeplitz(diag, name="toeplitz"):
    H = diag.shape[0]

    def body(d_ref, o_ref):
        x = jnp.broadcast_to(d_ref[0], (BAND_TQ, TOEP_W))
        o_ref[0] = jnp.where(_band_in_window(), _skew_bits(x, left=False)[:, BAND_TQ:], NEG)

    return pl.pallas_call(
        body, name=name, grid=(H,),
        in_specs=[pl.BlockSpec((1, 1, TOEP_W), lambda h: (h, 0, 0))],
        out_specs=pl.BlockSpec((1, BAND_TQ, BAND_W), lambda h: (h, 0, 0)),
        out_shape=jax.ShapeDtypeStruct((H, BAND_TQ, BAND_W), F32),
        compiler_params=_params(("parallel",)),
    )(diag.reshape(H, 1, TOEP_W))


def _toeplitz_bwd(dbias, name="toeplitz_bwd"):
    H = dbias.shape[0]

    def body(d_ref, o_ref):
        x = jnp.concatenate([jnp.zeros((BAND_TQ, BAND_TQ), F32), d_ref[0]], axis=1)
        o_ref[0] = jnp.sum(_skew_bits(x, left=True), axis=0, keepdims=True)

    return pl.pallas_call(
        body, name=name, grid=(H,),
        in_specs=[pl.BlockSpec((1, BAND_TQ, BAND_W), lambda h: (h, 0, 0))],
        out_specs=pl.BlockSpec((1, 1, TOEP_W), lambda h: (h, 0, 0)),
        out_shape=jax.ShapeDtypeStruct((H, 1, TOEP_W), F32),
        compiler_params=_params(("parallel",)),
    )(dbias).reshape(H, TOEP_W)


_HBM = pl.BlockSpec(memory_space=pltpu.HBM)
_SEM = pl.BlockSpec(memory_space=pltpu.SEMAPHORE)
_EFFECT = pltpu.SideEffectType.DATAFLOW_SIDE_EFFECTING


def _peers():
    x, y, c = lax.axis_index("x"), lax.axis_index("y"), lax.axis_index("c")
    out = []
    for k in range(1, N_DEV):
        peer = (1 - x if (k >> 2) & 1 else x, 1 - y if (k >> 1) & 1 else y, 1 - c if k & 1 else c)
        out.append((peer, 4 * peer[0] + 2 * peer[1] + peer[2]))
    return 4 * x + 2 * y + c, out


def _split_copies(ins, lands, scatter, send_sem, recv_sem, arriving):
    me, peers = _peers()
    out = []
    for a in range(len(ins)):
        for peer, idx in peers:
            out.append(pltpu.make_async_remote_copy(
                src_ref=ins[a].at[idx] if scatter[a] else ins[a],
                dst_ref=lands[a].at[idx if arriving else me], send_sem=send_sem, recv_sem=recv_sem,
                device_id=peer, device_id_type=pl.DeviceIdType.MESH))
    return out


def _landing_zones(arrays, scatter):
    return [lax.empty((N_DEV,) + (a.shape[1:] if s else a.shape), a.dtype) for a, s in zip(arrays, scatter)]


def _place_own(arrays, scatter, name):
    n = len(arrays)
    lands = _landing_zones(arrays, scatter)
    me = (4 * lax.axis_index("x") + 2 * lax.axis_index("y") + lax.axis_index("c")).astype(jnp.int32).reshape(1)

    def body(me_ref, *refs):
        for a in range(n):
            refs[2 * n + a][...] = refs[a][...].reshape(refs[2 * n + a].shape)

    def row_spec(shape):
        zeros = (0,) * (len(shape) - 1)
        return pl.BlockSpec((1,) + tuple(shape[1:]), lambda i, me_ref: (me_ref[0],) + zeros)

    in_specs = [row_spec(a.shape) if s else pl.BlockSpec(a.shape, lambda i, me_ref, nd=a.ndim: (0,) * nd)
                for a, s in zip(arrays, scatter)]
    return pl.pallas_call(
        body, name=name,
        out_shape=[jax.ShapeDtypeStruct(l.shape, l.dtype) for l in lands],
        grid_spec=pltpu.PrefetchScalarGridSpec(
            num_scalar_prefetch=1, grid=(1,),
            in_specs=in_specs + [pl.BlockSpec(memory_space=pl.ANY)] * n,
            out_specs=[row_spec(l.shape) for l in lands]),
        input_output_aliases={1 + n + i: i for i in range(n)},
        compiler_params=_params(("arbitrary",)),
    )(me, *arrays, *lands)


def _exchange_start(arrays, scatter, after, name):
    n = len(arrays)
    lands = list(_place_own(arrays, scatter, name=name.replace("_start_", "_own_")))

    def body(*refs):
        ins, lnd = refs[:n], refs[n:2 * n]
        send_sem, recv_sem = refs[2 * n + 1:2 * n + 3]
        token = refs[-1]
        for cp in _split_copies(ins, lnd, scatter, send_sem, recv_sem, arriving=False):
            cp.start()
        token[...] = jnp.zeros_like(token)

    hbm = lambda a: pltpu.HBM(a.shape, a.dtype)
    out = pl.pallas_call(
        body, name=name,
        out_shape=(pltpu.SemaphoreType.DMA(()), pltpu.SemaphoreType.DMA(()),
                   *[hbm(a) for a in arrays], *[hbm(a) for a in lands],
                   jax.ShapeDtypeStruct((8, LANES), F32)),
        in_specs=[_HBM] * (2 * n) + [pl.BlockSpec(memory_space=pl.ANY)],
        out_specs=(_SEM, _SEM, *([_HBM] * (2 * n)), pl.BlockSpec(memory_space=pltpu.VMEM)),
        input_output_aliases={i: 2 + i for i in range(2 * n)},
        compiler_params=pltpu.CompilerParams(has_side_effects=_EFFECT),
    )(*[pltpu.with_memory_space_constraint(a, pltpu.HBM) for a in list(arrays) + lands], after)
    return (out[0], out[1], list(out[2:2 + n]), list(out[2 + n:2 + 2 * n]), tuple(scatter)), out[-1]


def _exchange_wait(handle, after, name):
    send_sem, recv_sem, ins, lands, scatter = handle
    n = len(ins)
    after = after if isinstance(after, tuple) else (after,)

    def body(*refs):
        i_ref, l_ref = refs[:n], refs[n:2 * n]
        s_sem, r_sem = refs[2 * n:2 * n + 2]
        for cp in _split_copies(i_ref, l_ref, scatter, s_sem, r_sem, arriving=False):
            cp.wait_send()
        for cp in _split_copies(i_ref, l_ref, scatter, s_sem, r_sem, arriving=True):
            cp.wait_recv()

    hbm = lambda a: pltpu.HBM(a.shape, a.dtype)
    out = pl.pallas_call(
        body, name=name,
        out_shape=tuple(hbm(a) for a in ins + lands),
        in_specs=[_HBM] * (2 * n) + [_SEM, _SEM] + [pl.BlockSpec(memory_space=pl.ANY)] * len(after),
        out_specs=tuple([_HBM] * (2 * n)),
        input_output_aliases={i: i for i in range(2 * n)},
        compiler_params=pltpu.CompilerParams(has_side_effects=_EFFECT),
    )(*ins, *lands, send_sem, recv_sem, *after)
    return list(out[n:])


def _adamw(w, parts, m, v, name="adamw"):
    R, C = w.shape
    L = len(parts)
    rl = R // L
    tr = max([t for t in range(16, 513, 16) if rl % t == 0], default=rl)
    nb = rl // tr
    c1 = 1.0 - ADAM_B1 ** ADAM_STEP
    c2 = 1.0 - ADAM_B2 ** ADAM_STEP

    def body(*refs):
        w_ref, p_refs, (m_ref, v_ref, g_ref, d_ref, nm_ref, nv_ref) = refs[0], refs[1:1 + L], refs[1 + L:]
        g = None
        for j, p_ref in enumerate(p_refs):
            gj = p_ref[0].astype(F32)
            for i in range(1, N_DEV):
                gj = gj + p_ref[i].astype(F32)
            g = gj if g is None else jnp.where(pl.program_id(0) == j, gj, g)
        nm = ADAM_B1 * m_ref[...] + (1.0 - ADAM_B1) * g
        nv = ADAM_B2 * v_ref[...] + (1.0 - ADAM_B2) * (g * g)
        g_ref[...] = g
        nm_ref[...] = nm
        nv_ref[...] = nv
        d_ref[...] = -ADAM_LR * ((nm / c1) / (jnp.sqrt(nv / c2) + ADAM_EPS) + ADAM_WD * w_ref[...])

    blk = pl.BlockSpec((tr, C), lambda l, i: (l * nb + i, 0))
    part = lambda j: pl.BlockSpec((N_DEV, tr, C), lambda l, i: (0, jnp.where(l == j, i, 0), 0))
    return pl.pallas_call(
        body, name=name, grid=(L, nb),
        in_specs=[blk] + [part(j) for j in range(L)] + [blk, blk],
        out_specs=[blk] * 4,
        out_shape=[jax.ShapeDtypeStruct((R, C), F32)] * 4,
        compiler_params=_params(("arbitrary", "arbitrary")),
    )(w, *parts, m, v)


_O1 = Q_LORA
_O2 = _O1 + KV_LORA
_O3 = _O2 + MLA_ROPE
_NB = SB_HEADS * SB_DIM
IN_W = _O2 + LANES + 3 * _NB
COL_KR = _O2 // LANES
COL_SB = COL_KR + 1


def _w_in_local(w):
    kr = w[_O2:_O3]
    pad = jnp.zeros((LANES - 2 * MLA_ROPE, w.shape[1]), w.dtype)
    return jnp.concatenate([w[:_O2], kr, kr, pad, w[_O3:]], axis=0)


def _w_in_grad(g):
    kr = (g[_O2:_O2 + MLA_ROPE].astype(F32) + g[_O2 + MLA_ROPE:_O2 + 2 * MLA_ROPE].astype(F32)).astype(g.dtype)
    return jnp.concatenate([g[:_O2], kr, g[_O2 + LANES:]], axis=0)


def _w_uq_local(w):
    w3 = w.reshape(MLA_HEADS // 2, 2, MLA_NOPE + MLA_ROPE, w.shape[1])
    nope = w3[:, :, :MLA_NOPE].reshape(MLA_HEADS // 2, 2 * MLA_NOPE, w.shape[1])
    rope = w3[:, :, MLA_NOPE:].reshape(MLA_HEADS // 2, 2 * MLA_ROPE, w.shape[1])
    pad = jnp.zeros((MLA_HEADS // 2, LANES - 2 * MLA_ROPE, w.shape[1]), w.dtype)
    return jnp.concatenate([nope, rope, pad], axis=1).reshape(-1, w.shape[1])


def _w_uq_grad(g):
    g3 = g.reshape(MLA_HEADS // 2, 2 * LANES, g.shape[1])
    nope = g3[:, :2 * MLA_NOPE].reshape(MLA_HEADS // 2, 2, MLA_NOPE, g.shape[1])
    rope = g3[:, LANES:LANES + 2 * MLA_ROPE].reshape(MLA_HEADS // 2, 2, MLA_ROPE, g.shape[1])
    return jnp.concatenate([nope, rope], axis=2).reshape(-1, g.shape[1])


def _w_ukv_local(w):
    w3 = w.reshape(MLA_HEADS, MLA_NOPE + MLA_V, w.shape[1])
    return jnp.concatenate([w3[:, :MLA_NOPE].reshape(-1, w.shape[1]),
                            w3[:, MLA_NOPE:].reshape(-1, w.shape[1])], axis=0)


def _w_ukv_grad(g):
    half = MLA_HEADS * MLA_NOPE
    kn = g[:half].reshape(MLA_HEADS, MLA_NOPE, g.shape[1])
    vv = g[half:].reshape(MLA_HEADS, MLA_V, g.shape[1])
    return jnp.concatenate([kn, vv], axis=1).reshape(-1, g.shape[1])


def _rope_tables(T):
    pos = jnp.arange(T, dtype=F32)
    inv_freq = ROPE_THETA ** (-jnp.arange(0, MLA_ROPE, 2, dtype=F32) / MLA_ROPE)
    ang = pos[:, None] * inv_freq[None, :]
    cos, sin = jnp.cos(ang), jnp.sin(ang)
    ones = jnp.ones((T, LANES - 2 * MLA_ROPE), F32)
    cos_k = jnp.concatenate([cos, cos, cos, cos, ones], axis=1)
    sin_k = jnp.concatenate([-sin, sin, -sin, sin, 0.0 * ones], axis=1)
    cos_q = jnp.concatenate([jnp.ones((T, LANES), F32), cos_k], axis=1)
    sin_q = jnp.concatenate([jnp.zeros((T, LANES), F32), sin_k], axis=1)
    return cos_q, sin_q, cos_k, sin_k


def _bias_diag_index():
    ell = np.arange(TOEP_W)
    return np.clip(BAND_W - ell, -REL_CLIP, REL_CLIP) + REL_CLIP


def _local_step(x, target, small, get_weights, put_grads):
    T = x.shape[0]
    cos_q, sin_q, cos_k, sin_k = _rope_tables(T)
    G = {}
    W = dict(small)

    u0 = _rms_fwd(x, W["g_mix"][0:1], name="rms_mix0")
    bias_w = _toeplitz(W["od_rel_bias"][:, _bias_diag_index()])
    W.update(get_weights("in0", (u0, bias_w)))
    proj = _mm(u0, W["w_in_t"], dims="nt", name="proj_in")
    W.update(get_weights("mix0", proj))
    c_q, c_kv = proj[:, :_O1], proj[:, _O1:_O2]
    nq = _rms_fwd(c_q, W["g_cq"], name="rms_cq")
    nkv = _rms_fwd(c_kv, W["g_ckv"], name="rms_ckv")
    qa_raw = _mm(nq, W["w_uq_t"], dims="nt", name="proj_uq")
    kv = _mm(nkv, W["w_ukv_t"], dims="nt", out_dtype=BF16, name="proj_ukv")
    kr = _rope(proj, cos_k, sin_k, COL_KR, 1, BF16, name="rope_k")
    o_a, lse = _mla_fwd(qa_raw, cos_q, sin_q, kv, kr)
    o_b, o_b32 = _sb_fwd(proj, COL_SB)
    o_ab = jnp.concatenate([o_a, o_b], axis=1)
    h1 = _mm(o_ab, W["ev_w_out"], res=x, name="out_ev")

    def ffn_fwd(h, layer):
        W.update(get_weights(f"ffn{layer}", h))
        return _ffn_fwd(h, W["g_ffn"][layer:layer + 1], W[f"w_gate_t{layer}"], W[f"w_up_t{layer}"],
                        W[f"w_down{layer}"], name=f"ffn_fwd{layer}")

    h2, u1, a0, b0 = ffn_fwd(h1, 0)

    W.update(get_weights("mix1", h2))
    u2 = _rms_fwd(h2, W["g_mix"][1:2], name="rms_mix1")
    qkv = _mm(u2, W["od_w_qkv_t"], dims="nt", out_dtype=BF16, name="proj_qkv")
    nc = C_HEADS * C_DIM
    pad = ((PAD_KEYS, 0), (0, 0))
    k_pad, v_pad = jnp.pad(qkv[:, nc:2 * nc], pad), jnp.pad(qkv[:, 2 * nc:], pad)
    o_c = _band_fwd(qkv, k_pad, v_pad, bias_w)
    h3 = _mm(o_c, W["od_w_out"], res=h2, name="out_od")
    h4, u3, a1, b1 = ffn_fwd(h3, 1)

    loss, dh, dhb, G["g_final"] = _loss_head(h4, W["g_final"], target)

    def ffn_bwd(dh, dhb, h, u, a, b, layer):
        du, g_gate, g_up, g_down = _ffn_bwd(dhb, u, a, b, W[f"w_gate_t{layer}"], W[f"w_up_t{layer}"],
                                            W[f"w_down{layer}"], name=f"ffn_bwd{layer}")
        tok = put_grads(f"ffn{layer}", {"w_gate_t": g_gate, "w_up_t": g_up, "w_down": g_down})
        return _rms_bwd(h, W["g_ffn"][layer:layer + 1] + tok[:1, :1], du, dres=dh, name=f"rms_ffn_bwd{layer}")

    dh3, dh3b, g_gffn1 = ffn_bwd(dh, dhb, h3, u3, a1, b1, 1)

    do_c = _mm(dh3b, W["od_w_out"], dims="nt", name="out_od_dx")
    g_od_out = _mm(o_c, dh3b, dims="tn", out_dtype=BF16, name="out_od_dw")
    dq_c, dk_p, dv_p, dbias_w = _band_bwd(qkv, k_pad, v_pad, bias_w, do_c)
    dqkv = jnp.concatenate([dq_c, dk_p[PAD_KEYS:], dv_p[PAD_KEYS:]], axis=1)
    du2 = _mm(dqkv, W["od_w_qkv_t"], name="proj_qkv_dx")
    tok = put_grads("mix1", {"od_w_qkv_t": _mm(dqkv, u2, dims="tn", out_dtype=BF16, name="proj_qkv_dw"),
                             "od_w_out": g_od_out})
    ddiag = _toeplitz_bwd(dbias_w)
    n_far = BAND_W - REL_CLIP + 1
    G["od_rel_bias"] = jnp.concatenate(
        [jnp.zeros((C_HEADS, REL_CLIP - BAND_TQ + 1), F32), ddiag[:, n_far:][:, ::-1],
         jnp.sum(ddiag[:, :n_far], axis=1, keepdims=True)], axis=1)
    dh2, dh2b, g_gmix1 = _rms_bwd(h2, W["g_mix"][1:2] + tok[:1, :1], du2, dres=dh3, name="rms_mix_bwd1")

    dh1, dh1b, g_gffn0 = ffn_bwd(dh2, dh2b, h1, u1, a0, b0, 0)
    G["g_ffn"] = jnp.concatenate([g_gffn0, g_gffn1], axis=0)

    do_ab = _mm(dh1b, W["ev_w_out"], dims="nt", name="out_ev_dx")
    g0 = {"ev_w_out": _mm(o_ab, dh1b, dims="tn", out_dtype=BF16, name="out_ev_dw")}
    dqa_raw, dkn, dva, dkr = _mla_bwd(qa_raw, cos_q, sin_q, kv, kr, o_a, lse, do_ab, 0)
    g0["w_uq_t"] = _mm(dqa_raw, nq, dims="tn", name="proj_uq_dw")
    dnq = _mm(dqa_raw, W["w_uq_t"], name="proj_uq_dx")
    _, dc_q, G["g_cq"] = _rms_bwd(c_q, W["g_cq"], dnq, name="rms_cq_bwd")
    dkv = jnp.concatenate([dkn, dva], axis=1)
    g0["w_ukv_t"] = _mm(dkv, nkv, dims="tn", name="proj_ukv_dw")
    dnkv = _mm(dkv, W["w_ukv_t"], name="proj_ukv_dx")
    _, dc_kv, G["g_ckv"] = _rms_bwd(c_kv, W["g_ckv"], dnkv, name="rms_ckv_bwd")
    tok = put_grads("mix0", g0)
    dqb, dkb, dvb = _sb_bwd(proj, COL_SB, o_b32, do_ab, MLA_HEADS // 2, tok)
    dkr_raw = _rope(dkr, cos_k, -sin_k, 0, 1, BF16, name="rope_k_bwd")
    dproj = jnp.concatenate([dc_q, dc_kv, dkr_raw, dqb, dkb, dvb], axis=1)
    du0 = _mm(dproj, W["w_in_t"], name="proj_in_dx")
    tok = put_grads("in0", {"w_in_t": _mm(dproj, u0, dims="tn", name="proj_in_dw")})
    dx, _, g_gmix0 = _rms_bwd(x, W["g_mix"][0:1] + tok[:1, :1], du0, dres=dh1, name="rms_mix_bwd0")
    G["g_mix"] = jnp.concatenate([g_gmix0, g_gmix1], axis=0)
    return loss[0, 0], dx, G


_BIG = ["ev_w_in", "ev_w_uq", "ev_w_ukv", "ev_w_out", "od_w_qkv", "od_w_out", "w_gate", "w_up", "w_down"]
_COL_SHARDED = {"ev_w_in", "ev_w_uq", "ev_w_ukv", "od_w_qkv", "w_gate", "w_up"}
_SMALL = ["ev_g_cq", "ev_g_ckv", "od_rel_bias", "g_mix", "g_ffn", "g_final"]
_GROUPS = {
    "in0": ["ev_w_in"],
    "mix0": ["ev_w_uq", "ev_w_ukv", "ev_w_out"],
    "ffn0": ["w_gate0", "w_up0", "w_down0"],
    "mix1": ["od_w_qkv", "od_w_out"],
    "ffn1": ["w_gate1", "w_up1", "w_down1"],
}
_GROUP_SRC = {n + str(l): (n, l) for n in ("w_gate", "w_up", "w_down") for l in (0, 1)}
_BATCHES = {"in0": ["in0"], "layer0": ["mix0", "ffn0"], "layer1": ["mix1", "ffn1"]}
_BATCH_OF = {grp: batch for batch, grps in _BATCHES.items() for grp in grps}
_SMALL_ROWS = 8
_SMALL_COLS = 1792


def _pack_small(vals):
    flat = jnp.concatenate([v.reshape(-1).astype(F32) for v in vals])
    flat = jnp.pad(flat, (0, _SMALL_ROWS * _SMALL_COLS - flat.shape[0]))
    return flat.reshape(_SMALL_ROWS, _SMALL_COLS)


def _unpack_small(packed, like):
    flat = packed.reshape(-1)
    out, off = [], 0
    for v in like:
        out.append(flat[off:off + v.size].reshape(v.shape))
        off += v.size
    return out


def kernel(x, ev_w_in, ev_g_cq, ev_w_uq, ev_g_ckv, ev_w_ukv, ev_w_out, od_w_qkv, od_rel_bias, od_w_out, g_mix, g_ffn, w_gate, w_up, w_down, g_final, loss_target, m_ev_w_in, m_ev_g_cq, m_ev_w_uq, m_ev_g_ckv, m_ev_w_ukv, m_ev_w_out, m_od_w_qkv, m_od_rel_bias, m_od_w_out, m_g_mix, m_g_ffn, m_w_gate, m_w_up, m_w_down, m_g_final, v_ev_w_in, v_ev_g_cq, v_ev_w_uq, v_ev_g_ckv, v_ev_w_ukv, v_ev_w_out, v_od_w_qkv, v_od_rel_bias, v_od_w_out, v_g_mix, v_g_ffn, v_w_gate, v_w_up, v_w_down, v_g_final):
    args = dict(locals())
    w = {n: args[n] for n in _BIG + _SMALL}
    mom = {n: args["m_" + n] for n in _BIG + _SMALL}
    var = {n: args["v_" + n] for n in _BIG + _SMALL}

    own = {}
    for grp, names in _GROUPS.items():
        for n in names:
            base, layer = _GROUP_SRC.get(n, (n, 0))
            shard = w[base][layer:layer + 1]
            own[n] = (jnp.swapaxes(shard, 1, 2) if base in _COL_SHARDED else shard).astype(BF16)
    gather, token = {}, x[0, :8, :LANES]
    for grp, names in _GROUPS.items():
        gather[grp], token = _exchange_start([own[n] for n in names], [False] * len(names), token,
                                             name="gather_start_" + grp)

    def get_weights(grp, after):
        names = _GROUPS[grp]
        lands = _exchange_wait(gather[grp], token if after is None else after, name="gather_wait_" + grp)
        full = {n: l.reshape(-1, l.shape[-1]) for n, l in zip(names, lands)}
        if grp == "in0":
            return {"w_in_t": _w_in_local(full["ev_w_in"])}
        if grp == "mix0":
            return {"w_uq_t": _w_uq_local(full["ev_w_uq"]), "w_ukv_t": _w_ukv_local(full["ev_w_ukv"]),
                    "ev_w_out": full["ev_w_out"]}
        if grp == "mix1":
            return {"od_w_qkv_t": full["od_w_qkv"], "od_w_out": full["od_w_out"]}
        layer = grp[-1]
        return {"w_gate_t" + layer: full["w_gate" + layer], "w_up_t" + layer: full["w_up" + layer],
                "w_down" + layer: full["w_down" + layer]}

    scatter, pending = {}, {}

    def put_grads(grp, g):
        if grp == "in0":
            g = {"ev_w_in": _w_in_grad(g["w_in_t"])}
        elif grp == "mix0":
            g = {"ev_w_uq": _w_uq_grad(g["w_uq_t"]), "ev_w_ukv": _w_ukv_grad(g["w_ukv_t"]),
                 "ev_w_out": g["ev_w_out"]}
        elif grp == "mix1":
            g = {"od_w_qkv": g["od_w_qkv_t"], "od_w_out": g["od_w_out"]}
        else:
            layer = grp[-1]
            g = {"w_gate" + layer: g["w_gate_t"], "w_up" + layer: g["w_up_t"], "w_down" + layer: g["w_down"]}
        pending.update({n: v.reshape(N_DEV, 1, v.shape[0] // N_DEV, v.shape[1]).astype(BF16) for n, v in g.items()})
        batch = _BATCH_OF[grp]
        names = [n for gr in _BATCHES[batch] for n in _GROUPS[gr]]
        if not all(n in pending for n in names):
            return jnp.zeros((8, LANES), F32)
        send = [pending[n] for n in names]
        scatter[batch], tok = _exchange_start(send, [True] * len(names), send[0], name="scatter_start_" + batch)
        return tok

    small = {"g_cq": ev_g_cq, "g_ckv": ev_g_ckv, "od_rel_bias": od_rel_bias[0],
             "g_mix": g_mix + token[0, 0], "g_ffn": g_ffn, "g_final": g_final.reshape(1, -1)}
    loss_part, dx, G = _local_step(x[0], loss_target[0], small, get_weights, put_grads)
    g_small = _pack_small([G["g_cq"], G["g_ckv"], G["od_rel_bias"], G["g_mix"], G["g_ffn"], G["g_final"],
                           loss_part.reshape(1)])
    small_handle, _ = _exchange_start([g_small], [False], dx, name="gather_start_small")

    grads, deltas, new_m, new_v = {}, {}, {}, {}
    parts, after = {}, dx

    def wait_parts(batch, after):
        lands = _exchange_wait(scatter[batch], after, name="scatter_wait_" + batch)
        parts.update(zip([n for grp in _BATCHES[batch] for n in _GROUPS[grp]], lands))
        return lands[0]

    def adamw(n):
        col = n in _COL_SHARDED
        rows = lambda a: (jnp.swapaxes(a, 1, 2) if col else a).reshape(-1, a.shape[1 if col else 2])
        layers = [parts[n]] if n in parts else [parts[n + "0"], parts[n + "1"]]
        res = _adamw(rows(w[n]), [p.reshape(N_DEV, -1, p.shape[-1]) for p in layers], rows(mom[n]), rows(var[n]),
                     name="adamw_" + n)
        L, a1, a2 = w[n].shape
        back = lambda r: jnp.swapaxes(r.reshape(L, a2, a1), 1, 2) if col else r.reshape(L, a1, a2)
        grads[n], deltas[n], new_m[n], new_v[n] = [back(r) for r in res]
        return res[0]

    for batch in ("layer1", "layer0"):
        after = wait_parts(batch, after)
    for n in _BIG[1:]:
        after = adamw(n)
    after = wait_parts("in0", after)
    after = adamw("ev_w_in")
    small_w = [w[n] for n in _SMALL]
    small_parts = _exchange_wait(small_handle, after, name="gather_wait_small")[0]
    loss = jnp.sum(small_parts.reshape(N_DEV, -1)[:, sum(v.size for v in small_w)])
    res = _adamw(_pack_small(small_w), [small_parts], _pack_small([mom[n] for n in _SMALL]),
                 _pack_small([var[n] for n in _SMALL]), name="adamw_small")
    for d, packed in zip((grads, deltas, new_m, new_v), res):
        for n, val in zip(_SMALL, _unpack_small(packed, small_w)):
            d[n] = val

    order = ["ev_w_in", "ev_g_cq", "ev_w_uq", "ev_g_ckv", "ev_w_ukv", "ev_w_out", "od_w_qkv", "od_rel_bias",
             "od_w_out", "g_mix", "g_ffn", "w_gate", "w_up", "w_down", "g_final"]
    out = [loss, dx[None]]
    for d in (grads, deltas, new_m, new_v):
        out += [d[n] for n in order]
    return tuple(out)
```

```python
import functools

import numpy as np
import jax
import jax.numpy as jnp
from jax import lax
from jax.experimental import pallas as pl
from jax.experimental.pallas import tpu as pltpu

F32 = jnp.float32
BF16 = jnp.bfloat16

D_MODEL = 1024
CHUNK = 64
MLA_HEADS = 8
MLA_NOPE = 64
MLA_ROPE = 32
MLA_V = 64
Q_LORA = 384
KV_LORA = 256
ROPE_THETA = 10000.0
SB_HEADS = 8
SB_DIM = 64
C_HEADS = 16
C_DIM = 64
LEFT_CHUNKS = 8
REL_CLIP = 256
D_FF = 2816
RMS_EPS = 1e-6
ADAM_LR = 0.001
ADAM_B1 = 0.9
ADAM_B2 = 0.999
ADAM_EPS = 1e-08
ADAM_WD = 0.01
ADAM_STEP = 10

N_DEV = 8
LANES = 128
VMEM_LIMIT = 56 * 1024 * 1024
NEG = -1e30
PAD_KEYS = LEFT_CHUNKS * CHUNK
BAND_TQ = 128
BAND_W = BAND_TQ + PAD_KEYS
TOEP_W = BAND_W + BAND_TQ

NN = (((1,), (0,)), ((), ()))
NT = (((1,), (1,)), ((), ()))
TN = (((0,), (0,)), ((), ()))


def _dot(a, b, dn):
    return lax.dot_general(a, b, dn, preferred_element_type=F32)


def _pick(dim, pref):
    if dim <= pref:
        return dim
    best = None
    for t in range(LANES, pref + 1, LANES):
        if dim % t == 0:
            best = t
    assert best is not None, (dim, pref)
    return best


def _params(sem):
    return pltpu.CompilerParams(dimension_semantics=sem, vmem_limit_bytes=VMEM_LIMIT)


def _mm(a, b, dims="nn", res=None, out_dtype=F32, name="mm"):
    if dims == "nn":
        (M, K), (K2, N) = a.shape, b.shape
    elif dims == "nt":
        (M, K), (N, K2) = a.shape, b.shape
    else:
        (K, M), (K2, N) = a.shape, b.shape
    assert K == K2, (a.shape, b.shape, dims)
    tm, tn, tk = _pick(M, 1024), _pick(N, 1152), _pick(K, 1024)
    nk = K // tk
    dn = {"nn": NN, "nt": NT, "tn": TN}[dims]
    has_res = res is not None

    def body(*refs):
        if has_res:
            a_ref, b_ref, r_ref, o_ref, acc = refs
        else:
            a_ref, b_ref, o_ref, acc = refs
        k = pl.program_id(2)

        @pl.when(k == 0)
        def _():
            acc[...] = jnp.zeros_like(acc)

        acc[...] += _dot(a_ref[...].astype(BF16), b_ref[...].astype(BF16), dn)

        @pl.when(k == nk - 1)
        def _():
            r = acc[...]
            if has_res:
                r = r + r_ref[...]
            o_ref[...] = r.astype(out_dtype)

    a_spec = (pl.BlockSpec((tk, tm), lambda i, j, k: (k, i)) if dims == "tn"
              else pl.BlockSpec((tm, tk), lambda i, j, k: (i, k)))
    b_spec = (pl.BlockSpec((tn, tk), lambda i, j, k: (j, k)) if dims == "nt"
              else pl.BlockSpec((tk, tn), lambda i, j, k: (k, j)))
    o_spec = pl.BlockSpec((tm, tn), lambda i, j, k: (i, j))
    in_specs = [a_spec, b_spec] + ([o_spec] if has_res else [])
    args = (a, b) + ((res,) if has_res else ())
    return pl.pallas_call(
        body, name=name, grid=(M // tm, N // tn, nk),
        in_specs=in_specs, out_specs=o_spec,
        out_shape=jax.ShapeDtypeStruct((M, N), out_dtype),
        scratch_shapes=[pltpu.VMEM((tm, tn), F32)],
        compiler_params=_params(("parallel", "parallel", "arbitrary")),
    )(*args)


def _rms_fwd(x, g, out_dtype=BF16, name="rms_fwd"):
    T, Fd = x.shape
    tm = _pick(T, 256)

    def body(x_ref, g_ref, o_ref):
        xv = x_ref[...]
        r = lax.rsqrt(jnp.mean(xv * xv, axis=-1, keepdims=True) + RMS_EPS)
        o_ref[...] = (xv * r * g_ref[...]).astype(out_dtype)

    return pl.pallas_call(
        body, name=name, grid=(T // tm,),
        in_specs=[pl.BlockSpec((tm, Fd), lambda i: (i, 0)), pl.BlockSpec((1, Fd), lambda i: (0, 0))],
        out_specs=pl.BlockSpec((tm, Fd), lambda i: (i, 0)),
        out_shape=jax.ShapeDtypeStruct((T, Fd), out_dtype),
        compiler_params=_params(("parallel",)),
    )(x, g)


def _rms_bwd(x, g, dy, dres=None, name="rms_bwd"):
    T, Fd = x.shape
    tm = _pick(T, 256)
    has_res = dres is not None

    def body(*refs):
        if has_res:
            x_ref, g_ref, dy_ref, r_ref, dx_ref, dxb_ref, dg_ref = refs
        else:
            x_ref, g_ref, dy_ref, dx_ref, dxb_ref, dg_ref = refs
        xv, dyv = x_ref[...], dy_ref[...]
        r = lax.rsqrt(jnp.mean(xv * xv, axis=-1, keepdims=True) + RMS_EPS)
        gdy = dyv * g_ref[...]
        dot = jnp.mean(xv * gdy, axis=-1, keepdims=True)
        dx = r * gdy - xv * (r * r * r * dot)
        if has_res:
            dx = dx + r_ref[...]
        dx_ref[...] = dx
        dxb_ref[...] = dx.astype(BF16)

        @pl.when(pl.program_id(0) == 0)
        def _():
            dg_ref[...] = jnp.zeros_like(dg_ref)

        dg_ref[...] += jnp.sum(dyv * xv * r, axis=0, keepdims=True)

    row = pl.BlockSpec((tm, Fd), lambda i: (i, 0))
    vec = pl.BlockSpec((1, Fd), lambda i: (0, 0))
    in_specs = [row, vec, row] + ([row] if has_res else [])
    args = (x, g, dy) + ((dres,) if has_res else ())
    return pl.pallas_call(
        body, name=name, grid=(T // tm,),
        in_specs=in_specs, out_specs=[row, row, vec],
        out_shape=[jax.ShapeDtypeStruct((T, Fd), F32), jax.ShapeDtypeStruct((T, Fd), BF16),
                   jax.ShapeDtypeStruct((1, Fd), F32)],
        compiler_params=_params(("arbitrary",)),
    )(*args)


def _loss_head(h, g, target, name="loss_head"):
    T, Fd = h.shape
    tm = _pick(T, 256)

    def body(h_ref, g_ref, t_ref, loss_ref, dh_ref, dhb_ref, dg_ref):
        xv = h_ref[...]
        r = lax.rsqrt(jnp.mean(xv * xv, axis=-1, keepdims=True) + RMS_EPS)
        diff = xv * r * g_ref[...] - t_ref[...]
        part = 0.5 * jnp.sum(jnp.mean(diff * diff, axis=-1, keepdims=True), axis=0, keepdims=True)
        dyv = diff * (1.0 / Fd)
        gdy = dyv * g_ref[...]
        dot = jnp.mean(xv * gdy, axis=-1, keepdims=True)
        dh = r * gdy - xv * (r * r * r * dot)
        dh_ref[...] = dh
        dhb_ref[...] = dh.astype(BF16)

        @pl.when(pl.program_id(0) == 0)
        def _():
            dg_ref[...] = jnp.zeros_like(dg_ref)
            loss_ref[...] = jnp.zeros_like(loss_ref)

        dg_ref[...] += jnp.sum(dyv * xv * r, axis=0, keepdims=True)
        loss_ref[...] += jnp.broadcast_to(part, loss_ref.shape)

    row = pl.BlockSpec((tm, Fd), lambda i: (i, 0))
    vec = pl.BlockSpec((1, Fd), lambda i: (0, 0))
    return pl.pallas_call(
        body, name=name, grid=(T // tm,),
        in_specs=[row, vec, row],
        out_specs=[pl.BlockSpec((1, LANES), lambda i: (0, 0)), row, row, vec],
        out_shape=[jax.ShapeDtypeStruct((1, LANES), F32), jax.ShapeDtypeStruct((T, Fd), F32),
                   jax.ShapeDtypeStruct((T, Fd), BF16), jax.ShapeDtypeStruct((1, Fd), F32)],
        compiler_params=_params(("arbitrary",)),
    )(h, g, target)


FFN_TF = 256


def _ffn_fwd(h, g, wg_t, wu_t, wd, name="ffn_fwd"):
    T, Dm = h.shape
    Fh = wd.shape[0]
    tm = _pick(T, 1024)
    nf = Fh // FFN_TF

    def body(h_ref, g_ref, wg_ref, wu_ref, wd_ref, o_ref, u_ref, a_ref, b_ref):
        j = pl.program_id(1)

        @pl.when(j == 0)
        def _():
            xv = h_ref[...]
            r = lax.rsqrt(jnp.mean(xv * xv, axis=-1, keepdims=True) + RMS_EPS)
            u_ref[...] = (xv * r * g_ref[...]).astype(BF16)
            o_ref[...] = xv

        u = u_ref[...]
        a = _dot(u, wg_ref[...], NT).astype(BF16)
        b = _dot(u, wu_ref[...], NT).astype(BF16)
        a_ref[...] = a
        b_ref[...] = b
        af = a.astype(F32)
        s = (af * jax.nn.sigmoid(af) * b.astype(F32)).astype(BF16)
        o_ref[...] += _dot(s, wd_ref[...], NN)

    row = pl.BlockSpec((tm, Dm), lambda i, j: (i, 0))
    wblk = pl.BlockSpec((FFN_TF, Dm), lambda i, j: (j, 0))
    ablk = pl.BlockSpec((tm, FFN_TF), lambda i, j: (i, j))
    return pl.pallas_call(
        body, name=name, grid=(T // tm, nf),
        in_specs=[row, pl.BlockSpec((1, Dm), lambda i, j: (0, 0)), wblk, wblk, wblk],
        out_specs=[row, row, ablk, ablk],
        out_shape=[jax.ShapeDtypeStruct((T, Dm), F32), jax.ShapeDtypeStruct((T, Dm), BF16),
                   jax.ShapeDtypeStruct((T, Fh), BF16), jax.ShapeDtypeStruct((T, Fh), BF16)],
        compiler_params=_params(("parallel", "arbitrary")),
    )(h, g, wg_t, wu_t, wd)


def _ffn_bwd(dh, u, a, b, wg_t, wu_t, wd, name="ffn_bwd"):
    T, Dm = dh.shape
    Fh = wd.shape[0]
    nf = Fh // FFN_TF
    once = pl.Buffered(1)

    def body(dh_ref, u_ref, a_ref, b_ref, wg_ref, wu_ref, wd_ref, du_ref, dwg_ref, dwu_ref, dwd_ref):
        j = pl.program_id(0)

        @pl.when(j == 0)
        def _():
            du_ref[...] = jnp.zeros_like(du_ref)

        ds = _dot(dh_ref[...], wd_ref[...], NT)
        af, bf = a_ref[...].astype(F32), b_ref[...].astype(F32)
        sig = jax.nn.sigmoid(af)
        sa = af * sig
        dwd_ref[...] = _dot((sa * bf).astype(BF16), dh_ref[...], TN).astype(BF16)
        dab = jnp.concatenate([(ds * bf * (sig * (1.0 + af * (1.0 - sig)))).astype(BF16),
                               (ds * sa).astype(BF16)], axis=1)
        dw = _dot(dab, u_ref[...], TN)
        dwg_ref[...] = dw[:FFN_TF].astype(BF16)
        dwu_ref[...] = dw[FFN_TF:].astype(BF16)
        du_ref[...] += _dot(dab, jnp.concatenate([wg_ref[...], wu_ref[...]], axis=0), NN)

    full = lambda: pl.BlockSpec((T, Dm), lambda j: (0, 0), pipeline_mode=once)
    wblk = pl.BlockSpec((FFN_TF, Dm), lambda j: (j, 0))
    ablk = pl.BlockSpec((T, FFN_TF), lambda j: (0, j))
    return pl.pallas_call(
        body, name=name, grid=(nf,),
        in_specs=[full(), full(), ablk, ablk, wblk, wblk, wblk],
        out_specs=[pl.BlockSpec((T, Dm), lambda j: (0, 0)), wblk, wblk, wblk],
        out_shape=[jax.ShapeDtypeStruct((T, Dm), F32)] + [jax.ShapeDtypeStruct((Fh, Dm), BF16)] * 3,
        compiler_params=_params(("arbitrary",)),
    )(dh, u, a, b, wg_t, wu_t, wd)


def _rope(x, cos_t, sin_t, col0, ncols, out_dtype, name="rope"):
    T = x.shape[0]
    wt = cos_t.shape[1]
    tm = _pick(T, 256)
    nb = ncols * LANES // wt
    half = MLA_ROPE // 2

    def body(x_ref, c_ref, s_ref, o_ref):
        xv = x_ref[...].astype(F32)
        lane = lax.broadcasted_iota(jnp.int32, xv.shape, 1)
        first = (lane & (MLA_ROPE - 1)) < half
        swapped = jnp.where(first, pltpu.roll(xv, wt - half, 1), pltpu.roll(xv, half, 1))
        o_ref[...] = (xv * c_ref[...] + swapped * s_ref[...]).astype(out_dtype)

    off = col0 * LANES // wt
    return pl.pallas_call(
        body, name=name, grid=(T // tm, nb),
        in_specs=[pl.BlockSpec((tm, wt), lambda i, j: (i, j + off)),
                  pl.BlockSpec((tm, wt), lambda i, j: (i, 0)),
                  pl.BlockSpec((tm, wt), lambda i, j: (i, 0))],
        out_specs=pl.BlockSpec((tm, wt), lambda i, j: (i, j)),
        out_shape=jax.ShapeDtypeStruct((T, ncols * LANES), out_dtype),
        compiler_params=_params(("parallel", "parallel")),
    )(x, cos_t, sin_t)


ATT_TQ = 512
ATT_TK = 256


def _mla_masks(shape):
    lane = lax.broadcasted_iota(jnp.int32, shape, 1)
    m0 = (lane < 64) | ((lane >= 128) & (lane < 160))
    m1 = ((lane >= 64) & (lane < 128)) | ((lane >= 160) & (lane < 192))
    return m0, m1


def _by_twos(n, step, carry):
    carry = lax.fori_loop(0, n // 2, lambda i, c: step(2 * i + 1, step(2 * i, c)), carry)
    return lax.fori_loop(0, n % 2, lambda _, c: step(n - 1, c), carry)


def _chunk_ok(tq, tk, d):
    row = lax.broadcasted_iota(jnp.int32, (tq, tk), 0)
    col = lax.broadcasted_iota(jnp.int32, (tq, tk), 1) + d * tk
    return jnp.concatenate([(col >> 6) <= (row >> 6)] * 2, axis=0)


def _rotate(x, cos_t, sin_t):
    half = MLA_ROPE // 2
    lane = lax.broadcasted_iota(jnp.int32, x.shape, 1)
    first = (lane & (MLA_ROPE - 1)) < half
    swapped = jnp.where(first, pltpu.roll(x, x.shape[1] - half, 1), pltpu.roll(x, half, 1))
    return x * cos_t + swapped * sin_t


def _mla_fwd(q, cos_q, sin_q, kv, kr, name="mla_fwd"):
    T = q.shape[0]
    tq, tk = _pick(T, ATT_TQ), _pick(T, ATT_TK)
    nd = tq // tk
    npair = MLA_HEADS // 2
    scale = (MLA_NOPE + MLA_ROPE) ** -0.5

    def body(q_ref, c_ref, s_ref, kn_ref, v_ref, kr_ref, o_ref, lse_ref):
        m_idx = pl.program_id(1)
        qv = _rotate(q_ref[...], c_ref[...], s_ref[...]).astype(BF16)
        m0, m1 = _mla_masks(qv.shape)
        qs = jnp.concatenate([jnp.where(m0, qv, 0), jnp.where(m1, qv, 0)], axis=0).astype(BF16)

        def block(kb, carry, ok):
            ks = pl.ds(pl.multiple_of(kb * tk, tk), tk)
            kcat = jnp.concatenate([kn_ref[ks, :], kr_ref[ks, :]], axis=1)
            mx, l, acc = carry
            s = _dot(qs, kcat, NT) * scale
            if ok is not None:
                s = jnp.where(ok, s, NEG)
            mn = jnp.maximum(mx, jnp.max(s, axis=-1, keepdims=True))
            alpha = jnp.exp(mx - mn)
            p = jnp.exp(s - mn)
            return (mn, alpha * l + jnp.sum(p, axis=-1, keepdims=True),
                    alpha * acc + _dot(p.astype(BF16), v_ref[ks, :], NN))

        init = (jnp.full((2 * tq, 1), NEG, F32), jnp.zeros((2 * tq, 1), F32), jnp.zeros((2 * tq, LANES), F32))
        res = init
        for d in range(nd):
            res = block(m_idx * nd + d, res, _chunk_ok(tq, tk, d))
        mx, l, acc = _by_twos(m_idx * nd, lambda kb, c: block(kb, c, None), res)
        h0 = lax.broadcasted_iota(jnp.int32, (tq, LANES), 1) < 64
        o_ref[...] = _two_heads(acc * (1.0 / l), h0).astype(o_ref.dtype)
        lse_ref[...] = _two_heads(jnp.broadcast_to(mx + jnp.log(l), (2 * tq, LANES)), h0)

    full = lambda col: pl.BlockSpec((T, LANES), col)
    table = pl.BlockSpec((tq, 2 * LANES), lambda p, m: (m, 0))
    return pl.pallas_call(
        body, name=name, grid=(npair, T // tq),
        in_specs=[pl.BlockSpec((tq, 2 * LANES), lambda p, m: (m, p)), table, table,
                  full(lambda p, m: (0, p)), full(lambda p, m: (0, npair + p)), full(lambda p, m: (0, 0))],
        out_specs=[pl.BlockSpec((tq, LANES), lambda p, m: (m, p)),
                   pl.BlockSpec((tq, LANES), lambda p, m: (m, p))],
        out_shape=[jax.ShapeDtypeStruct((T, npair * LANES), BF16),
                   jax.ShapeDtypeStruct((T, npair * LANES), F32)],
        compiler_params=_params(("parallel", "arbitrary")),
    )(q, cos_q, sin_q, kv, kv, kr)


def _mla_bwd(q, cos_q, sin_q, kv, kr, o, lse, do, do_col0, name="mla_bwd"):
    T = q.shape[0]
    tq, tk = _pick(T, ATT_TQ), _pick(T, ATT_TK)
    nd = tq // tk
    npair = MLA_HEADS // 2
    scale = (MLA_NOPE + MLA_ROPE) ** -0.5

    def body(q_ref, c_ref, s_ref, kn_ref, v_ref, kr_ref, o_ref, lse_ref, do_ref, dq_ref, dkn_ref, dv_ref, dkr_ref,
             dkn_acc, dv_acc):
        p_idx, m_idx = pl.program_id(0), pl.program_id(1)

        @pl.when(m_idx == 0)
        def _():
            dkn_acc[...] = jnp.zeros_like(dkn_acc)
            dv_acc[...] = jnp.zeros_like(dv_acc)

        @pl.when((m_idx == 0) & (p_idx == 0))
        def _():
            dkr_ref[...] = jnp.zeros_like(dkr_ref)

        qv = _rotate(q_ref[...], c_ref[...], s_ref[...]).astype(BF16)
        m0, m1 = _mla_masks(qv.shape)
        qs = jnp.concatenate([jnp.where(m0, qv, 0), jnp.where(m1, qv, 0)], axis=0).astype(BF16)
        dov = do_ref[...].astype(F32)
        h0 = lax.broadcasted_iota(jnp.int32, (tq, LANES), 1) < 64
        dos32 = jnp.concatenate([jnp.where(h0, dov, 0.0), jnp.where(h0, 0.0, dov)], axis=0)
        ov = o_ref[...].astype(F32)
        delta = jnp.sum(dos32 * jnp.concatenate([ov, ov], axis=0), axis=-1, keepdims=True)
        dos = dos32.astype(BF16)
        lsev = lse_ref[...]
        lse = jnp.concatenate([lsev[:, 0:1], lsev[:, 64:65]], axis=0)

        def block(kb, dq, ok):
            ks = pl.ds(pl.multiple_of(kb * tk, tk), tk)
            kcat = jnp.concatenate([kn_ref[ks, :], kr_ref[ks, :]], axis=1)
            vv = v_ref[ks, :]
            p = jnp.exp(_dot(qs, kcat, NT) * scale - lse)
            if ok is not None:
                p = jnp.where(ok, p, 0.0)
            ds = (p * (_dot(dos, vv, NT) - delta) * scale).astype(BF16)
            dkc = _dot(ds, qs, TN)
            dkn_acc[ks, :] += dkc[:, :LANES]
            dkr_ref[ks, :] += dkc[:, LANES:]
            dv_acc[ks, :] += _dot(p.astype(BF16), dos, TN)
            return dq + _dot(ds, kcat, NN)

        dq = jnp.zeros((2 * tq, 2 * LANES), F32)
        for d in range(nd):
            dq = block(m_idx * nd + d, dq, _chunk_ok(tq, tk, d))
        dq = _by_twos(m_idx * nd, lambda kb, c: block(kb, c, None), dq)
        dq_ref[...] = _rotate(jnp.where(m0, dq[:tq], jnp.where(m1, dq[tq:], 0.0)), c_ref[...],
                              -s_ref[...]).astype(BF16)

        @pl.when(m_idx == T // tq - 1)
        def _():
            dkn_ref[...] = dkn_acc[...].astype(BF16)
            dv_ref[...] = dv_acc[...].astype(BF16)

    full = lambda col: pl.BlockSpec((T, LANES), col)
    blk = lambda col: pl.BlockSpec((tq, LANES), col)
    table = pl.BlockSpec((tq, 2 * LANES), lambda p, m: (m, 0))
    return pl.pallas_call(
        body, name=name, grid=(npair, T // tq),
        in_specs=[pl.BlockSpec((tq, 2 * LANES), lambda p, m: (m, p)), table, table,
                  full(lambda p, m: (0, p)), full(lambda p, m: (0, npair + p)), full(lambda p, m: (0, 0)),
                  blk(lambda p, m: (m, p)), blk(lambda p, m: (m, p)),
                  blk(lambda p, m: (m, do_col0 + p))],
        out_specs=[pl.BlockSpec((tq, 2 * LANES), lambda p, m: (m, p)),
                   full(lambda p, m: (0, p)), full(lambda p, m: (0, p)), full(lambda p, m: (0, 0))],
        out_shape=[jax.ShapeDtypeStruct((T, npair * 2 * LANES), BF16),
                   jax.ShapeDtypeStruct((T, npair * LANES), BF16),
                   jax.ShapeDtypeStruct((T, npair * LANES), BF16),
                   jax.ShapeDtypeStruct((T, LANES), F32)],
        scratch_shapes=[pltpu.VMEM((T, LANES), F32)] * 2,
        compiler_params=_params(("arbitrary", "arbitrary")),
    )(q, cos_q, sin_q, kv, kv, kr, o, lse, do)


def _split_dot(x, tri):
    hi = x.astype(BF16)
    lo = (x - hi.astype(F32)).astype(BF16)
    both = _dot(jnp.concatenate([hi, lo], axis=0), tri, NN)
    return both[:x.shape[0]] + both[x.shape[0]:]


def _sb_terms(qh, kk, before):
    z = _dot(qh, kk, NT)
    sp = jnp.maximum(z, 0.0) + jnp.log(1.0 + jnp.exp(-jnp.abs(z)))
    lk = -sp if before is None else jnp.where(before, -sp, 0.0)
    return z, sp, lk


def _sb_setup(q_ref, tq, tk, scale):
    qv = (q_ref[...].astype(F32) * scale).astype(BF16)
    lane = lax.broadcasted_iota(jnp.int32, (tq, LANES), 1)
    h0 = lane < 64
    qs = jnp.concatenate([jnp.where(h0, qv, 0), jnp.where(h0, 0, qv)], axis=0).astype(BF16)
    row = lax.broadcasted_iota(jnp.int32, (tk, tk), 0)
    col = lax.broadcasted_iota(jnp.int32, (tk, tk), 1)
    return qs, h0, row, col


def _sb_before(tq, tk, d):
    row = lax.broadcasted_iota(jnp.int32, (tq, tk), 0)
    col = lax.broadcasted_iota(jnp.int32, (tq, tk), 1) + d * tk
    return jnp.concatenate([col < row] * 2, axis=0)


def _two_heads(x, h0):
    tq = x.shape[0] // 2
    return jnp.where(h0, x[:tq], x[tq:])


def _sb_fwd(qkv, col0, name="sb_fwd"):
    T = qkv.shape[0]
    tq, tk = _pick(T, ATT_TQ), _pick(T, ATT_TK)
    nd = tq // tk
    npair = SB_HEADS // 2
    scale = SB_DIM ** -0.5

    def body(q_ref, k_ref, v_ref, o_ref, o32_ref):
        m_idx = pl.program_id(1)
        qs, h0, row, col = _sb_setup(q_ref, tq, tk, scale)
        later = (row > col).astype(BF16)

        def block(kb, carry, before):
            ks = pl.ds(pl.multiple_of(kb * tk, tk), tk)
            c, acc = carry
            z, sp, lk = _sb_terms(qs, k_ref[ks, :].astype(BF16), before)
            w = jnp.exp((z - sp) + _split_dot(lk, later) + c)
            if before is not None:
                w = jnp.where(before, w, 0.0)
            return (c + jnp.sum(lk, axis=-1, keepdims=True),
                    acc + _dot(w.astype(BF16), v_ref[ks, :].astype(BF16), NN))

        init = (jnp.zeros((2 * tq, 1), F32), jnp.zeros((2 * tq, LANES), F32))
        res = init
        for d in reversed(range(nd)):
            res = block(m_idx * nd + d, res, _sb_before(tq, tk, d))
        res = _by_twos(m_idx * nd, lambda i, c: block(m_idx * nd - 1 - i, c, None), res)
        o = _two_heads(res[1], h0)
        o_ref[...] = o.astype(o_ref.dtype)
        o32_ref[...] = o

    full = lambda col: pl.BlockSpec((T, LANES), col)
    blk = pl.BlockSpec((tq, LANES), lambda p, m: (m, p))
    return pl.pallas_call(
        body, name=name, grid=(npair, T // tq),
        in_specs=[pl.BlockSpec((tq, LANES), lambda p, m: (m, col0 + p)),
                  full(lambda p, m: (0, col0 + npair + p)), full(lambda p, m: (0, col0 + 2 * npair + p))],
        out_specs=[blk, blk],
        out_shape=[jax.ShapeDtypeStruct((T, npair * LANES), BF16), jax.ShapeDtypeStruct((T, npair * LANES), F32)],
        compiler_params=_params(("parallel", "arbitrary")),
    )(qkv, qkv, qkv)


def _sb_bwd(qkv, col0, o32, do, do_col0, dep, name="sb_bwd"):
    T = qkv.shape[0]
    tq, tk = _pick(T, ATT_TQ), _pick(T, ATT_TK)
    nd = tq // tk
    npair = SB_HEADS // 2
    scale = SB_DIM ** -0.5

    def body(q_ref, k_ref, v_ref, o_ref, do_ref, dep_ref, dq_ref, dk_ref, dv_ref, dk_acc, dv_acc):
        m_idx = pl.program_id(1)

        @pl.when(m_idx == 0)
        def _():
            dk_acc[...] = jnp.zeros_like(dk_acc)
            dv_acc[...] = jnp.zeros_like(dv_acc)

        qs, h0, row, col = _sb_setup(q_ref, tq, tk, scale)
        dov = do_ref[...].astype(F32)
        dos = jnp.concatenate([jnp.where(h0, dov, 0.0), jnp.where(h0, 0.0, dov)], axis=0).astype(BF16)
        ov = o_ref[...]
        etot = jnp.sum(dos.astype(F32) * jnp.concatenate([ov, ov], axis=0), axis=-1, keepdims=True)
        later = (row > col).astype(BF16)
        from_here = (row >= col).astype(BF16)

        def block(kb, carry, before):
            ks = pl.ds(pl.multiple_of(kb * tk, tk), tk)
            kk = k_ref[ks, :].astype(BF16)
            vv = v_ref[ks, :].astype(BF16)
            c, es, dqa = carry
            z, sp, lk = _sb_terms(qs, kk, before)
            w = jnp.exp((z - sp) + _split_dot(lk, later) + c)
            if before is not None:
                w = jnp.where(before, w, 0.0)
            wb = w.astype(BF16)
            e = wb.astype(F32) * _dot(dos, vv, NT)
            prev = etot - (_split_dot(e, from_here) + es)
            sig_neg = jnp.exp(-sp)
            dz = e * sig_neg - (1.0 - sig_neg) * prev
            if before is not None:
                dz = jnp.where(before, dz, 0.0)
            dzb = dz.astype(BF16)
            dk_acc[ks, :] += _dot(dzb, qs, TN)
            dv_acc[ks, :] += _dot(wb, dos, TN)
            return (c + jnp.sum(lk, axis=-1, keepdims=True), es + jnp.sum(e, axis=-1, keepdims=True),
                    dqa + _dot(dzb, kk, NN))

        init = (jnp.zeros((2 * tq, 1), F32), jnp.zeros((2 * tq, 1), F32), jnp.zeros((2 * tq, LANES), F32))
        res = init
        for d in reversed(range(nd)):
            res = block(m_idx * nd + d, res, _sb_before(tq, tk, d))
        res = _by_twos(m_idx * nd, lambda i, c: block(m_idx * nd - 1 - i, c, None), res)
        dq_ref[...] = (_two_heads(res[2], h0) * scale).astype(BF16)

        @pl.when(m_idx == T // tq - 1)
        def _():
            dk_ref[...] = dk_acc[...].astype(BF16)
            dv_ref[...] = dv_acc[...].astype(BF16)

    full = lambda col: pl.BlockSpec((T, LANES), col)
    blk = lambda col: pl.BlockSpec((tq, LANES), col)
    return pl.pallas_call(
        body, name=name, grid=(npair, T // tq),
        in_specs=[blk(lambda p, m: (m, col0 + p)),
                  full(lambda p, m: (0, col0 + npair + p)), full(lambda p, m: (0, col0 + 2 * npair + p)),
                  blk(lambda p, m: (m, p)), blk(lambda p, m: (m, do_col0 + p)),
                  pl.BlockSpec((8, LANES), lambda p, m: (0, 0))],
        out_specs=[blk(lambda p, m: (m, p)), full(lambda p, m: (0, p)), full(lambda p, m: (0, p))],
        out_shape=[jax.ShapeDtypeStruct((T, npair * LANES), BF16)] * 3,
        scratch_shapes=[pltpu.VMEM((T, LANES), F32)] * 2,
        compiler_params=_params(("arbitrary", "arbitrary")),
    )(qkv, qkv, qkv, o32, do, dep)


def _band_in_window():
    cq = lax.broadcasted_iota(jnp.int32, (BAND_TQ, BAND_W), 0) >> 6
    ckp = lax.broadcasted_iota(jnp.int32, (BAND_TQ, BAND_W), 1) >> 6
    return (ckp >= cq) & (ckp <= cq + LEFT_CHUNKS)


def _band_real(m_idx):
    j = lax.broadcasted_iota(jnp.int32, (BAND_TQ, BAND_W), 1)
    return j >= PAD_KEYS - m_idx * BAND_TQ


def _band_probs(qh, kw, bias, real, scale):
    s = jnp.where(real, _dot(qh, kw, NT) * scale + bias, NEG)
    e = jnp.exp(s - jnp.max(s, axis=-1, keepdims=True))
    return e * (1.0 / jnp.sum(e, axis=-1, keepdims=True))


BAND_SUB = 4


def _band_fwd(qkv, k_pad, v_pad, bias_w, name="band_fwd"):
    T = qkv.shape[0]
    npair = C_HEADS // 2
    scale = C_DIM ** -0.5
    rows = BAND_SUB * BAND_TQ

    def body(q_ref, k_ref, v_ref, b_ref, o_ref):
        lane = lax.broadcasted_iota(jnp.int32, (BAND_TQ, LANES), 1)
        h0 = lane < 64
        bias = jnp.concatenate([b_ref[0], b_ref[1]], axis=0)
        for sub in range(BAND_SUB):
            m_idx = pl.program_id(1) * BAND_SUB + sub
            win = pl.ds(pl.multiple_of(m_idx * BAND_TQ, BAND_TQ), BAND_W)
            kw, vw = k_ref[win, :], v_ref[win, :]
            qv = q_ref[sub * BAND_TQ:(sub + 1) * BAND_TQ, :]
            qs = jnp.concatenate([jnp.where(h0, qv, 0), jnp.where(h0, 0, qv)], axis=0).astype(BF16)
            p = _band_probs(qs, kw, bias, jnp.concatenate([_band_real(m_idx)] * 2, axis=0), scale)
            o = _two_heads(_dot(p.astype(BF16), vw, NN), h0)
            o_ref[sub * BAND_TQ:(sub + 1) * BAND_TQ, :] = o.astype(o_ref.dtype)

    Tp = T + PAD_KEYS
    return pl.pallas_call(
        body, name=name, grid=(npair, T // rows),
        in_specs=[pl.BlockSpec((rows, LANES), lambda p, m: (m, p)),
                  pl.BlockSpec((Tp, LANES), lambda p, m: (0, p)),
                  pl.BlockSpec((Tp, LANES), lambda p, m: (0, p)),
                  pl.BlockSpec((2, BAND_TQ, BAND_W), lambda p, m: (p, 0, 0))],
        out_specs=pl.BlockSpec((rows, LANES), lambda p, m: (m, p)),
        out_shape=jax.ShapeDtypeStruct((T, npair * LANES), BF16),
        compiler_params=_params(("parallel", "arbitrary")),
    )(qkv, k_pad, v_pad, bias_w)


def _band_bwd(qkv, k_pad, v_pad, bias_w, do, name="band_bwd"):
    T = qkv.shape[0]
    npair = C_HEADS // 2
    scale = C_DIM ** -0.5

    rows = BAND_SUB * BAND_TQ

    def body(q_ref, k_ref, v_ref, b_ref, do_ref, dq_ref, dk_ref, dv_ref, db_ref, dk_acc, dv_acc):
        @pl.when(pl.program_id(1) == 0)
        def _():
            dk_acc[...] = jnp.zeros_like(dk_acc)
            dv_acc[...] = jnp.zeros_like(dv_acc)
            db_ref[...] = jnp.zeros_like(db_ref)

        lane = lax.broadcasted_iota(jnp.int32, (BAND_TQ, LANES), 1)
        h0 = lane < 64
        dbs = jnp.zeros((2 * BAND_TQ, BAND_W), F32)
        bias = jnp.concatenate([b_ref[0], b_ref[1]], axis=0)
        for sub in range(BAND_SUB):
            m_idx = pl.program_id(1) * BAND_SUB + sub
            win = pl.ds(pl.multiple_of(m_idx * BAND_TQ, BAND_TQ), BAND_W)
            kw, vw = k_ref[win, :], v_ref[win, :]
            qv = q_ref[sub * BAND_TQ:(sub + 1) * BAND_TQ, :]
            dov = do_ref[sub * BAND_TQ:(sub + 1) * BAND_TQ, :].astype(F32)
            qs = jnp.concatenate([jnp.where(h0, qv, 0), jnp.where(h0, 0, qv)], axis=0).astype(BF16)
            dos = jnp.concatenate([jnp.where(h0, dov, 0.0), jnp.where(h0, 0.0, dov)], axis=0).astype(BF16)
            p = _band_probs(qs, kw, bias, jnp.concatenate([_band_real(m_idx)] * 2, axis=0), scale)
            dp = _dot(dos, vw, NT)
            dsb = p * (dp - jnp.sum(p * dp, axis=-1, keepdims=True))
            dbs = dbs + dsb
            dsq = (dsb * scale).astype(BF16)
            dq_ref[sub * BAND_TQ:(sub + 1) * BAND_TQ, :] = _two_heads(_dot(dsq, kw, NN), h0).astype(BF16)
            dk_acc[win, :] += _dot(dsq, qs, TN)
            dv_acc[win, :] += _dot(p.astype(BF16), dos, TN)
        db_ref[0] += dbs[:BAND_TQ]
        db_ref[1] += dbs[BAND_TQ:]

        @pl.when(pl.program_id(1) == T // rows - 1)
        def _():
            dk_ref[...] = dk_acc[...].astype(BF16)
            dv_ref[...] = dv_acc[...].astype(BF16)

    Tp = T + PAD_KEYS
    blk = lambda col: pl.BlockSpec((rows, LANES), col)
    full = pl.BlockSpec((Tp, LANES), lambda p, m: (0, p))
    bias = pl.BlockSpec((2, BAND_TQ, BAND_W), lambda p, m: (p, 0, 0))
    return pl.pallas_call(
        body, name=name, grid=(npair, T // rows),
        in_specs=[blk(lambda p, m: (m, p)), full, full, bias, blk(lambda p, m: (m, p))],
        out_specs=[blk(lambda p, m: (m, p)), full, full, bias],
        out_shape=[jax.ShapeDtypeStruct((T, npair * LANES), BF16),
                   jax.ShapeDtypeStruct((Tp, npair * LANES), BF16),
                   jax.ShapeDtypeStruct((Tp, npair * LANES), BF16),
                   jax.ShapeDtypeStruct((C_HEADS, BAND_TQ, BAND_W), F32)],
        scratch_shapes=[pltpu.VMEM((Tp, LANES), F32)] * 2,
        compiler_params=_params(("arbitrary", "arbitrary")),
    )(qkv, k_pad, v_pad, bias_w, do)


def _skew_bits(x, left):
    w = x.shape[1]
    row = lax.broadcasted_iota(jnp.int32, x.shape, 0)
    for b in range(BAND_TQ.bit_length() - 1):
        amt = (w - (1 << b)) if left else (1 << b)
        x = jnp.where((row >> b) & 1 == 1, pltpu.roll(x, amt, 1), x)
    return x


def _toeplitz(diag, name="toeplitz"):
    H = diag.shape[0]

    def body(d_ref, o_ref):
        x = jnp.broadcast_to(d_ref[0], (BAND_TQ, TOEP_W))
        o_ref[0] = jnp.where(_band_in_window(), _skew_bits(x, left=False)[:, BAND_TQ:], NEG)

    return pl.pallas_call(
        body, name=name, grid=(H,),
        in_specs=[pl.BlockSpec((1, 1, TOEP_W), lambda h: (h, 0, 0))],
        out_specs=pl.BlockSpec((1, BAND_TQ, BAND_W), lambda h: (h, 0, 0)),
        out_shape=jax.ShapeDtypeStruct((H, BAND_TQ, BAND_W), F32),
        compiler_params=_params(("parallel",)),
    )(diag.reshape(H, 1, TOEP_W))


def _toeplitz_bwd(dbias, name="toeplitz_bwd"):
    H = dbias.shape[0]

    def body(d_ref, o_ref):
        x = jnp.concatenate([jnp.zeros((BAND_TQ, BAND_TQ), F32), d_ref[0]], axis=1)
        o_ref[0] = jnp.sum(_skew_bits(x, left=True), axis=0, keepdims=True)

    return pl.pallas_call(
        body, name=name, grid=(H,),
        in_specs=[pl.BlockSpec((1, BAND_TQ, BAND_W), lambda h: (h, 0, 0))],
        out_specs=pl.BlockSpec((1, 1, TOEP_W), lambda h: (h, 0, 0)),
        out_shape=jax.ShapeDtypeStruct((H, 1, TOEP_W), F32),
        compiler_params=_params(("parallel",)),
    )(dbias).reshape(H, TOEP_W)


_HBM = pl.BlockSpec(memory_space=pltpu.HBM)
_SEM = pl.BlockSpec(memory_space=pltpu.SEMAPHORE)
_EFFECT = pltpu.SideEffectType.DATAFLOW_SIDE_EFFECTING


def _peers():
    x, y, c = lax.axis_index("x"), lax.axis_index("y"), lax.axis_index("c")
    out = []
    for k in range(1, N_DEV):
        peer = (1 - x if (k >> 2) & 1 else x, 1 - y if (k >> 1) & 1 else y, 1 - c if k & 1 else c)
        out.append((peer, 4 * peer[0] + 2 * peer[1] + peer[2]))
    return 4 * x + 2 * y + c, out


def _split_copies(ins, lands, scatter, send_sem, recv_sem, arriving):
    me, peers = _peers()
    out = []
    for a in range(len(ins)):
        for peer, idx in peers:
            out.append(pltpu.make_async_remote_copy(
                src_ref=ins[a].at[idx] if scatter[a] else ins[a],
                dst_ref=lands[a].at[idx if arriving else me], send_sem=send_sem, recv_sem=recv_sem,
                device_id=peer, device_id_type=pl.DeviceIdType.MESH))
    return out


def _landing_zones(arrays, scatter):
    return [lax.empty((N_DEV,) + (a.shape[1:] if s else a.shape), a.dtype) for a, s in zip(arrays, scatter)]


def _place_own(arrays, scatter, name):
    n = len(arrays)
    lands = _landing_zones(arrays, scatter)
    me = (4 * lax.axis_index("x") + 2 * lax.axis_index("y") + lax.axis_index("c")).astype(jnp.int32).reshape(1)

    def body(me_ref, *refs):
        for a in range(n):
            refs[2 * n + a][...] = refs[a][...].reshape(refs[2 * n + a].shape)

    def row_spec(shape):
        zeros = (0,) * (len(shape) - 1)
        return pl.BlockSpec((1,) + tuple(shape[1:]), lambda i, me_ref: (me_ref[0],) + zeros)

    in_specs = [row_spec(a.shape) if s else pl.BlockSpec(a.shape, lambda i, me_ref, nd=a.ndim: (0,) * nd)
                for a, s in zip(arrays, scatter)]
    return pl.pallas_call(
        body, name=name,
        out_shape=[jax.ShapeDtypeStruct(l.shape, l.dtype) for l in lands],
        grid_spec=pltpu.PrefetchScalarGridSpec(
            num_scalar_prefetch=1, grid=(1,),
            in_specs=in_specs + [pl.BlockSpec(memory_space=pl.ANY)] * n,
            out_specs=[row_spec(l.shape) for l in lands]),
        input_output_aliases={1 + n + i: i for i in range(n)},
        compiler_params=_params(("arbitrary",)),
    )(me, *arrays, *lands)


def _exchange_start(arrays, scatter, after, name):
    n = len(arrays)
    lands = list(_place_own(arrays, scatter, name=name.replace("_start_", "_own_")))

    def body(*refs):
        ins, lnd = refs[:n], refs[n:2 * n]
        send_sem, recv_sem = refs[2 * n + 1:2 * n + 3]
        token = refs[-1]
        for cp in _split_copies(ins, lnd, scatter, send_sem, recv_sem, arriving=False):
            cp.start()
        token[...] = jnp.zeros_like(token)

    hbm = lambda a: pltpu.HBM(a.shape, a.dtype)
    out = pl.pallas_call(
        body, name=name,
        out_shape=(pltpu.SemaphoreType.DMA(()), pltpu.SemaphoreType.DMA(()),
                   *[hbm(a) for a in arrays], *[hbm(a) for a in lands],
                   jax.ShapeDtypeStruct((8, LANES), F32)),
        in_specs=[_HBM] * (2 * n) + [pl.BlockSpec(memory_space=pl.ANY)],
        out_specs=(_SEM, _SEM, *([_HBM] * (2 * n)), pl.BlockSpec(memory_space=pltpu.VMEM)),
        input_output_aliases={i: 2 + i for i in range(2 * n)},
        compiler_params=pltpu.CompilerParams(has_side_effects=_EFFECT),
    )(*[pltpu.with_memory_space_constraint(a, pltpu.HBM) for a in list(arrays) + lands], after)
    return (out[0], out[1], list(out[2:2 + n]), list(out[2 + n:2 + 2 * n]), tuple(scatter)), out[-1]


def _exchange_wait(handle, after, name):
    send_sem, recv_sem, ins, lands, scatter = handle
    n = len(ins)
    after = after if isinstance(after, tuple) else (after,)

    def body(*refs):
        i_ref, l_ref = refs[:n], refs[n:2 * n]
        s_sem, r_sem = refs[2 * n:2 * n + 2]
        for cp in _split_copies(i_ref, l_ref, scatter, s_sem, r_sem, arriving=False):
            cp.wait_send()
        for cp in _split_copies(i_ref, l_ref, scatter, s_sem, r_sem, arriving=True):
            cp.wait_recv()

    hbm = lambda a: pltpu.HBM(a.shape, a.dtype)
    out = pl.pallas_call(
        body, name=name,
        out_shape=tuple(hbm(a) for a in ins + lands),
        in_specs=[_HBM] * (2 * n) + [_SEM, _SEM] + [pl.BlockSpec(memory_space=pl.ANY)] * len(after),
        out_specs=tuple([_HBM] * (2 * n)),
        input_output_aliases={i: i for i in range(2 * n)},
        compiler_params=pltpu.CompilerParams(has_side_effects=_EFFECT),
    )(*ins, *lands, send_sem, recv_sem, *after)
    return list(out[n:])


def _adamw(w, parts, m, v, name="adamw"):
    R, C = w.shape
    L = len(parts)
    rl = R // L
    tr = max([t for t in range(16, 65, 16) if rl % t == 0], default=rl)
    nb = rl // tr
    c1 = 1.0 - ADAM_B1 ** ADAM_STEP
    c2 = 1.0 - ADAM_B2 ** ADAM_STEP

    def body(*refs):
        w_ref, p_refs, (m_ref, v_ref, g_ref, d_ref, nm_ref, nv_ref) = refs[0], refs[1:1 + L], refs[1 + L:]
        g = None
        for j, p_ref in enumerate(p_refs):
            gj = p_ref[0].astype(F32)
            for i in range(1, N_DEV):
                gj = gj + p_ref[i].astype(F32)
            g = gj if g is None else jnp.where(pl.program_id(0) == j, gj, g)
        nm = ADAM_B1 * m_ref[...] + (1.0 - ADAM_B1) * g
        nv = ADAM_B2 * v_ref[...] + (1.0 - ADAM_B2) * (g * g)
        g_ref[...] = g
        nm_ref[...] = nm
        nv_ref[...] = nv
        d_ref[...] = -ADAM_LR * ((nm / c1) / (jnp.sqrt(nv / c2) + ADAM_EPS) + ADAM_WD * w_ref[...])

    blk = pl.BlockSpec((tr, C), lambda l, i: (l * nb + i, 0))
    part = lambda j: pl.BlockSpec((N_DEV, tr, C), lambda l, i: (0, jnp.where(l == j, i, 0), 0))
    return pl.pallas_call(
        body, name=name, grid=(L, nb),
        in_specs=[blk] + [part(j) for j in range(L)] + [blk, blk],
        out_specs=[blk] * 4,
        out_shape=[jax.ShapeDtypeStruct((R, C), F32)] * 4,
        compiler_params=_params(("arbitrary", "arbitrary")),
    )(w, *parts, m, v)


_O1 = Q_LORA
_O2 = _O1 + KV_LORA
_O3 = _O2 + MLA_ROPE
_NB = SB_HEADS * SB_DIM
IN_W = _O2 + LANES + 3 * _NB
COL_KR = _O2 // LANES
COL_SB = COL_KR + 1


def _w_in_local(w):
    kr = w[_O2:_O3]
    pad = jnp.zeros((LANES - 2 * MLA_ROPE, w.shape[1]), w.dtype)
    return jnp.concatenate([w[:_O2], kr, kr, pad, w[_O3:]], axis=0)


def _w_in_grad(g):
    kr = (g[_O2:_O2 + MLA_ROPE].astype(F32) + g[_O2 + MLA_ROPE:_O2 + 2 * MLA_ROPE].astype(F32)).astype(g.dtype)
    return jnp.concatenate([g[:_O2], kr, g[_O2 + LANES:]], axis=0)


def _w_uq_local(w):
    w3 = w.reshape(MLA_HEADS // 2, 2, MLA_NOPE + MLA_ROPE, w.shape[1])
    nope = w3[:, :, :MLA_NOPE].reshape(MLA_HEADS // 2, 2 * MLA_NOPE, w.shape[1])
    rope = w3[:, :, MLA_NOPE:].reshape(MLA_HEADS // 2, 2 * MLA_ROPE, w.shape[1])
    pad = jnp.zeros((MLA_HEADS // 2, LANES - 2 * MLA_ROPE, w.shape[1]), w.dtype)
    return jnp.concatenate([nope, rope, pad], axis=1).reshape(-1, w.shape[1])


def _w_uq_grad(g):
    g3 = g.reshape(MLA_HEADS // 2, 2 * LANES, g.shape[1])
    nope = g3[:, :2 * MLA_NOPE].reshape(MLA_HEADS // 2, 2, MLA_NOPE, g.shape[1])
    rope = g3[:, LANES:LANES + 2 * MLA_ROPE].reshape(MLA_HEADS // 2, 2, MLA_ROPE, g.shape[1])
    return jnp.concatenate([nope, rope], axis=2).reshape(-1, g.shape[1])


def _w_ukv_local(w):
    w3 = w.reshape(MLA_HEADS, MLA_NOPE + MLA_V, w.shape[1])
    return jnp.concatenate([w3[:, :MLA_NOPE].reshape(-1, w.shape[1]),
                            w3[:, MLA_NOPE:].reshape(-1, w.shape[1])], axis=0)


def _w_ukv_grad(g):
    half = MLA_HEADS * MLA_NOPE
    kn = g[:half].reshape(MLA_HEADS, MLA_NOPE, g.shape[1])
    vv = g[half:].reshape(MLA_HEADS, MLA_V, g.shape[1])
    return jnp.concatenate([kn, vv], axis=1).reshape(-1, g.shape[1])


def _rope_tables(T):
    pos = jnp.arange(T, dtype=F32)
    inv_freq = ROPE_THETA ** (-jnp.arange(0, MLA_ROPE, 2, dtype=F32) / MLA_ROPE)
    ang = pos[:, None] * inv_freq[None, :]
    cos, sin = jnp.cos(ang), jnp.sin(ang)
    ones = jnp.ones((T, LANES - 2 * MLA_ROPE), F32)
    cos_k = jnp.concatenate([cos, cos, cos, cos, ones], axis=1)
    sin_k = jnp.concatenate([-sin, sin, -sin, sin, 0.0 * ones], axis=1)
    cos_q = jnp.concatenate([jnp.ones((T, LANES), F32), cos_k], axis=1)
    sin_q = jnp.concatenate([jnp.zeros((T, LANES), F32), sin_k], axis=1)
    return cos_q, sin_q, cos_k, sin_k


def _bias_diag_index():
    ell = np.arange(TOEP_W)
    return np.clip(BAND_W - ell, -REL_CLIP, REL_CLIP) + REL_CLIP


def _local_step(x, target, small, get_weights, put_grads):
    T = x.shape[0]
    cos_q, sin_q, cos_k, sin_k = _rope_tables(T)
    G = {}
    W = dict(small)

    u0 = _rms_fwd(x, W["g_mix"][0:1], name="rms_mix0")
    bias_w = _toeplitz(W["od_rel_bias"][:, _bias_diag_index()])
    W.update(get_weights("in0", (u0, bias_w)))
    proj = _mm(u0, W["w_in_t"], dims="nt", name="proj_in")
    W.update(get_weights("mix0", proj))
    c_q, c_kv = proj[:, :_O1], proj[:, _O1:_O2]
    nq = _rms_fwd(c_q, W["g_cq"], name="rms_cq")
    nkv = _rms_fwd(c_kv, W["g_ckv"], name="rms_ckv")
    qa_raw = _mm(nq, W["w_uq_t"], dims="nt", name="proj_uq")
    kv = _mm(nkv, W["w_ukv_t"], dims="nt", out_dtype=BF16, name="proj_ukv")
    kr = _rope(proj, cos_k, sin_k, COL_KR, 1, BF16, name="rope_k")
    o_a, lse = _mla_fwd(qa_raw, cos_q, sin_q, kv, kr)
    o_b, o_b32 = _sb_fwd(proj, COL_SB)
    o_ab = jnp.concatenate([o_a, o_b], axis=1)
    h1 = _mm(o_ab, W["ev_w_out"], res=x, name="out_ev")

    def ffn_fwd(h, layer):
        W.update(get_weights(f"ffn{layer}", h))
        return _ffn_fwd(h, W["g_ffn"][layer:layer + 1], W[f"w_gate_t{layer}"], W[f"w_up_t{layer}"],
                        W[f"w_down{layer}"], name=f"ffn_fwd{layer}")

    h2, u1, a0, b0 = ffn_fwd(h1, 0)

    W.update(get_weights("mix1", h2))
    u2 = _rms_fwd(h2, W["g_mix"][1:2], name="rms_mix1")
    qkv = _mm(u2, W["od_w_qkv_t"], dims="nt", out_dtype=BF16, name="proj_qkv")
    nc = C_HEADS * C_DIM
    pad = ((PAD_KEYS, 0), (0, 0))
    k_pad, v_pad = jnp.pad(qkv[:, nc:2 * nc], pad), jnp.pad(qkv[:, 2 * nc:], pad)
    o_c = _band_fwd(qkv, k_pad, v_pad, bias_w)
    h3 = _mm(o_c, W["od_w_out"], res=h2, name="out_od")
    h4, u3, a1, b1 = ffn_fwd(h3, 1)

    loss, dh, dhb, G["g_final"] = _loss_head(h4, W["g_final"], target)

    def ffn_bwd(dh, dhb, h, u, a, b, layer):
        du, g_gate, g_up, g_down = _ffn_bwd(dhb, u, a, b, W[f"w_gate_t{layer}"], W[f"w_up_t{layer}"],
                                            W[f"w_down{layer}"], name=f"ffn_bwd{layer}")
        tok = put_grads(f"ffn{layer}", {"w_gate_t": g_gate, "w_up_t": g_up, "w_down": g_down})
        return _rms_bwd(h, W["g_ffn"][layer:layer + 1] + tok[:1, :1], du, dres=dh, name=f"rms_ffn_bwd{layer}")

    dh3, dh3b, g_gffn1 = ffn_bwd(dh, dhb, h3, u3, a1, b1, 1)

    do_c = _mm(dh3b, W["od_w_out"], dims="nt", name="out_od_dx")
    g_od_out = _mm(o_c, dh3b, dims="tn", out_dtype=BF16, name="out_od_dw")
    dq_c, dk_p, dv_p, dbias_w = _band_bwd(qkv, k_pad, v_pad, bias_w, do_c)
    dqkv = jnp.concatenate([dq_c, dk_p[PAD_KEYS:], dv_p[PAD_KEYS:]], axis=1)
    du2 = _mm(dqkv, W["od_w_qkv_t"], name="proj_qkv_dx")
    tok = put_grads("mix1", {"od_w_qkv_t": _mm(dqkv, u2, dims="tn", out_dtype=BF16, name="proj_qkv_dw"),
                             "od_w_out": g_od_out})
    ddiag = _toeplitz_bwd(dbias_w)
    n_far = BAND_W - REL_CLIP + 1
    G["od_rel_bias"] = jnp.concatenate(
        [jnp.zeros((C_HEADS, REL_CLIP - BAND_TQ + 1), F32), ddiag[:, n_far:][:, ::-1],
         jnp.sum(ddiag[:, :n_far], axis=1, keepdims=True)], axis=1)
    dh2, dh2b, g_gmix1 = _rms_bwd(h2, W["g_mix"][1:2] + tok[:1, :1], du2, dres=dh3, name="rms_mix_bwd1")

    dh1, dh1b, g_gffn0 = ffn_bwd(dh2, dh2b, h1, u1, a0, b0, 0)
    G["g_ffn"] = jnp.concatenate([g_gffn0, g_gffn1], axis=0)

    do_ab = _mm(dh1b, W["ev_w_out"], dims="nt", name="out_ev_dx")
    g0 = {"ev_w_out": _mm(o_ab, dh1b, dims="tn", out_dtype=BF16, name="out_ev_dw")}
    dqa_raw, dkn, dva, dkr = _mla_bwd(qa_raw, cos_q, sin_q, kv, kr, o_a, lse, do_ab, 0)
    g0["w_uq_t"] = _mm(dqa_raw, nq, dims="tn", name="proj_uq_dw")
    dnq = _mm(dqa_raw, W["w_uq_t"], name="proj_uq_dx")
    _, dc_q, G["g_cq"] = _rms_bwd(c_q, W["g_cq"], dnq, name="rms_cq_bwd")
    dkv = jnp.concatenate([dkn, dva], axis=1)
    g0["w_ukv_t"] = _mm(dkv, nkv, dims="tn", name="proj_ukv_dw")
    dnkv = _mm(dkv, W["w_ukv_t"], name="proj_ukv_dx")
    _, dc_kv, G["g_ckv"] = _rms_bwd(c_kv, W["g_ckv"], dnkv, name="rms_ckv_bwd")
    tok = put_grads("mix0", g0)
    dqb, dkb, dvb = _sb_bwd(proj, COL_SB, o_b32, do_ab, MLA_HEADS // 2, tok)
    dkr_raw = _rope(dkr, cos_k, -sin_k, 0, 1, BF16, name="rope_k_bwd")
    dproj = jnp.concatenate([dc_q, dc_kv, dkr_raw, dqb, dkb, dvb], axis=1)
    du0 = _mm(dproj, W["w_in_t"], name="proj_in_dx")
    tok = put_grads("in0", {"w_in_t": _mm(dproj, u0, dims="tn", name="proj_in_dw")})
    dx, _, g_gmix0 = _rms_bwd(x, W["g_mix"][0:1] + tok[:1, :1], du0, dres=dh1, name="rms_mix_bwd0")
    G["g_mix"] = jnp.concatenate([g_gmix0, g_gmix1], axis=0)
    return loss[0, 0], dx, G


_BIG = ["ev_w_in", "ev_w_uq", "ev_w_ukv", "ev_w_out", "od_w_qkv", "od_w_out", "w_gate", "w_up", "w_down"]
_COL_SHARDED = {"ev_w_in", "ev_w_uq", "ev_w_ukv", "od_w_qkv", "w_gate", "w_up"}
_SMALL = ["ev_g_cq", "ev_g_ckv", "od_rel_bias", "g_mix", "g_ffn", "g_final"]
_GROUPS = {
    "in0": ["ev_w_in"],
    "mix0": ["ev_w_uq", "ev_w_ukv", "ev_w_out"],
    "ffn0": ["w_gate0", "w_up0", "w_down0"],
    "mix1": ["od_w_qkv", "od_w_out"],
    "ffn1": ["w_gate1", "w_up1", "w_down1"],
}
_GROUP_SRC = {n + str(l): (n, l) for n in ("w_gate", "w_up", "w_down") for l in (0, 1)}
_BATCHES = {"in0": ["in0"], "layer0": ["mix0", "ffn0"], "layer1": ["mix1", "ffn1"]}
_BATCH_OF = {grp: batch for batch, grps in _BATCHES.items() for grp in grps}
_SMALL_ROWS = 8
_SMALL_COLS = 1792


def _pack_small(vals):
    flat = jnp.concatenate([v.reshape(-1).astype(F32) for v in vals])
    flat = jnp.pad(flat, (0, _SMALL_ROWS * _SMALL_COLS - flat.shape[0]))
    return flat.reshape(_SMALL_ROWS, _SMALL_COLS)


def _unpack_small(packed, like):
    flat = packed.reshape(-1)
    out, off = [], 0
    for v in like:
        out.append(flat[off:off + v.size].reshape(v.shape))
        off += v.size
    return out


def kernel(x, ev_w_in, ev_g_cq, ev_w_uq, ev_g_ckv, ev_w_ukv, ev_w_out, od_w_qkv, od_rel_bias, od_w_out, g_mix, g_ffn, w_gate, w_up, w_down, g_final, loss_target, m_ev_w_in, m_ev_g_cq, m_ev_w_uq, m_ev_g_ckv, m_ev_w_ukv, m_ev_w_out, m_od_w_qkv, m_od_rel_bias, m_od_w_out, m_g_mix, m_g_ffn, m_w_gate, m_w_up, m_w_down, m_g_final, v_ev_w_in, v_ev_g_cq, v_ev_w_uq, v_ev_g_ckv, v_ev_w_ukv, v_ev_w_out, v_od_w_qkv, v_od_rel_bias, v_od_w_out, v_g_mix, v_g_ffn, v_w_gate, v_w_up, v_w_down, v_g_final):
    args = dict(locals())
    w = {n: args[n] for n in _BIG + _SMALL}
    mom = {n: args["m_" + n] for n in _BIG + _SMALL}
    var = {n: args["v_" + n] for n in _BIG + _SMALL}

    own = {}
    for grp, names in _GROUPS.items():
        for n in names:
            base, layer = _GROUP_SRC.get(n, (n, 0))
            shard = w[base][layer:layer + 1]
            own[n] = (jnp.swapaxes(shard, 1, 2) if base in _COL_SHARDED else shard).astype(BF16)
    gather, token = {}, x[0, :8, :LANES]
    for grp, names in _GROUPS.items():
        gather[grp], token = _exchange_start([own[n] for n in names], [False] * len(names), token,
                                             name="gather_start_" + grp)

    def get_weights(grp, after):
        names = _GROUPS[grp]
        lands = _exchange_wait(gather[grp], token if after is None else after, name="gather_wait_" + grp)
        full = {n: l.reshape(-1, l.shape[-1]) for n, l in zip(names, lands)}
        if grp == "in0":
            return {"w_in_t": _w_in_local(full["ev_w_in"])}
        if grp == "mix0":
            return {"w_uq_t": _w_uq_local(full["ev_w_uq"]), "w_ukv_t": _w_ukv_local(full["ev_w_ukv"]),
                    "ev_w_out": full["ev_w_out"]}
        if grp == "mix1":
            return {"od_w_qkv_t": full["od_w_qkv"], "od_w_out": full["od_w_out"]}
        layer = grp[-1]
        return {"w_gate_t" + layer: full["w_gate" + layer], "w_up_t" + layer: full["w_up" + layer],
                "w_down" + layer: full["w_down" + layer]}

    scatter, pending = {}, {}

    def put_grads(grp, g):
        if grp == "in0":
            g = {"ev_w_in": _w_in_grad(g["w_in_t"])}
        elif grp == "mix0":
            g = {"ev_w_uq": _w_uq_grad(g["w_uq_t"]), "ev_w_ukv": _w_ukv_grad(g["w_ukv_t"]),
                 "ev_w_out": g["ev_w_out"]}
        elif grp == "mix1":
            g = {"od_w_qkv": g["od_w_qkv_t"], "od_w_out": g["od_w_out"]}
        else:
            layer = grp[-1]
            g = {"w_gate" + layer: g["w_gate_t"], "w_up" + layer: g["w_up_t"], "w_down" + layer: g["w_down"]}
        pending.update({n: v.reshape(N_DEV, 1, v.shape[0] // N_DEV, v.shape[1]).astype(BF16) for n, v in g.items()})
        batch = _BATCH_OF[grp]
        names = [n for gr in _BATCHES[batch] for n in _GROUPS[gr]]
        if not all(n in pending for n in names):
            return jnp.zeros((8, LANES), F32)
        send = [pending[n] for n in names]
        scatter[batch], tok = _exchange_start(send, [True] * len(names), send[0], name="scatter_start_" + batch)
        return tok

    small = {"g_cq": ev_g_cq, "g_ckv": ev_g_ckv, "od_rel_bias": od_rel_bias[0],
             "g_mix": g_mix + token[0, 0], "g_ffn": g_ffn, "g_final": g_final.reshape(1, -1)}
    loss_part, dx, G = _local_step(x[0], loss_target[0], small, get_weights, put_grads)
    g_small = _pack_small([G["g_cq"], G["g_ckv"], G["od_rel_bias"], G["g_mix"], G["g_ffn"], G["g_final"],
                           loss_part.reshape(1)])
    small_handle, _ = _exchange_start([g_small], [False], dx, name="gather_start_small")

    grads, deltas, new_m, new_v = {}, {}, {}, {}
    parts, after = {}, dx

    def wait_parts(batch, after):
        lands = _exchange_wait(scatter[batch], after, name="scatter_wait_" + batch)
        parts.update(zip([n for grp in _BATCHES[batch] for n in _GROUPS[grp]], lands))
        return lands[0]

    def adamw(n):
        col = n in _COL_SHARDED
        rows = lambda a: (jnp.swapaxes(a, 1, 2) if col else a).reshape(-1, a.shape[1 if col else 2])
        layers = [parts[n]] if n in parts else [parts[n + "0"], parts[n + "1"]]
        res = _adamw(rows(w[n]), [p.reshape(N_DEV, -1, p.shape[-1]) for p in layers], rows(mom[n]), rows(var[n]),
                     name="adamw_" + n)
        L, a1, a2 = w[n].shape
        back = lambda r: jnp.swapaxes(r.reshape(L, a2, a1), 1, 2) if col else r.reshape(L, a1, a2)
        grads[n], deltas[n], new_m[n], new_v[n] = [back(r) for r in res]
        return res[0]

    for batch in ("layer1", "layer0"):
        after = wait_parts(batch, after)
    for n in _BIG[1:]:
        after = adamw(n)
    after = wait_parts("in0", after)
    after = adamw("ev_w_in")
    small_w = [w[n] for n in _SMALL]
    small_parts = _exchange_wait(small_handle, after, name="gather_wait_small")[0]
    loss = jnp.sum(small_parts.reshape(N_DEV, -1)[:, sum(v.size for v in small_w)])
    res = _adamw(_pack_small(small_w), [small_parts], _pack_small([mom[n] for n in _SMALL]),
                 _pack_small([var[n] for n in _SMALL]), name="adamw_small")
    for d, packed in zip((grads, deltas, new_m, new_v), res):
        for n, val in zip(_SMALL, _unpack_small(packed, small_w)):
            d[n] = val

    order = ["ev_w_in", "ev_g_cq", "ev_w_uq", "ev_g_ckv", "ev_w_ukv", "ev_w_out", "od_w_qkv", "od_rel_bias",
             "od_w_out", "g_mix", "g_ffn", "w_gate", "w_up", "w_down", "g_final"]
    out = [loss, dx[None]]
    for d in (grads, deltas, new_m, new_v):
        out += [d[n] for n in order]
    return tuple(out)
```

```python
import functools

import numpy as np
import jax
import jax.numpy as jnp
from jax import lax
from jax.experimental import pallas as pl
from jax.experimental.pallas import tpu as pltpu

F32 = jnp.float32
BF16 = jnp.bfloat16

D_MODEL = 1024
CHUNK = 64
MLA_HEADS = 8
MLA_NOPE = 64
MLA_ROPE = 32
MLA_V = 64
Q_LORA = 384
KV_LORA = 256
ROPE_THETA = 10000.0
SB_HEADS = 8
SB_DIM = 64
C_HEADS = 16
C_DIM = 64
LEFT_CHUNKS = 8
REL_CLIP = 256
D_FF = 2816
RMS_EPS = 1e-6
ADAM_LR = 0.001
ADAM_B1 = 0.9
ADAM_B2 = 0.999
ADAM_EPS = 1e-08
ADAM_WD = 0.01
ADAM_STEP = 10

N_DEV = 8
LANES = 128
VMEM_LIMIT = 56 * 1024 * 1024
NEG = -1e30
PAD_KEYS = LEFT_CHUNKS * CHUNK
BAND_TQ = 128
BAND_W = BAND_TQ + PAD_KEYS
TOEP_W = BAND_W + BAND_TQ

NN = (((1,), (0,)), ((), ()))
NT = (((1,), (1,)), ((), ()))
TN = (((0,), (0,)), ((), ()))


def _dot(a, b, dn):
    return lax.dot_general(a, b, dn, preferred_element_type=F32)


def _pick(dim, pref):
    if dim <= pref:
        return dim
    best = None
    for t in range(LANES, pref + 1, LANES):
        if dim % t == 0:
            best = t
    assert best is not None, (dim, pref)
    return best


def _params(sem):
    return pltpu.CompilerParams(dimension_semantics=sem, vmem_limit_bytes=VMEM_LIMIT)


def _mm(a, b, dims="nn", res=None, out_dtype=F32, name="mm"):
    if dims == "nn":
        (M, K), (K2, N) = a.shape, b.shape
    elif dims == "nt":
        (M, K), (N, K2) = a.shape, b.shape
    else:
        (K, M), (K2, N) = a.shape, b.shape
    assert K == K2, (a.shape, b.shape, dims)
    tm, tn, tk = _pick(M, 1024), _pick(N, 1152), _pick(K, 1024)
    nk = K // tk
    dn = {"nn": NN, "nt": NT, "tn": TN}[dims]
    has_res = res is not None

    def body(*refs):
        if has_res:
            a_ref, b_ref, r_ref, o_ref, acc = refs
        else:
            a_ref, b_ref, o_ref, acc = refs
        k = pl.program_id(2)

        @pl.when(k == 0)
        def _():
            acc[...] = jnp.zeros_like(acc)

        acc[...] += _dot(a_ref[...].astype(BF16), b_ref[...].astype(BF16), dn)

        @pl.when(k == nk - 1)
        def _():
            r = acc[...]
            if has_res:
                r = r + r_ref[...]
            o_ref[...] = r.astype(out_dtype)

    a_spec = (pl.BlockSpec((tk, tm), lambda i, j, k: (k, i)) if dims == "tn"
              else pl.BlockSpec((tm, tk), lambda i, j, k: (i, k)))
    b_spec = (pl.BlockSpec((tn, tk), lambda i, j, k: (j, k)) if dims == "nt"
              else pl.BlockSpec((tk, tn), lambda i, j, k: (k, j)))
    o_spec = pl.BlockSpec((tm, tn), lambda i, j, k: (i, j))
    in_specs = [a_spec, b_spec] + ([o_spec] if has_res else [])
    args = (a, b) + ((res,) if has_res else ())
    return pl.pallas_call(
        body, name=name, grid=(M // tm, N // tn, nk),
        in_specs=in_specs, out_specs=o_spec,
        out_shape=jax.ShapeDtypeStruct((M, N), out_dtype),
        scratch_shapes=[pltpu.VMEM((tm, tn), F32)],
        compiler_params=_params(("parallel", "parallel", "arbitrary")),
    )(*args)


def _rms_fwd(x, g, out_dtype=BF16, name="rms_fwd"):
    T, Fd = x.shape
    tm = _pick(T, 256)

    def body(x_ref, g_ref, o_ref):
        xv = x_ref[...]
        r = lax.rsqrt(jnp.mean(xv * xv, axis=-1, keepdims=True) + RMS_EPS)
        o_ref[...] = (xv * r * g_ref[...]).astype(out_dtype)

    return pl.pallas_call(
        body, name=name, grid=(T // tm,),
        in_specs=[pl.BlockSpec((tm, Fd), lambda i: (i, 0)), pl.BlockSpec((1, Fd), lambda i: (0, 0))],
        out_specs=pl.BlockSpec((tm, Fd), lambda i: (i, 0)),
        out_shape=jax.ShapeDtypeStruct((T, Fd), out_dtype),
        compiler_params=_params(("parallel",)),
    )(x, g)


def _rms_bwd(x, g, dy, dres=None, name="rms_bwd"):
    T, Fd = x.shape
    tm = _pick(T, 256)
    has_res = dres is not None

    def body(*refs):
        if has_res:
            x_ref, g_ref, dy_ref, r_ref, dx_ref, dxb_ref, dg_ref = refs
        else:
            x_ref, g_ref, dy_ref, dx_ref, dxb_ref, dg_ref = refs
        xv, dyv = x_ref[...], dy_ref[...]
        r = lax.rsqrt(jnp.mean(xv * xv, axis=-1, keepdims=True) + RMS_EPS)
        gdy = dyv * g_ref[...]
        dot = jnp.mean(xv * gdy, axis=-1, keepdims=True)
        dx = r * gdy - xv * (r * r * r * dot)
        if has_res:
            dx = dx + r_ref[...]
        dx_ref[...] = dx
        dxb_ref[...] = dx.astype(BF16)

        @pl.when(pl.program_id(0) == 0)
        def _():
            dg_ref[...] = jnp.zeros_like(dg_ref)

        dg_ref[...] += jnp.sum(dyv * xv * r, axis=0, keepdims=True)

    row = pl.BlockSpec((tm, Fd), lambda i: (i, 0))
    vec = pl.BlockSpec((1, Fd), lambda i: (0, 0))
    in_specs = [row, vec, row] + ([row] if has_res else [])
    args = (x, g, dy) + ((dres,) if has_res else ())
    return pl.pallas_call(
        body, name=name, grid=(T // tm,),
        in_specs=in_specs, out_specs=[row, row, vec],
        out_shape=[jax.ShapeDtypeStruct((T, Fd), F32), jax.ShapeDtypeStruct((T, Fd), BF16),
                   jax.ShapeDtypeStruct((1, Fd), F32)],
        compiler_params=_params(("arbitrary",)),
    )(*args)


def _loss_head(h, g, target, name="loss_head"):
    T, Fd = h.shape
    tm = _pick(T, 256)

    def body(h_ref, g_ref, t_ref, loss_ref, dh_ref, dhb_ref, dg_ref):
        xv = h_ref[...]
        r = lax.rsqrt(jnp.mean(xv * xv, axis=-1, keepdims=True) + RMS_EPS)
        diff = xv * r * g_ref[...] - t_ref[...]
        part = 0.5 * jnp.sum(jnp.mean(diff * diff, axis=-1, keepdims=True), axis=0, keepdims=True)
        dyv = diff * (1.0 / Fd)
        gdy = dyv * g_ref[...]
        dot = jnp.mean(xv * gdy, axis=-1, keepdims=True)
        dh = r * gdy - xv * (r * r * r * dot)
        dh_ref[...] = dh
        dhb_ref[...] = dh.astype(BF16)

        @pl.when(pl.program_id(0) == 0)
        def _():
            dg_ref[...] = jnp.zeros_like(dg_ref)
            loss_ref[...] = jnp.zeros_like(loss_ref)

        dg_ref[...] += jnp.sum(dyv * xv * r, axis=0, keepdims=True)
        loss_ref[...] += jnp.broadcast_to(part, loss_ref.shape)

    row = pl.BlockSpec((tm, Fd), lambda i: (i, 0))
    vec = pl.BlockSpec((1, Fd), lambda i: (0, 0))
    return pl.pallas_call(
        body, name=name, grid=(T // tm,),
        in_specs=[row, vec, row],
        out_specs=[pl.BlockSpec((1, LANES), lambda i: (0, 0)), row, row, vec],
        out_shape=[jax.ShapeDtypeStruct((1, LANES), F32), jax.ShapeDtypeStruct((T, Fd), F32),
                   jax.ShapeDtypeStruct((T, Fd), BF16), jax.ShapeDtypeStruct((1, Fd), F32)],
        compiler_params=_params(("arbitrary",)),
    )(h, g, target)


FFN_TF = 256


def _ffn_fwd(h, g, wg_t, wu_t, wd, name="ffn_fwd"):
    T, Dm = h.shape
    Fh = wd.shape[0]
    tm = _pick(T, 1024)
    nf = Fh // FFN_TF

    def body(h_ref, g_ref, wg_ref, wu_ref, wd_ref, o_ref, u_ref, a_ref, b_ref):
        j = pl.program_id(1)

        @pl.when(j == 0)
        def _():
            xv = h_ref[...]
            r = lax.rsqrt(jnp.mean(xv * xv, axis=-1, keepdims=True) + RMS_EPS)
            u_ref[...] = (xv * r * g_ref[...]).astype(BF16)
            o_ref[...] = xv

        u = u_ref[...]
        a = _dot(u, wg_ref[...], NT).astype(BF16)
        b = _dot(u, wu_ref[...], NT).astype(BF16)
        a_ref[...] = a
        b_ref[...] = b
        af = a.astype(F32)
        s = (af * jax.nn.sigmoid(af) * b.astype(F32)).astype(BF16)
        o_ref[...] += _dot(s, wd_ref[...], NN)

    row = pl.BlockSpec((tm, Dm), lambda i, j: (i, 0))
    wblk = pl.BlockSpec((FFN_TF, Dm), lambda i, j: (j, 0))
    ablk = pl.BlockSpec((tm, FFN_TF), lambda i, j: (i, j))
    return pl.pallas_call(
        body, name=name, grid=(T // tm, nf),
        in_specs=[row, pl.BlockSpec((1, Dm), lambda i, j: (0, 0)), wblk, wblk, wblk],
        out_specs=[row, row, ablk, ablk],
        out_shape=[jax.ShapeDtypeStruct((T, Dm), F32), jax.ShapeDtypeStruct((T, Dm), BF16),
                   jax.ShapeDtypeStruct((T, Fh), BF16), jax.ShapeDtypeStruct((T, Fh), BF16)],
        compiler_params=_params(("parallel", "arbitrary")),
    )(h, g, wg_t, wu_t, wd)


def _ffn_bwd(dh, u, a, b, wg_t, wu_t, wd, name="ffn_bwd"):
    T, Dm = dh.shape
    Fh = wd.shape[0]
    nf = Fh // FFN_TF
    once = pl.Buffered(1)

    def body(dh_ref, u_ref, a_ref, b_ref, wg_ref, wu_ref, wd_ref, du_ref, dwg_ref, dwu_ref, dwd_ref):
        j = pl.program_id(0)

        @pl.when(j == 0)
        def _():
            du_ref[...] = jnp.zeros_like(du_ref)

        ds = _dot(dh_ref[...], wd_ref[...], NT)
        af, bf = a_ref[...].astype(F32), b_ref[...].astype(F32)
        sig = jax.nn.sigmoid(af)
        sa = af * sig
        dwd_ref[...] = _dot((sa * bf).astype(BF16), dh_ref[...], TN).astype(BF16)
        dab = jnp.concatenate([(ds * bf * (sig * (1.0 + af * (1.0 - sig)))).astype(BF16),
                               (ds * sa).astype(BF16)], axis=1)
        dw = _dot(dab, u_ref[...], TN)
        dwg_ref[...] = dw[:FFN_TF].astype(BF16)
        dwu_ref[...] = dw[FFN_TF:].astype(BF16)
        du_ref[...] += _dot(dab, jnp.concatenate([wg_ref[...], wu_ref[...]], axis=0), NN)

    full = lambda: pl.BlockSpec((T, Dm), lambda j: (0, 0), pipeline_mode=once)
    wblk = pl.BlockSpec((FFN_TF, Dm), lambda j: (j, 0))
    ablk = pl.BlockSpec((T, FFN_TF), lambda j: (0, j))
    return pl.pallas_call(
        body, name=name, grid=(nf,),
        in_specs=[full(), full(), ablk, ablk, wblk, wblk, wblk],
        out_specs=[pl.BlockSpec((T, Dm), lambda j: (0, 0)), wblk, wblk, wblk],
        out_shape=[jax.ShapeDtypeStruct((T, Dm), F32)] + [jax.ShapeDtypeStruct((Fh, Dm), BF16)] * 3,
        compiler_params=_params(("arbitrary",)),
    )(dh, u, a, b, wg_t, wu_t, wd)


def _rope(x, cos_t, sin_t, col0, ncols, out_dtype, name="rope"):
    T = x.shape[0]
    wt = cos_t.shape[1]
    tm = _pick(T, 256)
    nb = ncols * LANES // wt
    half = MLA_ROPE // 2

    def body(x_ref, c_ref, s_ref, o_ref):
        xv = x_ref[...].astype(F32)
        lane = lax.broadcasted_iota(jnp.int32, xv.shape, 1)
        first = (lane & (MLA_ROPE - 1)) < half
        swapped = jnp.where(first, pltpu.roll(xv, wt - half, 1), pltpu.roll(xv, half, 1))
        o_ref[...] = (xv * c_ref[...] + swapped * s_ref[...]).astype(out_dtype)

    off = col0 * LANES // wt
    return pl.pallas_call(
        body, name=name, grid=(T // tm, nb),
        in_specs=[pl.BlockSpec((tm, wt), lambda i, j: (i, j + off)),
                  pl.BlockSpec((tm, wt), lambda i, j: (i, 0)),
                  pl.BlockSpec((tm, wt), lambda i, j: (i, 0))],
        out_specs=pl.BlockSpec((tm, wt), lambda i, j: (i, j)),
        out_shape=jax.ShapeDtypeStruct((T, ncols * LANES), out_dtype),
        compiler_params=_params(("parallel", "parallel")),
    )(x, cos_t, sin_t)


ATT_TQ = 512
ATT_TK = 256


def _mla_masks(shape):
    lane = lax.broadcasted_iota(jnp.int32, shape, 1)
    m0 = (lane < 64) | ((lane >= 128) & (lane < 160))
    m1 = ((lane >= 64) & (lane < 128)) | ((lane >= 160) & (lane < 192))
    return m0, m1


def _by_twos(n, step, carry):
    carry = lax.fori_loop(0, n // 2, lambda i, c: step(2 * i + 1, step(2 * i, c)), carry)
    return lax.fori_loop(0, n % 2, lambda _, c: step(n - 1, c), carry)


def _chunk_ok(tq, tk, d):
    row = lax.broadcasted_iota(jnp.int32, (tq, tk), 0)
    col = lax.broadcasted_iota(jnp.int32, (tq, tk), 1) + d * tk
    return jnp.concatenate([(col >> 6) <= (row >> 6)] * 2, axis=0)


def _rotate(x, cos_t, sin_t):
    half = MLA_ROPE // 2
    lane = lax.broadcasted_iota(jnp.int32, x.shape, 1)
    first = (lane & (MLA_ROPE - 1)) < half
    swapped = jnp.where(first, pltpu.roll(x, x.shape[1] - half, 1), pltpu.roll(x, half, 1))
    return x * cos_t + swapped * sin_t


def _mla_fwd(q, cos_q, sin_q, kv, kr, name="mla_fwd"):
    T = q.shape[0]
    tq, tk = _pick(T, ATT_TQ), _pick(T, ATT_TK)
    nd = tq // tk
    npair = MLA_HEADS // 2
    scale = (MLA_NOPE + MLA_ROPE) ** -0.5

    def body(q_ref, c_ref, s_ref, kn_ref, v_ref, kr_ref, o_ref, lse_ref):
        m_idx = pl.program_id(1)
        qv = _rotate(q_ref[...], c_ref[...], s_ref[...]).astype(BF16)
        m0, m1 = _mla_masks(qv.shape)
        qs = jnp.concatenate([jnp.where(m0, qv, 0), jnp.where(m1, qv, 0)], axis=0).astype(BF16)

        def block(kb, carry, ok):
            ks = pl.ds(pl.multiple_of(kb * tk, tk), tk)
            kcat = jnp.concatenate([kn_ref[ks, :], kr_ref[ks, :]], axis=1)
            mx, l, acc = carry
            s = _dot(qs, kcat, NT) * scale
            if ok is not None:
                s = jnp.where(ok, s, NEG)
            mn = jnp.maximum(mx, jnp.max(s, axis=-1, keepdims=True))
            alpha = jnp.exp(mx - mn)
            p = jnp.exp(s - mn)
            return (mn, alpha * l + jnp.sum(p, axis=-1, keepdims=True),
                    alpha * acc + _dot(p.astype(BF16), v_ref[ks, :], NN))

        init = (jnp.full((2 * tq, 1), NEG, F32), jnp.zeros((2 * tq, 1), F32), jnp.zeros((2 * tq, LANES), F32))
        res = init
        for d in range(nd):
            res = block(m_idx * nd + d, res, _chunk_ok(tq, tk, d))
        mx, l, acc = _by_twos(m_idx * nd, lambda kb, c: block(kb, c, None), res)
        h0 = lax.broadcasted_iota(jnp.int32, (tq, LANES), 1) < 64
        o_ref[...] = _two_heads(acc * (1.0 / l), h0).astype(o_ref.dtype)
        lse_ref[...] = _two_heads(jnp.broadcast_to(mx + jnp.log(l), (2 * tq, LANES)), h0)

    full = lambda col: pl.BlockSpec((T, LANES), col)
    table = pl.BlockSpec((tq, 2 * LANES), lambda p, m: (m, 0))
    return pl.pallas_call(
        body, name=name, grid=(npair, T // tq),
        in_specs=[pl.BlockSpec((tq, 2 * LANES), lambda p, m: (m, p)), table, table,
                  full(lambda p, m: (0, p)), full(lambda p, m: (0, npair + p)), full(lambda p, m: (0, 0))],
        out_specs=[pl.BlockSpec((tq, LANES), lambda p, m: (m, p)),
                   pl.BlockSpec((tq, LANES), lambda p, m: (m, p))],
        out_shape=[jax.ShapeDtypeStruct((T, npair * LANES), BF16),
                   jax.ShapeDtypeStruct((T, npair * LANES), F32)],
        compiler_params=_params(("parallel", "arbitrary")),
    )(q, cos_q, sin_q, kv, kv, kr)


def _mla_bwd(q, cos_q, sin_q, kv, kr, o, lse, do, do_col0, name="mla_bwd"):
    T = q.shape[0]
    tq, tk = _pick(T, ATT_TQ), _pick(T, ATT_TK)
    nd = tq // tk
    npair = MLA_HEADS // 2
    scale = (MLA_NOPE + MLA_ROPE) ** -0.5

    def body(q_ref, c_ref, s_ref, kn_ref, v_ref, kr_ref, o_ref, lse_ref, do_ref, dq_ref, dkn_ref, dv_ref, dkr_ref,
             dkn_acc, dv_acc):
        p_idx, m_idx = pl.program_id(0), pl.program_id(1)

        @pl.when(m_idx == 0)
        def _():
            dkn_acc[...] = jnp.zeros_like(dkn_acc)
            dv_acc[...] = jnp.zeros_like(dv_acc)

        @pl.when((m_idx == 0) & (p_idx == 0))
        def _():
            dkr_ref[...] = jnp.zeros_like(dkr_ref)

        qv = _rotate(q_ref[...], c_ref[...], s_ref[...]).astype(BF16)
        m0, m1 = _mla_masks(qv.shape)
        qs = jnp.concatenate([jnp.where(m0, qv, 0), jnp.where(m1, qv, 0)], axis=0).astype(BF16)
        dov = do_ref[...].astype(F32)
        h0 = lax.broadcasted_iota(jnp.int32, (tq, LANES), 1) < 64
        dos32 = jnp.concatenate([jnp.where(h0, dov, 0.0), jnp.where(h0, 0.0, dov)], axis=0)
        ov = o_ref[...].astype(F32)
        delta = jnp.sum(dos32 * jnp.concatenate([ov, ov], axis=0), axis=-1, keepdims=True)
        dos = dos32.astype(BF16)
        lsev = lse_ref[...]
        lse = jnp.concatenate([lsev[:, 0:1], lsev[:, 64:65]], axis=0)

        def block(kb, dq, ok):
            ks = pl.ds(pl.multiple_of(kb * tk, tk), tk)
            kcat = jnp.concatenate([kn_ref[ks, :], kr_ref[ks, :]], axis=1)
            vv = v_ref[ks, :]
            p = jnp.exp(_dot(qs, kcat, NT) * scale - lse)
            if ok is not None:
                p = jnp.where(ok, p, 0.0)
            ds = (p * (_dot(dos, vv, NT) - delta) * scale).astype(BF16)
            dkc = _dot(ds, qs, TN)
            dkn_acc[ks, :] += dkc[:, :LANES]
            dkr_ref[ks, :] += dkc[:, LANES:]
            dv_acc[ks, :] += _dot(p.astype(BF16), dos, TN)
            return dq + _dot(ds, kcat, NN)

        dq = jnp.zeros((2 * tq, 2 * LANES), F32)
        for d in range(nd):
            dq = block(m_idx * nd + d, dq, _chunk_ok(tq, tk, d))
        dq = _by_twos(m_idx * nd, lambda kb, c: block(kb, c, None), dq)
        dq_ref[...] = _rotate(jnp.where(m0, dq[:tq], jnp.where(m1, dq[tq:], 0.0)), c_ref[...],
                              -s_ref[...]).astype(BF16)

        @pl.when(m_idx == T // tq - 1)
        def _():
            dkn_ref[...] = dkn_acc[...].astype(BF16)
            dv_ref[...] = dv_acc[...].astype(BF16)

    full = lambda col: pl.BlockSpec((T, LANES), col)
    blk = lambda col: pl.BlockSpec((tq, LANES), col)
    table = pl.BlockSpec((tq, 2 * LANES), lambda p, m: (m, 0))
    return pl.pallas_call(
        body, name=name, grid=(npair, T // tq),
        in_specs=[pl.BlockSpec((tq, 2 * LANES), lambda p, m: (m, p)), table, table,
                  full(lambda p, m: (0, p)), full(lambda p, m: (0, npair + p)), full(lambda p, m: (0, 0)),
                  blk(lambda p, m: (m, p)), blk(lambda p, m: (m, p)),
                  blk(lambda p, m: (m, do_col0 + p))],
        out_specs=[pl.BlockSpec((tq, 2 * LANES), lambda p, m: (m, p)),
                   full(lambda p, m: (0, p)), full(lambda p, m: (0, p)), full(lambda p, m: (0, 0))],
        out_shape=[jax.ShapeDtypeStruct((T, npair * 2 * LANES), BF16),
                   jax.ShapeDtypeStruct((T, npair * LANES), BF16),
                   jax.ShapeDtypeStruct((T, npair * LANES), BF16),
                   jax.ShapeDtypeStruct((T, LANES), F32)],
        scratch_shapes=[pltpu.VMEM((T, LANES), F32)] * 2,
        compiler_params=_params(("arbitrary", "arbitrary")),
    )(q, cos_q, sin_q, kv, kv, kr, o, lse, do)


def _split_dot(x, tri):
    hi = x.astype(BF16)
    lo = (x - hi.astype(F32)).astype(BF16)
    both = _dot(jnp.concatenate([hi, lo], axis=0), tri, NN)
    return both[:x.shape[0]] + both[x.shape[0]:]


def _sb_terms(qh, kk, before):
    z = _dot(qh, kk, NT)
    sp = jnp.maximum(z, 0.0) + jnp.log(1.0 + jnp.exp(-jnp.abs(z)))
    lk = -sp if before is None else jnp.where(before, -sp, 0.0)
    return z, sp, lk


def _sb_setup(q_ref, tq, tk, scale):
    qv = (q_ref[...].astype(F32) * scale).astype(BF16)
    lane = lax.broadcasted_iota(jnp.int32, (tq, LANES), 1)
    h0 = lane < 64
    qs = jnp.concatenate([jnp.where(h0, qv, 0), jnp.where(h0, 0, qv)], axis=0).astype(BF16)
    row = lax.broadcasted_iota(jnp.int32, (tk, tk), 0)
    col = lax.broadcasted_iota(jnp.int32, (tk, tk), 1)
    return qs, h0, row, col


def _sb_before(tq, tk, d):
    row = lax.broadcasted_iota(jnp.int32, (tq, tk), 0)
    col = lax.broadcasted_iota(jnp.int32, (tq, tk), 1) + d * tk
    return jnp.concatenate([col < row] * 2, axis=0)


def _two_heads(x, h0):
    tq = x.shape[0] // 2
    return jnp.where(h0, x[:tq], x[tq:])


def _sb_fwd(qkv, col0, name="sb_fwd"):
    T = qkv.shape[0]
    tq, tk = _pick(T, ATT_TQ), _pick(T, ATT_TK)
    nd = tq // tk
    npair = SB_HEADS // 2
    scale = SB_DIM ** -0.5

    def body(q_ref, k_ref, v_ref, o_ref, o32_ref):
        m_idx = pl.program_id(1)
        qs, h0, row, col = _sb_setup(q_ref, tq, tk, scale)
        later = (row > col).astype(BF16)

        def block(kb, carry, before):
            ks = pl.ds(pl.multiple_of(kb * tk, tk), tk)
            c, acc = carry
            z, sp, lk = _sb_terms(qs, k_ref[ks, :].astype(BF16), before)
            w = jnp.exp((z - sp) + _split_dot(lk, later) + c)
            if before is not None:
                w = jnp.where(before, w, 0.0)
            return (c + jnp.sum(lk, axis=-1, keepdims=True),
                    acc + _dot(w.astype(BF16), v_ref[ks, :].astype(BF16), NN))

        init = (jnp.zeros((2 * tq, 1), F32), jnp.zeros((2 * tq, LANES), F32))
        res = init
        for d in reversed(range(nd)):
            res = block(m_idx * nd + d, res, _sb_before(tq, tk, d))
        res = _by_twos(m_idx * nd, lambda i, c: block(m_idx * nd - 1 - i, c, None), res)
        o = _two_heads(res[1], h0)
        o_ref[...] = o.astype(o_ref.dtype)
        o32_ref[...] = o

    full = lambda col: pl.BlockSpec((T, LANES), col)
    blk = pl.BlockSpec((tq, LANES), lambda p, m: (m, p))
    return pl.pallas_call(
        body, name=name, grid=(npair, T // tq),
        in_specs=[pl.BlockSpec((tq, LANES), lambda p, m: (m, col0 + p)),
                  full(lambda p, m: (0, col0 + npair + p)), full(lambda p, m: (0, col0 + 2 * npair + p))],
        out_specs=[blk, blk],
        out_shape=[jax.ShapeDtypeStruct((T, npair * LANES), BF16), jax.ShapeDtypeStruct((T, npair * LANES), F32)],
        compiler_params=_params(("parallel", "arbitrary")),
    )(qkv, qkv, qkv)


def _sb_bwd(qkv, col0, o32, do, do_col0, dep, name="sb_bwd"):
    T = qkv.shape[0]
    tq, tk = _pick(T, ATT_TQ), _pick(T, ATT_TK)
    nd = tq // tk
    npair = SB_HEADS // 2
    scale = SB_DIM ** -0.5

    def body(q_ref, k_ref, v_ref, o_ref, do_ref, dep_ref, dq_ref, dk_ref, dv_ref, dk_acc, dv_acc):
        m_idx = pl.program_id(1)

        @pl.when(m_idx == 0)
        def _():
            dk_acc[...] = jnp.zeros_like(dk_acc)
            dv_acc[...] = jnp.zeros_like(dv_acc)

        qs, h0, row, col = _sb_setup(q_ref, tq, tk, scale)
        dov = do_ref[...].astype(F32)
        dos = jnp.concatenate([jnp.where(h0, dov, 0.0), jnp.where(h0, 0.0, dov)], axis=0).astype(BF16)
        ov = o_ref[...]
        etot = jnp.sum(dos.astype(F32) * jnp.concatenate([ov, ov], axis=0), axis=-1, keepdims=True)
        later = (row > col).astype(BF16)
        from_here = (row >= col).astype(BF16)

        def block(kb, carry, before):
            ks = pl.ds(pl.multiple_of(kb * tk, tk), tk)
            kk = k_ref[ks, :].astype(BF16)
            vv = v_ref[ks, :].astype(BF16)
            c, es, dqa = carry
            z, sp, lk = _sb_terms(qs, kk, before)
            w = jnp.exp((z - sp) + _split_dot(lk, later) + c)
            if before is not None:
                w = jnp.where(before, w, 0.0)
            wb = w.astype(BF16)
            e = wb.astype(F32) * _dot(dos, vv, NT)
            prev = etot - (_split_dot(e, from_here) + es)
            sig_neg = jnp.exp(-sp)
            dz = e * sig_neg - (1.0 - sig_neg) * prev
            if before is not None:
                dz = jnp.where(before, dz, 0.0)
            dzb = dz.astype(BF16)
            dk_acc[ks, :] += _dot(dzb, qs, TN)
            dv_acc[ks, :] += _dot(wb, dos, TN)
            return (c + jnp.sum(lk, axis=-1, keepdims=True), es + jnp.sum(e, axis=-1, keepdims=True),
                    dqa + _dot(dzb, kk, NN))

        init = (jnp.zeros((2 * tq, 1), F32), jnp.zeros((2 * tq, 1), F32), jnp.zeros((2 * tq, LANES), F32))
        res = init
        for d in reversed(range(nd)):
            res = block(m_idx * nd + d, res, _sb_before(tq, tk, d))
        res = _by_twos(m_idx * nd, lambda i, c: block(m_idx * nd - 1 - i, c, None), res)
        dq_ref[...] = (_two_heads(res[2], h0) * scale).astype(BF16)

        @pl.when(m_idx == T // tq - 1)
        def _():
            dk_ref[...] = dk_acc[...].astype(BF16)
            dv_ref[...] = dv_acc[...].astype(BF16)

    full = lambda col: pl.BlockSpec((T, LANES), col)
    blk = lambda col: pl.BlockSpec((tq, LANES), col)
    return pl.pallas_call(
        body, name=name, grid=(npair, T // tq),
        in_specs=[blk(lambda p, m: (m, col0 + p)),
                  full(lambda p, m: (0, col0 + npair + p)), full(lambda p, m: (0, col0 + 2 * npair + p)),
                  blk(lambda p, m: (m, p)), blk(lambda p, m: (m, do_col0 + p)),
                  pl.BlockSpec((8, LANES), lambda p, m: (0, 0))],
        out_specs=[blk(lambda p, m: (m, p)), full(lambda p, m: (0, p)), full(lambda p, m: (0, p))],
        out_shape=[jax.ShapeDtypeStruct((T, npair * LANES), BF16)] * 3,
        scratch_shapes=[pltpu.VMEM((T, LANES), F32)] * 2,
        compiler_params=_params(("arbitrary", "arbitrary")),
    )(qkv, qkv, qkv, o32, do, dep)


def _band_in_window():
    cq = lax.broadcasted_iota(jnp.int32, (BAND_TQ, BAND_W), 0) >> 6
    ckp = lax.broadcasted_iota(jnp.int32, (BAND_TQ, BAND_W), 1) >> 6
    return (ckp >= cq) & (ckp <= cq + LEFT_CHUNKS)


def _band_real(m_idx):
    j = lax.broadcasted_iota(jnp.int32, (BAND_TQ, BAND_W), 1)
    return j >= PAD_KEYS - m_idx * BAND_TQ


def _band_probs(qh, kw, bias, real, scale):
    s = jnp.where(real, _dot(qh, kw, NT) * scale + bias, NEG)
    e = jnp.exp(s - jnp.max(s, axis=-1, keepdims=True))
    return e * (1.0 / jnp.sum(e, axis=-1, keepdims=True))


BAND_SUB = 4


def _band_fwd(qkv, k_pad, v_pad, bias_w, name="band_fwd"):
    T = qkv.shape[0]
    npair = C_HEADS // 2
    scale = C_DIM ** -0.5
    rows = BAND_SUB * BAND_TQ

    def body(q_ref, k_ref, v_ref, b_ref, o_ref):
        lane = lax.broadcasted_iota(jnp.int32, (BAND_TQ, LANES), 1)
        h0 = lane < 64
        bias = jnp.concatenate([b_ref[0], b_ref[1]], axis=0)
        for sub in range(BAND_SUB):
            m_idx = pl.program_id(1) * BAND_SUB + sub
            win = pl.ds(pl.multiple_of(m_idx * BAND_TQ, BAND_TQ), BAND_W)
            kw, vw = k_ref[win, :], v_ref[win, :]
            qv = q_ref[sub * BAND_TQ:(sub + 1) * BAND_TQ, :]
            qs = jnp.concatenate([jnp.where(h0, qv, 0), jnp.where(h0, 0, qv)], axis=0).astype(BF16)
            p = _band_probs(qs, kw, bias, jnp.concatenate([_band_real(m_idx)] * 2, axis=0), scale)
            o = _two_heads(_dot(p.astype(BF16), vw, NN), h0)
            o_ref[sub * BAND_TQ:(sub + 1) * BAND_TQ, :] = o.astype(o_ref.dtype)

    Tp = T + PAD_KEYS
    return pl.pallas_call(
        body, name=name, grid=(npair, T // rows),
        in_specs=[pl.BlockSpec((rows, LANES), lambda p, m: (m, p)),
                  pl.BlockSpec((Tp, LANES), lambda p, m: (0, p)),
                  pl.BlockSpec((Tp, LANES), lambda p, m: (0, p)),
                  pl.BlockSpec((2, BAND_TQ, BAND_W), lambda p, m: (p, 0, 0))],
        out_specs=pl.BlockSpec((rows, LANES), lambda p, m: (m, p)),
        out_shape=jax.ShapeDtypeStruct((T, npair * LANES), BF16),
        compiler_params=_params(("parallel", "arbitrary")),
    )(qkv, k_pad, v_pad, bias_w)


def _band_bwd(qkv, k_pad, v_pad, bias_w, do, name="band_bwd"):
    T = qkv.shape[0]
    npair = C_HEADS // 2
    scale = C_DIM ** -0.5

    rows = BAND_SUB * BAND_TQ

    def body(q_ref, k_ref, v_ref, b_ref, do_ref, dq_ref, dk_ref, dv_ref, db_ref, dk_acc, dv_acc):
        @pl.when(pl.program_id(1) == 0)
        def _():
            dk_acc[...] = jnp.zeros_like(dk_acc)
            dv_acc[...] = jnp.zeros_like(dv_acc)
            db_ref[...] = jnp.zeros_like(db_ref)

        lane = lax.broadcasted_iota(jnp.int32, (BAND_TQ, LANES), 1)
        h0 = lane < 64
        dbs = jnp.zeros((2 * BAND_TQ, BAND_W), F32)
        bias = jnp.concatenate([b_ref[0], b_ref[1]], axis=0)
        for sub in range(BAND_SUB):
            m_idx = pl.program_id(1) * BAND_SUB + sub
            win = pl.ds(pl.multiple_of(m_idx * BAND_TQ, BAND_TQ), BAND_W)
            kw, vw = k_ref[win, :], v_ref[win, :]
            qv = q_ref[sub * BAND_TQ:(sub + 1) * BAND_TQ, :]
            dov = do_ref[sub * BAND_TQ:(sub + 1) * BAND_TQ, :].astype(F32)
            qs = jnp.concatenate([jnp.where(h0, qv, 0), jnp.where(h0, 0, qv)], axis=0).astype(BF16)
            dos = jnp.concatenate([jnp.where(h0, dov, 0.0), jnp.where(h0, 0.0, dov)], axis=0).astype(BF16)
            p = _band_probs(qs, kw, bias, jnp.concatenate([_band_real(m_idx)] * 2, axis=0), scale)
            dp = _dot(dos, vw, NT)
            dsb = p * (dp - jnp.sum(p * dp, axis=-1, keepdims=True))
            dbs = dbs + dsb
            dsq = (dsb * scale).astype(BF16)
            dq_ref[sub * BAND_TQ:(sub + 1) * BAND_TQ, :] = _two_heads(_dot(dsq, kw, NN), h0).astype(BF16)
            dk_acc[win, :] += _dot(dsq, qs, TN)
            dv_acc[win, :] += _dot(p.astype(BF16), dos, TN)
        db_ref[0] += dbs[:BAND_TQ]
        db_ref[1] += dbs[BAND_TQ:]

        @pl.when(pl.program_id(1) == T // rows - 1)
        def _():
            dk_ref[...] = dk_acc[...].astype(BF16)
            dv_ref[...] = dv_acc[...].astype(BF16)

    Tp = T + PAD_KEYS
    blk = lambda col: pl.BlockSpec((rows, LANES), col)
    full = pl.BlockSpec((Tp, LANES), lambda p, m: (0, p))
    bias = pl.BlockSpec((2, BAND_TQ, BAND_W), lambda p, m: (p, 0, 0))
    return pl.pallas_call(
        body, name=name, grid=(npair, T // rows),
        in_specs=[blk(lambda p, m: (m, p)), full, full, bias, blk(lambda p, m: (m, p))],
        out_specs=[blk(lambda p, m: (m, p)), full, full, bias],
        out_shape=[jax.ShapeDtypeStruct((T, npair * LANES), BF16),
                   jax.ShapeDtypeStruct((Tp, npair * LANES), BF16),
                   jax.ShapeDtypeStruct((Tp, npair * LANES), BF16),
                   jax.ShapeDtypeStruct((C_HEADS, BAND_TQ, BAND_W), F32)],
        scratch_shapes=[pltpu.VMEM((Tp, LANES), F32)] * 2,
        compiler_params=_params(("arbitrary", "arbitrary")),
    )(qkv, k_pad, v_pad, bias_w, do)


def _skew_bits(x, left):
    w = x.shape[1]
    row = lax.broadcasted_iota(jnp.int32, x.shape, 0)
    for b in range(BAND_TQ.bit_length() - 1):
        amt = (w - (1 << b)) if left else (1 << b)
        x = jnp.where((row >> b) & 1 == 1, pltpu.roll(x, amt, 1), x)
    return x


def _toeplitz(diag, name="toeplitz"):
    H = diag.shape[0]

    def body(d_ref, o_ref):
        x = jnp.broadcast_to(d_ref[0], (BAND_TQ, TOEP_W))
        o_ref[0] = jnp.where(_band_in_window(), _skew_bits(x, left=False)[:, BAND_TQ:], NEG)

    return pl.pallas_call(
        body, name=name, grid=(H,),
        in_specs=[pl.BlockSpec((1, 1, TOEP_W), lambda h: (h, 0, 0))],
        out_specs=pl.BlockSpec((1, BAND_TQ, BAND_W), lambda h: (h, 0, 0)),
        out_shape=jax.ShapeDtypeStruct((H, BAND_TQ, BAND_W), F32),
        compiler_params=_params(("parallel",)),
    )(diag.reshape(H, 1, TOEP_W))


def _toeplitz_bwd(dbias, name="toeplitz_bwd"):
    H = dbias.shape[0]

    def body(d_ref, o_ref):
        x = jnp.concatenate([jnp.zeros((BAND_TQ, BAND_TQ), F32), d_ref[0]], axis=1)
        o_ref[0] = jnp.sum(_skew_bits(x, left=True), axis=0, keepdims=True)

    return pl.pallas_call(
        body, name=name, grid=(H,),
        in_specs=[pl.BlockSpec((1, BAND_TQ, BAND_W), lambda h: (h, 0, 0))],
        out_specs=pl.BlockSpec((1, 1, TOEP_W), lambda h: (h, 0, 0)),
        out_shape=jax.ShapeDtypeStruct((H, 1, TOEP_W), F32),
        compiler_params=_params(("parallel",)),
    )(dbias).reshape(H, TOEP_W)


_HBM = pl.BlockSpec(memory_space=pltpu.HBM)
_SEM = pl.BlockSpec(memory_space=pltpu.SEMAPHORE)
_EFFECT = pltpu.SideEffectType.DATAFLOW_SIDE_EFFECTING


def _peers():
    x, y, c = lax.axis_index("x"), lax.axis_index("y"), lax.axis_index("c")
    out = []
    for k in range(1, N_DEV):
        peer = (1 - x if (k >> 2) & 1 else x, 1 - y if (k >> 1) & 1 else y, 1 - c if k & 1 else c)
        out.append((peer, 4 * peer[0] + 2 * peer[1] + peer[2]))
    return 4 * x + 2 * y + c, out


def _split_copies(ins, lands, scatter, send_sem, recv_sem, arriving):
    me, peers = _peers()
    out = []
    for a in range(len(ins)):
        for peer, idx in peers:
            out.append(pltpu.make_async_remote_copy(
                src_ref=ins[a].at[idx] if scatter[a] else ins[a],
                dst_ref=lands[a].at[idx if arriving else me], send_sem=send_sem, recv_sem=recv_sem,
                device_id=peer, device_id_type=pl.DeviceIdType.MESH))
    return out


def _landing_zones(arrays, scatter):
    return [lax.empty((N_DEV,) + (a.shape[1:] if s else a.shape), a.dtype) for a, s in zip(arrays, scatter)]


def _place_own(arrays, scatter, name):
    n = len(arrays)
    lands = _landing_zones(arrays, scatter)
    me = (4 * lax.axis_index("x") + 2 * lax.axis_index("y") + lax.axis_index("c")).astype(jnp.int32).reshape(1)

    def body(me_ref, *refs):
        for a in range(n):
            refs[2 * n + a][...] = refs[a][...].reshape(refs[2 * n + a].shape)

    def row_spec(shape):
        zeros = (0,) * (len(shape) - 1)
        return pl.BlockSpec((1,) + tuple(shape[1:]), lambda i, me_ref: (me_ref[0],) + zeros)

    in_specs = [row_spec(a.shape) if s else pl.BlockSpec(a.shape, lambda i, me_ref, nd=a.ndim: (0,) * nd)
                for a, s in zip(arrays, scatter)]
    return pl.pallas_call(
        body, name=name,
        out_shape=[jax.ShapeDtypeStruct(l.shape, l.dtype) for l in lands],
        grid_spec=pltpu.PrefetchScalarGridSpec(
            num_scalar_prefetch=1, grid=(1,),
            in_specs=in_specs + [pl.BlockSpec(memory_space=pl.ANY)] * n,
            out_specs=[row_spec(l.shape) for l in lands]),
        input_output_aliases={1 + n + i: i for i in range(n)},
        compiler_params=_params(("arbitrary",)),
    )(me, *arrays, *lands)


def _exchange_start(arrays, scatter, after, name, lands=None):
    n = len(arrays)
    if lands is None:
        lands = list(_place_own(arrays, scatter, name=name.replace("_start_", "_own_")))

    def body(*refs):
        ins, lnd = refs[:n], refs[n:2 * n]
        send_sem, recv_sem = refs[2 * n + 1:2 * n + 3]
        token = refs[-1]
        for cp in _split_copies(ins, lnd, scatter, send_sem, recv_sem, arriving=False):
            cp.start()
        token[...] = jnp.zeros_like(token)

    hbm = lambda a: pltpu.HBM(a.shape, a.dtype)
    out = pl.pallas_call(
        body, name=name,
        out_shape=(pltpu.SemaphoreType.DMA(()), pltpu.SemaphoreType.DMA(()),
                   *[hbm(a) for a in arrays], *[hbm(a) for a in lands],
                   jax.ShapeDtypeStruct((8, LANES), F32)),
        in_specs=[_HBM] * (2 * n) + [pl.BlockSpec(memory_space=pl.ANY)],
        out_specs=(_SEM, _SEM, *([_HBM] * (2 * n)), pl.BlockSpec(memory_space=pltpu.VMEM)),
        input_output_aliases={i: 2 + i for i in range(2 * n)},
        compiler_params=pltpu.CompilerParams(has_side_effects=_EFFECT),
    )(*[pltpu.with_memory_space_constraint(a, pltpu.HBM) for a in list(arrays) + lands], after)
    return (out[0], out[1], list(out[2:2 + n]), list(out[2 + n:2 + 2 * n]), tuple(scatter)), out[-1]


def _exchange_wait(handle, after, name):
    send_sem, recv_sem, ins, lands, scatter = handle
    n = len(ins)
    after = after if isinstance(after, tuple) else (after,)

    def body(*refs):
        i_ref, l_ref = refs[:n], refs[n:2 * n]
        s_sem, r_sem = refs[2 * n:2 * n + 2]
        for cp in _split_copies(i_ref, l_ref, scatter, s_sem, r_sem, arriving=False):
            cp.wait_send()
        for cp in _split_copies(i_ref, l_ref, scatter, s_sem, r_sem, arriving=True):
            cp.wait_recv()

    hbm = lambda a: pltpu.HBM(a.shape, a.dtype)
    out = pl.pallas_call(
        body, name=name,
        out_shape=tuple(hbm(a) for a in ins + lands),
        in_specs=[_HBM] * (2 * n) + [_SEM, _SEM] + [pl.BlockSpec(memory_space=pl.ANY)] * len(after),
        out_specs=tuple([_HBM] * (2 * n)),
        input_output_aliases={i: i for i in range(2 * n)},
        compiler_params=pltpu.CompilerParams(has_side_effects=_EFFECT),
    )(*ins, *lands, send_sem, recv_sem, *after)
    return list(out[n:])


def _adamw(w, parts, m, v, name="adamw"):
    R, C = w.shape
    L = len(parts)
    rl = R // L
    tr = max([t for t in range(16, 513, 16) if rl % t == 0], default=rl)
    nb = rl // tr
    c1 = 1.0 - ADAM_B1 ** ADAM_STEP
    c2 = 1.0 - ADAM_B2 ** ADAM_STEP

    def body(*refs):
        w_ref, p_refs, (m_ref, v_ref, g_ref, d_ref, nm_ref, nv_ref) = refs[0], refs[1:1 + L], refs[1 + L:]
        g = None
        for j, p_ref in enumerate(p_refs):
            gj = p_ref[0].astype(F32)
            for i in range(1, N_DEV):
                gj = gj + p_ref[i].astype(F32)
            g = gj if g is None else jnp.where(pl.program_id(0) == j, gj, g)
        nm = ADAM_B1 * m_ref[...] + (1.0 - ADAM_B1) * g
        nv = ADAM_B2 * v_ref[...] + (1.0 - ADAM_B2) * (g * g)
        g_ref[...] = g
        nm_ref[...] = nm
        nv_ref[...] = nv
        d_ref[...] = -ADAM_LR * ((nm / c1) / (jnp.sqrt(nv / c2) + ADAM_EPS) + ADAM_WD * w_ref[...])

    blk = pl.BlockSpec((tr, C), lambda l, i: (l * nb + i, 0))
    part = lambda j: pl.BlockSpec((N_DEV, tr, C), lambda l, i: (0, jnp.where(l == j, i, 0), 0))
    return pl.pallas_call(
        body, name=name, grid=(L, nb),
        in_specs=[blk] + [part(j) for j in range(L)] + [blk, blk],
        out_specs=[blk] * 4,
        out_shape=[jax.ShapeDtypeStruct((R, C), F32)] * 4,
        compiler_params=_params(("arbitrary", "arbitrary")),
    )(w, *parts, m, v)


_O1 = Q_LORA
_O2 = _O1 + KV_LORA
_O3 = _O2 + MLA_ROPE
_NB = SB_HEADS * SB_DIM
IN_W = _O2 + LANES + 3 * _NB
COL_KR = _O2 // LANES
COL_SB = COL_KR + 1


def _w_in_local(w):
    kr = w[_O2:_O3]
    pad = jnp.zeros((LANES - 2 * MLA_ROPE, w.shape[1]), w.dtype)
    return jnp.concatenate([w[:_O2], kr, kr, pad, w[_O3:]], axis=0)


def _w_in_grad(g):
    kr = (g[_O2:_O2 + MLA_ROPE].astype(F32) + g[_O2 + MLA_ROPE:_O2 + 2 * MLA_ROPE].astype(F32)).astype(g.dtype)
    return jnp.concatenate([g[:_O2], kr, g[_O2 + LANES:]], axis=0)


def _w_uq_local(w):
    w3 = w.reshape(MLA_HEADS // 2, 2, MLA_NOPE + MLA_ROPE, w.shape[1])
    nope = w3[:, :, :MLA_NOPE].reshape(MLA_HEADS // 2, 2 * MLA_NOPE, w.shape[1])
    rope = w3[:, :, MLA_NOPE:].reshape(MLA_HEADS // 2, 2 * MLA_ROPE, w.shape[1])
    pad = jnp.zeros((MLA_HEADS // 2, LANES - 2 * MLA_ROPE, w.shape[1]), w.dtype)
    return jnp.concatenate([nope, rope, pad], axis=1).reshape(-1, w.shape[1])


def _w_uq_grad(g):
    g3 = g.reshape(MLA_HEADS // 2, 2 * LANES, g.shape[1])
    nope = g3[:, :2 * MLA_NOPE].reshape(MLA_HEADS // 2, 2, MLA_NOPE, g.shape[1])
    rope = g3[:, LANES:LANES + 2 * MLA_ROPE].reshape(MLA_HEADS // 2, 2, MLA_ROPE, g.shape[1])
    return jnp.concatenate([nope, rope], axis=2).reshape(-1, g.shape[1])


def _w_ukv_local(w):
    w3 = w.reshape(MLA_HEADS, MLA_NOPE + MLA_V, w.shape[1])
    return jnp.concatenate([w3[:, :MLA_NOPE].reshape(-1, w.shape[1]),
                            w3[:, MLA_NOPE:].reshape(-1, w.shape[1])], axis=0)


def _w_ukv_grad(g):
    half = MLA_HEADS * MLA_NOPE
    kn = g[:half].reshape(MLA_HEADS, MLA_NOPE, g.shape[1])
    vv = g[half:].reshape(MLA_HEADS, MLA_V, g.shape[1])
    return jnp.concatenate([kn, vv], axis=1).reshape(-1, g.shape[1])


def _rope_tables(T):
    pos = jnp.arange(T, dtype=F32)
    inv_freq = ROPE_THETA ** (-jnp.arange(0, MLA_ROPE, 2, dtype=F32) / MLA_ROPE)
    ang = pos[:, None] * inv_freq[None, :]
    cos, sin = jnp.cos(ang), jnp.sin(ang)
    ones = jnp.ones((T, LANES - 2 * MLA_ROPE), F32)
    cos_k = jnp.concatenate([cos, cos, cos, cos, ones], axis=1)
    sin_k = jnp.concatenate([-sin, sin, -sin, sin, 0.0 * ones], axis=1)
    cos_q = jnp.concatenate([jnp.ones((T, LANES), F32), cos_k], axis=1)
    sin_q = jnp.concatenate([jnp.zeros((T, LANES), F32), sin_k], axis=1)
    return cos_q, sin_q, cos_k, sin_k


def _bias_diag_index():
    ell = np.arange(TOEP_W)
    return np.clip(BAND_W - ell, -REL_CLIP, REL_CLIP) + REL_CLIP


def _local_step(x, target, small, get_weights, put_grads):
    T = x.shape[0]
    cos_q, sin_q, cos_k, sin_k = _rope_tables(T)
    G = {}
    W = dict(small)

    u0 = _rms_fwd(x, W["g_mix"][0:1], name="rms_mix0")
    bias_w = _toeplitz(W["od_rel_bias"][:, _bias_diag_index()])
    W.update(get_weights("in0", (u0, bias_w)))
    proj = _mm(u0, W["w_in_t"], dims="nt", name="proj_in")
    W.update(get_weights("mix0", proj))
    c_q, c_kv = proj[:, :_O1], proj[:, _O1:_O2]
    nq = _rms_fwd(c_q, W["g_cq"], name="rms_cq")
    nkv = _rms_fwd(c_kv, W["g_ckv"], name="rms_ckv")
    qa_raw = _mm(nq, W["w_uq_t"], dims="nt", name="proj_uq")
    kv = _mm(nkv, W["w_ukv_t"], dims="nt", out_dtype=BF16, name="proj_ukv")
    kr = _rope(proj, cos_k, sin_k, COL_KR, 1, BF16, name="rope_k")
    o_a, lse = _mla_fwd(qa_raw, cos_q, sin_q, kv, kr)
    o_b, o_b32 = _sb_fwd(proj, COL_SB)
    o_ab = jnp.concatenate([o_a, o_b], axis=1)
    h1 = _mm(o_ab, W["ev_w_out"], res=x, name="out_ev")

    def ffn_fwd(h, layer):
        W.update(get_weights(f"ffn{layer}", h))
        return _ffn_fwd(h, W["g_ffn"][layer:layer + 1], W[f"w_gate_t{layer}"], W[f"w_up_t{layer}"],
                        W[f"w_down{layer}"], name=f"ffn_fwd{layer}")

    h2, u1, a0, b0 = ffn_fwd(h1, 0)

    W.update(get_weights("mix1", h2))
    u2 = _rms_fwd(h2, W["g_mix"][1:2], name="rms_mix1")
    qkv = _mm(u2, W["od_w_qkv_t"], dims="nt", out_dtype=BF16, name="proj_qkv")
    nc = C_HEADS * C_DIM
    pad = ((PAD_KEYS, 0), (0, 0))
    k_pad, v_pad = jnp.pad(qkv[:, nc:2 * nc], pad), jnp.pad(qkv[:, 2 * nc:], pad)
    o_c = _band_fwd(qkv, k_pad, v_pad, bias_w)
    h3 = _mm(o_c, W["od_w_out"], res=h2, name="out_od")
    h4, u3, a1, b1 = ffn_fwd(h3, 1)

    loss, dh, dhb, G["g_final"] = _loss_head(h4, W["g_final"], target)

    def ffn_bwd(dh, dhb, h, u, a, b, layer):
        du, g_gate, g_up, g_down = _ffn_bwd(dhb, u, a, b, W[f"w_gate_t{layer}"], W[f"w_up_t{layer}"],
                                            W[f"w_down{layer}"], name=f"ffn_bwd{layer}")
        tok = put_grads(f"ffn{layer}", {"w_gate_t": g_gate, "w_up_t": g_up, "w_down": g_down})
        return _rms_bwd(h, W["g_ffn"][layer:layer + 1] + tok[:1, :1], du, dres=dh, name=f"rms_ffn_bwd{layer}")

    dh3, dh3b, g_gffn1 = ffn_bwd(dh, dhb, h3, u3, a1, b1, 1)

    do_c = _mm(dh3b, W["od_w_out"], dims="nt", name="out_od_dx")
    g_od_out = _mm(o_c, dh3b, dims="tn", out_dtype=BF16, name="out_od_dw")
    dq_c, dk_p, dv_p, dbias_w = _band_bwd(qkv, k_pad, v_pad, bias_w, do_c)
    dqkv = jnp.concatenate([dq_c, dk_p[PAD_KEYS:], dv_p[PAD_KEYS:]], axis=1)
    du2 = _mm(dqkv, W["od_w_qkv_t"], name="proj_qkv_dx")
    tok = put_grads("mix1", {"od_w_qkv_t": _mm(dqkv, u2, dims="tn", out_dtype=BF16, name="proj_qkv_dw"),
                             "od_w_out": g_od_out})
    ddiag = _toeplitz_bwd(dbias_w)
    n_far = BAND_W - REL_CLIP + 1
    G["od_rel_bias"] = jnp.concatenate(
        [jnp.zeros((C_HEADS, REL_CLIP - BAND_TQ + 1), F32), ddiag[:, n_far:][:, ::-1],
         jnp.sum(ddiag[:, :n_far], axis=1, keepdims=True)], axis=1)
    dh2, dh2b, g_gmix1 = _rms_bwd(h2, W["g_mix"][1:2] + tok[:1, :1], du2, dres=dh3, name="rms_mix_bwd1")

    dh1, dh1b, g_gffn0 = ffn_bwd(dh2, dh2b, h1, u1, a0, b0, 0)
    G["g_ffn"] = jnp.concatenate([g_gffn0, g_gffn1], axis=0)

    do_ab = _mm(dh1b, W["ev_w_out"], dims="nt", name="out_ev_dx")
    g0 = {"ev_w_out": _mm(o_ab, dh1b, dims="tn", out_dtype=BF16, name="out_ev_dw")}
    dqa_raw, dkn, dva, dkr = _mla_bwd(qa_raw, cos_q, sin_q, kv, kr, o_a, lse, do_ab, 0)
    g0["w_uq_t"] = _mm(dqa_raw, nq, dims="tn", name="proj_uq_dw")
    dnq = _mm(dqa_raw, W["w_uq_t"], name="proj_uq_dx")
    _, dc_q, G["g_cq"] = _rms_bwd(c_q, W["g_cq"], dnq, name="rms_cq_bwd")
    dkv = jnp.concatenate([dkn, dva], axis=1)
    g0["w_ukv_t"] = _mm(dkv, nkv, dims="tn", name="proj_ukv_dw")
    dnkv = _mm(dkv, W["w_ukv_t"], name="proj_ukv_dx")
    _, dc_kv, G["g_ckv"] = _rms_bwd(c_kv, W["g_ckv"], dnkv, name="rms_ckv_bwd")
    tok = put_grads("mix0", g0)
    dqb, dkb, dvb = _sb_bwd(proj, COL_SB, o_b32, do_ab, MLA_HEADS // 2, tok)
    dkr_raw = _rope(dkr, cos_k, -sin_k, 0, 1, BF16, name="rope_k_bwd")
    dproj = jnp.concatenate([dc_q, dc_kv, dkr_raw, dqb, dkb, dvb], axis=1)
    du0 = _mm(dproj, W["w_in_t"], name="proj_in_dx")
    tok = put_grads("in0", {"w_in_t": _mm(dproj, u0, dims="tn", name="proj_in_dw")})
    dx, _, g_gmix0 = _rms_bwd(x, W["g_mix"][0:1] + tok[:1, :1], du0, dres=dh1, name="rms_mix_bwd0")
    G["g_mix"] = jnp.concatenate([g_gmix0, g_gmix1], axis=0)
    return loss[0, 0], dx, G


_BIG = ["ev_w_in", "ev_w_uq", "ev_w_ukv", "ev_w_out", "od_w_qkv", "od_w_out", "w_gate", "w_up", "w_down"]
_COL_SHARDED = {"ev_w_in", "ev_w_uq", "ev_w_ukv", "od_w_qkv", "w_gate", "w_up"}
_SMALL = ["ev_g_cq", "ev_g_ckv", "od_rel_bias", "g_mix", "g_ffn", "g_final"]
_GROUPS = {
    "in0": ["ev_w_in"],
    "mix0": ["ev_w_uq", "ev_w_ukv", "ev_w_out"],
    "ffn0": ["w_gate0", "w_up0", "w_down0"],
    "mix1": ["od_w_qkv", "od_w_out"],
    "ffn1": ["w_gate1", "w_up1", "w_down1"],
}
_GROUP_SRC = {n + str(l): (n, l) for n in ("w_gate", "w_up", "w_down") for l in (0, 1)}
_BATCHES = {"in0": ["in0"], "layer0": ["mix0", "ffn0"], "layer1": ["mix1", "ffn1"]}
_BATCH_OF = {grp: batch for batch, grps in _BATCHES.items() for grp in grps}
_SMALL_ROWS = 8
_SMALL_COLS = 1792


def _pack_small(vals):
    flat = jnp.concatenate([v.reshape(-1).astype(F32) for v in vals])
    flat = jnp.pad(flat, (0, _SMALL_ROWS * _SMALL_COLS - flat.shape[0]))
    return flat.reshape(_SMALL_ROWS, _SMALL_COLS)


def _unpack_small(packed, like):
    flat = packed.reshape(-1)
    out, off = [], 0
    for v in like:
        out.append(flat[off:off + v.size].reshape(v.shape))
        off += v.size
    return out


def kernel(x, ev_w_in, ev_g_cq, ev_w_uq, ev_g_ckv, ev_w_ukv, ev_w_out, od_w_qkv, od_rel_bias, od_w_out, g_mix, g_ffn, w_gate, w_up, w_down, g_final, loss_target, m_ev_w_in, m_ev_g_cq, m_ev_w_uq, m_ev_g_ckv, m_ev_w_ukv, m_ev_w_out, m_od_w_qkv, m_od_rel_bias, m_od_w_out, m_g_mix, m_g_ffn, m_w_gate, m_w_up, m_w_down, m_g_final, v_ev_w_in, v_ev_g_cq, v_ev_w_uq, v_ev_g_ckv, v_ev_w_ukv, v_ev_w_out, v_od_w_qkv, v_od_rel_bias, v_od_w_out, v_g_mix, v_g_ffn, v_w_gate, v_w_up, v_w_down, v_g_final):
    args = dict(locals())
    w = {n: args[n] for n in _BIG + _SMALL}
    mom = {n: args["m_" + n] for n in _BIG + _SMALL}
    var = {n: args["v_" + n] for n in _BIG + _SMALL}

    own = {}
    for grp, names in _GROUPS.items():
        for n in names:
            base, layer = _GROUP_SRC.get(n, (n, 0))
            shard = w[base][layer:layer + 1]
            own[n] = (jnp.swapaxes(shard, 1, 2) if base in _COL_SHARDED else shard).astype(BF16)
    placed = dict(zip(own, _place_own(list(own.values()), [False] * len(own), name="gather_own")))
    gather, token = {}, x[0, :8, :LANES]
    for grp, names in _GROUPS.items():
        gather[grp], token = _exchange_start([own[n] for n in names], [False] * len(names), token,
                                             name="gather_start_" + grp, lands=[placed[n] for n in names])

    def get_weights(grp, after):
        names = _GROUPS[grp]
        lands = _exchange_wait(gather[grp], token if after is None else after, name="gather_wait_" + grp)
        full = {n: l.reshape(-1, l.shape[-1]) for n, l in zip(names, lands)}
        if grp == "in0":
            return {"w_in_t": _w_in_local(full["ev_w_in"])}
        if grp == "mix0":
            return {"w_uq_t": _w_uq_local(full["ev_w_uq"]), "w_ukv_t": _w_ukv_local(full["ev_w_ukv"]),
                    "ev_w_out": full["ev_w_out"]}
        if grp == "mix1":
            return {"od_w_qkv_t": full["od_w_qkv"], "od_w_out": full["od_w_out"]}
        layer = grp[-1]
        return {"w_gate_t" + layer: full["w_gate" + layer], "w_up_t" + layer: full["w_up" + layer],
                "w_down" + layer: full["w_down" + layer]}

    scatter, pending = {}, {}

    def put_grads(grp, g):
        if grp == "in0":
            g = {"ev_w_in": _w_in_grad(g["w_in_t"])}
        elif grp == "mix0":
            g = {"ev_w_uq": _w_uq_grad(g["w_uq_t"]), "ev_w_ukv": _w_ukv_grad(g["w_ukv_t"]),
                 "ev_w_out": g["ev_w_out"]}
        elif grp == "mix1":
            g = {"od_w_qkv": g["od_w_qkv_t"], "od_w_out": g["od_w_out"]}
        else:
            layer = grp[-1]
            g = {"w_gate" + layer: g["w_gate_t"], "w_up" + layer: g["w_up_t"], "w_down" + layer: g["w_down"]}
        pending.update({n: v.reshape(N_DEV, 1, v.shape[0] // N_DEV, v.shape[1]).astype(BF16) for n, v in g.items()})
        batch = _BATCH_OF[grp]
        names = [n for gr in _BATCHES[batch] for n in _GROUPS[gr]]
        if not all(n in pending for n in names):
            return jnp.zeros((8, LANES), F32)
        send = [pending[n] for n in names]
        scatter[batch], tok = _exchange_start(send, [True] * len(names), send[0], name="scatter_start_" + batch)
        return tok

    small = {"g_cq": ev_g_cq, "g_ckv": ev_g_ckv, "od_rel_bias": od_rel_bias[0],
             "g_mix": g_mix + token[0, 0], "g_ffn": g_ffn, "g_final": g_final.reshape(1, -1)}
    loss_part, dx, G = _local_step(x[0], loss_target[0], small, get_weights, put_grads)
    g_small = _pack_small([G["g_cq"], G["g_ckv"], G["od_rel_bias"], G["g_mix"], G["g_ffn"], G["g_final"],
                           loss_part.reshape(1)])
    small_handle, _ = _exchange_start([g_small], [False], dx, name="gather_start_small")

    grads, deltas, new_m, new_v = {}, {}, {}, {}
    parts, after = {}, dx

    def wait_parts(batch, after):
        lands = _exchange_wait(scatter[batch], after, name="scatter_wait_" + batch)
        parts.update(zip([n for grp in _BATCHES[batch] for n in _GROUPS[grp]], lands))
        return lands[0]

    def adamw(n):
        col = n in _COL_SHARDED
        rows = lambda a: (jnp.swapaxes(a, 1, 2) if col else a).reshape(-1, a.shape[1 if col else 2])
        layers = [parts[n]] if n in parts else [parts[n + "0"], parts[n + "1"]]
        res = _adamw(rows(w[n]), [p.reshape(N_DEV, -1, p.shape[-1]) for p in layers], rows(mom[n]), rows(var[n]),
                     name="adamw_" + n)
        L, a1, a2 = w[n].shape
        back = lambda r: jnp.swapaxes(r.reshape(L, a2, a1), 1, 2) if col else r.reshape(L, a1, a2)
        grads[n], deltas[n], new_m[n], new_v[n] = [back(r) for r in res]
        return res[0]

    for batch in ("layer1", "layer0"):
        after = wait_parts(batch, after)
    for n in _BIG[1:]:
        after = adamw(n)
    after = wait_parts("in0", after)
    after = adamw("ev_w_in")
    small_w = [w[n] for n in _SMALL]
    small_parts = _exchange_wait(small_handle, after, name="gather_wait_small")[0]
    loss = jnp.sum(small_parts.reshape(N_DEV, -1)[:, sum(v.size for v in small_w)])
    res = _adamw(_pack_small(small_w), [small_parts], _pack_small([mom[n] for n in _SMALL]),
                 _pack_small([var[n] for n in _SMALL]), name="adamw_small")
    for d, packed in zip((grads, deltas, new_m, new_v), res):
        for n, val in zip(_SMALL, _unpack_small(packed, small_w)):
            d[n] = val

    order = ["ev_w_in", "ev_g_cq", "ev_w_uq", "ev_g_ckv", "ev_w_ukv", "ev_w_out", "od_w_qkv", "od_rel_bias",
             "od_w_out", "g_mix", "g_ffn", "w_gate", "w_up", "w_down", "g_final"]
    out = [loss, dx[None]]
    for d in (grads, deltas, new_m, new_v):
        out += [d[n] for n in order]
    return tuple(out)
```

```python
import functools

import numpy as np
import jax
import jax.numpy as jnp
from jax import lax
from jax.experimental import pallas as pl
from jax.experimental.pallas import tpu as pltpu

F32 = jnp.float32
BF16 = jnp.bfloat16

D_MODEL = 1024
CHUNK = 64
MLA_HEADS = 8
MLA_NOPE = 64
MLA_ROPE = 32
MLA_V = 64
Q_LORA = 384
KV_LORA = 256
ROPE_THETA = 10000.0
SB_HEADS = 8
SB_DIM = 64
C_HEADS = 16
C_DIM = 64
LEFT_CHUNKS = 8
REL_CLIP = 256
D_FF = 2816
RMS_EPS = 1e-6
ADAM_LR = 0.001
ADAM_B1 = 0.9
ADAM_B2 = 0.999
ADAM_EPS = 1e-08
ADAM_WD = 0.01
ADAM_STEP = 10

N_DEV = 8
LANES = 128
VMEM_LIMIT = 56 * 1024 * 1024
NEG = -1e30
PAD_KEYS = LEFT_CHUNKS * CHUNK
BAND_TQ = 128
BAND_W = BAND_TQ + PAD_KEYS
TOEP_W = BAND_W + BAND_TQ

NN = (((1,), (0,)), ((), ()))
NT = (((1,), (1,)), ((), ()))
TN = (((0,), (0,)), ((), ()))


def _dot(a, b, dn):
    return lax.dot_general(a, b, dn, preferred_element_type=F32)


def _pick(dim, pref):
    if dim <= pref:
        return dim
    best = None
    for t in range(LANES, pref + 1, LANES):
        if dim % t == 0:
            best = t
    assert best is not None, (dim, pref)
    return best


def _params(sem):
    return pltpu.CompilerParams(dimension_semantics=sem, vmem_limit_bytes=VMEM_LIMIT)


def _mm(a, b, dims="nn", res=None, out_dtype=F32, name="mm"):
    if dims == "nn":
        (M, K), (K2, N) = a.shape, b.shape
    elif dims == "nt":
        (M, K), (N, K2) = a.shape, b.shape
    else:
        (K, M), (K2, N) = a.shape, b.shape
    assert K == K2, (a.shape, b.shape, dims)
    tm, tn, tk = _pick(M, 1024), _pick(N, 1152), _pick(K, 1024)
    nk = K // tk
    dn = {"nn": NN, "nt": NT, "tn": TN}[dims]
    has_res = res is not None

    def body(*refs):
        if has_res:
            a_ref, b_ref, r_ref, o_ref, acc = refs
        else:
            a_ref, b_ref, o_ref, acc = refs
        k = pl.program_id(2)

        @pl.when(k == 0)
        def _():
            acc[...] = jnp.zeros_like(acc)

        acc[...] += _dot(a_ref[...].astype(BF16), b_ref[...].astype(BF16), dn)

        @pl.when(k == nk - 1)
        def _():
            r = acc[...]
            if has_res:
                r = r + r_ref[...]
            o_ref[...] = r.astype(out_dtype)

    a_spec = (pl.BlockSpec((tk, tm), lambda i, j, k: (k, i)) if dims == "tn"
              else pl.BlockSpec((tm, tk), lambda i, j, k: (i, k)))
    b_spec = (pl.BlockSpec((tn, tk), lambda i, j, k: (j, k)) if dims == "nt"
              else pl.BlockSpec((tk, tn), lambda i, j, k: (k, j)))
    o_spec = pl.BlockSpec((tm, tn), lambda i, j, k: (i, j))
    in_specs = [a_spec, b_spec] + ([o_spec] if has_res else [])
    args = (a, b) + ((res,) if has_res else ())
    return pl.pallas_call(
        body, name=name, grid=(M // tm, N // tn, nk),
        in_specs=in_specs, out_specs=o_spec,
        out_shape=jax.ShapeDtypeStruct((M, N), out_dtype),
        scratch_shapes=[pltpu.VMEM((tm, tn), F32)],
        compiler_params=_params(("parallel", "parallel", "arbitrary")),
    )(*args)


def _rms_fwd(x, g, out_dtype=BF16, name="rms_fwd"):
    T, Fd = x.shape
    tm = _pick(T, 256)

    def body(x_ref, g_ref, o_ref):
        xv = x_ref[...]
        r = lax.rsqrt(jnp.mean(xv * xv, axis=-1, keepdims=True) + RMS_EPS)
        o_ref[...] = (xv * r * g_ref[...]).astype(out_dtype)

    return pl.pallas_call(
        body, name=name, grid=(T // tm,),
        in_specs=[pl.BlockSpec((tm, Fd), lambda i: (i, 0)), pl.BlockSpec((1, Fd), lambda i: (0, 0))],
        out_specs=pl.BlockSpec((tm, Fd), lambda i: (i, 0)),
        out_shape=jax.ShapeDtypeStruct((T, Fd), out_dtype),
        compiler_params=_params(("parallel",)),
    )(x, g)


def _rms_bwd(x, g, dy, dres=None, name="rms_bwd"):
    T, Fd = x.shape
    tm = _pick(T, 256)
    has_res = dres is not None

    def body(*refs):
        if has_res:
            x_ref, g_ref, dy_ref, r_ref, dx_ref, dxb_ref, dg_ref = refs
        else:
            x_ref, g_ref, dy_ref, dx_ref, dxb_ref, dg_ref = refs
        xv, dyv = x_ref[...], dy_ref[...]
        r = lax.rsqrt(jnp.mean(xv * xv, axis=-1, keepdims=True) + RMS_EPS)
        gdy = dyv * g_ref[...]
        dot = jnp.mean(xv * gdy, axis=-1, keepdims=True)
        dx = r * gdy - xv * (r * r * r * dot)
        if has_res:
            dx = dx + r_ref[...]
        dx_ref[...] = dx
        dxb_ref[...] = dx.astype(BF16)

        @pl.when(pl.program_id(0) == 0)
        def _():
            dg_ref[...] = jnp.zeros_like(dg_ref)

        dg_ref[...] += jnp.sum(dyv * xv * r, axis=0, keepdims=True)

    row = pl.BlockSpec((tm, Fd), lambda i: (i, 0))
    vec = pl.BlockSpec((1, Fd), lambda i: (0, 0))
    in_specs = [row, vec, row] + ([row] if has_res else [])
    args = (x, g, dy) + ((dres,) if has_res else ())
    return pl.pallas_call(
        body, name=name, grid=(T // tm,),
        in_specs=in_specs, out_specs=[row, row, vec],
        out_shape=[jax.ShapeDtypeStruct((T, Fd), F32), jax.ShapeDtypeStruct((T, Fd), BF16),
                   jax.ShapeDtypeStruct((1, Fd), F32)],
        compiler_params=_params(("arbitrary",)),
    )(*args)


def _loss_head(h, g, target, name="loss_head"):
    T, Fd = h.shape
    tm = _pick(T, 256)

    def body(h_ref, g_ref, t_ref, loss_ref, dh_ref, dhb_ref, dg_ref):
        xv = h_ref[...]
        r = lax.rsqrt(jnp.mean(xv * xv, axis=-1, keepdims=True) + RMS_EPS)
        diff = xv * r * g_ref[...] - t_ref[...]
        part = 0.5 * jnp.sum(jnp.mean(diff * diff, axis=-1, keepdims=True), axis=0, keepdims=True)
        dyv = diff * (1.0 / Fd)
        gdy = dyv * g_ref[...]
        dot = jnp.mean(xv * gdy, axis=-1, keepdims=True)
        dh = r * gdy - xv * (r * r * r * dot)
        dh_ref[...] = dh
        dhb_ref[...] = dh.astype(BF16)

        @pl.when(pl.program_id(0) == 0)
        def _():
            dg_ref[...] = jnp.zeros_like(dg_ref)
            loss_ref[...] = jnp.zeros_like(loss_ref)

        dg_ref[...] += jnp.sum(dyv * xv * r, axis=0, keepdims=True)
        loss_ref[...] += jnp.broadcast_to(part, loss_ref.shape)

    row = pl.BlockSpec((tm, Fd), lambda i: (i, 0))
    vec = pl.BlockSpec((1, Fd), lambda i: (0, 0))
    return pl.pallas_call(
        body, name=name, grid=(T // tm,),
        in_specs=[row, vec, row],
        out_specs=[pl.BlockSpec((1, LANES), lambda i: (0, 0)), row, row, vec],
        out_shape=[jax.ShapeDtypeStruct((1, LANES), F32), jax.ShapeDtypeStruct((T, Fd), F32),
                   jax.ShapeDtypeStruct((T, Fd), BF16), jax.ShapeDtypeStruct((1, Fd), F32)],
        compiler_params=_params(("arbitrary",)),
    )(h, g, target)


FFN_TF = 256


def _ffn_fwd(h, g, wg_t, wu_t, wd, name="ffn_fwd"):
    T, Dm = h.shape
    Fh = wd.shape[0]
    tm = _pick(T, 1024)
    nf = Fh // FFN_TF

    def body(h_ref, g_ref, wg_ref, wu_ref, wd_ref, o_ref, u_ref, a_ref, b_ref):
        j = pl.program_id(1)

        @pl.when(j == 0)
        def _():
            xv = h_ref[...]
            r = lax.rsqrt(jnp.mean(xv * xv, axis=-1, keepdims=True) + RMS_EPS)
            u_ref[...] = (xv * r * g_ref[...]).astype(BF16)
            o_ref[...] = xv

        u = u_ref[...]
        a = _dot(u, wg_ref[...], NT).astype(BF16)
        b = _dot(u, wu_ref[...], NT).astype(BF16)
        a_ref[...] = a
        b_ref[...] = b
        af = a.astype(F32)
        s = (af * jax.nn.sigmoid(af) * b.astype(F32)).astype(BF16)
        o_ref[...] += _dot(s, wd_ref[...], NN)

    row = pl.BlockSpec((tm, Dm), lambda i, j: (i, 0))
    wblk = pl.BlockSpec((FFN_TF, Dm), lambda i, j: (j, 0))
    ablk = pl.BlockSpec((tm, FFN_TF), lambda i, j: (i, j))
    return pl.pallas_call(
        body, name=name, grid=(T // tm, nf),
        in_specs=[row, pl.BlockSpec((1, Dm), lambda i, j: (0, 0)), wblk, wblk, wblk],
        out_specs=[row, row, ablk, ablk],
        out_shape=[jax.ShapeDtypeStruct((T, Dm), F32), jax.ShapeDtypeStruct((T, Dm), BF16),
                   jax.ShapeDtypeStruct((T, Fh), BF16), jax.ShapeDtypeStruct((T, Fh), BF16)],
        compiler_params=_params(("parallel", "arbitrary")),
    )(h, g, wg_t, wu_t, wd)


def _ffn_bwd(dh, u, a, b, wg_t, wu_t, wd, name="ffn_bwd"):
    T, Dm = dh.shape
    Fh = wd.shape[0]
    nf = Fh // FFN_TF
    once = pl.Buffered(1)

    def body(dh_ref, u_ref, a_ref, b_ref, wg_ref, wu_ref, wd_ref, du_ref, dwg_ref, dwu_ref, dwd_ref):
        j = pl.program_id(0)

        @pl.when(j == 0)
        def _():
            du_ref[...] = jnp.zeros_like(du_ref)

        ds = _dot(dh_ref[...], wd_ref[...], NT)
        af, bf = a_ref[...].astype(F32), b_ref[...].astype(F32)
        sig = jax.nn.sigmoid(af)
        sa = af * sig
        dwd_ref[...] = _dot((sa * bf).astype(BF16), dh_ref[...], TN).astype(BF16)
        dab = jnp.concatenate([(ds * bf * (sig * (1.0 + af * (1.0 - sig)))).astype(BF16),
                               (ds * sa).astype(BF16)], axis=1)
        dw = _dot(dab, u_ref[...], TN)
        dwg_ref[...] = dw[:FFN_TF].astype(BF16)
        dwu_ref[...] = dw[FFN_TF:].astype(BF16)
        du_ref[...] += _dot(dab, jnp.concatenate([wg_ref[...], wu_ref[...]], axis=0), NN)

    full = lambda: pl.BlockSpec((T, Dm), lambda j: (0, 0), pipeline_mode=once)
    wblk = pl.BlockSpec((FFN_TF, Dm), lambda j: (j, 0))
    ablk = pl.BlockSpec((T, FFN_TF), lambda j: (0, j))
    return pl.pallas_call(
        body, name=name, grid=(nf,),
        in_specs=[full(), full(), ablk, ablk, wblk, wblk, wblk],
        out_specs=[pl.BlockSpec((T, Dm), lambda j: (0, 0)), wblk, wblk, wblk],
        out_shape=[jax.ShapeDtypeStruct((T, Dm), F32)] + [jax.ShapeDtypeStruct((Fh, Dm), BF16)] * 3,
        compiler_params=_params(("arbitrary",)),
    )(dh, u, a, b, wg_t, wu_t, wd)


def _rope(x, cos_t, sin_t, col0, ncols, out_dtype, name="rope"):
    T = x.shape[0]
    wt = cos_t.shape[1]
    tm = _pick(T, 256)
    nb = ncols * LANES // wt
    half = MLA_ROPE // 2

    def body(x_ref, c_ref, s_ref, o_ref):
        xv = x_ref[...].astype(F32)
        lane = lax.broadcasted_iota(jnp.int32, xv.shape, 1)
        first = (lane & (MLA_ROPE - 1)) < half
        swapped = jnp.where(first, pltpu.roll(xv, wt - half, 1), pltpu.roll(xv, half, 1))
        o_ref[...] = (xv * c_ref[...] + swapped * s_ref[...]).astype(out_dtype)

    off = col0 * LANES // wt
    return pl.pallas_call(
        body, name=name, grid=(T // tm, nb),
        in_specs=[pl.BlockSpec((tm, wt), lambda i, j: (i, j + off)),
                  pl.BlockSpec((tm, wt), lambda i, j: (i, 0)),
                  pl.BlockSpec((tm, wt), lambda i, j: (i, 0))],
        out_specs=pl.BlockSpec((tm, wt), lambda i, j: (i, j)),
        out_shape=jax.ShapeDtypeStruct((T, ncols * LANES), out_dtype),
        compiler_params=_params(("parallel", "parallel")),
    )(x, cos_t, sin_t)


ATT_TQ = 512
ATT_TK = 256


def _mla_masks(shape):
    lane = lax.broadcasted_iota(jnp.int32, shape, 1)
    m0 = (lane < 64) | ((lane >= 128) & (lane < 160))
    m1 = ((lane >= 64) & (lane < 128)) | ((lane >= 160) & (lane < 192))
    return m0, m1


def _by_twos(n, step, carry):
    carry = lax.fori_loop(0, n // 2, lambda i, c: step(2 * i + 1, step(2 * i, c)), carry)
    return lax.fori_loop(0, n % 2, lambda _, c: step(n - 1, c), carry)


def _chunk_ok(tq, tk, d):
    row = lax.broadcasted_iota(jnp.int32, (tq, tk), 0)
    col = lax.broadcasted_iota(jnp.int32, (tq, tk), 1) + d * tk
    return jnp.concatenate([(col >> 6) <= (row >> 6)] * 2, axis=0)


def _rotate(x, cos_t, sin_t):
    half = MLA_ROPE // 2
    lane = lax.broadcasted_iota(jnp.int32, x.shape, 1)
    first = (lane & (MLA_ROPE - 1)) < half
    swapped = jnp.where(first, pltpu.roll(x, x.shape[1] - half, 1), pltpu.roll(x, half, 1))
    return x * cos_t + swapped * sin_t


def _mla_fwd(q, cos_q, sin_q, kv, kr, name="mla_fwd"):
    T = q.shape[0]
    tq, tk = _pick(T, ATT_TQ), _pick(T, ATT_TK)
    nd = tq // tk
    npair = MLA_HEADS // 2
    scale = (MLA_NOPE + MLA_ROPE) ** -0.5

    def body(q_ref, c_ref, s_ref, kn_ref, v_ref, kr_ref, o_ref, lse_ref):
        m_idx = pl.program_id(1)
        qv = _rotate(q_ref[...], c_ref[...], s_ref[...]).astype(BF16)
        m0, m1 = _mla_masks(qv.shape)
        qs = jnp.concatenate([jnp.where(m0, qv, 0), jnp.where(m1, qv, 0)], axis=0).astype(BF16)

        def block(kb, carry, ok):
            ks = pl.ds(pl.multiple_of(kb * tk, tk), tk)
            kcat = jnp.concatenate([kn_ref[ks, :], kr_ref[ks, :]], axis=1)
            mx, l, acc = carry
            s = _dot(qs, kcat, NT) * scale
            if ok is not None:
                s = jnp.where(ok, s, NEG)
            mn = jnp.maximum(mx, jnp.max(s, axis=-1, keepdims=True))
            alpha = jnp.exp(mx - mn)
            p = jnp.exp(s - mn)
            return (mn, alpha * l + jnp.sum(p, axis=-1, keepdims=True),
                    alpha * acc + _dot(p.astype(BF16), v_ref[ks, :], NN))

        init = (jnp.full((2 * tq, 1), NEG, F32), jnp.zeros((2 * tq, 1), F32), jnp.zeros((2 * tq, LANES), F32))
        res = init
        for d in range(nd):
            res = block(m_idx * nd + d, res, _chunk_ok(tq, tk, d))
        mx, l, acc = _by_twos(m_idx * nd, lambda kb, c: block(kb, c, None), res)
        h0 = lax.broadcasted_iota(jnp.int32, (tq, LANES), 1) < 64
        o_ref[...] = _two_heads(acc * (1.0 / l), h0).astype(o_ref.dtype)
        lse_ref[...] = _two_heads(jnp.broadcast_to(mx + jnp.log(l), (2 * tq, LANES)), h0)

    full = lambda col: pl.BlockSpec((T, LANES), col)
    table = pl.BlockSpec((tq, 2 * LANES), lambda p, m: (m, 0))
    return pl.pallas_call(
        body, name=name, grid=(npair, T // tq),
        in_specs=[pl.BlockSpec((tq, 2 * LANES), lambda p, m: (m, p)), table, table,
                  full(lambda p, m: (0, p)), full(lambda p, m: (0, npair + p)), full(lambda p, m: (0, 0))],
        out_specs=[pl.BlockSpec((tq, LANES), lambda p, m: (m, p)),
                   pl.BlockSpec((tq, LANES), lambda p, m: (m, p))],
        out_shape=[jax.ShapeDtypeStruct((T, npair * LANES), BF16),
                   jax.ShapeDtypeStruct((T, npair * LANES), F32)],
        compiler_params=_params(("parallel", "arbitrary")),
    )(q, cos_q, sin_q, kv, kv, kr)


def _mla_bwd(q, cos_q, sin_q, kv, kr, o, lse, do, do_col0, name="mla_bwd"):
    T = q.shape[0]
    tq, tk = _pick(T, ATT_TQ), _pick(T, ATT_TK)
    nd = tq // tk
    npair = MLA_HEADS // 2
    scale = (MLA_NOPE + MLA_ROPE) ** -0.5

    def body(q_ref, c_ref, s_ref, kn_ref, v_ref, kr_ref, o_ref, lse_ref, do_ref, dq_ref, dkn_ref, dv_ref, dkr_ref,
             dkn_acc, dv_acc):
        p_idx, m_idx = pl.program_id(0), pl.program_id(1)

        @pl.when(m_idx == 0)
        def _():
            dkn_acc[...] = jnp.zeros_like(dkn_acc)
            dv_acc[...] = jnp.zeros_like(dv_acc)

        @pl.when((m_idx == 0) & (p_idx == 0))
        def _():
            dkr_ref[...] = jnp.zeros_like(dkr_ref)

        qv = _rotate(q_ref[...], c_ref[...], s_ref[...]).astype(BF16)
        m0, m1 = _mla_masks(qv.shape)
        qs = jnp.concatenate([jnp.where(m0, qv, 0), jnp.where(m1, qv, 0)], axis=0).astype(BF16)
        dov = do_ref[...].astype(F32)
        h0 = lax.broadcasted_iota(jnp.int32, (tq, LANES), 1) < 64
        dos32 = jnp.concatenate([jnp.where(h0, dov, 0.0), jnp.where(h0, 0.0, dov)], axis=0)
        ov = o_ref[...].astype(F32)
        delta = jnp.sum(dos32 * jnp.concatenate([ov, ov], axis=0), axis=-1, keepdims=True)
        dos = dos32.astype(BF16)
        lsev = lse_ref[...]
        lse = jnp.concatenate([lsev[:, 0:1], lsev[:, 64:65]], axis=0)

        def block(kb, dq, ok):
            ks = pl.ds(pl.multiple_of(kb * tk, tk), tk)
            kcat = jnp.concatenate([kn_ref[ks, :], kr_ref[ks, :]], axis=1)
            vv = v_ref[ks, :]
            p = jnp.exp(_dot(qs, kcat, NT) * scale - lse)
            if ok is not None:
                p = jnp.where(ok, p, 0.0)
            ds = (p * (_dot(dos, vv, NT) - delta) * scale).astype(BF16)
            dkc = _dot(ds, qs, TN)
            dkn_acc[ks, :] += dkc[:, :LANES]
            dkr_ref[ks, :] += dkc[:, LANES:]
            dv_acc[ks, :] += _dot(p.astype(BF16), dos, TN)
            return dq + _dot(ds, kcat, NN)

        dq = jnp.zeros((2 * tq, 2 * LANES), F32)
        for d in range(nd):
            dq = block(m_idx * nd + d, dq, _chunk_ok(tq, tk, d))
        dq = _by_twos(m_idx * nd, lambda kb, c: block(kb, c, None), dq)
        dq_ref[...] = _rotate(jnp.where(m0, dq[:tq], jnp.where(m1, dq[tq:], 0.0)), c_ref[...],
                              -s_ref[...]).astype(BF16)

        @pl.when(m_idx == T // tq - 1)
        def _():
            dkn_ref[...] = dkn_acc[...].astype(BF16)
            dv_ref[...] = dv_acc[...].astype(BF16)

    full = lambda col: pl.BlockSpec((T, LANES), col)
    blk = lambda col: pl.BlockSpec((tq, LANES), col)
    table = pl.BlockSpec((tq, 2 * LANES), lambda p, m: (m, 0))
    return pl.pallas_call(
        body, name=name, grid=(npair, T // tq),
        in_specs=[pl.BlockSpec((tq, 2 * LANES), lambda p, m: (m, p)), table, table,
                  full(lambda p, m: (0, p)), full(lambda p, m: (0, npair + p)), full(lambda p, m: (0, 0)),
                  blk(lambda p, m: (m, p)), blk(lambda p, m: (m, p)),
                  blk(lambda p, m: (m, do_col0 + p))],
        out_specs=[pl.BlockSpec((tq, 2 * LANES), lambda p, m: (m, p)),
                   full(lambda p, m: (0, p)), full(lambda p, m: (0, p)), full(lambda p, m: (0, 0))],
        out_shape=[jax.ShapeDtypeStruct((T, npair * 2 * LANES), BF16),
                   jax.ShapeDtypeStruct((T, npair * LANES), BF16),
                   jax.ShapeDtypeStruct((T, npair * LANES), BF16),
                   jax.ShapeDtypeStruct((T, LANES), F32)],
        scratch_shapes=[pltpu.VMEM((T, LANES), F32)] * 2,
        compiler_params=_params(("arbitrary", "arbitrary")),
    )(q, cos_q, sin_q, kv, kv, kr, o, lse, do)


def _split_dot(x, tri):
    hi = x.astype(BF16)
    lo = (x - hi.astype(F32)).astype(BF16)
    both = _dot(jnp.concatenate([hi, lo], axis=0), tri, NN)
    return both[:x.shape[0]] + both[x.shape[0]:]


def _sb_terms(qh, kk, before):
    z = _dot(qh, kk, NT)
    sp = jnp.maximum(z, 0.0) + jnp.log(1.0 + jnp.exp(-jnp.abs(z)))
    lk = -sp if before is None else jnp.where(before, -sp, 0.0)
    return z, sp, lk


def _sb_setup(q_ref, tq, tk, scale):
    qv = (q_ref[...].astype(F32) * scale).astype(BF16)
    lane = lax.broadcasted_iota(jnp.int32, (tq, LANES), 1)
    h0 = lane < 64
    qs = jnp.concatenate([jnp.where(h0, qv, 0), jnp.where(h0, 0, qv)], axis=0).astype(BF16)
    row = lax.broadcasted_iota(jnp.int32, (tk, tk), 0)
    col = lax.broadcasted_iota(jnp.int32, (tk, tk), 1)
    return qs, h0, row, col


def _sb_before(tq, tk, d):
    row = lax.broadcasted_iota(jnp.int32, (tq, tk), 0)
    col = lax.broadcasted_iota(jnp.int32, (tq, tk), 1) + d * tk
    return jnp.concatenate([col < row] * 2, axis=0)


def _two_heads(x, h0):
    tq = x.shape[0] // 2
    return jnp.where(h0, x[:tq], x[tq:])


def _sb_fwd(qkv, col0, name="sb_fwd"):
    T = qkv.shape[0]
    tq, tk = _pick(T, ATT_TQ), _pick(T, ATT_TK)
    nd = tq // tk
    npair = SB_HEADS // 2
    scale = SB_DIM ** -0.5

    def body(q_ref, k_ref, v_ref, o_ref, o32_ref, w_ref):
        m_idx = pl.program_id(1)
        qs, h0, row, col = _sb_setup(q_ref, tq, tk, scale)
        later = (row > col).astype(BF16)

        def block(kb, carry, before):
            ks = pl.ds(pl.multiple_of(kb * tk, tk), tk)
            c, acc = carry
            z, sp, lk = _sb_terms(qs, k_ref[ks, :].astype(BF16), before)
            w = jnp.exp((z - sp) + _split_dot(lk, later) + c)
            if before is not None:
                w = jnp.where(before, w, 0.0)
            wb = w.astype(BF16)
            w_ref[0, 0, kb] = wb
            return (c + jnp.sum(lk, axis=-1, keepdims=True), acc + _dot(wb, v_ref[ks, :].astype(BF16), NN))

        init = (jnp.zeros((2 * tq, 1), F32), jnp.zeros((2 * tq, LANES), F32))
        res = init
        for d in reversed(range(nd)):
            res = block(m_idx * nd + d, res, _sb_before(tq, tk, d))
        res = _by_twos(m_idx * nd, lambda i, c: block(m_idx * nd - 1 - i, c, None), res)
        o = _two_heads(res[1], h0)
        o_ref[...] = o.astype(o_ref.dtype)
        o32_ref[...] = o

    full = lambda col: pl.BlockSpec((T, LANES), col)
    blk = pl.BlockSpec((tq, LANES), lambda p, m: (m, p))
    return pl.pallas_call(
        body, name=name, grid=(npair, T // tq),
        in_specs=[pl.BlockSpec((tq, LANES), lambda p, m: (m, col0 + p)),
                  full(lambda p, m: (0, col0 + npair + p)), full(lambda p, m: (0, col0 + 2 * npair + p))],
        out_specs=[blk, blk, pl.BlockSpec((1, 1, T // tk, 2 * tq, tk), lambda p, m: (p, m, 0, 0, 0))],
        out_shape=[jax.ShapeDtypeStruct((T, npair * LANES), BF16), jax.ShapeDtypeStruct((T, npair * LANES), F32),
                   jax.ShapeDtypeStruct((npair, T // tq, T // tk, 2 * tq, tk), BF16)],
        compiler_params=_params(("parallel", "arbitrary")),
    )(qkv, qkv, qkv)


def _sb_bwd(qkv, col0, o32, w_all, do, do_col0, dep, name="sb_bwd"):
    T = qkv.shape[0]
    tq, tk = _pick(T, ATT_TQ), _pick(T, ATT_TK)
    nd = tq // tk
    npair = SB_HEADS // 2
    scale = SB_DIM ** -0.5

    def body(q_ref, k_ref, v_ref, o_ref, w_ref, do_ref, dep_ref, dq_ref, dk_ref, dv_ref, dk_acc, dv_acc):
        m_idx = pl.program_id(1)

        @pl.when(m_idx == 0)
        def _():
            dk_acc[...] = jnp.zeros_like(dk_acc)
            dv_acc[...] = jnp.zeros_like(dv_acc)

        qs, h0, row, col = _sb_setup(q_ref, tq, tk, scale)
        dov = do_ref[...].astype(F32)
        dos = jnp.concatenate([jnp.where(h0, dov, 0.0), jnp.where(h0, 0.0, dov)], axis=0).astype(BF16)
        ov = o_ref[...]
        etot = jnp.sum(dos.astype(F32) * jnp.concatenate([ov, ov], axis=0), axis=-1, keepdims=True)
        from_here = (row >= col).astype(BF16)

        def block(kb, carry, before):
            ks = pl.ds(pl.multiple_of(kb * tk, tk), tk)
            kk = k_ref[ks, :].astype(BF16)
            vv = v_ref[ks, :].astype(BF16)
            es, dqa = carry
            _, sp, _ = _sb_terms(qs, kk, None)
            wb = w_ref[0, 0, kb]
            e = wb.astype(F32) * _dot(dos, vv, NT)
            prev = etot - (_split_dot(e, from_here) + es)
            sig_neg = jnp.exp(-sp)
            dz = e * sig_neg - (1.0 - sig_neg) * prev
            if before is not None:
                dz = jnp.where(before, dz, 0.0)
            dzb = dz.astype(BF16)
            dk_acc[ks, :] += _dot(dzb, qs, TN)
            dv_acc[ks, :] += _dot(wb, dos, TN)
            return es + jnp.sum(e, axis=-1, keepdims=True), dqa + _dot(dzb, kk, NN)

        init = (jnp.zeros((2 * tq, 1), F32), jnp.zeros((2 * tq, LANES), F32))
        res = init
        for d in reversed(range(nd)):
            res = block(m_idx * nd + d, res, _sb_before(tq, tk, d))
        res = _by_twos(m_idx * nd, lambda i, c: block(m_idx * nd - 1 - i, c, None), res)
        dq_ref[...] = (_two_heads(res[1], h0) * scale).astype(BF16)

        @pl.when(m_idx == T // tq - 1)
        def _():
            dk_ref[...] = dk_acc[...].astype(BF16)
            dv_ref[...] = dv_acc[...].astype(BF16)

    full = lambda col: pl.BlockSpec((T, LANES), col)
    blk = lambda col: pl.BlockSpec((tq, LANES), col)
    return pl.pallas_call(
        body, name=name, grid=(npair, T // tq),
        in_specs=[blk(lambda p, m: (m, col0 + p)),
                  full(lambda p, m: (0, col0 + npair + p)), full(lambda p, m: (0, col0 + 2 * npair + p)),
                  blk(lambda p, m: (m, p)),
                  pl.BlockSpec((1, 1, T // tk, 2 * tq, tk), lambda p, m: (p, m, 0, 0, 0)),
                  blk(lambda p, m: (m, do_col0 + p)), pl.BlockSpec((8, LANES), lambda p, m: (0, 0))],
        out_specs=[blk(lambda p, m: (m, p)), full(lambda p, m: (0, p)), full(lambda p, m: (0, p))],
        out_shape=[jax.ShapeDtypeStruct((T, npair * LANES), BF16)] * 3,
        scratch_shapes=[pltpu.VMEM((T, LANES), F32)] * 2,
        compiler_params=_params(("arbitrary", "arbitrary")),
    )(qkv, qkv, qkv, o32, w_all, do, dep)


def _band_in_window():
    cq = lax.broadcasted_iota(jnp.int32, (BAND_TQ, BAND_W), 0) >> 6
    ckp = lax.broadcasted_iota(jnp.int32, (BAND_TQ, BAND_W), 1) >> 6
    return (ckp >= cq) & (ckp <= cq + LEFT_CHUNKS)


def _band_real(m_idx):
    j = lax.broadcasted_iota(jnp.int32, (BAND_TQ, BAND_W), 1)
    return j >= PAD_KEYS - m_idx * BAND_TQ


def _band_probs(qh, kw, bias, real, scale):
    s = jnp.where(real, _dot(qh, kw, NT) * scale + bias, NEG)
    e = jnp.exp(s - jnp.max(s, axis=-1, keepdims=True))
    return e * (1.0 / jnp.sum(e, axis=-1, keepdims=True))


BAND_SUB = 4


def _band_fwd(qkv, k_pad, v_pad, bias_w, name="band_fwd"):
    T = qkv.shape[0]
    npair = C_HEADS // 2
    scale = C_DIM ** -0.5
    rows = BAND_SUB * BAND_TQ

    def body(q_ref, k_ref, v_ref, b_ref, o_ref, p_ref):
        lane = lax.broadcasted_iota(jnp.int32, (BAND_TQ, LANES), 1)
        h0 = lane < 64
        bias = jnp.concatenate([b_ref[0], b_ref[1]], axis=0)
        for sub in range(BAND_SUB):
            m_idx = pl.program_id(1) * BAND_SUB + sub
            win = pl.ds(pl.multiple_of(m_idx * BAND_TQ, BAND_TQ), BAND_W)
            kw, vw = k_ref[win, :], v_ref[win, :]
            qv = q_ref[sub * BAND_TQ:(sub + 1) * BAND_TQ, :]
            qs = jnp.concatenate([jnp.where(h0, qv, 0), jnp.where(h0, 0, qv)], axis=0).astype(BF16)
            p = _band_probs(qs, kw, bias, jnp.concatenate([_band_real(m_idx)] * 2, axis=0), scale).astype(BF16)
            p_ref[0, sub] = p
            o = _two_heads(_dot(p, vw, NN), h0)
            o_ref[sub * BAND_TQ:(sub + 1) * BAND_TQ, :] = o.astype(o_ref.dtype)

    Tp = T + PAD_KEYS
    return pl.pallas_call(
        body, name=name, grid=(npair, T // rows),
        in_specs=[pl.BlockSpec((rows, LANES), lambda p, m: (m, p)),
                  pl.BlockSpec((Tp, LANES), lambda p, m: (0, p)),
                  pl.BlockSpec((Tp, LANES), lambda p, m: (0, p)),
                  pl.BlockSpec((2, BAND_TQ, BAND_W), lambda p, m: (p, 0, 0))],
        out_specs=[pl.BlockSpec((rows, LANES), lambda p, m: (m, p)),
                   pl.BlockSpec((1, BAND_SUB, 2 * BAND_TQ, BAND_W), lambda p, m: (p, m, 0, 0))],
        out_shape=[jax.ShapeDtypeStruct((T, npair * LANES), BF16),
                   jax.ShapeDtypeStruct((npair, T // BAND_TQ, 2 * BAND_TQ, BAND_W), BF16)],
        compiler_params=_params(("parallel", "arbitrary")),
    )(qkv, k_pad, v_pad, bias_w)


def _band_bwd(qkv, k_pad, v_pad, probs, do, name="band_bwd"):
    T = qkv.shape[0]
    npair = C_HEADS // 2
    scale = C_DIM ** -0.5

    rows = BAND_SUB * BAND_TQ

    def body(q_ref, k_ref, v_ref, p_ref, do_ref, dq_ref, dk_ref, dv_ref, db_ref, dk_acc, dv_acc):
        @pl.when(pl.program_id(1) == 0)
        def _():
            dk_acc[...] = jnp.zeros_like(dk_acc)
            dv_acc[...] = jnp.zeros_like(dv_acc)
            db_ref[...] = jnp.zeros_like(db_ref)

        lane = lax.broadcasted_iota(jnp.int32, (BAND_TQ, LANES), 1)
        h0 = lane < 64
        dbs = jnp.zeros((2 * BAND_TQ, BAND_W), F32)
        for sub in range(BAND_SUB):
            m_idx = pl.program_id(1) * BAND_SUB + sub
            win = pl.ds(pl.multiple_of(m_idx * BAND_TQ, BAND_TQ), BAND_W)
            kw, vw = k_ref[win, :], v_ref[win, :]
            qv = q_ref[sub * BAND_TQ:(sub + 1) * BAND_TQ, :]
            dov = do_ref[sub * BAND_TQ:(sub + 1) * BAND_TQ, :].astype(F32)
            qs = jnp.concatenate([jnp.where(h0, qv, 0), jnp.where(h0, 0, qv)], axis=0).astype(BF16)
            dos = jnp.concatenate([jnp.where(h0, dov, 0.0), jnp.where(h0, 0.0, dov)], axis=0).astype(BF16)
            pb = p_ref[0, sub]
            p = pb.astype(F32)
            dp = _dot(dos, vw, NT)
            dsb = p * (dp - jnp.sum(p * dp, axis=-1, keepdims=True))
            dbs = dbs + dsb
            dsq = (dsb * scale).astype(BF16)
            dq_ref[sub * BAND_TQ:(sub + 1) * BAND_TQ, :] = _two_heads(_dot(dsq, kw, NN), h0).astype(BF16)
            dk_acc[win, :] += _dot(dsq, qs, TN)
            dv_acc[win, :] += _dot(pb, dos, TN)
        db_ref[0] += dbs[:BAND_TQ]
        db_ref[1] += dbs[BAND_TQ:]

        @pl.when(pl.program_id(1) == T // rows - 1)
        def _():
            dk_ref[...] = dk_acc[...].astype(BF16)
            dv_ref[...] = dv_acc[...].astype(BF16)

    Tp = T + PAD_KEYS
    blk = lambda col: pl.BlockSpec((rows, LANES), col)
    full = pl.BlockSpec((Tp, LANES), lambda p, m: (0, p))
    bias = pl.BlockSpec((2, BAND_TQ, BAND_W), lambda p, m: (p, 0, 0))
    prob = pl.BlockSpec((1, BAND_SUB, 2 * BAND_TQ, BAND_W), lambda p, m: (p, m, 0, 0))
    return pl.pallas_call(
        body, name=name, grid=(npair, T // rows),
        in_specs=[blk(lambda p, m: (m, p)), full, full, prob, blk(lambda p, m: (m, p))],
        out_specs=[blk(lambda p, m: (m, p)), full, full, bias],
        out_shape=[jax.ShapeDtypeStruct((T, npair * LANES), BF16),
                   jax.ShapeDtypeStruct((Tp, npair * LANES), BF16),
                   jax.ShapeDtypeStruct((Tp, npair * LANES), BF16),
                   jax.ShapeDtypeStruct((C_HEADS, BAND_TQ, BAND_W), F32)],
        scratch_shapes=[pltpu.VMEM((Tp, LANES), F32)] * 2,
        compiler_params=_params(("arbitrary", "arbitrary")),
    )(qkv, k_pad, v_pad, probs, do)


def _skew_bits(x, left):
    w = x.shape[1]
    row = lax.broadcasted_iota(jnp.int32, x.shape, 0)
    for b in range(BAND_TQ.bit_length() - 1):
        amt = (w - (1 << b)) if left else (1 << b)
        x = jnp.where((row >> b) & 1 == 1, pltpu.roll(x, amt, 1), x)
    return x


def _toeplitz(diag, name="toeplitz"):
    H = diag.shape[0]

    def body(d_ref, o_ref):
        x = jnp.broadcast_to(d_ref[0], (BAND_TQ, TOEP_W))
        o_ref[0] = jnp.where(_band_in_window(), _skew_bits(x, left=False)[:, BAND_TQ:], NEG)

    return pl.pallas_call(
        body, name=name, grid=(H,),
        in_specs=[pl.BlockSpec((1, 1, TOEP_W), lambda h: (h, 0, 0))],
        out_specs=pl.BlockSpec((1, BAND_TQ, BAND_W), lambda h: (h, 0, 0)),
        out_shape=jax.ShapeDtypeStruct((H, BAND_TQ, BAND_W), F32),
        compiler_params=_params(("parallel",)),
    )(diag.reshape(H, 1, TOEP_W))


def _toeplitz_bwd(dbias, name="toeplitz_bwd"):
    H = dbias.shape[0]

    def body(d_ref, o_ref):
        x = jnp.concatenate([jnp.zeros((BAND_TQ, BAND_TQ), F32), d_ref[0]], axis=1)
        o_ref[0] = jnp.sum(_skew_bits(x, left=True), axis=0, keepdims=True)

    return pl.pallas_call(
        body, name=name, grid=(H,),
        in_specs=[pl.BlockSpec((1, BAND_TQ, BAND_W), lambda h: (h, 0, 0))],
        out_specs=pl.BlockSpec((1, 1, TOEP_W), lambda h: (h, 0, 0)),
        out_shape=jax.ShapeDtypeStruct((H, 1, TOEP_W), F32),
        compiler_params=_params(("parallel",)),
    )(dbias).reshape(H, TOEP_W)


_HBM = pl.BlockSpec(memory_space=pltpu.HBM)
_SEM = pl.BlockSpec(memory_space=pltpu.SEMAPHORE)
_EFFECT = pltpu.SideEffectType.DATAFLOW_SIDE_EFFECTING


def _peers():
    x, y, c = lax.axis_index("x"), lax.axis_index("y"), lax.axis_index("c")
    out = []
    for k in range(1, N_DEV):
        peer = (1 - x if (k >> 2) & 1 else x, 1 - y if (k >> 1) & 1 else y, 1 - c if k & 1 else c)
        out.append((peer, 4 * peer[0] + 2 * peer[1] + peer[2]))
    return 4 * x + 2 * y + c, out


def _split_copies(ins, lands, scatter, send_sem, recv_sem, arriving):
    me, peers = _peers()
    out = []
    for a in range(len(ins)):
        for peer, idx in peers:
            out.append(pltpu.make_async_remote_copy(
                src_ref=ins[a].at[idx] if scatter[a] else ins[a],
                dst_ref=lands[a].at[idx if arriving else me], send_sem=send_sem, recv_sem=recv_sem,
                device_id=peer, device_id_type=pl.DeviceIdType.MESH))
    return out


def _landing_zones(arrays, scatter):
    return [lax.empty((N_DEV,) + (a.shape[1:] if s else a.shape), a.dtype) for a, s in zip(arrays, scatter)]


def _place_own(arrays, scatter, name):
    n = len(arrays)
    lands = _landing_zones(arrays, scatter)
    me = (4 * lax.axis_index("x") + 2 * lax.axis_index("y") + lax.axis_index("c")).astype(jnp.int32).reshape(1)

    def body(me_ref, *refs):
        for a in range(n):
            refs[2 * n + a][...] = refs[a][...].reshape(refs[2 * n + a].shape)

    def row_spec(shape):
        zeros = (0,) * (len(shape) - 1)
        return pl.BlockSpec((1,) + tuple(shape[1:]), lambda i, me_ref: (me_ref[0],) + zeros)

    in_specs = [row_spec(a.shape) if s else pl.BlockSpec(a.shape, lambda i, me_ref, nd=a.ndim: (0,) * nd)
                for a, s in zip(arrays, scatter)]
    return pl.pallas_call(
        body, name=name,
        out_shape=[jax.ShapeDtypeStruct(l.shape, l.dtype) for l in lands],
        grid_spec=pltpu.PrefetchScalarGridSpec(
            num_scalar_prefetch=1, grid=(1,),
            in_specs=in_specs + [pl.BlockSpec(memory_space=pl.ANY)] * n,
            out_specs=[row_spec(l.shape) for l in lands]),
        input_output_aliases={1 + n + i: i for i in range(n)},
        compiler_params=_params(("arbitrary",)),
    )(me, *arrays, *lands)


def _exchange_start(arrays, scatter, after, name, lands=None):
    n = len(arrays)
    if lands is None:
        lands = list(_place_own(arrays, scatter, name=name.replace("_start_", "_own_")))

    def body(*refs):
        ins, lnd = refs[:n], refs[n:2 * n]
        send_sem, recv_sem = refs[2 * n + 1:2 * n + 3]
        token = refs[-1]
        for cp in _split_copies(ins, lnd, scatter, send_sem, recv_sem, arriving=False):
            cp.start()
        token[...] = jnp.zeros_like(token)

    hbm = lambda a: pltpu.HBM(a.shape, a.dtype)
    out = pl.pallas_call(
        body, name=name,
        out_shape=(pltpu.SemaphoreType.DMA(()), pltpu.SemaphoreType.DMA(()),
                   *[hbm(a) for a in arrays], *[hbm(a) for a in lands],
                   jax.ShapeDtypeStruct((8, LANES), F32)),
        in_specs=[_HBM] * (2 * n) + [pl.BlockSpec(memory_space=pl.ANY)],
        out_specs=(_SEM, _SEM, *([_HBM] * (2 * n)), pl.BlockSpec(memory_space=pltpu.VMEM)),
        input_output_aliases={i: 2 + i for i in range(2 * n)},
        compiler_params=pltpu.CompilerParams(has_side_effects=_EFFECT),
    )(*[pltpu.with_memory_space_constraint(a, pltpu.HBM) for a in list(arrays) + lands], after)
    return (out[0], out[1], list(out[2:2 + n]), list(out[2 + n:2 + 2 * n]), tuple(scatter)), out[-1]


def _exchange_wait(handle, after, name):
    send_sem, recv_sem, ins, lands, scatter = handle
    n = len(ins)
    after = after if isinstance(after, tuple) else (after,)

    def body(*refs):
        i_ref, l_ref = refs[:n], refs[n:2 * n]
        s_sem, r_sem = refs[2 * n:2 * n + 2]
        for cp in _split_copies(i_ref, l_ref, scatter, s_sem, r_sem, arriving=False):
            cp.wait_send()
        for cp in _split_copies(i_ref, l_ref, scatter, s_sem, r_sem, arriving=True):
            cp.wait_recv()

    hbm = lambda a: pltpu.HBM(a.shape, a.dtype)
    out = pl.pallas_call(
        body, name=name,
        out_shape=tuple(hbm(a) for a in ins + lands),
        in_specs=[_HBM] * (2 * n) + [_SEM, _SEM] + [pl.BlockSpec(memory_space=pl.ANY)] * len(after),
        out_specs=tuple([_HBM] * (2 * n)),
        input_output_aliases={i: i for i in range(2 * n)},
        compiler_params=pltpu.CompilerParams(has_side_effects=_EFFECT),
    )(*ins, *lands, send_sem, recv_sem, *after)
    return list(out[n:])


def _adamw(w, parts, m, v, name="adamw"):
    R, C = w.shape
    L = len(parts)
    rl = R // L
    tr = max([t for t in range(16, 513, 16) if rl % t == 0], default=rl)
    nb = rl // tr
    c1 = 1.0 - ADAM_B1 ** ADAM_STEP
    c2 = 1.0 - ADAM_B2 ** ADAM_STEP

    def body(*refs):
        w_ref, p_refs, (m_ref, v_ref, g_ref, d_ref, nm_ref, nv_ref) = refs[0], refs[1:1 + L], refs[1 + L:]
        g = None
        for j, p_ref in enumerate(p_refs):
            gj = p_ref[0].astype(F32)
            for i in range(1, N_DEV):
                gj = gj + p_ref[i].astype(F32)
            g = gj if g is None else jnp.where(pl.program_id(0) == j, gj, g)
        nm = ADAM_B1 * m_ref[...] + (1.0 - ADAM_B1) * g
        nv = ADAM_B2 * v_ref[...] + (1.0 - ADAM_B2) * (g * g)
        g_ref[...] = g
        nm_ref[...] = nm
        nv_ref[...] = nv
        d_ref[...] = -ADAM_LR * ((nm / c1) / (jnp.sqrt(nv / c2) + ADAM_EPS) + ADAM_WD * w_ref[...])

    blk = pl.BlockSpec((tr, C), lambda l, i: (l * nb + i, 0))
    part = lambda j: pl.BlockSpec((N_DEV, tr, C), lambda l, i: (0, jnp.where(l == j, i, 0), 0))
    return pl.pallas_call(
        body, name=name, grid=(L, nb),
        in_specs=[blk] + [part(j) for j in range(L)] + [blk, blk],
        out_specs=[blk] * 4,
        out_shape=[jax.ShapeDtypeStruct((R, C), F32)] * 4,
        compiler_params=_params(("arbitrary", "arbitrary")),
    )(w, *parts, m, v)


_O1 = Q_LORA
_O2 = _O1 + KV_LORA
_O3 = _O2 + MLA_ROPE
_NB = SB_HEADS * SB_DIM
IN_W = _O2 + LANES + 3 * _NB
COL_KR = _O2 // LANES
COL_SB = COL_KR + 1


def _w_in_local(w):
    kr = w[_O2:_O3]
    pad = jnp.zeros((LANES - 2 * MLA_ROPE, w.shape[1]), w.dtype)
    return jnp.concatenate([w[:_O2], kr, kr, pad, w[_O3:]], axis=0)


def _w_in_grad(g):
    kr = (g[_O2:_O2 + MLA_ROPE].astype(F32) + g[_O2 + MLA_ROPE:_O2 + 2 * MLA_ROPE].astype(F32)).astype(g.dtype)
    return jnp.concatenate([g[:_O2], kr, g[_O2 + LANES:]], axis=0)


def _w_uq_local(w):
    w3 = w.reshape(MLA_HEADS // 2, 2, MLA_NOPE + MLA_ROPE, w.shape[1])
    nope = w3[:, :, :MLA_NOPE].reshape(MLA_HEADS // 2, 2 * MLA_NOPE, w.shape[1])
    rope = w3[:, :, MLA_NOPE:].reshape(MLA_HEADS // 2, 2 * MLA_ROPE, w.shape[1])
    pad = jnp.zeros((MLA_HEADS // 2, LANES - 2 * MLA_ROPE, w.shape[1]), w.dtype)
    return jnp.concatenate([nope, rope, pad], axis=1).reshape(-1, w.shape[1])


def _w_uq_grad(g):
    g3 = g.reshape(MLA_HEADS // 2, 2 * LANES, g.shape[1])
    nope = g3[:, :2 * MLA_NOPE].reshape(MLA_HEADS // 2, 2, MLA_NOPE, g.shape[1])
    rope = g3[:, LANES:LANES + 2 * MLA_ROPE].reshape(MLA_HEADS // 2, 2, MLA_ROPE, g.shape[1])
    return jnp.concatenate([nope, rope], axis=2).reshape(-1, g.shape[1])


def _w_ukv_local(w):
    w3 = w.reshape(MLA_HEADS, MLA_NOPE + MLA_V, w.shape[1])
    return jnp.concatenate([w3[:, :MLA_NOPE].reshape(-1, w.shape[1]),
                            w3[:, MLA_NOPE:].reshape(-1, w.shape[1])], axis=0)


def _w_ukv_grad(g):
    half = MLA_HEADS * MLA_NOPE
    kn = g[:half].reshape(MLA_HEADS, MLA_NOPE, g.shape[1])
    vv = g[half:].reshape(MLA_HEADS, MLA_V, g.shape[1])
    return jnp.concatenate([kn, vv], axis=1).reshape(-1, g.shape[1])


def _rope_tables(T):
    pos = jnp.arange(T, dtype=F32)
    inv_freq = ROPE_THETA ** (-jnp.arange(0, MLA_ROPE, 2, dtype=F32) / MLA_ROPE)
    ang = pos[:, None] * inv_freq[None, :]
    cos, sin = jnp.cos(ang), jnp.sin(ang)
    ones = jnp.ones((T, LANES - 2 * MLA_ROPE), F32)
    cos_k = jnp.concatenate([cos, cos, cos, cos, ones], axis=1)
    sin_k = jnp.concatenate([-sin, sin, -sin, sin, 0.0 * ones], axis=1)
    cos_q = jnp.concatenate([jnp.ones((T, LANES), F32), cos_k], axis=1)
    sin_q = jnp.concatenate([jnp.zeros((T, LANES), F32), sin_k], axis=1)
    return cos_q, sin_q, cos_k, sin_k


def _bias_diag_index():
    ell = np.arange(TOEP_W)
    return np.clip(BAND_W - ell, -REL_CLIP, REL_CLIP) + REL_CLIP


def _local_step(x, target, small, get_weights, put_grads):
    T = x.shape[0]
    cos_q, sin_q, cos_k, sin_k = _rope_tables(T)
    G = {}
    W = dict(small)

    u0 = _rms_fwd(x, W["g_mix"][0:1], name="rms_mix0")
    bias_w = _toeplitz(W["od_rel_bias"][:, _bias_diag_index()])
    W.update(get_weights("in0", (u0, bias_w)))
    proj = _mm(u0, W["w_in_t"], dims="nt", name="proj_in")
    W.update(get_weights("mix0", proj))
    c_q, c_kv = proj[:, :_O1], proj[:, _O1:_O2]
    nq = _rms_fwd(c_q, W["g_cq"], name="rms_cq")
    nkv = _rms_fwd(c_kv, W["g_ckv"], name="rms_ckv")
    qa_raw = _mm(nq, W["w_uq_t"], dims="nt", name="proj_uq")
    kv = _mm(nkv, W["w_ukv_t"], dims="nt", out_dtype=BF16, name="proj_ukv")
    kr = _rope(proj, cos_k, sin_k, COL_KR, 1, BF16, name="rope_k")
    o_a, lse = _mla_fwd(qa_raw, cos_q, sin_q, kv, kr)
    o_b, o_b32, w_b = _sb_fwd(proj, COL_SB)
    o_ab = jnp.concatenate([o_a, o_b], axis=1)
    h1 = _mm(o_ab, W["ev_w_out"], res=x, name="out_ev")

    def ffn_fwd(h, layer):
        W.update(get_weights(f"ffn{layer}", h))
        return _ffn_fwd(h, W["g_ffn"][layer:layer + 1], W[f"w_gate_t{layer}"], W[f"w_up_t{layer}"],
                        W[f"w_down{layer}"], name=f"ffn_fwd{layer}")

    h2, u1, a0, b0 = ffn_fwd(h1, 0)

    W.update(get_weights("mix1", h2))
    u2 = _rms_fwd(h2, W["g_mix"][1:2], name="rms_mix1")
    qkv = _mm(u2, W["od_w_qkv_t"], dims="nt", out_dtype=BF16, name="proj_qkv")
    nc = C_HEADS * C_DIM
    pad = ((PAD_KEYS, 0), (0, 0))
    k_pad, v_pad = jnp.pad(qkv[:, nc:2 * nc], pad), jnp.pad(qkv[:, 2 * nc:], pad)
    o_c, p_c = _band_fwd(qkv, k_pad, v_pad, bias_w)
    h3 = _mm(o_c, W["od_w_out"], res=h2, name="out_od")
    h4, u3, a1, b1 = ffn_fwd(h3, 1)

    loss, dh, dhb, G["g_final"] = _loss_head(h4, W["g_final"], target)

    def ffn_bwd(dh, dhb, h, u, a, b, layer):
        du, g_gate, g_up, g_down = _ffn_bwd(dhb, u, a, b, W[f"w_gate_t{layer}"], W[f"w_up_t{layer}"],
                                            W[f"w_down{layer}"], name=f"ffn_bwd{layer}")
        tok = put_grads(f"ffn{layer}", {"w_gate_t": g_gate, "w_up_t": g_up, "w_down": g_down})
        return _rms_bwd(h, W["g_ffn"][layer:layer + 1] + tok[:1, :1], du, dres=dh, name=f"rms_ffn_bwd{layer}")

    dh3, dh3b, g_gffn1 = ffn_bwd(dh, dhb, h3, u3, a1, b1, 1)

    do_c = _mm(dh3b, W["od_w_out"], dims="nt", name="out_od_dx")
    g_od_out = _mm(o_c, dh3b, dims="tn", out_dtype=BF16, name="out_od_dw")
    dq_c, dk_p, dv_p, dbias_w = _band_bwd(qkv, k_pad, v_pad, p_c, do_c)
    dqkv = jnp.concatenate([dq_c, dk_p[PAD_KEYS:], dv_p[PAD_KEYS:]], axis=1)
    du2 = _mm(dqkv, W["od_w_qkv_t"], name="proj_qkv_dx")
    tok = put_grads("mix1", {"od_w_qkv_t": _mm(dqkv, u2, dims="tn", out_dtype=BF16, name="proj_qkv_dw"),
                             "od_w_out": g_od_out})
    ddiag = _toeplitz_bwd(dbias_w)
    n_far = BAND_W - REL_CLIP + 1
    G["od_rel_bias"] = jnp.concatenate(
        [jnp.zeros((C_HEADS, REL_CLIP - BAND_TQ + 1), F32), ddiag[:, n_far:][:, ::-1],
         jnp.sum(ddiag[:, :n_far], axis=1, keepdims=True)], axis=1)
    dh2, dh2b, g_gmix1 = _rms_bwd(h2, W["g_mix"][1:2] + tok[:1, :1], du2, dres=dh3, name="rms_mix_bwd1")

    dh1, dh1b, g_gffn0 = ffn_bwd(dh2, dh2b, h1, u1, a0, b0, 0)
    G["g_ffn"] = jnp.concatenate([g_gffn0, g_gffn1], axis=0)

    do_ab = _mm(dh1b, W["ev_w_out"], dims="nt", name="out_ev_dx")
    g0 = {"ev_w_out": _mm(o_ab, dh1b, dims="tn", out_dtype=BF16, name="out_ev_dw")}
    dqa_raw, dkn, dva, dkr = _mla_bwd(qa_raw, cos_q, sin_q, kv, kr, o_a, lse, do_ab, 0)
    g0["w_uq_t"] = _mm(dqa_raw, nq, dims="tn", name="proj_uq_dw")
    dnq = _mm(dqa_raw, W["w_uq_t"], name="proj_uq_dx")
    _, dc_q, G["g_cq"] = _rms_bwd(c_q, W["g_cq"], dnq, name="rms_cq_bwd")
    dkv = jnp.concatenate([dkn, dva], axis=1)
    g0["w_ukv_t"] = _mm(dkv, nkv, dims="tn", name="proj_ukv_dw")
    dnkv = _mm(dkv, W["w_ukv_t"], name="proj_ukv_dx")
    _, dc_kv, G["g_ckv"] = _rms_bwd(c_kv, W["g_ckv"], dnkv, name="rms_ckv_bwd")
    tok = put_grads("mix0", g0)
    dqb, dkb, dvb = _sb_bwd(proj, COL_SB, o_b32, w_b, do_ab, MLA_HEADS // 2, tok)
    dkr_raw = _rope(dkr, cos_k, -sin_k, 0, 1, BF16, name="rope_k_bwd")
    dproj = jnp.concatenate([dc_q, dc_kv, dkr_raw, dqb, dkb, dvb], axis=1)
    du0 = _mm(dproj, W["w_in_t"], name="proj_in_dx")
    tok = put_grads("in0", {"w_in_t": _mm(dproj, u0, dims="tn", name="proj_in_dw")})
    dx, _, g_gmix0 = _rms_bwd(x, W["g_mix"][0:1] + tok[:1, :1], du0, dres=dh1, name="rms_mix_bwd0")
    G["g_mix"] = jnp.concatenate([g_gmix0, g_gmix1], axis=0)
    return loss[0, 0], dx, G


_BIG = ["ev_w_in", "ev_w_uq", "ev_w_ukv", "ev_w_out", "od_w_qkv", "od_w_out", "w_gate", "w_up", "w_down"]
_COL_SHARDED = {"ev_w_in", "ev_w_uq", "ev_w_ukv", "od_w_qkv", "w_gate", "w_up"}
_SMALL = ["ev_g_cq", "ev_g_ckv", "od_rel_bias", "g_mix", "g_ffn", "g_final"]
_GROUPS = {
    "in0": ["ev_w_in"],
    "mix0": ["ev_w_uq", "ev_w_ukv", "ev_w_out"],
    "ffn0": ["w_gate0", "w_up0", "w_down0"],
    "mix1": ["od_w_qkv", "od_w_out"],
    "ffn1": ["w_gate1", "w_up1", "w_down1"],
}
_GROUP_SRC = {n + str(l): (n, l) for n in ("w_gate", "w_up", "w_down") for l in (0, 1)}
_BATCHES = {"in0": ["in0"], "layer0": ["mix0", "ffn0"], "layer1": ["mix1", "ffn1"]}
_BATCH_OF = {grp: batch for batch, grps in _BATCHES.items() for grp in grps}
_SMALL_ROWS = 8
_SMALL_COLS = 1792


def _pack_small(vals):
    flat = jnp.concatenate([v.reshape(-1).astype(F32) for v in vals])
    flat = jnp.pad(flat, (0, _SMALL_ROWS * _SMALL_COLS - flat.shape[0]))
    return flat.reshape(_SMALL_ROWS, _SMALL_COLS)


def _unpack_small(packed, like):
    flat = packed.reshape(-1)
    out, off = [], 0
    for v in like:
        out.append(flat[off:off + v.size].reshape(v.shape))
        off += v.size
    return out


def kernel(x, ev_w_in, ev_g_cq, ev_w_uq, ev_g_ckv, ev_w_ukv, ev_w_out, od_w_qkv, od_rel_bias, od_w_out, g_mix, g_ffn, w_gate, w_up, w_down, g_final, loss_target, m_ev_w_in, m_ev_g_cq, m_ev_w_uq, m_ev_g_ckv, m_ev_w_ukv, m_ev_w_out, m_od_w_qkv, m_od_rel_bias, m_od_w_out, m_g_mix, m_g_ffn, m_w_gate, m_w_up, m_w_down, m_g_final, v_ev_w_in, v_ev_g_cq, v_ev_w_uq, v_ev_g_ckv, v_ev_w_ukv, v_ev_w_out, v_od_w_qkv, v_od_rel_bias, v_od_w_out, v_g_mix, v_g_ffn, v_w_gate, v_w_up, v_w_down, v_g_final):
    args = dict(locals())
    w = {n: args[n] for n in _BIG + _SMALL}
    mom = {n: args["m_" + n] for n in _BIG + _SMALL}
    var = {n: args["v_" + n] for n in _BIG + _SMALL}

    own = {}
    for grp, names in _GROUPS.items():
        for n in names:
            base, layer = _GROUP_SRC.get(n, (n, 0))
            shard = w[base][layer:layer + 1]
            own[n] = (jnp.swapaxes(shard, 1, 2) if base in _COL_SHARDED else shard).astype(BF16)
    placed = dict(zip(own, _place_own(list(own.values()), [False] * len(own), name="gather_own")))
    gather, token = {}, x[0, :8, :LANES]
    for grp, names in _GROUPS.items():
        gather[grp], token = _exchange_start([own[n] for n in names], [False] * len(names), token,
                                             name="gather_start_" + grp, lands=[placed[n] for n in names])

    def get_weights(grp, after):
        names = _GROUPS[grp]
        lands = _exchange_wait(gather[grp], token if after is None else after, name="gather_wait_" + grp)
        full = {n: l.reshape(-1, l.shape[-1]) for n, l in zip(names, lands)}
        if grp == "in0":
            return {"w_in_t": _w_in_local(full["ev_w_in"])}
        if grp == "mix0":
            return {"w_uq_t": _w_uq_local(full["ev_w_uq"]), "w_ukv_t": _w_ukv_local(full["ev_w_ukv"]),
                    "ev_w_out": full["ev_w_out"]}
        if grp == "mix1":
            return {"od_w_qkv_t": full["od_w_qkv"], "od_w_out": full["od_w_out"]}
        layer = grp[-1]
        return {"w_gate_t" + layer: full["w_gate" + layer], "w_up_t" + layer: full["w_up" + layer],
                "w_down" + layer: full["w_down" + layer]}

    scatter, pending = {}, {}

    def put_grads(grp, g):
        if grp == "in0":
            g = {"ev_w_in": _w_in_grad(g["w_in_t"])}
        elif grp == "mix0":
            g = {"ev_w_uq": _w_uq_grad(g["w_uq_t"]), "ev_w_ukv": _w_ukv_grad(g["w_ukv_t"]),
                 "ev_w_out": g["ev_w_out"]}
        elif grp == "mix1":
            g = {"od_w_qkv": g["od_w_qkv_t"], "od_w_out": g["od_w_out"]}
        else:
            layer = grp[-1]
            g = {"w_gate" + layer: g["w_gate_t"], "w_up" + layer: g["w_up_t"], "w_down" + layer: g["w_down"]}
        pending.update({n: v.reshape(N_DEV, 1, v.shape[0] // N_DEV, v.shape[1]).astype(BF16) for n, v in g.items()})
        batch = _BATCH_OF[grp]
        names = [n for gr in _BATCHES[batch] for n in _GROUPS[gr]]
        if not all(n in pending for n in names):
            return jnp.zeros((8, LANES), F32)
        send = [pending[n] for n in names]
        scatter[batch], tok = _exchange_start(send, [True] * len(names), send[0], name="scatter_start_" + batch)
        return tok

    small = {"g_cq": ev_g_cq, "g_ckv": ev_g_ckv, "od_rel_bias": od_rel_bias[0],
             "g_mix": g_mix + token[0, 0], "g_ffn": g_ffn, "g_final": g_final.reshape(1, -1)}
    loss_part, dx, G = _local_step(x[0], loss_target[0], small, get_weights, put_grads)
    g_small = _pack_small([G["g_cq"], G["g_ckv"], G["od_rel_bias"], G["g_mix"], G["g_ffn"], G["g_final"],
                           loss_part.reshape(1)])
    small_handle, _ = _exchange_start([g_small], [False], dx, name="gather_start_small")

    grads, deltas, new_m, new_v = {}, {}, {}, {}
    parts, after = {}, dx

    def wait_parts(batch, after):
        lands = _exchange_wait(scatter[batch], after, name="scatter_wait_" + batch)
        parts.update(zip([n for grp in _BATCHES[batch] for n in _GROUPS[grp]], lands))
        return lands[0]

    def adamw(n):
        col = n in _COL_SHARDED
        rows = lambda a: (jnp.swapaxes(a, 1, 2) if col else a).reshape(-1, a.shape[1 if col else 2])
        layers = [parts[n]] if n in parts else [parts[n + "0"], parts[n + "1"]]
        res = _adamw(rows(w[n]), [p.reshape(N_DEV, -1, p.shape[-1]) for p in layers], rows(mom[n]), rows(var[n]),
                     name="adamw_" + n)
        L, a1, a2 = w[n].shape
        back = lambda r: jnp.swapaxes(r.reshape(L, a2, a1), 1, 2) if col else r.reshape(L, a1, a2)
        grads[n], deltas[n], new_m[n], new_v[n] = [back(r) for r in res]
        return res[0]

    for batch in ("layer1", "layer0"):
        after = wait_parts(batch, after)
    for n in _BIG[1:]:
        after = adamw(n)
    after = wait_parts("in0", after)
    after = adamw("ev_w_in")
    small_w = [w[n] for n in _SMALL]
    small_parts = _exchange_wait(small_handle, after, name="gather_wait_small")[0]
    loss = jnp.sum(small_parts.reshape(N_DEV, -1)[:, sum(v.size for v in small_w)])
    res = _adamw(_pack_small(small_w), [small_parts], _pack_small([mom[n] for n in _SMALL]),
                 _pack_small([var[n] for n in _SMALL]), name="adamw_small")
    for d, packed in zip((grads, deltas, new_m, new_v), res):
        for n, val in zip(_SMALL, _unpack_small(packed, small_w)):
            d[n] = val

    order = ["ev_w_in", "ev_g_cq", "ev_w_uq", "ev_g_ckv", "ev_w_ukv", "ev_w_out", "od_w_qkv", "od_rel_bias",
             "od_w_out", "g_mix", "g_ffn", "w_gate", "w_up", "w_down", "g_final"]
    out = [loss, dx[None]]
    for d in (grads, deltas, new_m, new_v):
        out += [d[n] for n in order]
    return tuple(out)
```

```python
import functools

import numpy as np
import jax
import jax.numpy as jnp
from jax import lax
from jax.experimental import pallas as pl
from jax.experimental.pallas import tpu as pltpu

F32 = jnp.float32
BF16 = jnp.bfloat16

D_MODEL = 1024
CHUNK = 64
MLA_HEADS = 8
MLA_NOPE = 64
MLA_ROPE = 32
MLA_V = 64
Q_LORA = 384
KV_LORA = 256
ROPE_THETA = 10000.0
SB_HEADS = 8
SB_DIM = 64
C_HEADS = 16
C_DIM = 64
LEFT_CHUNKS = 8
REL_CLIP = 256
D_FF = 2816
RMS_EPS = 1e-6
ADAM_LR = 0.001
ADAM_B1 = 0.9
ADAM_B2 = 0.999
ADAM_EPS = 1e-08
ADAM_WD = 0.01
ADAM_STEP = 10

N_DEV = 8
LANES = 128
HEAD = 64
assert HEAD == MLA_NOPE == MLA_V == SB_DIM == C_DIM and 2 * HEAD == LANES
CHUNK_BITS = CHUNK.bit_length() - 1
assert 1 << CHUNK_BITS == CHUNK
VMEM_LIMIT = 56 * 1024 * 1024
NEG = -1e30
PAD_KEYS = LEFT_CHUNKS * CHUNK
BAND_TQ = 128
BAND_W = BAND_TQ + PAD_KEYS
TOEP_W = BAND_W + BAND_TQ

NN = (((1,), (0,)), ((), ()))
NT = (((1,), (1,)), ((), ()))
TN = (((0,), (0,)), ((), ()))


def _dot(a, b, dn):
    return lax.dot_general(a, b, dn, preferred_element_type=F32)


def _pick(dim, pref):
    if dim <= pref:
        return dim
    best = None
    for t in range(LANES, pref + 1, LANES):
        if dim % t == 0:
            best = t
    assert best is not None, (dim, pref)
    return best


def _params(sem):
    return pltpu.CompilerParams(dimension_semantics=sem, vmem_limit_bytes=VMEM_LIMIT)


def _mm(a, b, dims="nn", res=None, out_dtype=F32, name="mm"):
    if dims == "nn":
        (M, K), (K2, N) = a.shape, b.shape
    elif dims == "nt":
        (M, K), (N, K2) = a.shape, b.shape
    else:
        (K, M), (K2, N) = a.shape, b.shape
    assert K == K2, (a.shape, b.shape, dims)
    tm, tn, tk = _pick(M, 1024), _pick(N, 1152), _pick(K, 1024)
    nk = K // tk
    dn = {"nn": NN, "nt": NT, "tn": TN}[dims]
    has_res = res is not None

    def body(*refs):
        if has_res:
            a_ref, b_ref, r_ref, o_ref, acc = refs
        else:
            a_ref, b_ref, o_ref, acc = refs
        k = pl.program_id(2)

        @pl.when(k == 0)
        def _():
            acc[...] = jnp.zeros_like(acc)

        acc[...] += _dot(a_ref[...].astype(BF16), b_ref[...].astype(BF16), dn)

        @pl.when(k == nk - 1)
        def _():
            r = acc[...]
            if has_res:
                r = r + r_ref[...]
            o_ref[...] = r.astype(out_dtype)

    a_spec = (pl.BlockSpec((tk, tm), lambda i, j, k: (k, i)) if dims == "tn"
              else pl.BlockSpec((tm, tk), lambda i, j, k: (i, k)))
    b_spec = (pl.BlockSpec((tn, tk), lambda i, j, k: (j, k)) if dims == "nt"
              else pl.BlockSpec((tk, tn), lambda i, j, k: (k, j)))
    o_spec = pl.BlockSpec((tm, tn), lambda i, j, k: (i, j))
    in_specs = [a_spec, b_spec] + ([o_spec] if has_res else [])
    args = (a, b) + ((res,) if has_res else ())
    return pl.pallas_call(
        body, name=name, grid=(M // tm, N // tn, nk),
        in_specs=in_specs, out_specs=o_spec,
        out_shape=jax.ShapeDtypeStruct((M, N), out_dtype),
        scratch_shapes=[pltpu.VMEM((tm, tn), F32)],
        compiler_params=_params(("parallel", "parallel", "arbitrary")),
    )(*args)


def _rms_fwd(x, g, out_dtype=BF16, name="rms_fwd"):
    T, Fd = x.shape
    tm = _pick(T, 256)

    def body(x_ref, g_ref, o_ref):
        xv = x_ref[...]
        r = lax.rsqrt(jnp.mean(xv * xv, axis=-1, keepdims=True) + RMS_EPS)
        o_ref[...] = (xv * r * g_ref[...]).astype(out_dtype)

    return pl.pallas_call(
        body, name=name, grid=(T // tm,),
        in_specs=[pl.BlockSpec((tm, Fd), lambda i: (i, 0)), pl.BlockSpec((1, Fd), lambda i: (0, 0))],
        out_specs=pl.BlockSpec((tm, Fd), lambda i: (i, 0)),
        out_shape=jax.ShapeDtypeStruct((T, Fd), out_dtype),
        compiler_params=_params(("parallel",)),
    )(x, g)


def _rms_bwd(x, g, dy, dres=None, name="rms_bwd"):
    T, Fd = x.shape
    tm = _pick(T, 256)
    has_res = dres is not None

    def body(*refs):
        if has_res:
            x_ref, g_ref, dy_ref, r_ref, dx_ref, dxb_ref, dg_ref = refs
        else:
            x_ref, g_ref, dy_ref, dx_ref, dxb_ref, dg_ref = refs
        xv, dyv = x_ref[...], dy_ref[...]
        r = lax.rsqrt(jnp.mean(xv * xv, axis=-1, keepdims=True) + RMS_EPS)
        gdy = dyv * g_ref[...]
        dot = jnp.mean(xv * gdy, axis=-1, keepdims=True)
        dx = r * gdy - xv * (r * r * r * dot)
        if has_res:
            dx = dx + r_ref[...]
        dx_ref[...] = dx
        dxb_ref[...] = dx.astype(BF16)

        @pl.when(pl.program_id(0) == 0)
        def _():
            dg_ref[...] = jnp.zeros_like(dg_ref)

        dg_ref[...] += jnp.sum(dyv * xv * r, axis=0, keepdims=True)

    row = pl.BlockSpec((tm, Fd), lambda i: (i, 0))
    vec = pl.BlockSpec((1, Fd), lambda i: (0, 0))
    in_specs = [row, vec, row] + ([row] if has_res else [])
    args = (x, g, dy) + ((dres,) if has_res else ())
    return pl.pallas_call(
        body, name=name, grid=(T // tm,),
        in_specs=in_specs, out_specs=[row, row, vec],
        out_shape=[jax.ShapeDtypeStruct((T, Fd), F32), jax.ShapeDtypeStruct((T, Fd), BF16),
                   jax.ShapeDtypeStruct((1, Fd), F32)],
        compiler_params=_params(("arbitrary",)),
    )(*args)


def _mm_rms_bwd(a, b, x, g, dres, name="mm_rms_bwd"):
    T, K = a.shape
    Fd = b.shape[1]
    tm, tk = _pick(T, 512), _pick(K, 1024)
    nk = K // tk

    def body(a_ref, b_ref, x_ref, g_ref, r_ref, dx_ref, dxb_ref, dg_ref, acc):
        i, k = pl.program_id(0), pl.program_id(1)

        @pl.when(k == 0)
        def _():
            acc[...] = jnp.zeros_like(acc)

        @pl.when((k == 0) & (i == 0))
        def _():
            dg_ref[...] = jnp.zeros_like(dg_ref)

        acc[...] += _dot(a_ref[...].astype(BF16), b_ref[...].astype(BF16), NN)

        @pl.when(k == nk - 1)
        def _():
            xv, dyv = x_ref[...], acc[...]
            r = lax.rsqrt(jnp.mean(xv * xv, axis=-1, keepdims=True) + RMS_EPS)
            gdy = dyv * g_ref[...]
            dot = jnp.mean(xv * gdy, axis=-1, keepdims=True)
            dx = r * gdy - xv * (r * r * r * dot) + r_ref[...]
            dx_ref[...] = dx
            dxb_ref[...] = dx.astype(BF16)
            dg_ref[...] += jnp.sum(dyv * xv * r, axis=0, keepdims=True)

    row = pl.BlockSpec((tm, Fd), lambda i, k: (i, 0))
    vec = pl.BlockSpec((1, Fd), lambda i, k: (0, 0))
    return pl.pallas_call(
        body, name=name, grid=(T // tm, nk),
        in_specs=[pl.BlockSpec((tm, tk), lambda i, k: (i, k)), pl.BlockSpec((tk, Fd), lambda i, k: (k, 0)),
                  row, vec, row],
        out_specs=[row, row, vec],
        out_shape=[jax.ShapeDtypeStruct((T, Fd), F32), jax.ShapeDtypeStruct((T, Fd), BF16),
                   jax.ShapeDtypeStruct((1, Fd), F32)],
        scratch_shapes=[pltpu.VMEM((tm, Fd), F32)],
        compiler_params=_params(("arbitrary", "arbitrary")),
    )(a, b, x, g, dres)


def _loss_head(h, g, target, name="loss_head"):
    T, Fd = h.shape
    tm = _pick(T, 256)

    def body(h_ref, g_ref, t_ref, loss_ref, dh_ref, dhb_ref, dg_ref):
        xv = h_ref[...]
        r = lax.rsqrt(jnp.mean(xv * xv, axis=-1, keepdims=True) + RMS_EPS)
        diff = xv * r * g_ref[...] - t_ref[...]
        part = 0.5 * jnp.sum(jnp.mean(diff * diff, axis=-1, keepdims=True), axis=0, keepdims=True)
        dyv = diff * (1.0 / Fd)
        gdy = dyv * g_ref[...]
        dot = jnp.mean(xv * gdy, axis=-1, keepdims=True)
        dh = r * gdy - xv * (r * r * r * dot)
        dh_ref[...] = dh
        dhb_ref[...] = dh.astype(BF16)

        @pl.when(pl.program_id(0) == 0)
        def _():
            dg_ref[...] = jnp.zeros_like(dg_ref)
            loss_ref[...] = jnp.zeros_like(loss_ref)

        dg_ref[...] += jnp.sum(dyv * xv * r, axis=0, keepdims=True)
        loss_ref[...] += jnp.broadcast_to(part, loss_ref.shape)

    row = pl.BlockSpec((tm, Fd), lambda i: (i, 0))
    vec = pl.BlockSpec((1, Fd), lambda i: (0, 0))
    return pl.pallas_call(
        body, name=name, grid=(T // tm,),
        in_specs=[row, vec, row],
        out_specs=[pl.BlockSpec((1, LANES), lambda i: (0, 0)), row, row, vec],
        out_shape=[jax.ShapeDtypeStruct((1, LANES), F32), jax.ShapeDtypeStruct((T, Fd), F32),
                   jax.ShapeDtypeStruct((T, Fd), BF16), jax.ShapeDtypeStruct((1, Fd), F32)],
        compiler_params=_params(("arbitrary",)),
    )(h, g, target)


FFN_TF = 256


def _ffn_fwd(h, g, wg_t, wu_t, wd, name="ffn_fwd"):
    T, Dm = h.shape
    Fh = wd.shape[0]
    tm = _pick(T, 1024)
    nf = Fh // FFN_TF

    def body(h_ref, g_ref, wg_ref, wu_ref, wd_ref, o_ref, u_ref, a_ref, b_ref):
        j = pl.program_id(1)

        @pl.when(j == 0)
        def _():
            xv = h_ref[...]
            r = lax.rsqrt(jnp.mean(xv * xv, axis=-1, keepdims=True) + RMS_EPS)
            u_ref[...] = (xv * r * g_ref[...]).astype(BF16)
            o_ref[...] = xv

        u = u_ref[...]
        a = _dot(u, wg_ref[...], NT).astype(BF16)
        b = _dot(u, wu_ref[...], NT).astype(BF16)
        a_ref[...] = a
        b_ref[...] = b
        af = a.astype(F32)
        s = (af * jax.nn.sigmoid(af) * b.astype(F32)).astype(BF16)
        o_ref[...] += _dot(s, wd_ref[...], NN)

    row = pl.BlockSpec((tm, Dm), lambda i, j: (i, 0))
    wblk = pl.BlockSpec((FFN_TF, Dm), lambda i, j: (j, 0))
    ablk = pl.BlockSpec((tm, FFN_TF), lambda i, j: (i, j))
    return pl.pallas_call(
        body, name=name, grid=(T // tm, nf),
        in_specs=[row, pl.BlockSpec((1, Dm), lambda i, j: (0, 0)), wblk, wblk, wblk],
        out_specs=[row, row, ablk, ablk],
        out_shape=[jax.ShapeDtypeStruct((T, Dm), F32), jax.ShapeDtypeStruct((T, Dm), BF16),
                   jax.ShapeDtypeStruct((T, Fh), BF16), jax.ShapeDtypeStruct((T, Fh), BF16)],
        compiler_params=_params(("parallel", "arbitrary")),
    )(h, g, wg_t, wu_t, wd)


def _ffn_bwd(dh, u, a, b, wg_t, wu_t, wd, name="ffn_bwd"):
    T, Dm = dh.shape
    Fh = wd.shape[0]
    nf = Fh // FFN_TF
    once = pl.Buffered(1)

    def body(dh_ref, u_ref, a_ref, b_ref, wg_ref, wu_ref, wd_ref, du_ref, dwg_ref, dwu_ref, dwd_ref):
        j = pl.program_id(0)

        @pl.when(j == 0)
        def _():
            du_ref[...] = jnp.zeros_like(du_ref)

        ds = _dot(dh_ref[...], wd_ref[...], NT)
        af, bf = a_ref[...].astype(F32), b_ref[...].astype(F32)
        sig = jax.nn.sigmoid(af)
        sa = af * sig
        dwd_ref[...] = _dot((sa * bf).astype(BF16), dh_ref[...], TN).astype(BF16)
        dab = jnp.concatenate([(ds * bf * (sig * (1.0 + af * (1.0 - sig)))).astype(BF16),
                               (ds * sa).astype(BF16)], axis=1)
        dw = _dot(dab, u_ref[...], TN)
        dwg_ref[...] = dw[:FFN_TF].astype(BF16)
        dwu_ref[...] = dw[FFN_TF:].astype(BF16)
        du_ref[...] += _dot(dab, jnp.concatenate([wg_ref[...], wu_ref[...]], axis=0), NN)

    full = lambda: pl.BlockSpec((T, Dm), lambda j: (0, 0), pipeline_mode=once)
    wblk = pl.BlockSpec((FFN_TF, Dm), lambda j: (j, 0))
    ablk = pl.BlockSpec((T, FFN_TF), lambda j: (0, j))
    return pl.pallas_call(
        body, name=name, grid=(nf,),
        in_specs=[full(), full(), ablk, ablk, wblk, wblk, wblk],
        out_specs=[pl.BlockSpec((T, Dm), lambda j: (0, 0)), wblk, wblk, wblk],
        out_shape=[jax.ShapeDtypeStruct((T, Dm), F32)] + [jax.ShapeDtypeStruct((Fh, Dm), BF16)] * 3,
        compiler_params=_params(("arbitrary",)),
    )(dh, u, a, b, wg_t, wu_t, wd)


def _rope(x, cos_t, sin_t, col0, ncols, out_dtype, name="rope"):
    T = x.shape[0]
    wt = cos_t.shape[1]
    tm = _pick(T, 256)
    nb = ncols * LANES // wt
    half = MLA_ROPE // 2

    def body(x_ref, c_ref, s_ref, o_ref):
        xv = x_ref[...].astype(F32)
        lane = lax.broadcasted_iota(jnp.int32, xv.shape, 1)
        first = (lane & (MLA_ROPE - 1)) < half
        swapped = jnp.where(first, pltpu.roll(xv, wt - half, 1), pltpu.roll(xv, half, 1))
        o_ref[...] = (xv * c_ref[...] + swapped * s_ref[...]).astype(out_dtype)

    off = col0 * LANES // wt
    return pl.pallas_call(
        body, name=name, grid=(T // tm, nb),
        in_specs=[pl.BlockSpec((tm, wt), lambda i, j: (i, j + off)),
                  pl.BlockSpec((tm, wt), lambda i, j: (i, 0)),
                  pl.BlockSpec((tm, wt), lambda i, j: (i, 0))],
        out_specs=pl.BlockSpec((tm, wt), lambda i, j: (i, j)),
        out_shape=jax.ShapeDtypeStruct((T, ncols * LANES), out_dtype),
        compiler_params=_params(("parallel", "parallel")),
    )(x, cos_t, sin_t)


ATT_TQ = 512
ATT_TK = 256


def _mla_masks(shape):
    lane = lax.broadcasted_iota(jnp.int32, shape, 1)
    m0 = (lane < HEAD) | ((lane >= LANES) & (lane < LANES + MLA_ROPE))
    m1 = ((lane >= HEAD) & (lane < LANES)) | ((lane >= LANES + MLA_ROPE) & (lane < LANES + 2 * MLA_ROPE))
    return m0, m1


def _by_twos(n, step, carry):
    carry = lax.fori_loop(0, n // 2, lambda i, c: step(2 * i + 1, step(2 * i, c)), carry)
    return lax.fori_loop(0, n % 2, lambda _, c: step(n - 1, c), carry)


def _chunk_ok(tq, tk, d):
    row = lax.broadcasted_iota(jnp.int32, (tq, tk), 0)
    col = lax.broadcasted_iota(jnp.int32, (tq, tk), 1) + d * tk
    return jnp.concatenate([(col >> CHUNK_BITS) <= (row >> CHUNK_BITS)] * 2, axis=0)


def _rotate(x, cos_t, sin_t):
    half = MLA_ROPE // 2
    lane = lax.broadcasted_iota(jnp.int32, x.shape, 1)
    first = (lane & (MLA_ROPE - 1)) < half
    swapped = jnp.where(first, pltpu.roll(x, x.shape[1] - half, 1), pltpu.roll(x, half, 1))
    return x * cos_t + swapped * sin_t


def _mla_fwd(q, cos_q, sin_q, kv, kr, name="mla_fwd"):
    T = q.shape[0]
    tq, tk = _pick(T, ATT_TQ), _pick(T, ATT_TK)
    nd = tq // tk
    npair = MLA_HEADS // 2
    scale = (MLA_NOPE + MLA_ROPE) ** -0.5

    def body(q_ref, c_ref, s_ref, kn_ref, v_ref, kr_ref, o_ref, lse_ref):
        m_idx = pl.program_id(1)
        qv = _rotate(q_ref[...], c_ref[...], s_ref[...]).astype(BF16)
        m0, m1 = _mla_masks(qv.shape)
        qs = jnp.concatenate([jnp.where(m0, qv, 0), jnp.where(m1, qv, 0)], axis=0).astype(BF16)

        def block(kb, carry, ok):
            ks = pl.ds(pl.multiple_of(kb * tk, tk), tk)
            kcat = jnp.concatenate([kn_ref[ks, :], kr_ref[ks, :]], axis=1)
            mx, l, acc = carry
            s = _dot(qs, kcat, NT) * scale
            if ok is not None:
                s = jnp.where(ok, s, NEG)
            mn = jnp.maximum(mx, jnp.max(s, axis=-1, keepdims=True))
            alpha = jnp.exp(mx - mn)
            p = jnp.exp(s - mn)
            return (mn, alpha * l + jnp.sum(p, axis=-1, keepdims=True),
                    alpha * acc + _dot(p.astype(BF16), v_ref[ks, :], NN))

        init = (jnp.full((2 * tq, 1), NEG, F32), jnp.zeros((2 * tq, 1), F32), jnp.zeros((2 * tq, LANES), F32))
        res = init
        for d in range(nd):
            res = block(m_idx * nd + d, res, _chunk_ok(tq, tk, d))
        mx, l, acc = _by_twos(m_idx * nd, lambda kb, c: block(kb, c, None), res)
        h0 = lax.broadcasted_iota(jnp.int32, (tq, LANES), 1) < HEAD
        o_ref[...] = _two_heads(acc * (1.0 / l), h0).astype(o_ref.dtype)
        lse_ref[...] = _two_heads(jnp.broadcast_to(mx + jnp.log(l), (2 * tq, LANES)), h0)

    full = lambda col: pl.BlockSpec((T, LANES), col)
    table = pl.BlockSpec((tq, 2 * LANES), lambda p, m: (m, 0))
    return pl.pallas_call(
        body, name=name, grid=(npair, T // tq),
        in_specs=[pl.BlockSpec((tq, 2 * LANES), lambda p, m: (m, p)), table, table,
                  full(lambda p, m: (0, p)), full(lambda p, m: (0, npair + p)), full(lambda p, m: (0, 0))],
        out_specs=[pl.BlockSpec((tq, LANES), lambda p, m: (m, p)),
                   pl.BlockSpec((tq, LANES), lambda p, m: (m, p))],
        out_shape=[jax.ShapeDtypeStruct((T, npair * LANES), BF16),
                   jax.ShapeDtypeStruct((T, npair * LANES), F32)],
        compiler_params=_params(("parallel", "arbitrary")),
    )(q, cos_q, sin_q, kv, kv, kr)


def _mla_bwd(q, cos_q, sin_q, kv, kr, o, lse, do, do_col0, name="mla_bwd"):
    T = q.shape[0]
    tq, tk = _pick(T, ATT_TQ), _pick(T, ATT_TK)
    nd = tq // tk
    npair = MLA_HEADS // 2
    scale = (MLA_NOPE + MLA_ROPE) ** -0.5

    def body(q_ref, c_ref, s_ref, kn_ref, v_ref, kr_ref, o_ref, lse_ref, do_ref, dq_ref, dkn_ref, dv_ref, dkr_ref,
             dkn_acc, dv_acc):
        p_idx, m_idx = pl.program_id(0), pl.program_id(1)

        @pl.when(m_idx == 0)
        def _():
            dkn_acc[...] = jnp.zeros_like(dkn_acc)
            dv_acc[...] = jnp.zeros_like(dv_acc)

        @pl.when((m_idx == 0) & (p_idx == 0))
        def _():
            dkr_ref[...] = jnp.zeros_like(dkr_ref)

        qv = _rotate(q_ref[...], c_ref[...], s_ref[...]).astype(BF16)
        m0, m1 = _mla_masks(qv.shape)
        qs = jnp.concatenate([jnp.where(m0, qv, 0), jnp.where(m1, qv, 0)], axis=0).astype(BF16)
        dov = do_ref[...].astype(F32)
        h0 = lax.broadcasted_iota(jnp.int32, (tq, LANES), 1) < HEAD
        dos32 = jnp.concatenate([jnp.where(h0, dov, 0.0), jnp.where(h0, 0.0, dov)], axis=0)
        ov = o_ref[...].astype(F32)
        delta = jnp.sum(dos32 * jnp.concatenate([ov, ov], axis=0), axis=-1, keepdims=True)
        dos = dos32.astype(BF16)
        lsev = lse_ref[...]
        lse = jnp.concatenate([lsev[:, 0:1], lsev[:, HEAD:HEAD + 1]], axis=0)

        def block(kb, dq, ok):
            ks = pl.ds(pl.multiple_of(kb * tk, tk), tk)
            kcat = jnp.concatenate([kn_ref[ks, :], kr_ref[ks, :]], axis=1)
            vv = v_ref[ks, :]
            p = jnp.exp(_dot(qs, kcat, NT) * scale - lse)
            if ok is not None:
                p = jnp.where(ok, p, 0.0)
            ds = (p * (_dot(dos, vv, NT) - delta) * scale).astype(BF16)
            dkc = _dot(ds, qs, TN)
            dkn_acc[ks, :] += dkc[:, :LANES]
            dkr_ref[ks, :] += dkc[:, LANES:]
            dv_acc[ks, :] += _dot(p.astype(BF16), dos, TN)
            return dq + _dot(ds, kcat, NN)

        dq = jnp.zeros((2 * tq, 2 * LANES), F32)
        for d in range(nd):
            dq = block(m_idx * nd + d, dq, _chunk_ok(tq, tk, d))
        dq = _by_twos(m_idx * nd, lambda kb, c: block(kb, c, None), dq)
        dq_ref[...] = _rotate(jnp.where(m0, dq[:tq], jnp.where(m1, dq[tq:], 0.0)), c_ref[...],
                              -s_ref[...]).astype(BF16)

        @pl.when(m_idx == T // tq - 1)
        def _():
            dkn_ref[...] = dkn_acc[...].astype(BF16)
            dv_ref[...] = dv_acc[...].astype(BF16)

    full = lambda col: pl.BlockSpec((T, LANES), col)
    blk = lambda col: pl.BlockSpec((tq, LANES), col)
    table = pl.BlockSpec((tq, 2 * LANES), lambda p, m: (m, 0))
    return pl.pallas_call(
        body, name=name, grid=(npair, T // tq),
        in_specs=[pl.BlockSpec((tq, 2 * LANES), lambda p, m: (m, p)), table, table,
                  full(lambda p, m: (0, p)), full(lambda p, m: (0, npair + p)), full(lambda p, m: (0, 0)),
                  blk(lambda p, m: (m, p)), blk(lambda p, m: (m, p)),
                  blk(lambda p, m: (m, do_col0 + p))],
        out_specs=[pl.BlockSpec((tq, 2 * LANES), lambda p, m: (m, p)),
                   full(lambda p, m: (0, p)), full(lambda p, m: (0, p)), full(lambda p, m: (0, 0))],
        out_shape=[jax.ShapeDtypeStruct((T, npair * 2 * LANES), BF16),
                   jax.ShapeDtypeStruct((T, npair * LANES), BF16),
                   jax.ShapeDtypeStruct((T, npair * LANES), BF16),
                   jax.ShapeDtypeStruct((T, LANES), F32)],
        scratch_shapes=[pltpu.VMEM((T, LANES), F32)] * 2,
        compiler_params=_params(("arbitrary", "arbitrary")),
    )(q, cos_q, sin_q, kv, kv, kr, o, lse, do)


def _split_dot(x, tri):
    hi = x.astype(BF16)
    lo = (x - hi.astype(F32)).astype(BF16)
    both = _dot(jnp.concatenate([hi, lo], axis=0), tri, NN)
    return both[:x.shape[0]] + both[x.shape[0]:]


def _sb_terms(qh, kk, before):
    z = _dot(qh, kk, NT)
    sp = jnp.maximum(z, 0.0) + jnp.log(1.0 + jnp.exp(-jnp.abs(z)))
    lk = -sp if before is None else jnp.where(before, -sp, 0.0)
    return z, sp, lk


def _sb_setup(q_ref, tq, tk, scale):
    qv = (q_ref[...].astype(F32) * scale).astype(BF16)
    lane = lax.broadcasted_iota(jnp.int32, (tq, LANES), 1)
    h0 = lane < HEAD
    qs = jnp.concatenate([jnp.where(h0, qv, 0), jnp.where(h0, 0, qv)], axis=0).astype(BF16)
    row = lax.broadcasted_iota(jnp.int32, (tk, tk), 0)
    col = lax.broadcasted_iota(jnp.int32, (tk, tk), 1)
    return qs, h0, row, col


def _sb_before(tq, tk, d):
    row = lax.broadcasted_iota(jnp.int32, (tq, tk), 0)
    col = lax.broadcasted_iota(jnp.int32, (tq, tk), 1) + d * tk
    return jnp.concatenate([col < row] * 2, axis=0)


def _two_heads(x, h0):
    tq = x.shape[0] // 2
    return jnp.where(h0, x[:tq], x[tq:])


def _sb_fwd(qkv, col0, name="sb_fwd"):
    T = qkv.shape[0]
    tq, tk = _pick(T, ATT_TQ), _pick(T, ATT_TK)
    nd = tq // tk
    npair = SB_HEADS // 2
    scale = SB_DIM ** -0.5

    def body(q_ref, k_ref, v_ref, o_ref, o32_ref, w_ref):
        m_idx = pl.program_id(1)
        qs, h0, row, col = _sb_setup(q_ref, tq, tk, scale)
        later = (row > col).astype(BF16)

        def block(kb, carry, before):
            ks = pl.ds(pl.multiple_of(kb * tk, tk), tk)
            c, acc = carry
            z, sp, lk = _sb_terms(qs, k_ref[ks, :].astype(BF16), before)
            w = jnp.exp((z - sp) + _split_dot(lk, later) + c)
            if before is not None:
                w = jnp.where(before, w, 0.0)
            wb = w.astype(BF16)
            w_ref[0, 0, kb] = wb
            return (c + jnp.sum(lk, axis=-1, keepdims=True), acc + _dot(wb, v_ref[ks, :].astype(BF16), NN))

        init = (jnp.zeros((2 * tq, 1), F32), jnp.zeros((2 * tq, LANES), F32))
        res = init
        for d in reversed(range(nd)):
            res = block(m_idx * nd + d, res, _sb_before(tq, tk, d))
        res = _by_twos(m_idx * nd, lambda i, c: block(m_idx * nd - 1 - i, c, None), res)
        o = _two_heads(res[1], h0)
        o_ref[...] = o.astype(o_ref.dtype)
        o32_ref[...] = o

    full = lambda col: pl.BlockSpec((T, LANES), col)
    blk = pl.BlockSpec((tq, LANES), lambda p, m: (m, p))
    return pl.pallas_call(
        body, name=name, grid=(npair, T // tq),
        in_specs=[pl.BlockSpec((tq, LANES), lambda p, m: (m, col0 + p)),
                  full(lambda p, m: (0, col0 + npair + p)), full(lambda p, m: (0, col0 + 2 * npair + p))],
        out_specs=[blk, blk, pl.BlockSpec((1, 1, T // tk, 2 * tq, tk), lambda p, m: (p, m, 0, 0, 0))],
        out_shape=[jax.ShapeDtypeStruct((T, npair * LANES), BF16), jax.ShapeDtypeStruct((T, npair * LANES), F32),
                   jax.ShapeDtypeStruct((npair, T // tq, T // tk, 2 * tq, tk), BF16)],
        compiler_params=_params(("parallel", "arbitrary")),
    )(qkv, qkv, qkv)


def _sb_bwd(qkv, col0, o32, w_all, do, do_col0, dep, name="sb_bwd"):
    T = qkv.shape[0]
    tq, tk = _pick(T, ATT_TQ), _pick(T, ATT_TK)
    nd = tq // tk
    npair = SB_HEADS // 2
    scale = SB_DIM ** -0.5

    def body(q_ref, k_ref, v_ref, o_ref, w_ref, do_ref, dep_ref, dq_ref, dk_ref, dv_ref, dk_acc, dv_acc):
        m_idx = pl.program_id(1)

        @pl.when(m_idx == 0)
        def _():
            dk_acc[...] = jnp.zeros_like(dk_acc)
            dv_acc[...] = jnp.zeros_like(dv_acc)

        qs, h0, row, col = _sb_setup(q_ref, tq, tk, scale)
        dov = do_ref[...].astype(F32)
        dos = jnp.concatenate([jnp.where(h0, dov, 0.0), jnp.where(h0, 0.0, dov)], axis=0).astype(BF16)
        ov = o_ref[...]
        etot = jnp.sum(dos.astype(F32) * jnp.concatenate([ov, ov], axis=0), axis=-1, keepdims=True)
        from_here = (row >= col).astype(BF16)

        def block(kb, carry, before):
            ks = pl.ds(pl.multiple_of(kb * tk, tk), tk)
            kk = k_ref[ks, :].astype(BF16)
            vv = v_ref[ks, :].astype(BF16)
            es, dqa = carry
            _, sp, _ = _sb_terms(qs, kk, None)
            wb = w_ref[0, 0, kb]
            e = wb.astype(F32) * _dot(dos, vv, NT)
            prev = etot - (_split_dot(e, from_here) + es)
            sig_neg = jnp.exp(-sp)
            dz = e * sig_neg - (1.0 - sig_neg) * prev
            if before is not None:
                dz = jnp.where(before, dz, 0.0)
            dzb = dz.astype(BF16)
            dk_acc[ks, :] += _dot(dzb, qs, TN)
            dv_acc[ks, :] += _dot(wb, dos, TN)
            return es + jnp.sum(e, axis=-1, keepdims=True), dqa + _dot(dzb, kk, NN)

        init = (jnp.zeros((2 * tq, 1), F32), jnp.zeros((2 * tq, LANES), F32))
        res = init
        for d in reversed(range(nd)):
            res = block(m_idx * nd + d, res, _sb_before(tq, tk, d))
        res = _by_twos(m_idx * nd, lambda i, c: block(m_idx * nd - 1 - i, c, None), res)
        dq_ref[...] = (_two_heads(res[1], h0) * scale).astype(BF16)

        @pl.when(m_idx == T // tq - 1)
        def _():
            dk_ref[...] = dk_acc[...].astype(BF16)
            dv_ref[...] = dv_acc[...].astype(BF16)

    full = lambda col: pl.BlockSpec((T, LANES), col)
    blk = lambda col: pl.BlockSpec((tq, LANES), col)
    return pl.pallas_call(
        body, name=name, grid=(npair, T // tq),
        in_specs=[blk(lambda p, m: (m, col0 + p)),
                  full(lambda p, m: (0, col0 + npair + p)), full(lambda p, m: (0, col0 + 2 * npair + p)),
                  blk(lambda p, m: (m, p)),
                  pl.BlockSpec((1, 1, T // tk, 2 * tq, tk), lambda p, m: (p, m, 0, 0, 0)),
                  blk(lambda p, m: (m, do_col0 + p)), pl.BlockSpec((8, LANES), lambda p, m: (0, 0))],
        out_specs=[blk(lambda p, m: (m, p)), full(lambda p, m: (0, p)), full(lambda p, m: (0, p))],
        out_shape=[jax.ShapeDtypeStruct((T, npair * LANES), BF16)] * 3,
        scratch_shapes=[pltpu.VMEM((T, LANES), F32)] * 2,
        compiler_params=_params(("arbitrary", "arbitrary")),
    )(qkv, qkv, qkv, o32, w_all, do, dep)


def _band_in_window():
    cq = lax.broadcasted_iota(jnp.int32, (BAND_TQ, BAND_W), 0) >> CHUNK_BITS
    ckp = lax.broadcasted_iota(jnp.int32, (BAND_TQ, BAND_W), 1) >> CHUNK_BITS
    return (ckp >= cq) & (ckp <= cq + LEFT_CHUNKS)


def _band_real(m_idx):
    j = lax.broadcasted_iota(jnp.int32, (BAND_TQ, BAND_W), 1)
    return j >= PAD_KEYS - m_idx * BAND_TQ


def _band_probs(qh, kw, bias, real, scale):
    s = jnp.where(real, _dot(qh, kw, NT) * scale + bias, NEG)
    e = jnp.exp(s - jnp.max(s, axis=-1, keepdims=True))
    return e * (1.0 / jnp.sum(e, axis=-1, keepdims=True))


BAND_SUB = 4


def _band_fwd(qkv, k_pad, v_pad, bias_w, name="band_fwd"):
    T = qkv.shape[0]
    npair = C_HEADS // 2
    scale = C_DIM ** -0.5
    rows = BAND_SUB * BAND_TQ

    def body(q_ref, k_ref, v_ref, b_ref, o_ref, p_ref):
        lane = lax.broadcasted_iota(jnp.int32, (BAND_TQ, LANES), 1)
        h0 = lane < HEAD
        bias = jnp.concatenate([b_ref[0], b_ref[1]], axis=0)
        for sub in range(BAND_SUB):
            m_idx = pl.program_id(1) * BAND_SUB + sub
            win = pl.ds(pl.multiple_of(m_idx * BAND_TQ, BAND_TQ), BAND_W)
            kw, vw = k_ref[win, :], v_ref[win, :]
            qv = q_ref[sub * BAND_TQ:(sub + 1) * BAND_TQ, :]
            qs = jnp.concatenate([jnp.where(h0, qv, 0), jnp.where(h0, 0, qv)], axis=0).astype(BF16)
            p = _band_probs(qs, kw, bias, jnp.concatenate([_band_real(m_idx)] * 2, axis=0), scale).astype(BF16)
            p_ref[0, sub] = p
            o = _two_heads(_dot(p, vw, NN), h0)
            o_ref[sub * BAND_TQ:(sub + 1) * BAND_TQ, :] = o.astype(o_ref.dtype)

    Tp = T + PAD_KEYS
    return pl.pallas_call(
        body, name=name, grid=(npair, T // rows),
        in_specs=[pl.BlockSpec((rows, LANES), lambda p, m: (m, p)),
                  pl.BlockSpec((Tp, LANES), lambda p, m: (0, p)),
                  pl.BlockSpec((Tp, LANES), lambda p, m: (0, p)),
                  pl.BlockSpec((2, BAND_TQ, BAND_W), lambda p, m: (p, 0, 0))],
        out_specs=[pl.BlockSpec((rows, LANES), lambda p, m: (m, p)),
                   pl.BlockSpec((1, BAND_SUB, 2 * BAND_TQ, BAND_W), lambda p, m: (p, m, 0, 0))],
        out_shape=[jax.ShapeDtypeStruct((T, npair * LANES), BF16),
                   jax.ShapeDtypeStruct((npair, T // BAND_TQ, 2 * BAND_TQ, BAND_W), BF16)],
        compiler_params=_params(("parallel", "arbitrary")),
    )(qkv, k_pad, v_pad, bias_w)


def _band_bwd(qkv, k_pad, v_pad, probs, do, name="band_bwd"):
    T = qkv.shape[0]
    npair = C_HEADS // 2
    scale = C_DIM ** -0.5

    rows = BAND_SUB * BAND_TQ

    def body(q_ref, k_ref, v_ref, p_ref, do_ref, dq_ref, dk_ref, dv_ref, db_ref, dk_acc, dv_acc):
        @pl.when(pl.program_id(1) == 0)
        def _():
            dk_acc[...] = jnp.zeros_like(dk_acc)
            dv_acc[...] = jnp.zeros_like(dv_acc)
            db_ref[...] = jnp.zeros_like(db_ref)

        lane = lax.broadcasted_iota(jnp.int32, (BAND_TQ, LANES), 1)
        h0 = lane < HEAD
        dbs = jnp.zeros((2 * BAND_TQ, BAND_W), F32)
        for sub in range(BAND_SUB):
            m_idx = pl.program_id(1) * BAND_SUB + sub
            win = pl.ds(pl.multiple_of(m_idx * BAND_TQ, BAND_TQ), BAND_W)
            kw, vw = k_ref[win, :], v_ref[win, :]
            qv = q_ref[sub * BAND_TQ:(sub + 1) * BAND_TQ, :]
            dov = do_ref[sub * BAND_TQ:(sub + 1) * BAND_TQ, :].astype(F32)
            qs = jnp.concatenate([jnp.where(h0, qv, 0), jnp.where(h0, 0, qv)], axis=0).astype(BF16)
            dos = jnp.concatenate([jnp.where(h0, dov, 0.0), jnp.where(h0, 0.0, dov)], axis=0).astype(BF16)
            pb = p_ref[0, sub]
            p = pb.astype(F32)
            dp = _dot(dos, vw, NT)
            dsb = p * (dp - jnp.sum(p * dp, axis=-1, keepdims=True))
            dbs = dbs + dsb
            dsq = (dsb * scale).astype(BF16)
            dq_ref[sub * BAND_TQ:(sub + 1) * BAND_TQ, :] = _two_heads(_dot(dsq, kw, NN), h0).astype(BF16)
            dk_acc[win, :] += _dot(dsq, qs, TN)
            dv_acc[win, :] += _dot(pb, dos, TN)
        db_ref[0] += dbs[:BAND_TQ]
        db_ref[1] += dbs[BAND_TQ:]

        @pl.when(pl.program_id(1) == T // rows - 1)
        def _():
            dk_ref[...] = dk_acc[...].astype(BF16)
            dv_ref[...] = dv_acc[...].astype(BF16)

    Tp = T + PAD_KEYS
    blk = lambda col: pl.BlockSpec((rows, LANES), col)
    full = pl.BlockSpec((Tp, LANES), lambda p, m: (0, p))
    bias = pl.BlockSpec((2, BAND_TQ, BAND_W), lambda p, m: (p, 0, 0))
    prob = pl.BlockSpec((1, BAND_SUB, 2 * BAND_TQ, BAND_W), lambda p, m: (p, m, 0, 0))
    return pl.pallas_call(
        body, name=name, grid=(npair, T // rows),
        in_specs=[blk(lambda p, m: (m, p)), full, full, prob, blk(lambda p, m: (m, p))],
        out_specs=[blk(lambda p, m: (m, p)), full, full, bias],
        out_shape=[jax.ShapeDtypeStruct((T, npair * LANES), BF16),
                   jax.ShapeDtypeStruct((Tp, npair * LANES), BF16),
                   jax.ShapeDtypeStruct((Tp, npair * LANES), BF16),
                   jax.ShapeDtypeStruct((C_HEADS, BAND_TQ, BAND_W), F32)],
        scratch_shapes=[pltpu.VMEM((Tp, LANES), F32)] * 2,
        compiler_params=_params(("arbitrary", "arbitrary")),
    )(qkv, k_pad, v_pad, probs, do)


def _skew_bits(x, left):
    w = x.shape[1]
    row = lax.broadcasted_iota(jnp.int32, x.shape, 0)
    for b in range(BAND_TQ.bit_length() - 1):
        amt = (w - (1 << b)) if left else (1 << b)
        x = jnp.where((row >> b) & 1 == 1, pltpu.roll(x, amt, 1), x)
    return x


def _toeplitz(diag, name="toeplitz"):
    H = diag.shape[0]

    def body(d_ref, o_ref):
        x = jnp.broadcast_to(d_ref[0], (BAND_TQ, TOEP_W))
        o_ref[0] = jnp.where(_band_in_window(), _skew_bits(x, left=False)[:, BAND_TQ:], NEG)

    return pl.pallas_call(
        body, name=name, grid=(H,),
        in_specs=[pl.BlockSpec((1, 1, TOEP_W), lambda h: (h, 0, 0))],
        out_specs=pl.BlockSpec((1, BAND_TQ, BAND_W), lambda h: (h, 0, 0)),
        out_shape=jax.ShapeDtypeStruct((H, BAND_TQ, BAND_W), F32),
        compiler_params=_params(("parallel",)),
    )(diag.reshape(H, 1, TOEP_W))


def _toeplitz_bwd(dbias, name="toeplitz_bwd"):
    H = dbias.shape[0]

    def body(d_ref, o_ref):
        x = jnp.concatenate([jnp.zeros((BAND_TQ, BAND_TQ), F32), d_ref[0]], axis=1)
        o_ref[0] = jnp.sum(_skew_bits(x, left=True), axis=0, keepdims=True)

    return pl.pallas_call(
        body, name=name, grid=(H,),
        in_specs=[pl.BlockSpec((1, BAND_TQ, BAND_W), lambda h: (h, 0, 0))],
        out_specs=pl.BlockSpec((1, 1, TOEP_W), lambda h: (h, 0, 0)),
        out_shape=jax.ShapeDtypeStruct((H, 1, TOEP_W), F32),
        compiler_params=_params(("parallel",)),
    )(dbias).reshape(H, TOEP_W)


_HBM = pl.BlockSpec(memory_space=pltpu.HBM)
_SEM = pl.BlockSpec(memory_space=pltpu.SEMAPHORE)
_EFFECT = pltpu.SideEffectType.DATAFLOW_SIDE_EFFECTING


def _peers():
    x, y, c = lax.axis_index("x"), lax.axis_index("y"), lax.axis_index("c")
    out = []
    for k in range(1, N_DEV):
        peer = (1 - x if (k >> 2) & 1 else x, 1 - y if (k >> 1) & 1 else y, 1 - c if k & 1 else c)
        out.append((peer, 4 * peer[0] + 2 * peer[1] + peer[2]))
    return 4 * x + 2 * y + c, out


def _split_copies(ins, lands, scatter, send_sem, recv_sem, arriving):
    me, peers = _peers()
    out = []
    for a in range(len(ins)):
        for peer, idx in peers:
            out.append(pltpu.make_async_remote_copy(
                src_ref=ins[a].at[idx] if scatter[a] else ins[a],
                dst_ref=lands[a].at[idx if arriving else me], send_sem=send_sem, recv_sem=recv_sem,
                device_id=peer, device_id_type=pl.DeviceIdType.MESH))
    return out


def _landing_zones(arrays, scatter):
    return [lax.empty((N_DEV,) + (a.shape[1:] if s else a.shape), a.dtype) for a, s in zip(arrays, scatter)]


def _place_own(arrays, scatter, name):
    n = len(arrays)
    lands = _landing_zones(arrays, scatter)
    me = (4 * lax.axis_index("x") + 2 * lax.axis_index("y") + lax.axis_index("c")).astype(jnp.int32).reshape(1)

    def body(me_ref, *refs):
        for a in range(n):
            refs[2 * n + a][...] = refs[a][...].reshape(refs[2 * n + a].shape)

    def row_spec(shape):
        zeros = (0,) * (len(shape) - 1)
        return pl.BlockSpec((1,) + tuple(shape[1:]), lambda i, me_ref: (me_ref[0],) + zeros)

    in_specs = [row_spec(a.shape) if s else pl.BlockSpec(a.shape, lambda i, me_ref, nd=a.ndim: (0,) * nd)
                for a, s in zip(arrays, scatter)]
    return pl.pallas_call(
        body, name=name,
        out_shape=[jax.ShapeDtypeStruct(l.shape, l.dtype) for l in lands],
        grid_spec=pltpu.PrefetchScalarGridSpec(
            num_scalar_prefetch=1, grid=(1,),
            in_specs=in_specs + [pl.BlockSpec(memory_space=pl.ANY)] * n,
            out_specs=[row_spec(l.shape) for l in lands]),
        input_output_aliases={1 + n + i: i for i in range(n)},
        compiler_params=_params(("arbitrary",)),
    )(me, *arrays, *lands)


def _exchange_start(arrays, scatter, after, name, lands=None):
    n = len(arrays)
    if lands is None:
        lands = list(_place_own(arrays, scatter, name=name.replace("_start_", "_own_")))

    def body(*refs):
        ins, lnd = refs[:n], refs[n:2 * n]
        send_sem, recv_sem = refs[2 * n + 1:2 * n + 3]
        token = refs[-1]
        for cp in _split_copies(ins, lnd, scatter, send_sem, recv_sem, arriving=False):
            cp.start()
        token[...] = jnp.zeros_like(token)

    hbm = lambda a: pltpu.HBM(a.shape, a.dtype)
    out = pl.pallas_call(
        body, name=name,
        out_shape=(pltpu.SemaphoreType.DMA(()), pltpu.SemaphoreType.DMA(()),
                   *[hbm(a) for a in arrays], *[hbm(a) for a in lands],
                   jax.ShapeDtypeStruct((8, LANES), F32)),
        in_specs=[_HBM] * (2 * n) + [pl.BlockSpec(memory_space=pl.ANY)],
        out_specs=(_SEM, _SEM, *([_HBM] * (2 * n)), pl.BlockSpec(memory_space=pltpu.VMEM)),
        input_output_aliases={i: 2 + i for i in range(2 * n)},
        compiler_params=pltpu.CompilerParams(has_side_effects=_EFFECT),
    )(*[pltpu.with_memory_space_constraint(a, pltpu.HBM) for a in list(arrays) + lands], after)
    return (out[0], out[1], list(out[2:2 + n]), list(out[2 + n:2 + 2 * n]), tuple(scatter)), out[-1]


def _exchange_wait(handle, after, name):
    send_sem, recv_sem, ins, lands, scatter = handle
    n = len(ins)
    after = after if isinstance(after, tuple) else (after,)

    def body(*refs):
        i_ref, l_ref = refs[:n], refs[n:2 * n]
        s_sem, r_sem = refs[2 * n:2 * n + 2]
        for cp in _split_copies(i_ref, l_ref, scatter, s_sem, r_sem, arriving=False):
            cp.wait_send()
        for cp in _split_copies(i_ref, l_ref, scatter, s_sem, r_sem, arriving=True):
            cp.wait_recv()

    hbm = lambda a: pltpu.HBM(a.shape, a.dtype)
    out = pl.pallas_call(
        body, name=name,
        out_shape=tuple(hbm(a) for a in ins + lands),
        in_specs=[_HBM] * (2 * n) + [_SEM, _SEM] + [pl.BlockSpec(memory_space=pl.ANY)] * len(after),
        out_specs=tuple([_HBM] * (2 * n)),
        input_output_aliases={i: i for i in range(2 * n)},
        compiler_params=pltpu.CompilerParams(has_side_effects=_EFFECT),
    )(*ins, *lands, send_sem, recv_sem, *after)
    return list(out[n:])


def _adamw(w, parts, m, v, name="adamw"):
    R, C = w.shape
    L = len(parts)
    rl = R // L
    tr = max([t for t in range(16, 513, 16) if rl % t == 0], default=rl)
    nb = rl // tr
    c1 = 1.0 - ADAM_B1 ** ADAM_STEP
    c2 = 1.0 - ADAM_B2 ** ADAM_STEP

    def body(*refs):
        w_ref, p_refs, (m_ref, v_ref, g_ref, d_ref, nm_ref, nv_ref) = refs[0], refs[1:1 + L], refs[1 + L:]
        g = None
        for j, p_ref in enumerate(p_refs):
            gj = p_ref[0].astype(F32)
            for i in range(1, N_DEV):
                gj = gj + p_ref[i].astype(F32)
            g = gj if g is None else jnp.where(pl.program_id(0) == j, gj, g)
        nm = ADAM_B1 * m_ref[...] + (1.0 - ADAM_B1) * g
        nv = ADAM_B2 * v_ref[...] + (1.0 - ADAM_B2) * (g * g)
        g_ref[...] = g
        nm_ref[...] = nm
        nv_ref[...] = nv
        d_ref[...] = -ADAM_LR * ((nm / c1) / (jnp.sqrt(nv / c2) + ADAM_EPS) + ADAM_WD * w_ref[...])

    blk = pl.BlockSpec((tr, C), lambda l, i: (l * nb + i, 0))
    part = lambda j: pl.BlockSpec((N_DEV, tr, C), lambda l, i: (0, jnp.where(l == j, i, 0), 0))
    return pl.pallas_call(
        body, name=name, grid=(L, nb),
        in_specs=[blk] + [part(j) for j in range(L)] + [blk, blk],
        out_specs=[blk] * 4,
        out_shape=[jax.ShapeDtypeStruct((R, C), F32)] * 4,
        compiler_params=_params(("arbitrary", "arbitrary")),
    )(w, *parts, m, v)


_O1 = Q_LORA
_O2 = _O1 + KV_LORA
_O3 = _O2 + MLA_ROPE
_NB = SB_HEADS * SB_DIM
IN_W = _O2 + LANES + 3 * _NB
COL_KR = _O2 // LANES
COL_SB = COL_KR + 1


def _w_in_local(w):
    kr = w[_O2:_O3]
    pad = jnp.zeros((LANES - 2 * MLA_ROPE, w.shape[1]), w.dtype)
    return jnp.concatenate([w[:_O2], kr, kr, pad, w[_O3:]], axis=0)


def _w_in_grad(g):
    kr = (g[_O2:_O2 + MLA_ROPE].astype(F32) + g[_O2 + MLA_ROPE:_O2 + 2 * MLA_ROPE].astype(F32)).astype(g.dtype)
    return jnp.concatenate([g[:_O2], kr, g[_O2 + LANES:]], axis=0)


def _w_uq_local(w):
    w3 = w.reshape(MLA_HEADS // 2, 2, MLA_NOPE + MLA_ROPE, w.shape[1])
    nope = w3[:, :, :MLA_NOPE].reshape(MLA_HEADS // 2, 2 * MLA_NOPE, w.shape[1])
    rope = w3[:, :, MLA_NOPE:].reshape(MLA_HEADS // 2, 2 * MLA_ROPE, w.shape[1])
    pad = jnp.zeros((MLA_HEADS // 2, LANES - 2 * MLA_ROPE, w.shape[1]), w.dtype)
    return jnp.concatenate([nope, rope, pad], axis=1).reshape(-1, w.shape[1])


def _w_uq_grad(g):
    g3 = g.reshape(MLA_HEADS // 2, 2 * LANES, g.shape[1])
    nope = g3[:, :2 * MLA_NOPE].reshape(MLA_HEADS // 2, 2, MLA_NOPE, g.shape[1])
    rope = g3[:, LANES:LANES + 2 * MLA_ROPE].reshape(MLA_HEADS // 2, 2, MLA_ROPE, g.shape[1])
    return jnp.concatenate([nope, rope], axis=2).reshape(-1, g.shape[1])


def _w_ukv_local(w):
    w3 = w.reshape(MLA_HEADS, MLA_NOPE + MLA_V, w.shape[1])
    return jnp.concatenate([w3[:, :MLA_NOPE].reshape(-1, w.shape[1]),
                            w3[:, MLA_NOPE:].reshape(-1, w.shape[1])], axis=0)


def _w_ukv_grad(g):
    half = MLA_HEADS * MLA_NOPE
    kn = g[:half].reshape(MLA_HEADS, MLA_NOPE, g.shape[1])
    vv = g[half:].reshape(MLA_HEADS, MLA_V, g.shape[1])
    return jnp.concatenate([kn, vv], axis=1).reshape(-1, g.shape[1])


def _rope_tables(T):
    pos = jnp.arange(T, dtype=F32)
    inv_freq = ROPE_THETA ** (-jnp.arange(0, MLA_ROPE, 2, dtype=F32) / MLA_ROPE)
    ang = pos[:, None] * inv_freq[None, :]
    cos, sin = jnp.cos(ang), jnp.sin(ang)
    ones = jnp.ones((T, LANES - 2 * MLA_ROPE), F32)
    cos_k = jnp.concatenate([cos, cos, cos, cos, ones], axis=1)
    sin_k = jnp.concatenate([-sin, sin, -sin, sin, 0.0 * ones], axis=1)
    cos_q = jnp.concatenate([jnp.ones((T, LANES), F32), cos_k], axis=1)
    sin_q = jnp.concatenate([jnp.zeros((T, LANES), F32), sin_k], axis=1)
    return cos_q, sin_q, cos_k, sin_k


def _bias_diag_index():
    ell = np.arange(TOEP_W)
    return np.clip(BAND_W - ell, -REL_CLIP, REL_CLIP) + REL_CLIP


def _local_step(x, target, small, get_weights, put_grads):
    T = x.shape[0]
    cos_q, sin_q, cos_k, sin_k = _rope_tables(T)
    G = {}
    W = dict(small)

    u0 = _rms_fwd(x, W["g_mix"][0:1], name="rms_mix0")
    bias_w = _toeplitz(W["od_rel_bias"][:, _bias_diag_index()])
    W.update(get_weights("in0", (u0, bias_w)))
    proj = _mm(u0, W["w_in_t"], dims="nt", name="proj_in")
    W.update(get_weights("mix0", proj))
    c_q, c_kv = proj[:, :_O1], proj[:, _O1:_O2]
    nq = _rms_fwd(c_q, W["g_cq"], name="rms_cq")
    nkv = _rms_fwd(c_kv, W["g_ckv"], name="rms_ckv")
    qa_raw = _mm(nq, W["w_uq_t"], dims="nt", name="proj_uq")
    kv = _mm(nkv, W["w_ukv_t"], dims="nt", out_dtype=BF16, name="proj_ukv")
    kr = _rope(proj, cos_k, sin_k, COL_KR, 1, BF16, name="rope_k")
    o_a, lse = _mla_fwd(qa_raw, cos_q, sin_q, kv, kr)
    o_b, o_b32, w_b = _sb_fwd(proj, COL_SB)
    o_ab = jnp.concatenate([o_a, o_b], axis=1)
    h1 = _mm(o_ab, W["ev_w_out"], res=x, name="out_ev")

    def ffn_fwd(h, layer):
        W.update(get_weights(f"ffn{layer}", h))
        return _ffn_fwd(h, W["g_ffn"][layer:layer + 1], W[f"w_gate_t{layer}"], W[f"w_up_t{layer}"],
                        W[f"w_down{layer}"], name=f"ffn_fwd{layer}")

    h2, u1, a0, b0 = ffn_fwd(h1, 0)

    W.update(get_weights("mix1", h2))
    u2 = _rms_fwd(h2, W["g_mix"][1:2], name="rms_mix1")
    qkv = _mm(u2, W["od_w_qkv_t"], dims="nt", out_dtype=BF16, name="proj_qkv")
    nc = C_HEADS * C_DIM
    pad = ((PAD_KEYS, 0), (0, 0))
    k_pad, v_pad = jnp.pad(qkv[:, nc:2 * nc], pad), jnp.pad(qkv[:, 2 * nc:], pad)
    o_c, p_c = _band_fwd(qkv, k_pad, v_pad, bias_w)
    h3 = _mm(o_c, W["od_w_out"], res=h2, name="out_od")
    h4, u3, a1, b1 = ffn_fwd(h3, 1)

    loss, dh, dhb, G["g_final"] = _loss_head(h4, W["g_final"], target)

    def ffn_bwd(dh, dhb, h, u, a, b, layer):
        du, g_gate, g_up, g_down = _ffn_bwd(dhb, u, a, b, W[f"w_gate_t{layer}"], W[f"w_up_t{layer}"],
                                            W[f"w_down{layer}"], name=f"ffn_bwd{layer}")
        tok = put_grads(f"ffn{layer}", {"w_gate_t": g_gate, "w_up_t": g_up, "w_down": g_down})
        return _rms_bwd(h, W["g_ffn"][layer:layer + 1] + tok[:1, :1], du, dres=dh, name=f"rms_ffn_bwd{layer}")

    dh3, dh3b, g_gffn1 = ffn_bwd(dh, dhb, h3, u3, a1, b1, 1)

    do_c = _mm(dh3b, W["od_w_out"], dims="nt", name="out_od_dx")
    g_od_out = _mm(o_c, dh3b, dims="tn", out_dtype=BF16, name="out_od_dw")
    dq_c, dk_p, dv_p, dbias_w = _band_bwd(qkv, k_pad, v_pad, p_c, do_c)
    dqkv = jnp.concatenate([dq_c, dk_p[PAD_KEYS:], dv_p[PAD_KEYS:]], axis=1)
    tok = put_grads("mix1", {"od_w_qkv_t": _mm(dqkv, u2, dims="tn", out_dtype=BF16, name="proj_qkv_dw"),
                             "od_w_out": g_od_out})
    ddiag = _toeplitz_bwd(dbias_w)
    n_far = BAND_W - REL_CLIP + 1
    G["od_rel_bias"] = jnp.concatenate(
        [jnp.zeros((C_HEADS, REL_CLIP - BAND_TQ + 1), F32), ddiag[:, n_far:][:, ::-1],
         jnp.sum(ddiag[:, :n_far], axis=1, keepdims=True)], axis=1)
    dh2, dh2b, g_gmix1 = _mm_rms_bwd(dqkv, W["od_w_qkv_t"], h2, W["g_mix"][1:2] + tok[:1, :1], dh3,
                                     name="proj_qkv_dx")

    dh1, dh1b, g_gffn0 = ffn_bwd(dh2, dh2b, h1, u1, a0, b0, 0)
    G["g_ffn"] = jnp.concatenate([g_gffn0, g_gffn1], axis=0)

    do_ab = _mm(dh1b, W["ev_w_out"], dims="nt", name="out_ev_dx")
    g0 = {"ev_w_out": _mm(o_ab, dh1b, dims="tn", out_dtype=BF16, name="out_ev_dw")}
    dqa_raw, dkn, dva, dkr = _mla_bwd(qa_raw, cos_q, sin_q, kv, kr, o_a, lse, do_ab, 0)
    g0["w_uq_t"] = _mm(dqa_raw, nq, dims="tn", name="proj_uq_dw")
    dnq = _mm(dqa_raw, W["w_uq_t"], name="proj_uq_dx")
    _, dc_q, G["g_cq"] = _rms_bwd(c_q, W["g_cq"], dnq, name="rms_cq_bwd")
    dkv = jnp.concatenate([dkn, dva], axis=1)
    g0["w_ukv_t"] = _mm(dkv, nkv, dims="tn", name="proj_ukv_dw")
    dnkv = _mm(dkv, W["w_ukv_t"], name="proj_ukv_dx")
    _, dc_kv, G["g_ckv"] = _rms_bwd(c_kv, W["g_ckv"], dnkv, name="rms_ckv_bwd")
    tok = put_grads("mix0", g0)
    dqb, dkb, dvb = _sb_bwd(proj, COL_SB, o_b32, w_b, do_ab, MLA_HEADS // 2, tok)
    dkr_raw = _rope(dkr, cos_k, -sin_k, 0, 1, BF16, name="rope_k_bwd")
    dproj = jnp.concatenate([dc_q, dc_kv, dkr_raw, dqb, dkb, dvb], axis=1)
    tok = put_grads("in0", {"w_in_t": _mm(dproj, u0, dims="tn", name="proj_in_dw")})
    dx, _, g_gmix0 = _mm_rms_bwd(dproj, W["w_in_t"], x, W["g_mix"][0:1] + tok[:1, :1], dh1, name="proj_in_dx")
    G["g_mix"] = jnp.concatenate([g_gmix0, g_gmix1], axis=0)
    return loss[0, 0], dx, G


_BIG = ["ev_w_in", "ev_w_uq", "ev_w_ukv", "ev_w_out", "od_w_qkv", "od_w_out", "w_gate", "w_up", "w_down"]
_COL_SHARDED = {"ev_w_in", "ev_w_uq", "ev_w_ukv", "od_w_qkv", "w_gate", "w_up"}
_SMALL = ["ev_g_cq", "ev_g_ckv", "od_rel_bias", "g_mix", "g_ffn", "g_final"]
_GROUPS = {
    "in0": ["ev_w_in"],
    "mix0": ["ev_w_uq", "ev_w_ukv", "ev_w_out"],
    "ffn0": ["w_gate0", "w_up0", "w_down0"],
    "mix1": ["od_w_qkv", "od_w_out"],
    "ffn1": ["w_gate1", "w_up1", "w_down1"],
}
_GROUP_SRC = {n + str(l): (n, l) for n in ("w_gate", "w_up", "w_down") for l in (0, 1)}
_BATCHES = {"in0": ["in0"], "layer0": ["mix0", "ffn0"], "layer1": ["mix1", "ffn1"]}
_BATCH_OF = {grp: batch for batch, grps in _BATCHES.items() for grp in grps}
_SMALL_ROWS = 8
_SMALL_COLS = 1792


def _pack_small(vals):
    flat = jnp.concatenate([v.reshape(-1).astype(F32) for v in vals])
    flat = jnp.pad(flat, (0, _SMALL_ROWS * _SMALL_COLS - flat.shape[0]))
    return flat.reshape(_SMALL_ROWS, _SMALL_COLS)


def _unpack_small(packed, like):
    flat = packed.reshape(-1)
    out, off = [], 0
    for v in like:
        out.append(flat[off:off + v.size].reshape(v.shape))
        off += v.size
    return out


def kernel(x, ev_w_in, ev_g_cq, ev_w_uq, ev_g_ckv, ev_w_ukv, ev_w_out, od_w_qkv, od_rel_bias, od_w_out, g_mix, g_ffn, w_gate, w_up, w_down, g_final, loss_target, m_ev_w_in, m_ev_g_cq, m_ev_w_uq, m_ev_g_ckv, m_ev_w_ukv, m_ev_w_out, m_od_w_qkv, m_od_rel_bias, m_od_w_out, m_g_mix, m_g_ffn, m_w_gate, m_w_up, m_w_down, m_g_final, v_ev_w_in, v_ev_g_cq, v_ev_w_uq, v_ev_g_ckv, v_ev_w_ukv, v_ev_w_out, v_od_w_qkv, v_od_rel_bias, v_od_w_out, v_g_mix, v_g_ffn, v_w_gate, v_w_up, v_w_down, v_g_final):
    args = dict(locals())
    w = {n: args[n] for n in _BIG + _SMALL}
    mom = {n: args["m_" + n] for n in _BIG + _SMALL}
    var = {n: args["v_" + n] for n in _BIG + _SMALL}

    own = {}
    for grp, names in _GROUPS.items():
        for n in names:
            base, layer = _GROUP_SRC.get(n, (n, 0))
            shard = w[base][layer:layer + 1]
            own[n] = (jnp.swapaxes(shard, 1, 2) if base in _COL_SHARDED else shard).astype(BF16)
    placed = dict(zip(own, _place_own(list(own.values()), [False] * len(own), name="gather_own")))
    gather, token = {}, x[0, :8, :LANES]
    for grp, names in _GROUPS.items():
        gather[grp], token = _exchange_start([own[n] for n in names], [False] * len(names), token,
                                             name="gather_start_" + grp, lands=[placed[n] for n in names])

    def get_weights(grp, after):
        names = _GROUPS[grp]
        lands = _exchange_wait(gather[grp], token if after is None else after, name="gather_wait_" + grp)
        full = {n: l.reshape(-1, l.shape[-1]) for n, l in zip(names, lands)}
        if grp == "in0":
            return {"w_in_t": _w_in_local(full["ev_w_in"])}
        if grp == "mix0":
            return {"w_uq_t": _w_uq_local(full["ev_w_uq"]), "w_ukv_t": _w_ukv_local(full["ev_w_ukv"]),
                    "ev_w_out": full["ev_w_out"]}
        if grp == "mix1":
            return {"od_w_qkv_t": full["od_w_qkv"], "od_w_out": full["od_w_out"]}
        layer = grp[-1]
        return {"w_gate_t" + layer: full["w_gate" + layer], "w_up_t" + layer: full["w_up" + layer],
                "w_down" + layer: full["w_down" + layer]}

    scatter, pending = {}, {}

    def put_grads(grp, g):
        if grp == "in0":
            g = {"ev_w_in": _w_in_grad(g["w_in_t"])}
        elif grp == "mix0":
            g = {"ev_w_uq": _w_uq_grad(g["w_uq_t"]), "ev_w_ukv": _w_ukv_grad(g["w_ukv_t"]),
                 "ev_w_out": g["ev_w_out"]}
        elif grp == "mix1":
            g = {"od_w_qkv": g["od_w_qkv_t"], "od_w_out": g["od_w_out"]}
        else:
            layer = grp[-1]
            g = {"w_gate" + layer: g["w_gate_t"], "w_up" + layer: g["w_up_t"], "w_down" + layer: g["w_down"]}
        pending.update({n: v.reshape(N_DEV, 1, v.shape[0] // N_DEV, v.shape[1]).astype(BF16) for n, v in g.items()})
        batch = _BATCH_OF[grp]
        names = [n for gr in _BATCHES[batch] for n in _GROUPS[gr]]
        if not all(n in pending for n in names):
            return jnp.zeros((8, LANES), F32)
        send = [pending[n] for n in names]
        scatter[batch], tok = _exchange_start(send, [True] * len(names), send[0], name="scatter_start_" + batch)
        return tok

    small = {"g_cq": ev_g_cq, "g_ckv": ev_g_ckv, "od_rel_bias": od_rel_bias[0],
             "g_mix": g_mix + token[0, 0], "g_ffn": g_ffn, "g_final": g_final.reshape(1, -1)}
    loss_part, dx, G = _local_step(x[0], loss_target[0], small, get_weights, put_grads)
    g_small = _pack_small([G["g_cq"], G["g_ckv"], G["od_rel_bias"], G["g_mix"], G["g_ffn"], G["g_final"],
                           loss_part.reshape(1)])
    small_handle, _ = _exchange_start([g_small], [False], dx, name="gather_start_small")

    grads, deltas, new_m, new_v = {}, {}, {}, {}
    parts, after = {}, dx

    def wait_parts(batch, after):
        lands = _exchange_wait(scatter[batch], after, name="scatter_wait_" + batch)
        parts.update(zip([n for grp in _BATCHES[batch] for n in _GROUPS[grp]], lands))
        return lands[0]

    def adamw(n):
        col = n in _COL_SHARDED
        rows = lambda a: (jnp.swapaxes(a, 1, 2) if col else a).reshape(-1, a.shape[1 if col else 2])
        layers = [parts[n]] if n in parts else [parts[n + "0"], parts[n + "1"]]
        res = _adamw(rows(w[n]), [p.reshape(N_DEV, -1, p.shape[-1]) for p in layers], rows(mom[n]), rows(var[n]),
                     name="adamw_" + n)
        L, a1, a2 = w[n].shape
        back = lambda r: jnp.swapaxes(r.reshape(L, a2, a1), 1, 2) if col else r.reshape(L, a1, a2)
        grads[n], deltas[n], new_m[n], new_v[n] = [back(r) for r in res]
        return res[0]

    for batch in ("layer1", "layer0"):
        after = wait_parts(batch, after)
    for n in _BIG[1:]:
        after = adamw(n)
    after = wait_parts("in0", after)
    after = adamw("ev_w_in")
    small_w = [w[n] for n in _SMALL]
    small_parts = _exchange_wait(small_handle, after, name="gather_wait_small")[0]
    loss = jnp.sum(small_parts.reshape(N_DEV, -1)[:, sum(v.size for v in small_w)])
    res = _adamw(_pack_small(small_w), [small_parts], _pack_small([mom[n] for n in _SMALL]),
                 _pack_small([var[n] for n in _SMALL]), name="adamw_small")
    for d, packed in zip((grads, deltas, new_m, new_v), res):
        for n, val in zip(_SMALL, _unpack_small(packed, small_w)):
            d[n] = val

    order = ["ev_w_in", "ev_g_cq", "ev_w_uq", "ev_g_ckv", "ev_w_ukv", "ev_w_out", "od_w_qkv", "od_rel_bias",
             "od_w_out", "g_mix", "g_ffn", "w_gate", "w_up", "w_down", "g_final"]
    out = [loss, dx[None]]
    for d in (grads, deltas, new_m, new_v):
        out += [d[n] for n in order]
    return tuple(out)
```

```python
import functools

import numpy as np
import jax
import jax.numpy as jnp
from jax import lax
from jax.experimental import pallas as pl
from jax.experimental.pallas import tpu as pltpu

F32 = jnp.float32
BF16 = jnp.bfloat16

D_MODEL = 1024
CHUNK = 64
MLA_HEADS = 8
MLA_NOPE = 64
MLA_ROPE = 32
MLA_V = 64
Q_LORA = 384
KV_LORA = 256
ROPE_THETA = 10000.0
SB_HEADS = 8
SB_DIM = 64
C_HEADS = 16
C_DIM = 64
LEFT_CHUNKS = 8
REL_CLIP = 256
D_FF = 2816
RMS_EPS = 1e-6
ADAM_LR = 0.001
ADAM_B1 = 0.9
ADAM_B2 = 0.999
ADAM_EPS = 1e-08
ADAM_WD = 0.01
ADAM_STEP = 10

N_DEV = 8
LANES = 128
HEAD = 64
assert HEAD == MLA_NOPE == MLA_V == SB_DIM == C_DIM and 2 * HEAD == LANES
CHUNK_BITS = CHUNK.bit_length() - 1
assert 1 << CHUNK_BITS == CHUNK
VMEM_LIMIT = 56 * 1024 * 1024
NEG = -1e30
PAD_KEYS = LEFT_CHUNKS * CHUNK
BAND_TQ = 128
BAND_W = BAND_TQ + PAD_KEYS
TOEP_W = BAND_W + BAND_TQ

NN = (((1,), (0,)), ((), ()))
NT = (((1,), (1,)), ((), ()))
TN = (((0,), (0,)), ((), ()))


def _dot(a, b, dn):
    return lax.dot_general(a, b, dn, preferred_element_type=F32)


def _pick(dim, pref):
    if dim <= pref:
        return dim
    best = None
    for t in range(LANES, pref + 1, LANES):
        if dim % t == 0:
            best = t
    assert best is not None, (dim, pref)
    return best


def _params(sem):
    return pltpu.CompilerParams(dimension_semantics=sem, vmem_limit_bytes=VMEM_LIMIT)


def _mm(a, b, dims="nn", res=None, out_dtype=F32, name="mm"):
    if dims == "nn":
        (M, K), (K2, N) = a.shape, b.shape
    elif dims == "nt":
        (M, K), (N, K2) = a.shape, b.shape
    else:
        (K, M), (K2, N) = a.shape, b.shape
    assert K == K2, (a.shape, b.shape, dims)
    tm, tn, tk = _pick(M, 1024), _pick(N, 1152), _pick(K, 1024)
    nk = K // tk
    dn = {"nn": NN, "nt": NT, "tn": TN}[dims]
    has_res = res is not None

    def body(*refs):
        if has_res:
            a_ref, b_ref, r_ref, o_ref, acc = refs
        else:
            a_ref, b_ref, o_ref, acc = refs
        k = pl.program_id(2)

        @pl.when(k == 0)
        def _():
            acc[...] = jnp.zeros_like(acc)

        acc[...] += _dot(a_ref[...].astype(BF16), b_ref[...].astype(BF16), dn)

        @pl.when(k == nk - 1)
        def _():
            r = acc[...]
            if has_res:
                r = r + r_ref[...]
            o_ref[...] = r.astype(out_dtype)

    a_spec = (pl.BlockSpec((tk, tm), lambda i, j, k: (k, i)) if dims == "tn"
              else pl.BlockSpec((tm, tk), lambda i, j, k: (i, k)))
    b_spec = (pl.BlockSpec((tn, tk), lambda i, j, k: (j, k)) if dims == "nt"
              else pl.BlockSpec((tk, tn), lambda i, j, k: (k, j)))
    o_spec = pl.BlockSpec((tm, tn), lambda i, j, k: (i, j))
    in_specs = [a_spec, b_spec] + ([o_spec] if has_res else [])
    args = (a, b) + ((res,) if has_res else ())
    return pl.pallas_call(
        body, name=name, grid=(M // tm, N // tn, nk),
        in_specs=in_specs, out_specs=o_spec,
        out_shape=jax.ShapeDtypeStruct((M, N), out_dtype),
        scratch_shapes=[pltpu.VMEM((tm, tn), F32)],
        compiler_params=_params(("parallel", "parallel", "arbitrary")),
    )(*args)


def _rms_fwd(x, g, out_dtype=BF16, name="rms_fwd"):
    T, Fd = x.shape
    tm = _pick(T, 256)

    def body(x_ref, g_ref, o_ref):
        xv = x_ref[...]
        r = lax.rsqrt(jnp.mean(xv * xv, axis=-1, keepdims=True) + RMS_EPS)
        o_ref[...] = (xv * r * g_ref[...]).astype(out_dtype)

    return pl.pallas_call(
        body, name=name, grid=(T // tm,),
        in_specs=[pl.BlockSpec((tm, Fd), lambda i: (i, 0)), pl.BlockSpec((1, Fd), lambda i: (0, 0))],
        out_specs=pl.BlockSpec((tm, Fd), lambda i: (i, 0)),
        out_shape=jax.ShapeDtypeStruct((T, Fd), out_dtype),
        compiler_params=_params(("parallel",)),
    )(x, g)


def _rms_bwd(x, g, dy, dres=None, name="rms_bwd"):
    T, Fd = x.shape
    tm = _pick(T, 256)
    has_res = dres is not None

    def body(*refs):
        if has_res:
            x_ref, g_ref, dy_ref, r_ref, dx_ref, dxb_ref, dg_ref = refs
        else:
            x_ref, g_ref, dy_ref, dx_ref, dxb_ref, dg_ref = refs
        xv, dyv = x_ref[...], dy_ref[...]
        r = lax.rsqrt(jnp.mean(xv * xv, axis=-1, keepdims=True) + RMS_EPS)
        gdy = dyv * g_ref[...]
        dot = jnp.mean(xv * gdy, axis=-1, keepdims=True)
        dx = r * gdy - xv * (r * r * r * dot)
        if has_res:
            dx = dx + r_ref[...]
        dx_ref[...] = dx
        dxb_ref[...] = dx.astype(BF16)

        @pl.when(pl.program_id(0) == 0)
        def _():
            dg_ref[...] = jnp.zeros_like(dg_ref)

        dg_ref[...] += jnp.sum(dyv * xv * r, axis=0, keepdims=True)

    row = pl.BlockSpec((tm, Fd), lambda i: (i, 0))
    vec = pl.BlockSpec((1, Fd), lambda i: (0, 0))
    in_specs = [row, vec, row] + ([row] if has_res else [])
    args = (x, g, dy) + ((dres,) if has_res else ())
    return pl.pallas_call(
        body, name=name, grid=(T // tm,),
        in_specs=in_specs, out_specs=[row, row, vec],
        out_shape=[jax.ShapeDtypeStruct((T, Fd), F32), jax.ShapeDtypeStruct((T, Fd), BF16),
                   jax.ShapeDtypeStruct((1, Fd), F32)],
        compiler_params=_params(("arbitrary",)),
    )(*args)


def _mm_rms_bwd(a, b, x, g, dres, name="mm_rms_bwd"):
    T, K = a.shape
    Fd = b.shape[1]
    tm, tk = _pick(T, 512), _pick(K, 1024)
    nk = K // tk

    def body(a_ref, b_ref, x_ref, g_ref, r_ref, dx_ref, dxb_ref, dg_ref, acc):
        i, k = pl.program_id(0), pl.program_id(1)

        @pl.when(k == 0)
        def _():
            acc[...] = jnp.zeros_like(acc)

        @pl.when((k == 0) & (i == 0))
        def _():
            dg_ref[...] = jnp.zeros_like(dg_ref)

        acc[...] += _dot(a_ref[...].astype(BF16), b_ref[...].astype(BF16), NN)

        @pl.when(k == nk - 1)
        def _():
            xv, dyv = x_ref[...], acc[...]
            r = lax.rsqrt(jnp.mean(xv * xv, axis=-1, keepdims=True) + RMS_EPS)
            gdy = dyv * g_ref[...]
            dot = jnp.mean(xv * gdy, axis=-1, keepdims=True)
            dx = r * gdy - xv * (r * r * r * dot) + r_ref[...]
            dx_ref[...] = dx
            dxb_ref[...] = dx.astype(BF16)
            dg_ref[...] += jnp.sum(dyv * xv * r, axis=0, keepdims=True)

    row = pl.BlockSpec((tm, Fd), lambda i, k: (i, 0))
    vec = pl.BlockSpec((1, Fd), lambda i, k: (0, 0))
    return pl.pallas_call(
        body, name=name, grid=(T // tm, nk),
        in_specs=[pl.BlockSpec((tm, tk), lambda i, k: (i, k)), pl.BlockSpec((tk, Fd), lambda i, k: (k, 0)),
                  row, vec, row],
        out_specs=[row, row, vec],
        out_shape=[jax.ShapeDtypeStruct((T, Fd), F32), jax.ShapeDtypeStruct((T, Fd), BF16),
                   jax.ShapeDtypeStruct((1, Fd), F32)],
        scratch_shapes=[pltpu.VMEM((tm, Fd), F32)],
        compiler_params=_params(("arbitrary", "arbitrary")),
    )(a, b, x, g, dres)


def _loss_head(h, g, target, name="loss_head"):
    T, Fd = h.shape
    tm = _pick(T, 256)

    def body(h_ref, g_ref, t_ref, loss_ref, dh_ref, dhb_ref, dg_ref):
        xv = h_ref[...]
        r = lax.rsqrt(jnp.mean(xv * xv, axis=-1, keepdims=True) + RMS_EPS)
        diff = xv * r * g_ref[...] - t_ref[...]
        part = 0.5 * jnp.sum(jnp.mean(diff * diff, axis=-1, keepdims=True), axis=0, keepdims=True)
        dyv = diff * (1.0 / Fd)
        gdy = dyv * g_ref[...]
        dot = jnp.mean(xv * gdy, axis=-1, keepdims=True)
        dh = r * gdy - xv * (r * r * r * dot)
        dh_ref[...] = dh
        dhb_ref[...] = dh.astype(BF16)

        @pl.when(pl.program_id(0) == 0)
        def _():
            dg_ref[...] = jnp.zeros_like(dg_ref)
            loss_ref[...] = jnp.zeros_like(loss_ref)

        dg_ref[...] += jnp.sum(dyv * xv * r, axis=0, keepdims=True)
        loss_ref[...] += jnp.broadcast_to(part, loss_ref.shape)

    row = pl.BlockSpec((tm, Fd), lambda i: (i, 0))
    vec = pl.BlockSpec((1, Fd), lambda i: (0, 0))
    return pl.pallas_call(
        body, name=name, grid=(T // tm,),
        in_specs=[row, vec, row],
        out_specs=[pl.BlockSpec((1, LANES), lambda i: (0, 0)), row, row, vec],
        out_shape=[jax.ShapeDtypeStruct((1, LANES), F32), jax.ShapeDtypeStruct((T, Fd), F32),
                   jax.ShapeDtypeStruct((T, Fd), BF16), jax.ShapeDtypeStruct((1, Fd), F32)],
        compiler_params=_params(("arbitrary",)),
    )(h, g, target)


FFN_TF = 256


def _ffn_fwd(h, g, wg_t, wu_t, wd, name="ffn_fwd"):
    T, Dm = h.shape
    Fh = wd.shape[0]
    tm = _pick(T, 1024)
    nf = Fh // FFN_TF

    def body(h_ref, g_ref, wg_ref, wu_ref, wd_ref, o_ref, u_ref, a_ref, b_ref):
        j = pl.program_id(1)

        @pl.when(j == 0)
        def _():
            xv = h_ref[...]
            r = lax.rsqrt(jnp.mean(xv * xv, axis=-1, keepdims=True) + RMS_EPS)
            u_ref[...] = (xv * r * g_ref[...]).astype(BF16)
            o_ref[...] = xv

        u = u_ref[...]
        a = _dot(u, wg_ref[...], NT).astype(BF16)
        b = _dot(u, wu_ref[...], NT).astype(BF16)
        a_ref[...] = a
        b_ref[...] = b
        af = a.astype(F32)
        s = (af * jax.nn.sigmoid(af) * b.astype(F32)).astype(BF16)
        o_ref[...] += _dot(s, wd_ref[...], NN)

    row = pl.BlockSpec((tm, Dm), lambda i, j: (i, 0))
    wblk = pl.BlockSpec((FFN_TF, Dm), lambda i, j: (j, 0))
    ablk = pl.BlockSpec((tm, FFN_TF), lambda i, j: (i, j))
    return pl.pallas_call(
        body, name=name, grid=(T // tm, nf),
        in_specs=[row, pl.BlockSpec((1, Dm), lambda i, j: (0, 0)), wblk, wblk, wblk],
        out_specs=[row, row, ablk, ablk],
        out_shape=[jax.ShapeDtypeStruct((T, Dm), F32), jax.ShapeDtypeStruct((T, Dm), BF16),
                   jax.ShapeDtypeStruct((T, Fh), BF16), jax.ShapeDtypeStruct((T, Fh), BF16)],
        compiler_params=_params(("parallel", "arbitrary")),
    )(h, g, wg_t, wu_t, wd)


def _ffn_bwd(dh, u, a, b, wg_t, wu_t, wd, name="ffn_bwd"):
    T, Dm = dh.shape
    Fh = wd.shape[0]
    nf = Fh // FFN_TF
    once = pl.Buffered(1)

    def body(dh_ref, u_ref, a_ref, b_ref, wg_ref, wu_ref, wd_ref, du_ref, dwg_ref, dwu_ref, dwd_ref):
        j = pl.program_id(0)

        @pl.when(j == 0)
        def _():
            du_ref[...] = jnp.zeros_like(du_ref)

        ds = _dot(dh_ref[...], wd_ref[...], NT)
        af, bf = a_ref[...].astype(F32), b_ref[...].astype(F32)
        sig = jax.nn.sigmoid(af)
        sa = af * sig
        dwd_ref[...] = _dot((sa * bf).astype(BF16), dh_ref[...], TN).astype(BF16)
        dab = jnp.concatenate([(ds * bf * (sig * (1.0 + af * (1.0 - sig)))).astype(BF16),
                               (ds * sa).astype(BF16)], axis=1)
        dw = _dot(dab, u_ref[...], TN)
        dwg_ref[...] = dw[:FFN_TF].astype(BF16)
        dwu_ref[...] = dw[FFN_TF:].astype(BF16)
        du_ref[...] += _dot(dab, jnp.concatenate([wg_ref[...], wu_ref[...]], axis=0), NN)

    full = lambda: pl.BlockSpec((T, Dm), lambda j: (0, 0), pipeline_mode=once)
    wblk = pl.BlockSpec((FFN_TF, Dm), lambda j: (j, 0))
    ablk = pl.BlockSpec((T, FFN_TF), lambda j: (0, j))
    return pl.pallas_call(
        body, name=name, grid=(nf,),
        in_specs=[full(), full(), ablk, ablk, wblk, wblk, wblk],
        out_specs=[pl.BlockSpec((T, Dm), lambda j: (0, 0)), wblk, wblk, wblk],
        out_shape=[jax.ShapeDtypeStruct((T, Dm), F32)] + [jax.ShapeDtypeStruct((Fh, Dm), BF16)] * 3,
        compiler_params=_params(("arbitrary",)),
    )(dh, u, a, b, wg_t, wu_t, wd)


def _rope(x, cos_t, sin_t, col0, ncols, out_dtype, name="rope"):
    T = x.shape[0]
    wt = cos_t.shape[1]
    tm = _pick(T, 256)
    nb = ncols * LANES // wt
    half = MLA_ROPE // 2

    def body(x_ref, c_ref, s_ref, o_ref):
        xv = x_ref[...].astype(F32)
        lane = lax.broadcasted_iota(jnp.int32, xv.shape, 1)
        first = (lane & (MLA_ROPE - 1)) < half
        swapped = jnp.where(first, pltpu.roll(xv, wt - half, 1), pltpu.roll(xv, half, 1))
        o_ref[...] = (xv * c_ref[...] + swapped * s_ref[...]).astype(out_dtype)

    off = col0 * LANES // wt
    return pl.pallas_call(
        body, name=name, grid=(T // tm, nb),
        in_specs=[pl.BlockSpec((tm, wt), lambda i, j: (i, j + off)),
                  pl.BlockSpec((tm, wt), lambda i, j: (i, 0)),
                  pl.BlockSpec((tm, wt), lambda i, j: (i, 0))],
        out_specs=pl.BlockSpec((tm, wt), lambda i, j: (i, j)),
        out_shape=jax.ShapeDtypeStruct((T, ncols * LANES), out_dtype),
        compiler_params=_params(("parallel", "parallel")),
    )(x, cos_t, sin_t)


ATT_TQ = 512
ATT_TK = 256


def _mla_masks(shape):
    lane = lax.broadcasted_iota(jnp.int32, shape, 1)
    m0 = (lane < HEAD) | ((lane >= LANES) & (lane < LANES + MLA_ROPE))
    m1 = ((lane >= HEAD) & (lane < LANES)) | ((lane >= LANES + MLA_ROPE) & (lane < LANES + 2 * MLA_ROPE))
    return m0, m1


def _by_twos(n, step, carry):
    carry = lax.fori_loop(0, n // 2, lambda i, c: step(2 * i + 1, step(2 * i, c)), carry)
    return lax.fori_loop(0, n % 2, lambda _, c: step(n - 1, c), carry)


def _chunk_ok(tq, tk, d):
    row = lax.broadcasted_iota(jnp.int32, (tq, tk), 0)
    col = lax.broadcasted_iota(jnp.int32, (tq, tk), 1) + d * tk
    return jnp.concatenate([(col >> CHUNK_BITS) <= (row >> CHUNK_BITS)] * 2, axis=0)


def _rotate(x, cos_t, sin_t):
    half = MLA_ROPE // 2
    lane = lax.broadcasted_iota(jnp.int32, x.shape, 1)
    first = (lane & (MLA_ROPE - 1)) < half
    swapped = jnp.where(first, pltpu.roll(x, x.shape[1] - half, 1), pltpu.roll(x, half, 1))
    return x * cos_t + swapped * sin_t


def _mla_fwd(q, cos_q, sin_q, kv, kr, name="mla_fwd"):
    T = q.shape[0]
    tq, tk = _pick(T, ATT_TQ), _pick(T, ATT_TK)
    nd = tq // tk
    npair = MLA_HEADS // 2
    scale = (MLA_NOPE + MLA_ROPE) ** -0.5

    def body(q_ref, c_ref, s_ref, kn_ref, v_ref, kr_ref, o_ref, lse_ref):
        m_idx = pl.program_id(1)
        qv = _rotate(q_ref[...], c_ref[...], s_ref[...]).astype(BF16)
        m0, m1 = _mla_masks(qv.shape)
        qs = jnp.concatenate([jnp.where(m0, qv, 0), jnp.where(m1, qv, 0)], axis=0).astype(BF16)

        def block(kb, carry, ok):
            ks = pl.ds(pl.multiple_of(kb * tk, tk), tk)
            kcat = jnp.concatenate([kn_ref[ks, :], kr_ref[ks, :]], axis=1)
            mx, l, acc = carry
            s = _dot(qs, kcat, NT) * scale
            if ok is not None:
                s = jnp.where(ok, s, NEG)
            mn = jnp.maximum(mx, jnp.max(s, axis=-1, keepdims=True))
            alpha = jnp.exp(mx - mn)
            p = jnp.exp(s - mn)
            return (mn, alpha * l + jnp.sum(p, axis=-1, keepdims=True),
                    alpha * acc + _dot(p.astype(BF16), v_ref[ks, :], NN))

        init = (jnp.full((2 * tq, 1), NEG, F32), jnp.zeros((2 * tq, 1), F32), jnp.zeros((2 * tq, LANES), F32))
        res = init
        for d in range(nd):
            res = block(m_idx * nd + d, res, _chunk_ok(tq, tk, d))
        mx, l, acc = _by_twos(m_idx * nd, lambda kb, c: block(kb, c, None), res)
        h0 = lax.broadcasted_iota(jnp.int32, (tq, LANES), 1) < HEAD
        o_ref[...] = _two_heads(acc * (1.0 / l), h0).astype(o_ref.dtype)
        lse_ref[...] = _two_heads(jnp.broadcast_to(mx + jnp.log(l), (2 * tq, LANES)), h0)

    full = lambda col: pl.BlockSpec((T, LANES), col)
    table = pl.BlockSpec((tq, 2 * LANES), lambda p, m: (m, 0))
    return pl.pallas_call(
        body, name=name, grid=(npair, T // tq),
        in_specs=[pl.BlockSpec((tq, 2 * LANES), lambda p, m: (m, p)), table, table,
                  full(lambda p, m: (0, p)), full(lambda p, m: (0, npair + p)), full(lambda p, m: (0, 0))],
        out_specs=[pl.BlockSpec((tq, LANES), lambda p, m: (m, p)),
                   pl.BlockSpec((tq, LANES), lambda p, m: (m, p))],
        out_shape=[jax.ShapeDtypeStruct((T, npair * LANES), BF16),
                   jax.ShapeDtypeStruct((T, npair * LANES), F32)],
        compiler_params=_params(("parallel", "arbitrary")),
    )(q, cos_q, sin_q, kv, kv, kr)


def _mla_bwd(q, cos_q, sin_q, kv, kr, o, lse, do, do_col0, name="mla_bwd"):
    T = q.shape[0]
    tq, tk = _pick(T, ATT_TQ), _pick(T, ATT_TK)
    nd = tq // tk
    npair = MLA_HEADS // 2
    scale = (MLA_NOPE + MLA_ROPE) ** -0.5

    def body(q_ref, c_ref, s_ref, kn_ref, v_ref, kr_ref, o_ref, lse_ref, do_ref, dq_ref, dkn_ref, dv_ref, dkr_ref,
             dkn_acc, dv_acc):
        p_idx, m_idx = pl.program_id(0), pl.program_id(1)

        @pl.when(m_idx == 0)
        def _():
            dkn_acc[...] = jnp.zeros_like(dkn_acc)
            dv_acc[...] = jnp.zeros_like(dv_acc)

        @pl.when((m_idx == 0) & (p_idx == 0))
        def _():
            dkr_ref[...] = jnp.zeros_like(dkr_ref)

        qv = _rotate(q_ref[...], c_ref[...], s_ref[...]).astype(BF16)
        m0, m1 = _mla_masks(qv.shape)
        qs = jnp.concatenate([jnp.where(m0, qv, 0), jnp.where(m1, qv, 0)], axis=0).astype(BF16)
        dov = do_ref[...].astype(F32)
        h0 = lax.broadcasted_iota(jnp.int32, (tq, LANES), 1) < HEAD
        dos32 = jnp.concatenate([jnp.where(h0, dov, 0.0), jnp.where(h0, 0.0, dov)], axis=0)
        ov = o_ref[...].astype(F32)
        delta = jnp.sum(dos32 * jnp.concatenate([ov, ov], axis=0), axis=-1, keepdims=True)
        dos = dos32.astype(BF16)
        lsev = lse_ref[...]
        lse = jnp.concatenate([lsev[:, 0:1], lsev[:, HEAD:HEAD + 1]], axis=0)

        def block(kb, dq, ok):
            ks = pl.ds(pl.multiple_of(kb * tk, tk), tk)
            kcat = jnp.concatenate([kn_ref[ks, :], kr_ref[ks, :]], axis=1)
            vv = v_ref[ks, :]
            p = jnp.exp(_dot(qs, kcat, NT) * scale - lse)
            if ok is not None:
                p = jnp.where(ok, p, 0.0)
            ds = (p * (_dot(dos, vv, NT) - delta) * scale).astype(BF16)
            dkc = _dot(ds, qs, TN)
            dkn_acc[ks, :] += dkc[:, :LANES]
            dkr_ref[ks, :] += dkc[:, LANES:]
            dv_acc[ks, :] += _dot(p.astype(BF16), dos, TN)
            return dq + _dot(ds, kcat, NN)

        dq = jnp.zeros((2 * tq, 2 * LANES), F32)
        for d in range(nd):
            dq = block(m_idx * nd + d, dq, _chunk_ok(tq, tk, d))
        dq = _by_twos(m_idx * nd, lambda kb, c: block(kb, c, None), dq)
        dq_ref[...] = _rotate(jnp.where(m0, dq[:tq], jnp.where(m1, dq[tq:], 0.0)), c_ref[...],
                              -s_ref[...]).astype(BF16)

        @pl.when(m_idx == T // tq - 1)
        def _():
            dkn_ref[...] = dkn_acc[...].astype(BF16)
            dv_ref[...] = dv_acc[...].astype(BF16)

    full = lambda col: pl.BlockSpec((T, LANES), col)
    blk = lambda col: pl.BlockSpec((tq, LANES), col)
    table = pl.BlockSpec((tq, 2 * LANES), lambda p, m: (m, 0))
    return pl.pallas_call(
        body, name=name, grid=(npair, T // tq),
        in_specs=[pl.BlockSpec((tq, 2 * LANES), lambda p, m: (m, p)), table, table,
                  full(lambda p, m: (0, p)), full(lambda p, m: (0, npair + p)), full(lambda p, m: (0, 0)),
                  blk(lambda p, m: (m, p)), blk(lambda p, m: (m, p)),
                  blk(lambda p, m: (m, do_col0 + p))],
        out_specs=[pl.BlockSpec((tq, 2 * LANES), lambda p, m: (m, p)),
                   full(lambda p, m: (0, p)), full(lambda p, m: (0, p)), full(lambda p, m: (0, 0))],
        out_shape=[jax.ShapeDtypeStruct((T, npair * 2 * LANES), BF16),
                   jax.ShapeDtypeStruct((T, npair * LANES), BF16),
                   jax.ShapeDtypeStruct((T, npair * LANES), BF16),
                   jax.ShapeDtypeStruct((T, LANES), F32)],
        scratch_shapes=[pltpu.VMEM((T, LANES), F32)] * 2,
        compiler_params=_params(("arbitrary", "arbitrary")),
    )(q, cos_q, sin_q, kv, kv, kr, o, lse, do)


def _split_dot(x, tri):
    hi = x.astype(BF16)
    lo = (x - hi.astype(F32)).astype(BF16)
    both = _dot(jnp.concatenate([hi, lo], axis=0), tri, NN)
    return both[:x.shape[0]] + both[x.shape[0]:]


def _sb_terms(qh, kk, before):
    z = _dot(qh, kk, NT)
    sp = jnp.maximum(z, 0.0) + jnp.log(1.0 + jnp.exp(-jnp.abs(z)))
    lk = -sp if before is None else jnp.where(before, -sp, 0.0)
    return z, sp, lk


def _sb_setup(q_ref, tq, tk, scale):
    qv = (q_ref[...].astype(F32) * scale).astype(BF16)
    lane = lax.broadcasted_iota(jnp.int32, (tq, LANES), 1)
    h0 = lane < HEAD
    qs = jnp.concatenate([jnp.where(h0, qv, 0), jnp.where(h0, 0, qv)], axis=0).astype(BF16)
    row = lax.broadcasted_iota(jnp.int32, (tk, tk), 0)
    col = lax.broadcasted_iota(jnp.int32, (tk, tk), 1)
    return qs, h0, row, col


def _sb_before(tq, tk, d):
    row = lax.broadcasted_iota(jnp.int32, (tq, tk), 0)
    col = lax.broadcasted_iota(jnp.int32, (tq, tk), 1) + d * tk
    return jnp.concatenate([col < row] * 2, axis=0)


def _two_heads(x, h0):
    tq = x.shape[0] // 2
    return jnp.where(h0, x[:tq], x[tq:])


def _sb_fwd(qkv, col0, name="sb_fwd"):
    T = qkv.shape[0]
    tq, tk = _pick(T, ATT_TQ), _pick(T, ATT_TK)
    nd = tq // tk
    npair = SB_HEADS // 2
    scale = SB_DIM ** -0.5

    def body(q_ref, k_ref, v_ref, o_ref, o32_ref, w_ref):
        m_idx = pl.program_id(1)
        qs, h0, row, col = _sb_setup(q_ref, tq, tk, scale)
        later = (row > col).astype(BF16)

        def block(kb, carry, before):
            ks = pl.ds(pl.multiple_of(kb * tk, tk), tk)
            c, acc = carry
            z, sp, lk = _sb_terms(qs, k_ref[ks, :].astype(BF16), before)
            w = jnp.exp((z - sp) + _split_dot(lk, later) + c)
            if before is not None:
                w = jnp.where(before, w, 0.0)
            wb = w.astype(BF16)
            w_ref[0, 0, kb] = wb
            return (c + jnp.sum(lk, axis=-1, keepdims=True), acc + _dot(wb, v_ref[ks, :].astype(BF16), NN))

        init = (jnp.zeros((2 * tq, 1), F32), jnp.zeros((2 * tq, LANES), F32))
        res = init
        for d in reversed(range(nd)):
            res = block(m_idx * nd + d, res, _sb_before(tq, tk, d))
        res = _by_twos(m_idx * nd, lambda i, c: block(m_idx * nd - 1 - i, c, None), res)
        o = _two_heads(res[1], h0)
        o_ref[...] = o.astype(o_ref.dtype)
        o32_ref[...] = o

    full = lambda col: pl.BlockSpec((T, LANES), col)
    blk = pl.BlockSpec((tq, LANES), lambda p, m: (m, p))
    return pl.pallas_call(
        body, name=name, grid=(npair, T // tq),
        in_specs=[pl.BlockSpec((tq, LANES), lambda p, m: (m, col0 + p)),
                  full(lambda p, m: (0, col0 + npair + p)), full(lambda p, m: (0, col0 + 2 * npair + p))],
        out_specs=[blk, blk, pl.BlockSpec((1, 1, T // tk, 2 * tq, tk), lambda p, m: (p, m, 0, 0, 0))],
        out_shape=[jax.ShapeDtypeStruct((T, npair * LANES), BF16), jax.ShapeDtypeStruct((T, npair * LANES), F32),
                   jax.ShapeDtypeStruct((npair, T // tq, T // tk, 2 * tq, tk), BF16)],
        compiler_params=_params(("parallel", "arbitrary")),
    )(qkv, qkv, qkv)


def _sb_bwd(qkv, col0, o32, w_all, do, do_col0, dep, name="sb_bwd"):
    T = qkv.shape[0]
    tq, tk = _pick(T, ATT_TQ), _pick(T, ATT_TK)
    nd = tq // tk
    npair = SB_HEADS // 2
    scale = SB_DIM ** -0.5

    def body(q_ref, k_ref, v_ref, o_ref, w_ref, do_ref, dep_ref, dq_ref, dk_ref, dv_ref, dk_acc, dv_acc):
        m_idx = pl.program_id(1)

        @pl.when(m_idx == 0)
        def _():
            dk_acc[...] = jnp.zeros_like(dk_acc)
            dv_acc[...] = jnp.zeros_like(dv_acc)

        qs, h0, row, col = _sb_setup(q_ref, tq, tk, scale)
        dov = do_ref[...].astype(F32)
        dos = jnp.concatenate([jnp.where(h0, dov, 0.0), jnp.where(h0, 0.0, dov)], axis=0).astype(BF16)
        ov = o_ref[...]
        etot = jnp.sum(dos.astype(F32) * jnp.concatenate([ov, ov], axis=0), axis=-1, keepdims=True)
        from_here = (row >= col).astype(BF16)

        def block(kb, carry, before):
            ks = pl.ds(pl.multiple_of(kb * tk, tk), tk)
            kk = k_ref[ks, :].astype(BF16)
            vv = v_ref[ks, :].astype(BF16)
            es, dqa = carry
            _, sp, _ = _sb_terms(qs, kk, None)
            wb = w_ref[0, 0, kb]
            e = wb.astype(F32) * _dot(dos, vv, NT)
            prev = etot - (_split_dot(e, from_here) + es)
            sig_neg = jnp.exp(-sp)
            dz = e * sig_neg - (1.0 - sig_neg) * prev
            if before is not None:
                dz = jnp.where(before, dz, 0.0)
            dzb = dz.astype(BF16)
            dk_acc[ks, :] += _dot(dzb, qs, TN)
            dv_acc[ks, :] += _dot(wb, dos, TN)
            return es + jnp.sum(e, axis=-1, keepdims=True), dqa + _dot(dzb, kk, NN)

        init = (jnp.zeros((2 * tq, 1), F32), jnp.zeros((2 * tq, LANES), F32))
        res = init
        for d in reversed(range(nd)):
            res = block(m_idx * nd + d, res, _sb_before(tq, tk, d))
        res = _by_twos(m_idx * nd, lambda i, c: block(m_idx * nd - 1 - i, c, None), res)
        dq_ref[...] = (_two_heads(res[1], h0) * scale).astype(BF16)

        @pl.when(m_idx == T // tq - 1)
        def _():
            dk_ref[...] = dk_acc[...].astype(BF16)
            dv_ref[...] = dv_acc[...].astype(BF16)

    full = lambda col: pl.BlockSpec((T, LANES), col)
    blk = lambda col: pl.BlockSpec((tq, LANES), col)
    return pl.pallas_call(
        body, name=name, grid=(npair, T // tq),
        in_specs=[blk(lambda p, m: (m, col0 + p)),
                  full(lambda p, m: (0, col0 + npair + p)), full(lambda p, m: (0, col0 + 2 * npair + p)),
                  blk(lambda p, m: (m, p)),
                  pl.BlockSpec((1, 1, T // tk, 2 * tq, tk), lambda p, m: (p, m, 0, 0, 0)),
                  blk(lambda p, m: (m, do_col0 + p)), pl.BlockSpec((8, LANES), lambda p, m: (0, 0))],
        out_specs=[blk(lambda p, m: (m, p)), full(lambda p, m: (0, p)), full(lambda p, m: (0, p))],
        out_shape=[jax.ShapeDtypeStruct((T, npair * LANES), BF16)] * 3,
        scratch_shapes=[pltpu.VMEM((T, LANES), F32)] * 2,
        compiler_params=_params(("arbitrary", "arbitrary")),
    )(qkv, qkv, qkv, o32, w_all, do, dep)


def _band_in_window():
    cq = lax.broadcasted_iota(jnp.int32, (BAND_TQ, BAND_W), 0) >> CHUNK_BITS
    ckp = lax.broadcasted_iota(jnp.int32, (BAND_TQ, BAND_W), 1) >> CHUNK_BITS
    return (ckp >= cq) & (ckp <= cq + LEFT_CHUNKS)


def _band_real(m_idx):
    j = lax.broadcasted_iota(jnp.int32, (BAND_TQ, BAND_W), 1)
    return j >= PAD_KEYS - m_idx * BAND_TQ


def _band_probs(qh, kw, bias, real, scale):
    s = jnp.where(real, _dot(qh, kw, NT) * scale + bias, NEG)
    e = jnp.exp(s - jnp.max(s, axis=-1, keepdims=True))
    return e * (1.0 / jnp.sum(e, axis=-1, keepdims=True))


BAND_SUB = 4


def _band_fwd(qkv, k_pad, v_pad, bias_w, name="band_fwd"):
    T = qkv.shape[0]
    npair = C_HEADS // 2
    scale = C_DIM ** -0.5
    rows = BAND_SUB * BAND_TQ

    def body(q_ref, k_ref, v_ref, b_ref, o_ref, p_ref):
        lane = lax.broadcasted_iota(jnp.int32, (BAND_TQ, LANES), 1)
        h0 = lane < HEAD
        bias = jnp.concatenate([b_ref[0], b_ref[1]], axis=0)
        for sub in range(BAND_SUB):
            m_idx = pl.program_id(1) * BAND_SUB + sub
            win = pl.ds(pl.multiple_of(m_idx * BAND_TQ, BAND_TQ), BAND_W)
            kw, vw = k_ref[win, :], v_ref[win, :]
            qv = q_ref[sub * BAND_TQ:(sub + 1) * BAND_TQ, :]
            qs = jnp.concatenate([jnp.where(h0, qv, 0), jnp.where(h0, 0, qv)], axis=0).astype(BF16)
            p = _band_probs(qs, kw, bias, jnp.concatenate([_band_real(m_idx)] * 2, axis=0), scale).astype(BF16)
            p_ref[0, sub] = p
            o = _two_heads(_dot(p, vw, NN), h0)
            o_ref[sub * BAND_TQ:(sub + 1) * BAND_TQ, :] = o.astype(o_ref.dtype)

    Tp = T + PAD_KEYS
    return pl.pallas_call(
        body, name=name, grid=(npair, T // rows),
        in_specs=[pl.BlockSpec((rows, LANES), lambda p, m: (m, p)),
                  pl.BlockSpec((Tp, LANES), lambda p, m: (0, p)),
                  pl.BlockSpec((Tp, LANES), lambda p, m: (0, p)),
                  pl.BlockSpec((2, BAND_TQ, BAND_W), lambda p, m: (p, 0, 0))],
        out_specs=[pl.BlockSpec((rows, LANES), lambda p, m: (m, p)),
                   pl.BlockSpec((1, BAND_SUB, 2 * BAND_TQ, BAND_W), lambda p, m: (p, m, 0, 0))],
        out_shape=[jax.ShapeDtypeStruct((T, npair * LANES), BF16),
                   jax.ShapeDtypeStruct((npair, T // BAND_TQ, 2 * BAND_TQ, BAND_W), BF16)],
        compiler_params=_params(("parallel", "arbitrary")),
    )(qkv, k_pad, v_pad, bias_w)


def _band_bwd(qkv, k_pad, v_pad, probs, do, name="band_bwd"):
    T = qkv.shape[0]
    npair = C_HEADS // 2
    scale = C_DIM ** -0.5

    rows = BAND_SUB * BAND_TQ

    def body(q_ref, k_ref, v_ref, p_ref, do_ref, dq_ref, dk_ref, dv_ref, db_ref, dk_acc, dv_acc):
        @pl.when(pl.program_id(1) == 0)
        def _():
            dk_acc[...] = jnp.zeros_like(dk_acc)
            dv_acc[...] = jnp.zeros_like(dv_acc)
            db_ref[...] = jnp.zeros_like(db_ref)

        lane = lax.broadcasted_iota(jnp.int32, (BAND_TQ, LANES), 1)
        h0 = lane < HEAD
        dbs = jnp.zeros((2 * BAND_TQ, BAND_W), F32)
        for sub in range(BAND_SUB):
            m_idx = pl.program_id(1) * BAND_SUB + sub
            win = pl.ds(pl.multiple_of(m_idx * BAND_TQ, BAND_TQ), BAND_W)
            kw, vw = k_ref[win, :], v_ref[win, :]
            qv = q_ref[sub * BAND_TQ:(sub + 1) * BAND_TQ, :]
            dov = do_ref[sub * BAND_TQ:(sub + 1) * BAND_TQ, :].astype(F32)
            qs = jnp.concatenate([jnp.where(h0, qv, 0), jnp.where(h0, 0, qv)], axis=0).astype(BF16)
            dos = jnp.concatenate([jnp.where(h0, dov, 0.0), jnp.where(h0, 0.0, dov)], axis=0).astype(BF16)
            pb = p_ref[0, sub]
            p = pb.astype(F32)
            dp = _dot(dos, vw, NT)
            dsb = p * (dp - jnp.sum(p * dp, axis=-1, keepdims=True))
            dbs = dbs + dsb
            dsq = (dsb * scale).astype(BF16)
            dq_ref[sub * BAND_TQ:(sub + 1) * BAND_TQ, :] = _two_heads(_dot(dsq, kw, NN), h0).astype(BF16)
            dk_acc[win, :] += _dot(dsq, qs, TN)
            dv_acc[win, :] += _dot(pb, dos, TN)
        db_ref[0] += dbs[:BAND_TQ]
        db_ref[1] += dbs[BAND_TQ:]

        @pl.when(pl.program_id(1) == T // rows - 1)
        def _():
            dk_ref[...] = dk_acc[...].astype(BF16)
            dv_ref[...] = dv_acc[...].astype(BF16)

    Tp = T + PAD_KEYS
    blk = lambda col: pl.BlockSpec((rows, LANES), col)
    full = pl.BlockSpec((Tp, LANES), lambda p, m: (0, p))
    bias = pl.BlockSpec((2, BAND_TQ, BAND_W), lambda p, m: (p, 0, 0))
    prob = pl.BlockSpec((1, BAND_SUB, 2 * BAND_TQ, BAND_W), lambda p, m: (p, m, 0, 0))
    return pl.pallas_call(
        body, name=name, grid=(npair, T // rows),
        in_specs=[blk(lambda p, m: (m, p)), full, full, prob, blk(lambda p, m: (m, p))],
        out_specs=[blk(lambda p, m: (m, p)), full, full, bias],
        out_shape=[jax.ShapeDtypeStruct((T, npair * LANES), BF16),
                   jax.ShapeDtypeStruct((Tp, npair * LANES), BF16),
                   jax.ShapeDtypeStruct((Tp, npair * LANES), BF16),
                   jax.ShapeDtypeStruct((C_HEADS, BAND_TQ, BAND_W), F32)],
        scratch_shapes=[pltpu.VMEM((Tp, LANES), F32)] * 2,
        compiler_params=_params(("arbitrary", "arbitrary")),
    )(qkv, k_pad, v_pad, probs, do)


def _skew_bits(x, left):
    w = x.shape[1]
    row = lax.broadcasted_iota(jnp.int32, x.shape, 0)
    for b in range(BAND_TQ.bit_length() - 1):
        amt = (w - (1 << b)) if left else (1 << b)
        x = jnp.where((row >> b) & 1 == 1, pltpu.roll(x, amt, 1), x)
    return x


def _toeplitz(diag, name="toeplitz"):
    H = diag.shape[0]

    def body(d_ref, o_ref):
        x = jnp.broadcast_to(d_ref[0], (BAND_TQ, TOEP_W))
        o_ref[0] = jnp.where(_band_in_window(), _skew_bits(x, left=False)[:, BAND_TQ:], NEG)

    return pl.pallas_call(
        body, name=name, grid=(H,),
        in_specs=[pl.BlockSpec((1, 1, TOEP_W), lambda h: (h, 0, 0))],
        out_specs=pl.BlockSpec((1, BAND_TQ, BAND_W), lambda h: (h, 0, 0)),
        out_shape=jax.ShapeDtypeStruct((H, BAND_TQ, BAND_W), F32),
        compiler_params=_params(("parallel",)),
    )(diag.reshape(H, 1, TOEP_W))


def _toeplitz_bwd(dbias, name="toeplitz_bwd"):
    H = dbias.shape[0]

    def body(d_ref, o_ref):
        x = jnp.concatenate([jnp.zeros((BAND_TQ, BAND_TQ), F32), d_ref[0]], axis=1)
        o_ref[0] = jnp.sum(_skew_bits(x, left=True), axis=0, keepdims=True)

    return pl.pallas_call(
        body, name=name, grid=(H,),
        in_specs=[pl.BlockSpec((1, BAND_TQ, BAND_W), lambda h: (h, 0, 0))],
        out_specs=pl.BlockSpec((1, 1, TOEP_W), lambda h: (h, 0, 0)),
        out_shape=jax.ShapeDtypeStruct((H, 1, TOEP_W), F32),
        compiler_params=_params(("parallel",)),
    )(dbias).reshape(H, TOEP_W)


_HBM = pl.BlockSpec(memory_space=pltpu.HBM)
_SEM = pl.BlockSpec(memory_space=pltpu.SEMAPHORE)
_EFFECT = pltpu.SideEffectType.DATAFLOW_SIDE_EFFECTING


def _peers():
    x, y, c = lax.axis_index("x"), lax.axis_index("y"), lax.axis_index("c")
    out = []
    for k in range(1, N_DEV):
        peer = (1 - x if (k >> 2) & 1 else x, 1 - y if (k >> 1) & 1 else y, 1 - c if k & 1 else c)
        out.append((peer, 4 * peer[0] + 2 * peer[1] + peer[2]))
    return 4 * x + 2 * y + c, out


def _split_copies(ins, lands, scatter, send_sem, recv_sem, arriving):
    me, peers = _peers()
    out = []
    for a in range(len(ins)):
        for peer, idx in peers:
            out.append(pltpu.make_async_remote_copy(
                src_ref=ins[a].at[idx] if scatter[a] else ins[a],
                dst_ref=lands[a].at[idx if arriving else me], send_sem=send_sem, recv_sem=recv_sem,
                device_id=peer, device_id_type=pl.DeviceIdType.MESH))
    return out


def _landing_zones(arrays, scatter):
    return [lax.empty((N_DEV,) + (a.shape[1:] if s else a.shape), a.dtype) for a, s in zip(arrays, scatter)]


def _place_own(arrays, scatter, name):
    n = len(arrays)
    lands = _landing_zones(arrays, scatter)
    me = (4 * lax.axis_index("x") + 2 * lax.axis_index("y") + lax.axis_index("c")).astype(jnp.int32).reshape(1)

    def body(me_ref, *refs):
        for a in range(n):
            refs[2 * n + a][...] = refs[a][...].reshape(refs[2 * n + a].shape)

    def row_spec(shape):
        zeros = (0,) * (len(shape) - 1)
        return pl.BlockSpec((1,) + tuple(shape[1:]), lambda i, me_ref: (me_ref[0],) + zeros)

    in_specs = [row_spec(a.shape) if s else pl.BlockSpec(a.shape, lambda i, me_ref, nd=a.ndim: (0,) * nd)
                for a, s in zip(arrays, scatter)]
    return pl.pallas_call(
        body, name=name,
        out_shape=[jax.ShapeDtypeStruct(l.shape, l.dtype) for l in lands],
        grid_spec=pltpu.PrefetchScalarGridSpec(
            num_scalar_prefetch=1, grid=(1,),
            in_specs=in_specs + [pl.BlockSpec(memory_space=pl.ANY)] * n,
            out_specs=[row_spec(l.shape) for l in lands]),
        input_output_aliases={1 + n + i: i for i in range(n)},
        compiler_params=_params(("arbitrary",)),
    )(me, *arrays, *lands)


def _exchange_start(arrays, scatter, after, name, lands=None):
    n = len(arrays)
    if lands is None:
        lands = list(_place_own(arrays, scatter, name=name.replace("_start_", "_own_")))

    def body(*refs):
        ins, lnd = refs[:n], refs[n:2 * n]
        send_sem, recv_sem = refs[2 * n + 1:2 * n + 3]
        token = refs[-1]
        for cp in _split_copies(ins, lnd, scatter, send_sem, recv_sem, arriving=False):
            cp.start()
        token[...] = jnp.zeros_like(token)

    hbm = lambda a: pltpu.HBM(a.shape, a.dtype)
    out = pl.pallas_call(
        body, name=name,
        out_shape=(pltpu.SemaphoreType.DMA(()), pltpu.SemaphoreType.DMA(()),
                   *[hbm(a) for a in arrays], *[hbm(a) for a in lands],
                   jax.ShapeDtypeStruct((8, LANES), F32)),
        in_specs=[_HBM] * (2 * n) + [pl.BlockSpec(memory_space=pl.ANY)],
        out_specs=(_SEM, _SEM, *([_HBM] * (2 * n)), pl.BlockSpec(memory_space=pltpu.VMEM)),
        input_output_aliases={i: 2 + i for i in range(2 * n)},
        compiler_params=pltpu.CompilerParams(has_side_effects=_EFFECT),
    )(*[pltpu.with_memory_space_constraint(a, pltpu.HBM) for a in list(arrays) + lands], after)
    return (out[0], out[1], list(out[2:2 + n]), list(out[2 + n:2 + 2 * n]), tuple(scatter)), out[-1]


def _exchange_wait(handle, after, name):
    send_sem, recv_sem, ins, lands, scatter = handle
    n = len(ins)
    after = after if isinstance(after, tuple) else (after,)

    def body(*refs):
        i_ref, l_ref = refs[:n], refs[n:2 * n]
        s_sem, r_sem = refs[2 * n:2 * n + 2]
        for cp in _split_copies(i_ref, l_ref, scatter, s_sem, r_sem, arriving=False):
            cp.wait_send()
        for cp in _split_copies(i_ref, l_ref, scatter, s_sem, r_sem, arriving=True):
            cp.wait_recv()

    hbm = lambda a: pltpu.HBM(a.shape, a.dtype)
    out = pl.pallas_call(
        body, name=name,
        out_shape=tuple(hbm(a) for a in ins + lands),
        in_specs=[_HBM] * (2 * n) + [_SEM, _SEM] + [pl.BlockSpec(memory_space=pl.ANY)] * len(after),
        out_specs=tuple([_HBM] * (2 * n)),
        input_output_aliases={i: i for i in range(2 * n)},
        compiler_params=pltpu.CompilerParams(has_side_effects=_EFFECT),
    )(*ins, *lands, send_sem, recv_sem, *after)
    return list(out[n:])


def _adamw(w, parts, m, v, name="adamw"):
    R, C = w.shape
    L = len(parts)
    rl = R // L
    tr = max([t for t in range(16, 513, 16) if rl % t == 0], default=rl)
    nb = rl // tr
    c1 = 1.0 - ADAM_B1 ** ADAM_STEP
    c2 = 1.0 - ADAM_B2 ** ADAM_STEP

    def body(*refs):
        w_ref, p_refs, (m_ref, v_ref, g_ref, d_ref, nm_ref, nv_ref) = refs[0], refs[1:1 + L], refs[1 + L:]
        g = None
        for j, p_ref in enumerate(p_refs):
            gj = p_ref[0].astype(F32)
            for i in range(1, N_DEV):
                gj = gj + p_ref[i].astype(F32)
            g = gj if g is None else jnp.where(pl.program_id(0) == j, gj, g)
        nm = ADAM_B1 * m_ref[...] + (1.0 - ADAM_B1) * g
        nv = ADAM_B2 * v_ref[...] + (1.0 - ADAM_B2) * (g * g)
        g_ref[...] = g
        nm_ref[...] = nm
        nv_ref[...] = nv
        d_ref[...] = -ADAM_LR * ((nm / c1) / (jnp.sqrt(nv / c2) + ADAM_EPS) + ADAM_WD * w_ref[...])

    blk = pl.BlockSpec((tr, C), lambda l, i: (l * nb + i, 0))
    part = lambda j: pl.BlockSpec((N_DEV, tr, C), lambda l, i: (0, jnp.where(l == j, i, 0), 0))
    return pl.pallas_call(
        body, name=name, grid=(L, nb),
        in_specs=[blk] + [part(j) for j in range(L)] + [blk, blk],
        out_specs=[blk] * 4,
        out_shape=[jax.ShapeDtypeStruct((R, C), F32)] * 4,
        compiler_params=_params(("arbitrary", "arbitrary")),
    )(w, *parts, m, v)


_O1 = Q_LORA
_O2 = _O1 + KV_LORA
_O3 = _O2 + MLA_ROPE
_NB = SB_HEADS * SB_DIM
IN_W = _O2 + LANES + 3 * _NB
COL_KR = _O2 // LANES
COL_SB = COL_KR + 1


def _w_in_local(w):
    kr = w[_O2:_O3]
    pad = jnp.zeros((LANES - 2 * MLA_ROPE, w.shape[1]), w.dtype)
    return jnp.concatenate([w[:_O2], kr, kr, pad, w[_O3:]], axis=0)


def _w_in_grad(g):
    kr = (g[_O2:_O2 + MLA_ROPE].astype(F32) + g[_O2 + MLA_ROPE:_O2 + 2 * MLA_ROPE].astype(F32)).astype(g.dtype)
    return jnp.concatenate([g[:_O2], kr, g[_O2 + LANES:]], axis=0)


def _w_uq_local(w):
    w3 = w.reshape(MLA_HEADS // 2, 2, MLA_NOPE + MLA_ROPE, w.shape[1])
    nope = w3[:, :, :MLA_NOPE].reshape(MLA_HEADS // 2, 2 * MLA_NOPE, w.shape[1])
    rope = w3[:, :, MLA_NOPE:].reshape(MLA_HEADS // 2, 2 * MLA_ROPE, w.shape[1])
    pad = jnp.zeros((MLA_HEADS // 2, LANES - 2 * MLA_ROPE, w.shape[1]), w.dtype)
    return jnp.concatenate([nope, rope, pad], axis=1).reshape(-1, w.shape[1])


def _w_uq_grad(g):
    g3 = g.reshape(MLA_HEADS // 2, 2 * LANES, g.shape[1])
    nope = g3[:, :2 * MLA_NOPE].reshape(MLA_HEADS // 2, 2, MLA_NOPE, g.shape[1])
    rope = g3[:, LANES:LANES + 2 * MLA_ROPE].reshape(MLA_HEADS // 2, 2, MLA_ROPE, g.shape[1])
    return jnp.concatenate([nope, rope], axis=2).reshape(-1, g.shape[1])


def _w_ukv_local(w):
    w3 = w.reshape(MLA_HEADS, MLA_NOPE + MLA_V, w.shape[1])
    return jnp.concatenate([w3[:, :MLA_NOPE].reshape(-1, w.shape[1]),
                            w3[:, MLA_NOPE:].reshape(-1, w.shape[1])], axis=0)


def _w_ukv_grad(g):
    half = MLA_HEADS * MLA_NOPE
    kn = g[:half].reshape(MLA_HEADS, MLA_NOPE, g.shape[1])
    vv = g[half:].reshape(MLA_HEADS, MLA_V, g.shape[1])
    return jnp.concatenate([kn, vv], axis=1).reshape(-1, g.shape[1])


def _rope_tables(T):
    pos = jnp.arange(T, dtype=F32)
    inv_freq = ROPE_THETA ** (-jnp.arange(0, MLA_ROPE, 2, dtype=F32) / MLA_ROPE)
    ang = pos[:, None] * inv_freq[None, :]
    cos, sin = jnp.cos(ang), jnp.sin(ang)
    ones = jnp.ones((T, LANES - 2 * MLA_ROPE), F32)
    cos_k = jnp.concatenate([cos, cos, cos, cos, ones], axis=1)
    sin_k = jnp.concatenate([-sin, sin, -sin, sin, 0.0 * ones], axis=1)
    cos_q = jnp.concatenate([jnp.ones((T, LANES), F32), cos_k], axis=1)
    sin_q = jnp.concatenate([jnp.zeros((T, LANES), F32), sin_k], axis=1)
    return cos_q, sin_q, cos_k, sin_k


def _bias_diag_index():
    ell = np.arange(TOEP_W)
    return np.clip(BAND_W - ell, -REL_CLIP, REL_CLIP) + REL_CLIP


def _local_step(x, target, small, get_weights, put_grads):
    T = x.shape[0]
    cos_q, sin_q, cos_k, sin_k = _rope_tables(T)
    G = {}
    W = dict(small)

    u0 = _rms_fwd(x, W["g_mix"][0:1], name="rms_mix0")
    bias_w = _toeplitz(W["od_rel_bias"][:, _bias_diag_index()])
    W.update(get_weights("in0", (u0, bias_w)))
    proj = _mm(u0, W["w_in_t"], dims="nt", name="proj_in")
    W.update(get_weights("mix0", proj))
    c_q, c_kv = proj[:, :_O1], proj[:, _O1:_O2]
    nq = _rms_fwd(c_q, W["g_cq"], name="rms_cq")
    nkv = _rms_fwd(c_kv, W["g_ckv"], name="rms_ckv")
    qa_raw = _mm(nq, W["w_uq_t"], dims="nt", name="proj_uq")
    kv = _mm(nkv, W["w_ukv_t"], dims="nt", out_dtype=BF16, name="proj_ukv")
    kr = _rope(proj, cos_k, sin_k, COL_KR, 1, BF16, name="rope_k")
    o_a, lse = _mla_fwd(qa_raw, cos_q, sin_q, kv, kr)
    o_b, o_b32, w_b = _sb_fwd(proj, COL_SB)
    o_ab = jnp.concatenate([o_a, o_b], axis=1)
    h1 = _mm(o_ab, W["ev_w_out"], res=x, name="out_ev")

    def ffn_fwd(h, layer):
        W.update(get_weights(f"ffn{layer}", h))
        return _ffn_fwd(h, W["g_ffn"][layer:layer + 1], W[f"w_gate_t{layer}"], W[f"w_up_t{layer}"],
                        W[f"w_down{layer}"], name=f"ffn_fwd{layer}")

    h2, u1, a0, b0 = ffn_fwd(h1, 0)

    W.update(get_weights("mix1", h2))
    u2 = _rms_fwd(h2, W["g_mix"][1:2], name="rms_mix1")
    qkv = _mm(u2, W["od_w_qkv_t"], dims="nt", out_dtype=BF16, name="proj_qkv")
    nc = C_HEADS * C_DIM
    pad = ((PAD_KEYS, 0), (0, 0))
    k_pad, v_pad = jnp.pad(qkv[:, nc:2 * nc], pad), jnp.pad(qkv[:, 2 * nc:], pad)
    o_c, p_c = _band_fwd(qkv, k_pad, v_pad, bias_w)
    h3 = _mm(o_c, W["od_w_out"], res=h2, name="out_od")
    h4, u3, a1, b1 = ffn_fwd(h3, 1)

    loss, dh, dhb, G["g_final"] = _loss_head(h4, W["g_final"], target)

    def ffn_bwd(dh, dhb, h, u, a, b, layer):
        du, g_gate, g_up, g_down = _ffn_bwd(dhb, u, a, b, W[f"w_gate_t{layer}"], W[f"w_up_t{layer}"],
                                            W[f"w_down{layer}"], name=f"ffn_bwd{layer}")
        tok = put_grads(f"ffn{layer}", {"w_gate_t": g_gate, "w_up_t": g_up, "w_down": g_down})
        return _rms_bwd(h, W["g_ffn"][layer:layer + 1] + tok[:1, :1], du, dres=dh, name=f"rms_ffn_bwd{layer}")

    dh3, dh3b, g_gffn1 = ffn_bwd(dh, dhb, h3, u3, a1, b1, 1)

    do_c = _mm(dh3b, W["od_w_out"], dims="nt", name="out_od_dx")
    g_od_out = _mm(o_c, dh3b, dims="tn", out_dtype=BF16, name="out_od_dw")
    dq_c, dk_p, dv_p, dbias_w = _band_bwd(qkv, k_pad, v_pad, p_c, do_c)
    dqkv = jnp.concatenate([dq_c, dk_p[PAD_KEYS:], dv_p[PAD_KEYS:]], axis=1)
    tok = put_grads("mix1", {"od_w_qkv_t": _mm(dqkv, u2, dims="tn", out_dtype=BF16, name="proj_qkv_dw"),
                             "od_w_out": g_od_out})
    ddiag = _toeplitz_bwd(dbias_w)
    n_far = BAND_W - REL_CLIP + 1
    G["od_rel_bias"] = jnp.concatenate(
        [jnp.zeros((C_HEADS, REL_CLIP - BAND_TQ + 1), F32), ddiag[:, n_far:][:, ::-1],
         jnp.sum(ddiag[:, :n_far], axis=1, keepdims=True)], axis=1)
    dh2, dh2b, g_gmix1 = _mm_rms_bwd(dqkv, W["od_w_qkv_t"], h2, W["g_mix"][1:2] + tok[:1, :1], dh3,
                                     name="proj_qkv_dx")

    dh1, dh1b, g_gffn0 = ffn_bwd(dh2, dh2b, h1, u1, a0, b0, 0)
    G["g_ffn"] = jnp.concatenate([g_gffn0, g_gffn1], axis=0)

    do_ab = _mm(dh1b, W["ev_w_out"], dims="nt", name="out_ev_dx")
    g0 = {"ev_w_out": _mm(o_ab, dh1b, dims="tn", out_dtype=BF16, name="out_ev_dw")}
    dqa_raw, dkn, dva, dkr = _mla_bwd(qa_raw, cos_q, sin_q, kv, kr, o_a, lse, do_ab, 0)
    g0["w_uq_t"] = _mm(dqa_raw, nq, dims="tn", name="proj_uq_dw")
    dnq = _mm(dqa_raw, W["w_uq_t"], name="proj_uq_dx")
    _, dc_q, G["g_cq"] = _rms_bwd(c_q, W["g_cq"], dnq, name="rms_cq_bwd")
    dkv = jnp.concatenate([dkn, dva], axis=1)
    g0["w_ukv_t"] = _mm(dkv, nkv, dims="tn", name="proj_ukv_dw")
    dnkv = _mm(dkv, W["w_ukv_t"], name="proj_ukv_dx")
    _, dc_kv, G["g_ckv"] = _rms_bwd(c_kv, W["g_ckv"], dnkv, name="rms_ckv_bwd")
    tok = put_grads("mix0", g0)
    dqb, dkb, dvb = _sb_bwd(proj, COL_SB, o_b32, w_b, do_ab, MLA_HEADS // 2, tok)
    dkr_raw = _rope(dkr, cos_k, -sin_k, 0, 1, BF16, name="rope_k_bwd")
    dproj = jnp.concatenate([dc_q, dc_kv, dkr_raw, dqb, dkb, dvb], axis=1)
    tok = put_grads("in0", {"w_in_t": _mm(dproj, u0, dims="tn", name="proj_in_dw")})
    dx, _, g_gmix0 = _mm_rms_bwd(dproj, W["w_in_t"], x, W["g_mix"][0:1] + tok[:1, :1], dh1, name="proj_in_dx")
    G["g_mix"] = jnp.concatenate([g_gmix0, g_gmix1], axis=0)
    return loss[0, 0], dx, G


_BIG = ["ev_w_in", "ev_w_uq", "ev_w_ukv", "ev_w_out", "od_w_qkv", "od_w_out", "w_gate", "w_up", "w_down"]
_COL_SHARDED = {"ev_w_in", "ev_w_uq", "ev_w_ukv", "od_w_qkv", "w_gate", "w_up"}
_SMALL = ["ev_g_cq", "ev_g_ckv", "od_rel_bias", "g_mix", "g_ffn", "g_final"]
_GROUPS = {
    "in0": ["ev_w_in"],
    "mix0": ["ev_w_uq", "ev_w_ukv", "ev_w_out"],
    "ffn0": ["w_gate0", "w_up0", "w_down0"],
    "mix1": ["od_w_qkv", "od_w_out"],
    "ffn1": ["w_gate1", "w_up1", "w_down1"],
}
_GROUP_SRC = {n + str(l): (n, l) for n in ("w_gate", "w_up", "w_down") for l in (0, 1)}
_BATCHES = {"in0": ["in0"], "layer0": ["mix0", "ffn0"], "layer1": ["mix1", "ffn1"]}
_BATCH_OF = {grp: batch for batch, grps in _BATCHES.items() for grp in grps}
_SMALL_ROWS = 8
_SMALL_COLS = 1792


def _pack_small(vals):
    flat = jnp.concatenate([v.reshape(-1).astype(F32) for v in vals])
    flat = jnp.pad(flat, (0, _SMALL_ROWS * _SMALL_COLS - flat.shape[0]))
    return flat.reshape(_SMALL_ROWS, _SMALL_COLS)


def _unpack_small(packed, like):
    flat = packed.reshape(-1)
    out, off = [], 0
    for v in like:
        out.append(flat[off:off + v.size].reshape(v.shape))
        off += v.size
    return out


def kernel(x, ev_w_in, ev_g_cq, ev_w_uq, ev_g_ckv, ev_w_ukv, ev_w_out, od_w_qkv, od_rel_bias, od_w_out, g_mix, g_ffn, w_gate, w_up, w_down, g_final, loss_target, m_ev_w_in, m_ev_g_cq, m_ev_w_uq, m_ev_g_ckv, m_ev_w_ukv, m_ev_w_out, m_od_w_qkv, m_od_rel_bias, m_od_w_out, m_g_mix, m_g_ffn, m_w_gate, m_w_up, m_w_down, m_g_final, v_ev_w_in, v_ev_g_cq, v_ev_w_uq, v_ev_g_ckv, v_ev_w_ukv, v_ev_w_out, v_od_w_qkv, v_od_rel_bias, v_od_w_out, v_g_mix, v_g_ffn, v_w_gate, v_w_up, v_w_down, v_g_final):
    args = dict(locals())
    w = {n: args[n] for n in _BIG + _SMALL}
    mom = {n: args["m_" + n] for n in _BIG + _SMALL}
    var = {n: args["v_" + n] for n in _BIG + _SMALL}

    own = {}
    for grp, names in _GROUPS.items():
        for n in names:
            base, layer = _GROUP_SRC.get(n, (n, 0))
            shard = w[base][layer:layer + 1]
            own[n] = (jnp.swapaxes(shard, 1, 2) if base in _COL_SHARDED else shard).astype(BF16)
    placed = dict(zip(own, _place_own(list(own.values()), [False] * len(own), name="gather_own")))
    gather, token = {}, x[0, :8, :LANES]
    for grp, names in _GROUPS.items():
        gather[grp], token = _exchange_start([own[n] for n in names], [False] * len(names), token,
                                             name="gather_start_" + grp, lands=[placed[n] for n in names])

    def get_weights(grp, after):
        names = _GROUPS[grp]
        lands = _exchange_wait(gather[grp], token if after is None else after, name="gather_wait_" + grp)
        full = {n: l.reshape(-1, l.shape[-1]) for n, l in zip(names, lands)}
        if grp == "in0":
            return {"w_in_t": _w_in_local(full["ev_w_in"])}
        if grp == "mix0":
            return {"w_uq_t": _w_uq_local(full["ev_w_uq"]), "w_ukv_t": _w_ukv_local(full["ev_w_ukv"]),
                    "ev_w_out": full["ev_w_out"]}
        if grp == "mix1":
            return {"od_w_qkv_t": full["od_w_qkv"], "od_w_out": full["od_w_out"]}
        layer = grp[-1]
        return {"w_gate_t" + layer: full["w_gate" + layer], "w_up_t" + layer: full["w_up" + layer],
                "w_down" + layer: full["w_down" + layer]}

    scatter, pending = {}, {}

    def put_grads(grp, g):
        if grp == "in0":
            g = {"ev_w_in": _w_in_grad(g["w_in_t"])}
        elif grp == "mix0":
            g = {"ev_w_uq": _w_uq_grad(g["w_uq_t"]), "ev_w_ukv": _w_ukv_grad(g["w_ukv_t"]),
                 "ev_w_out": g["ev_w_out"]}
        elif grp == "mix1":
            g = {"od_w_qkv": g["od_w_qkv_t"], "od_w_out": g["od_w_out"]}
        else:
            layer = grp[-1]
            g = {"w_gate" + layer: g["w_gate_t"], "w_up" + layer: g["w_up_t"], "w_down" + layer: g["w_down"]}
        pending.update({n: v.reshape(N_DEV, 1, v.shape[0] // N_DEV, v.shape[1]).astype(BF16) for n, v in g.items()})
        batch = _BATCH_OF[grp]
        names = [n for gr in _BATCHES[batch] for n in _GROUPS[gr]]
        if not all(n in pending for n in names):
            return jnp.zeros((8, LANES), F32)
        send = [pending[n] for n in names]
        scatter[batch], tok = _exchange_start(send, [True] * len(names), send[0], name="scatter_start_" + batch)
        return tok

    small = {"g_cq": ev_g_cq, "g_ckv": ev_g_ckv, "od_rel_bias": od_rel_bias[0],
             "g_mix": g_mix + token[0, 0], "g_ffn": g_ffn, "g_final": g_final.reshape(1, -1)}
    loss_part, dx, G = _local_step(x[0], loss_target[0], small, get_weights, put_grads)
    g_small = _pack_small([G["g_cq"], G["g_ckv"], G["od_rel_bias"], G["g_mix"], G["g_ffn"], G["g_final"],
                           loss_part.reshape(1)])
    small_handle, _ = _exchange_start([g_small], [False], dx, name="gather_start_small")

    grads, deltas, new_m, new_v = {}, {}, {}, {}
    parts, after = {}, dx

    def wait_parts(batch, after):
        lands = _exchange_wait(scatter[batch], after, name="scatter_wait_" + batch)
        parts.update(zip([n for grp in _BATCHES[batch] for n in _GROUPS[grp]], lands))
        return lands[0]

    def adamw(n):
        col = n in _COL_SHARDED
        rows = lambda a: (jnp.swapaxes(a, 1, 2) if col else a).reshape(-1, a.shape[1 if col else 2])
        layers = [parts[n]] if n in parts else [parts[n + "0"], parts[n + "1"]]
        res = _adamw(rows(w[n]), [p.reshape(N_DEV, -1, p.shape[-1]) for p in layers], rows(mom[n]), rows(var[n]),
                     name="adamw_" + n)
        L, a1, a2 = w[n].shape
        back = lambda r: jnp.swapaxes(r.reshape(L, a2, a1), 1, 2) if col else r.reshape(L, a1, a2)
        grads[n], deltas[n], new_m[n], new_v[n] = [back(r) for r in res]
        return res[0]

    for batch in ("layer1", "layer0"):
        after = wait_parts(batch, after)
    after = wait_parts("in0", tuple(adamw(n) for n in _BIG[1:]))
    after = adamw("ev_w_in")
    small_w = [w[n] for n in _SMALL]
    small_parts = _exchange_wait(small_handle, after, name="gather_wait_small")[0]
    loss = jnp.sum(small_parts.reshape(N_DEV, -1)[:, sum(v.size for v in small_w)])
    res = _adamw(_pack_small(small_w), [small_parts], _pack_small([mom[n] for n in _SMALL]),
                 _pack_small([var[n] for n in _SMALL]), name="adamw_small")
    for d, packed in zip((grads, deltas, new_m, new_v), res):
        for n, val in zip(_SMALL, _unpack_small(packed, small_w)):
            d[n] = val

    order = ["ev_w_in", "ev_g_cq", "ev_w_uq", "ev_g_ckv", "ev_w_ukv", "ev_w_out", "od_w_qkv", "od_rel_bias",
             "od_w_out", "g_mix", "g_ffn", "w_gate", "w_up", "w_down", "g_final"]
    out = [loss, dx[None]]
    for d in (grads, deltas, new_m, new_v):
        out += [d[n] for n in order]
    return tuple(out)
```

```python
import functools

import numpy as np
import jax
import jax.numpy as jnp
from jax import lax
from jax.experimental import pallas as pl
from jax.experimental.pallas import tpu as pltpu

F32 = jnp.float32
BF16 = jnp.bfloat16

D_MODEL = 1024
CHUNK = 64
MLA_HEADS = 8
MLA_NOPE = 64
MLA_ROPE = 32
MLA_V = 64
Q_LORA = 384
KV_LORA = 256
ROPE_THETA = 10000.0
SB_HEADS = 8
SB_DIM = 64
C_HEADS = 16
C_DIM = 64
LEFT_CHUNKS = 8
REL_CLIP = 256
D_FF = 2816
RMS_EPS = 1e-6
ADAM_LR = 0.001
ADAM_B1 = 0.9
ADAM_B2 = 0.999
ADAM_EPS = 1e-08
ADAM_WD = 0.01
ADAM_STEP = 10

N_DEV = 8
LANES = 128
HEAD = 64
assert HEAD == MLA_NOPE == MLA_V == SB_DIM == C_DIM and 2 * HEAD == LANES
CHUNK_BITS = CHUNK.bit_length() - 1
assert 1 << CHUNK_BITS == CHUNK
VMEM_LIMIT = 56 * 1024 * 1024
NEG = -1e30
PAD_KEYS = LEFT_CHUNKS * CHUNK
BAND_TQ = 128
BAND_W = BAND_TQ + PAD_KEYS
TOEP_W = BAND_W + BAND_TQ

NN = (((1,), (0,)), ((), ()))
NT = (((1,), (1,)), ((), ()))
TN = (((0,), (0,)), ((), ()))


def _dot(a, b, dn):
    return lax.dot_general(a, b, dn, preferred_element_type=F32)


def _pick(dim, pref):
    if dim <= pref:
        return dim
    best = None
    for t in range(LANES, pref + 1, LANES):
        if dim % t == 0:
            best = t
    assert best is not None, (dim, pref)
    return best


def _params(sem):
    return pltpu.CompilerParams(dimension_semantics=sem, vmem_limit_bytes=VMEM_LIMIT)


def _mm(a, b, dims="nn", res=None, out_dtype=F32, name="mm"):
    if dims == "nn":
        (M, K), (K2, N) = a.shape, b.shape
    elif dims == "nt":
        (M, K), (N, K2) = a.shape, b.shape
    else:
        (K, M), (K2, N) = a.shape, b.shape
    assert K == K2, (a.shape, b.shape, dims)
    tm, tn, tk = _pick(M, 1024), _pick(N, 1152), _pick(K, 1024)
    nk = K // tk
    dn = {"nn": NN, "nt": NT, "tn": TN}[dims]
    has_res = res is not None

    def body(*refs):
        if has_res:
            a_ref, b_ref, r_ref, o_ref, acc = refs
        else:
            a_ref, b_ref, o_ref, acc = refs
        k = pl.program_id(2)

        @pl.when(k == 0)
        def _():
            acc[...] = jnp.zeros_like(acc)

        acc[...] += _dot(a_ref[...].astype(BF16), b_ref[...].astype(BF16), dn)

        @pl.when(k == nk - 1)
        def _():
            r = acc[...]
            if has_res:
                r = r + r_ref[...]
            o_ref[...] = r.astype(out_dtype)

    a_spec = (pl.BlockSpec((tk, tm), lambda i, j, k: (k, i)) if dims == "tn"
              else pl.BlockSpec((tm, tk), lambda i, j, k: (i, k)))
    b_spec = (pl.BlockSpec((tn, tk), lambda i, j, k: (j, k)) if dims == "nt"
              else pl.BlockSpec((tk, tn), lambda i, j, k: (k, j)))
    o_spec = pl.BlockSpec((tm, tn), lambda i, j, k: (i, j))
    in_specs = [a_spec, b_spec] + ([o_spec] if has_res else [])
    args = (a, b) + ((res,) if has_res else ())
    return pl.pallas_call(
        body, name=name, grid=(M // tm, N // tn, nk),
        in_specs=in_specs, out_specs=o_spec,
        out_shape=jax.ShapeDtypeStruct((M, N), out_dtype),
        scratch_shapes=[pltpu.VMEM((tm, tn), F32)],
        compiler_params=_params(("parallel", "parallel", "arbitrary")),
    )(*args)


def _rms_fwd(x, g, out_dtype=BF16, name="rms_fwd"):
    T, Fd = x.shape
    tm = _pick(T, 256)

    def body(x_ref, g_ref, o_ref):
        xv = x_ref[...]
        r = lax.rsqrt(jnp.mean(xv * xv, axis=-1, keepdims=True) + RMS_EPS)
        o_ref[...] = (xv * r * g_ref[...]).astype(out_dtype)

    return pl.pallas_call(
        body, name=name, grid=(T // tm,),
        in_specs=[pl.BlockSpec((tm, Fd), lambda i: (i, 0)), pl.BlockSpec((1, Fd), lambda i: (0, 0))],
        out_specs=pl.BlockSpec((tm, Fd), lambda i: (i, 0)),
        out_shape=jax.ShapeDtypeStruct((T, Fd), out_dtype),
        compiler_params=_params(("parallel",)),
    )(x, g)


def _rms_bwd(x, g, dy, dres=None, name="rms_bwd"):
    T, Fd = x.shape
    tm = _pick(T, 256)
    has_res = dres is not None

    def body(*refs):
        if has_res:
            x_ref, g_ref, dy_ref, r_ref, dx_ref, dxb_ref, dg_ref = refs
        else:
            x_ref, g_ref, dy_ref, dx_ref, dxb_ref, dg_ref = refs
        xv, dyv = x_ref[...], dy_ref[...]
        r = lax.rsqrt(jnp.mean(xv * xv, axis=-1, keepdims=True) + RMS_EPS)
        gdy = dyv * g_ref[...]
        dot = jnp.mean(xv * gdy, axis=-1, keepdims=True)
        dx = r * gdy - xv * (r * r * r * dot)
        if has_res:
            dx = dx + r_ref[...]
        dx_ref[...] = dx
        dxb_ref[...] = dx.astype(BF16)

        @pl.when(pl.program_id(0) == 0)
        def _():
            dg_ref[...] = jnp.zeros_like(dg_ref)

        dg_ref[...] += jnp.sum(dyv * xv * r, axis=0, keepdims=True)

    row = pl.BlockSpec((tm, Fd), lambda i: (i, 0))
    vec = pl.BlockSpec((1, Fd), lambda i: (0, 0))
    in_specs = [row, vec, row] + ([row] if has_res else [])
    args = (x, g, dy) + ((dres,) if has_res else ())
    return pl.pallas_call(
        body, name=name, grid=(T // tm,),
        in_specs=in_specs, out_specs=[row, row, vec],
        out_shape=[jax.ShapeDtypeStruct((T, Fd), F32), jax.ShapeDtypeStruct((T, Fd), BF16),
                   jax.ShapeDtypeStruct((1, Fd), F32)],
        compiler_params=_params(("arbitrary",)),
    )(*args)


def _mm_rms_bwd(a, b, x, g, dres, name="mm_rms_bwd"):
    T, K = a.shape
    Fd = b.shape[1]
    tm, tk = _pick(T, 512), _pick(K, 1024)
    nk = K // tk

    def body(a_ref, b_ref, x_ref, g_ref, r_ref, dx_ref, dxb_ref, dg_ref, acc):
        i, k = pl.program_id(0), pl.program_id(1)

        @pl.when(k == 0)
        def _():
            acc[...] = jnp.zeros_like(acc)

        @pl.when((k == 0) & (i == 0))
        def _():
            dg_ref[...] = jnp.zeros_like(dg_ref)

        acc[...] += _dot(a_ref[...].astype(BF16), b_ref[...].astype(BF16), NN)

        @pl.when(k == nk - 1)
        def _():
            xv, dyv = x_ref[...], acc[...]
            r = lax.rsqrt(jnp.mean(xv * xv, axis=-1, keepdims=True) + RMS_EPS)
            gdy = dyv * g_ref[...]
            dot = jnp.mean(xv * gdy, axis=-1, keepdims=True)
            dx = r * gdy - xv * (r * r * r * dot) + r_ref[...]
            dx_ref[...] = dx
            dxb_ref[...] = dx.astype(BF16)
            dg_ref[...] += jnp.sum(dyv * xv * r, axis=0, keepdims=True)

    row = pl.BlockSpec((tm, Fd), lambda i, k: (i, 0))
    vec = pl.BlockSpec((1, Fd), lambda i, k: (0, 0))
    return pl.pallas_call(
        body, name=name, grid=(T // tm, nk),
        in_specs=[pl.BlockSpec((tm, tk), lambda i, k: (i, k)), pl.BlockSpec((tk, Fd), lambda i, k: (k, 0)),
                  row, vec, row],
        out_specs=[row, row, vec],
        out_shape=[jax.ShapeDtypeStruct((T, Fd), F32), jax.ShapeDtypeStruct((T, Fd), BF16),
                   jax.ShapeDtypeStruct((1, Fd), F32)],
        scratch_shapes=[pltpu.VMEM((tm, Fd), F32)],
        compiler_params=_params(("arbitrary", "arbitrary")),
    )(a, b, x, g, dres)


def _loss_head(h, g, target, name="loss_head"):
    T, Fd = h.shape
    tm = _pick(T, 256)

    def body(h_ref, g_ref, t_ref, loss_ref, dh_ref, dhb_ref, dg_ref):
        xv = h_ref[...]
        r = lax.rsqrt(jnp.mean(xv * xv, axis=-1, keepdims=True) + RMS_EPS)
        diff = xv * r * g_ref[...] - t_ref[...]
        part = 0.5 * jnp.sum(jnp.mean(diff * diff, axis=-1, keepdims=True), axis=0, keepdims=True)
        dyv = diff * (1.0 / Fd)
        gdy = dyv * g_ref[...]
        dot = jnp.mean(xv * gdy, axis=-1, keepdims=True)
        dh = r * gdy - xv * (r * r * r * dot)
        dh_ref[...] = dh
        dhb_ref[...] = dh.astype(BF16)

        @pl.when(pl.program_id(0) == 0)
        def _():
            dg_ref[...] = jnp.zeros_like(dg_ref)
            loss_ref[...] = jnp.zeros_like(loss_ref)

        dg_ref[...] += jnp.sum(dyv * xv * r, axis=0, keepdims=True)
        loss_ref[...] += jnp.broadcast_to(part, loss_ref.shape)

    row = pl.BlockSpec((tm, Fd), lambda i: (i, 0))
    vec = pl.BlockSpec((1, Fd), lambda i: (0, 0))
    return pl.pallas_call(
        body, name=name, grid=(T // tm,),
        in_specs=[row, vec, row],
        out_specs=[pl.BlockSpec((1, LANES), lambda i: (0, 0)), row, row, vec],
        out_shape=[jax.ShapeDtypeStruct((1, LANES), F32), jax.ShapeDtypeStruct((T, Fd), F32),
                   jax.ShapeDtypeStruct((T, Fd), BF16), jax.ShapeDtypeStruct((1, Fd), F32)],
        compiler_params=_params(("arbitrary",)),
    )(h, g, target)


FFN_TF = 256


def _ffn_fwd(h, g, wg_t, wu_t, wd, name="ffn_fwd"):
    T, Dm = h.shape
    Fh = wd.shape[0]
    tm = _pick(T, 1024)
    nf = Fh // FFN_TF

    def body(h_ref, g_ref, wg_ref, wu_ref, wd_ref, o_ref, u_ref, a_ref, b_ref):
        j = pl.program_id(1)

        @pl.when(j == 0)
        def _():
            xv = h_ref[...]
            r = lax.rsqrt(jnp.mean(xv * xv, axis=-1, keepdims=True) + RMS_EPS)
            u_ref[...] = (xv * r * g_ref[...]).astype(BF16)
            o_ref[...] = xv

        u = u_ref[...]
        a = _dot(u, wg_ref[...], NT).astype(BF16)
        b = _dot(u, wu_ref[...], NT).astype(BF16)
        a_ref[...] = a
        b_ref[...] = b
        af = a.astype(F32)
        s = (af * jax.nn.sigmoid(af) * b.astype(F32)).astype(BF16)
        o_ref[...] += _dot(s, wd_ref[...], NN)

    row = pl.BlockSpec((tm, Dm), lambda i, j: (i, 0))
    wblk = pl.BlockSpec((FFN_TF, Dm), lambda i, j: (j, 0))
    ablk = pl.BlockSpec((tm, FFN_TF), lambda i, j: (i, j))
    return pl.pallas_call(
        body, name=name, grid=(T // tm, nf),
        in_specs=[row, pl.BlockSpec((1, Dm), lambda i, j: (0, 0)), wblk, wblk, wblk],
        out_specs=[row, row, ablk, ablk],
        out_shape=[jax.ShapeDtypeStruct((T, Dm), F32), jax.ShapeDtypeStruct((T, Dm), BF16),
                   jax.ShapeDtypeStruct((T, Fh), BF16), jax.ShapeDtypeStruct((T, Fh), BF16)],
        compiler_params=_params(("parallel", "arbitrary")),
    )(h, g, wg_t, wu_t, wd)


def _ffn_bwd(dh, u, a, b, wg_t, wu_t, wd, name="ffn_bwd"):
    T, Dm = dh.shape
    Fh = wd.shape[0]
    nf = Fh // FFN_TF
    once = pl.Buffered(1)

    def body(dh_ref, u_ref, a_ref, b_ref, wg_ref, wu_ref, wd_ref, du_ref, dwg_ref, dwu_ref, dwd_ref):
        j = pl.program_id(0)

        @pl.when(j == 0)
        def _():
            du_ref[...] = jnp.zeros_like(du_ref)

        ds = _dot(dh_ref[...], wd_ref[...], NT)
        af, bf = a_ref[...].astype(F32), b_ref[...].astype(F32)
        sig = jax.nn.sigmoid(af)
        sa = af * sig
        dwd_ref[...] = _dot((sa * bf).astype(BF16), dh_ref[...], TN).astype(BF16)
        dab = jnp.concatenate([(ds * bf * (sig * (1.0 + af * (1.0 - sig)))).astype(BF16),
                               (ds * sa).astype(BF16)], axis=1)
        dw = _dot(dab, u_ref[...], TN)
        dwg_ref[...] = dw[:FFN_TF].astype(BF16)
        dwu_ref[...] = dw[FFN_TF:].astype(BF16)
        du_ref[...] += _dot(dab, jnp.concatenate([wg_ref[...], wu_ref[...]], axis=0), NN)

    full = lambda: pl.BlockSpec((T, Dm), lambda j: (0, 0), pipeline_mode=once)
    wblk = pl.BlockSpec((FFN_TF, Dm), lambda j: (j, 0))
    ablk = pl.BlockSpec((T, FFN_TF), lambda j: (0, j))
    return pl.pallas_call(
        body, name=name, grid=(nf,),
        in_specs=[full(), full(), ablk, ablk, wblk, wblk, wblk],
        out_specs=[pl.BlockSpec((T, Dm), lambda j: (0, 0)), wblk, wblk, wblk],
        out_shape=[jax.ShapeDtypeStruct((T, Dm), F32)] + [jax.ShapeDtypeStruct((Fh, Dm), BF16)] * 3,
        compiler_params=_params(("arbitrary",)),
    )(dh, u, a, b, wg_t, wu_t, wd)


def _rope(x, cos_t, sin_t, col0, ncols, out_dtype, name="rope"):
    T = x.shape[0]
    wt = cos_t.shape[1]
    tm = _pick(T, 256)
    nb = ncols * LANES // wt
    half = MLA_ROPE // 2

    def body(x_ref, c_ref, s_ref, o_ref):
        xv = x_ref[...].astype(F32)
        lane = lax.broadcasted_iota(jnp.int32, xv.shape, 1)
        first = (lane & (MLA_ROPE - 1)) < half
        swapped = jnp.where(first, pltpu.roll(xv, wt - half, 1), pltpu.roll(xv, half, 1))
        o_ref[...] = (xv * c_ref[...] + swapped * s_ref[...]).astype(out_dtype)

    off = col0 * LANES // wt
    return pl.pallas_call(
        body, name=name, grid=(T // tm, nb),
        in_specs=[pl.BlockSpec((tm, wt), lambda i, j: (i, j + off)),
                  pl.BlockSpec((tm, wt), lambda i, j: (i, 0)),
                  pl.BlockSpec((tm, wt), lambda i, j: (i, 0))],
        out_specs=pl.BlockSpec((tm, wt), lambda i, j: (i, j)),
        out_shape=jax.ShapeDtypeStruct((T, ncols * LANES), out_dtype),
        compiler_params=_params(("parallel", "parallel")),
    )(x, cos_t, sin_t)


ATT_TQ = 512
ATT_TK = 256


def _mla_masks(shape):
    lane = lax.broadcasted_iota(jnp.int32, shape, 1)
    m0 = (lane < HEAD) | ((lane >= LANES) & (lane < LANES + MLA_ROPE))
    m1 = ((lane >= HEAD) & (lane < LANES)) | ((lane >= LANES + MLA_ROPE) & (lane < LANES + 2 * MLA_ROPE))
    return m0, m1


def _by_twos(n, step, carry):
    carry = lax.fori_loop(0, n // 2, lambda i, c: step(2 * i + 1, step(2 * i, c)), carry)
    return lax.fori_loop(0, n % 2, lambda _, c: step(n - 1, c), carry)


def _chunk_ok(tq, tk, d):
    row = lax.broadcasted_iota(jnp.int32, (tq, tk), 0)
    col = lax.broadcasted_iota(jnp.int32, (tq, tk), 1) + d * tk
    return jnp.concatenate([(col >> CHUNK_BITS) <= (row >> CHUNK_BITS)] * 2, axis=0)


def _rotate(x, cos_t, sin_t):
    half = MLA_ROPE // 2
    lane = lax.broadcasted_iota(jnp.int32, x.shape, 1)
    first = (lane & (MLA_ROPE - 1)) < half
    swapped = jnp.where(first, pltpu.roll(x, x.shape[1] - half, 1), pltpu.roll(x, half, 1))
    return x * cos_t + swapped * sin_t


def _mla_fwd(q, cos_q, sin_q, kv, kr, name="mla_fwd"):
    T = q.shape[0]
    tq, tk = _pick(T, ATT_TQ), _pick(T, ATT_TK)
    nd = tq // tk
    npair = MLA_HEADS // 2
    scale = (MLA_NOPE + MLA_ROPE) ** -0.5

    def body(q_ref, c_ref, s_ref, kn_ref, v_ref, kr_ref, o_ref, lse_ref):
        m_idx = pl.program_id(1)
        qv = _rotate(q_ref[...], c_ref[...], s_ref[...]).astype(BF16)
        m0, m1 = _mla_masks(qv.shape)
        qs = jnp.concatenate([jnp.where(m0, qv, 0), jnp.where(m1, qv, 0)], axis=0).astype(BF16)

        def block(kb, carry, ok):
            ks = pl.ds(pl.multiple_of(kb * tk, tk), tk)
            kcat = jnp.concatenate([kn_ref[ks, :], kr_ref[ks, :]], axis=1)
            mx, l, acc = carry
            s = _dot(qs, kcat, NT) * scale
            if ok is not None:
                s = jnp.where(ok, s, NEG)
            mn = jnp.maximum(mx, jnp.max(s, axis=-1, keepdims=True))
            alpha = jnp.exp(mx - mn)
            p = jnp.exp(s - mn)
            return (mn, alpha * l + jnp.sum(p, axis=-1, keepdims=True),
                    alpha * acc + _dot(p.astype(BF16), v_ref[ks, :], NN))

        init = (jnp.full((2 * tq, 1), NEG, F32), jnp.zeros((2 * tq, 1), F32), jnp.zeros((2 * tq, LANES), F32))
        res = init
        for d in range(nd):
            res = block(m_idx * nd + d, res, _chunk_ok(tq, tk, d))
        mx, l, acc = _by_twos(m_idx * nd, lambda kb, c: block(kb, c, None), res)
        h0 = lax.broadcasted_iota(jnp.int32, (tq, LANES), 1) < HEAD
        o_ref[...] = _two_heads(acc * (1.0 / l), h0).astype(o_ref.dtype)
        lse_ref[...] = _two_heads(jnp.broadcast_to(mx + jnp.log(l), (2 * tq, LANES)), h0)

    full = lambda col: pl.BlockSpec((T, LANES), col)
    table = pl.BlockSpec((tq, 2 * LANES), lambda p, m: (m, 0))
    return pl.pallas_call(
        body, name=name, grid=(npair, T // tq),
        in_specs=[pl.BlockSpec((tq, 2 * LANES), lambda p, m: (m, p)), table, table,
                  full(lambda p, m: (0, p)), full(lambda p, m: (0, npair + p)), full(lambda p, m: (0, 0))],
        out_specs=[pl.BlockSpec((tq, LANES), lambda p, m: (m, p)),
                   pl.BlockSpec((tq, LANES), lambda p, m: (m, p))],
        out_shape=[jax.ShapeDtypeStruct((T, npair * LANES), BF16),
                   jax.ShapeDtypeStruct((T, npair * LANES), F32)],
        compiler_params=_params(("parallel", "arbitrary")),
    )(q, cos_q, sin_q, kv, kv, kr)


def _mla_bwd(q, cos_q, sin_q, kv, kr, o, lse, do, do_col0, name="mla_bwd"):
    T = q.shape[0]
    tq, tk = _pick(T, ATT_TQ), _pick(T, ATT_TK)
    nd = tq // tk
    npair = MLA_HEADS // 2
    scale = (MLA_NOPE + MLA_ROPE) ** -0.5

    def body(q_ref, c_ref, s_ref, kn_ref, v_ref, kr_ref, o_ref, lse_ref, do_ref, dq_ref, dkn_ref, dv_ref, dkr_ref,
             dkn_acc, dv_acc):
        p_idx, m_idx = pl.program_id(0), pl.program_id(1)

        @pl.when(m_idx == 0)
        def _():
            dkn_acc[...] = jnp.zeros_like(dkn_acc)
            dv_acc[...] = jnp.zeros_like(dv_acc)

        @pl.when((m_idx == 0) & (p_idx == 0))
        def _():
            dkr_ref[...] = jnp.zeros_like(dkr_ref)

        qv = _rotate(q_ref[...], c_ref[...], s_ref[...]).astype(BF16)
        m0, m1 = _mla_masks(qv.shape)
        qs = jnp.concatenate([jnp.where(m0, qv, 0), jnp.where(m1, qv, 0)], axis=0).astype(BF16)
        dov = do_ref[...].astype(F32)
        h0 = lax.broadcasted_iota(jnp.int32, (tq, LANES), 1) < HEAD
        dos32 = jnp.concatenate([jnp.where(h0, dov, 0.0), jnp.where(h0, 0.0, dov)], axis=0)
        ov = o_ref[...].astype(F32)
        delta = jnp.sum(dos32 * jnp.concatenate([ov, ov], axis=0), axis=-1, keepdims=True)
        dos = dos32.astype(BF16)
        lsev = lse_ref[...]
        lse = jnp.concatenate([lsev[:, 0:1], lsev[:, HEAD:HEAD + 1]], axis=0)

        def block(kb, dq, ok):
            ks = pl.ds(pl.multiple_of(kb * tk, tk), tk)
            kcat = jnp.concatenate([kn_ref[ks, :], kr_ref[ks, :]], axis=1)
            vv = v_ref[ks, :]
            p = jnp.exp(_dot(qs, kcat, NT) * scale - lse)
            if ok is not None:
                p = jnp.where(ok, p, 0.0)
            ds = (p * (_dot(dos, vv, NT) - delta) * scale).astype(BF16)
            dkc = _dot(ds, qs, TN)
            dkn_acc[ks, :] += dkc[:, :LANES]
            dkr_ref[ks, :] += dkc[:, LANES:]
            dv_acc[ks, :] += _dot(p.astype(BF16), dos, TN)
            return dq + _dot(ds, kcat, NN)

        dq = jnp.zeros((2 * tq, 2 * LANES), F32)
        for d in range(nd):
            dq = block(m_idx * nd + d, dq, _chunk_ok(tq, tk, d))
        dq = _by_twos(m_idx * nd, lambda kb, c: block(kb, c, None), dq)
        dq_ref[...] = _rotate(jnp.where(m0, dq[:tq], jnp.where(m1, dq[tq:], 0.0)), c_ref[...],
                              -s_ref[...]).astype(BF16)

        @pl.when(m_idx == T // tq - 1)
        def _():
            dkn_ref[...] = dkn_acc[...].astype(BF16)
            dv_ref[...] = dv_acc[...].astype(BF16)

    full = lambda col: pl.BlockSpec((T, LANES), col)
    blk = lambda col: pl.BlockSpec((tq, LANES), col)
    table = pl.BlockSpec((tq, 2 * LANES), lambda p, m: (m, 0))
    return pl.pallas_call(
        body, name=name, grid=(npair, T // tq),
        in_specs=[pl.BlockSpec((tq, 2 * LANES), lambda p, m: (m, p)), table, table,
                  full(lambda p, m: (0, p)), full(lambda p, m: (0, npair + p)), full(lambda p, m: (0, 0)),
                  blk(lambda p, m: (m, p)), blk(lambda p, m: (m, p)),
                  blk(lambda p, m: (m, do_col0 + p))],
        out_specs=[pl.BlockSpec((tq, 2 * LANES), lambda p, m: (m, p)),
                   full(lambda p, m: (0, p)), full(lambda p, m: (0, p)), full(lambda p, m: (0, 0))],
        out_shape=[jax.ShapeDtypeStruct((T, npair * 2 * LANES), BF16),
                   jax.ShapeDtypeStruct((T, npair * LANES), BF16),
                   jax.ShapeDtypeStruct((T, npair * LANES), BF16),
                   jax.ShapeDtypeStruct((T, LANES), F32)],
        scratch_shapes=[pltpu.VMEM((T, LANES), F32)] * 2,
        compiler_params=_params(("arbitrary", "arbitrary")),
    )(q, cos_q, sin_q, kv, kv, kr, o, lse, do)


def _split_dot(x, tri):
    hi = x.astype(BF16)
    lo = (x - hi.astype(F32)).astype(BF16)
    both = _dot(jnp.concatenate([hi, lo], axis=0), tri, NN)
    return both[:x.shape[0]] + both[x.shape[0]:]


def _sb_terms(qh, kk, before):
    z = _dot(qh, kk, NT)
    sp = jnp.maximum(z, 0.0) + jnp.log(1.0 + jnp.exp(-jnp.abs(z)))
    lk = -sp if before is None else jnp.where(before, -sp, 0.0)
    return z, sp, lk


def _sb_setup(q_ref, tq, tk, scale):
    qv = (q_ref[...].astype(F32) * scale).astype(BF16)
    lane = lax.broadcasted_iota(jnp.int32, (tq, LANES), 1)
    h0 = lane < HEAD
    qs = jnp.concatenate([jnp.where(h0, qv, 0), jnp.where(h0, 0, qv)], axis=0).astype(BF16)
    row = lax.broadcasted_iota(jnp.int32, (tk, tk), 0)
    col = lax.broadcasted_iota(jnp.int32, (tk, tk), 1)
    return qs, h0, row, col


def _sb_before(tq, tk, d):
    row = lax.broadcasted_iota(jnp.int32, (tq, tk), 0)
    col = lax.broadcasted_iota(jnp.int32, (tq, tk), 1) + d * tk
    return jnp.concatenate([col < row] * 2, axis=0)


def _two_heads(x, h0):
    tq = x.shape[0] // 2
    return jnp.where(h0, x[:tq], x[tq:])


def _sb_fwd(qkv, col0, name="sb_fwd"):
    T = qkv.shape[0]
    tq, tk = _pick(T, ATT_TQ), _pick(T, ATT_TK)
    nd = tq // tk
    npair = SB_HEADS // 2
    scale = SB_DIM ** -0.5

    def body(q_ref, k_ref, v_ref, o_ref, o32_ref, w_ref, sp_ref):
        m_idx = pl.program_id(1)
        qs, h0, row, col = _sb_setup(q_ref, tq, tk, scale)
        later = (row > col).astype(BF16)

        def block(kb, carry, before):
            ks = pl.ds(pl.multiple_of(kb * tk, tk), tk)
            c, acc = carry
            z, sp, lk = _sb_terms(qs, k_ref[ks, :].astype(BF16), before)
            w = jnp.exp((z - sp) + _split_dot(lk, later) + c)
            if before is not None:
                w = jnp.where(before, w, 0.0)
            wb = w.astype(BF16)
            w_ref[0, 0, kb] = wb
            sp_ref[0, 0, kb] = sp.astype(BF16)
            return (c + jnp.sum(lk, axis=-1, keepdims=True), acc + _dot(wb, v_ref[ks, :].astype(BF16), NN))

        init = (jnp.zeros((2 * tq, 1), F32), jnp.zeros((2 * tq, LANES), F32))
        res = init
        for d in reversed(range(nd)):
            res = block(m_idx * nd + d, res, _sb_before(tq, tk, d))
        res = _by_twos(m_idx * nd, lambda i, c: block(m_idx * nd - 1 - i, c, None), res)
        o = _two_heads(res[1], h0)
        o_ref[...] = o.astype(o_ref.dtype)
        o32_ref[...] = o

    full = lambda col: pl.BlockSpec((T, LANES), col)
    blk = pl.BlockSpec((tq, LANES), lambda p, m: (m, p))
    return pl.pallas_call(
        body, name=name, grid=(npair, T // tq),
        in_specs=[pl.BlockSpec((tq, LANES), lambda p, m: (m, col0 + p)),
                  full(lambda p, m: (0, col0 + npair + p)), full(lambda p, m: (0, col0 + 2 * npair + p))],
        out_specs=[blk, blk] + [pl.BlockSpec((1, 1, T // tk, 2 * tq, tk), lambda p, m: (p, m, 0, 0, 0))] * 2,
        out_shape=[jax.ShapeDtypeStruct((T, npair * LANES), BF16), jax.ShapeDtypeStruct((T, npair * LANES), F32)]
        + [jax.ShapeDtypeStruct((npair, T // tq, T // tk, 2 * tq, tk), BF16)] * 2,
        compiler_params=_params(("parallel", "arbitrary")),
    )(qkv, qkv, qkv)


def _sb_bwd(qkv, col0, o32, w_all, sp_all, do, do_col0, dep, name="sb_bwd"):
    T = qkv.shape[0]
    tq, tk = _pick(T, ATT_TQ), _pick(T, ATT_TK)
    nd = tq // tk
    npair = SB_HEADS // 2
    scale = SB_DIM ** -0.5

    def body(q_ref, k_ref, v_ref, o_ref, w_ref, sp_ref, do_ref, dep_ref, dq_ref, dk_ref, dv_ref, dk_acc, dv_acc):
        m_idx = pl.program_id(1)

        @pl.when(m_idx == 0)
        def _():
            dk_acc[...] = jnp.zeros_like(dk_acc)
            dv_acc[...] = jnp.zeros_like(dv_acc)

        qs, h0, row, col = _sb_setup(q_ref, tq, tk, scale)
        dov = do_ref[...].astype(F32)
        dos = jnp.concatenate([jnp.where(h0, dov, 0.0), jnp.where(h0, 0.0, dov)], axis=0).astype(BF16)
        ov = o_ref[...]
        etot = jnp.sum(dos.astype(F32) * jnp.concatenate([ov, ov], axis=0), axis=-1, keepdims=True)
        from_here = (row >= col).astype(BF16)

        def block(kb, carry, before):
            ks = pl.ds(pl.multiple_of(kb * tk, tk), tk)
            kk = k_ref[ks, :].astype(BF16)
            vv = v_ref[ks, :].astype(BF16)
            es, dqa = carry
            wb = w_ref[0, 0, kb]
            e = wb.astype(F32) * _dot(dos, vv, NT)
            prev = etot - (_split_dot(e, from_here) + es)
            sig_neg = jnp.exp(-sp_ref[0, 0, kb].astype(F32))
            dz = e * sig_neg - (1.0 - sig_neg) * prev
            if before is not None:
                dz = jnp.where(before, dz, 0.0)
            dzb = dz.astype(BF16)
            dk_acc[ks, :] += _dot(dzb, qs, TN)
            dv_acc[ks, :] += _dot(wb, dos, TN)
            return es + jnp.sum(e, axis=-1, keepdims=True), dqa + _dot(dzb, kk, NN)

        init = (jnp.zeros((2 * tq, 1), F32), jnp.zeros((2 * tq, LANES), F32))
        res = init
        for d in reversed(range(nd)):
            res = block(m_idx * nd + d, res, _sb_before(tq, tk, d))
        res = _by_twos(m_idx * nd, lambda i, c: block(m_idx * nd - 1 - i, c, None), res)
        dq_ref[...] = (_two_heads(res[1], h0) * scale).astype(BF16)

        @pl.when(m_idx == T // tq - 1)
        def _():
            dk_ref[...] = dk_acc[...].astype(BF16)
            dv_ref[...] = dv_acc[...].astype(BF16)

    full = lambda col: pl.BlockSpec((T, LANES), col)
    blk = lambda col: pl.BlockSpec((tq, LANES), col)
    return pl.pallas_call(
        body, name=name, grid=(npair, T // tq),
        in_specs=[blk(lambda p, m: (m, col0 + p)),
                  full(lambda p, m: (0, col0 + npair + p)), full(lambda p, m: (0, col0 + 2 * npair + p)),
                  blk(lambda p, m: (m, p)),
                  pl.BlockSpec((1, 1, T // tk, 2 * tq, tk), lambda p, m: (p, m, 0, 0, 0)),
                  pl.BlockSpec((1, 1, T // tk, 2 * tq, tk), lambda p, m: (p, m, 0, 0, 0)),
                  blk(lambda p, m: (m, do_col0 + p)), pl.BlockSpec((8, LANES), lambda p, m: (0, 0))],
        out_specs=[blk(lambda p, m: (m, p)), full(lambda p, m: (0, p)), full(lambda p, m: (0, p))],
        out_shape=[jax.ShapeDtypeStruct((T, npair * LANES), BF16)] * 3,
        scratch_shapes=[pltpu.VMEM((T, LANES), F32)] * 2,
        compiler_params=_params(("arbitrary", "arbitrary")),
    )(qkv, qkv, qkv, o32, w_all, sp_all, do, dep)


def _band_in_window():
    cq = lax.broadcasted_iota(jnp.int32, (BAND_TQ, BAND_W), 0) >> CHUNK_BITS
    ckp = lax.broadcasted_iota(jnp.int32, (BAND_TQ, BAND_W), 1) >> CHUNK_BITS
    return (ckp >= cq) & (ckp <= cq + LEFT_CHUNKS)


def _band_real(m_idx):
    j = lax.broadcasted_iota(jnp.int32, (BAND_TQ, BAND_W), 1)
    return j >= PAD_KEYS - m_idx * BAND_TQ


def _band_probs(qh, kw, bias, real, scale):
    s = jnp.where(real, _dot(qh, kw, NT) * scale + bias, NEG)
    e = jnp.exp(s - jnp.max(s, axis=-1, keepdims=True))
    return e * (1.0 / jnp.sum(e, axis=-1, keepdims=True))


BAND_SUB = 4


def _band_fwd(qkv, k_pad, v_pad, bias_w, name="band_fwd"):
    T = qkv.shape[0]
    npair = C_HEADS // 2
    scale = C_DIM ** -0.5
    rows = BAND_SUB * BAND_TQ

    def body(q_ref, k_ref, v_ref, b_ref, o_ref, p_ref):
        lane = lax.broadcasted_iota(jnp.int32, (BAND_TQ, LANES), 1)
        h0 = lane < HEAD
        bias = jnp.concatenate([b_ref[0], b_ref[1]], axis=0)
        for sub in range(BAND_SUB):
            m_idx = pl.program_id(1) * BAND_SUB + sub
            win = pl.ds(pl.multiple_of(m_idx * BAND_TQ, BAND_TQ), BAND_W)
            kw, vw = k_ref[win, :], v_ref[win, :]
            qv = q_ref[sub * BAND_TQ:(sub + 1) * BAND_TQ, :]
            qs = jnp.concatenate([jnp.where(h0, qv, 0), jnp.where(h0, 0, qv)], axis=0).astype(BF16)
            p = _band_probs(qs, kw, bias, jnp.concatenate([_band_real(m_idx)] * 2, axis=0), scale).astype(BF16)
            p_ref[0, sub] = p
            o = _two_heads(_dot(p, vw, NN), h0)
            o_ref[sub * BAND_TQ:(sub + 1) * BAND_TQ, :] = o.astype(o_ref.dtype)

    Tp = T + PAD_KEYS
    return pl.pallas_call(
        body, name=name, grid=(npair, T // rows),
        in_specs=[pl.BlockSpec((rows, LANES), lambda p, m: (m, p)),
                  pl.BlockSpec((Tp, LANES), lambda p, m: (0, p)),
                  pl.BlockSpec((Tp, LANES), lambda p, m: (0, p)),
                  pl.BlockSpec((2, BAND_TQ, BAND_W), lambda p, m: (p, 0, 0))],
        out_specs=[pl.BlockSpec((rows, LANES), lambda p, m: (m, p)),
                   pl.BlockSpec((1, BAND_SUB, 2 * BAND_TQ, BAND_W), lambda p, m: (p, m, 0, 0))],
        out_shape=[jax.ShapeDtypeStruct((T, npair * LANES), BF16),
                   jax.ShapeDtypeStruct((npair, T // BAND_TQ, 2 * BAND_TQ, BAND_W), BF16)],
        compiler_params=_params(("parallel", "arbitrary")),
    )(qkv, k_pad, v_pad, bias_w)


def _band_bwd(qkv, k_pad, v_pad, probs, do, name="band_bwd"):
    T = qkv.shape[0]
    npair = C_HEADS // 2
    scale = C_DIM ** -0.5

    rows = BAND_SUB * BAND_TQ

    def body(q_ref, k_ref, v_ref, p_ref, do_ref, dq_ref, dk_ref, dv_ref, db_ref, dk_acc, dv_acc):
        @pl.when(pl.program_id(1) == 0)
        def _():
            dk_acc[...] = jnp.zeros_like(dk_acc)
            dv_acc[...] = jnp.zeros_like(dv_acc)
            db_ref[...] = jnp.zeros_like(db_ref)

        lane = lax.broadcasted_iota(jnp.int32, (BAND_TQ, LANES), 1)
        h0 = lane < HEAD
        dbs = jnp.zeros((2 * BAND_TQ, BAND_W), F32)
        for sub in range(BAND_SUB):
            m_idx = pl.program_id(1) * BAND_SUB + sub
            win = pl.ds(pl.multiple_of(m_idx * BAND_TQ, BAND_TQ), BAND_W)
            kw, vw = k_ref[win, :], v_ref[win, :]
            qv = q_ref[sub * BAND_TQ:(sub + 1) * BAND_TQ, :]
            dov = do_ref[sub * BAND_TQ:(sub + 1) * BAND_TQ, :].astype(F32)
            qs = jnp.concatenate([jnp.where(h0, qv, 0), jnp.where(h0, 0, qv)], axis=0).astype(BF16)
            dos = jnp.concatenate([jnp.where(h0, dov, 0.0), jnp.where(h0, 0.0, dov)], axis=0).astype(BF16)
            pb = p_ref[0, sub]
            p = pb.astype(F32)
            dp = _dot(dos, vw, NT)
            dsb = p * (dp - jnp.sum(p * dp, axis=-1, keepdims=True))
            dbs = dbs + dsb
            dsq = (dsb * scale).astype(BF16)
            dq_ref[sub * BAND_TQ:(sub + 1) * BAND_TQ, :] = _two_heads(_dot(dsq, kw, NN), h0).astype(BF16)
            dk_acc[win, :] += _dot(dsq, qs, TN)
            dv_acc[win, :] += _dot(pb, dos, TN)
        db_ref[0] += dbs[:BAND_TQ]
        db_ref[1] += dbs[BAND_TQ:]

        @pl.when(pl.program_id(1) == T // rows - 1)
        def _():
            dk_ref[...] = dk_acc[...].astype(BF16)
            dv_ref[...] = dv_acc[...].astype(BF16)

    Tp = T + PAD_KEYS
    blk = lambda col: pl.BlockSpec((rows, LANES), col)
    full = pl.BlockSpec((Tp, LANES), lambda p, m: (0, p))
    bias = pl.BlockSpec((2, BAND_TQ, BAND_W), lambda p, m: (p, 0, 0))
    prob = pl.BlockSpec((1, BAND_SUB, 2 * BAND_TQ, BAND_W), lambda p, m: (p, m, 0, 0))
    return pl.pallas_call(
        body, name=name, grid=(npair, T // rows),
        in_specs=[blk(lambda p, m: (m, p)), full, full, prob, blk(lambda p, m: (m, p))],
        out_specs=[blk(lambda p, m: (m, p)), full, full, bias],
        out_shape=[jax.ShapeDtypeStruct((T, npair * LANES), BF16),
                   jax.ShapeDtypeStruct((Tp, npair * LANES), BF16),
                   jax.ShapeDtypeStruct((Tp, npair * LANES), BF16),
                   jax.ShapeDtypeStruct((C_HEADS, BAND_TQ, BAND_W), F32)],
        scratch_shapes=[pltpu.VMEM((Tp, LANES), F32)] * 2,
        compiler_params=_params(("arbitrary", "arbitrary")),
    )(qkv, k_pad, v_pad, probs, do)


def _skew_bits(x, left):
    w = x.shape[1]
    row = lax.broadcasted_iota(jnp.int32, x.shape, 0)
    for b in range(BAND_TQ.bit_length() - 1):
        amt = (w - (1 << b)) if left else (1 << b)
        x = jnp.where((row >> b) & 1 == 1, pltpu.roll(x, amt, 1), x)
    return x


def _toeplitz(diag, name="toeplitz"):
    H = diag.shape[0]

    def body(d_ref, o_ref):
        x = jnp.broadcast_to(d_ref[0], (BAND_TQ, TOEP_W))
        o_ref[0] = jnp.where(_band_in_window(), _skew_bits(x, left=False)[:, BAND_TQ:], NEG)

    return pl.pallas_call(
        body, name=name, grid=(H,),
        in_specs=[pl.BlockSpec((1, 1, TOEP_W), lambda h: (h, 0, 0))],
        out_specs=pl.BlockSpec((1, BAND_TQ, BAND_W), lambda h: (h, 0, 0)),
        out_shape=jax.ShapeDtypeStruct((H, BAND_TQ, BAND_W), F32),
        compiler_params=_params(("parallel",)),
    )(diag.reshape(H, 1, TOEP_W))


def _toeplitz_bwd(dbias, name="toeplitz_bwd"):
    H = dbias.shape[0]

    def body(d_ref, o_ref):
        x = jnp.concatenate([jnp.zeros((BAND_TQ, BAND_TQ), F32), d_ref[0]], axis=1)
        o_ref[0] = jnp.sum(_skew_bits(x, left=True), axis=0, keepdims=True)

    return pl.pallas_call(
        body, name=name, grid=(H,),
        in_specs=[pl.BlockSpec((1, BAND_TQ, BAND_W), lambda h: (h, 0, 0))],
        out_specs=pl.BlockSpec((1, 1, TOEP_W), lambda h: (h, 0, 0)),
        out_shape=jax.ShapeDtypeStruct((H, 1, TOEP_W), F32),
        compiler_params=_params(("parallel",)),
    )(dbias).reshape(H, TOEP_W)


_HBM = pl.BlockSpec(memory_space=pltpu.HBM)
_SEM = pl.BlockSpec(memory_space=pltpu.SEMAPHORE)
_EFFECT = pltpu.SideEffectType.DATAFLOW_SIDE_EFFECTING


def _peers():
    x, y, c = lax.axis_index("x"), lax.axis_index("y"), lax.axis_index("c")
    out = []
    for k in range(1, N_DEV):
        peer = (1 - x if (k >> 2) & 1 else x, 1 - y if (k >> 1) & 1 else y, 1 - c if k & 1 else c)
        out.append((peer, 4 * peer[0] + 2 * peer[1] + peer[2]))
    return 4 * x + 2 * y + c, out


def _split_copies(ins, lands, scatter, send_sem, recv_sem, arriving):
    me, peers = _peers()
    out = []
    for a in range(len(ins)):
        for peer, idx in peers:
            out.append(pltpu.make_async_remote_copy(
                src_ref=ins[a].at[idx] if scatter[a] else ins[a],
                dst_ref=lands[a].at[idx if arriving else me], send_sem=send_sem, recv_sem=recv_sem,
                device_id=peer, device_id_type=pl.DeviceIdType.MESH))
    return out


def _landing_zones(arrays, scatter):
    return [lax.empty((N_DEV,) + (a.shape[1:] if s else a.shape), a.dtype) for a, s in zip(arrays, scatter)]


def _place_own(arrays, scatter, name):
    n = len(arrays)
    lands = _landing_zones(arrays, scatter)
    me = (4 * lax.axis_index("x") + 2 * lax.axis_index("y") + lax.axis_index("c")).astype(jnp.int32).reshape(1)

    def body(me_ref, *refs):
        for a in range(n):
            refs[2 * n + a][...] = refs[a][...].reshape(refs[2 * n + a].shape)

    def row_spec(shape):
        zeros = (0,) * (len(shape) - 1)
        return pl.BlockSpec((1,) + tuple(shape[1:]), lambda i, me_ref: (me_ref[0],) + zeros)

    in_specs = [row_spec(a.shape) if s else pl.BlockSpec(a.shape, lambda i, me_ref, nd=a.ndim: (0,) * nd)
                for a, s in zip(arrays, scatter)]
    return pl.pallas_call(
        body, name=name,
        out_shape=[jax.ShapeDtypeStruct(l.shape, l.dtype) for l in lands],
        grid_spec=pltpu.PrefetchScalarGridSpec(
            num_scalar_prefetch=1, grid=(1,),
            in_specs=in_specs + [pl.BlockSpec(memory_space=pl.ANY)] * n,
            out_specs=[row_spec(l.shape) for l in lands]),
        input_output_aliases={1 + n + i: i for i in range(n)},
        compiler_params=_params(("arbitrary",)),
    )(me, *arrays, *lands)


def _exchange_start(arrays, scatter, after, name, lands=None):
    n = len(arrays)
    if lands is None:
        lands = list(_place_own(arrays, scatter, name=name.replace("_start_", "_own_")))

    def body(*refs):
        ins, lnd = refs[:n], refs[n:2 * n]
        send_sem, recv_sem = refs[2 * n + 1:2 * n + 3]
        token = refs[-1]
        for cp in _split_copies(ins, lnd, scatter, send_sem, recv_sem, arriving=False):
            cp.start()
        token[...] = jnp.zeros_like(token)

    hbm = lambda a: pltpu.HBM(a.shape, a.dtype)
    out = pl.pallas_call(
        body, name=name,
        out_shape=(pltpu.SemaphoreType.DMA(()), pltpu.SemaphoreType.DMA(()),
                   *[hbm(a) for a in arrays], *[hbm(a) for a in lands],
                   jax.ShapeDtypeStruct((8, LANES), F32)),
        in_specs=[_HBM] * (2 * n) + [pl.BlockSpec(memory_space=pl.ANY)],
        out_specs=(_SEM, _SEM, *([_HBM] * (2 * n)), pl.BlockSpec(memory_space=pltpu.VMEM)),
        input_output_aliases={i: 2 + i for i in range(2 * n)},
        compiler_params=pltpu.CompilerParams(has_side_effects=_EFFECT),
    )(*[pltpu.with_memory_space_constraint(a, pltpu.HBM) for a in list(arrays) + lands], after)
    return (out[0], out[1], list(out[2:2 + n]), list(out[2 + n:2 + 2 * n]), tuple(scatter)), out[-1]


def _exchange_wait(handle, after, name):
    send_sem, recv_sem, ins, lands, scatter = handle
    n = len(ins)
    after = after if isinstance(after, tuple) else (after,)

    def body(*refs):
        i_ref, l_ref = refs[:n], refs[n:2 * n]
        s_sem, r_sem = refs[2 * n:2 * n + 2]
        for cp in _split_copies(i_ref, l_ref, scatter, s_sem, r_sem, arriving=False):
            cp.wait_send()
        for cp in _split_copies(i_ref, l_ref, scatter, s_sem, r_sem, arriving=True):
            cp.wait_recv()

    hbm = lambda a: pltpu.HBM(a.shape, a.dtype)
    out = pl.pallas_call(
        body, name=name,
        out_shape=tuple(hbm(a) for a in ins + lands),
        in_specs=[_HBM] * (2 * n) + [_SEM, _SEM] + [pl.BlockSpec(memory_space=pl.ANY)] * len(after),
        out_specs=tuple([_HBM] * (2 * n)),
        input_output_aliases={i: i for i in range(2 * n)},
        compiler_params=pltpu.CompilerParams(has_side_effects=_EFFECT),
    )(*ins, *lands, send_sem, recv_sem, *after)
    return list(out[n:])


def _adamw(w, parts, m, v, name="adamw"):
    R, C = w.shape
    L = len(parts)
    rl = R // L
    tr = max([t for t in range(16, 513, 16) if rl % t == 0], default=rl)
    nb = rl // tr
    c1 = 1.0 - ADAM_B1 ** ADAM_STEP
    c2 = 1.0 - ADAM_B2 ** ADAM_STEP

    def body(*refs):
        w_ref, p_refs, (m_ref, v_ref, g_ref, d_ref, nm_ref, nv_ref) = refs[0], refs[1:1 + L], refs[1 + L:]
        g = None
        for j, p_ref in enumerate(p_refs):
            gj = p_ref[0].astype(F32)
            for i in range(1, N_DEV):
                gj = gj + p_ref[i].astype(F32)
            g = gj if g is None else jnp.where(pl.program_id(0) == j, gj, g)
        nm = ADAM_B1 * m_ref[...] + (1.0 - ADAM_B1) * g
        nv = ADAM_B2 * v_ref[...] + (1.0 - ADAM_B2) * (g * g)
        g_ref[...] = g
        nm_ref[...] = nm
        nv_ref[...] = nv
        d_ref[...] = -ADAM_LR * ((nm / c1) / (jnp.sqrt(nv / c2) + ADAM_EPS) + ADAM_WD * w_ref[...])

    blk = pl.BlockSpec((tr, C), lambda l, i: (l * nb + i, 0))
    part = lambda j: pl.BlockSpec((N_DEV, tr, C), lambda l, i: (0, jnp.where(l == j, i, 0), 0))
    return pl.pallas_call(
        body, name=name, grid=(L, nb),
        in_specs=[blk] + [part(j) for j in range(L)] + [blk, blk],
        out_specs=[blk] * 4,
        out_shape=[jax.ShapeDtypeStruct((R, C), F32)] * 4,
        compiler_params=_params(("arbitrary", "arbitrary")),
    )(w, *parts, m, v)


_O1 = Q_LORA
_O2 = _O1 + KV_LORA
_O3 = _O2 + MLA_ROPE
_NB = SB_HEADS * SB_DIM
IN_W = _O2 + LANES + 3 * _NB
COL_KR = _O2 // LANES
COL_SB = COL_KR + 1


def _w_in_local(w):
    kr = w[_O2:_O3]
    pad = jnp.zeros((LANES - 2 * MLA_ROPE, w.shape[1]), w.dtype)
    return jnp.concatenate([w[:_O2], kr, kr, pad, w[_O3:]], axis=0)


def _w_in_grad(g):
    kr = (g[_O2:_O2 + MLA_ROPE].astype(F32) + g[_O2 + MLA_ROPE:_O2 + 2 * MLA_ROPE].astype(F32)).astype(g.dtype)
    return jnp.concatenate([g[:_O2], kr, g[_O2 + LANES:]], axis=0)


def _w_uq_local(w):
    w3 = w.reshape(MLA_HEADS // 2, 2, MLA_NOPE + MLA_ROPE, w.shape[1])
    nope = w3[:, :, :MLA_NOPE].reshape(MLA_HEADS // 2, 2 * MLA_NOPE, w.shape[1])
    rope = w3[:, :, MLA_NOPE:].reshape(MLA_HEADS // 2, 2 * MLA_ROPE, w.shape[1])
    pad = jnp.zeros((MLA_HEADS // 2, LANES - 2 * MLA_ROPE, w.shape[1]), w.dtype)
    return jnp.concatenate([nope, rope, pad], axis=1).reshape(-1, w.shape[1])


def _w_uq_grad(g):
    g3 = g.reshape(MLA_HEADS // 2, 2 * LANES, g.shape[1])
    nope = g3[:, :2 * MLA_NOPE].reshape(MLA_HEADS // 2, 2, MLA_NOPE, g.shape[1])
    rope = g3[:, LANES:LANES + 2 * MLA_ROPE].reshape(MLA_HEADS // 2, 2, MLA_ROPE, g.shape[1])
    return jnp.concatenate([nope, rope], axis=2).reshape(-1, g.shape[1])


def _w_ukv_local(w):
    w3 = w.reshape(MLA_HEADS, MLA_NOPE + MLA_V, w.shape[1])
    return jnp.concatenate([w3[:, :MLA_NOPE].reshape(-1, w.shape[1]),
                            w3[:, MLA_NOPE:].reshape(-1, w.shape[1])], axis=0)


def _w_ukv_grad(g):
    half = MLA_HEADS * MLA_NOPE
    kn = g[:half].reshape(MLA_HEADS, MLA_NOPE, g.shape[1])
    vv = g[half:].reshape(MLA_HEADS, MLA_V, g.shape[1])
    return jnp.concatenate([kn, vv], axis=1).reshape(-1, g.shape[1])


def _rope_tables(T):
    pos = jnp.arange(T, dtype=F32)
    inv_freq = ROPE_THETA ** (-jnp.arange(0, MLA_ROPE, 2, dtype=F32) / MLA_ROPE)
    ang = pos[:, None] * inv_freq[None, :]
    cos, sin = jnp.cos(ang), jnp.sin(ang)
    ones = jnp.ones((T, LANES - 2 * MLA_ROPE), F32)
    cos_k = jnp.concatenate([cos, cos, cos, cos, ones], axis=1)
    sin_k = jnp.concatenate([-sin, sin, -sin, sin, 0.0 * ones], axis=1)
    cos_q = jnp.concatenate([jnp.ones((T, LANES), F32), cos_k], axis=1)
    sin_q = jnp.concatenate([jnp.zeros((T, LANES), F32), sin_k], axis=1)
    return cos_q, sin_q, cos_k, sin_k


def _bias_diag_index():
    ell = np.arange(TOEP_W)
    return np.clip(BAND_W - ell, -REL_CLIP, REL_CLIP) + REL_CLIP


def _local_step(x, target, small, get_weights, put_grads):
    T = x.shape[0]
    cos_q, sin_q, cos_k, sin_k = _rope_tables(T)
    G = {}
    W = dict(small)

    u0 = _rms_fwd(x, W["g_mix"][0:1], name="rms_mix0")
    bias_w = _toeplitz(W["od_rel_bias"][:, _bias_diag_index()])
    W.update(get_weights("in0", (u0, bias_w)))
    proj = _mm(u0, W["w_in_t"], dims="nt", name="proj_in")
    W.update(get_weights("mix0", proj))
    c_q, c_kv = proj[:, :_O1], proj[:, _O1:_O2]
    nq = _rms_fwd(c_q, W["g_cq"], name="rms_cq")
    nkv = _rms_fwd(c_kv, W["g_ckv"], name="rms_ckv")
    qa_raw = _mm(nq, W["w_uq_t"], dims="nt", name="proj_uq")
    kv = _mm(nkv, W["w_ukv_t"], dims="nt", out_dtype=BF16, name="proj_ukv")
    kr = _rope(proj, cos_k, sin_k, COL_KR, 1, BF16, name="rope_k")
    o_a, lse = _mla_fwd(qa_raw, cos_q, sin_q, kv, kr)
    o_b, o_b32, w_b, sp_b = _sb_fwd(proj, COL_SB)
    o_ab = jnp.concatenate([o_a, o_b], axis=1)
    h1 = _mm(o_ab, W["ev_w_out"], res=x, name="out_ev")

    def ffn_fwd(h, layer):
        W.update(get_weights(f"ffn{layer}", h))
        return _ffn_fwd(h, W["g_ffn"][layer:layer + 1], W[f"w_gate_t{layer}"], W[f"w_up_t{layer}"],
                        W[f"w_down{layer}"], name=f"ffn_fwd{layer}")

    h2, u1, a0, b0 = ffn_fwd(h1, 0)

    W.update(get_weights("mix1", h2))
    u2 = _rms_fwd(h2, W["g_mix"][1:2], name="rms_mix1")
    qkv = _mm(u2, W["od_w_qkv_t"], dims="nt", out_dtype=BF16, name="proj_qkv")
    nc = C_HEADS * C_DIM
    pad = ((PAD_KEYS, 0), (0, 0))
    k_pad, v_pad = jnp.pad(qkv[:, nc:2 * nc], pad), jnp.pad(qkv[:, 2 * nc:], pad)
    o_c, p_c = _band_fwd(qkv, k_pad, v_pad, bias_w)
    h3 = _mm(o_c, W["od_w_out"], res=h2, name="out_od")
    h4, u3, a1, b1 = ffn_fwd(h3, 1)

    loss, dh, dhb, G["g_final"] = _loss_head(h4, W["g_final"], target)

    def ffn_bwd(dh, dhb, h, u, a, b, layer):
        du, g_gate, g_up, g_down = _ffn_bwd(dhb, u, a, b, W[f"w_gate_t{layer}"], W[f"w_up_t{layer}"],
                                            W[f"w_down{layer}"], name=f"ffn_bwd{layer}")
        tok = put_grads(f"ffn{layer}", {"w_gate_t": g_gate, "w_up_t": g_up, "w_down": g_down})
        return _rms_bwd(h, W["g_ffn"][layer:layer + 1] + tok[:1, :1], du, dres=dh, name=f"rms_ffn_bwd{layer}")

    dh3, dh3b, g_gffn1 = ffn_bwd(dh, dhb, h3, u3, a1, b1, 1)

    do_c = _mm(dh3b, W["od_w_out"], dims="nt", name="out_od_dx")
    g_od_out = _mm(o_c, dh3b, dims="tn", out_dtype=BF16, name="out_od_dw")
    dq_c, dk_p, dv_p, dbias_w = _band_bwd(qkv, k_pad, v_pad, p_c, do_c)
    dqkv = jnp.concatenate([dq_c, dk_p[PAD_KEYS:], dv_p[PAD_KEYS:]], axis=1)
    tok = put_grads("mix1", {"od_w_qkv_t": _mm(dqkv, u2, dims="tn", out_dtype=BF16, name="proj_qkv_dw"),
                             "od_w_out": g_od_out})
    ddiag = _toeplitz_bwd(dbias_w)
    n_far = BAND_W - REL_CLIP + 1
    G["od_rel_bias"] = jnp.concatenate(
        [jnp.zeros((C_HEADS, REL_CLIP - BAND_TQ + 1), F32), ddiag[:, n_far:][:, ::-1],
         jnp.sum(ddiag[:, :n_far], axis=1, keepdims=True)], axis=1)
    dh2, dh2b, g_gmix1 = _mm_rms_bwd(dqkv, W["od_w_qkv_t"], h2, W["g_mix"][1:2] + tok[:1, :1], dh3,
                                     name="proj_qkv_dx")

    dh1, dh1b, g_gffn0 = ffn_bwd(dh2, dh2b, h1, u1, a0, b0, 0)
    G["g_ffn"] = jnp.concatenate([g_gffn0, g_gffn1], axis=0)

    do_ab = _mm(dh1b, W["ev_w_out"], dims="nt", name="out_ev_dx")
    g0 = {"ev_w_out": _mm(o_ab, dh1b, dims="tn", out_dtype=BF16, name="out_ev_dw")}
    dqa_raw, dkn, dva, dkr = _mla_bwd(qa_raw, cos_q, sin_q, kv, kr, o_a, lse, do_ab, 0)
    g0["w_uq_t"] = _mm(dqa_raw, nq, dims="tn", name="proj_uq_dw")
    dnq = _mm(dqa_raw, W["w_uq_t"], name="proj_uq_dx")
    _, dc_q, G["g_cq"] = _rms_bwd(c_q, W["g_cq"], dnq, name="rms_cq_bwd")
    dkv = jnp.concatenate([dkn, dva], axis=1)
    g0["w_ukv_t"] = _mm(dkv, nkv, dims="tn", name="proj_ukv_dw")
    dnkv = _mm(dkv, W["w_ukv_t"], name="proj_ukv_dx")
    _, dc_kv, G["g_ckv"] = _rms_bwd(c_kv, W["g_ckv"], dnkv, name="rms_ckv_bwd")
    tok = put_grads("mix0", g0)
    dqb, dkb, dvb = _sb_bwd(proj, COL_SB, o_b32, w_b, sp_b, do_ab, MLA_HEADS // 2, tok)
    dkr_raw = _rope(dkr, cos_k, -sin_k, 0, 1, BF16, name="rope_k_bwd")
    dproj = jnp.concatenate([dc_q, dc_kv, dkr_raw, dqb, dkb, dvb], axis=1)
    tok = put_grads("in0", {"w_in_t": _mm(dproj, u0, dims="tn", name="proj_in_dw")})
    dx, _, g_gmix0 = _mm_rms_bwd(dproj, W["w_in_t"], x, W["g_mix"][0:1] + tok[:1, :1], dh1, name="proj_in_dx")
    G["g_mix"] = jnp.concatenate([g_gmix0, g_gmix1], axis=0)
    return loss[0, 0], dx, G


_BIG = ["ev_w_in", "ev_w_uq", "ev_w_ukv", "ev_w_out", "od_w_qkv", "od_w_out", "w_gate", "w_up", "w_down"]
_COL_SHARDED = {"ev_w_in", "ev_w_uq", "ev_w_ukv", "od_w_qkv", "w_gate", "w_up"}
_SMALL = ["ev_g_cq", "ev_g_ckv", "od_rel_bias", "g_mix", "g_ffn", "g_final"]
_GROUPS = {
    "in0": ["ev_w_in"],
    "mix0": ["ev_w_uq", "ev_w_ukv", "ev_w_out"],
    "ffn0": ["w_gate0", "w_up0", "w_down0"],
    "mix1": ["od_w_qkv", "od_w_out"],
    "ffn1": ["w_gate1", "w_up1", "w_down1"],
}
_GROUP_SRC = {n + str(l): (n, l) for n in ("w_gate", "w_up", "w_down") for l in (0, 1)}
_BATCHES = {"in0": ["in0"], "layer0": ["mix0", "ffn0"], "layer1": ["mix1", "ffn1"]}
_BATCH_OF = {grp: batch for batch, grps in _BATCHES.items() for grp in grps}
_SMALL_ROWS = 8
_SMALL_COLS = 1792


def _pack_small(vals):
    flat = jnp.concatenate([v.reshape(-1).astype(F32) for v in vals])
    flat = jnp.pad(flat, (0, _SMALL_ROWS * _SMALL_COLS - flat.shape[0]))
    return flat.reshape(_SMALL_ROWS, _SMALL_COLS)


def _unpack_small(packed, like):
    flat = packed.reshape(-1)
    out, off = [], 0
    for v in like:
        out.append(flat[off:off + v.size].reshape(v.shape))
        off += v.size
    return out


def kernel(x, ev_w_in, ev_g_cq, ev_w_uq, ev_g_ckv, ev_w_ukv, ev_w_out, od_w_qkv, od_rel_bias, od_w_out, g_mix, g_ffn, w_gate, w_up, w_down, g_final, loss_target, m_ev_w_in, m_ev_g_cq, m_ev_w_uq, m_ev_g_ckv, m_ev_w_ukv, m_ev_w_out, m_od_w_qkv, m_od_rel_bias, m_od_w_out, m_g_mix, m_g_ffn, m_w_gate, m_w_up, m_w_down, m_g_final, v_ev_w_in, v_ev_g_cq, v_ev_w_uq, v_ev_g_ckv, v_ev_w_ukv, v_ev_w_out, v_od_w_qkv, v_od_rel_bias, v_od_w_out, v_g_mix, v_g_ffn, v_w_gate, v_w_up, v_w_down, v_g_final):
    args = dict(locals())
    w = {n: args[n] for n in _BIG + _SMALL}
    mom = {n: args["m_" + n] for n in _BIG + _SMALL}
    var = {n: args["v_" + n] for n in _BIG + _SMALL}

    own = {}
    for grp, names in _GROUPS.items():
        for n in names:
            base, layer = _GROUP_SRC.get(n, (n, 0))
            shard = w[base][layer:layer + 1]
            own[n] = (jnp.swapaxes(shard, 1, 2) if base in _COL_SHARDED else shard).astype(BF16)
    placed = dict(zip(own, _place_own(list(own.values()), [False] * len(own), name="gather_own")))
    gather, token = {}, x[0, :8, :LANES]
    for grp, names in _GROUPS.items():
        gather[grp], token = _exchange_start([own[n] for n in names], [False] * len(names), token,
                                             name="gather_start_" + grp, lands=[placed[n] for n in names])

    def get_weights(grp, after):
        names = _GROUPS[grp]
        lands = _exchange_wait(gather[grp], token if after is None else after, name="gather_wait_" + grp)
        full = {n: l.reshape(-1, l.shape[-1]) for n, l in zip(names, lands)}
        if grp == "in0":
            return {"w_in_t": _w_in_local(full["ev_w_in"])}
        if grp == "mix0":
            return {"w_uq_t": _w_uq_local(full["ev_w_uq"]), "w_ukv_t": _w_ukv_local(full["ev_w_ukv"]),
                    "ev_w_out": full["ev_w_out"]}
        if grp == "mix1":
            return {"od_w_qkv_t": full["od_w_qkv"], "od_w_out": full["od_w_out"]}
        layer = grp[-1]
        return {"w_gate_t" + layer: full["w_gate" + layer], "w_up_t" + layer: full["w_up" + layer],
                "w_down" + layer: full["w_down" + layer]}

    scatter, pending = {}, {}

    def put_grads(grp, g):
        if grp == "in0":
            g = {"ev_w_in": _w_in_grad(g["w_in_t"])}
        elif grp == "mix0":
            g = {"ev_w_uq": _w_uq_grad(g["w_uq_t"]), "ev_w_ukv": _w_ukv_grad(g["w_ukv_t"]),
                 "ev_w_out": g["ev_w_out"]}
        elif grp == "mix1":
            g = {"od_w_qkv": g["od_w_qkv_t"], "od_w_out": g["od_w_out"]}
        else:
            layer = grp[-1]
            g = {"w_gate" + layer: g["w_gate_t"], "w_up" + layer: g["w_up_t"], "w_down" + layer: g["w_down"]}
        pending.update({n: v.reshape(N_DEV, 1, v.shape[0] // N_DEV, v.shape[1]).astype(BF16) for n, v in g.items()})
        batch = _BATCH_OF[grp]
        names = [n for gr in _BATCHES[batch] for n in _GROUPS[gr]]
        if not all(n in pending for n in names):
            return jnp.zeros((8, LANES), F32)
        send = [pending[n] for n in names]
        scatter[batch], tok = _exchange_start(send, [True] * len(names), send[0], name="scatter_start_" + batch)
        return tok

    small = {"g_cq": ev_g_cq, "g_ckv": ev_g_ckv, "od_rel_bias": od_rel_bias[0],
             "g_mix": g_mix + token[0, 0], "g_ffn": g_ffn, "g_final": g_final.reshape(1, -1)}
    loss_part, dx, G = _local_step(x[0], loss_target[0], small, get_weights, put_grads)
    g_small = _pack_small([G["g_cq"], G["g_ckv"], G["od_rel_bias"], G["g_mix"], G["g_ffn"], G["g_final"],
                           loss_part.reshape(1)])
    small_handle, _ = _exchange_start([g_small], [False], dx, name="gather_start_small")

    grads, deltas, new_m, new_v = {}, {}, {}, {}
    parts, after = {}, dx

    def wait_parts(batch, after):
        lands = _exchange_wait(scatter[batch], after, name="scatter_wait_" + batch)
        parts.update(zip([n for grp in _BATCHES[batch] for n in _GROUPS[grp]], lands))
        return lands[0]

    def adamw(n):
        col = n in _COL_SHARDED
        rows = lambda a: (jnp.swapaxes(a, 1, 2) if col else a).reshape(-1, a.shape[1 if col else 2])
        layers = [parts[n]] if n in parts else [parts[n + "0"], parts[n + "1"]]
        res = _adamw(rows(w[n]), [p.reshape(N_DEV, -1, p.shape[-1]) for p in layers], rows(mom[n]), rows(var[n]),
                     name="adamw_" + n)
        L, a1, a2 = w[n].shape
        back = lambda r: jnp.swapaxes(r.reshape(L, a2, a1), 1, 2) if col else r.reshape(L, a1, a2)
        grads[n], deltas[n], new_m[n], new_v[n] = [back(r) for r in res]
        return res[0]

    for batch in ("layer1", "layer0"):
        after = wait_parts(batch, after)
    after = wait_parts("in0", tuple(adamw(n) for n in _BIG[1:]))
    after = adamw("ev_w_in")
    small_w = [w[n] for n in _SMALL]
    small_parts = _exchange_wait(small_handle, after, name="gather_wait_small")[0]
    loss = jnp.sum(small_parts.reshape(N_DEV, -1)[:, sum(v.size for v in small_w)])
    res = _adamw(_pack_small(small_w), [small_parts], _pack_small([mom[n] for n in _SMALL]),
                 _pack_small([var[n] for n in _SMALL]), name="adamw_small")
    for d, packed in zip((grads, deltas, new_m, new_v), res):
        for n, val in zip(_SMALL, _unpack_small(packed, small_w)):
            d[n] = val

    order = ["ev_w_in", "ev_g_cq", "ev_w_uq", "ev_g_ckv", "ev_w_ukv", "ev_w_out", "od_w_qkv", "od_rel_bias",
             "od_w_out", "g_mix", "g_ffn", "w_gate", "w_up", "w_down", "g_final"]
    out = [loss, dx[None]]
    for d in (grads, deltas, new_m, new_v):
        out += [d[n] for n in order]
    return tuple(out)
```

```python
import functools

import numpy as np
import jax
import jax.numpy as jnp
from jax import lax
from jax.experimental import pallas as pl
from jax.experimental.pallas import tpu as pltpu

F32 = jnp.float32
BF16 = jnp.bfloat16

D_MODEL = 1024
CHUNK = 64
MLA_HEADS = 8
MLA_NOPE = 64
MLA_ROPE = 32
MLA_V = 64
Q_LORA = 384
KV_LORA = 256
ROPE_THETA = 10000.0
SB_HEADS = 8
SB_DIM = 64
C_HEADS = 16
C_DIM = 64
LEFT_CHUNKS = 8
REL_CLIP = 256
D_FF = 2816
RMS_EPS = 1e-6
ADAM_LR = 0.001
ADAM_B1 = 0.9
ADAM_B2 = 0.999
ADAM_EPS = 1e-08
ADAM_WD = 0.01
ADAM_STEP = 10

N_DEV = 8
LANES = 128
HEAD = 64
assert HEAD == MLA_NOPE == MLA_V == SB_DIM == C_DIM and 2 * HEAD == LANES
CHUNK_BITS = CHUNK.bit_length() - 1
assert 1 << CHUNK_BITS == CHUNK
VMEM_LIMIT = 56 * 1024 * 1024
NEG = -1e30
PAD_KEYS = LEFT_CHUNKS * CHUNK
BAND_TQ = 128
BAND_W = BAND_TQ + PAD_KEYS
TOEP_W = BAND_W + BAND_TQ

NN = (((1,), (0,)), ((), ()))
NT = (((1,), (1,)), ((), ()))
TN = (((0,), (0,)), ((), ()))


def _dot(a, b, dn):
    return lax.dot_general(a, b, dn, preferred_element_type=F32)


def _pick(dim, pref):
    if dim <= pref:
        return dim
    best = None
    for t in range(LANES, pref + 1, LANES):
        if dim % t == 0:
            best = t
    assert best is not None, (dim, pref)
    return best


def _params(sem):
    return pltpu.CompilerParams(dimension_semantics=sem, vmem_limit_bytes=VMEM_LIMIT)


def _mm(a, b, dims="nn", res=None, out_dtype=F32, name="mm"):
    if dims == "nn":
        (M, K), (K2, N) = a.shape, b.shape
    elif dims == "nt":
        (M, K), (N, K2) = a.shape, b.shape
    else:
        (K, M), (K2, N) = a.shape, b.shape
    assert K == K2, (a.shape, b.shape, dims)
    tm, tn, tk = _pick(M, 1024), _pick(N, 1152), _pick(K, 1024)
    nk = K // tk
    dn = {"nn": NN, "nt": NT, "tn": TN}[dims]
    has_res = res is not None

    def body(*refs):
        if has_res:
            a_ref, b_ref, r_ref, o_ref, acc = refs
        else:
            a_ref, b_ref, o_ref, acc = refs
        k = pl.program_id(2)

        @pl.when(k == 0)
        def _():
            acc[...] = jnp.zeros_like(acc)

        acc[...] += _dot(a_ref[...].astype(BF16), b_ref[...].astype(BF16), dn)

        @pl.when(k == nk - 1)
        def _():
            r = acc[...]
            if has_res:
                r = r + r_ref[...]
            o_ref[...] = r.astype(out_dtype)

    a_spec = (pl.BlockSpec((tk, tm), lambda i, j, k: (k, i)) if dims == "tn"
              else pl.BlockSpec((tm, tk), lambda i, j, k: (i, k)))
    b_spec = (pl.BlockSpec((tn, tk), lambda i, j, k: (j, k)) if dims == "nt"
              else pl.BlockSpec((tk, tn), lambda i, j, k: (k, j)))
    o_spec = pl.BlockSpec((tm, tn), lambda i, j, k: (i, j))
    in_specs = [a_spec, b_spec] + ([o_spec] if has_res else [])
    args = (a, b) + ((res,) if has_res else ())
    return pl.pallas_call(
        body, name=name, grid=(M // tm, N // tn, nk),
        in_specs=in_specs, out_specs=o_spec,
        out_shape=jax.ShapeDtypeStruct((M, N), out_dtype),
        scratch_shapes=[pltpu.VMEM((tm, tn), F32)],
        compiler_params=_params(("parallel", "parallel", "arbitrary")),
    )(*args)


def _rms_fwd(x, g, out_dtype=BF16, name="rms_fwd"):
    T, Fd = x.shape
    tm = _pick(T, 256)

    def body(x_ref, g_ref, o_ref):
        xv = x_ref[...]
        r = lax.rsqrt(jnp.mean(xv * xv, axis=-1, keepdims=True) + RMS_EPS)
        o_ref[...] = (xv * r * g_ref[...]).astype(out_dtype)

    return pl.pallas_call(
        body, name=name, grid=(T // tm,),
        in_specs=[pl.BlockSpec((tm, Fd), lambda i: (i, 0)), pl.BlockSpec((1, Fd), lambda i: (0, 0))],
        out_specs=pl.BlockSpec((tm, Fd), lambda i: (i, 0)),
        out_shape=jax.ShapeDtypeStruct((T, Fd), out_dtype),
        compiler_params=_params(("parallel",)),
    )(x, g)


def _rms_bwd(x, g, dy, dres=None, name="rms_bwd"):
    T, Fd = x.shape
    tm = _pick(T, 256)
    has_res = dres is not None

    def body(*refs):
        if has_res:
            x_ref, g_ref, dy_ref, r_ref, dx_ref, dxb_ref, dg_ref = refs
        else:
            x_ref, g_ref, dy_ref, dx_ref, dxb_ref, dg_ref = refs
        xv, dyv = x_ref[...], dy_ref[...]
        r = lax.rsqrt(jnp.mean(xv * xv, axis=-1, keepdims=True) + RMS_EPS)
        gdy = dyv * g_ref[...]
        dot = jnp.mean(xv * gdy, axis=-1, keepdims=True)
        dx = r * gdy - xv * (r * r * r * dot)
        if has_res:
            dx = dx + r_ref[...]
        dx_ref[...] = dx
        dxb_ref[...] = dx.astype(BF16)

        @pl.when(pl.program_id(0) == 0)
        def _():
            dg_ref[...] = jnp.zeros_like(dg_ref)

        dg_ref[...] += jnp.sum(dyv * xv * r, axis=0, keepdims=True)

    row = pl.BlockSpec((tm, Fd), lambda i: (i, 0))
    vec = pl.BlockSpec((1, Fd), lambda i: (0, 0))
    in_specs = [row, vec, row] + ([row] if has_res else [])
    args = (x, g, dy) + ((dres,) if has_res else ())
    return pl.pallas_call(
        body, name=name, grid=(T // tm,),
        in_specs=in_specs, out_specs=[row, row, vec],
        out_shape=[jax.ShapeDtypeStruct((T, Fd), F32), jax.ShapeDtypeStruct((T, Fd), BF16),
                   jax.ShapeDtypeStruct((1, Fd), F32)],
        compiler_params=_params(("arbitrary",)),
    )(*args)


def _mm_rms_bwd(a, b, x, g, dres, name="mm_rms_bwd"):
    T, K = a.shape
    Fd = b.shape[1]
    tm, tk = _pick(T, 512), _pick(K, 1024)
    nk = K // tk

    def body(a_ref, b_ref, x_ref, g_ref, r_ref, dx_ref, dxb_ref, dg_ref, acc):
        i, k = pl.program_id(0), pl.program_id(1)

        @pl.when(k == 0)
        def _():
            acc[...] = jnp.zeros_like(acc)

        @pl.when((k == 0) & (i == 0))
        def _():
            dg_ref[...] = jnp.zeros_like(dg_ref)

        acc[...] += _dot(a_ref[...].astype(BF16), b_ref[...].astype(BF16), NN)

        @pl.when(k == nk - 1)
        def _():
            xv, dyv = x_ref[...], acc[...]
            r = lax.rsqrt(jnp.mean(xv * xv, axis=-1, keepdims=True) + RMS_EPS)
            gdy = dyv * g_ref[...]
            dot = jnp.mean(xv * gdy, axis=-1, keepdims=True)
            dx = r * gdy - xv * (r * r * r * dot) + r_ref[...]
            dx_ref[...] = dx
            dxb_ref[...] = dx.astype(BF16)
            dg_ref[...] += jnp.sum(dyv * xv * r, axis=0, keepdims=True)

    row = pl.BlockSpec((tm, Fd), lambda i, k: (i, 0))
    vec = pl.BlockSpec((1, Fd), lambda i, k: (0, 0))
    return pl.pallas_call(
        body, name=name, grid=(T // tm, nk),
        in_specs=[pl.BlockSpec((tm, tk), lambda i, k: (i, k)), pl.BlockSpec((tk, Fd), lambda i, k: (k, 0)),
                  row, vec, row],
        out_specs=[row, row, vec],
        out_shape=[jax.ShapeDtypeStruct((T, Fd), F32), jax.ShapeDtypeStruct((T, Fd), BF16),
                   jax.ShapeDtypeStruct((1, Fd), F32)],
        scratch_shapes=[pltpu.VMEM((tm, Fd), F32)],
        compiler_params=_params(("arbitrary", "arbitrary")),
    )(a, b, x, g, dres)


def _latent_bwd(proj, nq, nkv, dqa, dkn, dva, dkr, cos_k, sin_k, g_cq, g_ckv, w_uq_t, w_ukv_t, name="latent_bwd"):
    T = proj.shape[0]
    tm = _pick(T, 512)
    wl = _O2

    def rms_bwd(xv, gv, dyv):
        r = lax.rsqrt(jnp.mean(xv * xv, axis=-1, keepdims=True) + RMS_EPS)
        gdy = dyv * gv
        dot = jnp.mean(xv * gdy, axis=-1, keepdims=True)
        return r * gdy - xv * (r * r * r * dot), jnp.sum(dyv * xv * r, axis=0, keepdims=True)

    def body(p_ref, nq_ref, nkv_ref, dqa_ref, dkn_ref, dva_ref, dkr_ref, c_ref, s_ref, gq_ref, gkv_ref, wq_ref, wkv_ref,
             dlat_ref, dwq_ref, dwkv_ref, dgq_ref, dgkv_ref):
        @pl.when(pl.program_id(0) == 0)
        def _():
            for ref in (dwq_ref, dwkv_ref, dgq_ref, dgkv_ref):
                ref[...] = jnp.zeros_like(ref)

        dqv = dqa_ref[...]
        dkv = jnp.concatenate([dkn_ref[...], dva_ref[...]], axis=1)
        pv = p_ref[...]
        dc_q, dgq = rms_bwd(pv[:, :_O1], gq_ref[...], _dot(dqv, wq_ref[...], NN))
        dc_kv, dgkv = rms_bwd(pv[:, _O1:], gkv_ref[...], _dot(dkv, wkv_ref[...], NN))
        dkr_raw = _rotate(dkr_ref[...], c_ref[...], -s_ref[...])
        dlat_ref[...] = jnp.concatenate([dc_q, dc_kv, dkr_raw], axis=1).astype(BF16)
        dwq_ref[...] += _dot(dqv, nq_ref[...], TN)
        dwkv_ref[...] += _dot(dkv, nkv_ref[...], TN)
        dgq_ref[...] += dgq
        dgkv_ref[...] += dgkv

    row = lambda w: pl.BlockSpec((tm, w), lambda i: (i, 0))
    const = lambda a: pl.BlockSpec(a.shape, lambda i: (0, 0))
    outs = [jax.ShapeDtypeStruct((T, wl + LANES), BF16), jax.ShapeDtypeStruct(w_uq_t.shape, F32),
            jax.ShapeDtypeStruct(w_ukv_t.shape, F32), jax.ShapeDtypeStruct(g_cq.shape, F32),
            jax.ShapeDtypeStruct(g_ckv.shape, F32)]
    return pl.pallas_call(
        body, name=name, grid=(T // tm,),
        in_specs=[row(wl), row(_O1), row(_O2 - _O1), row(dqa.shape[1]), row(dkn.shape[1]), row(dva.shape[1]),
                  row(LANES), row(LANES), row(LANES), const(g_cq), const(g_ckv), const(w_uq_t), const(w_ukv_t)],
        out_specs=[row(wl + LANES)] + [const(o) for o in outs[1:]],
        out_shape=outs,
        compiler_params=_params(("arbitrary",)),
    )(proj, nq, nkv, dqa, dkn, dva, dkr, cos_k, sin_k, g_cq, g_ckv, w_uq_t, w_ukv_t)


def _loss_head(h, g, target, name="loss_head"):
    T, Fd = h.shape
    tm = _pick(T, 256)

    def body(h_ref, g_ref, t_ref, loss_ref, dh_ref, dhb_ref, dg_ref):
        xv = h_ref[...]
        r = lax.rsqrt(jnp.mean(xv * xv, axis=-1, keepdims=True) + RMS_EPS)
        diff = xv * r * g_ref[...] - t_ref[...]
        part = 0.5 * jnp.sum(jnp.mean(diff * diff, axis=-1, keepdims=True), axis=0, keepdims=True)
        dyv = diff * (1.0 / Fd)
        gdy = dyv * g_ref[...]
        dot = jnp.mean(xv * gdy, axis=-1, keepdims=True)
        dh = r * gdy - xv * (r * r * r * dot)
        dh_ref[...] = dh
        dhb_ref[...] = dh.astype(BF16)

        @pl.when(pl.program_id(0) == 0)
        def _():
            dg_ref[...] = jnp.zeros_like(dg_ref)
            loss_ref[...] = jnp.zeros_like(loss_ref)

        dg_ref[...] += jnp.sum(dyv * xv * r, axis=0, keepdims=True)
        loss_ref[...] += jnp.broadcast_to(part, loss_ref.shape)

    row = pl.BlockSpec((tm, Fd), lambda i: (i, 0))
    vec = pl.BlockSpec((1, Fd), lambda i: (0, 0))
    return pl.pallas_call(
        body, name=name, grid=(T // tm,),
        in_specs=[row, vec, row],
        out_specs=[pl.BlockSpec((1, LANES), lambda i: (0, 0)), row, row, vec],
        out_shape=[jax.ShapeDtypeStruct((1, LANES), F32), jax.ShapeDtypeStruct((T, Fd), F32),
                   jax.ShapeDtypeStruct((T, Fd), BF16), jax.ShapeDtypeStruct((1, Fd), F32)],
        compiler_params=_params(("arbitrary",)),
    )(h, g, target)


FFN_TF = 256


def _ffn_fwd(h, g, wg_t, wu_t, wd, name="ffn_fwd"):
    T, Dm = h.shape
    Fh = wd.shape[0]
    tm = _pick(T, 1024)
    nf = Fh // FFN_TF

    def body(h_ref, g_ref, wg_ref, wu_ref, wd_ref, o_ref, u_ref, a_ref, b_ref):
        j = pl.program_id(1)

        @pl.when(j == 0)
        def _():
            xv = h_ref[...]
            r = lax.rsqrt(jnp.mean(xv * xv, axis=-1, keepdims=True) + RMS_EPS)
            u_ref[...] = (xv * r * g_ref[...]).astype(BF16)
            o_ref[...] = xv

        u = u_ref[...]
        a = _dot(u, wg_ref[...], NT).astype(BF16)
        b = _dot(u, wu_ref[...], NT).astype(BF16)
        a_ref[...] = a
        b_ref[...] = b
        af = a.astype(F32)
        s = (af * jax.nn.sigmoid(af) * b.astype(F32)).astype(BF16)
        o_ref[...] += _dot(s, wd_ref[...], NN)

    row = pl.BlockSpec((tm, Dm), lambda i, j: (i, 0))
    wblk = pl.BlockSpec((FFN_TF, Dm), lambda i, j: (j, 0))
    ablk = pl.BlockSpec((tm, FFN_TF), lambda i, j: (i, j))
    return pl.pallas_call(
        body, name=name, grid=(T // tm, nf),
        in_specs=[row, pl.BlockSpec((1, Dm), lambda i, j: (0, 0)), wblk, wblk, wblk],
        out_specs=[row, row, ablk, ablk],
        out_shape=[jax.ShapeDtypeStruct((T, Dm), F32), jax.ShapeDtypeStruct((T, Dm), BF16),
                   jax.ShapeDtypeStruct((T, Fh), BF16), jax.ShapeDtypeStruct((T, Fh), BF16)],
        compiler_params=_params(("parallel", "arbitrary")),
    )(h, g, wg_t, wu_t, wd)


def _ffn_bwd(dh, u, a, b, wg_t, wu_t, wd, name="ffn_bwd"):
    T, Dm = dh.shape
    Fh = wd.shape[0]
    nf = Fh // FFN_TF
    once = pl.Buffered(1)

    def body(dh_ref, u_ref, a_ref, b_ref, wg_ref, wu_ref, wd_ref, du_ref, dwg_ref, dwu_ref, dwd_ref):
        j = pl.program_id(0)

        @pl.when(j == 0)
        def _():
            du_ref[...] = jnp.zeros_like(du_ref)

        ds = _dot(dh_ref[...], wd_ref[...], NT)
        af, bf = a_ref[...].astype(F32), b_ref[...].astype(F32)
        sig = jax.nn.sigmoid(af)
        sa = af * sig
        dwd_ref[...] = _dot((sa * bf).astype(BF16), dh_ref[...], TN).astype(BF16)
        dab = jnp.concatenate([(ds * bf * (sig * (1.0 + af * (1.0 - sig)))).astype(BF16),
                               (ds * sa).astype(BF16)], axis=1)
        dw = _dot(dab, u_ref[...], TN)
        dwg_ref[...] = dw[:FFN_TF].astype(BF16)
        dwu_ref[...] = dw[FFN_TF:].astype(BF16)
        du_ref[...] += _dot(dab, jnp.concatenate([wg_ref[...], wu_ref[...]], axis=0), NN)

    full = lambda: pl.BlockSpec((T, Dm), lambda j: (0, 0), pipeline_mode=once)
    wblk = pl.BlockSpec((FFN_TF, Dm), lambda j: (j, 0))
    ablk = pl.BlockSpec((T, FFN_TF), lambda j: (0, j))
    return pl.pallas_call(
        body, name=name, grid=(nf,),
        in_specs=[full(), full(), ablk, ablk, wblk, wblk, wblk],
        out_specs=[pl.BlockSpec((T, Dm), lambda j: (0, 0)), wblk, wblk, wblk],
        out_shape=[jax.ShapeDtypeStruct((T, Dm), F32)] + [jax.ShapeDtypeStruct((Fh, Dm), BF16)] * 3,
        compiler_params=_params(("arbitrary",)),
    )(dh, u, a, b, wg_t, wu_t, wd)


def _rope(x, cos_t, sin_t, col0, ncols, out_dtype, name="rope"):
    T = x.shape[0]
    wt = cos_t.shape[1]
    tm = _pick(T, 256)
    nb = ncols * LANES // wt
    half = MLA_ROPE // 2

    def body(x_ref, c_ref, s_ref, o_ref):
        xv = x_ref[...].astype(F32)
        lane = lax.broadcasted_iota(jnp.int32, xv.shape, 1)
        first = (lane & (MLA_ROPE - 1)) < half
        swapped = jnp.where(first, pltpu.roll(xv, wt - half, 1), pltpu.roll(xv, half, 1))
        o_ref[...] = (xv * c_ref[...] + swapped * s_ref[...]).astype(out_dtype)

    off = col0 * LANES // wt
    return pl.pallas_call(
        body, name=name, grid=(T // tm, nb),
        in_specs=[pl.BlockSpec((tm, wt), lambda i, j: (i, j + off)),
                  pl.BlockSpec((tm, wt), lambda i, j: (i, 0)),
                  pl.BlockSpec((tm, wt), lambda i, j: (i, 0))],
        out_specs=pl.BlockSpec((tm, wt), lambda i, j: (i, j)),
        out_shape=jax.ShapeDtypeStruct((T, ncols * LANES), out_dtype),
        compiler_params=_params(("parallel", "parallel")),
    )(x, cos_t, sin_t)


ATT_TQ = 512
ATT_TK = 256


def _mla_masks(shape):
    lane = lax.broadcasted_iota(jnp.int32, shape, 1)
    m0 = (lane < HEAD) | ((lane >= LANES) & (lane < LANES + MLA_ROPE))
    m1 = ((lane >= HEAD) & (lane < LANES)) | ((lane >= LANES + MLA_ROPE) & (lane < LANES + 2 * MLA_ROPE))
    return m0, m1


def _by_twos(n, step, carry):
    carry = lax.fori_loop(0, n // 2, lambda i, c: step(2 * i + 1, step(2 * i, c)), carry)
    return lax.fori_loop(0, n % 2, lambda _, c: step(n - 1, c), carry)


def _chunk_ok(tq, tk, d):
    row = lax.broadcasted_iota(jnp.int32, (tq, tk), 0)
    col = lax.broadcasted_iota(jnp.int32, (tq, tk), 1) + d * tk
    return jnp.concatenate([(col >> CHUNK_BITS) <= (row >> CHUNK_BITS)] * 2, axis=0)


def _rotate(x, cos_t, sin_t):
    half = MLA_ROPE // 2
    lane = lax.broadcasted_iota(jnp.int32, x.shape, 1)
    first = (lane & (MLA_ROPE - 1)) < half
    swapped = jnp.where(first, pltpu.roll(x, x.shape[1] - half, 1), pltpu.roll(x, half, 1))
    return x * cos_t + swapped * sin_t


def _mla_fwd(q, cos_q, sin_q, kv, kr, name="mla_fwd"):
    T = q.shape[0]
    tq, tk = _pick(T, ATT_TQ), _pick(T, ATT_TK)
    nd = tq // tk
    npair = MLA_HEADS // 2
    scale = (MLA_NOPE + MLA_ROPE) ** -0.5

    def body(q_ref, c_ref, s_ref, kn_ref, v_ref, kr_ref, o_ref, lse_ref):
        m_idx = pl.program_id(1)
        qv = _rotate(q_ref[...], c_ref[...], s_ref[...]).astype(BF16)
        m0, m1 = _mla_masks(qv.shape)
        qs = jnp.concatenate([jnp.where(m0, qv, 0), jnp.where(m1, qv, 0)], axis=0).astype(BF16)

        def block(kb, carry, ok):
            ks = pl.ds(pl.multiple_of(kb * tk, tk), tk)
            kcat = jnp.concatenate([kn_ref[ks, :], kr_ref[ks, :]], axis=1)
            mx, l, acc = carry
            s = _dot(qs, kcat, NT) * scale
            if ok is not None:
                s = jnp.where(ok, s, NEG)
            mn = jnp.maximum(mx, jnp.max(s, axis=-1, keepdims=True))
            alpha = jnp.exp(mx - mn)
            p = jnp.exp(s - mn)
            return (mn, alpha * l + jnp.sum(p, axis=-1, keepdims=True),
                    alpha * acc + _dot(p.astype(BF16), v_ref[ks, :], NN))

        init = (jnp.full((2 * tq, 1), NEG, F32), jnp.zeros((2 * tq, 1), F32), jnp.zeros((2 * tq, LANES), F32))
        res = init
        for d in range(nd):
            res = block(m_idx * nd + d, res, _chunk_ok(tq, tk, d))
        mx, l, acc = _by_twos(m_idx * nd, lambda kb, c: block(kb, c, None), res)
        h0 = lax.broadcasted_iota(jnp.int32, (tq, LANES), 1) < HEAD
        o_ref[...] = _two_heads(acc * (1.0 / l), h0).astype(o_ref.dtype)
        lse_ref[...] = _two_heads(jnp.broadcast_to(mx + jnp.log(l), (2 * tq, LANES)), h0)

    full = lambda col: pl.BlockSpec((T, LANES), col)
    table = pl.BlockSpec((tq, 2 * LANES), lambda p, m: (m, 0))
    return pl.pallas_call(
        body, name=name, grid=(npair, T // tq),
        in_specs=[pl.BlockSpec((tq, 2 * LANES), lambda p, m: (m, p)), table, table,
                  full(lambda p, m: (0, p)), full(lambda p, m: (0, npair + p)), full(lambda p, m: (0, 0))],
        out_specs=[pl.BlockSpec((tq, LANES), lambda p, m: (m, p)),
                   pl.BlockSpec((tq, LANES), lambda p, m: (m, p))],
        out_shape=[jax.ShapeDtypeStruct((T, npair * LANES), BF16),
                   jax.ShapeDtypeStruct((T, npair * LANES), F32)],
        compiler_params=_params(("parallel", "arbitrary")),
    )(q, cos_q, sin_q, kv, kv, kr)


def _mla_bwd(q, cos_q, sin_q, kv, kr, o, lse, do, do_col0, name="mla_bwd"):
    T = q.shape[0]
    tq, tk = _pick(T, ATT_TQ), _pick(T, ATT_TK)
    nd = tq // tk
    npair = MLA_HEADS // 2
    scale = (MLA_NOPE + MLA_ROPE) ** -0.5

    def body(q_ref, c_ref, s_ref, kn_ref, v_ref, kr_ref, o_ref, lse_ref, do_ref, dq_ref, dkn_ref, dv_ref, dkr_ref,
             dkn_acc, dv_acc):
        p_idx, m_idx = pl.program_id(0), pl.program_id(1)

        @pl.when(m_idx == 0)
        def _():
            dkn_acc[...] = jnp.zeros_like(dkn_acc)
            dv_acc[...] = jnp.zeros_like(dv_acc)

        @pl.when((m_idx == 0) & (p_idx == 0))
        def _():
            dkr_ref[...] = jnp.zeros_like(dkr_ref)

        qv = _rotate(q_ref[...], c_ref[...], s_ref[...]).astype(BF16)
        m0, m1 = _mla_masks(qv.shape)
        qs = jnp.concatenate([jnp.where(m0, qv, 0), jnp.where(m1, qv, 0)], axis=0).astype(BF16)
        dov = do_ref[...].astype(F32)
        h0 = lax.broadcasted_iota(jnp.int32, (tq, LANES), 1) < HEAD
        dos32 = jnp.concatenate([jnp.where(h0, dov, 0.0), jnp.where(h0, 0.0, dov)], axis=0)
        ov = o_ref[...].astype(F32)
        delta = jnp.sum(dos32 * jnp.concatenate([ov, ov], axis=0), axis=-1, keepdims=True)
        dos = dos32.astype(BF16)
        lsev = lse_ref[...]
        lse = jnp.concatenate([lsev[:, 0:1], lsev[:, HEAD:HEAD + 1]], axis=0)

        def block(kb, dq, ok):
            ks = pl.ds(pl.multiple_of(kb * tk, tk), tk)
            kcat = jnp.concatenate([kn_ref[ks, :], kr_ref[ks, :]], axis=1)
            vv = v_ref[ks, :]
            p = jnp.exp(_dot(qs, kcat, NT) * scale - lse)
            if ok is not None:
                p = jnp.where(ok, p, 0.0)
            ds = (p * (_dot(dos, vv, NT) - delta) * scale).astype(BF16)
            dkc = _dot(ds, qs, TN)
            dkn_acc[ks, :] += dkc[:, :LANES]
            dkr_ref[ks, :] += dkc[:, LANES:]
            dv_acc[ks, :] += _dot(p.astype(BF16), dos, TN)
            return dq + _dot(ds, kcat, NN)

        dq = jnp.zeros((2 * tq, 2 * LANES), F32)
        for d in range(nd):
            dq = block(m_idx * nd + d, dq, _chunk_ok(tq, tk, d))
        dq = _by_twos(m_idx * nd, lambda kb, c: block(kb, c, None), dq)
        dq_ref[...] = _rotate(jnp.where(m0, dq[:tq], jnp.where(m1, dq[tq:], 0.0)), c_ref[...],
                              -s_ref[...]).astype(BF16)

        @pl.when(m_idx == T // tq - 1)
        def _():
            dkn_ref[...] = dkn_acc[...].astype(BF16)
            dv_ref[...] = dv_acc[...].astype(BF16)

    full = lambda col: pl.BlockSpec((T, LANES), col)
    blk = lambda col: pl.BlockSpec((tq, LANES), col)
    table = pl.BlockSpec((tq, 2 * LANES), lambda p, m: (m, 0))
    return pl.pallas_call(
        body, name=name, grid=(npair, T // tq),
        in_specs=[pl.BlockSpec((tq, 2 * LANES), lambda p, m: (m, p)), table, table,
                  full(lambda p, m: (0, p)), full(lambda p, m: (0, npair + p)), full(lambda p, m: (0, 0)),
                  blk(lambda p, m: (m, p)), blk(lambda p, m: (m, p)),
                  blk(lambda p, m: (m, do_col0 + p))],
        out_specs=[pl.BlockSpec((tq, 2 * LANES), lambda p, m: (m, p)),
                   full(lambda p, m: (0, p)), full(lambda p, m: (0, p)), full(lambda p, m: (0, 0))],
        out_shape=[jax.ShapeDtypeStruct((T, npair * 2 * LANES), BF16),
                   jax.ShapeDtypeStruct((T, npair * LANES), BF16),
                   jax.ShapeDtypeStruct((T, npair * LANES), BF16),
                   jax.ShapeDtypeStruct((T, LANES), F32)],
        scratch_shapes=[pltpu.VMEM((T, LANES), F32)] * 2,
        compiler_params=_params(("arbitrary", "arbitrary")),
    )(q, cos_q, sin_q, kv, kv, kr, o, lse, do)


def _split_dot(x, tri):
    hi = x.astype(BF16)
    lo = (x - hi.astype(F32)).astype(BF16)
    both = _dot(jnp.concatenate([hi, lo], axis=0), tri, NN)
    return both[:x.shape[0]] + both[x.shape[0]:]


def _sb_terms(qh, kk, before):
    z = _dot(qh, kk, NT)
    sp = jnp.maximum(z, 0.0) + jnp.log(1.0 + jnp.exp(-jnp.abs(z)))
    lk = -sp if before is None else jnp.where(before, -sp, 0.0)
    return z, sp, lk


def _sb_setup(q_ref, tq, tk, scale):
    qv = (q_ref[...].astype(F32) * scale).astype(BF16)
    lane = lax.broadcasted_iota(jnp.int32, (tq, LANES), 1)
    h0 = lane < HEAD
    qs = jnp.concatenate([jnp.where(h0, qv, 0), jnp.where(h0, 0, qv)], axis=0).astype(BF16)
    row = lax.broadcasted_iota(jnp.int32, (tk, tk), 0)
    col = lax.broadcasted_iota(jnp.int32, (tk, tk), 1)
    return qs, h0, row, col


def _sb_before(tq, tk, d):
    row = lax.broadcasted_iota(jnp.int32, (tq, tk), 0)
    col = lax.broadcasted_iota(jnp.int32, (tq, tk), 1) + d * tk
    return jnp.concatenate([col < row] * 2, axis=0)


def _two_heads(x, h0):
    tq = x.shape[0] // 2
    return jnp.where(h0, x[:tq], x[tq:])


def _sb_fwd(qkv, col0, name="sb_fwd"):
    T = qkv.shape[0]
    tq, tk = _pick(T, ATT_TQ), _pick(T, ATT_TK)
    nd = tq // tk
    npair = SB_HEADS // 2
    scale = SB_DIM ** -0.5

    def body(q_ref, k_ref, v_ref, o_ref, o32_ref, w_ref, sp_ref):
        m_idx = pl.program_id(1)
        qs, h0, row, col = _sb_setup(q_ref, tq, tk, scale)
        later = (row > col).astype(BF16)

        def block(kb, carry, before):
            ks = pl.ds(pl.multiple_of(kb * tk, tk), tk)
            c, acc = carry
            z, sp, lk = _sb_terms(qs, k_ref[ks, :].astype(BF16), before)
            w = jnp.exp((z - sp) + _split_dot(lk, later) + c)
            if before is not None:
                w = jnp.where(before, w, 0.0)
            wb = w.astype(BF16)
            w_ref[0, 0, kb] = wb
            sp_ref[0, 0, kb] = sp.astype(BF16)
            return (c + jnp.sum(lk, axis=-1, keepdims=True), acc + _dot(wb, v_ref[ks, :].astype(BF16), NN))

        init = (jnp.zeros((2 * tq, 1), F32), jnp.zeros((2 * tq, LANES), F32))
        res = init
        for d in reversed(range(nd)):
            res = block(m_idx * nd + d, res, _sb_before(tq, tk, d))
        res = _by_twos(m_idx * nd, lambda i, c: block(m_idx * nd - 1 - i, c, None), res)
        o = _two_heads(res[1], h0)
        o_ref[...] = o.astype(o_ref.dtype)
        o32_ref[...] = o

    full = lambda col: pl.BlockSpec((T, LANES), col)
    blk = pl.BlockSpec((tq, LANES), lambda p, m: (m, p))
    return pl.pallas_call(
        body, name=name, grid=(npair, T // tq),
        in_specs=[pl.BlockSpec((tq, LANES), lambda p, m: (m, col0 + p)),
                  full(lambda p, m: (0, col0 + npair + p)), full(lambda p, m: (0, col0 + 2 * npair + p))],
        out_specs=[blk, blk] + [pl.BlockSpec((1, 1, T // tk, 2 * tq, tk), lambda p, m: (p, m, 0, 0, 0))] * 2,
        out_shape=[jax.ShapeDtypeStruct((T, npair * LANES), BF16), jax.ShapeDtypeStruct((T, npair * LANES), F32)]
        + [jax.ShapeDtypeStruct((npair, T // tq, T // tk, 2 * tq, tk), BF16)] * 2,
        compiler_params=_params(("parallel", "arbitrary")),
    )(qkv, qkv, qkv)


def _sb_bwd(qkv, col0, o32, w_all, sp_all, do, do_col0, dep, name="sb_bwd"):
    T = qkv.shape[0]
    tq, tk = _pick(T, ATT_TQ), _pick(T, ATT_TK)
    nd = tq // tk
    npair = SB_HEADS // 2
    scale = SB_DIM ** -0.5

    def body(q_ref, k_ref, v_ref, o_ref, w_ref, sp_ref, do_ref, dep_ref, dq_ref, dk_ref, dv_ref, dk_acc, dv_acc):
        m_idx = pl.program_id(1)

        @pl.when(m_idx == 0)
        def _():
            dk_acc[...] = jnp.zeros_like(dk_acc)
            dv_acc[...] = jnp.zeros_like(dv_acc)

        qs, h0, row, col = _sb_setup(q_ref, tq, tk, scale)
        dov = do_ref[...].astype(F32)
        dos = jnp.concatenate([jnp.where(h0, dov, 0.0), jnp.where(h0, 0.0, dov)], axis=0).astype(BF16)
        ov = o_ref[...]
        etot = jnp.sum(dos.astype(F32) * jnp.concatenate([ov, ov], axis=0), axis=-1, keepdims=True)
        from_here = (row >= col).astype(BF16)

        def block(kb, carry, before):
            ks = pl.ds(pl.multiple_of(kb * tk, tk), tk)
            kk = k_ref[ks, :].astype(BF16)
            vv = v_ref[ks, :].astype(BF16)
            es, dqa = carry
            wb = w_ref[0, 0, kb]
            e = wb.astype(F32) * _dot(dos, vv, NT)
            prev = etot - (_split_dot(e, from_here) + es)
            sig_neg = jnp.exp(-sp_ref[0, 0, kb].astype(F32))
            dz = e * sig_neg - (1.0 - sig_neg) * prev
            if before is not None:
                dz = jnp.where(before, dz, 0.0)
            dzb = dz.astype(BF16)
            dk_acc[ks, :] += _dot(dzb, qs, TN)
            dv_acc[ks, :] += _dot(wb, dos, TN)
            return es + jnp.sum(e, axis=-1, keepdims=True), dqa + _dot(dzb, kk, NN)

        init = (jnp.zeros((2 * tq, 1), F32), jnp.zeros((2 * tq, LANES), F32))
        res = init
        for d in reversed(range(nd)):
            res = block(m_idx * nd + d, res, _sb_before(tq, tk, d))
        res = _by_twos(m_idx * nd, lambda i, c: block(m_idx * nd - 1 - i, c, None), res)
        dq_ref[...] = (_two_heads(res[1], h0) * scale).astype(BF16)

        @pl.when(m_idx == T // tq - 1)
        def _():
            dk_ref[...] = dk_acc[...].astype(BF16)
            dv_ref[...] = dv_acc[...].astype(BF16)

    full = lambda col: pl.BlockSpec((T, LANES), col)
    blk = lambda col: pl.BlockSpec((tq, LANES), col)
    return pl.pallas_call(
        body, name=name, grid=(npair, T // tq),
        in_specs=[blk(lambda p, m: (m, col0 + p)),
                  full(lambda p, m: (0, col0 + npair + p)), full(lambda p, m: (0, col0 + 2 * npair + p)),
                  blk(lambda p, m: (m, p)),
                  pl.BlockSpec((1, 1, T // tk, 2 * tq, tk), lambda p, m: (p, m, 0, 0, 0)),
                  pl.BlockSpec((1, 1, T // tk, 2 * tq, tk), lambda p, m: (p, m, 0, 0, 0)),
                  blk(lambda p, m: (m, do_col0 + p)), pl.BlockSpec((8, LANES), lambda p, m: (0, 0))],
        out_specs=[blk(lambda p, m: (m, p)), full(lambda p, m: (0, p)), full(lambda p, m: (0, p))],
        out_shape=[jax.ShapeDtypeStruct((T, npair * LANES), BF16)] * 3,
        scratch_shapes=[pltpu.VMEM((T, LANES), F32)] * 2,
        compiler_params=_params(("arbitrary", "arbitrary")),
    )(qkv, qkv, qkv, o32, w_all, sp_all, do, dep)


def _band_in_window():
    cq = lax.broadcasted_iota(jnp.int32, (BAND_TQ, BAND_W), 0) >> CHUNK_BITS
    ckp = lax.broadcasted_iota(jnp.int32, (BAND_TQ, BAND_W), 1) >> CHUNK_BITS
    return (ckp >= cq) & (ckp <= cq + LEFT_CHUNKS)


def _band_real(m_idx):
    j = lax.broadcasted_iota(jnp.int32, (BAND_TQ, BAND_W), 1)
    return j >= PAD_KEYS - m_idx * BAND_TQ


def _band_probs(qh, kw, bias, real, scale):
    s = jnp.where(real, _dot(qh, kw, NT) * scale + bias, NEG)
    e = jnp.exp(s - jnp.max(s, axis=-1, keepdims=True))
    return e * (1.0 / jnp.sum(e, axis=-1, keepdims=True))


BAND_SUB = 4


def _band_fwd(qkv, k_pad, v_pad, bias_w, name="band_fwd"):
    T = qkv.shape[0]
    npair = C_HEADS // 2
    scale = C_DIM ** -0.5
    rows = BAND_SUB * BAND_TQ

    def body(q_ref, k_ref, v_ref, b_ref, o_ref, p_ref):
        lane = lax.broadcasted_iota(jnp.int32, (BAND_TQ, LANES), 1)
        h0 = lane < HEAD
        bias = jnp.concatenate([b_ref[0], b_ref[1]], axis=0)
        for sub in range(BAND_SUB):
            m_idx = pl.program_id(1) * BAND_SUB + sub
            win = pl.ds(pl.multiple_of(m_idx * BAND_TQ, BAND_TQ), BAND_W)
            kw, vw = k_ref[win, :], v_ref[win, :]
            qv = q_ref[sub * BAND_TQ:(sub + 1) * BAND_TQ, :]
            qs = jnp.concatenate([jnp.where(h0, qv, 0), jnp.where(h0, 0, qv)], axis=0).astype(BF16)
            p = _band_probs(qs, kw, bias, jnp.concatenate([_band_real(m_idx)] * 2, axis=0), scale).astype(BF16)
            p_ref[0, sub] = p
            o = _two_heads(_dot(p, vw, NN), h0)
            o_ref[sub * BAND_TQ:(sub + 1) * BAND_TQ, :] = o.astype(o_ref.dtype)

    Tp = T + PAD_KEYS
    return pl.pallas_call(
        body, name=name, grid=(npair, T // rows),
        in_specs=[pl.BlockSpec((rows, LANES), lambda p, m: (m, p)),
                  pl.BlockSpec((Tp, LANES), lambda p, m: (0, p)),
                  pl.BlockSpec((Tp, LANES), lambda p, m: (0, p)),
                  pl.BlockSpec((2, BAND_TQ, BAND_W), lambda p, m: (p, 0, 0))],
        out_specs=[pl.BlockSpec((rows, LANES), lambda p, m: (m, p)),
                   pl.BlockSpec((1, BAND_SUB, 2 * BAND_TQ, BAND_W), lambda p, m: (p, m, 0, 0))],
        out_shape=[jax.ShapeDtypeStruct((T, npair * LANES), BF16),
                   jax.ShapeDtypeStruct((npair, T // BAND_TQ, 2 * BAND_TQ, BAND_W), BF16)],
        compiler_params=_params(("parallel", "arbitrary")),
    )(qkv, k_pad, v_pad, bias_w)


def _band_bwd(qkv, k_pad, v_pad, probs, do, name="band_bwd"):
    T = qkv.shape[0]
    npair = C_HEADS // 2
    scale = C_DIM ** -0.5

    rows = BAND_SUB * BAND_TQ

    def body(q_ref, k_ref, v_ref, p_ref, do_ref, dq_ref, dk_ref, dv_ref, db_ref, dk_acc, dv_acc):
        @pl.when(pl.program_id(1) == 0)
        def _():
            dk_acc[...] = jnp.zeros_like(dk_acc)
            dv_acc[...] = jnp.zeros_like(dv_acc)
            db_ref[...] = jnp.zeros_like(db_ref)

        lane = lax.broadcasted_iota(jnp.int32, (BAND_TQ, LANES), 1)
        h0 = lane < HEAD
        dbs = jnp.zeros((2 * BAND_TQ, BAND_W), F32)
        for sub in range(BAND_SUB):
            m_idx = pl.program_id(1) * BAND_SUB + sub
            win = pl.ds(pl.multiple_of(m_idx * BAND_TQ, BAND_TQ), BAND_W)
            kw, vw = k_ref[win, :], v_ref[win, :]
            qv = q_ref[sub * BAND_TQ:(sub + 1) * BAND_TQ, :]
            dov = do_ref[sub * BAND_TQ:(sub + 1) * BAND_TQ, :].astype(F32)
            qs = jnp.concatenate([jnp.where(h0, qv, 0), jnp.where(h0, 0, qv)], axis=0).astype(BF16)
            dos = jnp.concatenate([jnp.where(h0, dov, 0.0), jnp.where(h0, 0.0, dov)], axis=0).astype(BF16)
            pb = p_ref[0, sub]
            p = pb.astype(F32)
            dp = _dot(dos, vw, NT)
            dsb = p * (dp - jnp.sum(p * dp, axis=-1, keepdims=True))
            dbs = dbs + dsb
            dsq = (dsb * scale).astype(BF16)
            dq_ref[sub * BAND_TQ:(sub + 1) * BAND_TQ, :] = _two_heads(_dot(dsq, kw, NN), h0).astype(BF16)
            dk_acc[win, :] += _dot(dsq, qs, TN)
            dv_acc[win, :] += _dot(pb, dos, TN)
        db_ref[0] += dbs[:BAND_TQ]
        db_ref[1] += dbs[BAND_TQ:]

        @pl.when(pl.program_id(1) == T // rows - 1)
        def _():
            dk_ref[...] = dk_acc[...].astype(BF16)
            dv_ref[...] = dv_acc[...].astype(BF16)

    Tp = T + PAD_KEYS
    blk = lambda col: pl.BlockSpec((rows, LANES), col)
    full = pl.BlockSpec((Tp, LANES), lambda p, m: (0, p))
    bias = pl.BlockSpec((2, BAND_TQ, BAND_W), lambda p, m: (p, 0, 0))
    prob = pl.BlockSpec((1, BAND_SUB, 2 * BAND_TQ, BAND_W), lambda p, m: (p, m, 0, 0))
    return pl.pallas_call(
        body, name=name, grid=(npair, T // rows),
        in_specs=[blk(lambda p, m: (m, p)), full, full, prob, blk(lambda p, m: (m, p))],
        out_specs=[blk(lambda p, m: (m, p)), full, full, bias],
        out_shape=[jax.ShapeDtypeStruct((T, npair * LANES), BF16),
                   jax.ShapeDtypeStruct((Tp, npair * LANES), BF16),
                   jax.ShapeDtypeStruct((Tp, npair * LANES), BF16),
                   jax.ShapeDtypeStruct((C_HEADS, BAND_TQ, BAND_W), F32)],
        scratch_shapes=[pltpu.VMEM((Tp, LANES), F32)] * 2,
        compiler_params=_params(("arbitrary", "arbitrary")),
    )(qkv, k_pad, v_pad, probs, do)


def _skew_bits(x, left):
    w = x.shape[1]
    row = lax.broadcasted_iota(jnp.int32, x.shape, 0)
    for b in range(BAND_TQ.bit_length() - 1):
        amt = (w - (1 << b)) if left else (1 << b)
        x = jnp.where((row >> b) & 1 == 1, pltpu.roll(x, amt, 1), x)
    return x


def _toeplitz(diag, name="toeplitz"):
    H = diag.shape[0]

    def body(d_ref, o_ref):
        x = jnp.broadcast_to(d_ref[0], (BAND_TQ, TOEP_W))
        o_ref[0] = jnp.where(_band_in_window(), _skew_bits(x, left=False)[:, BAND_TQ:], NEG)

    return pl.pallas_call(
        body, name=name, grid=(H,),
        in_specs=[pl.BlockSpec((1, 1, TOEP_W), lambda h: (h, 0, 0))],
        out_specs=pl.BlockSpec((1, BAND_TQ, BAND_W), lambda h: (h, 0, 0)),
        out_shape=jax.ShapeDtypeStruct((H, BAND_TQ, BAND_W), F32),
        compiler_params=_params(("parallel",)),
    )(diag.reshape(H, 1, TOEP_W))


def _toeplitz_bwd(dbias, name="toeplitz_bwd"):
    H = dbias.shape[0]

    def body(d_ref, o_ref):
        x = jnp.concatenate([jnp.zeros((BAND_TQ, BAND_TQ), F32), d_ref[0]], axis=1)
        o_ref[0] = jnp.sum(_skew_bits(x, left=True), axis=0, keepdims=True)

    return pl.pallas_call(
        body, name=name, grid=(H,),
        in_specs=[pl.BlockSpec((1, BAND_TQ, BAND_W), lambda h: (h, 0, 0))],
        out_specs=pl.BlockSpec((1, 1, TOEP_W), lambda h: (h, 0, 0)),
        out_shape=jax.ShapeDtypeStruct((H, 1, TOEP_W), F32),
        compiler_params=_params(("parallel",)),
    )(dbias).reshape(H, TOEP_W)


_HBM = pl.BlockSpec(memory_space=pltpu.HBM)
_SEM = pl.BlockSpec(memory_space=pltpu.SEMAPHORE)
_EFFECT = pltpu.SideEffectType.DATAFLOW_SIDE_EFFECTING


def _peers():
    x, y, c = lax.axis_index("x"), lax.axis_index("y"), lax.axis_index("c")
    out = []
    for k in range(1, N_DEV):
        peer = (1 - x if (k >> 2) & 1 else x, 1 - y if (k >> 1) & 1 else y, 1 - c if k & 1 else c)
        out.append((peer, 4 * peer[0] + 2 * peer[1] + peer[2]))
    return 4 * x + 2 * y + c, out


def _split_copies(ins, lands, scatter, send_sem, recv_sem, arriving):
    me, peers = _peers()
    out = []
    for a in range(len(ins)):
        for peer, idx in peers:
            out.append(pltpu.make_async_remote_copy(
                src_ref=ins[a].at[idx] if scatter[a] else ins[a],
                dst_ref=lands[a].at[idx if arriving else me], send_sem=send_sem, recv_sem=recv_sem,
                device_id=peer, device_id_type=pl.DeviceIdType.MESH))
    return out


def _landing_zones(arrays, scatter):
    return [lax.empty((N_DEV,) + (a.shape[1:] if s else a.shape), a.dtype) for a, s in zip(arrays, scatter)]


def _place_own(arrays, scatter, name):
    n = len(arrays)
    lands = _landing_zones(arrays, scatter)
    me = (4 * lax.axis_index("x") + 2 * lax.axis_index("y") + lax.axis_index("c")).astype(jnp.int32).reshape(1)

    def body(me_ref, *refs):
        for a in range(n):
            refs[2 * n + a][...] = refs[a][...].reshape(refs[2 * n + a].shape)

    def row_spec(shape):
        zeros = (0,) * (len(shape) - 1)
        return pl.BlockSpec((1,) + tuple(shape[1:]), lambda i, me_ref: (me_ref[0],) + zeros)

    in_specs = [row_spec(a.shape) if s else pl.BlockSpec(a.shape, lambda i, me_ref, nd=a.ndim: (0,) * nd)
                for a, s in zip(arrays, scatter)]
    return pl.pallas_call(
        body, name=name,
        out_shape=[jax.ShapeDtypeStruct(l.shape, l.dtype) for l in lands],
        grid_spec=pltpu.PrefetchScalarGridSpec(
            num_scalar_prefetch=1, grid=(1,),
            in_specs=in_specs + [pl.BlockSpec(memory_space=pl.ANY)] * n,
            out_specs=[row_spec(l.shape) for l in lands]),
        input_output_aliases={1 + n + i: i for i in range(n)},
        compiler_params=_params(("arbitrary",)),
    )(me, *arrays, *lands)


def _exchange_start(arrays, scatter, after, name, lands=None):
    n = len(arrays)
    if lands is None:
        lands = list(_place_own(arrays, scatter, name=name.replace("_start_", "_own_")))

    def body(*refs):
        ins, lnd = refs[:n], refs[n:2 * n]
        send_sem, recv_sem = refs[2 * n + 1:2 * n + 3]
        token = refs[-1]
        for cp in _split_copies(ins, lnd, scatter, send_sem, recv_sem, arriving=False):
            cp.start()
        token[...] = jnp.zeros_like(token)

    hbm = lambda a: pltpu.HBM(a.shape, a.dtype)
    out = pl.pallas_call(
        body, name=name,
        out_shape=(pltpu.SemaphoreType.DMA(()), pltpu.SemaphoreType.DMA(()),
                   *[hbm(a) for a in arrays], *[hbm(a) for a in lands],
                   jax.ShapeDtypeStruct((8, LANES), F32)),
        in_specs=[_HBM] * (2 * n) + [pl.BlockSpec(memory_space=pl.ANY)],
        out_specs=(_SEM, _SEM, *([_HBM] * (2 * n)), pl.BlockSpec(memory_space=pltpu.VMEM)),
        input_output_aliases={i: 2 + i for i in range(2 * n)},
        compiler_params=pltpu.CompilerParams(has_side_effects=_EFFECT),
    )(*[pltpu.with_memory_space_constraint(a, pltpu.HBM) for a in list(arrays) + lands], after)
    return (out[0], out[1], list(out[2:2 + n]), list(out[2 + n:2 + 2 * n]), tuple(scatter)), out[-1]


def _exchange_wait(handle, after, name):
    send_sem, recv_sem, ins, lands, scatter = handle
    n = len(ins)
    after = after if isinstance(after, tuple) else (after,)

    def body(*refs):
        i_ref, l_ref = refs[:n], refs[n:2 * n]
        s_sem, r_sem = refs[2 * n:2 * n + 2]
        for cp in _split_copies(i_ref, l_ref, scatter, s_sem, r_sem, arriving=False):
            cp.wait_send()
        for cp in _split_copies(i_ref, l_ref, scatter, s_sem, r_sem, arriving=True):
            cp.wait_recv()

    hbm = lambda a: pltpu.HBM(a.shape, a.dtype)
    out = pl.pallas_call(
        body, name=name,
        out_shape=tuple(hbm(a) for a in ins + lands),
        in_specs=[_HBM] * (2 * n) + [_SEM, _SEM] + [pl.BlockSpec(memory_space=pl.ANY)] * len(after),
        out_specs=tuple([_HBM] * (2 * n)),
        input_output_aliases={i: i for i in range(2 * n)},
        compiler_params=pltpu.CompilerParams(has_side_effects=_EFFECT),
    )(*ins, *lands, send_sem, recv_sem, *after)
    return list(out[n:])


def _adamw(w, parts, m, v, name="adamw"):
    R, C = w.shape
    L = len(parts)
    rl = R // L
    tr = max([t for t in range(16, 513, 16) if rl % t == 0], default=rl)
    nb = rl // tr
    c1 = 1.0 - ADAM_B1 ** ADAM_STEP
    c2 = 1.0 - ADAM_B2 ** ADAM_STEP

    def body(*refs):
        w_ref, p_refs, (m_ref, v_ref, g_ref, d_ref, nm_ref, nv_ref) = refs[0], refs[1:1 + L], refs[1 + L:]
        g = None
        for j, p_ref in enumerate(p_refs):
            gj = p_ref[0].astype(F32)
            for i in range(1, N_DEV):
                gj = gj + p_ref[i].astype(F32)
            g = gj if g is None else jnp.where(pl.program_id(0) == j, gj, g)
        nm = ADAM_B1 * m_ref[...] + (1.0 - ADAM_B1) * g
        nv = ADAM_B2 * v_ref[...] + (1.0 - ADAM_B2) * (g * g)
        g_ref[...] = g
        nm_ref[...] = nm
        nv_ref[...] = nv
        d_ref[...] = -ADAM_LR * ((nm / c1) / (jnp.sqrt(nv / c2) + ADAM_EPS) + ADAM_WD * w_ref[...])

    blk = pl.BlockSpec((tr, C), lambda l, i: (l * nb + i, 0))
    part = lambda j: pl.BlockSpec((N_DEV, tr, C), lambda l, i: (0, jnp.where(l == j, i, 0), 0))
    return pl.pallas_call(
        body, name=name, grid=(L, nb),
        in_specs=[blk] + [part(j) for j in range(L)] + [blk, blk],
        out_specs=[blk] * 4,
        out_shape=[jax.ShapeDtypeStruct((R, C), F32)] * 4,
        compiler_params=_params(("arbitrary", "arbitrary")),
    )(w, *parts, m, v)


_O1 = Q_LORA
_O2 = _O1 + KV_LORA
_O3 = _O2 + MLA_ROPE
_NB = SB_HEADS * SB_DIM
IN_W = _O2 + LANES + 3 * _NB
COL_KR = _O2 // LANES
COL_SB = COL_KR + 1


def _w_in_local(w):
    kr = w[_O2:_O3]
    pad = jnp.zeros((LANES - 2 * MLA_ROPE, w.shape[1]), w.dtype)
    return jnp.concatenate([w[:_O2], kr, kr, pad, w[_O3:]], axis=0)


def _w_in_grad(g):
    kr = (g[_O2:_O2 + MLA_ROPE].astype(F32) + g[_O2 + MLA_ROPE:_O2 + 2 * MLA_ROPE].astype(F32)).astype(g.dtype)
    return jnp.concatenate([g[:_O2], kr, g[_O2 + LANES:]], axis=0)


def _w_uq_local(w):
    w3 = w.reshape(MLA_HEADS // 2, 2, MLA_NOPE + MLA_ROPE, w.shape[1])
    nope = w3[:, :, :MLA_NOPE].reshape(MLA_HEADS // 2, 2 * MLA_NOPE, w.shape[1])
    rope = w3[:, :, MLA_NOPE:].reshape(MLA_HEADS // 2, 2 * MLA_ROPE, w.shape[1])
    pad = jnp.zeros((MLA_HEADS // 2, LANES - 2 * MLA_ROPE, w.shape[1]), w.dtype)
    return jnp.concatenate([nope, rope, pad], axis=1).reshape(-1, w.shape[1])


def _w_uq_grad(g):
    g3 = g.reshape(MLA_HEADS // 2, 2 * LANES, g.shape[1])
    nope = g3[:, :2 * MLA_NOPE].reshape(MLA_HEADS // 2, 2, MLA_NOPE, g.shape[1])
    rope = g3[:, LANES:LANES + 2 * MLA_ROPE].reshape(MLA_HEADS // 2, 2, MLA_ROPE, g.shape[1])
    return jnp.concatenate([nope, rope], axis=2).reshape(-1, g.shape[1])


def _w_ukv_local(w):
    w3 = w.reshape(MLA_HEADS, MLA_NOPE + MLA_V, w.shape[1])
    return jnp.concatenate([w3[:, :MLA_NOPE].reshape(-1, w.shape[1]),
                            w3[:, MLA_NOPE:].reshape(-1, w.shape[1])], axis=0)


def _w_ukv_grad(g):
    half = MLA_HEADS * MLA_NOPE
    kn = g[:half].reshape(MLA_HEADS, MLA_NOPE, g.shape[1])
    vv = g[half:].reshape(MLA_HEADS, MLA_V, g.shape[1])
    return jnp.concatenate([kn, vv], axis=1).reshape(-1, g.shape[1])


def _rope_tables(T):
    pos = jnp.arange(T, dtype=F32)
    inv_freq = ROPE_THETA ** (-jnp.arange(0, MLA_ROPE, 2, dtype=F32) / MLA_ROPE)
    ang = pos[:, None] * inv_freq[None, :]
    cos, sin = jnp.cos(ang), jnp.sin(ang)
    ones = jnp.ones((T, LANES - 2 * MLA_ROPE), F32)
    cos_k = jnp.concatenate([cos, cos, cos, cos, ones], axis=1)
    sin_k = jnp.concatenate([-sin, sin, -sin, sin, 0.0 * ones], axis=1)
    cos_q = jnp.concatenate([jnp.ones((T, LANES), F32), cos_k], axis=1)
    sin_q = jnp.concatenate([jnp.zeros((T, LANES), F32), sin_k], axis=1)
    return cos_q, sin_q, cos_k, sin_k


def _bias_diag_index():
    ell = np.arange(TOEP_W)
    return np.clip(BAND_W - ell, -REL_CLIP, REL_CLIP) + REL_CLIP


def _local_step(x, target, small, get_weights, put_grads):
    T = x.shape[0]
    cos_q, sin_q, cos_k, sin_k = _rope_tables(T)
    G = {}
    W = dict(small)

    u0 = _rms_fwd(x, W["g_mix"][0:1], name="rms_mix0")
    bias_w = _toeplitz(W["od_rel_bias"][:, _bias_diag_index()])
    W.update(get_weights("in0", (u0, bias_w)))
    proj = _mm(u0, W["w_in_t"], dims="nt", name="proj_in")
    W.update(get_weights("mix0", proj))
    c_q, c_kv = proj[:, :_O1], proj[:, _O1:_O2]
    nq = _rms_fwd(c_q, W["g_cq"], name="rms_cq")
    nkv = _rms_fwd(c_kv, W["g_ckv"], name="rms_ckv")
    qa_raw = _mm(nq, W["w_uq_t"], dims="nt", name="proj_uq")
    kv = _mm(nkv, W["w_ukv_t"], dims="nt", out_dtype=BF16, name="proj_ukv")
    kr = _rope(proj, cos_k, sin_k, COL_KR, 1, BF16, name="rope_k")
    o_a, lse = _mla_fwd(qa_raw, cos_q, sin_q, kv, kr)
    o_b, o_b32, w_b, sp_b = _sb_fwd(proj, COL_SB)
    o_ab = jnp.concatenate([o_a, o_b], axis=1)
    h1 = _mm(o_ab, W["ev_w_out"], res=x, name="out_ev")

    def ffn_fwd(h, layer):
        W.update(get_weights(f"ffn{layer}", h))
        return _ffn_fwd(h, W["g_ffn"][layer:layer + 1], W[f"w_gate_t{layer}"], W[f"w_up_t{layer}"],
                        W[f"w_down{layer}"], name=f"ffn_fwd{layer}")

    h2, u1, a0, b0 = ffn_fwd(h1, 0)

    W.update(get_weights("mix1", h2))
    u2 = _rms_fwd(h2, W["g_mix"][1:2], name="rms_mix1")
    qkv = _mm(u2, W["od_w_qkv_t"], dims="nt", out_dtype=BF16, name="proj_qkv")
    nc = C_HEADS * C_DIM
    pad = ((PAD_KEYS, 0), (0, 0))
    k_pad, v_pad = jnp.pad(qkv[:, nc:2 * nc], pad), jnp.pad(qkv[:, 2 * nc:], pad)
    o_c, p_c = _band_fwd(qkv, k_pad, v_pad, bias_w)
    h3 = _mm(o_c, W["od_w_out"], res=h2, name="out_od")
    h4, u3, a1, b1 = ffn_fwd(h3, 1)

    loss, dh, dhb, G["g_final"] = _loss_head(h4, W["g_final"], target)

    def ffn_bwd(dh, dhb, h, u, a, b, layer):
        du, g_gate, g_up, g_down = _ffn_bwd(dhb, u, a, b, W[f"w_gate_t{layer}"], W[f"w_up_t{layer}"],
                                            W[f"w_down{layer}"], name=f"ffn_bwd{layer}")
        tok = put_grads(f"ffn{layer}", {"w_gate_t": g_gate, "w_up_t": g_up, "w_down": g_down})
        return _rms_bwd(h, W["g_ffn"][layer:layer + 1] + tok[:1, :1], du, dres=dh, name=f"rms_ffn_bwd{layer}")

    dh3, dh3b, g_gffn1 = ffn_bwd(dh, dhb, h3, u3, a1, b1, 1)

    do_c = _mm(dh3b, W["od_w_out"], dims="nt", name="out_od_dx")
    g_od_out = _mm(o_c, dh3b, dims="tn", out_dtype=BF16, name="out_od_dw")
    dq_c, dk_p, dv_p, dbias_w = _band_bwd(qkv, k_pad, v_pad, p_c, do_c)
    dqkv = jnp.concatenate([dq_c, dk_p[PAD_KEYS:], dv_p[PAD_KEYS:]], axis=1)
    tok = put_grads("mix1", {"od_w_qkv_t": _mm(dqkv, u2, dims="tn", out_dtype=BF16, name="proj_qkv_dw"),
                             "od_w_out": g_od_out})
    ddiag = _toeplitz_bwd(dbias_w)
    n_far = BAND_W - REL_CLIP + 1
    G["od_rel_bias"] = jnp.concatenate(
        [jnp.zeros((C_HEADS, REL_CLIP - BAND_TQ + 1), F32), ddiag[:, n_far:][:, ::-1],
         jnp.sum(ddiag[:, :n_far], axis=1, keepdims=True)], axis=1)
    dh2, dh2b, g_gmix1 = _mm_rms_bwd(dqkv, W["od_w_qkv_t"], h2, W["g_mix"][1:2] + tok[:1, :1], dh3,
                                     name="proj_qkv_dx")

    dh1, dh1b, g_gffn0 = ffn_bwd(dh2, dh2b, h1, u1, a0, b0, 0)
    G["g_ffn"] = jnp.concatenate([g_gffn0, g_gffn1], axis=0)

    do_ab = _mm(dh1b, W["ev_w_out"], dims="nt", name="out_ev_dx")
    g0 = {"ev_w_out": _mm(o_ab, dh1b, dims="tn", out_dtype=BF16, name="out_ev_dw")}
    dqa_raw, dkn, dva, dkr = _mla_bwd(qa_raw, cos_q, sin_q, kv, kr, o_a, lse, do_ab, 0)
    dlat, g0["w_uq_t"], g0["w_ukv_t"], G["g_cq"], G["g_ckv"] = _latent_bwd(
        proj, nq, nkv, dqa_raw, dkn, dva, dkr, cos_k, sin_k, W["g_cq"], W["g_ckv"], W["w_uq_t"], W["w_ukv_t"])
    tok = put_grads("mix0", g0)
    dqb, dkb, dvb = _sb_bwd(proj, COL_SB, o_b32, w_b, sp_b, do_ab, MLA_HEADS // 2, tok)
    dproj = jnp.concatenate([dlat, dqb, dkb, dvb], axis=1)
    tok = put_grads("in0", {"w_in_t": _mm(dproj, u0, dims="tn", name="proj_in_dw")})
    dx, _, g_gmix0 = _mm_rms_bwd(dproj, W["w_in_t"], x, W["g_mix"][0:1] + tok[:1, :1], dh1, name="proj_in_dx")
    G["g_mix"] = jnp.concatenate([g_gmix0, g_gmix1], axis=0)
    return loss[0, 0], dx, G


_BIG = ["ev_w_in", "ev_w_uq", "ev_w_ukv", "ev_w_out", "od_w_qkv", "od_w_out", "w_gate", "w_up", "w_down"]
_COL_SHARDED = {"ev_w_in", "ev_w_uq", "ev_w_ukv", "od_w_qkv", "w_gate", "w_up"}
_SMALL = ["ev_g_cq", "ev_g_ckv", "od_rel_bias", "g_mix", "g_ffn", "g_final"]
_GROUPS = {
    "in0": ["ev_w_in"],
    "mix0": ["ev_w_uq", "ev_w_ukv", "ev_w_out"],
    "ffn0": ["w_gate0", "w_up0", "w_down0"],
    "mix1": ["od_w_qkv", "od_w_out"],
    "ffn1": ["w_gate1", "w_up1", "w_down1"],
}
_GROUP_SRC = {n + str(l): (n, l) for n in ("w_gate", "w_up", "w_down") for l in (0, 1)}
_BATCHES = {"in0": ["in0"], "layer0": ["mix0", "ffn0"], "layer1": ["mix1", "ffn1"]}
_BATCH_OF = {grp: batch for batch, grps in _BATCHES.items() for grp in grps}
_SMALL_ROWS = 8
_SMALL_COLS = 1792


def _pack_small(vals):
    flat = jnp.concatenate([v.reshape(-1).astype(F32) for v in vals])
    flat = jnp.pad(flat, (0, _SMALL_ROWS * _SMALL_COLS - flat.shape[0]))
    return flat.reshape(_SMALL_ROWS, _SMALL_COLS)


def _unpack_small(packed, like):
    flat = packed.reshape(-1)
    out, off = [], 0
    for v in like:
        out.append(flat[off:off + v.size].reshape(v.shape))
        off += v.size
    return out


def kernel(x, ev_w_in, ev_g_cq, ev_w_uq, ev_g_ckv, ev_w_ukv, ev_w_out, od_w_qkv, od_rel_bias, od_w_out, g_mix, g_ffn, w_gate, w_up, w_down, g_final, loss_target, m_ev_w_in, m_ev_g_cq, m_ev_w_uq, m_ev_g_ckv, m_ev_w_ukv, m_ev_w_out, m_od_w_qkv, m_od_rel_bias, m_od_w_out, m_g_mix, m_g_ffn, m_w_gate, m_w_up, m_w_down, m_g_final, v_ev_w_in, v_ev_g_cq, v_ev_w_uq, v_ev_g_ckv, v_ev_w_ukv, v_ev_w_out, v_od_w_qkv, v_od_rel_bias, v_od_w_out, v_g_mix, v_g_ffn, v_w_gate, v_w_up, v_w_down, v_g_final):
    args = dict(locals())
    w = {n: args[n] for n in _BIG + _SMALL}
    mom = {n: args["m_" + n] for n in _BIG + _SMALL}
    var = {n: args["v_" + n] for n in _BIG + _SMALL}

    own = {}
    for grp, names in _GROUPS.items():
        for n in names:
            base, layer = _GROUP_SRC.get(n, (n, 0))
            shard = w[base][layer:layer + 1]
            own[n] = (jnp.swapaxes(shard, 1, 2) if base in _COL_SHARDED else shard).astype(BF16)
    placed = dict(zip(own, _place_own(list(own.values()), [False] * len(own), name="gather_own")))
    gather, token = {}, x[0, :8, :LANES]
    for grp, names in _GROUPS.items():
        gather[grp], token = _exchange_start([own[n] for n in names], [False] * len(names), token,
                                             name="gather_start_" + grp, lands=[placed[n] for n in names])

    def get_weights(grp, after):
        names = _GROUPS[grp]
        lands = _exchange_wait(gather[grp], token if after is None else after, name="gather_wait_" + grp)
        full = {n: l.reshape(-1, l.shape[-1]) for n, l in zip(names, lands)}
        if grp == "in0":
            return {"w_in_t": _w_in_local(full["ev_w_in"])}
        if grp == "mix0":
            return {"w_uq_t": _w_uq_local(full["ev_w_uq"]), "w_ukv_t": _w_ukv_local(full["ev_w_ukv"]),
                    "ev_w_out": full["ev_w_out"]}
        if grp == "mix1":
            return {"od_w_qkv_t": full["od_w_qkv"], "od_w_out": full["od_w_out"]}
        layer = grp[-1]
        return {"w_gate_t" + layer: full["w_gate" + layer], "w_up_t" + layer: full["w_up" + layer],
                "w_down" + layer: full["w_down" + layer]}

    scatter, pending = {}, {}

    def put_grads(grp, g):
        if grp == "in0":
            g = {"ev_w_in": _w_in_grad(g["w_in_t"])}
        elif grp == "mix0":
            g = {"ev_w_uq": _w_uq_grad(g["w_uq_t"]), "ev_w_ukv": _w_ukv_grad(g["w_ukv_t"]),
                 "ev_w_out": g["ev_w_out"]}
        elif grp == "mix1":
            g = {"od_w_qkv": g["od_w_qkv_t"], "od_w_out": g["od_w_out"]}
        else:
            layer = grp[-1]
            g = {"w_gate" + layer: g["w_gate_t"], "w_up" + layer: g["w_up_t"], "w_down" + layer: g["w_down"]}
        pending.update({n: v.reshape(N_DEV, 1, v.shape[0] // N_DEV, v.shape[1]).astype(BF16) for n, v in g.items()})
        batch = _BATCH_OF[grp]
        names = [n for gr in _BATCHES[batch] for n in _GROUPS[gr]]
        if not all(n in pending for n in names):
            return jnp.zeros((8, LANES), F32)
        send = [pending[n] for n in names]
        scatter[batch], tok = _exchange_start(send, [True] * len(names), send[0], name="scatter_start_" + batch)
        return tok

    small = {"g_cq": ev_g_cq, "g_ckv": ev_g_ckv, "od_rel_bias": od_rel_bias[0],
             "g_mix": g_mix + token[0, 0], "g_ffn": g_ffn, "g_final": g_final.reshape(1, -1)}
    loss_part, dx, G = _local_step(x[0], loss_target[0], small, get_weights, put_grads)
    g_small = _pack_small([G["g_cq"], G["g_ckv"], G["od_rel_bias"], G["g_mix"], G["g_ffn"], G["g_final"],
                           loss_part.reshape(1)])
    small_handle, _ = _exchange_start([g_small], [False], dx, name="gather_start_small")

    grads, deltas, new_m, new_v = {}, {}, {}, {}
    parts, after = {}, dx

    def wait_parts(batch, after):
        lands = _exchange_wait(scatter[batch], after, name="scatter_wait_" + batch)
        parts.update(zip([n for grp in _BATCHES[batch] for n in _GROUPS[grp]], lands))
        return lands[0]

    def adamw(n):
        col = n in _COL_SHARDED
        rows = lambda a: (jnp.swapaxes(a, 1, 2) if col else a).reshape(-1, a.shape[1 if col else 2])
        layers = [parts[n]] if n in parts else [parts[n + "0"], parts[n + "1"]]
        res = _adamw(rows(w[n]), [p.reshape(N_DEV, -1, p.shape[-1]) for p in layers], rows(mom[n]), rows(var[n]),
                     name="adamw_" + n)
        L, a1, a2 = w[n].shape
        back = lambda r: jnp.swapaxes(r.reshape(L, a2, a1), 1, 2) if col else r.reshape(L, a1, a2)
        grads[n], deltas[n], new_m[n], new_v[n] = [back(r) for r in res]
        return res[0]

    for batch in ("layer1", "layer0"):
        after = wait_parts(batch, after)
    after = wait_parts("in0", tuple(adamw(n) for n in _BIG[1:]))
    after = adamw("ev_w_in")
    small_w = [w[n] for n in _SMALL]
    small_parts = _exchange_wait(small_handle, after, name="gather_wait_small")[0]
    loss = jnp.sum(small_parts.reshape(N_DEV, -1)[:, sum(v.size for v in small_w)])
    res = _adamw(_pack_small(small_w), [small_parts], _pack_small([mom[n] for n in _SMALL]),
                 _pack_small([var[n] for n in _SMALL]), name="adamw_small")
    for d, packed in zip((grads, deltas, new_m, new_v), res):
        for n, val in zip(_SMALL, _unpack_small(packed, small_w)):
            d[n] = val

    order = ["ev_w_in", "ev_g_cq", "ev_w_uq", "ev_g_ckv", "ev_w_ukv", "ev_w_out", "od_w_qkv", "od_rel_bias",
             "od_w_out", "g_mix", "g_ffn", "w_gate", "w_up", "w_down", "g_final"]
    out = [loss, dx[None]]
    for d in (grads, deltas, new_m, new_v):
        out += [d[n] for n in order]
    return tuple(out)
```

```python
import functools

import numpy as np
import jax
import jax.numpy as jnp
from jax import lax
from jax.experimental import pallas as pl
from jax.experimental.pallas import tpu as pltpu

F32 = jnp.float32
BF16 = jnp.bfloat16

D_MODEL = 1024
CHUNK = 64
MLA_HEADS = 8
MLA_NOPE = 64
MLA_ROPE = 32
MLA_V = 64
Q_LORA = 384
KV_LORA = 256
ROPE_THETA = 10000.0
SB_HEADS = 8
SB_DIM = 64
C_HEADS = 16
C_DIM = 64
LEFT_CHUNKS = 8
REL_CLIP = 256
D_FF = 2816
RMS_EPS = 1e-6
ADAM_LR = 0.001
ADAM_B1 = 0.9
ADAM_B2 = 0.999
ADAM_EPS = 1e-08
ADAM_WD = 0.01
ADAM_STEP = 10

N_DEV = 8
LANES = 128
HEAD = 64
assert HEAD == MLA_NOPE == MLA_V == SB_DIM == C_DIM and 2 * HEAD == LANES
CHUNK_BITS = CHUNK.bit_length() - 1
assert 1 << CHUNK_BITS == CHUNK
VMEM_LIMIT = 56 * 1024 * 1024
NEG = -1e30
PAD_KEYS = LEFT_CHUNKS * CHUNK
BAND_TQ = 128
BAND_W = BAND_TQ + PAD_KEYS
TOEP_W = BAND_W + BAND_TQ

NN = (((1,), (0,)), ((), ()))
NT = (((1,), (1,)), ((), ()))
TN = (((0,), (0,)), ((), ()))


def _dot(a, b, dn):
    return lax.dot_general(a, b, dn, preferred_element_type=F32)


def _pick(dim, pref):
    if dim <= pref:
        return dim
    best = None
    for t in range(LANES, pref + 1, LANES):
        if dim % t == 0:
            best = t
    assert best is not None, (dim, pref)
    return best


def _params(sem):
    return pltpu.CompilerParams(dimension_semantics=sem, vmem_limit_bytes=VMEM_LIMIT)


def _mm(a, b, dims="nn", res=None, out_dtype=F32, name="mm"):
    if dims == "nn":
        (M, K), (K2, N) = a.shape, b.shape
    elif dims == "nt":
        (M, K), (N, K2) = a.shape, b.shape
    else:
        (K, M), (K2, N) = a.shape, b.shape
    assert K == K2, (a.shape, b.shape, dims)
    tm, tn, tk = _pick(M, 1024), _pick(N, 1152), _pick(K, 1024)
    nk = K // tk
    dn = {"nn": NN, "nt": NT, "tn": TN}[dims]
    has_res = res is not None

    def body(*refs):
        if has_res:
            a_ref, b_ref, r_ref, o_ref, acc = refs
        else:
            a_ref, b_ref, o_ref, acc = refs
        k = pl.program_id(2)

        @pl.when(k == 0)
        def _():
            acc[...] = jnp.zeros_like(acc)

        acc[...] += _dot(a_ref[...].astype(BF16), b_ref[...].astype(BF16), dn)

        @pl.when(k == nk - 1)
        def _():
            r = acc[...]
            if has_res:
                r = r + r_ref[...]
            o_ref[...] = r.astype(out_dtype)

    a_spec = (pl.BlockSpec((tk, tm), lambda i, j, k: (k, i)) if dims == "tn"
              else pl.BlockSpec((tm, tk), lambda i, j, k: (i, k)))
    b_spec = (pl.BlockSpec((tn, tk), lambda i, j, k: (j, k)) if dims == "nt"
              else pl.BlockSpec((tk, tn), lambda i, j, k: (k, j)))
    o_spec = pl.BlockSpec((tm, tn), lambda i, j, k: (i, j))
    in_specs = [a_spec, b_spec] + ([o_spec] if has_res else [])
    args = (a, b) + ((res,) if has_res else ())
    return pl.pallas_call(
        body, name=name, grid=(M // tm, N // tn, nk),
        in_specs=in_specs, out_specs=o_spec,
        out_shape=jax.ShapeDtypeStruct((M, N), out_dtype),
        scratch_shapes=[pltpu.VMEM((tm, tn), F32)],
        compiler_params=_params(("parallel", "parallel", "arbitrary")),
    )(*args)


def _rms_fwd(x, g, out_dtype=BF16, name="rms_fwd"):
    T, Fd = x.shape
    tm = _pick(T, 256)

    def body(x_ref, g_ref, o_ref):
        xv = x_ref[...]
        r = lax.rsqrt(jnp.mean(xv * xv, axis=-1, keepdims=True) + RMS_EPS)
        o_ref[...] = (xv * r * g_ref[...]).astype(out_dtype)

    return pl.pallas_call(
        body, name=name, grid=(T // tm,),
        in_specs=[pl.BlockSpec((tm, Fd), lambda i: (i, 0)), pl.BlockSpec((1, Fd), lambda i: (0, 0))],
        out_specs=pl.BlockSpec((tm, Fd), lambda i: (i, 0)),
        out_shape=jax.ShapeDtypeStruct((T, Fd), out_dtype),
        compiler_params=_params(("parallel",)),
    )(x, g)


def _rms_bwd(x, g, dy, dres=None, name="rms_bwd"):
    T, Fd = x.shape
    tm = _pick(T, 256)
    has_res = dres is not None

    def body(*refs):
        if has_res:
            x_ref, g_ref, dy_ref, r_ref, dx_ref, dxb_ref, dg_ref = refs
        else:
            x_ref, g_ref, dy_ref, dx_ref, dxb_ref, dg_ref = refs
        xv, dyv = x_ref[...], dy_ref[...]
        r = lax.rsqrt(jnp.mean(xv * xv, axis=-1, keepdims=True) + RMS_EPS)
        gdy = dyv * g_ref[...]
        dot = jnp.mean(xv * gdy, axis=-1, keepdims=True)
        dx = r * gdy - xv * (r * r * r * dot)
        if has_res:
            dx = dx + r_ref[...]
        dx_ref[...] = dx
        dxb_ref[...] = dx.astype(BF16)

        @pl.when(pl.program_id(0) == 0)
        def _():
            dg_ref[...] = jnp.zeros_like(dg_ref)

        dg_ref[...] += jnp.sum(dyv * xv * r, axis=0, keepdims=True)

    row = pl.BlockSpec((tm, Fd), lambda i: (i, 0))
    vec = pl.BlockSpec((1, Fd), lambda i: (0, 0))
    in_specs = [row, vec, row] + ([row] if has_res else [])
    args = (x, g, dy) + ((dres,) if has_res else ())
    return pl.pallas_call(
        body, name=name, grid=(T // tm,),
        in_specs=in_specs, out_specs=[row, row, vec],
        out_shape=[jax.ShapeDtypeStruct((T, Fd), F32), jax.ShapeDtypeStruct((T, Fd), BF16),
                   jax.ShapeDtypeStruct((1, Fd), F32)],
        compiler_params=_params(("arbitrary",)),
    )(*args)


def _mm_rms_bwd(a, b, x, g, dres, name="mm_rms_bwd"):
    T, K = a.shape
    Fd = b.shape[1]
    tm, tk = _pick(T, 512), _pick(K, 1024)
    nk = K // tk

    def body(a_ref, b_ref, x_ref, g_ref, r_ref, dx_ref, dxb_ref, dg_ref, acc):
        i, k = pl.program_id(0), pl.program_id(1)

        @pl.when(k == 0)
        def _():
            acc[...] = jnp.zeros_like(acc)

        @pl.when((k == 0) & (i == 0))
        def _():
            dg_ref[...] = jnp.zeros_like(dg_ref)

        acc[...] += _dot(a_ref[...].astype(BF16), b_ref[...].astype(BF16), NN)

        @pl.when(k == nk - 1)
        def _():
            xv, dyv = x_ref[...], acc[...]
            r = lax.rsqrt(jnp.mean(xv * xv, axis=-1, keepdims=True) + RMS_EPS)
            gdy = dyv * g_ref[...]
            dot = jnp.mean(xv * gdy, axis=-1, keepdims=True)
            dx = r * gdy - xv * (r * r * r * dot) + r_ref[...]
            dx_ref[...] = dx
            dxb_ref[...] = dx.astype(BF16)
            dg_ref[...] += jnp.sum(dyv * xv * r, axis=0, keepdims=True)

    row = pl.BlockSpec((tm, Fd), lambda i, k: (i, 0))
    vec = pl.BlockSpec((1, Fd), lambda i, k: (0, 0))
    return pl.pallas_call(
        body, name=name, grid=(T // tm, nk),
        in_specs=[pl.BlockSpec((tm, tk), lambda i, k: (i, k)), pl.BlockSpec((tk, Fd), lambda i, k: (k, 0)),
                  row, vec, row],
        out_specs=[row, row, vec],
        out_shape=[jax.ShapeDtypeStruct((T, Fd), F32), jax.ShapeDtypeStruct((T, Fd), BF16),
                   jax.ShapeDtypeStruct((1, Fd), F32)],
        scratch_shapes=[pltpu.VMEM((tm, Fd), F32)],
        compiler_params=_params(("arbitrary", "arbitrary")),
    )(a, b, x, g, dres)


def _latent_bwd(proj, nq, nkv, dqa, dkn, dva, dkr, cos_k, sin_k, g_cq, g_ckv, w_uq_t, w_ukv_t, name="latent_bwd"):
    T = proj.shape[0]
    tm = _pick(T, 512)
    wl = _O2

    def rms_bwd(xv, gv, dyv):
        r = lax.rsqrt(jnp.mean(xv * xv, axis=-1, keepdims=True) + RMS_EPS)
        gdy = dyv * gv
        dot = jnp.mean(xv * gdy, axis=-1, keepdims=True)
        return r * gdy - xv * (r * r * r * dot), jnp.sum(dyv * xv * r, axis=0, keepdims=True)

    def body(p_ref, nq_ref, nkv_ref, dqa_ref, dkn_ref, dva_ref, dkr_ref, c_ref, s_ref, gq_ref, gkv_ref, wq_ref, wkv_ref,
             dlat_ref, dwq_ref, dwkv_ref, dgq_ref, dgkv_ref):
        @pl.when(pl.program_id(0) == 0)
        def _():
            for ref in (dwq_ref, dwkv_ref, dgq_ref, dgkv_ref):
                ref[...] = jnp.zeros_like(ref)

        dqv = dqa_ref[...]
        dkv = jnp.concatenate([dkn_ref[...], dva_ref[...]], axis=1)
        pv = p_ref[...]
        dc_q, dgq = rms_bwd(pv[:, :_O1], gq_ref[...], _dot(dqv, wq_ref[...], NN))
        dc_kv, dgkv = rms_bwd(pv[:, _O1:], gkv_ref[...], _dot(dkv, wkv_ref[...], NN))
        dkr_raw = _rotate(dkr_ref[...], c_ref[...], -s_ref[...])
        dlat_ref[...] = jnp.concatenate([dc_q, dc_kv, dkr_raw], axis=1).astype(BF16)
        dwq_ref[...] += _dot(dqv, nq_ref[...], TN)
        dwkv_ref[...] += _dot(dkv, nkv_ref[...], TN)
        dgq_ref[...] += dgq
        dgkv_ref[...] += dgkv

    row = lambda w: pl.BlockSpec((tm, w), lambda i: (i, 0))
    const = lambda a: pl.BlockSpec(a.shape, lambda i: (0, 0))
    outs = [jax.ShapeDtypeStruct((T, wl + LANES), BF16), jax.ShapeDtypeStruct(w_uq_t.shape, F32),
            jax.ShapeDtypeStruct(w_ukv_t.shape, F32), jax.ShapeDtypeStruct(g_cq.shape, F32),
            jax.ShapeDtypeStruct(g_ckv.shape, F32)]
    return pl.pallas_call(
        body, name=name, grid=(T // tm,),
        in_specs=[row(wl), row(_O1), row(_O2 - _O1), row(dqa.shape[1]), row(dkn.shape[1]), row(dva.shape[1]),
                  row(LANES), row(LANES), row(LANES), const(g_cq), const(g_ckv), const(w_uq_t), const(w_ukv_t)],
        out_specs=[row(wl + LANES)] + [const(o) for o in outs[1:]],
        out_shape=outs,
        compiler_params=_params(("arbitrary",)),
    )(proj, nq, nkv, dqa, dkn, dva, dkr, cos_k, sin_k, g_cq, g_ckv, w_uq_t, w_ukv_t)


def _loss_head(h, g, target, name="loss_head"):
    T, Fd = h.shape
    tm = _pick(T, 256)

    def body(h_ref, g_ref, t_ref, loss_ref, dh_ref, dhb_ref, dg_ref):
        xv = h_ref[...]
        r = lax.rsqrt(jnp.mean(xv * xv, axis=-1, keepdims=True) + RMS_EPS)
        diff = xv * r * g_ref[...] - t_ref[...]
        part = 0.5 * jnp.sum(jnp.mean(diff * diff, axis=-1, keepdims=True), axis=0, keepdims=True)
        dyv = diff * (1.0 / Fd)
        gdy = dyv * g_ref[...]
        dot = jnp.mean(xv * gdy, axis=-1, keepdims=True)
        dh = r * gdy - xv * (r * r * r * dot)
        dh_ref[...] = dh
        dhb_ref[...] = dh.astype(BF16)

        @pl.when(pl.program_id(0) == 0)
        def _():
            dg_ref[...] = jnp.zeros_like(dg_ref)
            loss_ref[...] = jnp.zeros_like(loss_ref)

        dg_ref[...] += jnp.sum(dyv * xv * r, axis=0, keepdims=True)
        loss_ref[...] += jnp.broadcast_to(part, loss_ref.shape)

    row = pl.BlockSpec((tm, Fd), lambda i: (i, 0))
    vec = pl.BlockSpec((1, Fd), lambda i: (0, 0))
    return pl.pallas_call(
        body, name=name, grid=(T // tm,),
        in_specs=[row, vec, row],
        out_specs=[pl.BlockSpec((1, LANES), lambda i: (0, 0)), row, row, vec],
        out_shape=[jax.ShapeDtypeStruct((1, LANES), F32), jax.ShapeDtypeStruct((T, Fd), F32),
                   jax.ShapeDtypeStruct((T, Fd), BF16), jax.ShapeDtypeStruct((1, Fd), F32)],
        compiler_params=_params(("arbitrary",)),
    )(h, g, target)


FFN_TF = 256


def _ffn_fwd(h, g, wg_t, wu_t, wd, name="ffn_fwd"):
    T, Dm = h.shape
    Fh = wd.shape[0]
    tm = _pick(T, 1024)
    nf = Fh // FFN_TF

    def body(h_ref, g_ref, wg_ref, wu_ref, wd_ref, o_ref, u_ref, a_ref, b_ref):
        j = pl.program_id(1)

        @pl.when(j == 0)
        def _():
            xv = h_ref[...]
            r = lax.rsqrt(jnp.mean(xv * xv, axis=-1, keepdims=True) + RMS_EPS)
            u_ref[...] = (xv * r * g_ref[...]).astype(BF16)
            o_ref[...] = xv

        u = u_ref[...]
        a = _dot(u, wg_ref[...], NT).astype(BF16)
        b = _dot(u, wu_ref[...], NT).astype(BF16)
        a_ref[...] = a
        b_ref[...] = b
        af = a.astype(F32)
        s = (af * jax.nn.sigmoid(af) * b.astype(F32)).astype(BF16)
        o_ref[...] += _dot(s, wd_ref[...], NN)

    row = pl.BlockSpec((tm, Dm), lambda i, j: (i, 0))
    wblk = pl.BlockSpec((FFN_TF, Dm), lambda i, j: (j, 0))
    ablk = pl.BlockSpec((tm, FFN_TF), lambda i, j: (i, j))
    return pl.pallas_call(
        body, name=name, grid=(T // tm, nf),
        in_specs=[row, pl.BlockSpec((1, Dm), lambda i, j: (0, 0)), wblk, wblk, wblk],
        out_specs=[row, row, ablk, ablk],
        out_shape=[jax.ShapeDtypeStruct((T, Dm), F32), jax.ShapeDtypeStruct((T, Dm), BF16),
                   jax.ShapeDtypeStruct((T, Fh), BF16), jax.ShapeDtypeStruct((T, Fh), BF16)],
        compiler_params=_params(("parallel", "arbitrary")),
    )(h, g, wg_t, wu_t, wd)


def _ffn_bwd(dh, u, a, b, wg_t, wu_t, wd, name="ffn_bwd"):
    T, Dm = dh.shape
    Fh = wd.shape[0]
    nf = Fh // FFN_TF
    once = pl.Buffered(1)

    def body(dh_ref, u_ref, a_ref, b_ref, wg_ref, wu_ref, wd_ref, du_ref, dwg_ref, dwu_ref, dwd_ref):
        j = pl.program_id(0)

        @pl.when(j == 0)
        def _():
            du_ref[...] = jnp.zeros_like(du_ref)

        ds = _dot(dh_ref[...], wd_ref[...], NT)
        af, bf = a_ref[...].astype(F32), b_ref[...].astype(F32)
        sig = jax.nn.sigmoid(af)
        sa = af * sig
        dwd_ref[...] = _dot((sa * bf).astype(BF16), dh_ref[...], TN).astype(BF16)
        dab = jnp.concatenate([(ds * bf * (sig * (1.0 + af * (1.0 - sig)))).astype(BF16),
                               (ds * sa).astype(BF16)], axis=1)
        dw = _dot(dab, u_ref[...], TN)
        dwg_ref[...] = dw[:FFN_TF].astype(BF16)
        dwu_ref[...] = dw[FFN_TF:].astype(BF16)
        du_ref[...] += _dot(dab, jnp.concatenate([wg_ref[...], wu_ref[...]], axis=0), NN)

    full = lambda: pl.BlockSpec((T, Dm), lambda j: (0, 0), pipeline_mode=once)
    wblk = pl.BlockSpec((FFN_TF, Dm), lambda j: (j, 0))
    ablk = pl.BlockSpec((T, FFN_TF), lambda j: (0, j))
    return pl.pallas_call(
        body, name=name, grid=(nf,),
        in_specs=[full(), full(), ablk, ablk, wblk, wblk, wblk],
        out_specs=[pl.BlockSpec((T, Dm), lambda j: (0, 0)), wblk, wblk, wblk],
        out_shape=[jax.ShapeDtypeStruct((T, Dm), F32)] + [jax.ShapeDtypeStruct((Fh, Dm), BF16)] * 3,
        compiler_params=_params(("arbitrary",)),
    )(dh, u, a, b, wg_t, wu_t, wd)


def _rope(x, cos_t, sin_t, col0, ncols, out_dtype, name="rope"):
    T = x.shape[0]
    wt = cos_t.shape[1]
    tm = _pick(T, 256)
    nb = ncols * LANES // wt
    half = MLA_ROPE // 2

    def body(x_ref, c_ref, s_ref, o_ref):
        xv = x_ref[...].astype(F32)
        lane = lax.broadcasted_iota(jnp.int32, xv.shape, 1)
        first = (lane & (MLA_ROPE - 1)) < half
        swapped = jnp.where(first, pltpu.roll(xv, wt - half, 1), pltpu.roll(xv, half, 1))
        o_ref[...] = (xv * c_ref[...] + swapped * s_ref[...]).astype(out_dtype)

    off = col0 * LANES // wt
    return pl.pallas_call(
        body, name=name, grid=(T // tm, nb),
        in_specs=[pl.BlockSpec((tm, wt), lambda i, j: (i, j + off)),
                  pl.BlockSpec((tm, wt), lambda i, j: (i, 0)),
                  pl.BlockSpec((tm, wt), lambda i, j: (i, 0))],
        out_specs=pl.BlockSpec((tm, wt), lambda i, j: (i, j)),
        out_shape=jax.ShapeDtypeStruct((T, ncols * LANES), out_dtype),
        compiler_params=_params(("parallel", "parallel")),
    )(x, cos_t, sin_t)


ATT_TQ = 512
ATT_TK = 256


def _mla_masks(shape):
    lane = lax.broadcasted_iota(jnp.int32, shape, 1)
    m0 = (lane < HEAD) | ((lane >= LANES) & (lane < LANES + MLA_ROPE))
    m1 = ((lane >= HEAD) & (lane < LANES)) | ((lane >= LANES + MLA_ROPE) & (lane < LANES + 2 * MLA_ROPE))
    return m0, m1


def _by_twos(n, step, carry):
    carry = lax.fori_loop(0, n // 2, lambda i, c: step(2 * i + 1, step(2 * i, c)), carry)
    return lax.fori_loop(0, n % 2, lambda _, c: step(n - 1, c), carry)


def _chunk_ok(tq, tk, d):
    row = lax.broadcasted_iota(jnp.int32, (tq, tk), 0)
    col = lax.broadcasted_iota(jnp.int32, (tq, tk), 1) + d * tk
    return jnp.concatenate([(col >> CHUNK_BITS) <= (row >> CHUNK_BITS)] * 2, axis=0)


def _rotate(x, cos_t, sin_t):
    half = MLA_ROPE // 2
    lane = lax.broadcasted_iota(jnp.int32, x.shape, 1)
    first = (lane & (MLA_ROPE - 1)) < half
    swapped = jnp.where(first, pltpu.roll(x, x.shape[1] - half, 1), pltpu.roll(x, half, 1))
    return x * cos_t + swapped * sin_t


def _mla_fwd(q, cos_q, sin_q, kv, kr, name="mla_fwd"):
    T = q.shape[0]
    tq, tk = _pick(T, ATT_TQ), _pick(T, ATT_TK)
    nd = tq // tk
    npair = MLA_HEADS // 2
    scale = (MLA_NOPE + MLA_ROPE) ** -0.5

    def body(q_ref, c_ref, s_ref, kn_ref, v_ref, kr_ref, o_ref, lse_ref):
        m_idx = pl.program_id(1)
        qv = _rotate(q_ref[...], c_ref[...], s_ref[...]).astype(BF16)
        m0, m1 = _mla_masks(qv.shape)
        qs = jnp.concatenate([jnp.where(m0, qv, 0), jnp.where(m1, qv, 0)], axis=0).astype(BF16)

        def block(kb, carry, ok):
            ks = pl.ds(pl.multiple_of(kb * tk, tk), tk)
            kcat = jnp.concatenate([kn_ref[ks, :], kr_ref[ks, :]], axis=1)
            mx, l, acc = carry
            s = _dot(qs, kcat, NT) * scale
            if ok is not None:
                s = jnp.where(ok, s, NEG)
            mn = jnp.maximum(mx, jnp.max(s, axis=-1, keepdims=True))
            alpha = jnp.exp(mx - mn)
            p = jnp.exp(s - mn)
            return (mn, alpha * l + jnp.sum(p, axis=-1, keepdims=True),
                    alpha * acc + _dot(p.astype(BF16), v_ref[ks, :], NN))

        init = (jnp.full((2 * tq, 1), NEG, F32), jnp.zeros((2 * tq, 1), F32), jnp.zeros((2 * tq, LANES), F32))
        res = init
        for d in range(nd):
            res = block(m_idx * nd + d, res, _chunk_ok(tq, tk, d))
        mx, l, acc = _by_twos(m_idx * nd, lambda kb, c: block(kb, c, None), res)
        h0 = lax.broadcasted_iota(jnp.int32, (tq, LANES), 1) < HEAD
        o_ref[...] = _two_heads(acc * (1.0 / l), h0).astype(o_ref.dtype)
        lse_ref[...] = _two_heads(jnp.broadcast_to(mx + jnp.log(l), (2 * tq, LANES)), h0)

    full = lambda col: pl.BlockSpec((T, LANES), col)
    table = pl.BlockSpec((tq, 2 * LANES), lambda p, m: (m, 0))
    return pl.pallas_call(
        body, name=name, grid=(npair, T // tq),
        in_specs=[pl.BlockSpec((tq, 2 * LANES), lambda p, m: (m, p)), table, table,
                  full(lambda p, m: (0, p)), full(lambda p, m: (0, npair + p)), full(lambda p, m: (0, 0))],
        out_specs=[pl.BlockSpec((tq, LANES), lambda p, m: (m, p)),
                   pl.BlockSpec((tq, LANES), lambda p, m: (m, p))],
        out_shape=[jax.ShapeDtypeStruct((T, npair * LANES), BF16),
                   jax.ShapeDtypeStruct((T, npair * LANES), F32)],
        compiler_params=_params(("parallel", "arbitrary")),
    )(q, cos_q, sin_q, kv, kv, kr)


def _mla_bwd(q, cos_q, sin_q, kv, kr, o, lse, do, do_col0, name="mla_bwd"):
    T = q.shape[0]
    tq, tk = _pick(T, ATT_TQ), _pick(T, ATT_TK)
    nd = tq // tk
    npair = MLA_HEADS // 2
    scale = (MLA_NOPE + MLA_ROPE) ** -0.5

    def body(q_ref, c_ref, s_ref, kn_ref, v_ref, kr_ref, o_ref, lse_ref, do_ref, dq_ref, dkn_ref, dv_ref, dkr_ref,
             dkn_acc, dv_acc):
        p_idx, m_idx = pl.program_id(0), pl.program_id(1)

        @pl.when(m_idx == 0)
        def _():
            dkn_acc[...] = jnp.zeros_like(dkn_acc)
            dv_acc[...] = jnp.zeros_like(dv_acc)

        @pl.when((m_idx == 0) & (p_idx == 0))
        def _():
            dkr_ref[...] = jnp.zeros_like(dkr_ref)

        qv = _rotate(q_ref[...], c_ref[...], s_ref[...]).astype(BF16)
        m0, m1 = _mla_masks(qv.shape)
        qs = jnp.concatenate([jnp.where(m0, qv, 0), jnp.where(m1, qv, 0)], axis=0).astype(BF16)
        dov = do_ref[...].astype(F32)
        h0 = lax.broadcasted_iota(jnp.int32, (tq, LANES), 1) < HEAD
        dos32 = jnp.concatenate([jnp.where(h0, dov, 0.0), jnp.where(h0, 0.0, dov)], axis=0)
        ov = o_ref[...].astype(F32)
        delta = jnp.sum(dos32 * jnp.concatenate([ov, ov], axis=0), axis=-1, keepdims=True)
        dos = dos32.astype(BF16)
        lsev = lse_ref[...]
        lse = jnp.concatenate([lsev[:, 0:1], lsev[:, HEAD:HEAD + 1]], axis=0)

        def block(kb, dq, ok):
            ks = pl.ds(pl.multiple_of(kb * tk, tk), tk)
            kcat = jnp.concatenate([kn_ref[ks, :], kr_ref[ks, :]], axis=1)
            vv = v_ref[ks, :]
            p = jnp.exp(_dot(qs, kcat, NT) * scale - lse)
            if ok is not None:
                p = jnp.where(ok, p, 0.0)
            ds = (p * (_dot(dos, vv, NT) - delta) * scale).astype(BF16)
            dkc = _dot(ds, qs, TN)
            dkn_acc[ks, :] += dkc[:, :LANES]
            dkr_ref[ks, :] += dkc[:, LANES:]
            dv_acc[ks, :] += _dot(p.astype(BF16), dos, TN)
            return dq + _dot(ds, kcat, NN)

        dq = jnp.zeros((2 * tq, 2 * LANES), F32)
        for d in range(nd):
            dq = block(m_idx * nd + d, dq, _chunk_ok(tq, tk, d))
        dq = _by_twos(m_idx * nd, lambda kb, c: block(kb, c, None), dq)
        dq_ref[...] = _rotate(jnp.where(m0, dq[:tq], jnp.where(m1, dq[tq:], 0.0)), c_ref[...],
                              -s_ref[...]).astype(BF16)

        @pl.when(m_idx == T // tq - 1)
        def _():
            dkn_ref[...] = dkn_acc[...].astype(BF16)
            dv_ref[...] = dv_acc[...].astype(BF16)

    full = lambda col: pl.BlockSpec((T, LANES), col)
    blk = lambda col: pl.BlockSpec((tq, LANES), col)
    table = pl.BlockSpec((tq, 2 * LANES), lambda p, m: (m, 0))
    return pl.pallas_call(
        body, name=name, grid=(npair, T // tq),
        in_specs=[pl.BlockSpec((tq, 2 * LANES), lambda p, m: (m, p)), table, table,
                  full(lambda p, m: (0, p)), full(lambda p, m: (0, npair + p)), full(lambda p, m: (0, 0)),
                  blk(lambda p, m: (m, p)), blk(lambda p, m: (m, p)),
                  blk(lambda p, m: (m, do_col0 + p))],
        out_specs=[pl.BlockSpec((tq, 2 * LANES), lambda p, m: (m, p)),
                   full(lambda p, m: (0, p)), full(lambda p, m: (0, p)), full(lambda p, m: (0, 0))],
        out_shape=[jax.ShapeDtypeStruct((T, npair * 2 * LANES), BF16),
                   jax.ShapeDtypeStruct((T, npair * LANES), BF16),
                   jax.ShapeDtypeStruct((T, npair * LANES), BF16),
                   jax.ShapeDtypeStruct((T, LANES), F32)],
        scratch_shapes=[pltpu.VMEM((T, LANES), F32)] * 2,
        compiler_params=_params(("arbitrary", "arbitrary")),
    )(q, cos_q, sin_q, kv, kv, kr, o, lse, do)


def _split_dot(x, tri):
    hi = x.astype(BF16)
    lo = (x - hi.astype(F32)).astype(BF16)
    both = _dot(jnp.concatenate([hi, lo], axis=0), tri, NN)
    return both[:x.shape[0]] + both[x.shape[0]:]


def _sb_terms(qh, kk, before):
    z = _dot(qh, kk, NT)
    sp = jnp.maximum(z, 0.0) + jnp.log(1.0 + jnp.exp(-jnp.abs(z)))
    lk = -sp if before is None else jnp.where(before, -sp, 0.0)
    return z, sp, lk


def _sb_setup(q_ref, tq, tk, scale):
    qv = (q_ref[...].astype(F32) * scale).astype(BF16)
    lane = lax.broadcasted_iota(jnp.int32, (tq, LANES), 1)
    h0 = lane < HEAD
    qs = jnp.concatenate([jnp.where(h0, qv, 0), jnp.where(h0, 0, qv)], axis=0).astype(BF16)
    row = lax.broadcasted_iota(jnp.int32, (tk, tk), 0)
    col = lax.broadcasted_iota(jnp.int32, (tk, tk), 1)
    return qs, h0, row, col


def _sb_before(tq, tk, d):
    row = lax.broadcasted_iota(jnp.int32, (tq, tk), 0)
    col = lax.broadcasted_iota(jnp.int32, (tq, tk), 1) + d * tk
    return jnp.concatenate([col < row] * 2, axis=0)


def _two_heads(x, h0):
    tq = x.shape[0] // 2
    return jnp.where(h0, x[:tq], x[tq:])


def _sb_fwd(qkv, col0, name="sb_fwd"):
    T = qkv.shape[0]
    tq, tk = _pick(T, ATT_TQ), _pick(T, ATT_TK)
    nd = tq // tk
    npair = SB_HEADS // 2
    scale = SB_DIM ** -0.5

    def body(q_ref, k_ref, v_ref, o_ref, o32_ref, w_ref, sp_ref):
        m_idx = pl.program_id(1)
        qs, h0, row, col = _sb_setup(q_ref, tq, tk, scale)
        later = (row > col).astype(BF16)

        def block(kb, carry, before):
            ks = pl.ds(pl.multiple_of(kb * tk, tk), tk)
            c, acc = carry
            z, sp, lk = _sb_terms(qs, k_ref[ks, :].astype(BF16), before)
            w = jnp.exp((z - sp) + _split_dot(lk, later) + c)
            if before is not None:
                w = jnp.where(before, w, 0.0)
            wb = w.astype(BF16)
            w_ref[0, 0, kb] = wb
            sp_ref[0, 0, kb] = sp.astype(BF16)
            return (c + jnp.sum(lk, axis=-1, keepdims=True), acc + _dot(wb, v_ref[ks, :].astype(BF16), NN))

        init = (jnp.zeros((2 * tq, 1), F32), jnp.zeros((2 * tq, LANES), F32))
        res = init
        for d in reversed(range(nd)):
            res = block(m_idx * nd + d, res, _sb_before(tq, tk, d))
        res = _by_twos(m_idx * nd, lambda i, c: block(m_idx * nd - 1 - i, c, None), res)
        o = _two_heads(res[1], h0)
        o_ref[...] = o.astype(o_ref.dtype)
        o32_ref[...] = o

    full = lambda col: pl.BlockSpec((T, LANES), col)
    blk = pl.BlockSpec((tq, LANES), lambda p, m: (m, p))
    return pl.pallas_call(
        body, name=name, grid=(npair, T // tq),
        in_specs=[pl.BlockSpec((tq, LANES), lambda p, m: (m, col0 + p)),
                  full(lambda p, m: (0, col0 + npair + p)), full(lambda p, m: (0, col0 + 2 * npair + p))],
        out_specs=[blk, blk] + [pl.BlockSpec((1, 1, T // tk, 2 * tq, tk), lambda p, m: (p, m, 0, 0, 0))] * 2,
        out_shape=[jax.ShapeDtypeStruct((T, npair * LANES), BF16), jax.ShapeDtypeStruct((T, npair * LANES), F32)]
        + [jax.ShapeDtypeStruct((npair, T // tq, T // tk, 2 * tq, tk), BF16)] * 2,
        compiler_params=_params(("parallel", "arbitrary")),
    )(qkv, qkv, qkv)


def _sb_bwd(qkv, col0, o32, w_all, sp_all, do, do_col0, dep, name="sb_bwd"):
    T = qkv.shape[0]
    tq, tk = _pick(T, ATT_TQ), _pick(T, ATT_TK)
    nd = tq // tk
    npair = SB_HEADS // 2
    scale = SB_DIM ** -0.5

    def body(q_ref, k_ref, v_ref, o_ref, w_ref, sp_ref, do_ref, dep_ref, dq_ref, dk_ref, dv_ref, dk_acc, dv_acc):
        m_idx = pl.program_id(1)

        @pl.when(m_idx == 0)
        def _():
            dk_acc[...] = jnp.zeros_like(dk_acc)
            dv_acc[...] = jnp.zeros_like(dv_acc)

        qs, h0, row, col = _sb_setup(q_ref, tq, tk, scale)
        dov = do_ref[...].astype(F32)
        dos = jnp.concatenate([jnp.where(h0, dov, 0.0), jnp.where(h0, 0.0, dov)], axis=0).astype(BF16)
        ov = o_ref[...]
        etot = jnp.sum(dos.astype(F32) * jnp.concatenate([ov, ov], axis=0), axis=-1, keepdims=True)
        from_here = (row >= col).astype(BF16)

        def block(kb, carry, before):
            ks = pl.ds(pl.multiple_of(kb * tk, tk), tk)
            kk = k_ref[ks, :].astype(BF16)
            vv = v_ref[ks, :].astype(BF16)
            es, dqa = carry
            wb = w_ref[0, 0, kb]
            e = wb.astype(F32) * _dot(dos, vv, NT)
            prev = etot - (_split_dot(e, from_here) + es)
            sig_neg = jnp.exp(-sp_ref[0, 0, kb].astype(F32))
            dz = e * sig_neg - (1.0 - sig_neg) * prev
            if before is not None:
                dz = jnp.where(before, dz, 0.0)
            dzb = dz.astype(BF16)
            dk_acc[ks, :] += _dot(dzb, qs, TN)
            dv_acc[ks, :] += _dot(wb, dos, TN)
            return es + jnp.sum(e, axis=-1, keepdims=True), dqa + _dot(dzb, kk, NN)

        init = (jnp.zeros((2 * tq, 1), F32), jnp.zeros((2 * tq, LANES), F32))
        res = init
        for d in reversed(range(nd)):
            res = block(m_idx * nd + d, res, _sb_before(tq, tk, d))
        res = _by_twos(m_idx * nd, lambda i, c: block(m_idx * nd - 1 - i, c, None), res)
        dq_ref[...] = (_two_heads(res[1], h0) * scale).astype(BF16)

        @pl.when(m_idx == T // tq - 1)
        def _():
            dk_ref[...] = dk_acc[...].astype(BF16)
            dv_ref[...] = dv_acc[...].astype(BF16)

    full = lambda col: pl.BlockSpec((T, LANES), col)
    blk = lambda col: pl.BlockSpec((tq, LANES), col)
    return pl.pallas_call(
        body, name=name, grid=(npair, T // tq),
        in_specs=[blk(lambda p, m: (m, col0 + p)),
                  full(lambda p, m: (0, col0 + npair + p)), full(lambda p, m: (0, col0 + 2 * npair + p)),
                  blk(lambda p, m: (m, p)),
                  pl.BlockSpec((1, 1, T // tk, 2 * tq, tk), lambda p, m: (p, m, 0, 0, 0)),
                  pl.BlockSpec((1, 1, T // tk, 2 * tq, tk), lambda p, m: (p, m, 0, 0, 0)),
                  blk(lambda p, m: (m, do_col0 + p)), pl.BlockSpec((8, LANES), lambda p, m: (0, 0))],
        out_specs=[blk(lambda p, m: (m, p)), full(lambda p, m: (0, p)), full(lambda p, m: (0, p))],
        out_shape=[jax.ShapeDtypeStruct((T, npair * LANES), BF16)] * 3,
        scratch_shapes=[pltpu.VMEM((T, LANES), F32)] * 2,
        compiler_params=_params(("arbitrary", "arbitrary")),
    )(qkv, qkv, qkv, o32, w_all, sp_all, do, dep)


def _band_in_window():
    cq = lax.broadcasted_iota(jnp.int32, (BAND_TQ, BAND_W), 0) >> CHUNK_BITS
    ckp = lax.broadcasted_iota(jnp.int32, (BAND_TQ, BAND_W), 1) >> CHUNK_BITS
    return (ckp >= cq) & (ckp <= cq + LEFT_CHUNKS)


def _band_real(m_idx):
    j = lax.broadcasted_iota(jnp.int32, (BAND_TQ, BAND_W), 1)
    return j >= PAD_KEYS - m_idx * BAND_TQ


def _band_probs(qh, kw, bias, real, scale):
    s = jnp.where(real, _dot(qh, kw, NT) * scale + bias, NEG)
    e = jnp.exp(s - jnp.max(s, axis=-1, keepdims=True))
    return e * (1.0 / jnp.sum(e, axis=-1, keepdims=True))


BAND_SUB = 4


def _band_fwd(qkv, k_pad, v_pad, bias_w, name="band_fwd"):
    T = qkv.shape[0]
    npair = C_HEADS // 2
    scale = C_DIM ** -0.5
    rows = BAND_SUB * BAND_TQ

    def body(q_ref, k_ref, v_ref, b_ref, o_ref, p_ref):
        lane = lax.broadcasted_iota(jnp.int32, (BAND_TQ, LANES), 1)
        h0 = lane < HEAD
        bias = jnp.concatenate([b_ref[0], b_ref[1]], axis=0)
        for sub in range(BAND_SUB):
            m_idx = pl.program_id(1) * BAND_SUB + sub
            win = pl.ds(pl.multiple_of(m_idx * BAND_TQ, BAND_TQ), BAND_W)
            kw, vw = k_ref[win, :], v_ref[win, :]
            qv = q_ref[sub * BAND_TQ:(sub + 1) * BAND_TQ, :]
            qs = jnp.concatenate([jnp.where(h0, qv, 0), jnp.where(h0, 0, qv)], axis=0).astype(BF16)
            p = _band_probs(qs, kw, bias, jnp.concatenate([_band_real(m_idx)] * 2, axis=0), scale).astype(BF16)
            p_ref[0, sub] = p
            o = _two_heads(_dot(p, vw, NN), h0)
            o_ref[sub * BAND_TQ:(sub + 1) * BAND_TQ, :] = o.astype(o_ref.dtype)

    Tp = T + PAD_KEYS
    return pl.pallas_call(
        body, name=name, grid=(npair, T // rows),
        in_specs=[pl.BlockSpec((rows, LANES), lambda p, m: (m, p)),
                  pl.BlockSpec((Tp, LANES), lambda p, m: (0, p)),
                  pl.BlockSpec((Tp, LANES), lambda p, m: (0, p)),
                  pl.BlockSpec((2, BAND_TQ, BAND_W), lambda p, m: (p, 0, 0))],
        out_specs=[pl.BlockSpec((rows, LANES), lambda p, m: (m, p)),
                   pl.BlockSpec((1, BAND_SUB, 2 * BAND_TQ, BAND_W), lambda p, m: (p, m, 0, 0))],
        out_shape=[jax.ShapeDtypeStruct((T, npair * LANES), BF16),
                   jax.ShapeDtypeStruct((npair, T // BAND_TQ, 2 * BAND_TQ, BAND_W), BF16)],
        compiler_params=_params(("parallel", "arbitrary")),
    )(qkv, k_pad, v_pad, bias_w)


def _band_bwd(qkv, k_pad, v_pad, probs, do, name="band_bwd"):
    T = qkv.shape[0]
    npair = C_HEADS // 2
    scale = C_DIM ** -0.5

    rows = BAND_SUB * BAND_TQ

    def body(q_ref, k_ref, v_ref, p_ref, do_ref, dq_ref, dk_ref, dv_ref, db_ref, dk_acc, dv_acc):
        @pl.when(pl.program_id(1) == 0)
        def _():
            dk_acc[...] = jnp.zeros_like(dk_acc)
            dv_acc[...] = jnp.zeros_like(dv_acc)
            db_ref[...] = jnp.zeros_like(db_ref)

        lane = lax.broadcasted_iota(jnp.int32, (BAND_TQ, LANES), 1)
        h0 = lane < HEAD
        dbs = jnp.zeros((2 * BAND_TQ, BAND_W), F32)
        for sub in range(BAND_SUB):
            m_idx = pl.program_id(1) * BAND_SUB + sub
            win = pl.ds(pl.multiple_of(m_idx * BAND_TQ, BAND_TQ), BAND_W)
            kw, vw = k_ref[win, :], v_ref[win, :]
            qv = q_ref[sub * BAND_TQ:(sub + 1) * BAND_TQ, :]
            dov = do_ref[sub * BAND_TQ:(sub + 1) * BAND_TQ, :].astype(F32)
            qs = jnp.concatenate([jnp.where(h0, qv, 0), jnp.where(h0, 0, qv)], axis=0).astype(BF16)
            dos = jnp.concatenate([jnp.where(h0, dov, 0.0), jnp.where(h0, 0.0, dov)], axis=0).astype(BF16)
            pb = p_ref[0, sub]
            p = pb.astype(F32)
            dp = _dot(dos, vw, NT)
            dsb = p * (dp - jnp.sum(p * dp, axis=-1, keepdims=True))
            dbs = dbs + dsb
            dsq = (dsb * scale).astype(BF16)
            dq_ref[sub * BAND_TQ:(sub + 1) * BAND_TQ, :] = _two_heads(_dot(dsq, kw, NN), h0).astype(BF16)
            dk_acc[win, :] += _dot(dsq, qs, TN)
            dv_acc[win, :] += _dot(pb, dos, TN)
        db_ref[0] += dbs[:BAND_TQ]
        db_ref[1] += dbs[BAND_TQ:]

        @pl.when(pl.program_id(1) == T // rows - 1)
        def _():
            dk_ref[...] = dk_acc[...].astype(BF16)
            dv_ref[...] = dv_acc[...].astype(BF16)

    Tp = T + PAD_KEYS
    blk = lambda col: pl.BlockSpec((rows, LANES), col)
    full = pl.BlockSpec((Tp, LANES), lambda p, m: (0, p))
    bias = pl.BlockSpec((2, BAND_TQ, BAND_W), lambda p, m: (p, 0, 0))
    prob = pl.BlockSpec((1, BAND_SUB, 2 * BAND_TQ, BAND_W), lambda p, m: (p, m, 0, 0))
    return pl.pallas_call(
        body, name=name, grid=(npair, T // rows),
        in_specs=[blk(lambda p, m: (m, p)), full, full, prob, blk(lambda p, m: (m, p))],
        out_specs=[blk(lambda p, m: (m, p)), full, full, bias],
        out_shape=[jax.ShapeDtypeStruct((T, npair * LANES), BF16),
                   jax.ShapeDtypeStruct((Tp, npair * LANES), BF16),
                   jax.ShapeDtypeStruct((Tp, npair * LANES), BF16),
                   jax.ShapeDtypeStruct((C_HEADS, BAND_TQ, BAND_W), F32)],
        scratch_shapes=[pltpu.VMEM((Tp, LANES), F32)] * 2,
        compiler_params=_params(("arbitrary", "arbitrary")),
    )(qkv, k_pad, v_pad, probs, do)


def _skew_bits(x, left):
    w = x.shape[1]
    row = lax.broadcasted_iota(jnp.int32, x.shape, 0)
    for b in range(BAND_TQ.bit_length() - 1):
        amt = (w - (1 << b)) if left else (1 << b)
        x = jnp.where((row >> b) & 1 == 1, pltpu.roll(x, amt, 1), x)
    return x


def _toeplitz(diag, name="toeplitz"):
    H = diag.shape[0]

    def body(d_ref, o_ref):
        x = jnp.broadcast_to(d_ref[0], (BAND_TQ, TOEP_W))
        o_ref[0] = jnp.where(_band_in_window(), _skew_bits(x, left=False)[:, BAND_TQ:], NEG)

    return pl.pallas_call(
        body, name=name, grid=(H,),
        in_specs=[pl.BlockSpec((1, 1, TOEP_W), lambda h: (h, 0, 0))],
        out_specs=pl.BlockSpec((1, BAND_TQ, BAND_W), lambda h: (h, 0, 0)),
        out_shape=jax.ShapeDtypeStruct((H, BAND_TQ, BAND_W), F32),
        compiler_params=_params(("parallel",)),
    )(diag.reshape(H, 1, TOEP_W))


def _toeplitz_bwd(dbias, name="toeplitz_bwd"):
    H = dbias.shape[0]

    def body(d_ref, o_ref):
        x = jnp.concatenate([jnp.zeros((BAND_TQ, BAND_TQ), F32), d_ref[0]], axis=1)
        h = BAND_TQ // 2
        while h >= 8:
            x = x[:h] + pltpu.roll(x[h:2 * h], TOEP_W - h, 1)
            h //= 2
        o_ref[0] = jnp.sum(_skew_bits(x, left=True), axis=0, keepdims=True)

    return pl.pallas_call(
        body, name=name, grid=(H,),
        in_specs=[pl.BlockSpec((1, BAND_TQ, BAND_W), lambda h: (h, 0, 0))],
        out_specs=pl.BlockSpec((1, 1, TOEP_W), lambda h: (h, 0, 0)),
        out_shape=jax.ShapeDtypeStruct((H, 1, TOEP_W), F32),
        compiler_params=_params(("parallel",)),
    )(dbias).reshape(H, TOEP_W)


_HBM = pl.BlockSpec(memory_space=pltpu.HBM)
_SEM = pl.BlockSpec(memory_space=pltpu.SEMAPHORE)
_EFFECT = pltpu.SideEffectType.DATAFLOW_SIDE_EFFECTING


def _peers():
    x, y, c = lax.axis_index("x"), lax.axis_index("y"), lax.axis_index("c")
    out = []
    for k in range(1, N_DEV):
        peer = (1 - x if (k >> 2) & 1 else x, 1 - y if (k >> 1) & 1 else y, 1 - c if k & 1 else c)
        out.append((peer, 4 * peer[0] + 2 * peer[1] + peer[2]))
    return 4 * x + 2 * y + c, out


def _split_copies(ins, lands, scatter, send_sem, recv_sem, arriving):
    me, peers = _peers()
    out = []
    for a in range(len(ins)):
        for peer, idx in peers:
            out.append(pltpu.make_async_remote_copy(
                src_ref=ins[a].at[idx] if scatter[a] else ins[a],
                dst_ref=lands[a].at[idx if arriving else me], send_sem=send_sem, recv_sem=recv_sem,
                device_id=peer, device_id_type=pl.DeviceIdType.MESH))
    return out


def _landing_zones(arrays, scatter):
    return [lax.empty((N_DEV,) + (a.shape[1:] if s else a.shape), a.dtype) for a, s in zip(arrays, scatter)]


def _place_own(arrays, scatter, name):
    n = len(arrays)
    lands = _landing_zones(arrays, scatter)
    me = (4 * lax.axis_index("x") + 2 * lax.axis_index("y") + lax.axis_index("c")).astype(jnp.int32).reshape(1)

    def body(me_ref, *refs):
        for a in range(n):
            refs[2 * n + a][...] = refs[a][...].reshape(refs[2 * n + a].shape)

    def row_spec(shape):
        zeros = (0,) * (len(shape) - 1)
        return pl.BlockSpec((1,) + tuple(shape[1:]), lambda i, me_ref: (me_ref[0],) + zeros)

    in_specs = [row_spec(a.shape) if s else pl.BlockSpec(a.shape, lambda i, me_ref, nd=a.ndim: (0,) * nd)
                for a, s in zip(arrays, scatter)]
    return pl.pallas_call(
        body, name=name,
        out_shape=[jax.ShapeDtypeStruct(l.shape, l.dtype) for l in lands],
        grid_spec=pltpu.PrefetchScalarGridSpec(
            num_scalar_prefetch=1, grid=(1,),
            in_specs=in_specs + [pl.BlockSpec(memory_space=pl.ANY)] * n,
            out_specs=[row_spec(l.shape) for l in lands]),
        input_output_aliases={1 + n + i: i for i in range(n)},
        compiler_params=_params(("arbitrary",)),
    )(me, *arrays, *lands)


def _exchange_start(arrays, scatter, after, name, lands=None):
    n = len(arrays)
    if lands is None:
        lands = list(_place_own(arrays, scatter, name=name.replace("_start_", "_own_")))

    def body(*refs):
        ins, lnd = refs[:n], refs[n:2 * n]
        send_sem, recv_sem = refs[2 * n + 1:2 * n + 3]
        token = refs[-1]
        for cp in _split_copies(ins, lnd, scatter, send_sem, recv_sem, arriving=False):
            cp.start()
        token[...] = jnp.zeros_like(token)

    hbm = lambda a: pltpu.HBM(a.shape, a.dtype)
    out = pl.pallas_call(
        body, name=name,
        out_shape=(pltpu.SemaphoreType.DMA(()), pltpu.SemaphoreType.DMA(()),
                   *[hbm(a) for a in arrays], *[hbm(a) for a in lands],
                   jax.ShapeDtypeStruct((8, LANES), F32)),
        in_specs=[_HBM] * (2 * n) + [pl.BlockSpec(memory_space=pl.ANY)],
        out_specs=(_SEM, _SEM, *([_HBM] * (2 * n)), pl.BlockSpec(memory_space=pltpu.VMEM)),
        input_output_aliases={i: 2 + i for i in range(2 * n)},
        compiler_params=pltpu.CompilerParams(has_side_effects=_EFFECT),
    )(*[pltpu.with_memory_space_constraint(a, pltpu.HBM) for a in list(arrays) + lands], after)
    return (out[0], out[1], list(out[2:2 + n]), list(out[2 + n:2 + 2 * n]), tuple(scatter)), out[-1]


def _exchange_wait(handle, after, name):
    send_sem, recv_sem, ins, lands, scatter = handle
    n = len(ins)
    after = after if isinstance(after, tuple) else (after,)

    def body(*refs):
        i_ref, l_ref = refs[:n], refs[n:2 * n]
        s_sem, r_sem = refs[2 * n:2 * n + 2]
        for cp in _split_copies(i_ref, l_ref, scatter, s_sem, r_sem, arriving=False):
            cp.wait_send()
        for cp in _split_copies(i_ref, l_ref, scatter, s_sem, r_sem, arriving=True):
            cp.wait_recv()

    hbm = lambda a: pltpu.HBM(a.shape, a.dtype)
    out = pl.pallas_call(
        body, name=name,
        out_shape=tuple(hbm(a) for a in ins + lands),
        in_specs=[_HBM] * (2 * n) + [_SEM, _SEM] + [pl.BlockSpec(memory_space=pl.ANY)] * len(after),
        out_specs=tuple([_HBM] * (2 * n)),
        input_output_aliases={i: i for i in range(2 * n)},
        compiler_params=pltpu.CompilerParams(has_side_effects=_EFFECT),
    )(*ins, *lands, send_sem, recv_sem, *after)
    return list(out[n:])


def _adamw(w, parts, m, v, name="adamw"):
    R, C = w.shape
    L = len(parts)
    rl = R // L
    tr = max([t for t in range(16, 257, 16) if rl % t == 0], default=rl)
    nb = rl // tr
    c1 = 1.0 - ADAM_B1 ** ADAM_STEP
    c2 = 1.0 - ADAM_B2 ** ADAM_STEP

    def body(*refs):
        w_ref, p_refs, (m_ref, v_ref, g_ref, d_ref, nm_ref, nv_ref) = refs[0], refs[1:1 + L], refs[1 + L:]
        g = None
        for j, p_ref in enumerate(p_refs):
            gj = p_ref[0].astype(F32)
            for i in range(1, N_DEV):
                gj = gj + p_ref[i].astype(F32)
            g = gj if g is None else jnp.where(pl.program_id(0) == j, gj, g)
        nm = ADAM_B1 * m_ref[...] + (1.0 - ADAM_B1) * g
        nv = ADAM_B2 * v_ref[...] + (1.0 - ADAM_B2) * (g * g)
        g_ref[...] = g
        nm_ref[...] = nm
        nv_ref[...] = nv
        d_ref[...] = -ADAM_LR * ((nm / c1) / (jnp.sqrt(nv / c2) + ADAM_EPS) + ADAM_WD * w_ref[...])

    blk = pl.BlockSpec((tr, C), lambda l, i: (l * nb + i, 0))
    part = lambda j: pl.BlockSpec((N_DEV, tr, C), lambda l, i: (0, jnp.where(l == j, i, 0), 0))
    return pl.pallas_call(
        body, name=name, grid=(L, nb),
        in_specs=[blk] + [part(j) for j in range(L)] + [blk, blk],
        out_specs=[blk] * 4,
        out_shape=[jax.ShapeDtypeStruct((R, C), F32)] * 4,
        compiler_params=_params(("arbitrary", "arbitrary")),
    )(w, *parts, m, v)


_O1 = Q_LORA
_O2 = _O1 + KV_LORA
_O3 = _O2 + MLA_ROPE
_NB = SB_HEADS * SB_DIM
IN_W = _O2 + LANES + 3 * _NB
COL_KR = _O2 // LANES
COL_SB = COL_KR + 1


def _w_in_local(w):
    kr = w[_O2:_O3]
    pad = jnp.zeros((LANES - 2 * MLA_ROPE, w.shape[1]), w.dtype)
    return jnp.concatenate([w[:_O2], kr, kr, pad, w[_O3:]], axis=0)


def _w_in_grad(g):
    kr = (g[_O2:_O2 + MLA_ROPE].astype(F32) + g[_O2 + MLA_ROPE:_O2 + 2 * MLA_ROPE].astype(F32)).astype(g.dtype)
    return jnp.concatenate([g[:_O2], kr, g[_O2 + LANES:]], axis=0)


def _w_uq_local(w):
    w3 = w.reshape(MLA_HEADS // 2, 2, MLA_NOPE + MLA_ROPE, w.shape[1])
    nope = w3[:, :, :MLA_NOPE].reshape(MLA_HEADS // 2, 2 * MLA_NOPE, w.shape[1])
    rope = w3[:, :, MLA_NOPE:].reshape(MLA_HEADS // 2, 2 * MLA_ROPE, w.shape[1])
    pad = jnp.zeros((MLA_HEADS // 2, LANES - 2 * MLA_ROPE, w.shape[1]), w.dtype)
    return jnp.concatenate([nope, rope, pad], axis=1).reshape(-1, w.shape[1])


def _w_uq_grad(g):
    g3 = g.reshape(MLA_HEADS // 2, 2 * LANES, g.shape[1])
    nope = g3[:, :2 * MLA_NOPE].reshape(MLA_HEADS // 2, 2, MLA_NOPE, g.shape[1])
    rope = g3[:, LANES:LANES + 2 * MLA_ROPE].reshape(MLA_HEADS // 2, 2, MLA_ROPE, g.shape[1])
    return jnp.concatenate([nope, rope], axis=2).reshape(-1, g.shape[1])


def _w_ukv_local(w):
    w3 = w.reshape(MLA_HEADS, MLA_NOPE + MLA_V, w.shape[1])
    return jnp.concatenate([w3[:, :MLA_NOPE].reshape(-1, w.shape[1]),
                            w3[:, MLA_NOPE:].reshape(-1, w.shape[1])], axis=0)


def _w_ukv_grad(g):
    half = MLA_HEADS * MLA_NOPE
    kn = g[:half].reshape(MLA_HEADS, MLA_NOPE, g.shape[1])
    vv = g[half:].reshape(MLA_HEADS, MLA_V, g.shape[1])
    return jnp.concatenate([kn, vv], axis=1).reshape(-1, g.shape[1])


def _rope_tables(T):
    pos = jnp.arange(T, dtype=F32)
    inv_freq = ROPE_THETA ** (-jnp.arange(0, MLA_ROPE, 2, dtype=F32) / MLA_ROPE)
    ang = pos[:, None] * inv_freq[None, :]
    cos, sin = jnp.cos(ang), jnp.sin(ang)
    ones = jnp.ones((T, LANES - 2 * MLA_ROPE), F32)
    cos_k = jnp.concatenate([cos, cos, cos, cos, ones], axis=1)
    sin_k = jnp.concatenate([-sin, sin, -sin, sin, 0.0 * ones], axis=1)
    cos_q = jnp.concatenate([jnp.ones((T, LANES), F32), cos_k], axis=1)
    sin_q = jnp.concatenate([jnp.zeros((T, LANES), F32), sin_k], axis=1)
    return cos_q, sin_q, cos_k, sin_k


def _bias_diag_index():
    ell = np.arange(TOEP_W)
    return np.clip(BAND_W - ell, -REL_CLIP, REL_CLIP) + REL_CLIP


def _local_step(x, target, small, get_weights, put_grads):
    T = x.shape[0]
    cos_q, sin_q, cos_k, sin_k = _rope_tables(T)
    G = {}
    W = dict(small)

    u0 = _rms_fwd(x, W["g_mix"][0:1], name="rms_mix0")
    bias_w = _toeplitz(W["od_rel_bias"][:, _bias_diag_index()])
    W.update(get_weights("in0", (u0, bias_w)))
    proj = _mm(u0, W["w_in_t"], dims="nt", name="proj_in")
    W.update(get_weights("mix0", proj))
    c_q, c_kv = proj[:, :_O1], proj[:, _O1:_O2]
    nq = _rms_fwd(c_q, W["g_cq"], name="rms_cq")
    nkv = _rms_fwd(c_kv, W["g_ckv"], name="rms_ckv")
    qa_raw = _mm(nq, W["w_uq_t"], dims="nt", name="proj_uq")
    kv = _mm(nkv, W["w_ukv_t"], dims="nt", out_dtype=BF16, name="proj_ukv")
    kr = _rope(proj, cos_k, sin_k, COL_KR, 1, BF16, name="rope_k")
    o_a, lse = _mla_fwd(qa_raw, cos_q, sin_q, kv, kr)
    o_b, o_b32, w_b, sp_b = _sb_fwd(proj, COL_SB)
    o_ab = jnp.concatenate([o_a, o_b], axis=1)
    h1 = _mm(o_ab, W["ev_w_out"], res=x, name="out_ev")

    def ffn_fwd(h, layer):
        W.update(get_weights(f"ffn{layer}", h))
        return _ffn_fwd(h, W["g_ffn"][layer:layer + 1], W[f"w_gate_t{layer}"], W[f"w_up_t{layer}"],
                        W[f"w_down{layer}"], name=f"ffn_fwd{layer}")

    h2, u1, a0, b0 = ffn_fwd(h1, 0)

    W.update(get_weights("mix1", h2))
    u2 = _rms_fwd(h2, W["g_mix"][1:2], name="rms_mix1")
    qkv = _mm(u2, W["od_w_qkv_t"], dims="nt", out_dtype=BF16, name="proj_qkv")
    nc = C_HEADS * C_DIM
    pad = ((PAD_KEYS, 0), (0, 0))
    k_pad, v_pad = jnp.pad(qkv[:, nc:2 * nc], pad), jnp.pad(qkv[:, 2 * nc:], pad)
    o_c, p_c = _band_fwd(qkv, k_pad, v_pad, bias_w)
    h3 = _mm(o_c, W["od_w_out"], res=h2, name="out_od")
    h4, u3, a1, b1 = ffn_fwd(h3, 1)

    loss, dh, dhb, G["g_final"] = _loss_head(h4, W["g_final"], target)

    def ffn_bwd(dh, dhb, h, u, a, b, layer):
        du, g_gate, g_up, g_down = _ffn_bwd(dhb, u, a, b, W[f"w_gate_t{layer}"], W[f"w_up_t{layer}"],
                                            W[f"w_down{layer}"], name=f"ffn_bwd{layer}")
        tok = put_grads(f"ffn{layer}", {"w_gate_t": g_gate, "w_up_t": g_up, "w_down": g_down})
        return _rms_bwd(h, W["g_ffn"][layer:layer + 1] + tok[:1, :1], du, dres=dh, name=f"rms_ffn_bwd{layer}")

    dh3, dh3b, g_gffn1 = ffn_bwd(dh, dhb, h3, u3, a1, b1, 1)

    do_c = _mm(dh3b, W["od_w_out"], dims="nt", name="out_od_dx")
    g_od_out = _mm(o_c, dh3b, dims="tn", out_dtype=BF16, name="out_od_dw")
    dq_c, dk_p, dv_p, dbias_w = _band_bwd(qkv, k_pad, v_pad, p_c, do_c)
    dqkv = jnp.concatenate([dq_c, dk_p[PAD_KEYS:], dv_p[PAD_KEYS:]], axis=1)
    tok = put_grads("mix1", {"od_w_qkv_t": _mm(dqkv, u2, dims="tn", out_dtype=BF16, name="proj_qkv_dw"),
                             "od_w_out": g_od_out})
    ddiag = _toeplitz_bwd(dbias_w)
    n_far = BAND_W - REL_CLIP + 1
    G["od_rel_bias"] = jnp.concatenate(
        [jnp.zeros((C_HEADS, REL_CLIP - BAND_TQ + 1), F32), ddiag[:, n_far:][:, ::-1],
         jnp.sum(ddiag[:, :n_far], axis=1, keepdims=True)], axis=1)
    dh2, dh2b, g_gmix1 = _mm_rms_bwd(dqkv, W["od_w_qkv_t"], h2, W["g_mix"][1:2] + tok[:1, :1], dh3,
                                     name="proj_qkv_dx")

    dh1, dh1b, g_gffn0 = ffn_bwd(dh2, dh2b, h1, u1, a0, b0, 0)
    G["g_ffn"] = jnp.concatenate([g_gffn0, g_gffn1], axis=0)

    do_ab = _mm(dh1b, W["ev_w_out"], dims="nt", name="out_ev_dx")
    g0 = {"ev_w_out": _mm(o_ab, dh1b, dims="tn", out_dtype=BF16, name="out_ev_dw")}
    dqa_raw, dkn, dva, dkr = _mla_bwd(qa_raw, cos_q, sin_q, kv, kr, o_a, lse, do_ab, 0)
    dlat, g0["w_uq_t"], g0["w_ukv_t"], G["g_cq"], G["g_ckv"] = _latent_bwd(
        proj, nq, nkv, dqa_raw, dkn, dva, dkr, cos_k, sin_k, W["g_cq"], W["g_ckv"], W["w_uq_t"], W["w_ukv_t"])
    tok = put_grads("mix0", g0)
    dqb, dkb, dvb = _sb_bwd(proj, COL_SB, o_b32, w_b, sp_b, do_ab, MLA_HEADS // 2, tok)
    dproj = jnp.concatenate([dlat, dqb, dkb, dvb], axis=1)
    tok = put_grads("in0", {"w_in_t": _mm(dproj, u0, dims="tn", name="proj_in_dw")})
    dx, _, g_gmix0 = _mm_rms_bwd(dproj, W["w_in_t"], x, W["g_mix"][0:1] + tok[:1, :1], dh1, name="proj_in_dx")
    G["g_mix"] = jnp.concatenate([g_gmix0, g_gmix1], axis=0)
    return loss[0, 0], dx, G


_BIG = ["ev_w_in", "ev_w_uq", "ev_w_ukv", "ev_w_out", "od_w_qkv", "od_w_out", "w_gate", "w_up", "w_down"]
_COL_SHARDED = {"ev_w_in", "ev_w_uq", "ev_w_ukv", "od_w_qkv", "w_gate", "w_up"}
_SMALL = ["ev_g_cq", "ev_g_ckv", "od_rel_bias", "g_mix", "g_ffn", "g_final"]
_GROUPS = {
    "in0": ["ev_w_in"],
    "mix0": ["ev_w_uq", "ev_w_ukv", "ev_w_out"],
    "ffn0": ["w_gate0", "w_up0", "w_down0"],
    "mix1": ["od_w_qkv", "od_w_out"],
    "ffn1": ["w_gate1", "w_up1", "w_down1"],
}
_GROUP_SRC = {n + str(l): (n, l) for n in ("w_gate", "w_up", "w_down") for l in (0, 1)}
_BATCHES = {"in0": ["in0"], "layer0": ["mix0", "ffn0"], "layer1": ["mix1", "ffn1"]}
_BATCH_OF = {grp: batch for batch, grps in _BATCHES.items() for grp in grps}
_SMALL_ROWS = 8
_SMALL_COLS = 1792


def _pack_small(vals):
    flat = jnp.concatenate([v.reshape(-1).astype(F32) for v in vals])
    flat = jnp.pad(flat, (0, _SMALL_ROWS * _SMALL_COLS - flat.shape[0]))
    return flat.reshape(_SMALL_ROWS, _SMALL_COLS)


def _unpack_small(packed, like):
    flat = packed.reshape(-1)
    out, off = [], 0
    for v in like:
        out.append(flat[off:off + v.size].reshape(v.shape))
        off += v.size
    return out


def kernel(x, ev_w_in, ev_g_cq, ev_w_uq, ev_g_ckv, ev_w_ukv, ev_w_out, od_w_qkv, od_rel_bias, od_w_out, g_mix, g_ffn, w_gate, w_up, w_down, g_final, loss_target, m_ev_w_in, m_ev_g_cq, m_ev_w_uq, m_ev_g_ckv, m_ev_w_ukv, m_ev_w_out, m_od_w_qkv, m_od_rel_bias, m_od_w_out, m_g_mix, m_g_ffn, m_w_gate, m_w_up, m_w_down, m_g_final, v_ev_w_in, v_ev_g_cq, v_ev_w_uq, v_ev_g_ckv, v_ev_w_ukv, v_ev_w_out, v_od_w_qkv, v_od_rel_bias, v_od_w_out, v_g_mix, v_g_ffn, v_w_gate, v_w_up, v_w_down, v_g_final):
    args = dict(locals())
    w = {n: args[n] for n in _BIG + _SMALL}
    mom = {n: args["m_" + n] for n in _BIG + _SMALL}
    var = {n: args["v_" + n] for n in _BIG + _SMALL}

    own = {}
    for grp, names in _GROUPS.items():
        for n in names:
            base, layer = _GROUP_SRC.get(n, (n, 0))
            shard = w[base][layer:layer + 1]
            own[n] = (jnp.swapaxes(shard, 1, 2) if base in _COL_SHARDED else shard).astype(BF16)
    placed = dict(zip(own, _place_own(list(own.values()), [False] * len(own), name="gather_own")))
    gather, token = {}, x[0, :8, :LANES]
    for grp, names in _GROUPS.items():
        gather[grp], token = _exchange_start([own[n] for n in names], [False] * len(names), token,
                                             name="gather_start_" + grp, lands=[placed[n] for n in names])

    def get_weights(grp, after):
        names = _GROUPS[grp]
        lands = _exchange_wait(gather[grp], token if after is None else after, name="gather_wait_" + grp)
        full = {n: l.reshape(-1, l.shape[-1]) for n, l in zip(names, lands)}
        if grp == "in0":
            return {"w_in_t": _w_in_local(full["ev_w_in"])}
        if grp == "mix0":
            return {"w_uq_t": _w_uq_local(full["ev_w_uq"]), "w_ukv_t": _w_ukv_local(full["ev_w_ukv"]),
                    "ev_w_out": full["ev_w_out"]}
        if grp == "mix1":
            return {"od_w_qkv_t": full["od_w_qkv"], "od_w_out": full["od_w_out"]}
        layer = grp[-1]
        return {"w_gate_t" + layer: full["w_gate" + layer], "w_up_t" + layer: full["w_up" + layer],
                "w_down" + layer: full["w_down" + layer]}

    scatter, pending = {}, {}

    def put_grads(grp, g):
        if grp == "in0":
            g = {"ev_w_in": _w_in_grad(g["w_in_t"])}
        elif grp == "mix0":
            g = {"ev_w_uq": _w_uq_grad(g["w_uq_t"]), "ev_w_ukv": _w_ukv_grad(g["w_ukv_t"]),
                 "ev_w_out": g["ev_w_out"]}
        elif grp == "mix1":
            g = {"od_w_qkv": g["od_w_qkv_t"], "od_w_out": g["od_w_out"]}
        else:
            layer = grp[-1]
            g = {"w_gate" + layer: g["w_gate_t"], "w_up" + layer: g["w_up_t"], "w_down" + layer: g["w_down"]}
        pending.update({n: v.reshape(N_DEV, 1, v.shape[0] // N_DEV, v.shape[1]).astype(BF16) for n, v in g.items()})
        batch = _BATCH_OF[grp]
        names = [n for gr in _BATCHES[batch] for n in _GROUPS[gr]]
        if not all(n in pending for n in names):
            return jnp.zeros((8, LANES), F32)
        send = [pending[n] for n in names]
        scatter[batch], tok = _exchange_start(send, [True] * len(names), send[0], name="scatter_start_" + batch)
        return tok

    small = {"g_cq": ev_g_cq, "g_ckv": ev_g_ckv, "od_rel_bias": od_rel_bias[0],
             "g_mix": g_mix + token[0, 0], "g_ffn": g_ffn, "g_final": g_final.reshape(1, -1)}
    loss_part, dx, G = _local_step(x[0], loss_target[0], small, get_weights, put_grads)
    g_small = _pack_small([G["g_cq"], G["g_ckv"], G["od_rel_bias"], G["g_mix"], G["g_ffn"], G["g_final"],
                           loss_part.reshape(1)])
    small_handle, _ = _exchange_start([g_small], [False], dx, name="gather_start_small")

    grads, deltas, new_m, new_v = {}, {}, {}, {}
    parts, after = {}, dx

    def wait_parts(batch, after):
        lands = _exchange_wait(scatter[batch], after, name="scatter_wait_" + batch)
        parts.update(zip([n for grp in _BATCHES[batch] for n in _GROUPS[grp]], lands))
        return lands[0]

    def adamw(n):
        col = n in _COL_SHARDED
        rows = lambda a: (jnp.swapaxes(a, 1, 2) if col else a).reshape(-1, a.shape[1 if col else 2])
        layers = [parts[n]] if n in parts else [parts[n + "0"], parts[n + "1"]]
        res = _adamw(rows(w[n]), [p.reshape(N_DEV, -1, p.shape[-1]) for p in layers], rows(mom[n]), rows(var[n]),
                     name="adamw_" + n)
        L, a1, a2 = w[n].shape
        back = lambda r: jnp.swapaxes(r.reshape(L, a2, a1), 1, 2) if col else r.reshape(L, a1, a2)
        grads[n], deltas[n], new_m[n], new_v[n] = [back(r) for r in res]
        return res[0]

    for batch in ("layer1", "layer0"):
        after = wait_parts(batch, after)
    after = wait_parts("in0", tuple(adamw(n) for n in _BIG[1:]))
    after = adamw("ev_w_in")
    small_w = [w[n] for n in _SMALL]
    small_parts = _exchange_wait(small_handle, after, name="gather_wait_small")[0]
    loss = jnp.sum(small_parts.reshape(N_DEV, -1)[:, sum(v.size for v in small_w)])
    res = _adamw(_pack_small(small_w), [small_parts], _pack_small([mom[n] for n in _SMALL]),
                 _pack_small([var[n] for n in _SMALL]), name="adamw_small")
    for d, packed in zip((grads, deltas, new_m, new_v), res):
        for n, val in zip(_SMALL, _unpack_small(packed, small_w)):
            d[n] = val

    order = ["ev_w_in", "ev_g_cq", "ev_w_uq", "ev_g_ckv", "ev_w_ukv", "ev_w_out", "od_w_qkv", "od_rel_bias",
             "od_w_out", "g_mix", "g_ffn", "w_gate", "w_up", "w_down", "g_final"]
    out = [loss, dx[None]]
    for d in (grads, deltas, new_m, new_v):
        out += [d[n] for n in order]
    return tuple(out)
```

```python
import functools

import numpy as np
import jax
import jax.numpy as jnp
from jax import lax
from jax.experimental import pallas as pl
from jax.experimental.pallas import tpu as pltpu

F32 = jnp.float32
BF16 = jnp.bfloat16

D_MODEL = 1024
CHUNK = 64
MLA_HEADS = 8
MLA_NOPE = 64
MLA_ROPE = 32
MLA_V = 64
Q_LORA = 384
KV_LORA = 256
ROPE_THETA = 10000.0
SB_HEADS = 8
SB_DIM = 64
C_HEADS = 16
C_DIM = 64
LEFT_CHUNKS = 8
REL_CLIP = 256
D_FF = 2816
RMS_EPS = 1e-6
ADAM_LR = 0.001
ADAM_B1 = 0.9
ADAM_B2 = 0.999
ADAM_EPS = 1e-08
ADAM_WD = 0.01
ADAM_STEP = 10

N_DEV = 8
LANES = 128
HEAD = 64
assert HEAD == MLA_NOPE == MLA_V == SB_DIM == C_DIM and 2 * HEAD == LANES
CHUNK_BITS = CHUNK.bit_length() - 1
assert 1 << CHUNK_BITS == CHUNK
VMEM_LIMIT = 56 * 1024 * 1024
NEG = -1e30
PAD_KEYS = LEFT_CHUNKS * CHUNK
BAND_TQ = 128
BAND_W = BAND_TQ + PAD_KEYS
TOEP_W = BAND_W + BAND_TQ

NN = (((1,), (0,)), ((), ()))
NT = (((1,), (1,)), ((), ()))
TN = (((0,), (0,)), ((), ()))


def _dot(a, b, dn):
    return lax.dot_general(a, b, dn, preferred_element_type=F32)


def _pick(dim, pref):
    if dim <= pref:
        return dim
    best = None
    for t in range(LANES, pref + 1, LANES):
        if dim % t == 0:
            best = t
    assert best is not None, (dim, pref)
    return best


def _params(sem):
    return pltpu.CompilerParams(dimension_semantics=sem, vmem_limit_bytes=VMEM_LIMIT)


def _mm(a, b, dims="nn", res=None, out_dtype=F32, name="mm"):
    if dims == "nn":
        (M, K), (K2, N) = a.shape, b.shape
    elif dims == "nt":
        (M, K), (N, K2) = a.shape, b.shape
    else:
        (K, M), (K2, N) = a.shape, b.shape
    assert K == K2, (a.shape, b.shape, dims)
    tm, tn, tk = _pick(M, 1024), _pick(N, 1152), _pick(K, 1024)
    nk = K // tk
    dn = {"nn": NN, "nt": NT, "tn": TN}[dims]
    has_res = res is not None

    def body(*refs):
        if has_res:
            a_ref, b_ref, r_ref, o_ref, acc = refs
        else:
            a_ref, b_ref, o_ref, acc = refs
        k = pl.program_id(2)

        @pl.when(k == 0)
        def _():
            acc[...] = jnp.zeros_like(acc)

        acc[...] += _dot(a_ref[...].astype(BF16), b_ref[...].astype(BF16), dn)

        @pl.when(k == nk - 1)
        def _():
            r = acc[...]
            if has_res:
                r = r + r_ref[...]
            o_ref[...] = r.astype(out_dtype)

    a_spec = (pl.BlockSpec((tk, tm), lambda i, j, k: (k, i)) if dims == "tn"
              else pl.BlockSpec((tm, tk), lambda i, j, k: (i, k)))
    b_spec = (pl.BlockSpec((tn, tk), lambda i, j, k: (j, k)) if dims == "nt"
              else pl.BlockSpec((tk, tn), lambda i, j, k: (k, j)))
    o_spec = pl.BlockSpec((tm, tn), lambda i, j, k: (i, j))
    in_specs = [a_spec, b_spec] + ([o_spec] if has_res else [])
    args = (a, b) + ((res,) if has_res else ())
    return pl.pallas_call(
        body, name=name, grid=(M // tm, N // tn, nk),
        in_specs=in_specs, out_specs=o_spec,
        out_shape=jax.ShapeDtypeStruct((M, N), out_dtype),
        scratch_shapes=[pltpu.VMEM((tm, tn), F32)],
        compiler_params=_params(("parallel", "parallel", "arbitrary")),
    )(*args)


def _rms_fwd(x, g, out_dtype=BF16, name="rms_fwd"):
    T, Fd = x.shape
    tm = _pick(T, 256)

    def body(x_ref, g_ref, o_ref):
        xv = x_ref[...]
        r = lax.rsqrt(jnp.mean(xv * xv, axis=-1, keepdims=True) + RMS_EPS)
        o_ref[...] = (xv * r * g_ref[...]).astype(out_dtype)

    return pl.pallas_call(
        body, name=name, grid=(T // tm,),
        in_specs=[pl.BlockSpec((tm, Fd), lambda i: (i, 0)), pl.BlockSpec((1, Fd), lambda i: (0, 0))],
        out_specs=pl.BlockSpec((tm, Fd), lambda i: (i, 0)),
        out_shape=jax.ShapeDtypeStruct((T, Fd), out_dtype),
        compiler_params=_params(("parallel",)),
    )(x, g)


def _rms_bwd(x, g, dy, dres=None, name="rms_bwd"):
    T, Fd = x.shape
    tm = _pick(T, 256)
    has_res = dres is not None

    def body(*refs):
        if has_res:
            x_ref, g_ref, dy_ref, r_ref, dx_ref, dxb_ref, dg_ref = refs
        else:
            x_ref, g_ref, dy_ref, dx_ref, dxb_ref, dg_ref = refs
        xv, dyv = x_ref[...], dy_ref[...]
        r = lax.rsqrt(jnp.mean(xv * xv, axis=-1, keepdims=True) + RMS_EPS)
        gdy = dyv * g_ref[...]
        dot = jnp.mean(xv * gdy, axis=-1, keepdims=True)
        dx = r * gdy - xv * (r * r * r * dot)
        if has_res:
            dx = dx + r_ref[...]
        dx_ref[...] = dx
        dxb_ref[...] = dx.astype(BF16)

        @pl.when(pl.program_id(0) == 0)
        def _():
            dg_ref[...] = jnp.zeros_like(dg_ref)

        dg_ref[...] += jnp.sum(dyv * xv * r, axis=0, keepdims=True)

    row = pl.BlockSpec((tm, Fd), lambda i: (i, 0))
    vec = pl.BlockSpec((1, Fd), lambda i: (0, 0))
    in_specs = [row, vec, row] + ([row] if has_res else [])
    args = (x, g, dy) + ((dres,) if has_res else ())
    return pl.pallas_call(
        body, name=name, grid=(T // tm,),
        in_specs=in_specs, out_specs=[row, row, vec],
        out_shape=[jax.ShapeDtypeStruct((T, Fd), F32), jax.ShapeDtypeStruct((T, Fd), BF16),
                   jax.ShapeDtypeStruct((1, Fd), F32)],
        compiler_params=_params(("arbitrary",)),
    )(*args)


def _mm_rms_bwd(a, b, x, g, dres, name="mm_rms_bwd"):
    T, K = a.shape
    Fd = b.shape[1]
    tm, tk = _pick(T, 512), _pick(K, 1024)
    nk = K // tk

    def body(a_ref, b_ref, x_ref, g_ref, r_ref, dx_ref, dxb_ref, dg_ref, acc):
        i, k = pl.program_id(0), pl.program_id(1)

        @pl.when(k == 0)
        def _():
            acc[...] = jnp.zeros_like(acc)

        @pl.when((k == 0) & (i == 0))
        def _():
            dg_ref[...] = jnp.zeros_like(dg_ref)

        acc[...] += _dot(a_ref[...].astype(BF16), b_ref[...].astype(BF16), NN)

        @pl.when(k == nk - 1)
        def _():
            xv, dyv = x_ref[...], acc[...]
            r = lax.rsqrt(jnp.mean(xv * xv, axis=-1, keepdims=True) + RMS_EPS)
            gdy = dyv * g_ref[...]
            dot = jnp.mean(xv * gdy, axis=-1, keepdims=True)
            dx = r * gdy - xv * (r * r * r * dot) + r_ref[...]
            dx_ref[...] = dx
            dxb_ref[...] = dx.astype(BF16)
            dg_ref[...] += jnp.sum(dyv * xv * r, axis=0, keepdims=True)

    row = pl.BlockSpec((tm, Fd), lambda i, k: (i, 0))
    vec = pl.BlockSpec((1, Fd), lambda i, k: (0, 0))
    return pl.pallas_call(
        body, name=name, grid=(T // tm, nk),
        in_specs=[pl.BlockSpec((tm, tk), lambda i, k: (i, k)), pl.BlockSpec((tk, Fd), lambda i, k: (k, 0)),
                  row, vec, row],
        out_specs=[row, row, vec],
        out_shape=[jax.ShapeDtypeStruct((T, Fd), F32), jax.ShapeDtypeStruct((T, Fd), BF16),
                   jax.ShapeDtypeStruct((1, Fd), F32)],
        scratch_shapes=[pltpu.VMEM((tm, Fd), F32)],
        compiler_params=_params(("arbitrary", "arbitrary")),
    )(a, b, x, g, dres)


def _latent_bwd(proj, nq, nkv, dqa, dkn, dva, dkr, cos_k, sin_k, g_cq, g_ckv, w_uq_t, w_ukv_t, name="latent_bwd"):
    T = proj.shape[0]
    tm = _pick(T, 512)
    wl = _O2

    def rms_bwd(xv, gv, dyv):
        r = lax.rsqrt(jnp.mean(xv * xv, axis=-1, keepdims=True) + RMS_EPS)
        gdy = dyv * gv
        dot = jnp.mean(xv * gdy, axis=-1, keepdims=True)
        return r * gdy - xv * (r * r * r * dot), jnp.sum(dyv * xv * r, axis=0, keepdims=True)

    def body(p_ref, nq_ref, nkv_ref, dqa_ref, dkn_ref, dva_ref, dkr_ref, c_ref, s_ref, gq_ref, gkv_ref, wq_ref, wkv_ref,
             dlat_ref, dwq_ref, dwkv_ref, dgq_ref, dgkv_ref):
        @pl.when(pl.program_id(0) == 0)
        def _():
            for ref in (dwq_ref, dwkv_ref, dgq_ref, dgkv_ref):
                ref[...] = jnp.zeros_like(ref)

        dqv = dqa_ref[...]
        dkv = jnp.concatenate([dkn_ref[...], dva_ref[...]], axis=1)
        pv = p_ref[...]
        dc_q, dgq = rms_bwd(pv[:, :_O1], gq_ref[...], _dot(dqv, wq_ref[...], NN))
        dc_kv, dgkv = rms_bwd(pv[:, _O1:], gkv_ref[...], _dot(dkv, wkv_ref[...], NN))
        dkr_raw = _rotate(dkr_ref[...], c_ref[...], -s_ref[...])
        dlat_ref[...] = jnp.concatenate([dc_q, dc_kv, dkr_raw], axis=1).astype(BF16)
        dwq_ref[...] += _dot(dqv, nq_ref[...], TN)
        dwkv_ref[...] += _dot(dkv, nkv_ref[...], TN)
        dgq_ref[...] += dgq
        dgkv_ref[...] += dgkv

    row = lambda w: pl.BlockSpec((tm, w), lambda i: (i, 0))
    const = lambda a: pl.BlockSpec(a.shape, lambda i: (0, 0))
    outs = [jax.ShapeDtypeStruct((T, wl + LANES), BF16), jax.ShapeDtypeStruct(w_uq_t.shape, F32),
            jax.ShapeDtypeStruct(w_ukv_t.shape, F32), jax.ShapeDtypeStruct(g_cq.shape, F32),
            jax.ShapeDtypeStruct(g_ckv.shape, F32)]
    return pl.pallas_call(
        body, name=name, grid=(T // tm,),
        in_specs=[row(wl), row(_O1), row(_O2 - _O1), row(dqa.shape[1]), row(dkn.shape[1]), row(dva.shape[1]),
                  row(LANES), row(LANES), row(LANES), const(g_cq), const(g_ckv), const(w_uq_t), const(w_ukv_t)],
        out_specs=[row(wl + LANES)] + [const(o) for o in outs[1:]],
        out_shape=outs,
        compiler_params=_params(("arbitrary",)),
    )(proj, nq, nkv, dqa, dkn, dva, dkr, cos_k, sin_k, g_cq, g_ckv, w_uq_t, w_ukv_t)


def _loss_head(h, g, target, name="loss_head"):
    T, Fd = h.shape
    tm = _pick(T, 256)

    def body(h_ref, g_ref, t_ref, loss_ref, dh_ref, dhb_ref, dg_ref):
        xv = h_ref[...]
        r = lax.rsqrt(jnp.mean(xv * xv, axis=-1, keepdims=True) + RMS_EPS)
        diff = xv * r * g_ref[...] - t_ref[...]
        part = 0.5 * jnp.sum(jnp.mean(diff * diff, axis=-1, keepdims=True), axis=0, keepdims=True)
        dyv = diff * (1.0 / Fd)
        gdy = dyv * g_ref[...]
        dot = jnp.mean(xv * gdy, axis=-1, keepdims=True)
        dh = r * gdy - xv * (r * r * r * dot)
        dh_ref[...] = dh
        dhb_ref[...] = dh.astype(BF16)

        @pl.when(pl.program_id(0) == 0)
        def _():
            dg_ref[...] = jnp.zeros_like(dg_ref)
            loss_ref[...] = jnp.zeros_like(loss_ref)

        dg_ref[...] += jnp.sum(dyv * xv * r, axis=0, keepdims=True)
        loss_ref[...] += jnp.broadcast_to(part, loss_ref.shape)

    row = pl.BlockSpec((tm, Fd), lambda i: (i, 0))
    vec = pl.BlockSpec((1, Fd), lambda i: (0, 0))
    return pl.pallas_call(
        body, name=name, grid=(T // tm,),
        in_specs=[row, vec, row],
        out_specs=[pl.BlockSpec((1, LANES), lambda i: (0, 0)), row, row, vec],
        out_shape=[jax.ShapeDtypeStruct((1, LANES), F32), jax.ShapeDtypeStruct((T, Fd), F32),
                   jax.ShapeDtypeStruct((T, Fd), BF16), jax.ShapeDtypeStruct((1, Fd), F32)],
        compiler_params=_params(("arbitrary",)),
    )(h, g, target)


FFN_TF = 256


def _ffn_fwd(h, g, wg_t, wu_t, wd, name="ffn_fwd"):
    T, Dm = h.shape
    Fh = wd.shape[0]
    tm = _pick(T, 1024)
    nf = Fh // FFN_TF

    def body(h_ref, g_ref, wg_ref, wu_ref, wd_ref, o_ref, u_ref, a_ref, b_ref):
        j = pl.program_id(1)

        @pl.when(j == 0)
        def _():
            xv = h_ref[...]
            r = lax.rsqrt(jnp.mean(xv * xv, axis=-1, keepdims=True) + RMS_EPS)
            u_ref[...] = (xv * r * g_ref[...]).astype(BF16)
            o_ref[...] = xv

        u = u_ref[...]
        a = _dot(u, wg_ref[...], NT).astype(BF16)
        b = _dot(u, wu_ref[...], NT).astype(BF16)
        a_ref[...] = a
        b_ref[...] = b
        af = a.astype(F32)
        s = (af * jax.nn.sigmoid(af) * b.astype(F32)).astype(BF16)
        o_ref[...] += _dot(s, wd_ref[...], NN)

    row = pl.BlockSpec((tm, Dm), lambda i, j: (i, 0))
    wblk = pl.BlockSpec((FFN_TF, Dm), lambda i, j: (j, 0))
    ablk = pl.BlockSpec((tm, FFN_TF), lambda i, j: (i, j))
    return pl.pallas_call(
        body, name=name, grid=(T // tm, nf),
        in_specs=[row, pl.BlockSpec((1, Dm), lambda i, j: (0, 0)), wblk, wblk, wblk],
        out_specs=[row, row, ablk, ablk],
        out_shape=[jax.ShapeDtypeStruct((T, Dm), F32), jax.ShapeDtypeStruct((T, Dm), BF16),
                   jax.ShapeDtypeStruct((T, Fh), BF16), jax.ShapeDtypeStruct((T, Fh), BF16)],
        compiler_params=_params(("parallel", "arbitrary")),
    )(h, g, wg_t, wu_t, wd)


def _ffn_bwd(dh, u, a, b, wg_t, wu_t, wd, name="ffn_bwd"):
    T, Dm = dh.shape
    Fh = wd.shape[0]
    nf = Fh // FFN_TF
    once = pl.Buffered(1)

    def body(dh_ref, u_ref, a_ref, b_ref, wg_ref, wu_ref, wd_ref, du_ref, dwg_ref, dwu_ref, dwd_ref):
        j = pl.program_id(0)

        @pl.when(j == 0)
        def _():
            du_ref[...] = jnp.zeros_like(du_ref)

        ds = _dot(dh_ref[...], wd_ref[...], NT)
        af, bf = a_ref[...].astype(F32), b_ref[...].astype(F32)
        sig = jax.nn.sigmoid(af)
        sa = af * sig
        dwd_ref[...] = _dot((sa * bf).astype(BF16), dh_ref[...], TN).astype(BF16)
        dab = jnp.concatenate([(ds * bf * (sig * (1.0 + af * (1.0 - sig)))).astype(BF16),
                               (ds * sa).astype(BF16)], axis=1)
        dw = _dot(dab, u_ref[...], TN)
        dwg_ref[...] = dw[:FFN_TF].astype(BF16)
        dwu_ref[...] = dw[FFN_TF:].astype(BF16)
        du_ref[...] += _dot(dab, jnp.concatenate([wg_ref[...], wu_ref[...]], axis=0), NN)

    full = lambda: pl.BlockSpec((T, Dm), lambda j: (0, 0), pipeline_mode=once)
    wblk = pl.BlockSpec((FFN_TF, Dm), lambda j: (j, 0))
    ablk = pl.BlockSpec((T, FFN_TF), lambda j: (0, j))
    return pl.pallas_call(
        body, name=name, grid=(nf,),
        in_specs=[full(), full(), ablk, ablk, wblk, wblk, wblk],
        out_specs=[pl.BlockSpec((T, Dm), lambda j: (0, 0)), wblk, wblk, wblk],
        out_shape=[jax.ShapeDtypeStruct((T, Dm), F32)] + [jax.ShapeDtypeStruct((Fh, Dm), BF16)] * 3,
        compiler_params=_params(("arbitrary",)),
    )(dh, u, a, b, wg_t, wu_t, wd)


def _rope(x, cos_t, sin_t, col0, ncols, out_dtype, name="rope"):
    T = x.shape[0]
    wt = cos_t.shape[1]
    tm = _pick(T, 256)
    nb = ncols * LANES // wt
    half = MLA_ROPE // 2

    def body(x_ref, c_ref, s_ref, o_ref):
        xv = x_ref[...].astype(F32)
        lane = lax.broadcasted_iota(jnp.int32, xv.shape, 1)
        first = (lane & (MLA_ROPE - 1)) < half
        swapped = jnp.where(first, pltpu.roll(xv, wt - half, 1), pltpu.roll(xv, half, 1))
        o_ref[...] = (xv * c_ref[...] + swapped * s_ref[...]).astype(out_dtype)

    off = col0 * LANES // wt
    return pl.pallas_call(
        body, name=name, grid=(T // tm, nb),
        in_specs=[pl.BlockSpec((tm, wt), lambda i, j: (i, j + off)),
                  pl.BlockSpec((tm, wt), lambda i, j: (i, 0)),
                  pl.BlockSpec((tm, wt), lambda i, j: (i, 0))],
        out_specs=pl.BlockSpec((tm, wt), lambda i, j: (i, j)),
        out_shape=jax.ShapeDtypeStruct((T, ncols * LANES), out_dtype),
        compiler_params=_params(("parallel", "parallel")),
    )(x, cos_t, sin_t)


ATT_TQ = 512
ATT_TK = 256


def _mla_masks(shape):
    lane = lax.broadcasted_iota(jnp.int32, shape, 1)
    m0 = (lane < HEAD) | ((lane >= LANES) & (lane < LANES + MLA_ROPE))
    m1 = ((lane >= HEAD) & (lane < LANES)) | ((lane >= LANES + MLA_ROPE) & (lane < LANES + 2 * MLA_ROPE))
    return m0, m1


def _by_twos(n, step, carry):
    carry = lax.fori_loop(0, n // 2, lambda i, c: step(2 * i + 1, step(2 * i, c)), carry)
    return lax.fori_loop(0, n % 2, lambda _, c: step(n - 1, c), carry)


def _chunk_ok(tq, tk, d):
    row = lax.broadcasted_iota(jnp.int32, (tq, tk), 0)
    col = lax.broadcasted_iota(jnp.int32, (tq, tk), 1) + d * tk
    return jnp.concatenate([(col >> CHUNK_BITS) <= (row >> CHUNK_BITS)] * 2, axis=0)


def _rotate(x, cos_t, sin_t):
    half = MLA_ROPE // 2
    lane = lax.broadcasted_iota(jnp.int32, x.shape, 1)
    first = (lane & (MLA_ROPE - 1)) < half
    swapped = jnp.where(first, pltpu.roll(x, x.shape[1] - half, 1), pltpu.roll(x, half, 1))
    return x * cos_t + swapped * sin_t


def _mla_fwd(q, cos_q, sin_q, kv, kr, name="mla_fwd"):
    T = q.shape[0]
    tq, tk = _pick(T, ATT_TQ), _pick(T, ATT_TK)
    nd = tq // tk
    npair = MLA_HEADS // 2
    scale = (MLA_NOPE + MLA_ROPE) ** -0.5

    def body(q_ref, c_ref, s_ref, kn_ref, v_ref, kr_ref, o_ref, lse_ref):
        m_idx = pl.program_id(1)
        qv = _rotate(q_ref[...], c_ref[...], s_ref[...]).astype(BF16)
        m0, m1 = _mla_masks(qv.shape)
        qs = jnp.concatenate([jnp.where(m0, qv, 0), jnp.where(m1, qv, 0)], axis=0).astype(BF16)

        def block(kb, carry, ok):
            ks = pl.ds(pl.multiple_of(kb * tk, tk), tk)
            kcat = jnp.concatenate([kn_ref[ks, :], kr_ref[ks, :]], axis=1)
            mx, l, acc = carry
            s = _dot(qs, kcat, NT) * scale
            if ok is not None:
                s = jnp.where(ok, s, NEG)
            mn = jnp.maximum(mx, jnp.max(s, axis=-1, keepdims=True))
            alpha = jnp.exp(mx - mn)
            p = jnp.exp(s - mn)
            return (mn, alpha * l + jnp.sum(p, axis=-1, keepdims=True),
                    alpha * acc + _dot(p.astype(BF16), v_ref[ks, :], NN))

        init = (jnp.full((2 * tq, 1), NEG, F32), jnp.zeros((2 * tq, 1), F32), jnp.zeros((2 * tq, LANES), F32))
        res = init
        for d in range(nd):
            res = block(m_idx * nd + d, res, _chunk_ok(tq, tk, d))
        mx, l, acc = _by_twos(m_idx * nd, lambda kb, c: block(kb, c, None), res)
        h0 = lax.broadcasted_iota(jnp.int32, (tq, LANES), 1) < HEAD
        o_ref[...] = _two_heads(acc * (1.0 / l), h0).astype(o_ref.dtype)
        lse_ref[...] = _two_heads(jnp.broadcast_to(mx + jnp.log(l), (2 * tq, LANES)), h0)

    full = lambda col: pl.BlockSpec((T, LANES), col)
    table = pl.BlockSpec((tq, 2 * LANES), lambda p, m: (m, 0))
    return pl.pallas_call(
        body, name=name, grid=(npair, T // tq),
        in_specs=[pl.BlockSpec((tq, 2 * LANES), lambda p, m: (m, p)), table, table,
                  full(lambda p, m: (0, p)), full(lambda p, m: (0, npair + p)), full(lambda p, m: (0, 0))],
        out_specs=[pl.BlockSpec((tq, LANES), lambda p, m: (m, p)),
                   pl.BlockSpec((tq, LANES), lambda p, m: (m, p))],
        out_shape=[jax.ShapeDtypeStruct((T, npair * LANES), BF16),
                   jax.ShapeDtypeStruct((T, npair * LANES), F32)],
        compiler_params=_params(("parallel", "arbitrary")),
    )(q, cos_q, sin_q, kv, kv, kr)


def _mla_bwd(q, cos_q, sin_q, kv, kr, o, lse, do, do_col0, name="mla_bwd"):
    T = q.shape[0]
    tq, tk = _pick(T, ATT_TQ), _pick(T, ATT_TK)
    nd = tq // tk
    npair = MLA_HEADS // 2
    scale = (MLA_NOPE + MLA_ROPE) ** -0.5

    def body(q_ref, c_ref, s_ref, kn_ref, v_ref, kr_ref, o_ref, lse_ref, do_ref, dq_ref, dkn_ref, dv_ref, dkr_ref,
             dkn_acc, dv_acc):
        p_idx, m_idx = pl.program_id(0), pl.program_id(1)

        @pl.when(m_idx == 0)
        def _():
            dkn_acc[...] = jnp.zeros_like(dkn_acc)
            dv_acc[...] = jnp.zeros_like(dv_acc)

        @pl.when((m_idx == 0) & (p_idx == 0))
        def _():
            dkr_ref[...] = jnp.zeros_like(dkr_ref)

        qv = _rotate(q_ref[...], c_ref[...], s_ref[...]).astype(BF16)
        m0, m1 = _mla_masks(qv.shape)
        qs = jnp.concatenate([jnp.where(m0, qv, 0), jnp.where(m1, qv, 0)], axis=0).astype(BF16)
        dov = do_ref[...].astype(F32)
        h0 = lax.broadcasted_iota(jnp.int32, (tq, LANES), 1) < HEAD
        dos32 = jnp.concatenate([jnp.where(h0, dov, 0.0), jnp.where(h0, 0.0, dov)], axis=0)
        ov = o_ref[...].astype(F32)
        delta = jnp.sum(dos32 * jnp.concatenate([ov, ov], axis=0), axis=-1, keepdims=True)
        dos = dos32.astype(BF16)
        lsev = lse_ref[...]
        lse = jnp.concatenate([lsev[:, 0:1], lsev[:, HEAD:HEAD + 1]], axis=0)

        def block(kb, dq, ok):
            ks = pl.ds(pl.multiple_of(kb * tk, tk), tk)
            kcat = jnp.concatenate([kn_ref[ks, :], kr_ref[ks, :]], axis=1)
            vv = v_ref[ks, :]
            p = jnp.exp(_dot(qs, kcat, NT) * scale - lse)
            if ok is not None:
                p = jnp.where(ok, p, 0.0)
            ds = (p * (_dot(dos, vv, NT) - delta) * scale).astype(BF16)
            dkc = _dot(ds, qs, TN)
            dkn_acc[ks, :] += dkc[:, :LANES]
            dkr_ref[ks, :] += dkc[:, LANES:]
            dv_acc[ks, :] += _dot(p.astype(BF16), dos, TN)
            return dq + _dot(ds, kcat, NN)

        dq = jnp.zeros((2 * tq, 2 * LANES), F32)
        for d in range(nd):
            dq = block(m_idx * nd + d, dq, _chunk_ok(tq, tk, d))
        dq = _by_twos(m_idx * nd, lambda kb, c: block(kb, c, None), dq)
        dq_ref[...] = _rotate(jnp.where(m0, dq[:tq], jnp.where(m1, dq[tq:], 0.0)), c_ref[...],
                              -s_ref[...]).astype(BF16)

        @pl.when(m_idx == T // tq - 1)
        def _():
            dkn_ref[...] = dkn_acc[...].astype(BF16)
            dv_ref[...] = dv_acc[...].astype(BF16)

    full = lambda col: pl.BlockSpec((T, LANES), col)
    blk = lambda col: pl.BlockSpec((tq, LANES), col)
    table = pl.BlockSpec((tq, 2 * LANES), lambda p, m: (m, 0))
    return pl.pallas_call(
        body, name=name, grid=(npair, T // tq),
        in_specs=[pl.BlockSpec((tq, 2 * LANES), lambda p, m: (m, p)), table, table,
                  full(lambda p, m: (0, p)), full(lambda p, m: (0, npair + p)), full(lambda p, m: (0, 0)),
                  blk(lambda p, m: (m, p)), blk(lambda p, m: (m, p)),
                  blk(lambda p, m: (m, do_col0 + p))],
        out_specs=[pl.BlockSpec((tq, 2 * LANES), lambda p, m: (m, p)),
                   full(lambda p, m: (0, p)), full(lambda p, m: (0, p)), full(lambda p, m: (0, 0))],
        out_shape=[jax.ShapeDtypeStruct((T, npair * 2 * LANES), BF16),
                   jax.ShapeDtypeStruct((T, npair * LANES), BF16),
                   jax.ShapeDtypeStruct((T, npair * LANES), BF16),
                   jax.ShapeDtypeStruct((T, LANES), F32)],
        scratch_shapes=[pltpu.VMEM((T, LANES), F32)] * 2,
        compiler_params=_params(("arbitrary", "arbitrary")),
    )(q, cos_q, sin_q, kv, kv, kr, o, lse, do)


def _split_dot(x, tri):
    hi = x.astype(BF16)
    lo = (x - hi.astype(F32)).astype(BF16)
    both = _dot(jnp.concatenate([hi, lo], axis=0), tri, NN)
    return both[:x.shape[0]] + both[x.shape[0]:]


def _sb_terms(qh, kk, before):
    z = _dot(qh, kk, NT)
    sp = jnp.maximum(z, 0.0) + jnp.log(1.0 + jnp.exp(-jnp.abs(z)))
    lk = -sp if before is None else jnp.where(before, -sp, 0.0)
    return z, sp, lk


def _sb_setup(q_ref, tq, tk, scale):
    qv = (q_ref[...].astype(F32) * scale).astype(BF16)
    lane = lax.broadcasted_iota(jnp.int32, (tq, LANES), 1)
    h0 = lane < HEAD
    qs = jnp.concatenate([jnp.where(h0, qv, 0), jnp.where(h0, 0, qv)], axis=0).astype(BF16)
    row = lax.broadcasted_iota(jnp.int32, (tk, tk), 0)
    col = lax.broadcasted_iota(jnp.int32, (tk, tk), 1)
    return qs, h0, row, col


def _sb_before(tq, tk, d):
    row = lax.broadcasted_iota(jnp.int32, (tq, tk), 0)
    col = lax.broadcasted_iota(jnp.int32, (tq, tk), 1) + d * tk
    return jnp.concatenate([col < row] * 2, axis=0)


def _two_heads(x, h0):
    tq = x.shape[0] // 2
    return jnp.where(h0, x[:tq], x[tq:])


def _sb_fwd(qkv, col0, name="sb_fwd"):
    T = qkv.shape[0]
    tq, tk = _pick(T, ATT_TQ), _pick(T, ATT_TK)
    nd = tq // tk
    npair = SB_HEADS // 2
    scale = SB_DIM ** -0.5

    def body(q_ref, k_ref, v_ref, o_ref, o32_ref, w_ref, sp_ref):
        m_idx = pl.program_id(1)
        qs, h0, row, col = _sb_setup(q_ref, tq, tk, scale)
        later = (row > col).astype(BF16)

        def block(kb, carry, before):
            ks = pl.ds(pl.multiple_of(kb * tk, tk), tk)
            c, acc = carry
            z, sp, lk = _sb_terms(qs, k_ref[ks, :].astype(BF16), before)
            w = jnp.exp((z - sp) + _split_dot(lk, later) + c)
            if before is not None:
                w = jnp.where(before, w, 0.0)
            wb = w.astype(BF16)
            w_ref[0, 0, kb] = wb
            sp_ref[0, 0, kb] = sp.astype(BF16)
            return (c + jnp.sum(lk, axis=-1, keepdims=True), acc + _dot(wb, v_ref[ks, :].astype(BF16), NN))

        init = (jnp.zeros((2 * tq, 1), F32), jnp.zeros((2 * tq, LANES), F32))
        res = init
        for d in reversed(range(nd)):
            res = block(m_idx * nd + d, res, _sb_before(tq, tk, d))
        res = _by_twos(m_idx * nd, lambda i, c: block(m_idx * nd - 1 - i, c, None), res)
        o = _two_heads(res[1], h0)
        o_ref[...] = o.astype(o_ref.dtype)
        o32_ref[...] = o

    full = lambda col: pl.BlockSpec((T, LANES), col)
    blk = pl.BlockSpec((tq, LANES), lambda p, m: (m, p))
    return pl.pallas_call(
        body, name=name, grid=(npair, T // tq),
        in_specs=[pl.BlockSpec((tq, LANES), lambda p, m: (m, col0 + p)),
                  full(lambda p, m: (0, col0 + npair + p)), full(lambda p, m: (0, col0 + 2 * npair + p))],
        out_specs=[blk, blk] + [pl.BlockSpec((1, 1, T // tk, 2 * tq, tk), lambda p, m: (p, m, 0, 0, 0))] * 2,
        out_shape=[jax.ShapeDtypeStruct((T, npair * LANES), BF16), jax.ShapeDtypeStruct((T, npair * LANES), F32)]
        + [jax.ShapeDtypeStruct((npair, T // tq, T // tk, 2 * tq, tk), BF16)] * 2,
        compiler_params=_params(("parallel", "arbitrary")),
    )(qkv, qkv, qkv)


def _sb_bwd(qkv, col0, o32, w_all, sp_all, do, do_col0, dep, name="sb_bwd"):
    T = qkv.shape[0]
    tq, tk = _pick(T, ATT_TQ), _pick(T, ATT_TK)
    nd = tq // tk
    npair = SB_HEADS // 2
    scale = SB_DIM ** -0.5

    def body(q_ref, k_ref, v_ref, o_ref, w_ref, sp_ref, do_ref, dep_ref, dq_ref, dk_ref, dv_ref, dk_acc, dv_acc):
        m_idx = pl.program_id(1)

        @pl.when(m_idx == 0)
        def _():
            dk_acc[...] = jnp.zeros_like(dk_acc)
            dv_acc[...] = jnp.zeros_like(dv_acc)

        qs, h0, row, col = _sb_setup(q_ref, tq, tk, scale)
        dov = do_ref[...].astype(F32)
        dos = jnp.concatenate([jnp.where(h0, dov, 0.0), jnp.where(h0, 0.0, dov)], axis=0).astype(BF16)
        ov = o_ref[...]
        etot = jnp.sum(dos.astype(F32) * jnp.concatenate([ov, ov], axis=0), axis=-1, keepdims=True)
        from_here = (row >= col).astype(BF16)

        def block(kb, carry, before):
            ks = pl.ds(pl.multiple_of(kb * tk, tk), tk)
            kk = k_ref[ks, :].astype(BF16)
            vv = v_ref[ks, :].astype(BF16)
            es, dqa = carry
            wb = w_ref[0, 0, kb]
            e = wb.astype(F32) * _dot(dos, vv, NT)
            prev = etot - (_split_dot(e, from_here) + es)
            sig_neg = jnp.exp(-sp_ref[0, 0, kb].astype(F32))
            dz = e * sig_neg - (1.0 - sig_neg) * prev
            if before is not None:
                dz = jnp.where(before, dz, 0.0)
            dzb = dz.astype(BF16)
            dk_acc[ks, :] += _dot(dzb, qs, TN)
            dv_acc[ks, :] += _dot(wb, dos, TN)
            return es + jnp.sum(e, axis=-1, keepdims=True), dqa + _dot(dzb, kk, NN)

        init = (jnp.zeros((2 * tq, 1), F32), jnp.zeros((2 * tq, LANES), F32))
        res = init
        for d in reversed(range(nd)):
            res = block(m_idx * nd + d, res, _sb_before(tq, tk, d))
        res = _by_twos(m_idx * nd, lambda i, c: block(m_idx * nd - 1 - i, c, None), res)
        dq_ref[...] = (_two_heads(res[1], h0) * scale).astype(BF16)

        @pl.when(m_idx == T // tq - 1)
        def _():
            dk_ref[...] = dk_acc[...].astype(BF16)
            dv_ref[...] = dv_acc[...].astype(BF16)

    full = lambda col: pl.BlockSpec((T, LANES), col)
    blk = lambda col: pl.BlockSpec((tq, LANES), col)
    return pl.pallas_call(
        body, name=name, grid=(npair, T // tq),
        in_specs=[blk(lambda p, m: (m, col0 + p)),
                  full(lambda p, m: (0, col0 + npair + p)), full(lambda p, m: (0, col0 + 2 * npair + p)),
                  blk(lambda p, m: (m, p)),
                  pl.BlockSpec((1, 1, T // tk, 2 * tq, tk), lambda p, m: (p, m, 0, 0, 0)),
                  pl.BlockSpec((1, 1, T // tk, 2 * tq, tk), lambda p, m: (p, m, 0, 0, 0)),
                  blk(lambda p, m: (m, do_col0 + p)), pl.BlockSpec((8, LANES), lambda p, m: (0, 0))],
        out_specs=[blk(lambda p, m: (m, p)), full(lambda p, m: (0, p)), full(lambda p, m: (0, p))],
        out_shape=[jax.ShapeDtypeStruct((T, npair * LANES), BF16)] * 3,
        scratch_shapes=[pltpu.VMEM((T, LANES), F32)] * 2,
        compiler_params=_params(("arbitrary", "arbitrary")),
    )(qkv, qkv, qkv, o32, w_all, sp_all, do, dep)


def _band_in_window():
    cq = lax.broadcasted_iota(jnp.int32, (BAND_TQ, BAND_W), 0) >> CHUNK_BITS
    ckp = lax.broadcasted_iota(jnp.int32, (BAND_TQ, BAND_W), 1) >> CHUNK_BITS
    return (ckp >= cq) & (ckp <= cq + LEFT_CHUNKS)


def _band_real(m_idx):
    j = lax.broadcasted_iota(jnp.int32, (BAND_TQ, BAND_W), 1)
    return j >= PAD_KEYS - m_idx * BAND_TQ


def _band_probs(qh, kw, bias, real, scale):
    s = jnp.where(real, _dot(qh, kw, NT) * scale + bias, NEG)
    e = jnp.exp(s - jnp.max(s, axis=-1, keepdims=True))
    return e * (1.0 / jnp.sum(e, axis=-1, keepdims=True))


BAND_SUB = 4


def _band_fwd(qkv, k_pad, v_pad, bias_w, name="band_fwd"):
    T = qkv.shape[0]
    npair = C_HEADS // 2
    scale = C_DIM ** -0.5
    rows = BAND_SUB * BAND_TQ

    def body(q_ref, k_ref, v_ref, b_ref, o_ref, p_ref):
        lane = lax.broadcasted_iota(jnp.int32, (BAND_TQ, LANES), 1)
        h0 = lane < HEAD
        bias = jnp.concatenate([b_ref[0], b_ref[1]], axis=0)
        for sub in range(BAND_SUB):
            m_idx = pl.program_id(1) * BAND_SUB + sub
            win = pl.ds(pl.multiple_of(m_idx * BAND_TQ, BAND_TQ), BAND_W)
            kw, vw = k_ref[win, :], v_ref[win, :]
            qv = q_ref[sub * BAND_TQ:(sub + 1) * BAND_TQ, :]
            qs = jnp.concatenate([jnp.where(h0, qv, 0), jnp.where(h0, 0, qv)], axis=0).astype(BF16)
            p = _band_probs(qs, kw, bias, jnp.concatenate([_band_real(m_idx)] * 2, axis=0), scale).astype(BF16)
            p_ref[0, sub] = p
            o = _two_heads(_dot(p, vw, NN), h0)
            o_ref[sub * BAND_TQ:(sub + 1) * BAND_TQ, :] = o.astype(o_ref.dtype)

    Tp = T + PAD_KEYS
    return pl.pallas_call(
        body, name=name, grid=(npair, T // rows),
        in_specs=[pl.BlockSpec((rows, LANES), lambda p, m: (m, p)),
                  pl.BlockSpec((Tp, LANES), lambda p, m: (0, p)),
                  pl.BlockSpec((Tp, LANES), lambda p, m: (0, p)),
                  pl.BlockSpec((2, BAND_TQ, BAND_W), lambda p, m: (p, 0, 0))],
        out_specs=[pl.BlockSpec((rows, LANES), lambda p, m: (m, p)),
                   pl.BlockSpec((1, BAND_SUB, 2 * BAND_TQ, BAND_W), lambda p, m: (p, m, 0, 0))],
        out_shape=[jax.ShapeDtypeStruct((T, npair * LANES), BF16),
                   jax.ShapeDtypeStruct((npair, T // BAND_TQ, 2 * BAND_TQ, BAND_W), BF16)],
        compiler_params=_params(("parallel", "arbitrary")),
    )(qkv, k_pad, v_pad, bias_w)


def _band_bwd(qkv, k_pad, v_pad, probs, do, name="band_bwd"):
    T = qkv.shape[0]
    npair = C_HEADS // 2
    scale = C_DIM ** -0.5

    rows = BAND_SUB * BAND_TQ

    def body(q_ref, k_ref, v_ref, p_ref, do_ref, dq_ref, dk_ref, dv_ref, db_ref, dk_acc, dv_acc):
        @pl.when(pl.program_id(1) == 0)
        def _():
            dk_acc[...] = jnp.zeros_like(dk_acc)
            dv_acc[...] = jnp.zeros_like(dv_acc)
            db_ref[...] = jnp.zeros_like(db_ref)

        lane = lax.broadcasted_iota(jnp.int32, (BAND_TQ, LANES), 1)
        h0 = lane < HEAD
        dbs = jnp.zeros((2 * BAND_TQ, BAND_W), F32)
        for sub in range(BAND_SUB):
            m_idx = pl.program_id(1) * BAND_SUB + sub
            win = pl.ds(pl.multiple_of(m_idx * BAND_TQ, BAND_TQ), BAND_W)
            kw, vw = k_ref[win, :], v_ref[win, :]
            qv = q_ref[sub * BAND_TQ:(sub + 1) * BAND_TQ, :]
            dov = do_ref[sub * BAND_TQ:(sub + 1) * BAND_TQ, :].astype(F32)
            qs = jnp.concatenate([jnp.where(h0, qv, 0), jnp.where(h0, 0, qv)], axis=0).astype(BF16)
            dos = jnp.concatenate([jnp.where(h0, dov, 0.0), jnp.where(h0, 0.0, dov)], axis=0).astype(BF16)
            pb = p_ref[0, sub]
            p = pb.astype(F32)
            dp = _dot(dos, vw, NT)
            dsb = p * (dp - jnp.sum(p * dp, axis=-1, keepdims=True))
            dbs = dbs + dsb
            dsq = (dsb * scale).astype(BF16)
            dq_ref[sub * BAND_TQ:(sub + 1) * BAND_TQ, :] = _two_heads(_dot(dsq, kw, NN), h0).astype(BF16)
            dk_acc[win, :] += _dot(dsq, qs, TN)
            dv_acc[win, :] += _dot(pb, dos, TN)
        db_ref[0] += dbs[:BAND_TQ]
        db_ref[1] += dbs[BAND_TQ:]

        @pl.when(pl.program_id(1) == T // rows - 1)
        def _():
            dk_ref[...] = dk_acc[...].astype(BF16)
            dv_ref[...] = dv_acc[...].astype(BF16)

    Tp = T + PAD_KEYS
    blk = lambda col: pl.BlockSpec((rows, LANES), col)
    full = pl.BlockSpec((Tp, LANES), lambda p, m: (0, p))
    bias = pl.BlockSpec((2, BAND_TQ, BAND_W), lambda p, m: (p, 0, 0))
    prob = pl.BlockSpec((1, BAND_SUB, 2 * BAND_TQ, BAND_W), lambda p, m: (p, m, 0, 0))
    return pl.pallas_call(
        body, name=name, grid=(npair, T // rows),
        in_specs=[blk(lambda p, m: (m, p)), full, full, prob, blk(lambda p, m: (m, p))],
        out_specs=[blk(lambda p, m: (m, p)), full, full, bias],
        out_shape=[jax.ShapeDtypeStruct((T, npair * LANES), BF16),
                   jax.ShapeDtypeStruct((Tp, npair * LANES), BF16),
                   jax.ShapeDtypeStruct((Tp, npair * LANES), BF16),
                   jax.ShapeDtypeStruct((C_HEADS, BAND_TQ, BAND_W), F32)],
        scratch_shapes=[pltpu.VMEM((Tp, LANES), F32)] * 2,
        compiler_params=_params(("arbitrary", "arbitrary")),
    )(qkv, k_pad, v_pad, probs, do)


def _skew_bits(x, left):
    w = x.shape[1]
    row = lax.broadcasted_iota(jnp.int32, x.shape, 0)
    for b in range(BAND_TQ.bit_length() - 1):
        amt = (w - (1 << b)) if left else (1 << b)
        x = jnp.where((row >> b) & 1 == 1, pltpu.roll(x, amt, 1), x)
    return x


def _toeplitz(diag, name="toeplitz"):
    H = diag.shape[0]

    def body(d_ref, o_ref):
        x = jnp.broadcast_to(d_ref[0], (BAND_TQ, TOEP_W))
        o_ref[0] = jnp.where(_band_in_window(), _skew_bits(x, left=False)[:, BAND_TQ:], NEG)

    return pl.pallas_call(
        body, name=name, grid=(H,),
        in_specs=[pl.BlockSpec((1, 1, TOEP_W), lambda h: (h, 0, 0))],
        out_specs=pl.BlockSpec((1, BAND_TQ, BAND_W), lambda h: (h, 0, 0)),
        out_shape=jax.ShapeDtypeStruct((H, BAND_TQ, BAND_W), F32),
        compiler_params=_params(("parallel",)),
    )(diag.reshape(H, 1, TOEP_W))


def _toeplitz_bwd(dbias, name="toeplitz_bwd"):
    H = dbias.shape[0]

    def body(d_ref, o_ref):
        x = jnp.concatenate([jnp.zeros((BAND_TQ, BAND_TQ), F32), d_ref[0]], axis=1)
        h = BAND_TQ // 2
        while h >= 8:
            x = x[:h] + pltpu.roll(x[h:2 * h], TOEP_W - h, 1)
            h //= 2
        o_ref[0] = jnp.sum(_skew_bits(x, left=True), axis=0, keepdims=True)

    return pl.pallas_call(
        body, name=name, grid=(H,),
        in_specs=[pl.BlockSpec((1, BAND_TQ, BAND_W), lambda h: (h, 0, 0))],
        out_specs=pl.BlockSpec((1, 1, TOEP_W), lambda h: (h, 0, 0)),
        out_shape=jax.ShapeDtypeStruct((H, 1, TOEP_W), F32),
        compiler_params=_params(("parallel",)),
    )(dbias).reshape(H, TOEP_W)


_HBM = pl.BlockSpec(memory_space=pltpu.HBM)
_SEM = pl.BlockSpec(memory_space=pltpu.SEMAPHORE)
_EFFECT = pltpu.SideEffectType.DATAFLOW_SIDE_EFFECTING


def _peers():
    x, y, c = lax.axis_index("x"), lax.axis_index("y"), lax.axis_index("c")
    out = []
    for k in range(1, N_DEV):
        peer = (1 - x if (k >> 2) & 1 else x, 1 - y if (k >> 1) & 1 else y, 1 - c if k & 1 else c)
        out.append((peer, 4 * peer[0] + 2 * peer[1] + peer[2]))
    return 4 * x + 2 * y + c, out


def _split_copies(ins, lands, scatter, send_sem, recv_sem, arriving):
    me, peers = _peers()
    out = []
    for a in range(len(ins)):
        for peer, idx in peers:
            out.append(pltpu.make_async_remote_copy(
                src_ref=ins[a].at[idx] if scatter[a] else ins[a],
                dst_ref=lands[a].at[idx if arriving else me], send_sem=send_sem, recv_sem=recv_sem,
                device_id=peer, device_id_type=pl.DeviceIdType.MESH))
    return out


def _landing_zones(arrays, scatter):
    return [lax.empty((N_DEV,) + (a.shape[1:] if s else a.shape), a.dtype) for a, s in zip(arrays, scatter)]


def _place_own(arrays, scatter, name):
    n = len(arrays)
    lands = _landing_zones(arrays, scatter)
    me = (4 * lax.axis_index("x") + 2 * lax.axis_index("y") + lax.axis_index("c")).astype(jnp.int32).reshape(1)

    def body(me_ref, *refs):
        for a in range(n):
            refs[2 * n + a][...] = refs[a][...].reshape(refs[2 * n + a].shape)

    def row_spec(shape):
        zeros = (0,) * (len(shape) - 1)
        return pl.BlockSpec((1,) + tuple(shape[1:]), lambda i, me_ref: (me_ref[0],) + zeros)

    in_specs = [row_spec(a.shape) if s else pl.BlockSpec(a.shape, lambda i, me_ref, nd=a.ndim: (0,) * nd)
                for a, s in zip(arrays, scatter)]
    return pl.pallas_call(
        body, name=name,
        out_shape=[jax.ShapeDtypeStruct(l.shape, l.dtype) for l in lands],
        grid_spec=pltpu.PrefetchScalarGridSpec(
            num_scalar_prefetch=1, grid=(1,),
            in_specs=in_specs + [pl.BlockSpec(memory_space=pl.ANY)] * n,
            out_specs=[row_spec(l.shape) for l in lands]),
        input_output_aliases={1 + n + i: i for i in range(n)},
        compiler_params=_params(("arbitrary",)),
    )(me, *arrays, *lands)


def _exchange_start_groups(groups, scatter, after, name, lands=None):
    sizes = [len(g) for g in groups]
    arrays = [a for g in groups for a in g]
    n, ng = len(arrays), len(groups)
    flags = [scatter] * n
    if lands is None:
        lands = list(_place_own(arrays, flags, name=name.replace("_start_", "_own_")))
    else:
        lands = [l for g in lands for l in g]
    starts = np.cumsum([0] + sizes)

    def body(*refs):
        ins, lnd = refs[:n], refs[n:2 * n]
        sems = refs[2 * n + 1:2 * n + 1 + 2 * ng]
        token = refs[-1]
        for g in range(ng):
            sl = slice(starts[g], starts[g + 1])
            for cp in _split_copies(ins[sl], lnd[sl], flags[sl], sems[2 * g], sems[2 * g + 1], arriving=False):
                cp.start()
        token[...] = jnp.zeros_like(token)

    hbm = lambda a: pltpu.HBM(a.shape, a.dtype)
    out = pl.pallas_call(
        body, name=name,
        out_shape=(*[pltpu.SemaphoreType.DMA(())] * (2 * ng),
                   *[hbm(a) for a in arrays], *[hbm(a) for a in lands],
                   jax.ShapeDtypeStruct((8, LANES), F32)),
        in_specs=[_HBM] * (2 * n) + [pl.BlockSpec(memory_space=pl.ANY)],
        out_specs=(*[_SEM] * (2 * ng), *([_HBM] * (2 * n)), pl.BlockSpec(memory_space=pltpu.VMEM)),
        input_output_aliases={i: 2 * ng + i for i in range(2 * n)},
        compiler_params=pltpu.CompilerParams(has_side_effects=_EFFECT),
    )(*[pltpu.with_memory_space_constraint(a, pltpu.HBM) for a in list(arrays) + lands], after)
    ins_out, lands_out = out[2 * ng:2 * ng + n], out[2 * ng + n:2 * ng + 2 * n]
    handles = [(out[2 * g], out[2 * g + 1], list(ins_out[starts[g]:starts[g + 1]]),
                list(lands_out[starts[g]:starts[g + 1]]), tuple(flags[starts[g]:starts[g + 1]]))
               for g in range(ng)]
    return handles, out[-1]


def _exchange_start(arrays, scatter, after, name):
    assert len(set(scatter)) == 1
    handles, token = _exchange_start_groups([list(arrays)], scatter[0], after, name)
    return handles[0], token


def _exchange_wait(handle, after, name):
    send_sem, recv_sem, ins, lands, scatter = handle
    n = len(ins)
    after = after if isinstance(after, tuple) else (after,)

    def body(*refs):
        i_ref, l_ref = refs[:n], refs[n:2 * n]
        s_sem, r_sem = refs[2 * n:2 * n + 2]
        for cp in _split_copies(i_ref, l_ref, scatter, s_sem, r_sem, arriving=False):
            cp.wait_send()
        for cp in _split_copies(i_ref, l_ref, scatter, s_sem, r_sem, arriving=True):
            cp.wait_recv()

    hbm = lambda a: pltpu.HBM(a.shape, a.dtype)
    out = pl.pallas_call(
        body, name=name,
        out_shape=tuple(hbm(a) for a in ins + lands),
        in_specs=[_HBM] * (2 * n) + [_SEM, _SEM] + [pl.BlockSpec(memory_space=pl.ANY)] * len(after),
        out_specs=tuple([_HBM] * (2 * n)),
        input_output_aliases={i: i for i in range(2 * n)},
        compiler_params=pltpu.CompilerParams(has_side_effects=_EFFECT),
    )(*ins, *lands, send_sem, recv_sem, *after)
    return list(out[n:])


def _adamw(w, parts, m, v, name="adamw"):
    R, C = w.shape
    L = len(parts)
    rl = R // L
    tr = max([t for t in range(16, 257, 16) if rl % t == 0], default=rl)
    nb = rl // tr
    c1 = 1.0 - ADAM_B1 ** ADAM_STEP
    c2 = 1.0 - ADAM_B2 ** ADAM_STEP

    def body(*refs):
        w_ref, p_refs, (m_ref, v_ref, g_ref, d_ref, nm_ref, nv_ref) = refs[0], refs[1:1 + L], refs[1 + L:]
        g = None
        for j, p_ref in enumerate(p_refs):
            gj = p_ref[0].astype(F32)
            for i in range(1, N_DEV):
                gj = gj + p_ref[i].astype(F32)
            g = gj if g is None else jnp.where(pl.program_id(0) == j, gj, g)
        nm = ADAM_B1 * m_ref[...] + (1.0 - ADAM_B1) * g
        nv = ADAM_B2 * v_ref[...] + (1.0 - ADAM_B2) * (g * g)
        g_ref[...] = g
        nm_ref[...] = nm
        nv_ref[...] = nv
        d_ref[...] = -ADAM_LR * ((nm / c1) / (jnp.sqrt(nv / c2) + ADAM_EPS) + ADAM_WD * w_ref[...])

    blk = pl.BlockSpec((tr, C), lambda l, i: (l * nb + i, 0))
    part = lambda j: pl.BlockSpec((N_DEV, tr, C), lambda l, i: (0, jnp.where(l == j, i, 0), 0))
    return pl.pallas_call(
        body, name=name, grid=(L, nb),
        in_specs=[blk] + [part(j) for j in range(L)] + [blk, blk],
        out_specs=[blk] * 4,
        out_shape=[jax.ShapeDtypeStruct((R, C), F32)] * 4,
        compiler_params=_params(("arbitrary", "arbitrary")),
    )(w, *parts, m, v)


_O1 = Q_LORA
_O2 = _O1 + KV_LORA
_O3 = _O2 + MLA_ROPE
_NB = SB_HEADS * SB_DIM
IN_W = _O2 + LANES + 3 * _NB
COL_KR = _O2 // LANES
COL_SB = COL_KR + 1


def _w_in_local(w):
    kr = w[_O2:_O3]
    pad = jnp.zeros((LANES - 2 * MLA_ROPE, w.shape[1]), w.dtype)
    return jnp.concatenate([w[:_O2], kr, kr, pad, w[_O3:]], axis=0)


def _w_in_grad(g):
    kr = (g[_O2:_O2 + MLA_ROPE].astype(F32) + g[_O2 + MLA_ROPE:_O2 + 2 * MLA_ROPE].astype(F32)).astype(g.dtype)
    return jnp.concatenate([g[:_O2], kr, g[_O2 + LANES:]], axis=0)


def _w_uq_local(w):
    w3 = w.reshape(MLA_HEADS // 2, 2, MLA_NOPE + MLA_ROPE, w.shape[1])
    nope = w3[:, :, :MLA_NOPE].reshape(MLA_HEADS // 2, 2 * MLA_NOPE, w.shape[1])
    rope = w3[:, :, MLA_NOPE:].reshape(MLA_HEADS // 2, 2 * MLA_ROPE, w.shape[1])
    pad = jnp.zeros((MLA_HEADS // 2, LANES - 2 * MLA_ROPE, w.shape[1]), w.dtype)
    return jnp.concatenate([nope, rope, pad], axis=1).reshape(-1, w.shape[1])


def _w_uq_grad(g):
    g3 = g.reshape(MLA_HEADS // 2, 2 * LANES, g.shape[1])
    nope = g3[:, :2 * MLA_NOPE].reshape(MLA_HEADS // 2, 2, MLA_NOPE, g.shape[1])
    rope = g3[:, LANES:LANES + 2 * MLA_ROPE].reshape(MLA_HEADS // 2, 2, MLA_ROPE, g.shape[1])
    return jnp.concatenate([nope, rope], axis=2).reshape(-1, g.shape[1])


def _w_ukv_local(w):
    w3 = w.reshape(MLA_HEADS, MLA_NOPE + MLA_V, w.shape[1])
    return jnp.concatenate([w3[:, :MLA_NOPE].reshape(-1, w.shape[1]),
                            w3[:, MLA_NOPE:].reshape(-1, w.shape[1])], axis=0)


def _w_ukv_grad(g):
    half = MLA_HEADS * MLA_NOPE
    kn = g[:half].reshape(MLA_HEADS, MLA_NOPE, g.shape[1])
    vv = g[half:].reshape(MLA_HEADS, MLA_V, g.shape[1])
    return jnp.concatenate([kn, vv], axis=1).reshape(-1, g.shape[1])


def _rope_tables(T):
    pos = jnp.arange(T, dtype=F32)
    inv_freq = ROPE_THETA ** (-jnp.arange(0, MLA_ROPE, 2, dtype=F32) / MLA_ROPE)
    ang = pos[:, None] * inv_freq[None, :]
    cos, sin = jnp.cos(ang), jnp.sin(ang)
    ones = jnp.ones((T, LANES - 2 * MLA_ROPE), F32)
    cos_k = jnp.concatenate([cos, cos, cos, cos, ones], axis=1)
    sin_k = jnp.concatenate([-sin, sin, -sin, sin, 0.0 * ones], axis=1)
    cos_q = jnp.concatenate([jnp.ones((T, LANES), F32), cos_k], axis=1)
    sin_q = jnp.concatenate([jnp.zeros((T, LANES), F32), sin_k], axis=1)
    return cos_q, sin_q, cos_k, sin_k


def _bias_diag_index():
    ell = np.arange(TOEP_W)
    return np.clip(BAND_W - ell, -REL_CLIP, REL_CLIP) + REL_CLIP


def _local_step(x, target, small, get_weights, put_grads):
    T = x.shape[0]
    cos_q, sin_q, cos_k, sin_k = _rope_tables(T)
    G = {}
    W = dict(small)

    u0 = _rms_fwd(x, W["g_mix"][0:1], name="rms_mix0")
    bias_w = _toeplitz(W["od_rel_bias"][:, _bias_diag_index()])
    W.update(get_weights("in0", (u0, bias_w)))
    proj = _mm(u0, W["w_in_t"], dims="nt", name="proj_in")
    W.update(get_weights("mix0", proj))
    c_q, c_kv = proj[:, :_O1], proj[:, _O1:_O2]
    nq = _rms_fwd(c_q, W["g_cq"], name="rms_cq")
    nkv = _rms_fwd(c_kv, W["g_ckv"], name="rms_ckv")
    qa_raw = _mm(nq, W["w_uq_t"], dims="nt", name="proj_uq")
    kv = _mm(nkv, W["w_ukv_t"], dims="nt", out_dtype=BF16, name="proj_ukv")
    kr = _rope(proj, cos_k, sin_k, COL_KR, 1, BF16, name="rope_k")
    o_a, lse = _mla_fwd(qa_raw, cos_q, sin_q, kv, kr)
    o_b, o_b32, w_b, sp_b = _sb_fwd(proj, COL_SB)
    o_ab = jnp.concatenate([o_a, o_b], axis=1)
    h1 = _mm(o_ab, W["ev_w_out"], res=x, name="out_ev")

    def ffn_fwd(h, layer):
        W.update(get_weights(f"ffn{layer}", h))
        return _ffn_fwd(h, W["g_ffn"][layer:layer + 1], W[f"w_gate_t{layer}"], W[f"w_up_t{layer}"],
                        W[f"w_down{layer}"], name=f"ffn_fwd{layer}")

    h2, u1, a0, b0 = ffn_fwd(h1, 0)

    W.update(get_weights("mix1", h2))
    u2 = _rms_fwd(h2, W["g_mix"][1:2], name="rms_mix1")
    qkv = _mm(u2, W["od_w_qkv_t"], dims="nt", out_dtype=BF16, name="proj_qkv")
    nc = C_HEADS * C_DIM
    pad = ((PAD_KEYS, 0), (0, 0))
    k_pad, v_pad = jnp.pad(qkv[:, nc:2 * nc], pad), jnp.pad(qkv[:, 2 * nc:], pad)
    o_c, p_c = _band_fwd(qkv, k_pad, v_pad, bias_w)
    h3 = _mm(o_c, W["od_w_out"], res=h2, name="out_od")
    h4, u3, a1, b1 = ffn_fwd(h3, 1)

    loss, dh, dhb, G["g_final"] = _loss_head(h4, W["g_final"], target)

    def ffn_bwd(dh, dhb, h, u, a, b, layer):
        du, g_gate, g_up, g_down = _ffn_bwd(dhb, u, a, b, W[f"w_gate_t{layer}"], W[f"w_up_t{layer}"],
                                            W[f"w_down{layer}"], name=f"ffn_bwd{layer}")
        tok = put_grads(f"ffn{layer}", {"w_gate_t": g_gate, "w_up_t": g_up, "w_down": g_down})
        return _rms_bwd(h, W["g_ffn"][layer:layer + 1] + tok[:1, :1], du, dres=dh, name=f"rms_ffn_bwd{layer}")

    dh3, dh3b, g_gffn1 = ffn_bwd(dh, dhb, h3, u3, a1, b1, 1)

    do_c = _mm(dh3b, W["od_w_out"], dims="nt", name="out_od_dx")
    g_od_out = _mm(o_c, dh3b, dims="tn", out_dtype=BF16, name="out_od_dw")
    dq_c, dk_p, dv_p, dbias_w = _band_bwd(qkv, k_pad, v_pad, p_c, do_c)
    dqkv = jnp.concatenate([dq_c, dk_p[PAD_KEYS:], dv_p[PAD_KEYS:]], axis=1)
    tok = put_grads("mix1", {"od_w_qkv_t": _mm(dqkv, u2, dims="tn", out_dtype=BF16, name="proj_qkv_dw"),
                             "od_w_out": g_od_out})
    ddiag = _toeplitz_bwd(dbias_w)
    n_far = BAND_W - REL_CLIP + 1
    G["od_rel_bias"] = jnp.concatenate(
        [jnp.zeros((C_HEADS, REL_CLIP - BAND_TQ + 1), F32), ddiag[:, n_far:][:, ::-1],
         jnp.sum(ddiag[:, :n_far], axis=1, keepdims=True)], axis=1)
    dh2, dh2b, g_gmix1 = _mm_rms_bwd(dqkv, W["od_w_qkv_t"], h2, W["g_mix"][1:2] + tok[:1, :1], dh3,
                                     name="proj_qkv_dx")

    dh1, dh1b, g_gffn0 = ffn_bwd(dh2, dh2b, h1, u1, a0, b0, 0)
    G["g_ffn"] = jnp.concatenate([g_gffn0, g_gffn1], axis=0)

    do_ab = _mm(dh1b, W["ev_w_out"], dims="nt", name="out_ev_dx")
    g0 = {"ev_w_out": _mm(o_ab, dh1b, dims="tn", out_dtype=BF16, name="out_ev_dw")}
    dqa_raw, dkn, dva, dkr = _mla_bwd(qa_raw, cos_q, sin_q, kv, kr, o_a, lse, do_ab, 0)
    dlat, g0["w_uq_t"], g0["w_ukv_t"], G["g_cq"], G["g_ckv"] = _latent_bwd(
        proj, nq, nkv, dqa_raw, dkn, dva, dkr, cos_k, sin_k, W["g_cq"], W["g_ckv"], W["w_uq_t"], W["w_ukv_t"])
    tok = put_grads("mix0", g0)
    dqb, dkb, dvb = _sb_bwd(proj, COL_SB, o_b32, w_b, sp_b, do_ab, MLA_HEADS // 2, tok)
    dproj = jnp.concatenate([dlat, dqb, dkb, dvb], axis=1)
    tok = put_grads("in0", {"w_in_t": _mm(dproj, u0, dims="tn", name="proj_in_dw")})
    dx, _, g_gmix0 = _mm_rms_bwd(dproj, W["w_in_t"], x, W["g_mix"][0:1] + tok[:1, :1], dh1, name="proj_in_dx")
    G["g_mix"] = jnp.concatenate([g_gmix0, g_gmix1], axis=0)
    return loss[0, 0], dx, G


_BIG = ["ev_w_in", "ev_w_uq", "ev_w_ukv", "ev_w_out", "od_w_qkv", "od_w_out", "w_gate", "w_up", "w_down"]
_COL_SHARDED = {"ev_w_in", "ev_w_uq", "ev_w_ukv", "od_w_qkv", "w_gate", "w_up"}
_SMALL = ["ev_g_cq", "ev_g_ckv", "od_rel_bias", "g_mix", "g_ffn", "g_final"]
_GROUPS = {
    "in0": ["ev_w_in"],
    "mix0": ["ev_w_uq", "ev_w_ukv", "ev_w_out"],
    "ffn0": ["w_gate0", "w_up0", "w_down0"],
    "mix1": ["od_w_qkv", "od_w_out"],
    "ffn1": ["w_gate1", "w_up1", "w_down1"],
}
_GROUP_SRC = {n + str(l): (n, l) for n in ("w_gate", "w_up", "w_down") for l in (0, 1)}
_BATCHES = {"in0": ["in0"], "layer0": ["mix0", "ffn0"], "layer1": ["mix1", "ffn1"]}
_BATCH_OF = {grp: batch for batch, grps in _BATCHES.items() for grp in grps}
_SMALL_ROWS = 8
_SMALL_COLS = 1792


def _pack_small(vals):
    flat = jnp.concatenate([v.reshape(-1).astype(F32) for v in vals])
    flat = jnp.pad(flat, (0, _SMALL_ROWS * _SMALL_COLS - flat.shape[0]))
    return flat.reshape(_SMALL_ROWS, _SMALL_COLS)


def _unpack_small(packed, like):
    flat = packed.reshape(-1)
    out, off = [], 0
    for v in like:
        out.append(flat[off:off + v.size].reshape(v.shape))
        off += v.size
    return out


def kernel(x, ev_w_in, ev_g_cq, ev_w_uq, ev_g_ckv, ev_w_ukv, ev_w_out, od_w_qkv, od_rel_bias, od_w_out, g_mix, g_ffn, w_gate, w_up, w_down, g_final, loss_target, m_ev_w_in, m_ev_g_cq, m_ev_w_uq, m_ev_g_ckv, m_ev_w_ukv, m_ev_w_out, m_od_w_qkv, m_od_rel_bias, m_od_w_out, m_g_mix, m_g_ffn, m_w_gate, m_w_up, m_w_down, m_g_final, v_ev_w_in, v_ev_g_cq, v_ev_w_uq, v_ev_g_ckv, v_ev_w_ukv, v_ev_w_out, v_od_w_qkv, v_od_rel_bias, v_od_w_out, v_g_mix, v_g_ffn, v_w_gate, v_w_up, v_w_down, v_g_final):
    args = dict(locals())
    w = {n: args[n] for n in _BIG + _SMALL}
    mom = {n: args["m_" + n] for n in _BIG + _SMALL}
    var = {n: args["v_" + n] for n in _BIG + _SMALL}

    own = {}
    for grp, names in _GROUPS.items():
        for n in names:
            base, layer = _GROUP_SRC.get(n, (n, 0))
            shard = w[base][layer:layer + 1]
            own[n] = (jnp.swapaxes(shard, 1, 2) if base in _COL_SHARDED else shard).astype(BF16)
    placed = dict(zip(own, _place_own(list(own.values()), [False] * len(own), name="gather_own")))
    handles, token = _exchange_start_groups(
        [[own[n] for n in names] for names in _GROUPS.values()], False, x[0, :8, :LANES], name="gather_start",
        lands=[[placed[n] for n in names] for names in _GROUPS.values()])
    gather = dict(zip(_GROUPS, handles))

    def get_weights(grp, after):
        names = _GROUPS[grp]
        lands = _exchange_wait(gather[grp], token if after is None else after, name="gather_wait_" + grp)
        full = {n: l.reshape(-1, l.shape[-1]) for n, l in zip(names, lands)}
        if grp == "in0":
            return {"w_in_t": _w_in_local(full["ev_w_in"])}
        if grp == "mix0":
            return {"w_uq_t": _w_uq_local(full["ev_w_uq"]), "w_ukv_t": _w_ukv_local(full["ev_w_ukv"]),
                    "ev_w_out": full["ev_w_out"]}
        if grp == "mix1":
            return {"od_w_qkv_t": full["od_w_qkv"], "od_w_out": full["od_w_out"]}
        layer = grp[-1]
        return {"w_gate_t" + layer: full["w_gate" + layer], "w_up_t" + layer: full["w_up" + layer],
                "w_down" + layer: full["w_down" + layer]}

    scatter, pending = {}, {}

    def put_grads(grp, g):
        if grp == "in0":
            g = {"ev_w_in": _w_in_grad(g["w_in_t"])}
        elif grp == "mix0":
            g = {"ev_w_uq": _w_uq_grad(g["w_uq_t"]), "ev_w_ukv": _w_ukv_grad(g["w_ukv_t"]),
                 "ev_w_out": g["ev_w_out"]}
        elif grp == "mix1":
            g = {"od_w_qkv": g["od_w_qkv_t"], "od_w_out": g["od_w_out"]}
        else:
            layer = grp[-1]
            g = {"w_gate" + layer: g["w_gate_t"], "w_up" + layer: g["w_up_t"], "w_down" + layer: g["w_down"]}
        pending.update({n: v.reshape(N_DEV, 1, v.shape[0] // N_DEV, v.shape[1]).astype(BF16) for n, v in g.items()})
        batch = _BATCH_OF[grp]
        names = [n for gr in _BATCHES[batch] for n in _GROUPS[gr]]
        if not all(n in pending for n in names):
            return jnp.zeros((8, LANES), F32)
        send = [pending[n] for n in names]
        scatter[batch], tok = _exchange_start(send, [True] * len(names), send[0], name="scatter_start_" + batch)
        return tok

    small = {"g_cq": ev_g_cq, "g_ckv": ev_g_ckv, "od_rel_bias": od_rel_bias[0],
             "g_mix": g_mix + token[0, 0], "g_ffn": g_ffn, "g_final": g_final.reshape(1, -1)}
    loss_part, dx, G = _local_step(x[0], loss_target[0], small, get_weights, put_grads)
    g_small = _pack_small([G["g_cq"], G["g_ckv"], G["od_rel_bias"], G["g_mix"], G["g_ffn"], G["g_final"],
                           loss_part.reshape(1)])
    small_handle, _ = _exchange_start([g_small], [False], dx, name="gather_start_small")

    grads, deltas, new_m, new_v = {}, {}, {}, {}
    parts, after = {}, dx

    def wait_parts(batch, after):
        lands = _exchange_wait(scatter[batch], after, name="scatter_wait_" + batch)
        parts.update(zip([n for grp in _BATCHES[batch] for n in _GROUPS[grp]], lands))
        return lands[0]

    def adamw(n):
        col = n in _COL_SHARDED
        rows = lambda a: (jnp.swapaxes(a, 1, 2) if col else a).reshape(-1, a.shape[1 if col else 2])
        layers = [parts[n]] if n in parts else [parts[n + "0"], parts[n + "1"]]
        res = _adamw(rows(w[n]), [p.reshape(N_DEV, -1, p.shape[-1]) for p in layers], rows(mom[n]), rows(var[n]),
                     name="adamw_" + n)
        L, a1, a2 = w[n].shape
        back = lambda r: jnp.swapaxes(r.reshape(L, a2, a1), 1, 2) if col else r.reshape(L, a1, a2)
        grads[n], deltas[n], new_m[n], new_v[n] = [back(r) for r in res]
        return res[0]

    for batch in ("layer1", "layer0"):
        after = wait_parts(batch, after)
    after = wait_parts("in0", tuple(adamw(n) for n in _BIG[1:]))
    after = adamw("ev_w_in")
    small_w = [w[n] for n in _SMALL]
    small_parts = _exchange_wait(small_handle, after, name="gather_wait_small")[0]
    loss = jnp.sum(small_parts.reshape(N_DEV, -1)[:, sum(v.size for v in small_w)])
    res = _adamw(_pack_small(small_w), [small_parts], _pack_small([mom[n] for n in _SMALL]),
                 _pack_small([var[n] for n in _SMALL]), name="adamw_small")
    for d, packed in zip((grads, deltas, new_m, new_v), res):
        for n, val in zip(_SMALL, _unpack_small(packed, small_w)):
            d[n] = val

    order = ["ev_w_in", "ev_g_cq", "ev_w_uq", "ev_g_ckv", "ev_w_ukv", "ev_w_out", "od_w_qkv", "od_rel_bias",
             "od_w_out", "g_mix", "g_ffn", "w_gate", "w_up", "w_down", "g_final"]
    out = [loss, dx[None]]
    for d in (grads, deltas, new_m, new_v):
        out += [d[n] for n in order]
    return tuple(out)
```

```python
import functools

import numpy as np
import jax
import jax.numpy as jnp
from jax import lax
from jax.experimental import pallas as pl
from jax.experimental.pallas import tpu as pltpu

F32 = jnp.float32
BF16 = jnp.bfloat16

D_MODEL = 1024
CHUNK = 64
MLA_HEADS = 8
MLA_NOPE = 64
MLA_ROPE = 32
MLA_V = 64
Q_LORA = 384
KV_LORA = 256
ROPE_THETA = 10000.0
SB_HEADS = 8
SB_DIM = 64
C_HEADS = 16
C_DIM = 64
LEFT_CHUNKS = 8
REL_CLIP = 256
D_FF = 2816
RMS_EPS = 1e-6
ADAM_LR = 0.001
ADAM_B1 = 0.9
ADAM_B2 = 0.999
ADAM_EPS = 1e-08
ADAM_WD = 0.01
ADAM_STEP = 10

N_DEV = 8
LANES = 128
HEAD = 64
assert HEAD == MLA_NOPE == MLA_V == SB_DIM == C_DIM and 2 * HEAD == LANES
CHUNK_BITS = CHUNK.bit_length() - 1
assert 1 << CHUNK_BITS == CHUNK
VMEM_LIMIT = 56 * 1024 * 1024
NEG = -1e30
PAD_KEYS = LEFT_CHUNKS * CHUNK
BAND_TQ = 128
BAND_W = BAND_TQ + PAD_KEYS
TOEP_W = BAND_W + BAND_TQ

NN = (((1,), (0,)), ((), ()))
NT = (((1,), (1,)), ((), ()))
TN = (((0,), (0,)), ((), ()))


def _dot(a, b, dn):
    return lax.dot_general(a, b, dn, preferred_element_type=F32)


def _pick(dim, pref):
    if dim <= pref:
        return dim
    best = None
    for t in range(LANES, pref + 1, LANES):
        if dim % t == 0:
            best = t
    assert best is not None, (dim, pref)
    return best


def _params(sem):
    return pltpu.CompilerParams(dimension_semantics=sem, vmem_limit_bytes=VMEM_LIMIT)


def _mm(a, b, dims="nn", res=None, out_dtype=F32, name="mm"):
    if dims == "nn":
        (M, K), (K2, N) = a.shape, b.shape
    elif dims == "nt":
        (M, K), (N, K2) = a.shape, b.shape
    else:
        (K, M), (K2, N) = a.shape, b.shape
    assert K == K2, (a.shape, b.shape, dims)
    tm, tn, tk = _pick(M, 1024), _pick(N, 1152), _pick(K, 1024)
    nk = K // tk
    dn = {"nn": NN, "nt": NT, "tn": TN}[dims]
    has_res = res is not None

    def body(*refs):
        if has_res:
            a_ref, b_ref, r_ref, o_ref, acc = refs
        else:
            a_ref, b_ref, o_ref, acc = refs
        k = pl.program_id(2)

        @pl.when(k == 0)
        def _():
            acc[...] = jnp.zeros_like(acc)

        acc[...] += _dot(a_ref[...].astype(BF16), b_ref[...].astype(BF16), dn)

        @pl.when(k == nk - 1)
        def _():
            r = acc[...]
            if has_res:
                r = r + r_ref[...]
            o_ref[...] = r.astype(out_dtype)

    a_spec = (pl.BlockSpec((tk, tm), lambda i, j, k: (k, i)) if dims == "tn"
              else pl.BlockSpec((tm, tk), lambda i, j, k: (i, k)))
    b_spec = (pl.BlockSpec((tn, tk), lambda i, j, k: (j, k)) if dims == "nt"
              else pl.BlockSpec((tk, tn), lambda i, j, k: (k, j)))
    o_spec = pl.BlockSpec((tm, tn), lambda i, j, k: (i, j))
    in_specs = [a_spec, b_spec] + ([o_spec] if has_res else [])
    args = (a, b) + ((res,) if has_res else ())
    return pl.pallas_call(
        body, name=name, grid=(M // tm, N // tn, nk),
        in_specs=in_specs, out_specs=o_spec,
        out_shape=jax.ShapeDtypeStruct((M, N), out_dtype),
        scratch_shapes=[pltpu.VMEM((tm, tn), F32)],
        compiler_params=_params(("parallel", "parallel", "arbitrary")),
    )(*args)


def _rms_fwd(x, g, out_dtype=BF16, name="rms_fwd"):
    T, Fd = x.shape
    tm = _pick(T, 256)

    def body(x_ref, g_ref, o_ref):
        xv = x_ref[...]
        r = lax.rsqrt(jnp.mean(xv * xv, axis=-1, keepdims=True) + RMS_EPS)
        o_ref[...] = (xv * r * g_ref[...]).astype(out_dtype)

    return pl.pallas_call(
        body, name=name, grid=(T // tm,),
        in_specs=[pl.BlockSpec((tm, Fd), lambda i: (i, 0)), pl.BlockSpec((1, Fd), lambda i: (0, 0))],
        out_specs=pl.BlockSpec((tm, Fd), lambda i: (i, 0)),
        out_shape=jax.ShapeDtypeStruct((T, Fd), out_dtype),
        compiler_params=_params(("parallel",)),
    )(x, g)


def _rms_bwd(x, g, dy, dres=None, name="rms_bwd"):
    T, Fd = x.shape
    tm = _pick(T, 256)
    has_res = dres is not None

    def body(*refs):
        if has_res:
            x_ref, g_ref, dy_ref, r_ref, dx_ref, dxb_ref, dg_ref = refs
        else:
            x_ref, g_ref, dy_ref, dx_ref, dxb_ref, dg_ref = refs
        xv, dyv = x_ref[...], dy_ref[...]
        r = lax.rsqrt(jnp.mean(xv * xv, axis=-1, keepdims=True) + RMS_EPS)
        gdy = dyv * g_ref[...]
        dot = jnp.mean(xv * gdy, axis=-1, keepdims=True)
        dx = r * gdy - xv * (r * r * r * dot)
        if has_res:
            dx = dx + r_ref[...]
        dx_ref[...] = dx
        dxb_ref[...] = dx.astype(BF16)

        @pl.when(pl.program_id(0) == 0)
        def _():
            dg_ref[...] = jnp.zeros_like(dg_ref)

        dg_ref[...] += jnp.sum(dyv * xv * r, axis=0, keepdims=True)

    row = pl.BlockSpec((tm, Fd), lambda i: (i, 0))
    vec = pl.BlockSpec((1, Fd), lambda i: (0, 0))
    in_specs = [row, vec, row] + ([row] if has_res else [])
    args = (x, g, dy) + ((dres,) if has_res else ())
    return pl.pallas_call(
        body, name=name, grid=(T // tm,),
        in_specs=in_specs, out_specs=[row, row, vec],
        out_shape=[jax.ShapeDtypeStruct((T, Fd), F32), jax.ShapeDtypeStruct((T, Fd), BF16),
                   jax.ShapeDtypeStruct((1, Fd), F32)],
        compiler_params=_params(("arbitrary",)),
    )(*args)


def _mm_rms_bwd(a, b, x, g, dres, name="mm_rms_bwd"):
    T, K = a.shape
    Fd = b.shape[1]
    tm, tk = _pick(T, 512), _pick(K, 1024)
    nk = K // tk

    def body(a_ref, b_ref, x_ref, g_ref, r_ref, dx_ref, dxb_ref, dg_ref, acc):
        i, k = pl.program_id(0), pl.program_id(1)

        @pl.when(k == 0)
        def _():
            acc[...] = jnp.zeros_like(acc)

        @pl.when((k == 0) & (i == 0))
        def _():
            dg_ref[...] = jnp.zeros_like(dg_ref)

        acc[...] += _dot(a_ref[...].astype(BF16), b_ref[...].astype(BF16), NN)

        @pl.when(k == nk - 1)
        def _():
            xv, dyv = x_ref[...], acc[...]
            r = lax.rsqrt(jnp.mean(xv * xv, axis=-1, keepdims=True) + RMS_EPS)
            gdy = dyv * g_ref[...]
            dot = jnp.mean(xv * gdy, axis=-1, keepdims=True)
            dx = r * gdy - xv * (r * r * r * dot) + r_ref[...]
            dx_ref[...] = dx
            dxb_ref[...] = dx.astype(BF16)
            dg_ref[...] += jnp.sum(dyv * xv * r, axis=0, keepdims=True)

    row = pl.BlockSpec((tm, Fd), lambda i, k: (i, 0))
    vec = pl.BlockSpec((1, Fd), lambda i, k: (0, 0))
    return pl.pallas_call(
        body, name=name, grid=(T // tm, nk),
        in_specs=[pl.BlockSpec((tm, tk), lambda i, k: (i, k)), pl.BlockSpec((tk, Fd), lambda i, k: (k, 0)),
                  row, vec, row],
        out_specs=[row, row, vec],
        out_shape=[jax.ShapeDtypeStruct((T, Fd), F32), jax.ShapeDtypeStruct((T, Fd), BF16),
                   jax.ShapeDtypeStruct((1, Fd), F32)],
        scratch_shapes=[pltpu.VMEM((tm, Fd), F32)],
        compiler_params=_params(("arbitrary", "arbitrary")),
    )(a, b, x, g, dres)


def _latent_bwd(proj, nq, nkv, dqa, dkn, dva, dkr, cos_k, sin_k, g_cq, g_ckv, w_uq_t, w_ukv_t, name="latent_bwd"):
    T = proj.shape[0]
    tm = _pick(T, 512)
    wl = _O2

    def rms_bwd(xv, gv, dyv):
        r = lax.rsqrt(jnp.mean(xv * xv, axis=-1, keepdims=True) + RMS_EPS)
        gdy = dyv * gv
        dot = jnp.mean(xv * gdy, axis=-1, keepdims=True)
        return r * gdy - xv * (r * r * r * dot), jnp.sum(dyv * xv * r, axis=0, keepdims=True)

    def body(p_ref, nq_ref, nkv_ref, dqa_ref, dkn_ref, dva_ref, dkr_ref, c_ref, s_ref, gq_ref, gkv_ref, wq_ref, wkv_ref,
             dlat_ref, dwq_ref, dwkv_ref, dgq_ref, dgkv_ref):
        @pl.when(pl.program_id(0) == 0)
        def _():
            for ref in (dwq_ref, dwkv_ref, dgq_ref, dgkv_ref):
                ref[...] = jnp.zeros_like(ref)

        dqv = dqa_ref[...]
        dkv = jnp.concatenate([dkn_ref[...], dva_ref[...]], axis=1)
        pv = p_ref[...]
        dc_q, dgq = rms_bwd(pv[:, :_O1], gq_ref[...], _dot(dqv, wq_ref[...], NN))
        dc_kv, dgkv = rms_bwd(pv[:, _O1:], gkv_ref[...], _dot(dkv, wkv_ref[...], NN))
        dkr_raw = _rotate(dkr_ref[...], c_ref[...], -s_ref[...])
        dlat_ref[...] = jnp.concatenate([dc_q, dc_kv, dkr_raw], axis=1).astype(BF16)
        dwq_ref[...] += _dot(dqv, nq_ref[...], TN)
        dwkv_ref[...] += _dot(dkv, nkv_ref[...], TN)
        dgq_ref[...] += dgq
        dgkv_ref[...] += dgkv

    row = lambda w: pl.BlockSpec((tm, w), lambda i: (i, 0))
    const = lambda a: pl.BlockSpec(a.shape, lambda i: (0, 0))
    outs = [jax.ShapeDtypeStruct((T, wl + LANES), BF16), jax.ShapeDtypeStruct(w_uq_t.shape, F32),
            jax.ShapeDtypeStruct(w_ukv_t.shape, F32), jax.ShapeDtypeStruct(g_cq.shape, F32),
            jax.ShapeDtypeStruct(g_ckv.shape, F32)]
    return pl.pallas_call(
        body, name=name, grid=(T // tm,),
        in_specs=[row(wl), row(_O1), row(_O2 - _O1), row(dqa.shape[1]), row(dkn.shape[1]), row(dva.shape[1]),
                  row(LANES), row(LANES), row(LANES), const(g_cq), const(g_ckv), const(w_uq_t), const(w_ukv_t)],
        out_specs=[row(wl + LANES)] + [const(o) for o in outs[1:]],
        out_shape=outs,
        compiler_params=_params(("arbitrary",)),
    )(proj, nq, nkv, dqa, dkn, dva, dkr, cos_k, sin_k, g_cq, g_ckv, w_uq_t, w_ukv_t)


def _loss_head(h, g, target, name="loss_head"):
    T, Fd = h.shape
    tm = _pick(T, 256)

    def body(h_ref, g_ref, t_ref, loss_ref, dh_ref, dhb_ref, dg_ref):
        xv = h_ref[...]
        r = lax.rsqrt(jnp.mean(xv * xv, axis=-1, keepdims=True) + RMS_EPS)
        diff = xv * r * g_ref[...] - t_ref[...]
        part = 0.5 * jnp.sum(jnp.mean(diff * diff, axis=-1, keepdims=True), axis=0, keepdims=True)
        dyv = diff * (1.0 / Fd)
        gdy = dyv * g_ref[...]
        dot = jnp.mean(xv * gdy, axis=-1, keepdims=True)
        dh = r * gdy - xv * (r * r * r * dot)
        dh_ref[...] = dh
        dhb_ref[...] = dh.astype(BF16)

        @pl.when(pl.program_id(0) == 0)
        def _():
            dg_ref[...] = jnp.zeros_like(dg_ref)
            loss_ref[...] = jnp.zeros_like(loss_ref)

        dg_ref[...] += jnp.sum(dyv * xv * r, axis=0, keepdims=True)
        loss_ref[...] += jnp.broadcast_to(part, loss_ref.shape)

    row = pl.BlockSpec((tm, Fd), lambda i: (i, 0))
    vec = pl.BlockSpec((1, Fd), lambda i: (0, 0))
    return pl.pallas_call(
        body, name=name, grid=(T // tm,),
        in_specs=[row, vec, row],
        out_specs=[pl.BlockSpec((1, LANES), lambda i: (0, 0)), row, row, vec],
        out_shape=[jax.ShapeDtypeStruct((1, LANES), F32), jax.ShapeDtypeStruct((T, Fd), F32),
                   jax.ShapeDtypeStruct((T, Fd), BF16), jax.ShapeDtypeStruct((1, Fd), F32)],
        compiler_params=_params(("arbitrary",)),
    )(h, g, target)


FFN_TF = 256


def _ffn_fwd(h, g, wg_t, wu_t, wd, name="ffn_fwd"):
    T, Dm = h.shape
    Fh = wd.shape[0]
    tm = _pick(T, 1024)
    nf = Fh // FFN_TF

    def body(h_ref, g_ref, wg_ref, wu_ref, wd_ref, o_ref, u_ref, a_ref, b_ref):
        j = pl.program_id(1)

        @pl.when(j == 0)
        def _():
            xv = h_ref[...]
            r = lax.rsqrt(jnp.mean(xv * xv, axis=-1, keepdims=True) + RMS_EPS)
            u_ref[...] = (xv * r * g_ref[...]).astype(BF16)
            o_ref[...] = xv

        u = u_ref[...]
        a = _dot(u, wg_ref[...], NT).astype(BF16)
        b = _dot(u, wu_ref[...], NT).astype(BF16)
        a_ref[...] = a
        b_ref[...] = b
        af = a.astype(F32)
        s = (af * jax.nn.sigmoid(af) * b.astype(F32)).astype(BF16)
        o_ref[...] += _dot(s, wd_ref[...], NN)

    row = pl.BlockSpec((tm, Dm), lambda i, j: (i, 0))
    wblk = pl.BlockSpec((FFN_TF, Dm), lambda i, j: (j, 0))
    ablk = pl.BlockSpec((tm, FFN_TF), lambda i, j: (i, j))
    return pl.pallas_call(
        body, name=name, grid=(T // tm, nf),
        in_specs=[row, pl.BlockSpec((1, Dm), lambda i, j: (0, 0)), wblk, wblk, wblk],
        out_specs=[row, row, ablk, ablk],
        out_shape=[jax.ShapeDtypeStruct((T, Dm), F32), jax.ShapeDtypeStruct((T, Dm), BF16),
                   jax.ShapeDtypeStruct((T, Fh), BF16), jax.ShapeDtypeStruct((T, Fh), BF16)],
        compiler_params=_params(("parallel", "arbitrary")),
    )(h, g, wg_t, wu_t, wd)


def _ffn_bwd(dh, u, a, b, wg_t, wu_t, wd, name="ffn_bwd"):
    T, Dm = dh.shape
    Fh = wd.shape[0]
    nf = Fh // FFN_TF
    once = pl.Buffered(1)

    def body(dh_ref, u_ref, a_ref, b_ref, wg_ref, wu_ref, wd_ref, du_ref, dwg_ref, dwu_ref, dwd_ref):
        j = pl.program_id(0)

        @pl.when(j == 0)
        def _():
            du_ref[...] = jnp.zeros_like(du_ref)

        ds = _dot(dh_ref[...], wd_ref[...], NT)
        af, bf = a_ref[...].astype(F32), b_ref[...].astype(F32)
        sig = jax.nn.sigmoid(af)
        sa = af * sig
        dwd_ref[...] = _dot((sa * bf).astype(BF16), dh_ref[...], TN).astype(BF16)
        dab = jnp.concatenate([(ds * bf * (sig * (1.0 + af * (1.0 - sig)))).astype(BF16),
                               (ds * sa).astype(BF16)], axis=1)
        dw = _dot(dab, u_ref[...], TN)
        dwg_ref[...] = dw[:FFN_TF].astype(BF16)
        dwu_ref[...] = dw[FFN_TF:].astype(BF16)
        du_ref[...] += _dot(dab, jnp.concatenate([wg_ref[...], wu_ref[...]], axis=0), NN)

    full = lambda: pl.BlockSpec((T, Dm), lambda j: (0, 0), pipeline_mode=once)
    wblk = pl.BlockSpec((FFN_TF, Dm), lambda j: (j, 0))
    ablk = pl.BlockSpec((T, FFN_TF), lambda j: (0, j))
    return pl.pallas_call(
        body, name=name, grid=(nf,),
        in_specs=[full(), full(), ablk, ablk, wblk, wblk, wblk],
        out_specs=[pl.BlockSpec((T, Dm), lambda j: (0, 0)), wblk, wblk, wblk],
        out_shape=[jax.ShapeDtypeStruct((T, Dm), F32)] + [jax.ShapeDtypeStruct((Fh, Dm), BF16)] * 3,
        compiler_params=_params(("arbitrary",)),
    )(dh, u, a, b, wg_t, wu_t, wd)


def _rope(x, cos_t, sin_t, col0, ncols, out_dtype, name="rope"):
    T = x.shape[0]
    wt = cos_t.shape[1]
    tm = _pick(T, 256)
    nb = ncols * LANES // wt
    half = MLA_ROPE // 2

    def body(x_ref, c_ref, s_ref, o_ref):
        xv = x_ref[...].astype(F32)
        lane = lax.broadcasted_iota(jnp.int32, xv.shape, 1)
        first = (lane & (MLA_ROPE - 1)) < half
        swapped = jnp.where(first, pltpu.roll(xv, wt - half, 1), pltpu.roll(xv, half, 1))
        o_ref[...] = (xv * c_ref[...] + swapped * s_ref[...]).astype(out_dtype)

    off = col0 * LANES // wt
    return pl.pallas_call(
        body, name=name, grid=(T // tm, nb),
        in_specs=[pl.BlockSpec((tm, wt), lambda i, j: (i, j + off)),
                  pl.BlockSpec((tm, wt), lambda i, j: (i, 0)),
                  pl.BlockSpec((tm, wt), lambda i, j: (i, 0))],
        out_specs=pl.BlockSpec((tm, wt), lambda i, j: (i, j)),
        out_shape=jax.ShapeDtypeStruct((T, ncols * LANES), out_dtype),
        compiler_params=_params(("parallel", "parallel")),
    )(x, cos_t, sin_t)


ATT_TQ = 512
ATT_TK = 256


def _mla_masks(shape):
    lane = lax.broadcasted_iota(jnp.int32, shape, 1)
    m0 = (lane < HEAD) | ((lane >= LANES) & (lane < LANES + MLA_ROPE))
    m1 = ((lane >= HEAD) & (lane < LANES)) | ((lane >= LANES + MLA_ROPE) & (lane < LANES + 2 * MLA_ROPE))
    return m0, m1


def _by_twos(n, step, carry):
    carry = lax.fori_loop(0, n // 2, lambda i, c: step(2 * i + 1, step(2 * i, c)), carry)
    return lax.fori_loop(0, n % 2, lambda _, c: step(n - 1, c), carry)


def _chunk_ok(tq, tk, d):
    row = lax.broadcasted_iota(jnp.int32, (tq, tk), 0)
    col = lax.broadcasted_iota(jnp.int32, (tq, tk), 1) + d * tk
    return jnp.concatenate([(col >> CHUNK_BITS) <= (row >> CHUNK_BITS)] * 2, axis=0)


def _rotate(x, cos_t, sin_t):
    half = MLA_ROPE // 2
    lane = lax.broadcasted_iota(jnp.int32, x.shape, 1)
    first = (lane & (MLA_ROPE - 1)) < half
    swapped = jnp.where(first, pltpu.roll(x, x.shape[1] - half, 1), pltpu.roll(x, half, 1))
    return x * cos_t + swapped * sin_t


def _mla_fwd(q, cos_q, sin_q, kv, kr, name="mla_fwd"):
    T = q.shape[0]
    tq, tk = _pick(T, ATT_TQ), _pick(T, ATT_TK)
    nd = tq // tk
    npair = MLA_HEADS // 2
    scale = (MLA_NOPE + MLA_ROPE) ** -0.5

    def body(q_ref, c_ref, s_ref, kn_ref, v_ref, kr_ref, o_ref, lse_ref):
        m_idx = pl.program_id(1)
        qv = _rotate(q_ref[...], c_ref[...], s_ref[...]).astype(BF16)
        m0, m1 = _mla_masks(qv.shape)
        qs = jnp.concatenate([jnp.where(m0, qv, 0), jnp.where(m1, qv, 0)], axis=0).astype(BF16)

        def block(kb, carry, ok):
            ks = pl.ds(pl.multiple_of(kb * tk, tk), tk)
            kcat = jnp.concatenate([kn_ref[ks, :], kr_ref[ks, :]], axis=1)
            mx, l, acc = carry
            s = _dot(qs, kcat, NT) * scale
            if ok is not None:
                s = jnp.where(ok, s, NEG)
            mn = jnp.maximum(mx, jnp.max(s, axis=-1, keepdims=True))
            alpha = jnp.exp(mx - mn)
            p = jnp.exp(s - mn)
            return (mn, alpha * l + jnp.sum(p, axis=-1, keepdims=True),
                    alpha * acc + _dot(p.astype(BF16), v_ref[ks, :], NN))

        init = (jnp.full((2 * tq, 1), NEG, F32), jnp.zeros((2 * tq, 1), F32), jnp.zeros((2 * tq, LANES), F32))
        res = init
        for d in range(nd):
            res = block(m_idx * nd + d, res, _chunk_ok(tq, tk, d))
        mx, l, acc = _by_twos(m_idx * nd, lambda kb, c: block(kb, c, None), res)
        h0 = lax.broadcasted_iota(jnp.int32, (tq, LANES), 1) < HEAD
        o_ref[...] = _two_heads(acc * (1.0 / l), h0).astype(o_ref.dtype)
        lse_ref[...] = _two_heads(jnp.broadcast_to(mx + jnp.log(l), (2 * tq, LANES)), h0)

    full = lambda col: pl.BlockSpec((T, LANES), col)
    table = pl.BlockSpec((tq, 2 * LANES), lambda p, m: (m, 0))
    return pl.pallas_call(
        body, name=name, grid=(npair, T // tq),
        in_specs=[pl.BlockSpec((tq, 2 * LANES), lambda p, m: (m, p)), table, table,
                  full(lambda p, m: (0, p)), full(lambda p, m: (0, npair + p)), full(lambda p, m: (0, 0))],
        out_specs=[pl.BlockSpec((tq, LANES), lambda p, m: (m, p)),
                   pl.BlockSpec((tq, LANES), lambda p, m: (m, p))],
        out_shape=[jax.ShapeDtypeStruct((T, npair * LANES), BF16),
                   jax.ShapeDtypeStruct((T, npair * LANES), F32)],
        compiler_params=_params(("parallel", "arbitrary")),
    )(q, cos_q, sin_q, kv, kv, kr)


def _mla_bwd(q, cos_q, sin_q, kv, kr, o, lse, do, do_col0, name="mla_bwd"):
    T = q.shape[0]
    tq, tk = _pick(T, ATT_TQ), _pick(T, ATT_TK)
    nd = tq // tk
    npair = MLA_HEADS // 2
    scale = (MLA_NOPE + MLA_ROPE) ** -0.5

    def body(q_ref, c_ref, s_ref, kn_ref, v_ref, kr_ref, o_ref, lse_ref, do_ref, dq_ref, dkn_ref, dv_ref, dkr_ref,
             dkn_acc, dv_acc):
        p_idx, m_idx = pl.program_id(0), pl.program_id(1)

        @pl.when(m_idx == 0)
        def _():
            dkn_acc[...] = jnp.zeros_like(dkn_acc)
            dv_acc[...] = jnp.zeros_like(dv_acc)

        @pl.when((m_idx == 0) & (p_idx == 0))
        def _():
            dkr_ref[...] = jnp.zeros_like(dkr_ref)

        qv = _rotate(q_ref[...], c_ref[...], s_ref[...]).astype(BF16)
        m0, m1 = _mla_masks(qv.shape)
        qs = jnp.concatenate([jnp.where(m0, qv, 0), jnp.where(m1, qv, 0)], axis=0).astype(BF16)
        dov = do_ref[...].astype(F32)
        h0 = lax.broadcasted_iota(jnp.int32, (tq, LANES), 1) < HEAD
        dos32 = jnp.concatenate([jnp.where(h0, dov, 0.0), jnp.where(h0, 0.0, dov)], axis=0)
        ov = o_ref[...].astype(F32)
        delta = jnp.sum(dos32 * jnp.concatenate([ov, ov], axis=0), axis=-1, keepdims=True)
        dos = dos32.astype(BF16)
        lsev = lse_ref[...]
        lse = jnp.concatenate([lsev[:, 0:1], lsev[:, HEAD:HEAD + 1]], axis=0)

        def block(kb, dq, ok):
            ks = pl.ds(pl.multiple_of(kb * tk, tk), tk)
            kcat = jnp.concatenate([kn_ref[ks, :], kr_ref[ks, :]], axis=1)
            vv = v_ref[ks, :]
            p = jnp.exp(_dot(qs, kcat, NT) * scale - lse)
            if ok is not None:
                p = jnp.where(ok, p, 0.0)
            ds = (p * (_dot(dos, vv, NT) - delta) * scale).astype(BF16)
            dkc = _dot(ds, qs, TN)
            dkn_acc[ks, :] += dkc[:, :LANES]
            dkr_ref[ks, :] += dkc[:, LANES:]
            dv_acc[ks, :] += _dot(p.astype(BF16), dos, TN)
            return dq + _dot(ds, kcat, NN)

        dq = jnp.zeros((2 * tq, 2 * LANES), F32)
        for d in range(nd):
            dq = block(m_idx * nd + d, dq, _chunk_ok(tq, tk, d))
        dq = _by_twos(m_idx * nd, lambda kb, c: block(kb, c, None), dq)
        dq_ref[...] = _rotate(jnp.where(m0, dq[:tq], jnp.where(m1, dq[tq:], 0.0)), c_ref[...],
                              -s_ref[...]).astype(BF16)

        @pl.when(m_idx == T // tq - 1)
        def _():
            dkn_ref[...] = dkn_acc[...].astype(BF16)
            dv_ref[...] = dv_acc[...].astype(BF16)

    full = lambda col: pl.BlockSpec((T, LANES), col)
    blk = lambda col: pl.BlockSpec((tq, LANES), col)
    table = pl.BlockSpec((tq, 2 * LANES), lambda p, m: (m, 0))
    return pl.pallas_call(
        body, name=name, grid=(npair, T // tq),
        in_specs=[pl.BlockSpec((tq, 2 * LANES), lambda p, m: (m, p)), table, table,
                  full(lambda p, m: (0, p)), full(lambda p, m: (0, npair + p)), full(lambda p, m: (0, 0)),
                  blk(lambda p, m: (m, p)), blk(lambda p, m: (m, p)),
                  blk(lambda p, m: (m, do_col0 + p))],
        out_specs=[pl.BlockSpec((tq, 2 * LANES), lambda p, m: (m, p)),
                   full(lambda p, m: (0, p)), full(lambda p, m: (0, p)), full(lambda p, m: (0, 0))],
        out_shape=[jax.ShapeDtypeStruct((T, npair * 2 * LANES), BF16),
                   jax.ShapeDtypeStruct((T, npair * LANES), BF16),
                   jax.ShapeDtypeStruct((T, npair * LANES), BF16),
                   jax.ShapeDtypeStruct((T, LANES), F32)],
        scratch_shapes=[pltpu.VMEM((T, LANES), F32)] * 2,
        compiler_params=_params(("arbitrary", "arbitrary")),
    )(q, cos_q, sin_q, kv, kv, kr, o, lse, do)


def _split_dot(x, tri):
    hi = x.astype(BF16)
    lo = (x - hi.astype(F32)).astype(BF16)
    both = _dot(jnp.concatenate([hi, lo], axis=0), tri, NN)
    return both[:x.shape[0]] + both[x.shape[0]:]


def _sb_terms(qh, kk, before):
    z = _dot(qh, kk, NT)
    sp = jnp.maximum(z, 0.0) + jnp.log(1.0 + jnp.exp(-jnp.abs(z)))
    lk = -sp if before is None else jnp.where(before, -sp, 0.0)
    return z, sp, lk


def _sb_setup(q_ref, tq, tk, scale):
    qv = (q_ref[...].astype(F32) * scale).astype(BF16)
    lane = lax.broadcasted_iota(jnp.int32, (tq, LANES), 1)
    h0 = lane < HEAD
    qs = jnp.concatenate([jnp.where(h0, qv, 0), jnp.where(h0, 0, qv)], axis=0).astype(BF16)
    row = lax.broadcasted_iota(jnp.int32, (tk, tk), 0)
    col = lax.broadcasted_iota(jnp.int32, (tk, tk), 1)
    return qs, h0, row, col


def _sb_before(tq, tk, d):
    row = lax.broadcasted_iota(jnp.int32, (tq, tk), 0)
    col = lax.broadcasted_iota(jnp.int32, (tq, tk), 1) + d * tk
    return jnp.concatenate([col < row] * 2, axis=0)


def _two_heads(x, h0):
    tq = x.shape[0] // 2
    return jnp.where(h0, x[:tq], x[tq:])


def _sb_fwd(qkv, col0, name="sb_fwd"):
    T = qkv.shape[0]
    tq, tk = _pick(T, ATT_TQ), _pick(T, ATT_TK)
    nd = tq // tk
    npair = SB_HEADS // 2
    scale = SB_DIM ** -0.5

    def body(q_ref, k_ref, v_ref, o_ref, o32_ref, w_ref, sp_ref):
        m_idx = pl.program_id(1)
        qs, h0, row, col = _sb_setup(q_ref, tq, tk, scale)
        later = (row > col).astype(BF16)

        def block(kb, carry, before):
            ks = pl.ds(pl.multiple_of(kb * tk, tk), tk)
            c, acc = carry
            z, sp, lk = _sb_terms(qs, k_ref[ks, :].astype(BF16), before)
            w = jnp.exp((z - sp) + _split_dot(lk, later) + c)
            if before is not None:
                w = jnp.where(before, w, 0.0)
            wb = w.astype(BF16)
            w_ref[0, 0, kb] = wb
            sp_ref[0, 0, kb] = sp.astype(BF16)
            return (c + jnp.sum(lk, axis=-1, keepdims=True), acc + _dot(wb, v_ref[ks, :].astype(BF16), NN))

        init = (jnp.zeros((2 * tq, 1), F32), jnp.zeros((2 * tq, LANES), F32))
        res = init
        for d in reversed(range(nd)):
            res = block(m_idx * nd + d, res, _sb_before(tq, tk, d))
        res = _by_twos(m_idx * nd, lambda i, c: block(m_idx * nd - 1 - i, c, None), res)
        o = _two_heads(res[1], h0)
        o_ref[...] = o.astype(o_ref.dtype)
        o32_ref[...] = o

    full = lambda col: pl.BlockSpec((T, LANES), col)
    blk = pl.BlockSpec((tq, LANES), lambda p, m: (m, p))
    return pl.pallas_call(
        body, name=name, grid=(npair, T // tq),
        in_specs=[pl.BlockSpec((tq, LANES), lambda p, m: (m, col0 + p)),
                  full(lambda p, m: (0, col0 + npair + p)), full(lambda p, m: (0, col0 + 2 * npair + p))],
        out_specs=[blk, blk] + [pl.BlockSpec((1, 1, T // tk, 2 * tq, tk), lambda p, m: (p, m, 0, 0, 0))] * 2,
        out_shape=[jax.ShapeDtypeStruct((T, npair * LANES), BF16), jax.ShapeDtypeStruct((T, npair * LANES), F32)]
        + [jax.ShapeDtypeStruct((npair, T // tq, T // tk, 2 * tq, tk), BF16)] * 2,
        compiler_params=_params(("parallel", "arbitrary")),
    )(qkv, qkv, qkv)


def _sb_bwd(qkv, col0, o32, w_all, sp_all, do, do_col0, dep, name="sb_bwd"):
    T = qkv.shape[0]
    tq, tk = _pick(T, ATT_TQ), _pick(T, ATT_TK)
    nd = tq // tk
    npair = SB_HEADS // 2
    scale = SB_DIM ** -0.5

    def body(q_ref, k_ref, v_ref, o_ref, w_ref, sp_ref, do_ref, dep_ref, dq_ref, dk_ref, dv_ref, dk_acc, dv_acc):
        m_idx = pl.program_id(1)

        @pl.when(m_idx == 0)
        def _():
            dk_acc[...] = jnp.zeros_like(dk_acc)
            dv_acc[...] = jnp.zeros_like(dv_acc)

        qs, h0, row, col = _sb_setup(q_ref, tq, tk, scale)
        dov = do_ref[...].astype(F32)
        dos = jnp.concatenate([jnp.where(h0, dov, 0.0), jnp.where(h0, 0.0, dov)], axis=0).astype(BF16)
        ov = o_ref[...]
        etot = jnp.sum(dos.astype(F32) * jnp.concatenate([ov, ov], axis=0), axis=-1, keepdims=True)
        from_here = (row >= col).astype(BF16)

        def block(kb, carry, before):
            ks = pl.ds(pl.multiple_of(kb * tk, tk), tk)
            kk = k_ref[ks, :].astype(BF16)
            vv = v_ref[ks, :].astype(BF16)
            es, dqa = carry
            wb = w_ref[0, 0, kb]
            e = wb.astype(F32) * _dot(dos, vv, NT)
            prev = etot - (_split_dot(e, from_here) + es)
            sig_neg = jnp.exp(-sp_ref[0, 0, kb].astype(F32))
            dz = e * sig_neg - (1.0 - sig_neg) * prev
            if before is not None:
                dz = jnp.where(before, dz, 0.0)
            dzb = dz.astype(BF16)
            dk_acc[ks, :] += _dot(dzb, qs, TN)
            dv_acc[ks, :] += _dot(wb, dos, TN)
            return es + jnp.sum(e, axis=-1, keepdims=True), dqa + _dot(dzb, kk, NN)

        init = (jnp.zeros((2 * tq, 1), F32), jnp.zeros((2 * tq, LANES), F32))
        res = init
        for d in reversed(range(nd)):
            res = block(m_idx * nd + d, res, _sb_before(tq, tk, d))
        res = _by_twos(m_idx * nd, lambda i, c: block(m_idx * nd - 1 - i, c, None), res)
        dq_ref[...] = (_two_heads(res[1], h0) * scale).astype(BF16)

        @pl.when(m_idx == T // tq - 1)
        def _():
            dk_ref[...] = dk_acc[...].astype(BF16)
            dv_ref[...] = dv_acc[...].astype(BF16)

    full = lambda col: pl.BlockSpec((T, LANES), col)
    blk = lambda col: pl.BlockSpec((tq, LANES), col)
    return pl.pallas_call(
        body, name=name, grid=(npair, T // tq),
        in_specs=[blk(lambda p, m: (m, col0 + p)),
                  full(lambda p, m: (0, col0 + npair + p)), full(lambda p, m: (0, col0 + 2 * npair + p)),
                  blk(lambda p, m: (m, p)),
                  pl.BlockSpec((1, 1, T // tk, 2 * tq, tk), lambda p, m: (p, m, 0, 0, 0)),
                  pl.BlockSpec((1, 1, T // tk, 2 * tq, tk), lambda p, m: (p, m, 0, 0, 0)),
                  blk(lambda p, m: (m, do_col0 + p)), pl.BlockSpec((8, LANES), lambda p, m: (0, 0))],
        out_specs=[blk(lambda p, m: (m, p)), full(lambda p, m: (0, p)), full(lambda p, m: (0, p))],
        out_shape=[jax.ShapeDtypeStruct((T, npair * LANES), BF16)] * 3,
        scratch_shapes=[pltpu.VMEM((T, LANES), F32)] * 2,
        compiler_params=_params(("arbitrary", "arbitrary")),
    )(qkv, qkv, qkv, o32, w_all, sp_all, do, dep)


def _band_in_window():
    cq = lax.broadcasted_iota(jnp.int32, (BAND_TQ, BAND_W), 0) >> CHUNK_BITS
    ckp = lax.broadcasted_iota(jnp.int32, (BAND_TQ, BAND_W), 1) >> CHUNK_BITS
    return (ckp >= cq) & (ckp <= cq + LEFT_CHUNKS)


def _band_real(m_idx):
    j = lax.broadcasted_iota(jnp.int32, (BAND_TQ, BAND_W), 1)
    return j >= PAD_KEYS - m_idx * BAND_TQ


def _band_probs(qh, kw, bias, real, scale):
    s = jnp.where(real, _dot(qh, kw, NT) * scale + bias, NEG)
    e = jnp.exp(s - jnp.max(s, axis=-1, keepdims=True))
    return e * (1.0 / jnp.sum(e, axis=-1, keepdims=True))


BAND_SUB = 4


def _band_fwd(qkv, k_pad, v_pad, bias_w, name="band_fwd"):
    T = qkv.shape[0]
    npair = C_HEADS // 2
    scale = C_DIM ** -0.5
    rows = BAND_SUB * BAND_TQ

    def body(q_ref, k_ref, v_ref, b_ref, o_ref, p_ref):
        lane = lax.broadcasted_iota(jnp.int32, (BAND_TQ, LANES), 1)
        h0 = lane < HEAD
        bias = jnp.concatenate([b_ref[0], b_ref[1]], axis=0)
        for sub in range(BAND_SUB):
            m_idx = pl.program_id(1) * BAND_SUB + sub
            win = pl.ds(pl.multiple_of(m_idx * BAND_TQ, BAND_TQ), BAND_W)
            kw, vw = k_ref[win, :], v_ref[win, :]
            qv = q_ref[sub * BAND_TQ:(sub + 1) * BAND_TQ, :]
            qs = jnp.concatenate([jnp.where(h0, qv, 0), jnp.where(h0, 0, qv)], axis=0).astype(BF16)
            p = _band_probs(qs, kw, bias, jnp.concatenate([_band_real(m_idx)] * 2, axis=0), scale).astype(BF16)
            p_ref[0, sub] = p
            o = _two_heads(_dot(p, vw, NN), h0)
            o_ref[sub * BAND_TQ:(sub + 1) * BAND_TQ, :] = o.astype(o_ref.dtype)

    Tp = T + PAD_KEYS
    return pl.pallas_call(
        body, name=name, grid=(npair, T // rows),
        in_specs=[pl.BlockSpec((rows, LANES), lambda p, m: (m, p)),
                  pl.BlockSpec((Tp, LANES), lambda p, m: (0, p)),
                  pl.BlockSpec((Tp, LANES), lambda p, m: (0, p)),
                  pl.BlockSpec((2, BAND_TQ, BAND_W), lambda p, m: (p, 0, 0))],
        out_specs=[pl.BlockSpec((rows, LANES), lambda p, m: (m, p)),
                   pl.BlockSpec((1, BAND_SUB, 2 * BAND_TQ, BAND_W), lambda p, m: (p, m, 0, 0))],
        out_shape=[jax.ShapeDtypeStruct((T, npair * LANES), BF16),
                   jax.ShapeDtypeStruct((npair, T // BAND_TQ, 2 * BAND_TQ, BAND_W), BF16)],
        compiler_params=_params(("parallel", "arbitrary")),
    )(qkv, k_pad, v_pad, bias_w)


def _band_bwd(qkv, k_pad, v_pad, probs, do, name="band_bwd"):
    T = qkv.shape[0]
    npair = C_HEADS // 2
    scale = C_DIM ** -0.5

    rows = BAND_SUB * BAND_TQ

    def body(q_ref, k_ref, v_ref, p_ref, do_ref, dq_ref, dk_ref, dv_ref, db_ref, dk_acc, dv_acc):
        @pl.when(pl.program_id(1) == 0)
        def _():
            dk_acc[...] = jnp.zeros_like(dk_acc)
            dv_acc[...] = jnp.zeros_like(dv_acc)
            db_ref[...] = jnp.zeros_like(db_ref)

        lane = lax.broadcasted_iota(jnp.int32, (BAND_TQ, LANES), 1)
        h0 = lane < HEAD
        dbs = jnp.zeros((2 * BAND_TQ, BAND_W), F32)
        for sub in range(BAND_SUB):
            m_idx = pl.program_id(1) * BAND_SUB + sub
            win = pl.ds(pl.multiple_of(m_idx * BAND_TQ, BAND_TQ), BAND_W)
            kw, vw = k_ref[win, :], v_ref[win, :]
            qv = q_ref[sub * BAND_TQ:(sub + 1) * BAND_TQ, :]
            dov = do_ref[sub * BAND_TQ:(sub + 1) * BAND_TQ, :].astype(F32)
            qs = jnp.concatenate([jnp.where(h0, qv, 0), jnp.where(h0, 0, qv)], axis=0).astype(BF16)
            dos = jnp.concatenate([jnp.where(h0, dov, 0.0), jnp.where(h0, 0.0, dov)], axis=0).astype(BF16)
            pb = p_ref[0, sub]
            p = pb.astype(F32)
            dp = _dot(dos, vw, NT)
            dsb = p * (dp - jnp.sum(p * dp, axis=-1, keepdims=True))
            dbs = dbs + dsb
            dsq = (dsb * scale).astype(BF16)
            dq_ref[sub * BAND_TQ:(sub + 1) * BAND_TQ, :] = _two_heads(_dot(dsq, kw, NN), h0).astype(BF16)
            dk_acc[win, :] += _dot(dsq, qs, TN)
            dv_acc[win, :] += _dot(pb, dos, TN)
        db_ref[0] += dbs[:BAND_TQ]
        db_ref[1] += dbs[BAND_TQ:]

        @pl.when(pl.program_id(1) == T // rows - 1)
        def _():
            dk_ref[...] = dk_acc[...].astype(BF16)
            dv_ref[...] = dv_acc[...].astype(BF16)

    Tp = T + PAD_KEYS
    blk = lambda col: pl.BlockSpec((rows, LANES), col)
    full = pl.BlockSpec((Tp, LANES), lambda p, m: (0, p))
    bias = pl.BlockSpec((2, BAND_TQ, BAND_W), lambda p, m: (p, 0, 0))
    prob = pl.BlockSpec((1, BAND_SUB, 2 * BAND_TQ, BAND_W), lambda p, m: (p, m, 0, 0))
    return pl.pallas_call(
        body, name=name, grid=(npair, T // rows),
        in_specs=[blk(lambda p, m: (m, p)), full, full, prob, blk(lambda p, m: (m, p))],
        out_specs=[blk(lambda p, m: (m, p)), full, full, bias],
        out_shape=[jax.ShapeDtypeStruct((T, npair * LANES), BF16),
                   jax.ShapeDtypeStruct((Tp, npair * LANES), BF16),
                   jax.ShapeDtypeStruct((Tp, npair * LANES), BF16),
                   jax.ShapeDtypeStruct((C_HEADS, BAND_TQ, BAND_W), F32)],
        scratch_shapes=[pltpu.VMEM((Tp, LANES), F32)] * 2,
        compiler_params=_params(("arbitrary", "arbitrary")),
    )(qkv, k_pad, v_pad, probs, do)


def _skew_bits(x, left):
    w = x.shape[1]
    row = lax.broadcasted_iota(jnp.int32, x.shape, 0)
    for b in range(BAND_TQ.bit_length() - 1):
        amt = (w - (1 << b)) if left else (1 << b)
        x = jnp.where((row >> b) & 1 == 1, pltpu.roll(x, amt, 1), x)
    return x


def _toeplitz(diag, name="toeplitz"):
    H = diag.shape[0]

    def body(d_ref, o_ref):
        x = jnp.broadcast_to(d_ref[0], (BAND_TQ, TOEP_W))
        o_ref[0] = jnp.where(_band_in_window(), _skew_bits(x, left=False)[:, BAND_TQ:], NEG)

    return pl.pallas_call(
        body, name=name, grid=(H,),
        in_specs=[pl.BlockSpec((1, 1, TOEP_W), lambda h: (h, 0, 0))],
        out_specs=pl.BlockSpec((1, BAND_TQ, BAND_W), lambda h: (h, 0, 0)),
        out_shape=jax.ShapeDtypeStruct((H, BAND_TQ, BAND_W), F32),
        compiler_params=_params(("parallel",)),
    )(diag.reshape(H, 1, TOEP_W))


def _toeplitz_bwd(dbias, name="toeplitz_bwd"):
    H = dbias.shape[0]

    def body(d_ref, o_ref):
        x = jnp.concatenate([jnp.zeros((BAND_TQ, BAND_TQ), F32), d_ref[0]], axis=1)
        h = BAND_TQ // 2
        while h >= 8:
            x = x[:h] + pltpu.roll(x[h:2 * h], TOEP_W - h, 1)
            h //= 2
        o_ref[0] = jnp.sum(_skew_bits(x, left=True), axis=0, keepdims=True)

    return pl.pallas_call(
        body, name=name, grid=(H,),
        in_specs=[pl.BlockSpec((1, BAND_TQ, BAND_W), lambda h: (h, 0, 0))],
        out_specs=pl.BlockSpec((1, 1, TOEP_W), lambda h: (h, 0, 0)),
        out_shape=jax.ShapeDtypeStruct((H, 1, TOEP_W), F32),
        compiler_params=_params(("parallel",)),
    )(dbias).reshape(H, TOEP_W)


_HBM = pl.BlockSpec(memory_space=pltpu.HBM)
_SEM = pl.BlockSpec(memory_space=pltpu.SEMAPHORE)
_EFFECT = pltpu.SideEffectType.DATAFLOW_SIDE_EFFECTING


def _peers():
    x, y, c = lax.axis_index("x"), lax.axis_index("y"), lax.axis_index("c")
    out = []
    for k in range(1, N_DEV):
        peer = (1 - x if (k >> 2) & 1 else x, 1 - y if (k >> 1) & 1 else y, 1 - c if k & 1 else c)
        out.append((peer, 4 * peer[0] + 2 * peer[1] + peer[2]))
    return 4 * x + 2 * y + c, out


def _split_copies(ins, lands, scatter, send_sem, recv_sem, arriving):
    me, peers = _peers()
    out = []
    for a in range(len(ins)):
        for peer, idx in peers:
            out.append(pltpu.make_async_remote_copy(
                src_ref=ins[a].at[idx] if scatter[a] else ins[a],
                dst_ref=lands[a].at[idx if arriving else me], send_sem=send_sem, recv_sem=recv_sem,
                device_id=peer, device_id_type=pl.DeviceIdType.MESH))
    return out


def _landing_zones(arrays, scatter):
    return [lax.empty((N_DEV,) + (a.shape[1:] if s else a.shape), a.dtype) for a, s in zip(arrays, scatter)]


def _place_own(arrays, scatter, name):
    n = len(arrays)
    lands = _landing_zones(arrays, scatter)
    me = (4 * lax.axis_index("x") + 2 * lax.axis_index("y") + lax.axis_index("c")).astype(jnp.int32).reshape(1)

    def body(me_ref, *refs):
        for a in range(n):
            refs[2 * n + a][...] = refs[a][...].reshape(refs[2 * n + a].shape)

    def row_spec(shape):
        zeros = (0,) * (len(shape) - 1)
        return pl.BlockSpec((1,) + tuple(shape[1:]), lambda i, me_ref: (me_ref[0],) + zeros)

    in_specs = [row_spec(a.shape) if s else pl.BlockSpec(a.shape, lambda i, me_ref, nd=a.ndim: (0,) * nd)
                for a, s in zip(arrays, scatter)]
    return pl.pallas_call(
        body, name=name,
        out_shape=[jax.ShapeDtypeStruct(l.shape, l.dtype) for l in lands],
        grid_spec=pltpu.PrefetchScalarGridSpec(
            num_scalar_prefetch=1, grid=(1,),
            in_specs=in_specs + [pl.BlockSpec(memory_space=pl.ANY)] * n,
            out_specs=[row_spec(l.shape) for l in lands]),
        input_output_aliases={1 + n + i: i for i in range(n)},
        compiler_params=_params(("arbitrary",)),
    )(me, *arrays, *lands)


def _exchange_start_groups(groups, scatter, after, name, lands=None):
    sizes = [len(g) for g in groups]
    arrays = [a for g in groups for a in g]
    n, ng = len(arrays), len(groups)
    flags = [scatter] * n
    if lands is None:
        lands = list(_place_own(arrays, flags, name=name.replace("_start_", "_own_")))
    else:
        lands = [l for g in lands for l in g]
    starts = np.cumsum([0] + sizes)

    def body(*refs):
        ins, lnd = refs[:n], refs[n:2 * n]
        sems = refs[2 * n + 1:2 * n + 1 + 2 * ng]
        token = refs[-1]
        for g in range(ng):
            sl = slice(starts[g], starts[g + 1])
            for cp in _split_copies(ins[sl], lnd[sl], flags[sl], sems[2 * g], sems[2 * g + 1], arriving=False):
                cp.start()
        token[...] = jnp.zeros_like(token)

    hbm = lambda a: pltpu.HBM(a.shape, a.dtype)
    out = pl.pallas_call(
        body, name=name,
        out_shape=(*[pltpu.SemaphoreType.DMA(())] * (2 * ng),
                   *[hbm(a) for a in arrays], *[hbm(a) for a in lands],
                   jax.ShapeDtypeStruct((8, LANES), F32)),
        in_specs=[_HBM] * (2 * n) + [pl.BlockSpec(memory_space=pl.ANY)],
        out_specs=(*[_SEM] * (2 * ng), *([_HBM] * (2 * n)), pl.BlockSpec(memory_space=pltpu.VMEM)),
        input_output_aliases={i: 2 * ng + i for i in range(2 * n)},
        compiler_params=pltpu.CompilerParams(has_side_effects=_EFFECT),
    )(*[pltpu.with_memory_space_constraint(a, pltpu.HBM) for a in list(arrays) + lands], after)
    ins_out, lands_out = out[2 * ng:2 * ng + n], out[2 * ng + n:2 * ng + 2 * n]
    handles = [(out[2 * g], out[2 * g + 1], list(ins_out[starts[g]:starts[g + 1]]),
                list(lands_out[starts[g]:starts[g + 1]]), tuple(flags[starts[g]:starts[g + 1]]))
               for g in range(ng)]
    return handles, out[-1]


def _exchange_start(arrays, scatter, after, name):
    assert len(set(scatter)) == 1
    handles, token = _exchange_start_groups([list(arrays)], scatter[0], after, name)
    return handles[0], token


def _exchange_wait(handle, after, name):
    send_sem, recv_sem, ins, lands, scatter = handle
    n = len(ins)
    after = after if isinstance(after, tuple) else (after,)

    def body(*refs):
        i_ref, l_ref = refs[:n], refs[n:2 * n]
        s_sem, r_sem = refs[2 * n:2 * n + 2]
        for cp in _split_copies(i_ref, l_ref, scatter, s_sem, r_sem, arriving=False):
            cp.wait_send()
        for cp in _split_copies(i_ref, l_ref, scatter, s_sem, r_sem, arriving=True):
            cp.wait_recv()

    hbm = lambda a: pltpu.HBM(a.shape, a.dtype)
    out = pl.pallas_call(
        body, name=name,
        out_shape=tuple(hbm(a) for a in ins + lands),
        in_specs=[_HBM] * (2 * n) + [_SEM, _SEM] + [pl.BlockSpec(memory_space=pl.ANY)] * len(after),
        out_specs=tuple([_HBM] * (2 * n)),
        input_output_aliases={i: i for i in range(2 * n)},
        compiler_params=pltpu.CompilerParams(has_side_effects=_EFFECT),
    )(*ins, *lands, send_sem, recv_sem, *after)
    return list(out[n:])


_SIBLING = 1
_CHIPS = (4, 2, 6)


def _peer_of(k):
    x, y, c = lax.axis_index("x"), lax.axis_index("y"), lax.axis_index("c")
    peer = (1 - x if (k >> 2) & 1 else x, 1 - y if (k >> 1) & 1 else y, 1 - c if k & 1 else c)
    return peer, 4 * peer[0] + 2 * peer[1] + peer[2]


def _rcopy(src, dst, send_sem, recv_sem, k):
    return pltpu.make_async_remote_copy(src_ref=src, dst_ref=dst, send_sem=send_sem, recv_sem=recv_sem,
                                        device_id=_peer_of(k)[0], device_id_type=pl.DeviceIdType.MESH)


def _gather2_start(groups, lands, after, name):
    sizes = [len(g) for g in groups]
    arrays = [a for g in groups for a in g]
    lands = [l for g in lands for l in g]
    n, ng = len(arrays), len(groups)
    starts = np.cumsum([0] + sizes)

    def body(*refs):
        ins, lnd = refs[:n], refs[n:2 * n]
        sems = refs[2 * n + 1:2 * n + 1 + 4 * ng]
        me, _ = _peers()
        for g in range(ng):
            send_d, recv_d, send_i, recv_i = sems[4 * g:4 * g + 4]
            for a in range(starts[g], starts[g + 1]):
                for k in _CHIPS:
                    _rcopy(ins[a], lnd[a].at[me], send_i, recv_i, k).start()
                _rcopy(ins[a], lnd[a].at[me], send_d, recv_d, _SIBLING).start()
        refs[-1][...] = jnp.zeros_like(refs[-1])

    hbm = lambda a: pltpu.HBM(a.shape, a.dtype)
    out = pl.pallas_call(
        body, name=name,
        out_shape=(*[pltpu.SemaphoreType.DMA(())] * (4 * ng), *[hbm(a) for a in arrays], *[hbm(a) for a in lands],
                   jax.ShapeDtypeStruct((8, LANES), F32)),
        in_specs=[_HBM] * (2 * n) + [pl.BlockSpec(memory_space=pl.ANY)],
        out_specs=(*[_SEM] * (4 * ng), *([_HBM] * (2 * n)), pl.BlockSpec(memory_space=pltpu.VMEM)),
        input_output_aliases={i: 4 * ng + i for i in range(2 * n)},
        compiler_params=pltpu.CompilerParams(has_side_effects=_EFFECT),
    )(*[pltpu.with_memory_space_constraint(a, pltpu.HBM) for a in arrays + lands], after)
    ins_out, lands_out = out[4 * ng:4 * ng + n], out[4 * ng + n:4 * ng + 2 * n]
    handles = [dict(sems=out[4 * g:4 * g + 4], ins=list(ins_out[starts[g]:starts[g + 1]]),
                    lands=list(lands_out[starts[g]:starts[g + 1]])) for g in range(ng)]
    return handles, out[-1]


def _gather2_pass_on(handle, after, name):
    lands, recv_i = handle["lands"], handle["sems"][3]
    n = len(lands)
    after = after if isinstance(after, tuple) else (after,)

    def body(*refs):
        lnd, r_i = refs[:n], refs[n]
        send_f, recv_f = refs[n + 1 + len(after):n + 3 + len(after)]
        for a in range(n):
            for k in _CHIPS:
                row = _peer_of(k)[1]
                _rcopy(lnd[a].at[row], lnd[a].at[row], send_f, r_i, k).wait_recv()
        for a in range(n):
            for k in _CHIPS:
                row = _peer_of(k)[1]
                _rcopy(lnd[a].at[row], lnd[a].at[row], send_f, recv_f, _SIBLING).start()
        refs[-1][...] = jnp.zeros_like(refs[-1])

    hbm = lambda a: pltpu.HBM(a.shape, a.dtype)
    out = pl.pallas_call(
        body, name=name,
        out_shape=(pltpu.SemaphoreType.DMA(()), pltpu.SemaphoreType.DMA(()), *[hbm(a) for a in lands],
                   jax.ShapeDtypeStruct((8, LANES), F32)),
        in_specs=[_HBM] * n + [_SEM] + [pl.BlockSpec(memory_space=pl.ANY)] * len(after),
        out_specs=(_SEM, _SEM, *([_HBM] * n), pl.BlockSpec(memory_space=pltpu.VMEM)),
        input_output_aliases={i: 2 + i for i in range(n)},
        compiler_params=pltpu.CompilerParams(has_side_effects=_EFFECT),
    )(*lands, recv_i, *after)
    return dict(handle, lands=list(out[2:2 + n]), passed=(out[0], out[1])), out[-1]


def _gather2_wait(handle, after, name):
    ins, lands = handle["ins"], handle["lands"]
    send_d, recv_d, send_i, _ = handle["sems"]
    send_f, recv_f = handle["passed"]
    n = len(ins)
    after = after if isinstance(after, tuple) else (after,)

    def body(*refs):
        i_ref, lnd = refs[:n], refs[n:2 * n]
        s_d, r_d, s_i, s_f, r_f = refs[2 * n:2 * n + 5]
        me, _ = _peers()
        sib = _peer_of(_SIBLING)[1]
        for a in range(n):
            _rcopy(i_ref[a], lnd[a].at[sib], s_d, r_d, _SIBLING).wait_send()
            _rcopy(i_ref[a], lnd[a].at[sib], s_d, r_d, _SIBLING).wait_recv()
            for k in _CHIPS:
                row = _peer_of(k)[1]
                _rcopy(i_ref[a], lnd[a].at[me], s_i, r_d, k).wait_send()
                _rcopy(lnd[a].at[row], lnd[a].at[row], s_f, r_f, _SIBLING).wait_send()
                _rcopy(lnd[a].at[row], lnd[a].at[_peer_of(k ^ _SIBLING)[1]], s_f, r_f, _SIBLING).wait_recv()

    hbm = lambda a: pltpu.HBM(a.shape, a.dtype)
    out = pl.pallas_call(
        body, name=name,
        out_shape=tuple(hbm(a) for a in ins + lands),
        in_specs=[_HBM] * (2 * n) + [_SEM] * 5 + [pl.BlockSpec(memory_space=pl.ANY)] * len(after),
        out_specs=tuple([_HBM] * (2 * n)),
        input_output_aliases={i: i for i in range(2 * n)},
        compiler_params=pltpu.CompilerParams(has_side_effects=_EFFECT),
    )(*ins, *lands, send_d, recv_d, send_i, send_f, recv_f, *after)
    return list(out[n:])


def _adamw(w, parts, m, v, name="adamw"):
    R, C = w.shape
    L = len(parts)
    rl = R // L
    tr = max([t for t in range(16, 257, 16) if rl % t == 0], default=rl)
    nb = rl // tr
    c1 = 1.0 - ADAM_B1 ** ADAM_STEP
    c2 = 1.0 - ADAM_B2 ** ADAM_STEP

    def body(*refs):
        w_ref, p_refs, (m_ref, v_ref, g_ref, d_ref, nm_ref, nv_ref) = refs[0], refs[1:1 + L], refs[1 + L:]
        g = None
        for j, p_ref in enumerate(p_refs):
            gj = p_ref[0].astype(F32)
            for i in range(1, N_DEV):
                gj = gj + p_ref[i].astype(F32)
            g = gj if g is None else jnp.where(pl.program_id(0) == j, gj, g)
        nm = ADAM_B1 * m_ref[...] + (1.0 - ADAM_B1) * g
        nv = ADAM_B2 * v_ref[...] + (1.0 - ADAM_B2) * (g * g)
        g_ref[...] = g
        nm_ref[...] = nm
        nv_ref[...] = nv
        d_ref[...] = -ADAM_LR * ((nm / c1) / (jnp.sqrt(nv / c2) + ADAM_EPS) + ADAM_WD * w_ref[...])

    blk = pl.BlockSpec((tr, C), lambda l, i: (l * nb + i, 0))
    part = lambda j: pl.BlockSpec((N_DEV, tr, C), lambda l, i: (0, jnp.where(l == j, i, 0), 0))
    return pl.pallas_call(
        body, name=name, grid=(L, nb),
        in_specs=[blk] + [part(j) for j in range(L)] + [blk, blk],
        out_specs=[blk] * 4,
        out_shape=[jax.ShapeDtypeStruct((R, C), F32)] * 4,
        compiler_params=_params(("arbitrary", "arbitrary")),
    )(w, *parts, m, v)


_O1 = Q_LORA
_O2 = _O1 + KV_LORA
_O3 = _O2 + MLA_ROPE
_NB = SB_HEADS * SB_DIM
IN_W = _O2 + LANES + 3 * _NB
COL_KR = _O2 // LANES
COL_SB = COL_KR + 1


def _w_in_local(w):
    kr = w[_O2:_O3]
    pad = jnp.zeros((LANES - 2 * MLA_ROPE, w.shape[1]), w.dtype)
    return jnp.concatenate([w[:_O2], kr, kr, pad, w[_O3:]], axis=0)


def _w_in_grad(g):
    kr = (g[_O2:_O2 + MLA_ROPE].astype(F32) + g[_O2 + MLA_ROPE:_O2 + 2 * MLA_ROPE].astype(F32)).astype(g.dtype)
    return jnp.concatenate([g[:_O2], kr, g[_O2 + LANES:]], axis=0)


def _w_uq_local(w):
    w3 = w.reshape(MLA_HEADS // 2, 2, MLA_NOPE + MLA_ROPE, w.shape[1])
    nope = w3[:, :, :MLA_NOPE].reshape(MLA_HEADS // 2, 2 * MLA_NOPE, w.shape[1])
    rope = w3[:, :, MLA_NOPE:].reshape(MLA_HEADS // 2, 2 * MLA_ROPE, w.shape[1])
    pad = jnp.zeros((MLA_HEADS // 2, LANES - 2 * MLA_ROPE, w.shape[1]), w.dtype)
    return jnp.concatenate([nope, rope, pad], axis=1).reshape(-1, w.shape[1])


def _w_uq_grad(g):
    g3 = g.reshape(MLA_HEADS // 2, 2 * LANES, g.shape[1])
    nope = g3[:, :2 * MLA_NOPE].reshape(MLA_HEADS // 2, 2, MLA_NOPE, g.shape[1])
    rope = g3[:, LANES:LANES + 2 * MLA_ROPE].reshape(MLA_HEADS // 2, 2, MLA_ROPE, g.shape[1])
    return jnp.concatenate([nope, rope], axis=2).reshape(-1, g.shape[1])


def _w_ukv_local(w):
    w3 = w.reshape(MLA_HEADS, MLA_NOPE + MLA_V, w.shape[1])
    return jnp.concatenate([w3[:, :MLA_NOPE].reshape(-1, w.shape[1]),
                            w3[:, MLA_NOPE:].reshape(-1, w.shape[1])], axis=0)


def _w_ukv_grad(g):
    half = MLA_HEADS * MLA_NOPE
    kn = g[:half].reshape(MLA_HEADS, MLA_NOPE, g.shape[1])
    vv = g[half:].reshape(MLA_HEADS, MLA_V, g.shape[1])
    return jnp.concatenate([kn, vv], axis=1).reshape(-1, g.shape[1])


def _rope_tables(T):
    pos = jnp.arange(T, dtype=F32)
    inv_freq = ROPE_THETA ** (-jnp.arange(0, MLA_ROPE, 2, dtype=F32) / MLA_ROPE)
    ang = pos[:, None] * inv_freq[None, :]
    cos, sin = jnp.cos(ang), jnp.sin(ang)
    ones = jnp.ones((T, LANES - 2 * MLA_ROPE), F32)
    cos_k = jnp.concatenate([cos, cos, cos, cos, ones], axis=1)
    sin_k = jnp.concatenate([-sin, sin, -sin, sin, 0.0 * ones], axis=1)
    cos_q = jnp.concatenate([jnp.ones((T, LANES), F32), cos_k], axis=1)
    sin_q = jnp.concatenate([jnp.zeros((T, LANES), F32), sin_k], axis=1)
    return cos_q, sin_q, cos_k, sin_k


def _bias_diag_index():
    ell = np.arange(TOEP_W)
    return np.clip(BAND_W - ell, -REL_CLIP, REL_CLIP) + REL_CLIP


def _local_step(x, target, small, get_weights, put_grads):
    T = x.shape[0]
    cos_q, sin_q, cos_k, sin_k = _rope_tables(T)
    G = {}
    W = dict(small)

    u0 = _rms_fwd(x, W["g_mix"][0:1], name="rms_mix0")
    bias_w = _toeplitz(W["od_rel_bias"][:, _bias_diag_index()])
    W.update(get_weights("in0", (u0, bias_w)))
    proj = _mm(u0, W["w_in_t"], dims="nt", name="proj_in")
    W.update(get_weights("mix0", proj))
    c_q, c_kv = proj[:, :_O1], proj[:, _O1:_O2]
    nq = _rms_fwd(c_q, W["g_cq"], name="rms_cq")
    nkv = _rms_fwd(c_kv, W["g_ckv"], name="rms_ckv")
    qa_raw = _mm(nq, W["w_uq_t"], dims="nt", name="proj_uq")
    kv = _mm(nkv, W["w_ukv_t"], dims="nt", out_dtype=BF16, name="proj_ukv")
    kr = _rope(proj, cos_k, sin_k, COL_KR, 1, BF16, name="rope_k")
    o_a, lse = _mla_fwd(qa_raw, cos_q, sin_q, kv, kr)
    o_b, o_b32, w_b, sp_b = _sb_fwd(proj, COL_SB)
    o_ab = jnp.concatenate([o_a, o_b], axis=1)
    h1 = _mm(o_ab, W["ev_w_out"], res=x, name="out_ev")

    def ffn_fwd(h, layer):
        W.update(get_weights(f"ffn{layer}", h))
        return _ffn_fwd(h, W["g_ffn"][layer:layer + 1], W[f"w_gate_t{layer}"], W[f"w_up_t{layer}"],
                        W[f"w_down{layer}"], name=f"ffn_fwd{layer}")

    h2, u1, a0, b0 = ffn_fwd(h1, 0)

    W.update(get_weights("mix1", h2))
    u2 = _rms_fwd(h2, W["g_mix"][1:2], name="rms_mix1")
    qkv = _mm(u2, W["od_w_qkv_t"], dims="nt", out_dtype=BF16, name="proj_qkv")
    nc = C_HEADS * C_DIM
    pad = ((PAD_KEYS, 0), (0, 0))
    k_pad, v_pad = jnp.pad(qkv[:, nc:2 * nc], pad), jnp.pad(qkv[:, 2 * nc:], pad)
    o_c, p_c = _band_fwd(qkv, k_pad, v_pad, bias_w)
    h3 = _mm(o_c, W["od_w_out"], res=h2, name="out_od")
    h4, u3, a1, b1 = ffn_fwd(h3, 1)

    loss, dh, dhb, G["g_final"] = _loss_head(h4, W["g_final"], target)

    def ffn_bwd(dh, dhb, h, u, a, b, layer):
        du, g_gate, g_up, g_down = _ffn_bwd(dhb, u, a, b, W[f"w_gate_t{layer}"], W[f"w_up_t{layer}"],
                                            W[f"w_down{layer}"], name=f"ffn_bwd{layer}")
        tok = put_grads(f"ffn{layer}", {"w_gate_t": g_gate, "w_up_t": g_up, "w_down": g_down})
        return _rms_bwd(h, W["g_ffn"][layer:layer + 1] + tok[:1, :1], du, dres=dh, name=f"rms_ffn_bwd{layer}")

    dh3, dh3b, g_gffn1 = ffn_bwd(dh, dhb, h3, u3, a1, b1, 1)

    do_c = _mm(dh3b, W["od_w_out"], dims="nt", name="out_od_dx")
    g_od_out = _mm(o_c, dh3b, dims="tn", out_dtype=BF16, name="out_od_dw")
    dq_c, dk_p, dv_p, dbias_w = _band_bwd(qkv, k_pad, v_pad, p_c, do_c)
    dqkv = jnp.concatenate([dq_c, dk_p[PAD_KEYS:], dv_p[PAD_KEYS:]], axis=1)
    tok = put_grads("mix1", {"od_w_qkv_t": _mm(dqkv, u2, dims="tn", out_dtype=BF16, name="proj_qkv_dw"),
                             "od_w_out": g_od_out})
    ddiag = _toeplitz_bwd(dbias_w)
    n_far = BAND_W - REL_CLIP + 1
    G["od_rel_bias"] = jnp.concatenate(
        [jnp.zeros((C_HEADS, REL_CLIP - BAND_TQ + 1), F32), ddiag[:, n_far:][:, ::-1],
         jnp.sum(ddiag[:, :n_far], axis=1, keepdims=True)], axis=1)
    dh2, dh2b, g_gmix1 = _mm_rms_bwd(dqkv, W["od_w_qkv_t"], h2, W["g_mix"][1:2] + tok[:1, :1], dh3,
                                     name="proj_qkv_dx")

    dh1, dh1b, g_gffn0 = ffn_bwd(dh2, dh2b, h1, u1, a0, b0, 0)
    G["g_ffn"] = jnp.concatenate([g_gffn0, g_gffn1], axis=0)

    do_ab = _mm(dh1b, W["ev_w_out"], dims="nt", name="out_ev_dx")
    g0 = {"ev_w_out": _mm(o_ab, dh1b, dims="tn", out_dtype=BF16, name="out_ev_dw")}
    dqa_raw, dkn, dva, dkr = _mla_bwd(qa_raw, cos_q, sin_q, kv, kr, o_a, lse, do_ab, 0)
    dlat, g0["w_uq_t"], g0["w_ukv_t"], G["g_cq"], G["g_ckv"] = _latent_bwd(
        proj, nq, nkv, dqa_raw, dkn, dva, dkr, cos_k, sin_k, W["g_cq"], W["g_ckv"], W["w_uq_t"], W["w_ukv_t"])
    tok = put_grads("mix0", g0)
    dqb, dkb, dvb = _sb_bwd(proj, COL_SB, o_b32, w_b, sp_b, do_ab, MLA_HEADS // 2, tok)
    dproj = jnp.concatenate([dlat, dqb, dkb, dvb], axis=1)
    tok = put_grads("in0", {"w_in_t": _mm(dproj, u0, dims="tn", name="proj_in_dw")})
    dx, _, g_gmix0 = _mm_rms_bwd(dproj, W["w_in_t"], x, W["g_mix"][0:1] + tok[:1, :1], dh1, name="proj_in_dx")
    G["g_mix"] = jnp.concatenate([g_gmix0, g_gmix1], axis=0)
    return loss[0, 0], dx, G


_BIG = ["ev_w_in", "ev_w_uq", "ev_w_ukv", "ev_w_out", "od_w_qkv", "od_w_out", "w_gate", "w_up", "w_down"]
_COL_SHARDED = {"ev_w_in", "ev_w_uq", "ev_w_ukv", "od_w_qkv", "w_gate", "w_up"}
_SMALL = ["ev_g_cq", "ev_g_ckv", "od_rel_bias", "g_mix", "g_ffn", "g_final"]
_GROUPS = {
    "in0": ["ev_w_in"],
    "mix0": ["ev_w_uq", "ev_w_ukv", "ev_w_out"],
    "ffn0": ["w_gate0", "w_up0", "w_down0"],
    "mix1": ["od_w_qkv", "od_w_out"],
    "ffn1": ["w_gate1", "w_up1", "w_down1"],
}
_GROUP_SRC = {n + str(l): (n, l) for n in ("w_gate", "w_up", "w_down") for l in (0, 1)}
_BATCHES = {"in0": ["in0"], "layer0": ["mix0", "ffn0"], "layer1": ["mix1", "ffn1"]}
_BATCH_OF = {grp: batch for batch, grps in _BATCHES.items() for grp in grps}
_SMALL_ROWS = 8
_SMALL_COLS = 1792


def _pack_small(vals):
    flat = jnp.concatenate([v.reshape(-1).astype(F32) for v in vals])
    flat = jnp.pad(flat, (0, _SMALL_ROWS * _SMALL_COLS - flat.shape[0]))
    return flat.reshape(_SMALL_ROWS, _SMALL_COLS)


def _unpack_small(packed, like):
    flat = packed.reshape(-1)
    out, off = [], 0
    for v in like:
        out.append(flat[off:off + v.size].reshape(v.shape))
        off += v.size
    return out


def kernel(x, ev_w_in, ev_g_cq, ev_w_uq, ev_g_ckv, ev_w_ukv, ev_w_out, od_w_qkv, od_rel_bias, od_w_out, g_mix, g_ffn, w_gate, w_up, w_down, g_final, loss_target, m_ev_w_in, m_ev_g_cq, m_ev_w_uq, m_ev_g_ckv, m_ev_w_ukv, m_ev_w_out, m_od_w_qkv, m_od_rel_bias, m_od_w_out, m_g_mix, m_g_ffn, m_w_gate, m_w_up, m_w_down, m_g_final, v_ev_w_in, v_ev_g_cq, v_ev_w_uq, v_ev_g_ckv, v_ev_w_ukv, v_ev_w_out, v_od_w_qkv, v_od_rel_bias, v_od_w_out, v_g_mix, v_g_ffn, v_w_gate, v_w_up, v_w_down, v_g_final):
    args = dict(locals())
    w = {n: args[n] for n in _BIG + _SMALL}
    mom = {n: args["m_" + n] for n in _BIG + _SMALL}
    var = {n: args["v_" + n] for n in _BIG + _SMALL}

    own = {}
    for grp, names in _GROUPS.items():
        for n in names:
            base, layer = _GROUP_SRC.get(n, (n, 0))
            shard = w[base][layer:layer + 1]
            own[n] = (jnp.swapaxes(shard, 1, 2) if base in _COL_SHARDED else shard).astype(BF16)
    placed = dict(zip(own, _place_own(list(own.values()), [False] * len(own), name="gather_own")))
    handles, token = _gather2_start(
        [[own[n] for n in names] for names in _GROUPS.values()],
        [[placed[n] for n in names] for names in _GROUPS.values()], x[0, :8, :LANES], name="gather_start")
    gather = dict(zip(_GROUPS, handles))
    pass_before = {"in0": ["in0"], "mix0": ["mix0"]}
    pass_after = {"mix0": ("ffn0", "g_cq"), "ffn0": ("mix1", "g_ffn"), "mix1": ("ffn1", "g_mix")}

    def get_weights(grp, after):
        names = _GROUPS[grp]
        after = token if after is None else after
        for g in pass_before.get(grp, []):
            gather[g], _ = _gather2_pass_on(gather[g], after, name="gather_pass_" + g)
        lands = _gather2_wait(gather[grp], after, name="gather_wait_" + grp)
        full = {n: l.reshape(-1, l.shape[-1]) for n, l in zip(names, lands)}
        out = {}
        if grp in pass_after:
            g, gain = pass_after[grp]
            gather[g], tok = _gather2_pass_on(gather[g], lands[0], name="gather_pass_" + g)
            out[gain] = small[gain] + tok[:1, :1]
        if grp == "in0":
            out.update({"w_in_t": _w_in_local(full["ev_w_in"])})
        elif grp == "mix0":
            out.update({"w_uq_t": _w_uq_local(full["ev_w_uq"]), "w_ukv_t": _w_ukv_local(full["ev_w_ukv"]),
                        "ev_w_out": full["ev_w_out"]})
        elif grp == "mix1":
            out.update({"od_w_qkv_t": full["od_w_qkv"], "od_w_out": full["od_w_out"]})
        else:
            layer = grp[-1]
            out.update({"w_gate_t" + layer: full["w_gate" + layer], "w_up_t" + layer: full["w_up" + layer],
                        "w_down" + layer: full["w_down" + layer]})
        return out

    scatter, pending = {}, {}

    def put_grads(grp, g):
        if grp == "in0":
            g = {"ev_w_in": _w_in_grad(g["w_in_t"])}
        elif grp == "mix0":
            g = {"ev_w_uq": _w_uq_grad(g["w_uq_t"]), "ev_w_ukv": _w_ukv_grad(g["w_ukv_t"]),
                 "ev_w_out": g["ev_w_out"]}
        elif grp == "mix1":
            g = {"od_w_qkv": g["od_w_qkv_t"], "od_w_out": g["od_w_out"]}
        else:
            layer = grp[-1]
            g = {"w_gate" + layer: g["w_gate_t"], "w_up" + layer: g["w_up_t"], "w_down" + layer: g["w_down"]}
        pending.update({n: v.reshape(N_DEV, 1, v.shape[0] // N_DEV, v.shape[1]).astype(BF16) for n, v in g.items()})
        batch = _BATCH_OF[grp]
        names = [n for gr in _BATCHES[batch] for n in _GROUPS[gr]]
        if not all(n in pending for n in names):
            return jnp.zeros((8, LANES), F32)
        send = [pending[n] for n in names]
        scatter[batch], tok = _exchange_start(send, [True] * len(names), send[0], name="scatter_start_" + batch)
        return tok

    small = {"g_cq": ev_g_cq, "g_ckv": ev_g_ckv, "od_rel_bias": od_rel_bias[0],
             "g_mix": g_mix + token[0, 0], "g_ffn": g_ffn, "g_final": g_final.reshape(1, -1)}
    loss_part, dx, G = _local_step(x[0], loss_target[0], small, get_weights, put_grads)
    g_small = _pack_small([G["g_cq"], G["g_ckv"], G["od_rel_bias"], G["g_mix"], G["g_ffn"], G["g_final"],
                           loss_part.reshape(1)])
    small_handle, _ = _exchange_start([g_small], [False], dx, name="gather_start_small")

    grads, deltas, new_m, new_v = {}, {}, {}, {}
    parts, after = {}, dx

    def wait_parts(batch, after):
        lands = _exchange_wait(scatter[batch], after, name="scatter_wait_" + batch)
        parts.update(zip([n for grp in _BATCHES[batch] for n in _GROUPS[grp]], lands))
        return lands[0]

    def adamw(n):
        col = n in _COL_SHARDED
        rows = lambda a: (jnp.swapaxes(a, 1, 2) if col else a).reshape(-1, a.shape[1 if col else 2])
        layers = [parts[n]] if n in parts else [parts[n + "0"], parts[n + "1"]]
        res = _adamw(rows(w[n]), [p.reshape(N_DEV, -1, p.shape[-1]) for p in layers], rows(mom[n]), rows(var[n]),
                     name="adamw_" + n)
        L, a1, a2 = w[n].shape
        back = lambda r: jnp.swapaxes(r.reshape(L, a2, a1), 1, 2) if col else r.reshape(L, a1, a2)
        grads[n], deltas[n], new_m[n], new_v[n] = [back(r) for r in res]
        return res[0]

    for batch in ("layer1", "layer0"):
        after = wait_parts(batch, after)
    after = wait_parts("in0", tuple(adamw(n) for n in _BIG[1:]))
    after = adamw("ev_w_in")
    small_w = [w[n] for n in _SMALL]
    small_parts = _exchange_wait(small_handle, after, name="gather_wait_small")[0]
    loss = jnp.sum(small_parts.reshape(N_DEV, -1)[:, sum(v.size for v in small_w)])
    res = _adamw(_pack_small(small_w), [small_parts], _pack_small([mom[n] for n in _SMALL]),
                 _pack_small([var[n] for n in _SMALL]), name="adamw_small")
    for d, packed in zip((grads, deltas, new_m, new_v), res):
        for n, val in zip(_SMALL, _unpack_small(packed, small_w)):
            d[n] = val

    order = ["ev_w_in", "ev_g_cq", "ev_w_uq", "ev_g_ckv", "ev_w_ukv", "ev_w_out", "od_w_qkv", "od_rel_bias",
             "od_w_out", "g_mix", "g_ffn", "w_gate", "w_up", "w_down", "g_final"]
    out = [loss, dx[None]]
    for d in (grads, deltas, new_m, new_v):
        out += [d[n] for n in order]
    return tuple(out)
```

```python
import functools

import numpy as np
import jax
import jax.numpy as jnp
from jax import lax
from jax.experimental import pallas as pl
from jax.experimental.pallas import tpu as pltpu

F32 = jnp.float32
BF16 = jnp.bfloat16

D_MODEL = 1024
CHUNK = 64
MLA_HEADS = 8
MLA_NOPE = 64
MLA_ROPE = 32
MLA_V = 64
Q_LORA = 384
KV_LORA = 256
ROPE_THETA = 10000.0
SB_HEADS = 8
SB_DIM = 64
C_HEADS = 16
C_DIM = 64
LEFT_CHUNKS = 8
REL_CLIP = 256
D_FF = 2816
RMS_EPS = 1e-6
ADAM_LR = 0.001
ADAM_B1 = 0.9
ADAM_B2 = 0.999
ADAM_EPS = 1e-08
ADAM_WD = 0.01
ADAM_STEP = 10

N_DEV = 8
LANES = 128
HEAD = 64
assert HEAD == MLA_NOPE == MLA_V == SB_DIM == C_DIM and 2 * HEAD == LANES
CHUNK_BITS = CHUNK.bit_length() - 1
assert 1 << CHUNK_BITS == CHUNK
VMEM_LIMIT = 56 * 1024 * 1024
NEG = -1e30
PAD_KEYS = LEFT_CHUNKS * CHUNK
BAND_TQ = 128
BAND_W = BAND_TQ + PAD_KEYS
TOEP_W = BAND_W + BAND_TQ

NN = (((1,), (0,)), ((), ()))
NT = (((1,), (1,)), ((), ()))
TN = (((0,), (0,)), ((), ()))


def _dot(a, b, dn):
    return lax.dot_general(a, b, dn, preferred_element_type=F32)


def _pick(dim, pref):
    if dim <= pref:
        return dim
    best = None
    for t in range(LANES, pref + 1, LANES):
        if dim % t == 0:
            best = t
    assert best is not None, (dim, pref)
    return best


def _params(sem):
    return pltpu.CompilerParams(dimension_semantics=sem, vmem_limit_bytes=VMEM_LIMIT)


def _mm(a, b, dims="nn", res=None, out_dtype=F32, name="mm"):
    if dims == "nn":
        (M, K), (K2, N) = a.shape, b.shape
    elif dims == "nt":
        (M, K), (N, K2) = a.shape, b.shape
    else:
        (K, M), (K2, N) = a.shape, b.shape
    assert K == K2, (a.shape, b.shape, dims)
    tm, tn, tk = _pick(M, 1024), _pick(N, 1152), _pick(K, 1024)
    nk = K // tk
    dn = {"nn": NN, "nt": NT, "tn": TN}[dims]
    has_res = res is not None

    def body(*refs):
        if has_res:
            a_ref, b_ref, r_ref, o_ref, acc = refs
        else:
            a_ref, b_ref, o_ref, acc = refs
        k = pl.program_id(2)

        @pl.when(k == 0)
        def _():
            acc[...] = jnp.zeros_like(acc)

        acc[...] += _dot(a_ref[...].astype(BF16), b_ref[...].astype(BF16), dn)

        @pl.when(k == nk - 1)
        def _():
            r = acc[...]
            if has_res:
                r = r + r_ref[...]
            o_ref[...] = r.astype(out_dtype)

    a_spec = (pl.BlockSpec((tk, tm), lambda i, j, k: (k, i)) if dims == "tn"
              else pl.BlockSpec((tm, tk), lambda i, j, k: (i, k)))
    b_spec = (pl.BlockSpec((tn, tk), lambda i, j, k: (j, k)) if dims == "nt"
              else pl.BlockSpec((tk, tn), lambda i, j, k: (k, j)))
    o_spec = pl.BlockSpec((tm, tn), lambda i, j, k: (i, j))
    in_specs = [a_spec, b_spec] + ([o_spec] if has_res else [])
    args = (a, b) + ((res,) if has_res else ())
    return pl.pallas_call(
        body, name=name, grid=(M // tm, N // tn, nk),
        in_specs=in_specs, out_specs=o_spec,
        out_shape=jax.ShapeDtypeStruct((M, N), out_dtype),
        scratch_shapes=[pltpu.VMEM((tm, tn), F32)],
        compiler_params=_params(("parallel", "parallel", "arbitrary")),
    )(*args)


def _rms_fwd(x, g, out_dtype=BF16, name="rms_fwd"):
    T, Fd = x.shape
    tm = _pick(T, 256)

    def body(x_ref, g_ref, o_ref):
        xv = x_ref[...]
        r = lax.rsqrt(jnp.mean(xv * xv, axis=-1, keepdims=True) + RMS_EPS)
        o_ref[...] = (xv * r * g_ref[...]).astype(out_dtype)

    return pl.pallas_call(
        body, name=name, grid=(T // tm,),
        in_specs=[pl.BlockSpec((tm, Fd), lambda i: (i, 0)), pl.BlockSpec((1, Fd), lambda i: (0, 0))],
        out_specs=pl.BlockSpec((tm, Fd), lambda i: (i, 0)),
        out_shape=jax.ShapeDtypeStruct((T, Fd), out_dtype),
        compiler_params=_params(("parallel",)),
    )(x, g)


def _rms_bwd(x, g, dy, dres=None, name="rms_bwd"):
    T, Fd = x.shape
    tm = _pick(T, 256)
    has_res = dres is not None

    def body(*refs):
        if has_res:
            x_ref, g_ref, dy_ref, r_ref, dx_ref, dxb_ref, dg_ref = refs
        else:
            x_ref, g_ref, dy_ref, dx_ref, dxb_ref, dg_ref = refs
        xv, dyv = x_ref[...], dy_ref[...]
        r = lax.rsqrt(jnp.mean(xv * xv, axis=-1, keepdims=True) + RMS_EPS)
        gdy = dyv * g_ref[...]
        dot = jnp.mean(xv * gdy, axis=-1, keepdims=True)
        dx = r * gdy - xv * (r * r * r * dot)
        if has_res:
            dx = dx + r_ref[...]
        dx_ref[...] = dx
        dxb_ref[...] = dx.astype(BF16)

        @pl.when(pl.program_id(0) == 0)
        def _():
            dg_ref[...] = jnp.zeros_like(dg_ref)

        dg_ref[...] += jnp.sum(dyv * xv * r, axis=0, keepdims=True)

    row = pl.BlockSpec((tm, Fd), lambda i: (i, 0))
    vec = pl.BlockSpec((1, Fd), lambda i: (0, 0))
    in_specs = [row, vec, row] + ([row] if has_res else [])
    args = (x, g, dy) + ((dres,) if has_res else ())
    return pl.pallas_call(
        body, name=name, grid=(T // tm,),
        in_specs=in_specs, out_specs=[row, row, vec],
        out_shape=[jax.ShapeDtypeStruct((T, Fd), F32), jax.ShapeDtypeStruct((T, Fd), BF16),
                   jax.ShapeDtypeStruct((1, Fd), F32)],
        compiler_params=_params(("arbitrary",)),
    )(*args)


def _mm_rms_bwd(a, b, x, g, dres, name="mm_rms_bwd"):
    T, K = a.shape
    Fd = b.shape[1]
    tm, tk = _pick(T, 512), _pick(K, 1024)
    nk = K // tk

    def body(a_ref, b_ref, x_ref, g_ref, r_ref, dx_ref, dxb_ref, dg_ref, acc):
        i, k = pl.program_id(0), pl.program_id(1)

        @pl.when(k == 0)
        def _():
            acc[...] = jnp.zeros_like(acc)

        @pl.when((k == 0) & (i == 0))
        def _():
            dg_ref[...] = jnp.zeros_like(dg_ref)

        acc[...] += _dot(a_ref[...].astype(BF16), b_ref[...].astype(BF16), NN)

        @pl.when(k == nk - 1)
        def _():
            xv, dyv = x_ref[...], acc[...]
            r = lax.rsqrt(jnp.mean(xv * xv, axis=-1, keepdims=True) + RMS_EPS)
            gdy = dyv * g_ref[...]
            dot = jnp.mean(xv * gdy, axis=-1, keepdims=True)
            dx = r * gdy - xv * (r * r * r * dot) + r_ref[...]
            dx_ref[...] = dx
            dxb_ref[...] = dx.astype(BF16)
            dg_ref[...] += jnp.sum(dyv * xv * r, axis=0, keepdims=True)

    row = pl.BlockSpec((tm, Fd), lambda i, k: (i, 0))
    vec = pl.BlockSpec((1, Fd), lambda i, k: (0, 0))
    return pl.pallas_call(
        body, name=name, grid=(T // tm, nk),
        in_specs=[pl.BlockSpec((tm, tk), lambda i, k: (i, k)), pl.BlockSpec((tk, Fd), lambda i, k: (k, 0)),
                  row, vec, row],
        out_specs=[row, row, vec],
        out_shape=[jax.ShapeDtypeStruct((T, Fd), F32), jax.ShapeDtypeStruct((T, Fd), BF16),
                   jax.ShapeDtypeStruct((1, Fd), F32)],
        scratch_shapes=[pltpu.VMEM((tm, Fd), F32)],
        compiler_params=_params(("arbitrary", "arbitrary")),
    )(a, b, x, g, dres)


def _latent_bwd(proj, nq, nkv, dqa, dkn, dva, dkr, cos_k, sin_k, g_cq, g_ckv, w_uq_t, w_ukv_t, name="latent_bwd"):
    T = proj.shape[0]
    tm = _pick(T, 512)
    wl = _O2

    def rms_bwd(xv, gv, dyv):
        r = lax.rsqrt(jnp.mean(xv * xv, axis=-1, keepdims=True) + RMS_EPS)
        gdy = dyv * gv
        dot = jnp.mean(xv * gdy, axis=-1, keepdims=True)
        return r * gdy - xv * (r * r * r * dot), jnp.sum(dyv * xv * r, axis=0, keepdims=True)

    def body(p_ref, nq_ref, nkv_ref, dqa_ref, dkn_ref, dva_ref, dkr_ref, c_ref, s_ref, gq_ref, gkv_ref, wq_ref, wkv_ref,
             dlat_ref, dwq_ref, dwkv_ref, dgq_ref, dgkv_ref):
        @pl.when(pl.program_id(0) == 0)
        def _():
            for ref in (dwq_ref, dwkv_ref, dgq_ref, dgkv_ref):
                ref[...] = jnp.zeros_like(ref)

        dqv = dqa_ref[...]
        dkv = jnp.concatenate([dkn_ref[...], dva_ref[...]], axis=1)
        pv = p_ref[...]
        dc_q, dgq = rms_bwd(pv[:, :_O1], gq_ref[...], _dot(dqv, wq_ref[...], NN))
        dc_kv, dgkv = rms_bwd(pv[:, _O1:], gkv_ref[...], _dot(dkv, wkv_ref[...], NN))
        dkr_raw = _rotate(dkr_ref[...], c_ref[...], -s_ref[...])
        dlat_ref[...] = jnp.concatenate([dc_q, dc_kv, dkr_raw], axis=1).astype(BF16)
        dwq_ref[...] += _dot(dqv, nq_ref[...], TN)
        dwkv_ref[...] += _dot(dkv, nkv_ref[...], TN)
        dgq_ref[...] += dgq
        dgkv_ref[...] += dgkv

    row = lambda w: pl.BlockSpec((tm, w), lambda i: (i, 0))
    const = lambda a: pl.BlockSpec(a.shape, lambda i: (0, 0))
    outs = [jax.ShapeDtypeStruct((T, wl + LANES), BF16), jax.ShapeDtypeStruct(w_uq_t.shape, F32),
            jax.ShapeDtypeStruct(w_ukv_t.shape, F32), jax.ShapeDtypeStruct(g_cq.shape, F32),
            jax.ShapeDtypeStruct(g_ckv.shape, F32)]
    return pl.pallas_call(
        body, name=name, grid=(T // tm,),
        in_specs=[row(wl), row(_O1), row(_O2 - _O1), row(dqa.shape[1]), row(dkn.shape[1]), row(dva.shape[1]),
                  row(LANES), row(LANES), row(LANES), const(g_cq), const(g_ckv), const(w_uq_t), const(w_ukv_t)],
        out_specs=[row(wl + LANES)] + [const(o) for o in outs[1:]],
        out_shape=outs,
        compiler_params=_params(("arbitrary",)),
    )(proj, nq, nkv, dqa, dkn, dva, dkr, cos_k, sin_k, g_cq, g_ckv, w_uq_t, w_ukv_t)


def _loss_head(h, g, target, name="loss_head"):
    T, Fd = h.shape
    tm = _pick(T, 256)

    def body(h_ref, g_ref, t_ref, loss_ref, dh_ref, dhb_ref, dg_ref):
        xv = h_ref[...]
        r = lax.rsqrt(jnp.mean(xv * xv, axis=-1, keepdims=True) + RMS_EPS)
        diff = xv * r * g_ref[...] - t_ref[...]
        part = 0.5 * jnp.sum(jnp.mean(diff * diff, axis=-1, keepdims=True), axis=0, keepdims=True)
        dyv = diff * (1.0 / Fd)
        gdy = dyv * g_ref[...]
        dot = jnp.mean(xv * gdy, axis=-1, keepdims=True)
        dh = r * gdy - xv * (r * r * r * dot)
        dh_ref[...] = dh
        dhb_ref[...] = dh.astype(BF16)

        @pl.when(pl.program_id(0) == 0)
        def _():
            dg_ref[...] = jnp.zeros_like(dg_ref)
            loss_ref[...] = jnp.zeros_like(loss_ref)

        dg_ref[...] += jnp.sum(dyv * xv * r, axis=0, keepdims=True)
        loss_ref[...] += jnp.broadcast_to(part, loss_ref.shape)

    row = pl.BlockSpec((tm, Fd), lambda i: (i, 0))
    vec = pl.BlockSpec((1, Fd), lambda i: (0, 0))
    return pl.pallas_call(
        body, name=name, grid=(T // tm,),
        in_specs=[row, vec, row],
        out_specs=[pl.BlockSpec((1, LANES), lambda i: (0, 0)), row, row, vec],
        out_shape=[jax.ShapeDtypeStruct((1, LANES), F32), jax.ShapeDtypeStruct((T, Fd), F32),
                   jax.ShapeDtypeStruct((T, Fd), BF16), jax.ShapeDtypeStruct((1, Fd), F32)],
        compiler_params=_params(("arbitrary",)),
    )(h, g, target)


FFN_TF = 256


def _ffn_fwd(h, g, wg_t, wu_t, wd, name="ffn_fwd"):
    T, Dm = h.shape
    Fh = wd.shape[0]
    tm = _pick(T, 1024)
    nf = Fh // FFN_TF

    def body(h_ref, g_ref, wg_ref, wu_ref, wd_ref, o_ref, u_ref, a_ref, b_ref):
        j = pl.program_id(1)

        @pl.when(j == 0)
        def _():
            xv = h_ref[...]
            r = lax.rsqrt(jnp.mean(xv * xv, axis=-1, keepdims=True) + RMS_EPS)
            u_ref[...] = (xv * r * g_ref[...]).astype(BF16)
            o_ref[...] = xv

        u = u_ref[...]
        a = _dot(u, wg_ref[...], NT).astype(BF16)
        b = _dot(u, wu_ref[...], NT).astype(BF16)
        a_ref[...] = a
        b_ref[...] = b
        af = a.astype(F32)
        s = (af * jax.nn.sigmoid(af) * b.astype(F32)).astype(BF16)
        o_ref[...] += _dot(s, wd_ref[...], NN)

    row = pl.BlockSpec((tm, Dm), lambda i, j: (i, 0))
    wblk = pl.BlockSpec((FFN_TF, Dm), lambda i, j: (j, 0))
    ablk = pl.BlockSpec((tm, FFN_TF), lambda i, j: (i, j))
    return pl.pallas_call(
        body, name=name, grid=(T // tm, nf),
        in_specs=[row, pl.BlockSpec((1, Dm), lambda i, j: (0, 0)), wblk, wblk, wblk],
        out_specs=[row, row, ablk, ablk],
        out_shape=[jax.ShapeDtypeStruct((T, Dm), F32), jax.ShapeDtypeStruct((T, Dm), BF16),
                   jax.ShapeDtypeStruct((T, Fh), BF16), jax.ShapeDtypeStruct((T, Fh), BF16)],
        compiler_params=_params(("parallel", "arbitrary")),
    )(h, g, wg_t, wu_t, wd)


def _ffn_bwd(dh, u, a, b, wg_t, wu_t, wd, name="ffn_bwd"):
    T, Dm = dh.shape
    Fh = wd.shape[0]
    nf = Fh // FFN_TF
    once = pl.Buffered(1)

    def body(dh_ref, u_ref, a_ref, b_ref, wg_ref, wu_ref, wd_ref, du_ref, dwg_ref, dwu_ref, dwd_ref):
        j = pl.program_id(0)

        @pl.when(j == 0)
        def _():
            du_ref[...] = jnp.zeros_like(du_ref)

        ds = _dot(dh_ref[...], wd_ref[...], NT)
        af, bf = a_ref[...].astype(F32), b_ref[...].astype(F32)
        sig = jax.nn.sigmoid(af)
        sa = af * sig
        dwd_ref[...] = _dot((sa * bf).astype(BF16), dh_ref[...], TN).astype(BF16)
        dab = jnp.concatenate([(ds * bf * (sig * (1.0 + af * (1.0 - sig)))).astype(BF16),
                               (ds * sa).astype(BF16)], axis=1)
        dw = _dot(dab, u_ref[...], TN)
        dwg_ref[...] = dw[:FFN_TF].astype(BF16)
        dwu_ref[...] = dw[FFN_TF:].astype(BF16)
        du_ref[...] += _dot(dab, jnp.concatenate([wg_ref[...], wu_ref[...]], axis=0), NN)

    full = lambda: pl.BlockSpec((T, Dm), lambda j: (0, 0), pipeline_mode=once)
    wblk = pl.BlockSpec((FFN_TF, Dm), lambda j: (j, 0))
    ablk = pl.BlockSpec((T, FFN_TF), lambda j: (0, j))
    return pl.pallas_call(
        body, name=name, grid=(nf,),
        in_specs=[full(), full(), ablk, ablk, wblk, wblk, wblk],
        out_specs=[pl.BlockSpec((T, Dm), lambda j: (0, 0)), wblk, wblk, wblk],
        out_shape=[jax.ShapeDtypeStruct((T, Dm), F32)] + [jax.ShapeDtypeStruct((Fh, Dm), BF16)] * 3,
        compiler_params=_params(("arbitrary",)),
    )(dh, u, a, b, wg_t, wu_t, wd)


def _rope(x, cos_t, sin_t, col0, ncols, out_dtype, name="rope"):
    T = x.shape[0]
    wt = cos_t.shape[1]
    tm = _pick(T, 256)
    nb = ncols * LANES // wt
    half = MLA_ROPE // 2

    def body(x_ref, c_ref, s_ref, o_ref):
        xv = x_ref[...].astype(F32)
        lane = lax.broadcasted_iota(jnp.int32, xv.shape, 1)
        first = (lane & (MLA_ROPE - 1)) < half
        swapped = jnp.where(first, pltpu.roll(xv, wt - half, 1), pltpu.roll(xv, half, 1))
        o_ref[...] = (xv * c_ref[...] + swapped * s_ref[...]).astype(out_dtype)

    off = col0 * LANES // wt
    return pl.pallas_call(
        body, name=name, grid=(T // tm, nb),
        in_specs=[pl.BlockSpec((tm, wt), lambda i, j: (i, j + off)),
                  pl.BlockSpec((tm, wt), lambda i, j: (i, 0)),
                  pl.BlockSpec((tm, wt), lambda i, j: (i, 0))],
        out_specs=pl.BlockSpec((tm, wt), lambda i, j: (i, j)),
        out_shape=jax.ShapeDtypeStruct((T, ncols * LANES), out_dtype),
        compiler_params=_params(("parallel", "parallel")),
    )(x, cos_t, sin_t)


ATT_TQ = 512
ATT_TK = 256


def _mla_masks(shape):
    lane = lax.broadcasted_iota(jnp.int32, shape, 1)
    m0 = (lane < HEAD) | ((lane >= LANES) & (lane < LANES + MLA_ROPE))
    m1 = ((lane >= HEAD) & (lane < LANES)) | ((lane >= LANES + MLA_ROPE) & (lane < LANES + 2 * MLA_ROPE))
    return m0, m1


def _by_twos(n, step, carry):
    carry = lax.fori_loop(0, n // 2, lambda i, c: step(2 * i + 1, step(2 * i, c)), carry)
    return lax.fori_loop(0, n % 2, lambda _, c: step(n - 1, c), carry)


def _chunk_ok(tq, tk, d):
    row = lax.broadcasted_iota(jnp.int32, (tq, tk), 0)
    col = lax.broadcasted_iota(jnp.int32, (tq, tk), 1) + d * tk
    return jnp.concatenate([(col >> CHUNK_BITS) <= (row >> CHUNK_BITS)] * 2, axis=0)


def _rotate(x, cos_t, sin_t):
    half = MLA_ROPE // 2
    lane = lax.broadcasted_iota(jnp.int32, x.shape, 1)
    first = (lane & (MLA_ROPE - 1)) < half
    swapped = jnp.where(first, pltpu.roll(x, x.shape[1] - half, 1), pltpu.roll(x, half, 1))
    return x * cos_t + swapped * sin_t


def _mla_fwd(q, cos_q, sin_q, kv, kr, name="mla_fwd"):
    T = q.shape[0]
    tq, tk = _pick(T, ATT_TQ), _pick(T, ATT_TK)
    nd = tq // tk
    npair = MLA_HEADS // 2
    scale = (MLA_NOPE + MLA_ROPE) ** -0.5

    def body(q_ref, c_ref, s_ref, kn_ref, v_ref, kr_ref, o_ref, lse_ref):
        m_idx = pl.program_id(1)
        qv = _rotate(q_ref[...], c_ref[...], s_ref[...]).astype(BF16)
        m0, m1 = _mla_masks(qv.shape)
        qs = jnp.concatenate([jnp.where(m0, qv, 0), jnp.where(m1, qv, 0)], axis=0).astype(BF16)

        def block(kb, carry, ok):
            ks = pl.ds(pl.multiple_of(kb * tk, tk), tk)
            kcat = jnp.concatenate([kn_ref[ks, :], kr_ref[ks, :]], axis=1)
            mx, l, acc = carry
            s = _dot(qs, kcat, NT) * scale
            if ok is not None:
                s = jnp.where(ok, s, NEG)
            mn = jnp.maximum(mx, jnp.max(s, axis=-1, keepdims=True))
            alpha = jnp.exp(mx - mn)
            p = jnp.exp(s - mn)
            return (mn, alpha * l + jnp.sum(p, axis=-1, keepdims=True),
                    alpha * acc + _dot(p.astype(BF16), v_ref[ks, :], NN))

        init = (jnp.full((2 * tq, 1), NEG, F32), jnp.zeros((2 * tq, 1), F32), jnp.zeros((2 * tq, LANES), F32))
        res = init
        for d in range(nd):
            res = block(m_idx * nd + d, res, _chunk_ok(tq, tk, d))
        mx, l, acc = _by_twos(m_idx * nd, lambda kb, c: block(kb, c, None), res)
        h0 = lax.broadcasted_iota(jnp.int32, (tq, LANES), 1) < HEAD
        o_ref[...] = _two_heads(acc * (1.0 / l), h0).astype(o_ref.dtype)
        lse_ref[...] = _two_heads(jnp.broadcast_to(mx + jnp.log(l), (2 * tq, LANES)), h0)

    full = lambda col: pl.BlockSpec((T, LANES), col)
    table = pl.BlockSpec((tq, 2 * LANES), lambda p, m: (m, 0))
    return pl.pallas_call(
        body, name=name, grid=(npair, T // tq),
        in_specs=[pl.BlockSpec((tq, 2 * LANES), lambda p, m: (m, p)), table, table,
                  full(lambda p, m: (0, p)), full(lambda p, m: (0, npair + p)), full(lambda p, m: (0, 0))],
        out_specs=[pl.BlockSpec((tq, LANES), lambda p, m: (m, p)),
                   pl.BlockSpec((tq, LANES), lambda p, m: (m, p))],
        out_shape=[jax.ShapeDtypeStruct((T, npair * LANES), BF16),
                   jax.ShapeDtypeStruct((T, npair * LANES), F32)],
        compiler_params=_params(("parallel", "arbitrary")),
    )(q, cos_q, sin_q, kv, kv, kr)


def _mla_bwd(q, cos_q, sin_q, kv, kr, o, lse, do, do_col0, name="mla_bwd"):
    T = q.shape[0]
    tq, tk = _pick(T, ATT_TQ), _pick(T, ATT_TK)
    nd = tq // tk
    npair = MLA_HEADS // 2
    scale = (MLA_NOPE + MLA_ROPE) ** -0.5

    def body(q_ref, c_ref, s_ref, kn_ref, v_ref, kr_ref, o_ref, lse_ref, do_ref, dq_ref, dkn_ref, dv_ref, dkr_ref,
             dkn_acc, dv_acc):
        p_idx, m_idx = pl.program_id(0), pl.program_id(1)

        @pl.when(m_idx == 0)
        def _():
            dkn_acc[...] = jnp.zeros_like(dkn_acc)
            dv_acc[...] = jnp.zeros_like(dv_acc)

        @pl.when((m_idx == 0) & (p_idx == 0))
        def _():
            dkr_ref[...] = jnp.zeros_like(dkr_ref)

        qv = _rotate(q_ref[...], c_ref[...], s_ref[...]).astype(BF16)
        m0, m1 = _mla_masks(qv.shape)
        qs = jnp.concatenate([jnp.where(m0, qv, 0), jnp.where(m1, qv, 0)], axis=0).astype(BF16)
        dov = do_ref[...].astype(F32)
        h0 = lax.broadcasted_iota(jnp.int32, (tq, LANES), 1) < HEAD
        dos32 = jnp.concatenate([jnp.where(h0, dov, 0.0), jnp.where(h0, 0.0, dov)], axis=0)
        ov = o_ref[...].astype(F32)
        delta = jnp.sum(dos32 * jnp.concatenate([ov, ov], axis=0), axis=-1, keepdims=True)
        dos = dos32.astype(BF16)
        lsev = lse_ref[...]
        lse = jnp.concatenate([lsev[:, 0:1], lsev[:, HEAD:HEAD + 1]], axis=0)

        def block(kb, dq, ok):
            ks = pl.ds(pl.multiple_of(kb * tk, tk), tk)
            kcat = jnp.concatenate([kn_ref[ks, :], kr_ref[ks, :]], axis=1)
            vv = v_ref[ks, :]
            p = jnp.exp(_dot(qs, kcat, NT) * scale - lse)
            if ok is not None:
                p = jnp.where(ok, p, 0.0)
            ds = (p * (_dot(dos, vv, NT) - delta) * scale).astype(BF16)
            dkc = _dot(ds, qs, TN)
            dkn_acc[ks, :] += dkc[:, :LANES]
            dkr_ref[ks, :] += dkc[:, LANES:]
            dv_acc[ks, :] += _dot(p.astype(BF16), dos, TN)
            return dq + _dot(ds, kcat, NN)

        dq = jnp.zeros((2 * tq, 2 * LANES), F32)
        for d in range(nd):
            dq = block(m_idx * nd + d, dq, _chunk_ok(tq, tk, d))
        dq = _by_twos(m_idx * nd, lambda kb, c: block(kb, c, None), dq)
        dq_ref[...] = _rotate(jnp.where(m0, dq[:tq], jnp.where(m1, dq[tq:], 0.0)), c_ref[...],
                              -s_ref[...]).astype(BF16)

        @pl.when(m_idx == T // tq - 1)
        def _():
            dkn_ref[...] = dkn_acc[...].astype(BF16)
            dv_ref[...] = dv_acc[...].astype(BF16)

    full = lambda col: pl.BlockSpec((T, LANES), col)
    blk = lambda col: pl.BlockSpec((tq, LANES), col)
    table = pl.BlockSpec((tq, 2 * LANES), lambda p, m: (m, 0))
    return pl.pallas_call(
        body, name=name, grid=(npair, T // tq),
        in_specs=[pl.BlockSpec((tq, 2 * LANES), lambda p, m: (m, p)), table, table,
                  full(lambda p, m: (0, p)), full(lambda p, m: (0, npair + p)), full(lambda p, m: (0, 0)),
                  blk(lambda p, m: (m, p)), blk(lambda p, m: (m, p)),
                  blk(lambda p, m: (m, do_col0 + p))],
        out_specs=[pl.BlockSpec((tq, 2 * LANES), lambda p, m: (m, p)),
                   full(lambda p, m: (0, p)), full(lambda p, m: (0, p)), full(lambda p, m: (0, 0))],
        out_shape=[jax.ShapeDtypeStruct((T, npair * 2 * LANES), BF16),
                   jax.ShapeDtypeStruct((T, npair * LANES), BF16),
                   jax.ShapeDtypeStruct((T, npair * LANES), BF16),
                   jax.ShapeDtypeStruct((T, LANES), F32)],
        scratch_shapes=[pltpu.VMEM((T, LANES), F32)] * 2,
        compiler_params=_params(("arbitrary", "arbitrary")),
    )(q, cos_q, sin_q, kv, kv, kr, o, lse, do)


def _split_dot(x, tri):
    hi = x.astype(BF16)
    lo = (x - hi.astype(F32)).astype(BF16)
    both = _dot(jnp.concatenate([hi, lo], axis=0), tri, NN)
    return both[:x.shape[0]] + both[x.shape[0]:]


def _sb_terms(qh, kk, before):
    z = _dot(qh, kk, NT)
    sp = jnp.maximum(z, 0.0) + jnp.log(1.0 + jnp.exp(-jnp.abs(z)))
    lk = -sp if before is None else jnp.where(before, -sp, 0.0)
    return z, sp, lk


def _sb_setup(q_ref, tq, tk, scale):
    qv = (q_ref[...].astype(F32) * scale).astype(BF16)
    lane = lax.broadcasted_iota(jnp.int32, (tq, LANES), 1)
    h0 = lane < HEAD
    qs = jnp.concatenate([jnp.where(h0, qv, 0), jnp.where(h0, 0, qv)], axis=0).astype(BF16)
    row = lax.broadcasted_iota(jnp.int32, (tk, tk), 0)
    col = lax.broadcasted_iota(jnp.int32, (tk, tk), 1)
    return qs, h0, row, col


def _sb_before(tq, tk, d):
    row = lax.broadcasted_iota(jnp.int32, (tq, tk), 0)
    col = lax.broadcasted_iota(jnp.int32, (tq, tk), 1) + d * tk
    return jnp.concatenate([col < row] * 2, axis=0)


def _two_heads(x, h0):
    tq = x.shape[0] // 2
    return jnp.where(h0, x[:tq], x[tq:])


def _sb_fwd(qkv, col0, dep, name="sb_fwd"):
    T = qkv.shape[0]
    tq, tk = _pick(T, ATT_TQ), _pick(T, ATT_TK)
    nd = tq // tk
    npair = SB_HEADS // 2
    scale = SB_DIM ** -0.5

    def body(q_ref, k_ref, v_ref, dep_ref, o_ref, o32_ref, w_ref, sp_ref):
        m_idx = pl.program_id(1)
        qs, h0, row, col = _sb_setup(q_ref, tq, tk, scale)
        later = (row > col).astype(BF16)

        def block(kb, carry, before):
            ks = pl.ds(pl.multiple_of(kb * tk, tk), tk)
            c, acc = carry
            z, sp, lk = _sb_terms(qs, k_ref[ks, :].astype(BF16), before)
            w = jnp.exp((z - sp) + _split_dot(lk, later) + c)
            if before is not None:
                w = jnp.where(before, w, 0.0)
            wb = w.astype(BF16)
            w_ref[0, 0, kb] = wb
            sp_ref[0, 0, kb] = sp.astype(BF16)
            return (c + jnp.sum(lk, axis=-1, keepdims=True), acc + _dot(wb, v_ref[ks, :].astype(BF16), NN))

        init = (jnp.zeros((2 * tq, 1), F32), jnp.zeros((2 * tq, LANES), F32))
        res = init
        for d in reversed(range(nd)):
            res = block(m_idx * nd + d, res, _sb_before(tq, tk, d))
        res = _by_twos(m_idx * nd, lambda i, c: block(m_idx * nd - 1 - i, c, None), res)
        o = _two_heads(res[1], h0)
        o_ref[...] = o.astype(o_ref.dtype)
        o32_ref[...] = o

    full = lambda col: pl.BlockSpec((T, LANES), col)
    blk = pl.BlockSpec((tq, LANES), lambda p, m: (m, p))
    return pl.pallas_call(
        body, name=name, grid=(npair, T // tq),
        in_specs=[pl.BlockSpec((tq, LANES), lambda p, m: (m, col0 + p)),
                  full(lambda p, m: (0, col0 + npair + p)), full(lambda p, m: (0, col0 + 2 * npair + p)),
                  pl.BlockSpec((8, LANES), lambda p, m: (0, 0))],
        out_specs=[blk, blk] + [pl.BlockSpec((1, 1, T // tk, 2 * tq, tk), lambda p, m: (p, m, 0, 0, 0))] * 2,
        out_shape=[jax.ShapeDtypeStruct((T, npair * LANES), BF16), jax.ShapeDtypeStruct((T, npair * LANES), F32)]
        + [jax.ShapeDtypeStruct((npair, T // tq, T // tk, 2 * tq, tk), BF16)] * 2,
        compiler_params=_params(("parallel", "arbitrary")),
    )(qkv, qkv, qkv, dep)


def _sb_bwd(qkv, col0, o32, w_all, sp_all, do, do_col0, dep, name="sb_bwd"):
    T = qkv.shape[0]
    tq, tk = _pick(T, ATT_TQ), _pick(T, ATT_TK)
    nd = tq // tk
    npair = SB_HEADS // 2
    scale = SB_DIM ** -0.5

    def body(q_ref, k_ref, v_ref, o_ref, w_ref, sp_ref, do_ref, dep_ref, dq_ref, dk_ref, dv_ref, dk_acc, dv_acc):
        m_idx = pl.program_id(1)

        @pl.when(m_idx == 0)
        def _():
            dk_acc[...] = jnp.zeros_like(dk_acc)
            dv_acc[...] = jnp.zeros_like(dv_acc)

        qs, h0, row, col = _sb_setup(q_ref, tq, tk, scale)
        dov = do_ref[...].astype(F32)
        dos = jnp.concatenate([jnp.where(h0, dov, 0.0), jnp.where(h0, 0.0, dov)], axis=0).astype(BF16)
        ov = o_ref[...]
        etot = jnp.sum(dos.astype(F32) * jnp.concatenate([ov, ov], axis=0), axis=-1, keepdims=True)
        from_here = (row >= col).astype(BF16)

        def block(kb, carry, before):
            ks = pl.ds(pl.multiple_of(kb * tk, tk), tk)
            kk = k_ref[ks, :].astype(BF16)
            vv = v_ref[ks, :].astype(BF16)
            es, dqa = carry
            wb = w_ref[0, 0, kb]
            e = wb.astype(F32) * _dot(dos, vv, NT)
            prev = etot - (_split_dot(e, from_here) + es)
            sig_neg = jnp.exp(-sp_ref[0, 0, kb].astype(F32))
            dz = e * sig_neg - (1.0 - sig_neg) * prev
            if before is not None:
                dz = jnp.where(before, dz, 0.0)
            dzb = dz.astype(BF16)
            dk_acc[ks, :] += _dot(dzb, qs, TN)
            dv_acc[ks, :] += _dot(wb, dos, TN)
            return es + jnp.sum(e, axis=-1, keepdims=True), dqa + _dot(dzb, kk, NN)

        init = (jnp.zeros((2 * tq, 1), F32), jnp.zeros((2 * tq, LANES), F32))
        res = init
        for d in reversed(range(nd)):
            res = block(m_idx * nd + d, res, _sb_before(tq, tk, d))
        res = _by_twos(m_idx * nd, lambda i, c: block(m_idx * nd - 1 - i, c, None), res)
        dq_ref[...] = (_two_heads(res[1], h0) * scale).astype(BF16)

        @pl.when(m_idx == T // tq - 1)
        def _():
            dk_ref[...] = dk_acc[...].astype(BF16)
            dv_ref[...] = dv_acc[...].astype(BF16)

    full = lambda col: pl.BlockSpec((T, LANES), col)
    blk = lambda col: pl.BlockSpec((tq, LANES), col)
    return pl.pallas_call(
        body, name=name, grid=(npair, T // tq),
        in_specs=[blk(lambda p, m: (m, col0 + p)),
                  full(lambda p, m: (0, col0 + npair + p)), full(lambda p, m: (0, col0 + 2 * npair + p)),
                  blk(lambda p, m: (m, p)),
                  pl.BlockSpec((1, 1, T // tk, 2 * tq, tk), lambda p, m: (p, m, 0, 0, 0)),
                  pl.BlockSpec((1, 1, T // tk, 2 * tq, tk), lambda p, m: (p, m, 0, 0, 0)),
                  blk(lambda p, m: (m, do_col0 + p)), pl.BlockSpec((8, LANES), lambda p, m: (0, 0))],
        out_specs=[blk(lambda p, m: (m, p)), full(lambda p, m: (0, p)), full(lambda p, m: (0, p))],
        out_shape=[jax.ShapeDtypeStruct((T, npair * LANES), BF16)] * 3,
        scratch_shapes=[pltpu.VMEM((T, LANES), F32)] * 2,
        compiler_params=_params(("arbitrary", "arbitrary")),
    )(qkv, qkv, qkv, o32, w_all, sp_all, do, dep)


def _band_in_window():
    cq = lax.broadcasted_iota(jnp.int32, (BAND_TQ, BAND_W), 0) >> CHUNK_BITS
    ckp = lax.broadcasted_iota(jnp.int32, (BAND_TQ, BAND_W), 1) >> CHUNK_BITS
    return (ckp >= cq) & (ckp <= cq + LEFT_CHUNKS)


def _band_real(m_idx):
    j = lax.broadcasted_iota(jnp.int32, (BAND_TQ, BAND_W), 1)
    return j >= PAD_KEYS - m_idx * BAND_TQ


def _band_probs(qh, kw, bias, real, scale):
    s = jnp.where(real, _dot(qh, kw, NT) * scale + bias, NEG)
    e = jnp.exp(s - jnp.max(s, axis=-1, keepdims=True))
    return e * (1.0 / jnp.sum(e, axis=-1, keepdims=True))


BAND_SUB = 4


def _band_fwd(qkv, k_pad, v_pad, bias_w, name="band_fwd"):
    T = qkv.shape[0]
    npair = C_HEADS // 2
    scale = C_DIM ** -0.5
    rows = BAND_SUB * BAND_TQ

    def body(q_ref, k_ref, v_ref, b_ref, o_ref, p_ref):
        lane = lax.broadcasted_iota(jnp.int32, (BAND_TQ, LANES), 1)
        h0 = lane < HEAD
        bias = jnp.concatenate([b_ref[0], b_ref[1]], axis=0)
        for sub in range(BAND_SUB):
            m_idx = pl.program_id(1) * BAND_SUB + sub
            win = pl.ds(pl.multiple_of(m_idx * BAND_TQ, BAND_TQ), BAND_W)
            kw, vw = k_ref[win, :], v_ref[win, :]
            qv = q_ref[sub * BAND_TQ:(sub + 1) * BAND_TQ, :]
            qs = jnp.concatenate([jnp.where(h0, qv, 0), jnp.where(h0, 0, qv)], axis=0).astype(BF16)
            p = _band_probs(qs, kw, bias, jnp.concatenate([_band_real(m_idx)] * 2, axis=0), scale).astype(BF16)
            p_ref[0, sub] = p
            o = _two_heads(_dot(p, vw, NN), h0)
            o_ref[sub * BAND_TQ:(sub + 1) * BAND_TQ, :] = o.astype(o_ref.dtype)

    Tp = T + PAD_KEYS
    return pl.pallas_call(
        body, name=name, grid=(npair, T // rows),
        in_specs=[pl.BlockSpec((rows, LANES), lambda p, m: (m, p)),
                  pl.BlockSpec((Tp, LANES), lambda p, m: (0, p)),
                  pl.BlockSpec((Tp, LANES), lambda p, m: (0, p)),
                  pl.BlockSpec((2, BAND_TQ, BAND_W), lambda p, m: (p, 0, 0))],
        out_specs=[pl.BlockSpec((rows, LANES), lambda p, m: (m, p)),
                   pl.BlockSpec((1, BAND_SUB, 2 * BAND_TQ, BAND_W), lambda p, m: (p, m, 0, 0))],
        out_shape=[jax.ShapeDtypeStruct((T, npair * LANES), BF16),
                   jax.ShapeDtypeStruct((npair, T // BAND_TQ, 2 * BAND_TQ, BAND_W), BF16)],
        compiler_params=_params(("parallel", "arbitrary")),
    )(qkv, k_pad, v_pad, bias_w)


def _band_bwd(qkv, k_pad, v_pad, probs, do, name="band_bwd"):
    T = qkv.shape[0]
    npair = C_HEADS // 2
    scale = C_DIM ** -0.5

    rows = BAND_SUB * BAND_TQ

    def body(q_ref, k_ref, v_ref, p_ref, do_ref, dq_ref, dk_ref, dv_ref, db_ref, dk_acc, dv_acc):
        @pl.when(pl.program_id(1) == 0)
        def _():
            dk_acc[...] = jnp.zeros_like(dk_acc)
            dv_acc[...] = jnp.zeros_like(dv_acc)
            db_ref[...] = jnp.zeros_like(db_ref)

        lane = lax.broadcasted_iota(jnp.int32, (BAND_TQ, LANES), 1)
        h0 = lane < HEAD
        dbs = jnp.zeros((2 * BAND_TQ, BAND_W), F32)
        for sub in range(BAND_SUB):
            m_idx = pl.program_id(1) * BAND_SUB + sub
            win = pl.ds(pl.multiple_of(m_idx * BAND_TQ, BAND_TQ), BAND_W)
            kw, vw = k_ref[win, :], v_ref[win, :]
            qv = q_ref[sub * BAND_TQ:(sub + 1) * BAND_TQ, :]
            dov = do_ref[sub * BAND_TQ:(sub + 1) * BAND_TQ, :].astype(F32)
            qs = jnp.concatenate([jnp.where(h0, qv, 0), jnp.where(h0, 0, qv)], axis=0).astype(BF16)
            dos = jnp.concatenate([jnp.where(h0, dov, 0.0), jnp.where(h0, 0.0, dov)], axis=0).astype(BF16)
            pb = p_ref[0, sub]
            p = pb.astype(F32)
            dp = _dot(dos, vw, NT)
            dsb = p * (dp - jnp.sum(p * dp, axis=-1, keepdims=True))
            dbs = dbs + dsb
            dsq = (dsb * scale).astype(BF16)
            dq_ref[sub * BAND_TQ:(sub + 1) * BAND_TQ, :] = _two_heads(_dot(dsq, kw, NN), h0).astype(BF16)
            dk_acc[win, :] += _dot(dsq, qs, TN)
            dv_acc[win, :] += _dot(pb, dos, TN)
        db_ref[0] += dbs[:BAND_TQ]
        db_ref[1] += dbs[BAND_TQ:]

        @pl.when(pl.program_id(1) == T // rows - 1)
        def _():
            dk_ref[...] = dk_acc[...].astype(BF16)
            dv_ref[...] = dv_acc[...].astype(BF16)

    Tp = T + PAD_KEYS
    blk = lambda col: pl.BlockSpec((rows, LANES), col)
    full = pl.BlockSpec((Tp, LANES), lambda p, m: (0, p))
    bias = pl.BlockSpec((2, BAND_TQ, BAND_W), lambda p, m: (p, 0, 0))
    prob = pl.BlockSpec((1, BAND_SUB, 2 * BAND_TQ, BAND_W), lambda p, m: (p, m, 0, 0))
    return pl.pallas_call(
        body, name=name, grid=(npair, T // rows),
        in_specs=[blk(lambda p, m: (m, p)), full, full, prob, blk(lambda p, m: (m, p))],
        out_specs=[blk(lambda p, m: (m, p)), full, full, bias],
        out_shape=[jax.ShapeDtypeStruct((T, npair * LANES), BF16),
                   jax.ShapeDtypeStruct((Tp, npair * LANES), BF16),
                   jax.ShapeDtypeStruct((Tp, npair * LANES), BF16),
                   jax.ShapeDtypeStruct((C_HEADS, BAND_TQ, BAND_W), F32)],
        scratch_shapes=[pltpu.VMEM((Tp, LANES), F32)] * 2,
        compiler_params=_params(("arbitrary", "arbitrary")),
    )(qkv, k_pad, v_pad, probs, do)


def _skew_bits(x, left):
    w = x.shape[1]
    row = lax.broadcasted_iota(jnp.int32, x.shape, 0)
    for b in range(BAND_TQ.bit_length() - 1):
        amt = (w - (1 << b)) if left else (1 << b)
        x = jnp.where((row >> b) & 1 == 1, pltpu.roll(x, amt, 1), x)
    return x


def _toeplitz(diag, name="toeplitz"):
    H = diag.shape[0]

    def body(d_ref, o_ref):
        x = jnp.broadcast_to(d_ref[0], (BAND_TQ, TOEP_W))
        o_ref[0] = jnp.where(_band_in_window(), _skew_bits(x, left=False)[:, BAND_TQ:], NEG)

    return pl.pallas_call(
        body, name=name, grid=(H,),
        in_specs=[pl.BlockSpec((1, 1, TOEP_W), lambda h: (h, 0, 0))],
        out_specs=pl.BlockSpec((1, BAND_TQ, BAND_W), lambda h: (h, 0, 0)),
        out_shape=jax.ShapeDtypeStruct((H, BAND_TQ, BAND_W), F32),
        compiler_params=_params(("parallel",)),
    )(diag.reshape(H, 1, TOEP_W))


def _toeplitz_bwd(dbias, name="toeplitz_bwd"):
    H = dbias.shape[0]

    def body(d_ref, o_ref):
        x = jnp.concatenate([jnp.zeros((BAND_TQ, BAND_TQ), F32), d_ref[0]], axis=1)
        h = BAND_TQ // 2
        while h >= 8:
            x = x[:h] + pltpu.roll(x[h:2 * h], TOEP_W - h, 1)
            h //= 2
        o_ref[0] = jnp.sum(_skew_bits(x, left=True), axis=0, keepdims=True)

    return pl.pallas_call(
        body, name=name, grid=(H,),
        in_specs=[pl.BlockSpec((1, BAND_TQ, BAND_W), lambda h: (h, 0, 0))],
        out_specs=pl.BlockSpec((1, 1, TOEP_W), lambda h: (h, 0, 0)),
        out_shape=jax.ShapeDtypeStruct((H, 1, TOEP_W), F32),
        compiler_params=_params(("parallel",)),
    )(dbias).reshape(H, TOEP_W)


_HBM = pl.BlockSpec(memory_space=pltpu.HBM)
_SEM = pl.BlockSpec(memory_space=pltpu.SEMAPHORE)
_EFFECT = pltpu.SideEffectType.DATAFLOW_SIDE_EFFECTING


def _peers():
    x, y, c = lax.axis_index("x"), lax.axis_index("y"), lax.axis_index("c")
    out = []
    for k in range(1, N_DEV):
        peer = (1 - x if (k >> 2) & 1 else x, 1 - y if (k >> 1) & 1 else y, 1 - c if k & 1 else c)
        out.append((peer, 4 * peer[0] + 2 * peer[1] + peer[2]))
    return 4 * x + 2 * y + c, out


def _split_copies(ins, lands, scatter, send_sem, recv_sem, arriving):
    me, peers = _peers()
    out = []
    for a in range(len(ins)):
        for peer, idx in peers:
            out.append(pltpu.make_async_remote_copy(
                src_ref=ins[a].at[idx] if scatter[a] else ins[a],
                dst_ref=lands[a].at[idx if arriving else me], send_sem=send_sem, recv_sem=recv_sem,
                device_id=peer, device_id_type=pl.DeviceIdType.MESH))
    return out


def _landing_zones(arrays, scatter):
    return [lax.empty((N_DEV,) + (a.shape[1:] if s else a.shape), a.dtype) for a, s in zip(arrays, scatter)]


def _place_own(arrays, scatter, name):
    n = len(arrays)
    lands = _landing_zones(arrays, scatter)
    me = (4 * lax.axis_index("x") + 2 * lax.axis_index("y") + lax.axis_index("c")).astype(jnp.int32).reshape(1)

    def body(me_ref, *refs):
        for a in range(n):
            refs[2 * n + a][...] = refs[a][...].reshape(refs[2 * n + a].shape)

    def row_spec(shape):
        zeros = (0,) * (len(shape) - 1)
        return pl.BlockSpec((1,) + tuple(shape[1:]), lambda i, me_ref: (me_ref[0],) + zeros)

    in_specs = [row_spec(a.shape) if s else pl.BlockSpec(a.shape, lambda i, me_ref, nd=a.ndim: (0,) * nd)
                for a, s in zip(arrays, scatter)]
    return pl.pallas_call(
        body, name=name,
        out_shape=[jax.ShapeDtypeStruct(l.shape, l.dtype) for l in lands],
        grid_spec=pltpu.PrefetchScalarGridSpec(
            num_scalar_prefetch=1, grid=(1,),
            in_specs=in_specs + [pl.BlockSpec(memory_space=pl.ANY)] * n,
            out_specs=[row_spec(l.shape) for l in lands]),
        input_output_aliases={1 + n + i: i for i in range(n)},
        compiler_params=_params(("arbitrary",)),
    )(me, *arrays, *lands)


def _exchange_start_groups(groups, scatter, after, name, lands=None):
    sizes = [len(g) for g in groups]
    arrays = [a for g in groups for a in g]
    n, ng = len(arrays), len(groups)
    flags = [scatter] * n
    if lands is None:
        lands = list(_place_own(arrays, flags, name=name.replace("_start_", "_own_")))
    else:
        lands = [l for g in lands for l in g]
    starts = np.cumsum([0] + sizes)

    def body(*refs):
        ins, lnd = refs[:n], refs[n:2 * n]
        sems = refs[2 * n + 1:2 * n + 1 + 2 * ng]
        token = refs[-1]
        for g in range(ng):
            sl = slice(starts[g], starts[g + 1])
            for cp in _split_copies(ins[sl], lnd[sl], flags[sl], sems[2 * g], sems[2 * g + 1], arriving=False):
                cp.start()
        token[...] = jnp.zeros_like(token)

    hbm = lambda a: pltpu.HBM(a.shape, a.dtype)
    out = pl.pallas_call(
        body, name=name,
        out_shape=(*[pltpu.SemaphoreType.DMA(())] * (2 * ng),
                   *[hbm(a) for a in arrays], *[hbm(a) for a in lands],
                   jax.ShapeDtypeStruct((8, LANES), F32)),
        in_specs=[_HBM] * (2 * n) + [pl.BlockSpec(memory_space=pl.ANY)],
        out_specs=(*[_SEM] * (2 * ng), *([_HBM] * (2 * n)), pl.BlockSpec(memory_space=pltpu.VMEM)),
        input_output_aliases={i: 2 * ng + i for i in range(2 * n)},
        compiler_params=pltpu.CompilerParams(has_side_effects=_EFFECT),
    )(*[pltpu.with_memory_space_constraint(a, pltpu.HBM) for a in list(arrays) + lands], after)
    ins_out, lands_out = out[2 * ng:2 * ng + n], out[2 * ng + n:2 * ng + 2 * n]
    handles = [(out[2 * g], out[2 * g + 1], list(ins_out[starts[g]:starts[g + 1]]),
                list(lands_out[starts[g]:starts[g + 1]]), tuple(flags[starts[g]:starts[g + 1]]))
               for g in range(ng)]
    return handles, out[-1]


def _exchange_start(arrays, scatter, after, name):
    assert len(set(scatter)) == 1
    handles, token = _exchange_start_groups([list(arrays)], scatter[0], after, name)
    return handles[0], token


def _exchange_wait(handle, after, name):
    send_sem, recv_sem, ins, lands, scatter = handle
    n = len(ins)
    after = after if isinstance(after, tuple) else (after,)

    def body(*refs):
        i_ref, l_ref = refs[:n], refs[n:2 * n]
        s_sem, r_sem = refs[2 * n:2 * n + 2]
        for cp in _split_copies(i_ref, l_ref, scatter, s_sem, r_sem, arriving=False):
            cp.wait_send()
        for cp in _split_copies(i_ref, l_ref, scatter, s_sem, r_sem, arriving=True):
            cp.wait_recv()

    hbm = lambda a: pltpu.HBM(a.shape, a.dtype)
    out = pl.pallas_call(
        body, name=name,
        out_shape=tuple(hbm(a) for a in ins + lands),
        in_specs=[_HBM] * (2 * n) + [_SEM, _SEM] + [pl.BlockSpec(memory_space=pl.ANY)] * len(after),
        out_specs=tuple([_HBM] * (2 * n)),
        input_output_aliases={i: i for i in range(2 * n)},
        compiler_params=pltpu.CompilerParams(has_side_effects=_EFFECT),
    )(*ins, *lands, send_sem, recv_sem, *after)
    return list(out[n:])


_SIBLING = 1
_CHIPS = (4, 2, 6)


def _peer_of(k):
    x, y, c = lax.axis_index("x"), lax.axis_index("y"), lax.axis_index("c")
    peer = (1 - x if (k >> 2) & 1 else x, 1 - y if (k >> 1) & 1 else y, 1 - c if k & 1 else c)
    return peer, 4 * peer[0] + 2 * peer[1] + peer[2]


def _rcopy(src, dst, send_sem, recv_sem, k):
    return pltpu.make_async_remote_copy(src_ref=src, dst_ref=dst, send_sem=send_sem, recv_sem=recv_sem,
                                        device_id=_peer_of(k)[0], device_id_type=pl.DeviceIdType.MESH)


def _gather2_start(groups, lands, after, name):
    sizes = [len(g) for g in groups]
    arrays = [a for g in groups for a in g]
    lands = [l for g in lands for l in g]
    n, ng = len(arrays), len(groups)
    starts = np.cumsum([0] + sizes)

    def body(*refs):
        ins, lnd = refs[:n], refs[n:2 * n]
        sems = refs[2 * n + 1:2 * n + 1 + 4 * ng]
        me, _ = _peers()
        for g in range(ng):
            send_d, recv_d, send_i, recv_i = sems[4 * g:4 * g + 4]
            for a in range(starts[g], starts[g + 1]):
                for k in _CHIPS:
                    _rcopy(ins[a], lnd[a].at[me], send_i, recv_i, k).start()
                _rcopy(ins[a], lnd[a].at[me], send_d, recv_d, _SIBLING).start()
        refs[-1][...] = jnp.zeros_like(refs[-1])

    hbm = lambda a: pltpu.HBM(a.shape, a.dtype)
    out = pl.pallas_call(
        body, name=name,
        out_shape=(*[pltpu.SemaphoreType.DMA(())] * (4 * ng), *[hbm(a) for a in arrays], *[hbm(a) for a in lands],
                   jax.ShapeDtypeStruct((8, LANES), F32)),
        in_specs=[_HBM] * (2 * n) + [pl.BlockSpec(memory_space=pl.ANY)],
        out_specs=(*[_SEM] * (4 * ng), *([_HBM] * (2 * n)), pl.BlockSpec(memory_space=pltpu.VMEM)),
        input_output_aliases={i: 4 * ng + i for i in range(2 * n)},
        compiler_params=pltpu.CompilerParams(has_side_effects=_EFFECT),
    )(*[pltpu.with_memory_space_constraint(a, pltpu.HBM) for a in arrays + lands], after)
    ins_out, lands_out = out[4 * ng:4 * ng + n], out[4 * ng + n:4 * ng + 2 * n]
    handles = [dict(sems=out[4 * g:4 * g + 4], ins=list(ins_out[starts[g]:starts[g + 1]]),
                    lands=list(lands_out[starts[g]:starts[g + 1]])) for g in range(ng)]
    return handles, out[-1]


def _gather2_pass_on(handle, after, name):
    lands, recv_i = handle["lands"], handle["sems"][3]
    n = len(lands)
    after = after if isinstance(after, tuple) else (after,)

    def body(*refs):
        lnd, r_i = refs[:n], refs[n]
        send_f, recv_f = refs[n + 1 + len(after):n + 3 + len(after)]
        for a in range(n):
            for k in _CHIPS:
                row = _peer_of(k)[1]
                _rcopy(lnd[a].at[row], lnd[a].at[row], send_f, r_i, k).wait_recv()
        for a in range(n):
            for k in _CHIPS:
                row = _peer_of(k)[1]
                _rcopy(lnd[a].at[row], lnd[a].at[row], send_f, recv_f, _SIBLING).start()
        refs[-1][...] = jnp.zeros_like(refs[-1])

    hbm = lambda a: pltpu.HBM(a.shape, a.dtype)
    out = pl.pallas_call(
        body, name=name,
        out_shape=(pltpu.SemaphoreType.DMA(()), pltpu.SemaphoreType.DMA(()), *[hbm(a) for a in lands],
                   jax.ShapeDtypeStruct((8, LANES), F32)),
        in_specs=[_HBM] * n + [_SEM] + [pl.BlockSpec(memory_space=pl.ANY)] * len(after),
        out_specs=(_SEM, _SEM, *([_HBM] * n), pl.BlockSpec(memory_space=pltpu.VMEM)),
        input_output_aliases={i: 2 + i for i in range(n)},
        compiler_params=pltpu.CompilerParams(has_side_effects=_EFFECT),
    )(*lands, recv_i, *after)
    return dict(handle, lands=list(out[2:2 + n]), passed=(out[0], out[1])), out[-1]


def _gather2_wait(handle, after, name):
    ins, lands = handle["ins"], handle["lands"]
    send_d, recv_d, send_i, _ = handle["sems"]
    send_f, recv_f = handle["passed"]
    n = len(ins)
    after = after if isinstance(after, tuple) else (after,)

    def body(*refs):
        i_ref, lnd = refs[:n], refs[n:2 * n]
        s_d, r_d, s_i, s_f, r_f = refs[2 * n:2 * n + 5]
        me, _ = _peers()
        sib = _peer_of(_SIBLING)[1]
        for a in range(n):
            _rcopy(i_ref[a], lnd[a].at[sib], s_d, r_d, _SIBLING).wait_send()
            _rcopy(i_ref[a], lnd[a].at[sib], s_d, r_d, _SIBLING).wait_recv()
            for k in _CHIPS:
                row = _peer_of(k)[1]
                _rcopy(i_ref[a], lnd[a].at[me], s_i, r_d, k).wait_send()
                _rcopy(lnd[a].at[row], lnd[a].at[row], s_f, r_f, _SIBLING).wait_send()
                _rcopy(lnd[a].at[row], lnd[a].at[_peer_of(k ^ _SIBLING)[1]], s_f, r_f, _SIBLING).wait_recv()

    hbm = lambda a: pltpu.HBM(a.shape, a.dtype)
    out = pl.pallas_call(
        body, name=name,
        out_shape=tuple(hbm(a) for a in ins + lands),
        in_specs=[_HBM] * (2 * n) + [_SEM] * 5 + [pl.BlockSpec(memory_space=pl.ANY)] * len(after),
        out_specs=tuple([_HBM] * (2 * n)),
        input_output_aliases={i: i for i in range(2 * n)},
        compiler_params=pltpu.CompilerParams(has_side_effects=_EFFECT),
    )(*ins, *lands, send_d, recv_d, send_i, send_f, recv_f, *after)
    return list(out[n:])


def _adamw(w, parts, m, v, name="adamw"):
    R, C = w.shape
    L = len(parts)
    rl = R // L
    tr = max([t for t in range(16, 257, 16) if rl % t == 0], default=rl)
    nb = rl // tr
    c1 = 1.0 - ADAM_B1 ** ADAM_STEP
    c2 = 1.0 - ADAM_B2 ** ADAM_STEP

    def body(*refs):
        w_ref, p_refs, (m_ref, v_ref, g_ref, d_ref, nm_ref, nv_ref) = refs[0], refs[1:1 + L], refs[1 + L:]
        g = None
        for j, p_ref in enumerate(p_refs):
            gj = p_ref[0].astype(F32)
            for i in range(1, N_DEV):
                gj = gj + p_ref[i].astype(F32)
            g = gj if g is None else jnp.where(pl.program_id(0) == j, gj, g)
        nm = ADAM_B1 * m_ref[...] + (1.0 - ADAM_B1) * g
        nv = ADAM_B2 * v_ref[...] + (1.0 - ADAM_B2) * (g * g)
        g_ref[...] = g
        nm_ref[...] = nm
        nv_ref[...] = nv
        d_ref[...] = -ADAM_LR * ((nm / c1) / (jnp.sqrt(nv / c2) + ADAM_EPS) + ADAM_WD * w_ref[...])

    blk = pl.BlockSpec((tr, C), lambda l, i: (l * nb + i, 0))
    part = lambda j: pl.BlockSpec((N_DEV, tr, C), lambda l, i: (0, jnp.where(l == j, i, 0), 0))
    return pl.pallas_call(
        body, name=name, grid=(L, nb),
        in_specs=[blk] + [part(j) for j in range(L)] + [blk, blk],
        out_specs=[blk] * 4,
        out_shape=[jax.ShapeDtypeStruct((R, C), F32)] * 4,
        compiler_params=_params(("arbitrary", "arbitrary")),
    )(w, *parts, m, v)


_O1 = Q_LORA
_O2 = _O1 + KV_LORA
_O3 = _O2 + MLA_ROPE
_NB = SB_HEADS * SB_DIM
IN_W = _O2 + LANES + 3 * _NB
COL_KR = _O2 // LANES
COL_SB = COL_KR + 1


def _w_in_local(w):
    kr = w[_O2:_O3]
    pad = jnp.zeros((LANES - 2 * MLA_ROPE, w.shape[1]), w.dtype)
    return jnp.concatenate([w[:_O2], kr, kr, pad, w[_O3:]], axis=0)


def _w_in_grad(g):
    kr = (g[_O2:_O2 + MLA_ROPE].astype(F32) + g[_O2 + MLA_ROPE:_O2 + 2 * MLA_ROPE].astype(F32)).astype(g.dtype)
    return jnp.concatenate([g[:_O2], kr, g[_O2 + LANES:]], axis=0)


def _w_uq_local(w):
    w3 = w.reshape(MLA_HEADS // 2, 2, MLA_NOPE + MLA_ROPE, w.shape[1])
    nope = w3[:, :, :MLA_NOPE].reshape(MLA_HEADS // 2, 2 * MLA_NOPE, w.shape[1])
    rope = w3[:, :, MLA_NOPE:].reshape(MLA_HEADS // 2, 2 * MLA_ROPE, w.shape[1])
    pad = jnp.zeros((MLA_HEADS // 2, LANES - 2 * MLA_ROPE, w.shape[1]), w.dtype)
    return jnp.concatenate([nope, rope, pad], axis=1).reshape(-1, w.shape[1])


def _w_uq_grad(g):
    g3 = g.reshape(MLA_HEADS // 2, 2 * LANES, g.shape[1])
    nope = g3[:, :2 * MLA_NOPE].reshape(MLA_HEADS // 2, 2, MLA_NOPE, g.shape[1])
    rope = g3[:, LANES:LANES + 2 * MLA_ROPE].reshape(MLA_HEADS // 2, 2, MLA_ROPE, g.shape[1])
    return jnp.concatenate([nope, rope], axis=2).reshape(-1, g.shape[1])


def _w_ukv_local(w):
    w3 = w.reshape(MLA_HEADS, MLA_NOPE + MLA_V, w.shape[1])
    return jnp.concatenate([w3[:, :MLA_NOPE].reshape(-1, w.shape[1]),
                            w3[:, MLA_NOPE:].reshape(-1, w.shape[1])], axis=0)


def _w_ukv_grad(g):
    half = MLA_HEADS * MLA_NOPE
    kn = g[:half].reshape(MLA_HEADS, MLA_NOPE, g.shape[1])
    vv = g[half:].reshape(MLA_HEADS, MLA_V, g.shape[1])
    return jnp.concatenate([kn, vv], axis=1).reshape(-1, g.shape[1])


def _rope_tables(T):
    pos = jnp.arange(T, dtype=F32)
    inv_freq = ROPE_THETA ** (-jnp.arange(0, MLA_ROPE, 2, dtype=F32) / MLA_ROPE)
    ang = pos[:, None] * inv_freq[None, :]
    cos, sin = jnp.cos(ang), jnp.sin(ang)
    ones = jnp.ones((T, LANES - 2 * MLA_ROPE), F32)
    cos_k = jnp.concatenate([cos, cos, cos, cos, ones], axis=1)
    sin_k = jnp.concatenate([-sin, sin, -sin, sin, 0.0 * ones], axis=1)
    cos_q = jnp.concatenate([jnp.ones((T, LANES), F32), cos_k], axis=1)
    sin_q = jnp.concatenate([jnp.zeros((T, LANES), F32), sin_k], axis=1)
    return cos_q, sin_q, cos_k, sin_k


def _bias_diag_index():
    ell = np.arange(TOEP_W)
    return np.clip(BAND_W - ell, -REL_CLIP, REL_CLIP) + REL_CLIP


def _local_step(x, target, small, get_weights, put_grads, prefetch):
    T = x.shape[0]
    cos_q, sin_q, cos_k, sin_k = _rope_tables(T)
    G = {}
    W = dict(small)

    u0 = _rms_fwd(x, W["g_mix"][0:1], name="rms_mix0")
    bias_w = _toeplitz(W["od_rel_bias"][:, _bias_diag_index()])
    W.update(get_weights("in0", (u0, bias_w)))
    proj = _mm(u0, W["w_in_t"], dims="nt", name="proj_in")
    W.update(get_weights("mix0", proj))
    c_q, c_kv = proj[:, :_O1], proj[:, _O1:_O2]
    nq = _rms_fwd(c_q, W["g_cq"], name="rms_cq")
    nkv = _rms_fwd(c_kv, W["g_ckv"], name="rms_ckv")
    qa_raw = _mm(nq, W["w_uq_t"], dims="nt", name="proj_uq")
    kv = _mm(nkv, W["w_ukv_t"], dims="nt", out_dtype=BF16, name="proj_ukv")
    kr = _rope(proj, cos_k, sin_k, COL_KR, 1, BF16, name="rope_k")
    o_a, lse = _mla_fwd(qa_raw, cos_q, sin_q, kv, kr)
    o_b, o_b32, w_b, sp_b = _sb_fwd(proj, COL_SB, prefetch("ffn0", o_a))
    o_ab = jnp.concatenate([o_a, o_b], axis=1)
    h1 = _mm(o_ab, W["ev_w_out"], res=x, name="out_ev")

    def ffn_fwd(h, layer):
        W.update(get_weights(f"ffn{layer}", h))
        return _ffn_fwd(h, W["g_ffn"][layer:layer + 1], W[f"w_gate_t{layer}"], W[f"w_up_t{layer}"],
                        W[f"w_down{layer}"], name=f"ffn_fwd{layer}")

    h2, u1, a0, b0 = ffn_fwd(h1, 0)

    W.update(get_weights("mix1", h2))
    u2 = _rms_fwd(h2, W["g_mix"][1:2], name="rms_mix1")
    qkv = _mm(u2, W["od_w_qkv_t"], dims="nt", out_dtype=BF16, name="proj_qkv")
    nc = C_HEADS * C_DIM
    pad = ((PAD_KEYS, 0), (0, 0))
    k_pad, v_pad = jnp.pad(qkv[:, nc:2 * nc], pad), jnp.pad(qkv[:, 2 * nc:], pad)
    o_c, p_c = _band_fwd(qkv, k_pad, v_pad, bias_w)
    h3 = _mm(o_c, W["od_w_out"], res=h2, name="out_od")
    h4, u3, a1, b1 = ffn_fwd(h3, 1)

    loss, dh, dhb, G["g_final"] = _loss_head(h4, W["g_final"], target)

    def ffn_bwd(dh, dhb, h, u, a, b, layer):
        du, g_gate, g_up, g_down = _ffn_bwd(dhb, u, a, b, W[f"w_gate_t{layer}"], W[f"w_up_t{layer}"],
                                            W[f"w_down{layer}"], name=f"ffn_bwd{layer}")
        tok = put_grads(f"ffn{layer}", {"w_gate_t": g_gate, "w_up_t": g_up, "w_down": g_down})
        return _rms_bwd(h, W["g_ffn"][layer:layer + 1] + tok[:1, :1], du, dres=dh, name=f"rms_ffn_bwd{layer}")

    dh3, dh3b, g_gffn1 = ffn_bwd(dh, dhb, h3, u3, a1, b1, 1)

    do_c = _mm(dh3b, W["od_w_out"], dims="nt", name="out_od_dx")
    g_od_out = _mm(o_c, dh3b, dims="tn", out_dtype=BF16, name="out_od_dw")
    dq_c, dk_p, dv_p, dbias_w = _band_bwd(qkv, k_pad, v_pad, p_c, do_c)
    dqkv = jnp.concatenate([dq_c, dk_p[PAD_KEYS:], dv_p[PAD_KEYS:]], axis=1)
    tok = put_grads("mix1", {"od_w_qkv_t": _mm(dqkv, u2, dims="tn", out_dtype=BF16, name="proj_qkv_dw"),
                             "od_w_out": g_od_out})
    ddiag = _toeplitz_bwd(dbias_w)
    n_far = BAND_W - REL_CLIP + 1
    G["od_rel_bias"] = jnp.concatenate(
        [jnp.zeros((C_HEADS, REL_CLIP - BAND_TQ + 1), F32), ddiag[:, n_far:][:, ::-1],
         jnp.sum(ddiag[:, :n_far], axis=1, keepdims=True)], axis=1)
    dh2, dh2b, g_gmix1 = _mm_rms_bwd(dqkv, W["od_w_qkv_t"], h2, W["g_mix"][1:2] + tok[:1, :1], dh3,
                                     name="proj_qkv_dx")

    dh1, dh1b, g_gffn0 = ffn_bwd(dh2, dh2b, h1, u1, a0, b0, 0)
    G["g_ffn"] = jnp.concatenate([g_gffn0, g_gffn1], axis=0)

    do_ab = _mm(dh1b, W["ev_w_out"], dims="nt", name="out_ev_dx")
    g0 = {"ev_w_out": _mm(o_ab, dh1b, dims="tn", out_dtype=BF16, name="out_ev_dw")}
    dqa_raw, dkn, dva, dkr = _mla_bwd(qa_raw, cos_q, sin_q, kv, kr, o_a, lse, do_ab, 0)
    dlat, g0["w_uq_t"], g0["w_ukv_t"], G["g_cq"], G["g_ckv"] = _latent_bwd(
        proj, nq, nkv, dqa_raw, dkn, dva, dkr, cos_k, sin_k, W["g_cq"], W["g_ckv"], W["w_uq_t"], W["w_ukv_t"])
    tok = put_grads("mix0", g0)
    dqb, dkb, dvb = _sb_bwd(proj, COL_SB, o_b32, w_b, sp_b, do_ab, MLA_HEADS // 2, tok)
    dproj = jnp.concatenate([dlat, dqb, dkb, dvb], axis=1)
    tok = put_grads("in0", {"w_in_t": _mm(dproj, u0, dims="tn", name="proj_in_dw")})
    dx, _, g_gmix0 = _mm_rms_bwd(dproj, W["w_in_t"], x, W["g_mix"][0:1] + tok[:1, :1], dh1, name="proj_in_dx")
    G["g_mix"] = jnp.concatenate([g_gmix0, g_gmix1], axis=0)
    return loss[0, 0], dx, G


_BIG = ["ev_w_in", "ev_w_uq", "ev_w_ukv", "ev_w_out", "od_w_qkv", "od_w_out", "w_gate", "w_up", "w_down"]
_COL_SHARDED = {"ev_w_in", "ev_w_uq", "ev_w_ukv", "od_w_qkv", "w_gate", "w_up"}
_SMALL = ["ev_g_cq", "ev_g_ckv", "od_rel_bias", "g_mix", "g_ffn", "g_final"]
_GROUPS = {
    "in0": ["ev_w_in"],
    "mix0": ["ev_w_uq", "ev_w_ukv", "ev_w_out"],
    "ffn0": ["w_gate0", "w_up0", "w_down0"],
    "mix1": ["od_w_qkv", "od_w_out"],
    "ffn1": ["w_gate1", "w_up1", "w_down1"],
}
_GROUP_SRC = {n + str(l): (n, l) for n in ("w_gate", "w_up", "w_down") for l in (0, 1)}
_BATCHES = {"in0": ["in0"], "layer0": ["mix0", "ffn0"], "layer1": ["mix1", "ffn1"]}
_BATCH_OF = {grp: batch for batch, grps in _BATCHES.items() for grp in grps}
_SMALL_ROWS = 8
_SMALL_COLS = 1792


def _pack_small(vals):
    flat = jnp.concatenate([v.reshape(-1).astype(F32) for v in vals])
    flat = jnp.pad(flat, (0, _SMALL_ROWS * _SMALL_COLS - flat.shape[0]))
    return flat.reshape(_SMALL_ROWS, _SMALL_COLS)


def _unpack_small(packed, like):
    flat = packed.reshape(-1)
    out, off = [], 0
    for v in like:
        out.append(flat[off:off + v.size].reshape(v.shape))
        off += v.size
    return out


def kernel(x, ev_w_in, ev_g_cq, ev_w_uq, ev_g_ckv, ev_w_ukv, ev_w_out, od_w_qkv, od_rel_bias, od_w_out, g_mix, g_ffn, w_gate, w_up, w_down, g_final, loss_target, m_ev_w_in, m_ev_g_cq, m_ev_w_uq, m_ev_g_ckv, m_ev_w_ukv, m_ev_w_out, m_od_w_qkv, m_od_rel_bias, m_od_w_out, m_g_mix, m_g_ffn, m_w_gate, m_w_up, m_w_down, m_g_final, v_ev_w_in, v_ev_g_cq, v_ev_w_uq, v_ev_g_ckv, v_ev_w_ukv, v_ev_w_out, v_od_w_qkv, v_od_rel_bias, v_od_w_out, v_g_mix, v_g_ffn, v_w_gate, v_w_up, v_w_down, v_g_final):
    args = dict(locals())
    w = {n: args[n] for n in _BIG + _SMALL}
    mom = {n: args["m_" + n] for n in _BIG + _SMALL}
    var = {n: args["v_" + n] for n in _BIG + _SMALL}

    own = {}
    for grp, names in _GROUPS.items():
        for n in names:
            base, layer = _GROUP_SRC.get(n, (n, 0))
            shard = w[base][layer:layer + 1]
            own[n] = (jnp.swapaxes(shard, 1, 2) if base in _COL_SHARDED else shard).astype(BF16)
    placed = dict(zip(own, _place_own(list(own.values()), [False] * len(own), name="gather_own")))
    handles, token = _gather2_start(
        [[own[n] for n in names] for names in _GROUPS.values()],
        [[placed[n] for n in names] for names in _GROUPS.values()], x[0, :8, :LANES], name="gather_start")
    gather = dict(zip(_GROUPS, handles))
    pass_before = {"in0": ["in0"], "mix0": ["mix0"]}
    pass_after = {"ffn0": ("mix1", "g_ffn"), "mix1": ("ffn1", "g_mix")}

    def prefetch(grp, after):
        gather[grp], tok = _gather2_pass_on(gather[grp], after, name="gather_pass_" + grp)
        return tok

    def get_weights(grp, after):
        names = _GROUPS[grp]
        after = token if after is None else after
        for g in pass_before.get(grp, []):
            gather[g], _ = _gather2_pass_on(gather[g], after, name="gather_pass_" + g)
        lands = _gather2_wait(gather[grp], after, name="gather_wait_" + grp)
        full = {n: l.reshape(-1, l.shape[-1]) for n, l in zip(names, lands)}
        out = {}
        if grp in pass_after:
            g, gain = pass_after[grp]
            gather[g], tok = _gather2_pass_on(gather[g], lands[0], name="gather_pass_" + g)
            out[gain] = small[gain] + tok[:1, :1]
        if grp == "in0":
            out.update({"w_in_t": _w_in_local(full["ev_w_in"])})
        elif grp == "mix0":
            out.update({"w_uq_t": _w_uq_local(full["ev_w_uq"]), "w_ukv_t": _w_ukv_local(full["ev_w_ukv"]),
                        "ev_w_out": full["ev_w_out"]})
        elif grp == "mix1":
            out.update({"od_w_qkv_t": full["od_w_qkv"], "od_w_out": full["od_w_out"]})
        else:
            layer = grp[-1]
            out.update({"w_gate_t" + layer: full["w_gate" + layer], "w_up_t" + layer: full["w_up" + layer],
                        "w_down" + layer: full["w_down" + layer]})
        return out

    scatter, pending = {}, {}

    def put_grads(grp, g):
        if grp == "in0":
            g = {"ev_w_in": _w_in_grad(g["w_in_t"])}
        elif grp == "mix0":
            g = {"ev_w_uq": _w_uq_grad(g["w_uq_t"]), "ev_w_ukv": _w_ukv_grad(g["w_ukv_t"]),
                 "ev_w_out": g["ev_w_out"]}
        elif grp == "mix1":
            g = {"od_w_qkv": g["od_w_qkv_t"], "od_w_out": g["od_w_out"]}
        else:
            layer = grp[-1]
            g = {"w_gate" + layer: g["w_gate_t"], "w_up" + layer: g["w_up_t"], "w_down" + layer: g["w_down"]}
        pending.update({n: v.reshape(N_DEV, 1, v.shape[0] // N_DEV, v.shape[1]).astype(BF16) for n, v in g.items()})
        batch = _BATCH_OF[grp]
        names = [n for gr in _BATCHES[batch] for n in _GROUPS[gr]]
        if not all(n in pending for n in names):
            return jnp.zeros((8, LANES), F32)
        send = [pending[n] for n in names]
        scatter[batch], tok = _exchange_start(send, [True] * len(names), send[0], name="scatter_start_" + batch)
        return tok

    small = {"g_cq": ev_g_cq, "g_ckv": ev_g_ckv, "od_rel_bias": od_rel_bias[0],
             "g_mix": g_mix + token[0, 0], "g_ffn": g_ffn, "g_final": g_final.reshape(1, -1)}
    loss_part, dx, G = _local_step(x[0], loss_target[0], small, get_weights, put_grads, prefetch)
    g_small = _pack_small([G["g_cq"], G["g_ckv"], G["od_rel_bias"], G["g_mix"], G["g_ffn"], G["g_final"],
                           loss_part.reshape(1)])
    small_handle, _ = _exchange_start([g_small], [False], dx, name="gather_start_small")

    grads, deltas, new_m, new_v = {}, {}, {}, {}
    parts, after = {}, dx

    def wait_parts(batch, after):
        lands = _exchange_wait(scatter[batch], after, name="scatter_wait_" + batch)
        parts.update(zip([n for grp in _BATCHES[batch] for n in _GROUPS[grp]], lands))
        return lands[0]

    def adamw(n):
        col = n in _COL_SHARDED
        rows = lambda a: (jnp.swapaxes(a, 1, 2) if col else a).reshape(-1, a.shape[1 if col else 2])
        layers = [parts[n]] if n in parts else [parts[n + "0"], parts[n + "1"]]
        res = _adamw(rows(w[n]), [p.reshape(N_DEV, -1, p.shape[-1]) for p in layers], rows(mom[n]), rows(var[n]),
                     name="adamw_" + n)
        L, a1, a2 = w[n].shape
        back = lambda r: jnp.swapaxes(r.reshape(L, a2, a1), 1, 2) if col else r.reshape(L, a1, a2)
        grads[n], deltas[n], new_m[n], new_v[n] = [back(r) for r in res]
        return res[0]

    for batch in ("layer1", "layer0"):
        after = wait_parts(batch, after)
    after = wait_parts("in0", tuple(adamw(n) for n in _BIG[1:]))
    after = adamw("ev_w_in")
    small_w = [w[n] for n in _SMALL]
    small_parts = _exchange_wait(small_handle, after, name="gather_wait_small")[0]
    loss = jnp.sum(small_parts.reshape(N_DEV, -1)[:, sum(v.size for v in small_w)])
    res = _adamw(_pack_small(small_w), [small_parts], _pack_small([mom[n] for n in _SMALL]),
                 _pack_small([var[n] for n in _SMALL]), name="adamw_small")
    for d, packed in zip((grads, deltas, new_m, new_v), res):
        for n, val in zip(_SMALL, _unpack_small(packed, small_w)):
            d[n] = val

    order = ["ev_w_in", "ev_g_cq", "ev_w_uq", "ev_g_ckv", "ev_w_ukv", "ev_w_out", "od_w_qkv", "od_rel_bias",
             "od_w_out", "g_mix", "g_ffn", "w_gate", "w_up", "w_down", "g_final"]
    out = [loss, dx[None]]
    for d in (grads, deltas, new_m, new_v):
        out += [d[n] for n in order]
    return tuple(out)
```

```python
import functools

import numpy as np
import jax
import jax.numpy as jnp
from jax import lax
from jax.experimental import pallas as pl
from jax.experimental.pallas import tpu as pltpu

F32 = jnp.float32
BF16 = jnp.bfloat16

D_MODEL = 1024
CHUNK = 64
MLA_HEADS = 8
MLA_NOPE = 64
MLA_ROPE = 32
MLA_V = 64
Q_LORA = 384
KV_LORA = 256
ROPE_THETA = 10000.0
SB_HEADS = 8
SB_DIM = 64
C_HEADS = 16
C_DIM = 64
LEFT_CHUNKS = 8
REL_CLIP = 256
D_FF = 2816
RMS_EPS = 1e-6
ADAM_LR = 0.001
ADAM_B1 = 0.9
ADAM_B2 = 0.999
ADAM_EPS = 1e-08
ADAM_WD = 0.01
ADAM_STEP = 10

N_DEV = 8
LANES = 128
HEAD = 64
assert HEAD == MLA_NOPE == MLA_V == SB_DIM == C_DIM and 2 * HEAD == LANES
CHUNK_BITS = CHUNK.bit_length() - 1
assert 1 << CHUNK_BITS == CHUNK
VMEM_LIMIT = 56 * 1024 * 1024
NEG = -1e30
PAD_KEYS = LEFT_CHUNKS * CHUNK
BAND_TQ = 128
BAND_W = BAND_TQ + PAD_KEYS
TOEP_W = BAND_W + BAND_TQ

NN = (((1,), (0,)), ((), ()))
NT = (((1,), (1,)), ((), ()))
TN = (((0,), (0,)), ((), ()))


def _dot(a, b, dn):
    return lax.dot_general(a, b, dn, preferred_element_type=F32)


def _pick(dim, pref):
    if dim <= pref:
        return dim
    best = None
    for t in range(LANES, pref + 1, LANES):
        if dim % t == 0:
            best = t
    assert best is not None, (dim, pref)
    return best


def _params(sem):
    return pltpu.CompilerParams(dimension_semantics=sem, vmem_limit_bytes=VMEM_LIMIT)


def _mm(a, b, dims="nn", res=None, out_dtype=F32, name="mm"):
    if dims == "nn":
        (M, K), (K2, N) = a.shape, b.shape
    elif dims == "nt":
        (M, K), (N, K2) = a.shape, b.shape
    else:
        (K, M), (K2, N) = a.shape, b.shape
    assert K == K2, (a.shape, b.shape, dims)
    tm, tn, tk = _pick(M, 1024), _pick(N, 1152), _pick(K, 1024)
    nk = K // tk
    dn = {"nn": NN, "nt": NT, "tn": TN}[dims]
    has_res = res is not None

    def body(*refs):
        if has_res:
            a_ref, b_ref, r_ref, o_ref, acc = refs
        else:
            a_ref, b_ref, o_ref, acc = refs
        k = pl.program_id(2)

        @pl.when(k == 0)
        def _():
            acc[...] = jnp.zeros_like(acc)

        acc[...] += _dot(a_ref[...].astype(BF16), b_ref[...].astype(BF16), dn)

        @pl.when(k == nk - 1)
        def _():
            r = acc[...]
            if has_res:
                r = r + r_ref[...]
            o_ref[...] = r.astype(out_dtype)

    a_spec = (pl.BlockSpec((tk, tm), lambda i, j, k: (k, i)) if dims == "tn"
              else pl.BlockSpec((tm, tk), lambda i, j, k: (i, k)))
    b_spec = (pl.BlockSpec((tn, tk), lambda i, j, k: (j, k)) if dims == "nt"
              else pl.BlockSpec((tk, tn), lambda i, j, k: (k, j)))
    o_spec = pl.BlockSpec((tm, tn), lambda i, j, k: (i, j))
    in_specs = [a_spec, b_spec] + ([o_spec] if has_res else [])
    args = (a, b) + ((res,) if has_res else ())
    return pl.pallas_call(
        body, name=name, grid=(M // tm, N // tn, nk),
        in_specs=in_specs, out_specs=o_spec,
        out_shape=jax.ShapeDtypeStruct((M, N), out_dtype),
        scratch_shapes=[pltpu.VMEM((tm, tn), F32)],
        compiler_params=_params(("parallel", "parallel", "arbitrary")),
    )(*args)


def _rms_fwd(x, g, out_dtype=BF16, name="rms_fwd"):
    T, Fd = x.shape
    tm = _pick(T, 256)

    def body(x_ref, g_ref, o_ref):
        xv = x_ref[...]
        r = lax.rsqrt(jnp.mean(xv * xv, axis=-1, keepdims=True) + RMS_EPS)
        o_ref[...] = (xv * r * g_ref[...]).astype(out_dtype)

    return pl.pallas_call(
        body, name=name, grid=(T // tm,),
        in_specs=[pl.BlockSpec((tm, Fd), lambda i: (i, 0)), pl.BlockSpec((1, Fd), lambda i: (0, 0))],
        out_specs=pl.BlockSpec((tm, Fd), lambda i: (i, 0)),
        out_shape=jax.ShapeDtypeStruct((T, Fd), out_dtype),
        compiler_params=_params(("parallel",)),
    )(x, g)


def _rms_bwd(x, g, dy, dres=None, name="rms_bwd"):
    T, Fd = x.shape
    tm = _pick(T, 256)
    has_res = dres is not None

    def body(*refs):
        if has_res:
            x_ref, g_ref, dy_ref, r_ref, dx_ref, dxb_ref, dg_ref = refs
        else:
            x_ref, g_ref, dy_ref, dx_ref, dxb_ref, dg_ref = refs
        xv, dyv = x_ref[...], dy_ref[...]
        r = lax.rsqrt(jnp.mean(xv * xv, axis=-1, keepdims=True) + RMS_EPS)
        gdy = dyv * g_ref[...]
        dot = jnp.mean(xv * gdy, axis=-1, keepdims=True)
        dx = r * gdy - xv * (r * r * r * dot)
        if has_res:
            dx = dx + r_ref[...]
        dx_ref[...] = dx
        dxb_ref[...] = dx.astype(BF16)

        @pl.when(pl.program_id(0) == 0)
        def _():
            dg_ref[...] = jnp.zeros_like(dg_ref)

        dg_ref[...] += jnp.sum(dyv * xv * r, axis=0, keepdims=True)

    row = pl.BlockSpec((tm, Fd), lambda i: (i, 0))
    vec = pl.BlockSpec((1, Fd), lambda i: (0, 0))
    in_specs = [row, vec, row] + ([row] if has_res else [])
    args = (x, g, dy) + ((dres,) if has_res else ())
    return pl.pallas_call(
        body, name=name, grid=(T // tm,),
        in_specs=in_specs, out_specs=[row, row, vec],
        out_shape=[jax.ShapeDtypeStruct((T, Fd), F32), jax.ShapeDtypeStruct((T, Fd), BF16),
                   jax.ShapeDtypeStruct((1, Fd), F32)],
        compiler_params=_params(("arbitrary",)),
    )(*args)


def _mm_rms_bwd(a, b, x, g, dres, name="mm_rms_bwd"):
    T, K = a.shape
    Fd = b.shape[1]
    tm, tk = _pick(T, 512), _pick(K, 1024)
    nk = K // tk

    def body(a_ref, b_ref, x_ref, g_ref, r_ref, dx_ref, dxb_ref, dg_ref, acc):
        i, k = pl.program_id(0), pl.program_id(1)

        @pl.when(k == 0)
        def _():
            acc[...] = jnp.zeros_like(acc)

        @pl.when((k == 0) & (i == 0))
        def _():
            dg_ref[...] = jnp.zeros_like(dg_ref)

        acc[...] += _dot(a_ref[...].astype(BF16), b_ref[...].astype(BF16), NN)

        @pl.when(k == nk - 1)
        def _():
            xv, dyv = x_ref[...], acc[...]
            r = lax.rsqrt(jnp.mean(xv * xv, axis=-1, keepdims=True) + RMS_EPS)
            gdy = dyv * g_ref[...]
            dot = jnp.mean(xv * gdy, axis=-1, keepdims=True)
            dx = r * gdy - xv * (r * r * r * dot) + r_ref[...]
            dx_ref[...] = dx
            dxb_ref[...] = dx.astype(BF16)
            dg_ref[...] += jnp.sum(dyv * xv * r, axis=0, keepdims=True)

    row = pl.BlockSpec((tm, Fd), lambda i, k: (i, 0))
    vec = pl.BlockSpec((1, Fd), lambda i, k: (0, 0))
    return pl.pallas_call(
        body, name=name, grid=(T // tm, nk),
        in_specs=[pl.BlockSpec((tm, tk), lambda i, k: (i, k)), pl.BlockSpec((tk, Fd), lambda i, k: (k, 0)),
                  row, vec, row],
        out_specs=[row, row, vec],
        out_shape=[jax.ShapeDtypeStruct((T, Fd), F32), jax.ShapeDtypeStruct((T, Fd), BF16),
                   jax.ShapeDtypeStruct((1, Fd), F32)],
        scratch_shapes=[pltpu.VMEM((tm, Fd), F32)],
        compiler_params=_params(("arbitrary", "arbitrary")),
    )(a, b, x, g, dres)


def _latent_bwd(proj, nq, nkv, dqa, dkn, dva, dkr, cos_k, sin_k, g_cq, g_ckv, w_uq_t, w_ukv_t, name="latent_bwd"):
    T = proj.shape[0]
    tm = _pick(T, 512)
    wl = _O2

    def rms_bwd(xv, gv, dyv):
        r = lax.rsqrt(jnp.mean(xv * xv, axis=-1, keepdims=True) + RMS_EPS)
        gdy = dyv * gv
        dot = jnp.mean(xv * gdy, axis=-1, keepdims=True)
        return r * gdy - xv * (r * r * r * dot), jnp.sum(dyv * xv * r, axis=0, keepdims=True)

    def body(p_ref, nq_ref, nkv_ref, dqa_ref, dkn_ref, dva_ref, dkr_ref, c_ref, s_ref, gq_ref, gkv_ref, wq_ref, wkv_ref,
             dlat_ref, dwq_ref, dwkv_ref, dgq_ref, dgkv_ref):
        @pl.when(pl.program_id(0) == 0)
        def _():
            for ref in (dwq_ref, dwkv_ref, dgq_ref, dgkv_ref):
                ref[...] = jnp.zeros_like(ref)

        dqv = dqa_ref[...]
        dkv = jnp.concatenate([dkn_ref[...], dva_ref[...]], axis=1)
        pv = p_ref[...]
        dc_q, dgq = rms_bwd(pv[:, :_O1], gq_ref[...], _dot(dqv, wq_ref[...], NN))
        dc_kv, dgkv = rms_bwd(pv[:, _O1:], gkv_ref[...], _dot(dkv, wkv_ref[...], NN))
        dkr_raw = _rotate(dkr_ref[...], c_ref[...], -s_ref[...])
        dlat_ref[...] = jnp.concatenate([dc_q, dc_kv, dkr_raw], axis=1).astype(BF16)
        dwq_ref[...] += _dot(dqv, nq_ref[...], TN)
        dwkv_ref[...] += _dot(dkv, nkv_ref[...], TN)
        dgq_ref[...] += dgq
        dgkv_ref[...] += dgkv

    row = lambda w: pl.BlockSpec((tm, w), lambda i: (i, 0))
    const = lambda a: pl.BlockSpec(a.shape, lambda i: (0, 0))
    outs = [jax.ShapeDtypeStruct((T, wl + LANES), BF16), jax.ShapeDtypeStruct(w_uq_t.shape, F32),
            jax.ShapeDtypeStruct(w_ukv_t.shape, F32), jax.ShapeDtypeStruct(g_cq.shape, F32),
            jax.ShapeDtypeStruct(g_ckv.shape, F32)]
    return pl.pallas_call(
        body, name=name, grid=(T // tm,),
        in_specs=[row(wl), row(_O1), row(_O2 - _O1), row(dqa.shape[1]), row(dkn.shape[1]), row(dva.shape[1]),
                  row(LANES), row(LANES), row(LANES), const(g_cq), const(g_ckv), const(w_uq_t), const(w_ukv_t)],
        out_specs=[row(wl + LANES)] + [const(o) for o in outs[1:]],
        out_shape=outs,
        compiler_params=_params(("arbitrary",)),
    )(proj, nq, nkv, dqa, dkn, dva, dkr, cos_k, sin_k, g_cq, g_ckv, w_uq_t, w_ukv_t)


def _loss_head(h, g, target, name="loss_head"):
    T, Fd = h.shape
    tm = _pick(T, 256)

    def body(h_ref, g_ref, t_ref, loss_ref, dh_ref, dhb_ref, dg_ref):
        xv = h_ref[...]
        r = lax.rsqrt(jnp.mean(xv * xv, axis=-1, keepdims=True) + RMS_EPS)
        diff = xv * r * g_ref[...] - t_ref[...]
        part = 0.5 * jnp.sum(jnp.mean(diff * diff, axis=-1, keepdims=True), axis=0, keepdims=True)
        dyv = diff * (1.0 / Fd)
        gdy = dyv * g_ref[...]
        dot = jnp.mean(xv * gdy, axis=-1, keepdims=True)
        dh = r * gdy - xv * (r * r * r * dot)
        dh_ref[...] = dh
        dhb_ref[...] = dh.astype(BF16)

        @pl.when(pl.program_id(0) == 0)
        def _():
            dg_ref[...] = jnp.zeros_like(dg_ref)
            loss_ref[...] = jnp.zeros_like(loss_ref)

        dg_ref[...] += jnp.sum(dyv * xv * r, axis=0, keepdims=True)
        loss_ref[...] += jnp.broadcast_to(part, loss_ref.shape)

    row = pl.BlockSpec((tm, Fd), lambda i: (i, 0))
    vec = pl.BlockSpec((1, Fd), lambda i: (0, 0))
    return pl.pallas_call(
        body, name=name, grid=(T // tm,),
        in_specs=[row, vec, row],
        out_specs=[pl.BlockSpec((1, LANES), lambda i: (0, 0)), row, row, vec],
        out_shape=[jax.ShapeDtypeStruct((1, LANES), F32), jax.ShapeDtypeStruct((T, Fd), F32),
                   jax.ShapeDtypeStruct((T, Fd), BF16), jax.ShapeDtypeStruct((1, Fd), F32)],
        compiler_params=_params(("arbitrary",)),
    )(h, g, target)


FFN_TF = 256


def _ffn_fwd(h, g, wg_t, wu_t, wd, name="ffn_fwd"):
    T, Dm = h.shape
    Fh = wd.shape[0]
    tm = _pick(T, 2048)
    nf = Fh // FFN_TF

    def body(h_ref, g_ref, wg_ref, wu_ref, wd_ref, o_ref, u_ref, a_ref, b_ref):
        j = pl.program_id(1)

        @pl.when(j == 0)
        def _():
            xv = h_ref[...]
            r = lax.rsqrt(jnp.mean(xv * xv, axis=-1, keepdims=True) + RMS_EPS)
            u_ref[...] = (xv * r * g_ref[...]).astype(BF16)
            o_ref[...] = xv

        u = u_ref[...]
        a = _dot(u, wg_ref[...], NT).astype(BF16)
        b = _dot(u, wu_ref[...], NT).astype(BF16)
        a_ref[...] = a
        b_ref[...] = b
        af = a.astype(F32)
        s = (af * jax.nn.sigmoid(af) * b.astype(F32)).astype(BF16)
        o_ref[...] += _dot(s, wd_ref[...], NN)

    row = pl.BlockSpec((tm, Dm), lambda i, j: (i, 0))
    wblk = pl.BlockSpec((FFN_TF, Dm), lambda i, j: (j, 0))
    ablk = pl.BlockSpec((tm, FFN_TF), lambda i, j: (i, j))
    return pl.pallas_call(
        body, name=name, grid=(T // tm, nf),
        in_specs=[pl.BlockSpec((tm, Dm), lambda i, j: (i, 0), pipeline_mode=pl.Buffered(1)),
                  pl.BlockSpec((1, Dm), lambda i, j: (0, 0)), wblk, wblk, wblk],
        out_specs=[row, row, ablk, ablk],
        out_shape=[jax.ShapeDtypeStruct((T, Dm), F32), jax.ShapeDtypeStruct((T, Dm), BF16),
                   jax.ShapeDtypeStruct((T, Fh), BF16), jax.ShapeDtypeStruct((T, Fh), BF16)],
        compiler_params=_params(("parallel", "arbitrary")),
    )(h, g, wg_t, wu_t, wd)


def _ffn_bwd(dh, u, a, b, wg_t, wu_t, wd, name="ffn_bwd"):
    T, Dm = dh.shape
    Fh = wd.shape[0]
    nf = Fh // FFN_TF
    once = pl.Buffered(1)

    def body(dh_ref, u_ref, a_ref, b_ref, wg_ref, wu_ref, wd_ref, du_ref, dwg_ref, dwu_ref, dwd_ref):
        j = pl.program_id(0)

        @pl.when(j == 0)
        def _():
            du_ref[...] = jnp.zeros_like(du_ref)

        ds = _dot(dh_ref[...], wd_ref[...], NT)
        af, bf = a_ref[...].astype(F32), b_ref[...].astype(F32)
        sig = jax.nn.sigmoid(af)
        sa = af * sig
        dwd_ref[...] = _dot((sa * bf).astype(BF16), dh_ref[...], TN).astype(BF16)
        dab = jnp.concatenate([(ds * bf * (sig * (1.0 + af * (1.0 - sig)))).astype(BF16),
                               (ds * sa).astype(BF16)], axis=1)
        dw = _dot(dab, u_ref[...], TN)
        dwg_ref[...] = dw[:FFN_TF].astype(BF16)
        dwu_ref[...] = dw[FFN_TF:].astype(BF16)
        du_ref[...] += _dot(dab, jnp.concatenate([wg_ref[...], wu_ref[...]], axis=0), NN)

    full = lambda: pl.BlockSpec((T, Dm), lambda j: (0, 0), pipeline_mode=once)
    wblk = pl.BlockSpec((FFN_TF, Dm), lambda j: (j, 0))
    ablk = pl.BlockSpec((T, FFN_TF), lambda j: (0, j))
    return pl.pallas_call(
        body, name=name, grid=(nf,),
        in_specs=[full(), full(), ablk, ablk, wblk, wblk, wblk],
        out_specs=[pl.BlockSpec((T, Dm), lambda j: (0, 0)), wblk, wblk, wblk],
        out_shape=[jax.ShapeDtypeStruct((T, Dm), F32)] + [jax.ShapeDtypeStruct((Fh, Dm), BF16)] * 3,
        compiler_params=_params(("arbitrary",)),
    )(dh, u, a, b, wg_t, wu_t, wd)


def _rope(x, cos_t, sin_t, col0, ncols, out_dtype, name="rope"):
    T = x.shape[0]
    wt = cos_t.shape[1]
    tm = _pick(T, 256)
    nb = ncols * LANES // wt
    half = MLA_ROPE // 2

    def body(x_ref, c_ref, s_ref, o_ref):
        xv = x_ref[...].astype(F32)
        lane = lax.broadcasted_iota(jnp.int32, xv.shape, 1)
        first = (lane & (MLA_ROPE - 1)) < half
        swapped = jnp.where(first, pltpu.roll(xv, wt - half, 1), pltpu.roll(xv, half, 1))
        o_ref[...] = (xv * c_ref[...] + swapped * s_ref[...]).astype(out_dtype)

    off = col0 * LANES // wt
    return pl.pallas_call(
        body, name=name, grid=(T // tm, nb),
        in_specs=[pl.BlockSpec((tm, wt), lambda i, j: (i, j + off)),
                  pl.BlockSpec((tm, wt), lambda i, j: (i, 0)),
                  pl.BlockSpec((tm, wt), lambda i, j: (i, 0))],
        out_specs=pl.BlockSpec((tm, wt), lambda i, j: (i, j)),
        out_shape=jax.ShapeDtypeStruct((T, ncols * LANES), out_dtype),
        compiler_params=_params(("parallel", "parallel")),
    )(x, cos_t, sin_t)


ATT_TQ = 512
ATT_TK = 256


def _mla_masks(shape):
    lane = lax.broadcasted_iota(jnp.int32, shape, 1)
    m0 = (lane < HEAD) | ((lane >= LANES) & (lane < LANES + MLA_ROPE))
    m1 = ((lane >= HEAD) & (lane < LANES)) | ((lane >= LANES + MLA_ROPE) & (lane < LANES + 2 * MLA_ROPE))
    return m0, m1


def _by_twos(n, step, carry):
    carry = lax.fori_loop(0, n // 2, lambda i, c: step(2 * i + 1, step(2 * i, c)), carry)
    return lax.fori_loop(0, n % 2, lambda _, c: step(n - 1, c), carry)


def _chunk_ok(tq, tk, d):
    row = lax.broadcasted_iota(jnp.int32, (tq, tk), 0)
    col = lax.broadcasted_iota(jnp.int32, (tq, tk), 1) + d * tk
    return jnp.concatenate([(col >> CHUNK_BITS) <= (row >> CHUNK_BITS)] * 2, axis=0)


def _rotate(x, cos_t, sin_t):
    half = MLA_ROPE // 2
    lane = lax.broadcasted_iota(jnp.int32, x.shape, 1)
    first = (lane & (MLA_ROPE - 1)) < half
    swapped = jnp.where(first, pltpu.roll(x, x.shape[1] - half, 1), pltpu.roll(x, half, 1))
    return x * cos_t + swapped * sin_t


def _mla_fwd(q, cos_q, sin_q, kv, kr, name="mla_fwd"):
    T = q.shape[0]
    tq, tk = _pick(T, ATT_TQ), _pick(T, ATT_TK)
    nd = tq // tk
    npair = MLA_HEADS // 2
    scale = (MLA_NOPE + MLA_ROPE) ** -0.5

    def body(q_ref, c_ref, s_ref, kn_ref, v_ref, kr_ref, o_ref, lse_ref):
        m_idx = pl.program_id(1)
        qv = _rotate(q_ref[...], c_ref[...], s_ref[...]).astype(BF16)
        m0, m1 = _mla_masks(qv.shape)
        qs = jnp.concatenate([jnp.where(m0, qv, 0), jnp.where(m1, qv, 0)], axis=0).astype(BF16)

        def block(kb, carry, ok):
            ks = pl.ds(pl.multiple_of(kb * tk, tk), tk)
            kcat = jnp.concatenate([kn_ref[ks, :], kr_ref[ks, :]], axis=1)
            mx, l, acc = carry
            s = _dot(qs, kcat, NT) * scale
            if ok is not None:
                s = jnp.where(ok, s, NEG)
            mn = jnp.maximum(mx, jnp.max(s, axis=-1, keepdims=True))
            alpha = jnp.exp(mx - mn)
            p = jnp.exp(s - mn)
            return (mn, alpha * l + jnp.sum(p, axis=-1, keepdims=True),
                    alpha * acc + _dot(p.astype(BF16), v_ref[ks, :], NN))

        init = (jnp.full((2 * tq, 1), NEG, F32), jnp.zeros((2 * tq, 1), F32), jnp.zeros((2 * tq, LANES), F32))
        res = init
        for d in range(nd):
            res = block(m_idx * nd + d, res, _chunk_ok(tq, tk, d))
        mx, l, acc = _by_twos(m_idx * nd, lambda kb, c: block(kb, c, None), res)
        h0 = lax.broadcasted_iota(jnp.int32, (tq, LANES), 1) < HEAD
        o_ref[...] = _two_heads(acc * (1.0 / l), h0).astype(o_ref.dtype)
        lse_ref[...] = _two_heads(jnp.broadcast_to(mx + jnp.log(l), (2 * tq, LANES)), h0)

    full = lambda col: pl.BlockSpec((T, LANES), col)
    table = pl.BlockSpec((tq, 2 * LANES), lambda p, m: (m, 0))
    return pl.pallas_call(
        body, name=name, grid=(npair, T // tq),
        in_specs=[pl.BlockSpec((tq, 2 * LANES), lambda p, m: (m, p)), table, table,
                  full(lambda p, m: (0, p)), full(lambda p, m: (0, npair + p)), full(lambda p, m: (0, 0))],
        out_specs=[pl.BlockSpec((tq, LANES), lambda p, m: (m, p)),
                   pl.BlockSpec((tq, LANES), lambda p, m: (m, p))],
        out_shape=[jax.ShapeDtypeStruct((T, npair * LANES), BF16),
                   jax.ShapeDtypeStruct((T, npair * LANES), F32)],
        compiler_params=_params(("parallel", "arbitrary")),
    )(q, cos_q, sin_q, kv, kv, kr)


def _mla_bwd(q, cos_q, sin_q, kv, kr, o, lse, do, do_col0, name="mla_bwd"):
    T = q.shape[0]
    tq, tk = _pick(T, ATT_TQ), _pick(T, ATT_TK)
    nd = tq // tk
    npair = MLA_HEADS // 2
    scale = (MLA_NOPE + MLA_ROPE) ** -0.5

    def body(q_ref, c_ref, s_ref, kn_ref, v_ref, kr_ref, o_ref, lse_ref, do_ref, dq_ref, dkn_ref, dv_ref, dkr_ref,
             dkn_acc, dv_acc):
        p_idx, m_idx = pl.program_id(0), pl.program_id(1)

        @pl.when(m_idx == 0)
        def _():
            dkn_acc[...] = jnp.zeros_like(dkn_acc)
            dv_acc[...] = jnp.zeros_like(dv_acc)

        @pl.when((m_idx == 0) & (p_idx == 0))
        def _():
            dkr_ref[...] = jnp.zeros_like(dkr_ref)

        qv = _rotate(q_ref[...], c_ref[...], s_ref[...]).astype(BF16)
        m0, m1 = _mla_masks(qv.shape)
        qs = jnp.concatenate([jnp.where(m0, qv, 0), jnp.where(m1, qv, 0)], axis=0).astype(BF16)
        dov = do_ref[...].astype(F32)
        h0 = lax.broadcasted_iota(jnp.int32, (tq, LANES), 1) < HEAD
        dos32 = jnp.concatenate([jnp.where(h0, dov, 0.0), jnp.where(h0, 0.0, dov)], axis=0)
        ov = o_ref[...].astype(F32)
        delta = jnp.sum(dos32 * jnp.concatenate([ov, ov], axis=0), axis=-1, keepdims=True)
        dos = dos32.astype(BF16)
        lsev = lse_ref[...]
        lse = jnp.concatenate([lsev[:, 0:1], lsev[:, HEAD:HEAD + 1]], axis=0)

        def block(kb, dq, ok):
            ks = pl.ds(pl.multiple_of(kb * tk, tk), tk)
            kcat = jnp.concatenate([kn_ref[ks, :], kr_ref[ks, :]], axis=1)
            vv = v_ref[ks, :]
            p = jnp.exp(_dot(qs, kcat, NT) * scale - lse)
            if ok is not None:
                p = jnp.where(ok, p, 0.0)
            ds = (p * (_dot(dos, vv, NT) - delta) * scale).astype(BF16)
            dkc = _dot(ds, qs, TN)
            dkn_acc[ks, :] += dkc[:, :LANES]
            dkr_ref[ks, :] += dkc[:, LANES:]
            dv_acc[ks, :] += _dot(p.astype(BF16), dos, TN)
            return dq + _dot(ds, kcat, NN)

        dq = jnp.zeros((2 * tq, 2 * LANES), F32)
        for d in range(nd):
            dq = block(m_idx * nd + d, dq, _chunk_ok(tq, tk, d))
        dq = _by_twos(m_idx * nd, lambda kb, c: block(kb, c, None), dq)
        dq_ref[...] = _rotate(jnp.where(m0, dq[:tq], jnp.where(m1, dq[tq:], 0.0)), c_ref[...],
                              -s_ref[...]).astype(BF16)

        @pl.when(m_idx == T // tq - 1)
        def _():
            dkn_ref[...] = dkn_acc[...].astype(BF16)
            dv_ref[...] = dv_acc[...].astype(BF16)

    full = lambda col: pl.BlockSpec((T, LANES), col)
    blk = lambda col: pl.BlockSpec((tq, LANES), col)
    table = pl.BlockSpec((tq, 2 * LANES), lambda p, m: (m, 0))
    return pl.pallas_call(
        body, name=name, grid=(npair, T // tq),
        in_specs=[pl.BlockSpec((tq, 2 * LANES), lambda p, m: (m, p)), table, table,
                  full(lambda p, m: (0, p)), full(lambda p, m: (0, npair + p)), full(lambda p, m: (0, 0)),
                  blk(lambda p, m: (m, p)), blk(lambda p, m: (m, p)),
                  blk(lambda p, m: (m, do_col0 + p))],
        out_specs=[pl.BlockSpec((tq, 2 * LANES), lambda p, m: (m, p)),
                   full(lambda p, m: (0, p)), full(lambda p, m: (0, p)), full(lambda p, m: (0, 0))],
        out_shape=[jax.ShapeDtypeStruct((T, npair * 2 * LANES), BF16),
                   jax.ShapeDtypeStruct((T, npair * LANES), BF16),
                   jax.ShapeDtypeStruct((T, npair * LANES), BF16),
                   jax.ShapeDtypeStruct((T, LANES), F32)],
        scratch_shapes=[pltpu.VMEM((T, LANES), F32)] * 2,
        compiler_params=_params(("arbitrary", "arbitrary")),
    )(q, cos_q, sin_q, kv, kv, kr, o, lse, do)


def _split_dot(x, tri):
    hi = x.astype(BF16)
    lo = (x - hi.astype(F32)).astype(BF16)
    both = _dot(jnp.concatenate([hi, lo], axis=0), tri, NN)
    return both[:x.shape[0]] + both[x.shape[0]:]


def _sb_terms(qh, kk, before):
    z = _dot(qh, kk, NT)
    sp = jnp.maximum(z, 0.0) + jnp.log(1.0 + jnp.exp(-jnp.abs(z)))
    lk = -sp if before is None else jnp.where(before, -sp, 0.0)
    return z, sp, lk


def _sb_setup(q_ref, tq, tk, scale):
    qv = (q_ref[...].astype(F32) * scale).astype(BF16)
    lane = lax.broadcasted_iota(jnp.int32, (tq, LANES), 1)
    h0 = lane < HEAD
    qs = jnp.concatenate([jnp.where(h0, qv, 0), jnp.where(h0, 0, qv)], axis=0).astype(BF16)
    row = lax.broadcasted_iota(jnp.int32, (tk, tk), 0)
    col = lax.broadcasted_iota(jnp.int32, (tk, tk), 1)
    return qs, h0, row, col


def _sb_before(tq, tk, d):
    row = lax.broadcasted_iota(jnp.int32, (tq, tk), 0)
    col = lax.broadcasted_iota(jnp.int32, (tq, tk), 1) + d * tk
    return jnp.concatenate([col < row] * 2, axis=0)


def _two_heads(x, h0):
    tq = x.shape[0] // 2
    return jnp.where(h0, x[:tq], x[tq:])


def _sb_fwd(qkv, col0, dep, name="sb_fwd"):
    T = qkv.shape[0]
    tq, tk = _pick(T, ATT_TQ), _pick(T, ATT_TK)
    nd = tq // tk
    npair = SB_HEADS // 2
    scale = SB_DIM ** -0.5

    def body(q_ref, k_ref, v_ref, dep_ref, o_ref, o32_ref, w_ref, sp_ref):
        m_idx = pl.program_id(1)
        qs, h0, row, col = _sb_setup(q_ref, tq, tk, scale)
        later = (row > col).astype(BF16)

        def block(kb, carry, before):
            ks = pl.ds(pl.multiple_of(kb * tk, tk), tk)
            c, acc = carry
            z, sp, lk = _sb_terms(qs, k_ref[ks, :].astype(BF16), before)
            w = jnp.exp((z - sp) + _split_dot(lk, later) + c)
            if before is not None:
                w = jnp.where(before, w, 0.0)
            wb = w.astype(BF16)
            w_ref[0, 0, kb] = wb
            sp_ref[0, 0, kb] = sp.astype(BF16)
            return (c + jnp.sum(lk, axis=-1, keepdims=True), acc + _dot(wb, v_ref[ks, :].astype(BF16), NN))

        init = (jnp.zeros((2 * tq, 1), F32), jnp.zeros((2 * tq, LANES), F32))
        res = init
        for d in reversed(range(nd)):
            res = block(m_idx * nd + d, res, _sb_before(tq, tk, d))
        res = _by_twos(m_idx * nd, lambda i, c: block(m_idx * nd - 1 - i, c, None), res)
        o = _two_heads(res[1], h0)
        o_ref[...] = o.astype(o_ref.dtype)
        o32_ref[...] = o

    full = lambda col: pl.BlockSpec((T, LANES), col)
    blk = pl.BlockSpec((tq, LANES), lambda p, m: (m, p))
    return pl.pallas_call(
        body, name=name, grid=(npair, T // tq),
        in_specs=[pl.BlockSpec((tq, LANES), lambda p, m: (m, col0 + p)),
                  full(lambda p, m: (0, col0 + npair + p)), full(lambda p, m: (0, col0 + 2 * npair + p)),
                  pl.BlockSpec((8, LANES), lambda p, m: (0, 0))],
        out_specs=[blk, blk] + [pl.BlockSpec((1, 1, T // tk, 2 * tq, tk), lambda p, m: (p, m, 0, 0, 0))] * 2,
        out_shape=[jax.ShapeDtypeStruct((T, npair * LANES), BF16), jax.ShapeDtypeStruct((T, npair * LANES), F32)]
        + [jax.ShapeDtypeStruct((npair, T // tq, T // tk, 2 * tq, tk), BF16)] * 2,
        compiler_params=_params(("parallel", "arbitrary")),
    )(qkv, qkv, qkv, dep)


def _sb_bwd(qkv, col0, o32, w_all, sp_all, do, do_col0, dep, name="sb_bwd"):
    T = qkv.shape[0]
    tq, tk = _pick(T, ATT_TQ), _pick(T, ATT_TK)
    nd = tq // tk
    npair = SB_HEADS // 2
    scale = SB_DIM ** -0.5

    def body(q_ref, k_ref, v_ref, o_ref, w_ref, sp_ref, do_ref, dep_ref, dq_ref, dk_ref, dv_ref, dk_acc, dv_acc):
        m_idx = pl.program_id(1)

        @pl.when(m_idx == 0)
        def _():
            dk_acc[...] = jnp.zeros_like(dk_acc)
            dv_acc[...] = jnp.zeros_like(dv_acc)

        qs, h0, row, col = _sb_setup(q_ref, tq, tk, scale)
        dov = do_ref[...].astype(F32)
        dos = jnp.concatenate([jnp.where(h0, dov, 0.0), jnp.where(h0, 0.0, dov)], axis=0).astype(BF16)
        ov = o_ref[...]
        etot = jnp.sum(dos.astype(F32) * jnp.concatenate([ov, ov], axis=0), axis=-1, keepdims=True)
        from_here = (row >= col).astype(BF16)

        def block(kb, carry, before):
            ks = pl.ds(pl.multiple_of(kb * tk, tk), tk)
            kk = k_ref[ks, :].astype(BF16)
            vv = v_ref[ks, :].astype(BF16)
            es, dqa = carry
            wb = w_ref[0, 0, kb]
            e = wb.astype(F32) * _dot(dos, vv, NT)
            prev = etot - (_split_dot(e, from_here) + es)
            sig_neg = jnp.exp(-sp_ref[0, 0, kb].astype(F32))
            dz = e * sig_neg - (1.0 - sig_neg) * prev
            if before is not None:
                dz = jnp.where(before, dz, 0.0)
            dzb = dz.astype(BF16)
            dk_acc[ks, :] += _dot(dzb, qs, TN)
            dv_acc[ks, :] += _dot(wb, dos, TN)
            return es + jnp.sum(e, axis=-1, keepdims=True), dqa + _dot(dzb, kk, NN)

        init = (jnp.zeros((2 * tq, 1), F32), jnp.zeros((2 * tq, LANES), F32))
        res = init
        for d in reversed(range(nd)):
            res = block(m_idx * nd + d, res, _sb_before(tq, tk, d))
        res = _by_twos(m_idx * nd, lambda i, c: block(m_idx * nd - 1 - i, c, None), res)
        dq_ref[...] = (_two_heads(res[1], h0) * scale).astype(BF16)

        @pl.when(m_idx == T // tq - 1)
        def _():
            dk_ref[...] = dk_acc[...].astype(BF16)
            dv_ref[...] = dv_acc[...].astype(BF16)

    full = lambda col: pl.BlockSpec((T, LANES), col)
    blk = lambda col: pl.BlockSpec((tq, LANES), col)
    return pl.pallas_call(
        body, name=name, grid=(npair, T // tq),
        in_specs=[blk(lambda p, m: (m, col0 + p)),
                  full(lambda p, m: (0, col0 + npair + p)), full(lambda p, m: (0, col0 + 2 * npair + p)),
                  blk(lambda p, m: (m, p)),
                  pl.BlockSpec((1, 1, T // tk, 2 * tq, tk), lambda p, m: (p, m, 0, 0, 0)),
                  pl.BlockSpec((1, 1, T // tk, 2 * tq, tk), lambda p, m: (p, m, 0, 0, 0)),
                  blk(lambda p, m: (m, do_col0 + p)), pl.BlockSpec((8, LANES), lambda p, m: (0, 0))],
        out_specs=[blk(lambda p, m: (m, p)), full(lambda p, m: (0, p)), full(lambda p, m: (0, p))],
        out_shape=[jax.ShapeDtypeStruct((T, npair * LANES), BF16)] * 3,
        scratch_shapes=[pltpu.VMEM((T, LANES), F32)] * 2,
        compiler_params=_params(("arbitrary", "arbitrary")),
    )(qkv, qkv, qkv, o32, w_all, sp_all, do, dep)


def _band_in_window():
    cq = lax.broadcasted_iota(jnp.int32, (BAND_TQ, BAND_W), 0) >> CHUNK_BITS
    ckp = lax.broadcasted_iota(jnp.int32, (BAND_TQ, BAND_W), 1) >> CHUNK_BITS
    return (ckp >= cq) & (ckp <= cq + LEFT_CHUNKS)


def _band_real(m_idx):
    j = lax.broadcasted_iota(jnp.int32, (BAND_TQ, BAND_W), 1)
    return j >= PAD_KEYS - m_idx * BAND_TQ


def _band_probs(qh, kw, bias, real, scale):
    s = jnp.where(real, _dot(qh, kw, NT) * scale + bias, NEG)
    e = jnp.exp(s - jnp.max(s, axis=-1, keepdims=True))
    return e * (1.0 / jnp.sum(e, axis=-1, keepdims=True))


BAND_SUB = 4


def _band_fwd(qkv, k_pad, v_pad, bias_w, name="band_fwd"):
    T = qkv.shape[0]
    npair = C_HEADS // 2
    scale = C_DIM ** -0.5
    rows = BAND_SUB * BAND_TQ

    def body(q_ref, k_ref, v_ref, b_ref, o_ref, p_ref):
        lane = lax.broadcasted_iota(jnp.int32, (BAND_TQ, LANES), 1)
        h0 = lane < HEAD
        bias = jnp.concatenate([b_ref[0], b_ref[1]], axis=0)
        for sub in range(BAND_SUB):
            m_idx = pl.program_id(1) * BAND_SUB + sub
            win = pl.ds(pl.multiple_of(m_idx * BAND_TQ, BAND_TQ), BAND_W)
            kw, vw = k_ref[win, :], v_ref[win, :]
            qv = q_ref[sub * BAND_TQ:(sub + 1) * BAND_TQ, :]
            qs = jnp.concatenate([jnp.where(h0, qv, 0), jnp.where(h0, 0, qv)], axis=0).astype(BF16)
            p = _band_probs(qs, kw, bias, jnp.concatenate([_band_real(m_idx)] * 2, axis=0), scale).astype(BF16)
            p_ref[0, sub] = p
            o = _two_heads(_dot(p, vw, NN), h0)
            o_ref[sub * BAND_TQ:(sub + 1) * BAND_TQ, :] = o.astype(o_ref.dtype)

    Tp = T + PAD_KEYS
    return pl.pallas_call(
        body, name=name, grid=(npair, T // rows),
        in_specs=[pl.BlockSpec((rows, LANES), lambda p, m: (m, p)),
                  pl.BlockSpec((Tp, LANES), lambda p, m: (0, p)),
                  pl.BlockSpec((Tp, LANES), lambda p, m: (0, p)),
                  pl.BlockSpec((2, BAND_TQ, BAND_W), lambda p, m: (p, 0, 0))],
        out_specs=[pl.BlockSpec((rows, LANES), lambda p, m: (m, p)),
                   pl.BlockSpec((1, BAND_SUB, 2 * BAND_TQ, BAND_W), lambda p, m: (p, m, 0, 0))],
        out_shape=[jax.ShapeDtypeStruct((T, npair * LANES), BF16),
                   jax.ShapeDtypeStruct((npair, T // BAND_TQ, 2 * BAND_TQ, BAND_W), BF16)],
        compiler_params=_params(("parallel", "arbitrary")),
    )(qkv, k_pad, v_pad, bias_w)


def _band_bwd(qkv, k_pad, v_pad, probs, do, name="band_bwd"):
    T = qkv.shape[0]
    npair = C_HEADS // 2
    scale = C_DIM ** -0.5

    rows = BAND_SUB * BAND_TQ

    def body(q_ref, k_ref, v_ref, p_ref, do_ref, dq_ref, dk_ref, dv_ref, db_ref, dk_acc, dv_acc):
        @pl.when(pl.program_id(1) == 0)
        def _():
            dk_acc[...] = jnp.zeros_like(dk_acc)
            dv_acc[...] = jnp.zeros_like(dv_acc)
            db_ref[...] = jnp.zeros_like(db_ref)

        lane = lax.broadcasted_iota(jnp.int32, (BAND_TQ, LANES), 1)
        h0 = lane < HEAD
        dbs = jnp.zeros((2 * BAND_TQ, BAND_W), F32)
        for sub in range(BAND_SUB):
            m_idx = pl.program_id(1) * BAND_SUB + sub
            win = pl.ds(pl.multiple_of(m_idx * BAND_TQ, BAND_TQ), BAND_W)
            kw, vw = k_ref[win, :], v_ref[win, :]
            qv = q_ref[sub * BAND_TQ:(sub + 1) * BAND_TQ, :]
            dov = do_ref[sub * BAND_TQ:(sub + 1) * BAND_TQ, :].astype(F32)
            qs = jnp.concatenate([jnp.where(h0, qv, 0), jnp.where(h0, 0, qv)], axis=0).astype(BF16)
            dos = jnp.concatenate([jnp.where(h0, dov, 0.0), jnp.where(h0, 0.0, dov)], axis=0).astype(BF16)
            pb = p_ref[0, sub]
            p = pb.astype(F32)
            dp = _dot(dos, vw, NT)
            dsb = p * (dp - jnp.sum(p * dp, axis=-1, keepdims=True))
            dbs = dbs + dsb
            dsq = (dsb * scale).astype(BF16)
            dq_ref[sub * BAND_TQ:(sub + 1) * BAND_TQ, :] = _two_heads(_dot(dsq, kw, NN), h0).astype(BF16)
            dk_acc[win, :] += _dot(dsq, qs, TN)
            dv_acc[win, :] += _dot(pb, dos, TN)
        db_ref[0] += dbs[:BAND_TQ]
        db_ref[1] += dbs[BAND_TQ:]

        @pl.when(pl.program_id(1) == T // rows - 1)
        def _():
            dk_ref[...] = dk_acc[...].astype(BF16)
            dv_ref[...] = dv_acc[...].astype(BF16)

    Tp = T + PAD_KEYS
    blk = lambda col: pl.BlockSpec((rows, LANES), col)
    full = pl.BlockSpec((Tp, LANES), lambda p, m: (0, p))
    bias = pl.BlockSpec((2, BAND_TQ, BAND_W), lambda p, m: (p, 0, 0))
    prob = pl.BlockSpec((1, BAND_SUB, 2 * BAND_TQ, BAND_W), lambda p, m: (p, m, 0, 0))
    return pl.pallas_call(
        body, name=name, grid=(npair, T // rows),
        in_specs=[blk(lambda p, m: (m, p)), full, full, prob, blk(lambda p, m: (m, p))],
        out_specs=[blk(lambda p, m: (m, p)), full, full, bias],
        out_shape=[jax.ShapeDtypeStruct((T, npair * LANES), BF16),
                   jax.ShapeDtypeStruct((Tp, npair * LANES), BF16),
                   jax.ShapeDtypeStruct((Tp, npair * LANES), BF16),
                   jax.ShapeDtypeStruct((C_HEADS, BAND_TQ, BAND_W), F32)],
        scratch_shapes=[pltpu.VMEM((Tp, LANES), F32)] * 2,
        compiler_params=_params(("arbitrary", "arbitrary")),
    )(qkv, k_pad, v_pad, probs, do)


def _skew_bits(x, left):
    w = x.shape[1]
    row = lax.broadcasted_iota(jnp.int32, x.shape, 0)
    for b in range(BAND_TQ.bit_length() - 1):
        amt = (w - (1 << b)) if left else (1 << b)
        x = jnp.where((row >> b) & 1 == 1, pltpu.roll(x, amt, 1), x)
    return x


def _toeplitz(diag, name="toeplitz"):
    H = diag.shape[0]

    def body(d_ref, o_ref):
        x = jnp.broadcast_to(d_ref[0], (BAND_TQ, TOEP_W))
        o_ref[0] = jnp.where(_band_in_window(), _skew_bits(x, left=False)[:, BAND_TQ:], NEG)

    return pl.pallas_call(
        body, name=name, grid=(H,),
        in_specs=[pl.BlockSpec((1, 1, TOEP_W), lambda h: (h, 0, 0))],
        out_specs=pl.BlockSpec((1, BAND_TQ, BAND_W), lambda h: (h, 0, 0)),
        out_shape=jax.ShapeDtypeStruct((H, BAND_TQ, BAND_W), F32),
        compiler_params=_params(("parallel",)),
    )(diag.reshape(H, 1, TOEP_W))


def _toeplitz_bwd(dbias, name="toeplitz_bwd"):
    H = dbias.shape[0]

    def body(d_ref, o_ref):
        x = jnp.concatenate([jnp.zeros((BAND_TQ, BAND_TQ), F32), d_ref[0]], axis=1)
        h = BAND_TQ // 2
        while h >= 8:
            x = x[:h] + pltpu.roll(x[h:2 * h], TOEP_W - h, 1)
            h //= 2
        o_ref[0] = jnp.sum(_skew_bits(x, left=True), axis=0, keepdims=True)

    return pl.pallas_call(
        body, name=name, grid=(H,),
        in_specs=[pl.BlockSpec((1, BAND_TQ, BAND_W), lambda h: (h, 0, 0))],
        out_specs=pl.BlockSpec((1, 1, TOEP_W), lambda h: (h, 0, 0)),
        out_shape=jax.ShapeDtypeStruct((H, 1, TOEP_W), F32),
        compiler_params=_params(("parallel",)),
    )(dbias).reshape(H, TOEP_W)


_HBM = pl.BlockSpec(memory_space=pltpu.HBM)
_SEM = pl.BlockSpec(memory_space=pltpu.SEMAPHORE)
_EFFECT = pltpu.SideEffectType.DATAFLOW_SIDE_EFFECTING


def _peers():
    x, y, c = lax.axis_index("x"), lax.axis_index("y"), lax.axis_index("c")
    out = []
    for k in range(1, N_DEV):
        peer = (1 - x if (k >> 2) & 1 else x, 1 - y if (k >> 1) & 1 else y, 1 - c if k & 1 else c)
        out.append((peer, 4 * peer[0] + 2 * peer[1] + peer[2]))
    return 4 * x + 2 * y + c, out


def _split_copies(ins, lands, scatter, send_sem, recv_sem, arriving):
    me, peers = _peers()
    out = []
    for a in range(len(ins)):
        for peer, idx in peers:
            out.append(pltpu.make_async_remote_copy(
                src_ref=ins[a].at[idx] if scatter[a] else ins[a],
                dst_ref=lands[a].at[idx if arriving else me], send_sem=send_sem, recv_sem=recv_sem,
                device_id=peer, device_id_type=pl.DeviceIdType.MESH))
    return out


def _landing_zones(arrays, scatter):
    return [lax.empty((N_DEV,) + (a.shape[1:] if s else a.shape), a.dtype) for a, s in zip(arrays, scatter)]


def _place_own(arrays, scatter, name):
    n = len(arrays)
    lands = _landing_zones(arrays, scatter)
    me = (4 * lax.axis_index("x") + 2 * lax.axis_index("y") + lax.axis_index("c")).astype(jnp.int32).reshape(1)

    def body(me_ref, *refs):
        for a in range(n):
            refs[2 * n + a][...] = refs[a][...].reshape(refs[2 * n + a].shape)

    def row_spec(shape):
        zeros = (0,) * (len(shape) - 1)
        return pl.BlockSpec((1,) + tuple(shape[1:]), lambda i, me_ref: (me_ref[0],) + zeros)

    in_specs = [row_spec(a.shape) if s else pl.BlockSpec(a.shape, lambda i, me_ref, nd=a.ndim: (0,) * nd)
                for a, s in zip(arrays, scatter)]
    return pl.pallas_call(
        body, name=name,
        out_shape=[jax.ShapeDtypeStruct(l.shape, l.dtype) for l in lands],
        grid_spec=pltpu.PrefetchScalarGridSpec(
            num_scalar_prefetch=1, grid=(1,),
            in_specs=in_specs + [pl.BlockSpec(memory_space=pl.ANY)] * n,
            out_specs=[row_spec(l.shape) for l in lands]),
        input_output_aliases={1 + n + i: i for i in range(n)},
        compiler_params=_params(("arbitrary",)),
    )(me, *arrays, *lands)


def _exchange_start_groups(groups, scatter, after, name, lands=None):
    sizes = [len(g) for g in groups]
    arrays = [a for g in groups for a in g]
    n, ng = len(arrays), len(groups)
    flags = [scatter] * n
    if lands is None:
        lands = list(_place_own(arrays, flags, name=name.replace("_start_", "_own_")))
    else:
        lands = [l for g in lands for l in g]
    starts = np.cumsum([0] + sizes)

    def body(*refs):
        ins, lnd = refs[:n], refs[n:2 * n]
        sems = refs[2 * n + 1:2 * n + 1 + 2 * ng]
        token = refs[-1]
        for g in range(ng):
            sl = slice(starts[g], starts[g + 1])
            for cp in _split_copies(ins[sl], lnd[sl], flags[sl], sems[2 * g], sems[2 * g + 1], arriving=False):
                cp.start()
        token[...] = jnp.zeros_like(token)

    hbm = lambda a: pltpu.HBM(a.shape, a.dtype)
    out = pl.pallas_call(
        body, name=name,
        out_shape=(*[pltpu.SemaphoreType.DMA(())] * (2 * ng),
                   *[hbm(a) for a in arrays], *[hbm(a) for a in lands],
                   jax.ShapeDtypeStruct((8, LANES), F32)),
        in_specs=[_HBM] * (2 * n) + [pl.BlockSpec(memory_space=pl.ANY)],
        out_specs=(*[_SEM] * (2 * ng), *([_HBM] * (2 * n)), pl.BlockSpec(memory_space=pltpu.VMEM)),
        input_output_aliases={i: 2 * ng + i for i in range(2 * n)},
        compiler_params=pltpu.CompilerParams(has_side_effects=_EFFECT),
    )(*[pltpu.with_memory_space_constraint(a, pltpu.HBM) for a in list(arrays) + lands], after)
    ins_out, lands_out = out[2 * ng:2 * ng + n], out[2 * ng + n:2 * ng + 2 * n]
    handles = [(out[2 * g], out[2 * g + 1], list(ins_out[starts[g]:starts[g + 1]]),
                list(lands_out[starts[g]:starts[g + 1]]), tuple(flags[starts[g]:starts[g + 1]]))
               for g in range(ng)]
    return handles, out[-1]


def _exchange_start(arrays, scatter, after, name):
    assert len(set(scatter)) == 1
    handles, token = _exchange_start_groups([list(arrays)], scatter[0], after, name)
    return handles[0], token


def _exchange_wait(handle, after, name):
    send_sem, recv_sem, ins, lands, scatter = handle
    n = len(ins)
    after = after if isinstance(after, tuple) else (after,)

    def body(*refs):
        i_ref, l_ref = refs[:n], refs[n:2 * n]
        s_sem, r_sem = refs[2 * n:2 * n + 2]
        for cp in _split_copies(i_ref, l_ref, scatter, s_sem, r_sem, arriving=False):
            cp.wait_send()
        for cp in _split_copies(i_ref, l_ref, scatter, s_sem, r_sem, arriving=True):
            cp.wait_recv()

    hbm = lambda a: pltpu.HBM(a.shape, a.dtype)
    out = pl.pallas_call(
        body, name=name,
        out_shape=tuple(hbm(a) for a in ins + lands),
        in_specs=[_HBM] * (2 * n) + [_SEM, _SEM] + [pl.BlockSpec(memory_space=pl.ANY)] * len(after),
        out_specs=tuple([_HBM] * (2 * n)),
        input_output_aliases={i: i for i in range(2 * n)},
        compiler_params=pltpu.CompilerParams(has_side_effects=_EFFECT),
    )(*ins, *lands, send_sem, recv_sem, *after)
    return list(out[n:])


_SIBLING = 1
_CHIPS = (4, 2, 6)


def _peer_of(k):
    x, y, c = lax.axis_index("x"), lax.axis_index("y"), lax.axis_index("c")
    peer = (1 - x if (k >> 2) & 1 else x, 1 - y if (k >> 1) & 1 else y, 1 - c if k & 1 else c)
    return peer, 4 * peer[0] + 2 * peer[1] + peer[2]


def _rcopy(src, dst, send_sem, recv_sem, k):
    return pltpu.make_async_remote_copy(src_ref=src, dst_ref=dst, send_sem=send_sem, recv_sem=recv_sem,
                                        device_id=_peer_of(k)[0], device_id_type=pl.DeviceIdType.MESH)


def _gather2_start(groups, lands, after, name):
    sizes = [len(g) for g in groups]
    arrays = [a for g in groups for a in g]
    lands = [l for g in lands for l in g]
    n, ng = len(arrays), len(groups)
    starts = np.cumsum([0] + sizes)

    def body(*refs):
        ins, lnd = refs[:n], refs[n:2 * n]
        sems = refs[2 * n + 1:2 * n + 1 + 4 * ng]
        me, _ = _peers()
        for g in range(ng):
            send_d, recv_d, send_i, recv_i = sems[4 * g:4 * g + 4]
            for a in range(starts[g], starts[g + 1]):
                for k in _CHIPS:
                    _rcopy(ins[a], lnd[a].at[me], send_i, recv_i, k).start()
                _rcopy(ins[a], lnd[a].at[me], send_d, recv_d, _SIBLING).start()
        refs[-1][...] = jnp.zeros_like(refs[-1])

    hbm = lambda a: pltpu.HBM(a.shape, a.dtype)
    out = pl.pallas_call(
        body, name=name,
        out_shape=(*[pltpu.SemaphoreType.DMA(())] * (4 * ng), *[hbm(a) for a in arrays], *[hbm(a) for a in lands],
                   jax.ShapeDtypeStruct((8, LANES), F32)),
        in_specs=[_HBM] * (2 * n) + [pl.BlockSpec(memory_space=pl.ANY)],
        out_specs=(*[_SEM] * (4 * ng), *([_HBM] * (2 * n)), pl.BlockSpec(memory_space=pltpu.VMEM)),
        input_output_aliases={i: 4 * ng + i for i in range(2 * n)},
        compiler_params=pltpu.CompilerParams(has_side_effects=_EFFECT),
    )(*[pltpu.with_memory_space_constraint(a, pltpu.HBM) for a in arrays + lands], after)
    ins_out, lands_out = out[4 * ng:4 * ng + n], out[4 * ng + n:4 * ng + 2 * n]
    handles = [dict(sems=out[4 * g:4 * g + 4], ins=list(ins_out[starts[g]:starts[g + 1]]),
                    lands=list(lands_out[starts[g]:starts[g + 1]])) for g in range(ng)]
    return handles, out[-1]


def _gather2_pass_on(handle, after, name):
    lands, recv_i = handle["lands"], handle["sems"][3]
    n = len(lands)
    after = after if isinstance(after, tuple) else (after,)

    def body(*refs):
        lnd, r_i = refs[:n], refs[n]
        send_f, recv_f = refs[n + 1 + len(after):n + 3 + len(after)]
        for a in range(n):
            for k in _CHIPS:
                row = _peer_of(k)[1]
                _rcopy(lnd[a].at[row], lnd[a].at[row], send_f, r_i, k).wait_recv()
        for a in range(n):
            for k in _CHIPS:
                row = _peer_of(k)[1]
                _rcopy(lnd[a].at[row], lnd[a].at[row], send_f, recv_f, _SIBLING).start()
        refs[-1][...] = jnp.zeros_like(refs[-1])

    hbm = lambda a: pltpu.HBM(a.shape, a.dtype)
    out = pl.pallas_call(
        body, name=name,
        out_shape=(pltpu.SemaphoreType.DMA(()), pltpu.SemaphoreType.DMA(()), *[hbm(a) for a in lands],
                   jax.ShapeDtypeStruct((8, LANES), F32)),
        in_specs=[_HBM] * n + [_SEM] + [pl.BlockSpec(memory_space=pl.ANY)] * len(after),
        out_specs=(_SEM, _SEM, *([_HBM] * n), pl.BlockSpec(memory_space=pltpu.VMEM)),
        input_output_aliases={i: 2 + i for i in range(n)},
        compiler_params=pltpu.CompilerParams(has_side_effects=_EFFECT),
    )(*lands, recv_i, *after)
    return dict(handle, lands=list(out[2:2 + n]), passed=(out[0], out[1])), out[-1]


def _gather2_wait(handle, after, name):
    ins, lands = handle["ins"], handle["lands"]
    send_d, recv_d, send_i, _ = handle["sems"]
    send_f, recv_f = handle["passed"]
    n = len(ins)
    after = after if isinstance(after, tuple) else (after,)

    def body(*refs):
        i_ref, lnd = refs[:n], refs[n:2 * n]
        s_d, r_d, s_i, s_f, r_f = refs[2 * n:2 * n + 5]
        me, _ = _peers()
        sib = _peer_of(_SIBLING)[1]
        for a in range(n):
            _rcopy(i_ref[a], lnd[a].at[sib], s_d, r_d, _SIBLING).wait_send()
            _rcopy(i_ref[a], lnd[a].at[sib], s_d, r_d, _SIBLING).wait_recv()
            for k in _CHIPS:
                row = _peer_of(k)[1]
                _rcopy(i_ref[a], lnd[a].at[me], s_i, r_d, k).wait_send()
                _rcopy(lnd[a].at[row], lnd[a].at[row], s_f, r_f, _SIBLING).wait_send()
                _rcopy(lnd[a].at[row], lnd[a].at[_peer_of(k ^ _SIBLING)[1]], s_f, r_f, _SIBLING).wait_recv()

    hbm = lambda a: pltpu.HBM(a.shape, a.dtype)
    out = pl.pallas_call(
        body, name=name,
        out_shape=tuple(hbm(a) for a in ins + lands),
        in_specs=[_HBM] * (2 * n) + [_SEM] * 5 + [pl.BlockSpec(memory_space=pl.ANY)] * len(after),
        out_specs=tuple([_HBM] * (2 * n)),
        input_output_aliases={i: i for i in range(2 * n)},
        compiler_params=pltpu.CompilerParams(has_side_effects=_EFFECT),
    )(*ins, *lands, send_d, recv_d, send_i, send_f, recv_f, *after)
    return list(out[n:])


def _adamw(w, parts, m, v, name="adamw"):
    R, C = w.shape
    L = len(parts)
    rl = R // L
    tr = max([t for t in range(16, 257, 16) if rl % t == 0], default=rl)
    nb = rl // tr
    c1 = 1.0 - ADAM_B1 ** ADAM_STEP
    c2 = 1.0 - ADAM_B2 ** ADAM_STEP

    def body(*refs):
        w_ref, p_refs, (m_ref, v_ref, g_ref, d_ref, nm_ref, nv_ref) = refs[0], refs[1:1 + L], refs[1 + L:]
        g = None
        for j, p_ref in enumerate(p_refs):
            gj = p_ref[0].astype(F32)
            for i in range(1, N_DEV):
                gj = gj + p_ref[i].astype(F32)
            g = gj if g is None else jnp.where(pl.program_id(0) == j, gj, g)
        nm = ADAM_B1 * m_ref[...] + (1.0 - ADAM_B1) * g
        nv = ADAM_B2 * v_ref[...] + (1.0 - ADAM_B2) * (g * g)
        g_ref[...] = g
        nm_ref[...] = nm
        nv_ref[...] = nv
        d_ref[...] = -ADAM_LR * ((nm / c1) / (jnp.sqrt(nv / c2) + ADAM_EPS) + ADAM_WD * w_ref[...])

    blk = pl.BlockSpec((tr, C), lambda l, i: (l * nb + i, 0))
    part = lambda j: pl.BlockSpec((N_DEV, tr, C), lambda l, i: (0, jnp.where(l == j, i, 0), 0))
    return pl.pallas_call(
        body, name=name, grid=(L, nb),
        in_specs=[blk] + [part(j) for j in range(L)] + [blk, blk],
        out_specs=[blk] * 4,
        out_shape=[jax.ShapeDtypeStruct((R, C), F32)] * 4,
        compiler_params=_params(("arbitrary", "arbitrary")),
    )(w, *parts, m, v)


_O1 = Q_LORA
_O2 = _O1 + KV_LORA
_O3 = _O2 + MLA_ROPE
_NB = SB_HEADS * SB_DIM
IN_W = _O2 + LANES + 3 * _NB
COL_KR = _O2 // LANES
COL_SB = COL_KR + 1


def _w_in_local(w):
    kr = w[_O2:_O3]
    pad = jnp.zeros((LANES - 2 * MLA_ROPE, w.shape[1]), w.dtype)
    return jnp.concatenate([w[:_O2], kr, kr, pad, w[_O3:]], axis=0)


def _w_in_grad(g):
    kr = (g[_O2:_O2 + MLA_ROPE].astype(F32) + g[_O2 + MLA_ROPE:_O2 + 2 * MLA_ROPE].astype(F32)).astype(g.dtype)
    return jnp.concatenate([g[:_O2], kr, g[_O2 + LANES:]], axis=0)


def _w_uq_local(w):
    w3 = w.reshape(MLA_HEADS // 2, 2, MLA_NOPE + MLA_ROPE, w.shape[1])
    nope = w3[:, :, :MLA_NOPE].reshape(MLA_HEADS // 2, 2 * MLA_NOPE, w.shape[1])
    rope = w3[:, :, MLA_NOPE:].reshape(MLA_HEADS // 2, 2 * MLA_ROPE, w.shape[1])
    pad = jnp.zeros((MLA_HEADS // 2, LANES - 2 * MLA_ROPE, w.shape[1]), w.dtype)
    return jnp.concatenate([nope, rope, pad], axis=1).reshape(-1, w.shape[1])


def _w_uq_grad(g):
    g3 = g.reshape(MLA_HEADS // 2, 2 * LANES, g.shape[1])
    nope = g3[:, :2 * MLA_NOPE].reshape(MLA_HEADS // 2, 2, MLA_NOPE, g.shape[1])
    rope = g3[:, LANES:LANES + 2 * MLA_ROPE].reshape(MLA_HEADS // 2, 2, MLA_ROPE, g.shape[1])
    return jnp.concatenate([nope, rope], axis=2).reshape(-1, g.shape[1])


def _w_ukv_local(w):
    w3 = w.reshape(MLA_HEADS, MLA_NOPE + MLA_V, w.shape[1])
    return jnp.concatenate([w3[:, :MLA_NOPE].reshape(-1, w.shape[1]),
                            w3[:, MLA_NOPE:].reshape(-1, w.shape[1])], axis=0)


def _w_ukv_grad(g):
    half = MLA_HEADS * MLA_NOPE
    kn = g[:half].reshape(MLA_HEADS, MLA_NOPE, g.shape[1])
    vv = g[half:].reshape(MLA_HEADS, MLA_V, g.shape[1])
    return jnp.concatenate([kn, vv], axis=1).reshape(-1, g.shape[1])


def _rope_tables(T):
    pos = jnp.arange(T, dtype=F32)
    inv_freq = ROPE_THETA ** (-jnp.arange(0, MLA_ROPE, 2, dtype=F32) / MLA_ROPE)
    ang = pos[:, None] * inv_freq[None, :]
    cos, sin = jnp.cos(ang), jnp.sin(ang)
    ones = jnp.ones((T, LANES - 2 * MLA_ROPE), F32)
    cos_k = jnp.concatenate([cos, cos, cos, cos, ones], axis=1)
    sin_k = jnp.concatenate([-sin, sin, -sin, sin, 0.0 * ones], axis=1)
    cos_q = jnp.concatenate([jnp.ones((T, LANES), F32), cos_k], axis=1)
    sin_q = jnp.concatenate([jnp.zeros((T, LANES), F32), sin_k], axis=1)
    return cos_q, sin_q, cos_k, sin_k


def _bias_diag_index():
    ell = np.arange(TOEP_W)
    return np.clip(BAND_W - ell, -REL_CLIP, REL_CLIP) + REL_CLIP


def _local_step(x, target, small, get_weights, put_grads, prefetch):
    T = x.shape[0]
    cos_q, sin_q, cos_k, sin_k = _rope_tables(T)
    G = {}
    W = dict(small)

    u0 = _rms_fwd(x, W["g_mix"][0:1], name="rms_mix0")
    bias_w = _toeplitz(W["od_rel_bias"][:, _bias_diag_index()])
    W.update(get_weights("in0", (u0, bias_w)))
    proj = _mm(u0, W["w_in_t"], dims="nt", name="proj_in")
    W.update(get_weights("mix0", proj))
    c_q, c_kv = proj[:, :_O1], proj[:, _O1:_O2]
    nq = _rms_fwd(c_q, W["g_cq"], name="rms_cq")
    nkv = _rms_fwd(c_kv, W["g_ckv"], name="rms_ckv")
    qa_raw = _mm(nq, W["w_uq_t"], dims="nt", name="proj_uq")
    kv = _mm(nkv, W["w_ukv_t"], dims="nt", out_dtype=BF16, name="proj_ukv")
    kr = _rope(proj, cos_k, sin_k, COL_KR, 1, BF16, name="rope_k")
    o_a, lse = _mla_fwd(qa_raw, cos_q, sin_q, kv, kr)
    o_b, o_b32, w_b, sp_b = _sb_fwd(proj, COL_SB, prefetch("ffn0", o_a))
    o_ab = jnp.concatenate([o_a, o_b], axis=1)
    h1 = _mm(o_ab, W["ev_w_out"], res=x, name="out_ev")

    def ffn_fwd(h, layer):
        W.update(get_weights(f"ffn{layer}", h))
        return _ffn_fwd(h, W["g_ffn"][layer:layer + 1], W[f"w_gate_t{layer}"], W[f"w_up_t{layer}"],
                        W[f"w_down{layer}"], name=f"ffn_fwd{layer}")

    h2, u1, a0, b0 = ffn_fwd(h1, 0)

    W.update(get_weights("mix1", h2))
    u2 = _rms_fwd(h2, W["g_mix"][1:2], name="rms_mix1")
    qkv = _mm(u2, W["od_w_qkv_t"], dims="nt", out_dtype=BF16, name="proj_qkv")
    nc = C_HEADS * C_DIM
    pad = ((PAD_KEYS, 0), (0, 0))
    k_pad, v_pad = jnp.pad(qkv[:, nc:2 * nc], pad), jnp.pad(qkv[:, 2 * nc:], pad)
    o_c, p_c = _band_fwd(qkv, k_pad, v_pad, bias_w)
    h3 = _mm(o_c, W["od_w_out"], res=h2, name="out_od")
    h4, u3, a1, b1 = ffn_fwd(h3, 1)

    loss, dh, dhb, G["g_final"] = _loss_head(h4, W["g_final"], target)

    def ffn_bwd(dh, dhb, h, u, a, b, layer):
        du, g_gate, g_up, g_down = _ffn_bwd(dhb, u, a, b, W[f"w_gate_t{layer}"], W[f"w_up_t{layer}"],
                                            W[f"w_down{layer}"], name=f"ffn_bwd{layer}")
        tok = put_grads(f"ffn{layer}", {"w_gate_t": g_gate, "w_up_t": g_up, "w_down": g_down})
        return _rms_bwd(h, W["g_ffn"][layer:layer + 1] + tok[:1, :1], du, dres=dh, name=f"rms_ffn_bwd{layer}")

    dh3, dh3b, g_gffn1 = ffn_bwd(dh, dhb, h3, u3, a1, b1, 1)

    do_c = _mm(dh3b, W["od_w_out"], dims="nt", name="out_od_dx")
    g_od_out = _mm(o_c, dh3b, dims="tn", out_dtype=BF16, name="out_od_dw")
    dq_c, dk_p, dv_p, dbias_w = _band_bwd(qkv, k_pad, v_pad, p_c, do_c)
    dqkv = jnp.concatenate([dq_c, dk_p[PAD_KEYS:], dv_p[PAD_KEYS:]], axis=1)
    tok = put_grads("mix1", {"od_w_qkv_t": _mm(dqkv, u2, dims="tn", out_dtype=BF16, name="proj_qkv_dw"),
                             "od_w_out": g_od_out})
    ddiag = _toeplitz_bwd(dbias_w)
    n_far = BAND_W - REL_CLIP + 1
    G["od_rel_bias"] = jnp.concatenate(
        [jnp.zeros((C_HEADS, REL_CLIP - BAND_TQ + 1), F32), ddiag[:, n_far:][:, ::-1],
         jnp.sum(ddiag[:, :n_far], axis=1, keepdims=True)], axis=1)
    dh2, dh2b, g_gmix1 = _mm_rms_bwd(dqkv, W["od_w_qkv_t"], h2, W["g_mix"][1:2] + tok[:1, :1], dh3,
                                     name="proj_qkv_dx")

    dh1, dh1b, g_gffn0 = ffn_bwd(dh2, dh2b, h1, u1, a0, b0, 0)
    G["g_ffn"] = jnp.concatenate([g_gffn0, g_gffn1], axis=0)

    do_ab = _mm(dh1b, W["ev_w_out"], dims="nt", name="out_ev_dx")
    g0 = {"ev_w_out": _mm(o_ab, dh1b, dims="tn", out_dtype=BF16, name="out_ev_dw")}
    dqa_raw, dkn, dva, dkr = _mla_bwd(qa_raw, cos_q, sin_q, kv, kr, o_a, lse, do_ab, 0)
    dlat, g0["w_uq_t"], g0["w_ukv_t"], G["g_cq"], G["g_ckv"] = _latent_bwd(
        proj, nq, nkv, dqa_raw, dkn, dva, dkr, cos_k, sin_k, W["g_cq"], W["g_ckv"], W["w_uq_t"], W["w_ukv_t"])
    tok = put_grads("mix0", g0)
    dqb, dkb, dvb = _sb_bwd(proj, COL_SB, o_b32, w_b, sp_b, do_ab, MLA_HEADS // 2, tok)
    dproj = jnp.concatenate([dlat, dqb, dkb, dvb], axis=1)
    tok = put_grads("in0", {"w_in_t": _mm(dproj, u0, dims="tn", name="proj_in_dw")})
    dx, _, g_gmix0 = _mm_rms_bwd(dproj, W["w_in_t"], x, W["g_mix"][0:1] + tok[:1, :1], dh1, name="proj_in_dx")
    G["g_mix"] = jnp.concatenate([g_gmix0, g_gmix1], axis=0)
    return loss[0, 0], dx, G


_BIG = ["ev_w_in", "ev_w_uq", "ev_w_ukv", "ev_w_out", "od_w_qkv", "od_w_out", "w_gate", "w_up", "w_down"]
_COL_SHARDED = {"ev_w_in", "ev_w_uq", "ev_w_ukv", "od_w_qkv", "w_gate", "w_up"}
_SMALL = ["ev_g_cq", "ev_g_ckv", "od_rel_bias", "g_mix", "g_ffn", "g_final"]
_GROUPS = {
    "in0": ["ev_w_in"],
    "mix0": ["ev_w_uq", "ev_w_ukv", "ev_w_out"],
    "ffn0": ["w_gate0", "w_up0", "w_down0"],
    "mix1": ["od_w_qkv", "od_w_out"],
    "ffn1": ["w_gate1", "w_up1", "w_down1"],
}
_GROUP_SRC = {n + str(l): (n, l) for n in ("w_gate", "w_up", "w_down") for l in (0, 1)}
_BATCHES = {"in0": ["in0"], "layer0": ["mix0", "ffn0"], "layer1": ["mix1", "ffn1"]}
_BATCH_OF = {grp: batch for batch, grps in _BATCHES.items() for grp in grps}
_SMALL_ROWS = 8
_SMALL_COLS = 1792


def _pack_small(vals):
    flat = jnp.concatenate([v.reshape(-1).astype(F32) for v in vals])
    flat = jnp.pad(flat, (0, _SMALL_ROWS * _SMALL_COLS - flat.shape[0]))
    return flat.reshape(_SMALL_ROWS, _SMALL_COLS)


def _unpack_small(packed, like):
    flat = packed.reshape(-1)
    out, off = [], 0
    for v in like:
        out.append(flat[off:off + v.size].reshape(v.shape))
        off += v.size
    return out


def kernel(x, ev_w_in, ev_g_cq, ev_w_uq, ev_g_ckv, ev_w_ukv, ev_w_out, od_w_qkv, od_rel_bias, od_w_out, g_mix, g_ffn, w_gate, w_up, w_down, g_final, loss_target, m_ev_w_in, m_ev_g_cq, m_ev_w_uq, m_ev_g_ckv, m_ev_w_ukv, m_ev_w_out, m_od_w_qkv, m_od_rel_bias, m_od_w_out, m_g_mix, m_g_ffn, m_w_gate, m_w_up, m_w_down, m_g_final, v_ev_w_in, v_ev_g_cq, v_ev_w_uq, v_ev_g_ckv, v_ev_w_ukv, v_ev_w_out, v_od_w_qkv, v_od_rel_bias, v_od_w_out, v_g_mix, v_g_ffn, v_w_gate, v_w_up, v_w_down, v_g_final):
    args = dict(locals())
    w = {n: args[n] for n in _BIG + _SMALL}
    mom = {n: args["m_" + n] for n in _BIG + _SMALL}
    var = {n: args["v_" + n] for n in _BIG + _SMALL}

    own = {}
    for grp, names in _GROUPS.items():
        for n in names:
            base, layer = _GROUP_SRC.get(n, (n, 0))
            shard = w[base][layer:layer + 1]
            own[n] = (jnp.swapaxes(shard, 1, 2) if base in _COL_SHARDED else shard).astype(BF16)
    placed = dict(zip(own, _place_own(list(own.values()), [False] * len(own), name="gather_own")))
    handles, token = _gather2_start(
        [[own[n] for n in names] for names in _GROUPS.values()],
        [[placed[n] for n in names] for names in _GROUPS.values()], x[0, :8, :LANES], name="gather_start")
    gather = dict(zip(_GROUPS, handles))
    pass_before = {"in0": ["in0"], "mix0": ["mix0"]}
    pass_after = {"ffn0": ("mix1", "g_ffn"), "mix1": ("ffn1", "g_mix")}

    def prefetch(grp, after):
        gather[grp], tok = _gather2_pass_on(gather[grp], after, name="gather_pass_" + grp)
        return tok

    def get_weights(grp, after):
        names = _GROUPS[grp]
        after = token if after is None else after
        for g in pass_before.get(grp, []):
            gather[g], _ = _gather2_pass_on(gather[g], after, name="gather_pass_" + g)
        lands = _gather2_wait(gather[grp], after, name="gather_wait_" + grp)
        full = {n: l.reshape(-1, l.shape[-1]) for n, l in zip(names, lands)}
        out = {}
        if grp in pass_after:
            g, gain = pass_after[grp]
            gather[g], tok = _gather2_pass_on(gather[g], lands[0], name="gather_pass_" + g)
            out[gain] = small[gain] + tok[:1, :1]
        if grp == "in0":
            out.update({"w_in_t": _w_in_local(full["ev_w_in"])})
        elif grp == "mix0":
            out.update({"w_uq_t": _w_uq_local(full["ev_w_uq"]), "w_ukv_t": _w_ukv_local(full["ev_w_ukv"]),
                        "ev_w_out": full["ev_w_out"]})
        elif grp == "mix1":
            out.update({"od_w_qkv_t": full["od_w_qkv"], "od_w_out": full["od_w_out"]})
        else:
            layer = grp[-1]
            out.update({"w_gate_t" + layer: full["w_gate" + layer], "w_up_t" + layer: full["w_up" + layer],
                        "w_down" + layer: full["w_down" + layer]})
        return out

    scatter, pending = {}, {}

    def put_grads(grp, g):
        if grp == "in0":
            g = {"ev_w_in": _w_in_grad(g["w_in_t"])}
        elif grp == "mix0":
            g = {"ev_w_uq": _w_uq_grad(g["w_uq_t"]), "ev_w_ukv": _w_ukv_grad(g["w_ukv_t"]),
                 "ev_w_out": g["ev_w_out"]}
        elif grp == "mix1":
            g = {"od_w_qkv": g["od_w_qkv_t"], "od_w_out": g["od_w_out"]}
        else:
            layer = grp[-1]
            g = {"w_gate" + layer: g["w_gate_t"], "w_up" + layer: g["w_up_t"], "w_down" + layer: g["w_down"]}
        pending.update({n: v.reshape(N_DEV, 1, v.shape[0] // N_DEV, v.shape[1]).astype(BF16) for n, v in g.items()})
        batch = _BATCH_OF[grp]
        names = [n for gr in _BATCHES[batch] for n in _GROUPS[gr]]
        if not all(n in pending for n in names):
            return jnp.zeros((8, LANES), F32)
        send = [pending[n] for n in names]
        scatter[batch], tok = _exchange_start(send, [True] * len(names), send[0], name="scatter_start_" + batch)
        return tok

    small = {"g_cq": ev_g_cq, "g_ckv": ev_g_ckv, "od_rel_bias": od_rel_bias[0],
             "g_mix": g_mix + token[0, 0], "g_ffn": g_ffn, "g_final": g_final.reshape(1, -1)}
    loss_part, dx, G = _local_step(x[0], loss_target[0], small, get_weights, put_grads, prefetch)
    g_small = _pack_small([G["g_cq"], G["g_ckv"], G["od_rel_bias"], G["g_mix"], G["g_ffn"], G["g_final"],
                           loss_part.reshape(1)])
    small_handle, _ = _exchange_start([g_small], [False], dx, name="gather_start_small")

    grads, deltas, new_m, new_v = {}, {}, {}, {}
    parts, after = {}, dx

    def wait_parts(batch, after):
        lands = _exchange_wait(scatter[batch], after, name="scatter_wait_" + batch)
        parts.update(zip([n for grp in _BATCHES[batch] for n in _GROUPS[grp]], lands))
        return lands[0]

    def adamw(n):
        col = n in _COL_SHARDED
        rows = lambda a: (jnp.swapaxes(a, 1, 2) if col else a).reshape(-1, a.shape[1 if col else 2])
        layers = [parts[n]] if n in parts else [parts[n + "0"], parts[n + "1"]]
        res = _adamw(rows(w[n]), [p.reshape(N_DEV, -1, p.shape[-1]) for p in layers], rows(mom[n]), rows(var[n]),
                     name="adamw_" + n)
        L, a1, a2 = w[n].shape
        back = lambda r: jnp.swapaxes(r.reshape(L, a2, a1), 1, 2) if col else r.reshape(L, a1, a2)
        grads[n], deltas[n], new_m[n], new_v[n] = [back(r) for r in res]
        return res[0]

    for batch in ("layer1", "layer0"):
        after = wait_parts(batch, after)
    after = wait_parts("in0", tuple(adamw(n) for n in _BIG[1:]))
    after = adamw("ev_w_in")
    small_w = [w[n] for n in _SMALL]
    small_parts = _exchange_wait(small_handle, after, name="gather_wait_small")[0]
    loss = jnp.sum(small_parts.reshape(N_DEV, -1)[:, sum(v.size for v in small_w)])
    res = _adamw(_pack_small(small_w), [small_parts], _pack_small([mom[n] for n in _SMALL]),
                 _pack_small([var[n] for n in _SMALL]), name="adamw_small")
    for d, packed in zip((grads, deltas, new_m, new_v), res):
        for n, val in zip(_SMALL, _unpack_small(packed, small_w)):
            d[n] = val

    order = ["ev_w_in", "ev_g_cq", "ev_w_uq", "ev_g_ckv", "ev_w_ukv", "ev_w_out", "od_w_qkv", "od_rel_bias",
             "od_w_out", "g_mix", "g_ffn", "w_gate", "w_up", "w_down", "g_final"]
    out = [loss, dx[None]]
    for d in (grads, deltas, new_m, new_v):
        out += [d[n] for n in order]
    return tuple(out)
```

```python
import functools

import numpy as np
import jax
import jax.numpy as jnp
from jax import lax
from jax.experimental import pallas as pl
from jax.experimental.pallas import tpu as pltpu

F32 = jnp.float32
BF16 = jnp.bfloat16

D_MODEL = 1024
CHUNK = 64
MLA_HEADS = 8
MLA_NOPE = 64
MLA_ROPE = 32
MLA_V = 64
Q_LORA = 384
KV_LORA = 256
ROPE_THETA = 10000.0
SB_HEADS = 8
SB_DIM = 64
C_HEADS = 16
C_DIM = 64
LEFT_CHUNKS = 8
REL_CLIP = 256
D_FF = 2816
RMS_EPS = 1e-6
ADAM_LR = 0.001
ADAM_B1 = 0.9
ADAM_B2 = 0.999
ADAM_EPS = 1e-08
ADAM_WD = 0.01
ADAM_STEP = 10

N_DEV = 8
LANES = 128
HEAD = 64
assert HEAD == MLA_NOPE == MLA_V == SB_DIM == C_DIM and 2 * HEAD == LANES
CHUNK_BITS = CHUNK.bit_length() - 1
assert 1 << CHUNK_BITS == CHUNK
VMEM_LIMIT = 56 * 1024 * 1024
NEG = -1e30
PAD_KEYS = LEFT_CHUNKS * CHUNK
BAND_TQ = 128
BAND_W = BAND_TQ + PAD_KEYS
TOEP_W = BAND_W + BAND_TQ

NN = (((1,), (0,)), ((), ()))
NT = (((1,), (1,)), ((), ()))
TN = (((0,), (0,)), ((), ()))


def _dot(a, b, dn):
    return lax.dot_general(a, b, dn, preferred_element_type=F32)


def _pick(dim, pref):
    if dim <= pref:
        return dim
    best = None
    for t in range(LANES, pref + 1, LANES):
        if dim % t == 0:
            best = t
    assert best is not None, (dim, pref)
    return best


def _params(sem):
    return pltpu.CompilerParams(dimension_semantics=sem, vmem_limit_bytes=VMEM_LIMIT)


def _mm(a, b, dims="nn", res=None, out_dtype=F32, name="mm"):
    if dims == "nn":
        (M, K), (K2, N) = a.shape, b.shape
    elif dims == "nt":
        (M, K), (N, K2) = a.shape, b.shape
    else:
        (K, M), (K2, N) = a.shape, b.shape
    assert K == K2, (a.shape, b.shape, dims)
    tm, tn, tk = _pick(M, 1024), _pick(N, 1152), _pick(K, 1024)
    nk = K // tk
    dn = {"nn": NN, "nt": NT, "tn": TN}[dims]
    has_res = res is not None

    def body(*refs):
        if has_res:
            a_ref, b_ref, r_ref, o_ref, acc = refs
        else:
            a_ref, b_ref, o_ref, acc = refs
        k = pl.program_id(2)

        @pl.when(k == 0)
        def _():
            acc[...] = jnp.zeros_like(acc)

        acc[...] += _dot(a_ref[...].astype(BF16), b_ref[...].astype(BF16), dn)

        @pl.when(k == nk - 1)
        def _():
            r = acc[...]
            if has_res:
                r = r + r_ref[...]
            o_ref[...] = r.astype(out_dtype)

    a_spec = (pl.BlockSpec((tk, tm), lambda i, j, k: (k, i)) if dims == "tn"
              else pl.BlockSpec((tm, tk), lambda i, j, k: (i, k)))
    b_spec = (pl.BlockSpec((tn, tk), lambda i, j, k: (j, k)) if dims == "nt"
              else pl.BlockSpec((tk, tn), lambda i, j, k: (k, j)))
    o_spec = pl.BlockSpec((tm, tn), lambda i, j, k: (i, j))
    in_specs = [a_spec, b_spec] + ([o_spec] if has_res else [])
    args = (a, b) + ((res,) if has_res else ())
    return pl.pallas_call(
        body, name=name, grid=(M // tm, N // tn, nk),
        in_specs=in_specs, out_specs=o_spec,
        out_shape=jax.ShapeDtypeStruct((M, N), out_dtype),
        scratch_shapes=[pltpu.VMEM((tm, tn), F32)],
        compiler_params=_params(("parallel", "parallel", "arbitrary")),
    )(*args)


def _rms_fwd(x, g, out_dtype=BF16, name="rms_fwd"):
    T, Fd = x.shape
    tm = _pick(T, 256)

    def body(x_ref, g_ref, o_ref):
        xv = x_ref[...]
        r = lax.rsqrt(jnp.mean(xv * xv, axis=-1, keepdims=True) + RMS_EPS)
        o_ref[...] = (xv * r * g_ref[...]).astype(out_dtype)

    return pl.pallas_call(
        body, name=name, grid=(T // tm,),
        in_specs=[pl.BlockSpec((tm, Fd), lambda i: (i, 0)), pl.BlockSpec((1, Fd), lambda i: (0, 0))],
        out_specs=pl.BlockSpec((tm, Fd), lambda i: (i, 0)),
        out_shape=jax.ShapeDtypeStruct((T, Fd), out_dtype),
        compiler_params=_params(("parallel",)),
    )(x, g)


def _rms_bwd(x, g, dy, dres=None, name="rms_bwd"):
    T, Fd = x.shape
    tm = _pick(T, 256)
    has_res = dres is not None

    def body(*refs):
        if has_res:
            x_ref, g_ref, dy_ref, r_ref, dx_ref, dxb_ref, dg_ref = refs
        else:
            x_ref, g_ref, dy_ref, dx_ref, dxb_ref, dg_ref = refs
        xv, dyv = x_ref[...], dy_ref[...]
        r = lax.rsqrt(jnp.mean(xv * xv, axis=-1, keepdims=True) + RMS_EPS)
        gdy = dyv * g_ref[...]
        dot = jnp.mean(xv * gdy, axis=-1, keepdims=True)
        dx = r * gdy - xv * (r * r * r * dot)
        if has_res:
            dx = dx + r_ref[...]
        dx_ref[...] = dx
        dxb_ref[...] = dx.astype(BF16)

        @pl.when(pl.program_id(0) == 0)
        def _():
            dg_ref[...] = jnp.zeros_like(dg_ref)

        dg_ref[...] += jnp.sum(dyv * xv * r, axis=0, keepdims=True)

    row = pl.BlockSpec((tm, Fd), lambda i: (i, 0))
    vec = pl.BlockSpec((1, Fd), lambda i: (0, 0))
    in_specs = [row, vec, row] + ([row] if has_res else [])
    args = (x, g, dy) + ((dres,) if has_res else ())
    return pl.pallas_call(
        body, name=name, grid=(T // tm,),
        in_specs=in_specs, out_specs=[row, row, vec],
        out_shape=[jax.ShapeDtypeStruct((T, Fd), F32), jax.ShapeDtypeStruct((T, Fd), BF16),
                   jax.ShapeDtypeStruct((1, Fd), F32)],
        compiler_params=_params(("arbitrary",)),
    )(*args)


def _mm_rms_bwd(a, b, x, g, dres, name="mm_rms_bwd"):
    T, K = a.shape
    Fd = b.shape[1]
    tm, tk = _pick(T, 512), _pick(K, 1024)
    nk = K // tk

    def body(a_ref, b_ref, x_ref, g_ref, r_ref, dx_ref, dxb_ref, dg_ref, acc):
        i, k = pl.program_id(0), pl.program_id(1)

        @pl.when(k == 0)
        def _():
            acc[...] = jnp.zeros_like(acc)

        @pl.when((k == 0) & (i == 0))
        def _():
            dg_ref[...] = jnp.zeros_like(dg_ref)

        acc[...] += _dot(a_ref[...].astype(BF16), b_ref[...].astype(BF16), NN)

        @pl.when(k == nk - 1)
        def _():
            xv, dyv = x_ref[...], acc[...]
            r = lax.rsqrt(jnp.mean(xv * xv, axis=-1, keepdims=True) + RMS_EPS)
            gdy = dyv * g_ref[...]
            dot = jnp.mean(xv * gdy, axis=-1, keepdims=True)
            dx = r * gdy - xv * (r * r * r * dot) + r_ref[...]
            dx_ref[...] = dx
            dxb_ref[...] = dx.astype(BF16)
            dg_ref[...] += jnp.sum(dyv * xv * r, axis=0, keepdims=True)

    row = pl.BlockSpec((tm, Fd), lambda i, k: (i, 0))
    vec = pl.BlockSpec((1, Fd), lambda i, k: (0, 0))
    return pl.pallas_call(
        body, name=name, grid=(T // tm, nk),
        in_specs=[pl.BlockSpec((tm, tk), lambda i, k: (i, k)), pl.BlockSpec((tk, Fd), lambda i, k: (k, 0)),
                  row, vec, row],
        out_specs=[row, row, vec],
        out_shape=[jax.ShapeDtypeStruct((T, Fd), F32), jax.ShapeDtypeStruct((T, Fd), BF16),
                   jax.ShapeDtypeStruct((1, Fd), F32)],
        scratch_shapes=[pltpu.VMEM((tm, Fd), F32)],
        compiler_params=_params(("arbitrary", "arbitrary")),
    )(a, b, x, g, dres)


def _latent_bwd(proj, nq, nkv, dqa, dkn, dva, dkr, cos_k, sin_k, g_cq, g_ckv, w_uq_t, w_ukv_t, name="latent_bwd"):
    T = proj.shape[0]
    tm = _pick(T, 512)
    wl = _O2

    def rms_bwd(xv, gv, dyv):
        r = lax.rsqrt(jnp.mean(xv * xv, axis=-1, keepdims=True) + RMS_EPS)
        gdy = dyv * gv
        dot = jnp.mean(xv * gdy, axis=-1, keepdims=True)
        return r * gdy - xv * (r * r * r * dot), jnp.sum(dyv * xv * r, axis=0, keepdims=True)

    def body(p_ref, nq_ref, nkv_ref, dqa_ref, dkn_ref, dva_ref, dkr_ref, c_ref, s_ref, gq_ref, gkv_ref, wq_ref, wkv_ref,
             dlat_ref, dwq_ref, dwkv_ref, dgq_ref, dgkv_ref):
        @pl.when(pl.program_id(0) == 0)
        def _():
            for ref in (dwq_ref, dwkv_ref, dgq_ref, dgkv_ref):
                ref[...] = jnp.zeros_like(ref)

        dqv = dqa_ref[...]
        dkv = jnp.concatenate([dkn_ref[...], dva_ref[...]], axis=1)
        pv = p_ref[...]
        dc_q, dgq = rms_bwd(pv[:, :_O1], gq_ref[...], _dot(dqv, wq_ref[...], NN))
        dc_kv, dgkv = rms_bwd(pv[:, _O1:], gkv_ref[...], _dot(dkv, wkv_ref[...], NN))
        dkr_raw = _rotate(dkr_ref[...], c_ref[...], -s_ref[...])
        dlat_ref[...] = jnp.concatenate([dc_q, dc_kv, dkr_raw], axis=1).astype(BF16)
        dwq_ref[...] += _dot(dqv, nq_ref[...], TN)
        dwkv_ref[...] += _dot(dkv, nkv_ref[...], TN)
        dgq_ref[...] += dgq
        dgkv_ref[...] += dgkv

    row = lambda w: pl.BlockSpec((tm, w), lambda i: (i, 0))
    const = lambda a: pl.BlockSpec(a.shape, lambda i: (0, 0))
    outs = [jax.ShapeDtypeStruct((T, wl + LANES), BF16), jax.ShapeDtypeStruct(w_uq_t.shape, F32),
            jax.ShapeDtypeStruct(w_ukv_t.shape, F32), jax.ShapeDtypeStruct(g_cq.shape, F32),
            jax.ShapeDtypeStruct(g_ckv.shape, F32)]
    return pl.pallas_call(
        body, name=name, grid=(T // tm,),
        in_specs=[row(wl), row(_O1), row(_O2 - _O1), row(dqa.shape[1]), row(dkn.shape[1]), row(dva.shape[1]),
                  row(LANES), row(LANES), row(LANES), const(g_cq), const(g_ckv), const(w_uq_t), const(w_ukv_t)],
        out_specs=[row(wl + LANES)] + [const(o) for o in outs[1:]],
        out_shape=outs,
        compiler_params=_params(("arbitrary",)),
    )(proj, nq, nkv, dqa, dkn, dva, dkr, cos_k, sin_k, g_cq, g_ckv, w_uq_t, w_ukv_t)


def _loss_head(h, g, target, name="loss_head"):
    T, Fd = h.shape
    tm = _pick(T, 256)

    def body(h_ref, g_ref, t_ref, loss_ref, dh_ref, dhb_ref, dg_ref):
        xv = h_ref[...]
        r = lax.rsqrt(jnp.mean(xv * xv, axis=-1, keepdims=True) + RMS_EPS)
        diff = xv * r * g_ref[...] - t_ref[...]
        part = 0.5 * jnp.sum(jnp.mean(diff * diff, axis=-1, keepdims=True), axis=0, keepdims=True)
        dyv = diff * (1.0 / Fd)
        gdy = dyv * g_ref[...]
        dot = jnp.mean(xv * gdy, axis=-1, keepdims=True)
        dh = r * gdy - xv * (r * r * r * dot)
        dh_ref[...] = dh
        dhb_ref[...] = dh.astype(BF16)

        @pl.when(pl.program_id(0) == 0)
        def _():
            dg_ref[...] = jnp.zeros_like(dg_ref)
            loss_ref[...] = jnp.zeros_like(loss_ref)

        dg_ref[...] += jnp.sum(dyv * xv * r, axis=0, keepdims=True)
        loss_ref[...] += jnp.broadcast_to(part, loss_ref.shape)

    row = pl.BlockSpec((tm, Fd), lambda i: (i, 0))
    vec = pl.BlockSpec((1, Fd), lambda i: (0, 0))
    return pl.pallas_call(
        body, name=name, grid=(T // tm,),
        in_specs=[row, vec, row],
        out_specs=[pl.BlockSpec((1, LANES), lambda i: (0, 0)), row, row, vec],
        out_shape=[jax.ShapeDtypeStruct((1, LANES), F32), jax.ShapeDtypeStruct((T, Fd), F32),
                   jax.ShapeDtypeStruct((T, Fd), BF16), jax.ShapeDtypeStruct((1, Fd), F32)],
        compiler_params=_params(("arbitrary",)),
    )(h, g, target)


FFN_TF = 256


def _ffn_fwd(h, g, wg_t, wu_t, wd, name="ffn_fwd"):
    T, Dm = h.shape
    Fh = wd.shape[0]
    tm = _pick(T, 2048)
    nf = Fh // FFN_TF

    def body(h_ref, g_ref, wg_ref, wu_ref, wd_ref, o_ref, u_ref, a_ref, b_ref):
        j = pl.program_id(1)

        @pl.when(j == 0)
        def _():
            xv = h_ref[...]
            r = lax.rsqrt(jnp.mean(xv * xv, axis=-1, keepdims=True) + RMS_EPS)
            u_ref[...] = (xv * r * g_ref[...]).astype(BF16)
            o_ref[...] = xv

        u = u_ref[...]
        a = _dot(u, wg_ref[...], NT).astype(BF16)
        b = _dot(u, wu_ref[...], NT).astype(BF16)
        a_ref[...] = a
        b_ref[...] = b
        af = a.astype(F32)
        s = (af * jax.nn.sigmoid(af) * b.astype(F32)).astype(BF16)
        o_ref[...] += _dot(s, wd_ref[...], NN)

    row = pl.BlockSpec((tm, Dm), lambda i, j: (i, 0))
    wblk = pl.BlockSpec((FFN_TF, Dm), lambda i, j: (j, 0))
    ablk = pl.BlockSpec((tm, FFN_TF), lambda i, j: (i, j))
    return pl.pallas_call(
        body, name=name, grid=(T // tm, nf),
        in_specs=[pl.BlockSpec((tm, Dm), lambda i, j: (i, 0), pipeline_mode=pl.Buffered(1)),
                  pl.BlockSpec((1, Dm), lambda i, j: (0, 0)), wblk, wblk, wblk],
        out_specs=[row, row, ablk, ablk],
        out_shape=[jax.ShapeDtypeStruct((T, Dm), F32), jax.ShapeDtypeStruct((T, Dm), BF16),
                   jax.ShapeDtypeStruct((T, Fh), BF16), jax.ShapeDtypeStruct((T, Fh), BF16)],
        compiler_params=_params(("parallel", "arbitrary")),
    )(h, g, wg_t, wu_t, wd)


def _ffn_bwd(dh, u, a, b, wg_t, wu_t, wd, name="ffn_bwd"):
    T, Dm = dh.shape
    Fh = wd.shape[0]
    nf = Fh // FFN_TF
    once = pl.Buffered(1)

    def body(dh_ref, u_ref, a_ref, b_ref, wg_ref, wu_ref, wd_ref, du_ref, dwg_ref, dwu_ref, dwd_ref):
        j = pl.program_id(0)

        @pl.when(j == 0)
        def _():
            du_ref[...] = jnp.zeros_like(du_ref)

        ds = _dot(dh_ref[...], wd_ref[...], NT)
        af, bf = a_ref[...].astype(F32), b_ref[...].astype(F32)
        sig = jax.nn.sigmoid(af)
        sa = af * sig
        dwd_ref[...] = _dot((sa * bf).astype(BF16), dh_ref[...], TN).astype(BF16)
        dab = jnp.concatenate([(ds * bf * (sig * (1.0 + af * (1.0 - sig)))).astype(BF16),
                               (ds * sa).astype(BF16)], axis=1)
        dw = _dot(dab, u_ref[...], TN)
        dwg_ref[...] = dw[:FFN_TF].astype(BF16)
        dwu_ref[...] = dw[FFN_TF:].astype(BF16)
        du_ref[...] += _dot(dab, jnp.concatenate([wg_ref[...], wu_ref[...]], axis=0), NN)

    full = lambda: pl.BlockSpec((T, Dm), lambda j: (0, 0), pipeline_mode=once)
    wblk = pl.BlockSpec((FFN_TF, Dm), lambda j: (j, 0))
    ablk = pl.BlockSpec((T, FFN_TF), lambda j: (0, j))
    return pl.pallas_call(
        body, name=name, grid=(nf,),
        in_specs=[full(), full(), ablk, ablk, wblk, wblk, wblk],
        out_specs=[pl.BlockSpec((T, Dm), lambda j: (0, 0)), wblk, wblk, wblk],
        out_shape=[jax.ShapeDtypeStruct((T, Dm), F32)] + [jax.ShapeDtypeStruct((Fh, Dm), BF16)] * 3,
        compiler_params=_params(("arbitrary",)),
    )(dh, u, a, b, wg_t, wu_t, wd)


def _rope(x, cos_t, sin_t, col0, ncols, out_dtype, name="rope"):
    T = x.shape[0]
    wt = cos_t.shape[1]
    tm = _pick(T, 256)
    nb = ncols * LANES // wt
    half = MLA_ROPE // 2

    def body(x_ref, c_ref, s_ref, o_ref):
        xv = x_ref[...].astype(F32)
        lane = lax.broadcasted_iota(jnp.int32, xv.shape, 1)
        first = (lane & (MLA_ROPE - 1)) < half
        swapped = jnp.where(first, pltpu.roll(xv, wt - half, 1), pltpu.roll(xv, half, 1))
        o_ref[...] = (xv * c_ref[...] + swapped * s_ref[...]).astype(out_dtype)

    off = col0 * LANES // wt
    return pl.pallas_call(
        body, name=name, grid=(T // tm, nb),
        in_specs=[pl.BlockSpec((tm, wt), lambda i, j: (i, j + off)),
                  pl.BlockSpec((tm, wt), lambda i, j: (i, 0)),
                  pl.BlockSpec((tm, wt), lambda i, j: (i, 0))],
        out_specs=pl.BlockSpec((tm, wt), lambda i, j: (i, j)),
        out_shape=jax.ShapeDtypeStruct((T, ncols * LANES), out_dtype),
        compiler_params=_params(("parallel", "parallel")),
    )(x, cos_t, sin_t)


ATT_TQ = 512
ATT_TK = 256


def _mla_masks(shape):
    lane = lax.broadcasted_iota(jnp.int32, shape, 1)
    m0 = (lane < HEAD) | ((lane >= LANES) & (lane < LANES + MLA_ROPE))
    m1 = ((lane >= HEAD) & (lane < LANES)) | ((lane >= LANES + MLA_ROPE) & (lane < LANES + 2 * MLA_ROPE))
    return m0, m1


def _by_twos(n, step, carry):
    carry = lax.fori_loop(0, n // 2, lambda i, c: step(2 * i + 1, step(2 * i, c)), carry)
    return lax.fori_loop(0, n % 2, lambda _, c: step(n - 1, c), carry)


def _chunk_ok(tq, tk, d):
    row = lax.broadcasted_iota(jnp.int32, (tq, tk), 0)
    col = lax.broadcasted_iota(jnp.int32, (tq, tk), 1) + d * tk
    return jnp.concatenate([(col >> CHUNK_BITS) <= (row >> CHUNK_BITS)] * 2, axis=0)


def _rotate(x, cos_t, sin_t):
    half = MLA_ROPE // 2
    lane = lax.broadcasted_iota(jnp.int32, x.shape, 1)
    first = (lane & (MLA_ROPE - 1)) < half
    swapped = jnp.where(first, pltpu.roll(x, x.shape[1] - half, 1), pltpu.roll(x, half, 1))
    return x * cos_t + swapped * sin_t


def _mla_fwd(q, cos_q, sin_q, kv, kr, name="mla_fwd"):
    T = q.shape[0]
    tq, tk = _pick(T, ATT_TQ), _pick(T, ATT_TK)
    nd = tq // tk
    npair = MLA_HEADS // 2
    scale = (MLA_NOPE + MLA_ROPE) ** -0.5

    def body(q_ref, c_ref, s_ref, kn_ref, v_ref, kr_ref, o_ref, lse_ref):
        m_idx = pl.program_id(1)
        qv = _rotate(q_ref[...], c_ref[...], s_ref[...]).astype(BF16)
        m0, m1 = _mla_masks(qv.shape)
        qs = jnp.concatenate([jnp.where(m0, qv, 0), jnp.where(m1, qv, 0)], axis=0).astype(BF16)

        def block(kb, carry, ok):
            ks = pl.ds(pl.multiple_of(kb * tk, tk), tk)
            kcat = jnp.concatenate([kn_ref[ks, :], kr_ref[ks, :]], axis=1)
            mx, l, acc = carry
            s = _dot(qs, kcat, NT) * scale
            if ok is not None:
                s = jnp.where(ok, s, NEG)
            mn = jnp.maximum(mx, jnp.max(s, axis=-1, keepdims=True))
            alpha = jnp.exp(mx - mn)
            p = jnp.exp(s - mn)
            return (mn, alpha * l + jnp.sum(p, axis=-1, keepdims=True),
                    alpha * acc + _dot(p.astype(BF16), v_ref[ks, :], NN))

        init = (jnp.full((2 * tq, 1), NEG, F32), jnp.zeros((2 * tq, 1), F32), jnp.zeros((2 * tq, LANES), F32))
        res = init
        for d in range(nd):
            res = block(m_idx * nd + d, res, _chunk_ok(tq, tk, d))
        mx, l, acc = _by_twos(m_idx * nd, lambda kb, c: block(kb, c, None), res)
        h0 = lax.broadcasted_iota(jnp.int32, (tq, LANES), 1) < HEAD
        o_ref[...] = _two_heads(acc * (1.0 / l), h0).astype(o_ref.dtype)
        lse_ref[...] = _two_heads(jnp.broadcast_to(mx + jnp.log(l), (2 * tq, LANES)), h0)

    full = lambda col: pl.BlockSpec((T, LANES), col)
    table = pl.BlockSpec((tq, 2 * LANES), lambda p, m: (m, 0))
    return pl.pallas_call(
        body, name=name, grid=(npair, T // tq),
        in_specs=[pl.BlockSpec((tq, 2 * LANES), lambda p, m: (m, p)), table, table,
                  full(lambda p, m: (0, p)), full(lambda p, m: (0, npair + p)), full(lambda p, m: (0, 0))],
        out_specs=[pl.BlockSpec((tq, LANES), lambda p, m: (m, p)),
                   pl.BlockSpec((tq, LANES), lambda p, m: (m, p))],
        out_shape=[jax.ShapeDtypeStruct((T, npair * LANES), BF16),
                   jax.ShapeDtypeStruct((T, npair * LANES), F32)],
        compiler_params=_params(("parallel", "arbitrary")),
    )(q, cos_q, sin_q, kv, kv, kr)


def _mla_bwd(q, cos_q, sin_q, kv, kr, o, lse, do, do_col0, name="mla_bwd"):
    T = q.shape[0]
    tq, tk = _pick(T, ATT_TQ), _pick(T, ATT_TK)
    nd = tq // tk
    npair = MLA_HEADS // 2
    scale = (MLA_NOPE + MLA_ROPE) ** -0.5

    def body(q_ref, c_ref, s_ref, kn_ref, v_ref, kr_ref, o_ref, lse_ref, do_ref, dq_ref, dkn_ref, dv_ref, dkr_ref,
             dkn_acc, dv_acc):
        p_idx, m_idx = pl.program_id(0), pl.program_id(1)

        @pl.when(m_idx == 0)
        def _():
            dkn_acc[...] = jnp.zeros_like(dkn_acc)
            dv_acc[...] = jnp.zeros_like(dv_acc)

        @pl.when((m_idx == 0) & (p_idx == 0))
        def _():
            dkr_ref[...] = jnp.zeros_like(dkr_ref)

        qv = _rotate(q_ref[...], c_ref[...], s_ref[...]).astype(BF16)
        m0, m1 = _mla_masks(qv.shape)
        qs = jnp.concatenate([jnp.where(m0, qv, 0), jnp.where(m1, qv, 0)], axis=0).astype(BF16)
        dov = do_ref[...].astype(F32)
        h0 = lax.broadcasted_iota(jnp.int32, (tq, LANES), 1) < HEAD
        dos32 = jnp.concatenate([jnp.where(h0, dov, 0.0), jnp.where(h0, 0.0, dov)], axis=0)
        ov = o_ref[...].astype(F32)
        delta = jnp.sum(dos32 * jnp.concatenate([ov, ov], axis=0), axis=-1, keepdims=True)
        dos = dos32.astype(BF16)
        lsev = lse_ref[...]
        lse = jnp.concatenate([lsev[:, 0:1], lsev[:, HEAD:HEAD + 1]], axis=0)

        def block(kb, dq, ok):
            ks = pl.ds(pl.multiple_of(kb * tk, tk), tk)
            kcat = jnp.concatenate([kn_ref[ks, :], kr_ref[ks, :]], axis=1)
            vv = v_ref[ks, :]
            p = jnp.exp(_dot(qs, kcat, NT) * scale - lse)
            if ok is not None:
                p = jnp.where(ok, p, 0.0)
            ds = (p * (_dot(dos, vv, NT) - delta) * scale).astype(BF16)
            dkc = _dot(ds, qs, TN)
            dkn_acc[ks, :] += dkc[:, :LANES]
            dkr_ref[ks, :] += dkc[:, LANES:]
            dv_acc[ks, :] += _dot(p.astype(BF16), dos, TN)
            return dq + _dot(ds, kcat, NN)

        dq = jnp.zeros((2 * tq, 2 * LANES), F32)
        for d in range(nd):
            dq = block(m_idx * nd + d, dq, _chunk_ok(tq, tk, d))
        dq = _by_twos(m_idx * nd, lambda kb, c: block(kb, c, None), dq)
        dq_ref[...] = _rotate(jnp.where(m0, dq[:tq], jnp.where(m1, dq[tq:], 0.0)), c_ref[...],
                              -s_ref[...]).astype(BF16)

        @pl.when(m_idx == T // tq - 1)
        def _():
            dkn_ref[...] = dkn_acc[...].astype(BF16)
            dv_ref[...] = dv_acc[...].astype(BF16)

    full = lambda col: pl.BlockSpec((T, LANES), col)
    blk = lambda col: pl.BlockSpec((tq, LANES), col)
    table = pl.BlockSpec((tq, 2 * LANES), lambda p, m: (m, 0))
    return pl.pallas_call(
        body, name=name, grid=(npair, T // tq),
        in_specs=[pl.BlockSpec((tq, 2 * LANES), lambda p, m: (m, p)), table, table,
                  full(lambda p, m: (0, p)), full(lambda p, m: (0, npair + p)), full(lambda p, m: (0, 0)),
                  blk(lambda p, m: (m, p)), blk(lambda p, m: (m, p)),
                  blk(lambda p, m: (m, do_col0 + p))],
        out_specs=[pl.BlockSpec((tq, 2 * LANES), lambda p, m: (m, p)),
                   full(lambda p, m: (0, p)), full(lambda p, m: (0, p)), full(lambda p, m: (0, 0))],
        out_shape=[jax.ShapeDtypeStruct((T, npair * 2 * LANES), BF16),
                   jax.ShapeDtypeStruct((T, npair * LANES), BF16),
                   jax.ShapeDtypeStruct((T, npair * LANES), BF16),
                   jax.ShapeDtypeStruct((T, LANES), F32)],
        scratch_shapes=[pltpu.VMEM((T, LANES), F32)] * 2,
        compiler_params=_params(("arbitrary", "arbitrary")),
    )(q, cos_q, sin_q, kv, kv, kr, o, lse, do)


def _split_dot(x, tri):
    hi = x.astype(BF16)
    lo = (x - hi.astype(F32)).astype(BF16)
    both = _dot(jnp.concatenate([hi, lo], axis=0), tri, NN)
    return both[:x.shape[0]] + both[x.shape[0]:]


def _sb_terms(qh, kk, before):
    z = _dot(qh, kk, NT)
    sp = jnp.maximum(z, 0.0) + jnp.log(1.0 + jnp.exp(-jnp.abs(z)))
    lk = -sp if before is None else jnp.where(before, -sp, 0.0)
    return z, sp, lk


def _sb_setup(q_ref, tq, tk, scale):
    qv = (q_ref[...].astype(F32) * scale).astype(BF16)
    lane = lax.broadcasted_iota(jnp.int32, (tq, LANES), 1)
    h0 = lane < HEAD
    qs = jnp.concatenate([jnp.where(h0, qv, 0), jnp.where(h0, 0, qv)], axis=0).astype(BF16)
    row = lax.broadcasted_iota(jnp.int32, (tk, tk), 0)
    col = lax.broadcasted_iota(jnp.int32, (tk, tk), 1)
    return qs, h0, row, col


def _sb_before(tq, tk, d):
    row = lax.broadcasted_iota(jnp.int32, (tq, tk), 0)
    col = lax.broadcasted_iota(jnp.int32, (tq, tk), 1) + d * tk
    return jnp.concatenate([col < row] * 2, axis=0)


def _two_heads(x, h0):
    tq = x.shape[0] // 2
    return jnp.where(h0, x[:tq], x[tq:])


def _sb_fwd(qkv, col0, dep, name="sb_fwd"):
    T = qkv.shape[0]
    tq, tk = _pick(T, ATT_TQ), _pick(T, ATT_TK)
    nd = tq // tk
    npair = SB_HEADS // 2
    scale = SB_DIM ** -0.5

    def body(q_ref, k_ref, v_ref, dep_ref, o_ref, o32_ref, w_ref, sp_ref):
        m_idx = pl.program_id(1)
        qs, h0, row, col = _sb_setup(q_ref, tq, tk, scale)
        later = (row > col).astype(BF16)

        def block(kb, carry, before):
            ks = pl.ds(pl.multiple_of(kb * tk, tk), tk)
            c, acc = carry
            z, sp, lk = _sb_terms(qs, k_ref[ks, :].astype(BF16), before)
            w = jnp.exp((z - sp) + _split_dot(lk, later) + c)
            if before is not None:
                w = jnp.where(before, w, 0.0)
            wb = w.astype(BF16)
            w_ref[0, 0, kb] = wb
            sp_ref[0, 0, kb] = sp.astype(BF16)
            return (c + jnp.sum(lk, axis=-1, keepdims=True), acc + _dot(wb, v_ref[ks, :].astype(BF16), NN))

        init = (jnp.zeros((2 * tq, 1), F32), jnp.zeros((2 * tq, LANES), F32))
        res = init
        for d in reversed(range(nd)):
            res = block(m_idx * nd + d, res, _sb_before(tq, tk, d))
        res = _by_twos(m_idx * nd, lambda i, c: block(m_idx * nd - 1 - i, c, None), res)
        o = _two_heads(res[1], h0)
        o_ref[...] = o.astype(o_ref.dtype)
        o32_ref[...] = o

    full = lambda col: pl.BlockSpec((T, LANES), col)
    blk = pl.BlockSpec((tq, LANES), lambda p, m: (m, p))
    return pl.pallas_call(
        body, name=name, grid=(npair, T // tq),
        in_specs=[pl.BlockSpec((tq, LANES), lambda p, m: (m, col0 + p)),
                  full(lambda p, m: (0, col0 + npair + p)), full(lambda p, m: (0, col0 + 2 * npair + p)),
                  pl.BlockSpec((8, LANES), lambda p, m: (0, 0))],
        out_specs=[blk, blk] + [pl.BlockSpec((1, 1, T // tk, 2 * tq, tk), lambda p, m: (p, m, 0, 0, 0))] * 2,
        out_shape=[jax.ShapeDtypeStruct((T, npair * LANES), BF16), jax.ShapeDtypeStruct((T, npair * LANES), F32)]
        + [jax.ShapeDtypeStruct((npair, T // tq, T // tk, 2 * tq, tk), BF16)] * 2,
        compiler_params=_params(("parallel", "arbitrary")),
    )(qkv, qkv, qkv, dep)


def _sb_bwd(qkv, col0, o32, w_all, sp_all, do, do_col0, dep, name="sb_bwd"):
    T = qkv.shape[0]
    tq, tk = _pick(T, ATT_TQ), _pick(T, ATT_TK)
    nd = tq // tk
    npair = SB_HEADS // 2
    scale = SB_DIM ** -0.5

    def body(q_ref, k_ref, v_ref, o_ref, w_ref, sp_ref, do_ref, dep_ref, dq_ref, dk_ref, dv_ref, dk_acc, dv_acc):
        m_idx = pl.program_id(1)

        @pl.when(m_idx == 0)
        def _():
            dk_acc[...] = jnp.zeros_like(dk_acc)
            dv_acc[...] = jnp.zeros_like(dv_acc)

        qs, h0, row, col = _sb_setup(q_ref, tq, tk, scale)
        dov = do_ref[...].astype(F32)
        dos = jnp.concatenate([jnp.where(h0, dov, 0.0), jnp.where(h0, 0.0, dov)], axis=0).astype(BF16)
        ov = o_ref[...]
        etot = jnp.sum(dos.astype(F32) * jnp.concatenate([ov, ov], axis=0), axis=-1, keepdims=True)
        from_here = (row >= col).astype(BF16)

        def block(kb, carry, before):
            ks = pl.ds(pl.multiple_of(kb * tk, tk), tk)
            kk = k_ref[ks, :].astype(BF16)
            vv = v_ref[ks, :].astype(BF16)
            es, dqa = carry
            wb = w_ref[0, 0, kb]
            e = wb.astype(F32) * _dot(dos, vv, NT)
            prev = etot - (_split_dot(e, from_here) + es)
            sig_neg = jnp.exp(-sp_ref[0, 0, kb].astype(F32))
            dz = e * sig_neg - (1.0 - sig_neg) * prev
            if before is not None:
                dz = jnp.where(before, dz, 0.0)
            dzb = dz.astype(BF16)
            dk_acc[ks, :] += _dot(dzb, qs, TN)
            dv_acc[ks, :] += _dot(wb, dos, TN)
            return es + jnp.sum(e, axis=-1, keepdims=True), dqa + _dot(dzb, kk, NN)

        init = (jnp.zeros((2 * tq, 1), F32), jnp.zeros((2 * tq, LANES), F32))
        res = init
        for d in reversed(range(nd)):
            res = block(m_idx * nd + d, res, _sb_before(tq, tk, d))
        res = _by_twos(m_idx * nd, lambda i, c: block(m_idx * nd - 1 - i, c, None), res)
        dq_ref[...] = (_two_heads(res[1], h0) * scale).astype(BF16)

        @pl.when(m_idx == T // tq - 1)
        def _():
            dk_ref[...] = dk_acc[...].astype(BF16)
            dv_ref[...] = dv_acc[...].astype(BF16)

    full = lambda col: pl.BlockSpec((T, LANES), col)
    blk = lambda col: pl.BlockSpec((tq, LANES), col)
    return pl.pallas_call(
        body, name=name, grid=(npair, T // tq),
        in_specs=[blk(lambda p, m: (m, col0 + p)),
                  full(lambda p, m: (0, col0 + npair + p)), full(lambda p, m: (0, col0 + 2 * npair + p)),
                  blk(lambda p, m: (m, p)),
                  pl.BlockSpec((1, 1, T // tk, 2 * tq, tk), lambda p, m: (p, m, 0, 0, 0)),
                  pl.BlockSpec((1, 1, T // tk, 2 * tq, tk), lambda p, m: (p, m, 0, 0, 0)),
                  blk(lambda p, m: (m, do_col0 + p)), pl.BlockSpec((8, LANES), lambda p, m: (0, 0))],
        out_specs=[blk(lambda p, m: (m, p)), full(lambda p, m: (0, p)), full(lambda p, m: (0, p))],
        out_shape=[jax.ShapeDtypeStruct((T, npair * LANES), BF16)] * 3,
        scratch_shapes=[pltpu.VMEM((T, LANES), F32)] * 2,
        compiler_params=_params(("arbitrary", "arbitrary")),
    )(qkv, qkv, qkv, o32, w_all, sp_all, do, dep)


def _band_in_window():
    cq = lax.broadcasted_iota(jnp.int32, (BAND_TQ, BAND_W), 0) >> CHUNK_BITS
    ckp = lax.broadcasted_iota(jnp.int32, (BAND_TQ, BAND_W), 1) >> CHUNK_BITS
    return (ckp >= cq) & (ckp <= cq + LEFT_CHUNKS)


def _band_real(m_idx):
    j = lax.broadcasted_iota(jnp.int32, (BAND_TQ, BAND_W), 1)
    return j >= PAD_KEYS - m_idx * BAND_TQ


def _band_probs(qh, kw, bias, real, scale):
    s = jnp.where(real, _dot(qh, kw, NT) * scale + bias, NEG)
    e = jnp.exp(s - jnp.max(s, axis=-1, keepdims=True))
    return e * (1.0 / jnp.sum(e, axis=-1, keepdims=True))


BAND_SUB = 4


def _band_fwd(qkv, k_pad, v_pad, bias_w, name="band_fwd"):
    T = qkv.shape[0]
    npair = C_HEADS // 2
    scale = C_DIM ** -0.5
    rows = BAND_SUB * BAND_TQ

    def body(q_ref, k_ref, v_ref, b_ref, o_ref, p_ref):
        lane = lax.broadcasted_iota(jnp.int32, (BAND_TQ, LANES), 1)
        h0 = lane < HEAD
        bias = jnp.concatenate([b_ref[0], b_ref[1]], axis=0)
        for sub in range(BAND_SUB):
            m_idx = pl.program_id(1) * BAND_SUB + sub
            win = pl.ds(pl.multiple_of(m_idx * BAND_TQ, BAND_TQ), BAND_W)
            kw, vw = k_ref[win, :], v_ref[win, :]
            qv = q_ref[sub * BAND_TQ:(sub + 1) * BAND_TQ, :]
            qs = jnp.concatenate([jnp.where(h0, qv, 0), jnp.where(h0, 0, qv)], axis=0).astype(BF16)
            p = _band_probs(qs, kw, bias, jnp.concatenate([_band_real(m_idx)] * 2, axis=0), scale).astype(BF16)
            p_ref[0, sub] = p
            o = _two_heads(_dot(p, vw, NN), h0)
            o_ref[sub * BAND_TQ:(sub + 1) * BAND_TQ, :] = o.astype(o_ref.dtype)

    Tp = T + PAD_KEYS
    return pl.pallas_call(
        body, name=name, grid=(npair, T // rows),
        in_specs=[pl.BlockSpec((rows, LANES), lambda p, m: (m, p)),
                  pl.BlockSpec((Tp, LANES), lambda p, m: (0, p)),
                  pl.BlockSpec((Tp, LANES), lambda p, m: (0, p)),
                  pl.BlockSpec((2, BAND_TQ, BAND_W), lambda p, m: (p, 0, 0))],
        out_specs=[pl.BlockSpec((rows, LANES), lambda p, m: (m, p)),
                   pl.BlockSpec((1, BAND_SUB, 2 * BAND_TQ, BAND_W), lambda p, m: (p, m, 0, 0))],
        out_shape=[jax.ShapeDtypeStruct((T, npair * LANES), BF16),
                   jax.ShapeDtypeStruct((npair, T // BAND_TQ, 2 * BAND_TQ, BAND_W), BF16)],
        compiler_params=_params(("parallel", "arbitrary")),
    )(qkv, k_pad, v_pad, bias_w)


def _band_bwd(qkv, k_pad, v_pad, probs, do, name="band_bwd"):
    T = qkv.shape[0]
    npair = C_HEADS // 2
    scale = C_DIM ** -0.5

    rows = BAND_SUB * BAND_TQ

    def body(q_ref, k_ref, v_ref, p_ref, do_ref, dq_ref, dk_ref, dv_ref, db_ref, dk_acc, dv_acc):
        @pl.when(pl.program_id(1) == 0)
        def _():
            dk_acc[...] = jnp.zeros_like(dk_acc)
            dv_acc[...] = jnp.zeros_like(dv_acc)
            db_ref[...] = jnp.zeros_like(db_ref)

        lane = lax.broadcasted_iota(jnp.int32, (BAND_TQ, LANES), 1)
        h0 = lane < HEAD
        dbs = jnp.zeros((2 * BAND_TQ, BAND_W), F32)
        for sub in range(BAND_SUB):
            m_idx = pl.program_id(1) * BAND_SUB + sub
            win = pl.ds(pl.multiple_of(m_idx * BAND_TQ, BAND_TQ), BAND_W)
            kw, vw = k_ref[win, :], v_ref[win, :]
            qv = q_ref[sub * BAND_TQ:(sub + 1) * BAND_TQ, :]
            dov = do_ref[sub * BAND_TQ:(sub + 1) * BAND_TQ, :].astype(F32)
            qs = jnp.concatenate([jnp.where(h0, qv, 0), jnp.where(h0, 0, qv)], axis=0).astype(BF16)
            dos = jnp.concatenate([jnp.where(h0, dov, 0.0), jnp.where(h0, 0.0, dov)], axis=0).astype(BF16)
            pb = p_ref[0, sub]
            p = pb.astype(F32)
            dp = _dot(dos, vw, NT)
            dsb = p * (dp - jnp.sum(p * dp, axis=-1, keepdims=True))
            dbs = dbs + dsb
            dsq = (dsb * scale).astype(BF16)
            dq_ref[sub * BAND_TQ:(sub + 1) * BAND_TQ, :] = _two_heads(_dot(dsq, kw, NN), h0).astype(BF16)
            dk_acc[win, :] += _dot(dsq, qs, TN)
            dv_acc[win, :] += _dot(pb, dos, TN)
        db_ref[0] += dbs[:BAND_TQ]
        db_ref[1] += dbs[BAND_TQ:]

        @pl.when(pl.program_id(1) == T // rows - 1)
        def _():
            dk_ref[...] = dk_acc[...].astype(BF16)
            dv_ref[...] = dv_acc[...].astype(BF16)

    Tp = T + PAD_KEYS
    blk = lambda col: pl.BlockSpec((rows, LANES), col)
    full = pl.BlockSpec((Tp, LANES), lambda p, m: (0, p))
    bias = pl.BlockSpec((2, BAND_TQ, BAND_W), lambda p, m: (p, 0, 0))
    prob = pl.BlockSpec((1, BAND_SUB, 2 * BAND_TQ, BAND_W), lambda p, m: (p, m, 0, 0))
    return pl.pallas_call(
        body, name=name, grid=(npair, T // rows),
        in_specs=[blk(lambda p, m: (m, p)), full, full, prob, blk(lambda p, m: (m, p))],
        out_specs=[blk(lambda p, m: (m, p)), full, full, bias],
        out_shape=[jax.ShapeDtypeStruct((T, npair * LANES), BF16),
                   jax.ShapeDtypeStruct((Tp, npair * LANES), BF16),
                   jax.ShapeDtypeStruct((Tp, npair * LANES), BF16),
                   jax.ShapeDtypeStruct((C_HEADS, BAND_TQ, BAND_W), F32)],
        scratch_shapes=[pltpu.VMEM((Tp, LANES), F32)] * 2,
        compiler_params=_params(("arbitrary", "arbitrary")),
    )(qkv, k_pad, v_pad, probs, do)


def _skew_bits(x, left):
    w = x.shape[1]
    row = lax.broadcasted_iota(jnp.int32, x.shape, 0)
    for b in range(BAND_TQ.bit_length() - 1):
        amt = (w - (1 << b)) if left else (1 << b)
        x = jnp.where((row >> b) & 1 == 1, pltpu.roll(x, amt, 1), x)
    return x


def _toeplitz(diag, name="toeplitz"):
    H = diag.shape[0]

    def body(d_ref, o_ref):
        x = jnp.broadcast_to(d_ref[0], (BAND_TQ, TOEP_W))
        o_ref[0] = jnp.where(_band_in_window(), _skew_bits(x, left=False)[:, BAND_TQ:], NEG)

    return pl.pallas_call(
        body, name=name, grid=(H,),
        in_specs=[pl.BlockSpec((1, 1, TOEP_W), lambda h: (h, 0, 0))],
        out_specs=pl.BlockSpec((1, BAND_TQ, BAND_W), lambda h: (h, 0, 0)),
        out_shape=jax.ShapeDtypeStruct((H, BAND_TQ, BAND_W), F32),
        compiler_params=_params(("parallel",)),
    )(diag.reshape(H, 1, TOEP_W))


def _toeplitz_bwd(dbias, name="toeplitz_bwd"):
    H = dbias.shape[0]

    def body(d_ref, o_ref):
        x = jnp.concatenate([jnp.zeros((BAND_TQ, BAND_TQ), F32), d_ref[0]], axis=1)
        h = BAND_TQ // 2
        while h >= 8:
            x = x[:h] + pltpu.roll(x[h:2 * h], TOEP_W - h, 1)
            h //= 2
        o_ref[0] = jnp.sum(_skew_bits(x, left=True), axis=0, keepdims=True)

    return pl.pallas_call(
        body, name=name, grid=(H,),
        in_specs=[pl.BlockSpec((1, BAND_TQ, BAND_W), lambda h: (h, 0, 0))],
        out_specs=pl.BlockSpec((1, 1, TOEP_W), lambda h: (h, 0, 0)),
        out_shape=jax.ShapeDtypeStruct((H, 1, TOEP_W), F32),
        compiler_params=_params(("parallel",)),
    )(dbias).reshape(H, TOEP_W)


_HBM = pl.BlockSpec(memory_space=pltpu.HBM)
_SEM = pl.BlockSpec(memory_space=pltpu.SEMAPHORE)
_EFFECT = pltpu.SideEffectType.DATAFLOW_SIDE_EFFECTING


def _peers():
    x, y, c = lax.axis_index("x"), lax.axis_index("y"), lax.axis_index("c")
    out = []
    for k in range(1, N_DEV):
        peer = (1 - x if (k >> 2) & 1 else x, 1 - y if (k >> 1) & 1 else y, 1 - c if k & 1 else c)
        out.append((peer, 4 * peer[0] + 2 * peer[1] + peer[2]))
    return 4 * x + 2 * y + c, out


def _split_copies(ins, lands, scatter, send_sem, recv_sem, arriving):
    me, peers = _peers()
    out = []
    for a in range(len(ins)):
        for peer, idx in peers:
            out.append(pltpu.make_async_remote_copy(
                src_ref=ins[a].at[idx] if scatter[a] else ins[a],
                dst_ref=lands[a].at[idx if arriving else me], send_sem=send_sem, recv_sem=recv_sem,
                device_id=peer, device_id_type=pl.DeviceIdType.MESH))
    return out


def _landing_zones(arrays, scatter):
    return [lax.empty((N_DEV,) + (a.shape[1:] if s else a.shape), a.dtype) for a, s in zip(arrays, scatter)]


def _place_own(arrays, scatter, name):
    n = len(arrays)
    lands = _landing_zones(arrays, scatter)
    me = (4 * lax.axis_index("x") + 2 * lax.axis_index("y") + lax.axis_index("c")).astype(jnp.int32).reshape(1)

    def body(me_ref, *refs):
        for a in range(n):
            refs[2 * n + a][...] = refs[a][...].reshape(refs[2 * n + a].shape)

    def row_spec(shape):
        zeros = (0,) * (len(shape) - 1)
        return pl.BlockSpec((1,) + tuple(shape[1:]), lambda i, me_ref: (me_ref[0],) + zeros)

    in_specs = [row_spec(a.shape) if s else pl.BlockSpec(a.shape, lambda i, me_ref, nd=a.ndim: (0,) * nd)
                for a, s in zip(arrays, scatter)]
    return pl.pallas_call(
        body, name=name,
        out_shape=[jax.ShapeDtypeStruct(l.shape, l.dtype) for l in lands],
        grid_spec=pltpu.PrefetchScalarGridSpec(
            num_scalar_prefetch=1, grid=(1,),
            in_specs=in_specs + [pl.BlockSpec(memory_space=pl.ANY)] * n,
            out_specs=[row_spec(l.shape) for l in lands]),
        input_output_aliases={1 + n + i: i for i in range(n)},
        compiler_params=_params(("arbitrary",)),
    )(me, *arrays, *lands)


def _exchange_start_groups(groups, scatter, after, name, lands=None):
    sizes = [len(g) for g in groups]
    arrays = [a for g in groups for a in g]
    n, ng = len(arrays), len(groups)
    flags = list(scatter) if isinstance(scatter, (list, tuple)) else [scatter] * n
    if lands is None:
        lands = list(_place_own(arrays, flags, name=name.replace("_start_", "_own_")))
    else:
        lands = [l for g in lands for l in g]
    starts = np.cumsum([0] + sizes)

    def body(*refs):
        ins, lnd = refs[:n], refs[n:2 * n]
        sems = refs[2 * n + 1:2 * n + 1 + 2 * ng]
        token = refs[-1]
        for g in range(ng):
            sl = slice(starts[g], starts[g + 1])
            for cp in _split_copies(ins[sl], lnd[sl], flags[sl], sems[2 * g], sems[2 * g + 1], arriving=False):
                cp.start()
        token[...] = jnp.zeros_like(token)

    hbm = lambda a: pltpu.HBM(a.shape, a.dtype)
    out = pl.pallas_call(
        body, name=name,
        out_shape=(*[pltpu.SemaphoreType.DMA(())] * (2 * ng),
                   *[hbm(a) for a in arrays], *[hbm(a) for a in lands],
                   jax.ShapeDtypeStruct((8, LANES), F32)),
        in_specs=[_HBM] * (2 * n) + [pl.BlockSpec(memory_space=pl.ANY)],
        out_specs=(*[_SEM] * (2 * ng), *([_HBM] * (2 * n)), pl.BlockSpec(memory_space=pltpu.VMEM)),
        input_output_aliases={i: 2 * ng + i for i in range(2 * n)},
        compiler_params=pltpu.CompilerParams(has_side_effects=_EFFECT),
    )(*[pltpu.with_memory_space_constraint(a, pltpu.HBM) for a in list(arrays) + lands], after)
    ins_out, lands_out = out[2 * ng:2 * ng + n], out[2 * ng + n:2 * ng + 2 * n]
    handles = [(out[2 * g], out[2 * g + 1], list(ins_out[starts[g]:starts[g + 1]]),
                list(lands_out[starts[g]:starts[g + 1]]), tuple(flags[starts[g]:starts[g + 1]]))
               for g in range(ng)]
    return handles, out[-1]


def _exchange_start(arrays, scatter, after, name):
    handles, token = _exchange_start_groups([list(arrays)], list(scatter), after, name)
    return handles[0], token


def _exchange_wait(handle, after, name):
    send_sem, recv_sem, ins, lands, scatter = handle
    n = len(ins)
    after = after if isinstance(after, tuple) else (after,)

    def body(*refs):
        i_ref, l_ref = refs[:n], refs[n:2 * n]
        s_sem, r_sem = refs[2 * n:2 * n + 2]
        for cp in _split_copies(i_ref, l_ref, scatter, s_sem, r_sem, arriving=False):
            cp.wait_send()
        for cp in _split_copies(i_ref, l_ref, scatter, s_sem, r_sem, arriving=True):
            cp.wait_recv()

    hbm = lambda a: pltpu.HBM(a.shape, a.dtype)
    out = pl.pallas_call(
        body, name=name,
        out_shape=tuple(hbm(a) for a in ins + lands),
        in_specs=[_HBM] * (2 * n) + [_SEM, _SEM] + [pl.BlockSpec(memory_space=pl.ANY)] * len(after),
        out_specs=tuple([_HBM] * (2 * n)),
        input_output_aliases={i: i for i in range(2 * n)},
        compiler_params=pltpu.CompilerParams(has_side_effects=_EFFECT),
    )(*ins, *lands, send_sem, recv_sem, *after)
    return list(out[n:])


_SIBLING = 1
_CHIPS = (4, 2, 6)


def _peer_of(k):
    x, y, c = lax.axis_index("x"), lax.axis_index("y"), lax.axis_index("c")
    peer = (1 - x if (k >> 2) & 1 else x, 1 - y if (k >> 1) & 1 else y, 1 - c if k & 1 else c)
    return peer, 4 * peer[0] + 2 * peer[1] + peer[2]


def _rcopy(src, dst, send_sem, recv_sem, k):
    return pltpu.make_async_remote_copy(src_ref=src, dst_ref=dst, send_sem=send_sem, recv_sem=recv_sem,
                                        device_id=_peer_of(k)[0], device_id_type=pl.DeviceIdType.MESH)


def _gather2_start(groups, lands, after, name):
    sizes = [len(g) for g in groups]
    arrays = [a for g in groups for a in g]
    lands = [l for g in lands for l in g]
    n, ng = len(arrays), len(groups)
    starts = np.cumsum([0] + sizes)

    def body(*refs):
        ins, lnd = refs[:n], refs[n:2 * n]
        sems = refs[2 * n + 1:2 * n + 1 + 4 * ng]
        me, _ = _peers()
        for g in range(ng):
            send_d, recv_d, send_i, recv_i = sems[4 * g:4 * g + 4]
            for a in range(starts[g], starts[g + 1]):
                for k in _CHIPS:
                    _rcopy(ins[a], lnd[a].at[me], send_i, recv_i, k).start()
                _rcopy(ins[a], lnd[a].at[me], send_d, recv_d, _SIBLING).start()
        refs[-1][...] = jnp.zeros_like(refs[-1])

    hbm = lambda a: pltpu.HBM(a.shape, a.dtype)
    out = pl.pallas_call(
        body, name=name,
        out_shape=(*[pltpu.SemaphoreType.DMA(())] * (4 * ng), *[hbm(a) for a in arrays], *[hbm(a) for a in lands],
                   jax.ShapeDtypeStruct((8, LANES), F32)),
        in_specs=[_HBM] * (2 * n) + [pl.BlockSpec(memory_space=pl.ANY)],
        out_specs=(*[_SEM] * (4 * ng), *([_HBM] * (2 * n)), pl.BlockSpec(memory_space=pltpu.VMEM)),
        input_output_aliases={i: 4 * ng + i for i in range(2 * n)},
        compiler_params=pltpu.CompilerParams(has_side_effects=_EFFECT),
    )(*[pltpu.with_memory_space_constraint(a, pltpu.HBM) for a in arrays + lands], after)
    ins_out, lands_out = out[4 * ng:4 * ng + n], out[4 * ng + n:4 * ng + 2 * n]
    handles = [dict(sems=out[4 * g:4 * g + 4], ins=list(ins_out[starts[g]:starts[g + 1]]),
                    lands=list(lands_out[starts[g]:starts[g + 1]])) for g in range(ng)]
    return handles, out[-1]


def _gather2_pass_on(handle, after, name):
    lands, recv_i = handle["lands"], handle["sems"][3]
    n = len(lands)
    after = after if isinstance(after, tuple) else (after,)

    def body(*refs):
        lnd, r_i = refs[:n], refs[n]
        send_f, recv_f = refs[n + 1 + len(after):n + 3 + len(after)]
        for a in range(n):
            for k in _CHIPS:
                row = _peer_of(k)[1]
                _rcopy(lnd[a].at[row], lnd[a].at[row], send_f, r_i, k).wait_recv()
        for a in range(n):
            for k in _CHIPS:
                row = _peer_of(k)[1]
                _rcopy(lnd[a].at[row], lnd[a].at[row], send_f, recv_f, _SIBLING).start()
        refs[-1][...] = jnp.zeros_like(refs[-1])

    hbm = lambda a: pltpu.HBM(a.shape, a.dtype)
    out = pl.pallas_call(
        body, name=name,
        out_shape=(pltpu.SemaphoreType.DMA(()), pltpu.SemaphoreType.DMA(()), *[hbm(a) for a in lands],
                   jax.ShapeDtypeStruct((8, LANES), F32)),
        in_specs=[_HBM] * n + [_SEM] + [pl.BlockSpec(memory_space=pl.ANY)] * len(after),
        out_specs=(_SEM, _SEM, *([_HBM] * n), pl.BlockSpec(memory_space=pltpu.VMEM)),
        input_output_aliases={i: 2 + i for i in range(n)},
        compiler_params=pltpu.CompilerParams(has_side_effects=_EFFECT),
    )(*lands, recv_i, *after)
    return dict(handle, lands=list(out[2:2 + n]), passed=(out[0], out[1])), out[-1]


def _gather2_wait(handle, after, name):
    ins, lands = handle["ins"], handle["lands"]
    send_d, recv_d, send_i, _ = handle["sems"]
    send_f, recv_f = handle["passed"]
    n = len(ins)
    after = after if isinstance(after, tuple) else (after,)

    def body(*refs):
        i_ref, lnd = refs[:n], refs[n:2 * n]
        s_d, r_d, s_i, s_f, r_f = refs[2 * n:2 * n + 5]
        me, _ = _peers()
        sib = _peer_of(_SIBLING)[1]
        for a in range(n):
            _rcopy(i_ref[a], lnd[a].at[sib], s_d, r_d, _SIBLING).wait_send()
            _rcopy(i_ref[a], lnd[a].at[sib], s_d, r_d, _SIBLING).wait_recv()
            for k in _CHIPS:
                row = _peer_of(k)[1]
                _rcopy(i_ref[a], lnd[a].at[me], s_i, r_d, k).wait_send()
                _rcopy(lnd[a].at[row], lnd[a].at[row], s_f, r_f, _SIBLING).wait_send()
                _rcopy(lnd[a].at[row], lnd[a].at[_peer_of(k ^ _SIBLING)[1]], s_f, r_f, _SIBLING).wait_recv()

    hbm = lambda a: pltpu.HBM(a.shape, a.dtype)
    out = pl.pallas_call(
        body, name=name,
        out_shape=tuple(hbm(a) for a in ins + lands),
        in_specs=[_HBM] * (2 * n) + [_SEM] * 5 + [pl.BlockSpec(memory_space=pl.ANY)] * len(after),
        out_specs=tuple([_HBM] * (2 * n)),
        input_output_aliases={i: i for i in range(2 * n)},
        compiler_params=pltpu.CompilerParams(has_side_effects=_EFFECT),
    )(*ins, *lands, send_d, recv_d, send_i, send_f, recv_f, *after)
    return list(out[n:])


def _adamw(w, parts, m, v, name="adamw"):
    R, C = w.shape
    L = len(parts)
    rl = R // L
    tr = max([t for t in range(16, 257, 16) if rl % t == 0], default=rl)
    nb = rl // tr
    c1 = 1.0 - ADAM_B1 ** ADAM_STEP
    c2 = 1.0 - ADAM_B2 ** ADAM_STEP

    def body(*refs):
        w_ref, p_refs, (m_ref, v_ref, g_ref, d_ref, nm_ref, nv_ref) = refs[0], refs[1:1 + L], refs[1 + L:]
        g = None
        for j, p_ref in enumerate(p_refs):
            gj = p_ref[0].astype(F32)
            for i in range(1, N_DEV):
                gj = gj + p_ref[i].astype(F32)
            g = gj if g is None else jnp.where(pl.program_id(0) == j, gj, g)
        nm = ADAM_B1 * m_ref[...] + (1.0 - ADAM_B1) * g
        nv = ADAM_B2 * v_ref[...] + (1.0 - ADAM_B2) * (g * g)
        g_ref[...] = g
        nm_ref[...] = nm
        nv_ref[...] = nv
        d_ref[...] = -ADAM_LR * ((nm / c1) / (jnp.sqrt(nv / c2) + ADAM_EPS) + ADAM_WD * w_ref[...])

    blk = pl.BlockSpec((tr, C), lambda l, i: (l * nb + i, 0))
    part = lambda j: pl.BlockSpec((N_DEV, tr, C), lambda l, i: (0, jnp.where(l == j, i, 0), 0))
    return pl.pallas_call(
        body, name=name, grid=(L, nb),
        in_specs=[blk] + [part(j) for j in range(L)] + [blk, blk],
        out_specs=[blk] * 4,
        out_shape=[jax.ShapeDtypeStruct((R, C), F32)] * 4,
        compiler_params=_params(("arbitrary", "arbitrary")),
    )(w, *parts, m, v)


_O1 = Q_LORA
_O2 = _O1 + KV_LORA
_O3 = _O2 + MLA_ROPE
_NB = SB_HEADS * SB_DIM
IN_W = _O2 + LANES + 3 * _NB
COL_KR = _O2 // LANES
COL_SB = COL_KR + 1


def _w_in_local(w):
    kr = w[_O2:_O3]
    pad = jnp.zeros((LANES - 2 * MLA_ROPE, w.shape[1]), w.dtype)
    return jnp.concatenate([w[:_O2], kr, kr, pad, w[_O3:]], axis=0)


def _w_in_grad(g):
    kr = (g[_O2:_O2 + MLA_ROPE].astype(F32) + g[_O2 + MLA_ROPE:_O2 + 2 * MLA_ROPE].astype(F32)).astype(g.dtype)
    return jnp.concatenate([g[:_O2], kr, g[_O2 + LANES:]], axis=0)


def _w_uq_local(w):
    w3 = w.reshape(MLA_HEADS // 2, 2, MLA_NOPE + MLA_ROPE, w.shape[1])
    nope = w3[:, :, :MLA_NOPE].reshape(MLA_HEADS // 2, 2 * MLA_NOPE, w.shape[1])
    rope = w3[:, :, MLA_NOPE:].reshape(MLA_HEADS // 2, 2 * MLA_ROPE, w.shape[1])
    pad = jnp.zeros((MLA_HEADS // 2, LANES - 2 * MLA_ROPE, w.shape[1]), w.dtype)
    return jnp.concatenate([nope, rope, pad], axis=1).reshape(-1, w.shape[1])


def _w_uq_grad(g):
    g3 = g.reshape(MLA_HEADS // 2, 2 * LANES, g.shape[1])
    nope = g3[:, :2 * MLA_NOPE].reshape(MLA_HEADS // 2, 2, MLA_NOPE, g.shape[1])
    rope = g3[:, LANES:LANES + 2 * MLA_ROPE].reshape(MLA_HEADS // 2, 2, MLA_ROPE, g.shape[1])
    return jnp.concatenate([nope, rope], axis=2).reshape(-1, g.shape[1])


def _w_ukv_local(w):
    w3 = w.reshape(MLA_HEADS, MLA_NOPE + MLA_V, w.shape[1])
    return jnp.concatenate([w3[:, :MLA_NOPE].reshape(-1, w.shape[1]),
                            w3[:, MLA_NOPE:].reshape(-1, w.shape[1])], axis=0)


def _w_ukv_grad(g):
    half = MLA_HEADS * MLA_NOPE
    kn = g[:half].reshape(MLA_HEADS, MLA_NOPE, g.shape[1])
    vv = g[half:].reshape(MLA_HEADS, MLA_V, g.shape[1])
    return jnp.concatenate([kn, vv], axis=1).reshape(-1, g.shape[1])


def _rope_tables(T):
    pos = jnp.arange(T, dtype=F32)
    inv_freq = ROPE_THETA ** (-jnp.arange(0, MLA_ROPE, 2, dtype=F32) / MLA_ROPE)
    ang = pos[:, None] * inv_freq[None, :]
    cos, sin = jnp.cos(ang), jnp.sin(ang)
    ones = jnp.ones((T, LANES - 2 * MLA_ROPE), F32)
    cos_k = jnp.concatenate([cos, cos, cos, cos, ones], axis=1)
    sin_k = jnp.concatenate([-sin, sin, -sin, sin, 0.0 * ones], axis=1)
    cos_q = jnp.concatenate([jnp.ones((T, LANES), F32), cos_k], axis=1)
    sin_q = jnp.concatenate([jnp.zeros((T, LANES), F32), sin_k], axis=1)
    return cos_q, sin_q, cos_k, sin_k


def _bias_diag_index():
    ell = np.arange(TOEP_W)
    return np.clip(BAND_W - ell, -REL_CLIP, REL_CLIP) + REL_CLIP


def _local_step(x, target, small, get_weights, put_grads, prefetch):
    T = x.shape[0]
    cos_q, sin_q, cos_k, sin_k = _rope_tables(T)
    G = {}
    W = dict(small)

    u0 = _rms_fwd(x, W["g_mix"][0:1], name="rms_mix0")
    bias_w = _toeplitz(W["od_rel_bias"][:, _bias_diag_index()])
    W.update(get_weights("in0", (u0, bias_w)))
    proj = _mm(u0, W["w_in_t"], dims="nt", name="proj_in")
    W.update(get_weights("mix0", proj))
    c_q, c_kv = proj[:, :_O1], proj[:, _O1:_O2]
    nq = _rms_fwd(c_q, W["g_cq"], name="rms_cq")
    nkv = _rms_fwd(c_kv, W["g_ckv"], name="rms_ckv")
    qa_raw = _mm(nq, W["w_uq_t"], dims="nt", name="proj_uq")
    kv = _mm(nkv, W["w_ukv_t"], dims="nt", out_dtype=BF16, name="proj_ukv")
    kr = _rope(proj, cos_k, sin_k, COL_KR, 1, BF16, name="rope_k")
    o_a, lse = _mla_fwd(qa_raw, cos_q, sin_q, kv, kr)
    o_b, o_b32, w_b, sp_b = _sb_fwd(proj, COL_SB, prefetch("ffn0", o_a))
    o_ab = jnp.concatenate([o_a, o_b], axis=1)
    h1 = _mm(o_ab, W["ev_w_out"], res=x, name="out_ev")

    def ffn_fwd(h, layer):
        W.update(get_weights(f"ffn{layer}", h))
        return _ffn_fwd(h, W["g_ffn"][layer:layer + 1], W[f"w_gate_t{layer}"], W[f"w_up_t{layer}"],
                        W[f"w_down{layer}"], name=f"ffn_fwd{layer}")

    h2, u1, a0, b0 = ffn_fwd(h1, 0)

    W.update(get_weights("mix1", h2))
    u2 = _rms_fwd(h2, W["g_mix"][1:2], name="rms_mix1")
    qkv = _mm(u2, W["od_w_qkv_t"], dims="nt", out_dtype=BF16, name="proj_qkv")
    nc = C_HEADS * C_DIM
    pad = ((PAD_KEYS, 0), (0, 0))
    k_pad, v_pad = jnp.pad(qkv[:, nc:2 * nc], pad), jnp.pad(qkv[:, 2 * nc:], pad)
    o_c, p_c = _band_fwd(qkv, k_pad, v_pad, bias_w)
    h3 = _mm(o_c, W["od_w_out"], res=h2, name="out_od")
    h4, u3, a1, b1 = ffn_fwd(h3, 1)

    loss, dh, dhb, G["g_final"] = _loss_head(h4, W["g_final"], target)

    def ffn_bwd(dh, dhb, h, u, a, b, layer):
        du, g_gate, g_up, g_down = _ffn_bwd(dhb, u, a, b, W[f"w_gate_t{layer}"], W[f"w_up_t{layer}"],
                                            W[f"w_down{layer}"], name=f"ffn_bwd{layer}")
        tok = put_grads(f"ffn{layer}", {"w_gate_t": g_gate, "w_up_t": g_up, "w_down": g_down})
        return _rms_bwd(h, W["g_ffn"][layer:layer + 1] + tok[:1, :1], du, dres=dh, name=f"rms_ffn_bwd{layer}")

    dh3, dh3b, g_gffn1 = ffn_bwd(dh, dhb, h3, u3, a1, b1, 1)

    do_c = _mm(dh3b, W["od_w_out"], dims="nt", name="out_od_dx")
    g_od_out = _mm(o_c, dh3b, dims="tn", out_dtype=BF16, name="out_od_dw")
    dq_c, dk_p, dv_p, dbias_w = _band_bwd(qkv, k_pad, v_pad, p_c, do_c)
    dqkv = jnp.concatenate([dq_c, dk_p[PAD_KEYS:], dv_p[PAD_KEYS:]], axis=1)
    tok = put_grads("mix1", {"od_w_qkv_t": _mm(dqkv, u2, dims="tn", out_dtype=BF16, name="proj_qkv_dw"),
                             "od_w_out": g_od_out})
    ddiag = _toeplitz_bwd(dbias_w)
    n_far = BAND_W - REL_CLIP + 1
    G["od_rel_bias"] = jnp.concatenate(
        [jnp.zeros((C_HEADS, REL_CLIP - BAND_TQ + 1), F32), ddiag[:, n_far:][:, ::-1],
         jnp.sum(ddiag[:, :n_far], axis=1, keepdims=True)], axis=1)
    dh2, dh2b, g_gmix1 = _mm_rms_bwd(dqkv, W["od_w_qkv_t"], h2, W["g_mix"][1:2] + tok[:1, :1], dh3,
                                     name="proj_qkv_dx")

    dh1, dh1b, g_gffn0 = ffn_bwd(dh2, dh2b, h1, u1, a0, b0, 0)
    G["g_ffn"] = jnp.concatenate([g_gffn0, g_gffn1], axis=0)

    do_ab = _mm(dh1b, W["ev_w_out"], dims="nt", name="out_ev_dx")
    g0 = {"ev_w_out": _mm(o_ab, dh1b, dims="tn", out_dtype=BF16, name="out_ev_dw")}
    dqa_raw, dkn, dva, dkr = _mla_bwd(qa_raw, cos_q, sin_q, kv, kr, o_a, lse, do_ab, 0)
    dlat, g0["w_uq_t"], g0["w_ukv_t"], G["g_cq"], G["g_ckv"] = _latent_bwd(
        proj, nq, nkv, dqa_raw, dkn, dva, dkr, cos_k, sin_k, W["g_cq"], W["g_ckv"], W["w_uq_t"], W["w_ukv_t"])
    tok = put_grads("mix0", g0)
    dqb, dkb, dvb = _sb_bwd(proj, COL_SB, o_b32, w_b, sp_b, do_ab, MLA_HEADS // 2, tok)
    dproj = jnp.concatenate([dlat, dqb, dkb, dvb], axis=1)
    tok = put_grads("in0", {"w_in_t": _mm(dproj, u0, dims="tn", name="proj_in_dw")})
    dx, _, g_gmix0 = _mm_rms_bwd(dproj, W["w_in_t"], x, W["g_mix"][0:1] + tok[:1, :1], dh1, name="proj_in_dx")
    G["g_mix"] = jnp.concatenate([g_gmix0, g_gmix1], axis=0)
    return loss[0, 0], dx, G


_BIG = ["ev_w_in", "ev_w_uq", "ev_w_ukv", "ev_w_out", "od_w_qkv", "od_w_out", "w_gate", "w_up", "w_down"]
_COL_SHARDED = {"ev_w_in", "ev_w_uq", "ev_w_ukv", "od_w_qkv", "w_gate", "w_up"}
_SMALL = ["ev_g_cq", "ev_g_ckv", "od_rel_bias", "g_mix", "g_ffn", "g_final"]
_GROUPS = {
    "in0": ["ev_w_in"],
    "mix0": ["ev_w_uq", "ev_w_ukv", "ev_w_out"],
    "ffn0": ["w_gate0", "w_up0", "w_down0"],
    "mix1": ["od_w_qkv", "od_w_out"],
    "ffn1": ["w_gate1", "w_up1", "w_down1"],
}
_GROUP_SRC = {n + str(l): (n, l) for n in ("w_gate", "w_up", "w_down") for l in (0, 1)}
_BATCHES = {"in0": ["in0"], "layer0": ["mix0", "ffn0"], "layer1": ["mix1", "ffn1"]}
_BATCH_OF = {grp: batch for batch, grps in _BATCHES.items() for grp in grps}
_SMALL_ROWS = 8
_SMALL_COLS = 1792


def _pack_small(vals):
    flat = jnp.concatenate([v.reshape(-1).astype(F32) for v in vals])
    flat = jnp.pad(flat, (0, _SMALL_ROWS * _SMALL_COLS - flat.shape[0]))
    return flat.reshape(_SMALL_ROWS, _SMALL_COLS)


def _unpack_small(packed, like):
    flat = packed.reshape(-1)
    out, off = [], 0
    for v in like:
        out.append(flat[off:off + v.size].reshape(v.shape))
        off += v.size
    return out


def kernel(x, ev_w_in, ev_g_cq, ev_w_uq, ev_g_ckv, ev_w_ukv, ev_w_out, od_w_qkv, od_rel_bias, od_w_out, g_mix, g_ffn, w_gate, w_up, w_down, g_final, loss_target, m_ev_w_in, m_ev_g_cq, m_ev_w_uq, m_ev_g_ckv, m_ev_w_ukv, m_ev_w_out, m_od_w_qkv, m_od_rel_bias, m_od_w_out, m_g_mix, m_g_ffn, m_w_gate, m_w_up, m_w_down, m_g_final, v_ev_w_in, v_ev_g_cq, v_ev_w_uq, v_ev_g_ckv, v_ev_w_ukv, v_ev_w_out, v_od_w_qkv, v_od_rel_bias, v_od_w_out, v_g_mix, v_g_ffn, v_w_gate, v_w_up, v_w_down, v_g_final):
    args = dict(locals())
    w = {n: args[n] for n in _BIG + _SMALL}
    mom = {n: args["m_" + n] for n in _BIG + _SMALL}
    var = {n: args["v_" + n] for n in _BIG + _SMALL}

    own = {}
    for grp, names in _GROUPS.items():
        for n in names:
            base, layer = _GROUP_SRC.get(n, (n, 0))
            shard = w[base][layer:layer + 1]
            own[n] = (jnp.swapaxes(shard, 1, 2) if base in _COL_SHARDED else shard).astype(BF16)
    placed = dict(zip(own, _place_own(list(own.values()), [False] * len(own), name="gather_own")))
    handles, token = _gather2_start(
        [[own[n] for n in names] for names in _GROUPS.values()],
        [[placed[n] for n in names] for names in _GROUPS.values()], x[0, :8, :LANES], name="gather_start")
    gather = dict(zip(_GROUPS, handles))
    pass_before = {"in0": ["in0"], "mix0": ["mix0"]}
    pass_after = {"ffn0": ("mix1", "g_ffn"), "mix1": ("ffn1", "g_mix")}

    def prefetch(grp, after):
        gather[grp], tok = _gather2_pass_on(gather[grp], after, name="gather_pass_" + grp)
        return tok

    def get_weights(grp, after):
        names = _GROUPS[grp]
        after = token if after is None else after
        for g in pass_before.get(grp, []):
            gather[g], _ = _gather2_pass_on(gather[g], after, name="gather_pass_" + g)
        lands = _gather2_wait(gather[grp], after, name="gather_wait_" + grp)
        full = {n: l.reshape(-1, l.shape[-1]) for n, l in zip(names, lands)}
        out = {}
        if grp in pass_after:
            g, gain = pass_after[grp]
            gather[g], tok = _gather2_pass_on(gather[g], lands[0], name="gather_pass_" + g)
            out[gain] = small[gain] + tok[:1, :1]
        if grp == "in0":
            out.update({"w_in_t": _w_in_local(full["ev_w_in"])})
        elif grp == "mix0":
            out.update({"w_uq_t": _w_uq_local(full["ev_w_uq"]), "w_ukv_t": _w_ukv_local(full["ev_w_ukv"]),
                        "ev_w_out": full["ev_w_out"]})
        elif grp == "mix1":
            out.update({"od_w_qkv_t": full["od_w_qkv"], "od_w_out": full["od_w_out"]})
        else:
            layer = grp[-1]
            out.update({"w_gate_t" + layer: full["w_gate" + layer], "w_up_t" + layer: full["w_up" + layer],
                        "w_down" + layer: full["w_down" + layer]})
        return out

    scatter, pending = {}, {}

    def put_grads(grp, g):
        if grp == "in0":
            g = {"ev_w_in": _w_in_grad(g["w_in_t"])}
        elif grp == "mix0":
            g = {"ev_w_uq": _w_uq_grad(g["w_uq_t"]), "ev_w_ukv": _w_ukv_grad(g["w_ukv_t"]),
                 "ev_w_out": g["ev_w_out"]}
        elif grp == "mix1":
            g = {"od_w_qkv": g["od_w_qkv_t"], "od_w_out": g["od_w_out"]}
        else:
            layer = grp[-1]
            g = {"w_gate" + layer: g["w_gate_t"], "w_up" + layer: g["w_up_t"], "w_down" + layer: g["w_down"]}
        pending.update({n: v.reshape(N_DEV, 1, v.shape[0] // N_DEV, v.shape[1]).astype(BF16) for n, v in g.items()})
        batch = _BATCH_OF[grp]
        names = [n for gr in _BATCHES[batch] for n in _GROUPS[gr]]
        if batch == "in0" or not all(n in pending for n in names):
            return jnp.zeros((8, LANES), F32)
        send = [pending[n] for n in names]
        scatter[batch], tok = _exchange_start(send, [True] * len(names), send[0], name="scatter_start_" + batch)
        return tok

    small = {"g_cq": ev_g_cq, "g_ckv": ev_g_ckv, "od_rel_bias": od_rel_bias[0],
             "g_mix": g_mix + token[0, 0], "g_ffn": g_ffn, "g_final": g_final.reshape(1, -1)}
    loss_part, dx, G = _local_step(x[0], loss_target[0], small, get_weights, put_grads, prefetch)
    g_small = _pack_small([G["g_cq"], G["g_ckv"], G["od_rel_bias"], G["g_mix"], G["g_ffn"], G["g_final"],
                           loss_part.reshape(1)])
    scatter["in0"], _ = _exchange_start([pending["ev_w_in"], g_small], [True, False], dx, name="scatter_start_in0")

    grads, deltas, new_m, new_v = {}, {}, {}, {}
    parts, after = {}, dx

    def wait_parts(batch, after):
        lands = _exchange_wait(scatter[batch], after, name="scatter_wait_" + batch)
        parts.update(zip([n for grp in _BATCHES[batch] for n in _GROUPS[grp]], lands))
        return lands[0]

    def adamw(n):
        col = n in _COL_SHARDED
        rows = lambda a: (jnp.swapaxes(a, 1, 2) if col else a).reshape(-1, a.shape[1 if col else 2])
        layers = [parts[n]] if n in parts else [parts[n + "0"], parts[n + "1"]]
        res = _adamw(rows(w[n]), [p.reshape(N_DEV, -1, p.shape[-1]) for p in layers], rows(mom[n]), rows(var[n]),
                     name="adamw_" + n)
        L, a1, a2 = w[n].shape
        back = lambda r: jnp.swapaxes(r.reshape(L, a2, a1), 1, 2) if col else r.reshape(L, a1, a2)
        grads[n], deltas[n], new_m[n], new_v[n] = [back(r) for r in res]
        return res[0]

    for batch in ("layer1", "layer0"):
        after = wait_parts(batch, after)
    parts["ev_w_in"], small_parts = _exchange_wait(scatter["in0"], tuple(adamw(n) for n in _BIG[1:]),
                                                   name="scatter_wait_in0")
    adamw("ev_w_in")
    small_w = [w[n] for n in _SMALL]
    loss = jnp.sum(small_parts.reshape(N_DEV, -1)[:, sum(v.size for v in small_w)])
    res = _adamw(_pack_small(small_w), [small_parts], _pack_small([mom[n] for n in _SMALL]),
                 _pack_small([var[n] for n in _SMALL]), name="adamw_small")
    for d, packed in zip((grads, deltas, new_m, new_v), res):
        for n, val in zip(_SMALL, _unpack_small(packed, small_w)):
            d[n] = val

    order = ["ev_w_in", "ev_g_cq", "ev_w_uq", "ev_g_ckv", "ev_w_ukv", "ev_w_out", "od_w_qkv", "od_rel_bias",
             "od_w_out", "g_mix", "g_ffn", "w_gate", "w_up", "w_down", "g_final"]
    out = [loss, dx[None]]
    for d in (grads, deltas, new_m, new_v):
        out += [d[n] for n in order]
    return tuple(out)
```

```python
import functools

import numpy as np
import jax
import jax.numpy as jnp
from jax import lax
from jax.experimental import pallas as pl
from jax.experimental.pallas import tpu as pltpu

F32 = jnp.float32
BF16 = jnp.bfloat16

D_MODEL = 1024
CHUNK = 64
MLA_HEADS = 8
MLA_NOPE = 64
MLA_ROPE = 32
MLA_V = 64
Q_LORA = 384
KV_LORA = 256
ROPE_THETA = 10000.0
SB_HEADS = 8
SB_DIM = 64
C_HEADS = 16
C_DIM = 64
LEFT_CHUNKS = 8
REL_CLIP = 256
D_FF = 2816
RMS_EPS = 1e-6
ADAM_LR = 0.001
ADAM_B1 = 0.9
ADAM_B2 = 0.999
ADAM_EPS = 1e-08
ADAM_WD = 0.01
ADAM_STEP = 10

N_DEV = 8
LANES = 128
HEAD = 64
assert HEAD == MLA_NOPE == MLA_V == SB_DIM == C_DIM and 2 * HEAD == LANES
CHUNK_BITS = CHUNK.bit_length() - 1
assert 1 << CHUNK_BITS == CHUNK
VMEM_LIMIT = 56 * 1024 * 1024
NEG = -1e30
PAD_KEYS = LEFT_CHUNKS * CHUNK
BAND_TQ = 128
BAND_W = BAND_TQ + PAD_KEYS
TOEP_W = BAND_W + BAND_TQ

NN = (((1,), (0,)), ((), ()))
NT = (((1,), (1,)), ((), ()))
TN = (((0,), (0,)), ((), ()))


def _dot(a, b, dn):
    return lax.dot_general(a, b, dn, preferred_element_type=F32)


def _pick(dim, pref):
    if dim <= pref:
        return dim
    best = None
    for t in range(LANES, pref + 1, LANES):
        if dim % t == 0:
            best = t
    assert best is not None, (dim, pref)
    return best


def _params(sem):
    return pltpu.CompilerParams(dimension_semantics=sem, vmem_limit_bytes=VMEM_LIMIT)


def _mm(a, b, dims="nn", res=None, out_dtype=F32, name="mm"):
    if dims == "nn":
        (M, K), (K2, N) = a.shape, b.shape
    elif dims == "nt":
        (M, K), (N, K2) = a.shape, b.shape
    else:
        (K, M), (K2, N) = a.shape, b.shape
    assert K == K2, (a.shape, b.shape, dims)
    tm, tn, tk = _pick(M, 1024), _pick(N, 1152), _pick(K, 1024)
    nk = K // tk
    dn = {"nn": NN, "nt": NT, "tn": TN}[dims]
    has_res = res is not None

    def body(*refs):
        if has_res:
            a_ref, b_ref, r_ref, o_ref, acc = refs
        else:
            a_ref, b_ref, o_ref, acc = refs
        k = pl.program_id(2)

        @pl.when(k == 0)
        def _():
            acc[...] = jnp.zeros_like(acc)

        acc[...] += _dot(a_ref[...].astype(BF16), b_ref[...].astype(BF16), dn)

        @pl.when(k == nk - 1)
        def _():
            r = acc[...]
            if has_res:
                r = r + r_ref[...]
            o_ref[...] = r.astype(out_dtype)

    a_spec = (pl.BlockSpec((tk, tm), lambda i, j, k: (k, i)) if dims == "tn"
              else pl.BlockSpec((tm, tk), lambda i, j, k: (i, k)))
    b_spec = (pl.BlockSpec((tn, tk), lambda i, j, k: (j, k)) if dims == "nt"
              else pl.BlockSpec((tk, tn), lambda i, j, k: (k, j)))
    o_spec = pl.BlockSpec((tm, tn), lambda i, j, k: (i, j))
    in_specs = [a_spec, b_spec] + ([o_spec] if has_res else [])
    args = (a, b) + ((res,) if has_res else ())
    return pl.pallas_call(
        body, name=name, grid=(M // tm, N // tn, nk),
        in_specs=in_specs, out_specs=o_spec,
        out_shape=jax.ShapeDtypeStruct((M, N), out_dtype),
        scratch_shapes=[pltpu.VMEM((tm, tn), F32)],
        compiler_params=_params(("parallel", "parallel", "arbitrary")),
    )(*args)


def _rms_fwd(x, g, out_dtype=BF16, name="rms_fwd"):
    T, Fd = x.shape
    tm = _pick(T, 256)

    def body(x_ref, g_ref, o_ref):
        xv = x_ref[...]
        r = lax.rsqrt(jnp.mean(xv * xv, axis=-1, keepdims=True) + RMS_EPS)
        o_ref[...] = (xv * r * g_ref[...]).astype(out_dtype)

    return pl.pallas_call(
        body, name=name, grid=(T // tm,),
        in_specs=[pl.BlockSpec((tm, Fd), lambda i: (i, 0)), pl.BlockSpec((1, Fd), lambda i: (0, 0))],
        out_specs=pl.BlockSpec((tm, Fd), lambda i: (i, 0)),
        out_shape=jax.ShapeDtypeStruct((T, Fd), out_dtype),
        compiler_params=_params(("parallel",)),
    )(x, g)


def _rms_bwd(x, g, dy, dres=None, name="rms_bwd"):
    T, Fd = x.shape
    tm = _pick(T, 256)
    has_res = dres is not None

    def body(*refs):
        if has_res:
            x_ref, g_ref, dy_ref, r_ref, dx_ref, dxb_ref, dg_ref = refs
        else:
            x_ref, g_ref, dy_ref, dx_ref, dxb_ref, dg_ref = refs
        xv, dyv = x_ref[...], dy_ref[...]
        r = lax.rsqrt(jnp.mean(xv * xv, axis=-1, keepdims=True) + RMS_EPS)
        gdy = dyv * g_ref[...]
        dot = jnp.mean(xv * gdy, axis=-1, keepdims=True)
        dx = r * gdy - xv * (r * r * r * dot)
        if has_res:
            dx = dx + r_ref[...]
        dx_ref[...] = dx
        dxb_ref[...] = dx.astype(BF16)

        @pl.when(pl.program_id(0) == 0)
        def _():
            dg_ref[...] = jnp.zeros_like(dg_ref)

        dg_ref[...] += jnp.sum(dyv * xv * r, axis=0, keepdims=True)

    row = pl.BlockSpec((tm, Fd), lambda i: (i, 0))
    vec = pl.BlockSpec((1, Fd), lambda i: (0, 0))
    in_specs = [row, vec, row] + ([row] if has_res else [])
    args = (x, g, dy) + ((dres,) if has_res else ())
    return pl.pallas_call(
        body, name=name, grid=(T // tm,),
        in_specs=in_specs, out_specs=[row, row, vec],
        out_shape=[jax.ShapeDtypeStruct((T, Fd), F32), jax.ShapeDtypeStruct((T, Fd), BF16),
                   jax.ShapeDtypeStruct((1, Fd), F32)],
        compiler_params=_params(("arbitrary",)),
    )(*args)


def _mm_rms_bwd(a, b, x, g, dres, name="mm_rms_bwd"):
    T, K = a.shape
    Fd = b.shape[1]
    tm, tk = _pick(T, 1024), _pick(K, 1024)
    nk = K // tk

    def body(a_ref, b_ref, x_ref, g_ref, r_ref, dx_ref, dxb_ref, dg_ref, acc):
        i, k = pl.program_id(0), pl.program_id(1)

        @pl.when(k == 0)
        def _():
            acc[...] = jnp.zeros_like(acc)

        @pl.when((k == 0) & (i == 0))
        def _():
            dg_ref[...] = jnp.zeros_like(dg_ref)

        acc[...] += _dot(a_ref[...].astype(BF16), b_ref[...].astype(BF16), NN)

        @pl.when(k == nk - 1)
        def _():
            xv, dyv = x_ref[...], acc[...]
            r = lax.rsqrt(jnp.mean(xv * xv, axis=-1, keepdims=True) + RMS_EPS)
            gdy = dyv * g_ref[...]
            dot = jnp.mean(xv * gdy, axis=-1, keepdims=True)
            dx = r * gdy - xv * (r * r * r * dot) + r_ref[...]
            dx_ref[...] = dx
            dxb_ref[...] = dx.astype(BF16)
            dg_ref[...] += jnp.sum(dyv * xv * r, axis=0, keepdims=True)

    row = pl.BlockSpec((tm, Fd), lambda i, k: (i, 0))
    vec = pl.BlockSpec((1, Fd), lambda i, k: (0, 0))
    return pl.pallas_call(
        body, name=name, grid=(T // tm, nk),
        in_specs=[pl.BlockSpec((tm, tk), lambda i, k: (i, k)), pl.BlockSpec((tk, Fd), lambda i, k: (k, 0)),
                  row, vec, row],
        out_specs=[row, row, vec],
        out_shape=[jax.ShapeDtypeStruct((T, Fd), F32), jax.ShapeDtypeStruct((T, Fd), BF16),
                   jax.ShapeDtypeStruct((1, Fd), F32)],
        scratch_shapes=[pltpu.VMEM((tm, Fd), F32)],
        compiler_params=_params(("arbitrary", "arbitrary")),
    )(a, b, x, g, dres)


def _latent_bwd(proj, nq, nkv, dqa, dkn, dva, dkr, cos_k, sin_k, g_cq, g_ckv, w_uq_t, w_ukv_t, name="latent_bwd"):
    T = proj.shape[0]
    tm = _pick(T, 512)
    wl = _O2

    def rms_bwd(xv, gv, dyv):
        r = lax.rsqrt(jnp.mean(xv * xv, axis=-1, keepdims=True) + RMS_EPS)
        gdy = dyv * gv
        dot = jnp.mean(xv * gdy, axis=-1, keepdims=True)
        return r * gdy - xv * (r * r * r * dot), jnp.sum(dyv * xv * r, axis=0, keepdims=True)

    def body(p_ref, nq_ref, nkv_ref, dqa_ref, dkn_ref, dva_ref, dkr_ref, c_ref, s_ref, gq_ref, gkv_ref, wq_ref, wkv_ref,
             dlat_ref, dwq_ref, dwkv_ref, dgq_ref, dgkv_ref):
        @pl.when(pl.program_id(0) == 0)
        def _():
            for ref in (dwq_ref, dwkv_ref, dgq_ref, dgkv_ref):
                ref[...] = jnp.zeros_like(ref)

        dqv = dqa_ref[...]
        dkv = jnp.concatenate([dkn_ref[...], dva_ref[...]], axis=1)
        pv = p_ref[...]
        dc_q, dgq = rms_bwd(pv[:, :_O1], gq_ref[...], _dot(dqv, wq_ref[...], NN))
        dc_kv, dgkv = rms_bwd(pv[:, _O1:], gkv_ref[...], _dot(dkv, wkv_ref[...], NN))
        dkr_raw = _rotate(dkr_ref[...], c_ref[...], -s_ref[...])
        dlat_ref[...] = jnp.concatenate([dc_q, dc_kv, dkr_raw], axis=1).astype(BF16)
        dwq_ref[...] += _dot(dqv, nq_ref[...], TN)
        dwkv_ref[...] += _dot(dkv, nkv_ref[...], TN)
        dgq_ref[...] += dgq
        dgkv_ref[...] += dgkv

    row = lambda w: pl.BlockSpec((tm, w), lambda i: (i, 0))
    const = lambda a: pl.BlockSpec(a.shape, lambda i: (0, 0))
    outs = [jax.ShapeDtypeStruct((T, wl + LANES), BF16), jax.ShapeDtypeStruct(w_uq_t.shape, F32),
            jax.ShapeDtypeStruct(w_ukv_t.shape, F32), jax.ShapeDtypeStruct(g_cq.shape, F32),
            jax.ShapeDtypeStruct(g_ckv.shape, F32)]
    return pl.pallas_call(
        body, name=name, grid=(T // tm,),
        in_specs=[row(wl), row(_O1), row(_O2 - _O1), row(dqa.shape[1]), row(dkn.shape[1]), row(dva.shape[1]),
                  row(LANES), row(LANES), row(LANES), const(g_cq), const(g_ckv), const(w_uq_t), const(w_ukv_t)],
        out_specs=[row(wl + LANES)] + [const(o) for o in outs[1:]],
        out_shape=outs,
        compiler_params=_params(("arbitrary",)),
    )(proj, nq, nkv, dqa, dkn, dva, dkr, cos_k, sin_k, g_cq, g_ckv, w_uq_t, w_ukv_t)


def _loss_head(h, g, target, name="loss_head"):
    T, Fd = h.shape
    tm = _pick(T, 256)

    def body(h_ref, g_ref, t_ref, loss_ref, dh_ref, dhb_ref, dg_ref):
        xv = h_ref[...]
        r = lax.rsqrt(jnp.mean(xv * xv, axis=-1, keepdims=True) + RMS_EPS)
        diff = xv * r * g_ref[...] - t_ref[...]
        part = 0.5 * jnp.sum(jnp.mean(diff * diff, axis=-1, keepdims=True), axis=0, keepdims=True)
        dyv = diff * (1.0 / Fd)
        gdy = dyv * g_ref[...]
        dot = jnp.mean(xv * gdy, axis=-1, keepdims=True)
        dh = r * gdy - xv * (r * r * r * dot)
        dh_ref[...] = dh
        dhb_ref[...] = dh.astype(BF16)

        @pl.when(pl.program_id(0) == 0)
        def _():
            dg_ref[...] = jnp.zeros_like(dg_ref)
            loss_ref[...] = jnp.zeros_like(loss_ref)

        dg_ref[...] += jnp.sum(dyv * xv * r, axis=0, keepdims=True)
        loss_ref[...] += jnp.broadcast_to(part, loss_ref.shape)

    row = pl.BlockSpec((tm, Fd), lambda i: (i, 0))
    vec = pl.BlockSpec((1, Fd), lambda i: (0, 0))
    return pl.pallas_call(
        body, name=name, grid=(T // tm,),
        in_specs=[row, vec, row],
        out_specs=[pl.BlockSpec((1, LANES), lambda i: (0, 0)), row, row, vec],
        out_shape=[jax.ShapeDtypeStruct((1, LANES), F32), jax.ShapeDtypeStruct((T, Fd), F32),
                   jax.ShapeDtypeStruct((T, Fd), BF16), jax.ShapeDtypeStruct((1, Fd), F32)],
        compiler_params=_params(("arbitrary",)),
    )(h, g, target)


FFN_TF = 256


def _ffn_fwd(h, g, wg_t, wu_t, wd, name="ffn_fwd"):
    T, Dm = h.shape
    Fh = wd.shape[0]
    tm = _pick(T, 2048)
    nf = Fh // FFN_TF

    def body(h_ref, g_ref, wg_ref, wu_ref, wd_ref, o_ref, u_ref, a_ref, b_ref):
        j = pl.program_id(1)

        @pl.when(j == 0)
        def _():
            xv = h_ref[...]
            r = lax.rsqrt(jnp.mean(xv * xv, axis=-1, keepdims=True) + RMS_EPS)
            u_ref[...] = (xv * r * g_ref[...]).astype(BF16)
            o_ref[...] = xv

        u = u_ref[...]
        a = _dot(u, wg_ref[...], NT).astype(BF16)
        b = _dot(u, wu_ref[...], NT).astype(BF16)
        a_ref[...] = a
        b_ref[...] = b
        af = a.astype(F32)
        s = (af * jax.nn.sigmoid(af) * b.astype(F32)).astype(BF16)
        o_ref[...] += _dot(s, wd_ref[...], NN)

    row = pl.BlockSpec((tm, Dm), lambda i, j: (i, 0))
    wblk = pl.BlockSpec((FFN_TF, Dm), lambda i, j: (j, 0))
    ablk = pl.BlockSpec((tm, FFN_TF), lambda i, j: (i, j))
    return pl.pallas_call(
        body, name=name, grid=(T // tm, nf),
        in_specs=[pl.BlockSpec((tm, Dm), lambda i, j: (i, 0), pipeline_mode=pl.Buffered(1)),
                  pl.BlockSpec((1, Dm), lambda i, j: (0, 0)), wblk, wblk, wblk],
        out_specs=[row, row, ablk, ablk],
        out_shape=[jax.ShapeDtypeStruct((T, Dm), F32), jax.ShapeDtypeStruct((T, Dm), BF16),
                   jax.ShapeDtypeStruct((T, Fh), BF16), jax.ShapeDtypeStruct((T, Fh), BF16)],
        compiler_params=_params(("parallel", "arbitrary")),
    )(h, g, wg_t, wu_t, wd)


def _ffn_bwd(dh, u, a, b, wg_t, wu_t, wd, name="ffn_bwd"):
    T, Dm = dh.shape
    Fh = wd.shape[0]
    nf = Fh // FFN_TF
    once = pl.Buffered(1)

    def body(dh_ref, u_ref, a_ref, b_ref, wg_ref, wu_ref, wd_ref, du_ref, dwg_ref, dwu_ref, dwd_ref):
        j = pl.program_id(0)

        @pl.when(j == 0)
        def _():
            du_ref[...] = jnp.zeros_like(du_ref)

        ds = _dot(dh_ref[...], wd_ref[...], NT)
        af, bf = a_ref[...].astype(F32), b_ref[...].astype(F32)
        sig = jax.nn.sigmoid(af)
        sa = af * sig
        dwd_ref[...] = _dot((sa * bf).astype(BF16), dh_ref[...], TN).astype(BF16)
        dab = jnp.concatenate([(ds * bf * (sig * (1.0 + af * (1.0 - sig)))).astype(BF16),
                               (ds * sa).astype(BF16)], axis=1)
        dw = _dot(dab, u_ref[...], TN)
        dwg_ref[...] = dw[:FFN_TF].astype(BF16)
        dwu_ref[...] = dw[FFN_TF:].astype(BF16)
        du_ref[...] += _dot(dab, jnp.concatenate([wg_ref[...], wu_ref[...]], axis=0), NN)

    full = lambda: pl.BlockSpec((T, Dm), lambda j: (0, 0), pipeline_mode=once)
    wblk = pl.BlockSpec((FFN_TF, Dm), lambda j: (j, 0))
    ablk = pl.BlockSpec((T, FFN_TF), lambda j: (0, j))
    return pl.pallas_call(
        body, name=name, grid=(nf,),
        in_specs=[full(), full(), ablk, ablk, wblk, wblk, wblk],
        out_specs=[pl.BlockSpec((T, Dm), lambda j: (0, 0)), wblk, wblk, wblk],
        out_shape=[jax.ShapeDtypeStruct((T, Dm), F32)] + [jax.ShapeDtypeStruct((Fh, Dm), BF16)] * 3,
        compiler_params=_params(("arbitrary",)),
    )(dh, u, a, b, wg_t, wu_t, wd)


def _rope(x, cos_t, sin_t, col0, ncols, out_dtype, name="rope"):
    T = x.shape[0]
    wt = cos_t.shape[1]
    tm = _pick(T, 256)
    nb = ncols * LANES // wt
    half = MLA_ROPE // 2

    def body(x_ref, c_ref, s_ref, o_ref):
        xv = x_ref[...].astype(F32)
        lane = lax.broadcasted_iota(jnp.int32, xv.shape, 1)
        first = (lane & (MLA_ROPE - 1)) < half
        swapped = jnp.where(first, pltpu.roll(xv, wt - half, 1), pltpu.roll(xv, half, 1))
        o_ref[...] = (xv * c_ref[...] + swapped * s_ref[...]).astype(out_dtype)

    off = col0 * LANES // wt
    return pl.pallas_call(
        body, name=name, grid=(T // tm, nb),
        in_specs=[pl.BlockSpec((tm, wt), lambda i, j: (i, j + off)),
                  pl.BlockSpec((tm, wt), lambda i, j: (i, 0)),
                  pl.BlockSpec((tm, wt), lambda i, j: (i, 0))],
        out_specs=pl.BlockSpec((tm, wt), lambda i, j: (i, j)),
        out_shape=jax.ShapeDtypeStruct((T, ncols * LANES), out_dtype),
        compiler_params=_params(("parallel", "parallel")),
    )(x, cos_t, sin_t)


ATT_TQ = 512
ATT_TK = 256


def _mla_masks(shape):
    lane = lax.broadcasted_iota(jnp.int32, shape, 1)
    m0 = (lane < HEAD) | ((lane >= LANES) & (lane < LANES + MLA_ROPE))
    m1 = ((lane >= HEAD) & (lane < LANES)) | ((lane >= LANES + MLA_ROPE) & (lane < LANES + 2 * MLA_ROPE))
    return m0, m1


def _by_twos(n, step, carry):
    carry = lax.fori_loop(0, n // 2, lambda i, c: step(2 * i + 1, step(2 * i, c)), carry)
    return lax.fori_loop(0, n % 2, lambda _, c: step(n - 1, c), carry)


def _chunk_ok(tq, tk, d):
    row = lax.broadcasted_iota(jnp.int32, (tq, tk), 0)
    col = lax.broadcasted_iota(jnp.int32, (tq, tk), 1) + d * tk
    return jnp.concatenate([(col >> CHUNK_BITS) <= (row >> CHUNK_BITS)] * 2, axis=0)


def _rotate(x, cos_t, sin_t):
    half = MLA_ROPE // 2
    lane = lax.broadcasted_iota(jnp.int32, x.shape, 1)
    first = (lane & (MLA_ROPE - 1)) < half
    swapped = jnp.where(first, pltpu.roll(x, x.shape[1] - half, 1), pltpu.roll(x, half, 1))
    return x * cos_t + swapped * sin_t


def _mla_fwd(q, cos_q, sin_q, kv, kr, name="mla_fwd"):
    T = q.shape[0]
    tq, tk = _pick(T, ATT_TQ), _pick(T, ATT_TK)
    nd = tq // tk
    npair = MLA_HEADS // 2
    scale = (MLA_NOPE + MLA_ROPE) ** -0.5

    def body(q_ref, c_ref, s_ref, kn_ref, v_ref, kr_ref, o_ref, lse_ref):
        m_idx = pl.program_id(1)
        qv = _rotate(q_ref[...], c_ref[...], s_ref[...]).astype(BF16)
        m0, m1 = _mla_masks(qv.shape)
        qs = jnp.concatenate([jnp.where(m0, qv, 0), jnp.where(m1, qv, 0)], axis=0).astype(BF16)

        def block(kb, carry, ok):
            ks = pl.ds(pl.multiple_of(kb * tk, tk), tk)
            kcat = jnp.concatenate([kn_ref[ks, :], kr_ref[ks, :]], axis=1)
            mx, l, acc = carry
            s = _dot(qs, kcat, NT) * scale
            if ok is not None:
                s = jnp.where(ok, s, NEG)
            mn = jnp.maximum(mx, jnp.max(s, axis=-1, keepdims=True))
            alpha = jnp.exp(mx - mn)
            p = jnp.exp(s - mn)
            return (mn, alpha * l + jnp.sum(p, axis=-1, keepdims=True),
                    alpha * acc + _dot(p.astype(BF16), v_ref[ks, :], NN))

        init = (jnp.full((2 * tq, 1), NEG, F32), jnp.zeros((2 * tq, 1), F32), jnp.zeros((2 * tq, LANES), F32))
        res = init
        for d in range(nd):
            res = block(m_idx * nd + d, res, _chunk_ok(tq, tk, d))
        mx, l, acc = _by_twos(m_idx * nd, lambda kb, c: block(kb, c, None), res)
        h0 = lax.broadcasted_iota(jnp.int32, (tq, LANES), 1) < HEAD
        o_ref[...] = _two_heads(acc * (1.0 / l), h0).astype(o_ref.dtype)
        lse_ref[...] = _two_heads(jnp.broadcast_to(mx + jnp.log(l), (2 * tq, LANES)), h0)

    full = lambda col: pl.BlockSpec((T, LANES), col)
    table = pl.BlockSpec((tq, 2 * LANES), lambda p, m: (m, 0))
    return pl.pallas_call(
        body, name=name, grid=(npair, T // tq),
        in_specs=[pl.BlockSpec((tq, 2 * LANES), lambda p, m: (m, p)), table, table,
                  full(lambda p, m: (0, p)), full(lambda p, m: (0, npair + p)), full(lambda p, m: (0, 0))],
        out_specs=[pl.BlockSpec((tq, LANES), lambda p, m: (m, p)),
                   pl.BlockSpec((tq, LANES), lambda p, m: (m, p))],
        out_shape=[jax.ShapeDtypeStruct((T, npair * LANES), BF16),
                   jax.ShapeDtypeStruct((T, npair * LANES), F32)],
        compiler_params=_params(("parallel", "arbitrary")),
    )(q, cos_q, sin_q, kv, kv, kr)


def _mla_bwd(q, cos_q, sin_q, kv, kr, o, lse, do, do_col0, name="mla_bwd"):
    T = q.shape[0]
    tq, tk = _pick(T, ATT_TQ), _pick(T, ATT_TK)
    nd = tq // tk
    npair = MLA_HEADS // 2
    scale = (MLA_NOPE + MLA_ROPE) ** -0.5

    def body(q_ref, c_ref, s_ref, kn_ref, v_ref, kr_ref, o_ref, lse_ref, do_ref, dq_ref, dkn_ref, dv_ref, dkr_ref,
             dkn_acc, dv_acc):
        p_idx, m_idx = pl.program_id(0), pl.program_id(1)

        @pl.when(m_idx == 0)
        def _():
            dkn_acc[...] = jnp.zeros_like(dkn_acc)
            dv_acc[...] = jnp.zeros_like(dv_acc)

        @pl.when((m_idx == 0) & (p_idx == 0))
        def _():
            dkr_ref[...] = jnp.zeros_like(dkr_ref)

        qv = _rotate(q_ref[...], c_ref[...], s_ref[...]).astype(BF16)
        m0, m1 = _mla_masks(qv.shape)
        qs = jnp.concatenate([jnp.where(m0, qv, 0), jnp.where(m1, qv, 0)], axis=0).astype(BF16)
        dov = do_ref[...].astype(F32)
        h0 = lax.broadcasted_iota(jnp.int32, (tq, LANES), 1) < HEAD
        dos32 = jnp.concatenate([jnp.where(h0, dov, 0.0), jnp.where(h0, 0.0, dov)], axis=0)
        ov = o_ref[...].astype(F32)
        delta = jnp.sum(dos32 * jnp.concatenate([ov, ov], axis=0), axis=-1, keepdims=True)
        dos = dos32.astype(BF16)
        lsev = lse_ref[...]
        lse = jnp.concatenate([lsev[:, 0:1], lsev[:, HEAD:HEAD + 1]], axis=0)

        def block(kb, dq, ok):
            ks = pl.ds(pl.multiple_of(kb * tk, tk), tk)
            kcat = jnp.concatenate([kn_ref[ks, :], kr_ref[ks, :]], axis=1)
            vv = v_ref[ks, :]
            p = jnp.exp(_dot(qs, kcat, NT) * scale - lse)
            if ok is not None:
                p = jnp.where(ok, p, 0.0)
            ds = (p * (_dot(dos, vv, NT) - delta) * scale).astype(BF16)
            dkc = _dot(ds, qs, TN)
            dkn_acc[ks, :] += dkc[:, :LANES]
            dkr_ref[ks, :] += dkc[:, LANES:]
            dv_acc[ks, :] += _dot(p.astype(BF16), dos, TN)
            return dq + _dot(ds, kcat, NN)

        dq = jnp.zeros((2 * tq, 2 * LANES), F32)
        for d in range(nd):
            dq = block(m_idx * nd + d, dq, _chunk_ok(tq, tk, d))
        dq = _by_twos(m_idx * nd, lambda kb, c: block(kb, c, None), dq)
        dq_ref[...] = _rotate(jnp.where(m0, dq[:tq], jnp.where(m1, dq[tq:], 0.0)), c_ref[...],
                              -s_ref[...]).astype(BF16)

        @pl.when(m_idx == T // tq - 1)
        def _():
            dkn_ref[...] = dkn_acc[...].astype(BF16)
            dv_ref[...] = dv_acc[...].astype(BF16)

    full = lambda col: pl.BlockSpec((T, LANES), col)
    blk = lambda col: pl.BlockSpec((tq, LANES), col)
    table = pl.BlockSpec((tq, 2 * LANES), lambda p, m: (m, 0))
    return pl.pallas_call(
        body, name=name, grid=(npair, T // tq),
        in_specs=[pl.BlockSpec((tq, 2 * LANES), lambda p, m: (m, p)), table, table,
                  full(lambda p, m: (0, p)), full(lambda p, m: (0, npair + p)), full(lambda p, m: (0, 0)),
                  blk(lambda p, m: (m, p)), blk(lambda p, m: (m, p)),
                  blk(lambda p, m: (m, do_col0 + p))],
        out_specs=[pl.BlockSpec((tq, 2 * LANES), lambda p, m: (m, p)),
                   full(lambda p, m: (0, p)), full(lambda p, m: (0, p)), full(lambda p, m: (0, 0))],
        out_shape=[jax.ShapeDtypeStruct((T, npair * 2 * LANES), BF16),
                   jax.ShapeDtypeStruct((T, npair * LANES), BF16),
                   jax.ShapeDtypeStruct((T, npair * LANES), BF16),
                   jax.ShapeDtypeStruct((T, LANES), F32)],
        scratch_shapes=[pltpu.VMEM((T, LANES), F32)] * 2,
        compiler_params=_params(("arbitrary", "arbitrary")),
    )(q, cos_q, sin_q, kv, kv, kr, o, lse, do)


def _split_dot(x, tri):
    hi = x.astype(BF16)
    lo = (x - hi.astype(F32)).astype(BF16)
    both = _dot(jnp.concatenate([hi, lo], axis=0), tri, NN)
    return both[:x.shape[0]] + both[x.shape[0]:]


def _sb_terms(qh, kk, before):
    z = _dot(qh, kk, NT)
    sp = jnp.maximum(z, 0.0) + jnp.log(1.0 + jnp.exp(-jnp.abs(z)))
    lk = -sp if before is None else jnp.where(before, -sp, 0.0)
    return z, sp, lk


def _sb_setup(q_ref, tq, tk, scale):
    qv = (q_ref[...].astype(F32) * scale).astype(BF16)
    lane = lax.broadcasted_iota(jnp.int32, (tq, LANES), 1)
    h0 = lane < HEAD
    qs = jnp.concatenate([jnp.where(h0, qv, 0), jnp.where(h0, 0, qv)], axis=0).astype(BF16)
    row = lax.broadcasted_iota(jnp.int32, (tk, tk), 0)
    col = lax.broadcasted_iota(jnp.int32, (tk, tk), 1)
    return qs, h0, row, col


def _sb_before(tq, tk, d):
    row = lax.broadcasted_iota(jnp.int32, (tq, tk), 0)
    col = lax.broadcasted_iota(jnp.int32, (tq, tk), 1) + d * tk
    return jnp.concatenate([col < row] * 2, axis=0)


def _two_heads(x, h0):
    tq = x.shape[0] // 2
    return jnp.where(h0, x[:tq], x[tq:])


def _sb_fwd(qkv, col0, dep, name="sb_fwd"):
    T = qkv.shape[0]
    tq, tk = _pick(T, ATT_TQ), _pick(T, ATT_TK)
    nd = tq // tk
    npair = SB_HEADS // 2
    scale = SB_DIM ** -0.5

    def body(q_ref, k_ref, v_ref, dep_ref, o_ref, o32_ref, w_ref, sp_ref):
        m_idx = pl.program_id(1)
        qs, h0, row, col = _sb_setup(q_ref, tq, tk, scale)
        later = (row > col).astype(BF16)

        def block(kb, carry, before):
            ks = pl.ds(pl.multiple_of(kb * tk, tk), tk)
            c, acc = carry
            z, sp, lk = _sb_terms(qs, k_ref[ks, :].astype(BF16), before)
            w = jnp.exp((z - sp) + _split_dot(lk, later) + c)
            if before is not None:
                w = jnp.where(before, w, 0.0)
            wb = w.astype(BF16)
            w_ref[0, 0, kb] = wb
            sp_ref[0, 0, kb] = sp.astype(BF16)
            return (c + jnp.sum(lk, axis=-1, keepdims=True), acc + _dot(wb, v_ref[ks, :].astype(BF16), NN))

        init = (jnp.zeros((2 * tq, 1), F32), jnp.zeros((2 * tq, LANES), F32))
        res = init
        for d in reversed(range(nd)):
            res = block(m_idx * nd + d, res, _sb_before(tq, tk, d))
        res = _by_twos(m_idx * nd, lambda i, c: block(m_idx * nd - 1 - i, c, None), res)
        o = _two_heads(res[1], h0)
        o_ref[...] = o.astype(o_ref.dtype)
        o32_ref[...] = o

    full = lambda col: pl.BlockSpec((T, LANES), col)
    blk = pl.BlockSpec((tq, LANES), lambda p, m: (m, p))
    return pl.pallas_call(
        body, name=name, grid=(npair, T // tq),
        in_specs=[pl.BlockSpec((tq, LANES), lambda p, m: (m, col0 + p)),
                  full(lambda p, m: (0, col0 + npair + p)), full(lambda p, m: (0, col0 + 2 * npair + p)),
                  pl.BlockSpec((8, LANES), lambda p, m: (0, 0))],
        out_specs=[blk, blk] + [pl.BlockSpec((1, 1, T // tk, 2 * tq, tk), lambda p, m: (p, m, 0, 0, 0))] * 2,
        out_shape=[jax.ShapeDtypeStruct((T, npair * LANES), BF16), jax.ShapeDtypeStruct((T, npair * LANES), F32)]
        + [jax.ShapeDtypeStruct((npair, T // tq, T // tk, 2 * tq, tk), BF16)] * 2,
        compiler_params=_params(("parallel", "arbitrary")),
    )(qkv, qkv, qkv, dep)


def _sb_bwd(qkv, col0, o32, w_all, sp_all, do, do_col0, dep, name="sb_bwd"):
    T = qkv.shape[0]
    tq, tk = _pick(T, ATT_TQ), _pick(T, ATT_TK)
    nd = tq // tk
    npair = SB_HEADS // 2
    scale = SB_DIM ** -0.5

    def body(q_ref, k_ref, v_ref, o_ref, w_ref, sp_ref, do_ref, dep_ref, dq_ref, dk_ref, dv_ref, dk_acc, dv_acc):
        m_idx = pl.program_id(1)

        @pl.when(m_idx == 0)
        def _():
            dk_acc[...] = jnp.zeros_like(dk_acc)
            dv_acc[...] = jnp.zeros_like(dv_acc)

        qs, h0, row, col = _sb_setup(q_ref, tq, tk, scale)
        dov = do_ref[...].astype(F32)
        dos = jnp.concatenate([jnp.where(h0, dov, 0.0), jnp.where(h0, 0.0, dov)], axis=0).astype(BF16)
        ov = o_ref[...]
        etot = jnp.sum(dos.astype(F32) * jnp.concatenate([ov, ov], axis=0), axis=-1, keepdims=True)
        from_here = (row >= col).astype(BF16)

        def block(kb, carry, before):
            ks = pl.ds(pl.multiple_of(kb * tk, tk), tk)
            kk = k_ref[ks, :].astype(BF16)
            vv = v_ref[ks, :].astype(BF16)
            es, dqa = carry
            wb = w_ref[0, 0, kb]
            e = wb.astype(F32) * _dot(dos, vv, NT)
            prev = etot - (_split_dot(e, from_here) + es)
            sig_neg = jnp.exp(-sp_ref[0, 0, kb].astype(F32))
            dz = e * sig_neg - (1.0 - sig_neg) * prev
            if before is not None:
                dz = jnp.where(before, dz, 0.0)
            dzb = dz.astype(BF16)
            dk_acc[ks, :] += _dot(dzb, qs, TN)
            dv_acc[ks, :] += _dot(wb, dos, TN)
            return es + jnp.sum(e, axis=-1, keepdims=True), dqa + _dot(dzb, kk, NN)

        init = (jnp.zeros((2 * tq, 1), F32), jnp.zeros((2 * tq, LANES), F32))
        res = init
        for d in reversed(range(nd)):
            res = block(m_idx * nd + d, res, _sb_before(tq, tk, d))
        res = _by_twos(m_idx * nd, lambda i, c: block(m_idx * nd - 1 - i, c, None), res)
        dq_ref[...] = (_two_heads(res[1], h0) * scale).astype(BF16)

        @pl.when(m_idx == T // tq - 1)
        def _():
            dk_ref[...] = dk_acc[...].astype(BF16)
            dv_ref[...] = dv_acc[...].astype(BF16)

    full = lambda col: pl.BlockSpec((T, LANES), col)
    blk = lambda col: pl.BlockSpec((tq, LANES), col)
    return pl.pallas_call(
        body, name=name, grid=(npair, T // tq),
        in_specs=[blk(lambda p, m: (m, col0 + p)),
                  full(lambda p, m: (0, col0 + npair + p)), full(lambda p, m: (0, col0 + 2 * npair + p)),
                  blk(lambda p, m: (m, p)),
                  pl.BlockSpec((1, 1, T // tk, 2 * tq, tk), lambda p, m: (p, m, 0, 0, 0)),
                  pl.BlockSpec((1, 1, T // tk, 2 * tq, tk), lambda p, m: (p, m, 0, 0, 0)),
                  blk(lambda p, m: (m, do_col0 + p)), pl.BlockSpec((8, LANES), lambda p, m: (0, 0))],
        out_specs=[blk(lambda p, m: (m, p)), full(lambda p, m: (0, p)), full(lambda p, m: (0, p))],
        out_shape=[jax.ShapeDtypeStruct((T, npair * LANES), BF16)] * 3,
        scratch_shapes=[pltpu.VMEM((T, LANES), F32)] * 2,
        compiler_params=_params(("arbitrary", "arbitrary")),
    )(qkv, qkv, qkv, o32, w_all, sp_all, do, dep)


def _band_in_window():
    cq = lax.broadcasted_iota(jnp.int32, (BAND_TQ, BAND_W), 0) >> CHUNK_BITS
    ckp = lax.broadcasted_iota(jnp.int32, (BAND_TQ, BAND_W), 1) >> CHUNK_BITS
    return (ckp >= cq) & (ckp <= cq + LEFT_CHUNKS)


def _band_real(m_idx):
    j = lax.broadcasted_iota(jnp.int32, (BAND_TQ, BAND_W), 1)
    return j >= PAD_KEYS - m_idx * BAND_TQ


def _band_probs(qh, kw, bias, real, scale):
    s = jnp.where(real, _dot(qh, kw, NT) * scale + bias, NEG)
    e = jnp.exp(s - jnp.max(s, axis=-1, keepdims=True))
    return e * (1.0 / jnp.sum(e, axis=-1, keepdims=True))


BAND_SUB = 4


def _band_fwd(qkv, k_pad, v_pad, bias_w, name="band_fwd"):
    T = qkv.shape[0]
    npair = C_HEADS // 2
    scale = C_DIM ** -0.5
    rows = BAND_SUB * BAND_TQ

    def body(q_ref, k_ref, v_ref, b_ref, o_ref, p_ref):
        lane = lax.broadcasted_iota(jnp.int32, (BAND_TQ, LANES), 1)
        h0 = lane < HEAD
        bias = jnp.concatenate([b_ref[0], b_ref[1]], axis=0)
        for sub in range(BAND_SUB):
            m_idx = pl.program_id(1) * BAND_SUB + sub
            win = pl.ds(pl.multiple_of(m_idx * BAND_TQ, BAND_TQ), BAND_W)
            kw, vw = k_ref[win, :], v_ref[win, :]
            qv = q_ref[sub * BAND_TQ:(sub + 1) * BAND_TQ, :]
            qs = jnp.concatenate([jnp.where(h0, qv, 0), jnp.where(h0, 0, qv)], axis=0).astype(BF16)
            p = _band_probs(qs, kw, bias, jnp.concatenate([_band_real(m_idx)] * 2, axis=0), scale).astype(BF16)
            p_ref[0, sub] = p
            o = _two_heads(_dot(p, vw, NN), h0)
            o_ref[sub * BAND_TQ:(sub + 1) * BAND_TQ, :] = o.astype(o_ref.dtype)

    Tp = T + PAD_KEYS
    return pl.pallas_call(
        body, name=name, grid=(npair, T // rows),
        in_specs=[pl.BlockSpec((rows, LANES), lambda p, m: (m, p)),
                  pl.BlockSpec((Tp, LANES), lambda p, m: (0, p)),
                  pl.BlockSpec((Tp, LANES), lambda p, m: (0, p)),
                  pl.BlockSpec((2, BAND_TQ, BAND_W), lambda p, m: (p, 0, 0))],
        out_specs=[pl.BlockSpec((rows, LANES), lambda p, m: (m, p)),
                   pl.BlockSpec((1, BAND_SUB, 2 * BAND_TQ, BAND_W), lambda p, m: (p, m, 0, 0))],
        out_shape=[jax.ShapeDtypeStruct((T, npair * LANES), BF16),
                   jax.ShapeDtypeStruct((npair, T // BAND_TQ, 2 * BAND_TQ, BAND_W), BF16)],
        compiler_params=_params(("parallel", "arbitrary")),
    )(qkv, k_pad, v_pad, bias_w)


def _band_bwd(qkv, k_pad, v_pad, probs, do, name="band_bwd"):
    T = qkv.shape[0]
    npair = C_HEADS // 2
    scale = C_DIM ** -0.5

    rows = BAND_SUB * BAND_TQ

    def body(q_ref, k_ref, v_ref, p_ref, do_ref, dq_ref, dk_ref, dv_ref, db_ref, dk_acc, dv_acc):
        @pl.when(pl.program_id(1) == 0)
        def _():
            dk_acc[...] = jnp.zeros_like(dk_acc)
            dv_acc[...] = jnp.zeros_like(dv_acc)
            db_ref[...] = jnp.zeros_like(db_ref)

        lane = lax.broadcasted_iota(jnp.int32, (BAND_TQ, LANES), 1)
        h0 = lane < HEAD
        dbs = jnp.zeros((2 * BAND_TQ, BAND_W), F32)
        for sub in range(BAND_SUB):
            m_idx = pl.program_id(1) * BAND_SUB + sub
            win = pl.ds(pl.multiple_of(m_idx * BAND_TQ, BAND_TQ), BAND_W)
            kw, vw = k_ref[win, :], v_ref[win, :]
            qv = q_ref[sub * BAND_TQ:(sub + 1) * BAND_TQ, :]
            dov = do_ref[sub * BAND_TQ:(sub + 1) * BAND_TQ, :].astype(F32)
            qs = jnp.concatenate([jnp.where(h0, qv, 0), jnp.where(h0, 0, qv)], axis=0).astype(BF16)
            dos = jnp.concatenate([jnp.where(h0, dov, 0.0), jnp.where(h0, 0.0, dov)], axis=0).astype(BF16)
            pb = p_ref[0, sub]
            p = pb.astype(F32)
            dp = _dot(dos, vw, NT)
            dsb = p * (dp - jnp.sum(p * dp, axis=-1, keepdims=True))
            dbs = dbs + dsb
            dsq = (dsb * scale).astype(BF16)
            dq_ref[sub * BAND_TQ:(sub + 1) * BAND_TQ, :] = _two_heads(_dot(dsq, kw, NN), h0).astype(BF16)
            dk_acc[win, :] += _dot(dsq, qs, TN)
            dv_acc[win, :] += _dot(pb, dos, TN)
        db_ref[0] += dbs[:BAND_TQ]
        db_ref[1] += dbs[BAND_TQ:]

        @pl.when(pl.program_id(1) == T // rows - 1)
        def _():
            dk_ref[...] = dk_acc[...].astype(BF16)
            dv_ref[...] = dv_acc[...].astype(BF16)

    Tp = T + PAD_KEYS
    blk = lambda col: pl.BlockSpec((rows, LANES), col)
    full = pl.BlockSpec((Tp, LANES), lambda p, m: (0, p))
    bias = pl.BlockSpec((2, BAND_TQ, BAND_W), lambda p, m: (p, 0, 0))
    prob = pl.BlockSpec((1, BAND_SUB, 2 * BAND_TQ, BAND_W), lambda p, m: (p, m, 0, 0))
    return pl.pallas_call(
        body, name=name, grid=(npair, T // rows),
        in_specs=[blk(lambda p, m: (m, p)), full, full, prob, blk(lambda p, m: (m, p))],
        out_specs=[blk(lambda p, m: (m, p)), full, full, bias],
        out_shape=[jax.ShapeDtypeStruct((T, npair * LANES), BF16),
                   jax.ShapeDtypeStruct((Tp, npair * LANES), BF16),
                   jax.ShapeDtypeStruct((Tp, npair * LANES), BF16),
                   jax.ShapeDtypeStruct((C_HEADS, BAND_TQ, BAND_W), F32)],
        scratch_shapes=[pltpu.VMEM((Tp, LANES), F32)] * 2,
        compiler_params=_params(("arbitrary", "arbitrary")),
    )(qkv, k_pad, v_pad, probs, do)


def _skew_bits(x, left):
    w = x.shape[1]
    row = lax.broadcasted_iota(jnp.int32, x.shape, 0)
    for b in range(BAND_TQ.bit_length() - 1):
        amt = (w - (1 << b)) if left else (1 << b)
        x = jnp.where((row >> b) & 1 == 1, pltpu.roll(x, amt, 1), x)
    return x


def _toeplitz(diag, name="toeplitz"):
    H = diag.shape[0]

    def body(d_ref, o_ref):
        x = jnp.broadcast_to(d_ref[0], (BAND_TQ, TOEP_W))
        o_ref[0] = jnp.where(_band_in_window(), _skew_bits(x, left=False)[:, BAND_TQ:], NEG)

    return pl.pallas_call(
        body, name=name, grid=(H,),
        in_specs=[pl.BlockSpec((1, 1, TOEP_W), lambda h: (h, 0, 0))],
        out_specs=pl.BlockSpec((1, BAND_TQ, BAND_W), lambda h: (h, 0, 0)),
        out_shape=jax.ShapeDtypeStruct((H, BAND_TQ, BAND_W), F32),
        compiler_params=_params(("parallel",)),
    )(diag.reshape(H, 1, TOEP_W))


def _toeplitz_bwd(dbias, name="toeplitz_bwd"):
    H = dbias.shape[0]

    def body(d_ref, o_ref):
        x = jnp.concatenate([jnp.zeros((BAND_TQ, BAND_TQ), F32), d_ref[0]], axis=1)
        h = BAND_TQ // 2
        while h >= 8:
            x = x[:h] + pltpu.roll(x[h:2 * h], TOEP_W - h, 1)
            h //= 2
        o_ref[0] = jnp.sum(_skew_bits(x, left=True), axis=0, keepdims=True)

    return pl.pallas_call(
        body, name=name, grid=(H,),
        in_specs=[pl.BlockSpec((1, BAND_TQ, BAND_W), lambda h: (h, 0, 0))],
        out_specs=pl.BlockSpec((1, 1, TOEP_W), lambda h: (h, 0, 0)),
        out_shape=jax.ShapeDtypeStruct((H, 1, TOEP_W), F32),
        compiler_params=_params(("parallel",)),
    )(dbias).reshape(H, TOEP_W)


_HBM = pl.BlockSpec(memory_space=pltpu.HBM)
_SEM = pl.BlockSpec(memory_space=pltpu.SEMAPHORE)
_EFFECT = pltpu.SideEffectType.DATAFLOW_SIDE_EFFECTING


def _peers():
    x, y, c = lax.axis_index("x"), lax.axis_index("y"), lax.axis_index("c")
    out = []
    for k in range(1, N_DEV):
        peer = (1 - x if (k >> 2) & 1 else x, 1 - y if (k >> 1) & 1 else y, 1 - c if k & 1 else c)
        out.append((peer, 4 * peer[0] + 2 * peer[1] + peer[2]))
    return 4 * x + 2 * y + c, out


def _split_copies(ins, lands, scatter, send_sem, recv_sem, arriving):
    me, peers = _peers()
    out = []
    for a in range(len(ins)):
        for peer, idx in peers:
            out.append(pltpu.make_async_remote_copy(
                src_ref=ins[a].at[idx] if scatter[a] else ins[a],
                dst_ref=lands[a].at[idx if arriving else me], send_sem=send_sem, recv_sem=recv_sem,
                device_id=peer, device_id_type=pl.DeviceIdType.MESH))
    return out


def _landing_zones(arrays, scatter):
    return [lax.empty((N_DEV,) + (a.shape[1:] if s else a.shape), a.dtype) for a, s in zip(arrays, scatter)]


def _place_own(arrays, scatter, name):
    n = len(arrays)
    lands = _landing_zones(arrays, scatter)
    me = (4 * lax.axis_index("x") + 2 * lax.axis_index("y") + lax.axis_index("c")).astype(jnp.int32).reshape(1)

    def body(me_ref, *refs):
        for a in range(n):
            refs[2 * n + a][...] = refs[a][...].reshape(refs[2 * n + a].shape)

    def row_spec(shape):
        zeros = (0,) * (len(shape) - 1)
        return pl.BlockSpec((1,) + tuple(shape[1:]), lambda i, me_ref: (me_ref[0],) + zeros)

    in_specs = [row_spec(a.shape) if s else pl.BlockSpec(a.shape, lambda i, me_ref, nd=a.ndim: (0,) * nd)
                for a, s in zip(arrays, scatter)]
    return pl.pallas_call(
        body, name=name,
        out_shape=[jax.ShapeDtypeStruct(l.shape, l.dtype) for l in lands],
        grid_spec=pltpu.PrefetchScalarGridSpec(
            num_scalar_prefetch=1, grid=(1,),
            in_specs=in_specs + [pl.BlockSpec(memory_space=pl.ANY)] * n,
            out_specs=[row_spec(l.shape) for l in lands]),
        input_output_aliases={1 + n + i: i for i in range(n)},
        compiler_params=_params(("arbitrary",)),
    )(me, *arrays, *lands)


def _exchange_start_groups(groups, scatter, after, name, lands=None):
    sizes = [len(g) for g in groups]
    arrays = [a for g in groups for a in g]
    n, ng = len(arrays), len(groups)
    flags = list(scatter) if isinstance(scatter, (list, tuple)) else [scatter] * n
    if lands is None:
        lands = list(_place_own(arrays, flags, name=name.replace("_start_", "_own_")))
    else:
        lands = [l for g in lands for l in g]
    starts = np.cumsum([0] + sizes)

    def body(*refs):
        ins, lnd = refs[:n], refs[n:2 * n]
        sems = refs[2 * n + 1:2 * n + 1 + 2 * ng]
        token = refs[-1]
        for g in range(ng):
            sl = slice(starts[g], starts[g + 1])
            for cp in _split_copies(ins[sl], lnd[sl], flags[sl], sems[2 * g], sems[2 * g + 1], arriving=False):
                cp.start()
        token[...] = jnp.zeros_like(token)

    hbm = lambda a: pltpu.HBM(a.shape, a.dtype)
    out = pl.pallas_call(
        body, name=name,
        out_shape=(*[pltpu.SemaphoreType.DMA(())] * (2 * ng),
                   *[hbm(a) for a in arrays], *[hbm(a) for a in lands],
                   jax.ShapeDtypeStruct((8, LANES), F32)),
        in_specs=[_HBM] * (2 * n) + [pl.BlockSpec(memory_space=pl.ANY)],
        out_specs=(*[_SEM] * (2 * ng), *([_HBM] * (2 * n)), pl.BlockSpec(memory_space=pltpu.VMEM)),
        input_output_aliases={i: 2 * ng + i for i in range(2 * n)},
        compiler_params=pltpu.CompilerParams(has_side_effects=_EFFECT),
    )(*[pltpu.with_memory_space_constraint(a, pltpu.HBM) for a in list(arrays) + lands], after)
    ins_out, lands_out = out[2 * ng:2 * ng + n], out[2 * ng + n:2 * ng + 2 * n]
    handles = [(out[2 * g], out[2 * g + 1], list(ins_out[starts[g]:starts[g + 1]]),
                list(lands_out[starts[g]:starts[g + 1]]), tuple(flags[starts[g]:starts[g + 1]]))
               for g in range(ng)]
    return handles, out[-1]


def _exchange_start(arrays, scatter, after, name):
    handles, token = _exchange_start_groups([list(arrays)], list(scatter), after, name)
    return handles[0], token


def _exchange_wait(handle, after, name):
    send_sem, recv_sem, ins, lands, scatter = handle
    n = len(ins)
    after = after if isinstance(after, tuple) else (after,)

    def body(*refs):
        i_ref, l_ref = refs[:n], refs[n:2 * n]
        s_sem, r_sem = refs[2 * n:2 * n + 2]
        for cp in _split_copies(i_ref, l_ref, scatter, s_sem, r_sem, arriving=False):
            cp.wait_send()
        for cp in _split_copies(i_ref, l_ref, scatter, s_sem, r_sem, arriving=True):
            cp.wait_recv()

    hbm = lambda a: pltpu.HBM(a.shape, a.dtype)
    out = pl.pallas_call(
        body, name=name,
        out_shape=tuple(hbm(a) for a in ins + lands),
        in_specs=[_HBM] * (2 * n) + [_SEM, _SEM] + [pl.BlockSpec(memory_space=pl.ANY)] * len(after),
        out_specs=tuple([_HBM] * (2 * n)),
        input_output_aliases={i: i for i in range(2 * n)},
        compiler_params=pltpu.CompilerParams(has_side_effects=_EFFECT),
    )(*ins, *lands, send_sem, recv_sem, *after)
    return list(out[n:])


_SIBLING = 1
_CHIPS = (4, 2, 6)


def _peer_of(k):
    x, y, c = lax.axis_index("x"), lax.axis_index("y"), lax.axis_index("c")
    peer = (1 - x if (k >> 2) & 1 else x, 1 - y if (k >> 1) & 1 else y, 1 - c if k & 1 else c)
    return peer, 4 * peer[0] + 2 * peer[1] + peer[2]


def _rcopy(src, dst, send_sem, recv_sem, k):
    return pltpu.make_async_remote_copy(src_ref=src, dst_ref=dst, send_sem=send_sem, recv_sem=recv_sem,
                                        device_id=_peer_of(k)[0], device_id_type=pl.DeviceIdType.MESH)


def _gather2_start(groups, lands, after, name):
    sizes = [len(g) for g in groups]
    arrays = [a for g in groups for a in g]
    lands = [l for g in lands for l in g]
    n, ng = len(arrays), len(groups)
    starts = np.cumsum([0] + sizes)

    def body(*refs):
        ins, lnd = refs[:n], refs[n:2 * n]
        sems = refs[2 * n + 1:2 * n + 1 + 4 * ng]
        me, _ = _peers()
        for g in range(ng):
            send_d, recv_d, send_i, recv_i = sems[4 * g:4 * g + 4]
            for a in range(starts[g], starts[g + 1]):
                for k in _CHIPS:
                    _rcopy(ins[a], lnd[a].at[me], send_i, recv_i, k).start()
                _rcopy(ins[a], lnd[a].at[me], send_d, recv_d, _SIBLING).start()
        refs[-1][...] = jnp.zeros_like(refs[-1])

    hbm = lambda a: pltpu.HBM(a.shape, a.dtype)
    out = pl.pallas_call(
        body, name=name,
        out_shape=(*[pltpu.SemaphoreType.DMA(())] * (4 * ng), *[hbm(a) for a in arrays], *[hbm(a) for a in lands],
                   jax.ShapeDtypeStruct((8, LANES), F32)),
        in_specs=[_HBM] * (2 * n) + [pl.BlockSpec(memory_space=pl.ANY)],
        out_specs=(*[_SEM] * (4 * ng), *([_HBM] * (2 * n)), pl.BlockSpec(memory_space=pltpu.VMEM)),
        input_output_aliases={i: 4 * ng + i for i in range(2 * n)},
        compiler_params=pltpu.CompilerParams(has_side_effects=_EFFECT),
    )(*[pltpu.with_memory_space_constraint(a, pltpu.HBM) for a in arrays + lands], after)
    ins_out, lands_out = out[4 * ng:4 * ng + n], out[4 * ng + n:4 * ng + 2 * n]
    handles = [dict(sems=out[4 * g:4 * g + 4], ins=list(ins_out[starts[g]:starts[g + 1]]),
                    lands=list(lands_out[starts[g]:starts[g + 1]])) for g in range(ng)]
    return handles, out[-1]


def _gather2_pass_on(handle, after, name):
    lands, recv_i = handle["lands"], handle["sems"][3]
    n = len(lands)
    after = after if isinstance(after, tuple) else (after,)

    def body(*refs):
        lnd, r_i = refs[:n], refs[n]
        send_f, recv_f = refs[n + 1 + len(after):n + 3 + len(after)]
        for a in range(n):
            for k in _CHIPS:
                row = _peer_of(k)[1]
                _rcopy(lnd[a].at[row], lnd[a].at[row], send_f, r_i, k).wait_recv()
        for a in range(n):
            for k in _CHIPS:
                row = _peer_of(k)[1]
                _rcopy(lnd[a].at[row], lnd[a].at[row], send_f, recv_f, _SIBLING).start()
        refs[-1][...] = jnp.zeros_like(refs[-1])

    hbm = lambda a: pltpu.HBM(a.shape, a.dtype)
    out = pl.pallas_call(
        body, name=name,
        out_shape=(pltpu.SemaphoreType.DMA(()), pltpu.SemaphoreType.DMA(()), *[hbm(a) for a in lands],
                   jax.ShapeDtypeStruct((8, LANES), F32)),
        in_specs=[_HBM] * n + [_SEM] + [pl.BlockSpec(memory_space=pl.ANY)] * len(after),
        out_specs=(_SEM, _SEM, *([_HBM] * n), pl.BlockSpec(memory_space=pltpu.VMEM)),
        input_output_aliases={i: 2 + i for i in range(n)},
        compiler_params=pltpu.CompilerParams(has_side_effects=_EFFECT),
    )(*lands, recv_i, *after)
    return dict(handle, lands=list(out[2:2 + n]), passed=(out[0], out[1])), out[-1]


def _gather2_wait(handle, after, name):
    ins, lands = handle["ins"], handle["lands"]
    send_d, recv_d, send_i, _ = handle["sems"]
    send_f, recv_f = handle["passed"]
    n = len(ins)
    after = after if isinstance(after, tuple) else (after,)

    def body(*refs):
        i_ref, lnd = refs[:n], refs[n:2 * n]
        s_d, r_d, s_i, s_f, r_f = refs[2 * n:2 * n + 5]
        me, _ = _peers()
        sib = _peer_of(_SIBLING)[1]
        for a in range(n):
            _rcopy(i_ref[a], lnd[a].at[sib], s_d, r_d, _SIBLING).wait_send()
            _rcopy(i_ref[a], lnd[a].at[sib], s_d, r_d, _SIBLING).wait_recv()
            for k in _CHIPS:
                row = _peer_of(k)[1]
                _rcopy(i_ref[a], lnd[a].at[me], s_i, r_d, k).wait_send()
                _rcopy(lnd[a].at[row], lnd[a].at[row], s_f, r_f, _SIBLING).wait_send()
                _rcopy(lnd[a].at[row], lnd[a].at[_peer_of(k ^ _SIBLING)[1]], s_f, r_f, _SIBLING).wait_recv()

    hbm = lambda a: pltpu.HBM(a.shape, a.dtype)
    out = pl.pallas_call(
        body, name=name,
        out_shape=tuple(hbm(a) for a in ins + lands),
        in_specs=[_HBM] * (2 * n) + [_SEM] * 5 + [pl.BlockSpec(memory_space=pl.ANY)] * len(after),
        out_specs=tuple([_HBM] * (2 * n)),
        input_output_aliases={i: i for i in range(2 * n)},
        compiler_params=pltpu.CompilerParams(has_side_effects=_EFFECT),
    )(*ins, *lands, send_d, recv_d, send_i, send_f, recv_f, *after)
    return list(out[n:])


def _adamw(w, parts, m, v, name="adamw"):
    R, C = w.shape
    L = len(parts)
    rl = R // L
    tr = max([t for t in range(16, 257, 16) if rl % t == 0], default=rl)
    nb = rl // tr
    c1 = 1.0 - ADAM_B1 ** ADAM_STEP
    c2 = 1.0 - ADAM_B2 ** ADAM_STEP

    def body(*refs):
        w_ref, p_refs, (m_ref, v_ref, g_ref, d_ref, nm_ref, nv_ref) = refs[0], refs[1:1 + L], refs[1 + L:]
        g = None
        for j, p_ref in enumerate(p_refs):
            gj = p_ref[0].astype(F32)
            for i in range(1, N_DEV):
                gj = gj + p_ref[i].astype(F32)
            g = gj if g is None else jnp.where(pl.program_id(0) == j, gj, g)
        nm = ADAM_B1 * m_ref[...] + (1.0 - ADAM_B1) * g
        nv = ADAM_B2 * v_ref[...] + (1.0 - ADAM_B2) * (g * g)
        g_ref[...] = g
        nm_ref[...] = nm
        nv_ref[...] = nv
        d_ref[...] = -ADAM_LR * ((nm / c1) / (jnp.sqrt(nv / c2) + ADAM_EPS) + ADAM_WD * w_ref[...])

    blk = pl.BlockSpec((tr, C), lambda l, i: (l * nb + i, 0))
    part = lambda j: pl.BlockSpec((N_DEV, tr, C), lambda l, i: (0, jnp.where(l == j, i, 0), 0))
    return pl.pallas_call(
        body, name=name, grid=(L, nb),
        in_specs=[blk] + [part(j) for j in range(L)] + [blk, blk],
        out_specs=[blk] * 4,
        out_shape=[jax.ShapeDtypeStruct((R, C), F32)] * 4,
        compiler_params=_params(("arbitrary", "arbitrary")),
    )(w, *parts, m, v)


_O1 = Q_LORA
_O2 = _O1 + KV_LORA
_O3 = _O2 + MLA_ROPE
_NB = SB_HEADS * SB_DIM
IN_W = _O2 + LANES + 3 * _NB
COL_KR = _O2 // LANES
COL_SB = COL_KR + 1


def _w_in_local(w):
    kr = w[_O2:_O3]
    pad = jnp.zeros((LANES - 2 * MLA_ROPE, w.shape[1]), w.dtype)
    return jnp.concatenate([w[:_O2], kr, kr, pad, w[_O3:]], axis=0)


def _w_in_grad(g):
    kr = (g[_O2:_O2 + MLA_ROPE].astype(F32) + g[_O2 + MLA_ROPE:_O2 + 2 * MLA_ROPE].astype(F32)).astype(g.dtype)
    return jnp.concatenate([g[:_O2], kr, g[_O2 + LANES:]], axis=0)


def _w_uq_local(w):
    w3 = w.reshape(MLA_HEADS // 2, 2, MLA_NOPE + MLA_ROPE, w.shape[1])
    nope = w3[:, :, :MLA_NOPE].reshape(MLA_HEADS // 2, 2 * MLA_NOPE, w.shape[1])
    rope = w3[:, :, MLA_NOPE:].reshape(MLA_HEADS // 2, 2 * MLA_ROPE, w.shape[1])
    pad = jnp.zeros((MLA_HEADS // 2, LANES - 2 * MLA_ROPE, w.shape[1]), w.dtype)
    return jnp.concatenate([nope, rope, pad], axis=1).reshape(-1, w.shape[1])


def _w_uq_grad(g):
    g3 = g.reshape(MLA_HEADS // 2, 2 * LANES, g.shape[1])
    nope = g3[:, :2 * MLA_NOPE].reshape(MLA_HEADS // 2, 2, MLA_NOPE, g.shape[1])
    rope = g3[:, LANES:LANES + 2 * MLA_ROPE].reshape(MLA_HEADS // 2, 2, MLA_ROPE, g.shape[1])
    return jnp.concatenate([nope, rope], axis=2).reshape(-1, g.shape[1])


def _w_ukv_local(w):
    w3 = w.reshape(MLA_HEADS, MLA_NOPE + MLA_V, w.shape[1])
    return jnp.concatenate([w3[:, :MLA_NOPE].reshape(-1, w.shape[1]),
                            w3[:, MLA_NOPE:].reshape(-1, w.shape[1])], axis=0)


def _w_ukv_grad(g):
    half = MLA_HEADS * MLA_NOPE
    kn = g[:half].reshape(MLA_HEADS, MLA_NOPE, g.shape[1])
    vv = g[half:].reshape(MLA_HEADS, MLA_V, g.shape[1])
    return jnp.concatenate([kn, vv], axis=1).reshape(-1, g.shape[1])


def _rope_tables(T):
    pos = jnp.arange(T, dtype=F32)
    inv_freq = ROPE_THETA ** (-jnp.arange(0, MLA_ROPE, 2, dtype=F32) / MLA_ROPE)
    ang = pos[:, None] * inv_freq[None, :]
    cos, sin = jnp.cos(ang), jnp.sin(ang)
    ones = jnp.ones((T, LANES - 2 * MLA_ROPE), F32)
    cos_k = jnp.concatenate([cos, cos, cos, cos, ones], axis=1)
    sin_k = jnp.concatenate([-sin, sin, -sin, sin, 0.0 * ones], axis=1)
    cos_q = jnp.concatenate([jnp.ones((T, LANES), F32), cos_k], axis=1)
    sin_q = jnp.concatenate([jnp.zeros((T, LANES), F32), sin_k], axis=1)
    return cos_q, sin_q, cos_k, sin_k


def _bias_diag_index():
    ell = np.arange(TOEP_W)
    return np.clip(BAND_W - ell, -REL_CLIP, REL_CLIP) + REL_CLIP


def _local_step(x, target, small, get_weights, put_grads, prefetch):
    T = x.shape[0]
    cos_q, sin_q, cos_k, sin_k = _rope_tables(T)
    G = {}
    W = dict(small)

    u0 = _rms_fwd(x, W["g_mix"][0:1], name="rms_mix0")
    bias_w = _toeplitz(W["od_rel_bias"][:, _bias_diag_index()])
    W.update(get_weights("in0", (u0, bias_w)))
    proj = _mm(u0, W["w_in_t"], dims="nt", name="proj_in")
    W.update(get_weights("mix0", proj))
    c_q, c_kv = proj[:, :_O1], proj[:, _O1:_O2]
    nq = _rms_fwd(c_q, W["g_cq"], name="rms_cq")
    nkv = _rms_fwd(c_kv, W["g_ckv"], name="rms_ckv")
    qa_raw = _mm(nq, W["w_uq_t"], dims="nt", name="proj_uq")
    kv = _mm(nkv, W["w_ukv_t"], dims="nt", out_dtype=BF16, name="proj_ukv")
    kr = _rope(proj, cos_k, sin_k, COL_KR, 1, BF16, name="rope_k")
    o_a, lse = _mla_fwd(qa_raw, cos_q, sin_q, kv, kr)
    o_b, o_b32, w_b, sp_b = _sb_fwd(proj, COL_SB, prefetch("ffn0", o_a))
    o_ab = jnp.concatenate([o_a, o_b], axis=1)
    h1 = _mm(o_ab, W["ev_w_out"], res=x, name="out_ev")

    def ffn_fwd(h, layer):
        W.update(get_weights(f"ffn{layer}", h))
        return _ffn_fwd(h, W["g_ffn"][layer:layer + 1], W[f"w_gate_t{layer}"], W[f"w_up_t{layer}"],
                        W[f"w_down{layer}"], name=f"ffn_fwd{layer}")

    h2, u1, a0, b0 = ffn_fwd(h1, 0)

    W.update(get_weights("mix1", h2))
    u2 = _rms_fwd(h2, W["g_mix"][1:2], name="rms_mix1")
    qkv = _mm(u2, W["od_w_qkv_t"], dims="nt", out_dtype=BF16, name="proj_qkv")
    nc = C_HEADS * C_DIM
    pad = ((PAD_KEYS, 0), (0, 0))
    k_pad, v_pad = jnp.pad(qkv[:, nc:2 * nc], pad), jnp.pad(qkv[:, 2 * nc:], pad)
    o_c, p_c = _band_fwd(qkv, k_pad, v_pad, bias_w)
    h3 = _mm(o_c, W["od_w_out"], res=h2, name="out_od")
    h4, u3, a1, b1 = ffn_fwd(h3, 1)

    loss, dh, dhb, G["g_final"] = _loss_head(h4, W["g_final"], target)

    def ffn_bwd(dh, dhb, h, u, a, b, layer):
        du, g_gate, g_up, g_down = _ffn_bwd(dhb, u, a, b, W[f"w_gate_t{layer}"], W[f"w_up_t{layer}"],
                                            W[f"w_down{layer}"], name=f"ffn_bwd{layer}")
        tok = put_grads(f"ffn{layer}", {"w_gate_t": g_gate, "w_up_t": g_up, "w_down": g_down})
        return _rms_bwd(h, W["g_ffn"][layer:layer + 1] + tok[:1, :1], du, dres=dh, name=f"rms_ffn_bwd{layer}")

    dh3, dh3b, g_gffn1 = ffn_bwd(dh, dhb, h3, u3, a1, b1, 1)

    do_c = _mm(dh3b, W["od_w_out"], dims="nt", name="out_od_dx")
    g_od_out = _mm(o_c, dh3b, dims="tn", out_dtype=BF16, name="out_od_dw")
    dq_c, dk_p, dv_p, dbias_w = _band_bwd(qkv, k_pad, v_pad, p_c, do_c)
    dqkv = jnp.concatenate([dq_c, dk_p[PAD_KEYS:], dv_p[PAD_KEYS:]], axis=1)
    tok = put_grads("mix1", {"od_w_qkv_t": _mm(dqkv, u2, dims="tn", out_dtype=BF16, name="proj_qkv_dw"),
                             "od_w_out": g_od_out})
    ddiag = _toeplitz_bwd(dbias_w)
    n_far = BAND_W - REL_CLIP + 1
    G["od_rel_bias"] = jnp.concatenate(
        [jnp.zeros((C_HEADS, REL_CLIP - BAND_TQ + 1), F32), ddiag[:, n_far:][:, ::-1],
         jnp.sum(ddiag[:, :n_far], axis=1, keepdims=True)], axis=1)
    dh2, dh2b, g_gmix1 = _mm_rms_bwd(dqkv, W["od_w_qkv_t"], h2, W["g_mix"][1:2] + tok[:1, :1], dh3,
                                     name="proj_qkv_dx")

    dh1, dh1b, g_gffn0 = ffn_bwd(dh2, dh2b, h1, u1, a0, b0, 0)
    G["g_ffn"] = jnp.concatenate([g_gffn0, g_gffn1], axis=0)

    do_ab = _mm(dh1b, W["ev_w_out"], dims="nt", name="out_ev_dx")
    g0 = {"ev_w_out": _mm(o_ab, dh1b, dims="tn", out_dtype=BF16, name="out_ev_dw")}
    dqa_raw, dkn, dva, dkr = _mla_bwd(qa_raw, cos_q, sin_q, kv, kr, o_a, lse, do_ab, 0)
    dlat, g0["w_uq_t"], g0["w_ukv_t"], G["g_cq"], G["g_ckv"] = _latent_bwd(
        proj, nq, nkv, dqa_raw, dkn, dva, dkr, cos_k, sin_k, W["g_cq"], W["g_ckv"], W["w_uq_t"], W["w_ukv_t"])
    tok = put_grads("mix0", g0)
    dqb, dkb, dvb = _sb_bwd(proj, COL_SB, o_b32, w_b, sp_b, do_ab, MLA_HEADS // 2, tok)
    dproj = jnp.concatenate([dlat, dqb, dkb, dvb], axis=1)
    tok = put_grads("in0", {"w_in_t": _mm(dproj, u0, dims="tn", name="proj_in_dw")})
    dx, _, g_gmix0 = _mm_rms_bwd(dproj, W["w_in_t"], x, W["g_mix"][0:1] + tok[:1, :1], dh1, name="proj_in_dx")
    G["g_mix"] = jnp.concatenate([g_gmix0, g_gmix1], axis=0)
    return loss[0, 0], dx, G


_BIG = ["ev_w_in", "ev_w_uq", "ev_w_ukv", "ev_w_out", "od_w_qkv", "od_w_out", "w_gate", "w_up", "w_down"]
_COL_SHARDED = {"ev_w_in", "ev_w_uq", "ev_w_ukv", "od_w_qkv", "w_gate", "w_up"}
_SMALL = ["ev_g_cq", "ev_g_ckv", "od_rel_bias", "g_mix", "g_ffn", "g_final"]
_GROUPS = {
    "in0": ["ev_w_in"],
    "mix0": ["ev_w_uq", "ev_w_ukv", "ev_w_out"],
    "ffn0": ["w_gate0", "w_up0", "w_down0"],
    "mix1": ["od_w_qkv", "od_w_out"],
    "ffn1": ["w_gate1", "w_up1", "w_down1"],
}
_GROUP_SRC = {n + str(l): (n, l) for n in ("w_gate", "w_up", "w_down") for l in (0, 1)}
_BATCHES = {"in0": ["in0"], "layer0": ["mix0", "ffn0"], "layer1": ["mix1", "ffn1"]}
_BATCH_OF = {grp: batch for batch, grps in _BATCHES.items() for grp in grps}
_SMALL_ROWS = 8
_SMALL_COLS = 1792


def _pack_small(vals):
    flat = jnp.concatenate([v.reshape(-1).astype(F32) for v in vals])
    flat = jnp.pad(flat, (0, _SMALL_ROWS * _SMALL_COLS - flat.shape[0]))
    return flat.reshape(_SMALL_ROWS, _SMALL_COLS)


def _unpack_small(packed, like):
    flat = packed.reshape(-1)
    out, off = [], 0
    for v in like:
        out.append(flat[off:off + v.size].reshape(v.shape))
        off += v.size
    return out


def kernel(x, ev_w_in, ev_g_cq, ev_w_uq, ev_g_ckv, ev_w_ukv, ev_w_out, od_w_qkv, od_rel_bias, od_w_out, g_mix, g_ffn, w_gate, w_up, w_down, g_final, loss_target, m_ev_w_in, m_ev_g_cq, m_ev_w_uq, m_ev_g_ckv, m_ev_w_ukv, m_ev_w_out, m_od_w_qkv, m_od_rel_bias, m_od_w_out, m_g_mix, m_g_ffn, m_w_gate, m_w_up, m_w_down, m_g_final, v_ev_w_in, v_ev_g_cq, v_ev_w_uq, v_ev_g_ckv, v_ev_w_ukv, v_ev_w_out, v_od_w_qkv, v_od_rel_bias, v_od_w_out, v_g_mix, v_g_ffn, v_w_gate, v_w_up, v_w_down, v_g_final):
    args = dict(locals())
    w = {n: args[n] for n in _BIG + _SMALL}
    mom = {n: args["m_" + n] for n in _BIG + _SMALL}
    var = {n: args["v_" + n] for n in _BIG + _SMALL}

    own = {}
    for grp, names in _GROUPS.items():
        for n in names:
            base, layer = _GROUP_SRC.get(n, (n, 0))
            shard = w[base][layer:layer + 1]
            own[n] = (jnp.swapaxes(shard, 1, 2) if base in _COL_SHARDED else shard).astype(BF16)
    placed = dict(zip(own, _place_own(list(own.values()), [False] * len(own), name="gather_own")))
    handles, token = _gather2_start(
        [[own[n] for n in names] for names in _GROUPS.values()],
        [[placed[n] for n in names] for names in _GROUPS.values()], x[0, :8, :LANES], name="gather_start")
    gather = dict(zip(_GROUPS, handles))
    pass_before = {"in0": ["in0"], "mix0": ["mix0"]}
    pass_after = {"ffn0": ("mix1", "g_ffn"), "mix1": ("ffn1", "g_mix")}

    def prefetch(grp, after):
        gather[grp], tok = _gather2_pass_on(gather[grp], after, name="gather_pass_" + grp)
        return tok

    def get_weights(grp, after):
        names = _GROUPS[grp]
        after = token if after is None else after
        for g in pass_before.get(grp, []):
            gather[g], _ = _gather2_pass_on(gather[g], after, name="gather_pass_" + g)
        lands = _gather2_wait(gather[grp], after, name="gather_wait_" + grp)
        full = {n: l.reshape(-1, l.shape[-1]) for n, l in zip(names, lands)}
        out = {}
        if grp in pass_after:
            g, gain = pass_after[grp]
            gather[g], tok = _gather2_pass_on(gather[g], lands[0], name="gather_pass_" + g)
            out[gain] = small[gain] + tok[:1, :1]
        if grp == "in0":
            out.update({"w_in_t": _w_in_local(full["ev_w_in"])})
        elif grp == "mix0":
            out.update({"w_uq_t": _w_uq_local(full["ev_w_uq"]), "w_ukv_t": _w_ukv_local(full["ev_w_ukv"]),
                        "ev_w_out": full["ev_w_out"]})
        elif grp == "mix1":
            out.update({"od_w_qkv_t": full["od_w_qkv"], "od_w_out": full["od_w_out"]})
        else:
            layer = grp[-1]
            out.update({"w_gate_t" + layer: full["w_gate" + layer], "w_up_t" + layer: full["w_up" + layer],
                        "w_down" + layer: full["w_down" + layer]})
        return out

    scatter, pending = {}, {}

    def put_grads(grp, g):
        if grp == "in0":
            g = {"ev_w_in": _w_in_grad(g["w_in_t"])}
        elif grp == "mix0":
            g = {"ev_w_uq": _w_uq_grad(g["w_uq_t"]), "ev_w_ukv": _w_ukv_grad(g["w_ukv_t"]),
                 "ev_w_out": g["ev_w_out"]}
        elif grp == "mix1":
            g = {"od_w_qkv": g["od_w_qkv_t"], "od_w_out": g["od_w_out"]}
        else:
            layer = grp[-1]
            g = {"w_gate" + layer: g["w_gate_t"], "w_up" + layer: g["w_up_t"], "w_down" + layer: g["w_down"]}
        pending.update({n: v.reshape(N_DEV, 1, v.shape[0] // N_DEV, v.shape[1]).astype(BF16) for n, v in g.items()})
        batch = _BATCH_OF[grp]
        names = [n for gr in _BATCHES[batch] for n in _GROUPS[gr]]
        if batch == "in0" or not all(n in pending for n in names):
            return jnp.zeros((8, LANES), F32)
        send = [pending[n] for n in names]
        scatter[batch], tok = _exchange_start(send, [True] * len(names), send[0], name="scatter_start_" + batch)
        return tok

    small = {"g_cq": ev_g_cq, "g_ckv": ev_g_ckv, "od_rel_bias": od_rel_bias[0],
             "g_mix": g_mix + token[0, 0], "g_ffn": g_ffn, "g_final": g_final.reshape(1, -1)}
    loss_part, dx, G = _local_step(x[0], loss_target[0], small, get_weights, put_grads, prefetch)
    g_small = _pack_small([G["g_cq"], G["g_ckv"], G["od_rel_bias"], G["g_mix"], G["g_ffn"], G["g_final"],
                           loss_part.reshape(1)])
    scatter["in0"], _ = _exchange_start([pending["ev_w_in"], g_small], [True, False], dx, name="scatter_start_in0")

    grads, deltas, new_m, new_v = {}, {}, {}, {}
    parts, after = {}, dx

    def wait_parts(batch, after):
        lands = _exchange_wait(scatter[batch], after, name="scatter_wait_" + batch)
        parts.update(zip([n for grp in _BATCHES[batch] for n in _GROUPS[grp]], lands))
        return lands[0]

    def adamw(n):
        col = n in _COL_SHARDED
        rows = lambda a: (jnp.swapaxes(a, 1, 2) if col else a).reshape(-1, a.shape[1 if col else 2])
        layers = [parts[n]] if n in parts else [parts[n + "0"], parts[n + "1"]]
        res = _adamw(rows(w[n]), [p.reshape(N_DEV, -1, p.shape[-1]) for p in layers], rows(mom[n]), rows(var[n]),
                     name="adamw_" + n)
        L, a1, a2 = w[n].shape
        back = lambda r: jnp.swapaxes(r.reshape(L, a2, a1), 1, 2) if col else r.reshape(L, a1, a2)
        grads[n], deltas[n], new_m[n], new_v[n] = [back(r) for r in res]
        return res[0]

    for batch in ("layer1", "layer0"):
        after = wait_parts(batch, after)
    parts["ev_w_in"], small_parts = _exchange_wait(scatter["in0"], tuple(adamw(n) for n in _BIG[1:]),
                                                   name="scatter_wait_in0")
    adamw("ev_w_in")
    small_w = [w[n] for n in _SMALL]
    loss = jnp.sum(small_parts.reshape(N_DEV, -1)[:, sum(v.size for v in small_w)])
    res = _adamw(_pack_small(small_w), [small_parts], _pack_small([mom[n] for n in _SMALL]),
                 _pack_small([var[n] for n in _SMALL]), name="adamw_small")
    for d, packed in zip((grads, deltas, new_m, new_v), res):
        for n, val in zip(_SMALL, _unpack_small(packed, small_w)):
            d[n] = val

    order = ["ev_w_in", "ev_g_cq", "ev_w_uq", "ev_g_ckv", "ev_w_ukv", "ev_w_out", "od_w_qkv", "od_rel_bias",
             "od_w_out", "g_mix", "g_ffn", "w_gate", "w_up", "w_down", "g_final"]
    out = [loss, dx[None]]
    for d in (grads, deltas, new_m, new_v):
        out += [d[n] for n in order]
    return tuple(out)
```

```python
import functools

import numpy as np
import jax
import jax.numpy as jnp
from jax import lax
from jax.experimental import pallas as pl
from jax.experimental.pallas import tpu as pltpu

F32 = jnp.float32
BF16 = jnp.bfloat16

D_MODEL = 1024
CHUNK = 64
MLA_HEADS = 8
MLA_NOPE = 64
MLA_ROPE = 32
MLA_V = 64
Q_LORA = 384
KV_LORA = 256
ROPE_THETA = 10000.0
SB_HEADS = 8
SB_DIM = 64
C_HEADS = 16
C_DIM = 64
LEFT_CHUNKS = 8
REL_CLIP = 256
D_FF = 2816
RMS_EPS = 1e-6
ADAM_LR = 0.001
ADAM_B1 = 0.9
ADAM_B2 = 0.999
ADAM_EPS = 1e-08
ADAM_WD = 0.01
ADAM_STEP = 10

N_DEV = 8
LANES = 128
HEAD = 64
assert HEAD == MLA_NOPE == MLA_V == SB_DIM == C_DIM and 2 * HEAD == LANES
CHUNK_BITS = CHUNK.bit_length() - 1
assert 1 << CHUNK_BITS == CHUNK
VMEM_LIMIT = 56 * 1024 * 1024
NEG = -1e30
PAD_KEYS = LEFT_CHUNKS * CHUNK
BAND_TQ = 128
BAND_W = BAND_TQ + PAD_KEYS
TOEP_W = BAND_W + BAND_TQ

NN = (((1,), (0,)), ((), ()))
NT = (((1,), (1,)), ((), ()))
TN = (((0,), (0,)), ((), ()))


def _dot(a, b, dn):
    return lax.dot_general(a, b, dn, preferred_element_type=F32)


def _pick(dim, pref):
    if dim <= pref:
        return dim
    best = None
    for t in range(LANES, pref + 1, LANES):
        if dim % t == 0:
            best = t
    assert best is not None, (dim, pref)
    return best


def _params(sem):
    return pltpu.CompilerParams(dimension_semantics=sem, vmem_limit_bytes=VMEM_LIMIT)


def _mm(a, b, dims="nn", res=None, out_dtype=F32, name="mm"):
    if dims == "nn":
        (M, K), (K2, N) = a.shape, b.shape
    elif dims == "nt":
        (M, K), (N, K2) = a.shape, b.shape
    else:
        (K, M), (K2, N) = a.shape, b.shape
    assert K == K2, (a.shape, b.shape, dims)
    tm, tn, tk = _pick(M, 1024), _pick(N, 1152), _pick(K, 1024)
    nk = K // tk
    dn = {"nn": NN, "nt": NT, "tn": TN}[dims]
    has_res = res is not None

    def body(*refs):
        if has_res:
            a_ref, b_ref, r_ref, o_ref, acc = refs
        else:
            a_ref, b_ref, o_ref, acc = refs
        k = pl.program_id(2)

        @pl.when(k == 0)
        def _():
            acc[...] = jnp.zeros_like(acc)

        acc[...] += _dot(a_ref[...].astype(BF16), b_ref[...].astype(BF16), dn)

        @pl.when(k == nk - 1)
        def _():
            r = acc[...]
            if has_res:
                r = r + r_ref[...]
            o_ref[...] = r.astype(out_dtype)

    a_spec = (pl.BlockSpec((tk, tm), lambda i, j, k: (k, i)) if dims == "tn"
              else pl.BlockSpec((tm, tk), lambda i, j, k: (i, k)))
    b_spec = (pl.BlockSpec((tn, tk), lambda i, j, k: (j, k)) if dims == "nt"
              else pl.BlockSpec((tk, tn), lambda i, j, k: (k, j)))
    o_spec = pl.BlockSpec((tm, tn), lambda i, j, k: (i, j))
    in_specs = [a_spec, b_spec] + ([o_spec] if has_res else [])
    args = (a, b) + ((res,) if has_res else ())
    return pl.pallas_call(
        body, name=name, grid=(M // tm, N // tn, nk),
        in_specs=in_specs, out_specs=o_spec,
        out_shape=jax.ShapeDtypeStruct((M, N), out_dtype),
        scratch_shapes=[pltpu.VMEM((tm, tn), F32)],
        compiler_params=_params(("parallel", "parallel", "arbitrary")),
    )(*args)


def _rms_fwd(x, g, out_dtype=BF16, name="rms_fwd"):
    T, Fd = x.shape
    tm = _pick(T, 256)

    def body(x_ref, g_ref, o_ref):
        xv = x_ref[...]
        r = lax.rsqrt(jnp.mean(xv * xv, axis=-1, keepdims=True) + RMS_EPS)
        o_ref[...] = (xv * r * g_ref[...]).astype(out_dtype)

    return pl.pallas_call(
        body, name=name, grid=(T // tm,),
        in_specs=[pl.BlockSpec((tm, Fd), lambda i: (i, 0)), pl.BlockSpec((1, Fd), lambda i: (0, 0))],
        out_specs=pl.BlockSpec((tm, Fd), lambda i: (i, 0)),
        out_shape=jax.ShapeDtypeStruct((T, Fd), out_dtype),
        compiler_params=_params(("parallel",)),
    )(x, g)


def _rms_bwd(x, g, dy, dres=None, name="rms_bwd"):
    T, Fd = x.shape
    tm = _pick(T, 256)
    has_res = dres is not None

    def body(*refs):
        if has_res:
            x_ref, g_ref, dy_ref, r_ref, dx_ref, dxb_ref, dg_ref = refs
        else:
            x_ref, g_ref, dy_ref, dx_ref, dxb_ref, dg_ref = refs
        xv, dyv = x_ref[...], dy_ref[...]
        r = lax.rsqrt(jnp.mean(xv * xv, axis=-1, keepdims=True) + RMS_EPS)
        gdy = dyv * g_ref[...]
        dot = jnp.mean(xv * gdy, axis=-1, keepdims=True)
        dx = r * gdy - xv * (r * r * r * dot)
        if has_res:
            dx = dx + r_ref[...]
        dx_ref[...] = dx
        dxb_ref[...] = dx.astype(BF16)

        @pl.when(pl.program_id(0) == 0)
        def _():
            dg_ref[...] = jnp.zeros_like(dg_ref)

        dg_ref[...] += jnp.sum(dyv * xv * r, axis=0, keepdims=True)

    row = pl.BlockSpec((tm, Fd), lambda i: (i, 0))
    vec = pl.BlockSpec((1, Fd), lambda i: (0, 0))
    in_specs = [row, vec, row] + ([row] if has_res else [])
    args = (x, g, dy) + ((dres,) if has_res else ())
    return pl.pallas_call(
        body, name=name, grid=(T // tm,),
        in_specs=in_specs, out_specs=[row, row, vec],
        out_shape=[jax.ShapeDtypeStruct((T, Fd), F32), jax.ShapeDtypeStruct((T, Fd), BF16),
                   jax.ShapeDtypeStruct((1, Fd), F32)],
        compiler_params=_params(("arbitrary",)),
    )(*args)


def _mm_rms_bwd(a, b, x, g, dres, name="mm_rms_bwd"):
    T, K = a.shape
    Fd = b.shape[1]
    tm, tk = _pick(T, 1024), _pick(K, 1024)
    nk = K // tk

    def body(a_ref, b_ref, x_ref, g_ref, r_ref, dx_ref, dxb_ref, dg_ref, acc):
        i, k = pl.program_id(0), pl.program_id(1)

        @pl.when(k == 0)
        def _():
            acc[...] = jnp.zeros_like(acc)

        @pl.when((k == 0) & (i == 0))
        def _():
            dg_ref[...] = jnp.zeros_like(dg_ref)

        acc[...] += _dot(a_ref[...].astype(BF16), b_ref[...].astype(BF16), NN)

        @pl.when(k == nk - 1)
        def _():
            xv, dyv = x_ref[...], acc[...]
            r = lax.rsqrt(jnp.mean(xv * xv, axis=-1, keepdims=True) + RMS_EPS)
            gdy = dyv * g_ref[...]
            dot = jnp.mean(xv * gdy, axis=-1, keepdims=True)
            dx = r * gdy - xv * (r * r * r * dot) + r_ref[...]
            dx_ref[...] = dx
            dxb_ref[...] = dx.astype(BF16)
            dg_ref[...] += jnp.sum(dyv * xv * r, axis=0, keepdims=True)

    row = pl.BlockSpec((tm, Fd), lambda i, k: (i, 0))
    vec = pl.BlockSpec((1, Fd), lambda i, k: (0, 0))
    return pl.pallas_call(
        body, name=name, grid=(T // tm, nk),
        in_specs=[pl.BlockSpec((tm, tk), lambda i, k: (i, k)), pl.BlockSpec((tk, Fd), lambda i, k: (k, 0)),
                  row, vec, row],
        out_specs=[row, row, vec],
        out_shape=[jax.ShapeDtypeStruct((T, Fd), F32), jax.ShapeDtypeStruct((T, Fd), BF16),
                   jax.ShapeDtypeStruct((1, Fd), F32)],
        scratch_shapes=[pltpu.VMEM((tm, Fd), F32)],
        compiler_params=_params(("arbitrary", "arbitrary")),
    )(a, b, x, g, dres)


def _latent_bwd(proj, nq, nkv, dqa, dkn, dva, dkr, cos_k, sin_k, g_cq, g_ckv, w_uq_t, w_ukv_t, name="latent_bwd"):
    T = proj.shape[0]
    tm = _pick(T, 512)
    wl = _O2

    def rms_bwd(xv, gv, dyv):
        r = lax.rsqrt(jnp.mean(xv * xv, axis=-1, keepdims=True) + RMS_EPS)
        gdy = dyv * gv
        dot = jnp.mean(xv * gdy, axis=-1, keepdims=True)
        return r * gdy - xv * (r * r * r * dot), jnp.sum(dyv * xv * r, axis=0, keepdims=True)

    def body(p_ref, nq_ref, nkv_ref, dqa_ref, dkn_ref, dva_ref, dkr_ref, c_ref, s_ref, gq_ref, gkv_ref, wq_ref, wkv_ref,
             dlat_ref, dwq_ref, dwkv_ref, dgq_ref, dgkv_ref):
        @pl.when(pl.program_id(0) == 0)
        def _():
            for ref in (dwq_ref, dwkv_ref, dgq_ref, dgkv_ref):
                ref[...] = jnp.zeros_like(ref)

        dqv = dqa_ref[...]
        dkv = jnp.concatenate([dkn_ref[...], dva_ref[...]], axis=1)
        pv = p_ref[...]
        dc_q, dgq = rms_bwd(pv[:, :_O1], gq_ref[...], _dot(dqv, wq_ref[...], NN))
        dc_kv, dgkv = rms_bwd(pv[:, _O1:], gkv_ref[...], _dot(dkv, wkv_ref[...], NN))
        dkr_raw = _rotate(dkr_ref[...], c_ref[...], -s_ref[...])
        dlat_ref[...] = jnp.concatenate([dc_q, dc_kv, dkr_raw], axis=1).astype(BF16)
        dwq_ref[...] += _dot(dqv, nq_ref[...], TN)
        dwkv_ref[...] += _dot(dkv, nkv_ref[...], TN)
        dgq_ref[...] += dgq
        dgkv_ref[...] += dgkv

    row = lambda w: pl.BlockSpec((tm, w), lambda i: (i, 0))
    const = lambda a: pl.BlockSpec(a.shape, lambda i: (0, 0))
    outs = [jax.ShapeDtypeStruct((T, wl + LANES), BF16), jax.ShapeDtypeStruct(w_uq_t.shape, F32),
            jax.ShapeDtypeStruct(w_ukv_t.shape, F32), jax.ShapeDtypeStruct(g_cq.shape, F32),
            jax.ShapeDtypeStruct(g_ckv.shape, F32)]
    return pl.pallas_call(
        body, name=name, grid=(T // tm,),
        in_specs=[row(wl), row(_O1), row(_O2 - _O1), row(dqa.shape[1]), row(dkn.shape[1]), row(dva.shape[1]),
                  row(LANES), row(LANES), row(LANES), const(g_cq), const(g_ckv), const(w_uq_t), const(w_ukv_t)],
        out_specs=[row(wl + LANES)] + [const(o) for o in outs[1:]],
        out_shape=outs,
        compiler_params=_params(("arbitrary",)),
    )(proj, nq, nkv, dqa, dkn, dva, dkr, cos_k, sin_k, g_cq, g_ckv, w_uq_t, w_ukv_t)


def _loss_head(h, g, target, name="loss_head"):
    T, Fd = h.shape
    tm = _pick(T, 256)

    def body(h_ref, g_ref, t_ref, loss_ref, dh_ref, dhb_ref, dg_ref):
        xv = h_ref[...]
        r = lax.rsqrt(jnp.mean(xv * xv, axis=-1, keepdims=True) + RMS_EPS)
        diff = xv * r * g_ref[...] - t_ref[...]
        part = 0.5 * jnp.sum(jnp.mean(diff * diff, axis=-1, keepdims=True), axis=0, keepdims=True)
        dyv = diff * (1.0 / Fd)
        gdy = dyv * g_ref[...]
        dot = jnp.mean(xv * gdy, axis=-1, keepdims=True)
        dh = r * gdy - xv * (r * r * r * dot)
        dh_ref[...] = dh
        dhb_ref[...] = dh.astype(BF16)

        @pl.when(pl.program_id(0) == 0)
        def _():
            dg_ref[...] = jnp.zeros_like(dg_ref)
            loss_ref[...] = jnp.zeros_like(loss_ref)

        dg_ref[...] += jnp.sum(dyv * xv * r, axis=0, keepdims=True)
        loss_ref[...] += jnp.broadcast_to(part, loss_ref.shape)

    row = pl.BlockSpec((tm, Fd), lambda i: (i, 0))
    vec = pl.BlockSpec((1, Fd), lambda i: (0, 0))
    return pl.pallas_call(
        body, name=name, grid=(T // tm,),
        in_specs=[row, vec, row],
        out_specs=[pl.BlockSpec((1, LANES), lambda i: (0, 0)), row, row, vec],
        out_shape=[jax.ShapeDtypeStruct((1, LANES), F32), jax.ShapeDtypeStruct((T, Fd), F32),
                   jax.ShapeDtypeStruct((T, Fd), BF16), jax.ShapeDtypeStruct((1, Fd), F32)],
        compiler_params=_params(("arbitrary",)),
    )(h, g, target)


FFN_TF = 256


def _ffn_fwd(h, g, wg_t, wu_t, wd, name="ffn_fwd"):
    T, Dm = h.shape
    Fh = wd.shape[0]
    tm = _pick(T, 2048)
    nf = Fh // FFN_TF

    def body(h_ref, g_ref, wg_ref, wu_ref, wd_ref, o_ref, u_ref, a_ref, b_ref):
        j = pl.program_id(1)

        @pl.when(j == 0)
        def _():
            xv = h_ref[...]
            r = lax.rsqrt(jnp.mean(xv * xv, axis=-1, keepdims=True) + RMS_EPS)
            u_ref[...] = (xv * r * g_ref[...]).astype(BF16)
            o_ref[...] = xv

        u = u_ref[...]
        a = _dot(u, wg_ref[...], NT).astype(BF16)
        b = _dot(u, wu_ref[...], NT).astype(BF16)
        a_ref[...] = a
        b_ref[...] = b
        af = a.astype(F32)
        s = (af * jax.nn.sigmoid(af) * b.astype(F32)).astype(BF16)
        o_ref[...] += _dot(s, wd_ref[...], NN)

    row = pl.BlockSpec((tm, Dm), lambda i, j: (i, 0))
    wblk = pl.BlockSpec((FFN_TF, Dm), lambda i, j: (j, 0))
    ablk = pl.BlockSpec((tm, FFN_TF), lambda i, j: (i, j))
    return pl.pallas_call(
        body, name=name, grid=(T // tm, nf),
        in_specs=[pl.BlockSpec((tm, Dm), lambda i, j: (i, 0), pipeline_mode=pl.Buffered(1)),
                  pl.BlockSpec((1, Dm), lambda i, j: (0, 0)), wblk, wblk, wblk],
        out_specs=[row, row, ablk, ablk],
        out_shape=[jax.ShapeDtypeStruct((T, Dm), F32), jax.ShapeDtypeStruct((T, Dm), BF16),
                   jax.ShapeDtypeStruct((T, Fh), BF16), jax.ShapeDtypeStruct((T, Fh), BF16)],
        compiler_params=_params(("parallel", "arbitrary")),
    )(h, g, wg_t, wu_t, wd)


def _ffn_bwd(dh, u, a, b, wg_t, wu_t, wd, name="ffn_bwd"):
    T, Dm = dh.shape
    Fh = wd.shape[0]
    nf = Fh // FFN_TF
    once = pl.Buffered(1)

    def body(dh_ref, u_ref, a_ref, b_ref, wg_ref, wu_ref, wd_ref, du_ref, dwg_ref, dwu_ref, dwd_ref):
        j = pl.program_id(0)

        @pl.when(j == 0)
        def _():
            du_ref[...] = jnp.zeros_like(du_ref)

        ds = _dot(dh_ref[...], wd_ref[...], NT)
        af, bf = a_ref[...].astype(F32), b_ref[...].astype(F32)
        sig = jax.nn.sigmoid(af)
        sa = af * sig
        dwd_ref[...] = _dot((sa * bf).astype(BF16), dh_ref[...], TN).astype(BF16)
        dab = jnp.concatenate([(ds * bf * (sig * (1.0 + af * (1.0 - sig)))).astype(BF16),
                               (ds * sa).astype(BF16)], axis=1)
        dw = _dot(dab, u_ref[...], TN)
        dwg_ref[...] = dw[:FFN_TF].astype(BF16)
        dwu_ref[...] = dw[FFN_TF:].astype(BF16)
        du_ref[...] += _dot(dab, jnp.concatenate([wg_ref[...], wu_ref[...]], axis=0), NN)

    full = lambda: pl.BlockSpec((T, Dm), lambda j: (0, 0), pipeline_mode=once)
    wblk = pl.BlockSpec((FFN_TF, Dm), lambda j: (j, 0))
    ablk = pl.BlockSpec((T, FFN_TF), lambda j: (0, j))
    return pl.pallas_call(
        body, name=name, grid=(nf,),
        in_specs=[full(), full(), ablk, ablk, wblk, wblk, wblk],
        out_specs=[pl.BlockSpec((T, Dm), lambda j: (0, 0)), wblk, wblk, wblk],
        out_shape=[jax.ShapeDtypeStruct((T, Dm), F32)] + [jax.ShapeDtypeStruct((Fh, Dm), BF16)] * 3,
        compiler_params=_params(("arbitrary",)),
    )(dh, u, a, b, wg_t, wu_t, wd)


def _rope(x, cos_t, sin_t, col0, ncols, out_dtype, name="rope"):
    T = x.shape[0]
    wt = cos_t.shape[1]
    tm = _pick(T, 256)
    nb = ncols * LANES // wt
    half = MLA_ROPE // 2

    def body(x_ref, c_ref, s_ref, o_ref):
        xv = x_ref[...].astype(F32)
        lane = lax.broadcasted_iota(jnp.int32, xv.shape, 1)
        first = (lane & (MLA_ROPE - 1)) < half
        swapped = jnp.where(first, pltpu.roll(xv, wt - half, 1), pltpu.roll(xv, half, 1))
        o_ref[...] = (xv * c_ref[...] + swapped * s_ref[...]).astype(out_dtype)

    off = col0 * LANES // wt
    return pl.pallas_call(
        body, name=name, grid=(T // tm, nb),
        in_specs=[pl.BlockSpec((tm, wt), lambda i, j: (i, j + off)),
                  pl.BlockSpec((tm, wt), lambda i, j: (i, 0)),
                  pl.BlockSpec((tm, wt), lambda i, j: (i, 0))],
        out_specs=pl.BlockSpec((tm, wt), lambda i, j: (i, j)),
        out_shape=jax.ShapeDtypeStruct((T, ncols * LANES), out_dtype),
        compiler_params=_params(("parallel", "parallel")),
    )(x, cos_t, sin_t)


ATT_TQ = 512
ATT_TK = 256
MLA_TK = 512


def _mla_masks(shape):
    lane = lax.broadcasted_iota(jnp.int32, shape, 1)
    m0 = (lane < HEAD) | ((lane >= LANES) & (lane < LANES + MLA_ROPE))
    m1 = ((lane >= HEAD) & (lane < LANES)) | ((lane >= LANES + MLA_ROPE) & (lane < LANES + 2 * MLA_ROPE))
    return m0, m1


def _by_twos(n, step, carry):
    carry = lax.fori_loop(0, n // 2, lambda i, c: step(2 * i + 1, step(2 * i, c)), carry)
    return lax.fori_loop(0, n % 2, lambda _, c: step(n - 1, c), carry)


def _chunk_ok(tq, tk, d):
    row = lax.broadcasted_iota(jnp.int32, (tq, tk), 0)
    col = lax.broadcasted_iota(jnp.int32, (tq, tk), 1) + d * tk
    return jnp.concatenate([(col >> CHUNK_BITS) <= (row >> CHUNK_BITS)] * 2, axis=0)


def _rotate(x, cos_t, sin_t):
    half = MLA_ROPE // 2
    lane = lax.broadcasted_iota(jnp.int32, x.shape, 1)
    first = (lane & (MLA_ROPE - 1)) < half
    swapped = jnp.where(first, pltpu.roll(x, x.shape[1] - half, 1), pltpu.roll(x, half, 1))
    return x * cos_t + swapped * sin_t


def _mla_fwd(q, cos_q, sin_q, kv, kr, name="mla_fwd"):
    T = q.shape[0]
    tq, tk = _pick(T, ATT_TQ), _pick(T, MLA_TK)
    nd = tq // tk
    npair = MLA_HEADS // 2
    scale = (MLA_NOPE + MLA_ROPE) ** -0.5

    def body(q_ref, c_ref, s_ref, kn_ref, v_ref, kr_ref, o_ref, lse_ref):
        m_idx = pl.program_id(1)
        qv = _rotate(q_ref[...], c_ref[...], s_ref[...]).astype(BF16)
        m0, m1 = _mla_masks(qv.shape)
        qs = jnp.concatenate([jnp.where(m0, qv, 0), jnp.where(m1, qv, 0)], axis=0).astype(BF16)

        def block(kb, carry, ok):
            ks = pl.ds(pl.multiple_of(kb * tk, tk), tk)
            kcat = jnp.concatenate([kn_ref[ks, :], kr_ref[ks, :]], axis=1)
            mx, l, acc = carry
            s = _dot(qs, kcat, NT) * scale
            if ok is not None:
                s = jnp.where(ok, s, NEG)
            mn = jnp.maximum(mx, jnp.max(s, axis=-1, keepdims=True))
            alpha = jnp.exp(mx - mn)
            p = jnp.exp(s - mn)
            return (mn, alpha * l + jnp.sum(p, axis=-1, keepdims=True),
                    alpha * acc + _dot(p.astype(BF16), v_ref[ks, :], NN))

        init = (jnp.full((2 * tq, 1), NEG, F32), jnp.zeros((2 * tq, 1), F32), jnp.zeros((2 * tq, LANES), F32))
        res = init
        for d in range(nd):
            res = block(m_idx * nd + d, res, _chunk_ok(tq, tk, d))
        mx, l, acc = _by_twos(m_idx * nd, lambda kb, c: block(kb, c, None), res)
        h0 = lax.broadcasted_iota(jnp.int32, (tq, LANES), 1) < HEAD
        o_ref[...] = _two_heads(acc * (1.0 / l), h0).astype(o_ref.dtype)
        lse_ref[...] = _two_heads(jnp.broadcast_to(mx + jnp.log(l), (2 * tq, LANES)), h0)

    full = lambda col: pl.BlockSpec((T, LANES), col)
    table = pl.BlockSpec((tq, 2 * LANES), lambda p, m: (m, 0))
    return pl.pallas_call(
        body, name=name, grid=(npair, T // tq),
        in_specs=[pl.BlockSpec((tq, 2 * LANES), lambda p, m: (m, p)), table, table,
                  full(lambda p, m: (0, p)), full(lambda p, m: (0, npair + p)), full(lambda p, m: (0, 0))],
        out_specs=[pl.BlockSpec((tq, LANES), lambda p, m: (m, p)),
                   pl.BlockSpec((tq, LANES), lambda p, m: (m, p))],
        out_shape=[jax.ShapeDtypeStruct((T, npair * LANES), BF16),
                   jax.ShapeDtypeStruct((T, npair * LANES), F32)],
        compiler_params=_params(("parallel", "arbitrary")),
    )(q, cos_q, sin_q, kv, kv, kr)


def _mla_bwd(q, cos_q, sin_q, kv, kr, o, lse, do, do_col0, name="mla_bwd"):
    T = q.shape[0]
    tq, tk = _pick(T, ATT_TQ), _pick(T, MLA_TK)
    nd = tq // tk
    npair = MLA_HEADS // 2
    scale = (MLA_NOPE + MLA_ROPE) ** -0.5

    def body(q_ref, c_ref, s_ref, kn_ref, v_ref, kr_ref, o_ref, lse_ref, do_ref, dq_ref, dkn_ref, dv_ref, dkr_ref,
             dkn_acc, dv_acc):
        p_idx, m_idx = pl.program_id(0), pl.program_id(1)

        @pl.when(m_idx == 0)
        def _():
            dkn_acc[...] = jnp.zeros_like(dkn_acc)
            dv_acc[...] = jnp.zeros_like(dv_acc)

        @pl.when((m_idx == 0) & (p_idx == 0))
        def _():
            dkr_ref[...] = jnp.zeros_like(dkr_ref)

        qv = _rotate(q_ref[...], c_ref[...], s_ref[...]).astype(BF16)
        m0, m1 = _mla_masks(qv.shape)
        qs = jnp.concatenate([jnp.where(m0, qv, 0), jnp.where(m1, qv, 0)], axis=0).astype(BF16)
        dov = do_ref[...].astype(F32)
        h0 = lax.broadcasted_iota(jnp.int32, (tq, LANES), 1) < HEAD
        dos32 = jnp.concatenate([jnp.where(h0, dov, 0.0), jnp.where(h0, 0.0, dov)], axis=0)
        ov = o_ref[...].astype(F32)
        delta = jnp.sum(dos32 * jnp.concatenate([ov, ov], axis=0), axis=-1, keepdims=True)
        dos = dos32.astype(BF16)
        lsev = lse_ref[...]
        lse = jnp.concatenate([lsev[:, 0:1], lsev[:, HEAD:HEAD + 1]], axis=0)

        def block(kb, dq, ok):
            ks = pl.ds(pl.multiple_of(kb * tk, tk), tk)
            kcat = jnp.concatenate([kn_ref[ks, :], kr_ref[ks, :]], axis=1)
            vv = v_ref[ks, :]
            p = jnp.exp(_dot(qs, kcat, NT) * scale - lse)
            if ok is not None:
                p = jnp.where(ok, p, 0.0)
            ds = (p * (_dot(dos, vv, NT) - delta) * scale).astype(BF16)
            dkc = _dot(ds, qs, TN)
            dkn_acc[ks, :] += dkc[:, :LANES]
            dkr_ref[ks, :] += dkc[:, LANES:]
            dv_acc[ks, :] += _dot(p.astype(BF16), dos, TN)
            return dq + _dot(ds, kcat, NN)

        dq = jnp.zeros((2 * tq, 2 * LANES), F32)
        for d in range(nd):
            dq = block(m_idx * nd + d, dq, _chunk_ok(tq, tk, d))
        dq = _by_twos(m_idx * nd, lambda kb, c: block(kb, c, None), dq)
        dq_ref[...] = _rotate(jnp.where(m0, dq[:tq], jnp.where(m1, dq[tq:], 0.0)), c_ref[...],
                              -s_ref[...]).astype(BF16)

        @pl.when(m_idx == T // tq - 1)
        def _():
            dkn_ref[...] = dkn_acc[...].astype(BF16)
            dv_ref[...] = dv_acc[...].astype(BF16)

    full = lambda col: pl.BlockSpec((T, LANES), col)
    blk = lambda col: pl.BlockSpec((tq, LANES), col)
    table = pl.BlockSpec((tq, 2 * LANES), lambda p, m: (m, 0))
    return pl.pallas_call(
        body, name=name, grid=(npair, T // tq),
        in_specs=[pl.BlockSpec((tq, 2 * LANES), lambda p, m: (m, p)), table, table,
                  full(lambda p, m: (0, p)), full(lambda p, m: (0, npair + p)), full(lambda p, m: (0, 0)),
                  blk(lambda p, m: (m, p)), blk(lambda p, m: (m, p)),
                  blk(lambda p, m: (m, do_col0 + p))],
        out_specs=[pl.BlockSpec((tq, 2 * LANES), lambda p, m: (m, p)),
                   full(lambda p, m: (0, p)), full(lambda p, m: (0, p)), full(lambda p, m: (0, 0))],
        out_shape=[jax.ShapeDtypeStruct((T, npair * 2 * LANES), BF16),
                   jax.ShapeDtypeStruct((T, npair * LANES), BF16),
                   jax.ShapeDtypeStruct((T, npair * LANES), BF16),
                   jax.ShapeDtypeStruct((T, LANES), F32)],
        scratch_shapes=[pltpu.VMEM((T, LANES), F32)] * 2,
        compiler_params=_params(("arbitrary", "arbitrary")),
    )(q, cos_q, sin_q, kv, kv, kr, o, lse, do)


def _split_dot(x, tri):
    hi = x.astype(BF16)
    lo = (x - hi.astype(F32)).astype(BF16)
    both = _dot(jnp.concatenate([hi, lo], axis=0), tri, NN)
    return both[:x.shape[0]] + both[x.shape[0]:]


def _sb_terms(qh, kk, before):
    z = _dot(qh, kk, NT)
    sp = jnp.maximum(z, 0.0) + jnp.log(1.0 + jnp.exp(-jnp.abs(z)))
    lk = -sp if before is None else jnp.where(before, -sp, 0.0)
    return z, sp, lk


def _sb_setup(q_ref, tq, tk, scale):
    qv = (q_ref[...].astype(F32) * scale).astype(BF16)
    lane = lax.broadcasted_iota(jnp.int32, (tq, LANES), 1)
    h0 = lane < HEAD
    qs = jnp.concatenate([jnp.where(h0, qv, 0), jnp.where(h0, 0, qv)], axis=0).astype(BF16)
    row = lax.broadcasted_iota(jnp.int32, (tk, tk), 0)
    col = lax.broadcasted_iota(jnp.int32, (tk, tk), 1)
    return qs, h0, row, col


def _sb_before(tq, tk, d):
    row = lax.broadcasted_iota(jnp.int32, (tq, tk), 0)
    col = lax.broadcasted_iota(jnp.int32, (tq, tk), 1) + d * tk
    return jnp.concatenate([col < row] * 2, axis=0)


def _two_heads(x, h0):
    tq = x.shape[0] // 2
    return jnp.where(h0, x[:tq], x[tq:])


def _sb_fwd(qkv, col0, dep, name="sb_fwd"):
    T = qkv.shape[0]
    tq, tk = _pick(T, ATT_TQ), _pick(T, ATT_TK)
    nd = tq // tk
    npair = SB_HEADS // 2
    scale = SB_DIM ** -0.5

    def body(q_ref, k_ref, v_ref, dep_ref, o_ref, o32_ref, w_ref, sp_ref):
        m_idx = pl.program_id(1)
        qs, h0, row, col = _sb_setup(q_ref, tq, tk, scale)
        later = (row > col).astype(BF16)

        def block(kb, carry, before):
            ks = pl.ds(pl.multiple_of(kb * tk, tk), tk)
            c, acc = carry
            z, sp, lk = _sb_terms(qs, k_ref[ks, :].astype(BF16), before)
            w = jnp.exp((z - sp) + _split_dot(lk, later) + c)
            if before is not None:
                w = jnp.where(before, w, 0.0)
            wb = w.astype(BF16)
            w_ref[0, 0, kb] = wb
            sp_ref[0, 0, kb] = sp.astype(BF16)
            return (c + jnp.sum(lk, axis=-1, keepdims=True), acc + _dot(wb, v_ref[ks, :].astype(BF16), NN))

        init = (jnp.zeros((2 * tq, 1), F32), jnp.zeros((2 * tq, LANES), F32))
        res = init
        for d in reversed(range(nd)):
            res = block(m_idx * nd + d, res, _sb_before(tq, tk, d))
        res = _by_twos(m_idx * nd, lambda i, c: block(m_idx * nd - 1 - i, c, None), res)
        o = _two_heads(res[1], h0)
        o_ref[...] = o.astype(o_ref.dtype)
        o32_ref[...] = o

    full = lambda col: pl.BlockSpec((T, LANES), col)
    blk = pl.BlockSpec((tq, LANES), lambda p, m: (m, p))
    return pl.pallas_call(
        body, name=name, grid=(npair, T // tq),
        in_specs=[pl.BlockSpec((tq, LANES), lambda p, m: (m, col0 + p)),
                  full(lambda p, m: (0, col0 + npair + p)), full(lambda p, m: (0, col0 + 2 * npair + p)),
                  pl.BlockSpec((8, LANES), lambda p, m: (0, 0))],
        out_specs=[blk, blk] + [pl.BlockSpec((1, 1, T // tk, 2 * tq, tk), lambda p, m: (p, m, 0, 0, 0))] * 2,
        out_shape=[jax.ShapeDtypeStruct((T, npair * LANES), BF16), jax.ShapeDtypeStruct((T, npair * LANES), F32)]
        + [jax.ShapeDtypeStruct((npair, T // tq, T // tk, 2 * tq, tk), BF16)] * 2,
        compiler_params=_params(("parallel", "arbitrary")),
    )(qkv, qkv, qkv, dep)


def _sb_bwd(qkv, col0, o32, w_all, sp_all, do, do_col0, dep, name="sb_bwd"):
    T = qkv.shape[0]
    tq, tk = _pick(T, ATT_TQ), _pick(T, ATT_TK)
    nd = tq // tk
    npair = SB_HEADS // 2
    scale = SB_DIM ** -0.5

    def body(q_ref, k_ref, v_ref, o_ref, w_ref, sp_ref, do_ref, dep_ref, dq_ref, dk_ref, dv_ref, dk_acc, dv_acc):
        m_idx = pl.program_id(1)

        @pl.when(m_idx == 0)
        def _():
            dk_acc[...] = jnp.zeros_like(dk_acc)
            dv_acc[...] = jnp.zeros_like(dv_acc)

        qs, h0, row, col = _sb_setup(q_ref, tq, tk, scale)
        dov = do_ref[...].astype(F32)
        dos = jnp.concatenate([jnp.where(h0, dov, 0.0), jnp.where(h0, 0.0, dov)], axis=0).astype(BF16)
        ov = o_ref[...]
        etot = jnp.sum(dos.astype(F32) * jnp.concatenate([ov, ov], axis=0), axis=-1, keepdims=True)
        from_here = (row >= col).astype(BF16)

        def block(kb, carry, before):
            ks = pl.ds(pl.multiple_of(kb * tk, tk), tk)
            kk = k_ref[ks, :].astype(BF16)
            vv = v_ref[ks, :].astype(BF16)
            es, dqa = carry
            wb = w_ref[0, 0, kb]
            e = wb.astype(F32) * _dot(dos, vv, NT)
            prev = etot - (_split_dot(e, from_here) + es)
            sig_neg = jnp.exp(-sp_ref[0, 0, kb].astype(F32))
            dz = e * sig_neg - (1.0 - sig_neg) * prev
            if before is not None:
                dz = jnp.where(before, dz, 0.0)
            dzb = dz.astype(BF16)
            dk_acc[ks, :] += _dot(dzb, qs, TN)
            dv_acc[ks, :] += _dot(wb, dos, TN)
            return es + jnp.sum(e, axis=-1, keepdims=True), dqa + _dot(dzb, kk, NN)

        init = (jnp.zeros((2 * tq, 1), F32), jnp.zeros((2 * tq, LANES), F32))
        res = init
        for d in reversed(range(nd)):
            res = block(m_idx * nd + d, res, _sb_before(tq, tk, d))
        res = _by_twos(m_idx * nd, lambda i, c: block(m_idx * nd - 1 - i, c, None), res)
        dq_ref[...] = (_two_heads(res[1], h0) * scale).astype(BF16)

        @pl.when(m_idx == T // tq - 1)
        def _():
            dk_ref[...] = dk_acc[...].astype(BF16)
            dv_ref[...] = dv_acc[...].astype(BF16)

    full = lambda col: pl.BlockSpec((T, LANES), col)
    blk = lambda col: pl.BlockSpec((tq, LANES), col)
    return pl.pallas_call(
        body, name=name, grid=(npair, T // tq),
        in_specs=[blk(lambda p, m: (m, col0 + p)),
                  full(lambda p, m: (0, col0 + npair + p)), full(lambda p, m: (0, col0 + 2 * npair + p)),
                  blk(lambda p, m: (m, p)),
                  pl.BlockSpec((1, 1, T // tk, 2 * tq, tk), lambda p, m: (p, m, 0, 0, 0)),
                  pl.BlockSpec((1, 1, T // tk, 2 * tq, tk), lambda p, m: (p, m, 0, 0, 0)),
                  blk(lambda p, m: (m, do_col0 + p)), pl.BlockSpec((8, LANES), lambda p, m: (0, 0))],
        out_specs=[blk(lambda p, m: (m, p)), full(lambda p, m: (0, p)), full(lambda p, m: (0, p))],
        out_shape=[jax.ShapeDtypeStruct((T, npair * LANES), BF16)] * 3,
        scratch_shapes=[pltpu.VMEM((T, LANES), F32)] * 2,
        compiler_params=_params(("arbitrary", "arbitrary")),
    )(qkv, qkv, qkv, o32, w_all, sp_all, do, dep)


def _band_in_window():
    cq = lax.broadcasted_iota(jnp.int32, (BAND_TQ, BAND_W), 0) >> CHUNK_BITS
    ckp = lax.broadcasted_iota(jnp.int32, (BAND_TQ, BAND_W), 1) >> CHUNK_BITS
    return (ckp >= cq) & (ckp <= cq + LEFT_CHUNKS)


def _band_real(m_idx):
    j = lax.broadcasted_iota(jnp.int32, (BAND_TQ, BAND_W), 1)
    return j >= PAD_KEYS - m_idx * BAND_TQ


def _band_probs(qh, kw, bias, real, scale):
    s = jnp.where(real, _dot(qh, kw, NT) * scale + bias, NEG)
    e = jnp.exp(s - jnp.max(s, axis=-1, keepdims=True))
    return e * (1.0 / jnp.sum(e, axis=-1, keepdims=True))


BAND_SUB = 4


def _band_fwd(qkv, k_pad, v_pad, bias_w, name="band_fwd"):
    T = qkv.shape[0]
    npair = C_HEADS // 2
    scale = C_DIM ** -0.5
    rows = BAND_SUB * BAND_TQ

    def body(q_ref, k_ref, v_ref, b_ref, o_ref, p_ref):
        lane = lax.broadcasted_iota(jnp.int32, (BAND_TQ, LANES), 1)
        h0 = lane < HEAD
        bias = jnp.concatenate([b_ref[0], b_ref[1]], axis=0)
        for sub in range(BAND_SUB):
            m_idx = pl.program_id(1) * BAND_SUB + sub
            win = pl.ds(pl.multiple_of(m_idx * BAND_TQ, BAND_TQ), BAND_W)
            kw, vw = k_ref[win, :], v_ref[win, :]
            qv = q_ref[sub * BAND_TQ:(sub + 1) * BAND_TQ, :]
            qs = jnp.concatenate([jnp.where(h0, qv, 0), jnp.where(h0, 0, qv)], axis=0).astype(BF16)
            p = _band_probs(qs, kw, bias, jnp.concatenate([_band_real(m_idx)] * 2, axis=0), scale).astype(BF16)
            p_ref[0, sub] = p
            o = _two_heads(_dot(p, vw, NN), h0)
            o_ref[sub * BAND_TQ:(sub + 1) * BAND_TQ, :] = o.astype(o_ref.dtype)

    Tp = T + PAD_KEYS
    return pl.pallas_call(
        body, name=name, grid=(npair, T // rows),
        in_specs=[pl.BlockSpec((rows, LANES), lambda p, m: (m, p)),
                  pl.BlockSpec((Tp, LANES), lambda p, m: (0, p)),
                  pl.BlockSpec((Tp, LANES), lambda p, m: (0, p)),
                  pl.BlockSpec((2, BAND_TQ, BAND_W), lambda p, m: (p, 0, 0))],
        out_specs=[pl.BlockSpec((rows, LANES), lambda p, m: (m, p)),
                   pl.BlockSpec((1, BAND_SUB, 2 * BAND_TQ, BAND_W), lambda p, m: (p, m, 0, 0))],
        out_shape=[jax.ShapeDtypeStruct((T, npair * LANES), BF16),
                   jax.ShapeDtypeStruct((npair, T // BAND_TQ, 2 * BAND_TQ, BAND_W), BF16)],
        compiler_params=_params(("parallel", "arbitrary")),
    )(qkv, k_pad, v_pad, bias_w)


def _band_bwd(qkv, k_pad, v_pad, probs, do, name="band_bwd"):
    T = qkv.shape[0]
    npair = C_HEADS // 2
    scale = C_DIM ** -0.5

    rows = BAND_SUB * BAND_TQ

    def body(q_ref, k_ref, v_ref, p_ref, do_ref, dq_ref, dk_ref, dv_ref, db_ref, dk_acc, dv_acc):
        @pl.when(pl.program_id(1) == 0)
        def _():
            dk_acc[...] = jnp.zeros_like(dk_acc)
            dv_acc[...] = jnp.zeros_like(dv_acc)
            db_ref[...] = jnp.zeros_like(db_ref)

        lane = lax.broadcasted_iota(jnp.int32, (BAND_TQ, LANES), 1)
        h0 = lane < HEAD
        dbs = jnp.zeros((2 * BAND_TQ, BAND_W), F32)
        for sub in range(BAND_SUB):
            m_idx = pl.program_id(1) * BAND_SUB + sub
            win = pl.ds(pl.multiple_of(m_idx * BAND_TQ, BAND_TQ), BAND_W)
            kw, vw = k_ref[win, :], v_ref[win, :]
            qv = q_ref[sub * BAND_TQ:(sub + 1) * BAND_TQ, :]
            dov = do_ref[sub * BAND_TQ:(sub + 1) * BAND_TQ, :].astype(F32)
            qs = jnp.concatenate([jnp.where(h0, qv, 0), jnp.where(h0, 0, qv)], axis=0).astype(BF16)
            dos = jnp.concatenate([jnp.where(h0, dov, 0.0), jnp.where(h0, 0.0, dov)], axis=0).astype(BF16)
            pb = p_ref[0, sub]
            p = pb.astype(F32)
            dp = _dot(dos, vw, NT)
            dsb = p * (dp - jnp.sum(p * dp, axis=-1, keepdims=True))
            dbs = dbs + dsb
            dsq = (dsb * scale).astype(BF16)
            dq_ref[sub * BAND_TQ:(sub + 1) * BAND_TQ, :] = _two_heads(_dot(dsq, kw, NN), h0).astype(BF16)
            dk_acc[win, :] += _dot(dsq, qs, TN)
            dv_acc[win, :] += _dot(pb, dos, TN)
        db_ref[0] += dbs[:BAND_TQ]
        db_ref[1] += dbs[BAND_TQ:]

        @pl.when(pl.program_id(1) == T // rows - 1)
        def _():
            dk_ref[...] = dk_acc[...].astype(BF16)
            dv_ref[...] = dv_acc[...].astype(BF16)

    Tp = T + PAD_KEYS
    blk = lambda col: pl.BlockSpec((rows, LANES), col)
    full = pl.BlockSpec((Tp, LANES), lambda p, m: (0, p))
    bias = pl.BlockSpec((2, BAND_TQ, BAND_W), lambda p, m: (p, 0, 0))
    prob = pl.BlockSpec((1, BAND_SUB, 2 * BAND_TQ, BAND_W), lambda p, m: (p, m, 0, 0))
    return pl.pallas_call(
        body, name=name, grid=(npair, T // rows),
        in_specs=[blk(lambda p, m: (m, p)), full, full, prob, blk(lambda p, m: (m, p))],
        out_specs=[blk(lambda p, m: (m, p)), full, full, bias],
        out_shape=[jax.ShapeDtypeStruct((T, npair * LANES), BF16),
                   jax.ShapeDtypeStruct((Tp, npair * LANES), BF16),
                   jax.ShapeDtypeStruct((Tp, npair * LANES), BF16),
                   jax.ShapeDtypeStruct((C_HEADS, BAND_TQ, BAND_W), F32)],
        scratch_shapes=[pltpu.VMEM((Tp, LANES), F32)] * 2,
        compiler_params=_params(("arbitrary", "arbitrary")),
    )(qkv, k_pad, v_pad, probs, do)


def _skew_bits(x, left):
    w = x.shape[1]
    row = lax.broadcasted_iota(jnp.int32, x.shape, 0)
    for b in range(BAND_TQ.bit_length() - 1):
        amt = (w - (1 << b)) if left else (1 << b)
        x = jnp.where((row >> b) & 1 == 1, pltpu.roll(x, amt, 1), x)
    return x


def _toeplitz(diag, name="toeplitz"):
    H = diag.shape[0]

    def body(d_ref, o_ref):
        x = jnp.broadcast_to(d_ref[0], (BAND_TQ, TOEP_W))
        o_ref[0] = jnp.where(_band_in_window(), _skew_bits(x, left=False)[:, BAND_TQ:], NEG)

    return pl.pallas_call(
        body, name=name, grid=(H,),
        in_specs=[pl.BlockSpec((1, 1, TOEP_W), lambda h: (h, 0, 0))],
        out_specs=pl.BlockSpec((1, BAND_TQ, BAND_W), lambda h: (h, 0, 0)),
        out_shape=jax.ShapeDtypeStruct((H, BAND_TQ, BAND_W), F32),
        compiler_params=_params(("parallel",)),
    )(diag.reshape(H, 1, TOEP_W))


def _toeplitz_bwd(dbias, name="toeplitz_bwd"):
    H = dbias.shape[0]

    def body(d_ref, o_ref):
        x = jnp.concatenate([jnp.zeros((BAND_TQ, BAND_TQ), F32), d_ref[0]], axis=1)
        h = BAND_TQ // 2
        while h >= 8:
            x = x[:h] + pltpu.roll(x[h:2 * h], TOEP_W - h, 1)
            h //= 2
        o_ref[0] = jnp.sum(_skew_bits(x, left=True), axis=0, keepdims=True)

    return pl.pallas_call(
        body, name=name, grid=(H,),
        in_specs=[pl.BlockSpec((1, BAND_TQ, BAND_W), lambda h: (h, 0, 0))],
        out_specs=pl.BlockSpec((1, 1, TOEP_W), lambda h: (h, 0, 0)),
        out_shape=jax.ShapeDtypeStruct((H, 1, TOEP_W), F32),
        compiler_params=_params(("parallel",)),
    )(dbias).reshape(H, TOEP_W)


_HBM = pl.BlockSpec(memory_space=pltpu.HBM)
_SEM = pl.BlockSpec(memory_space=pltpu.SEMAPHORE)
_EFFECT = pltpu.SideEffectType.DATAFLOW_SIDE_EFFECTING


def _peers():
    x, y, c = lax.axis_index("x"), lax.axis_index("y"), lax.axis_index("c")
    out = []
    for k in range(1, N_DEV):
        peer = (1 - x if (k >> 2) & 1 else x, 1 - y if (k >> 1) & 1 else y, 1 - c if k & 1 else c)
        out.append((peer, 4 * peer[0] + 2 * peer[1] + peer[2]))
    return 4 * x + 2 * y + c, out


def _split_copies(ins, lands, scatter, send_sem, recv_sem, arriving):
    me, peers = _peers()
    out = []
    for a in range(len(ins)):
        for peer, idx in peers:
            out.append(pltpu.make_async_remote_copy(
                src_ref=ins[a].at[idx] if scatter[a] else ins[a],
                dst_ref=lands[a].at[idx if arriving else me], send_sem=send_sem, recv_sem=recv_sem,
                device_id=peer, device_id_type=pl.DeviceIdType.MESH))
    return out


def _landing_zones(arrays, scatter):
    return [lax.empty((N_DEV,) + (a.shape[1:] if s else a.shape), a.dtype) for a, s in zip(arrays, scatter)]


def _place_own(arrays, scatter, name):
    n = len(arrays)
    lands = _landing_zones(arrays, scatter)
    me = (4 * lax.axis_index("x") + 2 * lax.axis_index("y") + lax.axis_index("c")).astype(jnp.int32).reshape(1)

    def body(me_ref, *refs):
        for a in range(n):
            refs[2 * n + a][...] = refs[a][...].reshape(refs[2 * n + a].shape)

    def row_spec(shape):
        zeros = (0,) * (len(shape) - 1)
        return pl.BlockSpec((1,) + tuple(shape[1:]), lambda i, me_ref: (me_ref[0],) + zeros)

    in_specs = [row_spec(a.shape) if s else pl.BlockSpec(a.shape, lambda i, me_ref, nd=a.ndim: (0,) * nd)
                for a, s in zip(arrays, scatter)]
    return pl.pallas_call(
        body, name=name,
        out_shape=[jax.ShapeDtypeStruct(l.shape, l.dtype) for l in lands],
        grid_spec=pltpu.PrefetchScalarGridSpec(
            num_scalar_prefetch=1, grid=(1,),
            in_specs=in_specs + [pl.BlockSpec(memory_space=pl.ANY)] * n,
            out_specs=[row_spec(l.shape) for l in lands]),
        input_output_aliases={1 + n + i: i for i in range(n)},
        compiler_params=_params(("arbitrary",)),
    )(me, *arrays, *lands)


def _exchange_start_groups(groups, scatter, after, name, lands=None):
    sizes = [len(g) for g in groups]
    arrays = [a for g in groups for a in g]
    n, ng = len(arrays), len(groups)
    flags = list(scatter) if isinstance(scatter, (list, tuple)) else [scatter] * n
    if lands is None:
        lands = list(_place_own(arrays, flags, name=name.replace("_start_", "_own_")))
    else:
        lands = [l for g in lands for l in g]
    starts = np.cumsum([0] + sizes)

    def body(*refs):
        ins, lnd = refs[:n], refs[n:2 * n]
        sems = refs[2 * n + 1:2 * n + 1 + 2 * ng]
        token = refs[-1]
        for g in range(ng):
            sl = slice(starts[g], starts[g + 1])
            for cp in _split_copies(ins[sl], lnd[sl], flags[sl], sems[2 * g], sems[2 * g + 1], arriving=False):
                cp.start()
        token[...] = jnp.zeros_like(token)

    hbm = lambda a: pltpu.HBM(a.shape, a.dtype)
    out = pl.pallas_call(
        body, name=name,
        out_shape=(*[pltpu.SemaphoreType.DMA(())] * (2 * ng),
                   *[hbm(a) for a in arrays], *[hbm(a) for a in lands],
                   jax.ShapeDtypeStruct((8, LANES), F32)),
        in_specs=[_HBM] * (2 * n) + [pl.BlockSpec(memory_space=pl.ANY)],
        out_specs=(*[_SEM] * (2 * ng), *([_HBM] * (2 * n)), pl.BlockSpec(memory_space=pltpu.VMEM)),
        input_output_aliases={i: 2 * ng + i for i in range(2 * n)},
        compiler_params=pltpu.CompilerParams(has_side_effects=_EFFECT),
    )(*[pltpu.with_memory_space_constraint(a, pltpu.HBM) for a in list(arrays) + lands], after)
    ins_out, lands_out = out[2 * ng:2 * ng + n], out[2 * ng + n:2 * ng + 2 * n]
    handles = [(out[2 * g], out[2 * g + 1], list(ins_out[starts[g]:starts[g + 1]]),
                list(lands_out[starts[g]:starts[g + 1]]), tuple(flags[starts[g]:starts[g + 1]]))
               for g in range(ng)]
    return handles, out[-1]


def _exchange_start(arrays, scatter, after, name):
    handles, token = _exchange_start_groups([list(arrays)], list(scatter), after, name)
    return handles[0], token


def _exchange_wait(handle, after, name):
    send_sem, recv_sem, ins, lands, scatter = handle
    n = len(ins)
    after = after if isinstance(after, tuple) else (after,)

    def body(*refs):
        i_ref, l_ref = refs[:n], refs[n:2 * n]
        s_sem, r_sem = refs[2 * n:2 * n + 2]
        for cp in _split_copies(i_ref, l_ref, scatter, s_sem, r_sem, arriving=False):
            cp.wait_send()
        for cp in _split_copies(i_ref, l_ref, scatter, s_sem, r_sem, arriving=True):
            cp.wait_recv()

    hbm = lambda a: pltpu.HBM(a.shape, a.dtype)
    out = pl.pallas_call(
        body, name=name,
        out_shape=tuple(hbm(a) for a in ins + lands),
        in_specs=[_HBM] * (2 * n) + [_SEM, _SEM] + [pl.BlockSpec(memory_space=pl.ANY)] * len(after),
        out_specs=tuple([_HBM] * (2 * n)),
        input_output_aliases={i: i for i in range(2 * n)},
        compiler_params=pltpu.CompilerParams(has_side_effects=_EFFECT),
    )(*ins, *lands, send_sem, recv_sem, *after)
    return list(out[n:])


_SIBLING = 1
_CHIPS = (4, 2, 6)


def _peer_of(k):
    x, y, c = lax.axis_index("x"), lax.axis_index("y"), lax.axis_index("c")
    peer = (1 - x if (k >> 2) & 1 else x, 1 - y if (k >> 1) & 1 else y, 1 - c if k & 1 else c)
    return peer, 4 * peer[0] + 2 * peer[1] + peer[2]


def _rcopy(src, dst, send_sem, recv_sem, k):
    return pltpu.make_async_remote_copy(src_ref=src, dst_ref=dst, send_sem=send_sem, recv_sem=recv_sem,
                                        device_id=_peer_of(k)[0], device_id_type=pl.DeviceIdType.MESH)


def _gather2_start(groups, lands, after, name):
    sizes = [len(g) for g in groups]
    arrays = [a for g in groups for a in g]
    lands = [l for g in lands for l in g]
    n, ng = len(arrays), len(groups)
    starts = np.cumsum([0] + sizes)

    def body(*refs):
        ins, lnd = refs[:n], refs[n:2 * n]
        sems = refs[2 * n + 1:2 * n + 1 + 4 * ng]
        me, _ = _peers()
        for g in range(ng):
            send_d, recv_d, send_i, recv_i = sems[4 * g:4 * g + 4]
            for a in range(starts[g], starts[g + 1]):
                for k in _CHIPS:
                    _rcopy(ins[a], lnd[a].at[me], send_i, recv_i, k).start()
                _rcopy(ins[a], lnd[a].at[me], send_d, recv_d, _SIBLING).start()
        refs[-1][...] = jnp.zeros_like(refs[-1])

    hbm = lambda a: pltpu.HBM(a.shape, a.dtype)
    out = pl.pallas_call(
        body, name=name,
        out_shape=(*[pltpu.SemaphoreType.DMA(())] * (4 * ng), *[hbm(a) for a in arrays], *[hbm(a) for a in lands],
                   jax.ShapeDtypeStruct((8, LANES), F32)),
        in_specs=[_HBM] * (2 * n) + [pl.BlockSpec(memory_space=pl.ANY)],
        out_specs=(*[_SEM] * (4 * ng), *([_HBM] * (2 * n)), pl.BlockSpec(memory_space=pltpu.VMEM)),
        input_output_aliases={i: 4 * ng + i for i in range(2 * n)},
        compiler_params=pltpu.CompilerParams(has_side_effects=_EFFECT),
    )(*[pltpu.with_memory_space_constraint(a, pltpu.HBM) for a in arrays + lands], after)
    ins_out, lands_out = out[4 * ng:4 * ng + n], out[4 * ng + n:4 * ng + 2 * n]
    handles = [dict(sems=out[4 * g:4 * g + 4], ins=list(ins_out[starts[g]:starts[g + 1]]),
                    lands=list(lands_out[starts[g]:starts[g + 1]])) for g in range(ng)]
    return handles, out[-1]


def _gather2_pass_on(handle, after, name):
    lands, recv_i = handle["lands"], handle["sems"][3]
    n = len(lands)
    after = after if isinstance(after, tuple) else (after,)

    def body(*refs):
        lnd, r_i = refs[:n], refs[n]
        send_f, recv_f = refs[n + 1 + len(after):n + 3 + len(after)]
        for a in range(n):
            for k in _CHIPS:
                row = _peer_of(k)[1]
                _rcopy(lnd[a].at[row], lnd[a].at[row], send_f, r_i, k).wait_recv()
        for a in range(n):
            for k in _CHIPS:
                row = _peer_of(k)[1]
                _rcopy(lnd[a].at[row], lnd[a].at[row], send_f, recv_f, _SIBLING).start()
        refs[-1][...] = jnp.zeros_like(refs[-1])

    hbm = lambda a: pltpu.HBM(a.shape, a.dtype)
    out = pl.pallas_call(
        body, name=name,
        out_shape=(pltpu.SemaphoreType.DMA(()), pltpu.SemaphoreType.DMA(()), *[hbm(a) for a in lands],
                   jax.ShapeDtypeStruct((8, LANES), F32)),
        in_specs=[_HBM] * n + [_SEM] + [pl.BlockSpec(memory_space=pl.ANY)] * len(after),
        out_specs=(_SEM, _SEM, *([_HBM] * n), pl.BlockSpec(memory_space=pltpu.VMEM)),
        input_output_aliases={i: 2 + i for i in range(n)},
        compiler_params=pltpu.CompilerParams(has_side_effects=_EFFECT),
    )(*lands, recv_i, *after)
    return dict(handle, lands=list(out[2:2 + n]), passed=(out[0], out[1])), out[-1]


def _gather2_wait(handle, after, name):
    ins, lands = handle["ins"], handle["lands"]
    send_d, recv_d, send_i, _ = handle["sems"]
    send_f, recv_f = handle["passed"]
    n = len(ins)
    after = after if isinstance(after, tuple) else (after,)

    def body(*refs):
        i_ref, lnd = refs[:n], refs[n:2 * n]
        s_d, r_d, s_i, s_f, r_f = refs[2 * n:2 * n + 5]
        me, _ = _peers()
        sib = _peer_of(_SIBLING)[1]
        for a in range(n):
            _rcopy(i_ref[a], lnd[a].at[sib], s_d, r_d, _SIBLING).wait_send()
            _rcopy(i_ref[a], lnd[a].at[sib], s_d, r_d, _SIBLING).wait_recv()
            for k in _CHIPS:
                row = _peer_of(k)[1]
                _rcopy(i_ref[a], lnd[a].at[me], s_i, r_d, k).wait_send()
                _rcopy(lnd[a].at[row], lnd[a].at[row], s_f, r_f, _SIBLING).wait_send()
                _rcopy(lnd[a].at[row], lnd[a].at[_peer_of(k ^ _SIBLING)[1]], s_f, r_f, _SIBLING).wait_recv()

    hbm = lambda a: pltpu.HBM(a.shape, a.dtype)
    out = pl.pallas_call(
        body, name=name,
        out_shape=tuple(hbm(a) for a in ins + lands),
        in_specs=[_HBM] * (2 * n) + [_SEM] * 5 + [pl.BlockSpec(memory_space=pl.ANY)] * len(after),
        out_specs=tuple([_HBM] * (2 * n)),
        input_output_aliases={i: i for i in range(2 * n)},
        compiler_params=pltpu.CompilerParams(has_side_effects=_EFFECT),
    )(*ins, *lands, send_d, recv_d, send_i, send_f, recv_f, *after)
    return list(out[n:])


def _adamw(w, parts, m, v, name="adamw"):
    R, C = w.shape
    L = len(parts)
    rl = R // L
    tr = max([t for t in range(16, 257, 16) if rl % t == 0], default=rl)
    nb = rl // tr
    c1 = 1.0 - ADAM_B1 ** ADAM_STEP
    c2 = 1.0 - ADAM_B2 ** ADAM_STEP

    def body(*refs):
        w_ref, p_refs, (m_ref, v_ref, g_ref, d_ref, nm_ref, nv_ref) = refs[0], refs[1:1 + L], refs[1 + L:]
        g = None
        for j, p_ref in enumerate(p_refs):
            gj = p_ref[0].astype(F32)
            for i in range(1, N_DEV):
                gj = gj + p_ref[i].astype(F32)
            g = gj if g is None else jnp.where(pl.program_id(0) == j, gj, g)
        nm = ADAM_B1 * m_ref[...] + (1.0 - ADAM_B1) * g
        nv = ADAM_B2 * v_ref[...] + (1.0 - ADAM_B2) * (g * g)
        g_ref[...] = g
        nm_ref[...] = nm
        nv_ref[...] = nv
        d_ref[...] = -ADAM_LR * ((nm / c1) / (jnp.sqrt(nv / c2) + ADAM_EPS) + ADAM_WD * w_ref[...])

    blk = pl.BlockSpec((tr, C), lambda l, i: (l * nb + i, 0))
    part = lambda j: pl.BlockSpec((N_DEV, tr, C), lambda l, i: (0, jnp.where(l == j, i, 0), 0))
    return pl.pallas_call(
        body, name=name, grid=(L, nb),
        in_specs=[blk] + [part(j) for j in range(L)] + [blk, blk],
        out_specs=[blk] * 4,
        out_shape=[jax.ShapeDtypeStruct((R, C), F32)] * 4,
        compiler_params=_params(("arbitrary", "arbitrary")),
    )(w, *parts, m, v)


_O1 = Q_LORA
_O2 = _O1 + KV_LORA
_O3 = _O2 + MLA_ROPE
_NB = SB_HEADS * SB_DIM
IN_W = _O2 + LANES + 3 * _NB
COL_KR = _O2 // LANES
COL_SB = COL_KR + 1


def _w_in_local(w):
    kr = w[_O2:_O3]
    pad = jnp.zeros((LANES - 2 * MLA_ROPE, w.shape[1]), w.dtype)
    return jnp.concatenate([w[:_O2], kr, kr, pad, w[_O3:]], axis=0)


def _w_in_grad(g):
    kr = (g[_O2:_O2 + MLA_ROPE].astype(F32) + g[_O2 + MLA_ROPE:_O2 + 2 * MLA_ROPE].astype(F32)).astype(g.dtype)
    return jnp.concatenate([g[:_O2], kr, g[_O2 + LANES:]], axis=0)


def _w_uq_local(w):
    w3 = w.reshape(MLA_HEADS // 2, 2, MLA_NOPE + MLA_ROPE, w.shape[1])
    nope = w3[:, :, :MLA_NOPE].reshape(MLA_HEADS // 2, 2 * MLA_NOPE, w.shape[1])
    rope = w3[:, :, MLA_NOPE:].reshape(MLA_HEADS // 2, 2 * MLA_ROPE, w.shape[1])
    pad = jnp.zeros((MLA_HEADS // 2, LANES - 2 * MLA_ROPE, w.shape[1]), w.dtype)
    return jnp.concatenate([nope, rope, pad], axis=1).reshape(-1, w.shape[1])


def _w_uq_grad(g):
    g3 = g.reshape(MLA_HEADS // 2, 2 * LANES, g.shape[1])
    nope = g3[:, :2 * MLA_NOPE].reshape(MLA_HEADS // 2, 2, MLA_NOPE, g.shape[1])
    rope = g3[:, LANES:LANES + 2 * MLA_ROPE].reshape(MLA_HEADS // 2, 2, MLA_ROPE, g.shape[1])
    return jnp.concatenate([nope, rope], axis=2).reshape(-1, g.shape[1])


def _w_ukv_local(w):
    w3 = w.reshape(MLA_HEADS, MLA_NOPE + MLA_V, w.shape[1])
    return jnp.concatenate([w3[:, :MLA_NOPE].reshape(-1, w.shape[1]),
                            w3[:, MLA_NOPE:].reshape(-1, w.shape[1])], axis=0)


def _w_ukv_grad(g):
    half = MLA_HEADS * MLA_NOPE
    kn = g[:half].reshape(MLA_HEADS, MLA_NOPE, g.shape[1])
    vv = g[half:].reshape(MLA_HEADS, MLA_V, g.shape[1])
    return jnp.concatenate([kn, vv], axis=1).reshape(-1, g.shape[1])


def _rope_tables(T):
    pos = jnp.arange(T, dtype=F32)
    inv_freq = ROPE_THETA ** (-jnp.arange(0, MLA_ROPE, 2, dtype=F32) / MLA_ROPE)
    ang = pos[:, None] * inv_freq[None, :]
    cos, sin = jnp.cos(ang), jnp.sin(ang)
    ones = jnp.ones((T, LANES - 2 * MLA_ROPE), F32)
    cos_k = jnp.concatenate([cos, cos, cos, cos, ones], axis=1)
    sin_k = jnp.concatenate([-sin, sin, -sin, sin, 0.0 * ones], axis=1)
    cos_q = jnp.concatenate([jnp.ones((T, LANES), F32), cos_k], axis=1)
    sin_q = jnp.concatenate([jnp.zeros((T, LANES), F32), sin_k], axis=1)
    return cos_q, sin_q, cos_k, sin_k


def _bias_diag_index():
    ell = np.arange(TOEP_W)
    return np.clip(BAND_W - ell, -REL_CLIP, REL_CLIP) + REL_CLIP


def _local_step(x, target, small, get_weights, put_grads, prefetch):
    T = x.shape[0]
    cos_q, sin_q, cos_k, sin_k = _rope_tables(T)
    G = {}
    W = dict(small)

    u0 = _rms_fwd(x, W["g_mix"][0:1], name="rms_mix0")
    bias_w = _toeplitz(W["od_rel_bias"][:, _bias_diag_index()])
    W.update(get_weights("in0", (u0, bias_w)))
    proj = _mm(u0, W["w_in_t"], dims="nt", name="proj_in")
    W.update(get_weights("mix0", proj))
    c_q, c_kv = proj[:, :_O1], proj[:, _O1:_O2]
    nq = _rms_fwd(c_q, W["g_cq"], name="rms_cq")
    nkv = _rms_fwd(c_kv, W["g_ckv"], name="rms_ckv")
    qa_raw = _mm(nq, W["w_uq_t"], dims="nt", name="proj_uq")
    kv = _mm(nkv, W["w_ukv_t"], dims="nt", out_dtype=BF16, name="proj_ukv")
    kr = _rope(proj, cos_k, sin_k, COL_KR, 1, BF16, name="rope_k")
    o_a, lse = _mla_fwd(qa_raw, cos_q, sin_q, kv, kr)
    o_b, o_b32, w_b, sp_b = _sb_fwd(proj, COL_SB, prefetch("ffn0", o_a))
    o_ab = jnp.concatenate([o_a, o_b], axis=1)
    h1 = _mm(o_ab, W["ev_w_out"], res=x, name="out_ev")

    def ffn_fwd(h, layer):
        W.update(get_weights(f"ffn{layer}", h))
        return _ffn_fwd(h, W["g_ffn"][layer:layer + 1], W[f"w_gate_t{layer}"], W[f"w_up_t{layer}"],
                        W[f"w_down{layer}"], name=f"ffn_fwd{layer}")

    h2, u1, a0, b0 = ffn_fwd(h1, 0)

    W.update(get_weights("mix1", h2))
    u2 = _rms_fwd(h2, W["g_mix"][1:2], name="rms_mix1")
    qkv = _mm(u2, W["od_w_qkv_t"], dims="nt", out_dtype=BF16, name="proj_qkv")
    nc = C_HEADS * C_DIM
    pad = ((PAD_KEYS, 0), (0, 0))
    k_pad, v_pad = jnp.pad(qkv[:, nc:2 * nc], pad), jnp.pad(qkv[:, 2 * nc:], pad)
    o_c, p_c = _band_fwd(qkv, k_pad, v_pad, bias_w)
    h3 = _mm(o_c, W["od_w_out"], res=h2, name="out_od")
    h4, u3, a1, b1 = ffn_fwd(h3, 1)

    loss, dh, dhb, G["g_final"] = _loss_head(h4, W["g_final"], target)

    def ffn_bwd(dh, dhb, h, u, a, b, layer):
        du, g_gate, g_up, g_down = _ffn_bwd(dhb, u, a, b, W[f"w_gate_t{layer}"], W[f"w_up_t{layer}"],
                                            W[f"w_down{layer}"], name=f"ffn_bwd{layer}")
        tok = put_grads(f"ffn{layer}", {"w_gate_t": g_gate, "w_up_t": g_up, "w_down": g_down})
        return _rms_bwd(h, W["g_ffn"][layer:layer + 1] + tok[:1, :1], du, dres=dh, name=f"rms_ffn_bwd{layer}")

    dh3, dh3b, g_gffn1 = ffn_bwd(dh, dhb, h3, u3, a1, b1, 1)

    do_c = _mm(dh3b, W["od_w_out"], dims="nt", name="out_od_dx")
    g_od_out = _mm(o_c, dh3b, dims="tn", out_dtype=BF16, name="out_od_dw")
    dq_c, dk_p, dv_p, dbias_w = _band_bwd(qkv, k_pad, v_pad, p_c, do_c)
    dqkv = jnp.concatenate([dq_c, dk_p[PAD_KEYS:], dv_p[PAD_KEYS:]], axis=1)
    tok = put_grads("mix1", {"od_w_qkv_t": _mm(dqkv, u2, dims="tn", out_dtype=BF16, name="proj_qkv_dw"),
                             "od_w_out": g_od_out})
    ddiag = _toeplitz_bwd(dbias_w)
    n_far = BAND_W - REL_CLIP + 1
    G["od_rel_bias"] = jnp.concatenate(
        [jnp.zeros((C_HEADS, REL_CLIP - BAND_TQ + 1), F32), ddiag[:, n_far:][:, ::-1],
         jnp.sum(ddiag[:, :n_far], axis=1, keepdims=True)], axis=1)
    dh2, dh2b, g_gmix1 = _mm_rms_bwd(dqkv, W["od_w_qkv_t"], h2, W["g_mix"][1:2] + tok[:1, :1], dh3,
                                     name="proj_qkv_dx")

    dh1, dh1b, g_gffn0 = ffn_bwd(dh2, dh2b, h1, u1, a0, b0, 0)
    G["g_ffn"] = jnp.concatenate([g_gffn0, g_gffn1], axis=0)

    do_ab = _mm(dh1b, W["ev_w_out"], dims="nt", name="out_ev_dx")
    g0 = {"ev_w_out": _mm(o_ab, dh1b, dims="tn", out_dtype=BF16, name="out_ev_dw")}
    dqa_raw, dkn, dva, dkr = _mla_bwd(qa_raw, cos_q, sin_q, kv, kr, o_a, lse, do_ab, 0)
    dlat, g0["w_uq_t"], g0["w_ukv_t"], G["g_cq"], G["g_ckv"] = _latent_bwd(
        proj, nq, nkv, dqa_raw, dkn, dva, dkr, cos_k, sin_k, W["g_cq"], W["g_ckv"], W["w_uq_t"], W["w_ukv_t"])
    tok = put_grads("mix0", g0)
    dqb, dkb, dvb = _sb_bwd(proj, COL_SB, o_b32, w_b, sp_b, do_ab, MLA_HEADS // 2, tok)
    dproj = jnp.concatenate([dlat, dqb, dkb, dvb], axis=1)
    tok = put_grads("in0", {"w_in_t": _mm(dproj, u0, dims="tn", name="proj_in_dw")})
    dx, _, g_gmix0 = _mm_rms_bwd(dproj, W["w_in_t"], x, W["g_mix"][0:1] + tok[:1, :1], dh1, name="proj_in_dx")
    G["g_mix"] = jnp.concatenate([g_gmix0, g_gmix1], axis=0)
    return loss[0, 0], dx, G


_BIG = ["ev_w_in", "ev_w_uq", "ev_w_ukv", "ev_w_out", "od_w_qkv", "od_w_out", "w_gate", "w_up", "w_down"]
_COL_SHARDED = {"ev_w_in", "ev_w_uq", "ev_w_ukv", "od_w_qkv", "w_gate", "w_up"}
_SMALL = ["ev_g_cq", "ev_g_ckv", "od_rel_bias", "g_mix", "g_ffn", "g_final"]
_GROUPS = {
    "in0": ["ev_w_in"],
    "mix0": ["ev_w_uq", "ev_w_ukv", "ev_w_out"],
    "ffn0": ["w_gate0", "w_up0", "w_down0"],
    "mix1": ["od_w_qkv", "od_w_out"],
    "ffn1": ["w_gate1", "w_up1", "w_down1"],
}
_GROUP_SRC = {n + str(l): (n, l) for n in ("w_gate", "w_up", "w_down") for l in (0, 1)}
_BATCHES = {"in0": ["in0"], "layer0": ["mix0", "ffn0"], "layer1": ["mix1", "ffn1"]}
_BATCH_OF = {grp: batch for batch, grps in _BATCHES.items() for grp in grps}
_SMALL_ROWS = 8
_SMALL_COLS = 1792


def _pack_small(vals):
    flat = jnp.concatenate([v.reshape(-1).astype(F32) for v in vals])
    flat = jnp.pad(flat, (0, _SMALL_ROWS * _SMALL_COLS - flat.shape[0]))
    return flat.reshape(_SMALL_ROWS, _SMALL_COLS)


def _unpack_small(packed, like):
    flat = packed.reshape(-1)
    out, off = [], 0
    for v in like:
        out.append(flat[off:off + v.size].reshape(v.shape))
        off += v.size
    return out


def kernel(x, ev_w_in, ev_g_cq, ev_w_uq, ev_g_ckv, ev_w_ukv, ev_w_out, od_w_qkv, od_rel_bias, od_w_out, g_mix, g_ffn, w_gate, w_up, w_down, g_final, loss_target, m_ev_w_in, m_ev_g_cq, m_ev_w_uq, m_ev_g_ckv, m_ev_w_ukv, m_ev_w_out, m_od_w_qkv, m_od_rel_bias, m_od_w_out, m_g_mix, m_g_ffn, m_w_gate, m_w_up, m_w_down, m_g_final, v_ev_w_in, v_ev_g_cq, v_ev_w_uq, v_ev_g_ckv, v_ev_w_ukv, v_ev_w_out, v_od_w_qkv, v_od_rel_bias, v_od_w_out, v_g_mix, v_g_ffn, v_w_gate, v_w_up, v_w_down, v_g_final):
    args = dict(locals())
    w = {n: args[n] for n in _BIG + _SMALL}
    mom = {n: args["m_" + n] for n in _BIG + _SMALL}
    var = {n: args["v_" + n] for n in _BIG + _SMALL}

    own = {}
    for grp, names in _GROUPS.items():
        for n in names:
            base, layer = _GROUP_SRC.get(n, (n, 0))
            shard = w[base][layer:layer + 1]
            own[n] = (jnp.swapaxes(shard, 1, 2) if base in _COL_SHARDED else shard).astype(BF16)
    placed = dict(zip(own, _place_own(list(own.values()), [False] * len(own), name="gather_own")))
    handles, token = _gather2_start(
        [[own[n] for n in names] for names in _GROUPS.values()],
        [[placed[n] for n in names] for names in _GROUPS.values()], x[0, :8, :LANES], name="gather_start")
    gather = dict(zip(_GROUPS, handles))
    pass_before = {"in0": ["in0"], "mix0": ["mix0"]}
    pass_after = {"ffn0": ("mix1", "g_ffn"), "mix1": ("ffn1", "g_mix")}

    def prefetch(grp, after):
        gather[grp], tok = _gather2_pass_on(gather[grp], after, name="gather_pass_" + grp)
        return tok

    def get_weights(grp, after):
        names = _GROUPS[grp]
        after = token if after is None else after
        for g in pass_before.get(grp, []):
            gather[g], _ = _gather2_pass_on(gather[g], after, name="gather_pass_" + g)
        lands = _gather2_wait(gather[grp], after, name="gather_wait_" + grp)
        full = {n: l.reshape(-1, l.shape[-1]) for n, l in zip(names, lands)}
        out = {}
        if grp in pass_after:
            g, gain = pass_after[grp]
            gather[g], tok = _gather2_pass_on(gather[g], lands[0], name="gather_pass_" + g)
            out[gain] = small[gain] + tok[:1, :1]
        if grp == "in0":
            out.update({"w_in_t": _w_in_local(full["ev_w_in"])})
        elif grp == "mix0":
            out.update({"w_uq_t": _w_uq_local(full["ev_w_uq"]), "w_ukv_t": _w_ukv_local(full["ev_w_ukv"]),
                        "ev_w_out": full["ev_w_out"]})
        elif grp == "mix1":
            out.update({"od_w_qkv_t": full["od_w_qkv"], "od_w_out": full["od_w_out"]})
        else:
            layer = grp[-1]
            out.update({"w_gate_t" + layer: full["w_gate" + layer], "w_up_t" + layer: full["w_up" + layer],
                        "w_down" + layer: full["w_down" + layer]})
        return out

    scatter, pending = {}, {}

    def put_grads(grp, g):
        if grp == "in0":
            g = {"ev_w_in": _w_in_grad(g["w_in_t"])}
        elif grp == "mix0":
            g = {"ev_w_uq": _w_uq_grad(g["w_uq_t"]), "ev_w_ukv": _w_ukv_grad(g["w_ukv_t"]),
                 "ev_w_out": g["ev_w_out"]}
        elif grp == "mix1":
            g = {"od_w_qkv": g["od_w_qkv_t"], "od_w_out": g["od_w_out"]}
        else:
            layer = grp[-1]
            g = {"w_gate" + layer: g["w_gate_t"], "w_up" + layer: g["w_up_t"], "w_down" + layer: g["w_down"]}
        pending.update({n: v.reshape(N_DEV, 1, v.shape[0] // N_DEV, v.shape[1]).astype(BF16) for n, v in g.items()})
        batch = _BATCH_OF[grp]
        names = [n for gr in _BATCHES[batch] for n in _GROUPS[gr]]
        if batch == "in0" or not all(n in pending for n in names):
            return jnp.zeros((8, LANES), F32)
        send = [pending[n] for n in names]
        scatter[batch], tok = _exchange_start(send, [True] * len(names), send[0], name="scatter_start_" + batch)
        return tok

    small = {"g_cq": ev_g_cq, "g_ckv": ev_g_ckv, "od_rel_bias": od_rel_bias[0],
             "g_mix": g_mix + token[0, 0], "g_ffn": g_ffn, "g_final": g_final.reshape(1, -1)}
    loss_part, dx, G = _local_step(x[0], loss_target[0], small, get_weights, put_grads, prefetch)
    g_small = _pack_small([G["g_cq"], G["g_ckv"], G["od_rel_bias"], G["g_mix"], G["g_ffn"], G["g_final"],
                           loss_part.reshape(1)])
    scatter["in0"], _ = _exchange_start([pending["ev_w_in"], g_small], [True, False], dx, name="scatter_start_in0")

    grads, deltas, new_m, new_v = {}, {}, {}, {}
    parts, after = {}, dx

    def wait_parts(batch, after):
        lands = _exchange_wait(scatter[batch], after, name="scatter_wait_" + batch)
        parts.update(zip([n for grp in _BATCHES[batch] for n in _GROUPS[grp]], lands))
        return lands[0]

    def adamw(n):
        col = n in _COL_SHARDED
        rows = lambda a: (jnp.swapaxes(a, 1, 2) if col else a).reshape(-1, a.shape[1 if col else 2])
        layers = [parts[n]] if n in parts else [parts[n + "0"], parts[n + "1"]]
        res = _adamw(rows(w[n]), [p.reshape(N_DEV, -1, p.shape[-1]) for p in layers], rows(mom[n]), rows(var[n]),
                     name="adamw_" + n)
        L, a1, a2 = w[n].shape
        back = lambda r: jnp.swapaxes(r.reshape(L, a2, a1), 1, 2) if col else r.reshape(L, a1, a2)
        grads[n], deltas[n], new_m[n], new_v[n] = [back(r) for r in res]
        return res[0]

    for batch in ("layer1", "layer0"):
        after = wait_parts(batch, after)
    parts["ev_w_in"], small_parts = _exchange_wait(scatter["in0"], tuple(adamw(n) for n in _BIG[1:]),
                                                   name="scatter_wait_in0")
    adamw("ev_w_in")
    small_w = [w[n] for n in _SMALL]
    loss = jnp.sum(small_parts.reshape(N_DEV, -1)[:, sum(v.size for v in small_w)])
    res = _adamw(_pack_small(small_w), [small_parts], _pack_small([mom[n] for n in _SMALL]),
                 _pack_small([var[n] for n in _SMALL]), name="adamw_small")
    for d, packed in zip((grads, deltas, new_m, new_v), res):
        for n, val in zip(_SMALL, _unpack_small(packed, small_w)):
            d[n] = val

    order = ["ev_w_in", "ev_g_cq", "ev_w_uq", "ev_g_ckv", "ev_w_ukv", "ev_w_out", "od_w_qkv", "od_rel_bias",
             "od_w_out", "g_mix", "g_ffn", "w_gate", "w_up", "w_down", "g_final"]
    out = [loss, dx[None]]
    for d in (grads, deltas, new_m, new_v):
        out += [d[n] for n in order]
    return tuple(out)
```

```python
import functools

import numpy as np
import jax
import jax.numpy as jnp
from jax import lax
from jax.experimental import pallas as pl
from jax.experimental.pallas import tpu as pltpu

F32 = jnp.float32
BF16 = jnp.bfloat16

D_MODEL = 1024
CHUNK = 64
MLA_HEADS = 8
MLA_NOPE = 64
MLA_ROPE = 32
MLA_V = 64
Q_LORA = 384
KV_LORA = 256
ROPE_THETA = 10000.0
SB_HEADS = 8
SB_DIM = 64
C_HEADS = 16
C_DIM = 64
LEFT_CHUNKS = 8
REL_CLIP = 256
D_FF = 2816
RMS_EPS = 1e-6
ADAM_LR = 0.001
ADAM_B1 = 0.9
ADAM_B2 = 0.999
ADAM_EPS = 1e-08
ADAM_WD = 0.01
ADAM_STEP = 10

N_DEV = 8
LANES = 128
HEAD = 64
assert HEAD == MLA_NOPE == MLA_V == SB_DIM == C_DIM and 2 * HEAD == LANES
CHUNK_BITS = CHUNK.bit_length() - 1
assert 1 << CHUNK_BITS == CHUNK
VMEM_LIMIT = 56 * 1024 * 1024
NEG = -1e30
PAD_KEYS = LEFT_CHUNKS * CHUNK
BAND_TQ = 128
BAND_W = BAND_TQ + PAD_KEYS
TOEP_W = BAND_W + BAND_TQ

NN = (((1,), (0,)), ((), ()))
NT = (((1,), (1,)), ((), ()))
TN = (((0,), (0,)), ((), ()))


def _dot(a, b, dn):
    return lax.dot_general(a, b, dn, preferred_element_type=F32)


def _pick(dim, pref):
    if dim <= pref:
        return dim
    best = None
    for t in range(LANES, pref + 1, LANES):
        if dim % t == 0:
            best = t
    assert best is not None, (dim, pref)
    return best


def _params(sem):
    return pltpu.CompilerParams(dimension_semantics=sem, vmem_limit_bytes=VMEM_LIMIT)


def _mm(a, b, dims="nn", res=None, out_dtype=F32, name="mm"):
    if dims == "nn":
        (M, K), (K2, N) = a.shape, b.shape
    elif dims == "nt":
        (M, K), (N, K2) = a.shape, b.shape
    else:
        (K, M), (K2, N) = a.shape, b.shape
    assert K == K2, (a.shape, b.shape, dims)
    tm, tn, tk = _pick(M, 1024), _pick(N, 1152), _pick(K, 1024)
    nk = K // tk
    dn = {"nn": NN, "nt": NT, "tn": TN}[dims]
    has_res = res is not None

    def body(*refs):
        if has_res:
            a_ref, b_ref, r_ref, o_ref, acc = refs
        else:
            a_ref, b_ref, o_ref, acc = refs
        k = pl.program_id(2)

        @pl.when(k == 0)
        def _():
            acc[...] = jnp.zeros_like(acc)

        acc[...] += _dot(a_ref[...].astype(BF16), b_ref[...].astype(BF16), dn)

        @pl.when(k == nk - 1)
        def _():
            r = acc[...]
            if has_res:
                r = r + r_ref[...]
            o_ref[...] = r.astype(out_dtype)

    a_spec = (pl.BlockSpec((tk, tm), lambda i, j, k: (k, i)) if dims == "tn"
              else pl.BlockSpec((tm, tk), lambda i, j, k: (i, k)))
    b_spec = (pl.BlockSpec((tn, tk), lambda i, j, k: (j, k)) if dims == "nt"
              else pl.BlockSpec((tk, tn), lambda i, j, k: (k, j)))
    o_spec = pl.BlockSpec((tm, tn), lambda i, j, k: (i, j))
    in_specs = [a_spec, b_spec] + ([o_spec] if has_res else [])
    args = (a, b) + ((res,) if has_res else ())
    return pl.pallas_call(
        body, name=name, grid=(M // tm, N // tn, nk),
        in_specs=in_specs, out_specs=o_spec,
        out_shape=jax.ShapeDtypeStruct((M, N), out_dtype),
        scratch_shapes=[pltpu.VMEM((tm, tn), F32)],
        compiler_params=_params(("parallel", "parallel", "arbitrary")),
    )(*args)


def _rms_fwd(x, g, out_dtype=BF16, name="rms_fwd"):
    T, Fd = x.shape
    tm = _pick(T, 256)

    def body(x_ref, g_ref, o_ref):
        xv = x_ref[...]
        r = lax.rsqrt(jnp.mean(xv * xv, axis=-1, keepdims=True) + RMS_EPS)
        o_ref[...] = (xv * r * g_ref[...]).astype(out_dtype)

    return pl.pallas_call(
        body, name=name, grid=(T // tm,),
        in_specs=[pl.BlockSpec((tm, Fd), lambda i: (i, 0)), pl.BlockSpec((1, Fd), lambda i: (0, 0))],
        out_specs=pl.BlockSpec((tm, Fd), lambda i: (i, 0)),
        out_shape=jax.ShapeDtypeStruct((T, Fd), out_dtype),
        compiler_params=_params(("parallel",)),
    )(x, g)


def _rms_bwd(x, g, dy, dres=None, name="rms_bwd"):
    T, Fd = x.shape
    tm = _pick(T, 256)
    has_res = dres is not None

    def body(*refs):
        if has_res:
            x_ref, g_ref, dy_ref, r_ref, dx_ref, dxb_ref, dg_ref = refs
        else:
            x_ref, g_ref, dy_ref, dx_ref, dxb_ref, dg_ref = refs
        xv, dyv = x_ref[...], dy_ref[...]
        r = lax.rsqrt(jnp.mean(xv * xv, axis=-1, keepdims=True) + RMS_EPS)
        gdy = dyv * g_ref[...]
        dot = jnp.mean(xv * gdy, axis=-1, keepdims=True)
        dx = r * gdy - xv * (r * r * r * dot)
        if has_res:
            dx = dx + r_ref[...]
        dx_ref[...] = dx
        dxb_ref[...] = dx.astype(BF16)

        @pl.when(pl.program_id(0) == 0)
        def _():
            dg_ref[...] = jnp.zeros_like(dg_ref)

        dg_ref[...] += jnp.sum(dyv * xv * r, axis=0, keepdims=True)

    row = pl.BlockSpec((tm, Fd), lambda i: (i, 0))
    vec = pl.BlockSpec((1, Fd), lambda i: (0, 0))
    in_specs = [row, vec, row] + ([row] if has_res else [])
    args = (x, g, dy) + ((dres,) if has_res else ())
    return pl.pallas_call(
        body, name=name, grid=(T // tm,),
        in_specs=in_specs, out_specs=[row, row, vec],
        out_shape=[jax.ShapeDtypeStruct((T, Fd), F32), jax.ShapeDtypeStruct((T, Fd), BF16),
                   jax.ShapeDtypeStruct((1, Fd), F32)],
        compiler_params=_params(("arbitrary",)),
    )(*args)


def _mm_rms_bwd(a, b, x, g, dres, name="mm_rms_bwd"):
    T, K = a.shape
    Fd = b.shape[1]
    tm, tk = _pick(T, 1024), _pick(K, 1024)
    nk = K // tk

    def body(a_ref, b_ref, x_ref, g_ref, r_ref, dx_ref, dxb_ref, dg_ref, acc):
        i, k = pl.program_id(0), pl.program_id(1)

        @pl.when(k == 0)
        def _():
            acc[...] = jnp.zeros_like(acc)

        @pl.when((k == 0) & (i == 0))
        def _():
            dg_ref[...] = jnp.zeros_like(dg_ref)

        acc[...] += _dot(a_ref[...].astype(BF16), b_ref[...].astype(BF16), NN)

        @pl.when(k == nk - 1)
        def _():
            xv, dyv = x_ref[...], acc[...]
            r = lax.rsqrt(jnp.mean(xv * xv, axis=-1, keepdims=True) + RMS_EPS)
            gdy = dyv * g_ref[...]
            dot = jnp.mean(xv * gdy, axis=-1, keepdims=True)
            dx = r * gdy - xv * (r * r * r * dot) + r_ref[...]
            dx_ref[...] = dx
            dxb_ref[...] = dx.astype(BF16)
            dg_ref[...] += jnp.sum(dyv * xv * r, axis=0, keepdims=True)

    row = pl.BlockSpec((tm, Fd), lambda i, k: (i, 0))
    vec = pl.BlockSpec((1, Fd), lambda i, k: (0, 0))
    return pl.pallas_call(
        body, name=name, grid=(T // tm, nk),
        in_specs=[pl.BlockSpec((tm, tk), lambda i, k: (i, k)), pl.BlockSpec((tk, Fd), lambda i, k: (k, 0)),
                  row, vec, row],
        out_specs=[row, row, vec],
        out_shape=[jax.ShapeDtypeStruct((T, Fd), F32), jax.ShapeDtypeStruct((T, Fd), BF16),
                   jax.ShapeDtypeStruct((1, Fd), F32)],
        scratch_shapes=[pltpu.VMEM((tm, Fd), F32)],
        compiler_params=_params(("arbitrary", "arbitrary")),
    )(a, b, x, g, dres)


def _latent_bwd(proj, nq, nkv, dqa, dkn, dva, dkr, cos_k, sin_k, g_cq, g_ckv, w_uq_t, w_ukv_t, name="latent_bwd"):
    T = proj.shape[0]
    tm = _pick(T, 512)
    wl = _O2

    def rms_bwd(xv, gv, dyv):
        r = lax.rsqrt(jnp.mean(xv * xv, axis=-1, keepdims=True) + RMS_EPS)
        gdy = dyv * gv
        dot = jnp.mean(xv * gdy, axis=-1, keepdims=True)
        return r * gdy - xv * (r * r * r * dot), jnp.sum(dyv * xv * r, axis=0, keepdims=True)

    def body(p_ref, nq_ref, nkv_ref, dqa_ref, dkn_ref, dva_ref, dkr_ref, c_ref, s_ref, gq_ref, gkv_ref, wq_ref, wkv_ref,
             dlat_ref, dwq_ref, dwkv_ref, dgq_ref, dgkv_ref):
        @pl.when(pl.program_id(0) == 0)
        def _():
            for ref in (dwq_ref, dwkv_ref, dgq_ref, dgkv_ref):
                ref[...] = jnp.zeros_like(ref)

        dqv = dqa_ref[...]
        dkv = jnp.concatenate([dkn_ref[...], dva_ref[...]], axis=1)
        pv = p_ref[...]
        dc_q, dgq = rms_bwd(pv[:, :_O1], gq_ref[...], _dot(dqv, wq_ref[...], NN))
        dc_kv, dgkv = rms_bwd(pv[:, _O1:], gkv_ref[...], _dot(dkv, wkv_ref[...], NN))
        dkr_raw = _rotate(dkr_ref[...], c_ref[...], -s_ref[...])
        dlat_ref[...] = jnp.concatenate([dc_q, dc_kv, dkr_raw], axis=1).astype(BF16)
        dwq_ref[...] += _dot(dqv, nq_ref[...], TN)
        dwkv_ref[...] += _dot(dkv, nkv_ref[...], TN)
        dgq_ref[...] += dgq
        dgkv_ref[...] += dgkv

    row = lambda w: pl.BlockSpec((tm, w), lambda i: (i, 0))
    const = lambda a: pl.BlockSpec(a.shape, lambda i: (0, 0))
    outs = [jax.ShapeDtypeStruct((T, wl + LANES), BF16), jax.ShapeDtypeStruct(w_uq_t.shape, F32),
            jax.ShapeDtypeStruct(w_ukv_t.shape, F32), jax.ShapeDtypeStruct(g_cq.shape, F32),
            jax.ShapeDtypeStruct(g_ckv.shape, F32)]
    return pl.pallas_call(
        body, name=name, grid=(T // tm,),
        in_specs=[row(wl), row(_O1), row(_O2 - _O1), row(dqa.shape[1]), row(dkn.shape[1]), row(dva.shape[1]),
                  row(LANES), row(LANES), row(LANES), const(g_cq), const(g_ckv), const(w_uq_t), const(w_ukv_t)],
        out_specs=[row(wl + LANES)] + [const(o) for o in outs[1:]],
        out_shape=outs,
        compiler_params=_params(("arbitrary",)),
    )(proj, nq, nkv, dqa, dkn, dva, dkr, cos_k, sin_k, g_cq, g_ckv, w_uq_t, w_ukv_t)


def _loss_head(h, g, target, name="loss_head"):
    T, Fd = h.shape
    tm = _pick(T, 256)

    def body(h_ref, g_ref, t_ref, loss_ref, dh_ref, dhb_ref, dg_ref):
        xv = h_ref[...]
        r = lax.rsqrt(jnp.mean(xv * xv, axis=-1, keepdims=True) + RMS_EPS)
        diff = xv * r * g_ref[...] - t_ref[...]
        part = 0.5 * jnp.sum(jnp.mean(diff * diff, axis=-1, keepdims=True), axis=0, keepdims=True)
        dyv = diff * (1.0 / Fd)
        gdy = dyv * g_ref[...]
        dot = jnp.mean(xv * gdy, axis=-1, keepdims=True)
        dh = r * gdy - xv * (r * r * r * dot)
        dh_ref[...] = dh
        dhb_ref[...] = dh.astype(BF16)

        @pl.when(pl.program_id(0) == 0)
        def _():
            dg_ref[...] = jnp.zeros_like(dg_ref)
            loss_ref[...] = jnp.zeros_like(loss_ref)

        dg_ref[...] += jnp.sum(dyv * xv * r, axis=0, keepdims=True)
        loss_ref[...] += jnp.broadcast_to(part, loss_ref.shape)

    row = pl.BlockSpec((tm, Fd), lambda i: (i, 0))
    vec = pl.BlockSpec((1, Fd), lambda i: (0, 0))
    return pl.pallas_call(
        body, name=name, grid=(T // tm,),
        in_specs=[row, vec, row],
        out_specs=[pl.BlockSpec((1, LANES), lambda i: (0, 0)), row, row, vec],
        out_shape=[jax.ShapeDtypeStruct((1, LANES), F32), jax.ShapeDtypeStruct((T, Fd), F32),
                   jax.ShapeDtypeStruct((T, Fd), BF16), jax.ShapeDtypeStruct((1, Fd), F32)],
        compiler_params=_params(("arbitrary",)),
    )(h, g, target)


FFN_TF = 256


def _ffn_fwd(h, g, wg_t, wu_t, wd, name="ffn_fwd"):
    T, Dm = h.shape
    Fh = wd.shape[0]
    tm = _pick(T, 2048)
    nf = Fh // FFN_TF

    def body(h_ref, g_ref, wg_ref, wu_ref, wd_ref, o_ref, u_ref, a_ref, b_ref):
        j = pl.program_id(1)

        @pl.when(j == 0)
        def _():
            xv = h_ref[...]
            r = lax.rsqrt(jnp.mean(xv * xv, axis=-1, keepdims=True) + RMS_EPS)
            u_ref[...] = (xv * r * g_ref[...]).astype(BF16)
            o_ref[...] = xv

        u = u_ref[...]
        a = _dot(u, wg_ref[...], NT).astype(BF16)
        b = _dot(u, wu_ref[...], NT).astype(BF16)
        a_ref[...] = a
        b_ref[...] = b
        af = a.astype(F32)
        s = (af * jax.nn.sigmoid(af) * b.astype(F32)).astype(BF16)
        o_ref[...] += _dot(s, wd_ref[...], NN)

    row = pl.BlockSpec((tm, Dm), lambda i, j: (i, 0))
    wblk = pl.BlockSpec((FFN_TF, Dm), lambda i, j: (j, 0))
    ablk = pl.BlockSpec((tm, FFN_TF), lambda i, j: (i, j))
    return pl.pallas_call(
        body, name=name, grid=(T // tm, nf),
        in_specs=[pl.BlockSpec((tm, Dm), lambda i, j: (i, 0), pipeline_mode=pl.Buffered(1)),
                  pl.BlockSpec((1, Dm), lambda i, j: (0, 0)), wblk, wblk, wblk],
        out_specs=[row, row, ablk, ablk],
        out_shape=[jax.ShapeDtypeStruct((T, Dm), F32), jax.ShapeDtypeStruct((T, Dm), BF16),
                   jax.ShapeDtypeStruct((T, Fh), BF16), jax.ShapeDtypeStruct((T, Fh), BF16)],
        compiler_params=_params(("parallel", "arbitrary")),
    )(h, g, wg_t, wu_t, wd)


def _ffn_bwd(dh, u, a, b, wg_t, wu_t, wd, name="ffn_bwd"):
    T, Dm = dh.shape
    Fh = wd.shape[0]
    nf = Fh // FFN_TF
    once = pl.Buffered(1)

    def body(dh_ref, u_ref, a_ref, b_ref, wg_ref, wu_ref, wd_ref, du_ref, dwg_ref, dwu_ref, dwd_ref):
        j = pl.program_id(0)

        @pl.when(j == 0)
        def _():
            du_ref[...] = jnp.zeros_like(du_ref)

        ds = _dot(dh_ref[...], wd_ref[...], NT)
        af, bf = a_ref[...].astype(F32), b_ref[...].astype(F32)
        sig = jax.nn.sigmoid(af)
        sa = af * sig
        dwd_ref[...] = _dot((sa * bf).astype(BF16), dh_ref[...], TN).astype(BF16)
        dab = jnp.concatenate([(ds * bf * (sig * (1.0 + af * (1.0 - sig)))).astype(BF16),
                               (ds * sa).astype(BF16)], axis=1)
        dw = _dot(dab, u_ref[...], TN)
        dwg_ref[...] = dw[:FFN_TF].astype(BF16)
        dwu_ref[...] = dw[FFN_TF:].astype(BF16)
        du_ref[...] += _dot(dab, jnp.concatenate([wg_ref[...], wu_ref[...]], axis=0), NN)

    full = lambda: pl.BlockSpec((T, Dm), lambda j: (0, 0), pipeline_mode=once)
    wblk = pl.BlockSpec((FFN_TF, Dm), lambda j: (j, 0))
    ablk = pl.BlockSpec((T, FFN_TF), lambda j: (0, j))
    return pl.pallas_call(
        body, name=name, grid=(nf,),
        in_specs=[full(), full(), ablk, ablk, wblk, wblk, wblk],
        out_specs=[pl.BlockSpec((T, Dm), lambda j: (0, 0)), wblk, wblk, wblk],
        out_shape=[jax.ShapeDtypeStruct((T, Dm), F32)] + [jax.ShapeDtypeStruct((Fh, Dm), BF16)] * 3,
        compiler_params=_params(("arbitrary",)),
    )(dh, u, a, b, wg_t, wu_t, wd)


def _rope(x, cos_t, sin_t, col0, ncols, out_dtype, name="rope"):
    T = x.shape[0]
    wt = cos_t.shape[1]
    tm = _pick(T, 256)
    nb = ncols * LANES // wt
    half = MLA_ROPE // 2

    def body(x_ref, c_ref, s_ref, o_ref):
        xv = x_ref[...].astype(F32)
        lane = lax.broadcasted_iota(jnp.int32, xv.shape, 1)
        first = (lane & (MLA_ROPE - 1)) < half
        swapped = jnp.where(first, pltpu.roll(xv, wt - half, 1), pltpu.roll(xv, half, 1))
        o_ref[...] = (xv * c_ref[...] + swapped * s_ref[...]).astype(out_dtype)

    off = col0 * LANES // wt
    return pl.pallas_call(
        body, name=name, grid=(T // tm, nb),
        in_specs=[pl.BlockSpec((tm, wt), lambda i, j: (i, j + off)),
                  pl.BlockSpec((tm, wt), lambda i, j: (i, 0)),
                  pl.BlockSpec((tm, wt), lambda i, j: (i, 0))],
        out_specs=pl.BlockSpec((tm, wt), lambda i, j: (i, j)),
        out_shape=jax.ShapeDtypeStruct((T, ncols * LANES), out_dtype),
        compiler_params=_params(("parallel", "parallel")),
    )(x, cos_t, sin_t)


ATT_TQ = 512
ATT_TK = 256
MLA_TK = 512


def _mla_masks(shape):
    lane = lax.broadcasted_iota(jnp.int32, shape, 1)
    m0 = (lane < HEAD) | ((lane >= LANES) & (lane < LANES + MLA_ROPE))
    m1 = ((lane >= HEAD) & (lane < LANES)) | ((lane >= LANES + MLA_ROPE) & (lane < LANES + 2 * MLA_ROPE))
    return m0, m1


def _by_twos(n, step, carry):
    carry = lax.fori_loop(0, n // 2, lambda i, c: step(2 * i + 1, step(2 * i, c)), carry)
    return lax.fori_loop(0, n % 2, lambda _, c: step(n - 1, c), carry)


def _chunk_ok(tq, tk, d):
    row = lax.broadcasted_iota(jnp.int32, (tq, tk), 0)
    col = lax.broadcasted_iota(jnp.int32, (tq, tk), 1) + d * tk
    return jnp.concatenate([(col >> CHUNK_BITS) <= (row >> CHUNK_BITS)] * 2, axis=0)


def _rotate(x, cos_t, sin_t):
    half = MLA_ROPE // 2
    lane = lax.broadcasted_iota(jnp.int32, x.shape, 1)
    first = (lane & (MLA_ROPE - 1)) < half
    swapped = jnp.where(first, pltpu.roll(x, x.shape[1] - half, 1), pltpu.roll(x, half, 1))
    return x * cos_t + swapped * sin_t


def _mla_fwd(q, cos_q, sin_q, kv, kr, name="mla_fwd"):
    T = q.shape[0]
    tq, tk = _pick(T, ATT_TQ), _pick(T, MLA_TK)
    nd = tq // tk
    npair = MLA_HEADS // 2
    scale = (MLA_NOPE + MLA_ROPE) ** -0.5

    def body(q_ref, c_ref, s_ref, kn_ref, v_ref, kr_ref, o_ref, lse_ref):
        m_idx = pl.program_id(1)
        qv = _rotate(q_ref[...], c_ref[...], s_ref[...]).astype(BF16)
        m0, m1 = _mla_masks(qv.shape)
        qs = jnp.concatenate([jnp.where(m0, qv, 0), jnp.where(m1, qv, 0)], axis=0).astype(BF16)

        def block(kb, carry, ok):
            ks = pl.ds(pl.multiple_of(kb * tk, tk), tk)
            kcat = jnp.concatenate([kn_ref[ks, :], kr_ref[ks, :]], axis=1)
            mx, l, acc = carry
            s = _dot(qs, kcat, NT) * scale
            if ok is not None:
                s = jnp.where(ok, s, NEG)
            mn = jnp.maximum(mx, jnp.max(s, axis=-1, keepdims=True))
            alpha = jnp.exp(mx - mn)
            p = jnp.exp(s - mn)
            return (mn, alpha * l + jnp.sum(p, axis=-1, keepdims=True),
                    alpha * acc + _dot(p.astype(BF16), v_ref[ks, :], NN))

        init = (jnp.full((2 * tq, 1), NEG, F32), jnp.zeros((2 * tq, 1), F32), jnp.zeros((2 * tq, LANES), F32))
        res = init
        for d in range(nd):
            res = block(m_idx * nd + d, res, _chunk_ok(tq, tk, d))
        mx, l, acc = _by_twos(m_idx * nd, lambda kb, c: block(kb, c, None), res)
        h0 = lax.broadcasted_iota(jnp.int32, (tq, LANES), 1) < HEAD
        o_ref[...] = _two_heads(acc * (1.0 / l), h0).astype(o_ref.dtype)
        lse_ref[...] = _two_heads(jnp.broadcast_to(mx + jnp.log(l), (2 * tq, LANES)), h0)

    full = lambda col: pl.BlockSpec((T, LANES), col)
    table = pl.BlockSpec((tq, 2 * LANES), lambda p, m: (m, 0))
    return pl.pallas_call(
        body, name=name, grid=(npair, T // tq),
        in_specs=[pl.BlockSpec((tq, 2 * LANES), lambda p, m: (m, p)), table, table,
                  full(lambda p, m: (0, p)), full(lambda p, m: (0, npair + p)), full(lambda p, m: (0, 0))],
        out_specs=[pl.BlockSpec((tq, LANES), lambda p, m: (m, p)),
                   pl.BlockSpec((tq, LANES), lambda p, m: (m, p))],
        out_shape=[jax.ShapeDtypeStruct((T, npair * LANES), BF16),
                   jax.ShapeDtypeStruct((T, npair * LANES), F32)],
        compiler_params=_params(("parallel", "arbitrary")),
    )(q, cos_q, sin_q, kv, kv, kr)


def _mla_bwd(q, cos_q, sin_q, kv, kr, o, lse, do, do_col0, name="mla_bwd"):
    T = q.shape[0]
    tq, tk = _pick(T, ATT_TQ), _pick(T, MLA_TK)
    nd = tq // tk
    npair = MLA_HEADS // 2
    scale = (MLA_NOPE + MLA_ROPE) ** -0.5

    def body(q_ref, c_ref, s_ref, kn_ref, v_ref, kr_ref, o_ref, lse_ref, do_ref, dq_ref, dkn_ref, dv_ref, dkr_ref,
             dkn_acc, dv_acc):
        p_idx, m_idx = pl.program_id(0), pl.program_id(1)

        @pl.when(m_idx == 0)
        def _():
            dkn_acc[...] = jnp.zeros_like(dkn_acc)
            dv_acc[...] = jnp.zeros_like(dv_acc)

        @pl.when((m_idx == 0) & (p_idx == 0))
        def _():
            dkr_ref[...] = jnp.zeros_like(dkr_ref)

        qv = _rotate(q_ref[...], c_ref[...], s_ref[...]).astype(BF16)
        m0, m1 = _mla_masks(qv.shape)
        qs = jnp.concatenate([jnp.where(m0, qv, 0), jnp.where(m1, qv, 0)], axis=0).astype(BF16)
        dov = do_ref[...].astype(F32)
        h0 = lax.broadcasted_iota(jnp.int32, (tq, LANES), 1) < HEAD
        dos32 = jnp.concatenate([jnp.where(h0, dov, 0.0), jnp.where(h0, 0.0, dov)], axis=0)
        ov = o_ref[...].astype(F32)
        delta = jnp.sum(dos32 * jnp.concatenate([ov, ov], axis=0), axis=-1, keepdims=True)
        dos = dos32.astype(BF16)
        lsev = lse_ref[...]
        lse = jnp.concatenate([lsev[:, 0:1], lsev[:, HEAD:HEAD + 1]], axis=0)

        def block(kb, dq, ok):
            ks = pl.ds(pl.multiple_of(kb * tk, tk), tk)
            kcat = jnp.concatenate([kn_ref[ks, :], kr_ref[ks, :]], axis=1)
            vv = v_ref[ks, :]
            p = jnp.exp(_dot(qs, kcat, NT) * scale - lse)
            if ok is not None:
                p = jnp.where(ok, p, 0.0)
            ds = (p * (_dot(dos, vv, NT) - delta) * scale).astype(BF16)
            dkc = _dot(ds, qs, TN)
            dkn_acc[ks, :] += dkc[:, :LANES]
            dkr_ref[ks, :] += dkc[:, LANES:]
            dv_acc[ks, :] += _dot(p.astype(BF16), dos, TN)
            return dq + _dot(ds, kcat, NN)

        dq = jnp.zeros((2 * tq, 2 * LANES), F32)
        for d in range(nd):
            dq = block(m_idx * nd + d, dq, _chunk_ok(tq, tk, d))
        dq = _by_twos(m_idx * nd, lambda kb, c: block(kb, c, None), dq)
        dq_ref[...] = _rotate(jnp.where(m0, dq[:tq], jnp.where(m1, dq[tq:], 0.0)), c_ref[...],
                              -s_ref[...]).astype(BF16)

        @pl.when(m_idx == T // tq - 1)
        def _():
            dkn_ref[...] = dkn_acc[...].astype(BF16)
            dv_ref[...] = dv_acc[...].astype(BF16)

    full = lambda col: pl.BlockSpec((T, LANES), col)
    blk = lambda col: pl.BlockSpec((tq, LANES), col)
    table = pl.BlockSpec((tq, 2 * LANES), lambda p, m: (m, 0))
    return pl.pallas_call(
        body, name=name, grid=(npair, T // tq),
        in_specs=[pl.BlockSpec((tq, 2 * LANES), lambda p, m: (m, p)), table, table,
                  full(lambda p, m: (0, p)), full(lambda p, m: (0, npair + p)), full(lambda p, m: (0, 0)),
                  blk(lambda p, m: (m, p)), blk(lambda p, m: (m, p)),
                  blk(lambda p, m: (m, do_col0 + p))],
        out_specs=[pl.BlockSpec((tq, 2 * LANES), lambda p, m: (m, p)),
                   full(lambda p, m: (0, p)), full(lambda p, m: (0, p)), full(lambda p, m: (0, 0))],
        out_shape=[jax.ShapeDtypeStruct((T, npair * 2 * LANES), BF16),
                   jax.ShapeDtypeStruct((T, npair * LANES), BF16),
                   jax.ShapeDtypeStruct((T, npair * LANES), BF16),
                   jax.ShapeDtypeStruct((T, LANES), F32)],
        scratch_shapes=[pltpu.VMEM((T, LANES), F32)] * 2,
        compiler_params=_params(("arbitrary", "arbitrary")),
    )(q, cos_q, sin_q, kv, kv, kr, o, lse, do)


def _split_dot(x, tri):
    hi = x.astype(BF16)
    lo = (x - hi.astype(F32)).astype(BF16)
    both = _dot(jnp.concatenate([hi, lo], axis=0), tri, NN)
    return both[:x.shape[0]] + both[x.shape[0]:]


def _sb_terms(qh, kk, before):
    z = _dot(qh, kk, NT)
    sp = jnp.maximum(z, 0.0) + jnp.log(1.0 + jnp.exp(-jnp.abs(z)))
    lk = -sp if before is None else jnp.where(before, -sp, 0.0)
    return z, sp, lk


def _sb_setup(q_ref, tq, tk, scale):
    qv = (q_ref[...].astype(F32) * scale).astype(BF16)
    lane = lax.broadcasted_iota(jnp.int32, (tq, LANES), 1)
    h0 = lane < HEAD
    qs = jnp.concatenate([jnp.where(h0, qv, 0), jnp.where(h0, 0, qv)], axis=0).astype(BF16)
    row = lax.broadcasted_iota(jnp.int32, (tk, tk), 0)
    col = lax.broadcasted_iota(jnp.int32, (tk, tk), 1)
    return qs, h0, row, col


def _sb_before(tq, tk, d):
    row = lax.broadcasted_iota(jnp.int32, (tq, tk), 0)
    col = lax.broadcasted_iota(jnp.int32, (tq, tk), 1) + d * tk
    return jnp.concatenate([col < row] * 2, axis=0)


def _two_heads(x, h0):
    tq = x.shape[0] // 2
    return jnp.where(h0, x[:tq], x[tq:])


def _sb_fwd(qkv, col0, dep, name="sb_fwd"):
    T = qkv.shape[0]
    tq, tk = _pick(T, ATT_TQ), _pick(T, ATT_TK)
    nd = tq // tk
    npair = SB_HEADS // 2
    scale = SB_DIM ** -0.5

    def body(q_ref, k_ref, v_ref, dep_ref, o_ref, o32_ref, w_ref, sp_ref):
        m_idx = pl.program_id(1)
        qs, h0, row, col = _sb_setup(q_ref, tq, tk, scale)
        later = (row > col).astype(BF16)

        def block(kb, carry, before):
            ks = pl.ds(pl.multiple_of(kb * tk, tk), tk)
            c, acc = carry
            z, sp, lk = _sb_terms(qs, k_ref[ks, :].astype(BF16), before)
            w = jnp.exp((z - sp) + _split_dot(lk, later) + c)
            if before is not None:
                w = jnp.where(before, w, 0.0)
            wb = w.astype(BF16)
            w_ref[0, 0, kb] = wb
            sp_ref[0, 0, kb] = sp.astype(BF16)
            return (c + jnp.sum(lk, axis=-1, keepdims=True), acc + _dot(wb, v_ref[ks, :].astype(BF16), NN))

        init = (jnp.zeros((2 * tq, 1), F32), jnp.zeros((2 * tq, LANES), F32))
        res = init
        for d in reversed(range(nd)):
            res = block(m_idx * nd + d, res, _sb_before(tq, tk, d))
        res = _by_twos(m_idx * nd, lambda i, c: block(m_idx * nd - 1 - i, c, None), res)
        o = _two_heads(res[1], h0)
        o_ref[...] = o.astype(o_ref.dtype)
        o32_ref[...] = o

    full = lambda col: pl.BlockSpec((T, LANES), col)
    blk = pl.BlockSpec((tq, LANES), lambda p, m: (m, p))
    return pl.pallas_call(
        body, name=name, grid=(npair, T // tq),
        in_specs=[pl.BlockSpec((tq, LANES), lambda p, m: (m, col0 + p)),
                  full(lambda p, m: (0, col0 + npair + p)), full(lambda p, m: (0, col0 + 2 * npair + p)),
                  pl.BlockSpec((8, LANES), lambda p, m: (0, 0))],
        out_specs=[blk, blk] + [pl.BlockSpec((1, 1, T // tk, 2 * tq, tk), lambda p, m: (p, m, 0, 0, 0))] * 2,
        out_shape=[jax.ShapeDtypeStruct((T, npair * LANES), BF16), jax.ShapeDtypeStruct((T, npair * LANES), F32)]
        + [jax.ShapeDtypeStruct((npair, T // tq, T // tk, 2 * tq, tk), BF16)] * 2,
        compiler_params=_params(("parallel", "arbitrary")),
    )(qkv, qkv, qkv, dep)


def _sb_bwd(qkv, col0, o32, w_all, sp_all, do, do_col0, dep, name="sb_bwd"):
    T = qkv.shape[0]
    tq, tk = _pick(T, ATT_TQ), _pick(T, ATT_TK)
    nd = tq // tk
    npair = SB_HEADS // 2
    scale = SB_DIM ** -0.5

    def body(q_ref, k_ref, v_ref, o_ref, w_ref, sp_ref, do_ref, dep_ref, dq_ref, dk_ref, dv_ref, dk_acc, dv_acc):
        m_idx = pl.program_id(1)

        @pl.when(m_idx == 0)
        def _():
            dk_acc[...] = jnp.zeros_like(dk_acc)
            dv_acc[...] = jnp.zeros_like(dv_acc)

        qs, h0, row, col = _sb_setup(q_ref, tq, tk, scale)
        dov = do_ref[...].astype(F32)
        dos = jnp.concatenate([jnp.where(h0, dov, 0.0), jnp.where(h0, 0.0, dov)], axis=0).astype(BF16)
        ov = o_ref[...]
        etot = jnp.sum(dos.astype(F32) * jnp.concatenate([ov, ov], axis=0), axis=-1, keepdims=True)
        from_here = (row >= col).astype(BF16)

        def block(kb, carry, before):
            ks = pl.ds(pl.multiple_of(kb * tk, tk), tk)
            kk = k_ref[ks, :].astype(BF16)
            vv = v_ref[ks, :].astype(BF16)
            es, dqa = carry
            wb = w_ref[0, 0, kb]
            e = wb.astype(F32) * _dot(dos, vv, NT)
            prev = etot - (_split_dot(e, from_here) + es)
            sig_neg = jnp.exp(-sp_ref[0, 0, kb].astype(F32))
            dz = e * sig_neg - (1.0 - sig_neg) * prev
            if before is not None:
                dz = jnp.where(before, dz, 0.0)
            dzb = dz.astype(BF16)
            dk_acc[ks, :] += _dot(dzb, qs, TN)
            dv_acc[ks, :] += _dot(wb, dos, TN)
            return es + jnp.sum(e, axis=-1, keepdims=True), dqa + _dot(dzb, kk, NN)

        init = (jnp.zeros((2 * tq, 1), F32), jnp.zeros((2 * tq, LANES), F32))
        res = init
        for d in reversed(range(nd)):
            res = block(m_idx * nd + d, res, _sb_before(tq, tk, d))
        res = _by_twos(m_idx * nd, lambda i, c: block(m_idx * nd - 1 - i, c, None), res)
        dq_ref[...] = (_two_heads(res[1], h0) * scale).astype(BF16)

        @pl.when(m_idx == T // tq - 1)
        def _():
            dk_ref[...] = dk_acc[...].astype(BF16)
            dv_ref[...] = dv_acc[...].astype(BF16)

    full = lambda col: pl.BlockSpec((T, LANES), col)
    blk = lambda col: pl.BlockSpec((tq, LANES), col)
    return pl.pallas_call(
        body, name=name, grid=(npair, T // tq),
        in_specs=[blk(lambda p, m: (m, col0 + p)),
                  full(lambda p, m: (0, col0 + npair + p)), full(lambda p, m: (0, col0 + 2 * npair + p)),
                  blk(lambda p, m: (m, p)),
                  pl.BlockSpec((1, 1, T // tk, 2 * tq, tk), lambda p, m: (p, m, 0, 0, 0)),
                  pl.BlockSpec((1, 1, T // tk, 2 * tq, tk), lambda p, m: (p, m, 0, 0, 0)),
                  blk(lambda p, m: (m, do_col0 + p)), pl.BlockSpec((8, LANES), lambda p, m: (0, 0))],
        out_specs=[blk(lambda p, m: (m, p)), full(lambda p, m: (0, p)), full(lambda p, m: (0, p))],
        out_shape=[jax.ShapeDtypeStruct((T, npair * LANES), BF16)] * 3,
        scratch_shapes=[pltpu.VMEM((T, LANES), F32)] * 2,
        compiler_params=_params(("arbitrary", "arbitrary")),
    )(qkv, qkv, qkv, o32, w_all, sp_all, do, dep)


def _band_in_window():
    cq = lax.broadcasted_iota(jnp.int32, (BAND_TQ, BAND_W), 0) >> CHUNK_BITS
    ckp = lax.broadcasted_iota(jnp.int32, (BAND_TQ, BAND_W), 1) >> CHUNK_BITS
    return (ckp >= cq) & (ckp <= cq + LEFT_CHUNKS)


def _band_real(m_idx):
    j = lax.broadcasted_iota(jnp.int32, (BAND_TQ, BAND_W), 1)
    return j >= PAD_KEYS - m_idx * BAND_TQ


def _band_probs(qh, kw, bias, real, scale):
    s = jnp.where(real, _dot(qh, kw, NT) * scale + bias, NEG)
    e = jnp.exp(s - jnp.max(s, axis=-1, keepdims=True))
    return e * (1.0 / jnp.sum(e, axis=-1, keepdims=True))


BAND_SUB = 8


def _band_fwd(qkv, k_pad, v_pad, bias_w, name="band_fwd"):
    T = qkv.shape[0]
    npair = C_HEADS // 2
    scale = C_DIM ** -0.5
    rows = BAND_SUB * BAND_TQ

    def body(q_ref, k_ref, v_ref, b_ref, o_ref, p_ref):
        lane = lax.broadcasted_iota(jnp.int32, (BAND_TQ, LANES), 1)
        h0 = lane < HEAD
        bias = jnp.concatenate([b_ref[0], b_ref[1]], axis=0)
        for sub in range(BAND_SUB):
            m_idx = pl.program_id(1) * BAND_SUB + sub
            win = pl.ds(pl.multiple_of(m_idx * BAND_TQ, BAND_TQ), BAND_W)
            kw, vw = k_ref[win, :], v_ref[win, :]
            qv = q_ref[sub * BAND_TQ:(sub + 1) * BAND_TQ, :]
            qs = jnp.concatenate([jnp.where(h0, qv, 0), jnp.where(h0, 0, qv)], axis=0).astype(BF16)
            p = _band_probs(qs, kw, bias, jnp.concatenate([_band_real(m_idx)] * 2, axis=0), scale).astype(BF16)
            p_ref[0, sub] = p
            o = _two_heads(_dot(p, vw, NN), h0)
            o_ref[sub * BAND_TQ:(sub + 1) * BAND_TQ, :] = o.astype(o_ref.dtype)

    Tp = T + PAD_KEYS
    return pl.pallas_call(
        body, name=name, grid=(npair, T // rows),
        in_specs=[pl.BlockSpec((rows, LANES), lambda p, m: (m, p)),
                  pl.BlockSpec((Tp, LANES), lambda p, m: (0, p)),
                  pl.BlockSpec((Tp, LANES), lambda p, m: (0, p)),
                  pl.BlockSpec((2, BAND_TQ, BAND_W), lambda p, m: (p, 0, 0))],
        out_specs=[pl.BlockSpec((rows, LANES), lambda p, m: (m, p)),
                   pl.BlockSpec((1, BAND_SUB, 2 * BAND_TQ, BAND_W), lambda p, m: (p, m, 0, 0))],
        out_shape=[jax.ShapeDtypeStruct((T, npair * LANES), BF16),
                   jax.ShapeDtypeStruct((npair, T // BAND_TQ, 2 * BAND_TQ, BAND_W), BF16)],
        compiler_params=_params(("parallel", "arbitrary")),
    )(qkv, k_pad, v_pad, bias_w)


def _band_bwd(qkv, k_pad, v_pad, probs, do, name="band_bwd"):
    T = qkv.shape[0]
    npair = C_HEADS // 2
    scale = C_DIM ** -0.5

    rows = BAND_SUB * BAND_TQ

    def body(q_ref, k_ref, v_ref, p_ref, do_ref, dq_ref, dk_ref, dv_ref, db_ref, dk_acc, dv_acc):
        @pl.when(pl.program_id(1) == 0)
        def _():
            dk_acc[...] = jnp.zeros_like(dk_acc)
            dv_acc[...] = jnp.zeros_like(dv_acc)
            db_ref[...] = jnp.zeros_like(db_ref)

        lane = lax.broadcasted_iota(jnp.int32, (BAND_TQ, LANES), 1)
        h0 = lane < HEAD
        dbs = jnp.zeros((2 * BAND_TQ, BAND_W), F32)
        for sub in range(BAND_SUB):
            m_idx = pl.program_id(1) * BAND_SUB + sub
            win = pl.ds(pl.multiple_of(m_idx * BAND_TQ, BAND_TQ), BAND_W)
            kw, vw = k_ref[win, :], v_ref[win, :]
            qv = q_ref[sub * BAND_TQ:(sub + 1) * BAND_TQ, :]
            dov = do_ref[sub * BAND_TQ:(sub + 1) * BAND_TQ, :].astype(F32)
            qs = jnp.concatenate([jnp.where(h0, qv, 0), jnp.where(h0, 0, qv)], axis=0).astype(BF16)
            dos = jnp.concatenate([jnp.where(h0, dov, 0.0), jnp.where(h0, 0.0, dov)], axis=0).astype(BF16)
            pb = p_ref[0, sub]
            p = pb.astype(F32)
            dp = _dot(dos, vw, NT)
            dsb = p * (dp - jnp.sum(p * dp, axis=-1, keepdims=True))
            dbs = dbs + dsb
            dsq = (dsb * scale).astype(BF16)
            dq_ref[sub * BAND_TQ:(sub + 1) * BAND_TQ, :] = _two_heads(_dot(dsq, kw, NN), h0).astype(BF16)
            dk_acc[win, :] += _dot(dsq, qs, TN)
            dv_acc[win, :] += _dot(pb, dos, TN)
        db_ref[0] += dbs[:BAND_TQ]
        db_ref[1] += dbs[BAND_TQ:]

        @pl.when(pl.program_id(1) == T // rows - 1)
        def _():
            dk_ref[...] = dk_acc[...].astype(BF16)
            dv_ref[...] = dv_acc[...].astype(BF16)

    Tp = T + PAD_KEYS
    blk = lambda col: pl.BlockSpec((rows, LANES), col)
    full = pl.BlockSpec((Tp, LANES), lambda p, m: (0, p))
    bias = pl.BlockSpec((2, BAND_TQ, BAND_W), lambda p, m: (p, 0, 0))
    prob = pl.BlockSpec((1, BAND_SUB, 2 * BAND_TQ, BAND_W), lambda p, m: (p, m, 0, 0))
    return pl.pallas_call(
        body, name=name, grid=(npair, T // rows),
        in_specs=[blk(lambda p, m: (m, p)), full, full, prob, blk(lambda p, m: (m, p))],
        out_specs=[blk(lambda p, m: (m, p)), full, full, bias],
        out_shape=[jax.ShapeDtypeStruct((T, npair * LANES), BF16),
                   jax.ShapeDtypeStruct((Tp, npair * LANES), BF16),
                   jax.ShapeDtypeStruct((Tp, npair * LANES), BF16),
                   jax.ShapeDtypeStruct((C_HEADS, BAND_TQ, BAND_W), F32)],
        scratch_shapes=[pltpu.VMEM((Tp, LANES), F32)] * 2,
        compiler_params=_params(("arbitrary", "arbitrary")),
    )(qkv, k_pad, v_pad, probs, do)


def _skew_bits(x, left):
    w = x.shape[1]
    row = lax.broadcasted_iota(jnp.int32, x.shape, 0)
    for b in range(BAND_TQ.bit_length() - 1):
        amt = (w - (1 << b)) if left else (1 << b)
        x = jnp.where((row >> b) & 1 == 1, pltpu.roll(x, amt, 1), x)
    return x


def _toeplitz(diag, name="toeplitz"):
    H = diag.shape[0]

    def body(d_ref, o_ref):
        x = jnp.broadcast_to(d_ref[0], (BAND_TQ, TOEP_W))
        o_ref[0] = jnp.where(_band_in_window(), _skew_bits(x, left=False)[:, BAND_TQ:], NEG)

    return pl.pallas_call(
        body, name=name, grid=(H,),
        in_specs=[pl.BlockSpec((1, 1, TOEP_W), lambda h: (h, 0, 0))],
        out_specs=pl.BlockSpec((1, BAND_TQ, BAND_W), lambda h: (h, 0, 0)),
        out_shape=jax.ShapeDtypeStruct((H, BAND_TQ, BAND_W), F32),
        compiler_params=_params(("parallel",)),
    )(diag.reshape(H, 1, TOEP_W))


def _toeplitz_bwd(dbias, name="toeplitz_bwd"):
    H = dbias.shape[0]

    def body(d_ref, o_ref):
        x = jnp.concatenate([jnp.zeros((BAND_TQ, BAND_TQ), F32), d_ref[0]], axis=1)
        h = BAND_TQ // 2
        while h >= 8:
            x = x[:h] + pltpu.roll(x[h:2 * h], TOEP_W - h, 1)
            h //= 2
        o_ref[0] = jnp.sum(_skew_bits(x, left=True), axis=0, keepdims=True)

    return pl.pallas_call(
        body, name=name, grid=(H,),
        in_specs=[pl.BlockSpec((1, BAND_TQ, BAND_W), lambda h: (h, 0, 0))],
        out_specs=pl.BlockSpec((1, 1, TOEP_W), lambda h: (h, 0, 0)),
        out_shape=jax.ShapeDtypeStruct((H, 1, TOEP_W), F32),
        compiler_params=_params(("parallel",)),
    )(dbias).reshape(H, TOEP_W)


_HBM = pl.BlockSpec(memory_space=pltpu.HBM)
_SEM = pl.BlockSpec(memory_space=pltpu.SEMAPHORE)
_EFFECT = pltpu.SideEffectType.DATAFLOW_SIDE_EFFECTING


def _peers():
    x, y, c = lax.axis_index("x"), lax.axis_index("y"), lax.axis_index("c")
    out = []
    for k in range(1, N_DEV):
        peer = (1 - x if (k >> 2) & 1 else x, 1 - y if (k >> 1) & 1 else y, 1 - c if k & 1 else c)
        out.append((peer, 4 * peer[0] + 2 * peer[1] + peer[2]))
    return 4 * x + 2 * y + c, out


def _split_copies(ins, lands, scatter, send_sem, recv_sem, arriving):
    me, peers = _peers()
    out = []
    for a in range(len(ins)):
        for peer, idx in peers:
            out.append(pltpu.make_async_remote_copy(
                src_ref=ins[a].at[idx] if scatter[a] else ins[a],
                dst_ref=lands[a].at[idx if arriving else me], send_sem=send_sem, recv_sem=recv_sem,
                device_id=peer, device_id_type=pl.DeviceIdType.MESH))
    return out


def _landing_zones(arrays, scatter):
    return [lax.empty((N_DEV,) + (a.shape[1:] if s else a.shape), a.dtype) for a, s in zip(arrays, scatter)]


def _place_own(arrays, scatter, name):
    n = len(arrays)
    lands = _landing_zones(arrays, scatter)
    me = (4 * lax.axis_index("x") + 2 * lax.axis_index("y") + lax.axis_index("c")).astype(jnp.int32).reshape(1)

    def body(me_ref, *refs):
        for a in range(n):
            refs[2 * n + a][...] = refs[a][...].reshape(refs[2 * n + a].shape)

    def row_spec(shape):
        zeros = (0,) * (len(shape) - 1)
        return pl.BlockSpec((1,) + tuple(shape[1:]), lambda i, me_ref: (me_ref[0],) + zeros)

    in_specs = [row_spec(a.shape) if s else pl.BlockSpec(a.shape, lambda i, me_ref, nd=a.ndim: (0,) * nd)
                for a, s in zip(arrays, scatter)]
    return pl.pallas_call(
        body, name=name,
        out_shape=[jax.ShapeDtypeStruct(l.shape, l.dtype) for l in lands],
        grid_spec=pltpu.PrefetchScalarGridSpec(
            num_scalar_prefetch=1, grid=(1,),
            in_specs=in_specs + [pl.BlockSpec(memory_space=pl.ANY)] * n,
            out_specs=[row_spec(l.shape) for l in lands]),
        input_output_aliases={1 + n + i: i for i in range(n)},
        compiler_params=_params(("arbitrary",)),
    )(me, *arrays, *lands)


def _exchange_start_groups(groups, scatter, after, name, lands=None):
    sizes = [len(g) for g in groups]
    arrays = [a for g in groups for a in g]
    n, ng = len(arrays), len(groups)
    flags = list(scatter) if isinstance(scatter, (list, tuple)) else [scatter] * n
    if lands is None:
        lands = list(_place_own(arrays, flags, name=name.replace("_start_", "_own_")))
    else:
        lands = [l for g in lands for l in g]
    starts = np.cumsum([0] + sizes)

    def body(*refs):
        ins, lnd = refs[:n], refs[n:2 * n]
        sems = refs[2 * n + 1:2 * n + 1 + 2 * ng]
        token = refs[-1]
        for g in range(ng):
            sl = slice(starts[g], starts[g + 1])
            for cp in _split_copies(ins[sl], lnd[sl], flags[sl], sems[2 * g], sems[2 * g + 1], arriving=False):
                cp.start()
        token[...] = jnp.zeros_like(token)

    hbm = lambda a: pltpu.HBM(a.shape, a.dtype)
    out = pl.pallas_call(
        body, name=name,
        out_shape=(*[pltpu.SemaphoreType.DMA(())] * (2 * ng),
                   *[hbm(a) for a in arrays], *[hbm(a) for a in lands],
                   jax.ShapeDtypeStruct((8, LANES), F32)),
        in_specs=[_HBM] * (2 * n) + [pl.BlockSpec(memory_space=pl.ANY)],
        out_specs=(*[_SEM] * (2 * ng), *([_HBM] * (2 * n)), pl.BlockSpec(memory_space=pltpu.VMEM)),
        input_output_aliases={i: 2 * ng + i for i in range(2 * n)},
        compiler_params=pltpu.CompilerParams(has_side_effects=_EFFECT),
    )(*[pltpu.with_memory_space_constraint(a, pltpu.HBM) for a in list(arrays) + lands], after)
    ins_out, lands_out = out[2 * ng:2 * ng + n], out[2 * ng + n:2 * ng + 2 * n]
    handles = [(out[2 * g], out[2 * g + 1], list(ins_out[starts[g]:starts[g + 1]]),
                list(lands_out[starts[g]:starts[g + 1]]), tuple(flags[starts[g]:starts[g + 1]]))
               for g in range(ng)]
    return handles, out[-1]


def _exchange_start(arrays, scatter, after, name):
    handles, token = _exchange_start_groups([list(arrays)], list(scatter), after, name)
    return handles[0], token


def _exchange_wait(handle, after, name):
    send_sem, recv_sem, ins, lands, scatter = handle
    n = len(ins)
    after = after if isinstance(after, tuple) else (after,)

    def body(*refs):
        i_ref, l_ref = refs[:n], refs[n:2 * n]
        s_sem, r_sem = refs[2 * n:2 * n + 2]
        for cp in _split_copies(i_ref, l_ref, scatter, s_sem, r_sem, arriving=False):
            cp.wait_send()
        for cp in _split_copies(i_ref, l_ref, scatter, s_sem, r_sem, arriving=True):
            cp.wait_recv()

    hbm = lambda a: pltpu.HBM(a.shape, a.dtype)
    out = pl.pallas_call(
        body, name=name,
        out_shape=tuple(hbm(a) for a in ins + lands),
        in_specs=[_HBM] * (2 * n) + [_SEM, _SEM] + [pl.BlockSpec(memory_space=pl.ANY)] * len(after),
        out_specs=tuple([_HBM] * (2 * n)),
        input_output_aliases={i: i for i in range(2 * n)},
        compiler_params=pltpu.CompilerParams(has_side_effects=_EFFECT),
    )(*ins, *lands, send_sem, recv_sem, *after)
    return list(out[n:])


_SIBLING = 1
_CHIPS = (4, 2, 6)


def _peer_of(k):
    x, y, c = lax.axis_index("x"), lax.axis_index("y"), lax.axis_index("c")
    peer = (1 - x if (k >> 2) & 1 else x, 1 - y if (k >> 1) & 1 else y, 1 - c if k & 1 else c)
    return peer, 4 * peer[0] + 2 * peer[1] + peer[2]


def _rcopy(src, dst, send_sem, recv_sem, k):
    return pltpu.make_async_remote_copy(src_ref=src, dst_ref=dst, send_sem=send_sem, recv_sem=recv_sem,
                                        device_id=_peer_of(k)[0], device_id_type=pl.DeviceIdType.MESH)


def _gather2_start(groups, lands, after, name):
    sizes = [len(g) for g in groups]
    arrays = [a for g in groups for a in g]
    lands = [l for g in lands for l in g]
    n, ng = len(arrays), len(groups)
    starts = np.cumsum([0] + sizes)

    def body(*refs):
        ins, lnd = refs[:n], refs[n:2 * n]
        sems = refs[2 * n + 1:2 * n + 1 + 4 * ng]
        me, _ = _peers()
        for g in range(ng):
            send_d, recv_d, send_i, recv_i = sems[4 * g:4 * g + 4]
            for a in range(starts[g], starts[g + 1]):
                for k in _CHIPS:
                    _rcopy(ins[a], lnd[a].at[me], send_i, recv_i, k).start()
                _rcopy(ins[a], lnd[a].at[me], send_d, recv_d, _SIBLING).start()
        refs[-1][...] = jnp.zeros_like(refs[-1])

    hbm = lambda a: pltpu.HBM(a.shape, a.dtype)
    out = pl.pallas_call(
        body, name=name,
        out_shape=(*[pltpu.SemaphoreType.DMA(())] * (4 * ng), *[hbm(a) for a in arrays], *[hbm(a) for a in lands],
                   jax.ShapeDtypeStruct((8, LANES), F32)),
        in_specs=[_HBM] * (2 * n) + [pl.BlockSpec(memory_space=pl.ANY)],
        out_specs=(*[_SEM] * (4 * ng), *([_HBM] * (2 * n)), pl.BlockSpec(memory_space=pltpu.VMEM)),
        input_output_aliases={i: 4 * ng + i for i in range(2 * n)},
        compiler_params=pltpu.CompilerParams(has_side_effects=_EFFECT),
    )(*[pltpu.with_memory_space_constraint(a, pltpu.HBM) for a in arrays + lands], after)
    ins_out, lands_out = out[4 * ng:4 * ng + n], out[4 * ng + n:4 * ng + 2 * n]
    handles = [dict(sems=out[4 * g:4 * g + 4], ins=list(ins_out[starts[g]:starts[g + 1]]),
                    lands=list(lands_out[starts[g]:starts[g + 1]])) for g in range(ng)]
    return handles, out[-1]


def _gather2_pass_on(handle, after, name):
    lands, recv_i = handle["lands"], handle["sems"][3]
    n = len(lands)
    after = after if isinstance(after, tuple) else (after,)

    def body(*refs):
        lnd, r_i = refs[:n], refs[n]
        send_f, recv_f = refs[n + 1 + len(after):n + 3 + len(after)]
        for a in range(n):
            for k in _CHIPS:
                row = _peer_of(k)[1]
                _rcopy(lnd[a].at[row], lnd[a].at[row], send_f, r_i, k).wait_recv()
        for a in range(n):
            for k in _CHIPS:
                row = _peer_of(k)[1]
                _rcopy(lnd[a].at[row], lnd[a].at[row], send_f, recv_f, _SIBLING).start()
        refs[-1][...] = jnp.zeros_like(refs[-1])

    hbm = lambda a: pltpu.HBM(a.shape, a.dtype)
    out = pl.pallas_call(
        body, name=name,
        out_shape=(pltpu.SemaphoreType.DMA(()), pltpu.SemaphoreType.DMA(()), *[hbm(a) for a in lands],
                   jax.ShapeDtypeStruct((8, LANES), F32)),
        in_specs=[_HBM] * n + [_SEM] + [pl.BlockSpec(memory_space=pl.ANY)] * len(after),
        out_specs=(_SEM, _SEM, *([_HBM] * n), pl.BlockSpec(memory_space=pltpu.VMEM)),
        input_output_aliases={i: 2 + i for i in range(n)},
        compiler_params=pltpu.CompilerParams(has_side_effects=_EFFECT),
    )(*lands, recv_i, *after)
    return dict(handle, lands=list(out[2:2 + n]), passed=(out[0], out[1])), out[-1]


def _gather2_wait(handle, after, name):
    ins, lands = handle["ins"], handle["lands"]
    send_d, recv_d, send_i, _ = handle["sems"]
    send_f, recv_f = handle["passed"]
    n = len(ins)
    after = after if isinstance(after, tuple) else (after,)

    def body(*refs):
        i_ref, lnd = refs[:n], refs[n:2 * n]
        s_d, r_d, s_i, s_f, r_f = refs[2 * n:2 * n + 5]
        me, _ = _peers()
        sib = _peer_of(_SIBLING)[1]
        for a in range(n):
            _rcopy(i_ref[a], lnd[a].at[sib], s_d, r_d, _SIBLING).wait_send()
            _rcopy(i_ref[a], lnd[a].at[sib], s_d, r_d, _SIBLING).wait_recv()
            for k in _CHIPS:
                row = _peer_of(k)[1]
                _rcopy(i_ref[a], lnd[a].at[me], s_i, r_d, k).wait_send()
                _rcopy(lnd[a].at[row], lnd[a].at[row], s_f, r_f, _SIBLING).wait_send()
                _rcopy(lnd[a].at[row], lnd[a].at[_peer_of(k ^ _SIBLING)[1]], s_f, r_f, _SIBLING).wait_recv()

    hbm = lambda a: pltpu.HBM(a.shape, a.dtype)
    out = pl.pallas_call(
        body, name=name,
        out_shape=tuple(hbm(a) for a in ins + lands),
        in_specs=[_HBM] * (2 * n) + [_SEM] * 5 + [pl.BlockSpec(memory_space=pl.ANY)] * len(after),
        out_specs=tuple([_HBM] * (2 * n)),
        input_output_aliases={i: i for i in range(2 * n)},
        compiler_params=pltpu.CompilerParams(has_side_effects=_EFFECT),
    )(*ins, *lands, send_d, recv_d, send_i, send_f, recv_f, *after)
    return list(out[n:])


def _adamw(w, parts, m, v, name="adamw"):
    R, C = w.shape
    L = len(parts)
    rl = R // L
    tr = max([t for t in range(16, 257, 16) if rl % t == 0], default=rl)
    nb = rl // tr
    c1 = 1.0 - ADAM_B1 ** ADAM_STEP
    c2 = 1.0 - ADAM_B2 ** ADAM_STEP

    def body(*refs):
        w_ref, p_refs, (m_ref, v_ref, g_ref, d_ref, nm_ref, nv_ref) = refs[0], refs[1:1 + L], refs[1 + L:]
        g = None
        for j, p_ref in enumerate(p_refs):
            gj = p_ref[0].astype(F32)
            for i in range(1, N_DEV):
                gj = gj + p_ref[i].astype(F32)
            g = gj if g is None else jnp.where(pl.program_id(0) == j, gj, g)
        nm = ADAM_B1 * m_ref[...] + (1.0 - ADAM_B1) * g
        nv = ADAM_B2 * v_ref[...] + (1.0 - ADAM_B2) * (g * g)
        g_ref[...] = g
        nm_ref[...] = nm
        nv_ref[...] = nv
        d_ref[...] = -ADAM_LR * ((nm / c1) / (jnp.sqrt(nv / c2) + ADAM_EPS) + ADAM_WD * w_ref[...])

    blk = pl.BlockSpec((tr, C), lambda l, i: (l * nb + i, 0))
    part = lambda j: pl.BlockSpec((N_DEV, tr, C), lambda l, i: (0, jnp.where(l == j, i, 0), 0))
    return pl.pallas_call(
        body, name=name, grid=(L, nb),
        in_specs=[blk] + [part(j) for j in range(L)] + [blk, blk],
        out_specs=[blk] * 4,
        out_shape=[jax.ShapeDtypeStruct((R, C), F32)] * 4,
        compiler_params=_params(("arbitrary", "arbitrary")),
    )(w, *parts, m, v)


_O1 = Q_LORA
_O2 = _O1 + KV_LORA
_O3 = _O2 + MLA_ROPE
_NB = SB_HEADS * SB_DIM
IN_W = _O2 + LANES + 3 * _NB
COL_KR = _O2 // LANES
COL_SB = COL_KR + 1


def _w_in_local(w):
    kr = w[_O2:_O3]
    pad = jnp.zeros((LANES - 2 * MLA_ROPE, w.shape[1]), w.dtype)
    return jnp.concatenate([w[:_O2], kr, kr, pad, w[_O3:]], axis=0)


def _w_in_grad(g):
    kr = (g[_O2:_O2 + MLA_ROPE].astype(F32) + g[_O2 + MLA_ROPE:_O2 + 2 * MLA_ROPE].astype(F32)).astype(g.dtype)
    return jnp.concatenate([g[:_O2], kr, g[_O2 + LANES:]], axis=0)


def _w_uq_local(w):
    w3 = w.reshape(MLA_HEADS // 2, 2, MLA_NOPE + MLA_ROPE, w.shape[1])
    nope = w3[:, :, :MLA_NOPE].reshape(MLA_HEADS // 2, 2 * MLA_NOPE, w.shape[1])
    rope = w3[:, :, MLA_NOPE:].reshape(MLA_HEADS // 2, 2 * MLA_ROPE, w.shape[1])
    pad = jnp.zeros((MLA_HEADS // 2, LANES - 2 * MLA_ROPE, w.shape[1]), w.dtype)
    return jnp.concatenate([nope, rope, pad], axis=1).reshape(-1, w.shape[1])


def _w_uq_grad(g):
    g3 = g.reshape(MLA_HEADS // 2, 2 * LANES, g.shape[1])
    nope = g3[:, :2 * MLA_NOPE].reshape(MLA_HEADS // 2, 2, MLA_NOPE, g.shape[1])
    rope = g3[:, LANES:LANES + 2 * MLA_ROPE].reshape(MLA_HEADS // 2, 2, MLA_ROPE, g.shape[1])
    return jnp.concatenate([nope, rope], axis=2).reshape(-1, g.shape[1])


def _w_ukv_local(w):
    w3 = w.reshape(MLA_HEADS, MLA_NOPE + MLA_V, w.shape[1])
    return jnp.concatenate([w3[:, :MLA_NOPE].reshape(-1, w.shape[1]),
                            w3[:, MLA_NOPE:].reshape(-1, w.shape[1])], axis=0)


def _w_ukv_grad(g):
    half = MLA_HEADS * MLA_NOPE
    kn = g[:half].reshape(MLA_HEADS, MLA_NOPE, g.shape[1])
    vv = g[half:].reshape(MLA_HEADS, MLA_V, g.shape[1])
    return jnp.concatenate([kn, vv], axis=1).reshape(-1, g.shape[1])


def _rope_tables(T):
    pos = jnp.arange(T, dtype=F32)
    inv_freq = ROPE_THETA ** (-jnp.arange(0, MLA_ROPE, 2, dtype=F32) / MLA_ROPE)
    ang = pos[:, None] * inv_freq[None, :]
    cos, sin = jnp.cos(ang), jnp.sin(ang)
    ones = jnp.ones((T, LANES - 2 * MLA_ROPE), F32)
    cos_k = jnp.concatenate([cos, cos, cos, cos, ones], axis=1)
    sin_k = jnp.concatenate([-sin, sin, -sin, sin, 0.0 * ones], axis=1)
    cos_q = jnp.concatenate([jnp.ones((T, LANES), F32), cos_k], axis=1)
    sin_q = jnp.concatenate([jnp.zeros((T, LANES), F32), sin_k], axis=1)
    return cos_q, sin_q, cos_k, sin_k


def _bias_diag_index():
    ell = np.arange(TOEP_W)
    return np.clip(BAND_W - ell, -REL_CLIP, REL_CLIP) + REL_CLIP


def _local_step(x, target, small, get_weights, put_grads, prefetch):
    T = x.shape[0]
    cos_q, sin_q, cos_k, sin_k = _rope_tables(T)
    G = {}
    W = dict(small)

    u0 = _rms_fwd(x, W["g_mix"][0:1], name="rms_mix0")
    bias_w = _toeplitz(W["od_rel_bias"][:, _bias_diag_index()])
    W.update(get_weights("in0", (u0, bias_w)))
    proj = _mm(u0, W["w_in_t"], dims="nt", name="proj_in")
    W.update(get_weights("mix0", proj))
    c_q, c_kv = proj[:, :_O1], proj[:, _O1:_O2]
    nq = _rms_fwd(c_q, W["g_cq"], name="rms_cq")
    nkv = _rms_fwd(c_kv, W["g_ckv"], name="rms_ckv")
    qa_raw = _mm(nq, W["w_uq_t"], dims="nt", name="proj_uq")
    kv = _mm(nkv, W["w_ukv_t"], dims="nt", out_dtype=BF16, name="proj_ukv")
    kr = _rope(proj, cos_k, sin_k, COL_KR, 1, BF16, name="rope_k")
    o_a, lse = _mla_fwd(qa_raw, cos_q, sin_q, kv, kr)
    o_b, o_b32, w_b, sp_b = _sb_fwd(proj, COL_SB, prefetch("ffn0", o_a))
    o_ab = jnp.concatenate([o_a, o_b], axis=1)
    h1 = _mm(o_ab, W["ev_w_out"], res=x, name="out_ev")

    def ffn_fwd(h, layer):
        W.update(get_weights(f"ffn{layer}", h))
        return _ffn_fwd(h, W["g_ffn"][layer:layer + 1], W[f"w_gate_t{layer}"], W[f"w_up_t{layer}"],
                        W[f"w_down{layer}"], name=f"ffn_fwd{layer}")

    h2, u1, a0, b0 = ffn_fwd(h1, 0)

    W.update(get_weights("mix1", h2))
    u2 = _rms_fwd(h2, W["g_mix"][1:2], name="rms_mix1")
    qkv = _mm(u2, W["od_w_qkv_t"], dims="nt", out_dtype=BF16, name="proj_qkv")
    nc = C_HEADS * C_DIM
    pad = ((PAD_KEYS, 0), (0, 0))
    k_pad, v_pad = jnp.pad(qkv[:, nc:2 * nc], pad), jnp.pad(qkv[:, 2 * nc:], pad)
    o_c, p_c = _band_fwd(qkv, k_pad, v_pad, bias_w)
    h3 = _mm(o_c, W["od_w_out"], res=h2, name="out_od")
    h4, u3, a1, b1 = ffn_fwd(h3, 1)

    loss, dh, dhb, G["g_final"] = _loss_head(h4, W["g_final"], target)

    def ffn_bwd(dh, dhb, h, u, a, b, layer):
        du, g_gate, g_up, g_down = _ffn_bwd(dhb, u, a, b, W[f"w_gate_t{layer}"], W[f"w_up_t{layer}"],
                                            W[f"w_down{layer}"], name=f"ffn_bwd{layer}")
        tok = put_grads(f"ffn{layer}", {"w_gate_t": g_gate, "w_up_t": g_up, "w_down": g_down})
        return _rms_bwd(h, W["g_ffn"][layer:layer + 1] + tok[:1, :1], du, dres=dh, name=f"rms_ffn_bwd{layer}")

    dh3, dh3b, g_gffn1 = ffn_bwd(dh, dhb, h3, u3, a1, b1, 1)

    do_c = _mm(dh3b, W["od_w_out"], dims="nt", name="out_od_dx")
    g_od_out = _mm(o_c, dh3b, dims="tn", out_dtype=BF16, name="out_od_dw")
    dq_c, dk_p, dv_p, dbias_w = _band_bwd(qkv, k_pad, v_pad, p_c, do_c)
    dqkv = jnp.concatenate([dq_c, dk_p[PAD_KEYS:], dv_p[PAD_KEYS:]], axis=1)
    tok = put_grads("mix1", {"od_w_qkv_t": _mm(dqkv, u2, dims="tn", out_dtype=BF16, name="proj_qkv_dw"),
                             "od_w_out": g_od_out})
    ddiag = _toeplitz_bwd(dbias_w)
    n_far = BAND_W - REL_CLIP + 1
    G["od_rel_bias"] = jnp.concatenate(
        [jnp.zeros((C_HEADS, REL_CLIP - BAND_TQ + 1), F32), ddiag[:, n_far:][:, ::-1],
         jnp.sum(ddiag[:, :n_far], axis=1, keepdims=True)], axis=1)
    dh2, dh2b, g_gmix1 = _mm_rms_bwd(dqkv, W["od_w_qkv_t"], h2, W["g_mix"][1:2] + tok[:1, :1], dh3,
                                     name="proj_qkv_dx")

    dh1, dh1b, g_gffn0 = ffn_bwd(dh2, dh2b, h1, u1, a0, b0, 0)
    G["g_ffn"] = jnp.concatenate([g_gffn0, g_gffn1], axis=0)

    do_ab = _mm(dh1b, W["ev_w_out"], dims="nt", name="out_ev_dx")
    g0 = {"ev_w_out": _mm(o_ab, dh1b, dims="tn", out_dtype=BF16, name="out_ev_dw")}
    dqa_raw, dkn, dva, dkr = _mla_bwd(qa_raw, cos_q, sin_q, kv, kr, o_a, lse, do_ab, 0)
    dlat, g0["w_uq_t"], g0["w_ukv_t"], G["g_cq"], G["g_ckv"] = _latent_bwd(
        proj, nq, nkv, dqa_raw, dkn, dva, dkr, cos_k, sin_k, W["g_cq"], W["g_ckv"], W["w_uq_t"], W["w_ukv_t"])
    tok = put_grads("mix0", g0)
    dqb, dkb, dvb = _sb_bwd(proj, COL_SB, o_b32, w_b, sp_b, do_ab, MLA_HEADS // 2, tok)
    dproj = jnp.concatenate([dlat, dqb, dkb, dvb], axis=1)
    tok = put_grads("in0", {"w_in_t": _mm(dproj, u0, dims="tn", name="proj_in_dw")})
    dx, _, g_gmix0 = _mm_rms_bwd(dproj, W["w_in_t"], x, W["g_mix"][0:1] + tok[:1, :1], dh1, name="proj_in_dx")
    G["g_mix"] = jnp.concatenate([g_gmix0, g_gmix1], axis=0)
    return loss[0, 0], dx, G


_BIG = ["ev_w_in", "ev_w_uq", "ev_w_ukv", "ev_w_out", "od_w_qkv", "od_w_out", "w_gate", "w_up", "w_down"]
_COL_SHARDED = {"ev_w_in", "ev_w_uq", "ev_w_ukv", "od_w_qkv", "w_gate", "w_up"}
_SMALL = ["ev_g_cq", "ev_g_ckv", "od_rel_bias", "g_mix", "g_ffn", "g_final"]
_GROUPS = {
    "in0": ["ev_w_in"],
    "mix0": ["ev_w_uq", "ev_w_ukv", "ev_w_out"],
    "ffn0": ["w_gate0", "w_up0", "w_down0"],
    "mix1": ["od_w_qkv", "od_w_out"],
    "ffn1": ["w_gate1", "w_up1", "w_down1"],
}
_GROUP_SRC = {n + str(l): (n, l) for n in ("w_gate", "w_up", "w_down") for l in (0, 1)}
_BATCHES = {"in0": ["in0"], "layer0": ["mix0", "ffn0"], "layer1": ["mix1", "ffn1"]}
_BATCH_OF = {grp: batch for batch, grps in _BATCHES.items() for grp in grps}
_SMALL_ROWS = 8
_SMALL_COLS = 1792


def _pack_small(vals):
    flat = jnp.concatenate([v.reshape(-1).astype(F32) for v in vals])
    flat = jnp.pad(flat, (0, _SMALL_ROWS * _SMALL_COLS - flat.shape[0]))
    return flat.reshape(_SMALL_ROWS, _SMALL_COLS)


def _unpack_small(packed, like):
    flat = packed.reshape(-1)
    out, off = [], 0
    for v in like:
        out.append(flat[off:off + v.size].reshape(v.shape))
        off += v.size
    return out


def kernel(x, ev_w_in, ev_g_cq, ev_w_uq, ev_g_ckv, ev_w_ukv, ev_w_out, od_w_qkv, od_rel_bias, od_w_out, g_mix, g_ffn, w_gate, w_up, w_down, g_final, loss_target, m_ev_w_in, m_ev_g_cq, m_ev_w_uq, m_ev_g_ckv, m_ev_w_ukv, m_ev_w_out, m_od_w_qkv, m_od_rel_bias, m_od_w_out, m_g_mix, m_g_ffn, m_w_gate, m_w_up, m_w_down, m_g_final, v_ev_w_in, v_ev_g_cq, v_ev_w_uq, v_ev_g_ckv, v_ev_w_ukv, v_ev_w_out, v_od_w_qkv, v_od_rel_bias, v_od_w_out, v_g_mix, v_g_ffn, v_w_gate, v_w_up, v_w_down, v_g_final):
    args = dict(locals())
    w = {n: args[n] for n in _BIG + _SMALL}
    mom = {n: args["m_" + n] for n in _BIG + _SMALL}
    var = {n: args["v_" + n] for n in _BIG + _SMALL}

    own = {}
    for grp, names in _GROUPS.items():
        for n in names:
            base, layer = _GROUP_SRC.get(n, (n, 0))
            shard = w[base][layer:layer + 1]
            own[n] = (jnp.swapaxes(shard, 1, 2) if base in _COL_SHARDED else shard).astype(BF16)
    placed = dict(zip(own, _place_own(list(own.values()), [False] * len(own), name="gather_own")))
    handles, token = _gather2_start(
        [[own[n] for n in names] for names in _GROUPS.values()],
        [[placed[n] for n in names] for names in _GROUPS.values()], x[0, :8, :LANES], name="gather_start")
    gather = dict(zip(_GROUPS, handles))
    pass_before = {"in0": ["in0"], "mix0": ["mix0"]}
    pass_after = {"ffn0": ("mix1", "g_ffn"), "mix1": ("ffn1", "g_mix")}

    def prefetch(grp, after):
        gather[grp], tok = _gather2_pass_on(gather[grp], after, name="gather_pass_" + grp)
        return tok

    def get_weights(grp, after):
        names = _GROUPS[grp]
        after = token if after is None else after
        for g in pass_before.get(grp, []):
            gather[g], _ = _gather2_pass_on(gather[g], after, name="gather_pass_" + g)
        lands = _gather2_wait(gather[grp], after, name="gather_wait_" + grp)
        full = {n: l.reshape(-1, l.shape[-1]) for n, l in zip(names, lands)}
        out = {}
        if grp in pass_after:
            g, gain = pass_after[grp]
            gather[g], tok = _gather2_pass_on(gather[g], lands[0], name="gather_pass_" + g)
            out[gain] = small[gain] + tok[:1, :1]
        if grp == "in0":
            out.update({"w_in_t": _w_in_local(full["ev_w_in"])})
        elif grp == "mix0":
            out.update({"w_uq_t": _w_uq_local(full["ev_w_uq"]), "w_ukv_t": _w_ukv_local(full["ev_w_ukv"]),
                        "ev_w_out": full["ev_w_out"]})
        elif grp == "mix1":
            out.update({"od_w_qkv_t": full["od_w_qkv"], "od_w_out": full["od_w_out"]})
        else:
            layer = grp[-1]
            out.update({"w_gate_t" + layer: full["w_gate" + layer], "w_up_t" + layer: full["w_up" + layer],
                        "w_down" + layer: full["w_down" + layer]})
        return out

    scatter, pending = {}, {}

    def put_grads(grp, g):
        if grp == "in0":
            g = {"ev_w_in": _w_in_grad(g["w_in_t"])}
        elif grp == "mix0":
            g = {"ev_w_uq": _w_uq_grad(g["w_uq_t"]), "ev_w_ukv": _w_ukv_grad(g["w_ukv_t"]),
                 "ev_w_out": g["ev_w_out"]}
        elif grp == "mix1":
            g = {"od_w_qkv": g["od_w_qkv_t"], "od_w_out": g["od_w_out"]}
        else:
            layer = grp[-1]
            g = {"w_gate" + layer: g["w_gate_t"], "w_up" + layer: g["w_up_t"], "w_down" + layer: g["w_down"]}
        pending.update({n: v.reshape(N_DEV, 1, v.shape[0] // N_DEV, v.shape[1]).astype(BF16) for n, v in g.items()})
        batch = _BATCH_OF[grp]
        names = [n for gr in _BATCHES[batch] for n in _GROUPS[gr]]
        if batch == "in0" or not all(n in pending for n in names):
            return jnp.zeros((8, LANES), F32)
        send = [pending[n] for n in names]
        scatter[batch], tok = _exchange_start(send, [True] * len(names), send[0], name="scatter_start_" + batch)
        return tok

    small = {"g_cq": ev_g_cq, "g_ckv": ev_g_ckv, "od_rel_bias": od_rel_bias[0],
             "g_mix": g_mix + token[0, 0], "g_ffn": g_ffn, "g_final": g_final.reshape(1, -1)}
    loss_part, dx, G = _local_step(x[0], loss_target[0], small, get_weights, put_grads, prefetch)
    g_small = _pack_small([G["g_cq"], G["g_ckv"], G["od_rel_bias"], G["g_mix"], G["g_ffn"], G["g_final"],
                           loss_part.reshape(1)])
    scatter["in0"], _ = _exchange_start([pending["ev_w_in"], g_small], [True, False], dx, name="scatter_start_in0")

    grads, deltas, new_m, new_v = {}, {}, {}, {}
    parts, after = {}, dx

    def wait_parts(batch, after):
        lands = _exchange_wait(scatter[batch], after, name="scatter_wait_" + batch)
        parts.update(zip([n for grp in _BATCHES[batch] for n in _GROUPS[grp]], lands))
        return lands[0]

    def adamw(n):
        col = n in _COL_SHARDED
        rows = lambda a: (jnp.swapaxes(a, 1, 2) if col else a).reshape(-1, a.shape[1 if col else 2])
        layers = [parts[n]] if n in parts else [parts[n + "0"], parts[n + "1"]]
        res = _adamw(rows(w[n]), [p.reshape(N_DEV, -1, p.shape[-1]) for p in layers], rows(mom[n]), rows(var[n]),
                     name="adamw_" + n)
        L, a1, a2 = w[n].shape
        back = lambda r: jnp.swapaxes(r.reshape(L, a2, a1), 1, 2) if col else r.reshape(L, a1, a2)
        grads[n], deltas[n], new_m[n], new_v[n] = [back(r) for r in res]
        return res[0]

    for batch in ("layer1", "layer0"):
        after = wait_parts(batch, after)
    parts["ev_w_in"], small_parts = _exchange_wait(scatter["in0"], tuple(adamw(n) for n in _BIG[1:]),
                                                   name="scatter_wait_in0")
    adamw("ev_w_in")
    small_w = [w[n] for n in _SMALL]
    loss = jnp.sum(small_parts.reshape(N_DEV, -1)[:, sum(v.size for v in small_w)])
    res = _adamw(_pack_small(small_w), [small_parts], _pack_small([mom[n] for n in _SMALL]),
                 _pack_small([var[n] for n in _SMALL]), name="adamw_small")
    for d, packed in zip((grads, deltas, new_m, new_v), res):
        for n, val in zip(_SMALL, _unpack_small(packed, small_w)):
            d[n] = val

    order = ["ev_w_in", "ev_g_cq", "ev_w_uq", "ev_g_ckv", "ev_w_ukv", "ev_w_out", "od_w_qkv", "od_rel_bias",
             "od_w_out", "g_mix", "g_ffn", "w_gate", "w_up", "w_down", "g_final"]
    out = [loss, dx[None]]
    for d in (grads, deltas, new_m, new_v):
        out += [d[n] for n in order]
    return tuple(out)
```

```python
import functools

import numpy as np
import jax
import jax.numpy as jnp
from jax import lax
from jax.experimental import pallas as pl
from jax.experimental.pallas import tpu as pltpu

F32 = jnp.float32
BF16 = jnp.bfloat16

D_MODEL = 1024
CHUNK = 64
MLA_HEADS = 8
MLA_NOPE = 64
MLA_ROPE = 32
MLA_V = 64
Q_LORA = 384
KV_LORA = 256
ROPE_THETA = 10000.0
SB_HEADS = 8
SB_DIM = 64
C_HEADS = 16
C_DIM = 64
LEFT_CHUNKS = 8
REL_CLIP = 256
D_FF = 2816
RMS_EPS = 1e-6
ADAM_LR = 0.001
ADAM_B1 = 0.9
ADAM_B2 = 0.999
ADAM_EPS = 1e-08
ADAM_WD = 0.01
ADAM_STEP = 10

N_DEV = 8
LANES = 128
HEAD = 64
assert HEAD == MLA_NOPE == MLA_V == SB_DIM == C_DIM and 2 * HEAD == LANES
CHUNK_BITS = CHUNK.bit_length() - 1
assert 1 << CHUNK_BITS == CHUNK
VMEM_LIMIT = 56 * 1024 * 1024
NEG = -1e30
PAD_KEYS = LEFT_CHUNKS * CHUNK
BAND_TQ = 128
BAND_W = BAND_TQ + PAD_KEYS
TOEP_W = BAND_W + BAND_TQ

NN = (((1,), (0,)), ((), ()))
NT = (((1,), (1,)), ((), ()))
TN = (((0,), (0,)), ((), ()))


def _dot(a, b, dn):
    return lax.dot_general(a, b, dn, preferred_element_type=F32)


def _pick(dim, pref):
    if dim <= pref:
        return dim
    best = None
    for t in range(LANES, pref + 1, LANES):
        if dim % t == 0:
            best = t
    assert best is not None, (dim, pref)
    return best


def _params(sem):
    return pltpu.CompilerParams(dimension_semantics=sem, vmem_limit_bytes=VMEM_LIMIT)


def _mm(a, b, dims="nn", res=None, out_dtype=F32, name="mm"):
    if dims == "nn":
        (M, K), (K2, N) = a.shape, b.shape
    elif dims == "nt":
        (M, K), (N, K2) = a.shape, b.shape
    else:
        (K, M), (K2, N) = a.shape, b.shape
    assert K == K2, (a.shape, b.shape, dims)
    tm, tn, tk = _pick(M, 1024), _pick(N, 1152), _pick(K, 1024)
    nk = K // tk
    dn = {"nn": NN, "nt": NT, "tn": TN}[dims]
    has_res = res is not None

    def body(*refs):
        if has_res:
            a_ref, b_ref, r_ref, o_ref, acc = refs
        else:
            a_ref, b_ref, o_ref, acc = refs
        k = pl.program_id(2)

        @pl.when(k == 0)
        def _():
            acc[...] = jnp.zeros_like(acc)

        acc[...] += _dot(a_ref[...].astype(BF16), b_ref[...].astype(BF16), dn)

        @pl.when(k == nk - 1)
        def _():
            r = acc[...]
            if has_res:
                r = r + r_ref[...]
            o_ref[...] = r.astype(out_dtype)

    a_spec = (pl.BlockSpec((tk, tm), lambda i, j, k: (k, i)) if dims == "tn"
              else pl.BlockSpec((tm, tk), lambda i, j, k: (i, k)))
    b_spec = (pl.BlockSpec((tn, tk), lambda i, j, k: (j, k)) if dims == "nt"
              else pl.BlockSpec((tk, tn), lambda i, j, k: (k, j)))
    o_spec = pl.BlockSpec((tm, tn), lambda i, j, k: (i, j))
    in_specs = [a_spec, b_spec] + ([o_spec] if has_res else [])
    args = (a, b) + ((res,) if has_res else ())
    return pl.pallas_call(
        body, name=name, grid=(M // tm, N // tn, nk),
        in_specs=in_specs, out_specs=o_spec,
        out_shape=jax.ShapeDtypeStruct((M, N), out_dtype),
        scratch_shapes=[pltpu.VMEM((tm, tn), F32)],
        compiler_params=_params(("parallel", "parallel", "arbitrary")),
    )(*args)


def _rms_fwd(x, g, out_dtype=BF16, name="rms_fwd"):
    T, Fd = x.shape
    tm = _pick(T, 256)

    def body(x_ref, g_ref, o_ref):
        xv = x_ref[...]
        r = lax.rsqrt(jnp.mean(xv * xv, axis=-1, keepdims=True) + RMS_EPS)
        o_ref[...] = (xv * r * g_ref[...]).astype(out_dtype)

    return pl.pallas_call(
        body, name=name, grid=(T // tm,),
        in_specs=[pl.BlockSpec((tm, Fd), lambda i: (i, 0)), pl.BlockSpec((1, Fd), lambda i: (0, 0))],
        out_specs=pl.BlockSpec((tm, Fd), lambda i: (i, 0)),
        out_shape=jax.ShapeDtypeStruct((T, Fd), out_dtype),
        compiler_params=_params(("parallel",)),
    )(x, g)


def _rms_bwd(x, g, dy, dres=None, name="rms_bwd"):
    T, Fd = x.shape
    tm = _pick(T, 256)
    has_res = dres is not None

    def body(*refs):
        if has_res:
            x_ref, g_ref, dy_ref, r_ref, dx_ref, dxb_ref, dg_ref = refs
        else:
            x_ref, g_ref, dy_ref, dx_ref, dxb_ref, dg_ref = refs
        xv, dyv = x_ref[...], dy_ref[...]
        r = lax.rsqrt(jnp.mean(xv * xv, axis=-1, keepdims=True) + RMS_EPS)
        gdy = dyv * g_ref[...]
        dot = jnp.mean(xv * gdy, axis=-1, keepdims=True)
        dx = r * gdy - xv * (r * r * r * dot)
        if has_res:
            dx = dx + r_ref[...]
        dx_ref[...] = dx
        dxb_ref[...] = dx.astype(BF16)

        @pl.when(pl.program_id(0) == 0)
        def _():
            dg_ref[...] = jnp.zeros_like(dg_ref)

        dg_ref[...] += jnp.sum(dyv * xv * r, axis=0, keepdims=True)

    row = pl.BlockSpec((tm, Fd), lambda i: (i, 0))
    vec = pl.BlockSpec((1, Fd), lambda i: (0, 0))
    in_specs = [row, vec, row] + ([row] if has_res else [])
    args = (x, g, dy) + ((dres,) if has_res else ())
    return pl.pallas_call(
        body, name=name, grid=(T // tm,),
        in_specs=in_specs, out_specs=[row, row, vec],
        out_shape=[jax.ShapeDtypeStruct((T, Fd), F32), jax.ShapeDtypeStruct((T, Fd), BF16),
                   jax.ShapeDtypeStruct((1, Fd), F32)],
        compiler_params=_params(("arbitrary",)),
    )(*args)


def _mm_rms_bwd(a, b, x, g, dres, name="mm_rms_bwd"):
    T, K = a.shape
    Fd = b.shape[1]
    tm, tk = _pick(T, 1024), _pick(K, 1024)
    nk = K // tk

    def body(a_ref, b_ref, x_ref, g_ref, r_ref, dx_ref, dxb_ref, dg_ref, acc):
        i, k = pl.program_id(0), pl.program_id(1)

        @pl.when(k == 0)
        def _():
            acc[...] = jnp.zeros_like(acc)

        @pl.when((k == 0) & (i == 0))
        def _():
            dg_ref[...] = jnp.zeros_like(dg_ref)

        acc[...] += _dot(a_ref[...].astype(BF16), b_ref[...].astype(BF16), NN)

        @pl.when(k == nk - 1)
        def _():
            xv, dyv = x_ref[...], acc[...]
            r = lax.rsqrt(jnp.mean(xv * xv, axis=-1, keepdims=True) + RMS_EPS)
            gdy = dyv * g_ref[...]
            dot = jnp.mean(xv * gdy, axis=-1, keepdims=True)
            dx = r * gdy - xv * (r * r * r * dot) + r_ref[...]
            dx_ref[...] = dx
            dxb_ref[...] = dx.astype(BF16)
            dg_ref[...] += jnp.sum(dyv * xv * r, axis=0, keepdims=True)

    row = pl.BlockSpec((tm, Fd), lambda i, k: (i, 0))
    vec = pl.BlockSpec((1, Fd), lambda i, k: (0, 0))
    return pl.pallas_call(
        body, name=name, grid=(T // tm, nk),
        in_specs=[pl.BlockSpec((tm, tk), lambda i, k: (i, k)), pl.BlockSpec((tk, Fd), lambda i, k: (k, 0)),
                  row, vec, row],
        out_specs=[row, row, vec],
        out_shape=[jax.ShapeDtypeStruct((T, Fd), F32), jax.ShapeDtypeStruct((T, Fd), BF16),
                   jax.ShapeDtypeStruct((1, Fd), F32)],
        scratch_shapes=[pltpu.VMEM((tm, Fd), F32)],
        compiler_params=_params(("arbitrary", "arbitrary")),
    )(a, b, x, g, dres)


def _latent_bwd(proj, nq, nkv, dqa, dkn, dva, dkr, cos_k, sin_k, g_cq, g_ckv, w_uq_t, w_ukv_t, name="latent_bwd"):
    T = proj.shape[0]
    tm = _pick(T, 512)
    wl = _O2

    def rms_bwd(xv, gv, dyv):
        r = lax.rsqrt(jnp.mean(xv * xv, axis=-1, keepdims=True) + RMS_EPS)
        gdy = dyv * gv
        dot = jnp.mean(xv * gdy, axis=-1, keepdims=True)
        return r * gdy - xv * (r * r * r * dot), jnp.sum(dyv * xv * r, axis=0, keepdims=True)

    def body(p_ref, nq_ref, nkv_ref, dqa_ref, dkn_ref, dva_ref, dkr_ref, c_ref, s_ref, gq_ref, gkv_ref, wq_ref, wkv_ref,
             dlat_ref, dwq_ref, dwkv_ref, dgq_ref, dgkv_ref):
        @pl.when(pl.program_id(0) == 0)
        def _():
            for ref in (dwq_ref, dwkv_ref, dgq_ref, dgkv_ref):
                ref[...] = jnp.zeros_like(ref)

        dqv = dqa_ref[...]
        dkv = jnp.concatenate([dkn_ref[...], dva_ref[...]], axis=1)
        pv = p_ref[...]
        dc_q, dgq = rms_bwd(pv[:, :_O1], gq_ref[...], _dot(dqv, wq_ref[...], NN))
        dc_kv, dgkv = rms_bwd(pv[:, _O1:], gkv_ref[...], _dot(dkv, wkv_ref[...], NN))
        dkr_raw = _rotate(dkr_ref[...], c_ref[...], -s_ref[...])
        dlat_ref[...] = jnp.concatenate([dc_q, dc_kv, dkr_raw], axis=1).astype(BF16)
        dwq_ref[...] += _dot(dqv, nq_ref[...], TN)
        dwkv_ref[...] += _dot(dkv, nkv_ref[...], TN)
        dgq_ref[...] += dgq
        dgkv_ref[...] += dgkv

    row = lambda w: pl.BlockSpec((tm, w), lambda i: (i, 0))
    const = lambda a: pl.BlockSpec(a.shape, lambda i: (0, 0))
    outs = [jax.ShapeDtypeStruct((T, wl + LANES), BF16), jax.ShapeDtypeStruct(w_uq_t.shape, F32),
            jax.ShapeDtypeStruct(w_ukv_t.shape, F32), jax.ShapeDtypeStruct(g_cq.shape, F32),
            jax.ShapeDtypeStruct(g_ckv.shape, F32)]
    return pl.pallas_call(
        body, name=name, grid=(T // tm,),
        in_specs=[row(wl), row(_O1), row(_O2 - _O1), row(dqa.shape[1]), row(dkn.shape[1]), row(dva.shape[1]),
                  row(LANES), row(LANES), row(LANES), const(g_cq), const(g_ckv), const(w_uq_t), const(w_ukv_t)],
        out_specs=[row(wl + LANES)] + [const(o) for o in outs[1:]],
        out_shape=outs,
        compiler_params=_params(("arbitrary",)),
    )(proj, nq, nkv, dqa, dkn, dva, dkr, cos_k, sin_k, g_cq, g_ckv, w_uq_t, w_ukv_t)


def _loss_head(h, g, target, name="loss_head"):
    T, Fd = h.shape
    tm = _pick(T, 256)

    def body(h_ref, g_ref, t_ref, loss_ref, dh_ref, dhb_ref, dg_ref):
        xv = h_ref[...]
        r = lax.rsqrt(jnp.mean(xv * xv, axis=-1, keepdims=True) + RMS_EPS)
        diff = xv * r * g_ref[...] - t_ref[...]
        part = 0.5 * jnp.sum(jnp.mean(diff * diff, axis=-1, keepdims=True), axis=0, keepdims=True)
        dyv = diff * (1.0 / Fd)
        gdy = dyv * g_ref[...]
        dot = jnp.mean(xv * gdy, axis=-1, keepdims=True)
        dh = r * gdy - xv * (r * r * r * dot)
        dh_ref[...] = dh
        dhb_ref[...] = dh.astype(BF16)

        @pl.when(pl.program_id(0) == 0)
        def _():
            dg_ref[...] = jnp.zeros_like(dg_ref)
            loss_ref[...] = jnp.zeros_like(loss_ref)

        dg_ref[...] += jnp.sum(dyv * xv * r, axis=0, keepdims=True)
        loss_ref[...] += jnp.broadcast_to(part, loss_ref.shape)

    row = pl.BlockSpec((tm, Fd), lambda i: (i, 0))
    vec = pl.BlockSpec((1, Fd), lambda i: (0, 0))
    return pl.pallas_call(
        body, name=name, grid=(T // tm,),
        in_specs=[row, vec, row],
        out_specs=[pl.BlockSpec((1, LANES), lambda i: (0, 0)), row, row, vec],
        out_shape=[jax.ShapeDtypeStruct((1, LANES), F32), jax.ShapeDtypeStruct((T, Fd), F32),
                   jax.ShapeDtypeStruct((T, Fd), BF16), jax.ShapeDtypeStruct((1, Fd), F32)],
        compiler_params=_params(("arbitrary",)),
    )(h, g, target)


FFN_TF = 256


def _ffn_fwd(h, g, wg_t, wu_t, wd, name="ffn_fwd"):
    T, Dm = h.shape
    Fh = wd.shape[0]
    tm = _pick(T, 2048)
    nf = Fh // FFN_TF

    def body(h_ref, g_ref, wg_ref, wu_ref, wd_ref, o_ref, u_ref, a_ref, b_ref):
        j = pl.program_id(1)

        @pl.when(j == 0)
        def _():
            xv = h_ref[...]
            r = lax.rsqrt(jnp.mean(xv * xv, axis=-1, keepdims=True) + RMS_EPS)
            u_ref[...] = (xv * r * g_ref[...]).astype(BF16)
            o_ref[...] = xv

        u = u_ref[...]
        a = _dot(u, wg_ref[...], NT).astype(BF16)
        b = _dot(u, wu_ref[...], NT).astype(BF16)
        a_ref[...] = a
        b_ref[...] = b
        af = a.astype(F32)
        s = (af * jax.nn.sigmoid(af) * b.astype(F32)).astype(BF16)
        o_ref[...] += _dot(s, wd_ref[...], NN)

    row = pl.BlockSpec((tm, Dm), lambda i, j: (i, 0))
    wblk = pl.BlockSpec((FFN_TF, Dm), lambda i, j: (j, 0))
    ablk = pl.BlockSpec((tm, FFN_TF), lambda i, j: (i, j))
    return pl.pallas_call(
        body, name=name, grid=(T // tm, nf),
        in_specs=[pl.BlockSpec((tm, Dm), lambda i, j: (i, 0), pipeline_mode=pl.Buffered(1)),
                  pl.BlockSpec((1, Dm), lambda i, j: (0, 0)), wblk, wblk, wblk],
        out_specs=[row, row, ablk, ablk],
        out_shape=[jax.ShapeDtypeStruct((T, Dm), F32), jax.ShapeDtypeStruct((T, Dm), BF16),
                   jax.ShapeDtypeStruct((T, Fh), BF16), jax.ShapeDtypeStruct((T, Fh), BF16)],
        compiler_params=_params(("parallel", "arbitrary")),
    )(h, g, wg_t, wu_t, wd)


def _ffn_bwd(dh, u, a, b, wg_t, wu_t, wd, name="ffn_bwd"):
    T, Dm = dh.shape
    Fh = wd.shape[0]
    nf = Fh // FFN_TF
    once = pl.Buffered(1)

    def body(dh_ref, u_ref, a_ref, b_ref, wg_ref, wu_ref, wd_ref, du_ref, dwg_ref, dwu_ref, dwd_ref):
        j = pl.program_id(0)

        @pl.when(j == 0)
        def _():
            du_ref[...] = jnp.zeros_like(du_ref)

        ds = _dot(dh_ref[...], wd_ref[...], NT)
        af, bf = a_ref[...].astype(F32), b_ref[...].astype(F32)
        sig = jax.nn.sigmoid(af)
        sa = af * sig
        dwd_ref[...] = _dot((sa * bf).astype(BF16), dh_ref[...], TN).astype(BF16)
        dab = jnp.concatenate([(ds * bf * (sig * (1.0 + af * (1.0 - sig)))).astype(BF16),
                               (ds * sa).astype(BF16)], axis=1)
        dw = _dot(dab, u_ref[...], TN)
        dwg_ref[...] = dw[:FFN_TF].astype(BF16)
        dwu_ref[...] = dw[FFN_TF:].astype(BF16)
        du_ref[...] += _dot(dab, jnp.concatenate([wg_ref[...], wu_ref[...]], axis=0), NN)

    full = lambda: pl.BlockSpec((T, Dm), lambda j: (0, 0), pipeline_mode=once)
    wblk = pl.BlockSpec((FFN_TF, Dm), lambda j: (j, 0))
    ablk = pl.BlockSpec((T, FFN_TF), lambda j: (0, j))
    return pl.pallas_call(
        body, name=name, grid=(nf,),
        in_specs=[full(), full(), ablk, ablk, wblk, wblk, wblk],
        out_specs=[pl.BlockSpec((T, Dm), lambda j: (0, 0)), wblk, wblk, wblk],
        out_shape=[jax.ShapeDtypeStruct((T, Dm), F32)] + [jax.ShapeDtypeStruct((Fh, Dm), BF16)] * 3,
        compiler_params=_params(("arbitrary",)),
    )(dh, u, a, b, wg_t, wu_t, wd)


def _rope(x, cos_t, sin_t, col0, ncols, out_dtype, name="rope"):
    T = x.shape[0]
    wt = cos_t.shape[1]
    tm = _pick(T, 256)
    nb = ncols * LANES // wt
    half = MLA_ROPE // 2

    def body(x_ref, c_ref, s_ref, o_ref):
        xv = x_ref[...].astype(F32)
        lane = lax.broadcasted_iota(jnp.int32, xv.shape, 1)
        first = (lane & (MLA_ROPE - 1)) < half
        swapped = jnp.where(first, pltpu.roll(xv, wt - half, 1), pltpu.roll(xv, half, 1))
        o_ref[...] = (xv * c_ref[...] + swapped * s_ref[...]).astype(out_dtype)

    off = col0 * LANES // wt
    return pl.pallas_call(
        body, name=name, grid=(T // tm, nb),
        in_specs=[pl.BlockSpec((tm, wt), lambda i, j: (i, j + off)),
                  pl.BlockSpec((tm, wt), lambda i, j: (i, 0)),
                  pl.BlockSpec((tm, wt), lambda i, j: (i, 0))],
        out_specs=pl.BlockSpec((tm, wt), lambda i, j: (i, j)),
        out_shape=jax.ShapeDtypeStruct((T, ncols * LANES), out_dtype),
        compiler_params=_params(("parallel", "parallel")),
    )(x, cos_t, sin_t)


ATT_TQ = 512
ATT_TK = 256
MLA_TK = 512


def _mla_masks(shape):
    lane = lax.broadcasted_iota(jnp.int32, shape, 1)
    m0 = (lane < HEAD) | ((lane >= LANES) & (lane < LANES + MLA_ROPE))
    m1 = ((lane >= HEAD) & (lane < LANES)) | ((lane >= LANES + MLA_ROPE) & (lane < LANES + 2 * MLA_ROPE))
    return m0, m1


def _by_twos(n, step, carry):
    carry = lax.fori_loop(0, n // 2, lambda i, c: step(2 * i + 1, step(2 * i, c)), carry)
    return lax.fori_loop(0, n % 2, lambda _, c: step(n - 1, c), carry)


def _chunk_ok(tq, tk, d):
    row = lax.broadcasted_iota(jnp.int32, (tq, tk), 0)
    col = lax.broadcasted_iota(jnp.int32, (tq, tk), 1) + d * tk
    return jnp.concatenate([(col >> CHUNK_BITS) <= (row >> CHUNK_BITS)] * 2, axis=0)


def _rotate(x, cos_t, sin_t):
    half = MLA_ROPE // 2
    lane = lax.broadcasted_iota(jnp.int32, x.shape, 1)
    first = (lane & (MLA_ROPE - 1)) < half
    swapped = jnp.where(first, pltpu.roll(x, x.shape[1] - half, 1), pltpu.roll(x, half, 1))
    return x * cos_t + swapped * sin_t


def _mla_fwd(q, cos_q, sin_q, kv, kr, name="mla_fwd"):
    T = q.shape[0]
    tq, tk = _pick(T, ATT_TQ), _pick(T, MLA_TK)
    nd = tq // tk
    npair = MLA_HEADS // 2
    scale = (MLA_NOPE + MLA_ROPE) ** -0.5

    def body(q_ref, c_ref, s_ref, kn_ref, v_ref, kr_ref, o_ref, lse_ref):
        m_idx = pl.program_id(1)
        qv = _rotate(q_ref[...], c_ref[...], s_ref[...]).astype(BF16)
        m0, m1 = _mla_masks(qv.shape)
        qs = jnp.concatenate([jnp.where(m0, qv, 0), jnp.where(m1, qv, 0)], axis=0).astype(BF16)

        def block(kb, carry, ok):
            ks = pl.ds(pl.multiple_of(kb * tk, tk), tk)
            kcat = jnp.concatenate([kn_ref[ks, :], kr_ref[ks, :]], axis=1)
            mx, l, acc = carry
            s = _dot(qs, kcat, NT) * scale
            if ok is not None:
                s = jnp.where(ok, s, NEG)
            mn = jnp.maximum(mx, jnp.max(s, axis=-1, keepdims=True))
            alpha = jnp.exp(mx - mn)
            p = jnp.exp(s - mn)
            return (mn, alpha * l + jnp.sum(p, axis=-1, keepdims=True),
                    alpha * acc + _dot(p.astype(BF16), v_ref[ks, :], NN))

        init = (jnp.full((2 * tq, 1), NEG, F32), jnp.zeros((2 * tq, 1), F32), jnp.zeros((2 * tq, LANES), F32))
        res = init
        for d in range(nd):
            res = block(m_idx * nd + d, res, _chunk_ok(tq, tk, d))
        mx, l, acc = _by_twos(m_idx * nd, lambda kb, c: block(kb, c, None), res)
        h0 = lax.broadcasted_iota(jnp.int32, (tq, LANES), 1) < HEAD
        o_ref[...] = _two_heads(acc * (1.0 / l), h0).astype(o_ref.dtype)
        lse_ref[...] = _two_heads(jnp.broadcast_to(mx + jnp.log(l), (2 * tq, LANES)), h0)

    full = lambda col: pl.BlockSpec((T, LANES), col)
    table = pl.BlockSpec((tq, 2 * LANES), lambda p, m: (m, 0))
    return pl.pallas_call(
        body, name=name, grid=(npair, T // tq),
        in_specs=[pl.BlockSpec((tq, 2 * LANES), lambda p, m: (m, p)), table, table,
                  full(lambda p, m: (0, p)), full(lambda p, m: (0, npair + p)), full(lambda p, m: (0, 0))],
        out_specs=[pl.BlockSpec((tq, LANES), lambda p, m: (m, p)),
                   pl.BlockSpec((tq, LANES), lambda p, m: (m, p))],
        out_shape=[jax.ShapeDtypeStruct((T, npair * LANES), BF16),
                   jax.ShapeDtypeStruct((T, npair * LANES), F32)],
        compiler_params=_params(("parallel", "arbitrary")),
    )(q, cos_q, sin_q, kv, kv, kr)


def _mla_bwd(q, cos_q, sin_q, kv, kr, o, lse, do, do_col0, name="mla_bwd"):
    T = q.shape[0]
    tq, tk = _pick(T, ATT_TQ), _pick(T, MLA_TK)
    nd = tq // tk
    npair = MLA_HEADS // 2
    scale = (MLA_NOPE + MLA_ROPE) ** -0.5

    def body(q_ref, c_ref, s_ref, kn_ref, v_ref, kr_ref, o_ref, lse_ref, do_ref, dq_ref, dkn_ref, dv_ref, dkr_ref,
             dkn_acc, dv_acc):
        p_idx, m_idx = pl.program_id(0), pl.program_id(1)

        @pl.when(m_idx == 0)
        def _():
            dkn_acc[...] = jnp.zeros_like(dkn_acc)
            dv_acc[...] = jnp.zeros_like(dv_acc)

        @pl.when((m_idx == 0) & (p_idx == 0))
        def _():
            dkr_ref[...] = jnp.zeros_like(dkr_ref)

        qv = _rotate(q_ref[...], c_ref[...], s_ref[...]).astype(BF16)
        m0, m1 = _mla_masks(qv.shape)
        qs = jnp.concatenate([jnp.where(m0, qv, 0), jnp.where(m1, qv, 0)], axis=0).astype(BF16)
        dov = do_ref[...].astype(F32)
        h0 = lax.broadcasted_iota(jnp.int32, (tq, LANES), 1) < HEAD
        dos32 = jnp.concatenate([jnp.where(h0, dov, 0.0), jnp.where(h0, 0.0, dov)], axis=0)
        ov = o_ref[...].astype(F32)
        delta = jnp.sum(dos32 * jnp.concatenate([ov, ov], axis=0), axis=-1, keepdims=True)
        dos = dos32.astype(BF16)
        lsev = lse_ref[...]
        lse = jnp.concatenate([lsev[:, 0:1], lsev[:, HEAD:HEAD + 1]], axis=0)

        def block(kb, dq, ok):
            ks = pl.ds(pl.multiple_of(kb * tk, tk), tk)
            kcat = jnp.concatenate([kn_ref[ks, :], kr_ref[ks, :]], axis=1)
            vv = v_ref[ks, :]
            p = jnp.exp(_dot(qs, kcat, NT) * scale - lse)
            if ok is not None:
                p = jnp.where(ok, p, 0.0)
            ds = (p * (_dot(dos, vv, NT) - delta) * scale).astype(BF16)
            dkc = _dot(ds, qs, TN)
            dkn_acc[ks, :] += dkc[:, :LANES]
            dkr_ref[ks, :] += dkc[:, LANES:]
            dv_acc[ks, :] += _dot(p.astype(BF16), dos, TN)
            return dq + _dot(ds, kcat, NN)

        dq = jnp.zeros((2 * tq, 2 * LANES), F32)
        for d in range(nd):
            dq = block(m_idx * nd + d, dq, _chunk_ok(tq, tk, d))
        dq = _by_twos(m_idx * nd, lambda kb, c: block(kb, c, None), dq)
        dq_ref[...] = _rotate(jnp.where(m0, dq[:tq], jnp.where(m1, dq[tq:], 0.0)), c_ref[...],
                              -s_ref[...]).astype(BF16)

        @pl.when(m_idx == T // tq - 1)
        def _():
            dkn_ref[...] = dkn_acc[...].astype(BF16)
            dv_ref[...] = dv_acc[...].astype(BF16)

    full = lambda col: pl.BlockSpec((T, LANES), col)
    blk = lambda col: pl.BlockSpec((tq, LANES), col)
    table = pl.BlockSpec((tq, 2 * LANES), lambda p, m: (m, 0))
    return pl.pallas_call(
        body, name=name, grid=(npair, T // tq),
        in_specs=[pl.BlockSpec((tq, 2 * LANES), lambda p, m: (m, p)), table, table,
                  full(lambda p, m: (0, p)), full(lambda p, m: (0, npair + p)), full(lambda p, m: (0, 0)),
                  blk(lambda p, m: (m, p)), blk(lambda p, m: (m, p)),
                  blk(lambda p, m: (m, do_col0 + p))],
        out_specs=[pl.BlockSpec((tq, 2 * LANES), lambda p, m: (m, p)),
                   full(lambda p, m: (0, p)), full(lambda p, m: (0, p)), full(lambda p, m: (0, 0))],
        out_shape=[jax.ShapeDtypeStruct((T, npair * 2 * LANES), BF16),
                   jax.ShapeDtypeStruct((T, npair * LANES), BF16),
                   jax.ShapeDtypeStruct((T, npair * LANES), BF16),
                   jax.ShapeDtypeStruct((T, LANES), F32)],
        scratch_shapes=[pltpu.VMEM((T, LANES), F32)] * 2,
        compiler_params=_params(("arbitrary", "arbitrary")),
    )(q, cos_q, sin_q, kv, kv, kr, o, lse, do)


def _split_dot(x, tri):
    hi = x.astype(BF16)
    lo = (x - hi.astype(F32)).astype(BF16)
    both = _dot(jnp.concatenate([hi, lo], axis=0), tri, NN)
    return both[:x.shape[0]] + both[x.shape[0]:]


def _sb_terms(qh, kk, before):
    z = _dot(qh, kk, NT)
    sp = jnp.maximum(z, 0.0) + jnp.log(1.0 + jnp.exp(-jnp.abs(z)))
    lk = -sp if before is None else jnp.where(before, -sp, 0.0)
    return z, sp, lk


def _sb_setup(q_ref, tq, tk, scale):
    qv = (q_ref[...].astype(F32) * scale).astype(BF16)
    lane = lax.broadcasted_iota(jnp.int32, (tq, LANES), 1)
    h0 = lane < HEAD
    qs = jnp.concatenate([jnp.where(h0, qv, 0), jnp.where(h0, 0, qv)], axis=0).astype(BF16)
    row = lax.broadcasted_iota(jnp.int32, (tk, tk), 0)
    col = lax.broadcasted_iota(jnp.int32, (tk, tk), 1)
    return qs, h0, row, col


def _sb_before(tq, tk, d):
    row = lax.broadcasted_iota(jnp.int32, (tq, tk), 0)
    col = lax.broadcasted_iota(jnp.int32, (tq, tk), 1) + d * tk
    return jnp.concatenate([col < row] * 2, axis=0)


def _two_heads(x, h0):
    tq = x.shape[0] // 2
    return jnp.where(h0, x[:tq], x[tq:])


def _sb_fwd(qkv, col0, dep, name="sb_fwd"):
    T = qkv.shape[0]
    tq, tk = _pick(T, ATT_TQ), _pick(T, ATT_TK)
    nd = tq // tk
    npair = SB_HEADS // 2
    scale = SB_DIM ** -0.5

    def body(q_ref, k_ref, v_ref, dep_ref, o_ref, o32_ref, w_ref, sp_ref):
        m_idx = pl.program_id(1)
        qs, h0, row, col = _sb_setup(q_ref, tq, tk, scale)
        later = (row > col).astype(BF16)

        def block(kb, carry, before):
            ks = pl.ds(pl.multiple_of(kb * tk, tk), tk)
            c, acc = carry
            z, sp, lk = _sb_terms(qs, k_ref[ks, :].astype(BF16), before)
            w = jnp.exp((z - sp) + _split_dot(lk, later) + c)
            if before is not None:
                w = jnp.where(before, w, 0.0)
            wb = w.astype(BF16)
            w_ref[0, 0, kb] = wb
            sp_ref[0, 0, kb] = sp.astype(BF16)
            return (c + jnp.sum(lk, axis=-1, keepdims=True), acc + _dot(wb, v_ref[ks, :].astype(BF16), NN))

        init = (jnp.zeros((2 * tq, 1), F32), jnp.zeros((2 * tq, LANES), F32))
        res = init
        for d in reversed(range(nd)):
            res = block(m_idx * nd + d, res, _sb_before(tq, tk, d))
        res = _by_twos(m_idx * nd, lambda i, c: block(m_idx * nd - 1 - i, c, None), res)
        o = _two_heads(res[1], h0)
        o_ref[...] = o.astype(o_ref.dtype)
        o32_ref[...] = o

    full = lambda col: pl.BlockSpec((T, LANES), col)
    blk = pl.BlockSpec((tq, LANES), lambda p, m: (m, p))
    return pl.pallas_call(
        body, name=name, grid=(npair, T // tq),
        in_specs=[pl.BlockSpec((tq, LANES), lambda p, m: (m, col0 + p)),
                  full(lambda p, m: (0, col0 + npair + p)), full(lambda p, m: (0, col0 + 2 * npair + p)),
                  pl.BlockSpec((8, LANES), lambda p, m: (0, 0))],
        out_specs=[blk, blk] + [pl.BlockSpec((1, 1, T // tk, 2 * tq, tk), lambda p, m: (p, m, 0, 0, 0))] * 2,
        out_shape=[jax.ShapeDtypeStruct((T, npair * LANES), BF16), jax.ShapeDtypeStruct((T, npair * LANES), F32)]
        + [jax.ShapeDtypeStruct((npair, T // tq, T // tk, 2 * tq, tk), BF16)] * 2,
        compiler_params=_params(("parallel", "arbitrary")),
    )(qkv, qkv, qkv, dep)


def _sb_bwd(qkv, col0, o32, w_all, sp_all, do, do_col0, dep, name="sb_bwd"):
    T = qkv.shape[0]
    tq, tk = _pick(T, ATT_TQ), _pick(T, ATT_TK)
    nd = tq // tk
    npair = SB_HEADS // 2
    scale = SB_DIM ** -0.5

    def body(q_ref, k_ref, v_ref, o_ref, w_ref, sp_ref, do_ref, dep_ref, dq_ref, dk_ref, dv_ref, dk_acc, dv_acc):
        m_idx = pl.program_id(1)

        @pl.when(m_idx == 0)
        def _():
            dk_acc[...] = jnp.zeros_like(dk_acc)
            dv_acc[...] = jnp.zeros_like(dv_acc)

        qs, h0, row, col = _sb_setup(q_ref, tq, tk, scale)
        dov = do_ref[...].astype(F32)
        dos = jnp.concatenate([jnp.where(h0, dov, 0.0), jnp.where(h0, 0.0, dov)], axis=0).astype(BF16)
        ov = o_ref[...]
        etot = jnp.sum(dos.astype(F32) * jnp.concatenate([ov, ov], axis=0), axis=-1, keepdims=True)
        from_here = (row >= col).astype(BF16)

        def block(kb, carry, before):
            ks = pl.ds(pl.multiple_of(kb * tk, tk), tk)
            kk = k_ref[ks, :].astype(BF16)
            vv = v_ref[ks, :].astype(BF16)
            es, dqa = carry
            wb = w_ref[0, 0, kb]
            e = wb.astype(F32) * _dot(dos, vv, NT)
            prev = etot - (_split_dot(e, from_here) + es)
            sig_neg = jnp.exp(-sp_ref[0, 0, kb].astype(F32))
            dz = e * sig_neg - (1.0 - sig_neg) * prev
            if before is not None:
                dz = jnp.where(before, dz, 0.0)
            dzb = dz.astype(BF16)
            dk_acc[ks, :] += _dot(dzb, qs, TN)
            dv_acc[ks, :] += _dot(wb, dos, TN)
            return es + jnp.sum(e, axis=-1, keepdims=True), dqa + _dot(dzb, kk, NN)

        init = (jnp.zeros((2 * tq, 1), F32), jnp.zeros((2 * tq, LANES), F32))
        res = init
        for d in reversed(range(nd)):
            res = block(m_idx * nd + d, res, _sb_before(tq, tk, d))
        res = _by_twos(m_idx * nd, lambda i, c: block(m_idx * nd - 1 - i, c, None), res)
        dq_ref[...] = (_two_heads(res[1], h0) * scale).astype(BF16)

        @pl.when(m_idx == T // tq - 1)
        def _():
            dk_ref[...] = dk_acc[...].astype(BF16)
            dv_ref[...] = dv_acc[...].astype(BF16)

    full = lambda col: pl.BlockSpec((T, LANES), col)
    blk = lambda col: pl.BlockSpec((tq, LANES), col)
    return pl.pallas_call(
        body, name=name, grid=(npair, T // tq),
        in_specs=[blk(lambda p, m: (m, col0 + p)),
                  full(lambda p, m: (0, col0 + npair + p)), full(lambda p, m: (0, col0 + 2 * npair + p)),
                  blk(lambda p, m: (m, p)),
                  pl.BlockSpec((1, 1, T // tk, 2 * tq, tk), lambda p, m: (p, m, 0, 0, 0)),
                  pl.BlockSpec((1, 1, T // tk, 2 * tq, tk), lambda p, m: (p, m, 0, 0, 0)),
                  blk(lambda p, m: (m, do_col0 + p)), pl.BlockSpec((8, LANES), lambda p, m: (0, 0))],
        out_specs=[blk(lambda p, m: (m, p)), full(lambda p, m: (0, p)), full(lambda p, m: (0, p))],
        out_shape=[jax.ShapeDtypeStruct((T, npair * LANES), BF16)] * 3,
        scratch_shapes=[pltpu.VMEM((T, LANES), F32)] * 2,
        compiler_params=_params(("arbitrary", "arbitrary")),
    )(qkv, qkv, qkv, o32, w_all, sp_all, do, dep)


def _band_in_window():
    cq = lax.broadcasted_iota(jnp.int32, (BAND_TQ, BAND_W), 0) >> CHUNK_BITS
    ckp = lax.broadcasted_iota(jnp.int32, (BAND_TQ, BAND_W), 1) >> CHUNK_BITS
    return (ckp >= cq) & (ckp <= cq + LEFT_CHUNKS)


def _band_real(m_idx):
    j = lax.broadcasted_iota(jnp.int32, (BAND_TQ, BAND_W), 1)
    return j >= PAD_KEYS - m_idx * BAND_TQ


def _band_probs(qh, kw, bias, real, scale):
    s = jnp.where(real, _dot(qh, kw, NT) * scale + bias, NEG)
    e = jnp.exp(s - jnp.max(s, axis=-1, keepdims=True))
    return e * (1.0 / jnp.sum(e, axis=-1, keepdims=True))


BAND_SUB = 16


def _band_fwd(qkv, k_pad, v_pad, bias_w, name="band_fwd"):
    T = qkv.shape[0]
    npair = C_HEADS // 2
    scale = C_DIM ** -0.5
    rows = BAND_SUB * BAND_TQ

    def body(q_ref, k_ref, v_ref, b_ref, o_ref, p_ref):
        lane = lax.broadcasted_iota(jnp.int32, (BAND_TQ, LANES), 1)
        h0 = lane < HEAD
        bias = jnp.concatenate([b_ref[0], b_ref[1]], axis=0)
        for sub in range(BAND_SUB):
            m_idx = pl.program_id(1) * BAND_SUB + sub
            win = pl.ds(pl.multiple_of(m_idx * BAND_TQ, BAND_TQ), BAND_W)
            kw, vw = k_ref[win, :], v_ref[win, :]
            qv = q_ref[sub * BAND_TQ:(sub + 1) * BAND_TQ, :]
            qs = jnp.concatenate([jnp.where(h0, qv, 0), jnp.where(h0, 0, qv)], axis=0).astype(BF16)
            p = _band_probs(qs, kw, bias, jnp.concatenate([_band_real(m_idx)] * 2, axis=0), scale).astype(BF16)
            p_ref[0, sub] = p
            o = _two_heads(_dot(p, vw, NN), h0)
            o_ref[sub * BAND_TQ:(sub + 1) * BAND_TQ, :] = o.astype(o_ref.dtype)

    Tp = T + PAD_KEYS
    return pl.pallas_call(
        body, name=name, grid=(npair, T // rows),
        in_specs=[pl.BlockSpec((rows, LANES), lambda p, m: (m, p)),
                  pl.BlockSpec((Tp, LANES), lambda p, m: (0, p)),
                  pl.BlockSpec((Tp, LANES), lambda p, m: (0, p)),
                  pl.BlockSpec((2, BAND_TQ, BAND_W), lambda p, m: (p, 0, 0))],
        out_specs=[pl.BlockSpec((rows, LANES), lambda p, m: (m, p)),
                   pl.BlockSpec((1, BAND_SUB, 2 * BAND_TQ, BAND_W), lambda p, m: (p, m, 0, 0))],
        out_shape=[jax.ShapeDtypeStruct((T, npair * LANES), BF16),
                   jax.ShapeDtypeStruct((npair, T // BAND_TQ, 2 * BAND_TQ, BAND_W), BF16)],
        compiler_params=_params(("parallel", "arbitrary")),
    )(qkv, k_pad, v_pad, bias_w)


def _band_bwd(qkv, k_pad, v_pad, probs, do, name="band_bwd"):
    T = qkv.shape[0]
    npair = C_HEADS // 2
    scale = C_DIM ** -0.5

    rows = BAND_SUB * BAND_TQ

    def body(q_ref, k_ref, v_ref, p_ref, do_ref, dq_ref, dk_ref, dv_ref, db_ref, dk_acc, dv_acc):
        @pl.when(pl.program_id(1) == 0)
        def _():
            dk_acc[...] = jnp.zeros_like(dk_acc)
            dv_acc[...] = jnp.zeros_like(dv_acc)
            db_ref[...] = jnp.zeros_like(db_ref)

        lane = lax.broadcasted_iota(jnp.int32, (BAND_TQ, LANES), 1)
        h0 = lane < HEAD
        dbs = jnp.zeros((2 * BAND_TQ, BAND_W), F32)
        for sub in range(BAND_SUB):
            m_idx = pl.program_id(1) * BAND_SUB + sub
            win = pl.ds(pl.multiple_of(m_idx * BAND_TQ, BAND_TQ), BAND_W)
            kw, vw = k_ref[win, :], v_ref[win, :]
            qv = q_ref[sub * BAND_TQ:(sub + 1) * BAND_TQ, :]
            dov = do_ref[sub * BAND_TQ:(sub + 1) * BAND_TQ, :].astype(F32)
            qs = jnp.concatenate([jnp.where(h0, qv, 0), jnp.where(h0, 0, qv)], axis=0).astype(BF16)
            dos = jnp.concatenate([jnp.where(h0, dov, 0.0), jnp.where(h0, 0.0, dov)], axis=0).astype(BF16)
            pb = p_ref[0, sub]
            p = pb.astype(F32)
            dp = _dot(dos, vw, NT)
            dsb = p * (dp - jnp.sum(p * dp, axis=-1, keepdims=True))
            dbs = dbs + dsb
            dsq = (dsb * scale).astype(BF16)
            dq_ref[sub * BAND_TQ:(sub + 1) * BAND_TQ, :] = _two_heads(_dot(dsq, kw, NN), h0).astype(BF16)
            dk_acc[win, :] += _dot(dsq, qs, TN)
            dv_acc[win, :] += _dot(pb, dos, TN)
        db_ref[0] += dbs[:BAND_TQ]
        db_ref[1] += dbs[BAND_TQ:]

        @pl.when(pl.program_id(1) == T // rows - 1)
        def _():
            dk_ref[...] = dk_acc[...].astype(BF16)
            dv_ref[...] = dv_acc[...].astype(BF16)

    Tp = T + PAD_KEYS
    blk = lambda col: pl.BlockSpec((rows, LANES), col)
    full = pl.BlockSpec((Tp, LANES), lambda p, m: (0, p))
    bias = pl.BlockSpec((2, BAND_TQ, BAND_W), lambda p, m: (p, 0, 0))
    prob = pl.BlockSpec((1, BAND_SUB, 2 * BAND_TQ, BAND_W), lambda p, m: (p, m, 0, 0))
    return pl.pallas_call(
        body, name=name, grid=(npair, T // rows),
        in_specs=[blk(lambda p, m: (m, p)), full, full, prob, blk(lambda p, m: (m, p))],
        out_specs=[blk(lambda p, m: (m, p)), full, full, bias],
        out_shape=[jax.ShapeDtypeStruct((T, npair * LANES), BF16),
                   jax.ShapeDtypeStruct((Tp, npair * LANES), BF16),
                   jax.ShapeDtypeStruct((Tp, npair * LANES), BF16),
                   jax.ShapeDtypeStruct((C_HEADS, BAND_TQ, BAND_W), F32)],
        scratch_shapes=[pltpu.VMEM((Tp, LANES), F32)] * 2,
        compiler_params=_params(("arbitrary", "arbitrary")),
    )(qkv, k_pad, v_pad, probs, do)


def _skew_bits(x, left):
    w = x.shape[1]
    row = lax.broadcasted_iota(jnp.int32, x.shape, 0)
    for b in range(BAND_TQ.bit_length() - 1):
        amt = (w - (1 << b)) if left else (1 << b)
        x = jnp.where((row >> b) & 1 == 1, pltpu.roll(x, amt, 1), x)
    return x


def _toeplitz(diag, name="toeplitz"):
    H = diag.shape[0]

    def body(d_ref, o_ref):
        x = jnp.broadcast_to(d_ref[0], (BAND_TQ, TOEP_W))
        o_ref[0] = jnp.where(_band_in_window(), _skew_bits(x, left=False)[:, BAND_TQ:], NEG)

    return pl.pallas_call(
        body, name=name, grid=(H,),
        in_specs=[pl.BlockSpec((1, 1, TOEP_W), lambda h: (h, 0, 0))],
        out_specs=pl.BlockSpec((1, BAND_TQ, BAND_W), lambda h: (h, 0, 0)),
        out_shape=jax.ShapeDtypeStruct((H, BAND_TQ, BAND_W), F32),
        compiler_params=_params(("parallel",)),
    )(diag.reshape(H, 1, TOEP_W))


def _toeplitz_bwd(dbias, name="toeplitz_bwd"):
    H = dbias.shape[0]

    def body(d_ref, o_ref):
        x = jnp.concatenate([jnp.zeros((BAND_TQ, BAND_TQ), F32), d_ref[0]], axis=1)
        h = BAND_TQ // 2
        while h >= 8:
            x = x[:h] + pltpu.roll(x[h:2 * h], TOEP_W - h, 1)
            h //= 2
        o_ref[0] = jnp.sum(_skew_bits(x, left=True), axis=0, keepdims=True)

    return pl.pallas_call(
        body, name=name, grid=(H,),
        in_specs=[pl.BlockSpec((1, BAND_TQ, BAND_W), lambda h: (h, 0, 0))],
        out_specs=pl.BlockSpec((1, 1, TOEP_W), lambda h: (h, 0, 0)),
        out_shape=jax.ShapeDtypeStruct((H, 1, TOEP_W), F32),
        compiler_params=_params(("parallel",)),
    )(dbias).reshape(H, TOEP_W)


_HBM = pl.BlockSpec(memory_space=pltpu.HBM)
_SEM = pl.BlockSpec(memory_space=pltpu.SEMAPHORE)
_EFFECT = pltpu.SideEffectType.DATAFLOW_SIDE_EFFECTING


def _peers():
    x, y, c = lax.axis_index("x"), lax.axis_index("y"), lax.axis_index("c")
    out = []
    for k in range(1, N_DEV):
        peer = (1 - x if (k >> 2) & 1 else x, 1 - y if (k >> 1) & 1 else y, 1 - c if k & 1 else c)
        out.append((peer, 4 * peer[0] + 2 * peer[1] + peer[2]))
    return 4 * x + 2 * y + c, out


def _split_copies(ins, lands, scatter, send_sem, recv_sem, arriving):
    me, peers = _peers()
    out = []
    for a in range(len(ins)):
        for peer, idx in peers:
            out.append(pltpu.make_async_remote_copy(
                src_ref=ins[a].at[idx] if scatter[a] else ins[a],
                dst_ref=lands[a].at[idx if arriving else me], send_sem=send_sem, recv_sem=recv_sem,
                device_id=peer, device_id_type=pl.DeviceIdType.MESH))
    return out


def _landing_zones(arrays, scatter):
    return [lax.empty((N_DEV,) + (a.shape[1:] if s else a.shape), a.dtype) for a, s in zip(arrays, scatter)]


def _place_own(arrays, scatter, name):
    n = len(arrays)
    lands = _landing_zones(arrays, scatter)
    me = (4 * lax.axis_index("x") + 2 * lax.axis_index("y") + lax.axis_index("c")).astype(jnp.int32).reshape(1)

    def body(me_ref, *refs):
        for a in range(n):
            refs[2 * n + a][...] = refs[a][...].reshape(refs[2 * n + a].shape)

    def row_spec(shape):
        zeros = (0,) * (len(shape) - 1)
        return pl.BlockSpec((1,) + tuple(shape[1:]), lambda i, me_ref: (me_ref[0],) + zeros)

    in_specs = [row_spec(a.shape) if s else pl.BlockSpec(a.shape, lambda i, me_ref, nd=a.ndim: (0,) * nd)
                for a, s in zip(arrays, scatter)]
    return pl.pallas_call(
        body, name=name,
        out_shape=[jax.ShapeDtypeStruct(l.shape, l.dtype) for l in lands],
        grid_spec=pltpu.PrefetchScalarGridSpec(
            num_scalar_prefetch=1, grid=(1,),
            in_specs=in_specs + [pl.BlockSpec(memory_space=pl.ANY)] * n,
            out_specs=[row_spec(l.shape) for l in lands]),
        input_output_aliases={1 + n + i: i for i in range(n)},
        compiler_params=_params(("arbitrary",)),
    )(me, *arrays, *lands)


def _exchange_start_groups(groups, scatter, after, name, lands=None):
    sizes = [len(g) for g in groups]
    arrays = [a for g in groups for a in g]
    n, ng = len(arrays), len(groups)
    flags = list(scatter) if isinstance(scatter, (list, tuple)) else [scatter] * n
    if lands is None:
        lands = list(_place_own(arrays, flags, name=name.replace("_start_", "_own_")))
    else:
        lands = [l for g in lands for l in g]
    starts = np.cumsum([0] + sizes)

    def body(*refs):
        ins, lnd = refs[:n], refs[n:2 * n]
        sems = refs[2 * n + 1:2 * n + 1 + 2 * ng]
        token = refs[-1]
        for g in range(ng):
            sl = slice(starts[g], starts[g + 1])
            for cp in _split_copies(ins[sl], lnd[sl], flags[sl], sems[2 * g], sems[2 * g + 1], arriving=False):
                cp.start()
        token[...] = jnp.zeros_like(token)

    hbm = lambda a: pltpu.HBM(a.shape, a.dtype)
    out = pl.pallas_call(
        body, name=name,
        out_shape=(*[pltpu.SemaphoreType.DMA(())] * (2 * ng),
                   *[hbm(a) for a in arrays], *[hbm(a) for a in lands],
                   jax.ShapeDtypeStruct((8, LANES), F32)),
        in_specs=[_HBM] * (2 * n) + [pl.BlockSpec(memory_space=pl.ANY)],
        out_specs=(*[_SEM] * (2 * ng), *([_HBM] * (2 * n)), pl.BlockSpec(memory_space=pltpu.VMEM)),
        input_output_aliases={i: 2 * ng + i for i in range(2 * n)},
        compiler_params=pltpu.CompilerParams(has_side_effects=_EFFECT),
    )(*[pltpu.with_memory_space_constraint(a, pltpu.HBM) for a in list(arrays) + lands], after)
    ins_out, lands_out = out[2 * ng:2 * ng + n], out[2 * ng + n:2 * ng + 2 * n]
    handles = [(out[2 * g], out[2 * g + 1], list(ins_out[starts[g]:starts[g + 1]]),
                list(lands_out[starts[g]:starts[g + 1]]), tuple(flags[starts[g]:starts[g + 1]]))
               for g in range(ng)]
    return handles, out[-1]


def _exchange_start(arrays, scatter, after, name):
    handles, token = _exchange_start_groups([list(arrays)], list(scatter), after, name)
    return handles[0], token


def _exchange_wait(handle, after, name):
    send_sem, recv_sem, ins, lands, scatter = handle
    n = len(ins)
    after = after if isinstance(after, tuple) else (after,)

    def body(*refs):
        i_ref, l_ref = refs[:n], refs[n:2 * n]
        s_sem, r_sem = refs[2 * n:2 * n + 2]
        for cp in _split_copies(i_ref, l_ref, scatter, s_sem, r_sem, arriving=False):
            cp.wait_send()
        for cp in _split_copies(i_ref, l_ref, scatter, s_sem, r_sem, arriving=True):
            cp.wait_recv()

    hbm = lambda a: pltpu.HBM(a.shape, a.dtype)
    out = pl.pallas_call(
        body, name=name,
        out_shape=tuple(hbm(a) for a in ins + lands),
        in_specs=[_HBM] * (2 * n) + [_SEM, _SEM] + [pl.BlockSpec(memory_space=pl.ANY)] * len(after),
        out_specs=tuple([_HBM] * (2 * n)),
        input_output_aliases={i: i for i in range(2 * n)},
        compiler_params=pltpu.CompilerParams(has_side_effects=_EFFECT),
    )(*ins, *lands, send_sem, recv_sem, *after)
    return list(out[n:])


_SIBLING = 1
_CHIPS = (4, 2, 6)


def _peer_of(k):
    x, y, c = lax.axis_index("x"), lax.axis_index("y"), lax.axis_index("c")
    peer = (1 - x if (k >> 2) & 1 else x, 1 - y if (k >> 1) & 1 else y, 1 - c if k & 1 else c)
    return peer, 4 * peer[0] + 2 * peer[1] + peer[2]


def _rcopy(src, dst, send_sem, recv_sem, k):
    return pltpu.make_async_remote_copy(src_ref=src, dst_ref=dst, send_sem=send_sem, recv_sem=recv_sem,
                                        device_id=_peer_of(k)[0], device_id_type=pl.DeviceIdType.MESH)


def _gather2_start(groups, lands, after, name):
    sizes = [len(g) for g in groups]
    arrays = [a for g in groups for a in g]
    lands = [l for g in lands for l in g]
    n, ng = len(arrays), len(groups)
    starts = np.cumsum([0] + sizes)

    def body(*refs):
        ins, lnd = refs[:n], refs[n:2 * n]
        sems = refs[2 * n + 1:2 * n + 1 + 4 * ng]
        me, _ = _peers()
        for g in range(ng):
            send_d, recv_d, send_i, recv_i = sems[4 * g:4 * g + 4]
            for a in range(starts[g], starts[g + 1]):
                for k in _CHIPS:
                    _rcopy(ins[a], lnd[a].at[me], send_i, recv_i, k).start()
                _rcopy(ins[a], lnd[a].at[me], send_d, recv_d, _SIBLING).start()
        refs[-1][...] = jnp.zeros_like(refs[-1])

    hbm = lambda a: pltpu.HBM(a.shape, a.dtype)
    out = pl.pallas_call(
        body, name=name,
        out_shape=(*[pltpu.SemaphoreType.DMA(())] * (4 * ng), *[hbm(a) for a in arrays], *[hbm(a) for a in lands],
                   jax.ShapeDtypeStruct((8, LANES), F32)),
        in_specs=[_HBM] * (2 * n) + [pl.BlockSpec(memory_space=pl.ANY)],
        out_specs=(*[_SEM] * (4 * ng), *([_HBM] * (2 * n)), pl.BlockSpec(memory_space=pltpu.VMEM)),
        input_output_aliases={i: 4 * ng + i for i in range(2 * n)},
        compiler_params=pltpu.CompilerParams(has_side_effects=_EFFECT),
    )(*[pltpu.with_memory_space_constraint(a, pltpu.HBM) for a in arrays + lands], after)
    ins_out, lands_out = out[4 * ng:4 * ng + n], out[4 * ng + n:4 * ng + 2 * n]
    handles = [dict(sems=out[4 * g:4 * g + 4], ins=list(ins_out[starts[g]:starts[g + 1]]),
                    lands=list(lands_out[starts[g]:starts[g + 1]])) for g in range(ng)]
    return handles, out[-1]


def _gather2_pass_on(handle, after, name):
    lands, recv_i = handle["lands"], handle["sems"][3]
    n = len(lands)
    after = after if isinstance(after, tuple) else (after,)

    def body(*refs):
        lnd, r_i = refs[:n], refs[n]
        send_f, recv_f = refs[n + 1 + len(after):n + 3 + len(after)]
        for a in range(n):
            for k in _CHIPS:
                row = _peer_of(k)[1]
                _rcopy(lnd[a].at[row], lnd[a].at[row], send_f, r_i, k).wait_recv()
        for a in range(n):
            for k in _CHIPS:
                row = _peer_of(k)[1]
                _rcopy(lnd[a].at[row], lnd[a].at[row], send_f, recv_f, _SIBLING).start()
        refs[-1][...] = jnp.zeros_like(refs[-1])

    hbm = lambda a: pltpu.HBM(a.shape, a.dtype)
    out = pl.pallas_call(
        body, name=name,
        out_shape=(pltpu.SemaphoreType.DMA(()), pltpu.SemaphoreType.DMA(()), *[hbm(a) for a in lands],
                   jax.ShapeDtypeStruct((8, LANES), F32)),
        in_specs=[_HBM] * n + [_SEM] + [pl.BlockSpec(memory_space=pl.ANY)] * len(after),
        out_specs=(_SEM, _SEM, *([_HBM] * n), pl.BlockSpec(memory_space=pltpu.VMEM)),
        input_output_aliases={i: 2 + i for i in range(n)},
        compiler_params=pltpu.CompilerParams(has_side_effects=_EFFECT),
    )(*lands, recv_i, *after)
    return dict(handle, lands=list(out[2:2 + n]), passed=(out[0], out[1])), out[-1]


def _gather2_wait(handle, after, name):
    ins, lands = handle["ins"], handle["lands"]
    send_d, recv_d, send_i, _ = handle["sems"]
    send_f, recv_f = handle["passed"]
    n = len(ins)
    after = after if isinstance(after, tuple) else (after,)

    def body(*refs):
        i_ref, lnd = refs[:n], refs[n:2 * n]
        s_d, r_d, s_i, s_f, r_f = refs[2 * n:2 * n + 5]
        me, _ = _peers()
        sib = _peer_of(_SIBLING)[1]
        for a in range(n):
            _rcopy(i_ref[a], lnd[a].at[sib], s_d, r_d, _SIBLING).wait_send()
            _rcopy(i_ref[a], lnd[a].at[sib], s_d, r_d, _SIBLING).wait_recv()
            for k in _CHIPS:
                row = _peer_of(k)[1]
                _rcopy(i_ref[a], lnd[a].at[me], s_i, r_d, k).wait_send()
                _rcopy(lnd[a].at[row], lnd[a].at[row], s_f, r_f, _SIBLING).wait_send()
                _rcopy(lnd[a].at[row], lnd[a].at[_peer_of(k ^ _SIBLING)[1]], s_f, r_f, _SIBLING).wait_recv()

    hbm = lambda a: pltpu.HBM(a.shape, a.dtype)
    out = pl.pallas_call(
        body, name=name,
        out_shape=tuple(hbm(a) for a in ins + lands),
        in_specs=[_HBM] * (2 * n) + [_SEM] * 5 + [pl.BlockSpec(memory_space=pl.ANY)] * len(after),
        out_specs=tuple([_HBM] * (2 * n)),
        input_output_aliases={i: i for i in range(2 * n)},
        compiler_params=pltpu.CompilerParams(has_side_effects=_EFFECT),
    )(*ins, *lands, send_d, recv_d, send_i, send_f, recv_f, *after)
    return list(out[n:])


def _adamw(w, parts, m, v, name="adamw"):
    R, C = w.shape
    L = len(parts)
    rl = R // L
    tr = max([t for t in range(16, 257, 16) if rl % t == 0], default=rl)
    nb = rl // tr
    c1 = 1.0 - ADAM_B1 ** ADAM_STEP
    c2 = 1.0 - ADAM_B2 ** ADAM_STEP

    def body(*refs):
        w_ref, p_refs, (m_ref, v_ref, g_ref, d_ref, nm_ref, nv_ref) = refs[0], refs[1:1 + L], refs[1 + L:]
        g = None
        for j, p_ref in enumerate(p_refs):
            gj = p_ref[0].astype(F32)
            for i in range(1, N_DEV):
                gj = gj + p_ref[i].astype(F32)
            g = gj if g is None else jnp.where(pl.program_id(0) == j, gj, g)
        nm = ADAM_B1 * m_ref[...] + (1.0 - ADAM_B1) * g
        nv = ADAM_B2 * v_ref[...] + (1.0 - ADAM_B2) * (g * g)
        g_ref[...] = g
        nm_ref[...] = nm
        nv_ref[...] = nv
        d_ref[...] = -ADAM_LR * ((nm / c1) / (jnp.sqrt(nv / c2) + ADAM_EPS) + ADAM_WD * w_ref[...])

    blk = pl.BlockSpec((tr, C), lambda l, i: (l * nb + i, 0))
    part = lambda j: pl.BlockSpec((N_DEV, tr, C), lambda l, i: (0, jnp.where(l == j, i, 0), 0))
    return pl.pallas_call(
        body, name=name, grid=(L, nb),
        in_specs=[blk] + [part(j) for j in range(L)] + [blk, blk],
        out_specs=[blk] * 4,
        out_shape=[jax.ShapeDtypeStruct((R, C), F32)] * 4,
        compiler_params=_params(("arbitrary", "arbitrary")),
    )(w, *parts, m, v)


_O1 = Q_LORA
_O2 = _O1 + KV_LORA
_O3 = _O2 + MLA_ROPE
_NB = SB_HEADS * SB_DIM
IN_W = _O2 + LANES + 3 * _NB
COL_KR = _O2 // LANES
COL_SB = COL_KR + 1


def _w_in_local(w):
    kr = w[_O2:_O3]
    pad = jnp.zeros((LANES - 2 * MLA_ROPE, w.shape[1]), w.dtype)
    return jnp.concatenate([w[:_O2], kr, kr, pad, w[_O3:]], axis=0)


def _w_in_grad(g):
    kr = (g[_O2:_O2 + MLA_ROPE].astype(F32) + g[_O2 + MLA_ROPE:_O2 + 2 * MLA_ROPE].astype(F32)).astype(g.dtype)
    return jnp.concatenate([g[:_O2], kr, g[_O2 + LANES:]], axis=0)


def _w_uq_local(w):
    w3 = w.reshape(MLA_HEADS // 2, 2, MLA_NOPE + MLA_ROPE, w.shape[1])
    nope = w3[:, :, :MLA_NOPE].reshape(MLA_HEADS // 2, 2 * MLA_NOPE, w.shape[1])
    rope = w3[:, :, MLA_NOPE:].reshape(MLA_HEADS // 2, 2 * MLA_ROPE, w.shape[1])
    pad = jnp.zeros((MLA_HEADS // 2, LANES - 2 * MLA_ROPE, w.shape[1]), w.dtype)
    return jnp.concatenate([nope, rope, pad], axis=1).reshape(-1, w.shape[1])


def _w_uq_grad(g):
    g3 = g.reshape(MLA_HEADS // 2, 2 * LANES, g.shape[1])
    nope = g3[:, :2 * MLA_NOPE].reshape(MLA_HEADS // 2, 2, MLA_NOPE, g.shape[1])
    rope = g3[:, LANES:LANES + 2 * MLA_ROPE].reshape(MLA_HEADS // 2, 2, MLA_ROPE, g.shape[1])
    return jnp.concatenate([nope, rope], axis=2).reshape(-1, g.shape[1])


def _w_ukv_local(w):
    w3 = w.reshape(MLA_HEADS, MLA_NOPE + MLA_V, w.shape[1])
    return jnp.concatenate([w3[:, :MLA_NOPE].reshape(-1, w.shape[1]),
                            w3[:, MLA_NOPE:].reshape(-1, w.shape[1])], axis=0)


def _w_ukv_grad(g):
    half = MLA_HEADS * MLA_NOPE
    kn = g[:half].reshape(MLA_HEADS, MLA_NOPE, g.shape[1])
    vv = g[half:].reshape(MLA_HEADS, MLA_V, g.shape[1])
    return jnp.concatenate([kn, vv], axis=1).reshape(-1, g.shape[1])


def _rope_tables(T):
    pos = jnp.arange(T, dtype=F32)
    inv_freq = ROPE_THETA ** (-jnp.arange(0, MLA_ROPE, 2, dtype=F32) / MLA_ROPE)
    ang = pos[:, None] * inv_freq[None, :]
    cos, sin = jnp.cos(ang), jnp.sin(ang)
    ones = jnp.ones((T, LANES - 2 * MLA_ROPE), F32)
    cos_k = jnp.concatenate([cos, cos, cos, cos, ones], axis=1)
    sin_k = jnp.concatenate([-sin, sin, -sin, sin, 0.0 * ones], axis=1)
    cos_q = jnp.concatenate([jnp.ones((T, LANES), F32), cos_k], axis=1)
    sin_q = jnp.concatenate([jnp.zeros((T, LANES), F32), sin_k], axis=1)
    return cos_q, sin_q, cos_k, sin_k


def _bias_diag_index():
    ell = np.arange(TOEP_W)
    return np.clip(BAND_W - ell, -REL_CLIP, REL_CLIP) + REL_CLIP


def _local_step(x, target, small, get_weights, put_grads, prefetch):
    T = x.shape[0]
    cos_q, sin_q, cos_k, sin_k = _rope_tables(T)
    G = {}
    W = dict(small)

    u0 = _rms_fwd(x, W["g_mix"][0:1], name="rms_mix0")
    bias_w = _toeplitz(W["od_rel_bias"][:, _bias_diag_index()])
    W.update(get_weights("in0", (u0, bias_w)))
    proj = _mm(u0, W["w_in_t"], dims="nt", name="proj_in")
    W.update(get_weights("mix0", proj))
    c_q, c_kv = proj[:, :_O1], proj[:, _O1:_O2]
    nq = _rms_fwd(c_q, W["g_cq"], name="rms_cq")
    nkv = _rms_fwd(c_kv, W["g_ckv"], name="rms_ckv")
    qa_raw = _mm(nq, W["w_uq_t"], dims="nt", name="proj_uq")
    kv = _mm(nkv, W["w_ukv_t"], dims="nt", out_dtype=BF16, name="proj_ukv")
    kr = _rope(proj, cos_k, sin_k, COL_KR, 1, BF16, name="rope_k")
    o_a, lse = _mla_fwd(qa_raw, cos_q, sin_q, kv, kr)
    o_b, o_b32, w_b, sp_b = _sb_fwd(proj, COL_SB, prefetch("ffn0", o_a))
    o_ab = jnp.concatenate([o_a, o_b], axis=1)
    h1 = _mm(o_ab, W["ev_w_out"], res=x, name="out_ev")

    def ffn_fwd(h, layer):
        W.update(get_weights(f"ffn{layer}", h))
        return _ffn_fwd(h, W["g_ffn"][layer:layer + 1], W[f"w_gate_t{layer}"], W[f"w_up_t{layer}"],
                        W[f"w_down{layer}"], name=f"ffn_fwd{layer}")

    h2, u1, a0, b0 = ffn_fwd(h1, 0)

    W.update(get_weights("mix1", h2))
    u2 = _rms_fwd(h2, W["g_mix"][1:2], name="rms_mix1")
    qkv = _mm(u2, W["od_w_qkv_t"], dims="nt", out_dtype=BF16, name="proj_qkv")
    nc = C_HEADS * C_DIM
    pad = ((PAD_KEYS, 0), (0, 0))
    k_pad, v_pad = jnp.pad(qkv[:, nc:2 * nc], pad), jnp.pad(qkv[:, 2 * nc:], pad)
    o_c, p_c = _band_fwd(qkv, k_pad, v_pad, bias_w)
    h3 = _mm(o_c, W["od_w_out"], res=h2, name="out_od")
    h4, u3, a1, b1 = ffn_fwd(h3, 1)

    loss, dh, dhb, G["g_final"] = _loss_head(h4, W["g_final"], target)

    def ffn_bwd(dh, dhb, h, u, a, b, layer):
        du, g_gate, g_up, g_down = _ffn_bwd(dhb, u, a, b, W[f"w_gate_t{layer}"], W[f"w_up_t{layer}"],
                                            W[f"w_down{layer}"], name=f"ffn_bwd{layer}")
        tok = put_grads(f"ffn{layer}", {"w_gate_t": g_gate, "w_up_t": g_up, "w_down": g_down})
        return _rms_bwd(h, W["g_ffn"][layer:layer + 1] + tok[:1, :1], du, dres=dh, name=f"rms_ffn_bwd{layer}")

    dh3, dh3b, g_gffn1 = ffn_bwd(dh, dhb, h3, u3, a1, b1, 1)

    do_c = _mm(dh3b, W["od_w_out"], dims="nt", name="out_od_dx")
    g_od_out = _mm(o_c, dh3b, dims="tn", out_dtype=BF16, name="out_od_dw")
    dq_c, dk_p, dv_p, dbias_w = _band_bwd(qkv, k_pad, v_pad, p_c, do_c)
    dqkv = jnp.concatenate([dq_c, dk_p[PAD_KEYS:], dv_p[PAD_KEYS:]], axis=1)
    tok = put_grads("mix1", {"od_w_qkv_t": _mm(dqkv, u2, dims="tn", out_dtype=BF16, name="proj_qkv_dw"),
                             "od_w_out": g_od_out})
    ddiag = _toeplitz_bwd(dbias_w)
    n_far = BAND_W - REL_CLIP + 1
    G["od_rel_bias"] = jnp.concatenate(
        [jnp.zeros((C_HEADS, REL_CLIP - BAND_TQ + 1), F32), ddiag[:, n_far:][:, ::-1],
         jnp.sum(ddiag[:, :n_far], axis=1, keepdims=True)], axis=1)
    dh2, dh2b, g_gmix1 = _mm_rms_bwd(dqkv, W["od_w_qkv_t"], h2, W["g_mix"][1:2] + tok[:1, :1], dh3,
                                     name="proj_qkv_dx")

    dh1, dh1b, g_gffn0 = ffn_bwd(dh2, dh2b, h1, u1, a0, b0, 0)
    G["g_ffn"] = jnp.concatenate([g_gffn0, g_gffn1], axis=0)

    do_ab = _mm(dh1b, W["ev_w_out"], dims="nt", name="out_ev_dx")
    g0 = {"ev_w_out": _mm(o_ab, dh1b, dims="tn", out_dtype=BF16, name="out_ev_dw")}
    dqa_raw, dkn, dva, dkr = _mla_bwd(qa_raw, cos_q, sin_q, kv, kr, o_a, lse, do_ab, 0)
    dlat, g0["w_uq_t"], g0["w_ukv_t"], G["g_cq"], G["g_ckv"] = _latent_bwd(
        proj, nq, nkv, dqa_raw, dkn, dva, dkr, cos_k, sin_k, W["g_cq"], W["g_ckv"], W["w_uq_t"], W["w_ukv_t"])
    tok = put_grads("mix0", g0)
    dqb, dkb, dvb = _sb_bwd(proj, COL_SB, o_b32, w_b, sp_b, do_ab, MLA_HEADS // 2, tok)
    dproj = jnp.concatenate([dlat, dqb, dkb, dvb], axis=1)
    tok = put_grads("in0", {"w_in_t": _mm(dproj, u0, dims="tn", name="proj_in_dw")})
    dx, _, g_gmix0 = _mm_rms_bwd(dproj, W["w_in_t"], x, W["g_mix"][0:1] + tok[:1, :1], dh1, name="proj_in_dx")
    G["g_mix"] = jnp.concatenate([g_gmix0, g_gmix1], axis=0)
    return loss[0, 0], dx, G


_BIG = ["ev_w_in", "ev_w_uq", "ev_w_ukv", "ev_w_out", "od_w_qkv", "od_w_out", "w_gate", "w_up", "w_down"]
_COL_SHARDED = {"ev_w_in", "ev_w_uq", "ev_w_ukv", "od_w_qkv", "w_gate", "w_up"}
_SMALL = ["ev_g_cq", "ev_g_ckv", "od_rel_bias", "g_mix", "g_ffn", "g_final"]
_GROUPS = {
    "in0": ["ev_w_in"],
    "mix0": ["ev_w_uq", "ev_w_ukv", "ev_w_out"],
    "ffn0": ["w_gate0", "w_up0", "w_down0"],
    "mix1": ["od_w_qkv", "od_w_out"],
    "ffn1": ["w_gate1", "w_up1", "w_down1"],
}
_GROUP_SRC = {n + str(l): (n, l) for n in ("w_gate", "w_up", "w_down") for l in (0, 1)}
_BATCHES = {"in0": ["in0"], "layer0": ["mix0", "ffn0"], "layer1": ["mix1", "ffn1"]}
_BATCH_OF = {grp: batch for batch, grps in _BATCHES.items() for grp in grps}
_SMALL_ROWS = 8
_SMALL_COLS = 1792


def _pack_small(vals):
    flat = jnp.concatenate([v.reshape(-1).astype(F32) for v in vals])
    flat = jnp.pad(flat, (0, _SMALL_ROWS * _SMALL_COLS - flat.shape[0]))
    return flat.reshape(_SMALL_ROWS, _SMALL_COLS)


def _unpack_small(packed, like):
    flat = packed.reshape(-1)
    out, off = [], 0
    for v in like:
        out.append(flat[off:off + v.size].reshape(v.shape))
        off += v.size
    return out


def kernel(x, ev_w_in, ev_g_cq, ev_w_uq, ev_g_ckv, ev_w_ukv, ev_w_out, od_w_qkv, od_rel_bias, od_w_out, g_mix, g_ffn, w_gate, w_up, w_down, g_final, loss_target, m_ev_w_in, m_ev_g_cq, m_ev_w_uq, m_ev_g_ckv, m_ev_w_ukv, m_ev_w_out, m_od_w_qkv, m_od_rel_bias, m_od_w_out, m_g_mix, m_g_ffn, m_w_gate, m_w_up, m_w_down, m_g_final, v_ev_w_in, v_ev_g_cq, v_ev_w_uq, v_ev_g_ckv, v_ev_w_ukv, v_ev_w_out, v_od_w_qkv, v_od_rel_bias, v_od_w_out, v_g_mix, v_g_ffn, v_w_gate, v_w_up, v_w_down, v_g_final):
    args = dict(locals())
    w = {n: args[n] for n in _BIG + _SMALL}
    mom = {n: args["m_" + n] for n in _BIG + _SMALL}
    var = {n: args["v_" + n] for n in _BIG + _SMALL}

    own = {}
    for grp, names in _GROUPS.items():
        for n in names:
            base, layer = _GROUP_SRC.get(n, (n, 0))
            shard = w[base][layer:layer + 1]
            own[n] = (jnp.swapaxes(shard, 1, 2) if base in _COL_SHARDED else shard).astype(BF16)
    placed = dict(zip(own, _place_own(list(own.values()), [False] * len(own), name="gather_own")))
    handles, token = _gather2_start(
        [[own[n] for n in names] for names in _GROUPS.values()],
        [[placed[n] for n in names] for names in _GROUPS.values()], x[0, :8, :LANES], name="gather_start")
    gather = dict(zip(_GROUPS, handles))
    pass_before = {"in0": ["in0"], "mix0": ["mix0"]}
    pass_after = {"ffn0": ("mix1", "g_ffn"), "mix1": ("ffn1", "g_mix")}

    def prefetch(grp, after):
        gather[grp], tok = _gather2_pass_on(gather[grp], after, name="gather_pass_" + grp)
        return tok

    def get_weights(grp, after):
        names = _GROUPS[grp]
        after = token if after is None else after
        for g in pass_before.get(grp, []):
            gather[g], _ = _gather2_pass_on(gather[g], after, name="gather_pass_" + g)
        lands = _gather2_wait(gather[grp], after, name="gather_wait_" + grp)
        full = {n: l.reshape(-1, l.shape[-1]) for n, l in zip(names, lands)}
        out = {}
        if grp in pass_after:
            g, gain = pass_after[grp]
            gather[g], tok = _gather2_pass_on(gather[g], lands[0], name="gather_pass_" + g)
            out[gain] = small[gain] + tok[:1, :1]
        if grp == "in0":
            out.update({"w_in_t": _w_in_local(full["ev_w_in"])})
        elif grp == "mix0":
            out.update({"w_uq_t": _w_uq_local(full["ev_w_uq"]), "w_ukv_t": _w_ukv_local(full["ev_w_ukv"]),
                        "ev_w_out": full["ev_w_out"]})
        elif grp == "mix1":
            out.update({"od_w_qkv_t": full["od_w_qkv"], "od_w_out": full["od_w_out"]})
        else:
            layer = grp[-1]
            out.update({"w_gate_t" + layer: full["w_gate" + layer], "w_up_t" + layer: full["w_up" + layer],
                        "w_down" + layer: full["w_down" + layer]})
        return out

    scatter, pending = {}, {}

    def put_grads(grp, g):
        if grp == "in0":
            g = {"ev_w_in": _w_in_grad(g["w_in_t"])}
        elif grp == "mix0":
            g = {"ev_w_uq": _w_uq_grad(g["w_uq_t"]), "ev_w_ukv": _w_ukv_grad(g["w_ukv_t"]),
                 "ev_w_out": g["ev_w_out"]}
        elif grp == "mix1":
            g = {"od_w_qkv": g["od_w_qkv_t"], "od_w_out": g["od_w_out"]}
        else:
            layer = grp[-1]
            g = {"w_gate" + layer: g["w_gate_t"], "w_up" + layer: g["w_up_t"], "w_down" + layer: g["w_down"]}
        pending.update({n: v.reshape(N_DEV, 1, v.shape[0] // N_DEV, v.shape[1]).astype(BF16) for n, v in g.items()})
        batch = _BATCH_OF[grp]
        names = [n for gr in _BATCHES[batch] for n in _GROUPS[gr]]
        if batch == "in0" or not all(n in pending for n in names):
            return jnp.zeros((8, LANES), F32)
        send = [pending[n] for n in names]
        scatter[batch], tok = _exchange_start(send, [True] * len(names), send[0], name="scatter_start_" + batch)
        return tok

    small = {"g_cq": ev_g_cq, "g_ckv": ev_g_ckv, "od_rel_bias": od_rel_bias[0],
             "g_mix": g_mix + token[0, 0], "g_ffn": g_ffn, "g_final": g_final.reshape(1, -1)}
    loss_part, dx, G = _local_step(x[0], loss_target[0], small, get_weights, put_grads, prefetch)
    g_small = _pack_small([G["g_cq"], G["g_ckv"], G["od_rel_bias"], G["g_mix"], G["g_ffn"], G["g_final"],
                           loss_part.reshape(1)])
    scatter["in0"], _ = _exchange_start([pending["ev_w_in"], g_small], [True, False], dx, name="scatter_start_in0")

    grads, deltas, new_m, new_v = {}, {}, {}, {}
    parts, after = {}, dx

    def wait_parts(batch, after):
        lands = _exchange_wait(scatter[batch], after, name="scatter_wait_" + batch)
        parts.update(zip([n for grp in _BATCHES[batch] for n in _GROUPS[grp]], lands))
        return lands[0]

    def adamw(n):
        col = n in _COL_SHARDED
        rows = lambda a: (jnp.swapaxes(a, 1, 2) if col else a).reshape(-1, a.shape[1 if col else 2])
        layers = [parts[n]] if n in parts else [parts[n + "0"], parts[n + "1"]]
        res = _adamw(rows(w[n]), [p.reshape(N_DEV, -1, p.shape[-1]) for p in layers], rows(mom[n]), rows(var[n]),
                     name="adamw_" + n)
        L, a1, a2 = w[n].shape
        back = lambda r: jnp.swapaxes(r.reshape(L, a2, a1), 1, 2) if col else r.reshape(L, a1, a2)
        grads[n], deltas[n], new_m[n], new_v[n] = [back(r) for r in res]
        return res[0]

    for batch in ("layer1", "layer0"):
        after = wait_parts(batch, after)
    parts["ev_w_in"], small_parts = _exchange_wait(scatter["in0"], tuple(adamw(n) for n in _BIG[1:]),
                                                   name="scatter_wait_in0")
    adamw("ev_w_in")
    small_w = [w[n] for n in _SMALL]
    loss = jnp.sum(small_parts.reshape(N_DEV, -1)[:, sum(v.size for v in small_w)])
    res = _adamw(_pack_small(small_w), [small_parts], _pack_small([mom[n] for n in _SMALL]),
                 _pack_small([var[n] for n in _SMALL]), name="adamw_small")
    for d, packed in zip((grads, deltas, new_m, new_v), res):
        for n, val in zip(_SMALL, _unpack_small(packed, small_w)):
            d[n] = val

    order = ["ev_w_in", "ev_g_cq", "ev_w_uq", "ev_g_ckv", "ev_w_ukv", "ev_w_out", "od_w_qkv", "od_rel_bias",
             "od_w_out", "g_mix", "g_ffn", "w_gate", "w_up", "w_down", "g_final"]
    out = [loss, dx[None]]
    for d in (grads, deltas, new_m, new_v):
        out += [d[n] for n in order]
    return tuple(out)
```

```python
import functools

import numpy as np
import jax
import jax.numpy as jnp
from jax import lax
from jax.experimental import pallas as pl
from jax.experimental.pallas import tpu as pltpu

F32 = jnp.float32
BF16 = jnp.bfloat16

D_MODEL = 1024
CHUNK = 64
MLA_HEADS = 8
MLA_NOPE = 64
MLA_ROPE = 32
MLA_V = 64
Q_LORA = 384
KV_LORA = 256
ROPE_THETA = 10000.0
SB_HEADS = 8
SB_DIM = 64
C_HEADS = 16
C_DIM = 64
LEFT_CHUNKS = 8
REL_CLIP = 256
D_FF = 2816
RMS_EPS = 1e-6
ADAM_LR = 0.001
ADAM_B1 = 0.9
ADAM_B2 = 0.999
ADAM_EPS = 1e-08
ADAM_WD = 0.01
ADAM_STEP = 10

N_DEV = 8
LANES = 128
HEAD = 64
assert HEAD == MLA_NOPE == MLA_V == SB_DIM == C_DIM and 2 * HEAD == LANES
CHUNK_BITS = CHUNK.bit_length() - 1
assert 1 << CHUNK_BITS == CHUNK
VMEM_LIMIT = 56 * 1024 * 1024
NEG = -1e30
PAD_KEYS = LEFT_CHUNKS * CHUNK
BAND_TQ = 128
BAND_W = BAND_TQ + PAD_KEYS
TOEP_W = BAND_W + BAND_TQ

NN = (((1,), (0,)), ((), ()))
NT = (((1,), (1,)), ((), ()))
TN = (((0,), (0,)), ((), ()))


def _dot(a, b, dn):
    return lax.dot_general(a, b, dn, preferred_element_type=F32)


def _pick(dim, pref):
    if dim <= pref:
        return dim
    best = None
    for t in range(LANES, pref + 1, LANES):
        if dim % t == 0:
            best = t
    assert best is not None, (dim, pref)
    return best


def _params(sem):
    return pltpu.CompilerParams(dimension_semantics=sem, vmem_limit_bytes=VMEM_LIMIT)


def _mm(a, b, dims="nn", res=None, out_dtype=F32, name="mm"):
    if dims == "nn":
        (M, K), (K2, N) = a.shape, b.shape
    elif dims == "nt":
        (M, K), (N, K2) = a.shape, b.shape
    else:
        (K, M), (K2, N) = a.shape, b.shape
    assert K == K2, (a.shape, b.shape, dims)
    tm, tn, tk = _pick(M, 1024), _pick(N, 1152), _pick(K, 1024)
    nk = K // tk
    dn = {"nn": NN, "nt": NT, "tn": TN}[dims]
    has_res = res is not None

    def body(*refs):
        if has_res:
            a_ref, b_ref, r_ref, o_ref, acc = refs
        else:
            a_ref, b_ref, o_ref, acc = refs
        k = pl.program_id(2)

        @pl.when(k == 0)
        def _():
            acc[...] = jnp.zeros_like(acc)

        acc[...] += _dot(a_ref[...].astype(BF16), b_ref[...].astype(BF16), dn)

        @pl.when(k == nk - 1)
        def _():
            r = acc[...]
            if has_res:
                r = r + r_ref[...]
            o_ref[...] = r.astype(out_dtype)

    a_spec = (pl.BlockSpec((tk, tm), lambda i, j, k: (k, i)) if dims == "tn"
              else pl.BlockSpec((tm, tk), lambda i, j, k: (i, k)))
    b_spec = (pl.BlockSpec((tn, tk), lambda i, j, k: (j, k)) if dims == "nt"
              else pl.BlockSpec((tk, tn), lambda i, j, k: (k, j)))
    o_spec = pl.BlockSpec((tm, tn), lambda i, j, k: (i, j))
    in_specs = [a_spec, b_spec] + ([o_spec] if has_res else [])
    args = (a, b) + ((res,) if has_res else ())
    return pl.pallas_call(
        body, name=name, grid=(M // tm, N // tn, nk),
        in_specs=in_specs, out_specs=o_spec,
        out_shape=jax.ShapeDtypeStruct((M, N), out_dtype),
        scratch_shapes=[pltpu.VMEM((tm, tn), F32)],
        compiler_params=_params(("parallel", "parallel", "arbitrary")),
    )(*args)


def _rms_fwd(x, g, out_dtype=BF16, name="rms_fwd"):
    T, Fd = x.shape
    tm = _pick(T, 256)

    def body(x_ref, g_ref, o_ref):
        xv = x_ref[...]
        r = lax.rsqrt(jnp.mean(xv * xv, axis=-1, keepdims=True) + RMS_EPS)
        o_ref[...] = (xv * r * g_ref[...]).astype(out_dtype)

    return pl.pallas_call(
        body, name=name, grid=(T // tm,),
        in_specs=[pl.BlockSpec((tm, Fd), lambda i: (i, 0)), pl.BlockSpec((1, Fd), lambda i: (0, 0))],
        out_specs=pl.BlockSpec((tm, Fd), lambda i: (i, 0)),
        out_shape=jax.ShapeDtypeStruct((T, Fd), out_dtype),
        compiler_params=_params(("parallel",)),
    )(x, g)


def _rms_bwd(x, g, dy, dres=None, name="rms_bwd"):
    T, Fd = x.shape
    tm = _pick(T, 256)
    has_res = dres is not None

    def body(*refs):
        if has_res:
            x_ref, g_ref, dy_ref, r_ref, dx_ref, dxb_ref, dg_ref = refs
        else:
            x_ref, g_ref, dy_ref, dx_ref, dxb_ref, dg_ref = refs
        xv, dyv = x_ref[...], dy_ref[...]
        r = lax.rsqrt(jnp.mean(xv * xv, axis=-1, keepdims=True) + RMS_EPS)
        gdy = dyv * g_ref[...]
        dot = jnp.mean(xv * gdy, axis=-1, keepdims=True)
        dx = r * gdy - xv * (r * r * r * dot)
        if has_res:
            dx = dx + r_ref[...]
        dx_ref[...] = dx
        dxb_ref[...] = dx.astype(BF16)

        @pl.when(pl.program_id(0) == 0)
        def _():
            dg_ref[...] = jnp.zeros_like(dg_ref)

        dg_ref[...] += jnp.sum(dyv * xv * r, axis=0, keepdims=True)

    row = pl.BlockSpec((tm, Fd), lambda i: (i, 0))
    vec = pl.BlockSpec((1, Fd), lambda i: (0, 0))
    in_specs = [row, vec, row] + ([row] if has_res else [])
    args = (x, g, dy) + ((dres,) if has_res else ())
    return pl.pallas_call(
        body, name=name, grid=(T // tm,),
        in_specs=in_specs, out_specs=[row, row, vec],
        out_shape=[jax.ShapeDtypeStruct((T, Fd), F32), jax.ShapeDtypeStruct((T, Fd), BF16),
                   jax.ShapeDtypeStruct((1, Fd), F32)],
        compiler_params=_params(("arbitrary",)),
    )(*args)


def _mm_rms_bwd(a, b, x, g, dres, name="mm_rms_bwd"):
    T, K = a.shape
    Fd = b.shape[1]
    tm, tk = _pick(T, 1024), _pick(K, 1024)
    nk = K // tk

    def body(a_ref, b_ref, x_ref, g_ref, r_ref, dx_ref, dxb_ref, dg_ref, acc):
        i, k = pl.program_id(0), pl.program_id(1)

        @pl.when(k == 0)
        def _():
            acc[...] = jnp.zeros_like(acc)

        @pl.when((k == 0) & (i == 0))
        def _():
            dg_ref[...] = jnp.zeros_like(dg_ref)

        acc[...] += _dot(a_ref[...].astype(BF16), b_ref[...].astype(BF16), NN)

        @pl.when(k == nk - 1)
        def _():
            xv, dyv = x_ref[...], acc[...]
            r = lax.rsqrt(jnp.mean(xv * xv, axis=-1, keepdims=True) + RMS_EPS)
            gdy = dyv * g_ref[...]
            dot = jnp.mean(xv * gdy, axis=-1, keepdims=True)
            dx = r * gdy - xv * (r * r * r * dot) + r_ref[...]
            dx_ref[...] = dx
            dxb_ref[...] = dx.astype(BF16)
            dg_ref[...] += jnp.sum(dyv * xv * r, axis=0, keepdims=True)

    row = pl.BlockSpec((tm, Fd), lambda i, k: (i, 0))
    vec = pl.BlockSpec((1, Fd), lambda i, k: (0, 0))
    return pl.pallas_call(
        body, name=name, grid=(T // tm, nk),
        in_specs=[pl.BlockSpec((tm, tk), lambda i, k: (i, k)), pl.BlockSpec((tk, Fd), lambda i, k: (k, 0)),
                  row, vec, row],
        out_specs=[row, row, vec],
        out_shape=[jax.ShapeDtypeStruct((T, Fd), F32), jax.ShapeDtypeStruct((T, Fd), BF16),
                   jax.ShapeDtypeStruct((1, Fd), F32)],
        scratch_shapes=[pltpu.VMEM((tm, Fd), F32)],
        compiler_params=_params(("arbitrary", "arbitrary")),
    )(a, b, x, g, dres)


def _latent_bwd(proj, nq, nkv, dqa, dkn, dva, dkr, cos_k, sin_k, g_cq, g_ckv, w_uq_t, w_ukv_t, name="latent_bwd"):
    T = proj.shape[0]
    tm = _pick(T, 512)
    wl = _O2

    def rms_bwd(xv, gv, dyv):
        r = lax.rsqrt(jnp.mean(xv * xv, axis=-1, keepdims=True) + RMS_EPS)
        gdy = dyv * gv
        dot = jnp.mean(xv * gdy, axis=-1, keepdims=True)
        return r * gdy - xv * (r * r * r * dot), jnp.sum(dyv * xv * r, axis=0, keepdims=True)

    def body(p_ref, nq_ref, nkv_ref, dqa_ref, dkn_ref, dva_ref, dkr_ref, c_ref, s_ref, gq_ref, gkv_ref, wq_ref, wkv_ref,
             dlat_ref, dwq_ref, dwkv_ref, dgq_ref, dgkv_ref):
        @pl.when(pl.program_id(0) == 0)
        def _():
            for ref in (dwq_ref, dwkv_ref, dgq_ref, dgkv_ref):
                ref[...] = jnp.zeros_like(ref)

        dqv = dqa_ref[...]
        dkv = jnp.concatenate([dkn_ref[...], dva_ref[...]], axis=1)
        pv = p_ref[...]
        dc_q, dgq = rms_bwd(pv[:, :_O1], gq_ref[...], _dot(dqv, wq_ref[...], NN))
        dc_kv, dgkv = rms_bwd(pv[:, _O1:], gkv_ref[...], _dot(dkv, wkv_ref[...], NN))
        dkr_raw = _rotate(dkr_ref[...], c_ref[...], -s_ref[...])
        dlat_ref[...] = jnp.concatenate([dc_q, dc_kv, dkr_raw], axis=1).astype(BF16)
        dwq_ref[...] += _dot(dqv, nq_ref[...], TN)
        dwkv_ref[...] += _dot(dkv, nkv_ref[...], TN)
        dgq_ref[...] += dgq
        dgkv_ref[...] += dgkv

    row = lambda w: pl.BlockSpec((tm, w), lambda i: (i, 0))
    const = lambda a: pl.BlockSpec(a.shape, lambda i: (0, 0))
    outs = [jax.ShapeDtypeStruct((T, wl + LANES), BF16), jax.ShapeDtypeStruct(w_uq_t.shape, F32),
            jax.ShapeDtypeStruct(w_ukv_t.shape, F32), jax.ShapeDtypeStruct(g_cq.shape, F32),
            jax.ShapeDtypeStruct(g_ckv.shape, F32)]
    return pl.pallas_call(
        body, name=name, grid=(T // tm,),
        in_specs=[row(wl), row(_O1), row(_O2 - _O1), row(dqa.shape[1]), row(dkn.shape[1]), row(dva.shape[1]),
                  row(LANES), row(LANES), row(LANES), const(g_cq), const(g_ckv), const(w_uq_t), const(w_ukv_t)],
        out_specs=[row(wl + LANES)] + [const(o) for o in outs[1:]],
        out_shape=outs,
        compiler_params=_params(("arbitrary",)),
    )(proj, nq, nkv, dqa, dkn, dva, dkr, cos_k, sin_k, g_cq, g_ckv, w_uq_t, w_ukv_t)


def _loss_head(h, g, target, name="loss_head"):
    T, Fd = h.shape
    tm = _pick(T, 256)

    def body(h_ref, g_ref, t_ref, loss_ref, dh_ref, dhb_ref, dg_ref):
        xv = h_ref[...]
        r = lax.rsqrt(jnp.mean(xv * xv, axis=-1, keepdims=True) + RMS_EPS)
        diff = xv * r * g_ref[...] - t_ref[...]
        part = 0.5 * jnp.sum(jnp.mean(diff * diff, axis=-1, keepdims=True), axis=0, keepdims=True)
        dyv = diff * (1.0 / Fd)
        gdy = dyv * g_ref[...]
        dot = jnp.mean(xv * gdy, axis=-1, keepdims=True)
        dh = r * gdy - xv * (r * r * r * dot)
        dh_ref[...] = dh
        dhb_ref[...] = dh.astype(BF16)

        @pl.when(pl.program_id(0) == 0)
        def _():
            dg_ref[...] = jnp.zeros_like(dg_ref)
            loss_ref[...] = jnp.zeros_like(loss_ref)

        dg_ref[...] += jnp.sum(dyv * xv * r, axis=0, keepdims=True)
        loss_ref[...] += jnp.broadcast_to(part, loss_ref.shape)

    row = pl.BlockSpec((tm, Fd), lambda i: (i, 0))
    vec = pl.BlockSpec((1, Fd), lambda i: (0, 0))
    return pl.pallas_call(
        body, name=name, grid=(T // tm,),
        in_specs=[row, vec, row],
        out_specs=[pl.BlockSpec((1, LANES), lambda i: (0, 0)), row, row, vec],
        out_shape=[jax.ShapeDtypeStruct((1, LANES), F32), jax.ShapeDtypeStruct((T, Fd), F32),
                   jax.ShapeDtypeStruct((T, Fd), BF16), jax.ShapeDtypeStruct((1, Fd), F32)],
        compiler_params=_params(("arbitrary",)),
    )(h, g, target)


FFN_TF = 256


def _ffn_fwd(h, g, wg_t, wu_t, wd, name="ffn_fwd"):
    T, Dm = h.shape
    Fh = wd.shape[0]
    tm = _pick(T, 2048)
    nf = Fh // FFN_TF

    def body(h_ref, g_ref, wg_ref, wu_ref, wd_ref, o_ref, u_ref, a_ref, b_ref):
        j = pl.program_id(1)

        @pl.when(j == 0)
        def _():
            xv = h_ref[...]
            r = lax.rsqrt(jnp.mean(xv * xv, axis=-1, keepdims=True) + RMS_EPS)
            u_ref[...] = (xv * r * g_ref[...]).astype(BF16)
            o_ref[...] = xv

        u = u_ref[...]
        a = _dot(u, wg_ref[...], NT).astype(BF16)
        b = _dot(u, wu_ref[...], NT).astype(BF16)
        a_ref[...] = a
        b_ref[...] = b
        af = a.astype(F32)
        s = (af * jax.nn.sigmoid(af) * b.astype(F32)).astype(BF16)
        o_ref[...] += _dot(s, wd_ref[...], NN)

    row = pl.BlockSpec((tm, Dm), lambda i, j: (i, 0))
    wblk = pl.BlockSpec((FFN_TF, Dm), lambda i, j: (j, 0))
    ablk = pl.BlockSpec((tm, FFN_TF), lambda i, j: (i, j))
    return pl.pallas_call(
        body, name=name, grid=(T // tm, nf),
        in_specs=[pl.BlockSpec((tm, Dm), lambda i, j: (i, 0), pipeline_mode=pl.Buffered(1)),
                  pl.BlockSpec((1, Dm), lambda i, j: (0, 0)), wblk, wblk, wblk],
        out_specs=[row, row, ablk, ablk],
        out_shape=[jax.ShapeDtypeStruct((T, Dm), F32), jax.ShapeDtypeStruct((T, Dm), BF16),
                   jax.ShapeDtypeStruct((T, Fh), BF16), jax.ShapeDtypeStruct((T, Fh), BF16)],
        compiler_params=_params(("parallel", "arbitrary")),
    )(h, g, wg_t, wu_t, wd)


def _ffn_bwd(dh, u, a, b, wg_t, wu_t, wd, name="ffn_bwd"):
    T, Dm = dh.shape
    Fh = wd.shape[0]
    nf = Fh // FFN_TF
    once = pl.Buffered(1)

    def body(dh_ref, u_ref, a_ref, b_ref, wg_ref, wu_ref, wd_ref, du_ref, dwg_ref, dwu_ref, dwd_ref):
        j = pl.program_id(0)

        @pl.when(j == 0)
        def _():
            du_ref[...] = jnp.zeros_like(du_ref)

        ds = _dot(dh_ref[...], wd_ref[...], NT)
        af, bf = a_ref[...].astype(F32), b_ref[...].astype(F32)
        sig = jax.nn.sigmoid(af)
        sa = af * sig
        dwd_ref[...] = _dot((sa * bf).astype(BF16), dh_ref[...], TN).astype(BF16)
        dab = jnp.concatenate([(ds * bf * (sig * (1.0 + af * (1.0 - sig)))).astype(BF16),
                               (ds * sa).astype(BF16)], axis=1)
        dw = _dot(dab, u_ref[...], TN)
        dwg_ref[...] = dw[:FFN_TF].astype(BF16)
        dwu_ref[...] = dw[FFN_TF:].astype(BF16)
        du_ref[...] += _dot(dab, jnp.concatenate([wg_ref[...], wu_ref[...]], axis=0), NN)

    full = lambda: pl.BlockSpec((T, Dm), lambda j: (0, 0), pipeline_mode=once)
    wblk = pl.BlockSpec((FFN_TF, Dm), lambda j: (j, 0))
    ablk = pl.BlockSpec((T, FFN_TF), lambda j: (0, j))
    return pl.pallas_call(
        body, name=name, grid=(nf,),
        in_specs=[full(), full(), ablk, ablk, wblk, wblk, wblk],
        out_specs=[pl.BlockSpec((T, Dm), lambda j: (0, 0)), wblk, wblk, wblk],
        out_shape=[jax.ShapeDtypeStruct((T, Dm), F32)] + [jax.ShapeDtypeStruct((Fh, Dm), BF16)] * 3,
        compiler_params=_params(("arbitrary",)),
    )(dh, u, a, b, wg_t, wu_t, wd)


def _rope(x, cos_t, sin_t, col0, ncols, out_dtype, name="rope"):
    T = x.shape[0]
    wt = cos_t.shape[1]
    tm = _pick(T, 256)
    nb = ncols * LANES // wt
    half = MLA_ROPE // 2

    def body(x_ref, c_ref, s_ref, o_ref):
        xv = x_ref[...].astype(F32)
        lane = lax.broadcasted_iota(jnp.int32, xv.shape, 1)
        first = (lane & (MLA_ROPE - 1)) < half
        swapped = jnp.where(first, pltpu.roll(xv, wt - half, 1), pltpu.roll(xv, half, 1))
        o_ref[...] = (xv * c_ref[...] + swapped * s_ref[...]).astype(out_dtype)

    off = col0 * LANES // wt
    return pl.pallas_call(
        body, name=name, grid=(T // tm, nb),
        in_specs=[pl.BlockSpec((tm, wt), lambda i, j: (i, j + off)),
                  pl.BlockSpec((tm, wt), lambda i, j: (i, 0)),
                  pl.BlockSpec((tm, wt), lambda i, j: (i, 0))],
        out_specs=pl.BlockSpec((tm, wt), lambda i, j: (i, j)),
        out_shape=jax.ShapeDtypeStruct((T, ncols * LANES), out_dtype),
        compiler_params=_params(("parallel", "parallel")),
    )(x, cos_t, sin_t)


ATT_TQ = 512
ATT_TK = 256
MLA_TK = 512


def _mla_masks(shape):
    lane = lax.broadcasted_iota(jnp.int32, shape, 1)
    m0 = (lane < HEAD) | ((lane >= LANES) & (lane < LANES + MLA_ROPE))
    m1 = ((lane >= HEAD) & (lane < LANES)) | ((lane >= LANES + MLA_ROPE) & (lane < LANES + 2 * MLA_ROPE))
    return m0, m1


def _by_twos(n, step, carry):
    carry = lax.fori_loop(0, n // 2, lambda i, c: step(2 * i + 1, step(2 * i, c)), carry)
    return lax.fori_loop(0, n % 2, lambda _, c: step(n - 1, c), carry)


def _chunk_ok(tq, tk, d):
    row = lax.broadcasted_iota(jnp.int32, (tq, tk), 0)
    col = lax.broadcasted_iota(jnp.int32, (tq, tk), 1) + d * tk
    return jnp.concatenate([(col >> CHUNK_BITS) <= (row >> CHUNK_BITS)] * 2, axis=0)


def _rotate(x, cos_t, sin_t):
    half = MLA_ROPE // 2
    lane = lax.broadcasted_iota(jnp.int32, x.shape, 1)
    first = (lane & (MLA_ROPE - 1)) < half
    swapped = jnp.where(first, pltpu.roll(x, x.shape[1] - half, 1), pltpu.roll(x, half, 1))
    return x * cos_t + swapped * sin_t


def _mla_fwd(q, cos_q, sin_q, kv, kr, name="mla_fwd"):
    T = q.shape[0]
    tq, tk = _pick(T, ATT_TQ), _pick(T, MLA_TK)
    nd = tq // tk
    npair = MLA_HEADS // 2
    scale = (MLA_NOPE + MLA_ROPE) ** -0.5

    def body(q_ref, c_ref, s_ref, kn_ref, v_ref, kr_ref, o_ref, lse_ref):
        m_idx = pl.program_id(1)
        qv = _rotate(q_ref[...], c_ref[...], s_ref[...]).astype(BF16)
        m0, m1 = _mla_masks(qv.shape)
        qs = jnp.concatenate([jnp.where(m0, qv, 0), jnp.where(m1, qv, 0)], axis=0).astype(BF16)

        def block(kb, carry, ok):
            ks = pl.ds(pl.multiple_of(kb * tk, tk), tk)
            kcat = jnp.concatenate([kn_ref[ks, :], kr_ref[ks, :]], axis=1)
            mx, l, acc = carry
            s = _dot(qs, kcat, NT) * scale
            if ok is not None:
                s = jnp.where(ok, s, NEG)
            mn = jnp.maximum(mx, jnp.max(s, axis=-1, keepdims=True))
            alpha = jnp.exp(mx - mn)
            p = jnp.exp(s - mn)
            return (mn, alpha * l + jnp.sum(p, axis=-1, keepdims=True),
                    alpha * acc + _dot(p.astype(BF16), v_ref[ks, :], NN))

        init = (jnp.full((2 * tq, 1), NEG, F32), jnp.zeros((2 * tq, 1), F32), jnp.zeros((2 * tq, LANES), F32))
        res = init
        for d in range(nd):
            res = block(m_idx * nd + d, res, _chunk_ok(tq, tk, d))
        mx, l, acc = _by_twos(m_idx * nd, lambda kb, c: block(kb, c, None), res)
        h0 = lax.broadcasted_iota(jnp.int32, (tq, LANES), 1) < HEAD
        o_ref[...] = _two_heads(acc * (1.0 / l), h0).astype(o_ref.dtype)
        lse_ref[...] = _two_heads(jnp.broadcast_to(mx + jnp.log(l), (2 * tq, LANES)), h0)

    full = lambda col: pl.BlockSpec((T, LANES), col)
    table = pl.BlockSpec((tq, 2 * LANES), lambda p, m: (m, 0))
    return pl.pallas_call(
        body, name=name, grid=(npair, T // tq),
        in_specs=[pl.BlockSpec((tq, 2 * LANES), lambda p, m: (m, p)), table, table,
                  full(lambda p, m: (0, p)), full(lambda p, m: (0, npair + p)), full(lambda p, m: (0, 0))],
        out_specs=[pl.BlockSpec((tq, LANES), lambda p, m: (m, p)),
                   pl.BlockSpec((tq, LANES), lambda p, m: (m, p))],
        out_shape=[jax.ShapeDtypeStruct((T, npair * LANES), BF16),
                   jax.ShapeDtypeStruct((T, npair * LANES), F32)],
        compiler_params=_params(("parallel", "arbitrary")),
    )(q, cos_q, sin_q, kv, kv, kr)


def _mla_bwd(q, cos_q, sin_q, kv, kr, o, lse, do, do_col0, name="mla_bwd"):
    T = q.shape[0]
    tq, tk = _pick(T, ATT_TQ), _pick(T, MLA_TK)
    nd = tq // tk
    npair = MLA_HEADS // 2
    scale = (MLA_NOPE + MLA_ROPE) ** -0.5

    def body(q_ref, c_ref, s_ref, kn_ref, v_ref, kr_ref, o_ref, lse_ref, do_ref, dq_ref, dkn_ref, dv_ref, dkr_ref,
             dkn_acc, dv_acc):
        p_idx, m_idx = pl.program_id(0), pl.program_id(1)

        @pl.when(m_idx == 0)
        def _():
            dkn_acc[...] = jnp.zeros_like(dkn_acc)
            dv_acc[...] = jnp.zeros_like(dv_acc)

        @pl.when((m_idx == 0) & (p_idx == 0))
        def _():
            dkr_ref[...] = jnp.zeros_like(dkr_ref)

        qv = _rotate(q_ref[...], c_ref[...], s_ref[...]).astype(BF16)
        m0, m1 = _mla_masks(qv.shape)
        qs = jnp.concatenate([jnp.where(m0, qv, 0), jnp.where(m1, qv, 0)], axis=0).astype(BF16)
        dov = do_ref[...].astype(F32)
        h0 = lax.broadcasted_iota(jnp.int32, (tq, LANES), 1) < HEAD
        dos32 = jnp.concatenate([jnp.where(h0, dov, 0.0), jnp.where(h0, 0.0, dov)], axis=0)
        ov = o_ref[...].astype(F32)
        delta = jnp.sum(dos32 * jnp.concatenate([ov, ov], axis=0), axis=-1, keepdims=True)
        dos = dos32.astype(BF16)
        lsev = lse_ref[...]
        lse = jnp.concatenate([lsev[:, 0:1], lsev[:, HEAD:HEAD + 1]], axis=0)

        def block(kb, dq, ok):
            ks = pl.ds(pl.multiple_of(kb * tk, tk), tk)
            kcat = jnp.concatenate([kn_ref[ks, :], kr_ref[ks, :]], axis=1)
            vv = v_ref[ks, :]
            p = jnp.exp(_dot(qs, kcat, NT) * scale - lse)
            if ok is not None:
                p = jnp.where(ok, p, 0.0)
            ds = (p * (_dot(dos, vv, NT) - delta) * scale).astype(BF16)
            dkc = _dot(ds, qs, TN)
            dkn_acc[ks, :] += dkc[:, :LANES]
            dkr_ref[ks, :] += dkc[:, LANES:]
            dv_acc[ks, :] += _dot(p.astype(BF16), dos, TN)
            return dq + _dot(ds, kcat, NN)

        dq = jnp.zeros((2 * tq, 2 * LANES), F32)
        for d in range(nd):
            dq = block(m_idx * nd + d, dq, _chunk_ok(tq, tk, d))
        dq = _by_twos(m_idx * nd, lambda kb, c: block(kb, c, None), dq)
        dq_ref[...] = _rotate(jnp.where(m0, dq[:tq], jnp.where(m1, dq[tq:], 0.0)), c_ref[...],
                              -s_ref[...]).astype(BF16)

        @pl.when(m_idx == T // tq - 1)
        def _():
            dkn_ref[...] = dkn_acc[...].astype(BF16)
            dv_ref[...] = dv_acc[...].astype(BF16)

    full = lambda col: pl.BlockSpec((T, LANES), col)
    blk = lambda col: pl.BlockSpec((tq, LANES), col)
    table = pl.BlockSpec((tq, 2 * LANES), lambda p, m: (m, 0))
    return pl.pallas_call(
        body, name=name, grid=(npair, T // tq),
        in_specs=[pl.BlockSpec((tq, 2 * LANES), lambda p, m: (m, p)), table, table,
                  full(lambda p, m: (0, p)), full(lambda p, m: (0, npair + p)), full(lambda p, m: (0, 0)),
                  blk(lambda p, m: (m, p)), blk(lambda p, m: (m, p)),
                  blk(lambda p, m: (m, do_col0 + p))],
        out_specs=[pl.BlockSpec((tq, 2 * LANES), lambda p, m: (m, p)),
                   full(lambda p, m: (0, p)), full(lambda p, m: (0, p)), full(lambda p, m: (0, 0))],
        out_shape=[jax.ShapeDtypeStruct((T, npair * 2 * LANES), BF16),
                   jax.ShapeDtypeStruct((T, npair * LANES), BF16),
                   jax.ShapeDtypeStruct((T, npair * LANES), BF16),
                   jax.ShapeDtypeStruct((T, LANES), F32)],
        scratch_shapes=[pltpu.VMEM((T, LANES), F32)] * 2,
        compiler_params=_params(("arbitrary", "arbitrary")),
    )(q, cos_q, sin_q, kv, kv, kr, o, lse, do)


def _split_dot(x, tri):
    hi = x.astype(BF16)
    lo = (x - hi.astype(F32)).astype(BF16)
    both = _dot(jnp.concatenate([hi, lo], axis=0), tri, NN)
    return both[:x.shape[0]] + both[x.shape[0]:]


def _sb_terms(qh, kk, before):
    z = _dot(qh, kk, NT)
    sp = jnp.maximum(z, 0.0) + jnp.log(1.0 + jnp.exp(-jnp.abs(z)))
    lk = -sp if before is None else jnp.where(before, -sp, 0.0)
    return z, sp, lk


def _sb_setup(q_ref, tq, tk, scale):
    qv = (q_ref[...].astype(F32) * scale).astype(BF16)
    lane = lax.broadcasted_iota(jnp.int32, (tq, LANES), 1)
    h0 = lane < HEAD
    qs = jnp.concatenate([jnp.where(h0, qv, 0), jnp.where(h0, 0, qv)], axis=0).astype(BF16)
    row = lax.broadcasted_iota(jnp.int32, (tk, tk), 0)
    col = lax.broadcasted_iota(jnp.int32, (tk, tk), 1)
    return qs, h0, row, col


def _sb_before(tq, tk, d):
    row = lax.broadcasted_iota(jnp.int32, (tq, tk), 0)
    col = lax.broadcasted_iota(jnp.int32, (tq, tk), 1) + d * tk
    return jnp.concatenate([col < row] * 2, axis=0)


def _two_heads(x, h0):
    tq = x.shape[0] // 2
    return jnp.where(h0, x[:tq], x[tq:])


def _sb_fwd(qkv, col0, dep, name="sb_fwd"):
    T = qkv.shape[0]
    tq, tk = _pick(T, ATT_TQ), _pick(T, ATT_TK)
    nd = tq // tk
    npair = SB_HEADS // 2
    scale = SB_DIM ** -0.5

    def body(q_ref, k_ref, v_ref, dep_ref, o_ref, o32_ref, w_ref, sp_ref):
        m_idx = pl.program_id(1)
        qs, h0, row, col = _sb_setup(q_ref, tq, tk, scale)
        later = (row > col).astype(BF16)

        def block(kb, carry, before):
            ks = pl.ds(pl.multiple_of(kb * tk, tk), tk)
            c, acc = carry
            z, sp, lk = _sb_terms(qs, k_ref[ks, :].astype(BF16), before)
            w = jnp.exp((z - sp) + _split_dot(lk, later) + c)
            if before is not None:
                w = jnp.where(before, w, 0.0)
            wb = w.astype(BF16)
            w_ref[0, 0, kb] = wb
            sp_ref[0, 0, kb] = sp.astype(BF16)
            return (c + jnp.sum(lk, axis=-1, keepdims=True), acc + _dot(wb, v_ref[ks, :].astype(BF16), NN))

        init = (jnp.zeros((2 * tq, 1), F32), jnp.zeros((2 * tq, LANES), F32))
        res = init
        for d in reversed(range(nd)):
            res = block(m_idx * nd + d, res, _sb_before(tq, tk, d))
        res = _by_twos(m_idx * nd, lambda i, c: block(m_idx * nd - 1 - i, c, None), res)
        o = _two_heads(res[1], h0)
        o_ref[...] = o.astype(o_ref.dtype)
        o32_ref[...] = o

    full = lambda col: pl.BlockSpec((T, LANES), col)
    blk = pl.BlockSpec((tq, LANES), lambda p, m: (m, p))
    return pl.pallas_call(
        body, name=name, grid=(npair, T // tq),
        in_specs=[pl.BlockSpec((tq, LANES), lambda p, m: (m, col0 + p)),
                  full(lambda p, m: (0, col0 + npair + p)), full(lambda p, m: (0, col0 + 2 * npair + p)),
                  pl.BlockSpec((8, LANES), lambda p, m: (0, 0))],
        out_specs=[blk, blk] + [pl.BlockSpec((1, 1, T // tk, 2 * tq, tk), lambda p, m: (p, m, 0, 0, 0))] * 2,
        out_shape=[jax.ShapeDtypeStruct((T, npair * LANES), BF16), jax.ShapeDtypeStruct((T, npair * LANES), F32)]
        + [jax.ShapeDtypeStruct((npair, T // tq, T // tk, 2 * tq, tk), BF16)] * 2,
        compiler_params=_params(("parallel", "arbitrary")),
    )(qkv, qkv, qkv, dep)


def _sb_bwd(qkv, col0, o32, w_all, sp_all, do, do_col0, dep, name="sb_bwd"):
    T = qkv.shape[0]
    tq, tk = _pick(T, ATT_TQ), _pick(T, ATT_TK)
    nd = tq // tk
    npair = SB_HEADS // 2
    scale = SB_DIM ** -0.5

    def body(q_ref, k_ref, v_ref, o_ref, w_ref, sp_ref, do_ref, dep_ref, dq_ref, dk_ref, dv_ref, dk_acc, dv_acc):
        m_idx = pl.program_id(1)

        @pl.when(m_idx == 0)
        def _():
            dk_acc[...] = jnp.zeros_like(dk_acc)
            dv_acc[...] = jnp.zeros_like(dv_acc)

        qs, h0, row, col = _sb_setup(q_ref, tq, tk, scale)
        dov = do_ref[...].astype(F32)
        dos = jnp.concatenate([jnp.where(h0, dov, 0.0), jnp.where(h0, 0.0, dov)], axis=0).astype(BF16)
        ov = o_ref[...]
        etot = jnp.sum(dos.astype(F32) * jnp.concatenate([ov, ov], axis=0), axis=-1, keepdims=True)
        from_here = (row >= col).astype(BF16)

        def block(kb, carry, before):
            ks = pl.ds(pl.multiple_of(kb * tk, tk), tk)
            kk = k_ref[ks, :].astype(BF16)
            vv = v_ref[ks, :].astype(BF16)
            es, dqa = carry
            wb = w_ref[0, 0, kb]
            e = wb.astype(F32) * _dot(dos, vv, NT)
            prev = etot - (_split_dot(e, from_here) + es)
            sig_neg = jnp.exp(-sp_ref[0, 0, kb].astype(F32))
            dz = e * sig_neg - (1.0 - sig_neg) * prev
            if before is not None:
                dz = jnp.where(before, dz, 0.0)
            dzb = dz.astype(BF16)
            dk_acc[ks, :] += _dot(dzb, qs, TN)
            dv_acc[ks, :] += _dot(wb, dos, TN)
            return es + jnp.sum(e, axis=-1, keepdims=True), dqa + _dot(dzb, kk, NN)

        init = (jnp.zeros((2 * tq, 1), F32), jnp.zeros((2 * tq, LANES), F32))
        res = init
        for d in reversed(range(nd)):
            res = block(m_idx * nd + d, res, _sb_before(tq, tk, d))
        res = _by_twos(m_idx * nd, lambda i, c: block(m_idx * nd - 1 - i, c, None), res)
        dq_ref[...] = (_two_heads(res[1], h0) * scale).astype(BF16)

        @pl.when(m_idx == T // tq - 1)
        def _():
            dk_ref[...] = dk_acc[...].astype(BF16)
            dv_ref[...] = dv_acc[...].astype(BF16)

    full = lambda col: pl.BlockSpec((T, LANES), col)
    blk = lambda col: pl.BlockSpec((tq, LANES), col)
    return pl.pallas_call(
        body, name=name, grid=(npair, T // tq),
        in_specs=[blk(lambda p, m: (m, col0 + p)),
                  full(lambda p, m: (0, col0 + npair + p)), full(lambda p, m: (0, col0 + 2 * npair + p)),
                  blk(lambda p, m: (m, p)),
                  pl.BlockSpec((1, 1, T // tk, 2 * tq, tk), lambda p, m: (p, m, 0, 0, 0)),
                  pl.BlockSpec((1, 1, T // tk, 2 * tq, tk), lambda p, m: (p, m, 0, 0, 0)),
                  blk(lambda p, m: (m, do_col0 + p)), pl.BlockSpec((8, LANES), lambda p, m: (0, 0))],
        out_specs=[blk(lambda p, m: (m, p)), full(lambda p, m: (0, p)), full(lambda p, m: (0, p))],
        out_shape=[jax.ShapeDtypeStruct((T, npair * LANES), BF16)] * 3,
        scratch_shapes=[pltpu.VMEM((T, LANES), F32)] * 2,
        compiler_params=_params(("arbitrary", "arbitrary")),
    )(qkv, qkv, qkv, o32, w_all, sp_all, do, dep)


def _band_in_window():
    cq = lax.broadcasted_iota(jnp.int32, (BAND_TQ, BAND_W), 0) >> CHUNK_BITS
    ckp = lax.broadcasted_iota(jnp.int32, (BAND_TQ, BAND_W), 1) >> CHUNK_BITS
    return (ckp >= cq) & (ckp <= cq + LEFT_CHUNKS)


def _band_real(m_idx):
    j = lax.broadcasted_iota(jnp.int32, (BAND_TQ, BAND_W), 1)
    return j >= PAD_KEYS - m_idx * BAND_TQ


def _band_probs(qh, kw, bias, real, scale):
    s = jnp.where(real, _dot(qh, kw, NT) * scale + bias, NEG)
    e = jnp.exp(s - jnp.max(s, axis=-1, keepdims=True))
    return e * (1.0 / jnp.sum(e, axis=-1, keepdims=True))


BAND_SUB = 16


def _band_fwd(qkv, k_pad, v_pad, bias_w, name="band_fwd"):
    T = qkv.shape[0]
    npair = C_HEADS // 2
    scale = C_DIM ** -0.5
    rows = BAND_SUB * BAND_TQ

    def body(q_ref, k_ref, v_ref, b_ref, o_ref, p_ref):
        lane = lax.broadcasted_iota(jnp.int32, (BAND_TQ, LANES), 1)
        h0 = lane < HEAD
        bias = jnp.concatenate([b_ref[0], b_ref[1]], axis=0)
        for sub in range(BAND_SUB):
            m_idx = pl.program_id(1) * BAND_SUB + sub
            win = pl.ds(pl.multiple_of(m_idx * BAND_TQ, BAND_TQ), BAND_W)
            kw, vw = k_ref[win, :], v_ref[win, :]
            qv = q_ref[sub * BAND_TQ:(sub + 1) * BAND_TQ, :]
            qs = jnp.concatenate([jnp.where(h0, qv, 0), jnp.where(h0, 0, qv)], axis=0).astype(BF16)
            p = _band_probs(qs, kw, bias, jnp.concatenate([_band_real(m_idx)] * 2, axis=0), scale).astype(BF16)
            p_ref[0, sub] = p
            o = _two_heads(_dot(p, vw, NN), h0)
            o_ref[sub * BAND_TQ:(sub + 1) * BAND_TQ, :] = o.astype(o_ref.dtype)

    Tp = T + PAD_KEYS
    return pl.pallas_call(
        body, name=name, grid=(npair, T // rows),
        in_specs=[pl.BlockSpec((rows, LANES), lambda p, m: (m, p)),
                  pl.BlockSpec((Tp, LANES), lambda p, m: (0, p)),
                  pl.BlockSpec((Tp, LANES), lambda p, m: (0, p)),
                  pl.BlockSpec((2, BAND_TQ, BAND_W), lambda p, m: (p, 0, 0))],
        out_specs=[pl.BlockSpec((rows, LANES), lambda p, m: (m, p)),
                   pl.BlockSpec((1, BAND_SUB, 2 * BAND_TQ, BAND_W), lambda p, m: (p, m, 0, 0))],
        out_shape=[jax.ShapeDtypeStruct((T, npair * LANES), BF16),
                   jax.ShapeDtypeStruct((npair, T // BAND_TQ, 2 * BAND_TQ, BAND_W), BF16)],
        compiler_params=_params(("parallel", "arbitrary")),
    )(qkv, k_pad, v_pad, bias_w)


def _band_bwd(qkv, k_pad, v_pad, probs, do, name="band_bwd"):
    T = qkv.shape[0]
    npair = C_HEADS // 2
    scale = C_DIM ** -0.5

    rows = BAND_SUB * BAND_TQ

    def body(q_ref, k_ref, v_ref, p_ref, do_ref, dq_ref, dk_ref, dv_ref, db_ref, dk_acc, dv_acc):
        @pl.when(pl.program_id(1) == 0)
        def _():
            dk_acc[...] = jnp.zeros_like(dk_acc)
            dv_acc[...] = jnp.zeros_like(dv_acc)
            db_ref[...] = jnp.zeros_like(db_ref)

        lane = lax.broadcasted_iota(jnp.int32, (BAND_TQ, LANES), 1)
        h0 = lane < HEAD
        dbs = jnp.zeros((2 * BAND_TQ, BAND_W), F32)
        for sub in range(BAND_SUB):
            m_idx = pl.program_id(1) * BAND_SUB + sub
            win = pl.ds(pl.multiple_of(m_idx * BAND_TQ, BAND_TQ), BAND_W)
            kw, vw = k_ref[win, :], v_ref[win, :]
            qv = q_ref[sub * BAND_TQ:(sub + 1) * BAND_TQ, :]
            dov = do_ref[sub * BAND_TQ:(sub + 1) * BAND_TQ, :].astype(F32)
            qs = jnp.concatenate([jnp.where(h0, qv, 0), jnp.where(h0, 0, qv)], axis=0).astype(BF16)
            dos = jnp.concatenate([jnp.where(h0, dov, 0.0), jnp.where(h0, 0.0, dov)], axis=0).astype(BF16)
            pb = p_ref[0, sub]
            p = pb.astype(F32)
            dp = _dot(dos, vw, NT)
            dsb = p * (dp - jnp.sum(p * dp, axis=-1, keepdims=True))
            dbs = dbs + dsb
            dsq = (dsb * scale).astype(BF16)
            dq_ref[sub * BAND_TQ:(sub + 1) * BAND_TQ, :] = _two_heads(_dot(dsq, kw, NN), h0).astype(BF16)
            dk_acc[win, :] += _dot(dsq, qs, TN)
            dv_acc[win, :] += _dot(pb, dos, TN)
        db_ref[0] += dbs[:BAND_TQ]
        db_ref[1] += dbs[BAND_TQ:]

        @pl.when(pl.program_id(1) == T // rows - 1)
        def _():
            dk_ref[...] = dk_acc[...].astype(BF16)
            dv_ref[...] = dv_acc[...].astype(BF16)

    Tp = T + PAD_KEYS
    blk = lambda col: pl.BlockSpec((rows, LANES), col)
    full = pl.BlockSpec((Tp, LANES), lambda p, m: (0, p))
    bias = pl.BlockSpec((2, BAND_TQ, BAND_W), lambda p, m: (p, 0, 0))
    prob = pl.BlockSpec((1, BAND_SUB, 2 * BAND_TQ, BAND_W), lambda p, m: (p, m, 0, 0))
    return pl.pallas_call(
        body, name=name, grid=(npair, T // rows),
        in_specs=[blk(lambda p, m: (m, p)), full, full, prob, blk(lambda p, m: (m, p))],
        out_specs=[blk(lambda p, m: (m, p)), full, full, bias],
        out_shape=[jax.ShapeDtypeStruct((T, npair * LANES), BF16),
                   jax.ShapeDtypeStruct((Tp, npair * LANES), BF16),
                   jax.ShapeDtypeStruct((Tp, npair * LANES), BF16),
                   jax.ShapeDtypeStruct((C_HEADS, BAND_TQ, BAND_W), F32)],
        scratch_shapes=[pltpu.VMEM((Tp, LANES), F32)] * 2,
        compiler_params=_params(("arbitrary", "arbitrary")),
    )(qkv, k_pad, v_pad, probs, do)


def _skew_bits(x, left):
    w = x.shape[1]
    row = lax.broadcasted_iota(jnp.int32, x.shape, 0)
    for b in range(BAND_TQ.bit_length() - 1):
        amt = (w - (1 << b)) if left else (1 << b)
        x = jnp.where((row >> b) & 1 == 1, pltpu.roll(x, amt, 1), x)
    return x


TOEP_HEADS = 4


def _toeplitz(diag, name="toeplitz"):
    H = diag.shape[0]

    def body(d_ref, o_ref):
        for i in range(TOEP_HEADS):
            x = jnp.broadcast_to(d_ref[i], (BAND_TQ, TOEP_W))
            o_ref[i] = jnp.where(_band_in_window(), _skew_bits(x, left=False)[:, BAND_TQ:], NEG)

    return pl.pallas_call(
        body, name=name, grid=(H // TOEP_HEADS,),
        in_specs=[pl.BlockSpec((TOEP_HEADS, 1, TOEP_W), lambda h: (h, 0, 0))],
        out_specs=pl.BlockSpec((TOEP_HEADS, BAND_TQ, BAND_W), lambda h: (h, 0, 0)),
        out_shape=jax.ShapeDtypeStruct((H, BAND_TQ, BAND_W), F32),
        compiler_params=_params(("parallel",)),
    )(diag.reshape(H, 1, TOEP_W))


def _toeplitz_bwd(dbias, name="toeplitz_bwd"):
    H = dbias.shape[0]

    def body(d_ref, o_ref):
        for i in range(TOEP_HEADS):
            x = jnp.concatenate([jnp.zeros((BAND_TQ, BAND_TQ), F32), d_ref[i]], axis=1)
            h = BAND_TQ // 2
            while h >= 8:
                x = x[:h] + pltpu.roll(x[h:2 * h], TOEP_W - h, 1)
                h //= 2
            o_ref[i] = jnp.sum(_skew_bits(x, left=True), axis=0, keepdims=True)

    return pl.pallas_call(
        body, name=name, grid=(H // TOEP_HEADS,),
        in_specs=[pl.BlockSpec((TOEP_HEADS, BAND_TQ, BAND_W), lambda h: (h, 0, 0))],
        out_specs=pl.BlockSpec((TOEP_HEADS, 1, TOEP_W), lambda h: (h, 0, 0)),
        out_shape=jax.ShapeDtypeStruct((H, 1, TOEP_W), F32),
        compiler_params=_params(("parallel",)),
    )(dbias).reshape(H, TOEP_W)


_HBM = pl.BlockSpec(memory_space=pltpu.HBM)
_SEM = pl.BlockSpec(memory_space=pltpu.SEMAPHORE)
_EFFECT = pltpu.SideEffectType.DATAFLOW_SIDE_EFFECTING


def _peers():
    x, y, c = lax.axis_index("x"), lax.axis_index("y"), lax.axis_index("c")
    out = []
    for k in range(1, N_DEV):
        peer = (1 - x if (k >> 2) & 1 else x, 1 - y if (k >> 1) & 1 else y, 1 - c if k & 1 else c)
        out.append((peer, 4 * peer[0] + 2 * peer[1] + peer[2]))
    return 4 * x + 2 * y + c, out


def _split_copies(ins, lands, scatter, send_sem, recv_sem, arriving):
    me, peers = _peers()
    out = []
    for a in range(len(ins)):
        for peer, idx in peers:
            out.append(pltpu.make_async_remote_copy(
                src_ref=ins[a].at[idx] if scatter[a] else ins[a],
                dst_ref=lands[a].at[idx if arriving else me], send_sem=send_sem, recv_sem=recv_sem,
                device_id=peer, device_id_type=pl.DeviceIdType.MESH))
    return out


def _landing_zones(arrays, scatter):
    return [lax.empty((N_DEV,) + (a.shape[1:] if s else a.shape), a.dtype) for a, s in zip(arrays, scatter)]


def _place_own(arrays, scatter, name):
    n = len(arrays)
    lands = _landing_zones(arrays, scatter)
    me = (4 * lax.axis_index("x") + 2 * lax.axis_index("y") + lax.axis_index("c")).astype(jnp.int32).reshape(1)

    def body(me_ref, *refs):
        for a in range(n):
            refs[2 * n + a][...] = refs[a][...].reshape(refs[2 * n + a].shape)

    def row_spec(shape):
        zeros = (0,) * (len(shape) - 1)
        return pl.BlockSpec((1,) + tuple(shape[1:]), lambda i, me_ref: (me_ref[0],) + zeros)

    in_specs = [row_spec(a.shape) if s else pl.BlockSpec(a.shape, lambda i, me_ref, nd=a.ndim: (0,) * nd)
                for a, s in zip(arrays, scatter)]
    return pl.pallas_call(
        body, name=name,
        out_shape=[jax.ShapeDtypeStruct(l.shape, l.dtype) for l in lands],
        grid_spec=pltpu.PrefetchScalarGridSpec(
            num_scalar_prefetch=1, grid=(1,),
            in_specs=in_specs + [pl.BlockSpec(memory_space=pl.ANY)] * n,
            out_specs=[row_spec(l.shape) for l in lands]),
        input_output_aliases={1 + n + i: i for i in range(n)},
        compiler_params=_params(("arbitrary",)),
    )(me, *arrays, *lands)


def _exchange_start_groups(groups, scatter, after, name, lands=None):
    sizes = [len(g) for g in groups]
    arrays = [a for g in groups for a in g]
    n, ng = len(arrays), len(groups)
    flags = list(scatter) if isinstance(scatter, (list, tuple)) else [scatter] * n
    if lands is None:
        lands = list(_place_own(arrays, flags, name=name.replace("_start_", "_own_")))
    else:
        lands = [l for g in lands for l in g]
    starts = np.cumsum([0] + sizes)

    def body(*refs):
        ins, lnd = refs[:n], refs[n:2 * n]
        sems = refs[2 * n + 1:2 * n + 1 + 2 * ng]
        token = refs[-1]
        for g in range(ng):
            sl = slice(starts[g], starts[g + 1])
            for cp in _split_copies(ins[sl], lnd[sl], flags[sl], sems[2 * g], sems[2 * g + 1], arriving=False):
                cp.start()
        token[...] = jnp.zeros_like(token)

    hbm = lambda a: pltpu.HBM(a.shape, a.dtype)
    out = pl.pallas_call(
        body, name=name,
        out_shape=(*[pltpu.SemaphoreType.DMA(())] * (2 * ng),
                   *[hbm(a) for a in arrays], *[hbm(a) for a in lands],
                   jax.ShapeDtypeStruct((8, LANES), F32)),
        in_specs=[_HBM] * (2 * n) + [pl.BlockSpec(memory_space=pl.ANY)],
        out_specs=(*[_SEM] * (2 * ng), *([_HBM] * (2 * n)), pl.BlockSpec(memory_space=pltpu.VMEM)),
        input_output_aliases={i: 2 * ng + i for i in range(2 * n)},
        compiler_params=pltpu.CompilerParams(has_side_effects=_EFFECT),
    )(*[pltpu.with_memory_space_constraint(a, pltpu.HBM) for a in list(arrays) + lands], after)
    ins_out, lands_out = out[2 * ng:2 * ng + n], out[2 * ng + n:2 * ng + 2 * n]
    handles = [(out[2 * g], out[2 * g + 1], list(ins_out[starts[g]:starts[g + 1]]),
                list(lands_out[starts[g]:starts[g + 1]]), tuple(flags[starts[g]:starts[g + 1]]))
               for g in range(ng)]
    return handles, out[-1]


def _exchange_start(arrays, scatter, after, name):
    handles, token = _exchange_start_groups([list(arrays)], list(scatter), after, name)
    return handles[0], token


def _exchange_wait(handle, after, name):
    send_sem, recv_sem, ins, lands, scatter = handle
    n = len(ins)
    after = after if isinstance(after, tuple) else (after,)

    def body(*refs):
        i_ref, l_ref = refs[:n], refs[n:2 * n]
        s_sem, r_sem = refs[2 * n:2 * n + 2]
        for cp in _split_copies(i_ref, l_ref, scatter, s_sem, r_sem, arriving=False):
            cp.wait_send()
        for cp in _split_copies(i_ref, l_ref, scatter, s_sem, r_sem, arriving=True):
            cp.wait_recv()

    hbm = lambda a: pltpu.HBM(a.shape, a.dtype)
    out = pl.pallas_call(
        body, name=name,
        out_shape=tuple(hbm(a) for a in ins + lands),
        in_specs=[_HBM] * (2 * n) + [_SEM, _SEM] + [pl.BlockSpec(memory_space=pl.ANY)] * len(after),
        out_specs=tuple([_HBM] * (2 * n)),
        input_output_aliases={i: i for i in range(2 * n)},
        compiler_params=pltpu.CompilerParams(has_side_effects=_EFFECT),
    )(*ins, *lands, send_sem, recv_sem, *after)
    return list(out[n:])


_SIBLING = 1
_CHIPS = (4, 2, 6)


def _peer_of(k):
    x, y, c = lax.axis_index("x"), lax.axis_index("y"), lax.axis_index("c")
    peer = (1 - x if (k >> 2) & 1 else x, 1 - y if (k >> 1) & 1 else y, 1 - c if k & 1 else c)
    return peer, 4 * peer[0] + 2 * peer[1] + peer[2]


def _rcopy(src, dst, send_sem, recv_sem, k):
    return pltpu.make_async_remote_copy(src_ref=src, dst_ref=dst, send_sem=send_sem, recv_sem=recv_sem,
                                        device_id=_peer_of(k)[0], device_id_type=pl.DeviceIdType.MESH)


def _gather2_start(groups, lands, after, name):
    sizes = [len(g) for g in groups]
    arrays = [a for g in groups for a in g]
    lands = [l for g in lands for l in g]
    n, ng = len(arrays), len(groups)
    starts = np.cumsum([0] + sizes)

    def body(*refs):
        ins, lnd = refs[:n], refs[n:2 * n]
        sems = refs[2 * n + 1:2 * n + 1 + 4 * ng]
        me, _ = _peers()
        for g in range(ng):
            send_d, recv_d, send_i, recv_i = sems[4 * g:4 * g + 4]
            for a in range(starts[g], starts[g + 1]):
                for k in _CHIPS:
                    _rcopy(ins[a], lnd[a].at[me], send_i, recv_i, k).start()
                _rcopy(ins[a], lnd[a].at[me], send_d, recv_d, _SIBLING).start()
        refs[-1][...] = jnp.zeros_like(refs[-1])

    hbm = lambda a: pltpu.HBM(a.shape, a.dtype)
    out = pl.pallas_call(
        body, name=name,
        out_shape=(*[pltpu.SemaphoreType.DMA(())] * (4 * ng), *[hbm(a) for a in arrays], *[hbm(a) for a in lands],
                   jax.ShapeDtypeStruct((8, LANES), F32)),
        in_specs=[_HBM] * (2 * n) + [pl.BlockSpec(memory_space=pl.ANY)],
        out_specs=(*[_SEM] * (4 * ng), *([_HBM] * (2 * n)), pl.BlockSpec(memory_space=pltpu.VMEM)),
        input_output_aliases={i: 4 * ng + i for i in range(2 * n)},
        compiler_params=pltpu.CompilerParams(has_side_effects=_EFFECT),
    )(*[pltpu.with_memory_space_constraint(a, pltpu.HBM) for a in arrays + lands], after)
    ins_out, lands_out = out[4 * ng:4 * ng + n], out[4 * ng + n:4 * ng + 2 * n]
    handles = [dict(sems=out[4 * g:4 * g + 4], ins=list(ins_out[starts[g]:starts[g + 1]]),
                    lands=list(lands_out[starts[g]:starts[g + 1]])) for g in range(ng)]
    return handles, out[-1]


def _gather2_pass_on(handle, after, name):
    lands, recv_i = handle["lands"], handle["sems"][3]
    n = len(lands)
    after = after if isinstance(after, tuple) else (after,)

    def body(*refs):
        lnd, r_i = refs[:n], refs[n]
        send_f, recv_f = refs[n + 1 + len(after):n + 3 + len(after)]
        for a in range(n):
            for k in _CHIPS:
                row = _peer_of(k)[1]
                _rcopy(lnd[a].at[row], lnd[a].at[row], send_f, r_i, k).wait_recv()
        for a in range(n):
            for k in _CHIPS:
                row = _peer_of(k)[1]
                _rcopy(lnd[a].at[row], lnd[a].at[row], send_f, recv_f, _SIBLING).start()
        refs[-1][...] = jnp.zeros_like(refs[-1])

    hbm = lambda a: pltpu.HBM(a.shape, a.dtype)
    out = pl.pallas_call(
        body, name=name,
        out_shape=(pltpu.SemaphoreType.DMA(()), pltpu.SemaphoreType.DMA(()), *[hbm(a) for a in lands],
                   jax.ShapeDtypeStruct((8, LANES), F32)),
        in_specs=[_HBM] * n + [_SEM] + [pl.BlockSpec(memory_space=pl.ANY)] * len(after),
        out_specs=(_SEM, _SEM, *([_HBM] * n), pl.BlockSpec(memory_space=pltpu.VMEM)),
        input_output_aliases={i: 2 + i for i in range(n)},
        compiler_params=pltpu.CompilerParams(has_side_effects=_EFFECT),
    )(*lands, recv_i, *after)
    return dict(handle, lands=list(out[2:2 + n]), passed=(out[0], out[1])), out[-1]


def _gather2_wait(handle, after, name):
    ins, lands = handle["ins"], handle["lands"]
    send_d, recv_d, send_i, _ = handle["sems"]
    send_f, recv_f = handle["passed"]
    n = len(ins)
    after = after if isinstance(after, tuple) else (after,)

    def body(*refs):
        i_ref, lnd = refs[:n], refs[n:2 * n]
        s_d, r_d, s_i, s_f, r_f = refs[2 * n:2 * n + 5]
        me, _ = _peers()
        sib = _peer_of(_SIBLING)[1]
        for a in range(n):
            _rcopy(i_ref[a], lnd[a].at[sib], s_d, r_d, _SIBLING).wait_send()
            _rcopy(i_ref[a], lnd[a].at[sib], s_d, r_d, _SIBLING).wait_recv()
            for k in _CHIPS:
                row = _peer_of(k)[1]
                _rcopy(i_ref[a], lnd[a].at[me], s_i, r_d, k).wait_send()
                _rcopy(lnd[a].at[row], lnd[a].at[row], s_f, r_f, _SIBLING).wait_send()
                _rcopy(lnd[a].at[row], lnd[a].at[_peer_of(k ^ _SIBLING)[1]], s_f, r_f, _SIBLING).wait_recv()

    hbm = lambda a: pltpu.HBM(a.shape, a.dtype)
    out = pl.pallas_call(
        body, name=name,
        out_shape=tuple(hbm(a) for a in ins + lands),
        in_specs=[_HBM] * (2 * n) + [_SEM] * 5 + [pl.BlockSpec(memory_space=pl.ANY)] * len(after),
        out_specs=tuple([_HBM] * (2 * n)),
        input_output_aliases={i: i for i in range(2 * n)},
        compiler_params=pltpu.CompilerParams(has_side_effects=_EFFECT),
    )(*ins, *lands, send_d, recv_d, send_i, send_f, recv_f, *after)
    return list(out[n:])


def _adamw(w, parts, m, v, name="adamw"):
    R, C = w.shape
    L = len(parts)
    rl = R // L
    tr = max([t for t in range(16, 257, 16) if rl % t == 0], default=rl)
    nb = rl // tr
    c1 = 1.0 - ADAM_B1 ** ADAM_STEP
    c2 = 1.0 - ADAM_B2 ** ADAM_STEP

    def body(*refs):
        w_ref, p_refs, (m_ref, v_ref, g_ref, d_ref, nm_ref, nv_ref) = refs[0], refs[1:1 + L], refs[1 + L:]
        g = None
        for j, p_ref in enumerate(p_refs):
            gj = p_ref[0].astype(F32)
            for i in range(1, N_DEV):
                gj = gj + p_ref[i].astype(F32)
            g = gj if g is None else jnp.where(pl.program_id(0) == j, gj, g)
        nm = ADAM_B1 * m_ref[...] + (1.0 - ADAM_B1) * g
        nv = ADAM_B2 * v_ref[...] + (1.0 - ADAM_B2) * (g * g)
        g_ref[...] = g
        nm_ref[...] = nm
        nv_ref[...] = nv
        d_ref[...] = -ADAM_LR * ((nm / c1) / (jnp.sqrt(nv / c2) + ADAM_EPS) + ADAM_WD * w_ref[...])

    blk = pl.BlockSpec((tr, C), lambda l, i: (l * nb + i, 0))
    part = lambda j: pl.BlockSpec((N_DEV, tr, C), lambda l, i: (0, jnp.where(l == j, i, 0), 0))
    return pl.pallas_call(
        body, name=name, grid=(L, nb),
        in_specs=[blk] + [part(j) for j in range(L)] + [blk, blk],
        out_specs=[blk] * 4,
        out_shape=[jax.ShapeDtypeStruct((R, C), F32)] * 4,
        compiler_params=_params(("arbitrary", "arbitrary")),
    )(w, *parts, m, v)


_O1 = Q_LORA
_O2 = _O1 + KV_LORA
_O3 = _O2 + MLA_ROPE
_NB = SB_HEADS * SB_DIM
IN_W = _O2 + LANES + 3 * _NB
COL_KR = _O2 // LANES
COL_SB = COL_KR + 1


def _w_in_local(w):
    kr = w[_O2:_O3]
    pad = jnp.zeros((LANES - 2 * MLA_ROPE, w.shape[1]), w.dtype)
    return jnp.concatenate([w[:_O2], kr, kr, pad, w[_O3:]], axis=0)


def _w_in_grad(g):
    kr = (g[_O2:_O2 + MLA_ROPE].astype(F32) + g[_O2 + MLA_ROPE:_O2 + 2 * MLA_ROPE].astype(F32)).astype(g.dtype)
    return jnp.concatenate([g[:_O2], kr, g[_O2 + LANES:]], axis=0)


def _w_uq_local(w):
    w3 = w.reshape(MLA_HEADS // 2, 2, MLA_NOPE + MLA_ROPE, w.shape[1])
    nope = w3[:, :, :MLA_NOPE].reshape(MLA_HEADS // 2, 2 * MLA_NOPE, w.shape[1])
    rope = w3[:, :, MLA_NOPE:].reshape(MLA_HEADS // 2, 2 * MLA_ROPE, w.shape[1])
    pad = jnp.zeros((MLA_HEADS // 2, LANES - 2 * MLA_ROPE, w.shape[1]), w.dtype)
    return jnp.concatenate([nope, rope, pad], axis=1).reshape(-1, w.shape[1])


def _w_uq_grad(g):
    g3 = g.reshape(MLA_HEADS // 2, 2 * LANES, g.shape[1])
    nope = g3[:, :2 * MLA_NOPE].reshape(MLA_HEADS // 2, 2, MLA_NOPE, g.shape[1])
    rope = g3[:, LANES:LANES + 2 * MLA_ROPE].reshape(MLA_HEADS // 2, 2, MLA_ROPE, g.shape[1])
    return jnp.concatenate([nope, rope], axis=2).reshape(-1, g.shape[1])


def _w_ukv_local(w):
    w3 = w.reshape(MLA_HEADS, MLA_NOPE + MLA_V, w.shape[1])
    return jnp.concatenate([w3[:, :MLA_NOPE].reshape(-1, w.shape[1]),
                            w3[:, MLA_NOPE:].reshape(-1, w.shape[1])], axis=0)


def _w_ukv_grad(g):
    half = MLA_HEADS * MLA_NOPE
    kn = g[:half].reshape(MLA_HEADS, MLA_NOPE, g.shape[1])
    vv = g[half:].reshape(MLA_HEADS, MLA_V, g.shape[1])
    return jnp.concatenate([kn, vv], axis=1).reshape(-1, g.shape[1])


def _rope_tables(T):
    pos = jnp.arange(T, dtype=F32)
    inv_freq = ROPE_THETA ** (-jnp.arange(0, MLA_ROPE, 2, dtype=F32) / MLA_ROPE)
    ang = pos[:, None] * inv_freq[None, :]
    cos, sin = jnp.cos(ang), jnp.sin(ang)
    ones = jnp.ones((T, LANES - 2 * MLA_ROPE), F32)
    cos_k = jnp.concatenate([cos, cos, cos, cos, ones], axis=1)
    sin_k = jnp.concatenate([-sin, sin, -sin, sin, 0.0 * ones], axis=1)
    cos_q = jnp.concatenate([jnp.ones((T, LANES), F32), cos_k], axis=1)
    sin_q = jnp.concatenate([jnp.zeros((T, LANES), F32), sin_k], axis=1)
    return cos_q, sin_q, cos_k, sin_k


def _bias_diag_index():
    ell = np.arange(TOEP_W)
    return np.clip(BAND_W - ell, -REL_CLIP, REL_CLIP) + REL_CLIP


def _local_step(x, target, small, get_weights, put_grads, prefetch):
    T = x.shape[0]
    cos_q, sin_q, cos_k, sin_k = _rope_tables(T)
    G = {}
    W = dict(small)

    u0 = _rms_fwd(x, W["g_mix"][0:1], name="rms_mix0")
    bias_w = _toeplitz(W["od_rel_bias"][:, _bias_diag_index()])
    W.update(get_weights("in0", (u0, bias_w)))
    proj = _mm(u0, W["w_in_t"], dims="nt", name="proj_in")
    W.update(get_weights("mix0", proj))
    c_q, c_kv = proj[:, :_O1], proj[:, _O1:_O2]
    nq = _rms_fwd(c_q, W["g_cq"], name="rms_cq")
    nkv = _rms_fwd(c_kv, W["g_ckv"], name="rms_ckv")
    qa_raw = _mm(nq, W["w_uq_t"], dims="nt", name="proj_uq")
    kv = _mm(nkv, W["w_ukv_t"], dims="nt", out_dtype=BF16, name="proj_ukv")
    kr = _rope(proj, cos_k, sin_k, COL_KR, 1, BF16, name="rope_k")
    o_a, lse = _mla_fwd(qa_raw, cos_q, sin_q, kv, kr)
    o_b, o_b32, w_b, sp_b = _sb_fwd(proj, COL_SB, prefetch("ffn0", o_a))
    o_ab = jnp.concatenate([o_a, o_b], axis=1)
    h1 = _mm(o_ab, W["ev_w_out"], res=x, name="out_ev")

    def ffn_fwd(h, layer):
        W.update(get_weights(f"ffn{layer}", h))
        return _ffn_fwd(h, W["g_ffn"][layer:layer + 1], W[f"w_gate_t{layer}"], W[f"w_up_t{layer}"],
                        W[f"w_down{layer}"], name=f"ffn_fwd{layer}")

    h2, u1, a0, b0 = ffn_fwd(h1, 0)

    W.update(get_weights("mix1", h2))
    u2 = _rms_fwd(h2, W["g_mix"][1:2], name="rms_mix1")
    qkv = _mm(u2, W["od_w_qkv_t"], dims="nt", out_dtype=BF16, name="proj_qkv")
    nc = C_HEADS * C_DIM
    pad = ((PAD_KEYS, 0), (0, 0))
    k_pad, v_pad = jnp.pad(qkv[:, nc:2 * nc], pad), jnp.pad(qkv[:, 2 * nc:], pad)
    o_c, p_c = _band_fwd(qkv, k_pad, v_pad, bias_w)
    h3 = _mm(o_c, W["od_w_out"], res=h2, name="out_od")
    h4, u3, a1, b1 = ffn_fwd(h3, 1)

    loss, dh, dhb, G["g_final"] = _loss_head(h4, W["g_final"], target)

    def ffn_bwd(dh, dhb, h, u, a, b, layer):
        du, g_gate, g_up, g_down = _ffn_bwd(dhb, u, a, b, W[f"w_gate_t{layer}"], W[f"w_up_t{layer}"],
                                            W[f"w_down{layer}"], name=f"ffn_bwd{layer}")
        tok = put_grads(f"ffn{layer}", {"w_gate_t": g_gate, "w_up_t": g_up, "w_down": g_down})
        return _rms_bwd(h, W["g_ffn"][layer:layer + 1] + tok[:1, :1], du, dres=dh, name=f"rms_ffn_bwd{layer}")

    dh3, dh3b, g_gffn1 = ffn_bwd(dh, dhb, h3, u3, a1, b1, 1)

    do_c = _mm(dh3b, W["od_w_out"], dims="nt", name="out_od_dx")
    g_od_out = _mm(o_c, dh3b, dims="tn", out_dtype=BF16, name="out_od_dw")
    dq_c, dk_p, dv_p, dbias_w = _band_bwd(qkv, k_pad, v_pad, p_c, do_c)
    dqkv = jnp.concatenate([dq_c, dk_p[PAD_KEYS:], dv_p[PAD_KEYS:]], axis=1)
    tok = put_grads("mix1", {"od_w_qkv_t": _mm(dqkv, u2, dims="tn", out_dtype=BF16, name="proj_qkv_dw"),
                             "od_w_out": g_od_out})
    ddiag = _toeplitz_bwd(dbias_w)
    n_far = BAND_W - REL_CLIP + 1
    G["od_rel_bias"] = jnp.concatenate(
        [jnp.zeros((C_HEADS, REL_CLIP - BAND_TQ + 1), F32), ddiag[:, n_far:][:, ::-1],
         jnp.sum(ddiag[:, :n_far], axis=1, keepdims=True)], axis=1)
    dh2, dh2b, g_gmix1 = _mm_rms_bwd(dqkv, W["od_w_qkv_t"], h2, W["g_mix"][1:2] + tok[:1, :1], dh3,
                                     name="proj_qkv_dx")

    dh1, dh1b, g_gffn0 = ffn_bwd(dh2, dh2b, h1, u1, a0, b0, 0)
    G["g_ffn"] = jnp.concatenate([g_gffn0, g_gffn1], axis=0)

    do_ab = _mm(dh1b, W["ev_w_out"], dims="nt", name="out_ev_dx")
    g0 = {"ev_w_out": _mm(o_ab, dh1b, dims="tn", out_dtype=BF16, name="out_ev_dw")}
    dqa_raw, dkn, dva, dkr = _mla_bwd(qa_raw, cos_q, sin_q, kv, kr, o_a, lse, do_ab, 0)
    dlat, g0["w_uq_t"], g0["w_ukv_t"], G["g_cq"], G["g_ckv"] = _latent_bwd(
        proj, nq, nkv, dqa_raw, dkn, dva, dkr, cos_k, sin_k, W["g_cq"], W["g_ckv"], W["w_uq_t"], W["w_ukv_t"])
    tok = put_grads("mix0", g0)
    dqb, dkb, dvb = _sb_bwd(proj, COL_SB, o_b32, w_b, sp_b, do_ab, MLA_HEADS // 2, tok)
    dproj = jnp.concatenate([dlat, dqb, dkb, dvb], axis=1)
    tok = put_grads("in0", {"w_in_t": _mm(dproj, u0, dims="tn", name="proj_in_dw")})
    dx, _, g_gmix0 = _mm_rms_bwd(dproj, W["w_in_t"], x, W["g_mix"][0:1] + tok[:1, :1], dh1, name="proj_in_dx")
    G["g_mix"] = jnp.concatenate([g_gmix0, g_gmix1], axis=0)
    return loss[0, 0], dx, G


_BIG = ["ev_w_in", "ev_w_uq", "ev_w_ukv", "ev_w_out", "od_w_qkv", "od_w_out", "w_gate", "w_up", "w_down"]
_COL_SHARDED = {"ev_w_in", "ev_w_uq", "ev_w_ukv", "od_w_qkv", "w_gate", "w_up"}
_SMALL = ["ev_g_cq", "ev_g_ckv", "od_rel_bias", "g_mix", "g_ffn", "g_final"]
_GROUPS = {
    "in0": ["ev_w_in"],
    "mix0": ["ev_w_uq", "ev_w_ukv", "ev_w_out"],
    "ffn0": ["w_gate0", "w_up0", "w_down0"],
    "mix1": ["od_w_qkv", "od_w_out"],
    "ffn1": ["w_gate1", "w_up1", "w_down1"],
}
_GROUP_SRC = {n + str(l): (n, l) for n in ("w_gate", "w_up", "w_down") for l in (0, 1)}
_BATCHES = {"in0": ["in0"], "layer0": ["mix0", "ffn0"], "layer1": ["mix1", "ffn1"]}
_BATCH_OF = {grp: batch for batch, grps in _BATCHES.items() for grp in grps}
_SMALL_ROWS = 8
_SMALL_COLS = 1792


def _pack_small(vals):
    flat = jnp.concatenate([v.reshape(-1).astype(F32) for v in vals])
    flat = jnp.pad(flat, (0, _SMALL_ROWS * _SMALL_COLS - flat.shape[0]))
    return flat.reshape(_SMALL_ROWS, _SMALL_COLS)


def _unpack_small(packed, like):
    flat = packed.reshape(-1)
    out, off = [], 0
    for v in like:
        out.append(flat[off:off + v.size].reshape(v.shape))
        off += v.size
    return out


def kernel(x, ev_w_in, ev_g_cq, ev_w_uq, ev_g_ckv, ev_w_ukv, ev_w_out, od_w_qkv, od_rel_bias, od_w_out, g_mix, g_ffn, w_gate, w_up, w_down, g_final, loss_target, m_ev_w_in, m_ev_g_cq, m_ev_w_uq, m_ev_g_ckv, m_ev_w_ukv, m_ev_w_out, m_od_w_qkv, m_od_rel_bias, m_od_w_out, m_g_mix, m_g_ffn, m_w_gate, m_w_up, m_w_down, m_g_final, v_ev_w_in, v_ev_g_cq, v_ev_w_uq, v_ev_g_ckv, v_ev_w_ukv, v_ev_w_out, v_od_w_qkv, v_od_rel_bias, v_od_w_out, v_g_mix, v_g_ffn, v_w_gate, v_w_up, v_w_down, v_g_final):
    args = dict(locals())
    w = {n: args[n] for n in _BIG + _SMALL}
    mom = {n: args["m_" + n] for n in _BIG + _SMALL}
    var = {n: args["v_" + n] for n in _BIG + _SMALL}

    own = {}
    for grp, names in _GROUPS.items():
        for n in names:
            base, layer = _GROUP_SRC.get(n, (n, 0))
            shard = w[base][layer:layer + 1]
            own[n] = (jnp.swapaxes(shard, 1, 2) if base in _COL_SHARDED else shard).astype(BF16)
    placed = dict(zip(own, _place_own(list(own.values()), [False] * len(own), name="gather_own")))
    handles, token = _gather2_start(
        [[own[n] for n in names] for names in _GROUPS.values()],
        [[placed[n] for n in names] for names in _GROUPS.values()], x[0, :8, :LANES], name="gather_start")
    gather = dict(zip(_GROUPS, handles))
    pass_before = {"in0": ["in0"], "mix0": ["mix0"]}
    pass_after = {"ffn0": ("mix1", "g_ffn"), "mix1": ("ffn1", "g_mix")}

    def prefetch(grp, after):
        gather[grp], tok = _gather2_pass_on(gather[grp], after, name="gather_pass_" + grp)
        return tok

    def get_weights(grp, after):
        names = _GROUPS[grp]
        after = token if after is None else after
        for g in pass_before.get(grp, []):
            gather[g], _ = _gather2_pass_on(gather[g], after, name="gather_pass_" + g)
        lands = _gather2_wait(gather[grp], after, name="gather_wait_" + grp)
        full = {n: l.reshape(-1, l.shape[-1]) for n, l in zip(names, lands)}
        out = {}
        if grp in pass_after:
            g, gain = pass_after[grp]
            gather[g], tok = _gather2_pass_on(gather[g], lands[0], name="gather_pass_" + g)
            out[gain] = small[gain] + tok[:1, :1]
        if grp == "in0":
            out.update({"w_in_t": _w_in_local(full["ev_w_in"])})
        elif grp == "mix0":
            out.update({"w_uq_t": _w_uq_local(full["ev_w_uq"]), "w_ukv_t": _w_ukv_local(full["ev_w_ukv"]),
                        "ev_w_out": full["ev_w_out"]})
        elif grp == "mix1":
            out.update({"od_w_qkv_t": full["od_w_qkv"], "od_w_out": full["od_w_out"]})
        else:
            layer = grp[-1]
            out.update({"w_gate_t" + layer: full["w_gate" + layer], "w_up_t" + layer: full["w_up" + layer],
                        "w_down" + layer: full["w_down" + layer]})
        return out

    scatter, pending = {}, {}

    def put_grads(grp, g):
        if grp == "in0":
            g = {"ev_w_in": _w_in_grad(g["w_in_t"])}
        elif grp == "mix0":
            g = {"ev_w_uq": _w_uq_grad(g["w_uq_t"]), "ev_w_ukv": _w_ukv_grad(g["w_ukv_t"]),
                 "ev_w_out": g["ev_w_out"]}
        elif grp == "mix1":
            g = {"od_w_qkv": g["od_w_qkv_t"], "od_w_out": g["od_w_out"]}
        else:
            layer = grp[-1]
            g = {"w_gate" + layer: g["w_gate_t"], "w_up" + layer: g["w_up_t"], "w_down" + layer: g["w_down"]}
        pending.update({n: v.reshape(N_DEV, 1, v.shape[0] // N_DEV, v.shape[1]).astype(BF16) for n, v in g.items()})
        batch = _BATCH_OF[grp]
        names = [n for gr in _BATCHES[batch] for n in _GROUPS[gr]]
        if batch == "in0" or not all(n in pending for n in names):
            return jnp.zeros((8, LANES), F32)
        send = [pending[n] for n in names]
        scatter[batch], tok = _exchange_start(send, [True] * len(names), send[0], name="scatter_start_" + batch)
        return tok

    small = {"g_cq": ev_g_cq, "g_ckv": ev_g_ckv, "od_rel_bias": od_rel_bias[0],
             "g_mix": g_mix + token[0, 0], "g_ffn": g_ffn, "g_final": g_final.reshape(1, -1)}
    loss_part, dx, G = _local_step(x[0], loss_target[0], small, get_weights, put_grads, prefetch)
    g_small = _pack_small([G["g_cq"], G["g_ckv"], G["od_rel_bias"], G["g_mix"], G["g_ffn"], G["g_final"],
                           loss_part.reshape(1)])
    scatter["in0"], _ = _exchange_start([pending["ev_w_in"], g_small], [True, False], dx, name="scatter_start_in0")

    grads, deltas, new_m, new_v = {}, {}, {}, {}
    parts, after = {}, dx

    def wait_parts(batch, after):
        lands = _exchange_wait(scatter[batch], after, name="scatter_wait_" + batch)
        parts.update(zip([n for grp in _BATCHES[batch] for n in _GROUPS[grp]], lands))
        return lands[0]

    def adamw(n):
        col = n in _COL_SHARDED
        rows = lambda a: (jnp.swapaxes(a, 1, 2) if col else a).reshape(-1, a.shape[1 if col else 2])
        layers = [parts[n]] if n in parts else [parts[n + "0"], parts[n + "1"]]
        res = _adamw(rows(w[n]), [p.reshape(N_DEV, -1, p.shape[-1]) for p in layers], rows(mom[n]), rows(var[n]),
                     name="adamw_" + n)
        L, a1, a2 = w[n].shape
        back = lambda r: jnp.swapaxes(r.reshape(L, a2, a1), 1, 2) if col else r.reshape(L, a1, a2)
        grads[n], deltas[n], new_m[n], new_v[n] = [back(r) for r in res]
        return res[0]

    for batch in ("layer1", "layer0"):
        after = wait_parts(batch, after)
    parts["ev_w_in"], small_parts = _exchange_wait(scatter["in0"], tuple(adamw(n) for n in _BIG[1:]),
                                                   name="scatter_wait_in0")
    adamw("ev_w_in")
    small_w = [w[n] for n in _SMALL]
    loss = jnp.sum(small_parts.reshape(N_DEV, -1)[:, sum(v.size for v in small_w)])
    res = _adamw(_pack_small(small_w), [small_parts], _pack_small([mom[n] for n in _SMALL]),
                 _pack_small([var[n] for n in _SMALL]), name="adamw_small")
    for d, packed in zip((grads, deltas, new_m, new_v), res):
        for n, val in zip(_SMALL, _unpack_small(packed, small_w)):
            d[n] = val

    order = ["ev_w_in", "ev_g_cq", "ev_w_uq", "ev_g_ckv", "ev_w_ukv", "ev_w_out", "od_w_qkv", "od_rel_bias",
             "od_w_out", "g_mix", "g_ffn", "w_gate", "w_up", "w_down", "g_final"]
    out = [loss, dx[None]]
    for d in (grads, deltas, new_m, new_v):
        out += [d[n] for n in order]
    return tuple(out)
```

```python
import functools

import numpy as np
import jax
import jax.numpy as jnp
from jax import lax
from jax.experimental import pallas as pl
from jax.experimental.pallas import tpu as pltpu

F32 = jnp.float32
BF16 = jnp.bfloat16

D_MODEL = 1024
CHUNK = 64
MLA_HEADS = 8
MLA_NOPE = 64
MLA_ROPE = 32
MLA_V = 64
Q_LORA = 384
KV_LORA = 256
ROPE_THETA = 10000.0
SB_HEADS = 8
SB_DIM = 64
C_HEADS = 16
C_DIM = 64
LEFT_CHUNKS = 8
REL_CLIP = 256
D_FF = 2816
RMS_EPS = 1e-6
ADAM_LR = 0.001
ADAM_B1 = 0.9
ADAM_B2 = 0.999
ADAM_EPS = 1e-08
ADAM_WD = 0.01
ADAM_STEP = 10

N_DEV = 8
LANES = 128
HEAD = 64
assert HEAD == MLA_NOPE == MLA_V == SB_DIM == C_DIM and 2 * HEAD == LANES
CHUNK_BITS = CHUNK.bit_length() - 1
assert 1 << CHUNK_BITS == CHUNK
VMEM_LIMIT = 56 * 1024 * 1024
NEG = -1e30
PAD_KEYS = LEFT_CHUNKS * CHUNK
BAND_TQ = 128
BAND_W = BAND_TQ + PAD_KEYS
TOEP_W = BAND_W + BAND_TQ

NN = (((1,), (0,)), ((), ()))
NT = (((1,), (1,)), ((), ()))
TN = (((0,), (0,)), ((), ()))


def _dot(a, b, dn):
    return lax.dot_general(a, b, dn, preferred_element_type=F32)


def _pick(dim, pref):
    if dim <= pref:
        return dim
    best = None
    for t in range(LANES, pref + 1, LANES):
        if dim % t == 0:
            best = t
    assert best is not None, (dim, pref)
    return best


def _params(sem):
    return pltpu.CompilerParams(dimension_semantics=sem, vmem_limit_bytes=VMEM_LIMIT)


def _mm(a, b, dims="nn", res=None, out_dtype=F32, name="mm"):
    if dims == "nn":
        (M, K), (K2, N) = a.shape, b.shape
    elif dims == "nt":
        (M, K), (N, K2) = a.shape, b.shape
    else:
        (K, M), (K2, N) = a.shape, b.shape
    assert K == K2, (a.shape, b.shape, dims)
    tm, tn, tk = _pick(M, 1024), _pick(N, 1152), _pick(K, 2048 if dims == "tn" else 1024)
    nk = K // tk
    dn = {"nn": NN, "nt": NT, "tn": TN}[dims]
    has_res = res is not None

    def body(*refs):
        if has_res:
            a_ref, b_ref, r_ref, o_ref, acc = refs
        else:
            a_ref, b_ref, o_ref, acc = refs
        k = pl.program_id(2)

        @pl.when(k == 0)
        def _():
            acc[...] = jnp.zeros_like(acc)

        acc[...] += _dot(a_ref[...].astype(BF16), b_ref[...].astype(BF16), dn)

        @pl.when(k == nk - 1)
        def _():
            r = acc[...]
            if has_res:
                r = r + r_ref[...]
            o_ref[...] = r.astype(out_dtype)

    a_spec = (pl.BlockSpec((tk, tm), lambda i, j, k: (k, i)) if dims == "tn"
              else pl.BlockSpec((tm, tk), lambda i, j, k: (i, k)))
    b_spec = (pl.BlockSpec((tn, tk), lambda i, j, k: (j, k)) if dims == "nt"
              else pl.BlockSpec((tk, tn), lambda i, j, k: (k, j)))
    o_spec = pl.BlockSpec((tm, tn), lambda i, j, k: (i, j))
    in_specs = [a_spec, b_spec] + ([o_spec] if has_res else [])
    args = (a, b) + ((res,) if has_res else ())
    return pl.pallas_call(
        body, name=name, grid=(M // tm, N // tn, nk),
        in_specs=in_specs, out_specs=o_spec,
        out_shape=jax.ShapeDtypeStruct((M, N), out_dtype),
        scratch_shapes=[pltpu.VMEM((tm, tn), F32)],
        compiler_params=_params(("parallel", "parallel", "arbitrary")),
    )(*args)


def _rms_fwd(x, g, out_dtype=BF16, name="rms_fwd"):
    T, Fd = x.shape
    tm = _pick(T, 256)

    def body(x_ref, g_ref, o_ref):
        xv = x_ref[...]
        r = lax.rsqrt(jnp.mean(xv * xv, axis=-1, keepdims=True) + RMS_EPS)
        o_ref[...] = (xv * r * g_ref[...]).astype(out_dtype)

    return pl.pallas_call(
        body, name=name, grid=(T // tm,),
        in_specs=[pl.BlockSpec((tm, Fd), lambda i: (i, 0)), pl.BlockSpec((1, Fd), lambda i: (0, 0))],
        out_specs=pl.BlockSpec((tm, Fd), lambda i: (i, 0)),
        out_shape=jax.ShapeDtypeStruct((T, Fd), out_dtype),
        compiler_params=_params(("parallel",)),
    )(x, g)


def _rms_bwd(x, g, dy, dres=None, name="rms_bwd"):
    T, Fd = x.shape
    tm = _pick(T, 256)
    has_res = dres is not None

    def body(*refs):
        if has_res:
            x_ref, g_ref, dy_ref, r_ref, dx_ref, dxb_ref, dg_ref = refs
        else:
            x_ref, g_ref, dy_ref, dx_ref, dxb_ref, dg_ref = refs
        xv, dyv = x_ref[...], dy_ref[...]
        r = lax.rsqrt(jnp.mean(xv * xv, axis=-1, keepdims=True) + RMS_EPS)
        gdy = dyv * g_ref[...]
        dot = jnp.mean(xv * gdy, axis=-1, keepdims=True)
        dx = r * gdy - xv * (r * r * r * dot)
        if has_res:
            dx = dx + r_ref[...]
        dx_ref[...] = dx
        dxb_ref[...] = dx.astype(BF16)

        @pl.when(pl.program_id(0) == 0)
        def _():
            dg_ref[...] = jnp.zeros_like(dg_ref)

        dg_ref[...] += jnp.sum(dyv * xv * r, axis=0, keepdims=True)

    row = pl.BlockSpec((tm, Fd), lambda i: (i, 0))
    vec = pl.BlockSpec((1, Fd), lambda i: (0, 0))
    in_specs = [row, vec, row] + ([row] if has_res else [])
    args = (x, g, dy) + ((dres,) if has_res else ())
    return pl.pallas_call(
        body, name=name, grid=(T // tm,),
        in_specs=in_specs, out_specs=[row, row, vec],
        out_shape=[jax.ShapeDtypeStruct((T, Fd), F32), jax.ShapeDtypeStruct((T, Fd), BF16),
                   jax.ShapeDtypeStruct((1, Fd), F32)],
        compiler_params=_params(("arbitrary",)),
    )(*args)


def _mm_rms_bwd(a, b, x, g, dres, name="mm_rms_bwd"):
    T, K = a.shape
    Fd = b.shape[1]
    tm, tk = _pick(T, 1024), _pick(K, 1024)
    nk = K // tk

    def body(a_ref, b_ref, x_ref, g_ref, r_ref, dx_ref, dxb_ref, dg_ref, acc):
        i, k = pl.program_id(0), pl.program_id(1)

        @pl.when(k == 0)
        def _():
            acc[...] = jnp.zeros_like(acc)

        @pl.when((k == 0) & (i == 0))
        def _():
            dg_ref[...] = jnp.zeros_like(dg_ref)

        acc[...] += _dot(a_ref[...].astype(BF16), b_ref[...].astype(BF16), NN)

        @pl.when(k == nk - 1)
        def _():
            xv, dyv = x_ref[...], acc[...]
            r = lax.rsqrt(jnp.mean(xv * xv, axis=-1, keepdims=True) + RMS_EPS)
            gdy = dyv * g_ref[...]
            dot = jnp.mean(xv * gdy, axis=-1, keepdims=True)
            dx = r * gdy - xv * (r * r * r * dot) + r_ref[...]
            dx_ref[...] = dx
            dxb_ref[...] = dx.astype(BF16)
            dg_ref[...] += jnp.sum(dyv * xv * r, axis=0, keepdims=True)

    row = pl.BlockSpec((tm, Fd), lambda i, k: (i, 0))
    vec = pl.BlockSpec((1, Fd), lambda i, k: (0, 0))
    return pl.pallas_call(
        body, name=name, grid=(T // tm, nk),
        in_specs=[pl.BlockSpec((tm, tk), lambda i, k: (i, k)), pl.BlockSpec((tk, Fd), lambda i, k: (k, 0)),
                  row, vec, row],
        out_specs=[row, row, vec],
        out_shape=[jax.ShapeDtypeStruct((T, Fd), F32), jax.ShapeDtypeStruct((T, Fd), BF16),
                   jax.ShapeDtypeStruct((1, Fd), F32)],
        scratch_shapes=[pltpu.VMEM((tm, Fd), F32)],
        compiler_params=_params(("arbitrary", "arbitrary")),
    )(a, b, x, g, dres)


def _latent_bwd(proj, nq, nkv, dqa, dkn, dva, dkr, cos_k, sin_k, g_cq, g_ckv, w_uq_t, w_ukv_t, name="latent_bwd"):
    T = proj.shape[0]
    tm = _pick(T, 512)
    wl = _O2

    def rms_bwd(xv, gv, dyv):
        r = lax.rsqrt(jnp.mean(xv * xv, axis=-1, keepdims=True) + RMS_EPS)
        gdy = dyv * gv
        dot = jnp.mean(xv * gdy, axis=-1, keepdims=True)
        return r * gdy - xv * (r * r * r * dot), jnp.sum(dyv * xv * r, axis=0, keepdims=True)

    def body(p_ref, nq_ref, nkv_ref, dqa_ref, dkn_ref, dva_ref, dkr_ref, c_ref, s_ref, gq_ref, gkv_ref, wq_ref, wkv_ref,
             dlat_ref, dwq_ref, dwkv_ref, dgq_ref, dgkv_ref):
        @pl.when(pl.program_id(0) == 0)
        def _():
            for ref in (dwq_ref, dwkv_ref, dgq_ref, dgkv_ref):
                ref[...] = jnp.zeros_like(ref)

        dqv = dqa_ref[...]
        dkv = jnp.concatenate([dkn_ref[...], dva_ref[...]], axis=1)
        pv = p_ref[...]
        dc_q, dgq = rms_bwd(pv[:, :_O1], gq_ref[...], _dot(dqv, wq_ref[...], NN))
        dc_kv, dgkv = rms_bwd(pv[:, _O1:], gkv_ref[...], _dot(dkv, wkv_ref[...], NN))
        dkr_raw = _rotate(dkr_ref[...], c_ref[...], -s_ref[...])
        dlat_ref[...] = jnp.concatenate([dc_q, dc_kv, dkr_raw], axis=1).astype(BF16)
        dwq_ref[...] += _dot(dqv, nq_ref[...], TN)
        dwkv_ref[...] += _dot(dkv, nkv_ref[...], TN)
        dgq_ref[...] += dgq
        dgkv_ref[...] += dgkv

    row = lambda w: pl.BlockSpec((tm, w), lambda i: (i, 0))
    const = lambda a: pl.BlockSpec(a.shape, lambda i: (0, 0))
    outs = [jax.ShapeDtypeStruct((T, wl + LANES), BF16), jax.ShapeDtypeStruct(w_uq_t.shape, F32),
            jax.ShapeDtypeStruct(w_ukv_t.shape, F32), jax.ShapeDtypeStruct(g_cq.shape, F32),
            jax.ShapeDtypeStruct(g_ckv.shape, F32)]
    return pl.pallas_call(
        body, name=name, grid=(T // tm,),
        in_specs=[row(wl), row(_O1), row(_O2 - _O1), row(dqa.shape[1]), row(dkn.shape[1]), row(dva.shape[1]),
                  row(LANES), row(LANES), row(LANES), const(g_cq), const(g_ckv), const(w_uq_t), const(w_ukv_t)],
        out_specs=[row(wl + LANES)] + [const(o) for o in outs[1:]],
        out_shape=outs,
        compiler_params=_params(("arbitrary",)),
    )(proj, nq, nkv, dqa, dkn, dva, dkr, cos_k, sin_k, g_cq, g_ckv, w_uq_t, w_ukv_t)


def _loss_head(h, g, target, name="loss_head"):
    T, Fd = h.shape
    tm = _pick(T, 256)

    def body(h_ref, g_ref, t_ref, loss_ref, dh_ref, dhb_ref, dg_ref):
        xv = h_ref[...]
        r = lax.rsqrt(jnp.mean(xv * xv, axis=-1, keepdims=True) + RMS_EPS)
        diff = xv * r * g_ref[...] - t_ref[...]
        part = 0.5 * jnp.sum(jnp.mean(diff * diff, axis=-1, keepdims=True), axis=0, keepdims=True)
        dyv = diff * (1.0 / Fd)
        gdy = dyv * g_ref[...]
        dot = jnp.mean(xv * gdy, axis=-1, keepdims=True)
        dh = r * gdy - xv * (r * r * r * dot)
        dh_ref[...] = dh
        dhb_ref[...] = dh.astype(BF16)

        @pl.when(pl.program_id(0) == 0)
        def _():
            dg_ref[...] = jnp.zeros_like(dg_ref)
            loss_ref[...] = jnp.zeros_like(loss_ref)

        dg_ref[...] += jnp.sum(dyv * xv * r, axis=0, keepdims=True)
        loss_ref[...] += jnp.broadcast_to(part, loss_ref.shape)

    row = pl.BlockSpec((tm, Fd), lambda i: (i, 0))
    vec = pl.BlockSpec((1, Fd), lambda i: (0, 0))
    return pl.pallas_call(
        body, name=name, grid=(T // tm,),
        in_specs=[row, vec, row],
        out_specs=[pl.BlockSpec((1, LANES), lambda i: (0, 0)), row, row, vec],
        out_shape=[jax.ShapeDtypeStruct((1, LANES), F32), jax.ShapeDtypeStruct((T, Fd), F32),
                   jax.ShapeDtypeStruct((T, Fd), BF16), jax.ShapeDtypeStruct((1, Fd), F32)],
        compiler_params=_params(("arbitrary",)),
    )(h, g, target)


FFN_TF = 256


def _ffn_fwd(h, g, wg_t, wu_t, wd, name="ffn_fwd"):
    T, Dm = h.shape
    Fh = wd.shape[0]
    tm = _pick(T, 2048)
    nf = Fh // FFN_TF

    def body(h_ref, g_ref, wg_ref, wu_ref, wd_ref, o_ref, u_ref, a_ref, b_ref):
        j = pl.program_id(1)

        @pl.when(j == 0)
        def _():
            xv = h_ref[...]
            r = lax.rsqrt(jnp.mean(xv * xv, axis=-1, keepdims=True) + RMS_EPS)
            u_ref[...] = (xv * r * g_ref[...]).astype(BF16)
            o_ref[...] = xv

        u = u_ref[...]
        a = _dot(u, wg_ref[...], NT).astype(BF16)
        b = _dot(u, wu_ref[...], NT).astype(BF16)
        a_ref[...] = a
        b_ref[...] = b
        af = a.astype(F32)
        s = (af * jax.nn.sigmoid(af) * b.astype(F32)).astype(BF16)
        o_ref[...] += _dot(s, wd_ref[...], NN)

    row = pl.BlockSpec((tm, Dm), lambda i, j: (i, 0))
    wblk = pl.BlockSpec((FFN_TF, Dm), lambda i, j: (j, 0))
    ablk = pl.BlockSpec((tm, FFN_TF), lambda i, j: (i, j))
    return pl.pallas_call(
        body, name=name, grid=(T // tm, nf),
        in_specs=[pl.BlockSpec((tm, Dm), lambda i, j: (i, 0), pipeline_mode=pl.Buffered(1)),
                  pl.BlockSpec((1, Dm), lambda i, j: (0, 0)), wblk, wblk, wblk],
        out_specs=[row, row, ablk, ablk],
        out_shape=[jax.ShapeDtypeStruct((T, Dm), F32), jax.ShapeDtypeStruct((T, Dm), BF16),
                   jax.ShapeDtypeStruct((T, Fh), BF16), jax.ShapeDtypeStruct((T, Fh), BF16)],
        compiler_params=_params(("parallel", "arbitrary")),
    )(h, g, wg_t, wu_t, wd)


def _ffn_bwd(dh, u, a, b, wg_t, wu_t, wd, name="ffn_bwd"):
    T, Dm = dh.shape
    Fh = wd.shape[0]
    nf = Fh // FFN_TF
    once = pl.Buffered(1)

    def body(dh_ref, u_ref, a_ref, b_ref, wg_ref, wu_ref, wd_ref, du_ref, dwg_ref, dwu_ref, dwd_ref):
        j = pl.program_id(0)

        @pl.when(j == 0)
        def _():
            du_ref[...] = jnp.zeros_like(du_ref)

        ds = _dot(dh_ref[...], wd_ref[...], NT)
        af, bf = a_ref[...].astype(F32), b_ref[...].astype(F32)
        sig = jax.nn.sigmoid(af)
        sa = af * sig
        dwd_ref[...] = _dot((sa * bf).astype(BF16), dh_ref[...], TN).astype(BF16)
        dab = jnp.concatenate([(ds * bf * (sig * (1.0 + af * (1.0 - sig)))).astype(BF16),
                               (ds * sa).astype(BF16)], axis=1)
        dw = _dot(dab, u_ref[...], TN)
        dwg_ref[...] = dw[:FFN_TF].astype(BF16)
        dwu_ref[...] = dw[FFN_TF:].astype(BF16)
        du_ref[...] += _dot(dab, jnp.concatenate([wg_ref[...], wu_ref[...]], axis=0), NN)

    full = lambda: pl.BlockSpec((T, Dm), lambda j: (0, 0), pipeline_mode=once)
    wblk = pl.BlockSpec((FFN_TF, Dm), lambda j: (j, 0))
    ablk = pl.BlockSpec((T, FFN_TF), lambda j: (0, j))
    return pl.pallas_call(
        body, name=name, grid=(nf,),
        in_specs=[full(), full(), ablk, ablk, wblk, wblk, wblk],
        out_specs=[pl.BlockSpec((T, Dm), lambda j: (0, 0)), wblk, wblk, wblk],
        out_shape=[jax.ShapeDtypeStruct((T, Dm), F32)] + [jax.ShapeDtypeStruct((Fh, Dm), BF16)] * 3,
        compiler_params=_params(("arbitrary",)),
    )(dh, u, a, b, wg_t, wu_t, wd)


def _rope(x, cos_t, sin_t, col0, ncols, out_dtype, name="rope"):
    T = x.shape[0]
    wt = cos_t.shape[1]
    tm = _pick(T, 256)
    nb = ncols * LANES // wt
    half = MLA_ROPE // 2

    def body(x_ref, c_ref, s_ref, o_ref):
        xv = x_ref[...].astype(F32)
        lane = lax.broadcasted_iota(jnp.int32, xv.shape, 1)
        first = (lane & (MLA_ROPE - 1)) < half
        swapped = jnp.where(first, pltpu.roll(xv, wt - half, 1), pltpu.roll(xv, half, 1))
        o_ref[...] = (xv * c_ref[...] + swapped * s_ref[...]).astype(out_dtype)

    off = col0 * LANES // wt
    return pl.pallas_call(
        body, name=name, grid=(T // tm, nb),
        in_specs=[pl.BlockSpec((tm, wt), lambda i, j: (i, j + off)),
                  pl.BlockSpec((tm, wt), lambda i, j: (i, 0)),
                  pl.BlockSpec((tm, wt), lambda i, j: (i, 0))],
        out_specs=pl.BlockSpec((tm, wt), lambda i, j: (i, j)),
        out_shape=jax.ShapeDtypeStruct((T, ncols * LANES), out_dtype),
        compiler_params=_params(("parallel", "parallel")),
    )(x, cos_t, sin_t)


ATT_TQ = 512
ATT_TK = 256
MLA_TK = 512


def _mla_masks(shape):
    lane = lax.broadcasted_iota(jnp.int32, shape, 1)
    m0 = (lane < HEAD) | ((lane >= LANES) & (lane < LANES + MLA_ROPE))
    m1 = ((lane >= HEAD) & (lane < LANES)) | ((lane >= LANES + MLA_ROPE) & (lane < LANES + 2 * MLA_ROPE))
    return m0, m1


def _by_twos(n, step, carry):
    carry = lax.fori_loop(0, n // 2, lambda i, c: step(2 * i + 1, step(2 * i, c)), carry)
    return lax.fori_loop(0, n % 2, lambda _, c: step(n - 1, c), carry)


def _chunk_ok(tq, tk, d):
    row = lax.broadcasted_iota(jnp.int32, (tq, tk), 0)
    col = lax.broadcasted_iota(jnp.int32, (tq, tk), 1) + d * tk
    return jnp.concatenate([(col >> CHUNK_BITS) <= (row >> CHUNK_BITS)] * 2, axis=0)


def _rotate(x, cos_t, sin_t):
    half = MLA_ROPE // 2
    lane = lax.broadcasted_iota(jnp.int32, x.shape, 1)
    first = (lane & (MLA_ROPE - 1)) < half
    swapped = jnp.where(first, pltpu.roll(x, x.shape[1] - half, 1), pltpu.roll(x, half, 1))
    return x * cos_t + swapped * sin_t


def _mla_fwd(q, cos_q, sin_q, kv, kr, name="mla_fwd"):
    T = q.shape[0]
    tq, tk = _pick(T, ATT_TQ), _pick(T, MLA_TK)
    nd = tq // tk
    npair = MLA_HEADS // 2
    scale = (MLA_NOPE + MLA_ROPE) ** -0.5

    def body(q_ref, c_ref, s_ref, kn_ref, v_ref, kr_ref, o_ref, lse_ref):
        m_idx = pl.program_id(1)
        qv = _rotate(q_ref[...], c_ref[...], s_ref[...]).astype(BF16)
        m0, m1 = _mla_masks(qv.shape)
        qs = jnp.concatenate([jnp.where(m0, qv, 0), jnp.where(m1, qv, 0)], axis=0).astype(BF16)

        def block(kb, carry, ok):
            ks = pl.ds(pl.multiple_of(kb * tk, tk), tk)
            kcat = jnp.concatenate([kn_ref[ks, :], kr_ref[ks, :]], axis=1)
            mx, l, acc = carry
            s = _dot(qs, kcat, NT) * scale
            if ok is not None:
                s = jnp.where(ok, s, NEG)
            mn = jnp.maximum(mx, jnp.max(s, axis=-1, keepdims=True))
            alpha = jnp.exp(mx - mn)
            p = jnp.exp(s - mn)
            return (mn, alpha * l + jnp.sum(p, axis=-1, keepdims=True),
                    alpha * acc + _dot(p.astype(BF16), v_ref[ks, :], NN))

        init = (jnp.full((2 * tq, 1), NEG, F32), jnp.zeros((2 * tq, 1), F32), jnp.zeros((2 * tq, LANES), F32))
        res = init
        for d in range(nd):
            res = block(m_idx * nd + d, res, _chunk_ok(tq, tk, d))
        mx, l, acc = _by_twos(m_idx * nd, lambda kb, c: block(kb, c, None), res)
        h0 = lax.broadcasted_iota(jnp.int32, (tq, LANES), 1) < HEAD
        o_ref[...] = _two_heads(acc * (1.0 / l), h0).astype(o_ref.dtype)
        lse_ref[...] = _two_heads(jnp.broadcast_to(mx + jnp.log(l), (2 * tq, LANES)), h0)

    full = lambda col: pl.BlockSpec((T, LANES), col)
    table = pl.BlockSpec((tq, 2 * LANES), lambda p, m: (m, 0))
    return pl.pallas_call(
        body, name=name, grid=(npair, T // tq),
        in_specs=[pl.BlockSpec((tq, 2 * LANES), lambda p, m: (m, p)), table, table,
                  full(lambda p, m: (0, p)), full(lambda p, m: (0, npair + p)), full(lambda p, m: (0, 0))],
        out_specs=[pl.BlockSpec((tq, LANES), lambda p, m: (m, p)),
                   pl.BlockSpec((tq, LANES), lambda p, m: (m, p))],
        out_shape=[jax.ShapeDtypeStruct((T, npair * LANES), BF16),
                   jax.ShapeDtypeStruct((T, npair * LANES), F32)],
        compiler_params=_params(("parallel", "arbitrary")),
    )(q, cos_q, sin_q, kv, kv, kr)


def _mla_bwd(q, cos_q, sin_q, kv, kr, o, lse, do, do_col0, name="mla_bwd"):
    T = q.shape[0]
    tq, tk = _pick(T, ATT_TQ), _pick(T, MLA_TK)
    nd = tq // tk
    npair = MLA_HEADS // 2
    scale = (MLA_NOPE + MLA_ROPE) ** -0.5

    def body(q_ref, c_ref, s_ref, kn_ref, v_ref, kr_ref, o_ref, lse_ref, do_ref, dq_ref, dkn_ref, dv_ref, dkr_ref,
             dkn_acc, dv_acc):
        p_idx, m_idx = pl.program_id(0), pl.program_id(1)

        @pl.when(m_idx == 0)
        def _():
            dkn_acc[...] = jnp.zeros_like(dkn_acc)
            dv_acc[...] = jnp.zeros_like(dv_acc)

        @pl.when((m_idx == 0) & (p_idx == 0))
        def _():
            dkr_ref[...] = jnp.zeros_like(dkr_ref)

        qv = _rotate(q_ref[...], c_ref[...], s_ref[...]).astype(BF16)
        m0, m1 = _mla_masks(qv.shape)
        qs = jnp.concatenate([jnp.where(m0, qv, 0), jnp.where(m1, qv, 0)], axis=0).astype(BF16)
        dov = do_ref[...].astype(F32)
        h0 = lax.broadcasted_iota(jnp.int32, (tq, LANES), 1) < HEAD
        dos32 = jnp.concatenate([jnp.where(h0, dov, 0.0), jnp.where(h0, 0.0, dov)], axis=0)
        ov = o_ref[...].astype(F32)
        delta = jnp.sum(dos32 * jnp.concatenate([ov, ov], axis=0), axis=-1, keepdims=True)
        dos = dos32.astype(BF16)
        lsev = lse_ref[...]
        lse = jnp.concatenate([lsev[:, 0:1], lsev[:, HEAD:HEAD + 1]], axis=0)

        def block(kb, dq, ok):
            ks = pl.ds(pl.multiple_of(kb * tk, tk), tk)
            kcat = jnp.concatenate([kn_ref[ks, :], kr_ref[ks, :]], axis=1)
            vv = v_ref[ks, :]
            p = jnp.exp(_dot(qs, kcat, NT) * scale - lse)
            if ok is not None:
                p = jnp.where(ok, p, 0.0)
            ds = (p * (_dot(dos, vv, NT) - delta) * scale).astype(BF16)
            dkc = _dot(ds, qs, TN)
            dkn_acc[ks, :] += dkc[:, :LANES]
            dkr_ref[ks, :] += dkc[:, LANES:]
            dv_acc[ks, :] += _dot(p.astype(BF16), dos, TN)
            return dq + _dot(ds, kcat, NN)

        dq = jnp.zeros((2 * tq, 2 * LANES), F32)
        for d in range(nd):
            dq = block(m_idx * nd + d, dq, _chunk_ok(tq, tk, d))
        dq = _by_twos(m_idx * nd, lambda kb, c: block(kb, c, None), dq)
        dq_ref[...] = _rotate(jnp.where(m0, dq[:tq], jnp.where(m1, dq[tq:], 0.0)), c_ref[...],
                              -s_ref[...]).astype(BF16)

        @pl.when(m_idx == T // tq - 1)
        def _():
            dkn_ref[...] = dkn_acc[...].astype(BF16)
            dv_ref[...] = dv_acc[...].astype(BF16)

    full = lambda col: pl.BlockSpec((T, LANES), col)
    blk = lambda col: pl.BlockSpec((tq, LANES), col)
    table = pl.BlockSpec((tq, 2 * LANES), lambda p, m: (m, 0))
    return pl.pallas_call(
        body, name=name, grid=(npair, T // tq),
        in_specs=[pl.BlockSpec((tq, 2 * LANES), lambda p, m: (m, p)), table, table,
                  full(lambda p, m: (0, p)), full(lambda p, m: (0, npair + p)), full(lambda p, m: (0, 0)),
                  blk(lambda p, m: (m, p)), blk(lambda p, m: (m, p)),
                  blk(lambda p, m: (m, do_col0 + p))],
        out_specs=[pl.BlockSpec((tq, 2 * LANES), lambda p, m: (m, p)),
                   full(lambda p, m: (0, p)), full(lambda p, m: (0, p)), full(lambda p, m: (0, 0))],
        out_shape=[jax.ShapeDtypeStruct((T, npair * 2 * LANES), BF16),
                   jax.ShapeDtypeStruct((T, npair * LANES), BF16),
                   jax.ShapeDtypeStruct((T, npair * LANES), BF16),
                   jax.ShapeDtypeStruct((T, LANES), F32)],
        scratch_shapes=[pltpu.VMEM((T, LANES), F32)] * 2,
        compiler_params=_params(("arbitrary", "arbitrary")),
    )(q, cos_q, sin_q, kv, kv, kr, o, lse, do)


def _split_dot(x, tri):
    hi = x.astype(BF16)
    lo = (x - hi.astype(F32)).astype(BF16)
    both = _dot(jnp.concatenate([hi, lo], axis=0), tri, NN)
    return both[:x.shape[0]] + both[x.shape[0]:]


def _sb_terms(qh, kk, before):
    z = _dot(qh, kk, NT)
    sp = jnp.maximum(z, 0.0) + jnp.log(1.0 + jnp.exp(-jnp.abs(z)))
    lk = -sp if before is None else jnp.where(before, -sp, 0.0)
    return z, sp, lk


def _sb_setup(q_ref, tq, tk, scale):
    qv = (q_ref[...].astype(F32) * scale).astype(BF16)
    lane = lax.broadcasted_iota(jnp.int32, (tq, LANES), 1)
    h0 = lane < HEAD
    qs = jnp.concatenate([jnp.where(h0, qv, 0), jnp.where(h0, 0, qv)], axis=0).astype(BF16)
    row = lax.broadcasted_iota(jnp.int32, (tk, tk), 0)
    col = lax.broadcasted_iota(jnp.int32, (tk, tk), 1)
    return qs, h0, row, col


def _sb_before(tq, tk, d):
    row = lax.broadcasted_iota(jnp.int32, (tq, tk), 0)
    col = lax.broadcasted_iota(jnp.int32, (tq, tk), 1) + d * tk
    return jnp.concatenate([col < row] * 2, axis=0)


def _two_heads(x, h0):
    tq = x.shape[0] // 2
    return jnp.where(h0, x[:tq], x[tq:])


def _sb_fwd(qkv, col0, dep, name="sb_fwd"):
    T = qkv.shape[0]
    tq, tk = _pick(T, ATT_TQ), _pick(T, ATT_TK)
    nd = tq // tk
    npair = SB_HEADS // 2
    scale = SB_DIM ** -0.5

    def body(q_ref, k_ref, v_ref, dep_ref, o_ref, o32_ref, w_ref, sp_ref):
        m_idx = pl.program_id(1)
        qs, h0, row, col = _sb_setup(q_ref, tq, tk, scale)
        later = (row > col).astype(BF16)

        def block(kb, carry, before):
            ks = pl.ds(pl.multiple_of(kb * tk, tk), tk)
            c, acc = carry
            z, sp, lk = _sb_terms(qs, k_ref[ks, :].astype(BF16), before)
            w = jnp.exp((z - sp) + _split_dot(lk, later) + c)
            if before is not None:
                w = jnp.where(before, w, 0.0)
            wb = w.astype(BF16)
            w_ref[0, 0, kb] = wb
            sp_ref[0, 0, kb] = sp.astype(BF16)
            return (c + jnp.sum(lk, axis=-1, keepdims=True), acc + _dot(wb, v_ref[ks, :].astype(BF16), NN))

        init = (jnp.zeros((2 * tq, 1), F32), jnp.zeros((2 * tq, LANES), F32))
        res = init
        for d in reversed(range(nd)):
            res = block(m_idx * nd + d, res, _sb_before(tq, tk, d))
        res = _by_twos(m_idx * nd, lambda i, c: block(m_idx * nd - 1 - i, c, None), res)
        o = _two_heads(res[1], h0)
        o_ref[...] = o.astype(o_ref.dtype)
        o32_ref[...] = o

    full = lambda col: pl.BlockSpec((T, LANES), col)
    blk = pl.BlockSpec((tq, LANES), lambda p, m: (m, p))
    return pl.pallas_call(
        body, name=name, grid=(npair, T // tq),
        in_specs=[pl.BlockSpec((tq, LANES), lambda p, m: (m, col0 + p)),
                  full(lambda p, m: (0, col0 + npair + p)), full(lambda p, m: (0, col0 + 2 * npair + p)),
                  pl.BlockSpec((8, LANES), lambda p, m: (0, 0))],
        out_specs=[blk, blk] + [pl.BlockSpec((1, 1, T // tk, 2 * tq, tk), lambda p, m: (p, m, 0, 0, 0))] * 2,
        out_shape=[jax.ShapeDtypeStruct((T, npair * LANES), BF16), jax.ShapeDtypeStruct((T, npair * LANES), F32)]
        + [jax.ShapeDtypeStruct((npair, T // tq, T // tk, 2 * tq, tk), BF16)] * 2,
        compiler_params=_params(("parallel", "arbitrary")),
    )(qkv, qkv, qkv, dep)


def _sb_bwd(qkv, col0, o32, w_all, sp_all, do, do_col0, dep, name="sb_bwd"):
    T = qkv.shape[0]
    tq, tk = _pick(T, ATT_TQ), _pick(T, ATT_TK)
    nd = tq // tk
    npair = SB_HEADS // 2
    scale = SB_DIM ** -0.5

    def body(q_ref, k_ref, v_ref, o_ref, w_ref, sp_ref, do_ref, dep_ref, dq_ref, dk_ref, dv_ref, dk_acc, dv_acc):
        m_idx = pl.program_id(1)

        @pl.when(m_idx == 0)
        def _():
            dk_acc[...] = jnp.zeros_like(dk_acc)
            dv_acc[...] = jnp.zeros_like(dv_acc)

        qs, h0, row, col = _sb_setup(q_ref, tq, tk, scale)
        dov = do_ref[...].astype(F32)
        dos = jnp.concatenate([jnp.where(h0, dov, 0.0), jnp.where(h0, 0.0, dov)], axis=0).astype(BF16)
        ov = o_ref[...]
        etot = jnp.sum(dos.astype(F32) * jnp.concatenate([ov, ov], axis=0), axis=-1, keepdims=True)
        from_here = (row >= col).astype(BF16)

        def block(kb, carry, before):
            ks = pl.ds(pl.multiple_of(kb * tk, tk), tk)
            kk = k_ref[ks, :].astype(BF16)
            vv = v_ref[ks, :].astype(BF16)
            es, dqa = carry
            wb = w_ref[0, 0, kb]
            e = wb.astype(F32) * _dot(dos, vv, NT)
            prev = etot - (_split_dot(e, from_here) + es)
            sig_neg = jnp.exp(-sp_ref[0, 0, kb].astype(F32))
            dz = e * sig_neg - (1.0 - sig_neg) * prev
            if before is not None:
                dz = jnp.where(before, dz, 0.0)
            dzb = dz.astype(BF16)
            dk_acc[ks, :] += _dot(dzb, qs, TN)
            dv_acc[ks, :] += _dot(wb, dos, TN)
            return es + jnp.sum(e, axis=-1, keepdims=True), dqa + _dot(dzb, kk, NN)

        init = (jnp.zeros((2 * tq, 1), F32), jnp.zeros((2 * tq, LANES), F32))
        res = init
        for d in reversed(range(nd)):
            res = block(m_idx * nd + d, res, _sb_before(tq, tk, d))
        res = _by_twos(m_idx * nd, lambda i, c: block(m_idx * nd - 1 - i, c, None), res)
        dq_ref[...] = (_two_heads(res[1], h0) * scale).astype(BF16)

        @pl.when(m_idx == T // tq - 1)
        def _():
            dk_ref[...] = dk_acc[...].astype(BF16)
            dv_ref[...] = dv_acc[...].astype(BF16)

    full = lambda col: pl.BlockSpec((T, LANES), col)
    blk = lambda col: pl.BlockSpec((tq, LANES), col)
    return pl.pallas_call(
        body, name=name, grid=(npair, T // tq),
        in_specs=[blk(lambda p, m: (m, col0 + p)),
                  full(lambda p, m: (0, col0 + npair + p)), full(lambda p, m: (0, col0 + 2 * npair + p)),
                  blk(lambda p, m: (m, p)),
                  pl.BlockSpec((1, 1, T // tk, 2 * tq, tk), lambda p, m: (p, m, 0, 0, 0)),
                  pl.BlockSpec((1, 1, T // tk, 2 * tq, tk), lambda p, m: (p, m, 0, 0, 0)),
                  blk(lambda p, m: (m, do_col0 + p)), pl.BlockSpec((8, LANES), lambda p, m: (0, 0))],
        out_specs=[blk(lambda p, m: (m, p)), full(lambda p, m: (0, p)), full(lambda p, m: (0, p))],
        out_shape=[jax.ShapeDtypeStruct((T, npair * LANES), BF16)] * 3,
        scratch_shapes=[pltpu.VMEM((T, LANES), F32)] * 2,
        compiler_params=_params(("arbitrary", "arbitrary")),
    )(qkv, qkv, qkv, o32, w_all, sp_all, do, dep)


def _band_in_window():
    cq = lax.broadcasted_iota(jnp.int32, (BAND_TQ, BAND_W), 0) >> CHUNK_BITS
    ckp = lax.broadcasted_iota(jnp.int32, (BAND_TQ, BAND_W), 1) >> CHUNK_BITS
    return (ckp >= cq) & (ckp <= cq + LEFT_CHUNKS)


def _band_real(m_idx):
    j = lax.broadcasted_iota(jnp.int32, (BAND_TQ, BAND_W), 1)
    return j >= PAD_KEYS - m_idx * BAND_TQ


def _band_probs(qh, kw, bias, real, scale):
    s = jnp.where(real, _dot(qh, kw, NT) * scale + bias, NEG)
    e = jnp.exp(s - jnp.max(s, axis=-1, keepdims=True))
    return e * (1.0 / jnp.sum(e, axis=-1, keepdims=True))


BAND_SUB = 16


def _band_fwd(qkv, k_pad, v_pad, bias_w, name="band_fwd"):
    T = qkv.shape[0]
    npair = C_HEADS // 2
    scale = C_DIM ** -0.5
    rows = BAND_SUB * BAND_TQ

    def body(q_ref, k_ref, v_ref, b_ref, o_ref, p_ref):
        lane = lax.broadcasted_iota(jnp.int32, (BAND_TQ, LANES), 1)
        h0 = lane < HEAD
        bias = jnp.concatenate([b_ref[0], b_ref[1]], axis=0)
        for sub in range(BAND_SUB):
            m_idx = pl.program_id(1) * BAND_SUB + sub
            win = pl.ds(pl.multiple_of(m_idx * BAND_TQ, BAND_TQ), BAND_W)
            kw, vw = k_ref[win, :], v_ref[win, :]
            qv = q_ref[sub * BAND_TQ:(sub + 1) * BAND_TQ, :]
            qs = jnp.concatenate([jnp.where(h0, qv, 0), jnp.where(h0, 0, qv)], axis=0).astype(BF16)
            p = _band_probs(qs, kw, bias, jnp.concatenate([_band_real(m_idx)] * 2, axis=0), scale).astype(BF16)
            p_ref[0, sub] = p
            o = _two_heads(_dot(p, vw, NN), h0)
            o_ref[sub * BAND_TQ:(sub + 1) * BAND_TQ, :] = o.astype(o_ref.dtype)

    Tp = T + PAD_KEYS
    return pl.pallas_call(
        body, name=name, grid=(npair, T // rows),
        in_specs=[pl.BlockSpec((rows, LANES), lambda p, m: (m, p)),
                  pl.BlockSpec((Tp, LANES), lambda p, m: (0, p)),
                  pl.BlockSpec((Tp, LANES), lambda p, m: (0, p)),
                  pl.BlockSpec((2, BAND_TQ, BAND_W), lambda p, m: (p, 0, 0))],
        out_specs=[pl.BlockSpec((rows, LANES), lambda p, m: (m, p)),
                   pl.BlockSpec((1, BAND_SUB, 2 * BAND_TQ, BAND_W), lambda p, m: (p, m, 0, 0))],
        out_shape=[jax.ShapeDtypeStruct((T, npair * LANES), BF16),
                   jax.ShapeDtypeStruct((npair, T // BAND_TQ, 2 * BAND_TQ, BAND_W), BF16)],
        compiler_params=_params(("parallel", "arbitrary")),
    )(qkv, k_pad, v_pad, bias_w)


def _band_bwd(qkv, k_pad, v_pad, probs, do, name="band_bwd"):
    T = qkv.shape[0]
    npair = C_HEADS // 2
    scale = C_DIM ** -0.5

    rows = BAND_SUB * BAND_TQ

    def body(q_ref, k_ref, v_ref, p_ref, do_ref, dq_ref, dk_ref, dv_ref, db_ref, dk_acc, dv_acc):
        @pl.when(pl.program_id(1) == 0)
        def _():
            dk_acc[...] = jnp.zeros_like(dk_acc)
            dv_acc[...] = jnp.zeros_like(dv_acc)
            db_ref[...] = jnp.zeros_like(db_ref)

        lane = lax.broadcasted_iota(jnp.int32, (BAND_TQ, LANES), 1)
        h0 = lane < HEAD
        dbs = jnp.zeros((2 * BAND_TQ, BAND_W), F32)
        for sub in range(BAND_SUB):
            m_idx = pl.program_id(1) * BAND_SUB + sub
            win = pl.ds(pl.multiple_of(m_idx * BAND_TQ, BAND_TQ), BAND_W)
            kw, vw = k_ref[win, :], v_ref[win, :]
            qv = q_ref[sub * BAND_TQ:(sub + 1) * BAND_TQ, :]
            dov = do_ref[sub * BAND_TQ:(sub + 1) * BAND_TQ, :].astype(F32)
            qs = jnp.concatenate([jnp.where(h0, qv, 0), jnp.where(h0, 0, qv)], axis=0).astype(BF16)
            dos = jnp.concatenate([jnp.where(h0, dov, 0.0), jnp.where(h0, 0.0, dov)], axis=0).astype(BF16)
            pb = p_ref[0, sub]
            p = pb.astype(F32)
            dp = _dot(dos, vw, NT)
            dsb = p * (dp - jnp.sum(p * dp, axis=-1, keepdims=True))
            dbs = dbs + dsb
            dsq = (dsb * scale).astype(BF16)
            dq_ref[sub * BAND_TQ:(sub + 1) * BAND_TQ, :] = _two_heads(_dot(dsq, kw, NN), h0).astype(BF16)
            dk_acc[win, :] += _dot(dsq, qs, TN)
            dv_acc[win, :] += _dot(pb, dos, TN)
        db_ref[0] += dbs[:BAND_TQ]
        db_ref[1] += dbs[BAND_TQ:]

        @pl.when(pl.program_id(1) == T // rows - 1)
        def _():
            dk_ref[...] = dk_acc[...].astype(BF16)
            dv_ref[...] = dv_acc[...].astype(BF16)

    Tp = T + PAD_KEYS
    blk = lambda col: pl.BlockSpec((rows, LANES), col)
    full = pl.BlockSpec((Tp, LANES), lambda p, m: (0, p))
    bias = pl.BlockSpec((2, BAND_TQ, BAND_W), lambda p, m: (p, 0, 0))
    prob = pl.BlockSpec((1, BAND_SUB, 2 * BAND_TQ, BAND_W), lambda p, m: (p, m, 0, 0))
    return pl.pallas_call(
        body, name=name, grid=(npair, T // rows),
        in_specs=[blk(lambda p, m: (m, p)), full, full, prob, blk(lambda p, m: (m, p))],
        out_specs=[blk(lambda p, m: (m, p)), full, full, bias],
        out_shape=[jax.ShapeDtypeStruct((T, npair * LANES), BF16),
                   jax.ShapeDtypeStruct((Tp, npair * LANES), BF16),
                   jax.ShapeDtypeStruct((Tp, npair * LANES), BF16),
                   jax.ShapeDtypeStruct((C_HEADS, BAND_TQ, BAND_W), F32)],
        scratch_shapes=[pltpu.VMEM((Tp, LANES), F32)] * 2,
        compiler_params=_params(("arbitrary", "arbitrary")),
    )(qkv, k_pad, v_pad, probs, do)


def _skew_bits(x, left):
    w = x.shape[1]
    row = lax.broadcasted_iota(jnp.int32, x.shape, 0)
    for b in range(BAND_TQ.bit_length() - 1):
        amt = (w - (1 << b)) if left else (1 << b)
        x = jnp.where((row >> b) & 1 == 1, pltpu.roll(x, amt, 1), x)
    return x


def _toeplitz(diag, name="toeplitz"):
    H = diag.shape[0]

    def body(d_ref, o_ref):
        x = jnp.broadcast_to(d_ref[0], (BAND_TQ, TOEP_W))
        o_ref[0] = jnp.where(_band_in_window(), _skew_bits(x, left=False)[:, BAND_TQ:], NEG)

    return pl.pallas_call(
        body, name=name, grid=(H,),
        in_specs=[pl.BlockSpec((1, 1, TOEP_W), lambda h: (h, 0, 0))],
        out_specs=pl.BlockSpec((1, BAND_TQ, BAND_W), lambda h: (h, 0, 0)),
        out_shape=jax.ShapeDtypeStruct((H, BAND_TQ, BAND_W), F32),
        compiler_params=_params(("parallel",)),
    )(diag.reshape(H, 1, TOEP_W))


def _toeplitz_bwd(dbias, name="toeplitz_bwd"):
    H = dbias.shape[0]

    def body(d_ref, o_ref):
        x = jnp.concatenate([jnp.zeros((BAND_TQ, BAND_TQ), F32), d_ref[0]], axis=1)
        h = BAND_TQ // 2
        while h >= 8:
            x = x[:h] + pltpu.roll(x[h:2 * h], TOEP_W - h, 1)
            h //= 2
        o_ref[0] = jnp.sum(_skew_bits(x, left=True), axis=0, keepdims=True)

    return pl.pallas_call(
        body, name=name, grid=(H,),
        in_specs=[pl.BlockSpec((1, BAND_TQ, BAND_W), lambda h: (h, 0, 0))],
        out_specs=pl.BlockSpec((1, 1, TOEP_W), lambda h: (h, 0, 0)),
        out_shape=jax.ShapeDtypeStruct((H, 1, TOEP_W), F32),
        compiler_params=_params(("parallel",)),
    )(dbias).reshape(H, TOEP_W)


_HBM = pl.BlockSpec(memory_space=pltpu.HBM)
_SEM = pl.BlockSpec(memory_space=pltpu.SEMAPHORE)
_EFFECT = pltpu.SideEffectType.DATAFLOW_SIDE_EFFECTING


def _peers():
    x, y, c = lax.axis_index("x"), lax.axis_index("y"), lax.axis_index("c")
    out = []
    for k in range(1, N_DEV):
        peer = (1 - x if (k >> 2) & 1 else x, 1 - y if (k >> 1) & 1 else y, 1 - c if k & 1 else c)
        out.append((peer, 4 * peer[0] + 2 * peer[1] + peer[2]))
    return 4 * x + 2 * y + c, out


def _split_copies(ins, lands, scatter, send_sem, recv_sem, arriving):
    me, peers = _peers()
    out = []
    for a in range(len(ins)):
        for peer, idx in peers:
            out.append(pltpu.make_async_remote_copy(
                src_ref=ins[a].at[idx] if scatter[a] else ins[a],
                dst_ref=lands[a].at[idx if arriving else me], send_sem=send_sem, recv_sem=recv_sem,
                device_id=peer, device_id_type=pl.DeviceIdType.MESH))
    return out


def _landing_zones(arrays, scatter):
    return [lax.empty((N_DEV,) + (a.shape[1:] if s else a.shape), a.dtype) for a, s in zip(arrays, scatter)]


def _place_own(arrays, scatter, name):
    n = len(arrays)
    lands = _landing_zones(arrays, scatter)
    me = (4 * lax.axis_index("x") + 2 * lax.axis_index("y") + lax.axis_index("c")).astype(jnp.int32).reshape(1)

    def body(me_ref, *refs):
        for a in range(n):
            refs[2 * n + a][...] = refs[a][...].reshape(refs[2 * n + a].shape)

    def row_spec(shape):
        zeros = (0,) * (len(shape) - 1)
        return pl.BlockSpec((1,) + tuple(shape[1:]), lambda i, me_ref: (me_ref[0],) + zeros)

    in_specs = [row_spec(a.shape) if s else pl.BlockSpec(a.shape, lambda i, me_ref, nd=a.ndim: (0,) * nd)
                for a, s in zip(arrays, scatter)]
    return pl.pallas_call(
        body, name=name,
        out_shape=[jax.ShapeDtypeStruct(l.shape, l.dtype) for l in lands],
        grid_spec=pltpu.PrefetchScalarGridSpec(
            num_scalar_prefetch=1, grid=(1,),
            in_specs=in_specs + [pl.BlockSpec(memory_space=pl.ANY)] * n,
            out_specs=[row_spec(l.shape) for l in lands]),
        input_output_aliases={1 + n + i: i for i in range(n)},
        compiler_params=_params(("arbitrary",)),
    )(me, *arrays, *lands)


def _exchange_start_groups(groups, scatter, after, name, lands=None):
    sizes = [len(g) for g in groups]
    arrays = [a for g in groups for a in g]
    n, ng = len(arrays), len(groups)
    flags = list(scatter) if isinstance(scatter, (list, tuple)) else [scatter] * n
    if lands is None:
        lands = list(_place_own(arrays, flags, name=name.replace("_start_", "_own_")))
    else:
        lands = [l for g in lands for l in g]
    starts = np.cumsum([0] + sizes)

    def body(*refs):
        ins, lnd = refs[:n], refs[n:2 * n]
        sems = refs[2 * n + 1:2 * n + 1 + 2 * ng]
        token = refs[-1]
        for g in range(ng):
            sl = slice(starts[g], starts[g + 1])
            for cp in _split_copies(ins[sl], lnd[sl], flags[sl], sems[2 * g], sems[2 * g + 1], arriving=False):
                cp.start()
        token[...] = jnp.zeros_like(token)

    hbm = lambda a: pltpu.HBM(a.shape, a.dtype)
    out = pl.pallas_call(
        body, name=name,
        out_shape=(*[pltpu.SemaphoreType.DMA(())] * (2 * ng),
                   *[hbm(a) for a in arrays], *[hbm(a) for a in lands],
                   jax.ShapeDtypeStruct((8, LANES), F32)),
        in_specs=[_HBM] * (2 * n) + [pl.BlockSpec(memory_space=pl.ANY)],
        out_specs=(*[_SEM] * (2 * ng), *([_HBM] * (2 * n)), pl.BlockSpec(memory_space=pltpu.VMEM)),
        input_output_aliases={i: 2 * ng + i for i in range(2 * n)},
        compiler_params=pltpu.CompilerParams(has_side_effects=_EFFECT),
    )(*[pltpu.with_memory_space_constraint(a, pltpu.HBM) for a in list(arrays) + lands], after)
    ins_out, lands_out = out[2 * ng:2 * ng + n], out[2 * ng + n:2 * ng + 2 * n]
    handles = [(out[2 * g], out[2 * g + 1], list(ins_out[starts[g]:starts[g + 1]]),
                list(lands_out[starts[g]:starts[g + 1]]), tuple(flags[starts[g]:starts[g + 1]]))
               for g in range(ng)]
    return handles, out[-1]


def _exchange_start(arrays, scatter, after, name):
    handles, token = _exchange_start_groups([list(arrays)], list(scatter), after, name)
    return handles[0], token


def _exchange_wait(handle, after, name):
    send_sem, recv_sem, ins, lands, scatter = handle
    n = len(ins)
    after = after if isinstance(after, tuple) else (after,)

    def body(*refs):
        i_ref, l_ref = refs[:n], refs[n:2 * n]
        s_sem, r_sem = refs[2 * n:2 * n + 2]
        for cp in _split_copies(i_ref, l_ref, scatter, s_sem, r_sem, arriving=False):
            cp.wait_send()
        for cp in _split_copies(i_ref, l_ref, scatter, s_sem, r_sem, arriving=True):
            cp.wait_recv()

    hbm = lambda a: pltpu.HBM(a.shape, a.dtype)
    out = pl.pallas_call(
        body, name=name,
        out_shape=tuple(hbm(a) for a in ins + lands),
        in_specs=[_HBM] * (2 * n) + [_SEM, _SEM] + [pl.BlockSpec(memory_space=pl.ANY)] * len(after),
        out_specs=tuple([_HBM] * (2 * n)),
        input_output_aliases={i: i for i in range(2 * n)},
        compiler_params=pltpu.CompilerParams(has_side_effects=_EFFECT),
    )(*ins, *lands, send_sem, recv_sem, *after)
    return list(out[n:])


_SIBLING = 1
_CHIPS = (4, 2, 6)


def _peer_of(k):
    x, y, c = lax.axis_index("x"), lax.axis_index("y"), lax.axis_index("c")
    peer = (1 - x if (k >> 2) & 1 else x, 1 - y if (k >> 1) & 1 else y, 1 - c if k & 1 else c)
    return peer, 4 * peer[0] + 2 * peer[1] + peer[2]


def _rcopy(src, dst, send_sem, recv_sem, k):
    return pltpu.make_async_remote_copy(src_ref=src, dst_ref=dst, send_sem=send_sem, recv_sem=recv_sem,
                                        device_id=_peer_of(k)[0], device_id_type=pl.DeviceIdType.MESH)


def _gather2_start(groups, lands, after, name):
    sizes = [len(g) for g in groups]
    arrays = [a for g in groups for a in g]
    lands = [l for g in lands for l in g]
    n, ng = len(arrays), len(groups)
    starts = np.cumsum([0] + sizes)

    def body(*refs):
        ins, lnd = refs[:n], refs[n:2 * n]
        sems = refs[2 * n + 1:2 * n + 1 + 4 * ng]
        me, _ = _peers()
        for g in range(ng):
            send_d, recv_d, send_i, recv_i = sems[4 * g:4 * g + 4]
            for a in range(starts[g], starts[g + 1]):
                for k in _CHIPS:
                    _rcopy(ins[a], lnd[a].at[me], send_i, recv_i, k).start()
                _rcopy(ins[a], lnd[a].at[me], send_d, recv_d, _SIBLING).start()
        refs[-1][...] = jnp.zeros_like(refs[-1])

    hbm = lambda a: pltpu.HBM(a.shape, a.dtype)
    out = pl.pallas_call(
        body, name=name,
        out_shape=(*[pltpu.SemaphoreType.DMA(())] * (4 * ng), *[hbm(a) for a in arrays], *[hbm(a) for a in lands],
                   jax.ShapeDtypeStruct((8, LANES), F32)),
        in_specs=[_HBM] * (2 * n) + [pl.BlockSpec(memory_space=pl.ANY)],
        out_specs=(*[_SEM] * (4 * ng), *([_HBM] * (2 * n)), pl.BlockSpec(memory_space=pltpu.VMEM)),
        input_output_aliases={i: 4 * ng + i for i in range(2 * n)},
        compiler_params=pltpu.CompilerParams(has_side_effects=_EFFECT),
    )(*[pltpu.with_memory_space_constraint(a, pltpu.HBM) for a in arrays + lands], after)
    ins_out, lands_out = out[4 * ng:4 * ng + n], out[4 * ng + n:4 * ng + 2 * n]
    handles = [dict(sems=out[4 * g:4 * g + 4], ins=list(ins_out[starts[g]:starts[g + 1]]),
                    lands=list(lands_out[starts[g]:starts[g + 1]])) for g in range(ng)]
    return handles, out[-1]


def _gather2_pass_on(handle, after, name):
    lands, recv_i = handle["lands"], handle["sems"][3]
    n = len(lands)
    after = after if isinstance(after, tuple) else (after,)

    def body(*refs):
        lnd, r_i = refs[:n], refs[n]
        send_f, recv_f = refs[n + 1 + len(after):n + 3 + len(after)]
        for a in range(n):
            for k in _CHIPS:
                row = _peer_of(k)[1]
                _rcopy(lnd[a].at[row], lnd[a].at[row], send_f, r_i, k).wait_recv()
        for a in range(n):
            for k in _CHIPS:
                row = _peer_of(k)[1]
                _rcopy(lnd[a].at[row], lnd[a].at[row], send_f, recv_f, _SIBLING).start()
        refs[-1][...] = jnp.zeros_like(refs[-1])

    hbm = lambda a: pltpu.HBM(a.shape, a.dtype)
    out = pl.pallas_call(
        body, name=name,
        out_shape=(pltpu.SemaphoreType.DMA(()), pltpu.SemaphoreType.DMA(()), *[hbm(a) for a in lands],
                   jax.ShapeDtypeStruct((8, LANES), F32)),
        in_specs=[_HBM] * n + [_SEM] + [pl.BlockSpec(memory_space=pl.ANY)] * len(after),
        out_specs=(_SEM, _SEM, *([_HBM] * n), pl.BlockSpec(memory_space=pltpu.VMEM)),
        input_output_aliases={i: 2 + i for i in range(n)},
        compiler_params=pltpu.CompilerParams(has_side_effects=_EFFECT),
    )(*lands, recv_i, *after)
    return dict(handle, lands=list(out[2:2 + n]), passed=(out[0], out[1])), out[-1]


def _gather2_wait(handle, after, name):
    ins, lands = handle["ins"], handle["lands"]
    send_d, recv_d, send_i, _ = handle["sems"]
    send_f, recv_f = handle["passed"]
    n = len(ins)
    after = after if isinstance(after, tuple) else (after,)

    def body(*refs):
        i_ref, lnd = refs[:n], refs[n:2 * n]
        s_d, r_d, s_i, s_f, r_f = refs[2 * n:2 * n + 5]
        me, _ = _peers()
        sib = _peer_of(_SIBLING)[1]
        for a in range(n):
            _rcopy(i_ref[a], lnd[a].at[sib], s_d, r_d, _SIBLING).wait_send()
            _rcopy(i_ref[a], lnd[a].at[sib], s_d, r_d, _SIBLING).wait_recv()
            for k in _CHIPS:
                row = _peer_of(k)[1]
                _rcopy(i_ref[a], lnd[a].at[me], s_i, r_d, k).wait_send()
                _rcopy(lnd[a].at[row], lnd[a].at[row], s_f, r_f, _SIBLING).wait_send()
                _rcopy(lnd[a].at[row], lnd[a].at[_peer_of(k ^ _SIBLING)[1]], s_f, r_f, _SIBLING).wait_recv()

    hbm = lambda a: pltpu.HBM(a.shape, a.dtype)
    out = pl.pallas_call(
        body, name=name,
        out_shape=tuple(hbm(a) for a in ins + lands),
        in_specs=[_HBM] * (2 * n) + [_SEM] * 5 + [pl.BlockSpec(memory_space=pl.ANY)] * len(after),
        out_specs=tuple([_HBM] * (2 * n)),
        input_output_aliases={i: i for i in range(2 * n)},
        compiler_params=pltpu.CompilerParams(has_side_effects=_EFFECT),
    )(*ins, *lands, send_d, recv_d, send_i, send_f, recv_f, *after)
    return list(out[n:])


def _adamw(w, parts, m, v, name="adamw"):
    R, C = w.shape
    L = len(parts)
    rl = R // L
    tr = max([t for t in range(16, 257, 16) if rl % t == 0], default=rl)
    nb = rl // tr
    c1 = 1.0 - ADAM_B1 ** ADAM_STEP
    c2 = 1.0 - ADAM_B2 ** ADAM_STEP

    def body(*refs):
        w_ref, p_refs, (m_ref, v_ref, g_ref, d_ref, nm_ref, nv_ref) = refs[0], refs[1:1 + L], refs[1 + L:]
        g = None
        for j, p_ref in enumerate(p_refs):
            gj = p_ref[0].astype(F32)
            for i in range(1, N_DEV):
                gj = gj + p_ref[i].astype(F32)
            g = gj if g is None else jnp.where(pl.program_id(0) == j, gj, g)
        nm = ADAM_B1 * m_ref[...] + (1.0 - ADAM_B1) * g
        nv = ADAM_B2 * v_ref[...] + (1.0 - ADAM_B2) * (g * g)
        g_ref[...] = g
        nm_ref[...] = nm
        nv_ref[...] = nv
        d_ref[...] = -ADAM_LR * ((nm / c1) / (jnp.sqrt(nv / c2) + ADAM_EPS) + ADAM_WD * w_ref[...])

    blk = pl.BlockSpec((tr, C), lambda l, i: (l * nb + i, 0))
    part = lambda j: pl.BlockSpec((N_DEV, tr, C), lambda l, i: (0, jnp.where(l == j, i, 0), 0))
    return pl.pallas_call(
        body, name=name, grid=(L, nb),
        in_specs=[blk] + [part(j) for j in range(L)] + [blk, blk],
        out_specs=[blk] * 4,
        out_shape=[jax.ShapeDtypeStruct((R, C), F32)] * 4,
        compiler_params=_params(("arbitrary", "arbitrary")),
    )(w, *parts, m, v)


_O1 = Q_LORA
_O2 = _O1 + KV_LORA
_O3 = _O2 + MLA_ROPE
_NB = SB_HEADS * SB_DIM
IN_W = _O2 + LANES + 3 * _NB
COL_KR = _O2 // LANES
COL_SB = COL_KR + 1


def _w_in_local(w):
    kr = w[_O2:_O3]
    pad = jnp.zeros((LANES - 2 * MLA_ROPE, w.shape[1]), w.dtype)
    return jnp.concatenate([w[:_O2], kr, kr, pad, w[_O3:]], axis=0)


def _w_in_grad(g):
    kr = (g[_O2:_O2 + MLA_ROPE].astype(F32) + g[_O2 + MLA_ROPE:_O2 + 2 * MLA_ROPE].astype(F32)).astype(g.dtype)
    return jnp.concatenate([g[:_O2], kr, g[_O2 + LANES:]], axis=0)


def _w_uq_local(w):
    w3 = w.reshape(MLA_HEADS // 2, 2, MLA_NOPE + MLA_ROPE, w.shape[1])
    nope = w3[:, :, :MLA_NOPE].reshape(MLA_HEADS // 2, 2 * MLA_NOPE, w.shape[1])
    rope = w3[:, :, MLA_NOPE:].reshape(MLA_HEADS // 2, 2 * MLA_ROPE, w.shape[1])
    pad = jnp.zeros((MLA_HEADS // 2, LANES - 2 * MLA_ROPE, w.shape[1]), w.dtype)
    return jnp.concatenate([nope, rope, pad], axis=1).reshape(-1, w.shape[1])


def _w_uq_grad(g):
    g3 = g.reshape(MLA_HEADS // 2, 2 * LANES, g.shape[1])
    nope = g3[:, :2 * MLA_NOPE].reshape(MLA_HEADS // 2, 2, MLA_NOPE, g.shape[1])
    rope = g3[:, LANES:LANES + 2 * MLA_ROPE].reshape(MLA_HEADS // 2, 2, MLA_ROPE, g.shape[1])
    return jnp.concatenate([nope, rope], axis=2).reshape(-1, g.shape[1])


def _w_ukv_local(w):
    w3 = w.reshape(MLA_HEADS, MLA_NOPE + MLA_V, w.shape[1])
    return jnp.concatenate([w3[:, :MLA_NOPE].reshape(-1, w.shape[1]),
                            w3[:, MLA_NOPE:].reshape(-1, w.shape[1])], axis=0)


def _w_ukv_grad(g):
    half = MLA_HEADS * MLA_NOPE
    kn = g[:half].reshape(MLA_HEADS, MLA_NOPE, g.shape[1])
    vv = g[half:].reshape(MLA_HEADS, MLA_V, g.shape[1])
    return jnp.concatenate([kn, vv], axis=1).reshape(-1, g.shape[1])


def _rope_tables(T):
    pos = jnp.arange(T, dtype=F32)
    inv_freq = ROPE_THETA ** (-jnp.arange(0, MLA_ROPE, 2, dtype=F32) / MLA_ROPE)
    ang = pos[:, None] * inv_freq[None, :]
    cos, sin = jnp.cos(ang), jnp.sin(ang)
    ones = jnp.ones((T, LANES - 2 * MLA_ROPE), F32)
    cos_k = jnp.concatenate([cos, cos, cos, cos, ones], axis=1)
    sin_k = jnp.concatenate([-sin, sin, -sin, sin, 0.0 * ones], axis=1)
    cos_q = jnp.concatenate([jnp.ones((T, LANES), F32), cos_k], axis=1)
    sin_q = jnp.concatenate([jnp.zeros((T, LANES), F32), sin_k], axis=1)
    return cos_q, sin_q, cos_k, sin_k


def _bias_diag_index():
    ell = np.arange(TOEP_W)
    return np.clip(BAND_W - ell, -REL_CLIP, REL_CLIP) + REL_CLIP


def _local_step(x, target, small, get_weights, put_grads, prefetch):
    T = x.shape[0]
    cos_q, sin_q, cos_k, sin_k = _rope_tables(T)
    G = {}
    W = dict(small)

    u0 = _rms_fwd(x, W["g_mix"][0:1], name="rms_mix0")
    bias_w = _toeplitz(W["od_rel_bias"][:, _bias_diag_index()])
    W.update(get_weights("in0", (u0, bias_w)))
    proj = _mm(u0, W["w_in_t"], dims="nt", name="proj_in")
    W.update(get_weights("mix0", proj))
    c_q, c_kv = proj[:, :_O1], proj[:, _O1:_O2]
    nq = _rms_fwd(c_q, W["g_cq"], name="rms_cq")
    nkv = _rms_fwd(c_kv, W["g_ckv"], name="rms_ckv")
    qa_raw = _mm(nq, W["w_uq_t"], dims="nt", name="proj_uq")
    kv = _mm(nkv, W["w_ukv_t"], dims="nt", out_dtype=BF16, name="proj_ukv")
    kr = _rope(proj, cos_k, sin_k, COL_KR, 1, BF16, name="rope_k")
    o_a, lse = _mla_fwd(qa_raw, cos_q, sin_q, kv, kr)
    o_b, o_b32, w_b, sp_b = _sb_fwd(proj, COL_SB, prefetch("ffn0", o_a))
    o_ab = jnp.concatenate([o_a, o_b], axis=1)
    h1 = _mm(o_ab, W["ev_w_out"], res=x, name="out_ev")

    def ffn_fwd(h, layer):
        W.update(get_weights(f"ffn{layer}", h))
        return _ffn_fwd(h, W["g_ffn"][layer:layer + 1], W[f"w_gate_t{layer}"], W[f"w_up_t{layer}"],
                        W[f"w_down{layer}"], name=f"ffn_fwd{layer}")

    h2, u1, a0, b0 = ffn_fwd(h1, 0)

    W.update(get_weights("mix1", h2))
    u2 = _rms_fwd(h2, W["g_mix"][1:2], name="rms_mix1")
    qkv = _mm(u2, W["od_w_qkv_t"], dims="nt", out_dtype=BF16, name="proj_qkv")
    nc = C_HEADS * C_DIM
    pad = ((PAD_KEYS, 0), (0, 0))
    k_pad, v_pad = jnp.pad(qkv[:, nc:2 * nc], pad), jnp.pad(qkv[:, 2 * nc:], pad)
    o_c, p_c = _band_fwd(qkv, k_pad, v_pad, bias_w)
    h3 = _mm(o_c, W["od_w_out"], res=h2, name="out_od")
    h4, u3, a1, b1 = ffn_fwd(h3, 1)

    loss, dh, dhb, G["g_final"] = _loss_head(h4, W["g_final"], target)

    def ffn_bwd(dh, dhb, h, u, a, b, layer):
        du, g_gate, g_up, g_down = _ffn_bwd(dhb, u, a, b, W[f"w_gate_t{layer}"], W[f"w_up_t{layer}"],
                                            W[f"w_down{layer}"], name=f"ffn_bwd{layer}")
        tok = put_grads(f"ffn{layer}", {"w_gate_t": g_gate, "w_up_t": g_up, "w_down": g_down})
        return _rms_bwd(h, W["g_ffn"][layer:layer + 1] + tok[:1, :1], du, dres=dh, name=f"rms_ffn_bwd{layer}")

    dh3, dh3b, g_gffn1 = ffn_bwd(dh, dhb, h3, u3, a1, b1, 1)

    do_c = _mm(dh3b, W["od_w_out"], dims="nt", name="out_od_dx")
    g_od_out = _mm(o_c, dh3b, dims="tn", out_dtype=BF16, name="out_od_dw")
    dq_c, dk_p, dv_p, dbias_w = _band_bwd(qkv, k_pad, v_pad, p_c, do_c)
    dqkv = jnp.concatenate([dq_c, dk_p[PAD_KEYS:], dv_p[PAD_KEYS:]], axis=1)
    tok = put_grads("mix1", {"od_w_qkv_t": _mm(dqkv, u2, dims="tn", out_dtype=BF16, name="proj_qkv_dw"),
                             "od_w_out": g_od_out})
    ddiag = _toeplitz_bwd(dbias_w)
    n_far = BAND_W - REL_CLIP + 1
    G["od_rel_bias"] = jnp.concatenate(
        [jnp.zeros((C_HEADS, REL_CLIP - BAND_TQ + 1), F32), ddiag[:, n_far:][:, ::-1],
         jnp.sum(ddiag[:, :n_far], axis=1, keepdims=True)], axis=1)
    dh2, dh2b, g_gmix1 = _mm_rms_bwd(dqkv, W["od_w_qkv_t"], h2, W["g_mix"][1:2] + tok[:1, :1], dh3,
                                     name="proj_qkv_dx")

    dh1, dh1b, g_gffn0 = ffn_bwd(dh2, dh2b, h1, u1, a0, b0, 0)
    G["g_ffn"] = jnp.concatenate([g_gffn0, g_gffn1], axis=0)

    do_ab = _mm(dh1b, W["ev_w_out"], dims="nt", name="out_ev_dx")
    g0 = {"ev_w_out": _mm(o_ab, dh1b, dims="tn", out_dtype=BF16, name="out_ev_dw")}
    dqa_raw, dkn, dva, dkr = _mla_bwd(qa_raw, cos_q, sin_q, kv, kr, o_a, lse, do_ab, 0)
    dlat, g0["w_uq_t"], g0["w_ukv_t"], G["g_cq"], G["g_ckv"] = _latent_bwd(
        proj, nq, nkv, dqa_raw, dkn, dva, dkr, cos_k, sin_k, W["g_cq"], W["g_ckv"], W["w_uq_t"], W["w_ukv_t"])
    tok = put_grads("mix0", g0)
    dqb, dkb, dvb = _sb_bwd(proj, COL_SB, o_b32, w_b, sp_b, do_ab, MLA_HEADS // 2, tok)
    dproj = jnp.concatenate([dlat, dqb, dkb, dvb], axis=1)
    tok = put_grads("in0", {"w_in_t": _mm(dproj, u0, dims="tn", name="proj_in_dw")})
    dx, _, g_gmix0 = _mm_rms_bwd(dproj, W["w_in_t"], x, W["g_mix"][0:1] + tok[:1, :1], dh1, name="proj_in_dx")
    G["g_mix"] = jnp.concatenate([g_gmix0, g_gmix1], axis=0)
    return loss[0, 0], dx, G


_BIG = ["ev_w_in", "ev_w_uq", "ev_w_ukv", "ev_w_out", "od_w_qkv", "od_w_out", "w_gate", "w_up", "w_down"]
_COL_SHARDED = {"ev_w_in", "ev_w_uq", "ev_w_ukv", "od_w_qkv", "w_gate", "w_up"}
_SMALL = ["ev_g_cq", "ev_g_ckv", "od_rel_bias", "g_mix", "g_ffn", "g_final"]
_GROUPS = {
    "in0": ["ev_w_in"],
    "mix0": ["ev_w_uq", "ev_w_ukv", "ev_w_out"],
    "ffn0": ["w_gate0", "w_up0", "w_down0"],
    "mix1": ["od_w_qkv", "od_w_out"],
    "ffn1": ["w_gate1", "w_up1", "w_down1"],
}
_GROUP_SRC = {n + str(l): (n, l) for n in ("w_gate", "w_up", "w_down") for l in (0, 1)}
_BATCHES = {"in0": ["in0"], "layer0": ["mix0", "ffn0"], "layer1": ["mix1", "ffn1"]}
_BATCH_OF = {grp: batch for batch, grps in _BATCHES.items() for grp in grps}
_SMALL_ROWS = 8
_SMALL_COLS = 1792


def _pack_small(vals):
    flat = jnp.concatenate([v.reshape(-1).astype(F32) for v in vals])
    flat = jnp.pad(flat, (0, _SMALL_ROWS * _SMALL_COLS - flat.shape[0]))
    return flat.reshape(_SMALL_ROWS, _SMALL_COLS)


def _unpack_small(packed, like):
    flat = packed.reshape(-1)
    out, off = [], 0
    for v in like:
        out.append(flat[off:off + v.size].reshape(v.shape))
        off += v.size
    return out


def kernel(x, ev_w_in, ev_g_cq, ev_w_uq, ev_g_ckv, ev_w_ukv, ev_w_out, od_w_qkv, od_rel_bias, od_w_out, g_mix, g_ffn, w_gate, w_up, w_down, g_final, loss_target, m_ev_w_in, m_ev_g_cq, m_ev_w_uq, m_ev_g_ckv, m_ev_w_ukv, m_ev_w_out, m_od_w_qkv, m_od_rel_bias, m_od_w_out, m_g_mix, m_g_ffn, m_w_gate, m_w_up, m_w_down, m_g_final, v_ev_w_in, v_ev_g_cq, v_ev_w_uq, v_ev_g_ckv, v_ev_w_ukv, v_ev_w_out, v_od_w_qkv, v_od_rel_bias, v_od_w_out, v_g_mix, v_g_ffn, v_w_gate, v_w_up, v_w_down, v_g_final):
    args = dict(locals())
    w = {n: args[n] for n in _BIG + _SMALL}
    mom = {n: args["m_" + n] for n in _BIG + _SMALL}
    var = {n: args["v_" + n] for n in _BIG + _SMALL}

    own = {}
    for grp, names in _GROUPS.items():
        for n in names:
            base, layer = _GROUP_SRC.get(n, (n, 0))
            shard = w[base][layer:layer + 1]
            own[n] = (jnp.swapaxes(shard, 1, 2) if base in _COL_SHARDED else shard).astype(BF16)
    placed = dict(zip(own, _place_own(list(own.values()), [False] * len(own), name="gather_own")))
    handles, token = _gather2_start(
        [[own[n] for n in names] for names in _GROUPS.values()],
        [[placed[n] for n in names] for names in _GROUPS.values()], x[0, :8, :LANES], name="gather_start")
    gather = dict(zip(_GROUPS, handles))
    pass_before = {"in0": ["in0"], "mix0": ["mix0"]}
    pass_after = {"ffn0": ("mix1", "g_ffn"), "mix1": ("ffn1", "g_mix")}

    def prefetch(grp, after):
        gather[grp], tok = _gather2_pass_on(gather[grp], after, name="gather_pass_" + grp)
        return tok

    def get_weights(grp, after):
        names = _GROUPS[grp]
        after = token if after is None else after
        for g in pass_before.get(grp, []):
            gather[g], _ = _gather2_pass_on(gather[g], after, name="gather_pass_" + g)
        lands = _gather2_wait(gather[grp], after, name="gather_wait_" + grp)
        full = {n: l.reshape(-1, l.shape[-1]) for n, l in zip(names, lands)}
        out = {}
        if grp in pass_after:
            g, gain = pass_after[grp]
            gather[g], tok = _gather2_pass_on(gather[g], lands[0], name="gather_pass_" + g)
            out[gain] = small[gain] + tok[:1, :1]
        if grp == "in0":
            out.update({"w_in_t": _w_in_local(full["ev_w_in"])})
        elif grp == "mix0":
            out.update({"w_uq_t": _w_uq_local(full["ev_w_uq"]), "w_ukv_t": _w_ukv_local(full["ev_w_ukv"]),
                        "ev_w_out": full["ev_w_out"]})
        elif grp == "mix1":
            out.update({"od_w_qkv_t": full["od_w_qkv"], "od_w_out": full["od_w_out"]})
        else:
            layer = grp[-1]
            out.update({"w_gate_t" + layer: full["w_gate" + layer], "w_up_t" + layer: full["w_up" + layer],
                        "w_down" + layer: full["w_down" + layer]})
        return out

    scatter, pending = {}, {}

    def put_grads(grp, g):
        if grp == "in0":
            g = {"ev_w_in": _w_in_grad(g["w_in_t"])}
        elif grp == "mix0":
            g = {"ev_w_uq": _w_uq_grad(g["w_uq_t"]), "ev_w_ukv": _w_ukv_grad(g["w_ukv_t"]),
                 "ev_w_out": g["ev_w_out"]}
        elif grp == "mix1":
            g = {"od_w_qkv": g["od_w_qkv_t"], "od_w_out": g["od_w_out"]}
        else:
            layer = grp[-1]
            g = {"w_gate" + layer: g["w_gate_t"], "w_up" + layer: g["w_up_t"], "w_down" + layer: g["w_down"]}
        pending.update({n: v.reshape(N_DEV, 1, v.shape[0] // N_DEV, v.shape[1]).astype(BF16) for n, v in g.items()})
        batch = _BATCH_OF[grp]
        names = [n for gr in _BATCHES[batch] for n in _GROUPS[gr]]
        if batch == "in0" or not all(n in pending for n in names):
            return jnp.zeros((8, LANES), F32)
        send = [pending[n] for n in names]
        scatter[batch], tok = _exchange_start(send, [True] * len(names), send[0], name="scatter_start_" + batch)
        return tok

    small = {"g_cq": ev_g_cq, "g_ckv": ev_g_ckv, "od_rel_bias": od_rel_bias[0],
             "g_mix": g_mix + token[0, 0], "g_ffn": g_ffn, "g_final": g_final.reshape(1, -1)}
    loss_part, dx, G = _local_step(x[0], loss_target[0], small, get_weights, put_grads, prefetch)
    g_small = _pack_small([G["g_cq"], G["g_ckv"], G["od_rel_bias"], G["g_mix"], G["g_ffn"], G["g_final"],
                           loss_part.reshape(1)])
    scatter["in0"], _ = _exchange_start([pending["ev_w_in"], g_small], [True, False], dx, name="scatter_start_in0")

    grads, deltas, new_m, new_v = {}, {}, {}, {}
    parts, after = {}, dx

    def wait_parts(batch, after):
        lands = _exchange_wait(scatter[batch], after, name="scatter_wait_" + batch)
        parts.update(zip([n for grp in _BATCHES[batch] for n in _GROUPS[grp]], lands))
        return lands[0]

    def adamw(n):
        col = n in _COL_SHARDED
        rows = lambda a: (jnp.swapaxes(a, 1, 2) if col else a).reshape(-1, a.shape[1 if col else 2])
        layers = [parts[n]] if n in parts else [parts[n + "0"], parts[n + "1"]]
        res = _adamw(rows(w[n]), [p.reshape(N_DEV, -1, p.shape[-1]) for p in layers], rows(mom[n]), rows(var[n]),
                     name="adamw_" + n)
        L, a1, a2 = w[n].shape
        back = lambda r: jnp.swapaxes(r.reshape(L, a2, a1), 1, 2) if col else r.reshape(L, a1, a2)
        grads[n], deltas[n], new_m[n], new_v[n] = [back(r) for r in res]
        return res[0]

    for batch in ("layer1", "layer0"):
        after = wait_parts(batch, after)
    parts["ev_w_in"], small_parts = _exchange_wait(scatter["in0"], tuple(adamw(n) for n in _BIG[1:]),
                                                   name="scatter_wait_in0")
    adamw("ev_w_in")
    small_w = [w[n] for n in _SMALL]
    loss = jnp.sum(small_parts.reshape(N_DEV, -1)[:, sum(v.size for v in small_w)])
    res = _adamw(_pack_small(small_w), [small_parts], _pack_small([mom[n] for n in _SMALL]),
                 _pack_small([var[n] for n in _SMALL]), name="adamw_small")
    for d, packed in zip((grads, deltas, new_m, new_v), res):
        for n, val in zip(_SMALL, _unpack_small(packed, small_w)):
            d[n] = val

    order = ["ev_w_in", "ev_g_cq", "ev_w_uq", "ev_g_ckv", "ev_w_ukv", "ev_w_out", "od_w_qkv", "od_rel_bias",
             "od_w_out", "g_mix", "g_ffn", "w_gate", "w_up", "w_down", "g_final"]
    out = [loss, dx[None]]
    for d in (grads, deltas, new_m, new_v):
        out += [d[n] for n in order]
    return tuple(out)
```

```python
import functools

import numpy as np
import jax
import jax.numpy as jnp
from jax import lax
from jax.experimental import pallas as pl
from jax.experimental.pallas import tpu as pltpu

F32 = jnp.float32
BF16 = jnp.bfloat16

D_MODEL = 1024
CHUNK = 64
MLA_HEADS = 8
MLA_NOPE = 64
MLA_ROPE = 32
MLA_V = 64
Q_LORA = 384
KV_LORA = 256
ROPE_THETA = 10000.0
SB_HEADS = 8
SB_DIM = 64
C_HEADS = 16
C_DIM = 64
LEFT_CHUNKS = 8
REL_CLIP = 256
D_FF = 2816
RMS_EPS = 1e-6
ADAM_LR = 0.001
ADAM_B1 = 0.9
ADAM_B2 = 0.999
ADAM_EPS = 1e-08
ADAM_WD = 0.01
ADAM_STEP = 10

N_DEV = 8
LANES = 128
HEAD = 64
assert HEAD == MLA_NOPE == MLA_V == SB_DIM == C_DIM and 2 * HEAD == LANES
CHUNK_BITS = CHUNK.bit_length() - 1
assert 1 << CHUNK_BITS == CHUNK
VMEM_LIMIT = 56 * 1024 * 1024
NEG = -1e30
PAD_KEYS = LEFT_CHUNKS * CHUNK
BAND_TQ = 128
BAND_W = BAND_TQ + PAD_KEYS
TOEP_W = BAND_W + BAND_TQ

NN = (((1,), (0,)), ((), ()))
NT = (((1,), (1,)), ((), ()))
TN = (((0,), (0,)), ((), ()))


def _dot(a, b, dn):
    return lax.dot_general(a, b, dn, preferred_element_type=F32)


def _pick(dim, pref):
    if dim <= pref:
        return dim
    best = None
    for t in range(LANES, pref + 1, LANES):
        if dim % t == 0:
            best = t
    assert best is not None, (dim, pref)
    return best


def _params(sem):
    return pltpu.CompilerParams(dimension_semantics=sem, vmem_limit_bytes=VMEM_LIMIT)


def _mm(a, b, dims="nn", res=None, out_dtype=F32, name="mm"):
    if dims == "nn":
        (M, K), (K2, N) = a.shape, b.shape
    elif dims == "nt":
        (M, K), (N, K2) = a.shape, b.shape
    else:
        (K, M), (K2, N) = a.shape, b.shape
    assert K == K2, (a.shape, b.shape, dims)
    tm, tn, tk = _pick(M, 1024), _pick(N, 1152), _pick(K, 1024)
    nk = K // tk
    dn = {"nn": NN, "nt": NT, "tn": TN}[dims]
    has_res = res is not None

    def body(*refs):
        if has_res:
            a_ref, b_ref, r_ref, o_ref, acc = refs
        else:
            a_ref, b_ref, o_ref, acc = refs
        k = pl.program_id(2)

        @pl.when(k == 0)
        def _():
            acc[...] = jnp.zeros_like(acc)

        acc[...] += _dot(a_ref[...].astype(BF16), b_ref[...].astype(BF16), dn)

        @pl.when(k == nk - 1)
        def _():
            r = acc[...]
            if has_res:
                r = r + r_ref[...]
            o_ref[...] = r.astype(out_dtype)

    a_spec = (pl.BlockSpec((tk, tm), lambda i, j, k: (k, i)) if dims == "tn"
              else pl.BlockSpec((tm, tk), lambda i, j, k: (i, k)))
    b_spec = (pl.BlockSpec((tn, tk), lambda i, j, k: (j, k)) if dims == "nt"
              else pl.BlockSpec((tk, tn), lambda i, j, k: (k, j)))
    o_spec = pl.BlockSpec((tm, tn), lambda i, j, k: (i, j))
    in_specs = [a_spec, b_spec] + ([o_spec] if has_res else [])
    args = (a, b) + ((res,) if has_res else ())
    return pl.pallas_call(
        body, name=name, grid=(M // tm, N // tn, nk),
        in_specs=in_specs, out_specs=o_spec,
        out_shape=jax.ShapeDtypeStruct((M, N), out_dtype),
        scratch_shapes=[pltpu.VMEM((tm, tn), F32)],
        compiler_params=_params(("parallel", "parallel", "arbitrary")),
    )(*args)


def _rms_fwd(x, g, out_dtype=BF16, name="rms_fwd"):
    T, Fd = x.shape
    tm = _pick(T, 256)

    def body(x_ref, g_ref, o_ref):
        xv = x_ref[...]
        r = lax.rsqrt(jnp.mean(xv * xv, axis=-1, keepdims=True) + RMS_EPS)
        o_ref[...] = (xv * r * g_ref[...]).astype(out_dtype)

    return pl.pallas_call(
        body, name=name, grid=(T // tm,),
        in_specs=[pl.BlockSpec((tm, Fd), lambda i: (i, 0)), pl.BlockSpec((1, Fd), lambda i: (0, 0))],
        out_specs=pl.BlockSpec((tm, Fd), lambda i: (i, 0)),
        out_shape=jax.ShapeDtypeStruct((T, Fd), out_dtype),
        compiler_params=_params(("parallel",)),
    )(x, g)


def _rms_bwd(x, g, dy, dres=None, name="rms_bwd"):
    T, Fd = x.shape
    tm = _pick(T, 256)
    has_res = dres is not None

    def body(*refs):
        if has_res:
            x_ref, g_ref, dy_ref, r_ref, dx_ref, dxb_ref, dg_ref = refs
        else:
            x_ref, g_ref, dy_ref, dx_ref, dxb_ref, dg_ref = refs
        xv, dyv = x_ref[...], dy_ref[...]
        r = lax.rsqrt(jnp.mean(xv * xv, axis=-1, keepdims=True) + RMS_EPS)
        gdy = dyv * g_ref[...]
        dot = jnp.mean(xv * gdy, axis=-1, keepdims=True)
        dx = r * gdy - xv * (r * r * r * dot)
        if has_res:
            dx = dx + r_ref[...]
        dx_ref[...] = dx
        dxb_ref[...] = dx.astype(BF16)

        @pl.when(pl.program_id(0) == 0)
        def _():
            dg_ref[...] = jnp.zeros_like(dg_ref)

        dg_ref[...] += jnp.sum(dyv * xv * r, axis=0, keepdims=True)

    row = pl.BlockSpec((tm, Fd), lambda i: (i, 0))
    vec = pl.BlockSpec((1, Fd), lambda i: (0, 0))
    in_specs = [row, vec, row] + ([row] if has_res else [])
    args = (x, g, dy) + ((dres,) if has_res else ())
    return pl.pallas_call(
        body, name=name, grid=(T // tm,),
        in_specs=in_specs, out_specs=[row, row, vec],
        out_shape=[jax.ShapeDtypeStruct((T, Fd), F32), jax.ShapeDtypeStruct((T, Fd), BF16),
                   jax.ShapeDtypeStruct((1, Fd), F32)],
        compiler_params=_params(("arbitrary",)),
    )(*args)


def _mm_rms_bwd(a, b, x, g, dres, name="mm_rms_bwd"):
    T, K = a.shape
    Fd = b.shape[1]
    tm, tk = _pick(T, 1024), _pick(K, 1024)
    nk = K // tk

    def body(a_ref, b_ref, x_ref, g_ref, r_ref, dx_ref, dxb_ref, dg_ref, acc):
        i, k = pl.program_id(0), pl.program_id(1)

        @pl.when(k == 0)
        def _():
            acc[...] = jnp.zeros_like(acc)

        @pl.when((k == 0) & (i == 0))
        def _():
            dg_ref[...] = jnp.zeros_like(dg_ref)

        acc[...] += _dot(a_ref[...].astype(BF16), b_ref[...].astype(BF16), NN)

        @pl.when(k == nk - 1)
        def _():
            xv, dyv = x_ref[...], acc[...]
            r = lax.rsqrt(jnp.mean(xv * xv, axis=-1, keepdims=True) + RMS_EPS)
            gdy = dyv * g_ref[...]
            dot = jnp.mean(xv * gdy, axis=-1, keepdims=True)
            dx = r * gdy - xv * (r * r * r * dot) + r_ref[...]
            dx_ref[...] = dx
            dxb_ref[...] = dx.astype(BF16)
            dg_ref[...] += jnp.sum(dyv * xv * r, axis=0, keepdims=True)

    row = pl.BlockSpec((tm, Fd), lambda i, k: (i, 0))
    vec = pl.BlockSpec((1, Fd), lambda i, k: (0, 0))
    return pl.pallas_call(
        body, name=name, grid=(T // tm, nk),
        in_specs=[pl.BlockSpec((tm, tk), lambda i, k: (i, k)), pl.BlockSpec((tk, Fd), lambda i, k: (k, 0)),
                  row, vec, row],
        out_specs=[row, row, vec],
        out_shape=[jax.ShapeDtypeStruct((T, Fd), F32), jax.ShapeDtypeStruct((T, Fd), BF16),
                   jax.ShapeDtypeStruct((1, Fd), F32)],
        scratch_shapes=[pltpu.VMEM((tm, Fd), F32)],
        compiler_params=_params(("arbitrary", "arbitrary")),
    )(a, b, x, g, dres)


def _latent_bwd(proj, nq, nkv, dqa, dkn, dva, dkr, cos_k, sin_k, g_cq, g_ckv, w_uq_t, w_ukv_t, name="latent_bwd"):
    T = proj.shape[0]
    tm = _pick(T, 512)
    wl = _O2

    def rms_bwd(xv, gv, dyv):
        r = lax.rsqrt(jnp.mean(xv * xv, axis=-1, keepdims=True) + RMS_EPS)
        gdy = dyv * gv
        dot = jnp.mean(xv * gdy, axis=-1, keepdims=True)
        return r * gdy - xv * (r * r * r * dot), jnp.sum(dyv * xv * r, axis=0, keepdims=True)

    def body(p_ref, nq_ref, nkv_ref, dqa_ref, dkn_ref, dva_ref, dkr_ref, c_ref, s_ref, gq_ref, gkv_ref, wq_ref, wkv_ref,
             dlat_ref, dwq_ref, dwkv_ref, dgq_ref, dgkv_ref):
        @pl.when(pl.program_id(0) == 0)
        def _():
            for ref in (dwq_ref, dwkv_ref, dgq_ref, dgkv_ref):
                ref[...] = jnp.zeros_like(ref)

        dqv = dqa_ref[...]
        dkv = jnp.concatenate([dkn_ref[...], dva_ref[...]], axis=1)
        pv = p_ref[...]
        dc_q, dgq = rms_bwd(pv[:, :_O1], gq_ref[...], _dot(dqv, wq_ref[...], NN))
        dc_kv, dgkv = rms_bwd(pv[:, _O1:], gkv_ref[...], _dot(dkv, wkv_ref[...], NN))
        dkr_raw = _rotate(dkr_ref[...], c_ref[...], -s_ref[...])
        dlat_ref[...] = jnp.concatenate([dc_q, dc_kv, dkr_raw], axis=1).astype(BF16)
        dwq_ref[...] += _dot(dqv, nq_ref[...], TN)
        dwkv_ref[...] += _dot(dkv, nkv_ref[...], TN)
        dgq_ref[...] += dgq
        dgkv_ref[...] += dgkv

    row = lambda w: pl.BlockSpec((tm, w), lambda i: (i, 0))
    const = lambda a: pl.BlockSpec(a.shape, lambda i: (0, 0))
    outs = [jax.ShapeDtypeStruct((T, wl + LANES), BF16), jax.ShapeDtypeStruct(w_uq_t.shape, F32),
            jax.ShapeDtypeStruct(w_ukv_t.shape, F32), jax.ShapeDtypeStruct(g_cq.shape, F32),
            jax.ShapeDtypeStruct(g_ckv.shape, F32)]
    return pl.pallas_call(
        body, name=name, grid=(T // tm,),
        in_specs=[row(wl), row(_O1), row(_O2 - _O1), row(dqa.shape[1]), row(dkn.shape[1]), row(dva.shape[1]),
                  row(LANES), row(LANES), row(LANES), const(g_cq), const(g_ckv), const(w_uq_t), const(w_ukv_t)],
        out_specs=[row(wl + LANES)] + [const(o) for o in outs[1:]],
        out_shape=outs,
        compiler_params=_params(("arbitrary",)),
    )(proj, nq, nkv, dqa, dkn, dva, dkr, cos_k, sin_k, g_cq, g_ckv, w_uq_t, w_ukv_t)


def _loss_head(h, g, target, name="loss_head"):
    T, Fd = h.shape
    tm = _pick(T, 256)

    def body(h_ref, g_ref, t_ref, loss_ref, dh_ref, dhb_ref, dg_ref):
        xv = h_ref[...]
        r = lax.rsqrt(jnp.mean(xv * xv, axis=-1, keepdims=True) + RMS_EPS)
        diff = xv * r * g_ref[...] - t_ref[...]
        part = 0.5 * jnp.sum(jnp.mean(diff * diff, axis=-1, keepdims=True), axis=0, keepdims=True)
        dyv = diff * (1.0 / Fd)
        gdy = dyv * g_ref[...]
        dot = jnp.mean(xv * gdy, axis=-1, keepdims=True)
        dh = r * gdy - xv * (r * r * r * dot)
        dh_ref[...] = dh
        dhb_ref[...] = dh.astype(BF16)

        @pl.when(pl.program_id(0) == 0)
        def _():
            dg_ref[...] = jnp.zeros_like(dg_ref)
            loss_ref[...] = jnp.zeros_like(loss_ref)

        dg_ref[...] += jnp.sum(dyv * xv * r, axis=0, keepdims=True)
        loss_ref[...] += jnp.broadcast_to(part, loss_ref.shape)

    row = pl.BlockSpec((tm, Fd), lambda i: (i, 0))
    vec = pl.BlockSpec((1, Fd), lambda i: (0, 0))
    return pl.pallas_call(
        body, name=name, grid=(T // tm,),
        in_specs=[row, vec, row],
        out_specs=[pl.BlockSpec((1, LANES), lambda i: (0, 0)), row, row, vec],
        out_shape=[jax.ShapeDtypeStruct((1, LANES), F32), jax.ShapeDtypeStruct((T, Fd), F32),
                   jax.ShapeDtypeStruct((T, Fd), BF16), jax.ShapeDtypeStruct((1, Fd), F32)],
        compiler_params=_params(("arbitrary",)),
    )(h, g, target)


FFN_TF = 256


def _ffn_fwd(h, g, wg_t, wu_t, wd, name="ffn_fwd"):
    T, Dm = h.shape
    Fh = wd.shape[0]
    tm = _pick(T, 2048)
    nf = Fh // FFN_TF

    def body(h_ref, g_ref, wg_ref, wu_ref, wd_ref, o_ref, u_ref, a_ref, b_ref):
        j = pl.program_id(1)

        @pl.when(j == 0)
        def _():
            xv = h_ref[...]
            r = lax.rsqrt(jnp.mean(xv * xv, axis=-1, keepdims=True) + RMS_EPS)
            u_ref[...] = (xv * r * g_ref[...]).astype(BF16)
            o_ref[...] = xv

        u = u_ref[...]
        a = _dot(u, wg_ref[...], NT).astype(BF16)
        b = _dot(u, wu_ref[...], NT).astype(BF16)
        a_ref[...] = a
        b_ref[...] = b
        af = a.astype(F32)
        s = (af * jax.nn.sigmoid(af) * b.astype(F32)).astype(BF16)
        o_ref[...] += _dot(s, wd_ref[...], NN)

    row = pl.BlockSpec((tm, Dm), lambda i, j: (i, 0))
    wblk = pl.BlockSpec((FFN_TF, Dm), lambda i, j: (j, 0))
    ablk = pl.BlockSpec((tm, FFN_TF), lambda i, j: (i, j))
    return pl.pallas_call(
        body, name=name, grid=(T // tm, nf),
        in_specs=[pl.BlockSpec((tm, Dm), lambda i, j: (i, 0), pipeline_mode=pl.Buffered(1)),
                  pl.BlockSpec((1, Dm), lambda i, j: (0, 0)), wblk, wblk, wblk],
        out_specs=[row, row, ablk, ablk],
        out_shape=[jax.ShapeDtypeStruct((T, Dm), F32), jax.ShapeDtypeStruct((T, Dm), BF16),
                   jax.ShapeDtypeStruct((T, Fh), BF16), jax.ShapeDtypeStruct((T, Fh), BF16)],
        compiler_params=_params(("parallel", "arbitrary")),
    )(h, g, wg_t, wu_t, wd)


def _ffn_bwd(dh, u, a, b, wg_t, wu_t, wd, name="ffn_bwd"):
    T, Dm = dh.shape
    Fh = wd.shape[0]
    nf = Fh // FFN_TF
    once = pl.Buffered(1)

    def body(dh_ref, u_ref, a_ref, b_ref, wg_ref, wu_ref, wd_ref, du_ref, dwg_ref, dwu_ref, dwd_ref):
        j = pl.program_id(0)

        @pl.when(j == 0)
        def _():
            du_ref[...] = jnp.zeros_like(du_ref)

        ds = _dot(dh_ref[...], wd_ref[...], NT)
        af, bf = a_ref[...].astype(F32), b_ref[...].astype(F32)
        sig = jax.nn.sigmoid(af)
        sa = af * sig
        dwd_ref[...] = _dot((sa * bf).astype(BF16), dh_ref[...], TN).astype(BF16)
        dab = jnp.concatenate([(ds * bf * (sig * (1.0 + af * (1.0 - sig)))).astype(BF16),
                               (ds * sa).astype(BF16)], axis=1)
        dw = _dot(dab, u_ref[...], TN)
        dwg_ref[...] = dw[:FFN_TF].astype(BF16)
        dwu_ref[...] = dw[FFN_TF:].astype(BF16)
        du_ref[...] += _dot(dab, jnp.concatenate([wg_ref[...], wu_ref[...]], axis=0), NN)

    full = lambda: pl.BlockSpec((T, Dm), lambda j: (0, 0), pipeline_mode=once)
    wblk = pl.BlockSpec((FFN_TF, Dm), lambda j: (j, 0))
    ablk = pl.BlockSpec((T, FFN_TF), lambda j: (0, j))
    return pl.pallas_call(
        body, name=name, grid=(nf,),
        in_specs=[full(), full(), ablk, ablk, wblk, wblk, wblk],
        out_specs=[pl.BlockSpec((T, Dm), lambda j: (0, 0)), wblk, wblk, wblk],
        out_shape=[jax.ShapeDtypeStruct((T, Dm), F32)] + [jax.ShapeDtypeStruct((Fh, Dm), BF16)] * 3,
        compiler_params=_params(("arbitrary",)),
    )(dh, u, a, b, wg_t, wu_t, wd)


def _rope(x, cos_t, sin_t, col0, ncols, out_dtype, name="rope"):
    T = x.shape[0]
    wt = cos_t.shape[1]
    tm = _pick(T, 256)
    nb = ncols * LANES // wt
    half = MLA_ROPE // 2

    def body(x_ref, c_ref, s_ref, o_ref):
        xv = x_ref[...].astype(F32)
        lane = lax.broadcasted_iota(jnp.int32, xv.shape, 1)
        first = (lane & (MLA_ROPE - 1)) < half
        swapped = jnp.where(first, pltpu.roll(xv, wt - half, 1), pltpu.roll(xv, half, 1))
        o_ref[...] = (xv * c_ref[...] + swapped * s_ref[...]).astype(out_dtype)

    off = col0 * LANES // wt
    return pl.pallas_call(
        body, name=name, grid=(T // tm, nb),
        in_specs=[pl.BlockSpec((tm, wt), lambda i, j: (i, j + off)),
                  pl.BlockSpec((tm, wt), lambda i, j: (i, 0)),
                  pl.BlockSpec((tm, wt), lambda i, j: (i, 0))],
        out_specs=pl.BlockSpec((tm, wt), lambda i, j: (i, j)),
        out_shape=jax.ShapeDtypeStruct((T, ncols * LANES), out_dtype),
        compiler_params=_params(("parallel", "parallel")),
    )(x, cos_t, sin_t)


ATT_TQ = 512
ATT_TK = 256
MLA_TK = 512


def _mla_masks(shape):
    lane = lax.broadcasted_iota(jnp.int32, shape, 1)
    m0 = (lane < HEAD) | ((lane >= LANES) & (lane < LANES + MLA_ROPE))
    m1 = ((lane >= HEAD) & (lane < LANES)) | ((lane >= LANES + MLA_ROPE) & (lane < LANES + 2 * MLA_ROPE))
    return m0, m1


def _by_twos(n, step, carry):
    carry = lax.fori_loop(0, n // 2, lambda i, c: step(2 * i + 1, step(2 * i, c)), carry)
    return lax.fori_loop(0, n % 2, lambda _, c: step(n - 1, c), carry)


def _chunk_ok(tq, tk, d):
    row = lax.broadcasted_iota(jnp.int32, (tq, tk), 0)
    col = lax.broadcasted_iota(jnp.int32, (tq, tk), 1) + d * tk
    return jnp.concatenate([(col >> CHUNK_BITS) <= (row >> CHUNK_BITS)] * 2, axis=0)


def _rotate(x, cos_t, sin_t):
    half = MLA_ROPE // 2
    lane = lax.broadcasted_iota(jnp.int32, x.shape, 1)
    first = (lane & (MLA_ROPE - 1)) < half
    swapped = jnp.where(first, pltpu.roll(x, x.shape[1] - half, 1), pltpu.roll(x, half, 1))
    return x * cos_t + swapped * sin_t


def _mla_fwd(q, cos_q, sin_q, kv, kr, name="mla_fwd"):
    T = q.shape[0]
    tq, tk = _pick(T, ATT_TQ), _pick(T, MLA_TK)
    nd = tq // tk
    npair = MLA_HEADS // 2
    scale = (MLA_NOPE + MLA_ROPE) ** -0.5

    def body(q_ref, c_ref, s_ref, kn_ref, v_ref, kr_ref, o_ref, lse_ref):
        m_idx = pl.program_id(1)
        qv = _rotate(q_ref[...], c_ref[...], s_ref[...]).astype(BF16)
        m0, m1 = _mla_masks(qv.shape)
        qs = jnp.concatenate([jnp.where(m0, qv, 0), jnp.where(m1, qv, 0)], axis=0).astype(BF16)

        def block(kb, carry, ok):
            ks = pl.ds(pl.multiple_of(kb * tk, tk), tk)
            kcat = jnp.concatenate([kn_ref[ks, :], kr_ref[ks, :]], axis=1)
            mx, l, acc = carry
            s = _dot(qs, kcat, NT) * scale
            if ok is not None:
                s = jnp.where(ok, s, NEG)
            mn = jnp.maximum(mx, jnp.max(s, axis=-1, keepdims=True))
            alpha = jnp.exp(mx - mn)
            p = jnp.exp(s - mn)
            return (mn, alpha * l + jnp.sum(p, axis=-1, keepdims=True),
                    alpha * acc + _dot(p.astype(BF16), v_ref[ks, :], NN))

        init = (jnp.full((2 * tq, 1), NEG, F32), jnp.zeros((2 * tq, 1), F32), jnp.zeros((2 * tq, LANES), F32))
        res = init
        for d in range(nd):
            res = block(m_idx * nd + d, res, _chunk_ok(tq, tk, d))
        mx, l, acc = _by_twos(m_idx * nd, lambda kb, c: block(kb, c, None), res)
        h0 = lax.broadcasted_iota(jnp.int32, (tq, LANES), 1) < HEAD
        o_ref[...] = _two_heads(acc * (1.0 / l), h0).astype(o_ref.dtype)
        lse_ref[...] = _two_heads(jnp.broadcast_to(mx + jnp.log(l), (2 * tq, LANES)), h0)

    full = lambda col: pl.BlockSpec((T, LANES), col)
    table = pl.BlockSpec((tq, 2 * LANES), lambda p, m: (m, 0))
    return pl.pallas_call(
        body, name=name, grid=(npair, T // tq),
        in_specs=[pl.BlockSpec((tq, 2 * LANES), lambda p, m: (m, p)), table, table,
                  full(lambda p, m: (0, p)), full(lambda p, m: (0, npair + p)), full(lambda p, m: (0, 0))],
        out_specs=[pl.BlockSpec((tq, LANES), lambda p, m: (m, p)),
                   pl.BlockSpec((tq, LANES), lambda p, m: (m, p))],
        out_shape=[jax.ShapeDtypeStruct((T, npair * LANES), BF16),
                   jax.ShapeDtypeStruct((T, npair * LANES), F32)],
        compiler_params=_params(("parallel", "arbitrary")),
    )(q, cos_q, sin_q, kv, kv, kr)


def _mla_bwd(q, cos_q, sin_q, kv, kr, o, lse, do, do_col0, name="mla_bwd"):
    T = q.shape[0]
    tq, tk = _pick(T, ATT_TQ), _pick(T, MLA_TK)
    nd = tq // tk
    npair = MLA_HEADS // 2
    scale = (MLA_NOPE + MLA_ROPE) ** -0.5

    def body(q_ref, c_ref, s_ref, kn_ref, v_ref, kr_ref, o_ref, lse_ref, do_ref, dq_ref, dkn_ref, dv_ref, dkr_ref,
             dkn_acc, dv_acc):
        p_idx, m_idx = pl.program_id(0), pl.program_id(1)

        @pl.when(m_idx == 0)
        def _():
            dkn_acc[...] = jnp.zeros_like(dkn_acc)
            dv_acc[...] = jnp.zeros_like(dv_acc)

        @pl.when((m_idx == 0) & (p_idx == 0))
        def _():
            dkr_ref[...] = jnp.zeros_like(dkr_ref)

        qv = _rotate(q_ref[...], c_ref[...], s_ref[...]).astype(BF16)
        m0, m1 = _mla_masks(qv.shape)
        qs = jnp.concatenate([jnp.where(m0, qv, 0), jnp.where(m1, qv, 0)], axis=0).astype(BF16)
        dov = do_ref[...].astype(F32)
        h0 = lax.broadcasted_iota(jnp.int32, (tq, LANES), 1) < HEAD
        dos32 = jnp.concatenate([jnp.where(h0, dov, 0.0), jnp.where(h0, 0.0, dov)], axis=0)
        ov = o_ref[...].astype(F32)
        delta = jnp.sum(dos32 * jnp.concatenate([ov, ov], axis=0), axis=-1, keepdims=True)
        dos = dos32.astype(BF16)
        lsev = lse_ref[...]
        lse = jnp.concatenate([lsev[:, 0:1], lsev[:, HEAD:HEAD + 1]], axis=0)

        def block(kb, dq, ok):
            ks = pl.ds(pl.multiple_of(kb * tk, tk), tk)
            kcat = jnp.concatenate([kn_ref[ks, :], kr_ref[ks, :]], axis=1)
            vv = v_ref[ks, :]
            p = jnp.exp(_dot(qs, kcat, NT) * scale - lse)
            if ok is not None:
                p = jnp.where(ok, p, 0.0)
            ds = (p * (_dot(dos, vv, NT) - delta) * scale).astype(BF16)
            dkc = _dot(ds, qs, TN)
            dkn_acc[ks, :] += dkc[:, :LANES]
            dkr_ref[ks, :] += dkc[:, LANES:]
            dv_acc[ks, :] += _dot(p.astype(BF16), dos, TN)
            return dq + _dot(ds, kcat, NN)

        dq = jnp.zeros((2 * tq, 2 * LANES), F32)
        for d in range(nd):
            dq = block(m_idx * nd + d, dq, _chunk_ok(tq, tk, d))
        dq = _by_twos(m_idx * nd, lambda kb, c: block(kb, c, None), dq)
        dq_ref[...] = _rotate(jnp.where(m0, dq[:tq], jnp.where(m1, dq[tq:], 0.0)), c_ref[...],
                              -s_ref[...]).astype(BF16)

        @pl.when(m_idx == T // tq - 1)
        def _():
            dkn_ref[...] = dkn_acc[...].astype(BF16)
            dv_ref[...] = dv_acc[...].astype(BF16)

    full = lambda col: pl.BlockSpec((T, LANES), col)
    blk = lambda col: pl.BlockSpec((tq, LANES), col)
    table = pl.BlockSpec((tq, 2 * LANES), lambda p, m: (m, 0))
    return pl.pallas_call(
        body, name=name, grid=(npair, T // tq),
        in_specs=[pl.BlockSpec((tq, 2 * LANES), lambda p, m: (m, p)), table, table,
                  full(lambda p, m: (0, p)), full(lambda p, m: (0, npair + p)), full(lambda p, m: (0, 0)),
                  blk(lambda p, m: (m, p)), blk(lambda p, m: (m, p)),
                  blk(lambda p, m: (m, do_col0 + p))],
        out_specs=[pl.BlockSpec((tq, 2 * LANES), lambda p, m: (m, p)),
                   full(lambda p, m: (0, p)), full(lambda p, m: (0, p)), full(lambda p, m: (0, 0))],
        out_shape=[jax.ShapeDtypeStruct((T, npair * 2 * LANES), BF16),
                   jax.ShapeDtypeStruct((T, npair * LANES), BF16),
                   jax.ShapeDtypeStruct((T, npair * LANES), BF16),
                   jax.ShapeDtypeStruct((T, LANES), F32)],
        scratch_shapes=[pltpu.VMEM((T, LANES), F32)] * 2,
        compiler_params=_params(("arbitrary", "arbitrary")),
    )(q, cos_q, sin_q, kv, kv, kr, o, lse, do)


def _split_dot(x, tri):
    hi = x.astype(BF16)
    lo = (x - hi.astype(F32)).astype(BF16)
    both = _dot(jnp.concatenate([hi, lo], axis=0), tri, NN)
    return both[:x.shape[0]] + both[x.shape[0]:]


def _sb_terms(qh, kk, before):
    z = _dot(qh, kk, NT)
    sp = jnp.maximum(z, 0.0) + jnp.log(1.0 + jnp.exp(-jnp.abs(z)))
    lk = -sp if before is None else jnp.where(before, -sp, 0.0)
    return z, sp, lk


def _sb_setup(q_ref, tq, tk, scale):
    qv = (q_ref[...].astype(F32) * scale).astype(BF16)
    lane = lax.broadcasted_iota(jnp.int32, (tq, LANES), 1)
    h0 = lane < HEAD
    qs = jnp.concatenate([jnp.where(h0, qv, 0), jnp.where(h0, 0, qv)], axis=0).astype(BF16)
    row = lax.broadcasted_iota(jnp.int32, (tk, tk), 0)
    col = lax.broadcasted_iota(jnp.int32, (tk, tk), 1)
    return qs, h0, row, col


def _sb_before(tq, tk, d):
    row = lax.broadcasted_iota(jnp.int32, (tq, tk), 0)
    col = lax.broadcasted_iota(jnp.int32, (tq, tk), 1) + d * tk
    return jnp.concatenate([col < row] * 2, axis=0)


def _two_heads(x, h0):
    tq = x.shape[0] // 2
    return jnp.where(h0, x[:tq], x[tq:])


def _sb_fwd(qkv, col0, dep, name="sb_fwd"):
    T = qkv.shape[0]
    tq, tk = _pick(T, ATT_TQ), _pick(T, ATT_TK)
    nd = tq // tk
    npair = SB_HEADS // 2
    scale = SB_DIM ** -0.5

    def body(q_ref, k_ref, v_ref, dep_ref, o_ref, o32_ref, w_ref, sp_ref):
        m_idx = pl.program_id(1)
        qs, h0, row, col = _sb_setup(q_ref, tq, tk, scale)
        later = (row > col).astype(BF16)

        def block(kb, carry, before):
            ks = pl.ds(pl.multiple_of(kb * tk, tk), tk)
            c, acc = carry
            z, sp, lk = _sb_terms(qs, k_ref[ks, :].astype(BF16), before)
            w = jnp.exp((z - sp) + _split_dot(lk, later) + c)
            if before is not None:
                w = jnp.where(before, w, 0.0)
            wb = w.astype(BF16)
            w_ref[0, 0, kb] = wb
            sp_ref[0, 0, kb] = sp.astype(BF16)
            return (c + jnp.sum(lk, axis=-1, keepdims=True), acc + _dot(wb, v_ref[ks, :].astype(BF16), NN))

        init = (jnp.zeros((2 * tq, 1), F32), jnp.zeros((2 * tq, LANES), F32))
        res = init
        for d in reversed(range(nd)):
            res = block(m_idx * nd + d, res, _sb_before(tq, tk, d))
        res = _by_twos(m_idx * nd, lambda i, c: block(m_idx * nd - 1 - i, c, None), res)
        o = _two_heads(res[1], h0)
        o_ref[...] = o.astype(o_ref.dtype)
        o32_ref[...] = o

    full = lambda col: pl.BlockSpec((T, LANES), col)
    blk = pl.BlockSpec((tq, LANES), lambda p, m: (m, p))
    return pl.pallas_call(
        body, name=name, grid=(npair, T // tq),
        in_specs=[pl.BlockSpec((tq, LANES), lambda p, m: (m, col0 + p)),
                  full(lambda p, m: (0, col0 + npair + p)), full(lambda p, m: (0, col0 + 2 * npair + p)),
                  pl.BlockSpec((8, LANES), lambda p, m: (0, 0))],
        out_specs=[blk, blk] + [pl.BlockSpec((1, 1, T // tk, 2 * tq, tk), lambda p, m: (p, m, 0, 0, 0))] * 2,
        out_shape=[jax.ShapeDtypeStruct((T, npair * LANES), BF16), jax.ShapeDtypeStruct((T, npair * LANES), F32)]
        + [jax.ShapeDtypeStruct((npair, T // tq, T // tk, 2 * tq, tk), BF16)] * 2,
        compiler_params=_params(("parallel", "arbitrary")),
    )(qkv, qkv, qkv, dep)


def _sb_bwd(qkv, col0, o32, w_all, sp_all, do, do_col0, dep, name="sb_bwd"):
    T = qkv.shape[0]
    tq, tk = _pick(T, ATT_TQ), _pick(T, ATT_TK)
    nd = tq // tk
    npair = SB_HEADS // 2
    scale = SB_DIM ** -0.5

    def body(q_ref, k_ref, v_ref, o_ref, w_ref, sp_ref, do_ref, dep_ref, dq_ref, dk_ref, dv_ref, dk_acc, dv_acc):
        m_idx = pl.program_id(1)

        @pl.when(m_idx == 0)
        def _():
            dk_acc[...] = jnp.zeros_like(dk_acc)
            dv_acc[...] = jnp.zeros_like(dv_acc)

        qs, h0, row, col = _sb_setup(q_ref, tq, tk, scale)
        dov = do_ref[...].astype(F32)
        dos = jnp.concatenate([jnp.where(h0, dov, 0.0), jnp.where(h0, 0.0, dov)], axis=0).astype(BF16)
        ov = o_ref[...]
        etot = jnp.sum(dos.astype(F32) * jnp.concatenate([ov, ov], axis=0), axis=-1, keepdims=True)
        from_here = (row >= col).astype(BF16)

        def block(kb, carry, before):
            ks = pl.ds(pl.multiple_of(kb * tk, tk), tk)
            kk = k_ref[ks, :].astype(BF16)
            vv = v_ref[ks, :].astype(BF16)
            es, dqa = carry
            wb = w_ref[0, 0, kb]
            e = wb.astype(F32) * _dot(dos, vv, NT)
            prev = etot - (_split_dot(e, from_here) + es)
            sig_neg = jnp.exp(-sp_ref[0, 0, kb].astype(F32))
            dz = e * sig_neg - (1.0 - sig_neg) * prev
            if before is not None:
                dz = jnp.where(before, dz, 0.0)
            dzb = dz.astype(BF16)
            dk_acc[ks, :] += _dot(dzb, qs, TN)
            dv_acc[ks, :] += _dot(wb, dos, TN)
            return es + jnp.sum(e, axis=-1, keepdims=True), dqa + _dot(dzb, kk, NN)

        init = (jnp.zeros((2 * tq, 1), F32), jnp.zeros((2 * tq, LANES), F32))
        res = init
        for d in reversed(range(nd)):
            res = block(m_idx * nd + d, res, _sb_before(tq, tk, d))
        res = _by_twos(m_idx * nd, lambda i, c: block(m_idx * nd - 1 - i, c, None), res)
        dq_ref[...] = (_two_heads(res[1], h0) * scale).astype(BF16)

        @pl.when(m_idx == T // tq - 1)
        def _():
            dk_ref[...] = dk_acc[...].astype(BF16)
            dv_ref[...] = dv_acc[...].astype(BF16)

    full = lambda col: pl.BlockSpec((T, LANES), col)
    blk = lambda col: pl.BlockSpec((tq, LANES), col)
    return pl.pallas_call(
        body, name=name, grid=(npair, T // tq),
        in_specs=[blk(lambda p, m: (m, col0 + p)),
                  full(lambda p, m: (0, col0 + npair + p)), full(lambda p, m: (0, col0 + 2 * npair + p)),
                  blk(lambda p, m: (m, p)),
                  pl.BlockSpec((1, 1, T // tk, 2 * tq, tk), lambda p, m: (p, m, 0, 0, 0)),
                  pl.BlockSpec((1, 1, T // tk, 2 * tq, tk), lambda p, m: (p, m, 0, 0, 0)),
                  blk(lambda p, m: (m, do_col0 + p)), pl.BlockSpec((8, LANES), lambda p, m: (0, 0))],
        out_specs=[blk(lambda p, m: (m, p)), full(lambda p, m: (0, p)), full(lambda p, m: (0, p))],
        out_shape=[jax.ShapeDtypeStruct((T, npair * LANES), BF16)] * 3,
        scratch_shapes=[pltpu.VMEM((T, LANES), F32)] * 2,
        compiler_params=_params(("arbitrary", "arbitrary")),
    )(qkv, qkv, qkv, o32, w_all, sp_all, do, dep)


def _band_in_window():
    cq = lax.broadcasted_iota(jnp.int32, (BAND_TQ, BAND_W), 0) >> CHUNK_BITS
    ckp = lax.broadcasted_iota(jnp.int32, (BAND_TQ, BAND_W), 1) >> CHUNK_BITS
    return (ckp >= cq) & (ckp <= cq + LEFT_CHUNKS)


def _band_real(m_idx):
    j = lax.broadcasted_iota(jnp.int32, (BAND_TQ, BAND_W), 1)
    return j >= PAD_KEYS - m_idx * BAND_TQ


def _band_probs(qh, kw, bias, real, scale):
    s = jnp.where(real, _dot(qh, kw, NT) * scale + bias, NEG)
    e = jnp.exp(s - jnp.max(s, axis=-1, keepdims=True))
    return e * (1.0 / jnp.sum(e, axis=-1, keepdims=True))


BAND_SUB = 16


def _band_fwd(qkv, k_pad, v_pad, bias_w, name="band_fwd"):
    T = qkv.shape[0]
    npair = C_HEADS // 2
    scale = C_DIM ** -0.5
    rows = BAND_SUB * BAND_TQ

    def body(q_ref, k_ref, v_ref, b_ref, o_ref, p_ref):
        lane = lax.broadcasted_iota(jnp.int32, (BAND_TQ, LANES), 1)
        h0 = lane < HEAD
        bias = jnp.concatenate([b_ref[0], b_ref[1]], axis=0)
        for sub in range(BAND_SUB):
            m_idx = pl.program_id(1) * BAND_SUB + sub
            win = pl.ds(pl.multiple_of(m_idx * BAND_TQ, BAND_TQ), BAND_W)
            kw, vw = k_ref[win, :], v_ref[win, :]
            qv = q_ref[sub * BAND_TQ:(sub + 1) * BAND_TQ, :]
            qs = jnp.concatenate([jnp.where(h0, qv, 0), jnp.where(h0, 0, qv)], axis=0).astype(BF16)
            p = _band_probs(qs, kw, bias, jnp.concatenate([_band_real(m_idx)] * 2, axis=0), scale).astype(BF16)
            p_ref[0, sub] = p
            o = _two_heads(_dot(p, vw, NN), h0)
            o_ref[sub * BAND_TQ:(sub + 1) * BAND_TQ, :] = o.astype(o_ref.dtype)

    Tp = T + PAD_KEYS
    return pl.pallas_call(
        body, name=name, grid=(npair, T // rows),
        in_specs=[pl.BlockSpec((rows, LANES), lambda p, m: (m, p)),
                  pl.BlockSpec((Tp, LANES), lambda p, m: (0, p)),
                  pl.BlockSpec((Tp, LANES), lambda p, m: (0, p)),
                  pl.BlockSpec((2, BAND_TQ, BAND_W), lambda p, m: (p, 0, 0))],
        out_specs=[pl.BlockSpec((rows, LANES), lambda p, m: (m, p)),
                   pl.BlockSpec((1, BAND_SUB, 2 * BAND_TQ, BAND_W), lambda p, m: (p, m, 0, 0))],
        out_shape=[jax.ShapeDtypeStruct((T, npair * LANES), BF16),
                   jax.ShapeDtypeStruct((npair, T // BAND_TQ, 2 * BAND_TQ, BAND_W), BF16)],
        compiler_params=_params(("parallel", "arbitrary")),
    )(qkv, k_pad, v_pad, bias_w)


def _band_bwd(qkv, k_pad, v_pad, probs, do, name="band_bwd"):
    T = qkv.shape[0]
    npair = C_HEADS // 2
    scale = C_DIM ** -0.5

    rows = BAND_SUB * BAND_TQ

    def body(q_ref, k_ref, v_ref, p_ref, do_ref, dq_ref, dk_ref, dv_ref, db_ref, dk_acc, dv_acc):
        @pl.when(pl.program_id(1) == 0)
        def _():
            dk_acc[...] = jnp.zeros_like(dk_acc)
            dv_acc[...] = jnp.zeros_like(dv_acc)
            db_ref[...] = jnp.zeros_like(db_ref)

        lane = lax.broadcasted_iota(jnp.int32, (BAND_TQ, LANES), 1)
        h0 = lane < HEAD
        dbs = jnp.zeros((2 * BAND_TQ, BAND_W), F32)
        for sub in range(BAND_SUB):
            m_idx = pl.program_id(1) * BAND_SUB + sub
            win = pl.ds(pl.multiple_of(m_idx * BAND_TQ, BAND_TQ), BAND_W)
            kw, vw = k_ref[win, :], v_ref[win, :]
            qv = q_ref[sub * BAND_TQ:(sub + 1) * BAND_TQ, :]
            dov = do_ref[sub * BAND_TQ:(sub + 1) * BAND_TQ, :].astype(F32)
            qs = jnp.concatenate([jnp.where(h0, qv, 0), jnp.where(h0, 0, qv)], axis=0).astype(BF16)
            dos = jnp.concatenate([jnp.where(h0, dov, 0.0), jnp.where(h0, 0.0, dov)], axis=0).astype(BF16)
            pb = p_ref[0, sub]
            p = pb.astype(F32)
            dp = _dot(dos, vw, NT)
            dsb = p * (dp - jnp.sum(p * dp, axis=-1, keepdims=True))
            dbs = dbs + dsb
            dsq = (dsb * scale).astype(BF16)
            dq_ref[sub * BAND_TQ:(sub + 1) * BAND_TQ, :] = _two_heads(_dot(dsq, kw, NN), h0).astype(BF16)
            dk_acc[win, :] += _dot(dsq, qs, TN)
            dv_acc[win, :] += _dot(pb, dos, TN)
        db_ref[0] += dbs[:BAND_TQ]
        db_ref[1] += dbs[BAND_TQ:]

        @pl.when(pl.program_id(1) == T // rows - 1)
        def _():
            dk_ref[...] = dk_acc[...].astype(BF16)
            dv_ref[...] = dv_acc[...].astype(BF16)

    Tp = T + PAD_KEYS
    blk = lambda col: pl.BlockSpec((rows, LANES), col)
    full = pl.BlockSpec((Tp, LANES), lambda p, m: (0, p))
    bias = pl.BlockSpec((2, BAND_TQ, BAND_W), lambda p, m: (p, 0, 0))
    prob = pl.BlockSpec((1, BAND_SUB, 2 * BAND_TQ, BAND_W), lambda p, m: (p, m, 0, 0))
    return pl.pallas_call(
        body, name=name, grid=(npair, T // rows),
        in_specs=[blk(lambda p, m: (m, p)), full, full, prob, blk(lambda p, m: (m, p))],
        out_specs=[blk(lambda p, m: (m, p)), full, full, bias],
        out_shape=[jax.ShapeDtypeStruct((T, npair * LANES), BF16),
                   jax.ShapeDtypeStruct((Tp, npair * LANES), BF16),
                   jax.ShapeDtypeStruct((Tp, npair * LANES), BF16),
                   jax.ShapeDtypeStruct((C_HEADS, BAND_TQ, BAND_W), F32)],
        scratch_shapes=[pltpu.VMEM((Tp, LANES), F32)] * 2,
        compiler_params=_params(("arbitrary", "arbitrary")),
    )(qkv, k_pad, v_pad, probs, do)


def _skew_bits(x, left):
    w = x.shape[1]
    row = lax.broadcasted_iota(jnp.int32, x.shape, 0)
    for b in range(BAND_TQ.bit_length() - 1):
        amt = (w - (1 << b)) if left else (1 << b)
        x = jnp.where((row >> b) & 1 == 1, pltpu.roll(x, amt, 1), x)
    return x


def _toeplitz(diag, name="toeplitz"):
    H = diag.shape[0]

    def body(d_ref, o_ref):
        x = jnp.broadcast_to(d_ref[0], (BAND_TQ, TOEP_W))
        o_ref[0] = jnp.where(_band_in_window(), _skew_bits(x, left=False)[:, BAND_TQ:], NEG)

    return pl.pallas_call(
        body, name=name, grid=(H,),
        in_specs=[pl.BlockSpec((1, 1, TOEP_W), lambda h: (h, 0, 0))],
        out_specs=pl.BlockSpec((1, BAND_TQ, BAND_W), lambda h: (h, 0, 0)),
        out_shape=jax.ShapeDtypeStruct((H, BAND_TQ, BAND_W), F32),
        compiler_params=_params(("parallel",)),
    )(diag.reshape(H, 1, TOEP_W))


def _toeplitz_bwd(dbias, name="toeplitz_bwd"):
    H = dbias.shape[0]

    def body(d_ref, o_ref):
        x = jnp.concatenate([jnp.zeros((BAND_TQ, BAND_TQ), F32), d_ref[0]], axis=1)
        h = BAND_TQ // 2
        while h >= 8:
            x = x[:h] + pltpu.roll(x[h:2 * h], TOEP_W - h, 1)
            h //= 2
        o_ref[0] = jnp.sum(_skew_bits(x, left=True), axis=0, keepdims=True)

    return pl.pallas_call(
        body, name=name, grid=(H,),
        in_specs=[pl.BlockSpec((1, BAND_TQ, BAND_W), lambda h: (h, 0, 0))],
        out_specs=pl.BlockSpec((1, 1, TOEP_W), lambda h: (h, 0, 0)),
        out_shape=jax.ShapeDtypeStruct((H, 1, TOEP_W), F32),
        compiler_params=_params(("parallel",)),
    )(dbias).reshape(H, TOEP_W)


_HBM = pl.BlockSpec(memory_space=pltpu.HBM)
_SEM = pl.BlockSpec(memory_space=pltpu.SEMAPHORE)
_EFFECT = pltpu.SideEffectType.DATAFLOW_SIDE_EFFECTING


def _peers():
    x, y, c = lax.axis_index("x"), lax.axis_index("y"), lax.axis_index("c")
    out = []
    for k in range(1, N_DEV):
        peer = (1 - x if (k >> 2) & 1 else x, 1 - y if (k >> 1) & 1 else y, 1 - c if k & 1 else c)
        out.append((peer, 4 * peer[0] + 2 * peer[1] + peer[2]))
    return 4 * x + 2 * y + c, out


def _split_copies(ins, lands, scatter, send_sem, recv_sem, arriving):
    me, peers = _peers()
    out = []
    for a in range(len(ins)):
        for peer, idx in peers:
            out.append(pltpu.make_async_remote_copy(
                src_ref=ins[a].at[idx] if scatter[a] else ins[a],
                dst_ref=lands[a].at[idx if arriving else me], send_sem=send_sem, recv_sem=recv_sem,
                device_id=peer, device_id_type=pl.DeviceIdType.MESH))
    return out


def _landing_zones(arrays, scatter):
    return [lax.empty((N_DEV,) + (a.shape[1:] if s else a.shape), a.dtype) for a, s in zip(arrays, scatter)]


def _place_own(arrays, scatter, name):
    n = len(arrays)
    lands = _landing_zones(arrays, scatter)
    me = (4 * lax.axis_index("x") + 2 * lax.axis_index("y") + lax.axis_index("c")).astype(jnp.int32).reshape(1)

    def body(me_ref, *refs):
        for a in range(n):
            refs[2 * n + a][...] = refs[a][...].reshape(refs[2 * n + a].shape)

    def row_spec(shape):
        zeros = (0,) * (len(shape) - 1)
        return pl.BlockSpec((1,) + tuple(shape[1:]), lambda i, me_ref: (me_ref[0],) + zeros)

    in_specs = [row_spec(a.shape) if s else pl.BlockSpec(a.shape, lambda i, me_ref, nd=a.ndim: (0,) * nd)
                for a, s in zip(arrays, scatter)]
    return pl.pallas_call(
        body, name=name,
        out_shape=[jax.ShapeDtypeStruct(l.shape, l.dtype) for l in lands],
        grid_spec=pltpu.PrefetchScalarGridSpec(
            num_scalar_prefetch=1, grid=(1,),
            in_specs=in_specs + [pl.BlockSpec(memory_space=pl.ANY)] * n,
            out_specs=[row_spec(l.shape) for l in lands]),
        input_output_aliases={1 + n + i: i for i in range(n)},
        compiler_params=_params(("arbitrary",)),
    )(me, *arrays, *lands)


def _exchange_start_groups(groups, scatter, after, name, lands=None):
    sizes = [len(g) for g in groups]
    arrays = [a for g in groups for a in g]
    n, ng = len(arrays), len(groups)
    flags = list(scatter) if isinstance(scatter, (list, tuple)) else [scatter] * n
    if lands is None:
        lands = list(_place_own(arrays, flags, name=name.replace("_start_", "_own_")))
    else:
        lands = [l for g in lands for l in g]
    starts = np.cumsum([0] + sizes)

    def body(*refs):
        ins, lnd = refs[:n], refs[n:2 * n]
        sems = refs[2 * n + 1:2 * n + 1 + 2 * ng]
        token = refs[-1]
        for g in range(ng):
            sl = slice(starts[g], starts[g + 1])
            for cp in _split_copies(ins[sl], lnd[sl], flags[sl], sems[2 * g], sems[2 * g + 1], arriving=False):
                cp.start()
        token[...] = jnp.zeros_like(token)

    hbm = lambda a: pltpu.HBM(a.shape, a.dtype)
    out = pl.pallas_call(
        body, name=name,
        out_shape=(*[pltpu.SemaphoreType.DMA(())] * (2 * ng),
                   *[hbm(a) for a in arrays], *[hbm(a) for a in lands],
                   jax.ShapeDtypeStruct((8, LANES), F32)),
        in_specs=[_HBM] * (2 * n) + [pl.BlockSpec(memory_space=pl.ANY)],
        out_specs=(*[_SEM] * (2 * ng), *([_HBM] * (2 * n)), pl.BlockSpec(memory_space=pltpu.VMEM)),
        input_output_aliases={i: 2 * ng + i for i in range(2 * n)},
        compiler_params=pltpu.CompilerParams(has_side_effects=_EFFECT),
    )(*[pltpu.with_memory_space_constraint(a, pltpu.HBM) for a in list(arrays) + lands], after)
    ins_out, lands_out = out[2 * ng:2 * ng + n], out[2 * ng + n:2 * ng + 2 * n]
    handles = [(out[2 * g], out[2 * g + 1], list(ins_out[starts[g]:starts[g + 1]]),
                list(lands_out[starts[g]:starts[g + 1]]), tuple(flags[starts[g]:starts[g + 1]]))
               for g in range(ng)]
    return handles, out[-1]


def _exchange_start(arrays, scatter, after, name):
    handles, token = _exchange_start_groups([list(arrays)], list(scatter), after, name)
    return handles[0], token


def _exchange_wait(handle, after, name):
    send_sem, recv_sem, ins, lands, scatter = handle
    n = len(ins)
    after = after if isinstance(after, tuple) else (after,)

    def body(*refs):
        i_ref, l_ref = refs[:n], refs[n:2 * n]
        s_sem, r_sem = refs[2 * n:2 * n + 2]
        for cp in _split_copies(i_ref, l_ref, scatter, s_sem, r_sem, arriving=False):
            cp.wait_send()
        for cp in _split_copies(i_ref, l_ref, scatter, s_sem, r_sem, arriving=True):
            cp.wait_recv()

    hbm = lambda a: pltpu.HBM(a.shape, a.dtype)
    out = pl.pallas_call(
        body, name=name,
        out_shape=tuple(hbm(a) for a in ins + lands),
        in_specs=[_HBM] * (2 * n) + [_SEM, _SEM] + [pl.BlockSpec(memory_space=pl.ANY)] * len(after),
        out_specs=tuple([_HBM] * (2 * n)),
        input_output_aliases={i: i for i in range(2 * n)},
        compiler_params=pltpu.CompilerParams(has_side_effects=_EFFECT),
    )(*ins, *lands, send_sem, recv_sem, *after)
    return list(out[n:])


_SIBLING = 1
_CHIPS = (4, 2, 6)


def _peer_of(k):
    x, y, c = lax.axis_index("x"), lax.axis_index("y"), lax.axis_index("c")
    peer = (1 - x if (k >> 2) & 1 else x, 1 - y if (k >> 1) & 1 else y, 1 - c if k & 1 else c)
    return peer, 4 * peer[0] + 2 * peer[1] + peer[2]


def _rcopy(src, dst, send_sem, recv_sem, k):
    return pltpu.make_async_remote_copy(src_ref=src, dst_ref=dst, send_sem=send_sem, recv_sem=recv_sem,
                                        device_id=_peer_of(k)[0], device_id_type=pl.DeviceIdType.MESH)


def _gather2_start(groups, lands, after, name):
    sizes = [len(g) for g in groups]
    arrays = [a for g in groups for a in g]
    lands = [l for g in lands for l in g]
    n, ng = len(arrays), len(groups)
    starts = np.cumsum([0] + sizes)

    def body(*refs):
        ins, lnd = refs[:n], refs[n:2 * n]
        sems = refs[2 * n + 1:2 * n + 1 + 4 * ng]
        me, _ = _peers()
        for g in range(ng):
            send_d, recv_d, send_i, recv_i = sems[4 * g:4 * g + 4]
            for a in range(starts[g], starts[g + 1]):
                for k in _CHIPS:
                    _rcopy(ins[a], lnd[a].at[me], send_i, recv_i, k).start()
                _rcopy(ins[a], lnd[a].at[me], send_d, recv_d, _SIBLING).start()
        refs[-1][...] = jnp.zeros_like(refs[-1])

    hbm = lambda a: pltpu.HBM(a.shape, a.dtype)
    out = pl.pallas_call(
        body, name=name,
        out_shape=(*[pltpu.SemaphoreType.DMA(())] * (4 * ng), *[hbm(a) for a in arrays], *[hbm(a) for a in lands],
                   jax.ShapeDtypeStruct((8, LANES), F32)),
        in_specs=[_HBM] * (2 * n) + [pl.BlockSpec(memory_space=pl.ANY)],
        out_specs=(*[_SEM] * (4 * ng), *([_HBM] * (2 * n)), pl.BlockSpec(memory_space=pltpu.VMEM)),
        input_output_aliases={i: 4 * ng + i for i in range(2 * n)},
        compiler_params=pltpu.CompilerParams(has_side_effects=_EFFECT),
    )(*[pltpu.with_memory_space_constraint(a, pltpu.HBM) for a in arrays + lands], after)
    ins_out, lands_out = out[4 * ng:4 * ng + n], out[4 * ng + n:4 * ng + 2 * n]
    handles = [dict(sems=out[4 * g:4 * g + 4], ins=list(ins_out[starts[g]:starts[g + 1]]),
                    lands=list(lands_out[starts[g]:starts[g + 1]])) for g in range(ng)]
    return handles, out[-1]


def _gather2_pass_on(handle, after, name):
    lands, recv_i = handle["lands"], handle["sems"][3]
    n = len(lands)
    after = after if isinstance(after, tuple) else (after,)

    def body(*refs):
        lnd, r_i = refs[:n], refs[n]
        send_f, recv_f = refs[n + 1 + len(after):n + 3 + len(after)]
        for a in range(n):
            for k in _CHIPS:
                row = _peer_of(k)[1]
                _rcopy(lnd[a].at[row], lnd[a].at[row], send_f, r_i, k).wait_recv()
        for a in range(n):
            for k in _CHIPS:
                row = _peer_of(k)[1]
                _rcopy(lnd[a].at[row], lnd[a].at[row], send_f, recv_f, _SIBLING).start()
        refs[-1][...] = jnp.zeros_like(refs[-1])

    hbm = lambda a: pltpu.HBM(a.shape, a.dtype)
    out = pl.pallas_call(
        body, name=name,
        out_shape=(pltpu.SemaphoreType.DMA(()), pltpu.SemaphoreType.DMA(()), *[hbm(a) for a in lands],
                   jax.ShapeDtypeStruct((8, LANES), F32)),
        in_specs=[_HBM] * n + [_SEM] + [pl.BlockSpec(memory_space=pl.ANY)] * len(after),
        out_specs=(_SEM, _SEM, *([_HBM] * n), pl.BlockSpec(memory_space=pltpu.VMEM)),
        input_output_aliases={i: 2 + i for i in range(n)},
        compiler_params=pltpu.CompilerParams(has_side_effects=_EFFECT),
    )(*lands, recv_i, *after)
    return dict(handle, lands=list(out[2:2 + n]), passed=(out[0], out[1])), out[-1]


def _gather2_wait(handle, after, name):
    ins, lands = handle["ins"], handle["lands"]
    send_d, recv_d, send_i, _ = handle["sems"]
    send_f, recv_f = handle["passed"]
    n = len(ins)
    after = after if isinstance(after, tuple) else (after,)

    def body(*refs):
        i_ref, lnd = refs[:n], refs[n:2 * n]
        s_d, r_d, s_i, s_f, r_f = refs[2 * n:2 * n + 5]
        me, _ = _peers()
        sib = _peer_of(_SIBLING)[1]
        for a in range(n):
            _rcopy(i_ref[a], lnd[a].at[sib], s_d, r_d, _SIBLING).wait_send()
            _rcopy(i_ref[a], lnd[a].at[sib], s_d, r_d, _SIBLING).wait_recv()
            for k in _CHIPS:
                row = _peer_of(k)[1]
                _rcopy(i_ref[a], lnd[a].at[me], s_i, r_d, k).wait_send()
                _rcopy(lnd[a].at[row], lnd[a].at[row], s_f, r_f, _SIBLING).wait_send()
                _rcopy(lnd[a].at[row], lnd[a].at[_peer_of(k ^ _SIBLING)[1]], s_f, r_f, _SIBLING).wait_recv()

    hbm = lambda a: pltpu.HBM(a.shape, a.dtype)
    out = pl.pallas_call(
        body, name=name,
        out_shape=tuple(hbm(a) for a in ins + lands),
        in_specs=[_HBM] * (2 * n) + [_SEM] * 5 + [pl.BlockSpec(memory_space=pl.ANY)] * len(after),
        out_specs=tuple([_HBM] * (2 * n)),
        input_output_aliases={i: i for i in range(2 * n)},
        compiler_params=pltpu.CompilerParams(has_side_effects=_EFFECT),
    )(*ins, *lands, send_d, recv_d, send_i, send_f, recv_f, *after)
    return list(out[n:])


def _adamw(w, parts, m, v, name="adamw"):
    R, C = w.shape
    L = len(parts)
    rl = R // L
    tr = max([t for t in range(16, 257, 16) if rl % t == 0], default=rl)
    nb = rl // tr
    c1 = 1.0 - ADAM_B1 ** ADAM_STEP
    c2 = 1.0 - ADAM_B2 ** ADAM_STEP

    n = L * nb
    NB = 3

    def body(*refs):
        w_hbm, p_hbm, (m_hbm, v_hbm) = refs[0], refs[1:1 + L], refs[1 + L:3 + L]
        outs = refs[3 + L:7 + L]
        wb, mb, vb, pb, ob, isem, osem = refs[7 + L:]

        def reads(s):
            k = s % NB
            rows = pl.ds(pl.multiple_of(s * tr, tr), tr)
            prow = pl.ds(pl.multiple_of((s % nb) * tr, tr), tr)
            base = [pltpu.make_async_copy(src.at[rows], dst.at[k], isem.at[j, k])
                    for j, (src, dst) in enumerate(((w_hbm, wb), (m_hbm, mb), (v_hbm, vb)))]
            ps = [pltpu.make_async_copy(p.at[:, prow], pb.at[k], isem.at[3, k]) for p in p_hbm]
            return base, ps

        def start_reads(s):
            base, ps = reads(s)
            for c in base:
                c.start()
            for j, c in enumerate(ps):
                pl.when(s // nb == j)(c.start)

        def writes(s):
            k = s % 2
            rows = pl.ds(pl.multiple_of(s * tr, tr), tr)
            return [pltpu.make_async_copy(ob.at[k, j], outs[j].at[rows], osem.at[j, k]) for j in range(4)]

        for s in range(min(NB, n)):
            start_reads(s)

        def step(s, carry):
            k = s % NB
            base, ps = reads(s)
            for c in base:
                c.wait()
            ps[0].wait()

            @pl.when(s >= 2)
            def _():
                for c in writes(s - 2):
                    c.wait()

            g = pb[k, 0].astype(F32)
            for i in range(1, N_DEV):
                g = g + pb[k, i].astype(F32)
            nm = ADAM_B1 * mb[k] + (1.0 - ADAM_B1) * g
            nv = ADAM_B2 * vb[k] + (1.0 - ADAM_B2) * (g * g)
            o = s % 2
            ob[o, 0] = g
            ob[o, 1] = -ADAM_LR * ((nm / c1) / (jnp.sqrt(nv / c2) + ADAM_EPS) + ADAM_WD * wb[k])
            ob[o, 2] = nm
            ob[o, 3] = nv
            for c in writes(s):
                c.start()

            @pl.when(s + NB < n)
            def _():
                start_reads(s + NB)
            return carry

        lax.fori_loop(0, n, step, 0)
        for s in range(max(0, n - 2), n):
            for c in writes(s):
                c.wait()

    any_spec = pl.BlockSpec(memory_space=pl.ANY)
    return pl.pallas_call(
        body, name=name,
        in_specs=[any_spec] * (3 + L),
        out_specs=[any_spec] * 4,
        out_shape=[jax.ShapeDtypeStruct((R, C), F32)] * 4,
        scratch_shapes=[pltpu.VMEM((NB, tr, C), F32)] * 3
        + [pltpu.VMEM((NB, N_DEV, tr, C), parts[0].dtype), pltpu.VMEM((2, 4, tr, C), F32),
           pltpu.SemaphoreType.DMA((4, NB)), pltpu.SemaphoreType.DMA((4, 2))],
        compiler_params=pltpu.CompilerParams(vmem_limit_bytes=VMEM_LIMIT),
    )(w, *parts, m, v)


_O1 = Q_LORA
_O2 = _O1 + KV_LORA
_O3 = _O2 + MLA_ROPE
_NB = SB_HEADS * SB_DIM
IN_W = _O2 + LANES + 3 * _NB
COL_KR = _O2 // LANES
COL_SB = COL_KR + 1


def _w_in_local(w):
    kr = w[_O2:_O3]
    pad = jnp.zeros((LANES - 2 * MLA_ROPE, w.shape[1]), w.dtype)
    return jnp.concatenate([w[:_O2], kr, kr, pad, w[_O3:]], axis=0)


def _w_in_grad(g):
    kr = (g[_O2:_O2 + MLA_ROPE].astype(F32) + g[_O2 + MLA_ROPE:_O2 + 2 * MLA_ROPE].astype(F32)).astype(g.dtype)
    return jnp.concatenate([g[:_O2], kr, g[_O2 + LANES:]], axis=0)


def _w_uq_local(w):
    w3 = w.reshape(MLA_HEADS // 2, 2, MLA_NOPE + MLA_ROPE, w.shape[1])
    nope = w3[:, :, :MLA_NOPE].reshape(MLA_HEADS // 2, 2 * MLA_NOPE, w.shape[1])
    rope = w3[:, :, MLA_NOPE:].reshape(MLA_HEADS // 2, 2 * MLA_ROPE, w.shape[1])
    pad = jnp.zeros((MLA_HEADS // 2, LANES - 2 * MLA_ROPE, w.shape[1]), w.dtype)
    return jnp.concatenate([nope, rope, pad], axis=1).reshape(-1, w.shape[1])


def _w_uq_grad(g):
    g3 = g.reshape(MLA_HEADS // 2, 2 * LANES, g.shape[1])
    nope = g3[:, :2 * MLA_NOPE].reshape(MLA_HEADS // 2, 2, MLA_NOPE, g.shape[1])
    rope = g3[:, LANES:LANES + 2 * MLA_ROPE].reshape(MLA_HEADS // 2, 2, MLA_ROPE, g.shape[1])
    return jnp.concatenate([nope, rope], axis=2).reshape(-1, g.shape[1])


def _w_ukv_local(w):
    w3 = w.reshape(MLA_HEADS, MLA_NOPE + MLA_V, w.shape[1])
    return jnp.concatenate([w3[:, :MLA_NOPE].reshape(-1, w.shape[1]),
                            w3[:, MLA_NOPE:].reshape(-1, w.shape[1])], axis=0)


def _w_ukv_grad(g):
    half = MLA_HEADS * MLA_NOPE
    kn = g[:half].reshape(MLA_HEADS, MLA_NOPE, g.shape[1])
    vv = g[half:].reshape(MLA_HEADS, MLA_V, g.shape[1])
    return jnp.concatenate([kn, vv], axis=1).reshape(-1, g.shape[1])


def _rope_tables(T):
    pos = jnp.arange(T, dtype=F32)
    inv_freq = ROPE_THETA ** (-jnp.arange(0, MLA_ROPE, 2, dtype=F32) / MLA_ROPE)
    ang = pos[:, None] * inv_freq[None, :]
    cos, sin = jnp.cos(ang), jnp.sin(ang)
    ones = jnp.ones((T, LANES - 2 * MLA_ROPE), F32)
    cos_k = jnp.concatenate([cos, cos, cos, cos, ones], axis=1)
    sin_k = jnp.concatenate([-sin, sin, -sin, sin, 0.0 * ones], axis=1)
    cos_q = jnp.concatenate([jnp.ones((T, LANES), F32), cos_k], axis=1)
    sin_q = jnp.concatenate([jnp.zeros((T, LANES), F32), sin_k], axis=1)
    return cos_q, sin_q, cos_k, sin_k


def _bias_diag_index():
    ell = np.arange(TOEP_W)
    return np.clip(BAND_W - ell, -REL_CLIP, REL_CLIP) + REL_CLIP


def _local_step(x, target, small, get_weights, put_grads, prefetch):
    T = x.shape[0]
    cos_q, sin_q, cos_k, sin_k = _rope_tables(T)
    G = {}
    W = dict(small)

    u0 = _rms_fwd(x, W["g_mix"][0:1], name="rms_mix0")
    bias_w = _toeplitz(W["od_rel_bias"][:, _bias_diag_index()])
    W.update(get_weights("in0", (u0, bias_w)))
    proj = _mm(u0, W["w_in_t"], dims="nt", name="proj_in")
    W.update(get_weights("mix0", proj))
    c_q, c_kv = proj[:, :_O1], proj[:, _O1:_O2]
    nq = _rms_fwd(c_q, W["g_cq"], name="rms_cq")
    nkv = _rms_fwd(c_kv, W["g_ckv"], name="rms_ckv")
    qa_raw = _mm(nq, W["w_uq_t"], dims="nt", name="proj_uq")
    kv = _mm(nkv, W["w_ukv_t"], dims="nt", out_dtype=BF16, name="proj_ukv")
    kr = _rope(proj, cos_k, sin_k, COL_KR, 1, BF16, name="rope_k")
    o_a, lse = _mla_fwd(qa_raw, cos_q, sin_q, kv, kr)
    o_b, o_b32, w_b, sp_b = _sb_fwd(proj, COL_SB, prefetch("ffn0", o_a))
    o_ab = jnp.concatenate([o_a, o_b], axis=1)
    h1 = _mm(o_ab, W["ev_w_out"], res=x, name="out_ev")

    def ffn_fwd(h, layer):
        W.update(get_weights(f"ffn{layer}", h))
        return _ffn_fwd(h, W["g_ffn"][layer:layer + 1], W[f"w_gate_t{layer}"], W[f"w_up_t{layer}"],
                        W[f"w_down{layer}"], name=f"ffn_fwd{layer}")

    h2, u1, a0, b0 = ffn_fwd(h1, 0)

    W.update(get_weights("mix1", h2))
    u2 = _rms_fwd(h2, W["g_mix"][1:2], name="rms_mix1")
    qkv = _mm(u2, W["od_w_qkv_t"], dims="nt", out_dtype=BF16, name="proj_qkv")
    nc = C_HEADS * C_DIM
    pad = ((PAD_KEYS, 0), (0, 0))
    k_pad, v_pad = jnp.pad(qkv[:, nc:2 * nc], pad), jnp.pad(qkv[:, 2 * nc:], pad)
    o_c, p_c = _band_fwd(qkv, k_pad, v_pad, bias_w)
    h3 = _mm(o_c, W["od_w_out"], res=h2, name="out_od")
    h4, u3, a1, b1 = ffn_fwd(h3, 1)

    loss, dh, dhb, G["g_final"] = _loss_head(h4, W["g_final"], target)

    def ffn_bwd(dh, dhb, h, u, a, b, layer):
        du, g_gate, g_up, g_down = _ffn_bwd(dhb, u, a, b, W[f"w_gate_t{layer}"], W[f"w_up_t{layer}"],
                                            W[f"w_down{layer}"], name=f"ffn_bwd{layer}")
        tok = put_grads(f"ffn{layer}", {"w_gate_t": g_gate, "w_up_t": g_up, "w_down": g_down})
        return _rms_bwd(h, W["g_ffn"][layer:layer + 1] + tok[:1, :1], du, dres=dh, name=f"rms_ffn_bwd{layer}")

    dh3, dh3b, g_gffn1 = ffn_bwd(dh, dhb, h3, u3, a1, b1, 1)

    do_c = _mm(dh3b, W["od_w_out"], dims="nt", name="out_od_dx")
    g_od_out = _mm(o_c, dh3b, dims="tn", out_dtype=BF16, name="out_od_dw")
    dq_c, dk_p, dv_p, dbias_w = _band_bwd(qkv, k_pad, v_pad, p_c, do_c)
    dqkv = jnp.concatenate([dq_c, dk_p[PAD_KEYS:], dv_p[PAD_KEYS:]], axis=1)
    tok = put_grads("mix1", {"od_w_qkv_t": _mm(dqkv, u2, dims="tn", out_dtype=BF16, name="proj_qkv_dw"),
                             "od_w_out": g_od_out})
    ddiag = _toeplitz_bwd(dbias_w)
    n_far = BAND_W - REL_CLIP + 1
    G["od_rel_bias"] = jnp.concatenate(
        [jnp.zeros((C_HEADS, REL_CLIP - BAND_TQ + 1), F32), ddiag[:, n_far:][:, ::-1],
         jnp.sum(ddiag[:, :n_far], axis=1, keepdims=True)], axis=1)
    dh2, dh2b, g_gmix1 = _mm_rms_bwd(dqkv, W["od_w_qkv_t"], h2, W["g_mix"][1:2] + tok[:1, :1], dh3,
                                     name="proj_qkv_dx")

    dh1, dh1b, g_gffn0 = ffn_bwd(dh2, dh2b, h1, u1, a0, b0, 0)
    G["g_ffn"] = jnp.concatenate([g_gffn0, g_gffn1], axis=0)

    do_ab = _mm(dh1b, W["ev_w_out"], dims="nt", name="out_ev_dx")
    g0 = {"ev_w_out": _mm(o_ab, dh1b, dims="tn", out_dtype=BF16, name="out_ev_dw")}
    dqa_raw, dkn, dva, dkr = _mla_bwd(qa_raw, cos_q, sin_q, kv, kr, o_a, lse, do_ab, 0)
    dlat, g0["w_uq_t"], g0["w_ukv_t"], G["g_cq"], G["g_ckv"] = _latent_bwd(
        proj, nq, nkv, dqa_raw, dkn, dva, dkr, cos_k, sin_k, W["g_cq"], W["g_ckv"], W["w_uq_t"], W["w_ukv_t"])
    tok = put_grads("mix0", g0)
    dqb, dkb, dvb = _sb_bwd(proj, COL_SB, o_b32, w_b, sp_b, do_ab, MLA_HEADS // 2, tok)
    dproj = jnp.concatenate([dlat, dqb, dkb, dvb], axis=1)
    tok = put_grads("in0", {"w_in_t": _mm(dproj, u0, dims="tn", name="proj_in_dw")})
    dx, _, g_gmix0 = _mm_rms_bwd(dproj, W["w_in_t"], x, W["g_mix"][0:1] + tok[:1, :1], dh1, name="proj_in_dx")
    G["g_mix"] = jnp.concatenate([g_gmix0, g_gmix1], axis=0)
    return loss[0, 0], dx, G


_BIG = ["ev_w_in", "ev_w_uq", "ev_w_ukv", "ev_w_out", "od_w_qkv", "od_w_out", "w_gate", "w_up", "w_down"]
_COL_SHARDED = {"ev_w_in", "ev_w_uq", "ev_w_ukv", "od_w_qkv", "w_gate", "w_up"}
_SMALL = ["ev_g_cq", "ev_g_ckv", "od_rel_bias", "g_mix", "g_ffn", "g_final"]
_GROUPS = {
    "in0": ["ev_w_in"],
    "mix0": ["ev_w_uq", "ev_w_ukv", "ev_w_out"],
    "ffn0": ["w_gate0", "w_up0", "w_down0"],
    "mix1": ["od_w_qkv", "od_w_out"],
    "ffn1": ["w_gate1", "w_up1", "w_down1"],
}
_GROUP_SRC = {n + str(l): (n, l) for n in ("w_gate", "w_up", "w_down") for l in (0, 1)}
_BATCHES = {"in0": ["in0"], "layer0": ["mix0", "ffn0"], "layer1": ["mix1", "ffn1"]}
_BATCH_OF = {grp: batch for batch, grps in _BATCHES.items() for grp in grps}
_SMALL_ROWS = 8
_SMALL_COLS = 1792


def _pack_small(vals):
    flat = jnp.concatenate([v.reshape(-1).astype(F32) for v in vals])
    flat = jnp.pad(flat, (0, _SMALL_ROWS * _SMALL_COLS - flat.shape[0]))
    return flat.reshape(_SMALL_ROWS, _SMALL_COLS)


def _unpack_small(packed, like):
    flat = packed.reshape(-1)
    out, off = [], 0
    for v in like:
        out.append(flat[off:off + v.size].reshape(v.shape))
        off += v.size
    return out


def kernel(x, ev_w_in, ev_g_cq, ev_w_uq, ev_g_ckv, ev_w_ukv, ev_w_out, od_w_qkv, od_rel_bias, od_w_out, g_mix, g_ffn, w_gate, w_up, w_down, g_final, loss_target, m_ev_w_in, m_ev_g_cq, m_ev_w_uq, m_ev_g_ckv, m_ev_w_ukv, m_ev_w_out, m_od_w_qkv, m_od_rel_bias, m_od_w_out, m_g_mix, m_g_ffn, m_w_gate, m_w_up, m_w_down, m_g_final, v_ev_w_in, v_ev_g_cq, v_ev_w_uq, v_ev_g_ckv, v_ev_w_ukv, v_ev_w_out, v_od_w_qkv, v_od_rel_bias, v_od_w_out, v_g_mix, v_g_ffn, v_w_gate, v_w_up, v_w_down, v_g_final):
    args = dict(locals())
    w = {n: args[n] for n in _BIG + _SMALL}
    mom = {n: args["m_" + n] for n in _BIG + _SMALL}
    var = {n: args["v_" + n] for n in _BIG + _SMALL}

    own = {}
    for grp, names in _GROUPS.items():
        for n in names:
            base, layer = _GROUP_SRC.get(n, (n, 0))
            shard = w[base][layer:layer + 1]
            own[n] = (jnp.swapaxes(shard, 1, 2) if base in _COL_SHARDED else shard).astype(BF16)
    placed = dict(zip(own, _place_own(list(own.values()), [False] * len(own), name="gather_own")))
    handles, token = _gather2_start(
        [[own[n] for n in names] for names in _GROUPS.values()],
        [[placed[n] for n in names] for names in _GROUPS.values()], x[0, :8, :LANES], name="gather_start")
    gather = dict(zip(_GROUPS, handles))
    pass_before = {"in0": ["in0"], "mix0": ["mix0"]}
    pass_after = {"ffn0": ("mix1", "g_ffn"), "mix1": ("ffn1", "g_mix")}

    def prefetch(grp, after):
        gather[grp], tok = _gather2_pass_on(gather[grp], after, name="gather_pass_" + grp)
        return tok

    def get_weights(grp, after):
        names = _GROUPS[grp]
        after = token if after is None else after
        for g in pass_before.get(grp, []):
            gather[g], _ = _gather2_pass_on(gather[g], after, name="gather_pass_" + g)
        lands = _gather2_wait(gather[grp], after, name="gather_wait_" + grp)
        full = {n: l.reshape(-1, l.shape[-1]) for n, l in zip(names, lands)}
        out = {}
        if grp in pass_after:
            g, gain = pass_after[grp]
            gather[g], tok = _gather2_pass_on(gather[g], lands[0], name="gather_pass_" + g)
            out[gain] = small[gain] + tok[:1, :1]
        if grp == "in0":
            out.update({"w_in_t": _w_in_local(full["ev_w_in"])})
        elif grp == "mix0":
            out.update({"w_uq_t": _w_uq_local(full["ev_w_uq"]), "w_ukv_t": _w_ukv_local(full["ev_w_ukv"]),
                        "ev_w_out": full["ev_w_out"]})
        elif grp == "mix1":
            out.update({"od_w_qkv_t": full["od_w_qkv"], "od_w_out": full["od_w_out"]})
        else:
            layer = grp[-1]
            out.update({"w_gate_t" + layer: full["w_gate" + layer], "w_up_t" + layer: full["w_up" + layer],
                        "w_down" + layer: full["w_down" + layer]})
        return out

    scatter, pending = {}, {}

    def put_grads(grp, g):
        if grp == "in0":
            g = {"ev_w_in": _w_in_grad(g["w_in_t"])}
        elif grp == "mix0":
            g = {"ev_w_uq": _w_uq_grad(g["w_uq_t"]), "ev_w_ukv": _w_ukv_grad(g["w_ukv_t"]),
                 "ev_w_out": g["ev_w_out"]}
        elif grp == "mix1":
            g = {"od_w_qkv": g["od_w_qkv_t"], "od_w_out": g["od_w_out"]}
        else:
            layer = grp[-1]
            g = {"w_gate" + layer: g["w_gate_t"], "w_up" + layer: g["w_up_t"], "w_down" + layer: g["w_down"]}
        pending.update({n: v.reshape(N_DEV, 1, v.shape[0] // N_DEV, v.shape[1]).astype(BF16) for n, v in g.items()})
        batch = _BATCH_OF[grp]
        names = [n for gr in _BATCHES[batch] for n in _GROUPS[gr]]
        if batch == "in0" or not all(n in pending for n in names):
            return jnp.zeros((8, LANES), F32)
        send = [pending[n] for n in names]
        scatter[batch], tok = _exchange_start(send, [True] * len(names), send[0], name="scatter_start_" + batch)
        return tok

    small = {"g_cq": ev_g_cq, "g_ckv": ev_g_ckv, "od_rel_bias": od_rel_bias[0],
             "g_mix": g_mix + token[0, 0], "g_ffn": g_ffn, "g_final": g_final.reshape(1, -1)}
    loss_part, dx, G = _local_step(x[0], loss_target[0], small, get_weights, put_grads, prefetch)
    g_small = _pack_small([G["g_cq"], G["g_ckv"], G["od_rel_bias"], G["g_mix"], G["g_ffn"], G["g_final"],
                           loss_part.reshape(1)])
    scatter["in0"], _ = _exchange_start([pending["ev_w_in"], g_small], [True, False], dx, name="scatter_start_in0")

    grads, deltas, new_m, new_v = {}, {}, {}, {}
    parts, after = {}, dx

    def wait_parts(batch, after):
        lands = _exchange_wait(scatter[batch], after, name="scatter_wait_" + batch)
        parts.update(zip([n for grp in _BATCHES[batch] for n in _GROUPS[grp]], lands))
        return lands[0]

    def adamw(n):
        col = n in _COL_SHARDED
        rows = lambda a: (jnp.swapaxes(a, 1, 2) if col else a).reshape(-1, a.shape[1 if col else 2])
        layers = [parts[n]] if n in parts else [parts[n + "0"], parts[n + "1"]]
        res = _adamw(rows(w[n]), [p.reshape(N_DEV, -1, p.shape[-1]) for p in layers], rows(mom[n]), rows(var[n]),
                     name="adamw_" + n)
        L, a1, a2 = w[n].shape
        back = lambda r: jnp.swapaxes(r.reshape(L, a2, a1), 1, 2) if col else r.reshape(L, a1, a2)
        grads[n], deltas[n], new_m[n], new_v[n] = [back(r) for r in res]
        return res[0]

    for batch in ("layer1", "layer0"):
        after = wait_parts(batch, after)
    parts["ev_w_in"], small_parts = _exchange_wait(scatter["in0"], tuple(adamw(n) for n in _BIG[1:]),
                                                   name="scatter_wait_in0")
    adamw("ev_w_in")
    small_w = [w[n] for n in _SMALL]
    loss = jnp.sum(small_parts.reshape(N_DEV, -1)[:, sum(v.size for v in small_w)])
    res = _adamw(_pack_small(small_w), [small_parts], _pack_small([mom[n] for n in _SMALL]),
                 _pack_small([var[n] for n in _SMALL]), name="adamw_small")
    for d, packed in zip((grads, deltas, new_m, new_v), res):
        for n, val in zip(_SMALL, _unpack_small(packed, small_w)):
            d[n] = val

    order = ["ev_w_in", "ev_g_cq", "ev_w_uq", "ev_g_ckv", "ev_w_ukv", "ev_w_out", "od_w_qkv", "od_rel_bias",
             "od_w_out", "g_mix", "g_ffn", "w_gate", "w_up", "w_down", "g_final"]
    out = [loss, dx[None]]
    for d in (grads, deltas, new_m, new_v):
        out += [d[n] for n in order]
    return tuple(out)
```

```python
import functools

import numpy as np
import jax
import jax.numpy as jnp
from jax import lax
from jax.experimental import pallas as pl
from jax.experimental.pallas import tpu as pltpu

F32 = jnp.float32
BF16 = jnp.bfloat16

D_MODEL = 1024
CHUNK = 64
MLA_HEADS = 8
MLA_NOPE = 64
MLA_ROPE = 32
MLA_V = 64
Q_LORA = 384
KV_LORA = 256
ROPE_THETA = 10000.0
SB_HEADS = 8
SB_DIM = 64
C_HEADS = 16
C_DIM = 64
LEFT_CHUNKS = 8
REL_CLIP = 256
D_FF = 2816
RMS_EPS = 1e-6
ADAM_LR = 0.001
ADAM_B1 = 0.9
ADAM_B2 = 0.999
ADAM_EPS = 1e-08
ADAM_WD = 0.01
ADAM_STEP = 10

N_DEV = 8
LANES = 128
HEAD = 64
assert HEAD == MLA_NOPE == MLA_V == SB_DIM == C_DIM and 2 * HEAD == LANES
CHUNK_BITS = CHUNK.bit_length() - 1
assert 1 << CHUNK_BITS == CHUNK
VMEM_LIMIT = 56 * 1024 * 1024
NEG = -1e30
PAD_KEYS = LEFT_CHUNKS * CHUNK
BAND_TQ = 128
BAND_W = BAND_TQ + PAD_KEYS
TOEP_W = BAND_W + BAND_TQ

NN = (((1,), (0,)), ((), ()))
NT = (((1,), (1,)), ((), ()))
TN = (((0,), (0,)), ((), ()))


def _dot(a, b, dn):
    return lax.dot_general(a, b, dn, preferred_element_type=F32)


def _pick(dim, pref):
    if dim <= pref:
        return dim
    best = None
    for t in range(LANES, pref + 1, LANES):
        if dim % t == 0:
            best = t
    assert best is not None, (dim, pref)
    return best


def _params(sem):
    return pltpu.CompilerParams(dimension_semantics=sem, vmem_limit_bytes=VMEM_LIMIT)


def _mm(a, b, dims="nn", res=None, out_dtype=F32, name="mm"):
    if dims == "nn":
        (M, K), (K2, N) = a.shape, b.shape
    elif dims == "nt":
        (M, K), (N, K2) = a.shape, b.shape
    else:
        (K, M), (K2, N) = a.shape, b.shape
    assert K == K2, (a.shape, b.shape, dims)
    tm, tn, tk = _pick(M, 1024), _pick(N, 1152), _pick(K, 1024)
    nk = K // tk
    dn = {"nn": NN, "nt": NT, "tn": TN}[dims]
    has_res = res is not None

    def body(*refs):
        if has_res:
            a_ref, b_ref, r_ref, o_ref, acc = refs
        else:
            a_ref, b_ref, o_ref, acc = refs
        k = pl.program_id(2)

        @pl.when(k == 0)
        def _():
            acc[...] = jnp.zeros_like(acc)

        acc[...] += _dot(a_ref[...].astype(BF16), b_ref[...].astype(BF16), dn)

        @pl.when(k == nk - 1)
        def _():
            r = acc[...]
            if has_res:
                r = r + r_ref[...]
            o_ref[...] = r.astype(out_dtype)

    a_spec = (pl.BlockSpec((tk, tm), lambda i, j, k: (k, i)) if dims == "tn"
              else pl.BlockSpec((tm, tk), lambda i, j, k: (i, k)))
    b_spec = (pl.BlockSpec((tn, tk), lambda i, j, k: (j, k)) if dims == "nt"
              else pl.BlockSpec((tk, tn), lambda i, j, k: (k, j)))
    o_spec = pl.BlockSpec((tm, tn), lambda i, j, k: (i, j))
    in_specs = [a_spec, b_spec] + ([o_spec] if has_res else [])
    args = (a, b) + ((res,) if has_res else ())
    return pl.pallas_call(
        body, name=name, grid=(M // tm, N // tn, nk),
        in_specs=in_specs, out_specs=o_spec,
        out_shape=jax.ShapeDtypeStruct((M, N), out_dtype),
        scratch_shapes=[pltpu.VMEM((tm, tn), F32)],
        compiler_params=_params(("parallel", "parallel", "arbitrary")),
    )(*args)


def _rms_fwd(x, g, out_dtype=BF16, name="rms_fwd"):
    T, Fd = x.shape
    tm = _pick(T, 256)

    def body(x_ref, g_ref, o_ref):
        xv = x_ref[...]
        r = lax.rsqrt(jnp.mean(xv * xv, axis=-1, keepdims=True) + RMS_EPS)
        o_ref[...] = (xv * r * g_ref[...]).astype(out_dtype)

    return pl.pallas_call(
        body, name=name, grid=(T // tm,),
        in_specs=[pl.BlockSpec((tm, Fd), lambda i: (i, 0)), pl.BlockSpec((1, Fd), lambda i: (0, 0))],
        out_specs=pl.BlockSpec((tm, Fd), lambda i: (i, 0)),
        out_shape=jax.ShapeDtypeStruct((T, Fd), out_dtype),
        compiler_params=_params(("parallel",)),
    )(x, g)


def _rms_bwd(x, g, dy, dres=None, name="rms_bwd"):
    T, Fd = x.shape
    tm = _pick(T, 256)
    has_res = dres is not None

    def body(*refs):
        if has_res:
            x_ref, g_ref, dy_ref, r_ref, dx_ref, dxb_ref, dg_ref = refs
        else:
            x_ref, g_ref, dy_ref, dx_ref, dxb_ref, dg_ref = refs
        xv, dyv = x_ref[...], dy_ref[...]
        r = lax.rsqrt(jnp.mean(xv * xv, axis=-1, keepdims=True) + RMS_EPS)
        gdy = dyv * g_ref[...]
        dot = jnp.mean(xv * gdy, axis=-1, keepdims=True)
        dx = r * gdy - xv * (r * r * r * dot)
        if has_res:
            dx = dx + r_ref[...]
        dx_ref[...] = dx
        dxb_ref[...] = dx.astype(BF16)

        @pl.when(pl.program_id(0) == 0)
        def _():
            dg_ref[...] = jnp.zeros_like(dg_ref)

        dg_ref[...] += jnp.sum(dyv * xv * r, axis=0, keepdims=True)

    row = pl.BlockSpec((tm, Fd), lambda i: (i, 0))
    vec = pl.BlockSpec((1, Fd), lambda i: (0, 0))
    in_specs = [row, vec, row] + ([row] if has_res else [])
    args = (x, g, dy) + ((dres,) if has_res else ())
    return pl.pallas_call(
        body, name=name, grid=(T // tm,),
        in_specs=in_specs, out_specs=[row, row, vec],
        out_shape=[jax.ShapeDtypeStruct((T, Fd), F32), jax.ShapeDtypeStruct((T, Fd), BF16),
                   jax.ShapeDtypeStruct((1, Fd), F32)],
        compiler_params=_params(("arbitrary",)),
    )(*args)


def _mm_rms_bwd(a, b, x, g, dres, name="mm_rms_bwd"):
    T, K = a.shape
    Fd = b.shape[1]
    tm, tk = _pick(T, 1024), _pick(K, 1024)
    nk = K // tk

    def body(a_ref, b_ref, x_ref, g_ref, r_ref, dx_ref, dxb_ref, dg_ref, acc):
        i, k = pl.program_id(0), pl.program_id(1)

        @pl.when(k == 0)
        def _():
            acc[...] = jnp.zeros_like(acc)

        @pl.when((k == 0) & (i == 0))
        def _():
            dg_ref[...] = jnp.zeros_like(dg_ref)

        acc[...] += _dot(a_ref[...].astype(BF16), b_ref[...].astype(BF16), NN)

        @pl.when(k == nk - 1)
        def _():
            xv, dyv = x_ref[...], acc[...]
            r = lax.rsqrt(jnp.mean(xv * xv, axis=-1, keepdims=True) + RMS_EPS)
            gdy = dyv * g_ref[...]
            dot = jnp.mean(xv * gdy, axis=-1, keepdims=True)
            dx = r * gdy - xv * (r * r * r * dot) + r_ref[...]
            dx_ref[...] = dx
            dxb_ref[...] = dx.astype(BF16)
            dg_ref[...] += jnp.sum(dyv * xv * r, axis=0, keepdims=True)

    row = pl.BlockSpec((tm, Fd), lambda i, k: (i, 0))
    vec = pl.BlockSpec((1, Fd), lambda i, k: (0, 0))
    return pl.pallas_call(
        body, name=name, grid=(T // tm, nk),
        in_specs=[pl.BlockSpec((tm, tk), lambda i, k: (i, k)), pl.BlockSpec((tk, Fd), lambda i, k: (k, 0)),
                  row, vec, row],
        out_specs=[row, row, vec],
        out_shape=[jax.ShapeDtypeStruct((T, Fd), F32), jax.ShapeDtypeStruct((T, Fd), BF16),
                   jax.ShapeDtypeStruct((1, Fd), F32)],
        scratch_shapes=[pltpu.VMEM((tm, Fd), F32)],
        compiler_params=_params(("arbitrary", "arbitrary")),
    )(a, b, x, g, dres)


def _latent_bwd(proj, nq, nkv, dqa, dkn, dva, dkr, cos_k, sin_k, g_cq, g_ckv, w_uq_t, w_ukv_t, name="latent_bwd"):
    T = proj.shape[0]
    tm = _pick(T, 512)
    wl = _O2

    def rms_bwd(xv, gv, dyv):
        r = lax.rsqrt(jnp.mean(xv * xv, axis=-1, keepdims=True) + RMS_EPS)
        gdy = dyv * gv
        dot = jnp.mean(xv * gdy, axis=-1, keepdims=True)
        return r * gdy - xv * (r * r * r * dot), jnp.sum(dyv * xv * r, axis=0, keepdims=True)

    def body(p_ref, nq_ref, nkv_ref, dqa_ref, dkn_ref, dva_ref, dkr_ref, c_ref, s_ref, gq_ref, gkv_ref, wq_ref, wkv_ref,
             dlat_ref, dwq_ref, dwkv_ref, dgq_ref, dgkv_ref):
        @pl.when(pl.program_id(0) == 0)
        def _():
            for ref in (dwq_ref, dwkv_ref, dgq_ref, dgkv_ref):
                ref[...] = jnp.zeros_like(ref)

        dqv = dqa_ref[...]
        dkv = jnp.concatenate([dkn_ref[...], dva_ref[...]], axis=1)
        pv = p_ref[...]
        dc_q, dgq = rms_bwd(pv[:, :_O1], gq_ref[...], _dot(dqv, wq_ref[...], NN))
        dc_kv, dgkv = rms_bwd(pv[:, _O1:], gkv_ref[...], _dot(dkv, wkv_ref[...], NN))
        dkr_raw = _rotate(dkr_ref[...], c_ref[...], -s_ref[...])
        dlat_ref[...] = jnp.concatenate([dc_q, dc_kv, dkr_raw], axis=1).astype(BF16)
        dwq_ref[...] += _dot(dqv, nq_ref[...], TN)
        dwkv_ref[...] += _dot(dkv, nkv_ref[...], TN)
        dgq_ref[...] += dgq
        dgkv_ref[...] += dgkv

    row = lambda w: pl.BlockSpec((tm, w), lambda i: (i, 0))
    const = lambda a: pl.BlockSpec(a.shape, lambda i: (0, 0))
    outs = [jax.ShapeDtypeStruct((T, wl + LANES), BF16), jax.ShapeDtypeStruct(w_uq_t.shape, F32),
            jax.ShapeDtypeStruct(w_ukv_t.shape, F32), jax.ShapeDtypeStruct(g_cq.shape, F32),
            jax.ShapeDtypeStruct(g_ckv.shape, F32)]
    return pl.pallas_call(
        body, name=name, grid=(T // tm,),
        in_specs=[row(wl), row(_O1), row(_O2 - _O1), row(dqa.shape[1]), row(dkn.shape[1]), row(dva.shape[1]),
                  row(LANES), row(LANES), row(LANES), const(g_cq), const(g_ckv), const(w_uq_t), const(w_ukv_t)],
        out_specs=[row(wl + LANES)] + [const(o) for o in outs[1:]],
        out_shape=outs,
        compiler_params=_params(("arbitrary",)),
    )(proj, nq, nkv, dqa, dkn, dva, dkr, cos_k, sin_k, g_cq, g_ckv, w_uq_t, w_ukv_t)


def _loss_head(h, g, target, name="loss_head"):
    T, Fd = h.shape
    tm = _pick(T, 256)

    def body(h_ref, g_ref, t_ref, loss_ref, dh_ref, dhb_ref, dg_ref):
        xv = h_ref[...]
        r = lax.rsqrt(jnp.mean(xv * xv, axis=-1, keepdims=True) + RMS_EPS)
        diff = xv * r * g_ref[...] - t_ref[...]
        part = 0.5 * jnp.sum(jnp.mean(diff * diff, axis=-1, keepdims=True), axis=0, keepdims=True)
        dyv = diff * (1.0 / Fd)
        gdy = dyv * g_ref[...]
        dot = jnp.mean(xv * gdy, axis=-1, keepdims=True)
        dh = r * gdy - xv * (r * r * r * dot)
        dh_ref[...] = dh
        dhb_ref[...] = dh.astype(BF16)

        @pl.when(pl.program_id(0) == 0)
        def _():
            dg_ref[...] = jnp.zeros_like(dg_ref)
            loss_ref[...] = jnp.zeros_like(loss_ref)

        dg_ref[...] += jnp.sum(dyv * xv * r, axis=0, keepdims=True)
        loss_ref[...] += jnp.broadcast_to(part, loss_ref.shape)

    row = pl.BlockSpec((tm, Fd), lambda i: (i, 0))
    vec = pl.BlockSpec((1, Fd), lambda i: (0, 0))
    return pl.pallas_call(
        body, name=name, grid=(T // tm,),
        in_specs=[row, vec, row],
        out_specs=[pl.BlockSpec((1, LANES), lambda i: (0, 0)), row, row, vec],
        out_shape=[jax.ShapeDtypeStruct((1, LANES), F32), jax.ShapeDtypeStruct((T, Fd), F32),
                   jax.ShapeDtypeStruct((T, Fd), BF16), jax.ShapeDtypeStruct((1, Fd), F32)],
        compiler_params=_params(("arbitrary",)),
    )(h, g, target)


FFN_TF = 256


def _ffn_fwd(h, g, wg_t, wu_t, wd, name="ffn_fwd"):
    T, Dm = h.shape
    Fh = wd.shape[0]
    tm = _pick(T, 2048)
    nf = Fh // FFN_TF

    def body(h_ref, g_ref, wg_ref, wu_ref, wd_ref, o_ref, u_ref, a_ref, b_ref):
        j = pl.program_id(1)

        @pl.when(j == 0)
        def _():
            xv = h_ref[...]
            r = lax.rsqrt(jnp.mean(xv * xv, axis=-1, keepdims=True) + RMS_EPS)
            u_ref[...] = (xv * r * g_ref[...]).astype(BF16)
            o_ref[...] = xv

        u = u_ref[...]
        a = _dot(u, wg_ref[...], NT).astype(BF16)
        b = _dot(u, wu_ref[...], NT).astype(BF16)
        a_ref[...] = a
        b_ref[...] = b
        af = a.astype(F32)
        s = (af * jax.nn.sigmoid(af) * b.astype(F32)).astype(BF16)
        o_ref[...] += _dot(s, wd_ref[...], NN)

    row = pl.BlockSpec((tm, Dm), lambda i, j: (i, 0))
    wblk = pl.BlockSpec((FFN_TF, Dm), lambda i, j: (j, 0))
    ablk = pl.BlockSpec((tm, FFN_TF), lambda i, j: (i, j))
    return pl.pallas_call(
        body, name=name, grid=(T // tm, nf),
        in_specs=[pl.BlockSpec((tm, Dm), lambda i, j: (i, 0), pipeline_mode=pl.Buffered(1)),
                  pl.BlockSpec((1, Dm), lambda i, j: (0, 0)), wblk, wblk, wblk],
        out_specs=[row, row, ablk, ablk],
        out_shape=[jax.ShapeDtypeStruct((T, Dm), F32), jax.ShapeDtypeStruct((T, Dm), BF16),
                   jax.ShapeDtypeStruct((T, Fh), BF16), jax.ShapeDtypeStruct((T, Fh), BF16)],
        compiler_params=_params(("parallel", "arbitrary")),
    )(h, g, wg_t, wu_t, wd)


def _ffn_bwd(dh, u, a, b, wg_t, wu_t, wd, name="ffn_bwd"):
    T, Dm = dh.shape
    Fh = wd.shape[0]
    nf = Fh // FFN_TF
    once = pl.Buffered(1)

    def body(dh_ref, u_ref, a_ref, b_ref, wg_ref, wu_ref, wd_ref, du_ref, dwg_ref, dwu_ref, dwd_ref):
        j = pl.program_id(0)

        @pl.when(j == 0)
        def _():
            du_ref[...] = jnp.zeros_like(du_ref)

        ds = _dot(dh_ref[...], wd_ref[...], NT)
        af, bf = a_ref[...].astype(F32), b_ref[...].astype(F32)
        sig = jax.nn.sigmoid(af)
        sa = af * sig
        dwd_ref[...] = _dot((sa * bf).astype(BF16), dh_ref[...], TN).astype(BF16)
        dab = jnp.concatenate([(ds * bf * (sig * (1.0 + af * (1.0 - sig)))).astype(BF16),
                               (ds * sa).astype(BF16)], axis=1)
        dw = _dot(dab, u_ref[...], TN)
        dwg_ref[...] = dw[:FFN_TF].astype(BF16)
        dwu_ref[...] = dw[FFN_TF:].astype(BF16)
        du_ref[...] += _dot(dab, jnp.concatenate([wg_ref[...], wu_ref[...]], axis=0), NN)

    full = lambda: pl.BlockSpec((T, Dm), lambda j: (0, 0), pipeline_mode=once)
    wblk = pl.BlockSpec((FFN_TF, Dm), lambda j: (j, 0))
    ablk = pl.BlockSpec((T, FFN_TF), lambda j: (0, j))
    return pl.pallas_call(
        body, name=name, grid=(nf,),
        in_specs=[full(), full(), ablk, ablk, wblk, wblk, wblk],
        out_specs=[pl.BlockSpec((T, Dm), lambda j: (0, 0)), wblk, wblk, wblk],
        out_shape=[jax.ShapeDtypeStruct((T, Dm), F32)] + [jax.ShapeDtypeStruct((Fh, Dm), BF16)] * 3,
        compiler_params=_params(("arbitrary",)),
    )(dh, u, a, b, wg_t, wu_t, wd)


def _rope(x, cos_t, sin_t, col0, ncols, out_dtype, name="rope"):
    T = x.shape[0]
    wt = cos_t.shape[1]
    tm = _pick(T, 256)
    nb = ncols * LANES // wt
    half = MLA_ROPE // 2

    def body(x_ref, c_ref, s_ref, o_ref):
        xv = x_ref[...].astype(F32)
        lane = lax.broadcasted_iota(jnp.int32, xv.shape, 1)
        first = (lane & (MLA_ROPE - 1)) < half
        swapped = jnp.where(first, pltpu.roll(xv, wt - half, 1), pltpu.roll(xv, half, 1))
        o_ref[...] = (xv * c_ref[...] + swapped * s_ref[...]).astype(out_dtype)

    off = col0 * LANES // wt
    return pl.pallas_call(
        body, name=name, grid=(T // tm, nb),
        in_specs=[pl.BlockSpec((tm, wt), lambda i, j: (i, j + off)),
                  pl.BlockSpec((tm, wt), lambda i, j: (i, 0)),
                  pl.BlockSpec((tm, wt), lambda i, j: (i, 0))],
        out_specs=pl.BlockSpec((tm, wt), lambda i, j: (i, j)),
        out_shape=jax.ShapeDtypeStruct((T, ncols * LANES), out_dtype),
        compiler_params=_params(("parallel", "parallel")),
    )(x, cos_t, sin_t)


ATT_TQ = 512
ATT_TK = 256
MLA_TK = 512


def _mla_masks(shape):
    lane = lax.broadcasted_iota(jnp.int32, shape, 1)
    m0 = (lane < HEAD) | ((lane >= LANES) & (lane < LANES + MLA_ROPE))
    m1 = ((lane >= HEAD) & (lane < LANES)) | ((lane >= LANES + MLA_ROPE) & (lane < LANES + 2 * MLA_ROPE))
    return m0, m1


def _by_twos(n, step, carry):
    carry = lax.fori_loop(0, n // 2, lambda i, c: step(2 * i + 1, step(2 * i, c)), carry)
    return lax.fori_loop(0, n % 2, lambda _, c: step(n - 1, c), carry)


def _chunk_ok(tq, tk, d):
    row = lax.broadcasted_iota(jnp.int32, (tq, tk), 0)
    col = lax.broadcasted_iota(jnp.int32, (tq, tk), 1) + d * tk
    return jnp.concatenate([(col >> CHUNK_BITS) <= (row >> CHUNK_BITS)] * 2, axis=0)


def _rotate(x, cos_t, sin_t):
    half = MLA_ROPE // 2
    lane = lax.broadcasted_iota(jnp.int32, x.shape, 1)
    first = (lane & (MLA_ROPE - 1)) < half
    swapped = jnp.where(first, pltpu.roll(x, x.shape[1] - half, 1), pltpu.roll(x, half, 1))
    return x * cos_t + swapped * sin_t


def _mla_fwd(q, cos_q, sin_q, kv, kr, name="mla_fwd"):
    T = q.shape[0]
    tq, tk = _pick(T, ATT_TQ), _pick(T, MLA_TK)
    nd = tq // tk
    npair = MLA_HEADS // 2
    scale = (MLA_NOPE + MLA_ROPE) ** -0.5

    def body(q_ref, c_ref, s_ref, kn_ref, v_ref, kr_ref, o_ref, lse_ref):
        m_idx = pl.program_id(1)
        qv = _rotate(q_ref[...], c_ref[...], s_ref[...]).astype(BF16)
        m0, m1 = _mla_masks(qv.shape)
        qs = jnp.concatenate([jnp.where(m0, qv, 0), jnp.where(m1, qv, 0)], axis=0).astype(BF16)

        def block(kb, carry, ok):
            ks = pl.ds(pl.multiple_of(kb * tk, tk), tk)
            kcat = jnp.concatenate([kn_ref[ks, :], kr_ref[ks, :]], axis=1)
            mx, l, acc = carry
            s = _dot(qs, kcat, NT) * scale
            if ok is not None:
                s = jnp.where(ok, s, NEG)
            mn = jnp.maximum(mx, jnp.max(s, axis=-1, keepdims=True))
            alpha = jnp.exp(mx - mn)
            p = jnp.exp(s - mn)
            return (mn, alpha * l + jnp.sum(p, axis=-1, keepdims=True),
                    alpha * acc + _dot(p.astype(BF16), v_ref[ks, :], NN))

        init = (jnp.full((2 * tq, 1), NEG, F32), jnp.zeros((2 * tq, 1), F32), jnp.zeros((2 * tq, LANES), F32))
        res = init
        for d in range(nd):
            res = block(m_idx * nd + d, res, _chunk_ok(tq, tk, d))
        mx, l, acc = _by_twos(m_idx * nd, lambda kb, c: block(kb, c, None), res)
        h0 = lax.broadcasted_iota(jnp.int32, (tq, LANES), 1) < HEAD
        o_ref[...] = _two_heads(acc * (1.0 / l), h0).astype(o_ref.dtype)
        lse_ref[...] = _two_heads(jnp.broadcast_to(mx + jnp.log(l), (2 * tq, LANES)), h0)

    full = lambda col: pl.BlockSpec((T, LANES), col)
    table = pl.BlockSpec((tq, 2 * LANES), lambda p, m: (m, 0))
    return pl.pallas_call(
        body, name=name, grid=(npair, T // tq),
        in_specs=[pl.BlockSpec((tq, 2 * LANES), lambda p, m: (m, p)), table, table,
                  full(lambda p, m: (0, p)), full(lambda p, m: (0, npair + p)), full(lambda p, m: (0, 0))],
        out_specs=[pl.BlockSpec((tq, LANES), lambda p, m: (m, p)),
                   pl.BlockSpec((tq, LANES), lambda p, m: (m, p))],
        out_shape=[jax.ShapeDtypeStruct((T, npair * LANES), BF16),
                   jax.ShapeDtypeStruct((T, npair * LANES), F32)],
        compiler_params=_params(("parallel", "arbitrary")),
    )(q, cos_q, sin_q, kv, kv, kr)


def _mla_bwd(q, cos_q, sin_q, kv, kr, o, lse, do, do_col0, name="mla_bwd"):
    T = q.shape[0]
    tq, tk = _pick(T, ATT_TQ), _pick(T, MLA_TK)
    nd = tq // tk
    npair = MLA_HEADS // 2
    scale = (MLA_NOPE + MLA_ROPE) ** -0.5

    def body(q_ref, c_ref, s_ref, kn_ref, v_ref, kr_ref, o_ref, lse_ref, do_ref, dq_ref, dkn_ref, dv_ref, dkr_ref,
             dkn_acc, dv_acc):
        p_idx, m_idx = pl.program_id(0), pl.program_id(1)

        @pl.when(m_idx == 0)
        def _():
            dkn_acc[...] = jnp.zeros_like(dkn_acc)
            dv_acc[...] = jnp.zeros_like(dv_acc)

        @pl.when((m_idx == 0) & (p_idx == 0))
        def _():
            dkr_ref[...] = jnp.zeros_like(dkr_ref)

        qv = _rotate(q_ref[...], c_ref[...], s_ref[...]).astype(BF16)
        m0, m1 = _mla_masks(qv.shape)
        qs = jnp.concatenate([jnp.where(m0, qv, 0), jnp.where(m1, qv, 0)], axis=0).astype(BF16)
        dov = do_ref[...].astype(F32)
        h0 = lax.broadcasted_iota(jnp.int32, (tq, LANES), 1) < HEAD
        dos32 = jnp.concatenate([jnp.where(h0, dov, 0.0), jnp.where(h0, 0.0, dov)], axis=0)
        ov = o_ref[...].astype(F32)
        delta = jnp.sum(dos32 * jnp.concatenate([ov, ov], axis=0), axis=-1, keepdims=True)
        dos = dos32.astype(BF16)
        lsev = lse_ref[...]
        lse = jnp.concatenate([lsev[:, 0:1], lsev[:, HEAD:HEAD + 1]], axis=0)

        def block(kb, dq, ok):
            ks = pl.ds(pl.multiple_of(kb * tk, tk), tk)
            kcat = jnp.concatenate([kn_ref[ks, :], kr_ref[ks, :]], axis=1)
            vv = v_ref[ks, :]
            p = jnp.exp(_dot(qs, kcat, NT) * scale - lse)
            if ok is not None:
                p = jnp.where(ok, p, 0.0)
            ds = (p * (_dot(dos, vv, NT) - delta) * scale).astype(BF16)
            dkc = _dot(ds, qs, TN)
            dkn_acc[ks, :] += dkc[:, :LANES]
            dkr_ref[ks, :] += dkc[:, LANES:]
            dv_acc[ks, :] += _dot(p.astype(BF16), dos, TN)
            return dq + _dot(ds, kcat, NN)

        dq = jnp.zeros((2 * tq, 2 * LANES), F32)
        for d in range(nd):
            dq = block(m_idx * nd + d, dq, _chunk_ok(tq, tk, d))
        dq = _by_twos(m_idx * nd, lambda kb, c: block(kb, c, None), dq)
        dq_ref[...] = _rotate(jnp.where(m0, dq[:tq], jnp.where(m1, dq[tq:], 0.0)), c_ref[...],
                              -s_ref[...]).astype(BF16)

        @pl.when(m_idx == T // tq - 1)
        def _():
            dkn_ref[...] = dkn_acc[...].astype(BF16)
            dv_ref[...] = dv_acc[...].astype(BF16)

    full = lambda col: pl.BlockSpec((T, LANES), col)
    blk = lambda col: pl.BlockSpec((tq, LANES), col)
    table = pl.BlockSpec((tq, 2 * LANES), lambda p, m: (m, 0))
    return pl.pallas_call(
        body, name=name, grid=(npair, T // tq),
        in_specs=[pl.BlockSpec((tq, 2 * LANES), lambda p, m: (m, p)), table, table,
                  full(lambda p, m: (0, p)), full(lambda p, m: (0, npair + p)), full(lambda p, m: (0, 0)),
                  blk(lambda p, m: (m, p)), blk(lambda p, m: (m, p)),
                  blk(lambda p, m: (m, do_col0 + p))],
        out_specs=[pl.BlockSpec((tq, 2 * LANES), lambda p, m: (m, p)),
                   full(lambda p, m: (0, p)), full(lambda p, m: (0, p)), full(lambda p, m: (0, 0))],
        out_shape=[jax.ShapeDtypeStruct((T, npair * 2 * LANES), BF16),
                   jax.ShapeDtypeStruct((T, npair * LANES), BF16),
                   jax.ShapeDtypeStruct((T, npair * LANES), BF16),
                   jax.ShapeDtypeStruct((T, LANES), F32)],
        scratch_shapes=[pltpu.VMEM((T, LANES), F32)] * 2,
        compiler_params=_params(("arbitrary", "arbitrary")),
    )(q, cos_q, sin_q, kv, kv, kr, o, lse, do)


def _split_dot(x, tri):
    hi = x.astype(BF16)
    lo = (x - hi.astype(F32)).astype(BF16)
    both = _dot(jnp.concatenate([hi, lo], axis=0), tri, NN)
    return both[:x.shape[0]] + both[x.shape[0]:]


def _sb_terms(qh, kk, before):
    z = _dot(qh, kk, NT)
    sp = jnp.maximum(z, 0.0) + jnp.log(1.0 + jnp.exp(-jnp.abs(z)))
    lk = -sp if before is None else jnp.where(before, -sp, 0.0)
    return z, sp, lk


def _sb_setup(q_ref, tq, tk, scale):
    qv = (q_ref[...].astype(F32) * scale).astype(BF16)
    lane = lax.broadcasted_iota(jnp.int32, (tq, LANES), 1)
    h0 = lane < HEAD
    qs = jnp.concatenate([jnp.where(h0, qv, 0), jnp.where(h0, 0, qv)], axis=0).astype(BF16)
    row = lax.broadcasted_iota(jnp.int32, (tk, tk), 0)
    col = lax.broadcasted_iota(jnp.int32, (tk, tk), 1)
    return qs, h0, row, col


def _sb_before(tq, tk, d):
    row = lax.broadcasted_iota(jnp.int32, (tq, tk), 0)
    col = lax.broadcasted_iota(jnp.int32, (tq, tk), 1) + d * tk
    return jnp.concatenate([col < row] * 2, axis=0)


def _two_heads(x, h0):
    tq = x.shape[0] // 2
    return jnp.where(h0, x[:tq], x[tq:])


def _sb_fwd(qkv, col0, dep, name="sb_fwd"):
    T = qkv.shape[0]
    tq, tk = _pick(T, ATT_TQ), _pick(T, ATT_TK)
    nd = tq // tk
    npair = SB_HEADS // 2
    scale = SB_DIM ** -0.5

    def body(q_ref, k_ref, v_ref, dep_ref, o_ref, o32_ref, w_ref, sp_ref):
        m_idx = pl.program_id(1)
        qs, h0, row, col = _sb_setup(q_ref, tq, tk, scale)
        later = (row > col).astype(BF16)

        def block(kb, carry, before):
            ks = pl.ds(pl.multiple_of(kb * tk, tk), tk)
            c, acc = carry
            z, sp, lk = _sb_terms(qs, k_ref[ks, :].astype(BF16), before)
            w = jnp.exp((z - sp) + _split_dot(lk, later) + c)
            if before is not None:
                w = jnp.where(before, w, 0.0)
            wb = w.astype(BF16)
            w_ref[0, 0, kb] = wb
            sp_ref[0, 0, kb] = sp.astype(BF16)
            return (c + jnp.sum(lk, axis=-1, keepdims=True), acc + _dot(wb, v_ref[ks, :].astype(BF16), NN))

        init = (jnp.zeros((2 * tq, 1), F32), jnp.zeros((2 * tq, LANES), F32))
        res = init
        for d in reversed(range(nd)):
            res = block(m_idx * nd + d, res, _sb_before(tq, tk, d))
        res = _by_twos(m_idx * nd, lambda i, c: block(m_idx * nd - 1 - i, c, None), res)
        o = _two_heads(res[1], h0)
        o_ref[...] = o.astype(o_ref.dtype)
        o32_ref[...] = o

    full = lambda col: pl.BlockSpec((T, LANES), col)
    blk = pl.BlockSpec((tq, LANES), lambda p, m: (m, p))
    return pl.pallas_call(
        body, name=name, grid=(npair, T // tq),
        in_specs=[pl.BlockSpec((tq, LANES), lambda p, m: (m, col0 + p)),
                  full(lambda p, m: (0, col0 + npair + p)), full(lambda p, m: (0, col0 + 2 * npair + p)),
                  pl.BlockSpec((8, LANES), lambda p, m: (0, 0))],
        out_specs=[blk, blk] + [pl.BlockSpec((1, 1, T // tk, 2 * tq, tk), lambda p, m: (p, m, 0, 0, 0))] * 2,
        out_shape=[jax.ShapeDtypeStruct((T, npair * LANES), BF16), jax.ShapeDtypeStruct((T, npair * LANES), F32)]
        + [jax.ShapeDtypeStruct((npair, T // tq, T // tk, 2 * tq, tk), BF16)] * 2,
        compiler_params=_params(("parallel", "arbitrary")),
    )(qkv, qkv, qkv, dep)


def _sb_bwd(qkv, col0, o32, w_all, sp_all, do, do_col0, dep, name="sb_bwd"):
    T = qkv.shape[0]
    tq, tk = _pick(T, ATT_TQ), _pick(T, ATT_TK)
    nd = tq // tk
    npair = SB_HEADS // 2
    scale = SB_DIM ** -0.5

    def body(q_ref, k_ref, v_ref, o_ref, w_ref, sp_ref, do_ref, dep_ref, dq_ref, dk_ref, dv_ref, dk_acc, dv_acc):
        m_idx = pl.program_id(1)

        @pl.when(m_idx == 0)
        def _():
            dk_acc[...] = jnp.zeros_like(dk_acc)
            dv_acc[...] = jnp.zeros_like(dv_acc)

        qs, h0, row, col = _sb_setup(q_ref, tq, tk, scale)
        dov = do_ref[...].astype(F32)
        dos = jnp.concatenate([jnp.where(h0, dov, 0.0), jnp.where(h0, 0.0, dov)], axis=0).astype(BF16)
        ov = o_ref[...]
        etot = jnp.sum(dos.astype(F32) * jnp.concatenate([ov, ov], axis=0), axis=-1, keepdims=True)
        from_here = (row >= col).astype(BF16)

        def block(kb, carry, before):
            ks = pl.ds(pl.multiple_of(kb * tk, tk), tk)
            kk = k_ref[ks, :].astype(BF16)
            vv = v_ref[ks, :].astype(BF16)
            es, dqa = carry
            wb = w_ref[0, 0, kb]
            e = wb.astype(F32) * _dot(dos, vv, NT)
            prev = etot - (_split_dot(e, from_here) + es)
            sig_neg = jnp.exp(-sp_ref[0, 0, kb].astype(F32))
            dz = e * sig_neg - (1.0 - sig_neg) * prev
            if before is not None:
                dz = jnp.where(before, dz, 0.0)
            dzb = dz.astype(BF16)
            dk_acc[ks, :] += _dot(dzb, qs, TN)
            dv_acc[ks, :] += _dot(wb, dos, TN)
            return es + jnp.sum(e, axis=-1, keepdims=True), dqa + _dot(dzb, kk, NN)

        init = (jnp.zeros((2 * tq, 1), F32), jnp.zeros((2 * tq, LANES), F32))
        res = init
        for d in reversed(range(nd)):
            res = block(m_idx * nd + d, res, _sb_before(tq, tk, d))
        res = _by_twos(m_idx * nd, lambda i, c: block(m_idx * nd - 1 - i, c, None), res)
        dq_ref[...] = (_two_heads(res[1], h0) * scale).astype(BF16)

        @pl.when(m_idx == T // tq - 1)
        def _():
            dk_ref[...] = dk_acc[...].astype(BF16)
            dv_ref[...] = dv_acc[...].astype(BF16)

    full = lambda col: pl.BlockSpec((T, LANES), col)
    blk = lambda col: pl.BlockSpec((tq, LANES), col)
    return pl.pallas_call(
        body, name=name, grid=(npair, T // tq),
        in_specs=[blk(lambda p, m: (m, col0 + p)),
                  full(lambda p, m: (0, col0 + npair + p)), full(lambda p, m: (0, col0 + 2 * npair + p)),
                  blk(lambda p, m: (m, p)),
                  pl.BlockSpec((1, 1, T // tk, 2 * tq, tk), lambda p, m: (p, m, 0, 0, 0)),
                  pl.BlockSpec((1, 1, T // tk, 2 * tq, tk), lambda p, m: (p, m, 0, 0, 0)),
                  blk(lambda p, m: (m, do_col0 + p)), pl.BlockSpec((8, LANES), lambda p, m: (0, 0))],
        out_specs=[blk(lambda p, m: (m, p)), full(lambda p, m: (0, p)), full(lambda p, m: (0, p))],
        out_shape=[jax.ShapeDtypeStruct((T, npair * LANES), BF16)] * 3,
        scratch_shapes=[pltpu.VMEM((T, LANES), F32)] * 2,
        compiler_params=_params(("arbitrary", "arbitrary")),
    )(qkv, qkv, qkv, o32, w_all, sp_all, do, dep)


def _band_in_window():
    cq = lax.broadcasted_iota(jnp.int32, (BAND_TQ, BAND_W), 0) >> CHUNK_BITS
    ckp = lax.broadcasted_iota(jnp.int32, (BAND_TQ, BAND_W), 1) >> CHUNK_BITS
    return (ckp >= cq) & (ckp <= cq + LEFT_CHUNKS)


def _band_real(m_idx):
    j = lax.broadcasted_iota(jnp.int32, (BAND_TQ, BAND_W), 1)
    return j >= PAD_KEYS - m_idx * BAND_TQ


def _band_probs(qh, kw, bias, real, scale):
    s = jnp.where(real, _dot(qh, kw, NT) * scale + bias, NEG)
    e = jnp.exp(s - jnp.max(s, axis=-1, keepdims=True))
    return e * (1.0 / jnp.sum(e, axis=-1, keepdims=True))


BAND_SUB = 16


def _band_fwd(qkv, k_pad, v_pad, bias_w, name="band_fwd"):
    T = qkv.shape[0]
    npair = C_HEADS // 2
    scale = C_DIM ** -0.5
    rows = BAND_SUB * BAND_TQ

    def body(q_ref, k_ref, v_ref, b_ref, o_ref, p_ref):
        lane = lax.broadcasted_iota(jnp.int32, (BAND_TQ, LANES), 1)
        h0 = lane < HEAD
        bias = jnp.concatenate([b_ref[0], b_ref[1]], axis=0)
        for sub in range(BAND_SUB):
            m_idx = pl.program_id(1) * BAND_SUB + sub
            win = pl.ds(pl.multiple_of(m_idx * BAND_TQ, BAND_TQ), BAND_W)
            kw, vw = k_ref[win, :], v_ref[win, :]
            qv = q_ref[sub * BAND_TQ:(sub + 1) * BAND_TQ, :]
            qs = jnp.concatenate([jnp.where(h0, qv, 0), jnp.where(h0, 0, qv)], axis=0).astype(BF16)
            p = _band_probs(qs, kw, bias, jnp.concatenate([_band_real(m_idx)] * 2, axis=0), scale).astype(BF16)
            p_ref[0, sub] = p
            o = _two_heads(_dot(p, vw, NN), h0)
            o_ref[sub * BAND_TQ:(sub + 1) * BAND_TQ, :] = o.astype(o_ref.dtype)

    Tp = T + PAD_KEYS
    return pl.pallas_call(
        body, name=name, grid=(npair, T // rows),
        in_specs=[pl.BlockSpec((rows, LANES), lambda p, m: (m, p)),
                  pl.BlockSpec((Tp, LANES), lambda p, m: (0, p)),
                  pl.BlockSpec((Tp, LANES), lambda p, m: (0, p)),
                  pl.BlockSpec((2, BAND_TQ, BAND_W), lambda p, m: (p, 0, 0))],
        out_specs=[pl.BlockSpec((rows, LANES), lambda p, m: (m, p)),
                   pl.BlockSpec((1, BAND_SUB, 2 * BAND_TQ, BAND_W), lambda p, m: (p, m, 0, 0))],
        out_shape=[jax.ShapeDtypeStruct((T, npair * LANES), BF16),
                   jax.ShapeDtypeStruct((npair, T // BAND_TQ, 2 * BAND_TQ, BAND_W), BF16)],
        compiler_params=_params(("parallel", "arbitrary")),
    )(qkv, k_pad, v_pad, bias_w)


def _band_bwd(qkv, k_pad, v_pad, probs, do, name="band_bwd"):
    T = qkv.shape[0]
    npair = C_HEADS // 2
    scale = C_DIM ** -0.5

    rows = BAND_SUB * BAND_TQ

    def body(q_ref, k_ref, v_ref, p_ref, do_ref, dq_ref, dk_ref, dv_ref, db_ref, dk_acc, dv_acc):
        @pl.when(pl.program_id(1) == 0)
        def _():
            dk_acc[...] = jnp.zeros_like(dk_acc)
            dv_acc[...] = jnp.zeros_like(dv_acc)
            db_ref[...] = jnp.zeros_like(db_ref)

        lane = lax.broadcasted_iota(jnp.int32, (BAND_TQ, LANES), 1)
        h0 = lane < HEAD
        dbs = jnp.zeros((2 * BAND_TQ, BAND_W), F32)
        for sub in range(BAND_SUB):
            m_idx = pl.program_id(1) * BAND_SUB + sub
            win = pl.ds(pl.multiple_of(m_idx * BAND_TQ, BAND_TQ), BAND_W)
            kw, vw = k_ref[win, :], v_ref[win, :]
            qv = q_ref[sub * BAND_TQ:(sub + 1) * BAND_TQ, :]
            dov = do_ref[sub * BAND_TQ:(sub + 1) * BAND_TQ, :].astype(F32)
            qs = jnp.concatenate([jnp.where(h0, qv, 0), jnp.where(h0, 0, qv)], axis=0).astype(BF16)
            dos = jnp.concatenate([jnp.where(h0, dov, 0.0), jnp.where(h0, 0.0, dov)], axis=0).astype(BF16)
            pb = p_ref[0, sub]
            p = pb.astype(F32)
            dp = _dot(dos, vw, NT)
            dsb = p * (dp - jnp.sum(p * dp, axis=-1, keepdims=True))
            dbs = dbs + dsb
            dsq = (dsb * scale).astype(BF16)
            dq_ref[sub * BAND_TQ:(sub + 1) * BAND_TQ, :] = _two_heads(_dot(dsq, kw, NN), h0).astype(BF16)
            dk_acc[win, :] += _dot(dsq, qs, TN)
            dv_acc[win, :] += _dot(pb, dos, TN)
        db_ref[0] += dbs[:BAND_TQ]
        db_ref[1] += dbs[BAND_TQ:]

        @pl.when(pl.program_id(1) == T // rows - 1)
        def _():
            dk_ref[...] = dk_acc[...].astype(BF16)
            dv_ref[...] = dv_acc[...].astype(BF16)

    Tp = T + PAD_KEYS
    blk = lambda col: pl.BlockSpec((rows, LANES), col)
    full = pl.BlockSpec((Tp, LANES), lambda p, m: (0, p))
    bias = pl.BlockSpec((2, BAND_TQ, BAND_W), lambda p, m: (p, 0, 0))
    prob = pl.BlockSpec((1, BAND_SUB, 2 * BAND_TQ, BAND_W), lambda p, m: (p, m, 0, 0))
    return pl.pallas_call(
        body, name=name, grid=(npair, T // rows),
        in_specs=[blk(lambda p, m: (m, p)), full, full, prob, blk(lambda p, m: (m, p))],
        out_specs=[blk(lambda p, m: (m, p)), full, full, bias],
        out_shape=[jax.ShapeDtypeStruct((T, npair * LANES), BF16),
                   jax.ShapeDtypeStruct((Tp, npair * LANES), BF16),
                   jax.ShapeDtypeStruct((Tp, npair * LANES), BF16),
                   jax.ShapeDtypeStruct((C_HEADS, BAND_TQ, BAND_W), F32)],
        scratch_shapes=[pltpu.VMEM((Tp, LANES), F32)] * 2,
        compiler_params=_params(("arbitrary", "arbitrary")),
    )(qkv, k_pad, v_pad, probs, do)


def _skew_bits(x, left):
    w = x.shape[1]
    row = lax.broadcasted_iota(jnp.int32, x.shape, 0)
    for b in range(BAND_TQ.bit_length() - 1):
        amt = (w - (1 << b)) if left else (1 << b)
        x = jnp.where((row >> b) & 1 == 1, pltpu.roll(x, amt, 1), x)
    return x


def _toeplitz(diag, name="toeplitz"):
    H = diag.shape[0]

    def body(d_ref, o_ref):
        x = jnp.broadcast_to(d_ref[0], (BAND_TQ, TOEP_W))
        o_ref[0] = jnp.where(_band_in_window(), _skew_bits(x, left=False)[:, BAND_TQ:], NEG)

    return pl.pallas_call(
        body, name=name, grid=(H,),
        in_specs=[pl.BlockSpec((1, 1, TOEP_W), lambda h: (h, 0, 0))],
        out_specs=pl.BlockSpec((1, BAND_TQ, BAND_W), lambda h: (h, 0, 0)),
        out_shape=jax.ShapeDtypeStruct((H, BAND_TQ, BAND_W), F32),
        compiler_params=_params(("parallel",)),
    )(diag.reshape(H, 1, TOEP_W))


def _toeplitz_bwd(dbias, name="toeplitz_bwd"):
    H = dbias.shape[0]

    def body(d_ref, o_ref):
        x = jnp.concatenate([jnp.zeros((BAND_TQ, BAND_TQ), F32), d_ref[0]], axis=1)
        h = BAND_TQ // 2
        while h >= 8:
            x = x[:h] + pltpu.roll(x[h:2 * h], TOEP_W - h, 1)
            h //= 2
        o_ref[0] = jnp.sum(_skew_bits(x, left=True), axis=0, keepdims=True)

    return pl.pallas_call(
        body, name=name, grid=(H,),
        in_specs=[pl.BlockSpec((1, BAND_TQ, BAND_W), lambda h: (h, 0, 0))],
        out_specs=pl.BlockSpec((1, 1, TOEP_W), lambda h: (h, 0, 0)),
        out_shape=jax.ShapeDtypeStruct((H, 1, TOEP_W), F32),
        compiler_params=_params(("parallel",)),
    )(dbias).reshape(H, TOEP_W)


_HBM = pl.BlockSpec(memory_space=pltpu.HBM)
_SEM = pl.BlockSpec(memory_space=pltpu.SEMAPHORE)
_EFFECT = pltpu.SideEffectType.DATAFLOW_SIDE_EFFECTING


def _peers():
    x, y, c = lax.axis_index("x"), lax.axis_index("y"), lax.axis_index("c")
    out = []
    for k in range(1, N_DEV):
        peer = (1 - x if (k >> 2) & 1 else x, 1 - y if (k >> 1) & 1 else y, 1 - c if k & 1 else c)
        out.append((peer, 4 * peer[0] + 2 * peer[1] + peer[2]))
    return 4 * x + 2 * y + c, out


def _split_copies(ins, lands, scatter, send_sem, recv_sem, arriving):
    me, peers = _peers()
    out = []
    for a in range(len(ins)):
        for peer, idx in peers:
            out.append(pltpu.make_async_remote_copy(
                src_ref=ins[a].at[idx] if scatter[a] else ins[a],
                dst_ref=lands[a].at[idx if arriving else me], send_sem=send_sem, recv_sem=recv_sem,
                device_id=peer, device_id_type=pl.DeviceIdType.MESH))
    return out


def _landing_zones(arrays, scatter):
    return [lax.empty((N_DEV,) + (a.shape[1:] if s else a.shape), a.dtype) for a, s in zip(arrays, scatter)]


def _place_own(arrays, scatter, name):
    n = len(arrays)
    lands = _landing_zones(arrays, scatter)
    me = (4 * lax.axis_index("x") + 2 * lax.axis_index("y") + lax.axis_index("c")).astype(jnp.int32).reshape(1)

    def body(me_ref, *refs):
        for a in range(n):
            refs[2 * n + a][...] = refs[a][...].reshape(refs[2 * n + a].shape)

    def row_spec(shape):
        zeros = (0,) * (len(shape) - 1)
        return pl.BlockSpec((1,) + tuple(shape[1:]), lambda i, me_ref: (me_ref[0],) + zeros)

    in_specs = [row_spec(a.shape) if s else pl.BlockSpec(a.shape, lambda i, me_ref, nd=a.ndim: (0,) * nd)
                for a, s in zip(arrays, scatter)]
    return pl.pallas_call(
        body, name=name,
        out_shape=[jax.ShapeDtypeStruct(l.shape, l.dtype) for l in lands],
        grid_spec=pltpu.PrefetchScalarGridSpec(
            num_scalar_prefetch=1, grid=(1,),
            in_specs=in_specs + [pl.BlockSpec(memory_space=pl.ANY)] * n,
            out_specs=[row_spec(l.shape) for l in lands]),
        input_output_aliases={1 + n + i: i for i in range(n)},
        compiler_params=_params(("arbitrary",)),
    )(me, *arrays, *lands)


def _exchange_start_groups(groups, scatter, after, name, lands=None):
    sizes = [len(g) for g in groups]
    arrays = [a for g in groups for a in g]
    n, ng = len(arrays), len(groups)
    flags = list(scatter) if isinstance(scatter, (list, tuple)) else [scatter] * n
    if lands is None:
        lands = list(_place_own(arrays, flags, name=name.replace("_start_", "_own_")))
    else:
        lands = [l for g in lands for l in g]
    starts = np.cumsum([0] + sizes)

    def body(*refs):
        ins, lnd = refs[:n], refs[n:2 * n]
        sems = refs[2 * n + 1:2 * n + 1 + 2 * ng]
        token = refs[-1]
        for g in range(ng):
            sl = slice(starts[g], starts[g + 1])
            for cp in _split_copies(ins[sl], lnd[sl], flags[sl], sems[2 * g], sems[2 * g + 1], arriving=False):
                cp.start()
        token[...] = jnp.zeros_like(token)

    hbm = lambda a: pltpu.HBM(a.shape, a.dtype)
    out = pl.pallas_call(
        body, name=name,
        out_shape=(*[pltpu.SemaphoreType.DMA(())] * (2 * ng),
                   *[hbm(a) for a in arrays], *[hbm(a) for a in lands],
                   jax.ShapeDtypeStruct((8, LANES), F32)),
        in_specs=[_HBM] * (2 * n) + [pl.BlockSpec(memory_space=pl.ANY)],
        out_specs=(*[_SEM] * (2 * ng), *([_HBM] * (2 * n)), pl.BlockSpec(memory_space=pltpu.VMEM)),
        input_output_aliases={i: 2 * ng + i for i in range(2 * n)},
        compiler_params=pltpu.CompilerParams(has_side_effects=_EFFECT),
    )(*[pltpu.with_memory_space_constraint(a, pltpu.HBM) for a in list(arrays) + lands], after)
    ins_out, lands_out = out[2 * ng:2 * ng + n], out[2 * ng + n:2 * ng + 2 * n]
    handles = [(out[2 * g], out[2 * g + 1], list(ins_out[starts[g]:starts[g + 1]]),
                list(lands_out[starts[g]:starts[g + 1]]), tuple(flags[starts[g]:starts[g + 1]]))
               for g in range(ng)]
    return handles, out[-1]


def _exchange_start(arrays, scatter, after, name):
    handles, token = _exchange_start_groups([list(arrays)], list(scatter), after, name)
    return handles[0], token


def _exchange_wait(handle, after, name):
    send_sem, recv_sem, ins, lands, scatter = handle
    n = len(ins)
    after = after if isinstance(after, tuple) else (after,)

    def body(*refs):
        i_ref, l_ref = refs[:n], refs[n:2 * n]
        s_sem, r_sem = refs[2 * n:2 * n + 2]
        for cp in _split_copies(i_ref, l_ref, scatter, s_sem, r_sem, arriving=False):
            cp.wait_send()
        for cp in _split_copies(i_ref, l_ref, scatter, s_sem, r_sem, arriving=True):
            cp.wait_recv()

    hbm = lambda a: pltpu.HBM(a.shape, a.dtype)
    out = pl.pallas_call(
        body, name=name,
        out_shape=tuple(hbm(a) for a in ins + lands),
        in_specs=[_HBM] * (2 * n) + [_SEM, _SEM] + [pl.BlockSpec(memory_space=pl.ANY)] * len(after),
        out_specs=tuple([_HBM] * (2 * n)),
        input_output_aliases={i: i for i in range(2 * n)},
        compiler_params=pltpu.CompilerParams(has_side_effects=_EFFECT),
    )(*ins, *lands, send_sem, recv_sem, *after)
    return list(out[n:])


_SIBLING = 1
_CHIPS = (4, 2, 6)


def _peer_of(k):
    x, y, c = lax.axis_index("x"), lax.axis_index("y"), lax.axis_index("c")
    peer = (1 - x if (k >> 2) & 1 else x, 1 - y if (k >> 1) & 1 else y, 1 - c if k & 1 else c)
    return peer, 4 * peer[0] + 2 * peer[1] + peer[2]


def _rcopy(src, dst, send_sem, recv_sem, k):
    return pltpu.make_async_remote_copy(src_ref=src, dst_ref=dst, send_sem=send_sem, recv_sem=recv_sem,
                                        device_id=_peer_of(k)[0], device_id_type=pl.DeviceIdType.MESH)


def _gather2_start(groups, lands, after, name):
    sizes = [len(g) for g in groups]
    arrays = [a for g in groups for a in g]
    lands = [l for g in lands for l in g]
    n, ng = len(arrays), len(groups)
    starts = np.cumsum([0] + sizes)

    def body(*refs):
        ins, lnd = refs[:n], refs[n:2 * n]
        sems = refs[2 * n + 1:2 * n + 1 + 4 * ng]
        me, _ = _peers()
        for g in range(ng):
            send_d, recv_d, send_i, recv_i = sems[4 * g:4 * g + 4]
            for a in range(starts[g], starts[g + 1]):
                for k in _CHIPS:
                    _rcopy(ins[a], lnd[a].at[me], send_i, recv_i, k).start()
                _rcopy(ins[a], lnd[a].at[me], send_d, recv_d, _SIBLING).start()
        refs[-1][...] = jnp.zeros_like(refs[-1])

    hbm = lambda a: pltpu.HBM(a.shape, a.dtype)
    out = pl.pallas_call(
        body, name=name,
        out_shape=(*[pltpu.SemaphoreType.DMA(())] * (4 * ng), *[hbm(a) for a in arrays], *[hbm(a) for a in lands],
                   jax.ShapeDtypeStruct((8, LANES), F32)),
        in_specs=[_HBM] * (2 * n) + [pl.BlockSpec(memory_space=pl.ANY)],
        out_specs=(*[_SEM] * (4 * ng), *([_HBM] * (2 * n)), pl.BlockSpec(memory_space=pltpu.VMEM)),
        input_output_aliases={i: 4 * ng + i for i in range(2 * n)},
        compiler_params=pltpu.CompilerParams(has_side_effects=_EFFECT),
    )(*[pltpu.with_memory_space_constraint(a, pltpu.HBM) for a in arrays + lands], after)
    ins_out, lands_out = out[4 * ng:4 * ng + n], out[4 * ng + n:4 * ng + 2 * n]
    handles = [dict(sems=out[4 * g:4 * g + 4], ins=list(ins_out[starts[g]:starts[g + 1]]),
                    lands=list(lands_out[starts[g]:starts[g + 1]])) for g in range(ng)]
    return handles, out[-1]


def _gather2_pass_on(handle, after, name):
    lands, recv_i = handle["lands"], handle["sems"][3]
    n = len(lands)
    after = after if isinstance(after, tuple) else (after,)

    def body(*refs):
        lnd, r_i = refs[:n], refs[n]
        send_f, recv_f = refs[n + 1 + len(after):n + 3 + len(after)]
        for a in range(n):
            for k in _CHIPS:
                row = _peer_of(k)[1]
                _rcopy(lnd[a].at[row], lnd[a].at[row], send_f, r_i, k).wait_recv()
        for a in range(n):
            for k in _CHIPS:
                row = _peer_of(k)[1]
                _rcopy(lnd[a].at[row], lnd[a].at[row], send_f, recv_f, _SIBLING).start()
        refs[-1][...] = jnp.zeros_like(refs[-1])

    hbm = lambda a: pltpu.HBM(a.shape, a.dtype)
    out = pl.pallas_call(
        body, name=name,
        out_shape=(pltpu.SemaphoreType.DMA(()), pltpu.SemaphoreType.DMA(()), *[hbm(a) for a in lands],
                   jax.ShapeDtypeStruct((8, LANES), F32)),
        in_specs=[_HBM] * n + [_SEM] + [pl.BlockSpec(memory_space=pl.ANY)] * len(after),
        out_specs=(_SEM, _SEM, *([_HBM] * n), pl.BlockSpec(memory_space=pltpu.VMEM)),
        input_output_aliases={i: 2 + i for i in range(n)},
        compiler_params=pltpu.CompilerParams(has_side_effects=_EFFECT),
    )(*lands, recv_i, *after)
    return dict(handle, lands=list(out[2:2 + n]), passed=(out[0], out[1])), out[-1]


def _gather2_wait(handle, after, name):
    ins, lands = handle["ins"], handle["lands"]
    send_d, recv_d, send_i, _ = handle["sems"]
    send_f, recv_f = handle["passed"]
    n = len(ins)
    after = after if isinstance(after, tuple) else (after,)

    def body(*refs):
        i_ref, lnd = refs[:n], refs[n:2 * n]
        s_d, r_d, s_i, s_f, r_f = refs[2 * n:2 * n + 5]
        me, _ = _peers()
        sib = _peer_of(_SIBLING)[1]
        for a in range(n):
            _rcopy(i_ref[a], lnd[a].at[sib], s_d, r_d, _SIBLING).wait_send()
            _rcopy(i_ref[a], lnd[a].at[sib], s_d, r_d, _SIBLING).wait_recv()
            for k in _CHIPS:
                row = _peer_of(k)[1]
                _rcopy(i_ref[a], lnd[a].at[me], s_i, r_d, k).wait_send()
                _rcopy(lnd[a].at[row], lnd[a].at[row], s_f, r_f, _SIBLING).wait_send()
                _rcopy(lnd[a].at[row], lnd[a].at[_peer_of(k ^ _SIBLING)[1]], s_f, r_f, _SIBLING).wait_recv()

    hbm = lambda a: pltpu.HBM(a.shape, a.dtype)
    out = pl.pallas_call(
        body, name=name,
        out_shape=tuple(hbm(a) for a in ins + lands),
        in_specs=[_HBM] * (2 * n) + [_SEM] * 5 + [pl.BlockSpec(memory_space=pl.ANY)] * len(after),
        out_specs=tuple([_HBM] * (2 * n)),
        input_output_aliases={i: i for i in range(2 * n)},
        compiler_params=pltpu.CompilerParams(has_side_effects=_EFFECT),
    )(*ins, *lands, send_d, recv_d, send_i, send_f, recv_f, *after)
    return list(out[n:])


def _adamw(w, parts, m, v, name="adamw"):
    R, C = w.shape
    L = len(parts)
    rl = R // L
    tr = max([t for t in range(16, 257, 16) if rl % t == 0], default=rl)
    nb = rl // tr
    c1 = 1.0 - ADAM_B1 ** ADAM_STEP
    c2 = 1.0 - ADAM_B2 ** ADAM_STEP

    n = L * nb
    NB = 3

    def body(*refs):
        w_hbm, p_hbm, (m_hbm, v_hbm) = refs[0], refs[1:1 + L], refs[1 + L:3 + L]
        outs = refs[3 + L:7 + L]
        wb, mb, vb, pb, ob, isem, osem = refs[7 + L:]

        def reads(s):
            k = s % NB
            rows = pl.ds(pl.multiple_of(s * tr, tr), tr)
            prow = pl.ds(pl.multiple_of((s % nb) * tr, tr), tr)
            base = [pltpu.make_async_copy(src.at[rows], dst.at[k], isem.at[j, k])
                    for j, (src, dst) in enumerate(((w_hbm, wb), (m_hbm, mb), (v_hbm, vb)))]
            ps = [pltpu.make_async_copy(p.at[:, prow], pb.at[k], isem.at[3, k]) for p in p_hbm]
            return base, ps

        def start_reads(s):
            base, ps = reads(s)
            for c in base:
                c.start()
            for j, c in enumerate(ps):
                pl.when(s // nb == j)(c.start)

        def writes(s):
            k = s % 2
            rows = pl.ds(pl.multiple_of(s * tr, tr), tr)
            return [pltpu.make_async_copy(ob.at[k, j], outs[j].at[rows], osem.at[j, k]) for j in range(4)]

        for s in range(min(NB, n)):
            start_reads(s)

        def step(s, carry):
            k = s % NB
            base, ps = reads(s)
            for c in base:
                c.wait()
            ps[0].wait()

            @pl.when(s >= 2)
            def _():
                for c in writes(s - 2):
                    c.wait()

            g = pb[k, 0].astype(F32)
            for i in range(1, N_DEV):
                g = g + pb[k, i].astype(F32)
            nm = ADAM_B1 * mb[k] + (1.0 - ADAM_B1) * g
            nv = ADAM_B2 * vb[k] + (1.0 - ADAM_B2) * (g * g)
            o = s % 2
            ob[o, 0] = g
            ob[o, 1] = -ADAM_LR * ((nm / c1) / (jnp.sqrt(nv / c2) + ADAM_EPS) + ADAM_WD * wb[k])
            ob[o, 2] = nm
            ob[o, 3] = nv
            for c in writes(s):
                c.start(priority=1)

            @pl.when(s + NB < n)
            def _():
                start_reads(s + NB)
            return carry

        lax.fori_loop(0, n, step, 0)
        for s in range(max(0, n - 2), n):
            for c in writes(s):
                c.wait()

    any_spec = pl.BlockSpec(memory_space=pl.ANY)
    return pl.pallas_call(
        body, name=name,
        in_specs=[any_spec] * (3 + L),
        out_specs=[any_spec] * 4,
        out_shape=[jax.ShapeDtypeStruct((R, C), F32)] * 4,
        scratch_shapes=[pltpu.VMEM((NB, tr, C), F32)] * 3
        + [pltpu.VMEM((NB, N_DEV, tr, C), parts[0].dtype), pltpu.VMEM((2, 4, tr, C), F32),
           pltpu.SemaphoreType.DMA((4, NB)), pltpu.SemaphoreType.DMA((4, 2))],
        compiler_params=pltpu.CompilerParams(vmem_limit_bytes=VMEM_LIMIT),
    )(w, *parts, m, v)


_O1 = Q_LORA
_O2 = _O1 + KV_LORA
_O3 = _O2 + MLA_ROPE
_NB = SB_HEADS * SB_DIM
IN_W = _O2 + LANES + 3 * _NB
COL_KR = _O2 // LANES
COL_SB = COL_KR + 1


def _w_in_local(w):
    kr = w[_O2:_O3]
    pad = jnp.zeros((LANES - 2 * MLA_ROPE, w.shape[1]), w.dtype)
    return jnp.concatenate([w[:_O2], kr, kr, pad, w[_O3:]], axis=0)


def _w_in_grad(g):
    kr = (g[_O2:_O2 + MLA_ROPE].astype(F32) + g[_O2 + MLA_ROPE:_O2 + 2 * MLA_ROPE].astype(F32)).astype(g.dtype)
    return jnp.concatenate([g[:_O2], kr, g[_O2 + LANES:]], axis=0)


def _w_uq_local(w):
    w3 = w.reshape(MLA_HEADS // 2, 2, MLA_NOPE + MLA_ROPE, w.shape[1])
    nope = w3[:, :, :MLA_NOPE].reshape(MLA_HEADS // 2, 2 * MLA_NOPE, w.shape[1])
    rope = w3[:, :, MLA_NOPE:].reshape(MLA_HEADS // 2, 2 * MLA_ROPE, w.shape[1])
    pad = jnp.zeros((MLA_HEADS // 2, LANES - 2 * MLA_ROPE, w.shape[1]), w.dtype)
    return jnp.concatenate([nope, rope, pad], axis=1).reshape(-1, w.shape[1])


def _w_uq_grad(g):
    g3 = g.reshape(MLA_HEADS // 2, 2 * LANES, g.shape[1])
    nope = g3[:, :2 * MLA_NOPE].reshape(MLA_HEADS // 2, 2, MLA_NOPE, g.shape[1])
    rope = g3[:, LANES:LANES + 2 * MLA_ROPE].reshape(MLA_HEADS // 2, 2, MLA_ROPE, g.shape[1])
    return jnp.concatenate([nope, rope], axis=2).reshape(-1, g.shape[1])


def _w_ukv_local(w):
    w3 = w.reshape(MLA_HEADS, MLA_NOPE + MLA_V, w.shape[1])
    return jnp.concatenate([w3[:, :MLA_NOPE].reshape(-1, w.shape[1]),
                            w3[:, MLA_NOPE:].reshape(-1, w.shape[1])], axis=0)


def _w_ukv_grad(g):
    half = MLA_HEADS * MLA_NOPE
    kn = g[:half].reshape(MLA_HEADS, MLA_NOPE, g.shape[1])
    vv = g[half:].reshape(MLA_HEADS, MLA_V, g.shape[1])
    return jnp.concatenate([kn, vv], axis=1).reshape(-1, g.shape[1])


def _rope_tables(T):
    pos = jnp.arange(T, dtype=F32)
    inv_freq = ROPE_THETA ** (-jnp.arange(0, MLA_ROPE, 2, dtype=F32) / MLA_ROPE)
    ang = pos[:, None] * inv_freq[None, :]
    cos, sin = jnp.cos(ang), jnp.sin(ang)
    ones = jnp.ones((T, LANES - 2 * MLA_ROPE), F32)
    cos_k = jnp.concatenate([cos, cos, cos, cos, ones], axis=1)
    sin_k = jnp.concatenate([-sin, sin, -sin, sin, 0.0 * ones], axis=1)
    cos_q = jnp.concatenate([jnp.ones((T, LANES), F32), cos_k], axis=1)
    sin_q = jnp.concatenate([jnp.zeros((T, LANES), F32), sin_k], axis=1)
    return cos_q, sin_q, cos_k, sin_k


def _bias_diag_index():
    ell = np.arange(TOEP_W)
    return np.clip(BAND_W - ell, -REL_CLIP, REL_CLIP) + REL_CLIP


def _local_step(x, target, small, get_weights, put_grads, prefetch):
    T = x.shape[0]
    cos_q, sin_q, cos_k, sin_k = _rope_tables(T)
    G = {}
    W = dict(small)

    u0 = _rms_fwd(x, W["g_mix"][0:1], name="rms_mix0")
    bias_w = _toeplitz(W["od_rel_bias"][:, _bias_diag_index()])
    W.update(get_weights("in0", (u0, bias_w)))
    proj = _mm(u0, W["w_in_t"], dims="nt", name="proj_in")
    W.update(get_weights("mix0", proj))
    c_q, c_kv = proj[:, :_O1], proj[:, _O1:_O2]
    nq = _rms_fwd(c_q, W["g_cq"], name="rms_cq")
    nkv = _rms_fwd(c_kv, W["g_ckv"], name="rms_ckv")
    qa_raw = _mm(nq, W["w_uq_t"], dims="nt", name="proj_uq")
    kv = _mm(nkv, W["w_ukv_t"], dims="nt", out_dtype=BF16, name="proj_ukv")
    kr = _rope(proj, cos_k, sin_k, COL_KR, 1, BF16, name="rope_k")
    o_a, lse = _mla_fwd(qa_raw, cos_q, sin_q, kv, kr)
    o_b, o_b32, w_b, sp_b = _sb_fwd(proj, COL_SB, prefetch("ffn0", o_a))
    o_ab = jnp.concatenate([o_a, o_b], axis=1)
    h1 = _mm(o_ab, W["ev_w_out"], res=x, name="out_ev")

    def ffn_fwd(h, layer):
        W.update(get_weights(f"ffn{layer}", h))
        return _ffn_fwd(h, W["g_ffn"][layer:layer + 1], W[f"w_gate_t{layer}"], W[f"w_up_t{layer}"],
                        W[f"w_down{layer}"], name=f"ffn_fwd{layer}")

    h2, u1, a0, b0 = ffn_fwd(h1, 0)

    W.update(get_weights("mix1", h2))
    u2 = _rms_fwd(h2, W["g_mix"][1:2], name="rms_mix1")
    qkv = _mm(u2, W["od_w_qkv_t"], dims="nt", out_dtype=BF16, name="proj_qkv")
    nc = C_HEADS * C_DIM
    pad = ((PAD_KEYS, 0), (0, 0))
    k_pad, v_pad = jnp.pad(qkv[:, nc:2 * nc], pad), jnp.pad(qkv[:, 2 * nc:], pad)
    o_c, p_c = _band_fwd(qkv, k_pad, v_pad, bias_w)
    h3 = _mm(o_c, W["od_w_out"], res=h2, name="out_od")
    h4, u3, a1, b1 = ffn_fwd(h3, 1)

    loss, dh, dhb, G["g_final"] = _loss_head(h4, W["g_final"], target)

    def ffn_bwd(dh, dhb, h, u, a, b, layer):
        du, g_gate, g_up, g_down = _ffn_bwd(dhb, u, a, b, W[f"w_gate_t{layer}"], W[f"w_up_t{layer}"],
                                            W[f"w_down{layer}"], name=f"ffn_bwd{layer}")
        tok = put_grads(f"ffn{layer}", {"w_gate_t": g_gate, "w_up_t": g_up, "w_down": g_down})
        return _rms_bwd(h, W["g_ffn"][layer:layer + 1] + tok[:1, :1], du, dres=dh, name=f"rms_ffn_bwd{layer}")

    dh3, dh3b, g_gffn1 = ffn_bwd(dh, dhb, h3, u3, a1, b1, 1)

    do_c = _mm(dh3b, W["od_w_out"], dims="nt", name="out_od_dx")
    g_od_out = _mm(o_c, dh3b, dims="tn", out_dtype=BF16, name="out_od_dw")
    dq_c, dk_p, dv_p, dbias_w = _band_bwd(qkv, k_pad, v_pad, p_c, do_c)
    dqkv = jnp.concatenate([dq_c, dk_p[PAD_KEYS:], dv_p[PAD_KEYS:]], axis=1)
    tok = put_grads("mix1", {"od_w_qkv_t": _mm(dqkv, u2, dims="tn", out_dtype=BF16, name="proj_qkv_dw"),
                             "od_w_out": g_od_out})
    ddiag = _toeplitz_bwd(dbias_w)
    n_far = BAND_W - REL_CLIP + 1
    G["od_rel_bias"] = jnp.concatenate(
        [jnp.zeros((C_HEADS, REL_CLIP - BAND_TQ + 1), F32), ddiag[:, n_far:][:, ::-1],
         jnp.sum(ddiag[:, :n_far], axis=1, keepdims=True)], axis=1)
    dh2, dh2b, g_gmix1 = _mm_rms_bwd(dqkv, W["od_w_qkv_t"], h2, W["g_mix"][1:2] + tok[:1, :1], dh3,
                                     name="proj_qkv_dx")

    dh1, dh1b, g_gffn0 = ffn_bwd(dh2, dh2b, h1, u1, a0, b0, 0)
    G["g_ffn"] = jnp.concatenate([g_gffn0, g_gffn1], axis=0)

    do_ab = _mm(dh1b, W["ev_w_out"], dims="nt", name="out_ev_dx")
    g0 = {"ev_w_out": _mm(o_ab, dh1b, dims="tn", out_dtype=BF16, name="out_ev_dw")}
    dqa_raw, dkn, dva, dkr = _mla_bwd(qa_raw, cos_q, sin_q, kv, kr, o_a, lse, do_ab, 0)
    dlat, g0["w_uq_t"], g0["w_ukv_t"], G["g_cq"], G["g_ckv"] = _latent_bwd(
        proj, nq, nkv, dqa_raw, dkn, dva, dkr, cos_k, sin_k, W["g_cq"], W["g_ckv"], W["w_uq_t"], W["w_ukv_t"])
    tok = put_grads("mix0", g0)
    dqb, dkb, dvb = _sb_bwd(proj, COL_SB, o_b32, w_b, sp_b, do_ab, MLA_HEADS // 2, tok)
    dproj = jnp.concatenate([dlat, dqb, dkb, dvb], axis=1)
    tok = put_grads("in0", {"w_in_t": _mm(dproj, u0, dims="tn", name="proj_in_dw")})
    dx, _, g_gmix0 = _mm_rms_bwd(dproj, W["w_in_t"], x, W["g_mix"][0:1] + tok[:1, :1], dh1, name="proj_in_dx")
    G["g_mix"] = jnp.concatenate([g_gmix0, g_gmix1], axis=0)
    return loss[0, 0], dx, G


_BIG = ["ev_w_in", "ev_w_uq", "ev_w_ukv", "ev_w_out", "od_w_qkv", "od_w_out", "w_gate", "w_up", "w_down"]
_COL_SHARDED = {"ev_w_in", "ev_w_uq", "ev_w_ukv", "od_w_qkv", "w_gate", "w_up"}
_SMALL = ["ev_g_cq", "ev_g_ckv", "od_rel_bias", "g_mix", "g_ffn", "g_final"]
_GROUPS = {
    "in0": ["ev_w_in"],
    "mix0": ["ev_w_uq", "ev_w_ukv", "ev_w_out"],
    "ffn0": ["w_gate0", "w_up0", "w_down0"],
    "mix1": ["od_w_qkv", "od_w_out"],
    "ffn1": ["w_gate1", "w_up1", "w_down1"],
}
_GROUP_SRC = {n + str(l): (n, l) for n in ("w_gate", "w_up", "w_down") for l in (0, 1)}
_BATCHES = {"in0": ["in0"], "layer0": ["mix0", "ffn0"], "layer1": ["mix1", "ffn1"]}
_BATCH_OF = {grp: batch for batch, grps in _BATCHES.items() for grp in grps}
_SMALL_ROWS = 8
_SMALL_COLS = 1792


def _pack_small(vals):
    flat = jnp.concatenate([v.reshape(-1).astype(F32) for v in vals])
    flat = jnp.pad(flat, (0, _SMALL_ROWS * _SMALL_COLS - flat.shape[0]))
    return flat.reshape(_SMALL_ROWS, _SMALL_COLS)


def _unpack_small(packed, like):
    flat = packed.reshape(-1)
    out, off = [], 0
    for v in like:
        out.append(flat[off:off + v.size].reshape(v.shape))
        off += v.size
    return out


def kernel(x, ev_w_in, ev_g_cq, ev_w_uq, ev_g_ckv, ev_w_ukv, ev_w_out, od_w_qkv, od_rel_bias, od_w_out, g_mix, g_ffn, w_gate, w_up, w_down, g_final, loss_target, m_ev_w_in, m_ev_g_cq, m_ev_w_uq, m_ev_g_ckv, m_ev_w_ukv, m_ev_w_out, m_od_w_qkv, m_od_rel_bias, m_od_w_out, m_g_mix, m_g_ffn, m_w_gate, m_w_up, m_w_down, m_g_final, v_ev_w_in, v_ev_g_cq, v_ev_w_uq, v_ev_g_ckv, v_ev_w_ukv, v_ev_w_out, v_od_w_qkv, v_od_rel_bias, v_od_w_out, v_g_mix, v_g_ffn, v_w_gate, v_w_up, v_w_down, v_g_final):
    args = dict(locals())
    w = {n: args[n] for n in _BIG + _SMALL}
    mom = {n: args["m_" + n] for n in _BIG + _SMALL}
    var = {n: args["v_" + n] for n in _BIG + _SMALL}

    own = {}
    for grp, names in _GROUPS.items():
        for n in names:
            base, layer = _GROUP_SRC.get(n, (n, 0))
            shard = w[base][layer:layer + 1]
            own[n] = (jnp.swapaxes(shard, 1, 2) if base in _COL_SHARDED else shard).astype(BF16)
    placed = dict(zip(own, _place_own(list(own.values()), [False] * len(own), name="gather_own")))
    handles, token = _gather2_start(
        [[own[n] for n in names] for names in _GROUPS.values()],
        [[placed[n] for n in names] for names in _GROUPS.values()], x[0, :8, :LANES], name="gather_start")
    gather = dict(zip(_GROUPS, handles))
    pass_before = {"in0": ["in0"], "mix0": ["mix0"]}
    pass_after = {"ffn0": ("mix1", "g_ffn"), "mix1": ("ffn1", "g_mix")}

    def prefetch(grp, after):
        gather[grp], tok = _gather2_pass_on(gather[grp], after, name="gather_pass_" + grp)
        return tok

    def get_weights(grp, after):
        names = _GROUPS[grp]
        after = token if after is None else after
        for g in pass_before.get(grp, []):
            gather[g], _ = _gather2_pass_on(gather[g], after, name="gather_pass_" + g)
        lands = _gather2_wait(gather[grp], after, name="gather_wait_" + grp)
        full = {n: l.reshape(-1, l.shape[-1]) for n, l in zip(names, lands)}
        out = {}
        if grp in pass_after:
            g, gain = pass_after[grp]
            gather[g], tok = _gather2_pass_on(gather[g], lands[0], name="gather_pass_" + g)
            out[gain] = small[gain] + tok[:1, :1]
        if grp == "in0":
            out.update({"w_in_t": _w_in_local(full["ev_w_in"])})
        elif grp == "mix0":
            out.update({"w_uq_t": _w_uq_local(full["ev_w_uq"]), "w_ukv_t": _w_ukv_local(full["ev_w_ukv"]),
                        "ev_w_out": full["ev_w_out"]})
        elif grp == "mix1":
            out.update({"od_w_qkv_t": full["od_w_qkv"], "od_w_out": full["od_w_out"]})
        else:
            layer = grp[-1]
            out.update({"w_gate_t" + layer: full["w_gate" + layer], "w_up_t" + layer: full["w_up" + layer],
                        "w_down" + layer: full["w_down" + layer]})
        return out

    scatter, pending = {}, {}

    def put_grads(grp, g):
        if grp == "in0":
            g = {"ev_w_in": _w_in_grad(g["w_in_t"])}
        elif grp == "mix0":
            g = {"ev_w_uq": _w_uq_grad(g["w_uq_t"]), "ev_w_ukv": _w_ukv_grad(g["w_ukv_t"]),
                 "ev_w_out": g["ev_w_out"]}
        elif grp == "mix1":
            g = {"od_w_qkv": g["od_w_qkv_t"], "od_w_out": g["od_w_out"]}
        else:
            layer = grp[-1]
            g = {"w_gate" + layer: g["w_gate_t"], "w_up" + layer: g["w_up_t"], "w_down" + layer: g["w_down"]}
        pending.update({n: v.reshape(N_DEV, 1, v.shape[0] // N_DEV, v.shape[1]).astype(BF16) for n, v in g.items()})
        batch = _BATCH_OF[grp]
        names = [n for gr in _BATCHES[batch] for n in _GROUPS[gr]]
        if batch == "in0" or not all(n in pending for n in names):
            return jnp.zeros((8, LANES), F32)
        send = [pending[n] for n in names]
        scatter[batch], tok = _exchange_start(send, [True] * len(names), send[0], name="scatter_start_" + batch)
        return tok

    small = {"g_cq": ev_g_cq, "g_ckv": ev_g_ckv, "od_rel_bias": od_rel_bias[0],
             "g_mix": g_mix + token[0, 0], "g_ffn": g_ffn, "g_final": g_final.reshape(1, -1)}
    loss_part, dx, G = _local_step(x[0], loss_target[0], small, get_weights, put_grads, prefetch)
    g_small = _pack_small([G["g_cq"], G["g_ckv"], G["od_rel_bias"], G["g_mix"], G["g_ffn"], G["g_final"],
                           loss_part.reshape(1)])
    scatter["in0"], _ = _exchange_start([pending["ev_w_in"], g_small], [True, False], dx, name="scatter_start_in0")

    grads, deltas, new_m, new_v = {}, {}, {}, {}
    parts, after = {}, dx

    def wait_parts(batch, after):
        lands = _exchange_wait(scatter[batch], after, name="scatter_wait_" + batch)
        parts.update(zip([n for grp in _BATCHES[batch] for n in _GROUPS[grp]], lands))
        return lands[0]

    def adamw(n):
        col = n in _COL_SHARDED
        rows = lambda a: (jnp.swapaxes(a, 1, 2) if col else a).reshape(-1, a.shape[1 if col else 2])
        layers = [parts[n]] if n in parts else [parts[n + "0"], parts[n + "1"]]
        res = _adamw(rows(w[n]), [p.reshape(N_DEV, -1, p.shape[-1]) for p in layers], rows(mom[n]), rows(var[n]),
                     name="adamw_" + n)
        L, a1, a2 = w[n].shape
        back = lambda r: jnp.swapaxes(r.reshape(L, a2, a1), 1, 2) if col else r.reshape(L, a1, a2)
        grads[n], deltas[n], new_m[n], new_v[n] = [back(r) for r in res]
        return res[0]

    for batch in ("layer1", "layer0"):
        after = wait_parts(batch, after)
    parts["ev_w_in"], small_parts = _exchange_wait(scatter["in0"], tuple(adamw(n) for n in _BIG[1:]),
                                                   name="scatter_wait_in0")
    adamw("ev_w_in")
    small_w = [w[n] for n in _SMALL]
    loss = jnp.sum(small_parts.reshape(N_DEV, -1)[:, sum(v.size for v in small_w)])
    res = _adamw(_pack_small(small_w), [small_parts], _pack_small([mom[n] for n in _SMALL]),
                 _pack_small([var[n] for n in _SMALL]), name="adamw_small")
    for d, packed in zip((grads, deltas, new_m, new_v), res):
        for n, val in zip(_SMALL, _unpack_small(packed, small_w)):
            d[n] = val

    order = ["ev_w_in", "ev_g_cq", "ev_w_uq", "ev_g_ckv", "ev_w_ukv", "ev_w_out", "od_w_qkv", "od_rel_bias",
             "od_w_out", "g_mix", "g_ffn", "w_gate", "w_up", "w_down", "g_final"]
    out = [loss, dx[None]]
    for d in (grads, deltas, new_m, new_v):
        out += [d[n] for n in order]
    return tuple(out)
```
